```python
import jax, jax.numpy as jnp
from jax import lax
import numpy as np

D_MODEL = 1024
BATCH = 16
SEQ = 2048
DEPTH = 1

HEAD_DIM = 64
HEADS_PER_GROUP = 4
DILATED_GROUPS = ((128, 1), (512, 4), (2048, 16))
N_GROUPS = 3
N_ATTN_HEADS = N_GROUPS * HEADS_PER_GROUP
ATTN_WIDTH = N_ATTN_HEADS * HEAD_DIM
ATTN_OUT_WIDTH = HEADS_PER_GROUP * HEAD_DIM
CONV_WIDTH = D_MODEL
CONV_K = 3
SUB_BLOCK = 128
ALIBI_MAX_EXP = 8.0
DEEPNORM_ALPHA = (2.0 * DEPTH) ** 0.25
DEEPNORM_BETA = (8.0 * DEPTH) ** -0.25
LN_EPS = 1e-5
IN_WIDTHS = (ATTN_WIDTH, ATTN_WIDTH, ATTN_WIDTH, ATTN_OUT_WIDTH,
             CONV_WIDTH, CONV_WIDTH, CONV_WIDTH, CONV_WIDTH, D_MODEL, D_MODEL)
D_IN = 3 * ATTN_WIDTH + ATTN_OUT_WIDTH + 4 * CONV_WIDTH + 2 * D_MODEL

kernel_name = "hybrid_dilated_attn_shortconv_deepnorm_adaln"


def _split_points():
    pts, acc = [], 0
    for w in IN_WIDTHS[:-1]:
        acc += w
        pts.append(acc)
    return pts


def _layer_norm(x, g, b):
    xf = x.astype(jnp.float32)
    mu = xf.mean(-1, keepdims=True)
    var = jnp.square(xf - mu).mean(-1, keepdims=True)
    y = (xf - mu) * lax.rsqrt(var + LN_EPS) * g.astype(jnp.float32) + b.astype(jnp.float32)
    return y.astype(x.dtype)


def _dilated_window_attention(q, k, v, window, dilation, slopes):
    bsz, s, h, dh = q.shape
    span = window // dilation
    n = s // dilation
    L = SUB_BLOCK
    nb = -(-n // L)
    pad = nb * L - n

    def to_sub(t):
        t = t.reshape(bsz, n, dilation, h, dh).transpose(0, 2, 1, 3, 4)
        t = jnp.pad(t, ((0, 0), (0, 0), (0, pad), (0, 0), (0, 0)))
        return t.reshape(bsz, dilation, nb, L, h, dh)

    def with_prev(t):
        prev = jnp.pad(t, ((0, 0), (0, 0), (1, 0), (0, 0), (0, 0), (0, 0)))[:, :, :-1]
        return jnp.concatenate([prev, t], axis=3)

    qs = to_sub(q)
    kb = with_prev(to_sub(k))
    vb = with_prev(to_sub(v))

    scores = jnp.einsum('brnqhd,brnkhd->brnhqk', qs, kb).astype(jnp.float32) * (dh ** -0.5)
    qi = jnp.arange(L)[:, None]
    kj = jnp.arange(2 * L)[None, :]
    delta = qi + L - kj
    key_sub = jnp.arange(nb)[:, None, None] * L + kj[None] - L
    valid = (delta >= 0) & (delta <= span) & (key_sub >= 0)
    alibi = -slopes[:, None, None] * (delta * dilation).astype(jnp.float32)[None]
    scores = scores + alibi[None, None, None]
    scores = jnp.where(valid[None, None, :, None], scores, -jnp.inf)
    m = scores.max(-1, keepdims=True)
    p = jnp.exp(scores - m)
    den = p.sum(-1, keepdims=True)
    out = jnp.einsum('brnhqk,brnkhd->brnqhd', (p / den).astype(v.dtype), vb)
    lse = jnp.swapaxes((m + jnp.log(den))[..., 0], 3, 4)

    def from_sub(t):
        t = t.reshape((bsz, dilation, nb * L) + t.shape[4:])[:, :, :n]
        return jnp.swapaxes(t, 1, 2).reshape((bsz, s) + t.shape[3:])

    return from_sub(out), from_sub(lse)


def _fwd_setup_inputs(seed: int = 0) -> dict:
    key = jax.random.key(seed)
    ks = jax.random.split(key, 16)
    f32 = jnp.float32
    x = jax.random.normal(ks[0], (BATCH, SEQ, D_MODEL), f32)
    c = jax.random.normal(ks[1], (BATCH, D_MODEL), f32)
    w_ada = jax.random.normal(ks[2], (DEPTH, D_MODEL, 3 * D_MODEL), f32) * (D_MODEL ** -0.5) * 0.5
    b_ada = jax.random.normal(ks[3], (DEPTH, 3 * D_MODEL), f32) * 0.01
    w_in = jax.random.normal(ks[4], (DEPTH, D_MODEL, D_IN), f32) * (D_MODEL ** -0.5)
    w_in = w_in.at[:, :, 2 * ATTN_WIDTH:3 * ATTN_WIDTH].multiply(DEEPNORM_BETA)
    b_in = jax.random.normal(ks[5], (DEPTH, D_IN), f32) * 0.01
    conv_w = jax.random.normal(ks[6], (DEPTH, CONV_K, CONV_WIDTH), f32) * (CONV_K ** -0.5)
    w_proj_attn = jax.random.normal(ks[7], (DEPTH, ATTN_OUT_WIDTH, D_MODEL), f32) * (ATTN_OUT_WIDTH ** -0.5) * DEEPNORM_BETA
    w_proj_conv = jax.random.normal(ks[8], (DEPTH, CONV_WIDTH, D_MODEL), f32) * (CONV_WIDTH ** -0.5) * DEEPNORM_BETA
    w_out = jax.random.normal(ks[9], (DEPTH, D_MODEL, D_MODEL), f32) * (D_MODEL ** -0.5) * DEEPNORM_BETA
    b_out = jax.random.normal(ks[10], (DEPTH, D_MODEL), f32) * 0.01
    ln_g = 1.0 + 0.02 * jax.random.normal(ks[11], (DEPTH, D_MODEL), f32)
    ln_b = 0.02 * jax.random.normal(ks[12], (DEPTH, D_MODEL), f32)
    return {"x": x, "c": c, "w_ada": w_ada, "b_ada": b_ada, "w_in": w_in, "b_in": b_in,
            "conv_w": conv_w, "w_proj_attn": w_proj_attn, "w_proj_conv": w_proj_conv,
            "w_out": w_out, "b_out": b_out, "ln_g": ln_g, "ln_b": ln_b}


def _fwd_reference(x, c, w_ada, b_ada, w_in, b_in, conv_w, w_proj_attn, w_proj_conv, w_out, b_out, ln_g, ln_b):
    bsz, s, _ = x.shape
    split_pts = _split_points()
    slopes = 2.0 ** (-ALIBI_MAX_EXP * (jnp.arange(N_ATTN_HEADS, dtype=jnp.float32) + 1.0) / N_ATTN_HEADS)
    c_act = jax.nn.silu(c)
    for layer in range(DEPTH):
        ada = (c_act @ w_ada[layer] + b_ada[layer])[:, None, :]
        shift, scale, gate = jnp.split(ada, 3, axis=-1)
        h = x * (1.0 + scale) + shift

        proj = h @ w_in[layer] + b_in[layer]
        q, k, v, z_attn, u_x, g_b, g_c, z_conv, g_mix_a, g_mix_b = jnp.split(proj, split_pts, axis=-1)
        q = q.reshape(bsz, s, N_ATTN_HEADS, HEAD_DIM)
        k = k.reshape(bsz, s, N_ATTN_HEADS, HEAD_DIM)
        v = v.reshape(bsz, s, N_ATTN_HEADS, HEAD_DIM)

        outs, lses = [], []
        for g, (window, dilation) in enumerate(DILATED_GROUPS):
            hs = slice(g * HEADS_PER_GROUP, (g + 1) * HEADS_PER_GROUP)
            o_g, lse_g = _dilated_window_attention(q[:, :, hs], k[:, :, hs], v[:, :, hs],
                                                   window, dilation, slopes[hs])
            outs.append(o_g.astype(jnp.float32))
            lses.append(lse_g)
        mix_w = jax.nn.softmax(jnp.stack(lses), axis=0)
        o_attn = jnp.einsum('gbsh,gbshd->bshd', mix_w, jnp.stack(outs)).astype(x.dtype)
        o_attn = o_attn.reshape(bsz, s, ATTN_OUT_WIDTH)
        y_attn = (o_attn * jax.nn.silu(z_attn)) @ w_proj_attn[layer]

        u = g_c * u_x
        u_pad = jnp.pad(u, ((0, 0), (CONV_K - 1, 0), (0, 0)))
        cw = conv_w[layer]
        conv = cw[0] * u_pad[:, 0:s]
        for j in range(1, CONV_K):
            conv = conv + cw[j] * u_pad[:, j:j + s]
        y_conv = (g_b * conv * jax.nn.silu(z_conv)) @ w_proj_conv[layer]

        merged = jax.nn.sigmoid(g_mix_a) * y_attn + jax.nn.sigmoid(g_mix_b) * y_conv
        sub = gate * (merged @ w_out[layer] + b_out[layer])
        x = _layer_norm(DEEPNORM_ALPHA * x + sub, ln_g[layer], ln_b[layer])
    return x


import jax as _jax
import jax.numpy as _jnp

TWIN_FORMAT = 'train_step'
FWD_PARAMS = ['x', 'c', 'w_ada', 'b_ada', 'w_in', 'b_in', 'conv_w', 'w_proj_attn', 'w_proj_conv', 'w_out', 'b_out', 'ln_g', 'ln_b']
TWIN_WEIGHTS = ['w_ada', 'b_ada', 'w_in', 'b_in', 'conv_w', 'w_proj_attn', 'w_proj_conv', 'w_out', 'b_out', 'ln_g', 'ln_b']
TWIN_DIFF_INPUT = 'x'
TWIN_INPUTS = ['x', 'c', 'w_ada', 'b_ada', 'w_in', 'b_in', 'conv_w', 'w_proj_attn', 'w_proj_conv', 'w_out', 'b_out', 'ln_g', 'ln_b', 'loss_target', 'm_w_ada', 'm_b_ada', 'm_w_in', 'm_b_in', 'm_conv_w', 'm_w_proj_attn', 'm_w_proj_conv', 'm_w_out', 'm_b_out', 'm_ln_g', 'm_ln_b', 'v_w_ada', 'v_b_ada', 'v_w_in', 'v_b_in', 'v_conv_w', 'v_w_proj_attn', 'v_w_proj_conv', 'v_w_out', 'v_b_out', 'v_ln_g', 'v_ln_b']
TWIN_OUTPUTS = ['loss', 'grad_x', 'grad_w_ada', 'grad_b_ada', 'grad_w_in', 'grad_b_in', 'grad_conv_w', 'grad_w_proj_attn', 'grad_w_proj_conv', 'grad_w_out', 'grad_b_out', 'grad_ln_g', 'grad_ln_b', 'delta_w_ada', 'delta_b_ada', 'delta_w_in', 'delta_b_in', 'delta_conv_w', 'delta_w_proj_attn', 'delta_w_proj_conv', 'delta_w_out', 'delta_b_out', 'delta_ln_g', 'delta_ln_b', 'new_m_w_ada', 'new_m_b_ada', 'new_m_w_in', 'new_m_b_in', 'new_m_conv_w', 'new_m_w_proj_attn', 'new_m_w_proj_conv', 'new_m_w_out', 'new_m_b_out', 'new_m_ln_g', 'new_m_ln_b', 'new_v_w_ada', 'new_v_b_ada', 'new_v_w_in', 'new_v_b_in', 'new_v_conv_w', 'new_v_w_proj_attn', 'new_v_w_proj_conv', 'new_v_w_out', 'new_v_b_out', 'new_v_ln_g', 'new_v_ln_b']
TWIN_LEAF_KINDS = {'loss': 'loss', 'grad_x': 'grad_x', 'grad_w_ada': 'grad_w', 'grad_b_ada': 'grad_w', 'grad_w_in': 'grad_w', 'grad_b_in': 'grad_w', 'grad_conv_w': 'grad_w', 'grad_w_proj_attn': 'grad_w', 'grad_w_proj_conv': 'grad_w', 'grad_w_out': 'grad_w', 'grad_b_out': 'grad_w', 'grad_ln_g': 'grad_w', 'grad_ln_b': 'grad_w', 'delta_w_ada': 'delta_w', 'delta_b_ada': 'delta_w', 'delta_w_in': 'delta_w', 'delta_b_in': 'delta_w', 'delta_conv_w': 'delta_w', 'delta_w_proj_attn': 'delta_w', 'delta_w_proj_conv': 'delta_w', 'delta_w_out': 'delta_w', 'delta_b_out': 'delta_w', 'delta_ln_g': 'delta_w', 'delta_ln_b': 'delta_w', 'new_m_w_ada': 'new_m', 'new_m_b_ada': 'new_m', 'new_m_w_in': 'new_m', 'new_m_b_in': 'new_m', 'new_m_conv_w': 'new_m', 'new_m_w_proj_attn': 'new_m', 'new_m_w_proj_conv': 'new_m', 'new_m_w_out': 'new_m', 'new_m_b_out': 'new_m', 'new_m_ln_g': 'new_m', 'new_m_ln_b': 'new_m', 'new_v_w_ada': 'new_v', 'new_v_b_ada': 'new_v', 'new_v_w_in': 'new_v', 'new_v_b_in': 'new_v', 'new_v_conv_w': 'new_v', 'new_v_w_proj_attn': 'new_v', 'new_v_w_proj_conv': 'new_v', 'new_v_w_out': 'new_v', 'new_v_b_out': 'new_v', 'new_v_ln_g': 'new_v', 'new_v_ln_b': 'new_v'}


def _forward(args):
    return _fwd_reference(*[args[k] for k in FWD_PARAMS])


def _output_shape():
    out = _jax.eval_shape(lambda: _forward(_fwd_setup_inputs(0)))
    return out.shape, out.dtype

N_MICROBATCH = 1
ADAM_LR = 0.001
ADAM_B1 = 0.9
ADAM_B2 = 0.999
ADAM_EPS = 1e-08
ADAM_WD = 0.01
ADAM_STEP = 10
PER_EXAMPLE_BATCH_AXIS = {'x': 0, 'c': 0, 'loss_target': 0}
SHARED_INPUTS = []
_WEIGHT_DTYPES = {'w_ada': _jnp.float32, 'b_ada': _jnp.float32, 'w_in': _jnp.float32, 'b_in': _jnp.float32, 'conv_w': _jnp.float32, 'w_proj_attn': _jnp.float32, 'w_proj_conv': _jnp.float32, 'w_out': _jnp.float32, 'b_out': _jnp.float32, 'ln_g': _jnp.float32, 'ln_b': _jnp.float32}
MOMENT_SCALE = {'w_ada': 1.199819e-02, 'b_ada': 1.996569e-02, 'w_in': 5.587172e-03, 'b_in': 5.306013e-03, 'conv_w': 8.296634e-03, 'w_proj_attn': 2.578454e-03, 'w_proj_conv': 1.302460e-02, 'w_out': 1.326019e-02, 'b_out': 6.854134e-02, 'ln_g': 3.198671e+01, 'ln_b': 4.602507e-01}


def _to_microbatches(a, axis):
    t = _jnp.moveaxis(a, axis, 0)
    t = t.reshape((N_MICROBATCH, t.shape[0] // N_MICROBATCH) + t.shape[1:])
    return _jnp.moveaxis(t, 1, axis + 1)


def setup_inputs(seed: int = 0) -> dict:
    inp = _fwd_setup_inputs(seed)
    key = _jax.random.fold_in(_jax.random.key(seed), 7919)
    shape, _ = _output_shape()
    out = dict(inp)
    out["loss_target"] = _jax.random.normal(_jax.random.fold_in(key, 0), shape, _jnp.float32)
    for i, name in enumerate(TWIN_WEIGHTS):
        w = inp[name].astype(_jnp.float32)
        if MOMENT_SCALE is None:
            s = _jnp.sqrt(_jnp.mean(_jnp.square(w)) + 1e-30)
        else:
            s = MOMENT_SCALE[name]
        km, kv = _jax.random.split(_jax.random.fold_in(key, i + 1))
        out[name] = w
        out["m_" + name] = s * _jax.random.normal(km, w.shape, _jnp.float32)
        out["v_" + name] = (s * s) * _jax.random.uniform(kv, w.shape, _jnp.float32, 0.5, 1.5)
    if N_MICROBATCH > 1:
        for name, axis in PER_EXAMPLE_BATCH_AXIS.items():
            out[name] = _to_microbatches(out[name], axis)
    return {'x': out['x'], 'c': out['c'], 'w_ada': out['w_ada'], 'b_ada': out['b_ada'], 'w_in': out['w_in'], 'b_in': out['b_in'], 'conv_w': out['conv_w'], 'w_proj_attn': out['w_proj_attn'], 'w_proj_conv': out['w_proj_conv'], 'w_out': out['w_out'], 'b_out': out['b_out'], 'ln_g': out['ln_g'], 'ln_b': out['ln_b'], 'loss_target': out['loss_target'], 'm_w_ada': out['m_w_ada'], 'm_b_ada': out['m_b_ada'], 'm_w_in': out['m_w_in'], 'm_b_in': out['m_b_in'], 'm_conv_w': out['m_conv_w'], 'm_w_proj_attn': out['m_w_proj_attn'], 'm_w_proj_conv': out['m_w_proj_conv'], 'm_w_out': out['m_w_out'], 'm_b_out': out['m_b_out'], 'm_ln_g': out['m_ln_g'], 'm_ln_b': out['m_ln_b'], 'v_w_ada': out['v_w_ada'], 'v_b_ada': out['v_b_ada'], 'v_w_in': out['v_w_in'], 'v_b_in': out['v_b_in'], 'v_conv_w': out['v_conv_w'], 'v_w_proj_attn': out['v_w_proj_attn'], 'v_w_proj_conv': out['v_w_proj_conv'], 'v_w_out': out['v_w_out'], 'v_b_out': out['v_b_out'], 'v_ln_g': out['v_ln_g'], 'v_ln_b': out['v_ln_b']}


def _loss(weights, diff, rest, loss_target):
    with _jax.named_scope("forward"):
        args = {**rest, TWIN_DIFF_INPUT: diff, **{k: w.astype(_WEIGHT_DTYPES[k]) for k, w in weights.items()}}
        y = _forward(args)
    with _jax.named_scope("loss_head"):
        err = _jnp.square(y.astype(_jnp.float32) - loss_target)
        return 0.5 * _jnp.sum(_jnp.mean(err, axis=-1)) if err.ndim else 0.5 * err


def _adamw(w, g, m, v):
    m = ADAM_B1 * m + (1.0 - ADAM_B1) * g
    v = ADAM_B2 * v + (1.0 - ADAM_B2) * _jnp.square(g)
    m_hat = m / (1.0 - ADAM_B1 ** ADAM_STEP)
    v_hat = v / (1.0 - ADAM_B2 ** ADAM_STEP)
    delta = -ADAM_LR * (m_hat / (_jnp.sqrt(v_hat) + ADAM_EPS) + ADAM_WD * w)
    return delta, m, v


def reference(x, c, w_ada, b_ada, w_in, b_in, conv_w, w_proj_attn, w_proj_conv, w_out, b_out, ln_g, ln_b, loss_target, m_w_ada, m_b_ada, m_w_in, m_b_in, m_conv_w, m_w_proj_attn, m_w_proj_conv, m_w_out, m_b_out, m_ln_g, m_ln_b, v_w_ada, v_b_ada, v_w_in, v_b_in, v_conv_w, v_w_proj_attn, v_w_proj_conv, v_w_out, v_b_out, v_ln_g, v_ln_b):
    given = dict(x=x, c=c, w_ada=w_ada, b_ada=b_ada, w_in=w_in, b_in=b_in, conv_w=conv_w, w_proj_attn=w_proj_attn, w_proj_conv=w_proj_conv, w_out=w_out, b_out=b_out, ln_g=ln_g, ln_b=ln_b, loss_target=loss_target, m_w_ada=m_w_ada, m_b_ada=m_b_ada, m_w_in=m_w_in, m_b_in=m_b_in, m_conv_w=m_conv_w, m_w_proj_attn=m_w_proj_attn, m_w_proj_conv=m_w_proj_conv, m_w_out=m_w_out, m_b_out=m_b_out, m_ln_g=m_ln_g, m_ln_b=m_ln_b, v_w_ada=v_w_ada, v_b_ada=v_b_ada, v_w_in=v_w_in, v_b_in=v_b_in, v_conv_w=v_conv_w, v_w_proj_attn=v_w_proj_attn, v_w_proj_conv=v_w_proj_conv, v_w_out=v_w_out, v_b_out=v_b_out, v_ln_g=v_ln_g, v_ln_b=v_ln_b)
    weights = {n: given[n] for n in TWIN_WEIGHTS}
    shared = {n: given[n] for n in SHARED_INPUTS}
    per_example = {n: given[n] for n in ['x', 'c']}
    grad_fn = _jax.value_and_grad(_loss, argnums=(0, 1))

    def one_microbatch(ex, loss_target):
        ex = dict(ex)
        diff = ex.pop(TWIN_DIFF_INPUT)
        return grad_fn(weights, diff, {**shared, **ex}, loss_target)

    if N_MICROBATCH == 1:
        loss, (grad_w, grad_x) = one_microbatch(per_example, given["loss_target"])
    else:
        def body(carry, xs):
            loss_sum, grad_sum = carry
            l_k, (gw_k, gx_k) = one_microbatch(xs[0], xs[1])
            with _jax.named_scope("update"):
                return (loss_sum + l_k, _jax.tree.map(_jnp.add, grad_sum, gw_k)), gx_k

        init = (_jnp.zeros((), _jnp.float32), _jax.tree.map(_jnp.zeros_like, weights))
        (loss, grad_w), grad_x = _jax.lax.scan(body, init, (per_example, given["loss_target"]))
    with _jax.named_scope("update"):
        delta_w, new_m, new_v = {}, {}, {}
        for n in TWIN_WEIGHTS:
            delta_w[n], new_m[n], new_v[n] = _adamw(weights[n], grad_w[n], given["m_" + n], given["v_" + n])
    return (loss, grad_x, *[grad_w[n] for n in TWIN_WEIGHTS], *[delta_w[n] for n in TWIN_WEIGHTS],
            *[new_m[n] for n in TWIN_WEIGHTS], *[new_v[n] for n in TWIN_WEIGHTS])
```

```python
import functools

import jax
import jax.numpy as jnp
from jax import lax
from jax.experimental import pallas as pl
from jax.experimental.pallas import tpu as pltpu

F32 = jnp.float32
BF16 = jnp.bfloat16
MESH = pl.DeviceIdType.MESH

HEAD_DIM = 64
N_GROUPS = 3
DILATIONS = (1, 4, 16)
N_HEADS = 12
SUB = 128
Q_WIDTH = 768
Z_WIDTH = 256
ATT = 3 * Q_WIDTH + Z_WIDTH
SLAB = 128
PAIR_SLABS = 10
PAIR_COLS = PAIR_SLABS * SLAB
CONV_TILE = 256
ALIBI_MAX_EXP = 8.0
ALPHA = 2.0 ** 0.25
LN_EPS = 1e-5
ADAM_LR, ADAM_B1, ADAM_B2, ADAM_EPS, ADAM_WD, ADAM_STEP = 0.001, 0.9, 0.999, 1e-08, 0.01, 10
N_CHIPS = 4
N_DEV = 8
VMEM_LIMIT_V7X = 60 * 1024 * 1024
NEG = -1e30


def _params(*sem):
    return pltpu.CompilerParams(dimension_semantics=sem, vmem_limit_bytes=VMEM_LIMIT_V7X)


def _sigmoid(v):
    return 1.0 / (1.0 + jnp.exp(-v))


class _Layout:
    def __init__(self, d):
        self.d = d
        self.din = ATT + 6 * d
        c0 = 3072
        while c0 % (2 * d):
            c0 += 1024
        self.c0, self.g0, self.np = c0, c0 + 4 * d, c0 + 6 * d
        self.n_conv_tiles = d // CONV_TILE

    def attn_nat_slab(self, s):
        p, i = s // PAIR_SLABS, s % PAIR_SLABS
        return jnp.where(i < 9, (i // 3) * 6 + (i % 3) * 2 + p, 18 + p)

    def rest_nat_tile(self, t):
        n4 = 4 * self.n_conv_tiles
        conv = ATT // CONV_TILE + (t % 4) * self.n_conv_tiles + t // 4
        return jnp.where(t < n4, conv, ATT // CONV_TILE + t)

    def perm_vector(self, v):
        parts = []
        for s in range(2 * PAIR_SLABS):
            p, i = divmod(s, PAIR_SLABS)
            ns = (i // 3) * 6 + (i % 3) * 2 + p if i < 9 else 18 + p
            parts.append(v[:, ns * SLAB:(ns + 1) * SLAB])
        parts.append(jnp.zeros((1, self.c0 - ATT), v.dtype))
        for j in range(self.n_conv_tiles):
            for k in range(4):
                a = ATT + k * self.d + j * CONV_TILE
                parts.append(v[:, a:a + CONV_TILE])
        parts.append(v[:, ATT + 4 * self.d:])
        return jnp.concatenate(parts, axis=1)


def _place():
    return lax.axis_index("x"), lax.axis_index("y"), lax.axis_index("c")


def _other_chips(x, y):
    return [(1 - x, y), (x, 1 - y), (1 - x, 1 - y)]


def _half_rows(ref, col_sharded, half):
    n = ref.shape[0] // 2
    return ref.at[pl.ds(half * n, n)]


def _shard_of(ref, col_sharded, chip, half=None):
    if col_sharded:
        cs = ref.shape[1] // N_CHIPS
        cols = pl.ds(pl.multiple_of(chip * cs, SLAB), cs)
        if half is None:
            return ref.at[:, cols]
        n = ref.shape[0] // 2
        return ref.at[pl.ds(half * n, n), cols]
    rs = ref.shape[0] // N_CHIPS
    if half is None:
        return ref.at[pl.ds(chip * rs, rs)]
    return ref.at[pl.ds(chip * rs + half * (rs // 2), rs // 2)]


def _gather_weights(shards, col_sharded, small):
    n = len(shards)
    full_shapes = []
    for s, cs in zip(shards, col_sharded):
        full_shapes.append((s.shape[0], s.shape[1] * N_CHIPS) if cs else (s.shape[0] * N_CHIPS, s.shape[1]))

    def body(*refs):
        ins, sm_in, outs, sm_out = refs[:n], refs[n], refs[n + 1:2 * n + 1], refs[2 * n + 1]
        send, recv, fsend, frecv, lsem, ssend, srecv = refs[2 * n + 2:]
        x, y, c = _place()
        mine = 2 * x + y
        sibling = (x, y, 1 - c)
        chips = _other_chips(x, y)

        local = [pltpu.make_async_copy(ins[w], _shard_of(outs[w], col_sharded[w], mine), lsem.at[w]) for w in range(n)]
        local.append(pltpu.make_async_copy(sm_in, _shard_of(sm_out, True, mine), lsem.at[n]))
        for cp in local:
            cp.start()
        sends = []
        for k, (cx, cy) in enumerate(chips):
            cp = pltpu.make_async_remote_copy(src_ref=sm_in, dst_ref=_shard_of(sm_out, True, mine), send_sem=ssend.at[k],
                                              recv_sem=srecv.at[k], device_id=(cx, cy, c), device_id_type=MESH)
            cp.start()
            sends.append(cp)
        for k, (cx, cy) in enumerate(chips):
            for w in range(n):
                cp = pltpu.make_async_remote_copy(
                    src_ref=_half_rows(ins[w], col_sharded[w], c), dst_ref=_shard_of(outs[w], col_sharded[w], mine, c),
                    send_sem=send.at[w, k], recv_sem=recv.at[w, k], device_id=(cx, cy, c), device_id_type=MESH)
                cp.start()
                sends.append(cp)
        for k, (cx, cy) in enumerate(chips):
            theirs = 2 * cx + cy
            for w in range(n):
                landed = _shard_of(outs[w], col_sharded[w], theirs, c)
                pltpu.make_async_remote_copy(src_ref=landed, dst_ref=landed, send_sem=send.at[w, k], recv_sem=recv.at[w, k],
                                             device_id=(cx, cy, c), device_id_type=MESH).wait_recv()
                cp = pltpu.make_async_remote_copy(src_ref=landed, dst_ref=landed, send_sem=fsend.at[w, k],
                                                  recv_sem=frecv.at[w, k], device_id=sibling, device_id_type=MESH)
                cp.start()
                sends.append(cp)
        for k, (cx, cy) in enumerate(chips):
            theirs = 2 * cx + cy
            for w in range(n):
                passed = _shard_of(outs[w], col_sharded[w], theirs, 1 - c)
                pltpu.make_async_remote_copy(src_ref=passed, dst_ref=passed, send_sem=fsend.at[w, k], recv_sem=frecv.at[w, k],
                                             device_id=sibling, device_id_type=MESH).wait_recv()
        for k, (cx, cy) in enumerate(chips):
            theirs = _shard_of(sm_out, True, 2 * cx + cy)
            pltpu.make_async_remote_copy(src_ref=theirs, dst_ref=theirs, send_sem=ssend.at[k], recv_sem=srecv.at[k],
                                         device_id=(cx, cy, c), device_id_type=MESH).wait_recv()
        for cp in sends:
            cp.wait_send()
        for cp in local:
            cp.wait()

    any_spec = pl.BlockSpec(memory_space=pl.ANY)
    outs = pl.pallas_call(
        body, name="gather_weights",
        out_shape=[jax.ShapeDtypeStruct(fs, BF16) for fs in full_shapes]
        + [jax.ShapeDtypeStruct((small.shape[0], small.shape[1] * N_CHIPS), small.dtype)],
        in_specs=[any_spec] * (n + 1), out_specs=[any_spec] * (n + 1),
        scratch_shapes=[pltpu.SemaphoreType.DMA((n, 3)), pltpu.SemaphoreType.DMA((n, 3)),
                        pltpu.SemaphoreType.DMA((n, 3)), pltpu.SemaphoreType.DMA((n, 3)), pltpu.SemaphoreType.DMA((n + 1,)),
                        pltpu.SemaphoreType.DMA((3,)), pltpu.SemaphoreType.DMA((3,))],
    )(*shards, small)
    return outs[:n], outs[n]


def _halves_view(g, col_sharded):
    if col_sharded:
        return g
    return g.reshape(N_CHIPS, g.shape[0] // N_CHIPS, g.shape[1])


def _pair_exchange_halves(grads, col_sharded):
    n = len(grads)
    views = [_halves_view(g, cs) for g, cs in zip(grads, col_sharded)]
    half_shapes = [v.shape[:-2] + (v.shape[-2] // 2, v.shape[-1]) for v in views]

    def half(ref, h):
        rows = ref.shape[-2] // 2
        if len(ref.shape) == 2:
            return ref.at[pl.ds(h * rows, rows)]
        return ref.at[:, pl.ds(h * rows, rows)]

    def body(*refs):
        ins, own, got = refs[:n], refs[n:2 * n], refs[2 * n:3 * n]
        send, recv, lsem = refs[3 * n:]
        x, y, c = _place()
        sibling = (x, y, 1 - c)
        local = [pltpu.make_async_copy(half(ins[w], c), own[w], lsem.at[w]) for w in range(n)]
        for cp in local:
            cp.start()
        sends = [pltpu.make_async_remote_copy(src_ref=half(ins[w], 1 - c), dst_ref=got[w], send_sem=send.at[w],
                                              recv_sem=recv.at[w], device_id=sibling, device_id_type=MESH) for w in range(n)]
        for cp in sends:
            cp.start()
        for cp in sends:
            cp.wait()
        for cp in local:
            cp.wait()

    any_spec = pl.BlockSpec(memory_space=pl.ANY)
    outs = pl.pallas_call(
        body, name="grads_pair_exchange",
        out_shape=[jax.ShapeDtypeStruct(s, v.dtype) for s, v in zip(half_shapes, views)] * 2,
        in_specs=[any_spec] * n, out_specs=[any_spec] * (2 * n),
        scratch_shapes=[pltpu.SemaphoreType.DMA((n,)), pltpu.SemaphoreType.DMA((n,)), pltpu.SemaphoreType.DMA((n,))],
    )(*views)
    return outs[:n], outs[n:]


def _chip_scatter(parts, col_sharded):
    n = len(parts)

    def piece(ref, cs, chip):
        if cs:
            w = ref.shape[1] // N_CHIPS
            return ref.at[:, pl.ds(pl.multiple_of(chip * w, SLAB), w)]
        return ref.at[chip]

    piece_shapes = [(p.shape[0], p.shape[1] // N_CHIPS) if cs else p.shape[1:] for p, cs in zip(parts, col_sharded)]

    def body(*refs):
        ins, own, got = refs[:n], refs[n:2 * n], refs[2 * n:3 * n]
        send, recv, lsem = refs[3 * n:]
        x, y, c = _place()
        mine = 2 * x + y
        local = [pltpu.make_async_copy(piece(ins[w], col_sharded[w], mine), own[w], lsem.at[w]) for w in range(n)]
        for cp in local:
            cp.start()
        sends = []
        for k, (cx, cy) in enumerate(_other_chips(x, y)):
            for w in range(n):
                cp = pltpu.make_async_remote_copy(
                    src_ref=piece(ins[w], col_sharded[w], 2 * cx + cy), dst_ref=got[w].at[k], send_sem=send.at[w, k],
                    recv_sem=recv.at[w, k], device_id=(cx, cy, c), device_id_type=MESH)
                cp.start()
                sends.append(cp)
        for cp in sends:
            cp.wait()
        for cp in local:
            cp.wait()

    any_spec = pl.BlockSpec(memory_space=pl.ANY)
    outs = pl.pallas_call(
        body, name="grads_chip_scatter",
        out_shape=[jax.ShapeDtypeStruct(s, p.dtype) for s, p in zip(piece_shapes, parts)]
        + [jax.ShapeDtypeStruct((3,) + s, p.dtype) for s, p in zip(piece_shapes, parts)],
        in_specs=[any_spec] * n, out_specs=[any_spec] * (2 * n),
        scratch_shapes=[pltpu.SemaphoreType.DMA((n, 3)), pltpu.SemaphoreType.DMA((n, 3)), pltpu.SemaphoreType.DMA((n,))],
    )(*parts)
    return outs[:n], outs[n:]


def _pair_join_halves(halves):
    n = len(halves)

    def body(*refs):
        ins, outs = refs[:n], refs[n:2 * n]
        send, recv, lsem = refs[2 * n:]
        x, y, c = _place()

        def dst(w):
            rows = ins[w].shape[0]
            return outs[w].at[pl.ds(c * rows, rows)]

        local = [pltpu.make_async_copy(ins[w], dst(w), lsem.at[w]) for w in range(n)]
        for cp in local:
            cp.start()
        sends = [pltpu.make_async_remote_copy(src_ref=ins[w], dst_ref=dst(w), send_sem=send.at[w], recv_sem=recv.at[w],
                                              device_id=(x, y, 1 - c), device_id_type=MESH) for w in range(n)]
        for cp in sends:
            cp.start()
        for cp in sends:
            cp.wait()
        for cp in local:
            cp.wait()

    any_spec = pl.BlockSpec(memory_space=pl.ANY)
    return pl.pallas_call(
        body, name="grads_pair_join",
        out_shape=[jax.ShapeDtypeStruct((2 * h.shape[0], h.shape[1]), h.dtype) for h in halves],
        in_specs=[any_spec] * n, out_specs=[any_spec] * n,
        scratch_shapes=[pltpu.SemaphoreType.DMA((n,)), pltpu.SemaphoreType.DMA((n,)), pltpu.SemaphoreType.DMA((n,))],
    )(*halves)


def _all_sum_small(vec, d):
    rows = vec.shape[0]

    def body(v_ref, o_ref, all_ref, send, recv):
        x, y, c = _place()
        me = 4 * x + 2 * y + c
        all_ref[me] = v_ref[...]
        copies = []
        for k in range(1, N_DEV):
            fx, fy, fc = (k >> 2) & 1, (k >> 1) & 1, k & 1
            peer = (x ^ fx, y ^ fy, c ^ fc)
            cp = pltpu.make_async_remote_copy(src_ref=v_ref, dst_ref=all_ref.at[me], send_sem=send.at[k - 1],
                                              recv_sem=recv.at[k - 1], device_id=peer, device_id_type=MESH)
            cp.start()
            copies.append((cp, 4 * peer[0] + 2 * peer[1] + peer[2]))
        for k, (cp, src) in enumerate(copies):
            pltpu.make_async_remote_copy(src_ref=v_ref, dst_ref=all_ref.at[src], send_sem=send.at[k], recv_sem=recv.at[k],
                                         device_id=(x, y, c), device_id_type=MESH).wait_recv()
        for cp, _ in copies:
            cp.wait_send()
        total = all_ref[0]
        for i in range(1, N_DEV):
            total = total + all_ref[i]
        o_ref[...] = total
        loss = 0.5 / d * jnp.sum(total[0:8, :])
        o_ref[0:8, :] = jnp.full((8, 128), loss, F32)

    return pl.pallas_call(
        body, name="all_sum_small",
        out_shape=jax.ShapeDtypeStruct((rows, 128), F32),
        in_specs=[pl.BlockSpec(memory_space=pltpu.VMEM)], out_specs=pl.BlockSpec(memory_space=pltpu.VMEM),
        scratch_shapes=[pltpu.VMEM((N_DEV, rows, 128), F32), pltpu.SemaphoreType.DMA((N_DEV - 1,)),
                        pltpu.SemaphoreType.DMA((N_DEV - 1,))],
        compiler_params=pltpu.CompilerParams(vmem_limit_bytes=VMEM_LIMIT_V7X),
    )(vec)


def _cast_bf16(arrays):
    n = len(arrays)

    def body(*refs):
        for i in range(n):
            refs[n + i][...] = refs[i][...].astype(BF16)

    vm = pl.BlockSpec(memory_space=pltpu.VMEM)
    return pl.pallas_call(body, name="cast_shards", out_shape=[jax.ShapeDtypeStruct(a.shape, BF16) for a in arrays],
                          in_specs=[vm] * n, out_specs=[vm] * n,
                          compiler_params=pltpu.CompilerParams(vmem_limit_bytes=VMEM_LIMIT_V7X))(*arrays)


def _row_tile(rows, cols, itemsize=4, budget=2 << 20):
    t = rows
    while t * cols * itemsize > budget and t % 16 == 0:
        t //= 2
    return t


def _sum_call(ins, out_dtype, name):
    rows, cols = ins[0].shape
    tr = _row_tile(rows, cols)
    spec = pl.BlockSpec((tr, cols), lambda i: (i, 0))
    args, specs = [], []
    for a in ins:
        if a.ndim == 2:
            args.append(a)
            specs.append(spec)
        else:
            for k in range(a.shape[0]):
                args.append(a)
                specs.append(pl.BlockSpec((None, tr, cols), lambda i, k=k: (k, i, 0)))
    n = len(args)

    def body(*refs):
        acc = refs[0][...].astype(F32)
        for r in refs[1:n]:
            acc = acc + r[...].astype(F32)
        refs[n][...] = acc.astype(out_dtype)

    return pl.pallas_call(body, name=name, grid=(rows // tr,), in_specs=specs, out_specs=spec,
                          out_shape=jax.ShapeDtypeStruct((rows, cols), out_dtype), compiler_params=_params("parallel"))(*args)


def _adamw(w, g, m, v, name):
    rows, cols = w.shape
    tr = _row_tile(rows, cols, budget=1 << 20)

    def body(w_ref, g_ref, m_ref, v_ref, d_ref, nm_ref, nv_ref):
        g_ = g_ref[...]
        nm = ADAM_B1 * m_ref[...] + (1.0 - ADAM_B1) * g_
        nv = ADAM_B2 * v_ref[...] + (1.0 - ADAM_B2) * (g_ * g_)
        m_hat = nm / (1.0 - ADAM_B1 ** ADAM_STEP)
        v_hat = nv / (1.0 - ADAM_B2 ** ADAM_STEP)
        d_ref[...] = -ADAM_LR * (m_hat / (jnp.sqrt(v_hat) + ADAM_EPS) + ADAM_WD * w_ref[...])
        nm_ref[...] = nm
        nv_ref[...] = nv

    spec = pl.BlockSpec((tr, cols), lambda i: (i, 0))
    shp = jax.ShapeDtypeStruct((rows, cols), F32)
    return pl.pallas_call(body, name=name, grid=(rows // tr,), in_specs=[spec] * 4, out_specs=[spec] * 3,
                          out_shape=[shp] * 3, compiler_params=_params("parallel"))(w, g, m, v)


def _ada(c, w_ada, b_ada):
    bsz, d = c.shape

    def body(c_ref, w_ref, b_ref, mod_ref, act_ref):
        cc = c_ref[...]
        act = cc * _sigmoid(cc)
        act_ref[...] = act
        mod_ref[...] = jnp.dot(act.astype(BF16), w_ref[...], preferred_element_type=F32) + b_ref[...]

    vm = pl.BlockSpec(memory_space=pltpu.VMEM)
    return pl.pallas_call(body, name="ada", in_specs=[vm] * 3, out_specs=[vm] * 2,
                          out_shape=[jax.ShapeDtypeStruct((bsz, 3 * d), F32), jax.ShapeDtypeStruct((bsz, d), F32)],
                          compiler_params=pltpu.CompilerParams(vmem_limit_bytes=VMEM_LIMIT_V7X))(c, w_ada, b_ada)


def _grad_w_ada(act_t, d_ada):
    d, bsz = act_t.shape
    n3 = d_ada.shape[1]
    tn = n3 // 3

    def body(a_ref, g_ref, o_ref):
        a, g = a_ref[...], g_ref[...]
        acc = a[:, 0:1] * g[0:1, :]
        for b in range(1, bsz):
            acc = acc + a[:, b:b + 1] * g[b:b + 1, :]
        o_ref[...] = acc.astype(BF16)

    return pl.pallas_call(body, name="grad_w_ada", grid=(n3 // tn,),
                          in_specs=[pl.BlockSpec((d, bsz), lambda j: (0, 0)), pl.BlockSpec((bsz, tn), lambda j: (0, j))],
                          out_specs=pl.BlockSpec((d, tn), lambda j: (0, j)),
                          out_shape=jax.ShapeDtypeStruct((d, n3), BF16), compiler_params=_params("parallel"))(act_t, d_ada)


def _permute_w_in(w_nat, lay):
    d = lay.d

    def copy_body(src_ref, dst_in_ref, dst_ref):
        del dst_in_ref
        dst_ref[...] = src_ref[...]

    def first_body(src_ref, dst_ref):
        dst_ref[...] = src_ref[...]

    w_all = pl.pallas_call(
        first_body, name="permute_w_attn", grid=(2 * PAIR_SLABS,),
        in_specs=[pl.BlockSpec((d, SLAB), lambda s: (0, lay.attn_nat_slab(s)))],
        out_specs=pl.BlockSpec((d, SLAB), lambda s: (0, s)),
        out_shape=jax.ShapeDtypeStruct((d, lay.np), BF16), compiler_params=_params("arbitrary"))(w_nat)
    n_rest = 6 * d // CONV_TILE
    base = lay.c0 // CONV_TILE
    return pl.pallas_call(
        copy_body, name="permute_w_rest", grid=(n_rest,),
        in_specs=[pl.BlockSpec((d, CONV_TILE), lambda t: (0, lay.rest_nat_tile(t))), pl.BlockSpec(memory_space=pl.ANY)],
        out_specs=pl.BlockSpec((d, CONV_TILE), lambda t: (0, base + t)),
        out_shape=jax.ShapeDtypeStruct((d, lay.np), BF16), input_output_aliases={1: 0},
        compiler_params=_params("arbitrary"))(w_nat, w_all)


def _project(x2, mod3, w_all, b_all, seq, col0, ncols, tn, out_dtype, want_ht, name):
    t, d = x2.shape
    tm = min(1024, seq)
    per_seq = seq // tm
    j0 = col0 // tn

    def body(x_ref, mod_ref, w_ref, b_ref, o_ref, *rest):
        h_ref = rest[-1]

        @pl.when(pl.program_id(1) == 0)
        def _():
            h = x_ref[...] * (1.0 + mod_ref[:, d:2 * d]) + mod_ref[:, 0:d]
            h_ref[...] = h.astype(BF16)
            if want_ht:
                rest[0][...] = h.T.astype(BF16)

        o_ref[...] = (jnp.dot(h_ref[...], w_ref[...], preferred_element_type=F32) + b_ref[...]).astype(out_dtype)

    out_shape = [jax.ShapeDtypeStruct((t, ncols), out_dtype)]
    out_specs = [pl.BlockSpec((tm, tn), lambda i, j: (i, j))]
    if want_ht:
        out_shape.append(jax.ShapeDtypeStruct((d, t), BF16))
        out_specs.append(pl.BlockSpec((d, tm), lambda i, j: (0, i)))
    return pl.pallas_call(
        body, name=name, grid=(t // tm, ncols // tn),
        in_specs=[pl.BlockSpec((tm, d), lambda i, j: (i, 0)),
                  pl.BlockSpec((None, 1, 3 * d), lambda i, j: (i // per_seq, 0, 0)),
                  pl.BlockSpec((d, tn), lambda i, j: (0, j0 + j)),
                  pl.BlockSpec((1, tn), lambda i, j: (0, j0 + j))],
        out_specs=out_specs, out_shape=out_shape,
        scratch_shapes=[pltpu.VMEM((tm, d), BF16)],
        compiler_params=_params("arbitrary", "arbitrary"))(x2, mod3, w_all, b_all)


def _slope(g, p, hh):
    head = 4 * g + 2 * p + hh
    return 2.0 ** (-ALIBI_MAX_EXP * (head + 1.0) / N_HEADS)


def _ld(ref, start, n, stride):
    if stride == 1:
        return ref[pl.ds(start, n), :]
    return ref[pl.ds(start, n, stride=stride), :]


def _st(ref, start, n, stride, val):
    if stride == 1:
        ref[pl.ds(start, n), :] = val
    else:
        ref[pl.ds(start, n, stride=stride), :] = val


def _block_place(it, g, seq):
    dil = DILATIONS[g]
    nb = seq // dil // SUB
    if nb == 1:
        return it, it, SUB, 0
    r, blk = it // nb, it % nb
    kb = jnp.maximum(blk - 1, 0)
    base = r + dil * SUB * blk
    kstart = r + dil * SUB * kb
    if dil == 1:
        base, kstart = pl.multiple_of(base, SUB), pl.multiple_of(kstart, SUB)
    return base, kstart, 2 * SUB, (blk - kb) * SUB


def _masked_scores(qh, kh, delta, slope_dil):
    s = lax.dot_general(qh, kh, (((1,), (1,)), ((), ())), preferred_element_type=F32) * (HEAD_DIM ** -0.5)
    s = s - slope_dil * delta.astype(F32)
    return jnp.where((delta >= 0) & (delta <= SUB), s, NEG)


def _unpack_group(pa_ref, slabs, g):
    for w in range(3):
        s = 3 * w + g
        slabs[w] = pa_ref[:, s * SLAB:(s + 1) * SLAB].astype(F32)


def _head_bcast(col0, col1, rows):
    return jnp.concatenate([jnp.broadcast_to(col0, (rows, HEAD_DIM)), jnp.broadcast_to(col1, (rows, HEAD_DIM))], axis=-1)


def _head_sums(t):
    rows = t.shape[0]
    return _head_bcast(jnp.sum(t[:, :HEAD_DIM], axis=-1, keepdims=True), jnp.sum(t[:, HEAD_DIM:], axis=-1, keepdims=True), rows)


def _attn_fwd(pa, bsz, seq):
    t = pa.shape[0]
    n_blocks = seq // SUB
    chunk = 256

    def body(pa_ref, o_ref, lse_ref, a_ref, slabs):
        p = pl.program_id(1)
        for g in range(N_GROUPS):
            dil = DILATIONS[g]
            _unpack_group(pa_ref, slabs, g)
            nk = SUB if seq // dil // SUB == 1 else 2 * SUB
            base_delta = (lax.broadcasted_iota(jnp.int32, (SUB, nk), 0) - lax.broadcasted_iota(jnp.int32, (SUB, nk), 1))

            def block(it, carry, g=g, dil=dil, nk=nk, base_delta=base_delta):
                base, kstart, _, off = _block_place(it, g, seq)
                delta = base_delta + off
                q = _ld(slabs.at[0], base, SUB, dil)
                kw = _ld(slabs.at[1], kstart, nk, dil)
                vw = _ld(slabs.at[2], kstart, nk, dil)
                outs, lses = [], []
                for hh in range(2):
                    cols = slice(hh * HEAD_DIM, (hh + 1) * HEAD_DIM)
                    slope = jnp.where(p == 0, _slope(g, 0, hh), _slope(g, 1, hh)) * dil
                    s = _masked_scores(q[:, cols].astype(BF16), kw[:, cols].astype(BF16), delta, slope)
                    m = jnp.max(s, axis=-1, keepdims=True)
                    e = jnp.exp(s - m)
                    den = jnp.sum(e, axis=-1, keepdims=True)
                    outs.append(jnp.dot((e / den).astype(BF16), vw[:, cols].astype(BF16), preferred_element_type=F32))
                    lses.append(m + jnp.log(den))
                _st(o_ref.at[g], base, SUB, dil, jnp.concatenate(outs, axis=-1))
                _st(lse_ref.at[g], base, SUB, dil, _head_bcast(lses[0], lses[1], SUB))
                return carry

            lax.fori_loop(0, n_blocks, block, 0)

        def mix(i, carry):
            rows = pl.ds(pl.multiple_of(i * chunk, chunk), chunk)
            l0, l1, l2 = lse_ref[0, rows, :], lse_ref[1, rows, :], lse_ref[2, rows, :]
            m = jnp.maximum(jnp.maximum(l0, l1), l2)
            e0, e1, e2 = jnp.exp(l0 - m), jnp.exp(l1 - m), jnp.exp(l2 - m)
            tot = e0 + e1 + e2
            o = (e0 / tot) * o_ref[0, rows, :] + (e1 / tot) * o_ref[1, rows, :] + (e2 / tot) * o_ref[2, rows, :]
            z = pa_ref[rows, 9 * SLAB:10 * SLAB].astype(F32)
            a_ref[rows, :] = (o * (z * _sigmoid(z))).astype(BF16)
            return carry

        lax.fori_loop(0, seq // chunk, mix, 0)

    big = jax.ShapeDtypeStruct((N_GROUPS, t, 2 * SLAB), F32)
    return pl.pallas_call(
        body, name="attn_fwd", grid=(bsz, 2),
        in_specs=[pl.BlockSpec((seq, PAIR_COLS), lambda b, p: (b, p))],
        out_specs=[pl.BlockSpec((N_GROUPS, seq, SLAB), lambda b, p: (0, b, p)),
                   pl.BlockSpec((N_GROUPS, seq, SLAB), lambda b, p: (0, b, p)),
                   pl.BlockSpec((seq, SLAB), lambda b, p: (b, p))],
        out_shape=[big, big, jax.ShapeDtypeStruct((t, 2 * SLAB), BF16)],
        scratch_shapes=[pltpu.VMEM((3, seq, SLAB), F32)],
        compiler_params=_params("arbitrary", "arbitrary"))(pa)


def _shift_down(v, k):
    rows = lax.broadcasted_iota(jnp.int32, v.shape, 0)
    return jnp.where(rows >= k, pltpu.roll(v, k, 0), 0.0)


def _shift_up(v, k):
    n = v.shape[0]
    rows = lax.broadcasted_iota(jnp.int32, v.shape, 0)
    return jnp.where(rows < n - k, pltpu.roll(v, n - k, 0), 0.0)


def _conv_fwd(pr, conv_w, bsz, seq, d):
    t = pr.shape[0]
    ct = CONV_TILE

    def body(p_ref, cw_ref, o_ref):
        u = p_ref[:, 2 * ct:3 * ct] * p_ref[:, 0:ct]
        cw = cw_ref[...]
        conv = cw[0:1, :] * _shift_down(u, 2)
        conv = conv + cw[1:2, :] * _shift_down(u, 1)
        conv = conv + cw[2:3, :] * u
        z = p_ref[:, 3 * ct:4 * ct]
        o_ref[...] = (p_ref[:, ct:2 * ct] * conv * (z * _sigmoid(z))).astype(BF16)

    return pl.pallas_call(
        body, name="conv_fwd", grid=(bsz, d // ct),
        in_specs=[pl.BlockSpec((seq, 4 * ct), lambda b, j: (b, j)), pl.BlockSpec((3, ct), lambda b, j: (0, j))],
        out_specs=pl.BlockSpec((seq, ct), lambda b, j: (b, j)),
        out_shape=jax.ShapeDtypeStruct((t, d), BF16), compiler_params=_params("parallel", "parallel"))(pr, conv_w)


def _tail(a_in, b_in, pr, x2, target2, mod3, w_pa, w_pc, w_out, b_out, ln_g, ln_b, seq, lay):
    t, d = x2.shape
    tm = 256
    per_seq = seq // tm
    n_steps = t // tm
    gate_blk = 4 * d // d

    def nt(a, b):
        return lax.dot_general(a, b, (((1,), (1,)), ((), ())), preferred_element_type=F32)

    def tn(a, b):
        return lax.dot_general(a, b, (((0,), (0,)), ((), ())), preferred_element_type=F32)

    def body(a_ref, b_ref, ga_ref, gb_ref, x_ref, tg_ref, mod_ref, wpa_ref, wpc_ref, wo_ref, bo_ref, lg_ref, lb_ref,
             dpg_ref, da_ref, db_ref, gx_ref, dgate_ref, small_ref, gwpa_hbm, gwpc_hbm, gwo_hbm,
             acc_pa, acc_pc, acc_o, sem):
        i = pl.program_id(0)

        @pl.when(i == 0)
        def _():
            acc_pa[...] = jnp.zeros_like(acc_pa)
            acc_pc[...] = jnp.zeros_like(acc_pc)
            acc_o[...] = jnp.zeros_like(acc_o)
            small_ref[...] = jnp.zeros_like(small_ref)

        @pl.when(i % per_seq == 0)
        def _():
            dgate_ref[...] = jnp.zeros_like(dgate_ref)

        a_bf, b_bf = a_ref[...], b_ref[...]
        y_attn = jnp.dot(a_bf, wpa_ref[...], preferred_element_type=F32)
        y_conv = jnp.dot(b_bf, wpc_ref[...], preferred_element_type=F32)
        sa, sb = _sigmoid(ga_ref[...]), _sigmoid(gb_ref[...])
        merged = (sa * y_attn + sb * y_conv).astype(BF16)
        mo = jnp.dot(merged, wo_ref[...], preferred_element_type=F32) + bo_ref[...]
        gate = mod_ref[:, 2 * d:3 * d]
        r = ALPHA * x_ref[...] + gate * mo
        mu = jnp.mean(r, axis=-1, keepdims=True)
        cen = r - mu
        var = jnp.mean(cen * cen, axis=-1, keepdims=True)
        rstd = lax.rsqrt(var + LN_EPS)
        xhat = cen * rstd
        err = xhat * lg_ref[...] + lb_ref[...] - tg_ref[...]
        dy = err * (1.0 / d)
        dxhat = dy * lg_ref[...]
        dr = rstd * (dxhat - jnp.mean(dxhat, axis=-1, keepdims=True) - xhat * jnp.mean(dxhat * xhat, axis=-1, keepdims=True))
        gx_ref[...] = ALPHA * dr
        dgate_ref[...] += jnp.sum(dr * mo, axis=0, keepdims=True)
        d_mo = dr * gate
        small_ref[0:1, :] += jnp.sum(d_mo, axis=0, keepdims=True)
        small_ref[1:2, :] += jnp.sum(dy * xhat, axis=0, keepdims=True)
        small_ref[2:3, :] += jnp.sum(dy, axis=0, keepdims=True)
        small_ref[3:4, :] += jnp.sum(err * err, axis=0, keepdims=True)
        d_mo_bf = d_mo.astype(BF16)
        acc_o[...] += tn(merged, d_mo_bf)
        dmerged = nt(d_mo_bf, wo_ref[...])
        dy_attn = (dmerged * sa).astype(BF16)
        dy_conv = (dmerged * sb).astype(BF16)
        dpg_ref[:, 0:d] = (dmerged * y_attn * sa * (1.0 - sa)).astype(BF16)
        dpg_ref[:, d:2 * d] = (dmerged * y_conv * sb * (1.0 - sb)).astype(BF16)
        acc_pa[...] += tn(a_bf, dy_attn)
        acc_pc[...] += tn(b_bf, dy_conv)
        da_ref[...] = nt(dy_attn, wpa_ref[...])
        db_ref[...] = nt(dy_conv, wpc_ref[...])

        @pl.when(i == n_steps - 1)
        def _():
            copies = [pltpu.make_async_copy(acc_pa, gwpa_hbm, sem.at[0]), pltpu.make_async_copy(acc_pc, gwpc_hbm, sem.at[1]),
                      pltpu.make_async_copy(acc_o, gwo_hbm, sem.at[2])]
            for cp in copies:
                cp.start()
            for cp in copies:
                cp.wait()

    row = lambda w: pl.BlockSpec((tm, w), lambda i: (i, 0))
    const = lambda shp: pl.BlockSpec(shp, lambda i: (0,) * len(shp), pipeline_mode=pl.Buffered(1))
    any_spec = pl.BlockSpec(memory_space=pl.ANY)
    return pl.pallas_call(
        body, name="tail", grid=(n_steps,),
        in_specs=[row(Z_WIDTH), row(d),
                  pl.BlockSpec((tm, d), lambda i: (i, gate_blk)), pl.BlockSpec((tm, d), lambda i: (i, gate_blk + 1)),
                  row(d), row(d), pl.BlockSpec((None, 1, 3 * d), lambda i: (i // per_seq, 0, 0)),
                  const((Z_WIDTH, d)), const((d, d)), const((d, d)), const((1, d)), const((1, d)), const((1, d))],
        out_specs=[pl.BlockSpec((tm, 2 * d), lambda i: (i, lay.g0 // (2 * d))), row(Z_WIDTH), row(d), row(d),
                   pl.BlockSpec((None, 1, d), lambda i: (i // per_seq, 0, 0)), pl.BlockSpec((8, d), lambda i: (0, 0)),
                   any_spec, any_spec, any_spec],
        out_shape=[jax.ShapeDtypeStruct((t, lay.np), BF16), jax.ShapeDtypeStruct((t, Z_WIDTH), F32),
                   jax.ShapeDtypeStruct((t, d), F32), jax.ShapeDtypeStruct((t, d), F32),
                   jax.ShapeDtypeStruct((t // seq, 1, d), F32), jax.ShapeDtypeStruct((8, d), F32),
                   jax.ShapeDtypeStruct((Z_WIDTH, d), F32), jax.ShapeDtypeStruct((d, d), F32),
                   jax.ShapeDtypeStruct((d, d), F32)],
        scratch_shapes=[pltpu.VMEM((Z_WIDTH, d), F32), pltpu.VMEM((d, d), F32), pltpu.VMEM((d, d), F32),
                        pltpu.SemaphoreType.DMA((3,))],
        compiler_params=_params("arbitrary"),
    )(a_in, b_in, pr, pr, x2, target2, mod3, w_pa, w_pc, w_out, b_out, ln_g, ln_b)


def _conv_bwd(dproj, db, pr, conv_w, bsz, seq, lay):
    d = lay.d
    ct = CONV_TILE
    base = lay.c0 // (4 * ct)

    def body(dp_in, db_ref, p_ref, cw_ref, dp_ref, gcw_ref):
        del dp_in
        u_x, g_b, g_c, z = p_ref[:, 0:ct], p_ref[:, ct:2 * ct], p_ref[:, 2 * ct:3 * ct], p_ref[:, 3 * ct:4 * ct]
        cw = cw_ref[...]
        u = g_c * u_x
        u1, u2 = _shift_down(u, 1), _shift_down(u, 2)
        conv = cw[0:1, :] * u2 + cw[1:2, :] * u1 + cw[2:3, :] * u
        sig = _sigmoid(z)
        sl = z * sig
        dbv = db_ref[...]
        gbc = g_b * conv
        dp_ref[:, ct:2 * ct] = (dbv * sl * conv).astype(BF16)
        dp_ref[:, 3 * ct:4 * ct] = (dbv * gbc * (sig * (1.0 + z * (1.0 - sig)))).astype(BF16)
        dconv = dbv * sl * g_b

        @pl.when(pl.program_id(1) == 0)
        def _():
            gcw_ref[...] = jnp.zeros_like(gcw_ref)

        gcw_ref[0:1, :] += jnp.sum(dconv * u2, axis=0, keepdims=True)
        gcw_ref[1:2, :] += jnp.sum(dconv * u1, axis=0, keepdims=True)
        gcw_ref[2:3, :] += jnp.sum(dconv * u, axis=0, keepdims=True)
        du = cw[2:3, :] * dconv + cw[1:2, :] * _shift_up(dconv, 1) + cw[0:1, :] * _shift_up(dconv, 2)
        dp_ref[:, 0:ct] = (du * g_c).astype(BF16)
        dp_ref[:, 2 * ct:3 * ct] = (du * u_x).astype(BF16)

    return pl.pallas_call(
        body, name="conv_bwd", grid=(d // ct, bsz),
        in_specs=[pl.BlockSpec(memory_space=pl.ANY), pl.BlockSpec((seq, ct), lambda j, b: (b, j)),
                  pl.BlockSpec((seq, 4 * ct), lambda j, b: (b, j)), pl.BlockSpec((3, ct), lambda j, b: (0, j))],
        out_specs=[pl.BlockSpec((seq, 4 * ct), lambda j, b: (b, base + j)), pl.BlockSpec((8, ct), lambda j, b: (0, j))],
        out_shape=[jax.ShapeDtypeStruct(dproj.shape, BF16), jax.ShapeDtypeStruct((8, d), F32)],
        input_output_aliases={0: 0}, compiler_params=_params("arbitrary", "arbitrary"))(dproj, db, pr, conv_w)


def _attn_bwd(dproj, pa, o_all, lse_all, da, bsz, seq):
    n_blocks = seq // SUB
    chunk = 256

    def body(dp_in, pa_ref, o_ref, lse_ref, da_ref, dp_ref, slabs, dslabs, dog, cvec):
        del dp_in
        p = pl.program_id(1)

        def mix_bwd(i, carry):
            rows = pl.ds(pl.multiple_of(i * chunk, chunk), chunk)
            ls = [lse_ref[g, rows, :] for g in range(N_GROUPS)]
            os_ = [o_ref[g, rows, :] for g in range(N_GROUPS)]
            m = jnp.maximum(jnp.maximum(ls[0], ls[1]), ls[2])
            es = [jnp.exp(l - m) for l in ls]
            tot = es[0] + es[1] + es[2]
            ws = [e / tot for e in es]
            o = ws[0] * os_[0] + ws[1] * os_[1] + ws[2] * os_[2]
            z = pa_ref[rows, 9 * SLAB:10 * SLAB].astype(F32)
            sig = _sigmoid(z)
            dav = da_ref[rows, :]
            do = dav * (z * sig)
            dp_ref[rows, 9 * SLAB:10 * SLAB] = (dav * o * (sig * (1.0 + z * (1.0 - sig)))).astype(BF16)
            wsum = _head_sums(do * o)
            for g in range(N_GROUPS):
                dog[g, rows, :] = ws[g] * do
                cvec[g, rows, :] = -(ws[g] * wsum)
            return carry

        lax.fori_loop(0, seq // chunk, mix_bwd, 0)

        for g in range(N_GROUPS):
            dil = DILATIONS[g]
            _unpack_group(pa_ref, slabs, g)
            dslabs[...] = jnp.zeros_like(dslabs)
            nk = SUB if seq // dil // SUB == 1 else 2 * SUB
            base_delta = (lax.broadcasted_iota(jnp.int32, (SUB, nk), 0) - lax.broadcasted_iota(jnp.int32, (SUB, nk), 1))

            def block(it, carry, g=g, dil=dil, nk=nk, base_delta=base_delta):
                base, kstart, _, off = _block_place(it, g, seq)
                delta = base_delta + off
                q = _ld(slabs.at[0], base, SUB, dil)
                kw = _ld(slabs.at[1], kstart, nk, dil)
                vw = _ld(slabs.at[2], kstart, nk, dil)
                do = _ld(dog.at[g], base, SUB, dil)
                cv = _ld(cvec.at[g], base, SUB, dil)
                lse = _ld(lse_ref.at[g], base, SUB, dil)
                dqs, dks, dvs = [], [], []
                for hh in range(2):
                    cols = slice(hh * HEAD_DIM, (hh + 1) * HEAD_DIM)
                    lane = hh * HEAD_DIM
                    slope = jnp.where(p == 0, _slope(g, 0, hh), _slope(g, 1, hh)) * dil
                    qh, kh, vh = q[:, cols].astype(BF16), kw[:, cols].astype(BF16), vw[:, cols].astype(BF16)
                    doh = do[:, cols].astype(BF16)
                    s = _masked_scores(qh, kh, delta, slope)
                    prob = jnp.exp(s - lse[:, lane:lane + 1])
                    dprob = lax.dot_general(doh, vh, (((1,), (1,)), ((), ())), preferred_element_type=F32)
                    ds = (prob * (dprob + cv[:, lane:lane + 1]) * (HEAD_DIM ** -0.5)).astype(BF16)
                    dqs.append(jnp.dot(ds, kh, preferred_element_type=F32))
                    dks.append(lax.dot_general(ds, qh, (((0,), (0,)), ((), ())), preferred_element_type=F32))
                    dvs.append(lax.dot_general(prob.astype(BF16), doh, (((0,), (0,)), ((), ())), preferred_element_type=F32))
                _st(dslabs.at[0], base, SUB, dil, jnp.concatenate(dqs, axis=-1))
                dk_old = _ld(dslabs.at[1], kstart, nk, dil)
                _st(dslabs.at[1], kstart, nk, dil, dk_old + jnp.concatenate(dks, axis=-1))
                dv_old = _ld(dslabs.at[2], kstart, nk, dil)
                _st(dslabs.at[2], kstart, nk, dil, dv_old + jnp.concatenate(dvs, axis=-1))
                return carry

            lax.fori_loop(0, n_blocks, block, 0)
            for w in range(3):
                s = 3 * w + g
                dp_ref[:, s * SLAB:(s + 1) * SLAB] = dslabs[w].astype(BF16)

    return pl.pallas_call(
        body, name="attn_bwd", grid=(bsz, 2),
        in_specs=[pl.BlockSpec(memory_space=pl.ANY), pl.BlockSpec((seq, PAIR_COLS), lambda b, p: (b, p)),
                  pl.BlockSpec((N_GROUPS, seq, SLAB), lambda b, p: (0, b, p)),
                  pl.BlockSpec((N_GROUPS, seq, SLAB), lambda b, p: (0, b, p)),
                  pl.BlockSpec((seq, SLAB), lambda b, p: (b, p))],
        out_specs=pl.BlockSpec((seq, PAIR_COLS), lambda b, p: (b, p)),
        out_shape=jax.ShapeDtypeStruct(dproj.shape, BF16), input_output_aliases={0: 0},
        scratch_shapes=[pltpu.VMEM((3, seq, SLAB), F32), pltpu.VMEM((3, seq, SLAB), F32),
                        pltpu.VMEM((3, seq, SLAB), F32), pltpu.VMEM((3, seq, SLAB), F32)],
        compiler_params=_params("arbitrary", "arbitrary"))(dproj, pa, o_all, lse_all, da)


def _real_k_tile(kk, lay, tk):
    return jnp.where(kk < ATT // tk, kk, kk + (lay.c0 - ATT) // tk)


def _grad_h(dproj, w_all, gx0, x2, mod3, seq, lay):
    t, d = x2.shape
    tm = min(1024, seq)
    tk = 512
    per_seq = seq // tm
    nk = (ATT + 6 * d) // tk

    def body(dp_ref, w_ref, gx0_ref, x_ref, mod_ref, gx_ref, dmod_ref, acc):
        i, kk = pl.program_id(0), pl.program_id(1)

        @pl.when(kk == 0)
        def _():
            acc[...] = jnp.zeros_like(acc)

        acc[...] += lax.dot_general(dp_ref[...], w_ref[...], (((1,), (1,)), ((), ())), preferred_element_type=F32)

        @pl.when(kk == nk - 1)
        def _():
            dh = acc[...]
            gx_ref[...] = gx0_ref[...] + dh * (1.0 + mod_ref[:, d:2 * d])

            @pl.when(i % per_seq == 0)
            def _():
                dmod_ref[...] = jnp.zeros_like(dmod_ref)

            dmod_ref[0:1, :] += jnp.sum(dh, axis=0, keepdims=True)
            dmod_ref[1:2, :] += jnp.sum(dh * x_ref[...], axis=0, keepdims=True)

    return pl.pallas_call(
        body, name="grad_h", grid=(t // tm, nk),
        in_specs=[pl.BlockSpec((tm, tk), lambda i, k: (i, _real_k_tile(k, lay, tk))),
                  pl.BlockSpec((d, tk), lambda i, k: (0, _real_k_tile(k, lay, tk))),
                  pl.BlockSpec((tm, d), lambda i, k: (i, 0)), pl.BlockSpec((tm, d), lambda i, k: (i, 0)),
                  pl.BlockSpec((None, 1, 3 * d), lambda i, k: (i // per_seq, 0, 0))],
        out_specs=[pl.BlockSpec((tm, d), lambda i, k: (i, 0)), pl.BlockSpec((None, 8, d), lambda i, k: (i // per_seq, 0, 0))],
        out_shape=[jax.ShapeDtypeStruct((t, d), F32), jax.ShapeDtypeStruct((t // seq, 8, d), F32)],
        scratch_shapes=[pltpu.VMEM((tm, d), F32)],
        compiler_params=_params("arbitrary", "arbitrary"))(dproj, w_all, gx0, x2, mod3)


def _grad_w_in(ht, dproj, seq, lay):
    d, t = ht.shape
    tm = seq
    n_i = t // tm

    def make_body(aliased):
        def body(*refs):
            if aliased:
                refs = refs[2:]
            ht_ref, dp_ref, gw_ref, gb_ref, acc, bacc = refs
            i = pl.program_id(1)

            @pl.when(i == 0)
            def _():
                acc[...] = jnp.zeros_like(acc)
                bacc[...] = jnp.zeros_like(bacc)

            dp = dp_ref[...]
            acc[...] += jnp.dot(ht_ref[...], dp, preferred_element_type=F32)
            bacc[...] += jnp.sum(dp.astype(F32), axis=0, keepdims=True)

            @pl.when(i == n_i - 1)
            def _():
                gw_ref[...] = acc[...].astype(BF16)
                gb_ref[...] = bacc[...]
        return body

    def call(name, tn, n_tiles, perm_tile, nat_tile, prev):
        in_specs = [pl.BlockSpec((d, tm), lambda j, i: (0, i)), pl.BlockSpec((tm, tn), lambda j, i: (i, perm_tile(j)))]
        args = [ht, dproj]
        aliases = {}
        if prev is not None:
            in_specs = [pl.BlockSpec(memory_space=pl.ANY)] * 2 + in_specs
            args = list(prev) + args
            aliases = {0: 0, 1: 1}
        return pl.pallas_call(
            make_body(prev is not None), name=name, grid=(n_tiles, n_i), in_specs=in_specs,
            out_specs=[pl.BlockSpec((d, tn), lambda j, i: (0, nat_tile(j))), pl.BlockSpec((1, tn), lambda j, i: (0, nat_tile(j)))],
            out_shape=[jax.ShapeDtypeStruct((d, lay.din), BF16), jax.ShapeDtypeStruct((1, lay.din), F32)],
            input_output_aliases=aliases,
            scratch_shapes=[pltpu.VMEM((d, tn), F32), pltpu.VMEM((1, tn), F32)],
            compiler_params=_params("arbitrary", "arbitrary"))(*args)

    first = call("grad_w_in_attn", SLAB, 2 * PAIR_SLABS, lambda j: j, lay.attn_nat_slab, None)
    base = lay.c0 // CONV_TILE
    return call("grad_w_in_rest", CONV_TILE, 6 * d // CONV_TILE, lambda j: base + j, lay.rest_nat_tile, first)


def _pack_rows(parts, width=128):
    flat = [p.reshape(-1) for p in parts]
    spans, rows = [], 0
    padded = []
    for f in flat:
        n = -(-f.shape[0] // (8 * width)) * 8
        padded.append(jnp.pad(f, (0, n * width - f.shape[0])).reshape(n, width))
        spans.append((rows, f.shape[0]))
        rows += n
    return jnp.concatenate(padded, axis=0), spans


def _unpack_rows(packed, spans, shapes, width=128):
    out = []
    for (row, n), shp in zip(spans, shapes):
        rows = -(-n // width)
        out.append(packed[row:row + rows].reshape(-1)[:n].reshape(shp))
    return out


def kernel(x, c, w_ada, b_ada, w_in, b_in, conv_w, w_proj_attn, w_proj_conv, w_out, b_out, ln_g, ln_b, loss_target, m_w_ada, m_b_ada, m_w_in, m_b_in, m_conv_w, m_w_proj_attn, m_w_proj_conv, m_w_out, m_b_out, m_ln_g, m_ln_b, v_w_ada, v_b_ada, v_w_in, v_b_in, v_conv_w, v_w_proj_attn, v_w_proj_conv, v_w_out, v_b_out, v_ln_g, v_ln_b):
    bsz, seq, d = x.shape
    t = bsz * seq
    lay = _Layout(d)
    col_sharded = [True, True, True, False, False]
    big_w = [w_ada[0], w_in[0], w_proj_attn[0], w_proj_conv[0], w_out[0]]
    big_m = [m_w_ada[0], m_w_in[0], m_w_proj_attn[0], m_w_proj_conv[0], m_w_out[0]]
    big_v = [v_w_ada[0], v_w_in[0], v_w_proj_attn[0], v_w_proj_conv[0], v_w_out[0]]
    chip = 2 * lax.axis_index("x") + lax.axis_index("y")

    cw_pad = jnp.pad(conv_w[0], ((0, 5), (0, 0)))
    (wa_f, wi_f, wpa_f, wpc_f, wo_f), cw8 = _gather_weights(_cast_bf16(big_w), col_sharded, cw_pad)
    cw_full = cw8[0:3]
    w_all = _permute_w_in(wi_f, lay)
    b_all = lay.perm_vector(b_in)

    x2 = x.reshape(t, d)
    target2 = loss_target.reshape(t, d)
    mod, c_act = _ada(c, wa_f, b_ada)
    mod3 = mod.reshape(bsz, 1, 3 * d)
    pa, = _project(x2, mod3, w_all, b_all, seq, 0, ATT, 512, BF16, False, "project_attn")
    pr, ht = _project(x2, mod3, w_all, b_all, seq, lay.c0, 6 * d, 512, F32, True, "project_rest")
    o_all, lse_all, a_in = _attn_fwd(pa, bsz, seq)
    b_in_act = _conv_fwd(pr, cw_full, bsz, seq, d)
    (dproj, da_in, db_in, gx0, dgate, small_tail, gw_pa, gw_pc, gw_out) = _tail(
        a_in, b_in_act, pr, x2, target2, mod3, wpa_f, wpc_f, wo_f, b_out, ln_g, ln_b, seq, lay)

    dproj, gcw = _conv_bwd(dproj, db_in, pr, cw_full, bsz, seq, lay)
    dproj = _attn_bwd(dproj, pa, o_all, lse_all, da_in, bsz, seq)
    grad_x2, dmod = _grad_h(dproj, w_all, gx0, x2, mod3, seq, lay)
    gw_in_bf, gb_in = _grad_w_in(ht, dproj, seq, lay)
    d_ada = jnp.concatenate([dmod[:, 0, :], dmod[:, 1, :], dgate[:, 0, :]], axis=1)
    gw_ada_bf = _grad_w_ada(c_act.T, d_ada)

    pieces = [small_tail[3], jnp.sum(d_ada, axis=0), gb_in[0], small_tail[0], small_tail[1], small_tail[2], gcw[0:3]]
    packed, spans = _pack_rows(pieces)
    summed = _all_sum_small(packed, d)
    loss = summed[0, 0]
    _, g_b_ada, g_b_in, g_b_out, g_ln_g, g_ln_b, g_cw_full = _unpack_rows(
        summed, spans, [(d,), (3 * d,), (lay.din,), (d,), (d,), (d,), (3, d)])
    g_cw = lax.dynamic_slice(g_cw_full, (0, chip * (d // N_CHIPS)), (3, d // N_CHIPS))

    grads = [gw_ada_bf, gw_in_bf, gw_pa, gw_pc, gw_out]
    own1, got1 = _pair_exchange_halves(grads, col_sharded)
    parts = []
    for w in range(5):
        shp = own1[w].shape
        flat = (-1, shp[-1])
        parts.append(_sum_call([own1[w].reshape(flat), got1[w].reshape(flat)], BF16, f"grads_pair_sum_{w}").reshape(shp))
    own2, got2 = _chip_scatter(parts, col_sharded)
    halves = [_sum_call([own2[w], got2[w]], F32, f"grads_chip_sum_{w}") for w in range(5)]
    g_big = _pair_join_halves(halves)

    big_out = [_adamw(big_w[w], g_big[w], big_m[w], big_v[w], f"adamw_{w}") for w in range(5)]
    small_w = [b_ada, b_in, conv_w[0], b_out, ln_g, ln_b]
    small_g = [g_b_ada, g_b_in, g_cw, g_b_out, g_ln_g, g_ln_b]
    small_m = [m_b_ada, m_b_in, m_conv_w[0], m_b_out, m_ln_g, m_ln_b]
    small_v = [v_b_ada, v_b_in, v_conv_w[0], v_b_out, v_ln_g, v_ln_b]
    pw, sp = _pack_rows(small_w)
    pg, _ = _pack_rows(small_g)
    pm, _ = _pack_rows(small_m)
    pv, _ = _pack_rows(small_v)
    sd, sm, sv = _adamw(pw, pg, pm, pv, "adamw_small")
    shapes = [a.shape for a in small_w]
    sd, sm, sv = _unpack_rows(sd, sp, shapes), _unpack_rows(sm, sp, shapes), _unpack_rows(sv, sp, shapes)

    def order(wa, bA, wi, bI, cw, wpa, wpc, wo, bO, lg, lb):
        return (wa[None], bA, wi[None], bI, cw[None], wpa[None], wpc[None], wo[None], bO, lg, lb)

    sg = [g.reshape(s) for g, s in zip(small_g, shapes)]
    grads_out = order(g_big[0], sg[0], g_big[1], sg[1], sg[2], g_big[2], g_big[3], g_big[4], sg[3], sg[4], sg[5])
    outs = []
    for idx, small in enumerate((sd, sm, sv)):
        outs.append(order(big_out[0][idx], small[0], big_out[1][idx], small[1], small[2], big_out[2][idx],
                          big_out[3][idx], big_out[4][idx], small[3], small[4], small[5]))
    return (loss, grad_x2.reshape(bsz, seq, d), *grads_out, *outs[0], *outs[1], *outs[2])
```

```python
import functools

import jax
import jax.numpy as jnp
from jax import lax
from jax.experimental import pallas as pl
from jax.experimental.pallas import tpu as pltpu

F32 = jnp.float32
BF16 = jnp.bfloat16
MESH = pl.DeviceIdType.MESH

HEAD_DIM = 64
N_GROUPS = 3
DILATIONS = (1, 4, 16)
N_HEADS = 12
SUB = 128
Q_WIDTH = 768
Z_WIDTH = 256
ATT = 3 * Q_WIDTH + Z_WIDTH
SLAB = 128
PAIR_SLABS = 10
PAIR_COLS = PAIR_SLABS * SLAB
CONV_TILE = 256
ALIBI_MAX_EXP = 8.0
ALPHA = 2.0 ** 0.25
LN_EPS = 1e-5
ADAM_LR, ADAM_B1, ADAM_B2, ADAM_EPS, ADAM_WD, ADAM_STEP = 0.001, 0.9, 0.999, 1e-08, 0.01, 10
N_CHIPS = 4
N_DEV = 8
VMEM_LIMIT_V7X = 60 * 1024 * 1024
NEG = -1e30


def _params(*sem):
    return pltpu.CompilerParams(dimension_semantics=sem, vmem_limit_bytes=VMEM_LIMIT_V7X)


def _sigmoid(v):
    return 1.0 / (1.0 + jnp.exp(-v))


class _Layout:
    def __init__(self, d):
        self.d = d
        self.din = ATT + 6 * d
        c0 = 3072
        while c0 % (2 * d):
            c0 += 1024
        self.c0, self.g0, self.np = c0, c0 + 4 * d, c0 + 6 * d
        self.n_conv_tiles = d // CONV_TILE

    def attn_nat_slab(self, s):
        p, i = s // PAIR_SLABS, s % PAIR_SLABS
        return jnp.where(i < 9, (i // 3) * 6 + (i % 3) * 2 + p, 18 + p)

    def rest_nat_tile(self, t):
        n4 = 4 * self.n_conv_tiles
        conv = ATT // CONV_TILE + (t % 4) * self.n_conv_tiles + t // 4
        return jnp.where(t < n4, conv, ATT // CONV_TILE + t)

    def perm_vector(self, v):
        parts = []
        for s in range(2 * PAIR_SLABS):
            p, i = divmod(s, PAIR_SLABS)
            ns = (i // 3) * 6 + (i % 3) * 2 + p if i < 9 else 18 + p
            parts.append(v[:, ns * SLAB:(ns + 1) * SLAB])
        parts.append(jnp.zeros((1, self.c0 - ATT), v.dtype))
        for j in range(self.n_conv_tiles):
            for k in range(4):
                a = ATT + k * self.d + j * CONV_TILE
                parts.append(v[:, a:a + CONV_TILE])
        parts.append(v[:, ATT + 4 * self.d:])
        return jnp.concatenate(parts, axis=1)


def _place():
    return lax.axis_index("x"), lax.axis_index("y"), lax.axis_index("c")


def _other_chips(x, y):
    return [(1 - x, y), (x, 1 - y), (1 - x, 1 - y)]


def _half_rows(ref, col_sharded, half):
    n = ref.shape[0] // 2
    return ref.at[pl.ds(half * n, n)]


def _shard_of(ref, col_sharded, chip, half=None):
    if col_sharded:
        cs = ref.shape[1] // N_CHIPS
        cols = pl.ds(pl.multiple_of(chip * cs, SLAB), cs)
        if half is None:
            return ref.at[:, cols]
        n = ref.shape[0] // 2
        return ref.at[pl.ds(half * n, n), cols]
    rs = ref.shape[0] // N_CHIPS
    if half is None:
        return ref.at[pl.ds(chip * rs, rs)]
    return ref.at[pl.ds(chip * rs + half * (rs // 2), rs // 2)]


def _gather_weights(shards, col_sharded, small):
    n = len(shards)
    full_shapes = []
    for s, cs in zip(shards, col_sharded):
        full_shapes.append((s.shape[0], s.shape[1] * N_CHIPS) if cs else (s.shape[0] * N_CHIPS, s.shape[1]))

    def body(*refs):
        ins, sm_in, outs, sm_out = refs[:n], refs[n], refs[n + 1:2 * n + 1], refs[2 * n + 1]
        send, recv, fsend, frecv, lsem, ssend, srecv = refs[2 * n + 2:]
        x, y, c = _place()
        mine = 2 * x + y
        sibling = (x, y, 1 - c)
        chips = _other_chips(x, y)

        local = [pltpu.make_async_copy(ins[w], _shard_of(outs[w], col_sharded[w], mine), lsem.at[w]) for w in range(n)]
        local.append(pltpu.make_async_copy(sm_in, _shard_of(sm_out, True, mine), lsem.at[n]))
        for cp in local:
            cp.start()
        sends = []
        for k, (cx, cy) in enumerate(chips):
            cp = pltpu.make_async_remote_copy(src_ref=sm_in, dst_ref=_shard_of(sm_out, True, mine), send_sem=ssend.at[k],
                                              recv_sem=srecv.at[k], device_id=(cx, cy, c), device_id_type=MESH)
            cp.start()
            sends.append(cp)
        for k, (cx, cy) in enumerate(chips):
            for w in range(n):
                cp = pltpu.make_async_remote_copy(
                    src_ref=_half_rows(ins[w], col_sharded[w], c), dst_ref=_shard_of(outs[w], col_sharded[w], mine, c),
                    send_sem=send.at[w, k], recv_sem=recv.at[w, k], device_id=(cx, cy, c), device_id_type=MESH)
                cp.start()
                sends.append(cp)
        for k, (cx, cy) in enumerate(chips):
            theirs = 2 * cx + cy
            for w in range(n):
                landed = _shard_of(outs[w], col_sharded[w], theirs, c)
                pltpu.make_async_remote_copy(src_ref=landed, dst_ref=landed, send_sem=send.at[w, k], recv_sem=recv.at[w, k],
                                             device_id=(cx, cy, c), device_id_type=MESH).wait_recv()
                cp = pltpu.make_async_remote_copy(src_ref=landed, dst_ref=landed, send_sem=fsend.at[w, k],
                                                  recv_sem=frecv.at[w, k], device_id=sibling, device_id_type=MESH)
                cp.start()
                sends.append(cp)
        for k, (cx, cy) in enumerate(chips):
            theirs = 2 * cx + cy
            for w in range(n):
                passed = _shard_of(outs[w], col_sharded[w], theirs, 1 - c)
                pltpu.make_async_remote_copy(src_ref=passed, dst_ref=passed, send_sem=fsend.at[w, k], recv_sem=frecv.at[w, k],
                                             device_id=sibling, device_id_type=MESH).wait_recv()
        for k, (cx, cy) in enumerate(chips):
            theirs = _shard_of(sm_out, True, 2 * cx + cy)
            pltpu.make_async_remote_copy(src_ref=theirs, dst_ref=theirs, send_sem=ssend.at[k], recv_sem=srecv.at[k],
                                         device_id=(cx, cy, c), device_id_type=MESH).wait_recv()
        for cp in sends:
            cp.wait_send()
        for cp in local:
            cp.wait()

    any_spec = pl.BlockSpec(memory_space=pl.ANY)
    outs = pl.pallas_call(
        body, name="gather_weights",
        out_shape=[jax.ShapeDtypeStruct(fs, BF16) for fs in full_shapes]
        + [jax.ShapeDtypeStruct((small.shape[0], small.shape[1] * N_CHIPS), small.dtype)],
        in_specs=[any_spec] * (n + 1), out_specs=[any_spec] * (n + 1),
        scratch_shapes=[pltpu.SemaphoreType.DMA((n, 3)), pltpu.SemaphoreType.DMA((n, 3)),
                        pltpu.SemaphoreType.DMA((n, 3)), pltpu.SemaphoreType.DMA((n, 3)), pltpu.SemaphoreType.DMA((n + 1,)),
                        pltpu.SemaphoreType.DMA((3,)), pltpu.SemaphoreType.DMA((3,))],
    )(*shards, small)
    return outs[:n], outs[n]


def _shard_views(gs, col_sharded):
    return [g.reshape(1, *g.shape) if cs else g.reshape(N_CHIPS, g.shape[0] // N_CHIPS, g.shape[1])
            for g, cs in zip(gs, col_sharded)]


DMA_CHUNK_BYTES = 1 << 20


def _chunk_rows(shape, itemsize):
    s, rows, cols = shape
    n = 1
    while s * (rows // n) * cols * itemsize > DMA_CHUNK_BYTES and (rows // n) % 32 == 0:
        n *= 2
    return rows // n


def _send_rows(src, src_row0, dst, dst_row0, rows, send_sem, recv_sem, device):
    step = _chunk_rows((src.shape[0], rows, src.shape[2]), src.dtype.itemsize)
    for r in range(0, rows, step):
        pltpu.make_async_remote_copy(src_ref=src.at[:, pl.ds(src_row0 + r, step)], dst_ref=dst.at[:, pl.ds(dst_row0 + r, step)],
                                     send_sem=send_sem, recv_sem=recv_sem, device_id=device, device_id_type=MESH).start()


def _pair_exchange_halves(views):
    n = len(views)
    half_shapes = [(v.shape[0], v.shape[1] // 2, v.shape[2]) for v in views]

    def body(*refs):
        ins, got = refs[:n], refs[n:2 * n]
        send, recv = refs[2 * n:]
        x, y, c = _place()
        sibling = (x, y, 1 - c)
        for w in range(n):
            hr = half_shapes[w][1]
            _send_rows(ins[w], (1 - c) * hr, got[w], 0, hr, send.at[w], recv.at[w], sibling)
        for w in range(n):
            hr = half_shapes[w][1]
            pltpu.make_async_remote_copy(src_ref=ins[w].at[:, pl.ds((1 - c) * hr, hr)], dst_ref=got[w], send_sem=send.at[w],
                                         recv_sem=recv.at[w], device_id=sibling, device_id_type=MESH).wait()

    any_spec = pl.BlockSpec(memory_space=pl.ANY)
    return pl.pallas_call(
        body, name="grads_pair_exchange",
        out_shape=[jax.ShapeDtypeStruct(s, v.dtype) for s, v in zip(half_shapes, views)],
        in_specs=[any_spec] * n, out_specs=[any_spec] * n,
        scratch_shapes=[pltpu.SemaphoreType.DMA((n,)), pltpu.SemaphoreType.DMA((n,))],
    )(*views)


def _pair_sum(view, got, core, name):
    s, r, cols = view.shape
    hr = r // 2
    tr = _row_tile(hr, cols)
    nb = hr // tr

    def body(core_ref, a_ref, b_ref, o_ref):
        del core_ref
        o_ref[...] = (a_ref[...].astype(F32) + b_ref[...].astype(F32)).astype(BF16)

    same = pl.BlockSpec((None, tr, cols), lambda j, i, core_ref: (j, i, 0))
    return pl.pallas_call(
        body, name=name,
        grid_spec=pltpu.PrefetchScalarGridSpec(
            num_scalar_prefetch=1, grid=(s, nb),
            in_specs=[pl.BlockSpec((None, tr, cols), lambda j, i, core_ref: (j, core_ref[0] * nb + i, 0)), same],
            out_specs=same),
        out_shape=jax.ShapeDtypeStruct((s, hr, cols), BF16), compiler_params=_params("parallel", "parallel"))(core, view, got)


def _piece_cols(part, col_sharded):
    return part.shape[2] // N_CHIPS if col_sharded else part.shape[2]


def _chip_scatter(parts, col_sharded):
    n = len(parts)

    def piece(ref, cs, chip):
        if cs:
            w = ref.shape[2] // N_CHIPS
            return ref.at[:, :, pl.ds(pl.multiple_of(chip * w, SLAB), w)]
        return ref.at[pl.ds(chip, 1)]

    def body(*refs):
        ins, got = refs[:n], refs[n:2 * n]
        send, recv = refs[2 * n:]
        x, y, c = _place()
        sends = []
        for k, (cx, cy) in enumerate(_other_chips(x, y)):
            for w in range(n):
                cp = pltpu.make_async_remote_copy(
                    src_ref=piece(ins[w], col_sharded[w], 2 * cx + cy), dst_ref=got[w].at[pl.ds(k, 1)], send_sem=send.at[w, k],
                    recv_sem=recv.at[w, k], device_id=(cx, cy, c), device_id_type=MESH)
                cp.start()
                sends.append(cp)
        for cp in sends:
            cp.wait()

    any_spec = pl.BlockSpec(memory_space=pl.ANY)
    return pl.pallas_call(
        body, name="grads_chip_scatter",
        out_shape=[jax.ShapeDtypeStruct((3, p.shape[1], _piece_cols(p, cs)), p.dtype) for p, cs in zip(parts, col_sharded)],
        in_specs=[any_spec] * n, out_specs=[any_spec] * n,
        scratch_shapes=[pltpu.SemaphoreType.DMA((n, 3)), pltpu.SemaphoreType.DMA((n, 3))],
    )(*parts)


def _chip_sum(part, got, col_sharded, place, name):
    _, hr, _ = part.shape
    cols = _piece_cols(part, col_sharded)
    tr = _row_tile(hr, cols)
    nb = hr // tr

    def body(place_ref, own_ref, g0_ref, g1_ref, g2_ref, o_ref):
        del place_ref
        acc = own_ref[...].astype(F32) + g0_ref[...].astype(F32)
        o_ref[...] = acc + g1_ref[...].astype(F32) + g2_ref[...].astype(F32)

    if col_sharded:
        own = pl.BlockSpec((None, tr, cols), lambda i, pr: (0, i, pr[0]))
    else:
        own = pl.BlockSpec((None, tr, cols), lambda i, pr: (pr[0], i, 0))
    others = [pl.BlockSpec((None, tr, cols), lambda i, pr, k=k: (k, i, 0)) for k in range(3)]
    return pl.pallas_call(
        body, name=name,
        grid_spec=pltpu.PrefetchScalarGridSpec(
            num_scalar_prefetch=1, grid=(nb,), in_specs=[own] + others,
            out_specs=pl.BlockSpec((tr, cols), lambda i, pr: (pr[1] * nb + i, 0))),
        out_shape=jax.ShapeDtypeStruct((2 * hr, cols), F32), compiler_params=_params("parallel"))(place, part, got, got, got)


def _pair_join_halves(fulls):
    n = len(fulls)
    views = [f.reshape(1, *f.shape) for f in fulls]

    def body(*refs):
        outs = refs[n:2 * n]
        send, recv = refs[2 * n:]
        x, y, c = _place()
        sibling = (x, y, 1 - c)
        for w in range(n):
            hr = outs[w].shape[1] // 2
            _send_rows(outs[w], c * hr, outs[w], c * hr, hr, send.at[w], recv.at[w], sibling)
        for w in range(n):
            hr = outs[w].shape[1] // 2
            pltpu.make_async_remote_copy(
                src_ref=outs[w].at[:, pl.ds(c * hr, hr)], dst_ref=outs[w].at[:, pl.ds((1 - c) * hr, hr)], send_sem=send.at[w],
                recv_sem=recv.at[w], device_id=sibling, device_id_type=MESH).wait()

    any_spec = pl.BlockSpec(memory_space=pl.ANY)
    outs = pl.pallas_call(
        body, name="grads_pair_join",
        out_shape=[jax.ShapeDtypeStruct(v.shape, v.dtype) for v in views],
        in_specs=[any_spec] * n, out_specs=[any_spec] * n, input_output_aliases={w: w for w in range(n)},
        scratch_shapes=[pltpu.SemaphoreType.DMA((n,)), pltpu.SemaphoreType.DMA((n,))],
    )(*views)
    return [o[0] for o in outs]


def _all_sum_small(vec, d):
    rows = vec.shape[0]

    def body(v_ref, o_ref, all_ref, send, recv):
        x, y, c = _place()
        me = 4 * x + 2 * y + c
        all_ref[me] = v_ref[...]
        copies = []
        for k in range(1, N_DEV):
            fx, fy, fc = (k >> 2) & 1, (k >> 1) & 1, k & 1
            peer = (x ^ fx, y ^ fy, c ^ fc)
            cp = pltpu.make_async_remote_copy(src_ref=v_ref, dst_ref=all_ref.at[me], send_sem=send.at[k - 1],
                                              recv_sem=recv.at[k - 1], device_id=peer, device_id_type=MESH)
            cp.start()
            copies.append((cp, 4 * peer[0] + 2 * peer[1] + peer[2]))
        for k, (cp, src) in enumerate(copies):
            pltpu.make_async_remote_copy(src_ref=v_ref, dst_ref=all_ref.at[src], send_sem=send.at[k], recv_sem=recv.at[k],
                                         device_id=(x, y, c), device_id_type=MESH).wait_recv()
        for cp, _ in copies:
            cp.wait_send()
        total = all_ref[0]
        for i in range(1, N_DEV):
            total = total + all_ref[i]
        o_ref[...] = total
        loss = 0.5 / d * jnp.sum(total[0:8, :])
        o_ref[0:8, :] = jnp.full((8, 128), loss, F32)

    return pl.pallas_call(
        body, name="all_sum_small",
        out_shape=jax.ShapeDtypeStruct((rows, 128), F32),
        in_specs=[pl.BlockSpec(memory_space=pltpu.VMEM)], out_specs=pl.BlockSpec(memory_space=pltpu.VMEM),
        scratch_shapes=[pltpu.VMEM((N_DEV, rows, 128), F32), pltpu.SemaphoreType.DMA((N_DEV - 1,)),
                        pltpu.SemaphoreType.DMA((N_DEV - 1,))],
        compiler_params=pltpu.CompilerParams(vmem_limit_bytes=VMEM_LIMIT_V7X),
    )(vec)


def _cast_bf16(arrays):
    n = len(arrays)

    def body(*refs):
        for i in range(n):
            refs[n + i][...] = refs[i][...].astype(BF16)

    vm = pl.BlockSpec(memory_space=pltpu.VMEM)
    return pl.pallas_call(body, name="cast_shards", out_shape=[jax.ShapeDtypeStruct(a.shape, BF16) for a in arrays],
                          in_specs=[vm] * n, out_specs=[vm] * n,
                          compiler_params=pltpu.CompilerParams(vmem_limit_bytes=VMEM_LIMIT_V7X))(*arrays)


def _row_tile(rows, cols, itemsize=4, budget=2 << 20):
    t = rows
    while t * cols * itemsize > budget and t % 16 == 0:
        t //= 2
    return t


def _adamw(w, g, m, v, name):
    rows, cols = w.shape
    tr = _row_tile(rows, cols, budget=1 << 20)

    def body(w_ref, g_ref, m_ref, v_ref, d_ref, nm_ref, nv_ref):
        g_ = g_ref[...]
        nm = ADAM_B1 * m_ref[...] + (1.0 - ADAM_B1) * g_
        nv = ADAM_B2 * v_ref[...] + (1.0 - ADAM_B2) * (g_ * g_)
        m_hat = nm / (1.0 - ADAM_B1 ** ADAM_STEP)
        v_hat = nv / (1.0 - ADAM_B2 ** ADAM_STEP)
        d_ref[...] = -ADAM_LR * (m_hat / (jnp.sqrt(v_hat) + ADAM_EPS) + ADAM_WD * w_ref[...])
        nm_ref[...] = nm
        nv_ref[...] = nv

    spec = pl.BlockSpec((tr, cols), lambda i: (i, 0))
    shp = jax.ShapeDtypeStruct((rows, cols), F32)
    return pl.pallas_call(body, name=name, grid=(rows // tr,), in_specs=[spec] * 4, out_specs=[spec] * 3,
                          out_shape=[shp] * 3, compiler_params=_params("parallel"))(w, g, m, v)


def _ada(c, w_ada, b_ada):
    bsz, d = c.shape

    def body(c_ref, w_ref, b_ref, mod_ref, act_ref):
        cc = c_ref[...]
        act = cc * _sigmoid(cc)
        act_ref[...] = act
        mod_ref[...] = jnp.dot(act.astype(BF16), w_ref[...], preferred_element_type=F32) + b_ref[...]

    vm = pl.BlockSpec(memory_space=pltpu.VMEM)
    return pl.pallas_call(body, name="ada", in_specs=[vm] * 3, out_specs=[vm] * 2,
                          out_shape=[jax.ShapeDtypeStruct((bsz, 3 * d), F32), jax.ShapeDtypeStruct((bsz, d), F32)],
                          compiler_params=pltpu.CompilerParams(vmem_limit_bytes=VMEM_LIMIT_V7X))(c, w_ada, b_ada)


def _grad_w_ada(act_t, d_ada):
    d, bsz = act_t.shape
    n3 = d_ada.shape[1]
    tn = n3 // 3

    def body(a_ref, g_ref, o_ref):
        a, g = a_ref[...], g_ref[...]
        acc = a[:, 0:1] * g[0:1, :]
        for b in range(1, bsz):
            acc = acc + a[:, b:b + 1] * g[b:b + 1, :]
        o_ref[...] = acc.astype(BF16)

    return pl.pallas_call(body, name="grad_w_ada", grid=(n3 // tn,),
                          in_specs=[pl.BlockSpec((d, bsz), lambda j: (0, 0)), pl.BlockSpec((bsz, tn), lambda j: (0, j))],
                          out_specs=pl.BlockSpec((d, tn), lambda j: (0, j)),
                          out_shape=jax.ShapeDtypeStruct((d, n3), BF16), compiler_params=_params("parallel"))(act_t, d_ada)


def _permute_w_in(w_nat, lay):
    d = lay.d

    def copy_body(src_ref, dst_in_ref, dst_ref):
        del dst_in_ref
        dst_ref[...] = src_ref[...]

    def first_body(src_ref, dst_ref):
        dst_ref[...] = src_ref[...]

    w_all = pl.pallas_call(
        first_body, name="permute_w_attn", grid=(2 * PAIR_SLABS,),
        in_specs=[pl.BlockSpec((d, SLAB), lambda s: (0, lay.attn_nat_slab(s)))],
        out_specs=pl.BlockSpec((d, SLAB), lambda s: (0, s)),
        out_shape=jax.ShapeDtypeStruct((d, lay.np), BF16), compiler_params=_params("arbitrary"))(w_nat)
    n_rest = 6 * d // CONV_TILE
    base = lay.c0 // CONV_TILE
    return pl.pallas_call(
        copy_body, name="permute_w_rest", grid=(n_rest,),
        in_specs=[pl.BlockSpec((d, CONV_TILE), lambda t: (0, lay.rest_nat_tile(t))), pl.BlockSpec(memory_space=pl.ANY)],
        out_specs=pl.BlockSpec((d, CONV_TILE), lambda t: (0, base + t)),
        out_shape=jax.ShapeDtypeStruct((d, lay.np), BF16), input_output_aliases={1: 0},
        compiler_params=_params("arbitrary"))(w_nat, w_all)


def _project(x2, mod3, w_all, b_all, seq, col0, ncols, tn, out_dtype, want_ht, name):
    t, d = x2.shape
    tm = min(1024, seq)
    per_seq = seq // tm
    j0 = col0 // tn

    def body(x_ref, mod_ref, w_ref, b_ref, o_ref, *rest):
        h_ref = rest[-1]

        @pl.when(pl.program_id(1) == 0)
        def _():
            h = x_ref[...] * (1.0 + mod_ref[:, d:2 * d]) + mod_ref[:, 0:d]
            h_ref[...] = h.astype(BF16)
            if want_ht:
                rest[0][...] = h.T.astype(BF16)

        o_ref[...] = (jnp.dot(h_ref[...], w_ref[...], preferred_element_type=F32) + b_ref[...]).astype(out_dtype)

    out_shape = [jax.ShapeDtypeStruct((t, ncols), out_dtype)]
    out_specs = [pl.BlockSpec((tm, tn), lambda i, j: (i, j))]
    if want_ht:
        out_shape.append(jax.ShapeDtypeStruct((d, t), BF16))
        out_specs.append(pl.BlockSpec((d, tm), lambda i, j: (0, i)))
    return pl.pallas_call(
        body, name=name, grid=(t // tm, ncols // tn),
        in_specs=[pl.BlockSpec((tm, d), lambda i, j: (i, 0)),
                  pl.BlockSpec((None, 1, 3 * d), lambda i, j: (i // per_seq, 0, 0)),
                  pl.BlockSpec((d, tn), lambda i, j: (0, j0 + j)),
                  pl.BlockSpec((1, tn), lambda i, j: (0, j0 + j))],
        out_specs=out_specs, out_shape=out_shape,
        scratch_shapes=[pltpu.VMEM((tm, d), BF16)],
        compiler_params=_params("arbitrary", "arbitrary"))(x2, mod3, w_all, b_all)


def _slope(g, p, hh):
    head = 4 * g + 2 * p + hh
    return 2.0 ** (-ALIBI_MAX_EXP * (head + 1.0) / N_HEADS)


def _ld(ref, start, n, stride):
    if stride == 1:
        return ref[pl.ds(start, n), :]
    return ref[pl.ds(start, n, stride=stride), :]


def _st(ref, start, n, stride, val):
    if stride == 1:
        ref[pl.ds(start, n), :] = val
    else:
        ref[pl.ds(start, n, stride=stride), :] = val


def _block_place(it, g, seq):
    dil = DILATIONS[g]
    nb = seq // dil // SUB
    if nb == 1:
        return it, it, SUB, 0
    r, blk = it // nb, it % nb
    kb = jnp.maximum(blk - 1, 0)
    base = r + dil * SUB * blk
    kstart = r + dil * SUB * kb
    if dil == 1:
        base, kstart = pl.multiple_of(base, SUB), pl.multiple_of(kstart, SUB)
    return base, kstart, 2 * SUB, (blk - kb) * SUB


def _masked_scores(qh, kh, delta, slope_dil):
    s = lax.dot_general(qh, kh, (((1,), (1,)), ((), ())), preferred_element_type=F32) * (HEAD_DIM ** -0.5)
    s = s - slope_dil * delta.astype(F32)
    return jnp.where((delta >= 0) & (delta <= SUB), s, NEG)


def _unpack_group(pa_ref, slabs, g):
    for w in range(3):
        s = 3 * w + g
        slabs[w] = pa_ref[:, s * SLAB:(s + 1) * SLAB].astype(F32)


def _head_bcast(col0, col1, rows):
    return jnp.concatenate([jnp.broadcast_to(col0, (rows, HEAD_DIM)), jnp.broadcast_to(col1, (rows, HEAD_DIM))], axis=-1)


def _head_sums(t):
    rows = t.shape[0]
    return _head_bcast(jnp.sum(t[:, :HEAD_DIM], axis=-1, keepdims=True), jnp.sum(t[:, HEAD_DIM:], axis=-1, keepdims=True), rows)


def _attn_fwd(pa, bsz, seq):
    t = pa.shape[0]
    n_blocks = seq // SUB
    chunk = 256

    def body(pa_ref, o_ref, lse_ref, a_ref, slabs):
        p = pl.program_id(1)
        for g in range(N_GROUPS):
            dil = DILATIONS[g]
            _unpack_group(pa_ref, slabs, g)
            nk = SUB if seq // dil // SUB == 1 else 2 * SUB
            base_delta = (lax.broadcasted_iota(jnp.int32, (SUB, nk), 0) - lax.broadcasted_iota(jnp.int32, (SUB, nk), 1))

            def block(it, carry, g=g, dil=dil, nk=nk, base_delta=base_delta):
                base, kstart, _, off = _block_place(it, g, seq)
                delta = base_delta + off
                q = _ld(slabs.at[0], base, SUB, dil)
                kw = _ld(slabs.at[1], kstart, nk, dil)
                vw = _ld(slabs.at[2], kstart, nk, dil)
                outs, lses = [], []
                for hh in range(2):
                    cols = slice(hh * HEAD_DIM, (hh + 1) * HEAD_DIM)
                    slope = jnp.where(p == 0, _slope(g, 0, hh), _slope(g, 1, hh)) * dil
                    s = _masked_scores(q[:, cols].astype(BF16), kw[:, cols].astype(BF16), delta, slope)
                    m = jnp.max(s, axis=-1, keepdims=True)
                    e = jnp.exp(s - m)
                    den = jnp.sum(e, axis=-1, keepdims=True)
                    outs.append(jnp.dot((e / den).astype(BF16), vw[:, cols].astype(BF16), preferred_element_type=F32))
                    lses.append(m + jnp.log(den))
                _st(o_ref.at[g], base, SUB, dil, jnp.concatenate(outs, axis=-1))
                _st(lse_ref.at[g], base, SUB, dil, _head_bcast(lses[0], lses[1], SUB))
                return carry

            lax.fori_loop(0, n_blocks, block, 0)

        def mix(i, carry):
            rows = pl.ds(pl.multiple_of(i * chunk, chunk), chunk)
            l0, l1, l2 = lse_ref[0, rows, :], lse_ref[1, rows, :], lse_ref[2, rows, :]
            m = jnp.maximum(jnp.maximum(l0, l1), l2)
            e0, e1, e2 = jnp.exp(l0 - m), jnp.exp(l1 - m), jnp.exp(l2 - m)
            tot = e0 + e1 + e2
            o = (e0 / tot) * o_ref[0, rows, :] + (e1 / tot) * o_ref[1, rows, :] + (e2 / tot) * o_ref[2, rows, :]
            z = pa_ref[rows, 9 * SLAB:10 * SLAB].astype(F32)
            a_ref[rows, :] = (o * (z * _sigmoid(z))).astype(BF16)
            return carry

        lax.fori_loop(0, seq // chunk, mix, 0)

    big = jax.ShapeDtypeStruct((N_GROUPS, t, 2 * SLAB), F32)
    return pl.pallas_call(
        body, name="attn_fwd", grid=(bsz, 2),
        in_specs=[pl.BlockSpec((seq, PAIR_COLS), lambda b, p: (b, p))],
        out_specs=[pl.BlockSpec((N_GROUPS, seq, SLAB), lambda b, p: (0, b, p)),
                   pl.BlockSpec((N_GROUPS, seq, SLAB), lambda b, p: (0, b, p)),
                   pl.BlockSpec((seq, SLAB), lambda b, p: (b, p))],
        out_shape=[big, big, jax.ShapeDtypeStruct((t, 2 * SLAB), BF16)],
        scratch_shapes=[pltpu.VMEM((3, seq, SLAB), F32)],
        compiler_params=_params("arbitrary", "arbitrary"))(pa)


def _shift_down(v, k):
    rows = lax.broadcasted_iota(jnp.int32, v.shape, 0)
    return jnp.where(rows >= k, pltpu.roll(v, k, 0), 0.0)


def _shift_up(v, k):
    n = v.shape[0]
    rows = lax.broadcasted_iota(jnp.int32, v.shape, 0)
    return jnp.where(rows < n - k, pltpu.roll(v, n - k, 0), 0.0)


def _conv_fwd(pr, conv_w, bsz, seq, d):
    t = pr.shape[0]
    ct = CONV_TILE

    def body(p_ref, cw_ref, o_ref):
        u = p_ref[:, 2 * ct:3 * ct] * p_ref[:, 0:ct]
        cw = cw_ref[...]
        conv = cw[0:1, :] * _shift_down(u, 2)
        conv = conv + cw[1:2, :] * _shift_down(u, 1)
        conv = conv + cw[2:3, :] * u
        z = p_ref[:, 3 * ct:4 * ct]
        o_ref[...] = (p_ref[:, ct:2 * ct] * conv * (z * _sigmoid(z))).astype(BF16)

    return pl.pallas_call(
        body, name="conv_fwd", grid=(bsz, d // ct),
        in_specs=[pl.BlockSpec((seq, 4 * ct), lambda b, j: (b, j)), pl.BlockSpec((3, ct), lambda b, j: (0, j))],
        out_specs=pl.BlockSpec((seq, ct), lambda b, j: (b, j)),
        out_shape=jax.ShapeDtypeStruct((t, d), BF16), compiler_params=_params("parallel", "parallel"))(pr, conv_w)


def _tail(a_in, b_in, pr, x2, target2, mod3, w_pa, w_pc, w_out, b_out, ln_g, ln_b, seq, lay):
    t, d = x2.shape
    tm = 256
    per_seq = seq // tm
    n_steps = t // tm
    gate_blk = 4 * d // d

    def nt(a, b):
        return lax.dot_general(a, b, (((1,), (1,)), ((), ())), preferred_element_type=F32)

    def tn(a, b):
        return lax.dot_general(a, b, (((0,), (0,)), ((), ())), preferred_element_type=F32)

    def body(a_ref, b_ref, ga_ref, gb_ref, x_ref, tg_ref, mod_ref, wpa_ref, wpc_ref, wo_ref, bo_ref, lg_ref, lb_ref,
             dpg_ref, da_ref, db_ref, gx_ref, dgate_ref, small_ref, gwpa_hbm, gwpc_hbm, gwo_hbm,
             acc_pa, acc_pc, acc_o, sem):
        i = pl.program_id(0)

        @pl.when(i == 0)
        def _():
            acc_pa[...] = jnp.zeros_like(acc_pa)
            acc_pc[...] = jnp.zeros_like(acc_pc)
            acc_o[...] = jnp.zeros_like(acc_o)
            small_ref[...] = jnp.zeros_like(small_ref)

        @pl.when(i % per_seq == 0)
        def _():
            dgate_ref[...] = jnp.zeros_like(dgate_ref)

        a_bf, b_bf = a_ref[...], b_ref[...]
        y_attn = jnp.dot(a_bf, wpa_ref[...], preferred_element_type=F32)
        y_conv = jnp.dot(b_bf, wpc_ref[...], preferred_element_type=F32)
        sa, sb = _sigmoid(ga_ref[...]), _sigmoid(gb_ref[...])
        merged = (sa * y_attn + sb * y_conv).astype(BF16)
        mo = jnp.dot(merged, wo_ref[...], preferred_element_type=F32) + bo_ref[...]
        gate = mod_ref[:, 2 * d:3 * d]
        r = ALPHA * x_ref[...] + gate * mo
        mu = jnp.mean(r, axis=-1, keepdims=True)
        cen = r - mu
        var = jnp.mean(cen * cen, axis=-1, keepdims=True)
        rstd = lax.rsqrt(var + LN_EPS)
        xhat = cen * rstd
        err = xhat * lg_ref[...] + lb_ref[...] - tg_ref[...]
        dy = err * (1.0 / d)
        dxhat = dy * lg_ref[...]
        dr = rstd * (dxhat - jnp.mean(dxhat, axis=-1, keepdims=True) - xhat * jnp.mean(dxhat * xhat, axis=-1, keepdims=True))
        gx_ref[...] = ALPHA * dr
        dgate_ref[...] += jnp.sum(dr * mo, axis=0, keepdims=True)
        d_mo = dr * gate
        small_ref[0:1, :] += jnp.sum(d_mo, axis=0, keepdims=True)
        small_ref[1:2, :] += jnp.sum(dy * xhat, axis=0, keepdims=True)
        small_ref[2:3, :] += jnp.sum(dy, axis=0, keepdims=True)
        small_ref[3:4, :] += jnp.sum(err * err, axis=0, keepdims=True)
        d_mo_bf = d_mo.astype(BF16)
        acc_o[...] += tn(merged, d_mo_bf)
        dmerged = nt(d_mo_bf, wo_ref[...])
        dy_attn = (dmerged * sa).astype(BF16)
        dy_conv = (dmerged * sb).astype(BF16)
        dpg_ref[:, 0:d] = (dmerged * y_attn * sa * (1.0 - sa)).astype(BF16)
        dpg_ref[:, d:2 * d] = (dmerged * y_conv * sb * (1.0 - sb)).astype(BF16)
        acc_pa[...] += tn(a_bf, dy_attn)
        acc_pc[...] += tn(b_bf, dy_conv)
        da_ref[...] = nt(dy_attn, wpa_ref[...])
        db_ref[...] = nt(dy_conv, wpc_ref[...])

        @pl.when(i == n_steps - 1)
        def _():
            copies = [pltpu.make_async_copy(acc_pa, gwpa_hbm, sem.at[0]), pltpu.make_async_copy(acc_pc, gwpc_hbm, sem.at[1]),
                      pltpu.make_async_copy(acc_o, gwo_hbm, sem.at[2])]
            for cp in copies:
                cp.start()
            for cp in copies:
                cp.wait()

    row = lambda w: pl.BlockSpec((tm, w), lambda i: (i, 0))
    const = lambda shp: pl.BlockSpec(shp, lambda i: (0,) * len(shp), pipeline_mode=pl.Buffered(1))
    any_spec = pl.BlockSpec(memory_space=pl.ANY)
    return pl.pallas_call(
        body, name="tail", grid=(n_steps,),
        in_specs=[row(Z_WIDTH), row(d),
                  pl.BlockSpec((tm, d), lambda i: (i, gate_blk)), pl.BlockSpec((tm, d), lambda i: (i, gate_blk + 1)),
                  row(d), row(d), pl.BlockSpec((None, 1, 3 * d), lambda i: (i // per_seq, 0, 0)),
                  const((Z_WIDTH, d)), const((d, d)), const((d, d)), const((1, d)), const((1, d)), const((1, d))],
        out_specs=[pl.BlockSpec((tm, 2 * d), lambda i: (i, lay.g0 // (2 * d))), row(Z_WIDTH), row(d), row(d),
                   pl.BlockSpec((None, 1, d), lambda i: (i // per_seq, 0, 0)), pl.BlockSpec((8, d), lambda i: (0, 0)),
                   any_spec, any_spec, any_spec],
        out_shape=[jax.ShapeDtypeStruct((t, lay.np), BF16), jax.ShapeDtypeStruct((t, Z_WIDTH), F32),
                   jax.ShapeDtypeStruct((t, d), F32), jax.ShapeDtypeStruct((t, d), F32),
                   jax.ShapeDtypeStruct((t // seq, 1, d), F32), jax.ShapeDtypeStruct((8, d), F32),
                   jax.ShapeDtypeStruct((Z_WIDTH, d), F32), jax.ShapeDtypeStruct((d, d), F32),
                   jax.ShapeDtypeStruct((d, d), F32)],
        scratch_shapes=[pltpu.VMEM((Z_WIDTH, d), F32), pltpu.VMEM((d, d), F32), pltpu.VMEM((d, d), F32),
                        pltpu.SemaphoreType.DMA((3,))],
        compiler_params=_params("arbitrary"),
    )(a_in, b_in, pr, pr, x2, target2, mod3, w_pa, w_pc, w_out, b_out, ln_g, ln_b)


def _conv_bwd(dproj, db, pr, conv_w, bsz, seq, lay):
    d = lay.d
    ct = CONV_TILE
    base = lay.c0 // (4 * ct)

    def body(dp_in, db_ref, p_ref, cw_ref, dp_ref, gcw_ref):
        del dp_in
        u_x, g_b, g_c, z = p_ref[:, 0:ct], p_ref[:, ct:2 * ct], p_ref[:, 2 * ct:3 * ct], p_ref[:, 3 * ct:4 * ct]
        cw = cw_ref[...]
        u = g_c * u_x
        u1, u2 = _shift_down(u, 1), _shift_down(u, 2)
        conv = cw[0:1, :] * u2 + cw[1:2, :] * u1 + cw[2:3, :] * u
        sig = _sigmoid(z)
        sl = z * sig
        dbv = db_ref[...]
        gbc = g_b * conv
        dp_ref[:, ct:2 * ct] = (dbv * sl * conv).astype(BF16)
        dp_ref[:, 3 * ct:4 * ct] = (dbv * gbc * (sig * (1.0 + z * (1.0 - sig)))).astype(BF16)
        dconv = dbv * sl * g_b

        @pl.when(pl.program_id(1) == 0)
        def _():
            gcw_ref[...] = jnp.zeros_like(gcw_ref)

        gcw_ref[0:1, :] += jnp.sum(dconv * u2, axis=0, keepdims=True)
        gcw_ref[1:2, :] += jnp.sum(dconv * u1, axis=0, keepdims=True)
        gcw_ref[2:3, :] += jnp.sum(dconv * u, axis=0, keepdims=True)
        du = cw[2:3, :] * dconv + cw[1:2, :] * _shift_up(dconv, 1) + cw[0:1, :] * _shift_up(dconv, 2)
        dp_ref[:, 0:ct] = (du * g_c).astype(BF16)
        dp_ref[:, 2 * ct:3 * ct] = (du * u_x).astype(BF16)

    return pl.pallas_call(
        body, name="conv_bwd", grid=(d // ct, bsz),
        in_specs=[pl.BlockSpec(memory_space=pl.ANY), pl.BlockSpec((seq, ct), lambda j, b: (b, j)),
                  pl.BlockSpec((seq, 4 * ct), lambda j, b: (b, j)), pl.BlockSpec((3, ct), lambda j, b: (0, j))],
        out_specs=[pl.BlockSpec((seq, 4 * ct), lambda j, b: (b, base + j)), pl.BlockSpec((8, ct), lambda j, b: (0, j))],
        out_shape=[jax.ShapeDtypeStruct(dproj.shape, BF16), jax.ShapeDtypeStruct((8, d), F32)],
        input_output_aliases={0: 0}, compiler_params=_params("arbitrary", "arbitrary"))(dproj, db, pr, conv_w)


def _attn_bwd(dproj, pa, o_all, lse_all, da, bsz, seq):
    n_blocks = seq // SUB
    chunk = 256

    def body(dp_in, pa_ref, o_ref, lse_ref, da_ref, dp_ref, slabs, dslabs, dog, cvec):
        del dp_in
        p = pl.program_id(1)

        def mix_bwd(i, carry):
            rows = pl.ds(pl.multiple_of(i * chunk, chunk), chunk)
            ls = [lse_ref[g, rows, :] for g in range(N_GROUPS)]
            os_ = [o_ref[g, rows, :] for g in range(N_GROUPS)]
            m = jnp.maximum(jnp.maximum(ls[0], ls[1]), ls[2])
            es = [jnp.exp(l - m) for l in ls]
            tot = es[0] + es[1] + es[2]
            ws = [e / tot for e in es]
            o = ws[0] * os_[0] + ws[1] * os_[1] + ws[2] * os_[2]
            z = pa_ref[rows, 9 * SLAB:10 * SLAB].astype(F32)
            sig = _sigmoid(z)
            dav = da_ref[rows, :]
            do = dav * (z * sig)
            dp_ref[rows, 9 * SLAB:10 * SLAB] = (dav * o * (sig * (1.0 + z * (1.0 - sig)))).astype(BF16)
            wsum = _head_sums(do * o)
            for g in range(N_GROUPS):
                dog[g, rows, :] = ws[g] * do
                cvec[g, rows, :] = -(ws[g] * wsum)
            return carry

        lax.fori_loop(0, seq // chunk, mix_bwd, 0)

        for g in range(N_GROUPS):
            dil = DILATIONS[g]
            _unpack_group(pa_ref, slabs, g)
            dslabs[...] = jnp.zeros_like(dslabs)
            nk = SUB if seq // dil // SUB == 1 else 2 * SUB
            base_delta = (lax.broadcasted_iota(jnp.int32, (SUB, nk), 0) - lax.broadcasted_iota(jnp.int32, (SUB, nk), 1))

            def block(it, carry, g=g, dil=dil, nk=nk, base_delta=base_delta):
                base, kstart, _, off = _block_place(it, g, seq)
                delta = base_delta + off
                q = _ld(slabs.at[0], base, SUB, dil)
                kw = _ld(slabs.at[1], kstart, nk, dil)
                vw = _ld(slabs.at[2], kstart, nk, dil)
                do = _ld(dog.at[g], base, SUB, dil)
                cv = _ld(cvec.at[g], base, SUB, dil)
                lse = _ld(lse_ref.at[g], base, SUB, dil)
                dqs, dks, dvs = [], [], []
                for hh in range(2):
                    cols = slice(hh * HEAD_DIM, (hh + 1) * HEAD_DIM)
                    lane = hh * HEAD_DIM
                    slope = jnp.where(p == 0, _slope(g, 0, hh), _slope(g, 1, hh)) * dil
                    qh, kh, vh = q[:, cols].astype(BF16), kw[:, cols].astype(BF16), vw[:, cols].astype(BF16)
                    doh = do[:, cols].astype(BF16)
                    s = _masked_scores(qh, kh, delta, slope)
                    prob = jnp.exp(s - lse[:, lane:lane + 1])
                    dprob = lax.dot_general(doh, vh, (((1,), (1,)), ((), ())), preferred_element_type=F32)
                    ds = (prob * (dprob + cv[:, lane:lane + 1]) * (HEAD_DIM ** -0.5)).astype(BF16)
                    dqs.append(jnp.dot(ds, kh, preferred_element_type=F32))
                    dks.append(lax.dot_general(ds, qh, (((0,), (0,)), ((), ())), preferred_element_type=F32))
                    dvs.append(lax.dot_general(prob.astype(BF16), doh, (((0,), (0,)), ((), ())), preferred_element_type=F32))
                _st(dslabs.at[0], base, SUB, dil, jnp.concatenate(dqs, axis=-1))
                dk_old = _ld(dslabs.at[1], kstart, nk, dil)
                _st(dslabs.at[1], kstart, nk, dil, dk_old + jnp.concatenate(dks, axis=-1))
                dv_old = _ld(dslabs.at[2], kstart, nk, dil)
                _st(dslabs.at[2], kstart, nk, dil, dv_old + jnp.concatenate(dvs, axis=-1))
                return carry

            lax.fori_loop(0, n_blocks, block, 0)
            for w in range(3):
                s = 3 * w + g
                dp_ref[:, s * SLAB:(s + 1) * SLAB] = dslabs[w].astype(BF16)

    return pl.pallas_call(
        body, name="attn_bwd", grid=(bsz, 2),
        in_specs=[pl.BlockSpec(memory_space=pl.ANY), pl.BlockSpec((seq, PAIR_COLS), lambda b, p: (b, p)),
                  pl.BlockSpec((N_GROUPS, seq, SLAB), lambda b, p: (0, b, p)),
                  pl.BlockSpec((N_GROUPS, seq, SLAB), lambda b, p: (0, b, p)),
                  pl.BlockSpec((seq, SLAB), lambda b, p: (b, p))],
        out_specs=pl.BlockSpec((seq, PAIR_COLS), lambda b, p: (b, p)),
        out_shape=jax.ShapeDtypeStruct(dproj.shape, BF16), input_output_aliases={0: 0},
        scratch_shapes=[pltpu.VMEM((3, seq, SLAB), F32), pltpu.VMEM((3, seq, SLAB), F32),
                        pltpu.VMEM((3, seq, SLAB), F32), pltpu.VMEM((3, seq, SLAB), F32)],
        compiler_params=_params("arbitrary", "arbitrary"))(dproj, pa, o_all, lse_all, da)


def _real_k_tile(kk, lay, tk):
    return jnp.where(kk < ATT // tk, kk, kk + (lay.c0 - ATT) // tk)


def _grad_h(dproj, w_all, gx0, x2, mod3, seq, lay):
    t, d = x2.shape
    tm = min(1024, seq)
    tk = 512
    per_seq = seq // tm
    nk = (ATT + 6 * d) // tk

    def body(dp_ref, w_ref, gx0_ref, x_ref, mod_ref, gx_ref, dmod_ref, acc):
        i, kk = pl.program_id(0), pl.program_id(1)

        @pl.when(kk == 0)
        def _():
            acc[...] = jnp.zeros_like(acc)

        acc[...] += lax.dot_general(dp_ref[...], w_ref[...], (((1,), (1,)), ((), ())), preferred_element_type=F32)

        @pl.when(kk == nk - 1)
        def _():
            dh = acc[...]
            gx_ref[...] = gx0_ref[...] + dh * (1.0 + mod_ref[:, d:2 * d])

            @pl.when(i % per_seq == 0)
            def _():
                dmod_ref[...] = jnp.zeros_like(dmod_ref)

            dmod_ref[0:1, :] += jnp.sum(dh, axis=0, keepdims=True)
            dmod_ref[1:2, :] += jnp.sum(dh * x_ref[...], axis=0, keepdims=True)

    return pl.pallas_call(
        body, name="grad_h", grid=(t // tm, nk),
        in_specs=[pl.BlockSpec((tm, tk), lambda i, k: (i, _real_k_tile(k, lay, tk))),
                  pl.BlockSpec((d, tk), lambda i, k: (0, _real_k_tile(k, lay, tk))),
                  pl.BlockSpec((tm, d), lambda i, k: (i, 0)), pl.BlockSpec((tm, d), lambda i, k: (i, 0)),
                  pl.BlockSpec((None, 1, 3 * d), lambda i, k: (i // per_seq, 0, 0))],
        out_specs=[pl.BlockSpec((tm, d), lambda i, k: (i, 0)), pl.BlockSpec((None, 8, d), lambda i, k: (i // per_seq, 0, 0))],
        out_shape=[jax.ShapeDtypeStruct((t, d), F32), jax.ShapeDtypeStruct((t // seq, 8, d), F32)],
        scratch_shapes=[pltpu.VMEM((tm, d), F32)],
        compiler_params=_params("arbitrary", "arbitrary"))(dproj, w_all, gx0, x2, mod3)


def _grad_w_in(ht, dproj, seq, lay):
    d, t = ht.shape
    tm = seq
    n_i = t // tm

    def make_body(aliased):
        def body(*refs):
            if aliased:
                refs = refs[2:]
            ht_ref, dp_ref, gw_ref, gb_ref, acc, bacc = refs
            i = pl.program_id(1)

            @pl.when(i == 0)
            def _():
                acc[...] = jnp.zeros_like(acc)
                bacc[...] = jnp.zeros_like(bacc)

            dp = dp_ref[...]
            acc[...] += jnp.dot(ht_ref[...], dp, preferred_element_type=F32)
            bacc[...] += jnp.sum(dp.astype(F32), axis=0, keepdims=True)

            @pl.when(i == n_i - 1)
            def _():
                gw_ref[...] = acc[...].astype(BF16)
                gb_ref[...] = bacc[...]
        return body

    def call(name, tn, n_tiles, perm_tile, nat_tile, prev):
        in_specs = [pl.BlockSpec((d, tm), lambda j, i: (0, i)), pl.BlockSpec((tm, tn), lambda j, i: (i, perm_tile(j)))]
        args = [ht, dproj]
        aliases = {}
        if prev is not None:
            in_specs = [pl.BlockSpec(memory_space=pl.ANY)] * 2 + in_specs
            args = list(prev) + args
            aliases = {0: 0, 1: 1}
        return pl.pallas_call(
            make_body(prev is not None), name=name, grid=(n_tiles, n_i), in_specs=in_specs,
            out_specs=[pl.BlockSpec((d, tn), lambda j, i: (0, nat_tile(j))), pl.BlockSpec((1, tn), lambda j, i: (0, nat_tile(j)))],
            out_shape=[jax.ShapeDtypeStruct((d, lay.din), BF16), jax.ShapeDtypeStruct((1, lay.din), F32)],
            input_output_aliases=aliases,
            scratch_shapes=[pltpu.VMEM((d, tn), F32), pltpu.VMEM((1, tn), F32)],
            compiler_params=_params("arbitrary", "arbitrary"))(*args)

    first = call("grad_w_in_attn", SLAB, 2 * PAIR_SLABS, lambda j: j, lay.attn_nat_slab, None)
    base = lay.c0 // CONV_TILE
    return call("grad_w_in_rest", CONV_TILE, 6 * d // CONV_TILE, lambda j: base + j, lay.rest_nat_tile, first)


def _pack_rows(parts, width=128):
    flat = [p.reshape(-1) for p in parts]
    spans, rows = [], 0
    padded = []
    for f in flat:
        n = -(-f.shape[0] // (8 * width)) * 8
        padded.append(jnp.pad(f, (0, n * width - f.shape[0])).reshape(n, width))
        spans.append((rows, f.shape[0]))
        rows += n
    return jnp.concatenate(padded, axis=0), spans


def _unpack_rows(packed, spans, shapes, width=128):
    out = []
    for (row, n), shp in zip(spans, shapes):
        rows = -(-n // width)
        out.append(packed[row:row + rows].reshape(-1)[:n].reshape(shp))
    return out


def kernel(x, c, w_ada, b_ada, w_in, b_in, conv_w, w_proj_attn, w_proj_conv, w_out, b_out, ln_g, ln_b, loss_target, m_w_ada, m_b_ada, m_w_in, m_b_in, m_conv_w, m_w_proj_attn, m_w_proj_conv, m_w_out, m_b_out, m_ln_g, m_ln_b, v_w_ada, v_b_ada, v_w_in, v_b_in, v_conv_w, v_w_proj_attn, v_w_proj_conv, v_w_out, v_b_out, v_ln_g, v_ln_b):
    bsz, seq, d = x.shape
    t = bsz * seq
    lay = _Layout(d)
    col_sharded = [True, True, True, False, False]
    big_w = [w_ada[0], w_in[0], w_proj_attn[0], w_proj_conv[0], w_out[0]]
    big_m = [m_w_ada[0], m_w_in[0], m_w_proj_attn[0], m_w_proj_conv[0], m_w_out[0]]
    big_v = [v_w_ada[0], v_w_in[0], v_w_proj_attn[0], v_w_proj_conv[0], v_w_out[0]]
    chip = 2 * lax.axis_index("x") + lax.axis_index("y")

    cw_pad = jnp.pad(conv_w[0], ((0, 5), (0, 0)))
    (wa_f, wi_f, wpa_f, wpc_f, wo_f), cw8 = _gather_weights(_cast_bf16(big_w), col_sharded, cw_pad)
    cw_full = cw8[0:3]
    w_all = _permute_w_in(wi_f, lay)
    b_all = lay.perm_vector(b_in)

    x2 = x.reshape(t, d)
    target2 = loss_target.reshape(t, d)
    mod, c_act = _ada(c, wa_f, b_ada)
    mod3 = mod.reshape(bsz, 1, 3 * d)
    pa, = _project(x2, mod3, w_all, b_all, seq, 0, ATT, 512, BF16, False, "project_attn")
    pr, ht = _project(x2, mod3, w_all, b_all, seq, lay.c0, 6 * d, 512, F32, True, "project_rest")
    o_all, lse_all, a_in = _attn_fwd(pa, bsz, seq)
    b_in_act = _conv_fwd(pr, cw_full, bsz, seq, d)
    (dproj, da_in, db_in, gx0, dgate, small_tail, gw_pa, gw_pc, gw_out) = _tail(
        a_in, b_in_act, pr, x2, target2, mod3, wpa_f, wpc_f, wo_f, b_out, ln_g, ln_b, seq, lay)

    dproj, gcw = _conv_bwd(dproj, db_in, pr, cw_full, bsz, seq, lay)
    dproj = _attn_bwd(dproj, pa, o_all, lse_all, da_in, bsz, seq)
    grad_x2, dmod = _grad_h(dproj, w_all, gx0, x2, mod3, seq, lay)
    gw_in_bf, gb_in = _grad_w_in(ht, dproj, seq, lay)
    d_ada = jnp.concatenate([dmod[:, 0, :], dmod[:, 1, :], dgate[:, 0, :]], axis=1)
    gw_ada_bf = _grad_w_ada(c_act.T, d_ada)

    pieces = [small_tail[3], jnp.sum(d_ada, axis=0), gb_in[0], small_tail[0], small_tail[1], small_tail[2], gcw[0:3]]
    packed, spans = _pack_rows(pieces)
    summed = _all_sum_small(packed, d)
    loss = summed[0, 0]
    _, g_b_ada, g_b_in, g_b_out, g_ln_g, g_ln_b, g_cw_full = _unpack_rows(
        summed, spans, [(d,), (3 * d,), (lay.din,), (d,), (d,), (d,), (3, d)])
    g_cw = lax.dynamic_slice(g_cw_full, (0, chip * (d // N_CHIPS)), (3, d // N_CHIPS))

    views = _shard_views([gw_ada_bf, gw_in_bf, gw_pa, gw_pc, gw_out], col_sharded)
    core = lax.axis_index("c").astype(jnp.int32).reshape(1)
    place = jnp.stack([chip, lax.axis_index("c")]).astype(jnp.int32)
    got1 = _pair_exchange_halves(views)
    parts = [_pair_sum(views[w], got1[w], core, f"grads_pair_sum_{w}") for w in range(5)]
    got2 = _chip_scatter(parts, col_sharded)
    fulls = [_chip_sum(parts[w], got2[w], col_sharded[w], place, f"grads_chip_sum_{w}") for w in range(5)]
    g_big = _pair_join_halves(fulls)

    big_out = [_adamw(big_w[w], g_big[w], big_m[w], big_v[w], f"adamw_{w}") for w in range(5)]
    small_w = [b_ada, b_in, conv_w[0], b_out, ln_g, ln_b]
    small_g = [g_b_ada, g_b_in, g_cw, g_b_out, g_ln_g, g_ln_b]
    small_m = [m_b_ada, m_b_in, m_conv_w[0], m_b_out, m_ln_g, m_ln_b]
    small_v = [v_b_ada, v_b_in, v_conv_w[0], v_b_out, v_ln_g, v_ln_b]
    pw, sp = _pack_rows(small_w)
    pg, _ = _pack_rows(small_g)
    pm, _ = _pack_rows(small_m)
    pv, _ = _pack_rows(small_v)
    sd, sm, sv = _adamw(pw, pg, pm, pv, "adamw_small")
    shapes = [a.shape for a in small_w]
    sd, sm, sv = _unpack_rows(sd, sp, shapes), _unpack_rows(sm, sp, shapes), _unpack_rows(sv, sp, shapes)

    def order(wa, bA, wi, bI, cw, wpa, wpc, wo, bO, lg, lb):
        return (wa[None], bA, wi[None], bI, cw[None], wpa[None], wpc[None], wo[None], bO, lg, lb)

    sg = [g.reshape(s) for g, s in zip(small_g, shapes)]
    grads_out = order(g_big[0], sg[0], g_big[1], sg[1], sg[2], g_big[2], g_big[3], g_big[4], sg[3], sg[4], sg[5])
    outs = []
    for idx, small in enumerate((sd, sm, sv)):
        outs.append(order(big_out[0][idx], small[0], big_out[1][idx], small[1], small[2], big_out[2][idx],
                          big_out[3][idx], big_out[4][idx], small[3], small[4], small[5]))
    return (loss, grad_x2.reshape(bsz, seq, d), *grads_out, *outs[0], *outs[1], *outs[2])
```

```python
import functools

import jax
import jax.numpy as jnp
from jax import lax
from jax.experimental import pallas as pl
from jax.experimental.pallas import tpu as pltpu

F32 = jnp.float32
BF16 = jnp.bfloat16
MESH = pl.DeviceIdType.MESH

HEAD_DIM = 64
N_GROUPS = 3
DILATIONS = (1, 4, 16)
N_HEADS = 12
SUB = 128
Q_WIDTH = 768
Z_WIDTH = 256
ATT = 3 * Q_WIDTH + Z_WIDTH
SLAB = 128
PAIR_SLABS = 10
PAIR_COLS = PAIR_SLABS * SLAB
CONV_TILE = 256
ALIBI_MAX_EXP = 8.0
ALPHA = 2.0 ** 0.25
LN_EPS = 1e-5
ADAM_LR, ADAM_B1, ADAM_B2, ADAM_EPS, ADAM_WD, ADAM_STEP = 0.001, 0.9, 0.999, 1e-08, 0.01, 10
N_CHIPS = 4
N_DEV = 8
VMEM_LIMIT_V7X = 60 * 1024 * 1024
NEG = -1e30


def _params(*sem):
    return pltpu.CompilerParams(dimension_semantics=sem, vmem_limit_bytes=VMEM_LIMIT_V7X)


def _sigmoid(v):
    return 1.0 / (1.0 + jnp.exp(-v))


class _Layout:
    def __init__(self, d):
        self.d = d
        self.din = ATT + 6 * d
        c0 = 3072
        while c0 % (2 * d):
            c0 += 1024
        self.c0, self.g0, self.np = c0, c0 + 4 * d, c0 + 6 * d
        self.n_conv_tiles = d // CONV_TILE

    def attn_nat_slab(self, s):
        p, i = s // PAIR_SLABS, s % PAIR_SLABS
        return jnp.where(i < 9, (i // 3) * 6 + (i % 3) * 2 + p, 18 + p)

    def rest_nat_tile(self, t):
        n4 = 4 * self.n_conv_tiles
        conv = ATT // CONV_TILE + (t % 4) * self.n_conv_tiles + t // 4
        return jnp.where(t < n4, conv, ATT // CONV_TILE + t)

    def perm_vector(self, v):
        parts = []
        for s in range(2 * PAIR_SLABS):
            p, i = divmod(s, PAIR_SLABS)
            ns = (i // 3) * 6 + (i % 3) * 2 + p if i < 9 else 18 + p
            parts.append(v[:, ns * SLAB:(ns + 1) * SLAB])
        parts.append(jnp.zeros((1, self.c0 - ATT), v.dtype))
        for j in range(self.n_conv_tiles):
            for k in range(4):
                a = ATT + k * self.d + j * CONV_TILE
                parts.append(v[:, a:a + CONV_TILE])
        parts.append(v[:, ATT + 4 * self.d:])
        return jnp.concatenate(parts, axis=1)


def _place():
    return lax.axis_index("x"), lax.axis_index("y"), lax.axis_index("c")


def _other_chips(x, y):
    return [(1 - x, y), (x, 1 - y), (1 - x, 1 - y)]


def _shard_of(ref, col_sharded, chip, half=None):
    if col_sharded:
        cs = ref.shape[1] // N_CHIPS
        cols = pl.ds(pl.multiple_of(chip * cs, SLAB), cs)
        if half is None:
            return ref.at[:, cols]
        n = ref.shape[0] // 2
        return ref.at[pl.ds(half * n, n), cols]
    rs = ref.shape[0] // N_CHIPS
    if half is None:
        return ref.at[pl.ds(chip * rs, rs)]
    return ref.at[pl.ds(chip * rs + half * (rs // 2), rs // 2)]


def _cast_into_full(shard, col_sharded, chip, name):
    rows, cols = shard.shape
    tr = _row_tile(rows, cols)
    nb = rows // tr

    def body(chip_ref, s_ref, o_ref):
        del chip_ref
        o_ref[...] = s_ref[...].astype(BF16)

    if col_sharded:
        full, out_spec = (rows, cols * N_CHIPS), pl.BlockSpec((tr, cols), lambda i, ch: (i, ch[0]))
    else:
        full, out_spec = (rows * N_CHIPS, cols), pl.BlockSpec((tr, cols), lambda i, ch: (ch[0] * nb + i, 0))
    return pl.pallas_call(
        body, name=name,
        grid_spec=pltpu.PrefetchScalarGridSpec(num_scalar_prefetch=1, grid=(nb,),
                                               in_specs=[pl.BlockSpec((tr, cols), lambda i, ch: (i, 0))], out_specs=out_spec),
        out_shape=jax.ShapeDtypeStruct(full, BF16), compiler_params=_params("parallel"))(chip, shard)


def _gather_weights(fulls, col_sharded, small):
    n = len(fulls)

    def body(*refs):
        sm_in, outs, sm_out = refs[n], refs[n + 1:2 * n + 1], refs[2 * n + 1]
        send, recv, fsend, frecv, lsem, ssend, srecv = refs[2 * n + 2:]
        x, y, c = _place()
        mine = 2 * x + y
        sibling = (x, y, 1 - c)
        chips = _other_chips(x, y)

        local = [pltpu.make_async_copy(sm_in, _shard_of(sm_out, True, mine), lsem)]
        for cp in local:
            cp.start()
        sends = []
        for k, (cx, cy) in enumerate(chips):
            cp = pltpu.make_async_remote_copy(src_ref=sm_in, dst_ref=_shard_of(sm_out, True, mine), send_sem=ssend.at[k],
                                              recv_sem=srecv.at[k], device_id=(cx, cy, c), device_id_type=MESH)
            cp.start()
            sends.append(cp)
        for k, (cx, cy) in enumerate(chips):
            for w in range(n):
                own_half = _shard_of(outs[w], col_sharded[w], mine, c)
                cp = pltpu.make_async_remote_copy(
                    src_ref=own_half, dst_ref=own_half,
                    send_sem=send.at[w, k], recv_sem=recv.at[w, k], device_id=(cx, cy, c), device_id_type=MESH)
                cp.start()
                sends.append(cp)
        for k, (cx, cy) in enumerate(chips):
            theirs = 2 * cx + cy
            for w in range(n):
                landed = _shard_of(outs[w], col_sharded[w], theirs, c)
                pltpu.make_async_remote_copy(src_ref=landed, dst_ref=landed, send_sem=send.at[w, k], recv_sem=recv.at[w, k],
                                             device_id=(cx, cy, c), device_id_type=MESH).wait_recv()
                cp = pltpu.make_async_remote_copy(src_ref=landed, dst_ref=landed, send_sem=fsend.at[w, k],
                                                  recv_sem=frecv.at[w, k], device_id=sibling, device_id_type=MESH)
                cp.start()
                sends.append(cp)
        for k, (cx, cy) in enumerate(chips):
            theirs = 2 * cx + cy
            for w in range(n):
                passed = _shard_of(outs[w], col_sharded[w], theirs, 1 - c)
                pltpu.make_async_remote_copy(src_ref=passed, dst_ref=passed, send_sem=fsend.at[w, k], recv_sem=frecv.at[w, k],
                                             device_id=sibling, device_id_type=MESH).wait_recv()
        for k, (cx, cy) in enumerate(chips):
            theirs = _shard_of(sm_out, True, 2 * cx + cy)
            pltpu.make_async_remote_copy(src_ref=theirs, dst_ref=theirs, send_sem=ssend.at[k], recv_sem=srecv.at[k],
                                         device_id=(cx, cy, c), device_id_type=MESH).wait_recv()
        for cp in sends:
            cp.wait_send()
        for cp in local:
            cp.wait()

    any_spec = pl.BlockSpec(memory_space=pl.ANY)
    outs = pl.pallas_call(
        body, name="gather_weights",
        out_shape=[jax.ShapeDtypeStruct(f.shape, BF16) for f in fulls]
        + [jax.ShapeDtypeStruct((small.shape[0], small.shape[1] * N_CHIPS), small.dtype)],
        in_specs=[any_spec] * (n + 1), out_specs=[any_spec] * (n + 1), input_output_aliases={w: w for w in range(n)},
        scratch_shapes=[pltpu.SemaphoreType.DMA((n, 3)), pltpu.SemaphoreType.DMA((n, 3)),
                        pltpu.SemaphoreType.DMA((n, 3)), pltpu.SemaphoreType.DMA((n, 3)), pltpu.SemaphoreType.DMA,
                        pltpu.SemaphoreType.DMA((3,)), pltpu.SemaphoreType.DMA((3,))],
    )(*fulls, small)
    return outs[:n], outs[n]


def _shard_views(gs, col_sharded):
    return [g.reshape(1, *g.shape) if cs else g.reshape(N_CHIPS, g.shape[0] // N_CHIPS, g.shape[1])
            for g, cs in zip(gs, col_sharded)]


DMA_CHUNK_BYTES = 1 << 20


def _chunk_rows(shape, itemsize):
    s, rows, cols = shape
    n = 1
    while s * (rows // n) * cols * itemsize > DMA_CHUNK_BYTES and (rows // n) % 32 == 0:
        n *= 2
    return rows // n


def _send_rows(src, src_row0, dst, dst_row0, rows, send_sem, recv_sem, device):
    step = _chunk_rows((src.shape[0], rows, src.shape[2]), src.dtype.itemsize)
    for r in range(0, rows, step):
        pltpu.make_async_remote_copy(src_ref=src.at[:, pl.ds(src_row0 + r, step)], dst_ref=dst.at[:, pl.ds(dst_row0 + r, step)],
                                     send_sem=send_sem, recv_sem=recv_sem, device_id=device, device_id_type=MESH).start()


def _pair_exchange_halves(views):
    n = len(views)
    half_shapes = [(v.shape[0], v.shape[1] // 2, v.shape[2]) for v in views]

    def body(*refs):
        ins, got = refs[:n], refs[n:2 * n]
        send, recv = refs[2 * n:]
        x, y, c = _place()
        sibling = (x, y, 1 - c)
        for w in range(n):
            hr = half_shapes[w][1]
            _send_rows(ins[w], (1 - c) * hr, got[w], 0, hr, send.at[w], recv.at[w], sibling)
        for w in range(n):
            hr = half_shapes[w][1]
            pltpu.make_async_remote_copy(src_ref=ins[w].at[:, pl.ds((1 - c) * hr, hr)], dst_ref=got[w], send_sem=send.at[w],
                                         recv_sem=recv.at[w], device_id=sibling, device_id_type=MESH).wait()

    any_spec = pl.BlockSpec(memory_space=pl.ANY)
    return pl.pallas_call(
        body, name="grads_pair_exchange",
        out_shape=[jax.ShapeDtypeStruct(s, v.dtype) for s, v in zip(half_shapes, views)],
        in_specs=[any_spec] * n, out_specs=[any_spec] * n,
        scratch_shapes=[pltpu.SemaphoreType.DMA((n,)), pltpu.SemaphoreType.DMA((n,))],
    )(*views)


def _pair_sum(view, got, core, name):
    s, r, cols = view.shape
    hr = r // 2
    tr = _row_tile(hr, cols)
    nb = hr // tr

    def body(core_ref, a_ref, b_ref, o_ref):
        del core_ref
        o_ref[...] = (a_ref[...].astype(F32) + b_ref[...].astype(F32)).astype(BF16)

    same = pl.BlockSpec((None, tr, cols), lambda j, i, core_ref: (j, i, 0))
    return pl.pallas_call(
        body, name=name,
        grid_spec=pltpu.PrefetchScalarGridSpec(
            num_scalar_prefetch=1, grid=(s, nb),
            in_specs=[pl.BlockSpec((None, tr, cols), lambda j, i, core_ref: (j, core_ref[0] * nb + i, 0)), same],
            out_specs=same),
        out_shape=jax.ShapeDtypeStruct((s, hr, cols), BF16), compiler_params=_params("parallel", "parallel"))(core, view, got)


def _piece_cols(part, col_sharded):
    return part.shape[2] // N_CHIPS if col_sharded else part.shape[2]


def _chip_scatter(parts, col_sharded):
    n = len(parts)

    def piece(ref, cs, chip):
        if cs:
            w = ref.shape[2] // N_CHIPS
            return ref.at[:, :, pl.ds(pl.multiple_of(chip * w, SLAB), w)]
        return ref.at[pl.ds(chip, 1)]

    def body(*refs):
        ins, got = refs[:n], refs[n:2 * n]
        send, recv = refs[2 * n:]
        x, y, c = _place()
        sends = []
        for k, (cx, cy) in enumerate(_other_chips(x, y)):
            for w in range(n):
                cp = pltpu.make_async_remote_copy(
                    src_ref=piece(ins[w], col_sharded[w], 2 * cx + cy), dst_ref=got[w].at[pl.ds(k, 1)], send_sem=send.at[w, k],
                    recv_sem=recv.at[w, k], device_id=(cx, cy, c), device_id_type=MESH)
                cp.start()
                sends.append(cp)
        for cp in sends:
            cp.wait()

    any_spec = pl.BlockSpec(memory_space=pl.ANY)
    return pl.pallas_call(
        body, name="grads_chip_scatter",
        out_shape=[jax.ShapeDtypeStruct((3, p.shape[1], _piece_cols(p, cs)), p.dtype) for p, cs in zip(parts, col_sharded)],
        in_specs=[any_spec] * n, out_specs=[any_spec] * n,
        scratch_shapes=[pltpu.SemaphoreType.DMA((n, 3)), pltpu.SemaphoreType.DMA((n, 3))],
    )(*parts)


def _chip_sum(part, got, col_sharded, place, name):
    _, hr, _ = part.shape
    cols = _piece_cols(part, col_sharded)
    tr = _row_tile(hr, cols)
    nb = hr // tr

    def body(place_ref, own_ref, g0_ref, g1_ref, g2_ref, o_ref):
        del place_ref
        acc = own_ref[...].astype(F32) + g0_ref[...].astype(F32)
        o_ref[...] = acc + g1_ref[...].astype(F32) + g2_ref[...].astype(F32)

    if col_sharded:
        own = pl.BlockSpec((None, tr, cols), lambda i, pr: (0, i, pr[0]))
    else:
        own = pl.BlockSpec((None, tr, cols), lambda i, pr: (pr[0], i, 0))
    others = [pl.BlockSpec((None, tr, cols), lambda i, pr, k=k: (k, i, 0)) for k in range(3)]
    return pl.pallas_call(
        body, name=name,
        grid_spec=pltpu.PrefetchScalarGridSpec(
            num_scalar_prefetch=1, grid=(nb,), in_specs=[own] + others,
            out_specs=pl.BlockSpec((tr, cols), lambda i, pr: (pr[1] * nb + i, 0))),
        out_shape=jax.ShapeDtypeStruct((2 * hr, cols), F32), compiler_params=_params("parallel"))(place, part, got, got, got)


def _pair_join_halves(fulls):
    n = len(fulls)
    views = [f.reshape(1, *f.shape) for f in fulls]

    def body(*refs):
        outs = refs[n:2 * n]
        send, recv = refs[2 * n:]
        x, y, c = _place()
        sibling = (x, y, 1 - c)
        for w in range(n):
            hr = outs[w].shape[1] // 2
            _send_rows(outs[w], c * hr, outs[w], c * hr, hr, send.at[w], recv.at[w], sibling)
        for w in range(n):
            hr = outs[w].shape[1] // 2
            pltpu.make_async_remote_copy(
                src_ref=outs[w].at[:, pl.ds(c * hr, hr)], dst_ref=outs[w].at[:, pl.ds((1 - c) * hr, hr)], send_sem=send.at[w],
                recv_sem=recv.at[w], device_id=sibling, device_id_type=MESH).wait()

    any_spec = pl.BlockSpec(memory_space=pl.ANY)
    outs = pl.pallas_call(
        body, name="grads_pair_join",
        out_shape=[jax.ShapeDtypeStruct(v.shape, v.dtype) for v in views],
        in_specs=[any_spec] * n, out_specs=[any_spec] * n, input_output_aliases={w: w for w in range(n)},
        scratch_shapes=[pltpu.SemaphoreType.DMA((n,)), pltpu.SemaphoreType.DMA((n,))],
    )(*views)
    return [o[0] for o in outs]


def _all_sum_small(vec, d):
    rows = vec.shape[0]

    def body(v_ref, o_ref, all_ref, send, recv):
        x, y, c = _place()
        me = 4 * x + 2 * y + c
        all_ref[me] = v_ref[...]
        copies = []
        for k in range(1, N_DEV):
            fx, fy, fc = (k >> 2) & 1, (k >> 1) & 1, k & 1
            peer = (x ^ fx, y ^ fy, c ^ fc)
            cp = pltpu.make_async_remote_copy(src_ref=v_ref, dst_ref=all_ref.at[me], send_sem=send.at[k - 1],
                                              recv_sem=recv.at[k - 1], device_id=peer, device_id_type=MESH)
            cp.start()
            copies.append((cp, 4 * peer[0] + 2 * peer[1] + peer[2]))
        for k, (cp, src) in enumerate(copies):
            pltpu.make_async_remote_copy(src_ref=v_ref, dst_ref=all_ref.at[src], send_sem=send.at[k], recv_sem=recv.at[k],
                                         device_id=(x, y, c), device_id_type=MESH).wait_recv()
        for cp, _ in copies:
            cp.wait_send()
        total = all_ref[0]
        for i in range(1, N_DEV):
            total = total + all_ref[i]
        o_ref[...] = total
        loss = 0.5 / d * jnp.sum(total[0:8, :])
        o_ref[0:8, :] = jnp.full((8, 128), loss, F32)

    return pl.pallas_call(
        body, name="all_sum_small",
        out_shape=jax.ShapeDtypeStruct((rows, 128), F32),
        in_specs=[pl.BlockSpec(memory_space=pltpu.VMEM)], out_specs=pl.BlockSpec(memory_space=pltpu.VMEM),
        scratch_shapes=[pltpu.VMEM((N_DEV, rows, 128), F32), pltpu.SemaphoreType.DMA((N_DEV - 1,)),
                        pltpu.SemaphoreType.DMA((N_DEV - 1,))],
        compiler_params=pltpu.CompilerParams(vmem_limit_bytes=VMEM_LIMIT_V7X),
    )(vec)


def _row_tile(rows, cols, itemsize=4, budget=2 << 20):
    t = rows
    while t * cols * itemsize > budget and t % 16 == 0:
        t //= 2
    return t


def _adamw(w, g, m, v, name):
    rows, cols = w.shape
    tr = _row_tile(rows, cols, budget=1 << 20)

    def body(w_ref, g_ref, m_ref, v_ref, d_ref, nm_ref, nv_ref):
        g_ = g_ref[...]
        nm = ADAM_B1 * m_ref[...] + (1.0 - ADAM_B1) * g_
        nv = ADAM_B2 * v_ref[...] + (1.0 - ADAM_B2) * (g_ * g_)
        m_hat = nm / (1.0 - ADAM_B1 ** ADAM_STEP)
        v_hat = nv / (1.0 - ADAM_B2 ** ADAM_STEP)
        d_ref[...] = -ADAM_LR * (m_hat / (jnp.sqrt(v_hat) + ADAM_EPS) + ADAM_WD * w_ref[...])
        nm_ref[...] = nm
        nv_ref[...] = nv

    spec = pl.BlockSpec((tr, cols), lambda i: (i, 0))
    shp = jax.ShapeDtypeStruct((rows, cols), F32)
    return pl.pallas_call(body, name=name, grid=(rows // tr,), in_specs=[spec] * 4, out_specs=[spec] * 3,
                          out_shape=[shp] * 3, compiler_params=_params("parallel"))(w, g, m, v)


def _ada(c, w_ada, b_ada):
    bsz, d = c.shape

    def body(c_ref, w_ref, b_ref, mod_ref, act_ref):
        cc = c_ref[...]
        act = cc * _sigmoid(cc)
        act_ref[...] = act
        mod_ref[...] = jnp.dot(act.astype(BF16), w_ref[...], preferred_element_type=F32) + b_ref[...]

    vm = pl.BlockSpec(memory_space=pltpu.VMEM)
    return pl.pallas_call(body, name="ada", in_specs=[vm] * 3, out_specs=[vm] * 2,
                          out_shape=[jax.ShapeDtypeStruct((bsz, 3 * d), F32), jax.ShapeDtypeStruct((bsz, d), F32)],
                          compiler_params=pltpu.CompilerParams(vmem_limit_bytes=VMEM_LIMIT_V7X))(c, w_ada, b_ada)


def _grad_w_ada(act_t, d_ada):
    d, bsz = act_t.shape
    n3 = d_ada.shape[1]
    tn = n3 // 3

    def body(a_ref, g_ref, o_ref):
        a, g = a_ref[...], g_ref[...]
        acc = a[:, 0:1] * g[0:1, :]
        for b in range(1, bsz):
            acc = acc + a[:, b:b + 1] * g[b:b + 1, :]
        o_ref[...] = acc.astype(BF16)

    return pl.pallas_call(body, name="grad_w_ada", grid=(n3 // tn,),
                          in_specs=[pl.BlockSpec((d, bsz), lambda j: (0, 0)), pl.BlockSpec((bsz, tn), lambda j: (0, j))],
                          out_specs=pl.BlockSpec((d, tn), lambda j: (0, j)),
                          out_shape=jax.ShapeDtypeStruct((d, n3), BF16), compiler_params=_params("parallel"))(act_t, d_ada)


def _permute_w_in(w_nat, lay):
    d = lay.d

    def copy_body(src_ref, dst_in_ref, dst_ref):
        del dst_in_ref
        dst_ref[...] = src_ref[...]

    def first_body(src_ref, dst_ref):
        dst_ref[...] = src_ref[...]

    w_all = pl.pallas_call(
        first_body, name="permute_w_attn", grid=(2 * PAIR_SLABS,),
        in_specs=[pl.BlockSpec((d, SLAB), lambda s: (0, lay.attn_nat_slab(s)))],
        out_specs=pl.BlockSpec((d, SLAB), lambda s: (0, s)),
        out_shape=jax.ShapeDtypeStruct((d, lay.np), BF16), compiler_params=_params("arbitrary"))(w_nat)
    n_rest = 6 * d // CONV_TILE
    base = lay.c0 // CONV_TILE
    return pl.pallas_call(
        copy_body, name="permute_w_rest", grid=(n_rest,),
        in_specs=[pl.BlockSpec((d, CONV_TILE), lambda t: (0, lay.rest_nat_tile(t))), pl.BlockSpec(memory_space=pl.ANY)],
        out_specs=pl.BlockSpec((d, CONV_TILE), lambda t: (0, base + t)),
        out_shape=jax.ShapeDtypeStruct((d, lay.np), BF16), input_output_aliases={1: 0},
        compiler_params=_params("arbitrary"))(w_nat, w_all)


def _project(x2, mod3, w_all, b_all, seq, col0, ncols, tn, out_dtype, want_ht, name):
    t, d = x2.shape
    tm = min(1024, seq)
    per_seq = seq // tm
    j0 = col0 // tn

    def body(x_ref, mod_ref, w_ref, b_ref, o_ref, *rest):
        h_ref = rest[-1]

        @pl.when(pl.program_id(1) == 0)
        def _():
            h = x_ref[...] * (1.0 + mod_ref[:, d:2 * d]) + mod_ref[:, 0:d]
            h_ref[...] = h.astype(BF16)
            if want_ht:
                rest[0][...] = h.T.astype(BF16)

        o_ref[...] = (jnp.dot(h_ref[...], w_ref[...], preferred_element_type=F32) + b_ref[...]).astype(out_dtype)

    out_shape = [jax.ShapeDtypeStruct((t, ncols), out_dtype)]
    out_specs = [pl.BlockSpec((tm, tn), lambda i, j: (i, j))]
    if want_ht:
        out_shape.append(jax.ShapeDtypeStruct((d, t), BF16))
        out_specs.append(pl.BlockSpec((d, tm), lambda i, j: (0, i)))
    return pl.pallas_call(
        body, name=name, grid=(t // tm, ncols // tn),
        in_specs=[pl.BlockSpec((tm, d), lambda i, j: (i, 0)),
                  pl.BlockSpec((None, 1, 3 * d), lambda i, j: (i // per_seq, 0, 0)),
                  pl.BlockSpec((d, tn), lambda i, j: (0, j0 + j)),
                  pl.BlockSpec((1, tn), lambda i, j: (0, j0 + j))],
        out_specs=out_specs, out_shape=out_shape,
        scratch_shapes=[pltpu.VMEM((tm, d), BF16)],
        compiler_params=_params("arbitrary", "arbitrary"))(x2, mod3, w_all, b_all)


def _slope(g, p, hh):
    head = 4 * g + 2 * p + hh
    return 2.0 ** (-ALIBI_MAX_EXP * (head + 1.0) / N_HEADS)


def _ld_rows(ref, start, n, stride):
    if stride == 1:
        return ref[pl.ds(start, n), :]
    return ref[pl.ds(start, n, stride=stride), :]


def _st_rows(ref, start, n, stride, val):
    if stride == 1:
        ref[pl.ds(start, n), :] = val
    else:
        ref[pl.ds(start, n, stride=stride), :] = val


def _sub_blocks(g, seq):
    return seq // DILATIONS[g] // SUB


def _key_rows(g, seq):
    return SUB if _sub_blocks(g, seq) == 1 else 2 * SUB


def _fill_bias(bias_ref, p, seq):
    for g in range(N_GROUPS):
        nk = _key_rows(g, seq)
        diff = lax.broadcasted_iota(jnp.int32, (SUB, nk), 0) - lax.broadcasted_iota(jnp.int32, (SUB, nk), 1)
        for i, off in enumerate((0, SUB)):
            if i == 1 and nk == SUB:
                continue
            delta = diff + off
            ok = (delta >= 0) & (delta <= SUB)
            dist = (delta * DILATIONS[g]).astype(F32)
            for hh in range(2):
                slope = jnp.where(p == 0, _slope(g, 0, hh), _slope(g, 1, hh))
                bias_ref[g, i, hh, :, 0:nk] = jnp.where(ok, -slope * dist, NEG)


def _to_sub_major(pa_ref, col, sub_ref, stage, dil, seq):
    cols = slice(col * SLAB, (col + 1) * SLAB)
    if dil == 1:
        sub_ref[...] = pa_ref[:, cols]
        return
    n = seq // dil
    stage[...] = pa_ref[:, cols].astype(F32)
    for r in range(dil):
        sub_ref[pl.ds(r * n, n), :] = stage[pl.ds(r, n, stride=dil), :].astype(BF16)


def _block_rows(it, g, seq):
    dil, nb = DILATIONS[g], _sub_blocks(g, seq)
    row0 = pl.multiple_of(it * SUB, SUB)
    if nb == 1:
        return row0, row0, 0, it
    blk = it % nb
    first = blk == 0
    krow0 = pl.multiple_of(row0 - jnp.where(first, 0, SUB), SUB)
    nat = row0 if dil == 1 else it // nb + dil * SUB * blk
    return row0, krow0, jnp.where(first, 0, 1), nat


def _nt(a, b):
    return lax.dot_general(a, b, (((1,), (1,)), ((), ())), preferred_element_type=F32)


def _tn(a, b):
    return lax.dot_general(a, b, (((0,), (0,)), ((), ())), preferred_element_type=F32)


def _head_sums(t):
    rows = t.shape[0]
    lo = jnp.broadcast_to(jnp.sum(t[:, :HEAD_DIM], axis=-1, keepdims=True), (rows, HEAD_DIM))
    hi = jnp.broadcast_to(jnp.sum(t[:, HEAD_DIM:], axis=-1, keepdims=True), (rows, HEAD_DIM))
    return jnp.concatenate([lo, hi], axis=-1)


def _attn_fwd(pa, bsz, seq):
    t = pa.shape[0]
    n_blocks = seq // SUB
    chunk = 256

    def body(pa_ref, o_ref, lse_ref, a_ref, sub, stage, bias_ref):
        p = pl.program_id(1)
        _fill_bias(bias_ref, p, seq)
        head0 = lax.broadcasted_iota(jnp.int32, (SUB, SLAB), 1) < HEAD_DIM
        for g in range(N_GROUPS):
            dil = DILATIONS[g]
            for w in range(3):
                _to_sub_major(pa_ref, 3 * w + g, sub.at[w], stage, dil, seq)
            nk = _key_rows(g, seq)

            def block(it, g=g, dil=dil, nk=nk):
                row0, krow0, bi, nat = _block_rows(it, g, seq)
                q = sub[0, pl.ds(row0, SUB), :]
                kw = sub[1, pl.ds(krow0, nk), :]
                vw = sub[2, pl.ds(krow0, nk), :]
                outs, lses = [], []
                for hh in range(2):
                    qm = jnp.where(head0 if hh == 0 else ~head0, q, jnp.zeros_like(q)) * (HEAD_DIM ** -0.5)
                    s = _nt(qm, kw) + bias_ref[g, bi, hh, :, 0:nk]
                    m = jnp.max(s, axis=-1, keepdims=True)
                    e = jnp.exp(s - m)
                    den = jnp.sum(e, axis=-1, keepdims=True)
                    outs.append(jnp.dot((e * (1.0 / den)).astype(BF16), vw, preferred_element_type=F32))
                    lses.append(m + jnp.log(den))
                _st_rows(o_ref.at[g], nat, SUB, dil, jnp.where(head0, outs[0], outs[1]))
                _st_rows(lse_ref.at[g], nat, SUB, dil, jnp.where(head0, lses[0], lses[1]))

            def two_blocks(i, carry, block=block):
                block(2 * i)
                block(2 * i + 1)
                return carry

            lax.fori_loop(0, n_blocks // 2, two_blocks, 0)

        def mix(i, carry):
            rows = pl.ds(pl.multiple_of(i * chunk, chunk), chunk)
            l0, l1, l2 = lse_ref[0, rows, :], lse_ref[1, rows, :], lse_ref[2, rows, :]
            m = jnp.maximum(jnp.maximum(l0, l1), l2)
            e0, e1, e2 = jnp.exp(l0 - m), jnp.exp(l1 - m), jnp.exp(l2 - m)
            tot = e0 + e1 + e2
            o = (e0 / tot) * o_ref[0, rows, :] + (e1 / tot) * o_ref[1, rows, :] + (e2 / tot) * o_ref[2, rows, :]
            z = pa_ref[rows, 9 * SLAB:10 * SLAB].astype(F32)
            a_ref[rows, :] = (o * (z * _sigmoid(z))).astype(BF16)
            return carry

        lax.fori_loop(0, seq // chunk, mix, 0)

    big = jax.ShapeDtypeStruct((N_GROUPS, t, 2 * SLAB), F32)
    return pl.pallas_call(
        body, name="attn_fwd", grid=(bsz, 2),
        in_specs=[pl.BlockSpec((seq, PAIR_COLS), lambda b, p: (b, p))],
        out_specs=[pl.BlockSpec((N_GROUPS, seq, SLAB), lambda b, p: (0, b, p)),
                   pl.BlockSpec((N_GROUPS, seq, SLAB), lambda b, p: (0, b, p)),
                   pl.BlockSpec((seq, SLAB), lambda b, p: (b, p))],
        out_shape=[big, big, jax.ShapeDtypeStruct((t, 2 * SLAB), BF16)],
        scratch_shapes=[pltpu.VMEM((3, seq, SLAB), BF16), pltpu.VMEM((seq, SLAB), F32),
                        pltpu.VMEM((N_GROUPS, 2, 2, SUB, 2 * SUB), F32)],
        compiler_params=_params("arbitrary", "arbitrary"))(pa)


def _shift_down(v, k):
    rows = lax.broadcasted_iota(jnp.int32, v.shape, 0)
    return jnp.where(rows >= k, pltpu.roll(v, k, 0), 0.0)


def _shift_up(v, k):
    n = v.shape[0]
    rows = lax.broadcasted_iota(jnp.int32, v.shape, 0)
    return jnp.where(rows < n - k, pltpu.roll(v, n - k, 0), 0.0)


def _conv_fwd(pr, conv_w, bsz, seq, d):
    t = pr.shape[0]
    ct = CONV_TILE

    def body(p_ref, cw_ref, o_ref):
        u = p_ref[:, 2 * ct:3 * ct] * p_ref[:, 0:ct]
        cw = cw_ref[...]
        conv = cw[0:1, :] * _shift_down(u, 2)
        conv = conv + cw[1:2, :] * _shift_down(u, 1)
        conv = conv + cw[2:3, :] * u
        z = p_ref[:, 3 * ct:4 * ct]
        o_ref[...] = (p_ref[:, ct:2 * ct] * conv * (z * _sigmoid(z))).astype(BF16)

    return pl.pallas_call(
        body, name="conv_fwd", grid=(bsz, d // ct),
        in_specs=[pl.BlockSpec((seq, 4 * ct), lambda b, j: (b, j)), pl.BlockSpec((3, ct), lambda b, j: (0, j))],
        out_specs=pl.BlockSpec((seq, ct), lambda b, j: (b, j)),
        out_shape=jax.ShapeDtypeStruct((t, d), BF16), compiler_params=_params("parallel", "parallel"))(pr, conv_w)


def _tail(a_in, b_in, pr, x2, target2, mod3, w_pa, w_pc, w_out, b_out, ln_g, ln_b, seq, lay):
    t, d = x2.shape
    tm = 256
    per_seq = seq // tm
    n_steps = t // tm
    gate_blk = 4 * d // d

    def nt(a, b):
        return lax.dot_general(a, b, (((1,), (1,)), ((), ())), preferred_element_type=F32)

    def tn(a, b):
        return lax.dot_general(a, b, (((0,), (0,)), ((), ())), preferred_element_type=F32)

    def body(a_ref, b_ref, ga_ref, gb_ref, x_ref, tg_ref, mod_ref, wpa_ref, wpc_ref, wo_ref, bo_ref, lg_ref, lb_ref,
             dpg_ref, da_ref, db_ref, gx_ref, dgate_ref, small_ref, gwpa_hbm, gwpc_hbm, gwo_hbm,
             acc_pa, acc_pc, acc_o, sem):
        i = pl.program_id(0)

        @pl.when(i == 0)
        def _():
            acc_pa[...] = jnp.zeros_like(acc_pa)
            acc_pc[...] = jnp.zeros_like(acc_pc)
            acc_o[...] = jnp.zeros_like(acc_o)
            small_ref[...] = jnp.zeros_like(small_ref)

        @pl.when(i % per_seq == 0)
        def _():
            dgate_ref[...] = jnp.zeros_like(dgate_ref)

        a_bf, b_bf = a_ref[...], b_ref[...]
        y_attn = jnp.dot(a_bf, wpa_ref[...], preferred_element_type=F32)
        y_conv = jnp.dot(b_bf, wpc_ref[...], preferred_element_type=F32)
        sa, sb = _sigmoid(ga_ref[...]), _sigmoid(gb_ref[...])
        merged = (sa * y_attn + sb * y_conv).astype(BF16)
        mo = jnp.dot(merged, wo_ref[...], preferred_element_type=F32) + bo_ref[...]
        gate = mod_ref[:, 2 * d:3 * d]
        r = ALPHA * x_ref[...] + gate * mo
        mu = jnp.mean(r, axis=-1, keepdims=True)
        cen = r - mu
        var = jnp.mean(cen * cen, axis=-1, keepdims=True)
        rstd = lax.rsqrt(var + LN_EPS)
        xhat = cen * rstd
        err = xhat * lg_ref[...] + lb_ref[...] - tg_ref[...]
        dy = err * (1.0 / d)
        dxhat = dy * lg_ref[...]
        dr = rstd * (dxhat - jnp.mean(dxhat, axis=-1, keepdims=True) - xhat * jnp.mean(dxhat * xhat, axis=-1, keepdims=True))
        gx_ref[...] = ALPHA * dr
        dgate_ref[...] += jnp.sum(dr * mo, axis=0, keepdims=True)
        d_mo = dr * gate
        small_ref[0:1, :] += jnp.sum(d_mo, axis=0, keepdims=True)
        small_ref[1:2, :] += jnp.sum(dy * xhat, axis=0, keepdims=True)
        small_ref[2:3, :] += jnp.sum(dy, axis=0, keepdims=True)
        small_ref[3:4, :] += jnp.sum(err * err, axis=0, keepdims=True)
        d_mo_bf = d_mo.astype(BF16)
        acc_o[...] += tn(merged, d_mo_bf)
        dmerged = nt(d_mo_bf, wo_ref[...])
        dy_attn = (dmerged * sa).astype(BF16)
        dy_conv = (dmerged * sb).astype(BF16)
        dpg_ref[:, 0:d] = (dmerged * y_attn * sa * (1.0 - sa)).astype(BF16)
        dpg_ref[:, d:2 * d] = (dmerged * y_conv * sb * (1.0 - sb)).astype(BF16)
        acc_pa[...] += tn(a_bf, dy_attn)
        acc_pc[...] += tn(b_bf, dy_conv)
        da_ref[...] = nt(dy_attn, wpa_ref[...])
        db_ref[...] = nt(dy_conv, wpc_ref[...])

        @pl.when(i == n_steps - 1)
        def _():
            copies = [pltpu.make_async_copy(acc_pa, gwpa_hbm, sem.at[0]), pltpu.make_async_copy(acc_pc, gwpc_hbm, sem.at[1]),
                      pltpu.make_async_copy(acc_o, gwo_hbm, sem.at[2])]
            for cp in copies:
                cp.start()
            for cp in copies:
                cp.wait()

    row = lambda w: pl.BlockSpec((tm, w), lambda i: (i, 0))
    const = lambda shp: pl.BlockSpec(shp, lambda i: (0,) * len(shp), pipeline_mode=pl.Buffered(1))
    any_spec = pl.BlockSpec(memory_space=pl.ANY)
    return pl.pallas_call(
        body, name="tail", grid=(n_steps,),
        in_specs=[row(Z_WIDTH), row(d),
                  pl.BlockSpec((tm, d), lambda i: (i, gate_blk)), pl.BlockSpec((tm, d), lambda i: (i, gate_blk + 1)),
                  row(d), row(d), pl.BlockSpec((None, 1, 3 * d), lambda i: (i // per_seq, 0, 0)),
                  const((Z_WIDTH, d)), const((d, d)), const((d, d)), const((1, d)), const((1, d)), const((1, d))],
        out_specs=[pl.BlockSpec((tm, 2 * d), lambda i: (i, lay.g0 // (2 * d))), row(Z_WIDTH), row(d), row(d),
                   pl.BlockSpec((None, 1, d), lambda i: (i // per_seq, 0, 0)), pl.BlockSpec((8, d), lambda i: (0, 0)),
                   any_spec, any_spec, any_spec],
        out_shape=[jax.ShapeDtypeStruct((t, lay.np), BF16), jax.ShapeDtypeStruct((t, Z_WIDTH), F32),
                   jax.ShapeDtypeStruct((t, d), F32), jax.ShapeDtypeStruct((t, d), F32),
                   jax.ShapeDtypeStruct((t // seq, 1, d), F32), jax.ShapeDtypeStruct((8, d), F32),
                   jax.ShapeDtypeStruct((Z_WIDTH, d), F32), jax.ShapeDtypeStruct((d, d), F32),
                   jax.ShapeDtypeStruct((d, d), F32)],
        scratch_shapes=[pltpu.VMEM((Z_WIDTH, d), F32), pltpu.VMEM((d, d), F32), pltpu.VMEM((d, d), F32),
                        pltpu.SemaphoreType.DMA((3,))],
        compiler_params=_params("arbitrary"),
    )(a_in, b_in, pr, pr, x2, target2, mod3, w_pa, w_pc, w_out, b_out, ln_g, ln_b)


def _conv_bwd(dproj, db, pr, conv_w, bsz, seq, lay):
    d = lay.d
    ct = CONV_TILE
    base = lay.c0 // (4 * ct)

    def body(dp_in, db_ref, p_ref, cw_ref, dp_ref, gcw_ref):
        del dp_in
        u_x, g_b, g_c, z = p_ref[:, 0:ct], p_ref[:, ct:2 * ct], p_ref[:, 2 * ct:3 * ct], p_ref[:, 3 * ct:4 * ct]
        cw = cw_ref[...]
        u = g_c * u_x
        u1, u2 = _shift_down(u, 1), _shift_down(u, 2)
        conv = cw[0:1, :] * u2 + cw[1:2, :] * u1 + cw[2:3, :] * u
        sig = _sigmoid(z)
        sl = z * sig
        dbv = db_ref[...]
        gbc = g_b * conv
        dp_ref[:, ct:2 * ct] = (dbv * sl * conv).astype(BF16)
        dp_ref[:, 3 * ct:4 * ct] = (dbv * gbc * (sig * (1.0 + z * (1.0 - sig)))).astype(BF16)
        dconv = dbv * sl * g_b

        @pl.when(pl.program_id(1) == 0)
        def _():
            gcw_ref[...] = jnp.zeros_like(gcw_ref)

        gcw_ref[0:1, :] += jnp.sum(dconv * u2, axis=0, keepdims=True)
        gcw_ref[1:2, :] += jnp.sum(dconv * u1, axis=0, keepdims=True)
        gcw_ref[2:3, :] += jnp.sum(dconv * u, axis=0, keepdims=True)
        du = cw[2:3, :] * dconv + cw[1:2, :] * _shift_up(dconv, 1) + cw[0:1, :] * _shift_up(dconv, 2)
        dp_ref[:, 0:ct] = (du * g_c).astype(BF16)
        dp_ref[:, 2 * ct:3 * ct] = (du * u_x).astype(BF16)

    return pl.pallas_call(
        body, name="conv_bwd", grid=(d // ct, bsz),
        in_specs=[pl.BlockSpec(memory_space=pl.ANY), pl.BlockSpec((seq, ct), lambda j, b: (b, j)),
                  pl.BlockSpec((seq, 4 * ct), lambda j, b: (b, j)), pl.BlockSpec((3, ct), lambda j, b: (0, j))],
        out_specs=[pl.BlockSpec((seq, 4 * ct), lambda j, b: (b, base + j)), pl.BlockSpec((8, ct), lambda j, b: (0, j))],
        out_shape=[jax.ShapeDtypeStruct(dproj.shape, BF16), jax.ShapeDtypeStruct((8, d), F32)],
        input_output_aliases={0: 0}, compiler_params=_params("arbitrary", "arbitrary"))(dproj, db, pr, conv_w)


def _attn_bwd(dproj, pa, o_all, lse_all, da, bsz, seq):
    n_blocks = seq // SUB
    chunk = 256

    def body(dp_in, pa_ref, o_ref, lse_ref, da_ref, dp_ref, sub, stage, dsub, dog, cvec, bias_ref):
        del dp_in
        p = pl.program_id(1)
        _fill_bias(bias_ref, p, seq)
        head0 = lax.broadcasted_iota(jnp.int32, (SUB, SLAB), 1) < HEAD_DIM

        def mix_bwd(i, carry):
            rows = pl.ds(pl.multiple_of(i * chunk, chunk), chunk)
            ls = [lse_ref[g, rows, :] for g in range(N_GROUPS)]
            os_ = [o_ref[g, rows, :] for g in range(N_GROUPS)]
            m = jnp.maximum(jnp.maximum(ls[0], ls[1]), ls[2])
            es = [jnp.exp(l - m) for l in ls]
            tot = es[0] + es[1] + es[2]
            ws = [e / tot for e in es]
            o = ws[0] * os_[0] + ws[1] * os_[1] + ws[2] * os_[2]
            z = pa_ref[rows, 9 * SLAB:10 * SLAB].astype(F32)
            sig = _sigmoid(z)
            dav = da_ref[rows, :]
            do = dav * (z * sig)
            dp_ref[rows, 9 * SLAB:10 * SLAB] = (dav * o * (sig * (1.0 + z * (1.0 - sig)))).astype(BF16)
            wsum = _head_sums(do * o)
            for g in range(N_GROUPS):
                dog[g, rows, :] = ws[g] * do
                cvec[g, rows, :] = -(ws[g] * wsum)
            return carry

        lax.fori_loop(0, seq // chunk, mix_bwd, 0)

        for g in range(N_GROUPS):
            dil = DILATIONS[g]
            for w in range(3):
                _to_sub_major(pa_ref, 3 * w + g, sub.at[w], stage, dil, seq)
            dsub[1] = jnp.zeros((seq, SLAB), F32)
            dsub[2] = jnp.zeros((seq, SLAB), F32)
            nk = _key_rows(g, seq)

            def block(it, g=g, dil=dil, nk=nk):
                row0, krow0, bi, nat = _block_rows(it, g, seq)
                q = sub[0, pl.ds(row0, SUB), :]
                kw = sub[1, pl.ds(krow0, nk), :]
                vw = sub[2, pl.ds(krow0, nk), :]
                do = _ld_rows(dog.at[g], nat, SUB, dil).astype(BF16)
                cv = _ld_rows(cvec.at[g], nat, SUB, dil)
                lse = _ld_rows(lse_ref.at[g], nat, SUB, dil)
                zero = jnp.zeros_like(q)
                dqs, dss, probs, qms, doms = [], [], [], [], []
                for hh in range(2):
                    mask = head0 if hh == 0 else ~head0
                    lane = hh * HEAD_DIM
                    qm = jnp.where(mask, q, zero)
                    dom = jnp.where(mask, do, zero)
                    s = _nt(qm * (HEAD_DIM ** -0.5), kw) + bias_ref[g, bi, hh, :, 0:nk]
                    prob = jnp.exp(s - lse[:, lane:lane + 1])
                    ds = (prob * (_nt(dom, vw) + cv[:, lane:lane + 1]) * (HEAD_DIM ** -0.5)).astype(BF16)
                    dqs.append(jnp.dot(ds, kw, preferred_element_type=F32))
                    dss.append(ds)
                    probs.append(prob.astype(BF16))
                    qms.append(qm)
                    doms.append(dom)
                dsub[0, pl.ds(row0, SUB), :] = jnp.where(head0, dqs[0], dqs[1])
                dk = _tn(jnp.concatenate(dss, axis=0), jnp.concatenate(qms, axis=0))
                dv = _tn(jnp.concatenate(probs, axis=0), jnp.concatenate(doms, axis=0))
                dsub[1, pl.ds(krow0, nk), :] += dk
                dsub[2, pl.ds(krow0, nk), :] += dv

            def two_blocks(i, carry, block=block):
                block(2 * i)
                block(2 * i + 1)
                return carry

            lax.fori_loop(0, n_blocks // 2, two_blocks, 0)
            for w in range(3):
                cols = slice((3 * w + g) * SLAB, (3 * w + g + 1) * SLAB)
                if dil == 1:
                    dp_ref[:, cols] = dsub[w].astype(BF16)
                else:
                    n = seq // dil
                    for r in range(dil):
                        stage[pl.ds(r, n, stride=dil), :] = dsub[w, pl.ds(r * n, n), :]
                    dp_ref[:, cols] = stage[...].astype(BF16)

    return pl.pallas_call(
        body, name="attn_bwd", grid=(bsz, 2),
        in_specs=[pl.BlockSpec(memory_space=pl.ANY), pl.BlockSpec((seq, PAIR_COLS), lambda b, p: (b, p)),
                  pl.BlockSpec((N_GROUPS, seq, SLAB), lambda b, p: (0, b, p)),
                  pl.BlockSpec((N_GROUPS, seq, SLAB), lambda b, p: (0, b, p)),
                  pl.BlockSpec((seq, SLAB), lambda b, p: (b, p))],
        out_specs=pl.BlockSpec((seq, PAIR_COLS), lambda b, p: (b, p)),
        out_shape=jax.ShapeDtypeStruct(dproj.shape, BF16), input_output_aliases={0: 0},
        scratch_shapes=[pltpu.VMEM((3, seq, SLAB), BF16), pltpu.VMEM((seq, SLAB), F32), pltpu.VMEM((3, seq, SLAB), F32),
                        pltpu.VMEM((3, seq, SLAB), F32), pltpu.VMEM((3, seq, SLAB), F32),
                        pltpu.VMEM((N_GROUPS, 2, 2, SUB, 2 * SUB), F32)],
        compiler_params=_params("arbitrary", "arbitrary"))(dproj, pa, o_all, lse_all, da)


def _real_k_tile(kk, lay, tk):
    return jnp.where(kk < ATT // tk, kk, kk + (lay.c0 - ATT) // tk)


def _grad_h(dproj, w_all, gx0, x2, mod3, seq, lay):
    t, d = x2.shape
    tm = min(1024, seq)
    tk = 512
    per_seq = seq // tm
    nk = (ATT + 6 * d) // tk

    def body(dp_ref, w_ref, gx0_ref, x_ref, mod_ref, gx_ref, dmod_ref, acc):
        i, kk = pl.program_id(0), pl.program_id(1)

        @pl.when(kk == 0)
        def _():
            acc[...] = jnp.zeros_like(acc)

        acc[...] += lax.dot_general(dp_ref[...], w_ref[...], (((1,), (1,)), ((), ())), preferred_element_type=F32)

        @pl.when(kk == nk - 1)
        def _():
            dh = acc[...]
            gx_ref[...] = gx0_ref[...] + dh * (1.0 + mod_ref[:, d:2 * d])

            @pl.when(i % per_seq == 0)
            def _():
                dmod_ref[...] = jnp.zeros_like(dmod_ref)

            dmod_ref[0:1, :] += jnp.sum(dh, axis=0, keepdims=True)
            dmod_ref[1:2, :] += jnp.sum(dh * x_ref[...], axis=0, keepdims=True)

    return pl.pallas_call(
        body, name="grad_h", grid=(t // tm, nk),
        in_specs=[pl.BlockSpec((tm, tk), lambda i, k: (i, _real_k_tile(k, lay, tk))),
                  pl.BlockSpec((d, tk), lambda i, k: (0, _real_k_tile(k, lay, tk))),
                  pl.BlockSpec((tm, d), lambda i, k: (i, 0)), pl.BlockSpec((tm, d), lambda i, k: (i, 0)),
                  pl.BlockSpec((None, 1, 3 * d), lambda i, k: (i // per_seq, 0, 0))],
        out_specs=[pl.BlockSpec((tm, d), lambda i, k: (i, 0)), pl.BlockSpec((None, 8, d), lambda i, k: (i // per_seq, 0, 0))],
        out_shape=[jax.ShapeDtypeStruct((t, d), F32), jax.ShapeDtypeStruct((t // seq, 8, d), F32)],
        scratch_shapes=[pltpu.VMEM((tm, d), F32)],
        compiler_params=_params("arbitrary", "arbitrary"))(dproj, w_all, gx0, x2, mod3)


def _grad_w_in(ht, dproj, seq, lay):
    d, t = ht.shape
    tm = seq
    n_i = t // tm

    def make_body(aliased):
        def body(*refs):
            if aliased:
                refs = refs[2:]
            ht_ref, dp_ref, gw_ref, gb_ref, acc, bacc = refs
            i = pl.program_id(1)

            @pl.when(i == 0)
            def _():
                acc[...] = jnp.zeros_like(acc)
                bacc[...] = jnp.zeros_like(bacc)

            dp = dp_ref[...]
            acc[...] += jnp.dot(ht_ref[...], dp, preferred_element_type=F32)
            bacc[...] += jnp.sum(dp.astype(F32), axis=0, keepdims=True)

            @pl.when(i == n_i - 1)
            def _():
                gw_ref[...] = acc[...].astype(BF16)
                gb_ref[...] = bacc[...]
        return body

    def call(name, tn, n_tiles, perm_tile, nat_tile, prev):
        in_specs = [pl.BlockSpec((d, tm), lambda j, i: (0, i)), pl.BlockSpec((tm, tn), lambda j, i: (i, perm_tile(j)))]
        args = [ht, dproj]
        aliases = {}
        if prev is not None:
            in_specs = [pl.BlockSpec(memory_space=pl.ANY)] * 2 + in_specs
            args = list(prev) + args
            aliases = {0: 0, 1: 1}
        return pl.pallas_call(
            make_body(prev is not None), name=name, grid=(n_tiles, n_i), in_specs=in_specs,
            out_specs=[pl.BlockSpec((d, tn), lambda j, i: (0, nat_tile(j))), pl.BlockSpec((1, tn), lambda j, i: (0, nat_tile(j)))],
            out_shape=[jax.ShapeDtypeStruct((d, lay.din), BF16), jax.ShapeDtypeStruct((1, lay.din), F32)],
            input_output_aliases=aliases,
            scratch_shapes=[pltpu.VMEM((d, tn), F32), pltpu.VMEM((1, tn), F32)],
            compiler_params=_params("arbitrary", "arbitrary"))(*args)

    first = call("grad_w_in_attn", SLAB, 2 * PAIR_SLABS, lambda j: j, lay.attn_nat_slab, None)
    base = lay.c0 // CONV_TILE
    return call("grad_w_in_rest", CONV_TILE, 6 * d // CONV_TILE, lambda j: base + j, lay.rest_nat_tile, first)


def _pack_rows(parts, width=128):
    flat = [p.reshape(-1) for p in parts]
    spans, rows = [], 0
    padded = []
    for f in flat:
        n = -(-f.shape[0] // (8 * width)) * 8
        padded.append(jnp.pad(f, (0, n * width - f.shape[0])).reshape(n, width))
        spans.append((rows, f.shape[0]))
        rows += n
    return jnp.concatenate(padded, axis=0), spans


def _unpack_rows(packed, spans, shapes, width=128):
    out = []
    for (row, n), shp in zip(spans, shapes):
        rows = -(-n // width)
        out.append(packed[row:row + rows].reshape(-1)[:n].reshape(shp))
    return out


def kernel(x, c, w_ada, b_ada, w_in, b_in, conv_w, w_proj_attn, w_proj_conv, w_out, b_out, ln_g, ln_b, loss_target, m_w_ada, m_b_ada, m_w_in, m_b_in, m_conv_w, m_w_proj_attn, m_w_proj_conv, m_w_out, m_b_out, m_ln_g, m_ln_b, v_w_ada, v_b_ada, v_w_in, v_b_in, v_conv_w, v_w_proj_attn, v_w_proj_conv, v_w_out, v_b_out, v_ln_g, v_ln_b):
    bsz, seq, d = x.shape
    t = bsz * seq
    lay = _Layout(d)
    col_sharded = [True, True, True, False, False]
    big_w = [w_ada[0], w_in[0], w_proj_attn[0], w_proj_conv[0], w_out[0]]
    big_m = [m_w_ada[0], m_w_in[0], m_w_proj_attn[0], m_w_proj_conv[0], m_w_out[0]]
    big_v = [v_w_ada[0], v_w_in[0], v_w_proj_attn[0], v_w_proj_conv[0], v_w_out[0]]
    chip = 2 * lax.axis_index("x") + lax.axis_index("y")

    cw_pad = jnp.pad(conv_w[0], ((0, 5), (0, 0)))
    chip1 = chip.astype(jnp.int32).reshape(1)
    own_in_full = [_cast_into_full(big_w[w], col_sharded[w], chip1, f"cast_shard_{w}") for w in range(5)]
    (wa_f, wi_f, wpa_f, wpc_f, wo_f), cw8 = _gather_weights(own_in_full, col_sharded, cw_pad)
    cw_full = cw8[0:3]
    w_all = _permute_w_in(wi_f, lay)
    b_all = lay.perm_vector(b_in)

    x2 = x.reshape(t, d)
    target2 = loss_target.reshape(t, d)
    mod, c_act = _ada(c, wa_f, b_ada)
    mod3 = mod.reshape(bsz, 1, 3 * d)
    pa, = _project(x2, mod3, w_all, b_all, seq, 0, ATT, 512, BF16, False, "project_attn")
    pr, ht = _project(x2, mod3, w_all, b_all, seq, lay.c0, 6 * d, 512, F32, True, "project_rest")
    o_all, lse_all, a_in = _attn_fwd(pa, bsz, seq)
    b_in_act = _conv_fwd(pr, cw_full, bsz, seq, d)
    (dproj, da_in, db_in, gx0, dgate, small_tail, gw_pa, gw_pc, gw_out) = _tail(
        a_in, b_in_act, pr, x2, target2, mod3, wpa_f, wpc_f, wo_f, b_out, ln_g, ln_b, seq, lay)

    dproj, gcw = _conv_bwd(dproj, db_in, pr, cw_full, bsz, seq, lay)
    dproj = _attn_bwd(dproj, pa, o_all, lse_all, da_in, bsz, seq)
    grad_x2, dmod = _grad_h(dproj, w_all, gx0, x2, mod3, seq, lay)
    gw_in_bf, gb_in = _grad_w_in(ht, dproj, seq, lay)
    d_ada = jnp.concatenate([dmod[:, 0, :], dmod[:, 1, :], dgate[:, 0, :]], axis=1)
    gw_ada_bf = _grad_w_ada(c_act.T, d_ada)

    pieces = [small_tail[3], jnp.sum(d_ada, axis=0), gb_in[0], small_tail[0], small_tail[1], small_tail[2], gcw[0:3]]
    packed, spans = _pack_rows(pieces)
    summed = _all_sum_small(packed, d)
    loss = summed[0, 0]
    _, g_b_ada, g_b_in, g_b_out, g_ln_g, g_ln_b, g_cw_full = _unpack_rows(
        summed, spans, [(d,), (3 * d,), (lay.din,), (d,), (d,), (d,), (3, d)])
    g_cw = lax.dynamic_slice(g_cw_full, (0, chip * (d // N_CHIPS)), (3, d // N_CHIPS))

    views = _shard_views([gw_ada_bf, gw_in_bf, gw_pa, gw_pc, gw_out], col_sharded)
    core = lax.axis_index("c").astype(jnp.int32).reshape(1)
    place = jnp.stack([chip, lax.axis_index("c")]).astype(jnp.int32)
    got1 = _pair_exchange_halves(views)
    parts = [_pair_sum(views[w], got1[w], core, f"grads_pair_sum_{w}") for w in range(5)]
    got2 = _chip_scatter(parts, col_sharded)
    fulls = [_chip_sum(parts[w], got2[w], col_sharded[w], place, f"grads_chip_sum_{w}") for w in range(5)]
    g_big = _pair_join_halves(fulls)

    big_out = [_adamw(big_w[w], g_big[w], big_m[w], big_v[w], f"adamw_{w}") for w in range(5)]
    small_w = [b_ada, b_in, conv_w[0], b_out, ln_g, ln_b]
    small_g = [g_b_ada, g_b_in, g_cw, g_b_out, g_ln_g, g_ln_b]
    small_m = [m_b_ada, m_b_in, m_conv_w[0], m_b_out, m_ln_g, m_ln_b]
    small_v = [v_b_ada, v_b_in, v_conv_w[0], v_b_out, v_ln_g, v_ln_b]
    pw, sp = _pack_rows(small_w)
    pg, _ = _pack_rows(small_g)
    pm, _ = _pack_rows(small_m)
    pv, _ = _pack_rows(small_v)
    sd, sm, sv = _adamw(pw, pg, pm, pv, "adamw_small")
    shapes = [a.shape for a in small_w]
    sd, sm, sv = _unpack_rows(sd, sp, shapes), _unpack_rows(sm, sp, shapes), _unpack_rows(sv, sp, shapes)

    def order(wa, bA, wi, bI, cw, wpa, wpc, wo, bO, lg, lb):
        return (wa[None], bA, wi[None], bI, cw[None], wpa[None], wpc[None], wo[None], bO, lg, lb)

    sg = [g.reshape(s) for g, s in zip(small_g, shapes)]
    grads_out = order(g_big[0], sg[0], g_big[1], sg[1], sg[2], g_big[2], g_big[3], g_big[4], sg[3], sg[4], sg[5])
    outs = []
    for idx, small in enumerate((sd, sm, sv)):
        outs.append(order(big_out[0][idx], small[0], big_out[1][idx], small[1], small[2], big_out[2][idx],
                          big_out[3][idx], big_out[4][idx], small[3], small[4], small[5]))
    return (loss, grad_x2.reshape(bsz, seq, d), *grads_out, *outs[0], *outs[1], *outs[2])
```

```python
import functools

import jax
import jax.numpy as jnp
from jax import lax
from jax.experimental import pallas as pl
from jax.experimental.pallas import tpu as pltpu

F32 = jnp.float32
BF16 = jnp.bfloat16
MESH = pl.DeviceIdType.MESH

HEAD_DIM = 64
N_GROUPS = 3
DILATIONS = (1, 4, 16)
N_HEADS = 12
SUB = 128
Q_WIDTH = 768
Z_WIDTH = 256
ATT = 3 * Q_WIDTH + Z_WIDTH
SLAB = 128
PAIR_SLABS = 10
PAIR_COLS = PAIR_SLABS * SLAB
CONV_TILE = 256
ALIBI_MAX_EXP = 8.0
ALPHA = 2.0 ** 0.25
LN_EPS = 1e-5
ADAM_LR, ADAM_B1, ADAM_B2, ADAM_EPS, ADAM_WD, ADAM_STEP = 0.001, 0.9, 0.999, 1e-08, 0.01, 10
N_CHIPS = 4
N_DEV = 8
VMEM_LIMIT_V7X = 60 * 1024 * 1024
NEG = -1e30


def _params(*sem):
    return pltpu.CompilerParams(dimension_semantics=sem, vmem_limit_bytes=VMEM_LIMIT_V7X)


def _sigmoid(v):
    return 1.0 / (1.0 + jnp.exp(-v))


class _Layout:
    def __init__(self, d):
        self.d = d
        self.din = ATT + 6 * d
        c0 = 3072
        while c0 % (2 * d):
            c0 += 1024
        self.c0, self.g0, self.np = c0, c0 + 4 * d, c0 + 6 * d
        self.n_conv_tiles = d // CONV_TILE

    def attn_nat_slab(self, s):
        p, i = s // PAIR_SLABS, s % PAIR_SLABS
        return jnp.where(i < 9, (i // 3) * 6 + (i % 3) * 2 + p, 18 + p)

    def rest_nat_tile(self, t):
        n4 = 4 * self.n_conv_tiles
        conv = ATT // CONV_TILE + (t % 4) * self.n_conv_tiles + t // 4
        return jnp.where(t < n4, conv, ATT // CONV_TILE + t)

    def perm_vector(self, v):
        parts = []
        for s in range(2 * PAIR_SLABS):
            p, i = divmod(s, PAIR_SLABS)
            ns = (i // 3) * 6 + (i % 3) * 2 + p if i < 9 else 18 + p
            parts.append(v[:, ns * SLAB:(ns + 1) * SLAB])
        parts.append(jnp.zeros((1, self.c0 - ATT), v.dtype))
        for j in range(self.n_conv_tiles):
            for k in range(4):
                a = ATT + k * self.d + j * CONV_TILE
                parts.append(v[:, a:a + CONV_TILE])
        parts.append(v[:, ATT + 4 * self.d:])
        return jnp.concatenate(parts, axis=1)


def _place():
    return lax.axis_index("x"), lax.axis_index("y"), lax.axis_index("c")


def _other_chips(x, y):
    return [(1 - x, y), (x, 1 - y), (1 - x, 1 - y)]


def _shard_of(ref, col_sharded, chip, half=None):
    if col_sharded:
        cs = ref.shape[1] // N_CHIPS
        cols = pl.ds(pl.multiple_of(chip * cs, SLAB), cs)
        if half is None:
            return ref.at[:, cols]
        n = ref.shape[0] // 2
        return ref.at[pl.ds(half * n, n), cols]
    rs = ref.shape[0] // N_CHIPS
    if half is None:
        return ref.at[pl.ds(chip * rs, rs)]
    return ref.at[pl.ds(chip * rs + half * (rs // 2), rs // 2)]


def _cast_into_full(shard, col_sharded, chip, name):
    rows, cols = shard.shape
    tr = _row_tile(rows, cols)
    nb = rows // tr

    def body(chip_ref, s_ref, o_ref):
        del chip_ref
        o_ref[...] = s_ref[...].astype(BF16)

    if col_sharded:
        full, out_spec = (rows, cols * N_CHIPS), pl.BlockSpec((tr, cols), lambda i, ch: (i, ch[0]))
    else:
        full, out_spec = (rows * N_CHIPS, cols), pl.BlockSpec((tr, cols), lambda i, ch: (ch[0] * nb + i, 0))
    return pl.pallas_call(
        body, name=name,
        grid_spec=pltpu.PrefetchScalarGridSpec(num_scalar_prefetch=1, grid=(nb,),
                                               in_specs=[pl.BlockSpec((tr, cols), lambda i, ch: (i, 0))], out_specs=out_spec),
        out_shape=jax.ShapeDtypeStruct(full, BF16), compiler_params=_params("parallel"))(chip, shard)


def _gather_weights(fulls, col_sharded, small):
    n = len(fulls)

    def body(*refs):
        sm_in, outs, sm_out = refs[n], refs[n + 1:2 * n + 1], refs[2 * n + 1]
        send, recv, fsend, frecv, lsem, ssend, srecv = refs[2 * n + 2:]
        x, y, c = _place()
        mine = 2 * x + y
        sibling = (x, y, 1 - c)
        chips = _other_chips(x, y)

        local = [pltpu.make_async_copy(sm_in, _shard_of(sm_out, True, mine), lsem)]
        for cp in local:
            cp.start()
        sends = []
        for k, (cx, cy) in enumerate(chips):
            cp = pltpu.make_async_remote_copy(src_ref=sm_in, dst_ref=_shard_of(sm_out, True, mine), send_sem=ssend.at[k],
                                              recv_sem=srecv.at[k], device_id=(cx, cy, c), device_id_type=MESH)
            cp.start()
            sends.append(cp)
        for k, (cx, cy) in enumerate(chips):
            for w in range(n):
                own_half = _shard_of(outs[w], col_sharded[w], mine, c)
                cp = pltpu.make_async_remote_copy(
                    src_ref=own_half, dst_ref=own_half,
                    send_sem=send.at[w, k], recv_sem=recv.at[w, k], device_id=(cx, cy, c), device_id_type=MESH)
                cp.start()
                sends.append(cp)
        for k, (cx, cy) in enumerate(chips):
            theirs = 2 * cx + cy
            for w in range(n):
                landed = _shard_of(outs[w], col_sharded[w], theirs, c)
                pltpu.make_async_remote_copy(src_ref=landed, dst_ref=landed, send_sem=send.at[w, k], recv_sem=recv.at[w, k],
                                             device_id=(cx, cy, c), device_id_type=MESH).wait_recv()
                cp = pltpu.make_async_remote_copy(src_ref=landed, dst_ref=landed, send_sem=fsend.at[w, k],
                                                  recv_sem=frecv.at[w, k], device_id=sibling, device_id_type=MESH)
                cp.start()
                sends.append(cp)
        for k, (cx, cy) in enumerate(chips):
            theirs = 2 * cx + cy
            for w in range(n):
                passed = _shard_of(outs[w], col_sharded[w], theirs, 1 - c)
                pltpu.make_async_remote_copy(src_ref=passed, dst_ref=passed, send_sem=fsend.at[w, k], recv_sem=frecv.at[w, k],
                                             device_id=sibling, device_id_type=MESH).wait_recv()
        for k, (cx, cy) in enumerate(chips):
            theirs = _shard_of(sm_out, True, 2 * cx + cy)
            pltpu.make_async_remote_copy(src_ref=theirs, dst_ref=theirs, send_sem=ssend.at[k], recv_sem=srecv.at[k],
                                         device_id=(cx, cy, c), device_id_type=MESH).wait_recv()
        for cp in sends:
            cp.wait_send()
        for cp in local:
            cp.wait()

    any_spec = pl.BlockSpec(memory_space=pl.ANY)
    outs = pl.pallas_call(
        body, name="gather_weights",
        out_shape=[jax.ShapeDtypeStruct(f.shape, BF16) for f in fulls]
        + [jax.ShapeDtypeStruct((small.shape[0], small.shape[1] * N_CHIPS), small.dtype)],
        in_specs=[any_spec] * (n + 1), out_specs=[any_spec] * (n + 1), input_output_aliases={w: w for w in range(n)},
        scratch_shapes=[pltpu.SemaphoreType.DMA((n, 3)), pltpu.SemaphoreType.DMA((n, 3)),
                        pltpu.SemaphoreType.DMA((n, 3)), pltpu.SemaphoreType.DMA((n, 3)), pltpu.SemaphoreType.DMA,
                        pltpu.SemaphoreType.DMA((3,)), pltpu.SemaphoreType.DMA((3,))],
    )(*fulls, small)
    return outs[:n], outs[n]


HBM_SPEC = pl.BlockSpec(memory_space=pltpu.HBM)
SEM_SPEC = pl.BlockSpec(memory_space=pltpu.SEMAPHORE)
DATAFLOW = pltpu.SideEffectType.DATAFLOW_SIDE_EFFECTING


def _start_copies(name, arrays, sem_shape, after, copies):
    n = len(arrays)

    def body(*refs):
        for cp in copies(refs[:n], refs[n + 1], refs[n + 2]):
            cp.start()
        token = refs[2 * n + 3]
        token[...] = jnp.zeros_like(token)

    res = pl.pallas_call(
        body, name=name,
        out_shape=(pltpu.SemaphoreType.DMA(sem_shape), pltpu.SemaphoreType.DMA(sem_shape),
                   *[pltpu.HBM(a.shape, a.dtype) for a in arrays], jax.ShapeDtypeStruct((8, 128), F32)),
        in_specs=[HBM_SPEC] * n + [pl.BlockSpec(memory_space=pl.ANY)],
        out_specs=(SEM_SPEC, SEM_SPEC, *([HBM_SPEC] * n), pl.BlockSpec(memory_space=pltpu.VMEM)),
        input_output_aliases={i: 2 + i for i in range(n)},
        compiler_params=pltpu.CompilerParams(has_side_effects=DATAFLOW),
    )(*[pltpu.with_memory_space_constraint(a, pltpu.HBM) for a in arrays], after)
    return res[0], res[1], list(res[2:2 + n]), res[2 + n][0, 0]


def _wait_copies(name, arrays, send, recv, after, copies):
    n = len(arrays)

    def body(*refs):
        for cp in copies(refs[:n], refs[n], refs[n + 1]):
            cp.wait_send()
            cp.wait_recv()

    return pl.pallas_call(
        body, name=name, out_shape=[pltpu.HBM(a.shape, a.dtype) for a in arrays],
        in_specs=[HBM_SPEC] * n + [SEM_SPEC, SEM_SPEC, pl.BlockSpec(memory_space=pl.ANY)], out_specs=[HBM_SPEC] * n,
        input_output_aliases={i: i for i in range(n)},
        compiler_params=pltpu.CompilerParams(has_side_effects=DATAFLOW),
    )(*arrays, send, recv, after)


def _direct_gather_copies(col_sharded):
    def copies(refs, send, recv):
        x, y, c = _place()
        mine = 2 * x + y
        out = []
        for w, ref in enumerate(refs):
            own_half = _shard_of(ref, col_sharded[w], mine, c)
            k = 0
            for cx, cy in _other_chips(x, y):
                for pc in (c, 1 - c):
                    out.append(pltpu.make_async_remote_copy(
                        src_ref=own_half, dst_ref=own_half, send_sem=send.at[6 * w + k], recv_sem=recv.at[6 * w + k],
                        device_id=(cx, cy, pc), device_id_type=MESH))
                    k += 1
        return out
    return copies


def _chip_scatter_copies(n, col_sharded):
    def piece(ref, cs, chip):
        if cs:
            w = ref.shape[2] // N_CHIPS
            return ref.at[:, :, pl.ds(pl.multiple_of(chip * w, SLAB), w)]
        return ref.at[pl.ds(chip, 1)]

    def copies(refs, send, recv):
        x, y, c = _place()
        out = []
        for k, (cx, cy) in enumerate(_other_chips(x, y)):
            for w in range(n):
                out.append(pltpu.make_async_remote_copy(
                    src_ref=piece(refs[w], col_sharded[w], 2 * cx + cy), dst_ref=refs[n + w].at[pl.ds(k, 1)],
                    send_sem=send.at[3 * w + k], recv_sem=recv.at[3 * w + k], device_id=(cx, cy, c), device_id_type=MESH))
        return out
    return copies


def _shard_views(gs, col_sharded):
    return [g.reshape(1, *g.shape) if cs else g.reshape(N_CHIPS, g.shape[0] // N_CHIPS, g.shape[1])
            for g, cs in zip(gs, col_sharded)]


DMA_CHUNK_BYTES = 1 << 20


def _chunk_rows(shape, itemsize):
    s, rows, cols = shape
    n = 1
    while s * (rows // n) * cols * itemsize > DMA_CHUNK_BYTES and (rows // n) % 32 == 0:
        n *= 2
    return rows // n


def _send_rows(src, src_row0, dst, dst_row0, rows, send_sem, recv_sem, device):
    step = _chunk_rows((src.shape[0], rows, src.shape[2]), src.dtype.itemsize)
    for r in range(0, rows, step):
        pltpu.make_async_remote_copy(src_ref=src.at[:, pl.ds(src_row0 + r, step)], dst_ref=dst.at[:, pl.ds(dst_row0 + r, step)],
                                     send_sem=send_sem, recv_sem=recv_sem, device_id=device, device_id_type=MESH).start()


def _pair_exchange_halves(views, name):
    n = len(views)
    half_shapes = [(v.shape[0], v.shape[1] // 2, v.shape[2]) for v in views]

    def body(*refs):
        ins, got = refs[:n], refs[n:2 * n]
        send, recv = refs[2 * n:]
        x, y, c = _place()
        sibling = (x, y, 1 - c)
        for w in range(n):
            hr = half_shapes[w][1]
            _send_rows(ins[w], (1 - c) * hr, got[w], 0, hr, send.at[w], recv.at[w], sibling)
        for w in range(n):
            hr = half_shapes[w][1]
            pltpu.make_async_remote_copy(src_ref=ins[w].at[:, pl.ds((1 - c) * hr, hr)], dst_ref=got[w], send_sem=send.at[w],
                                         recv_sem=recv.at[w], device_id=sibling, device_id_type=MESH).wait()

    any_spec = pl.BlockSpec(memory_space=pl.ANY)
    return pl.pallas_call(
        body, name=name,
        out_shape=[jax.ShapeDtypeStruct(s, v.dtype) for s, v in zip(half_shapes, views)],
        in_specs=[any_spec] * n, out_specs=[any_spec] * n,
        scratch_shapes=[pltpu.SemaphoreType.DMA((n,)), pltpu.SemaphoreType.DMA((n,))],
    )(*views)


def _pair_sum(view, got, core, name):
    s, r, cols = view.shape
    hr = r // 2
    tr = _row_tile(hr, cols)
    nb = hr // tr

    def body(core_ref, a_ref, b_ref, o_ref):
        del core_ref
        o_ref[...] = (a_ref[...].astype(F32) + b_ref[...].astype(F32)).astype(BF16)

    same = pl.BlockSpec((None, tr, cols), lambda j, i, core_ref: (j, i, 0))
    return pl.pallas_call(
        body, name=name,
        grid_spec=pltpu.PrefetchScalarGridSpec(
            num_scalar_prefetch=1, grid=(s, nb),
            in_specs=[pl.BlockSpec((None, tr, cols), lambda j, i, core_ref: (j, core_ref[0] * nb + i, 0)), same],
            out_specs=same),
        out_shape=jax.ShapeDtypeStruct((s, hr, cols), BF16), compiler_params=_params("parallel", "parallel"))(core, view, got)


def _piece_cols(part, col_sharded):
    return part.shape[2] // N_CHIPS if col_sharded else part.shape[2]


def _reduce_over_chips(parts, col_sharded, place, after, between, tag):
    n = len(parts)
    lands = [lax.empty((3, p.shape[1], _piece_cols(p, cs)), p.dtype) for p, cs in zip(parts, col_sharded)]
    copies = _chip_scatter_copies(n, col_sharded)
    send, recv, flying, token = _start_copies(f"grads_scatter_{tag}_start", list(parts) + lands, (3 * n,), after, copies)
    done, marker = between(token)
    landed = _wait_copies(f"grads_scatter_{tag}_wait", flying, send, recv, marker, copies)
    reduced = [_chip_sum(landed[w], landed[n + w], col_sharded[w], place, f"grads_chip_sum_{tag}_{w}") for w in range(n)]
    return reduced, done


def _chip_sum(part, got, col_sharded, place, name):
    _, hr, _ = part.shape
    cols = _piece_cols(part, col_sharded)
    tr = _row_tile(hr, cols)
    nb = hr // tr

    def body(place_ref, own_ref, g0_ref, g1_ref, g2_ref, o_ref):
        del place_ref
        acc = own_ref[...].astype(F32) + g0_ref[...].astype(F32)
        o_ref[...] = acc + g1_ref[...].astype(F32) + g2_ref[...].astype(F32)

    if col_sharded:
        own = pl.BlockSpec((None, tr, cols), lambda i, pr: (0, i, pr[0]))
    else:
        own = pl.BlockSpec((None, tr, cols), lambda i, pr: (pr[0], i, 0))
    others = [pl.BlockSpec((None, tr, cols), lambda i, pr, k=k: (k, i, 0)) for k in range(3)]
    return pl.pallas_call(
        body, name=name,
        grid_spec=pltpu.PrefetchScalarGridSpec(
            num_scalar_prefetch=1, grid=(nb,), in_specs=[own] + others,
            out_specs=pl.BlockSpec((tr, cols), lambda i, pr: (pr[1] * nb + i, 0))),
        out_shape=jax.ShapeDtypeStruct((2 * hr, cols), F32), compiler_params=_params("parallel"))(place, part, got, got, got)


def _pair_join_halves(fulls):
    n = len(fulls)
    views = [f.reshape(1, *f.shape) for f in fulls]

    def body(*refs):
        outs = refs[n:2 * n]
        send, recv = refs[2 * n:]
        x, y, c = _place()
        sibling = (x, y, 1 - c)
        for w in range(n):
            hr = outs[w].shape[1] // 2
            _send_rows(outs[w], c * hr, outs[w], c * hr, hr, send.at[w], recv.at[w], sibling)
        for w in range(n):
            hr = outs[w].shape[1] // 2
            pltpu.make_async_remote_copy(
                src_ref=outs[w].at[:, pl.ds(c * hr, hr)], dst_ref=outs[w].at[:, pl.ds((1 - c) * hr, hr)], send_sem=send.at[w],
                recv_sem=recv.at[w], device_id=sibling, device_id_type=MESH).wait()

    any_spec = pl.BlockSpec(memory_space=pl.ANY)
    outs = pl.pallas_call(
        body, name="grads_pair_join",
        out_shape=[jax.ShapeDtypeStruct(v.shape, v.dtype) for v in views],
        in_specs=[any_spec] * n, out_specs=[any_spec] * n, input_output_aliases={w: w for w in range(n)},
        scratch_shapes=[pltpu.SemaphoreType.DMA((n,)), pltpu.SemaphoreType.DMA((n,))],
    )(*views)
    return [o[0] for o in outs]


def _all_sum_small(vec, n_sum, d):
    rows = vec.shape[0]

    def body(v_ref, sum_ref, kept_ref, all_ref, send, recv):
        x, y, c = _place()
        me = 4 * x + 2 * y + c
        all_ref[me] = v_ref[...]
        peers = _all_devices(x, y, c)
        copies = []
        for k, (peer, _) in enumerate(peers):
            cp = pltpu.make_async_remote_copy(src_ref=v_ref, dst_ref=all_ref.at[me], send_sem=send.at[k], recv_sem=recv.at[k],
                                              device_id=peer, device_id_type=MESH)
            cp.start()
            copies.append(cp)
        for k, (_, src) in enumerate(peers):
            pltpu.make_async_remote_copy(src_ref=v_ref, dst_ref=all_ref.at[src], send_sem=send.at[k], recv_sem=recv.at[k],
                                         device_id=(x, y, c), device_id_type=MESH).wait_recv()
        for cp in copies:
            cp.wait_send()
        total = all_ref[0, 0:n_sum, :]
        for i in range(1, N_DEV):
            total = total + all_ref[i, 0:n_sum, :]
        sum_ref[...] = total
        loss = 0.5 / d * jnp.sum(total[0:8, :])
        sum_ref[0:8, :] = jnp.full((8, 128), loss, F32)
        for i in range(N_DEV):
            kept_ref[i] = all_ref[i, n_sum:rows, :]

    vm = pl.BlockSpec(memory_space=pltpu.VMEM)
    return pl.pallas_call(
        body, name="all_sum_small",
        out_shape=[jax.ShapeDtypeStruct((n_sum, 128), F32), jax.ShapeDtypeStruct((N_DEV, rows - n_sum, 128), F32)],
        in_specs=[vm], out_specs=[vm, vm],
        scratch_shapes=[pltpu.VMEM((N_DEV, rows, 128), F32), pltpu.SemaphoreType.DMA((N_DEV - 1,)),
                        pltpu.SemaphoreType.DMA((N_DEV - 1,))],
        compiler_params=pltpu.CompilerParams(vmem_limit_bytes=VMEM_LIMIT_V7X),
    )(vec)


def _row_tile(rows, cols, itemsize=4, budget=2 << 20):
    t = rows
    while t * cols * itemsize > budget and t % 16 == 0:
        t //= 2
    return t


def _adamw(w, g, m, v, name):
    rows, cols = w.shape
    tr = _row_tile(rows, cols, budget=1 << 20)

    def body(w_ref, g_ref, m_ref, v_ref, d_ref, nm_ref, nv_ref):
        g_ = g_ref[...]
        nm = ADAM_B1 * m_ref[...] + (1.0 - ADAM_B1) * g_
        nv = ADAM_B2 * v_ref[...] + (1.0 - ADAM_B2) * (g_ * g_)
        m_hat = nm / (1.0 - ADAM_B1 ** ADAM_STEP)
        v_hat = nv / (1.0 - ADAM_B2 ** ADAM_STEP)
        d_ref[...] = -ADAM_LR * (m_hat / (jnp.sqrt(v_hat) + ADAM_EPS) + ADAM_WD * w_ref[...])
        nm_ref[...] = nm
        nv_ref[...] = nv

    spec = pl.BlockSpec((tr, cols), lambda i: (i, 0))
    shp = jax.ShapeDtypeStruct((rows, cols), F32)
    return pl.pallas_call(body, name=name, grid=(rows // tr,), in_specs=[spec] * 4, out_specs=[spec] * 3,
                          out_shape=[shp] * 3, compiler_params=_params("parallel"))(w, g, m, v)


def _all_devices(x, y, c):
    out = []
    for k in range(1, N_DEV):
        peer = (x ^ ((k >> 2) & 1), y ^ ((k >> 1) & 1), c ^ (k & 1))
        out.append((peer, 4 * peer[0] + 2 * peer[1] + peer[2]))
    return out


def _ada_exchange(c, w_shard, b_ada):
    bsz, d = c.shape
    cs = w_shard.shape[1]

    def body(c_ref, w_ref, b_ref, mod_ref, act_ref, c_all, part, pieces, csend, crecv, psend, precv):
        x, y, core = _place()
        me = 4 * x + 2 * y + core
        chip = 2 * x + y
        c_all[me] = c_ref[...]
        peers = _all_devices(x, y, core)
        copies = []
        for k, (peer, _) in enumerate(peers):
            cp = pltpu.make_async_remote_copy(src_ref=c_ref, dst_ref=c_all.at[me], send_sem=csend.at[k], recv_sem=crecv.at[k],
                                              device_id=peer, device_id_type=MESH)
            cp.start()
            copies.append(cp)
        for k, (_, src) in enumerate(peers):
            pltpu.make_async_remote_copy(src_ref=c_ref, dst_ref=c_all.at[src], send_sem=csend.at[k], recv_sem=crecv.at[k],
                                         device_id=(x, y, core), device_id_type=MESH).wait_recv()
        rows = jnp.concatenate([c_all[i] for i in range(N_DEV)], axis=0)
        act = rows * _sigmoid(rows)
        act_ref[...] = act
        prod = jnp.dot(act.astype(BF16), w_ref[...].astype(BF16), preferred_element_type=F32)
        for i in range(N_DEV):
            part[i] = prod[i * bsz:(i + 1) * bsz, :]
        pieces[chip] = part[me]
        chips = _other_chips(x, y)
        for k, (cx, cy) in enumerate(chips):
            cp = pltpu.make_async_remote_copy(src_ref=part.at[4 * cx + 2 * cy + core], dst_ref=pieces.at[chip],
                                              send_sem=psend.at[k], recv_sem=precv.at[k], device_id=(cx, cy, core),
                                              device_id_type=MESH)
            cp.start()
            copies.append(cp)
        for k, (cx, cy) in enumerate(chips):
            pltpu.make_async_remote_copy(src_ref=part.at[me], dst_ref=pieces.at[2 * cx + cy], send_sem=psend.at[k],
                                         recv_sem=precv.at[k], device_id=(cx, cy, core), device_id_type=MESH).wait_recv()
        for cp in copies:
            cp.wait_send()
        mod_ref[...] = jnp.concatenate([pieces[j] for j in range(N_CHIPS)], axis=1) + b_ref[...]

    vm = pl.BlockSpec(memory_space=pltpu.VMEM)
    return pl.pallas_call(
        body, name="ada_exchange", in_specs=[vm] * 3, out_specs=[vm] * 2,
        out_shape=[jax.ShapeDtypeStruct((bsz, 3 * d), F32), jax.ShapeDtypeStruct((N_DEV * bsz, d), F32)],
        scratch_shapes=[pltpu.VMEM((N_DEV, bsz, d), F32), pltpu.VMEM((N_DEV, bsz, cs), F32), pltpu.VMEM((N_CHIPS, bsz, cs), F32),
                        pltpu.SemaphoreType.DMA((N_DEV - 1,)), pltpu.SemaphoreType.DMA((N_DEV - 1,)),
                        pltpu.SemaphoreType.DMA((3,)), pltpu.SemaphoreType.DMA((3,))],
        compiler_params=pltpu.CompilerParams(vmem_limit_bytes=VMEM_LIMIT_V7X))(c, w_shard, b_ada)


def _grad_w_ada_cols(act_t, d_cols):
    d, n = act_t.shape
    cs = d_cols.shape[1]

    def body(a_ref, g_ref, o_ref):
        a, g = a_ref[...], g_ref[...]
        acc = a[:, 0:1] * g[0:1, :]
        for b in range(1, n):
            acc = acc + a[:, b:b + 1] * g[b:b + 1, :]
        o_ref[...] = acc

    vm = pl.BlockSpec(memory_space=pltpu.VMEM)
    return pl.pallas_call(body, name="grad_w_ada", in_specs=[vm] * 2, out_specs=vm,
                          out_shape=jax.ShapeDtypeStruct((d, cs), F32),
                          compiler_params=pltpu.CompilerParams(vmem_limit_bytes=VMEM_LIMIT_V7X))(act_t, d_cols)


def _permute_w_in(w_nat, lay):
    d = lay.d

    def copy_body(src_ref, dst_in_ref, dst_ref):
        del dst_in_ref
        dst_ref[...] = src_ref[...]

    def first_body(src_ref, dst_ref):
        dst_ref[...] = src_ref[...]

    w_all = pl.pallas_call(
        first_body, name="permute_w_attn", grid=(2 * PAIR_SLABS,),
        in_specs=[pl.BlockSpec((d, SLAB), lambda s: (0, lay.attn_nat_slab(s)))],
        out_specs=pl.BlockSpec((d, SLAB), lambda s: (0, s)),
        out_shape=jax.ShapeDtypeStruct((d, lay.np), BF16), compiler_params=_params("arbitrary"))(w_nat)
    n_rest = 6 * d // CONV_TILE
    base = lay.c0 // CONV_TILE
    return pl.pallas_call(
        copy_body, name="permute_w_rest", grid=(n_rest,),
        in_specs=[pl.BlockSpec((d, CONV_TILE), lambda t: (0, lay.rest_nat_tile(t))), pl.BlockSpec(memory_space=pl.ANY)],
        out_specs=pl.BlockSpec((d, CONV_TILE), lambda t: (0, base + t)),
        out_shape=jax.ShapeDtypeStruct((d, lay.np), BF16), input_output_aliases={1: 0},
        compiler_params=_params("arbitrary"))(w_nat, w_all)


def _project(x2, mod3, w_all, b_all, seq, col0, ncols, tn, out_dtype, want_ht, name):
    t, d = x2.shape
    tm = min(1024, seq)
    per_seq = seq // tm
    j0 = col0 // tn

    def body(x_ref, mod_ref, w_ref, b_ref, o_ref, *rest):
        h_ref = rest[-1]

        @pl.when(pl.program_id(1) == 0)
        def _():
            h = x_ref[...] * (1.0 + mod_ref[:, d:2 * d]) + mod_ref[:, 0:d]
            h_ref[...] = h.astype(BF16)
            if want_ht:
                rest[0][...] = h.T.astype(BF16)

        o_ref[...] = (jnp.dot(h_ref[...], w_ref[...], preferred_element_type=F32) + b_ref[...]).astype(out_dtype)

    out_shape = [jax.ShapeDtypeStruct((t, ncols), out_dtype)]
    out_specs = [pl.BlockSpec((tm, tn), lambda i, j: (i, j))]
    if want_ht:
        out_shape.append(jax.ShapeDtypeStruct((d, t), BF16))
        out_specs.append(pl.BlockSpec((d, tm), lambda i, j: (0, i)))
    return pl.pallas_call(
        body, name=name, grid=(t // tm, ncols // tn),
        in_specs=[pl.BlockSpec((tm, d), lambda i, j: (i, 0)),
                  pl.BlockSpec((None, 1, 3 * d), lambda i, j: (i // per_seq, 0, 0)),
                  pl.BlockSpec((d, tn), lambda i, j: (0, j0 + j)),
                  pl.BlockSpec((1, tn), lambda i, j: (0, j0 + j))],
        out_specs=out_specs, out_shape=out_shape,
        scratch_shapes=[pltpu.VMEM((tm, d), BF16)],
        compiler_params=_params("arbitrary", "arbitrary"))(x2, mod3, w_all, b_all)


def _slope(g, p, hh):
    head = 4 * g + 2 * p + hh
    return 2.0 ** (-ALIBI_MAX_EXP * (head + 1.0) / N_HEADS)


def _ld_rows(ref, start, n, stride):
    if stride == 1:
        return ref[pl.ds(start, n), :]
    return ref[pl.ds(start, n, stride=stride), :]


def _st_rows(ref, start, n, stride, val):
    if stride == 1:
        ref[pl.ds(start, n), :] = val
    else:
        ref[pl.ds(start, n, stride=stride), :] = val


def _sub_blocks(g, seq):
    return seq // DILATIONS[g] // SUB


def _key_rows(g, seq):
    return SUB if _sub_blocks(g, seq) == 1 else 2 * SUB


def _fill_bias(bias_ref, p, seq):
    for g in range(N_GROUPS):
        nk = _key_rows(g, seq)
        diff = lax.broadcasted_iota(jnp.int32, (SUB, nk), 0) - lax.broadcasted_iota(jnp.int32, (SUB, nk), 1)
        for i, off in enumerate((0, SUB)):
            if i == 1 and nk == SUB:
                continue
            delta = diff + off
            ok = (delta >= 0) & (delta <= SUB)
            dist = (delta * DILATIONS[g]).astype(F32)
            for hh in range(2):
                slope = jnp.where(p == 0, _slope(g, 0, hh), _slope(g, 1, hh))
                bias_ref[g, i, hh, :, 0:nk] = jnp.where(ok, -slope * dist, NEG)


def _to_sub_major(pa_ref, col, sub_ref, stage, dil, seq):
    cols = slice(col * SLAB, (col + 1) * SLAB)
    if dil == 1:
        sub_ref[...] = pa_ref[:, cols]
        return
    n = seq // dil
    stage[...] = pa_ref[:, cols].astype(F32)
    for r in range(dil):
        sub_ref[pl.ds(r * n, n), :] = stage[pl.ds(r, n, stride=dil), :].astype(BF16)


def _block_rows(it, g, seq):
    dil, nb = DILATIONS[g], _sub_blocks(g, seq)
    row0 = pl.multiple_of(it * SUB, SUB)
    if nb == 1:
        return row0, row0, 0, it
    blk = it % nb
    first = blk == 0
    krow0 = pl.multiple_of(row0 - jnp.where(first, 0, SUB), SUB)
    nat = row0 if dil == 1 else it // nb + dil * SUB * blk
    return row0, krow0, jnp.where(first, 0, 1), nat


def _nt(a, b):
    return lax.dot_general(a, b, (((1,), (1,)), ((), ())), preferred_element_type=F32)


def _tn(a, b):
    return lax.dot_general(a, b, (((0,), (0,)), ((), ())), preferred_element_type=F32)


def _head_sums(t):
    rows = t.shape[0]
    lo = jnp.broadcast_to(jnp.sum(t[:, :HEAD_DIM], axis=-1, keepdims=True), (rows, HEAD_DIM))
    hi = jnp.broadcast_to(jnp.sum(t[:, HEAD_DIM:], axis=-1, keepdims=True), (rows, HEAD_DIM))
    return jnp.concatenate([lo, hi], axis=-1)


def _attn_fwd(pa, bsz, seq):
    t = pa.shape[0]
    n_blocks = seq // SUB
    chunk = 256

    def body(pa_ref, o_ref, lse_ref, a_ref, sub, stage, bias_ref):
        p = pl.program_id(1)
        _fill_bias(bias_ref, p, seq)
        head0 = lax.broadcasted_iota(jnp.int32, (SUB, SLAB), 1) < HEAD_DIM
        for g in range(N_GROUPS):
            dil = DILATIONS[g]
            for w in range(3):
                _to_sub_major(pa_ref, 3 * w + g, sub.at[w], stage, dil, seq)
            nk = _key_rows(g, seq)

            def block(it, g=g, dil=dil, nk=nk):
                row0, krow0, bi, nat = _block_rows(it, g, seq)
                q = sub[0, pl.ds(row0, SUB), :]
                kw = sub[1, pl.ds(krow0, nk), :]
                vw = sub[2, pl.ds(krow0, nk), :]
                outs, lses = [], []
                for hh in range(2):
                    qm = jnp.where(head0 if hh == 0 else ~head0, q, jnp.zeros_like(q)) * (HEAD_DIM ** -0.5)
                    s = _nt(qm, kw) + bias_ref[g, bi, hh, :, 0:nk]
                    m = jnp.max(s, axis=-1, keepdims=True)
                    e = jnp.exp(s - m)
                    den = jnp.sum(e, axis=-1, keepdims=True)
                    outs.append(jnp.dot((e * (1.0 / den)).astype(BF16), vw, preferred_element_type=F32))
                    lses.append(m + jnp.log(den))
                _st_rows(o_ref.at[g], nat, SUB, dil, jnp.where(head0, outs[0], outs[1]))
                _st_rows(lse_ref.at[g], nat, SUB, dil, jnp.where(head0, lses[0], lses[1]))

            def two_blocks(i, carry, block=block):
                block(2 * i)
                block(2 * i + 1)
                return carry

            lax.fori_loop(0, n_blocks // 2, two_blocks, 0)

        def mix(i, carry):
            rows = pl.ds(pl.multiple_of(i * chunk, chunk), chunk)
            l0, l1, l2 = lse_ref[0, rows, :], lse_ref[1, rows, :], lse_ref[2, rows, :]
            m = jnp.maximum(jnp.maximum(l0, l1), l2)
            e0, e1, e2 = jnp.exp(l0 - m), jnp.exp(l1 - m), jnp.exp(l2 - m)
            tot = e0 + e1 + e2
            o = (e0 / tot) * o_ref[0, rows, :] + (e1 / tot) * o_ref[1, rows, :] + (e2 / tot) * o_ref[2, rows, :]
            z = pa_ref[rows, 9 * SLAB:10 * SLAB].astype(F32)
            a_ref[rows, :] = (o * (z * _sigmoid(z))).astype(BF16)
            return carry

        lax.fori_loop(0, seq // chunk, mix, 0)

    big = jax.ShapeDtypeStruct((N_GROUPS, t, 2 * SLAB), F32)
    return pl.pallas_call(
        body, name="attn_fwd", grid=(bsz, 2),
        in_specs=[pl.BlockSpec((seq, PAIR_COLS), lambda b, p: (b, p))],
        out_specs=[pl.BlockSpec((N_GROUPS, seq, SLAB), lambda b, p: (0, b, p)),
                   pl.BlockSpec((N_GROUPS, seq, SLAB), lambda b, p: (0, b, p)),
                   pl.BlockSpec((seq, SLAB), lambda b, p: (b, p))],
        out_shape=[big, big, jax.ShapeDtypeStruct((t, 2 * SLAB), BF16)],
        scratch_shapes=[pltpu.VMEM((3, seq, SLAB), BF16), pltpu.VMEM((seq, SLAB), F32),
                        pltpu.VMEM((N_GROUPS, 2, 2, SUB, 2 * SUB), F32)],
        compiler_params=_params("arbitrary", "arbitrary"))(pa)


def _shift_down(v, k):
    rows = lax.broadcasted_iota(jnp.int32, v.shape, 0)
    return jnp.where(rows >= k, pltpu.roll(v, k, 0), 0.0)


def _shift_up(v, k):
    n = v.shape[0]
    rows = lax.broadcasted_iota(jnp.int32, v.shape, 0)
    return jnp.where(rows < n - k, pltpu.roll(v, n - k, 0), 0.0)


def _conv_fwd(pr, conv_w, bsz, seq, d):
    t = pr.shape[0]
    ct = CONV_TILE

    def body(p_ref, cw_ref, o_ref):
        u = p_ref[:, 2 * ct:3 * ct] * p_ref[:, 0:ct]
        cw = cw_ref[...]
        conv = cw[0:1, :] * _shift_down(u, 2)
        conv = conv + cw[1:2, :] * _shift_down(u, 1)
        conv = conv + cw[2:3, :] * u
        z = p_ref[:, 3 * ct:4 * ct]
        o_ref[...] = (p_ref[:, ct:2 * ct] * conv * (z * _sigmoid(z))).astype(BF16)

    return pl.pallas_call(
        body, name="conv_fwd", grid=(bsz, d // ct),
        in_specs=[pl.BlockSpec((seq, 4 * ct), lambda b, j: (b, j)), pl.BlockSpec((3, ct), lambda b, j: (0, j))],
        out_specs=pl.BlockSpec((seq, ct), lambda b, j: (b, j)),
        out_shape=jax.ShapeDtypeStruct((t, d), BF16), compiler_params=_params("parallel", "parallel"))(pr, conv_w)


def _tail(a_in, b_in, pr, x2, target2, mod3, w_pa, w_pc, w_out, b_out, ln_g, ln_b, seq, lay):
    t, d = x2.shape
    tm = 256
    per_seq = seq // tm
    n_steps = t // tm
    gate_blk = 4 * d // d

    def nt(a, b):
        return lax.dot_general(a, b, (((1,), (1,)), ((), ())), preferred_element_type=F32)

    def tn(a, b):
        return lax.dot_general(a, b, (((0,), (0,)), ((), ())), preferred_element_type=F32)

    def body(a_ref, b_ref, ga_ref, gb_ref, x_ref, tg_ref, mod_ref, wpa_ref, wpc_ref, wo_ref, bo_ref, lg_ref, lb_ref,
             dpg_ref, da_ref, db_ref, gx_ref, dgate_ref, small_ref, gwpa_hbm, gwpc_hbm, gwo_hbm,
             acc_pa, acc_pc, acc_o, sem):
        i = pl.program_id(0)

        @pl.when(i == 0)
        def _():
            acc_pa[...] = jnp.zeros_like(acc_pa)
            acc_pc[...] = jnp.zeros_like(acc_pc)
            acc_o[...] = jnp.zeros_like(acc_o)
            small_ref[...] = jnp.zeros_like(small_ref)

        @pl.when(i % per_seq == 0)
        def _():
            dgate_ref[...] = jnp.zeros_like(dgate_ref)

        a_bf, b_bf = a_ref[...], b_ref[...]
        y_attn = jnp.dot(a_bf, wpa_ref[...], preferred_element_type=F32)
        y_conv = jnp.dot(b_bf, wpc_ref[...], preferred_element_type=F32)
        sa, sb = _sigmoid(ga_ref[...]), _sigmoid(gb_ref[...])
        merged = (sa * y_attn + sb * y_conv).astype(BF16)
        mo = jnp.dot(merged, wo_ref[...], preferred_element_type=F32) + bo_ref[...]
        gate = mod_ref[:, 2 * d:3 * d]
        r = ALPHA * x_ref[...] + gate * mo
        mu = jnp.mean(r, axis=-1, keepdims=True)
        cen = r - mu
        var = jnp.mean(cen * cen, axis=-1, keepdims=True)
        rstd = lax.rsqrt(var + LN_EPS)
        xhat = cen * rstd
        err = xhat * lg_ref[...] + lb_ref[...] - tg_ref[...]
        dy = err * (1.0 / d)
        dxhat = dy * lg_ref[...]
        dr = rstd * (dxhat - jnp.mean(dxhat, axis=-1, keepdims=True) - xhat * jnp.mean(dxhat * xhat, axis=-1, keepdims=True))
        gx_ref[...] = ALPHA * dr
        dgate_ref[...] += jnp.sum(dr * mo, axis=0, keepdims=True)
        d_mo = dr * gate
        small_ref[0:1, :] += jnp.sum(d_mo, axis=0, keepdims=True)
        small_ref[1:2, :] += jnp.sum(dy * xhat, axis=0, keepdims=True)
        small_ref[2:3, :] += jnp.sum(dy, axis=0, keepdims=True)
        small_ref[3:4, :] += jnp.sum(err * err, axis=0, keepdims=True)
        d_mo_bf = d_mo.astype(BF16)
        acc_o[...] += tn(merged, d_mo_bf)
        dmerged = nt(d_mo_bf, wo_ref[...])
        dy_attn = (dmerged * sa).astype(BF16)
        dy_conv = (dmerged * sb).astype(BF16)
        dpg_ref[:, 0:d] = (dmerged * y_attn * sa * (1.0 - sa)).astype(BF16)
        dpg_ref[:, d:2 * d] = (dmerged * y_conv * sb * (1.0 - sb)).astype(BF16)
        acc_pa[...] += tn(a_bf, dy_attn)
        acc_pc[...] += tn(b_bf, dy_conv)
        da_ref[...] = nt(dy_attn, wpa_ref[...])
        db_ref[...] = nt(dy_conv, wpc_ref[...])

        @pl.when(i == n_steps - 1)
        def _():
            copies = [pltpu.make_async_copy(acc_pa, gwpa_hbm, sem.at[0]), pltpu.make_async_copy(acc_pc, gwpc_hbm, sem.at[1]),
                      pltpu.make_async_copy(acc_o, gwo_hbm, sem.at[2])]
            for cp in copies:
                cp.start()
            for cp in copies:
                cp.wait()

    row = lambda w: pl.BlockSpec((tm, w), lambda i: (i, 0))
    const = lambda shp: pl.BlockSpec(shp, lambda i: (0,) * len(shp), pipeline_mode=pl.Buffered(1))
    any_spec = pl.BlockSpec(memory_space=pl.ANY)
    return pl.pallas_call(
        body, name="tail", grid=(n_steps,),
        in_specs=[row(Z_WIDTH), row(d),
                  pl.BlockSpec((tm, d), lambda i: (i, gate_blk)), pl.BlockSpec((tm, d), lambda i: (i, gate_blk + 1)),
                  row(d), row(d), pl.BlockSpec((None, 1, 3 * d), lambda i: (i // per_seq, 0, 0)),
                  const((Z_WIDTH, d)), const((d, d)), const((d, d)), const((1, d)), const((1, d)), const((1, d))],
        out_specs=[pl.BlockSpec((tm, 2 * d), lambda i: (i, lay.g0 // (2 * d))), row(Z_WIDTH), row(d), row(d),
                   pl.BlockSpec((None, 1, d), lambda i: (i // per_seq, 0, 0)), pl.BlockSpec((8, d), lambda i: (0, 0)),
                   any_spec, any_spec, any_spec],
        out_shape=[jax.ShapeDtypeStruct((t, lay.np), BF16), jax.ShapeDtypeStruct((t, Z_WIDTH), F32),
                   jax.ShapeDtypeStruct((t, d), F32), jax.ShapeDtypeStruct((t, d), F32),
                   jax.ShapeDtypeStruct((t // seq, 1, d), F32), jax.ShapeDtypeStruct((8, d), F32),
                   jax.ShapeDtypeStruct((Z_WIDTH, d), F32), jax.ShapeDtypeStruct((d, d), F32),
                   jax.ShapeDtypeStruct((d, d), F32)],
        scratch_shapes=[pltpu.VMEM((Z_WIDTH, d), F32), pltpu.VMEM((d, d), F32), pltpu.VMEM((d, d), F32),
                        pltpu.SemaphoreType.DMA((3,))],
        compiler_params=_params("arbitrary"),
    )(a_in, b_in, pr, pr, x2, target2, mod3, w_pa, w_pc, w_out, b_out, ln_g, ln_b)


def _conv_bwd(dproj, db, pr, conv_w, bsz, seq, lay):
    d = lay.d
    ct = CONV_TILE
    base = lay.c0 // (4 * ct)

    def body(dp_in, db_ref, p_ref, cw_ref, dp_ref, gcw_ref):
        del dp_in
        u_x, g_b, g_c, z = p_ref[:, 0:ct], p_ref[:, ct:2 * ct], p_ref[:, 2 * ct:3 * ct], p_ref[:, 3 * ct:4 * ct]
        cw = cw_ref[...]
        u = g_c * u_x
        u1, u2 = _shift_down(u, 1), _shift_down(u, 2)
        conv = cw[0:1, :] * u2 + cw[1:2, :] * u1 + cw[2:3, :] * u
        sig = _sigmoid(z)
        sl = z * sig
        dbv = db_ref[...]
        gbc = g_b * conv
        dp_ref[:, ct:2 * ct] = (dbv * sl * conv).astype(BF16)
        dp_ref[:, 3 * ct:4 * ct] = (dbv * gbc * (sig * (1.0 + z * (1.0 - sig)))).astype(BF16)
        dconv = dbv * sl * g_b

        @pl.when(pl.program_id(1) == 0)
        def _():
            gcw_ref[...] = jnp.zeros_like(gcw_ref)

        gcw_ref[0:1, :] += jnp.sum(dconv * u2, axis=0, keepdims=True)
        gcw_ref[1:2, :] += jnp.sum(dconv * u1, axis=0, keepdims=True)
        gcw_ref[2:3, :] += jnp.sum(dconv * u, axis=0, keepdims=True)
        du = cw[2:3, :] * dconv + cw[1:2, :] * _shift_up(dconv, 1) + cw[0:1, :] * _shift_up(dconv, 2)
        dp_ref[:, 0:ct] = (du * g_c).astype(BF16)
        dp_ref[:, 2 * ct:3 * ct] = (du * u_x).astype(BF16)

    return pl.pallas_call(
        body, name="conv_bwd", grid=(d // ct, bsz),
        in_specs=[pl.BlockSpec(memory_space=pl.ANY), pl.BlockSpec((seq, ct), lambda j, b: (b, j)),
                  pl.BlockSpec((seq, 4 * ct), lambda j, b: (b, j)), pl.BlockSpec((3, ct), lambda j, b: (0, j))],
        out_specs=[pl.BlockSpec((seq, 4 * ct), lambda j, b: (b, base + j)), pl.BlockSpec((8, ct), lambda j, b: (0, j))],
        out_shape=[jax.ShapeDtypeStruct(dproj.shape, BF16), jax.ShapeDtypeStruct((8, d), F32)],
        input_output_aliases={0: 0}, compiler_params=_params("arbitrary", "arbitrary"))(dproj, db, pr, conv_w)


def _attn_bwd(dproj, pa, o_all, lse_all, da, bsz, seq):
    n_blocks = seq // SUB
    chunk = 256

    def body(dp_in, pa_ref, o_ref, lse_ref, da_ref, dp_ref, sub, stage, dsub, dog, cvec, bias_ref):
        del dp_in
        p = pl.program_id(1)
        _fill_bias(bias_ref, p, seq)
        head0 = lax.broadcasted_iota(jnp.int32, (SUB, SLAB), 1) < HEAD_DIM

        def mix_bwd(i, carry):
            rows = pl.ds(pl.multiple_of(i * chunk, chunk), chunk)
            ls = [lse_ref[g, rows, :] for g in range(N_GROUPS)]
            os_ = [o_ref[g, rows, :] for g in range(N_GROUPS)]
            m = jnp.maximum(jnp.maximum(ls[0], ls[1]), ls[2])
            es = [jnp.exp(l - m) for l in ls]
            tot = es[0] + es[1] + es[2]
            ws = [e / tot for e in es]
            o = ws[0] * os_[0] + ws[1] * os_[1] + ws[2] * os_[2]
            z = pa_ref[rows, 9 * SLAB:10 * SLAB].astype(F32)
            sig = _sigmoid(z)
            dav = da_ref[rows, :]
            do = dav * (z * sig)
            dp_ref[rows, 9 * SLAB:10 * SLAB] = (dav * o * (sig * (1.0 + z * (1.0 - sig)))).astype(BF16)
            wsum = _head_sums(do * o)
            for g in range(N_GROUPS):
                dog[g, rows, :] = ws[g] * do
                cvec[g, rows, :] = -(ws[g] * wsum)
            return carry

        lax.fori_loop(0, seq // chunk, mix_bwd, 0)

        for g in range(N_GROUPS):
            dil = DILATIONS[g]
            for w in range(3):
                _to_sub_major(pa_ref, 3 * w + g, sub.at[w], stage, dil, seq)
            dsub[1] = jnp.zeros((seq, SLAB), F32)
            dsub[2] = jnp.zeros((seq, SLAB), F32)
            nk = _key_rows(g, seq)

            def block(it, g=g, dil=dil, nk=nk):
                row0, krow0, bi, nat = _block_rows(it, g, seq)
                q = sub[0, pl.ds(row0, SUB), :]
                kw = sub[1, pl.ds(krow0, nk), :]
                vw = sub[2, pl.ds(krow0, nk), :]
                do = _ld_rows(dog.at[g], nat, SUB, dil).astype(BF16)
                cv = _ld_rows(cvec.at[g], nat, SUB, dil)
                lse = _ld_rows(lse_ref.at[g], nat, SUB, dil)
                zero = jnp.zeros_like(q)
                dqs, dss, probs, qms, doms = [], [], [], [], []
                for hh in range(2):
                    mask = head0 if hh == 0 else ~head0
                    lane = hh * HEAD_DIM
                    qm = jnp.where(mask, q, zero)
                    dom = jnp.where(mask, do, zero)
                    s = _nt(qm * (HEAD_DIM ** -0.5), kw) + bias_ref[g, bi, hh, :, 0:nk]
                    prob = jnp.exp(s - lse[:, lane:lane + 1])
                    ds = (prob * (_nt(dom, vw) + cv[:, lane:lane + 1]) * (HEAD_DIM ** -0.5)).astype(BF16)
                    dqs.append(jnp.dot(ds, kw, preferred_element_type=F32))
                    dss.append(ds)
                    probs.append(prob.astype(BF16))
                    qms.append(qm)
                    doms.append(dom)
                dsub[0, pl.ds(row0, SUB), :] = jnp.where(head0, dqs[0], dqs[1])
                dk = _tn(jnp.concatenate(dss, axis=0), jnp.concatenate(qms, axis=0))
                dv = _tn(jnp.concatenate(probs, axis=0), jnp.concatenate(doms, axis=0))
                dsub[1, pl.ds(krow0, nk), :] += dk
                dsub[2, pl.ds(krow0, nk), :] += dv

            def two_blocks(i, carry, block=block):
                block(2 * i)
                block(2 * i + 1)
                return carry

            lax.fori_loop(0, n_blocks // 2, two_blocks, 0)
            for w in range(3):
                cols = slice((3 * w + g) * SLAB, (3 * w + g + 1) * SLAB)
                if dil == 1:
                    dp_ref[:, cols] = dsub[w].astype(BF16)
                else:
                    n = seq // dil
                    for r in range(dil):
                        stage[pl.ds(r, n, stride=dil), :] = dsub[w, pl.ds(r * n, n), :]
                    dp_ref[:, cols] = stage[...].astype(BF16)

    return pl.pallas_call(
        body, name="attn_bwd", grid=(bsz, 2),
        in_specs=[pl.BlockSpec(memory_space=pl.ANY), pl.BlockSpec((seq, PAIR_COLS), lambda b, p: (b, p)),
                  pl.BlockSpec((N_GROUPS, seq, SLAB), lambda b, p: (0, b, p)),
                  pl.BlockSpec((N_GROUPS, seq, SLAB), lambda b, p: (0, b, p)),
                  pl.BlockSpec((seq, SLAB), lambda b, p: (b, p))],
        out_specs=pl.BlockSpec((seq, PAIR_COLS), lambda b, p: (b, p)),
        out_shape=jax.ShapeDtypeStruct(dproj.shape, BF16), input_output_aliases={0: 0},
        scratch_shapes=[pltpu.VMEM((3, seq, SLAB), BF16), pltpu.VMEM((seq, SLAB), F32), pltpu.VMEM((3, seq, SLAB), F32),
                        pltpu.VMEM((3, seq, SLAB), F32), pltpu.VMEM((3, seq, SLAB), F32),
                        pltpu.VMEM((N_GROUPS, 2, 2, SUB, 2 * SUB), F32)],
        compiler_params=_params("arbitrary", "arbitrary"))(dproj, pa, o_all, lse_all, da)


def _real_k_tile(kk, lay, tk):
    return jnp.where(kk < ATT // tk, kk, kk + (lay.c0 - ATT) // tk)


def _grad_h(dproj, w_all, gx0, x2, mod3, seq, lay):
    t, d = x2.shape
    tm = min(1024, seq)
    tk = 512
    per_seq = seq // tm
    nk = (ATT + 6 * d) // tk

    def body(dp_ref, w_ref, gx0_ref, x_ref, mod_ref, gx_ref, dmod_ref, acc):
        i, kk = pl.program_id(0), pl.program_id(1)

        @pl.when(kk == 0)
        def _():
            acc[...] = jnp.zeros_like(acc)

        acc[...] += lax.dot_general(dp_ref[...], w_ref[...], (((1,), (1,)), ((), ())), preferred_element_type=F32)

        @pl.when(kk == nk - 1)
        def _():
            dh = acc[...]
            gx_ref[...] = gx0_ref[...] + dh * (1.0 + mod_ref[:, d:2 * d])

            @pl.when(i % per_seq == 0)
            def _():
                dmod_ref[...] = jnp.zeros_like(dmod_ref)

            dmod_ref[0:1, :] += jnp.sum(dh, axis=0, keepdims=True)
            dmod_ref[1:2, :] += jnp.sum(dh * x_ref[...], axis=0, keepdims=True)

    return pl.pallas_call(
        body, name="grad_h", grid=(t // tm, nk),
        in_specs=[pl.BlockSpec((tm, tk), lambda i, k: (i, _real_k_tile(k, lay, tk))),
                  pl.BlockSpec((d, tk), lambda i, k: (0, _real_k_tile(k, lay, tk))),
                  pl.BlockSpec((tm, d), lambda i, k: (i, 0)), pl.BlockSpec((tm, d), lambda i, k: (i, 0)),
                  pl.BlockSpec((None, 1, 3 * d), lambda i, k: (i // per_seq, 0, 0))],
        out_specs=[pl.BlockSpec((tm, d), lambda i, k: (i, 0)), pl.BlockSpec((None, 8, d), lambda i, k: (i // per_seq, 0, 0))],
        out_shape=[jax.ShapeDtypeStruct((t, d), F32), jax.ShapeDtypeStruct((t // seq, 8, d), F32)],
        scratch_shapes=[pltpu.VMEM((tm, d), F32)],
        compiler_params=_params("arbitrary", "arbitrary"))(dproj, w_all, gx0, x2, mod3)


def _grad_w_in(ht, dproj, seq, lay):
    d, t = ht.shape
    tm = seq
    n_i = t // tm

    def make_body(aliased):
        def body(*refs):
            if aliased:
                refs = refs[2:]
            ht_ref, dp_ref, gw_ref, gb_ref, acc, bacc = refs
            i = pl.program_id(1)

            @pl.when(i == 0)
            def _():
                acc[...] = jnp.zeros_like(acc)
                bacc[...] = jnp.zeros_like(bacc)

            dp = dp_ref[...]
            acc[...] += jnp.dot(ht_ref[...], dp, preferred_element_type=F32)
            bacc[...] += jnp.sum(dp.astype(F32), axis=0, keepdims=True)

            @pl.when(i == n_i - 1)
            def _():
                gw_ref[...] = acc[...].astype(BF16)
                gb_ref[...] = bacc[...]
        return body

    def call(name, tn, n_tiles, perm_tile, nat_tile, prev):
        in_specs = [pl.BlockSpec((d, tm), lambda j, i: (0, i)), pl.BlockSpec((tm, tn), lambda j, i: (i, perm_tile(j)))]
        args = [ht, dproj]
        aliases = {}
        if prev is not None:
            in_specs = [pl.BlockSpec(memory_space=pl.ANY)] * 2 + in_specs
            args = list(prev) + args
            aliases = {0: 0, 1: 1}
        return pl.pallas_call(
            make_body(prev is not None), name=name, grid=(n_tiles, n_i), in_specs=in_specs,
            out_specs=[pl.BlockSpec((d, tn), lambda j, i: (0, nat_tile(j))), pl.BlockSpec((1, tn), lambda j, i: (0, nat_tile(j)))],
            out_shape=[jax.ShapeDtypeStruct((d, lay.din), BF16), jax.ShapeDtypeStruct((1, lay.din), F32)],
            input_output_aliases=aliases,
            scratch_shapes=[pltpu.VMEM((d, tn), F32), pltpu.VMEM((1, tn), F32)],
            compiler_params=_params("arbitrary", "arbitrary"))(*args)

    first = call("grad_w_in_attn", SLAB, 2 * PAIR_SLABS, lambda j: j, lay.attn_nat_slab, None)
    base = lay.c0 // CONV_TILE
    return call("grad_w_in_rest", CONV_TILE, 6 * d // CONV_TILE, lambda j: base + j, lay.rest_nat_tile, first)


def _pack_rows(parts, width=128):
    flat = [p.reshape(-1) for p in parts]
    spans, rows = [], 0
    padded = []
    for f in flat:
        n = -(-f.shape[0] // (8 * width)) * 8
        padded.append(jnp.pad(f, (0, n * width - f.shape[0])).reshape(n, width))
        spans.append((rows, f.shape[0]))
        rows += n
    return jnp.concatenate(padded, axis=0), spans


def _unpack_rows(packed, spans, shapes, width=128):
    out = []
    for (row, n), shp in zip(spans, shapes):
        rows = -(-n // width)
        out.append(packed[row:row + rows].reshape(-1)[:n].reshape(shp))
    return out


def kernel(x, c, w_ada, b_ada, w_in, b_in, conv_w, w_proj_attn, w_proj_conv, w_out, b_out, ln_g, ln_b, loss_target, m_w_ada, m_b_ada, m_w_in, m_b_in, m_conv_w, m_w_proj_attn, m_w_proj_conv, m_w_out, m_b_out, m_ln_g, m_ln_b, v_w_ada, v_b_ada, v_w_in, v_b_in, v_conv_w, v_w_proj_attn, v_w_proj_conv, v_w_out, v_b_out, v_ln_g, v_ln_b):
    bsz, seq, d = x.shape
    t = bsz * seq
    lay = _Layout(d)
    col_sharded = [True, True, False, False]
    red_w = [w_in[0], w_proj_attn[0], w_proj_conv[0], w_out[0]]
    chip = 2 * lax.axis_index("x") + lax.axis_index("y")
    chip1 = chip.astype(jnp.int32).reshape(1)
    core1 = lax.axis_index("c").astype(jnp.int32).reshape(1)
    place = jnp.stack([chip, lax.axis_index("c")]).astype(jnp.int32)
    x2 = x.reshape(t, d)
    target2 = loss_target.reshape(t, d)

    mod, act_all = _ada_exchange(c, w_ada[0], b_ada)
    mod3 = mod.reshape(bsz, 1, 3 * d)

    cw_pad = jnp.pad(conv_w[0], ((0, 5), (0, 0))) + 0.0 * mod[0, 0]
    own_in_full = [_cast_into_full(red_w[w], col_sharded[w], chip1, f"cast_shard_{w}") for w in range(4)]
    (wi_f,), cw8 = _gather_weights(own_in_full[:1], col_sharded[:1], cw_pad)
    cw_full = cw8[0:3]
    late_copies = _direct_gather_copies(col_sharded[1:])
    late_send, late_recv, late_flying, late_token = _start_copies("gather_late_start", own_in_full[1:], (18,), cw8, late_copies)
    w_all = _permute_w_in(wi_f, lay)
    b_all = lay.perm_vector(b_in) + late_token

    pa, = _project(x2, mod3, w_all, b_all, seq, 0, ATT, 512, BF16, False, "project_attn")
    pr, ht = _project(x2, mod3, w_all, b_all, seq, lay.c0, 6 * d, 512, F32, True, "project_rest")
    o_all, lse_all, a_in = _attn_fwd(pa, bsz, seq)
    b_in_act = _conv_fwd(pr, cw_full, bsz, seq, d)
    wpa_f, wpc_f, wo_f = _wait_copies("gather_late_wait", late_flying, late_send, late_recv, b_in_act, late_copies)
    (dproj, da_in, db_in, gx0, dgate, small_tail, gw_pa, gw_pc, gw_out) = _tail(
        a_in, b_in_act, pr, x2, target2, mod3, wpa_f, wpc_f, wo_f, b_out, ln_g, ln_b, seq, lay)

    def mixers_backward(token):
        dp, gcw_ = _conv_bwd(dproj, db_in, pr, cw_full + token, bsz, seq, lay)
        dp = _attn_bwd(dp, pa, o_all, lse_all, da_in, bsz, seq)
        gw_in_bf_, gb_in_ = _grad_w_in(ht, dp, seq, lay)
        return (dp, gcw_, gw_in_bf_, gb_in_), gw_in_bf_

    late_views = _shard_views([gw_pa, gw_pc, gw_out], col_sharded[1:])
    late_got = _pair_exchange_halves(late_views, "grads_pair_exchange_late")
    late_parts = [_pair_sum(late_views[w], late_got[w], core1, f"grads_pair_sum_late_{w}") for w in range(3)]
    late_red, (dproj, gcw, gw_in_bf, gb_in) = _reduce_over_chips(
        late_parts, col_sharded[1:], place, late_parts[0], mixers_backward, "late")

    def input_backward(token):
        gx, dm = _grad_h(dproj, w_all, gx0, x2, mod3 + token, seq, lay)
        return (gx, dm), gx

    in_view = _shard_views([gw_in_bf], col_sharded[:1])
    in_got = _pair_exchange_halves(in_view, "grads_pair_exchange_in")
    in_part = _pair_sum(in_view[0], in_got[0], core1, "grads_pair_sum_in")
    in_red, (grad_x2, dmod) = _reduce_over_chips([in_part], col_sharded[:1], place, late_red[0], input_backward, "in")
    g_red = _pair_join_halves(in_red + late_red)

    d_ada = jnp.concatenate([dmod[:, 0, :], dmod[:, 1, :], dgate[:, 0, :]], axis=1)
    pieces = [small_tail[3], jnp.sum(d_ada, axis=0), gb_in[0], small_tail[0], small_tail[1], small_tail[2], gcw[0:3]]
    packed, spans = _pack_rows(pieces)
    kept_in, _ = _pack_rows([d_ada])
    summed, kept = _all_sum_small(jnp.concatenate([packed, kept_in], axis=0), packed.shape[0], d)
    loss = summed[0, 0]
    _, g_b_ada, g_b_in, g_b_out, g_ln_g, g_ln_b, g_cw_full = _unpack_rows(
        summed, spans, [(d,), (3 * d,), (lay.din,), (d,), (d,), (d,), (3, d)])
    g_cw = lax.dynamic_slice(g_cw_full, (0, chip * (d // N_CHIPS)), (3, d // N_CHIPS))
    d_ada_all = kept.reshape(N_DEV, -1)[:, :bsz * 3 * d].reshape(N_DEV * bsz, 3 * d)
    ada_cols = 3 * d // N_CHIPS
    g_w_ada = _grad_w_ada_cols(act_all.T, lax.dynamic_slice(d_ada_all, (0, chip * ada_cols), (N_DEV * bsz, ada_cols)))

    big_w = [w_ada[0]] + red_w
    big_m = [m_w_ada[0], m_w_in[0], m_w_proj_attn[0], m_w_proj_conv[0], m_w_out[0]]
    big_v = [v_w_ada[0], v_w_in[0], v_w_proj_attn[0], v_w_proj_conv[0], v_w_out[0]]
    g_big = [g_w_ada] + g_red
    big_out = [_adamw(big_w[w], g_big[w], big_m[w], big_v[w], f"adamw_{w}") for w in range(5)]
    small_w = [b_ada, b_in, conv_w[0], b_out, ln_g, ln_b]
    small_g = [g_b_ada, g_b_in, g_cw, g_b_out, g_ln_g, g_ln_b]
    small_m = [m_b_ada, m_b_in, m_conv_w[0], m_b_out, m_ln_g, m_ln_b]
    small_v = [v_b_ada, v_b_in, v_conv_w[0], v_b_out, v_ln_g, v_ln_b]
    pw, sp = _pack_rows(small_w)
    pg, _ = _pack_rows(small_g)
    pm, _ = _pack_rows(small_m)
    pv, _ = _pack_rows(small_v)
    sd, sm, sv = _adamw(pw, pg, pm, pv, "adamw_small")
    shapes = [a.shape for a in small_w]
    sd, sm, sv = _unpack_rows(sd, sp, shapes), _unpack_rows(sm, sp, shapes), _unpack_rows(sv, sp, shapes)

    def order(wa, bA, wi, bI, cw, wpa, wpc, wo, bO, lg, lb):
        return (wa[None], bA, wi[None], bI, cw[None], wpa[None], wpc[None], wo[None], bO, lg, lb)

    sg = [g.reshape(s) for g, s in zip(small_g, shapes)]
    grads_out = order(g_big[0], sg[0], g_big[1], sg[1], sg[2], g_big[2], g_big[3], g_big[4], sg[3], sg[4], sg[5])
    outs = []
    for idx, small in enumerate((sd, sm, sv)):
        outs.append(order(big_out[0][idx], small[0], big_out[1][idx], small[1], small[2], big_out[2][idx],
                          big_out[3][idx], big_out[4][idx], small[3], small[4], small[5]))
    return (loss, grad_x2.reshape(bsz, seq, d), *grads_out, *outs[0], *outs[1], *outs[2])
```

```python
import functools

import jax
import jax.numpy as jnp
from jax import lax
from jax.experimental import pallas as pl
from jax.experimental.pallas import tpu as pltpu

F32 = jnp.float32
BF16 = jnp.bfloat16
MESH = pl.DeviceIdType.MESH

HEAD_DIM = 64
N_GROUPS = 3
DILATIONS = (1, 4, 16)
N_HEADS = 12
SUB = 128
Q_WIDTH = 768
Z_WIDTH = 256
ATT = 3 * Q_WIDTH + Z_WIDTH
SLAB = 128
PAIR_SLABS = 10
PAIR_COLS = PAIR_SLABS * SLAB
CONV_TILE = 256
ALIBI_MAX_EXP = 8.0
ALPHA = 2.0 ** 0.25
LN_EPS = 1e-5
ADAM_LR, ADAM_B1, ADAM_B2, ADAM_EPS, ADAM_WD, ADAM_STEP = 0.001, 0.9, 0.999, 1e-08, 0.01, 10
N_CHIPS = 4
N_DEV = 8
VMEM_LIMIT_V7X = 60 * 1024 * 1024
NEG = -1e30


def _params(*sem):
    return pltpu.CompilerParams(dimension_semantics=sem, vmem_limit_bytes=VMEM_LIMIT_V7X)


def _sigmoid(v):
    return 1.0 / (1.0 + jnp.exp(-v))


class _Layout:
    def __init__(self, d):
        self.d = d
        self.din = ATT + 6 * d
        c0 = 3072
        while c0 % (2 * d):
            c0 += 1024
        self.c0, self.g0, self.np = c0, c0 + 4 * d, c0 + 6 * d
        self.n_conv_tiles = d // CONV_TILE

    def attn_nat_slab(self, s):
        p, i = s // PAIR_SLABS, s % PAIR_SLABS
        return jnp.where(i < 9, (i // 3) * 6 + (i % 3) * 2 + p, 18 + p)

    def rest_nat_tile(self, t):
        n4 = 4 * self.n_conv_tiles
        conv = ATT // CONV_TILE + (t % 4) * self.n_conv_tiles + t // 4
        return jnp.where(t < n4, conv, ATT // CONV_TILE + t)

    def perm_vector(self, v):
        parts = []
        for s in range(2 * PAIR_SLABS):
            p, i = divmod(s, PAIR_SLABS)
            ns = (i // 3) * 6 + (i % 3) * 2 + p if i < 9 else 18 + p
            parts.append(v[:, ns * SLAB:(ns + 1) * SLAB])
        parts.append(jnp.zeros((1, self.c0 - ATT), v.dtype))
        for j in range(self.n_conv_tiles):
            for k in range(4):
                a = ATT + k * self.d + j * CONV_TILE
                parts.append(v[:, a:a + CONV_TILE])
        parts.append(v[:, ATT + 4 * self.d:])
        return jnp.concatenate(parts, axis=1)


def _place():
    return lax.axis_index("x"), lax.axis_index("y"), lax.axis_index("c")


def _other_chips(x, y):
    return [(1 - x, y), (x, 1 - y), (1 - x, 1 - y)]


def _shard_of(ref, col_sharded, chip, half=None):
    if col_sharded:
        cs = ref.shape[1] // N_CHIPS
        cols = pl.ds(pl.multiple_of(chip * cs, SLAB), cs)
        if half is None:
            return ref.at[:, cols]
        n = ref.shape[0] // 2
        return ref.at[pl.ds(half * n, n), cols]
    rs = ref.shape[0] // N_CHIPS
    if half is None:
        return ref.at[pl.ds(chip * rs, rs)]
    return ref.at[pl.ds(chip * rs + half * (rs // 2), rs // 2)]


def _cast_into_full(shard, col_sharded, chip, name):
    rows, cols = shard.shape
    tr = _row_tile(rows, cols)
    nb = rows // tr

    def body(chip_ref, s_ref, o_ref):
        del chip_ref
        o_ref[...] = s_ref[...].astype(BF16)

    if col_sharded:
        full, out_spec = (rows, cols * N_CHIPS), pl.BlockSpec((tr, cols), lambda i, ch: (i, ch[0]))
    else:
        full, out_spec = (rows * N_CHIPS, cols), pl.BlockSpec((tr, cols), lambda i, ch: (ch[0] * nb + i, 0))
    return pl.pallas_call(
        body, name=name,
        grid_spec=pltpu.PrefetchScalarGridSpec(num_scalar_prefetch=1, grid=(nb,),
                                               in_specs=[pl.BlockSpec((tr, cols), lambda i, ch: (i, 0))], out_specs=out_spec),
        out_shape=jax.ShapeDtypeStruct(full, BF16), compiler_params=_params("parallel"))(chip, shard)


def _gather_weights(fulls, col_sharded, small):
    n = len(fulls)

    def body(*refs):
        sm_in, outs, sm_out = refs[n], refs[n + 1:2 * n + 1], refs[2 * n + 1]
        send, recv, fsend, frecv, lsem, ssend, srecv = refs[2 * n + 2:]
        x, y, c = _place()
        mine = 2 * x + y
        sibling = (x, y, 1 - c)
        chips = _other_chips(x, y)

        local = [pltpu.make_async_copy(sm_in, _shard_of(sm_out, True, mine), lsem)]
        for cp in local:
            cp.start()
        sends = []
        for k, (cx, cy) in enumerate(chips):
            cp = pltpu.make_async_remote_copy(src_ref=sm_in, dst_ref=_shard_of(sm_out, True, mine), send_sem=ssend.at[k],
                                              recv_sem=srecv.at[k], device_id=(cx, cy, c), device_id_type=MESH)
            cp.start()
            sends.append(cp)
        for k, (cx, cy) in enumerate(chips):
            for w in range(n):
                own_half = _shard_of(outs[w], col_sharded[w], mine, c)
                cp = pltpu.make_async_remote_copy(
                    src_ref=own_half, dst_ref=own_half,
                    send_sem=send.at[w, k], recv_sem=recv.at[w, k], device_id=(cx, cy, c), device_id_type=MESH)
                cp.start()
                sends.append(cp)
        for k, (cx, cy) in enumerate(chips):
            theirs = 2 * cx + cy
            for w in range(n):
                landed = _shard_of(outs[w], col_sharded[w], theirs, c)
                pltpu.make_async_remote_copy(src_ref=landed, dst_ref=landed, send_sem=send.at[w, k], recv_sem=recv.at[w, k],
                                             device_id=(cx, cy, c), device_id_type=MESH).wait_recv()
                cp = pltpu.make_async_remote_copy(src_ref=landed, dst_ref=landed, send_sem=fsend.at[w, k],
                                                  recv_sem=frecv.at[w, k], device_id=sibling, device_id_type=MESH)
                cp.start()
                sends.append(cp)
        for k, (cx, cy) in enumerate(chips):
            theirs = 2 * cx + cy
            for w in range(n):
                passed = _shard_of(outs[w], col_sharded[w], theirs, 1 - c)
                pltpu.make_async_remote_copy(src_ref=passed, dst_ref=passed, send_sem=fsend.at[w, k], recv_sem=frecv.at[w, k],
                                             device_id=sibling, device_id_type=MESH).wait_recv()
        for k, (cx, cy) in enumerate(chips):
            theirs = _shard_of(sm_out, True, 2 * cx + cy)
            pltpu.make_async_remote_copy(src_ref=theirs, dst_ref=theirs, send_sem=ssend.at[k], recv_sem=srecv.at[k],
                                         device_id=(cx, cy, c), device_id_type=MESH).wait_recv()
        for cp in sends:
            cp.wait_send()
        for cp in local:
            cp.wait()

    any_spec = pl.BlockSpec(memory_space=pl.ANY)
    outs = pl.pallas_call(
        body, name="gather_weights",
        out_shape=[jax.ShapeDtypeStruct(f.shape, BF16) for f in fulls]
        + [jax.ShapeDtypeStruct((small.shape[0], small.shape[1] * N_CHIPS), small.dtype)],
        in_specs=[any_spec] * (n + 1), out_specs=[any_spec] * (n + 1), input_output_aliases={w: w for w in range(n)},
        scratch_shapes=[pltpu.SemaphoreType.DMA((n, 3)), pltpu.SemaphoreType.DMA((n, 3)),
                        pltpu.SemaphoreType.DMA((n, 3)), pltpu.SemaphoreType.DMA((n, 3)), pltpu.SemaphoreType.DMA,
                        pltpu.SemaphoreType.DMA((3,)), pltpu.SemaphoreType.DMA((3,))],
    )(*fulls, small)
    return outs[:n], outs[n]


HBM_SPEC = pl.BlockSpec(memory_space=pltpu.HBM)
SEM_SPEC = pl.BlockSpec(memory_space=pltpu.SEMAPHORE)
DATAFLOW = pltpu.SideEffectType.DATAFLOW_SIDE_EFFECTING


def _start_copies(name, arrays, sem_shape, after, copies):
    n = len(arrays)

    def body(*refs):
        for cp in copies(refs[:n], refs[n + 1], refs[n + 2]):
            cp.start()
        token = refs[2 * n + 3]
        token[...] = jnp.zeros_like(token)

    res = pl.pallas_call(
        body, name=name,
        out_shape=(pltpu.SemaphoreType.DMA(sem_shape), pltpu.SemaphoreType.DMA(sem_shape),
                   *[pltpu.HBM(a.shape, a.dtype) for a in arrays], jax.ShapeDtypeStruct((8, 128), F32)),
        in_specs=[HBM_SPEC] * n + [pl.BlockSpec(memory_space=pl.ANY)],
        out_specs=(SEM_SPEC, SEM_SPEC, *([HBM_SPEC] * n), pl.BlockSpec(memory_space=pltpu.VMEM)),
        input_output_aliases={i: 2 + i for i in range(n)},
        compiler_params=pltpu.CompilerParams(has_side_effects=DATAFLOW),
    )(*[pltpu.with_memory_space_constraint(a, pltpu.HBM) for a in arrays], after)
    return res[0], res[1], list(res[2:2 + n]), res[2 + n][0, 0]


def _wait_copies(name, arrays, send, recv, after, copies):
    n = len(arrays)

    def body(*refs):
        for cp in copies(refs[:n], refs[n], refs[n + 1]):
            cp.wait_send()
            cp.wait_recv()

    return pl.pallas_call(
        body, name=name, out_shape=[pltpu.HBM(a.shape, a.dtype) for a in arrays],
        in_specs=[HBM_SPEC] * n + [SEM_SPEC, SEM_SPEC, pl.BlockSpec(memory_space=pl.ANY)], out_specs=[HBM_SPEC] * n,
        input_output_aliases={i: i for i in range(n)},
        compiler_params=pltpu.CompilerParams(has_side_effects=DATAFLOW),
    )(*arrays, send, recv, after)


def _direct_gather_copies(col_sharded):
    def copies(refs, send, recv):
        x, y, c = _place()
        mine = 2 * x + y
        out = []
        for w, ref in enumerate(refs):
            own_half = _shard_of(ref, col_sharded[w], mine, c)
            k = 0
            for cx, cy in _other_chips(x, y):
                for pc in (c, 1 - c):
                    out.append(pltpu.make_async_remote_copy(
                        src_ref=own_half, dst_ref=own_half, send_sem=send.at[6 * w + k], recv_sem=recv.at[6 * w + k],
                        device_id=(cx, cy, pc), device_id_type=MESH))
                    k += 1
        return out
    return copies


def _chip_scatter_copies(n, col_sharded):
    def piece(ref, cs, chip):
        if cs:
            w = ref.shape[2] // N_CHIPS
            return ref.at[:, :, pl.ds(pl.multiple_of(chip * w, SLAB), w)]
        return ref.at[pl.ds(chip, 1)]

    def copies(refs, send, recv):
        x, y, c = _place()
        out = []
        for k, (cx, cy) in enumerate(_other_chips(x, y)):
            for w in range(n):
                out.append(pltpu.make_async_remote_copy(
                    src_ref=piece(refs[w], col_sharded[w], 2 * cx + cy), dst_ref=refs[n + w].at[pl.ds(k, 1)],
                    send_sem=send.at[3 * w + k], recv_sem=recv.at[3 * w + k], device_id=(cx, cy, c), device_id_type=MESH))
        return out
    return copies


def _shard_views(gs, col_sharded):
    return [g.reshape(1, *g.shape) if cs else g.reshape(N_CHIPS, g.shape[0] // N_CHIPS, g.shape[1])
            for g, cs in zip(gs, col_sharded)]


DMA_CHUNK_BYTES = 1 << 20


def _chunk_rows(shape, itemsize):
    s, rows, cols = shape
    n = 1
    while s * (rows // n) * cols * itemsize > DMA_CHUNK_BYTES and (rows // n) % 32 == 0:
        n *= 2
    return rows // n


def _send_rows(src, src_row0, dst, dst_row0, rows, send_sem, recv_sem, device):
    step = _chunk_rows((src.shape[0], rows, src.shape[2]), src.dtype.itemsize)
    for r in range(0, rows, step):
        pltpu.make_async_remote_copy(src_ref=src.at[:, pl.ds(src_row0 + r, step)], dst_ref=dst.at[:, pl.ds(dst_row0 + r, step)],
                                     send_sem=send_sem, recv_sem=recv_sem, device_id=device, device_id_type=MESH).start()


def _pair_exchange_halves(views, name):
    n = len(views)
    half_shapes = [(v.shape[0], v.shape[1] // 2, v.shape[2]) for v in views]

    def body(*refs):
        ins, got = refs[:n], refs[n:2 * n]
        send, recv = refs[2 * n:]
        x, y, c = _place()
        sibling = (x, y, 1 - c)
        for w in range(n):
            hr = half_shapes[w][1]
            _send_rows(ins[w], (1 - c) * hr, got[w], 0, hr, send.at[w], recv.at[w], sibling)
        for w in range(n):
            hr = half_shapes[w][1]
            pltpu.make_async_remote_copy(src_ref=ins[w].at[:, pl.ds((1 - c) * hr, hr)], dst_ref=got[w], send_sem=send.at[w],
                                         recv_sem=recv.at[w], device_id=sibling, device_id_type=MESH).wait()

    any_spec = pl.BlockSpec(memory_space=pl.ANY)
    return pl.pallas_call(
        body, name=name,
        out_shape=[jax.ShapeDtypeStruct(s, v.dtype) for s, v in zip(half_shapes, views)],
        in_specs=[any_spec] * n, out_specs=[any_spec] * n,
        scratch_shapes=[pltpu.SemaphoreType.DMA((n,)), pltpu.SemaphoreType.DMA((n,))],
    )(*views)


def _pair_sum(view, got, core, name):
    s, r, cols = view.shape
    hr = r // 2
    tr = _row_tile(hr, cols)
    nb = hr // tr

    def body(core_ref, a_ref, b_ref, o_ref):
        del core_ref
        o_ref[...] = (a_ref[...].astype(F32) + b_ref[...].astype(F32)).astype(BF16)

    same = pl.BlockSpec((None, tr, cols), lambda j, i, core_ref: (j, i, 0))
    return pl.pallas_call(
        body, name=name,
        grid_spec=pltpu.PrefetchScalarGridSpec(
            num_scalar_prefetch=1, grid=(s, nb),
            in_specs=[pl.BlockSpec((None, tr, cols), lambda j, i, core_ref: (j, core_ref[0] * nb + i, 0)), same],
            out_specs=same),
        out_shape=jax.ShapeDtypeStruct((s, hr, cols), BF16), compiler_params=_params("parallel", "parallel"))(core, view, got)


def _piece_cols(part, col_sharded):
    return part.shape[2] // N_CHIPS if col_sharded else part.shape[2]


def _reduce_over_chips(parts, col_sharded, place, after, between, tag):
    n = len(parts)
    lands = [lax.empty((3, p.shape[1], _piece_cols(p, cs)), p.dtype) for p, cs in zip(parts, col_sharded)]
    copies = _chip_scatter_copies(n, col_sharded)
    send, recv, flying, token = _start_copies(f"grads_scatter_{tag}_start", list(parts) + lands, (3 * n,), after, copies)
    done, marker = between(token)
    landed = _wait_copies(f"grads_scatter_{tag}_wait", flying, send, recv, marker, copies)
    reduced = [_chip_sum(landed[w], landed[n + w], col_sharded[w], place, f"grads_chip_sum_{tag}_{w}") for w in range(n)]
    return reduced, done


def _chip_sum(part, got, col_sharded, place, name):
    _, hr, _ = part.shape
    cols = _piece_cols(part, col_sharded)
    tr = _row_tile(hr, cols)
    nb = hr // tr

    def body(place_ref, own_ref, g0_ref, g1_ref, g2_ref, o_ref):
        del place_ref
        acc = own_ref[...].astype(F32) + g0_ref[...].astype(F32)
        o_ref[...] = acc + g1_ref[...].astype(F32) + g2_ref[...].astype(F32)

    if col_sharded:
        own = pl.BlockSpec((None, tr, cols), lambda i, pr: (0, i, pr[0]))
    else:
        own = pl.BlockSpec((None, tr, cols), lambda i, pr: (pr[0], i, 0))
    others = [pl.BlockSpec((None, tr, cols), lambda i, pr, k=k: (k, i, 0)) for k in range(3)]
    return pl.pallas_call(
        body, name=name,
        grid_spec=pltpu.PrefetchScalarGridSpec(
            num_scalar_prefetch=1, grid=(nb,), in_specs=[own] + others,
            out_specs=pl.BlockSpec((tr, cols), lambda i, pr: (pr[1] * nb + i, 0))),
        out_shape=jax.ShapeDtypeStruct((2 * hr, cols), F32), compiler_params=_params("parallel"))(place, part, got, got, got)


def _pair_join_halves(fulls):
    n = len(fulls)
    views = [f.reshape(1, *f.shape) for f in fulls]

    def body(*refs):
        outs = refs[n:2 * n]
        send, recv = refs[2 * n:]
        x, y, c = _place()
        sibling = (x, y, 1 - c)
        for w in range(n):
            hr = outs[w].shape[1] // 2
            _send_rows(outs[w], c * hr, outs[w], c * hr, hr, send.at[w], recv.at[w], sibling)
        for w in range(n):
            hr = outs[w].shape[1] // 2
            pltpu.make_async_remote_copy(
                src_ref=outs[w].at[:, pl.ds(c * hr, hr)], dst_ref=outs[w].at[:, pl.ds((1 - c) * hr, hr)], send_sem=send.at[w],
                recv_sem=recv.at[w], device_id=sibling, device_id_type=MESH).wait()

    any_spec = pl.BlockSpec(memory_space=pl.ANY)
    outs = pl.pallas_call(
        body, name="grads_pair_join",
        out_shape=[jax.ShapeDtypeStruct(v.shape, v.dtype) for v in views],
        in_specs=[any_spec] * n, out_specs=[any_spec] * n, input_output_aliases={w: w for w in range(n)},
        scratch_shapes=[pltpu.SemaphoreType.DMA((n,)), pltpu.SemaphoreType.DMA((n,))],
    )(*views)
    return [o[0] for o in outs]


def _all_sum_small(vec, n_sum, d):
    rows = vec.shape[0]

    def body(v_ref, sum_ref, kept_ref, all_ref, send, recv):
        x, y, c = _place()
        me = 4 * x + 2 * y + c
        all_ref[me] = v_ref[...]
        peers = _all_devices(x, y, c)
        copies = []
        for k, (peer, _) in enumerate(peers):
            cp = pltpu.make_async_remote_copy(src_ref=v_ref, dst_ref=all_ref.at[me], send_sem=send.at[k], recv_sem=recv.at[k],
                                              device_id=peer, device_id_type=MESH)
            cp.start()
            copies.append(cp)
        for k, (_, src) in enumerate(peers):
            pltpu.make_async_remote_copy(src_ref=v_ref, dst_ref=all_ref.at[src], send_sem=send.at[k], recv_sem=recv.at[k],
                                         device_id=(x, y, c), device_id_type=MESH).wait_recv()
        for cp in copies:
            cp.wait_send()
        total = all_ref[0, 0:n_sum, :]
        for i in range(1, N_DEV):
            total = total + all_ref[i, 0:n_sum, :]
        sum_ref[...] = total
        loss = 0.5 / d * jnp.sum(total[0:8, :])
        sum_ref[0:8, :] = jnp.full((8, 128), loss, F32)
        for i in range(N_DEV):
            kept_ref[i] = all_ref[i, n_sum:rows, :]

    vm = pl.BlockSpec(memory_space=pltpu.VMEM)
    return pl.pallas_call(
        body, name="all_sum_small",
        out_shape=[jax.ShapeDtypeStruct((n_sum, 128), F32), jax.ShapeDtypeStruct((N_DEV, rows - n_sum, 128), F32)],
        in_specs=[vm], out_specs=[vm, vm],
        scratch_shapes=[pltpu.VMEM((N_DEV, rows, 128), F32), pltpu.SemaphoreType.DMA((N_DEV - 1,)),
                        pltpu.SemaphoreType.DMA((N_DEV - 1,))],
        compiler_params=pltpu.CompilerParams(vmem_limit_bytes=VMEM_LIMIT_V7X),
    )(vec)


def _row_tile(rows, cols, itemsize=4, budget=2 << 20):
    t = rows
    while t * cols * itemsize > budget and t % 16 == 0:
        t //= 2
    return t


def _adamw(w, g, m, v, name):
    rows, cols = w.shape
    tr = _row_tile(rows, cols, budget=1 << 20)

    def body(w_ref, g_ref, m_ref, v_ref, d_ref, nm_ref, nv_ref):
        g_ = g_ref[...]
        nm = ADAM_B1 * m_ref[...] + (1.0 - ADAM_B1) * g_
        nv = ADAM_B2 * v_ref[...] + (1.0 - ADAM_B2) * (g_ * g_)
        m_hat = nm / (1.0 - ADAM_B1 ** ADAM_STEP)
        v_hat = nv / (1.0 - ADAM_B2 ** ADAM_STEP)
        d_ref[...] = -ADAM_LR * (m_hat / (jnp.sqrt(v_hat) + ADAM_EPS) + ADAM_WD * w_ref[...])
        nm_ref[...] = nm
        nv_ref[...] = nv

    spec = pl.BlockSpec((tr, cols), lambda i: (i, 0))
    shp = jax.ShapeDtypeStruct((rows, cols), F32)
    return pl.pallas_call(body, name=name, grid=(rows // tr,), in_specs=[spec] * 4, out_specs=[spec] * 3,
                          out_shape=[shp] * 3, compiler_params=_params("parallel"))(w, g, m, v)


def _all_devices(x, y, c):
    out = []
    for k in range(1, N_DEV):
        peer = (x ^ ((k >> 2) & 1), y ^ ((k >> 1) & 1), c ^ (k & 1))
        out.append((peer, 4 * peer[0] + 2 * peer[1] + peer[2]))
    return out


def _ada_exchange(c, w_shard, b_ada):
    bsz, d = c.shape
    cs = w_shard.shape[1]

    def body(c_ref, w_ref, b_ref, mod_ref, act_ref, c_all, part, pieces, csend, crecv, psend, precv):
        x, y, core = _place()
        me = 4 * x + 2 * y + core
        chip = 2 * x + y
        c_all[me] = c_ref[...]
        peers = _all_devices(x, y, core)
        copies = []
        for k, (peer, _) in enumerate(peers):
            cp = pltpu.make_async_remote_copy(src_ref=c_ref, dst_ref=c_all.at[me], send_sem=csend.at[k], recv_sem=crecv.at[k],
                                              device_id=peer, device_id_type=MESH)
            cp.start()
            copies.append(cp)
        for k, (_, src) in enumerate(peers):
            pltpu.make_async_remote_copy(src_ref=c_ref, dst_ref=c_all.at[src], send_sem=csend.at[k], recv_sem=crecv.at[k],
                                         device_id=(x, y, core), device_id_type=MESH).wait_recv()
        rows = jnp.concatenate([c_all[i] for i in range(N_DEV)], axis=0)
        act = rows * _sigmoid(rows)
        act_ref[...] = act
        prod = jnp.dot(act.astype(BF16), w_ref[...].astype(BF16), preferred_element_type=F32)
        for i in range(N_DEV):
            part[i] = prod[i * bsz:(i + 1) * bsz, :]
        pieces[chip] = part[me]
        chips = _other_chips(x, y)
        for k, (cx, cy) in enumerate(chips):
            cp = pltpu.make_async_remote_copy(src_ref=part.at[4 * cx + 2 * cy + core], dst_ref=pieces.at[chip],
                                              send_sem=psend.at[k], recv_sem=precv.at[k], device_id=(cx, cy, core),
                                              device_id_type=MESH)
            cp.start()
            copies.append(cp)
        for k, (cx, cy) in enumerate(chips):
            pltpu.make_async_remote_copy(src_ref=part.at[me], dst_ref=pieces.at[2 * cx + cy], send_sem=psend.at[k],
                                         recv_sem=precv.at[k], device_id=(cx, cy, core), device_id_type=MESH).wait_recv()
        for cp in copies:
            cp.wait_send()
        mod_ref[...] = jnp.concatenate([pieces[j] for j in range(N_CHIPS)], axis=1) + b_ref[...]

    vm = pl.BlockSpec(memory_space=pltpu.VMEM)
    return pl.pallas_call(
        body, name="ada_exchange", in_specs=[vm] * 3, out_specs=[vm] * 2,
        out_shape=[jax.ShapeDtypeStruct((bsz, 3 * d), F32), jax.ShapeDtypeStruct((N_DEV * bsz, d), F32)],
        scratch_shapes=[pltpu.VMEM((N_DEV, bsz, d), F32), pltpu.VMEM((N_DEV, bsz, cs), F32), pltpu.VMEM((N_CHIPS, bsz, cs), F32),
                        pltpu.SemaphoreType.DMA((N_DEV - 1,)), pltpu.SemaphoreType.DMA((N_DEV - 1,)),
                        pltpu.SemaphoreType.DMA((3,)), pltpu.SemaphoreType.DMA((3,))],
        compiler_params=pltpu.CompilerParams(vmem_limit_bytes=VMEM_LIMIT_V7X))(c, w_shard, b_ada)


def _grad_w_ada_cols(act_t, d_cols):
    d, n = act_t.shape
    cs = d_cols.shape[1]

    def body(a_ref, g_ref, o_ref):
        a, g = a_ref[...], g_ref[...]
        acc = a[:, 0:1] * g[0:1, :]
        for b in range(1, n):
            acc = acc + a[:, b:b + 1] * g[b:b + 1, :]
        o_ref[...] = acc

    vm = pl.BlockSpec(memory_space=pltpu.VMEM)
    return pl.pallas_call(body, name="grad_w_ada", in_specs=[vm] * 2, out_specs=vm,
                          out_shape=jax.ShapeDtypeStruct((d, cs), F32),
                          compiler_params=pltpu.CompilerParams(vmem_limit_bytes=VMEM_LIMIT_V7X))(act_t, d_cols)


def _permute_w_in(w_nat, lay):
    d = lay.d

    def copy_body(src_ref, dst_in_ref, dst_ref):
        del dst_in_ref
        dst_ref[...] = src_ref[...]

    def first_body(src_ref, dst_ref):
        dst_ref[...] = src_ref[...]

    w_all = pl.pallas_call(
        first_body, name="permute_w_attn", grid=(2 * PAIR_SLABS,),
        in_specs=[pl.BlockSpec((d, SLAB), lambda s: (0, lay.attn_nat_slab(s)))],
        out_specs=pl.BlockSpec((d, SLAB), lambda s: (0, s)),
        out_shape=jax.ShapeDtypeStruct((d, lay.np), BF16), compiler_params=_params("arbitrary"))(w_nat)
    n_rest = 6 * d // CONV_TILE
    base = lay.c0 // CONV_TILE
    return pl.pallas_call(
        copy_body, name="permute_w_rest", grid=(n_rest,),
        in_specs=[pl.BlockSpec((d, CONV_TILE), lambda t: (0, lay.rest_nat_tile(t))), pl.BlockSpec(memory_space=pl.ANY)],
        out_specs=pl.BlockSpec((d, CONV_TILE), lambda t: (0, base + t)),
        out_shape=jax.ShapeDtypeStruct((d, lay.np), BF16), input_output_aliases={1: 0},
        compiler_params=_params("arbitrary"))(w_nat, w_all)


def _project(x2, mod3, w_all, b_all, seq, col0, ncols, tn, out_dtype, want_ht, name):
    t, d = x2.shape
    tm = min(1024, seq)
    per_seq = seq // tm
    j0 = col0 // tn

    def body(x_ref, mod_ref, w_ref, b_ref, o_ref, *rest):
        h_ref = rest[-1]

        @pl.when(pl.program_id(1) == 0)
        def _():
            h = x_ref[...] * (1.0 + mod_ref[:, d:2 * d]) + mod_ref[:, 0:d]
            h_ref[...] = h.astype(BF16)
            if want_ht:
                rest[0][...] = h.T.astype(BF16)

        o_ref[...] = (jnp.dot(h_ref[...], w_ref[...], preferred_element_type=F32) + b_ref[...]).astype(out_dtype)

    out_shape = [jax.ShapeDtypeStruct((t, ncols), out_dtype)]
    out_specs = [pl.BlockSpec((tm, tn), lambda i, j: (i, j))]
    if want_ht:
        out_shape.append(jax.ShapeDtypeStruct((d, t), BF16))
        out_specs.append(pl.BlockSpec((d, tm), lambda i, j: (0, i)))
    return pl.pallas_call(
        body, name=name, grid=(t // tm, ncols // tn),
        in_specs=[pl.BlockSpec((tm, d), lambda i, j: (i, 0)),
                  pl.BlockSpec((None, 1, 3 * d), lambda i, j: (i // per_seq, 0, 0)),
                  pl.BlockSpec((d, tn), lambda i, j: (0, j0 + j)),
                  pl.BlockSpec((1, tn), lambda i, j: (0, j0 + j))],
        out_specs=out_specs, out_shape=out_shape,
        scratch_shapes=[pltpu.VMEM((tm, d), BF16)],
        compiler_params=_params("arbitrary", "arbitrary"))(x2, mod3, w_all, b_all)


def _slope(g, p, hh):
    head = 4 * g + 2 * p + hh
    return 2.0 ** (-ALIBI_MAX_EXP * (head + 1.0) / N_HEADS)


def _ld_rows(ref, start, n, stride):
    if stride == 1:
        return ref[pl.ds(start, n), :]
    return ref[pl.ds(start, n, stride=stride), :]


def _st_rows(ref, start, n, stride, val):
    if stride == 1:
        ref[pl.ds(start, n), :] = val
    else:
        ref[pl.ds(start, n, stride=stride), :] = val


def _sub_blocks(g, seq):
    return seq // DILATIONS[g] // SUB


def _key_rows(g, seq):
    return SUB if _sub_blocks(g, seq) == 1 else 2 * SUB


def _fill_bias(bias_ref, p, seq):
    for g in range(N_GROUPS):
        nk = _key_rows(g, seq)
        diff = lax.broadcasted_iota(jnp.int32, (SUB, nk), 0) - lax.broadcasted_iota(jnp.int32, (SUB, nk), 1)
        for i, off in enumerate((0, SUB)):
            if i == 1 and nk == SUB:
                continue
            delta = diff + off
            ok = (delta >= 0) & (delta <= SUB)
            dist = (delta * DILATIONS[g]).astype(F32)
            for hh in range(2):
                slope = jnp.where(p == 0, _slope(g, 0, hh), _slope(g, 1, hh))
                bias_ref[g, i, hh, :, 0:nk] = jnp.where(ok, -slope * dist, NEG)


def _to_sub_major(pa_ref, col, sub_ref, stage, dil, seq):
    cols = slice(col * SLAB, (col + 1) * SLAB)
    if dil == 1:
        sub_ref[...] = pa_ref[:, cols]
        return
    n = seq // dil
    stage[...] = pa_ref[:, cols].astype(F32)
    for r in range(dil):
        sub_ref[pl.ds(r * n, n), :] = stage[pl.ds(r, n, stride=dil), :].astype(BF16)


def _block_rows(it, g, seq):
    dil, nb = DILATIONS[g], _sub_blocks(g, seq)
    row0 = pl.multiple_of(it * SUB, SUB)
    if nb == 1:
        return row0, row0, 0, it
    blk = it % nb
    first = blk == 0
    krow0 = pl.multiple_of(row0 - jnp.where(first, 0, SUB), SUB)
    nat = row0 if dil == 1 else it // nb + dil * SUB * blk
    return row0, krow0, jnp.where(first, 0, 1), nat


def _nt(a, b):
    return lax.dot_general(a, b, (((1,), (1,)), ((), ())), preferred_element_type=F32)


def _tn(a, b):
    return lax.dot_general(a, b, (((0,), (0,)), ((), ())), preferred_element_type=F32)


def _head_sums(t):
    rows = t.shape[0]
    lo = jnp.broadcast_to(jnp.sum(t[:, :HEAD_DIM], axis=-1, keepdims=True), (rows, HEAD_DIM))
    hi = jnp.broadcast_to(jnp.sum(t[:, HEAD_DIM:], axis=-1, keepdims=True), (rows, HEAD_DIM))
    return jnp.concatenate([lo, hi], axis=-1)


def _attn_fwd(pa, bsz, seq):
    t = pa.shape[0]
    n_blocks = seq // SUB
    chunk = 256

    def body(pa_ref, o_ref, lse_ref, a_ref, sub, stage, bias_ref):
        p = pl.program_id(1)
        _fill_bias(bias_ref, p, seq)
        head0 = lax.broadcasted_iota(jnp.int32, (SUB, SLAB), 1) < HEAD_DIM
        for g in range(N_GROUPS):
            dil = DILATIONS[g]
            for w in range(3):
                _to_sub_major(pa_ref, 3 * w + g, sub.at[w], stage, dil, seq)
            nk = _key_rows(g, seq)

            def block(it, g=g, dil=dil, nk=nk):
                row0, krow0, bi, nat = _block_rows(it, g, seq)
                q = sub[0, pl.ds(row0, SUB), :]
                kw = sub[1, pl.ds(krow0, nk), :]
                vw = sub[2, pl.ds(krow0, nk), :]
                outs, lses = [], []
                for hh in range(2):
                    qm = jnp.where(head0 if hh == 0 else ~head0, q, jnp.zeros_like(q)) * (HEAD_DIM ** -0.5)
                    s = _nt(qm, kw) + bias_ref[g, bi, hh, :, 0:nk]
                    m = jnp.max(s, axis=-1, keepdims=True)
                    e = jnp.exp(s - m)
                    den = jnp.sum(e, axis=-1, keepdims=True)
                    outs.append(jnp.dot((e * (1.0 / den)).astype(BF16), vw, preferred_element_type=F32))
                    lses.append(m + jnp.log(den))
                _st_rows(o_ref.at[g], nat, SUB, dil, jnp.where(head0, outs[0], outs[1]))
                _st_rows(lse_ref.at[g], nat, SUB, dil, jnp.where(head0, lses[0], lses[1]))

            def two_blocks(i, carry, block=block):
                block(2 * i)
                block(2 * i + 1)
                return carry

            lax.fori_loop(0, n_blocks // 2, two_blocks, 0)

        def mix(i, carry):
            rows = pl.ds(pl.multiple_of(i * chunk, chunk), chunk)
            l0, l1, l2 = lse_ref[0, rows, :], lse_ref[1, rows, :], lse_ref[2, rows, :]
            m = jnp.maximum(jnp.maximum(l0, l1), l2)
            e0, e1, e2 = jnp.exp(l0 - m), jnp.exp(l1 - m), jnp.exp(l2 - m)
            tot = e0 + e1 + e2
            o = (e0 / tot) * o_ref[0, rows, :] + (e1 / tot) * o_ref[1, rows, :] + (e2 / tot) * o_ref[2, rows, :]
            z = pa_ref[rows, 9 * SLAB:10 * SLAB].astype(F32)
            a_ref[rows, :] = (o * (z * _sigmoid(z))).astype(BF16)
            return carry

        lax.fori_loop(0, seq // chunk, mix, 0)

    big = jax.ShapeDtypeStruct((N_GROUPS, t, 2 * SLAB), F32)
    return pl.pallas_call(
        body, name="attn_fwd", grid=(bsz, 2),
        in_specs=[pl.BlockSpec((seq, PAIR_COLS), lambda b, p: (b, p))],
        out_specs=[pl.BlockSpec((N_GROUPS, seq, SLAB), lambda b, p: (0, b, p)),
                   pl.BlockSpec((N_GROUPS, seq, SLAB), lambda b, p: (0, b, p)),
                   pl.BlockSpec((seq, SLAB), lambda b, p: (b, p))],
        out_shape=[big, big, jax.ShapeDtypeStruct((t, 2 * SLAB), BF16)],
        scratch_shapes=[pltpu.VMEM((3, seq, SLAB), BF16), pltpu.VMEM((seq, SLAB), F32),
                        pltpu.VMEM((N_GROUPS, 2, 2, SUB, 2 * SUB), F32)],
        compiler_params=_params("arbitrary", "arbitrary"))(pa)


def _shift_down(v, k):
    rows = lax.broadcasted_iota(jnp.int32, v.shape, 0)
    return jnp.where(rows >= k, pltpu.roll(v, k, 0), 0.0)


def _shift_up(v, k):
    n = v.shape[0]
    rows = lax.broadcasted_iota(jnp.int32, v.shape, 0)
    return jnp.where(rows < n - k, pltpu.roll(v, n - k, 0), 0.0)


def _conv_fwd(pr, conv_w, bsz, seq, d):
    t = pr.shape[0]
    ct = CONV_TILE

    def body(p_ref, cw_ref, o_ref):
        u = p_ref[:, 2 * ct:3 * ct].astype(F32) * p_ref[:, 0:ct].astype(F32)
        cw = cw_ref[...]
        conv = cw[0:1, :] * _shift_down(u, 2)
        conv = conv + cw[1:2, :] * _shift_down(u, 1)
        conv = conv + cw[2:3, :] * u
        z = p_ref[:, 3 * ct:4 * ct].astype(F32)
        o_ref[...] = (p_ref[:, ct:2 * ct].astype(F32) * conv * (z * _sigmoid(z))).astype(BF16)

    return pl.pallas_call(
        body, name="conv_fwd", grid=(bsz, d // ct),
        in_specs=[pl.BlockSpec((seq, 4 * ct), lambda b, j: (b, j)), pl.BlockSpec((3, ct), lambda b, j: (0, j))],
        out_specs=pl.BlockSpec((seq, ct), lambda b, j: (b, j)),
        out_shape=jax.ShapeDtypeStruct((t, d), BF16), compiler_params=_params("parallel", "parallel"))(pr, conv_w)


def _tail(a_in, b_in, pr, x2, target2, mod3, w_pa, w_pc, w_out, b_out, ln_g, ln_b, seq, lay):
    t, d = x2.shape
    tm = 256
    per_seq = seq // tm
    n_steps = t // tm
    gate_blk = 4 * d // d

    def nt(a, b):
        return lax.dot_general(a, b, (((1,), (1,)), ((), ())), preferred_element_type=F32)

    def tn(a, b):
        return lax.dot_general(a, b, (((0,), (0,)), ((), ())), preferred_element_type=F32)

    def body(a_ref, b_ref, ga_ref, gb_ref, x_ref, tg_ref, mod_ref, wpa_ref, wpc_ref, wo_ref, bo_ref, lg_ref, lb_ref,
             dpg_ref, da_ref, db_ref, gx_ref, dgate_ref, small_ref, gwpa_hbm, gwpc_hbm, gwo_hbm,
             acc_pa, acc_pc, acc_o, sem):
        i = pl.program_id(0)

        @pl.when(i == 0)
        def _():
            acc_pa[...] = jnp.zeros_like(acc_pa)
            acc_pc[...] = jnp.zeros_like(acc_pc)
            acc_o[...] = jnp.zeros_like(acc_o)
            small_ref[...] = jnp.zeros_like(small_ref)

        @pl.when(i % per_seq == 0)
        def _():
            dgate_ref[...] = jnp.zeros_like(dgate_ref)

        a_bf, b_bf = a_ref[...], b_ref[...]
        y_attn = jnp.dot(a_bf, wpa_ref[...], preferred_element_type=F32)
        y_conv = jnp.dot(b_bf, wpc_ref[...], preferred_element_type=F32)
        sa, sb = _sigmoid(ga_ref[...].astype(F32)), _sigmoid(gb_ref[...].astype(F32))
        merged = (sa * y_attn + sb * y_conv).astype(BF16)
        mo = jnp.dot(merged, wo_ref[...], preferred_element_type=F32) + bo_ref[...]
        gate = mod_ref[:, 2 * d:3 * d]
        r = ALPHA * x_ref[...] + gate * mo
        mu = jnp.mean(r, axis=-1, keepdims=True)
        cen = r - mu
        var = jnp.mean(cen * cen, axis=-1, keepdims=True)
        rstd = lax.rsqrt(var + LN_EPS)
        xhat = cen * rstd
        err = xhat * lg_ref[...] + lb_ref[...] - tg_ref[...]
        dy = err * (1.0 / d)
        dxhat = dy * lg_ref[...]
        dr = rstd * (dxhat - jnp.mean(dxhat, axis=-1, keepdims=True) - xhat * jnp.mean(dxhat * xhat, axis=-1, keepdims=True))
        gx_ref[...] = ALPHA * dr
        dgate_ref[...] += jnp.sum(dr * mo, axis=0, keepdims=True)
        d_mo = dr * gate
        small_ref[0:1, :] += jnp.sum(d_mo, axis=0, keepdims=True)
        small_ref[1:2, :] += jnp.sum(dy * xhat, axis=0, keepdims=True)
        small_ref[2:3, :] += jnp.sum(dy, axis=0, keepdims=True)
        small_ref[3:4, :] += jnp.sum(err * err, axis=0, keepdims=True)
        d_mo_bf = d_mo.astype(BF16)
        acc_o[...] += tn(merged, d_mo_bf)
        dmerged = nt(d_mo_bf, wo_ref[...])
        dy_attn = (dmerged * sa).astype(BF16)
        dy_conv = (dmerged * sb).astype(BF16)
        dpg_ref[:, 0:d] = (dmerged * y_attn * sa * (1.0 - sa)).astype(BF16)
        dpg_ref[:, d:2 * d] = (dmerged * y_conv * sb * (1.0 - sb)).astype(BF16)
        acc_pa[...] += tn(a_bf, dy_attn)
        acc_pc[...] += tn(b_bf, dy_conv)
        da_ref[...] = nt(dy_attn, wpa_ref[...])
        db_ref[...] = nt(dy_conv, wpc_ref[...])

        @pl.when(i == n_steps - 1)
        def _():
            copies = [pltpu.make_async_copy(acc_pa, gwpa_hbm, sem.at[0]), pltpu.make_async_copy(acc_pc, gwpc_hbm, sem.at[1]),
                      pltpu.make_async_copy(acc_o, gwo_hbm, sem.at[2])]
            for cp in copies:
                cp.start()
            for cp in copies:
                cp.wait()

    row = lambda w: pl.BlockSpec((tm, w), lambda i: (i, 0))
    const = lambda shp: pl.BlockSpec(shp, lambda i: (0,) * len(shp), pipeline_mode=pl.Buffered(1))
    any_spec = pl.BlockSpec(memory_space=pl.ANY)
    return pl.pallas_call(
        body, name="tail", grid=(n_steps,),
        in_specs=[row(Z_WIDTH), row(d),
                  pl.BlockSpec((tm, d), lambda i: (i, gate_blk)), pl.BlockSpec((tm, d), lambda i: (i, gate_blk + 1)),
                  row(d), row(d), pl.BlockSpec((None, 1, 3 * d), lambda i: (i // per_seq, 0, 0)),
                  const((Z_WIDTH, d)), const((d, d)), const((d, d)), const((1, d)), const((1, d)), const((1, d))],
        out_specs=[pl.BlockSpec((tm, 2 * d), lambda i: (i, lay.g0 // (2 * d))), row(Z_WIDTH), row(d), row(d),
                   pl.BlockSpec((None, 1, d), lambda i: (i // per_seq, 0, 0)), pl.BlockSpec((8, d), lambda i: (0, 0)),
                   any_spec, any_spec, any_spec],
        out_shape=[jax.ShapeDtypeStruct((t, lay.np), BF16), jax.ShapeDtypeStruct((t, Z_WIDTH), F32),
                   jax.ShapeDtypeStruct((t, d), F32), jax.ShapeDtypeStruct((t, d), F32),
                   jax.ShapeDtypeStruct((t // seq, 1, d), F32), jax.ShapeDtypeStruct((8, d), F32),
                   jax.ShapeDtypeStruct((Z_WIDTH, d), F32), jax.ShapeDtypeStruct((d, d), F32),
                   jax.ShapeDtypeStruct((d, d), F32)],
        scratch_shapes=[pltpu.VMEM((Z_WIDTH, d), F32), pltpu.VMEM((d, d), F32), pltpu.VMEM((d, d), F32),
                        pltpu.SemaphoreType.DMA((3,))],
        compiler_params=_params("arbitrary"),
    )(a_in, b_in, pr, pr, x2, target2, mod3, w_pa, w_pc, w_out, b_out, ln_g, ln_b)


def _conv_bwd(dproj, db, pr, conv_w, bsz, seq, lay):
    d = lay.d
    ct = CONV_TILE
    base = lay.c0 // (4 * ct)

    def body(dp_in, db_ref, p_ref, cw_ref, dp_ref, gcw_ref):
        del dp_in
        u_x, g_b, g_c, z = [p_ref[:, k * ct:(k + 1) * ct].astype(F32) for k in range(4)]
        cw = cw_ref[...]
        u = g_c * u_x
        u1, u2 = _shift_down(u, 1), _shift_down(u, 2)
        conv = cw[0:1, :] * u2 + cw[1:2, :] * u1 + cw[2:3, :] * u
        sig = _sigmoid(z)
        sl = z * sig
        dbv = db_ref[...]
        gbc = g_b * conv
        dp_ref[:, ct:2 * ct] = (dbv * sl * conv).astype(BF16)
        dp_ref[:, 3 * ct:4 * ct] = (dbv * gbc * (sig * (1.0 + z * (1.0 - sig)))).astype(BF16)
        dconv = dbv * sl * g_b

        @pl.when(pl.program_id(1) == 0)
        def _():
            gcw_ref[...] = jnp.zeros_like(gcw_ref)

        gcw_ref[0:1, :] += jnp.sum(dconv * u2, axis=0, keepdims=True)
        gcw_ref[1:2, :] += jnp.sum(dconv * u1, axis=0, keepdims=True)
        gcw_ref[2:3, :] += jnp.sum(dconv * u, axis=0, keepdims=True)
        du = cw[2:3, :] * dconv + cw[1:2, :] * _shift_up(dconv, 1) + cw[0:1, :] * _shift_up(dconv, 2)
        dp_ref[:, 0:ct] = (du * g_c).astype(BF16)
        dp_ref[:, 2 * ct:3 * ct] = (du * u_x).astype(BF16)

    return pl.pallas_call(
        body, name="conv_bwd", grid=(d // ct, bsz),
        in_specs=[pl.BlockSpec(memory_space=pl.ANY), pl.BlockSpec((seq, ct), lambda j, b: (b, j)),
                  pl.BlockSpec((seq, 4 * ct), lambda j, b: (b, j)), pl.BlockSpec((3, ct), lambda j, b: (0, j))],
        out_specs=[pl.BlockSpec((seq, 4 * ct), lambda j, b: (b, base + j)), pl.BlockSpec((8, ct), lambda j, b: (0, j))],
        out_shape=[jax.ShapeDtypeStruct(dproj.shape, BF16), jax.ShapeDtypeStruct((8, d), F32)],
        input_output_aliases={0: 0}, compiler_params=_params("arbitrary", "arbitrary"))(dproj, db, pr, conv_w)


def _attn_bwd(dproj, pa, o_all, lse_all, da, bsz, seq):
    n_blocks = seq // SUB
    chunk = 256

    def body(dp_in, pa_ref, o_ref, lse_ref, da_ref, dp_ref, sub, stage, dsub, dog, cvec, bias_ref):
        del dp_in
        p = pl.program_id(1)
        _fill_bias(bias_ref, p, seq)
        head0 = lax.broadcasted_iota(jnp.int32, (SUB, SLAB), 1) < HEAD_DIM

        def mix_bwd(i, carry):
            rows = pl.ds(pl.multiple_of(i * chunk, chunk), chunk)
            ls = [lse_ref[g, rows, :] for g in range(N_GROUPS)]
            os_ = [o_ref[g, rows, :] for g in range(N_GROUPS)]
            m = jnp.maximum(jnp.maximum(ls[0], ls[1]), ls[2])
            es = [jnp.exp(l - m) for l in ls]
            tot = es[0] + es[1] + es[2]
            ws = [e / tot for e in es]
            o = ws[0] * os_[0] + ws[1] * os_[1] + ws[2] * os_[2]
            z = pa_ref[rows, 9 * SLAB:10 * SLAB].astype(F32)
            sig = _sigmoid(z)
            dav = da_ref[rows, :]
            do = dav * (z * sig)
            dp_ref[rows, 9 * SLAB:10 * SLAB] = (dav * o * (sig * (1.0 + z * (1.0 - sig)))).astype(BF16)
            wsum = _head_sums(do * o)
            for g in range(N_GROUPS):
                dog[g, rows, :] = ws[g] * do
                cvec[g, rows, :] = -(ws[g] * wsum)
            return carry

        lax.fori_loop(0, seq // chunk, mix_bwd, 0)

        for g in range(N_GROUPS):
            dil = DILATIONS[g]
            for w in range(3):
                _to_sub_major(pa_ref, 3 * w + g, sub.at[w], stage, dil, seq)
            dsub[1] = jnp.zeros((seq, SLAB), F32)
            dsub[2] = jnp.zeros((seq, SLAB), F32)
            nk = _key_rows(g, seq)

            def block(it, g=g, dil=dil, nk=nk):
                row0, krow0, bi, nat = _block_rows(it, g, seq)
                q = sub[0, pl.ds(row0, SUB), :]
                kw = sub[1, pl.ds(krow0, nk), :]
                vw = sub[2, pl.ds(krow0, nk), :]
                do = _ld_rows(dog.at[g], nat, SUB, dil).astype(BF16)
                cv = _ld_rows(cvec.at[g], nat, SUB, dil)
                lse = _ld_rows(lse_ref.at[g], nat, SUB, dil)
                zero = jnp.zeros_like(q)
                dqs, dss, probs, qms, doms = [], [], [], [], []
                for hh in range(2):
                    mask = head0 if hh == 0 else ~head0
                    lane = hh * HEAD_DIM
                    qm = jnp.where(mask, q, zero)
                    dom = jnp.where(mask, do, zero)
                    s = _nt(qm * (HEAD_DIM ** -0.5), kw) + bias_ref[g, bi, hh, :, 0:nk]
                    prob = jnp.exp(s - lse[:, lane:lane + 1])
                    ds = (prob * (_nt(dom, vw) + cv[:, lane:lane + 1]) * (HEAD_DIM ** -0.5)).astype(BF16)
                    dqs.append(jnp.dot(ds, kw, preferred_element_type=F32))
                    dss.append(ds)
                    probs.append(prob.astype(BF16))
                    qms.append(qm)
                    doms.append(dom)
                dsub[0, pl.ds(row0, SUB), :] = jnp.where(head0, dqs[0], dqs[1])
                dk = _tn(jnp.concatenate(dss, axis=0), jnp.concatenate(qms, axis=0))
                dv = _tn(jnp.concatenate(probs, axis=0), jnp.concatenate(doms, axis=0))
                dsub[1, pl.ds(krow0, nk), :] += dk
                dsub[2, pl.ds(krow0, nk), :] += dv

            def two_blocks(i, carry, block=block):
                block(2 * i)
                block(2 * i + 1)
                return carry

            lax.fori_loop(0, n_blocks // 2, two_blocks, 0)
            for w in range(3):
                cols = slice((3 * w + g) * SLAB, (3 * w + g + 1) * SLAB)
                if dil == 1:
                    dp_ref[:, cols] = dsub[w].astype(BF16)
                else:
                    n = seq // dil
                    for r in range(dil):
                        stage[pl.ds(r, n, stride=dil), :] = dsub[w, pl.ds(r * n, n), :]
                    dp_ref[:, cols] = stage[...].astype(BF16)

    return pl.pallas_call(
        body, name="attn_bwd", grid=(bsz, 2),
        in_specs=[pl.BlockSpec(memory_space=pl.ANY), pl.BlockSpec((seq, PAIR_COLS), lambda b, p: (b, p)),
                  pl.BlockSpec((N_GROUPS, seq, SLAB), lambda b, p: (0, b, p)),
                  pl.BlockSpec((N_GROUPS, seq, SLAB), lambda b, p: (0, b, p)),
                  pl.BlockSpec((seq, SLAB), lambda b, p: (b, p))],
        out_specs=pl.BlockSpec((seq, PAIR_COLS), lambda b, p: (b, p)),
        out_shape=jax.ShapeDtypeStruct(dproj.shape, BF16), input_output_aliases={0: 0},
        scratch_shapes=[pltpu.VMEM((3, seq, SLAB), BF16), pltpu.VMEM((seq, SLAB), F32), pltpu.VMEM((3, seq, SLAB), F32),
                        pltpu.VMEM((3, seq, SLAB), F32), pltpu.VMEM((3, seq, SLAB), F32),
                        pltpu.VMEM((N_GROUPS, 2, 2, SUB, 2 * SUB), F32)],
        compiler_params=_params("arbitrary", "arbitrary"))(dproj, pa, o_all, lse_all, da)


def _grad_h(dproj, w_all, gx0, x2, mod3, seq, lay):
    t, d = x2.shape
    tm, tn = min(512, seq), min(512, d)
    per_seq = seq // tm

    def body(dp_ref, w_ref, gx0_ref, x_ref, scale_ref, gx_ref, dmod_ref):
        dh = _nt(dp_ref[:, 0:ATT], w_ref[:, 0:ATT]) + _nt(dp_ref[:, lay.c0:], w_ref[:, lay.c0:])
        gx_ref[...] = gx0_ref[...] + dh * (1.0 + scale_ref[...])

        @pl.when(pl.program_id(1) % per_seq == 0)
        def _():
            dmod_ref[...] = jnp.zeros_like(dmod_ref)

        dmod_ref[0:1, :] += jnp.sum(dh, axis=0, keepdims=True)
        dmod_ref[1:2, :] += jnp.sum(dh * x_ref[...], axis=0, keepdims=True)

    tile = pl.BlockSpec((tm, tn), lambda j, i: (i, j))
    return pl.pallas_call(
        body, name="grad_h", grid=(d // tn, t // tm),
        in_specs=[pl.BlockSpec((tm, lay.np), lambda j, i: (i, 0)), pl.BlockSpec((tn, lay.np), lambda j, i: (j, 0)), tile, tile,
                  pl.BlockSpec((None, 1, tn), lambda j, i: (i // per_seq, 0, d // tn + j))],
        out_specs=[tile, pl.BlockSpec((None, 8, tn), lambda j, i: (i // per_seq, 0, j))],
        out_shape=[jax.ShapeDtypeStruct((t, d), F32), jax.ShapeDtypeStruct((t // seq, 8, d), F32)],
        compiler_params=_params("arbitrary", "arbitrary"))(dproj, w_all, gx0, x2, mod3)


def _grad_w_in(ht, dproj, seq, lay):
    d, t = ht.shape
    tm = seq
    n_i = t // tm

    def make_body(n_skip, n_pieces):
        def body(*refs):
            refs = refs[n_skip:]
            ht_ref, dp_refs = refs[0], refs[1:1 + n_pieces]
            gw_ref, gb_ref, acc, bacc = refs[1 + n_pieces:]
            i = pl.program_id(1)

            @pl.when(i == 0)
            def _():
                acc[...] = jnp.zeros_like(acc)
                bacc[...] = jnp.zeros_like(bacc)

            dp = dp_refs[0][...] if n_pieces == 1 else jnp.concatenate([r[...] for r in dp_refs], axis=1)
            acc[...] += jnp.dot(ht_ref[...], dp, preferred_element_type=F32)
            bacc[...] += jnp.sum(dp.astype(F32), axis=0, keepdims=True)

            @pl.when(i == n_i - 1)
            def _():
                gw_ref[...] = acc[...].astype(BF16)
                gb_ref[...] = bacc[...]
        return body

    def call(name, pieces, n_tiles, nat_tile, prev):
        tn = sum(w for w, _ in pieces)
        in_specs = [pl.BlockSpec((d, tm), lambda j, i: (0, i))]
        in_specs += [pl.BlockSpec((tm, w), lambda j, i, f=f: (i, f(j))) for w, f in pieces]
        args = [ht] + [dproj] * len(pieces)
        aliases = {}
        if prev is not None:
            in_specs = [pl.BlockSpec(memory_space=pl.ANY)] * 2 + in_specs
            args = list(prev) + args
            aliases = {0: 0, 1: 1}
        return pl.pallas_call(
            make_body(0 if prev is None else 2, len(pieces)), name=name, grid=(n_tiles, n_i), in_specs=in_specs,
            out_specs=[pl.BlockSpec((d, tn), lambda j, i: (0, nat_tile(j))), pl.BlockSpec((1, tn), lambda j, i: (0, nat_tile(j)))],
            out_shape=[jax.ShapeDtypeStruct((d, lay.din), BF16), jax.ShapeDtypeStruct((1, lay.din), F32)],
            input_output_aliases=aliases,
            scratch_shapes=[pltpu.VMEM((d, tn), F32), pltpu.VMEM((1, tn), F32)],
            compiler_params=_params("arbitrary", "arbitrary"))(*args)

    first = call("grad_w_in_attn", [(SLAB, lambda j: j), (SLAB, lambda j: PAIR_SLABS + j)], PAIR_SLABS, lambda j: j, None)
    base = lay.c0 // CONV_TILE
    return call("grad_w_in_rest", [(CONV_TILE, lambda j: base + j)], 6 * d // CONV_TILE, lay.rest_nat_tile, first)


def _pack_rows(parts, width=128):
    flat = [p.reshape(-1) for p in parts]
    spans, rows = [], 0
    padded = []
    for f in flat:
        n = -(-f.shape[0] // (8 * width)) * 8
        padded.append(jnp.pad(f, (0, n * width - f.shape[0])).reshape(n, width))
        spans.append((rows, f.shape[0]))
        rows += n
    return jnp.concatenate(padded, axis=0), spans


def _unpack_rows(packed, spans, shapes, width=128):
    out = []
    for (row, n), shp in zip(spans, shapes):
        rows = -(-n // width)
        out.append(packed[row:row + rows].reshape(-1)[:n].reshape(shp))
    return out


def kernel(x, c, w_ada, b_ada, w_in, b_in, conv_w, w_proj_attn, w_proj_conv, w_out, b_out, ln_g, ln_b, loss_target, m_w_ada, m_b_ada, m_w_in, m_b_in, m_conv_w, m_w_proj_attn, m_w_proj_conv, m_w_out, m_b_out, m_ln_g, m_ln_b, v_w_ada, v_b_ada, v_w_in, v_b_in, v_conv_w, v_w_proj_attn, v_w_proj_conv, v_w_out, v_b_out, v_ln_g, v_ln_b):
    bsz, seq, d = x.shape
    t = bsz * seq
    lay = _Layout(d)
    col_sharded = [True, True, False, False]
    red_w = [w_in[0], w_proj_attn[0], w_proj_conv[0], w_out[0]]
    chip = 2 * lax.axis_index("x") + lax.axis_index("y")
    chip1 = chip.astype(jnp.int32).reshape(1)
    core1 = lax.axis_index("c").astype(jnp.int32).reshape(1)
    place = jnp.stack([chip, lax.axis_index("c")]).astype(jnp.int32)
    x2 = x.reshape(t, d)
    target2 = loss_target.reshape(t, d)

    mod, act_all = _ada_exchange(c, w_ada[0], b_ada)
    mod3 = mod.reshape(bsz, 1, 3 * d)

    cw_pad = jnp.pad(conv_w[0], ((0, 5), (0, 0))) + 0.0 * mod[0, 0]
    own_in_full = [_cast_into_full(red_w[w], col_sharded[w], chip1, f"cast_shard_{w}") for w in range(4)]
    (wi_f,), cw8 = _gather_weights(own_in_full[:1], col_sharded[:1], cw_pad)
    cw_full = cw8[0:3]
    late_copies = _direct_gather_copies(col_sharded[1:])
    late_send, late_recv, late_flying, late_token = _start_copies("gather_late_start", own_in_full[1:], (18,), cw8, late_copies)
    w_all = _permute_w_in(wi_f, lay)
    b_all = lay.perm_vector(b_in) + late_token

    rest_tn = 1024 if (6 * d) % 1024 == 0 else 512
    pa, = _project(x2, mod3, w_all, b_all, seq, 0, ATT, PAIR_COLS, BF16, False, "project_attn")
    pr, ht = _project(x2, mod3, w_all, b_all, seq, lay.c0, 6 * d, rest_tn, BF16, True, "project_rest")
    o_all, lse_all, a_in = _attn_fwd(pa, bsz, seq)
    b_in_act = _conv_fwd(pr, cw_full, bsz, seq, d)
    wpa_f, wpc_f, wo_f = _wait_copies("gather_late_wait", late_flying, late_send, late_recv, b_in_act, late_copies)
    (dproj, da_in, db_in, gx0, dgate, small_tail, gw_pa, gw_pc, gw_out) = _tail(
        a_in, b_in_act, pr, x2, target2, mod3, wpa_f, wpc_f, wo_f, b_out, ln_g, ln_b, seq, lay)

    def mixers_backward(token):
        dp, gcw_ = _conv_bwd(dproj, db_in, pr, cw_full + token, bsz, seq, lay)
        dp = _attn_bwd(dp, pa, o_all, lse_all, da_in, bsz, seq)
        gw_in_bf_, gb_in_ = _grad_w_in(ht, dp, seq, lay)
        return (dp, gcw_, gw_in_bf_, gb_in_), gw_in_bf_

    late_views = _shard_views([gw_pa, gw_pc, gw_out], col_sharded[1:])
    late_got = _pair_exchange_halves(late_views, "grads_pair_exchange_late")
    late_parts = [_pair_sum(late_views[w], late_got[w], core1, f"grads_pair_sum_late_{w}") for w in range(3)]
    late_red, (dproj, gcw, gw_in_bf, gb_in) = _reduce_over_chips(
        late_parts, col_sharded[1:], place, late_parts[0], mixers_backward, "late")

    def input_backward(token):
        gx, dm = _grad_h(dproj, w_all, gx0, x2, mod3 + token, seq, lay)
        return (gx, dm), gx

    in_view = _shard_views([gw_in_bf], col_sharded[:1])
    in_got = _pair_exchange_halves(in_view, "grads_pair_exchange_in")
    in_part = _pair_sum(in_view[0], in_got[0], core1, "grads_pair_sum_in")
    in_red, (grad_x2, dmod) = _reduce_over_chips([in_part], col_sharded[:1], place, late_red[0], input_backward, "in")
    g_red = _pair_join_halves(in_red + late_red)

    d_ada = jnp.concatenate([dmod[:, 0, :], dmod[:, 1, :], dgate[:, 0, :]], axis=1)
    pieces = [small_tail[3], jnp.sum(d_ada, axis=0), gb_in[0], small_tail[0], small_tail[1], small_tail[2], gcw[0:3]]
    packed, spans = _pack_rows(pieces)
    kept_in, _ = _pack_rows([d_ada])
    summed, kept = _all_sum_small(jnp.concatenate([packed, kept_in], axis=0), packed.shape[0], d)
    loss = summed[0, 0]
    _, g_b_ada, g_b_in, g_b_out, g_ln_g, g_ln_b, g_cw_full = _unpack_rows(
        summed, spans, [(d,), (3 * d,), (lay.din,), (d,), (d,), (d,), (3, d)])
    g_cw = lax.dynamic_slice(g_cw_full, (0, chip * (d // N_CHIPS)), (3, d // N_CHIPS))
    d_ada_all = kept.reshape(N_DEV, -1)[:, :bsz * 3 * d].reshape(N_DEV * bsz, 3 * d)
    ada_cols = 3 * d // N_CHIPS
    g_w_ada = _grad_w_ada_cols(act_all.T, lax.dynamic_slice(d_ada_all, (0, chip * ada_cols), (N_DEV * bsz, ada_cols)))

    big_w = [w_ada[0]] + red_w
    big_m = [m_w_ada[0], m_w_in[0], m_w_proj_attn[0], m_w_proj_conv[0], m_w_out[0]]
    big_v = [v_w_ada[0], v_w_in[0], v_w_proj_attn[0], v_w_proj_conv[0], v_w_out[0]]
    g_big = [g_w_ada] + g_red
    big_out = [_adamw(big_w[w], g_big[w], big_m[w], big_v[w], f"adamw_{w}") for w in range(5)]
    small_w = [b_ada, b_in, conv_w[0], b_out, ln_g, ln_b]
    small_g = [g_b_ada, g_b_in, g_cw, g_b_out, g_ln_g, g_ln_b]
    small_m = [m_b_ada, m_b_in, m_conv_w[0], m_b_out, m_ln_g, m_ln_b]
    small_v = [v_b_ada, v_b_in, v_conv_w[0], v_b_out, v_ln_g, v_ln_b]
    pw, sp = _pack_rows(small_w)
    pg, _ = _pack_rows(small_g)
    pm, _ = _pack_rows(small_m)
    pv, _ = _pack_rows(small_v)
    sd, sm, sv = _adamw(pw, pg, pm, pv, "adamw_small")
    shapes = [a.shape for a in small_w]
    sd, sm, sv = _unpack_rows(sd, sp, shapes), _unpack_rows(sm, sp, shapes), _unpack_rows(sv, sp, shapes)

    def order(wa, bA, wi, bI, cw, wpa, wpc, wo, bO, lg, lb):
        return (wa[None], bA, wi[None], bI, cw[None], wpa[None], wpc[None], wo[None], bO, lg, lb)

    sg = [g.reshape(s) for g, s in zip(small_g, shapes)]
    grads_out = order(g_big[0], sg[0], g_big[1], sg[1], sg[2], g_big[2], g_big[3], g_big[4], sg[3], sg[4], sg[5])
    outs = []
    for idx, small in enumerate((sd, sm, sv)):
        outs.append(order(big_out[0][idx], small[0], big_out[1][idx], small[1], small[2], big_out[2][idx],
                          big_out[3][idx], big_out[4][idx], small[3], small[4], small[5]))
    return (loss, grad_x2.reshape(bsz, seq, d), *grads_out, *outs[0], *outs[1], *outs[2])
```

```python
import functools

import jax
import jax.numpy as jnp
from jax import lax
from jax.experimental import pallas as pl
from jax.experimental.pallas import tpu as pltpu

F32 = jnp.float32
BF16 = jnp.bfloat16
MESH = pl.DeviceIdType.MESH

HEAD_DIM = 64
N_GROUPS = 3
DILATIONS = (1, 4, 16)
N_HEADS = 12
SUB = 128
Q_WIDTH = 768
Z_WIDTH = 256
ATT = 3 * Q_WIDTH + Z_WIDTH
SLAB = 128
PAIR_SLABS = 10
PAIR_COLS = PAIR_SLABS * SLAB
CONV_TILE = 256
ALIBI_MAX_EXP = 8.0
ALPHA = 2.0 ** 0.25
LN_EPS = 1e-5
ADAM_LR, ADAM_B1, ADAM_B2, ADAM_EPS, ADAM_WD, ADAM_STEP = 0.001, 0.9, 0.999, 1e-08, 0.01, 10
N_CHIPS = 4
N_DEV = 8
VMEM_LIMIT_V7X = 60 * 1024 * 1024
NEG = -1e30


def _params(*sem):
    return pltpu.CompilerParams(dimension_semantics=sem, vmem_limit_bytes=VMEM_LIMIT_V7X)


def _sigmoid(v):
    return 1.0 / (1.0 + jnp.exp(-v))


class _Layout:
    def __init__(self, d):
        self.d = d
        self.din = ATT + 6 * d
        c0 = 3072
        while c0 % (2 * d):
            c0 += 1024
        self.c0, self.g0, self.np = c0, c0 + 4 * d, c0 + 6 * d
        self.n_conv_tiles = d // CONV_TILE

    def attn_nat_slab(self, s):
        p, i = s // PAIR_SLABS, s % PAIR_SLABS
        return jnp.where(i < 9, (i // 3) * 6 + (i % 3) * 2 + p, 18 + p)

    def rest_nat_tile(self, t):
        n4 = 4 * self.n_conv_tiles
        conv = ATT // CONV_TILE + (t % 4) * self.n_conv_tiles + t // 4
        return jnp.where(t < n4, conv, ATT // CONV_TILE + t)

    def perm_vector(self, v):
        parts = []
        for s in range(2 * PAIR_SLABS):
            p, i = divmod(s, PAIR_SLABS)
            ns = (i // 3) * 6 + (i % 3) * 2 + p if i < 9 else 18 + p
            parts.append(v[:, ns * SLAB:(ns + 1) * SLAB])
        parts.append(jnp.zeros((1, self.c0 - ATT), v.dtype))
        for j in range(self.n_conv_tiles):
            for k in range(4):
                a = ATT + k * self.d + j * CONV_TILE
                parts.append(v[:, a:a + CONV_TILE])
        parts.append(v[:, ATT + 4 * self.d:])
        return jnp.concatenate(parts, axis=1)


def _place():
    return lax.axis_index("x"), lax.axis_index("y"), lax.axis_index("c")


def _other_chips(x, y):
    return [(1 - x, y), (x, 1 - y), (1 - x, 1 - y)]


def _shard_of(ref, col_sharded, chip, half=None):
    if col_sharded:
        cs = ref.shape[1] // N_CHIPS
        cols = pl.ds(pl.multiple_of(chip * cs, SLAB), cs)
        if half is None:
            return ref.at[:, cols]
        n = ref.shape[0] // 2
        return ref.at[pl.ds(half * n, n), cols]
    rs = ref.shape[0] // N_CHIPS
    if half is None:
        return ref.at[pl.ds(chip * rs, rs)]
    return ref.at[pl.ds(chip * rs + half * (rs // 2), rs // 2)]


def _cast_into_full(shard, col_sharded, chip, name):
    rows, cols = shard.shape
    tr = _row_tile(rows, cols)
    nb = rows // tr

    def body(chip_ref, s_ref, o_ref):
        del chip_ref
        o_ref[...] = s_ref[...].astype(BF16)

    if col_sharded:
        full, out_spec = (rows, cols * N_CHIPS), pl.BlockSpec((tr, cols), lambda i, ch: (i, ch[0]))
    else:
        full, out_spec = (rows * N_CHIPS, cols), pl.BlockSpec((tr, cols), lambda i, ch: (ch[0] * nb + i, 0))
    return pl.pallas_call(
        body, name=name,
        grid_spec=pltpu.PrefetchScalarGridSpec(num_scalar_prefetch=1, grid=(nb,),
                                               in_specs=[pl.BlockSpec((tr, cols), lambda i, ch: (i, 0))], out_specs=out_spec),
        out_shape=jax.ShapeDtypeStruct(full, BF16), compiler_params=_params("parallel"))(chip, shard)


def _gather_weights(fulls, col_sharded, small):
    n = len(fulls)

    def body(*refs):
        sm_in, outs, sm_out = refs[n], refs[n + 1:2 * n + 1], refs[2 * n + 1]
        send, recv, fsend, frecv, lsem, ssend, srecv = refs[2 * n + 2:]
        x, y, c = _place()
        mine = 2 * x + y
        sibling = (x, y, 1 - c)
        chips = _other_chips(x, y)

        local = [pltpu.make_async_copy(sm_in, _shard_of(sm_out, True, mine), lsem)]
        for cp in local:
            cp.start()
        sends = []
        for k, (cx, cy) in enumerate(chips):
            cp = pltpu.make_async_remote_copy(src_ref=sm_in, dst_ref=_shard_of(sm_out, True, mine), send_sem=ssend.at[k],
                                              recv_sem=srecv.at[k], device_id=(cx, cy, c), device_id_type=MESH)
            cp.start()
            sends.append(cp)
        for k, (cx, cy) in enumerate(chips):
            for w in range(n):
                own_half = _shard_of(outs[w], col_sharded[w], mine, c)
                cp = pltpu.make_async_remote_copy(
                    src_ref=own_half, dst_ref=own_half,
                    send_sem=send.at[w, k], recv_sem=recv.at[w, k], device_id=(cx, cy, c), device_id_type=MESH)
                cp.start()
                sends.append(cp)
        for k, (cx, cy) in enumerate(chips):
            theirs = 2 * cx + cy
            for w in range(n):
                landed = _shard_of(outs[w], col_sharded[w], theirs, c)
                pltpu.make_async_remote_copy(src_ref=landed, dst_ref=landed, send_sem=send.at[w, k], recv_sem=recv.at[w, k],
                                             device_id=(cx, cy, c), device_id_type=MESH).wait_recv()
                cp = pltpu.make_async_remote_copy(src_ref=landed, dst_ref=landed, send_sem=fsend.at[w, k],
                                                  recv_sem=frecv.at[w, k], device_id=sibling, device_id_type=MESH)
                cp.start()
                sends.append(cp)
        for k, (cx, cy) in enumerate(chips):
            theirs = 2 * cx + cy
            for w in range(n):
                passed = _shard_of(outs[w], col_sharded[w], theirs, 1 - c)
                pltpu.make_async_remote_copy(src_ref=passed, dst_ref=passed, send_sem=fsend.at[w, k], recv_sem=frecv.at[w, k],
                                             device_id=sibling, device_id_type=MESH).wait_recv()
        for k, (cx, cy) in enumerate(chips):
            theirs = _shard_of(sm_out, True, 2 * cx + cy)
            pltpu.make_async_remote_copy(src_ref=theirs, dst_ref=theirs, send_sem=ssend.at[k], recv_sem=srecv.at[k],
                                         device_id=(cx, cy, c), device_id_type=MESH).wait_recv()
        for cp in sends:
            cp.wait_send()
        for cp in local:
            cp.wait()

    any_spec = pl.BlockSpec(memory_space=pl.ANY)
    outs = pl.pallas_call(
        body, name="gather_weights",
        out_shape=[jax.ShapeDtypeStruct(f.shape, BF16) for f in fulls]
        + [jax.ShapeDtypeStruct((small.shape[0], small.shape[1] * N_CHIPS), small.dtype)],
        in_specs=[any_spec] * (n + 1), out_specs=[any_spec] * (n + 1), input_output_aliases={w: w for w in range(n)},
        scratch_shapes=[pltpu.SemaphoreType.DMA((n, 3)), pltpu.SemaphoreType.DMA((n, 3)),
                        pltpu.SemaphoreType.DMA((n, 3)), pltpu.SemaphoreType.DMA((n, 3)), pltpu.SemaphoreType.DMA,
                        pltpu.SemaphoreType.DMA((3,)), pltpu.SemaphoreType.DMA((3,))],
    )(*fulls, small)
    return outs[:n], outs[n]


HBM_SPEC = pl.BlockSpec(memory_space=pltpu.HBM)
SEM_SPEC = pl.BlockSpec(memory_space=pltpu.SEMAPHORE)
DATAFLOW = pltpu.SideEffectType.DATAFLOW_SIDE_EFFECTING


def _start_copies(name, arrays, sem_shape, after, copies):
    n = len(arrays)

    def body(*refs):
        for cp in copies(refs[:n], refs[n + 1], refs[n + 2]):
            cp.start()
        token = refs[2 * n + 3]
        token[...] = jnp.zeros_like(token)

    res = pl.pallas_call(
        body, name=name,
        out_shape=(pltpu.SemaphoreType.DMA(sem_shape), pltpu.SemaphoreType.DMA(sem_shape),
                   *[pltpu.HBM(a.shape, a.dtype) for a in arrays], jax.ShapeDtypeStruct((8, 128), F32)),
        in_specs=[HBM_SPEC] * n + [pl.BlockSpec(memory_space=pl.ANY)],
        out_specs=(SEM_SPEC, SEM_SPEC, *([HBM_SPEC] * n), pl.BlockSpec(memory_space=pltpu.VMEM)),
        input_output_aliases={i: 2 + i for i in range(n)},
        compiler_params=pltpu.CompilerParams(has_side_effects=DATAFLOW),
    )(*[pltpu.with_memory_space_constraint(a, pltpu.HBM) for a in arrays], after)
    return res[0], res[1], list(res[2:2 + n]), res[2 + n][0, 0]


def _wait_copies(name, arrays, send, recv, after, copies):
    n = len(arrays)

    def body(*refs):
        for cp in copies(refs[:n], refs[n], refs[n + 1]):
            cp.wait_send()
            cp.wait_recv()

    return pl.pallas_call(
        body, name=name, out_shape=[pltpu.HBM(a.shape, a.dtype) for a in arrays],
        in_specs=[HBM_SPEC] * n + [SEM_SPEC, SEM_SPEC, pl.BlockSpec(memory_space=pl.ANY)], out_specs=[HBM_SPEC] * n,
        input_output_aliases={i: i for i in range(n)},
        compiler_params=pltpu.CompilerParams(has_side_effects=DATAFLOW),
    )(*arrays, send, recv, after)


def _direct_gather_copies(col_sharded):
    def copies(refs, send, recv):
        x, y, c = _place()
        mine = 2 * x + y
        out = []
        for w, ref in enumerate(refs):
            own_half = _shard_of(ref, col_sharded[w], mine, c)
            k = 0
            for cx, cy in _other_chips(x, y):
                for pc in (c, 1 - c):
                    out.append(pltpu.make_async_remote_copy(
                        src_ref=own_half, dst_ref=own_half, send_sem=send.at[6 * w + k], recv_sem=recv.at[6 * w + k],
                        device_id=(cx, cy, pc), device_id_type=MESH))
                    k += 1
        return out
    return copies


def _chip_scatter_copies(n, col_sharded):
    def piece(ref, cs, chip):
        if cs:
            w = ref.shape[2] // N_CHIPS
            return ref.at[:, :, pl.ds(pl.multiple_of(chip * w, SLAB), w)]
        return ref.at[pl.ds(chip, 1)]

    def copies(refs, send, recv):
        x, y, c = _place()
        out = []
        for k, (cx, cy) in enumerate(_other_chips(x, y)):
            for w in range(n):
                out.append(pltpu.make_async_remote_copy(
                    src_ref=piece(refs[w], col_sharded[w], 2 * cx + cy), dst_ref=refs[n + w].at[pl.ds(k, 1)],
                    send_sem=send.at[3 * w + k], recv_sem=recv.at[3 * w + k], device_id=(cx, cy, c), device_id_type=MESH))
        return out
    return copies


def _shard_views(gs, col_sharded):
    return [g.reshape(1, *g.shape) if cs else g.reshape(N_CHIPS, g.shape[0] // N_CHIPS, g.shape[1])
            for g, cs in zip(gs, col_sharded)]


DMA_CHUNK_BYTES = 1 << 20


def _chunk_rows(shape, itemsize):
    s, rows, cols = shape
    n = 1
    while s * (rows // n) * cols * itemsize > DMA_CHUNK_BYTES and (rows // n) % 32 == 0:
        n *= 2
    return rows // n


def _send_rows(src, src_row0, dst, dst_row0, rows, send_sem, recv_sem, device):
    step = _chunk_rows((src.shape[0], rows, src.shape[2]), src.dtype.itemsize)
    for r in range(0, rows, step):
        pltpu.make_async_remote_copy(src_ref=src.at[:, pl.ds(src_row0 + r, step)], dst_ref=dst.at[:, pl.ds(dst_row0 + r, step)],
                                     send_sem=send_sem, recv_sem=recv_sem, device_id=device, device_id_type=MESH).start()


def _pair_exchange_halves(views, name):
    n = len(views)
    half_shapes = [(v.shape[0], v.shape[1] // 2, v.shape[2]) for v in views]

    def body(*refs):
        ins, got = refs[:n], refs[n:2 * n]
        send, recv = refs[2 * n:]
        x, y, c = _place()
        sibling = (x, y, 1 - c)
        for w in range(n):
            hr = half_shapes[w][1]
            _send_rows(ins[w], (1 - c) * hr, got[w], 0, hr, send.at[w], recv.at[w], sibling)
        for w in range(n):
            hr = half_shapes[w][1]
            pltpu.make_async_remote_copy(src_ref=ins[w].at[:, pl.ds((1 - c) * hr, hr)], dst_ref=got[w], send_sem=send.at[w],
                                         recv_sem=recv.at[w], device_id=sibling, device_id_type=MESH).wait()

    any_spec = pl.BlockSpec(memory_space=pl.ANY)
    return pl.pallas_call(
        body, name=name,
        out_shape=[jax.ShapeDtypeStruct(s, v.dtype) for s, v in zip(half_shapes, views)],
        in_specs=[any_spec] * n, out_specs=[any_spec] * n,
        scratch_shapes=[pltpu.SemaphoreType.DMA((n,)), pltpu.SemaphoreType.DMA((n,))],
    )(*views)


def _pair_sum(view, got, core, name):
    s, r, cols = view.shape
    hr = r // 2
    tr = _row_tile(hr, cols)
    nb = hr // tr

    def body(core_ref, a_ref, b_ref, o_ref):
        del core_ref
        o_ref[...] = (a_ref[...].astype(F32) + b_ref[...].astype(F32)).astype(BF16)

    same = pl.BlockSpec((None, tr, cols), lambda j, i, core_ref: (j, i, 0))
    return pl.pallas_call(
        body, name=name,
        grid_spec=pltpu.PrefetchScalarGridSpec(
            num_scalar_prefetch=1, grid=(s, nb),
            in_specs=[pl.BlockSpec((None, tr, cols), lambda j, i, core_ref: (j, core_ref[0] * nb + i, 0)), same],
            out_specs=same),
        out_shape=jax.ShapeDtypeStruct((s, hr, cols), BF16), compiler_params=_params("parallel", "parallel"))(core, view, got)


def _piece_cols(part, col_sharded):
    return part.shape[2] // N_CHIPS if col_sharded else part.shape[2]


def _reduce_over_chips(parts, col_sharded, place, after, between, tag):
    n = len(parts)
    lands = [lax.empty((3, p.shape[1], _piece_cols(p, cs)), p.dtype) for p, cs in zip(parts, col_sharded)]
    copies = _chip_scatter_copies(n, col_sharded)
    send, recv, flying, token = _start_copies(f"grads_scatter_{tag}_start", list(parts) + lands, (3 * n,), after, copies)
    done, marker = between(token)
    landed = _wait_copies(f"grads_scatter_{tag}_wait", flying, send, recv, marker, copies)
    reduced = [_chip_sum(landed[w], landed[n + w], col_sharded[w], place, f"grads_chip_sum_{tag}_{w}") for w in range(n)]
    return reduced, done


def _chip_sum(part, got, col_sharded, place, name):
    _, hr, _ = part.shape
    cols = _piece_cols(part, col_sharded)
    tr = _row_tile(hr, cols)
    nb = hr // tr

    def body(place_ref, own_ref, g0_ref, g1_ref, g2_ref, o_ref):
        del place_ref
        acc = own_ref[...].astype(F32) + g0_ref[...].astype(F32)
        o_ref[...] = acc + g1_ref[...].astype(F32) + g2_ref[...].astype(F32)

    if col_sharded:
        own = pl.BlockSpec((None, tr, cols), lambda i, pr: (0, i, pr[0]))
    else:
        own = pl.BlockSpec((None, tr, cols), lambda i, pr: (pr[0], i, 0))
    others = [pl.BlockSpec((None, tr, cols), lambda i, pr, k=k: (k, i, 0)) for k in range(3)]
    return pl.pallas_call(
        body, name=name,
        grid_spec=pltpu.PrefetchScalarGridSpec(
            num_scalar_prefetch=1, grid=(nb,), in_specs=[own] + others,
            out_specs=pl.BlockSpec((tr, cols), lambda i, pr: (pr[1] * nb + i, 0))),
        out_shape=jax.ShapeDtypeStruct((2 * hr, cols), F32), compiler_params=_params("parallel"))(place, part, got, got, got)


def _pair_join_halves(fulls):
    n = len(fulls)
    views = [f.reshape(1, *f.shape) for f in fulls]

    def body(*refs):
        outs = refs[n:2 * n]
        send, recv = refs[2 * n:]
        x, y, c = _place()
        sibling = (x, y, 1 - c)
        for w in range(n):
            hr = outs[w].shape[1] // 2
            _send_rows(outs[w], c * hr, outs[w], c * hr, hr, send.at[w], recv.at[w], sibling)
        for w in range(n):
            hr = outs[w].shape[1] // 2
            pltpu.make_async_remote_copy(
                src_ref=outs[w].at[:, pl.ds(c * hr, hr)], dst_ref=outs[w].at[:, pl.ds((1 - c) * hr, hr)], send_sem=send.at[w],
                recv_sem=recv.at[w], device_id=sibling, device_id_type=MESH).wait()

    any_spec = pl.BlockSpec(memory_space=pl.ANY)
    outs = pl.pallas_call(
        body, name="grads_pair_join",
        out_shape=[jax.ShapeDtypeStruct(v.shape, v.dtype) for v in views],
        in_specs=[any_spec] * n, out_specs=[any_spec] * n, input_output_aliases={w: w for w in range(n)},
        scratch_shapes=[pltpu.SemaphoreType.DMA((n,)), pltpu.SemaphoreType.DMA((n,))],
    )(*views)
    return [o[0] for o in outs]


def _all_sum_small(vec, n_sum, d):
    rows = vec.shape[0]

    def body(v_ref, sum_ref, kept_ref, all_ref, send, recv):
        x, y, c = _place()
        me = 4 * x + 2 * y + c
        all_ref[me] = v_ref[...]
        peers = _all_devices(x, y, c)
        copies = []
        for k, (peer, _) in enumerate(peers):
            cp = pltpu.make_async_remote_copy(src_ref=v_ref, dst_ref=all_ref.at[me], send_sem=send.at[k], recv_sem=recv.at[k],
                                              device_id=peer, device_id_type=MESH)
            cp.start()
            copies.append(cp)
        for k, (_, src) in enumerate(peers):
            pltpu.make_async_remote_copy(src_ref=v_ref, dst_ref=all_ref.at[src], send_sem=send.at[k], recv_sem=recv.at[k],
                                         device_id=(x, y, c), device_id_type=MESH).wait_recv()
        for cp in copies:
            cp.wait_send()
        total = all_ref[0, 0:n_sum, :]
        for i in range(1, N_DEV):
            total = total + all_ref[i, 0:n_sum, :]
        sum_ref[...] = total
        loss = 0.5 / d * jnp.sum(total[0:8, :])
        sum_ref[0:8, :] = jnp.full((8, 128), loss, F32)
        for i in range(N_DEV):
            kept_ref[i] = all_ref[i, n_sum:rows, :]

    vm = pl.BlockSpec(memory_space=pltpu.VMEM)
    return pl.pallas_call(
        body, name="all_sum_small",
        out_shape=[jax.ShapeDtypeStruct((n_sum, 128), F32), jax.ShapeDtypeStruct((N_DEV, rows - n_sum, 128), F32)],
        in_specs=[vm], out_specs=[vm, vm],
        scratch_shapes=[pltpu.VMEM((N_DEV, rows, 128), F32), pltpu.SemaphoreType.DMA((N_DEV - 1,)),
                        pltpu.SemaphoreType.DMA((N_DEV - 1,))],
        compiler_params=pltpu.CompilerParams(vmem_limit_bytes=VMEM_LIMIT_V7X),
    )(vec)


def _row_tile(rows, cols, itemsize=4, budget=2 << 20):
    t = rows
    while t * cols * itemsize > budget and t % 16 == 0:
        t //= 2
    return t


def _adamw(w, g, m, v, name):
    rows, cols = w.shape
    tr = _row_tile(rows, cols, budget=1 << 20)

    def body(w_ref, g_ref, m_ref, v_ref, d_ref, nm_ref, nv_ref):
        g_ = g_ref[...]
        nm = ADAM_B1 * m_ref[...] + (1.0 - ADAM_B1) * g_
        nv = ADAM_B2 * v_ref[...] + (1.0 - ADAM_B2) * (g_ * g_)
        m_hat = nm / (1.0 - ADAM_B1 ** ADAM_STEP)
        v_hat = nv / (1.0 - ADAM_B2 ** ADAM_STEP)
        d_ref[...] = -ADAM_LR * (m_hat / (jnp.sqrt(v_hat) + ADAM_EPS) + ADAM_WD * w_ref[...])
        nm_ref[...] = nm
        nv_ref[...] = nv

    spec = pl.BlockSpec((tr, cols), lambda i: (i, 0))
    shp = jax.ShapeDtypeStruct((rows, cols), F32)
    return pl.pallas_call(body, name=name, grid=(rows // tr,), in_specs=[spec] * 4, out_specs=[spec] * 3,
                          out_shape=[shp] * 3, compiler_params=_params("parallel"))(w, g, m, v)


def _all_devices(x, y, c):
    out = []
    for k in range(1, N_DEV):
        peer = (x ^ ((k >> 2) & 1), y ^ ((k >> 1) & 1), c ^ (k & 1))
        out.append((peer, 4 * peer[0] + 2 * peer[1] + peer[2]))
    return out


def _ada_exchange(c, w_shard, b_ada):
    bsz, d = c.shape
    cs = w_shard.shape[1]

    def body(c_ref, w_ref, b_ref, mod_ref, act_ref, c_all, part, pieces, csend, crecv, psend, precv):
        x, y, core = _place()
        me = 4 * x + 2 * y + core
        chip = 2 * x + y
        c_all[me] = c_ref[...]
        peers = _all_devices(x, y, core)
        copies = []
        for k, (peer, _) in enumerate(peers):
            cp = pltpu.make_async_remote_copy(src_ref=c_ref, dst_ref=c_all.at[me], send_sem=csend.at[k], recv_sem=crecv.at[k],
                                              device_id=peer, device_id_type=MESH)
            cp.start()
            copies.append(cp)
        for k, (_, src) in enumerate(peers):
            pltpu.make_async_remote_copy(src_ref=c_ref, dst_ref=c_all.at[src], send_sem=csend.at[k], recv_sem=crecv.at[k],
                                         device_id=(x, y, core), device_id_type=MESH).wait_recv()
        rows = jnp.concatenate([c_all[i] for i in range(N_DEV)], axis=0)
        act = rows * _sigmoid(rows)
        act_ref[...] = act
        prod = jnp.dot(act.astype(BF16), w_ref[...].astype(BF16), preferred_element_type=F32)
        for i in range(N_DEV):
            part[i] = prod[i * bsz:(i + 1) * bsz, :]
        pieces[chip] = part[me]
        chips = _other_chips(x, y)
        for k, (cx, cy) in enumerate(chips):
            cp = pltpu.make_async_remote_copy(src_ref=part.at[4 * cx + 2 * cy + core], dst_ref=pieces.at[chip],
                                              send_sem=psend.at[k], recv_sem=precv.at[k], device_id=(cx, cy, core),
                                              device_id_type=MESH)
            cp.start()
            copies.append(cp)
        for k, (cx, cy) in enumerate(chips):
            pltpu.make_async_remote_copy(src_ref=part.at[me], dst_ref=pieces.at[2 * cx + cy], send_sem=psend.at[k],
                                         recv_sem=precv.at[k], device_id=(cx, cy, core), device_id_type=MESH).wait_recv()
        for cp in copies:
            cp.wait_send()
        mod_ref[...] = jnp.concatenate([pieces[j] for j in range(N_CHIPS)], axis=1) + b_ref[...]

    vm = pl.BlockSpec(memory_space=pltpu.VMEM)
    return pl.pallas_call(
        body, name="ada_exchange", in_specs=[vm] * 3, out_specs=[vm] * 2,
        out_shape=[jax.ShapeDtypeStruct((bsz, 3 * d), F32), jax.ShapeDtypeStruct((N_DEV * bsz, d), F32)],
        scratch_shapes=[pltpu.VMEM((N_DEV, bsz, d), F32), pltpu.VMEM((N_DEV, bsz, cs), F32), pltpu.VMEM((N_CHIPS, bsz, cs), F32),
                        pltpu.SemaphoreType.DMA((N_DEV - 1,)), pltpu.SemaphoreType.DMA((N_DEV - 1,)),
                        pltpu.SemaphoreType.DMA((3,)), pltpu.SemaphoreType.DMA((3,))],
        compiler_params=pltpu.CompilerParams(vmem_limit_bytes=VMEM_LIMIT_V7X))(c, w_shard, b_ada)


def _grad_w_ada_cols(act_t, d_cols):
    d, n = act_t.shape
    cs = d_cols.shape[1]

    def body(a_ref, g_ref, o_ref):
        a, g = a_ref[...], g_ref[...]
        acc = a[:, 0:1] * g[0:1, :]
        for b in range(1, n):
            acc = acc + a[:, b:b + 1] * g[b:b + 1, :]
        o_ref[...] = acc

    vm = pl.BlockSpec(memory_space=pltpu.VMEM)
    return pl.pallas_call(body, name="grad_w_ada", in_specs=[vm] * 2, out_specs=vm,
                          out_shape=jax.ShapeDtypeStruct((d, cs), F32),
                          compiler_params=pltpu.CompilerParams(vmem_limit_bytes=VMEM_LIMIT_V7X))(act_t, d_cols)


def _permute_w_in(w_nat, lay):
    d = lay.d

    def copy_body(src_ref, dst_in_ref, dst_ref):
        del dst_in_ref
        dst_ref[...] = src_ref[...]

    def first_body(src_ref, dst_ref):
        dst_ref[...] = src_ref[...]

    w_all = pl.pallas_call(
        first_body, name="permute_w_attn", grid=(2 * PAIR_SLABS,),
        in_specs=[pl.BlockSpec((d, SLAB), lambda s: (0, lay.attn_nat_slab(s)))],
        out_specs=pl.BlockSpec((d, SLAB), lambda s: (0, s)),
        out_shape=jax.ShapeDtypeStruct((d, lay.np), BF16), compiler_params=_params("arbitrary"))(w_nat)
    n_rest = 6 * d // CONV_TILE
    base = lay.c0 // CONV_TILE
    return pl.pallas_call(
        copy_body, name="permute_w_rest", grid=(n_rest,),
        in_specs=[pl.BlockSpec((d, CONV_TILE), lambda t: (0, lay.rest_nat_tile(t))), pl.BlockSpec(memory_space=pl.ANY)],
        out_specs=pl.BlockSpec((d, CONV_TILE), lambda t: (0, base + t)),
        out_shape=jax.ShapeDtypeStruct((d, lay.np), BF16), input_output_aliases={1: 0},
        compiler_params=_params("arbitrary"))(w_nat, w_all)


def _project(x2, mod3, w_all, b_all, seq, col0, ncols, tn, out_dtype, want_ht, name):
    t, d = x2.shape
    tm = min(1024, seq)
    per_seq = seq // tm
    j0 = col0 // tn

    def body(x_ref, mod_ref, w_ref, b_ref, o_ref, *rest):
        h_ref = rest[-1]

        @pl.when(pl.program_id(1) == 0)
        def _():
            h = x_ref[...] * (1.0 + mod_ref[:, d:2 * d]) + mod_ref[:, 0:d]
            h_ref[...] = h.astype(BF16)
            if want_ht:
                rest[0][...] = h.T.astype(BF16)

        o_ref[...] = (jnp.dot(h_ref[...], w_ref[...], preferred_element_type=F32) + b_ref[...]).astype(out_dtype)

    out_shape = [jax.ShapeDtypeStruct((t, ncols), out_dtype)]
    out_specs = [pl.BlockSpec((tm, tn), lambda i, j: (i, j))]
    if want_ht:
        out_shape.append(jax.ShapeDtypeStruct((d, t), BF16))
        out_specs.append(pl.BlockSpec((d, tm), lambda i, j: (0, i)))
    return pl.pallas_call(
        body, name=name, grid=(t // tm, ncols // tn),
        in_specs=[pl.BlockSpec((tm, d), lambda i, j: (i, 0)),
                  pl.BlockSpec((None, 1, 3 * d), lambda i, j: (i // per_seq, 0, 0)),
                  pl.BlockSpec((d, tn), lambda i, j: (0, j0 + j)),
                  pl.BlockSpec((1, tn), lambda i, j: (0, j0 + j))],
        out_specs=out_specs, out_shape=out_shape,
        scratch_shapes=[pltpu.VMEM((tm, d), BF16)],
        compiler_params=_params("arbitrary", "arbitrary"))(x2, mod3, w_all, b_all)


def _slope(g, p, hh):
    head = 4 * g + 2 * p + hh
    return 2.0 ** (-ALIBI_MAX_EXP * (head + 1.0) / N_HEADS)


def _ld_rows(ref, start, n, stride):
    if stride == 1:
        return ref[pl.ds(start, n), :]
    return ref[pl.ds(start, n, stride=stride), :]


def _st_rows(ref, start, n, stride, val):
    if stride == 1:
        ref[pl.ds(start, n), :] = val
    else:
        ref[pl.ds(start, n, stride=stride), :] = val


def _sub_blocks(g, seq):
    return seq // DILATIONS[g] // SUB


def _key_rows(g, seq):
    return SUB if _sub_blocks(g, seq) == 1 else 2 * SUB


def _fill_bias(bias_ref, p, seq):
    for g in range(N_GROUPS):
        nk = _key_rows(g, seq)
        diff = lax.broadcasted_iota(jnp.int32, (SUB, nk), 0) - lax.broadcasted_iota(jnp.int32, (SUB, nk), 1)
        for i, off in enumerate((0, SUB)):
            if i == 1 and nk == SUB:
                continue
            delta = diff + off
            ok = (delta >= 0) & (delta <= SUB)
            dist = (delta * DILATIONS[g]).astype(F32)
            for hh in range(2):
                slope = jnp.where(p == 0, _slope(g, 0, hh), _slope(g, 1, hh))
                bias_ref[g, i, hh, :, 0:nk] = jnp.where(ok, -slope * dist, NEG)


def _to_sub_major(pa_ref, col, sub_ref, stage, dil, seq):
    cols = slice(col * SLAB, (col + 1) * SLAB)
    if dil == 1:
        sub_ref[...] = pa_ref[:, cols]
        return
    n = seq // dil
    stage[...] = pa_ref[:, cols].astype(F32)
    for r in range(dil):
        sub_ref[pl.ds(r * n, n), :] = stage[pl.ds(r, n, stride=dil), :].astype(BF16)


def _block_rows(it, g, seq):
    dil, nb = DILATIONS[g], _sub_blocks(g, seq)
    row0 = pl.multiple_of(it * SUB, SUB)
    if nb == 1:
        return row0, row0, 0, it
    blk = it % nb
    first = blk == 0
    krow0 = pl.multiple_of(row0 - jnp.where(first, 0, SUB), SUB)
    nat = row0 if dil == 1 else it // nb + dil * SUB * blk
    return row0, krow0, jnp.where(first, 0, 1), nat


def _nt(a, b):
    return lax.dot_general(a, b, (((1,), (1,)), ((), ())), preferred_element_type=F32)


def _tn(a, b):
    return lax.dot_general(a, b, (((0,), (0,)), ((), ())), preferred_element_type=F32)


def _head_sums(t):
    rows = t.shape[0]
    lo = jnp.broadcast_to(jnp.sum(t[:, :HEAD_DIM], axis=-1, keepdims=True), (rows, HEAD_DIM))
    hi = jnp.broadcast_to(jnp.sum(t[:, HEAD_DIM:], axis=-1, keepdims=True), (rows, HEAD_DIM))
    return jnp.concatenate([lo, hi], axis=-1)


def _attn_fwd(pa, bsz, seq):
    t = pa.shape[0]
    n_blocks = seq // SUB
    chunk = 256

    def body(pa_ref, o_ref, lse_ref, a_ref, sub, stage, bias_ref):
        p = pl.program_id(1)
        _fill_bias(bias_ref, p, seq)
        head0 = lax.broadcasted_iota(jnp.int32, (SUB, SLAB), 1) < HEAD_DIM
        for g in range(N_GROUPS):
            dil = DILATIONS[g]
            for w in range(3):
                _to_sub_major(pa_ref, 3 * w + g, sub.at[w], stage, dil, seq)
            nk = _key_rows(g, seq)

            def block(it, g=g, dil=dil, nk=nk):
                row0, krow0, bi, nat = _block_rows(it, g, seq)
                q = sub[0, pl.ds(row0, SUB), :]
                kw = sub[1, pl.ds(krow0, nk), :]
                vw = sub[2, pl.ds(krow0, nk), :]
                outs, lses = [], []
                for hh in range(2):
                    qm = jnp.where(head0 if hh == 0 else ~head0, q, jnp.zeros_like(q)) * (HEAD_DIM ** -0.5)
                    s = _nt(qm, kw) + bias_ref[g, bi, hh, :, 0:nk]
                    m = jnp.max(s, axis=-1, keepdims=True)
                    e = jnp.exp(s - m)
                    den = jnp.sum(e, axis=-1, keepdims=True)
                    outs.append(jnp.dot((e * (1.0 / den)).astype(BF16), vw, preferred_element_type=F32))
                    lses.append(m + jnp.log(den))
                _st_rows(o_ref.at[g], nat, SUB, dil, jnp.where(head0, outs[0], outs[1]))
                _st_rows(lse_ref.at[g], nat, SUB, dil, jnp.where(head0, lses[0], lses[1]))

            def two_blocks(i, carry, block=block):
                block(2 * i)
                block(2 * i + 1)
                return carry

            lax.fori_loop(0, n_blocks // 2, two_blocks, 0)

        def mix(i, carry):
            rows = pl.ds(pl.multiple_of(i * chunk, chunk), chunk)
            l0, l1, l2 = lse_ref[0, rows, :], lse_ref[1, rows, :], lse_ref[2, rows, :]
            m = jnp.maximum(jnp.maximum(l0, l1), l2)
            e0, e1, e2 = jnp.exp(l0 - m), jnp.exp(l1 - m), jnp.exp(l2 - m)
            tot = e0 + e1 + e2
            o = (e0 / tot) * o_ref[0, rows, :] + (e1 / tot) * o_ref[1, rows, :] + (e2 / tot) * o_ref[2, rows, :]
            z = pa_ref[rows, 9 * SLAB:10 * SLAB].astype(F32)
            a_ref[rows, :] = (o * (z * _sigmoid(z))).astype(BF16)
            return carry

        lax.fori_loop(0, seq // chunk, mix, 0)

    big = jax.ShapeDtypeStruct((N_GROUPS, t, 2 * SLAB), F32)
    return pl.pallas_call(
        body, name="attn_fwd", grid=(bsz, 2),
        in_specs=[pl.BlockSpec((seq, PAIR_COLS), lambda b, p: (b, p))],
        out_specs=[pl.BlockSpec((N_GROUPS, seq, SLAB), lambda b, p: (0, b, p)),
                   pl.BlockSpec((N_GROUPS, seq, SLAB), lambda b, p: (0, b, p)),
                   pl.BlockSpec((seq, SLAB), lambda b, p: (b, p))],
        out_shape=[big, big, jax.ShapeDtypeStruct((t, 2 * SLAB), BF16)],
        scratch_shapes=[pltpu.VMEM((3, seq, SLAB), BF16), pltpu.VMEM((seq, SLAB), F32),
                        pltpu.VMEM((N_GROUPS, 2, 2, SUB, 2 * SUB), F32)],
        compiler_params=_params("arbitrary", "arbitrary"))(pa)


def _shift_down(v, k):
    rows = lax.broadcasted_iota(jnp.int32, v.shape, 0)
    return jnp.where(rows >= k, pltpu.roll(v, k, 0), 0.0)


def _shift_up(v, k):
    n = v.shape[0]
    rows = lax.broadcasted_iota(jnp.int32, v.shape, 0)
    return jnp.where(rows < n - k, pltpu.roll(v, n - k, 0), 0.0)


def _conv_fwd(pr, conv_w, bsz, seq, d):
    t = pr.shape[0]
    ct = CONV_TILE

    def body(p_ref, cw_ref, o_ref):
        u = p_ref[:, 2 * ct:3 * ct].astype(F32) * p_ref[:, 0:ct].astype(F32)
        cw = cw_ref[...]
        conv = cw[0:1, :] * _shift_down(u, 2)
        conv = conv + cw[1:2, :] * _shift_down(u, 1)
        conv = conv + cw[2:3, :] * u
        z = p_ref[:, 3 * ct:4 * ct].astype(F32)
        o_ref[...] = (p_ref[:, ct:2 * ct].astype(F32) * conv * (z * _sigmoid(z))).astype(BF16)

    return pl.pallas_call(
        body, name="conv_fwd", grid=(bsz, d // ct),
        in_specs=[pl.BlockSpec((seq, 4 * ct), lambda b, j: (b, j)), pl.BlockSpec((3, ct), lambda b, j: (0, j))],
        out_specs=pl.BlockSpec((seq, ct), lambda b, j: (b, j)),
        out_shape=jax.ShapeDtypeStruct((t, d), BF16), compiler_params=_params("parallel", "parallel"))(pr, conv_w)


def _tail(a_in, b_in, pr, x2, target2, mod3, w_pa, w_pc, w_out, b_out, ln_g, ln_b, seq, lay):
    t, d = x2.shape
    tm = 256
    per_seq = seq // tm
    n_steps = t // tm
    gate_blk = 4 * d // d

    def nt(a, b):
        return lax.dot_general(a, b, (((1,), (1,)), ((), ())), preferred_element_type=F32)

    def tn(a, b):
        return lax.dot_general(a, b, (((0,), (0,)), ((), ())), preferred_element_type=F32)

    def body(a_ref, b_ref, ga_ref, gb_ref, x_ref, tg_ref, mod_ref, wpa_ref, wpc_ref, wo_ref, bo_ref, lg_ref, lb_ref,
             dpg_ref, da_ref, db_ref, gx_ref, dgate_ref, small_ref, gwpa_hbm, gwpc_hbm, gwo_hbm,
             acc_pa, acc_pc, acc_o, sem):
        i = pl.program_id(0)

        @pl.when(i == 0)
        def _():
            acc_pa[...] = jnp.zeros_like(acc_pa)
            acc_pc[...] = jnp.zeros_like(acc_pc)
            acc_o[...] = jnp.zeros_like(acc_o)
            small_ref[...] = jnp.zeros_like(small_ref)

        @pl.when(i % per_seq == 0)
        def _():
            dgate_ref[...] = jnp.zeros_like(dgate_ref)

        a_bf, b_bf = a_ref[...], b_ref[...]
        y_attn = jnp.dot(a_bf, wpa_ref[...], preferred_element_type=F32)
        y_conv = jnp.dot(b_bf, wpc_ref[...], preferred_element_type=F32)
        sa, sb = _sigmoid(ga_ref[...].astype(F32)), _sigmoid(gb_ref[...].astype(F32))
        merged = (sa * y_attn + sb * y_conv).astype(BF16)
        mo = jnp.dot(merged, wo_ref[...], preferred_element_type=F32) + bo_ref[...]
        gate = mod_ref[:, 2 * d:3 * d]
        r = ALPHA * x_ref[...] + gate * mo
        mu = jnp.mean(r, axis=-1, keepdims=True)
        cen = r - mu
        var = jnp.mean(cen * cen, axis=-1, keepdims=True)
        rstd = lax.rsqrt(var + LN_EPS)
        xhat = cen * rstd
        err = xhat * lg_ref[...] + lb_ref[...] - tg_ref[...]
        dy = err * (1.0 / d)
        dxhat = dy * lg_ref[...]
        dr = rstd * (dxhat - jnp.mean(dxhat, axis=-1, keepdims=True) - xhat * jnp.mean(dxhat * xhat, axis=-1, keepdims=True))
        gx_ref[...] = ALPHA * dr
        dgate_ref[...] += jnp.sum(dr * mo, axis=0, keepdims=True)
        d_mo = dr * gate
        small_ref[0:1, :] += jnp.sum(d_mo, axis=0, keepdims=True)
        small_ref[1:2, :] += jnp.sum(dy * xhat, axis=0, keepdims=True)
        small_ref[2:3, :] += jnp.sum(dy, axis=0, keepdims=True)
        small_ref[3:4, :] += jnp.sum(err * err, axis=0, keepdims=True)
        d_mo_bf = d_mo.astype(BF16)
        acc_o[...] += tn(merged, d_mo_bf)
        dmerged = nt(d_mo_bf, wo_ref[...])
        dy_attn = (dmerged * sa).astype(BF16)
        dy_conv = (dmerged * sb).astype(BF16)
        dpg_ref[:, 0:d] = (dmerged * y_attn * sa * (1.0 - sa)).astype(BF16)
        dpg_ref[:, d:2 * d] = (dmerged * y_conv * sb * (1.0 - sb)).astype(BF16)
        acc_pa[...] += tn(a_bf, dy_attn)
        acc_pc[...] += tn(b_bf, dy_conv)
        da_ref[...] = nt(dy_attn, wpa_ref[...])
        db_ref[...] = nt(dy_conv, wpc_ref[...])

        @pl.when(i == n_steps - 1)
        def _():
            copies = [pltpu.make_async_copy(acc_pa, gwpa_hbm, sem.at[0]), pltpu.make_async_copy(acc_pc, gwpc_hbm, sem.at[1]),
                      pltpu.make_async_copy(acc_o, gwo_hbm, sem.at[2])]
            for cp in copies:
                cp.start()
            for cp in copies:
                cp.wait()

    row = lambda w: pl.BlockSpec((tm, w), lambda i: (i, 0))
    const = lambda shp: pl.BlockSpec(shp, lambda i: (0,) * len(shp), pipeline_mode=pl.Buffered(1))
    any_spec = pl.BlockSpec(memory_space=pl.ANY)
    return pl.pallas_call(
        body, name="tail", grid=(n_steps,),
        in_specs=[row(Z_WIDTH), row(d),
                  pl.BlockSpec((tm, d), lambda i: (i, gate_blk)), pl.BlockSpec((tm, d), lambda i: (i, gate_blk + 1)),
                  row(d), row(d), pl.BlockSpec((None, 1, 3 * d), lambda i: (i // per_seq, 0, 0)),
                  const((Z_WIDTH, d)), const((d, d)), const((d, d)), const((1, d)), const((1, d)), const((1, d))],
        out_specs=[pl.BlockSpec((tm, 2 * d), lambda i: (i, lay.g0 // (2 * d))), row(Z_WIDTH), row(d), row(d),
                   pl.BlockSpec((None, 1, d), lambda i: (i // per_seq, 0, 0)), pl.BlockSpec((8, d), lambda i: (0, 0)),
                   any_spec, any_spec, any_spec],
        out_shape=[jax.ShapeDtypeStruct((t, lay.np), BF16), jax.ShapeDtypeStruct((t, Z_WIDTH), F32),
                   jax.ShapeDtypeStruct((t, d), F32), jax.ShapeDtypeStruct((t, d), F32),
                   jax.ShapeDtypeStruct((t // seq, 1, d), F32), jax.ShapeDtypeStruct((8, d), F32),
                   jax.ShapeDtypeStruct((Z_WIDTH, d), F32), jax.ShapeDtypeStruct((d, d), F32),
                   jax.ShapeDtypeStruct((d, d), F32)],
        scratch_shapes=[pltpu.VMEM((Z_WIDTH, d), F32), pltpu.VMEM((d, d), F32), pltpu.VMEM((d, d), F32),
                        pltpu.SemaphoreType.DMA((3,))],
        compiler_params=_params("arbitrary"),
    )(a_in, b_in, pr, pr, x2, target2, mod3, w_pa, w_pc, w_out, b_out, ln_g, ln_b)


def _conv_bwd(dproj, db, pr, conv_w, bsz, seq, lay):
    d = lay.d
    ct = CONV_TILE
    base = lay.c0 // (4 * ct)

    def body(dp_in, db_ref, p_ref, cw_ref, dp_ref, gcw_ref):
        del dp_in
        u_x, g_b, g_c, z = [p_ref[:, k * ct:(k + 1) * ct].astype(F32) for k in range(4)]
        cw = cw_ref[...]
        u = g_c * u_x
        u1, u2 = _shift_down(u, 1), _shift_down(u, 2)
        conv = cw[0:1, :] * u2 + cw[1:2, :] * u1 + cw[2:3, :] * u
        sig = _sigmoid(z)
        sl = z * sig
        dbv = db_ref[...]
        gbc = g_b * conv
        dp_ref[:, ct:2 * ct] = (dbv * sl * conv).astype(BF16)
        dp_ref[:, 3 * ct:4 * ct] = (dbv * gbc * (sig * (1.0 + z * (1.0 - sig)))).astype(BF16)
        dconv = dbv * sl * g_b

        @pl.when(pl.program_id(1) == 0)
        def _():
            gcw_ref[...] = jnp.zeros_like(gcw_ref)

        gcw_ref[0:1, :] += jnp.sum(dconv * u2, axis=0, keepdims=True)
        gcw_ref[1:2, :] += jnp.sum(dconv * u1, axis=0, keepdims=True)
        gcw_ref[2:3, :] += jnp.sum(dconv * u, axis=0, keepdims=True)
        du = cw[2:3, :] * dconv + cw[1:2, :] * _shift_up(dconv, 1) + cw[0:1, :] * _shift_up(dconv, 2)
        dp_ref[:, 0:ct] = (du * g_c).astype(BF16)
        dp_ref[:, 2 * ct:3 * ct] = (du * u_x).astype(BF16)

    return pl.pallas_call(
        body, name="conv_bwd", grid=(d // ct, bsz),
        in_specs=[pl.BlockSpec(memory_space=pl.ANY), pl.BlockSpec((seq, ct), lambda j, b: (b, j)),
                  pl.BlockSpec((seq, 4 * ct), lambda j, b: (b, j)), pl.BlockSpec((3, ct), lambda j, b: (0, j))],
        out_specs=[pl.BlockSpec((seq, 4 * ct), lambda j, b: (b, base + j)), pl.BlockSpec((8, ct), lambda j, b: (0, j))],
        out_shape=[jax.ShapeDtypeStruct(dproj.shape, BF16), jax.ShapeDtypeStruct((8, d), F32)],
        input_output_aliases={0: 0}, compiler_params=_params("arbitrary", "arbitrary"))(dproj, db, pr, conv_w)


def _attn_bwd(dproj, pa, o_all, lse_all, da, bsz, seq):
    n_blocks = seq // SUB
    chunk = 256

    def body(dp_in, pa_ref, o_ref, lse_ref, da_ref, dp_ref, sub, stage, dsub, dog, cvec, bias_ref):
        del dp_in
        p = pl.program_id(1)
        _fill_bias(bias_ref, p, seq)
        head0 = lax.broadcasted_iota(jnp.int32, (SUB, SLAB), 1) < HEAD_DIM

        def mix_bwd(i, carry):
            rows = pl.ds(pl.multiple_of(i * chunk, chunk), chunk)
            ls = [lse_ref[g, rows, :] for g in range(N_GROUPS)]
            os_ = [o_ref[g, rows, :] for g in range(N_GROUPS)]
            m = jnp.maximum(jnp.maximum(ls[0], ls[1]), ls[2])
            es = [jnp.exp(l - m) for l in ls]
            tot = es[0] + es[1] + es[2]
            ws = [e / tot for e in es]
            o = ws[0] * os_[0] + ws[1] * os_[1] + ws[2] * os_[2]
            z = pa_ref[rows, 9 * SLAB:10 * SLAB].astype(F32)
            sig = _sigmoid(z)
            dav = da_ref[rows, :]
            do = dav * (z * sig)
            dp_ref[rows, 9 * SLAB:10 * SLAB] = (dav * o * (sig * (1.0 + z * (1.0 - sig)))).astype(BF16)
            wsum = _head_sums(do * o)
            for g in range(N_GROUPS):
                dog[g, rows, :] = ws[g] * do
                cvec[g, rows, :] = -(ws[g] * wsum)
            return carry

        lax.fori_loop(0, seq // chunk, mix_bwd, 0)

        for g in range(N_GROUPS):
            dil = DILATIONS[g]
            for w in range(3):
                _to_sub_major(pa_ref, 3 * w + g, sub.at[w], stage, dil, seq)
            dsub[1] = jnp.zeros((seq, SLAB), F32)
            dsub[2] = jnp.zeros((seq, SLAB), F32)
            nk = _key_rows(g, seq)

            def block(it, g=g, dil=dil, nk=nk):
                row0, krow0, bi, nat = _block_rows(it, g, seq)
                q = sub[0, pl.ds(row0, SUB), :]
                kw = sub[1, pl.ds(krow0, nk), :]
                vw = sub[2, pl.ds(krow0, nk), :]
                do = _ld_rows(dog.at[g], nat, SUB, dil).astype(BF16)
                cv = _ld_rows(cvec.at[g], nat, SUB, dil)
                lse = _ld_rows(lse_ref.at[g], nat, SUB, dil)
                zero = jnp.zeros_like(q)
                dqs, dss, probs, qms, doms = [], [], [], [], []
                for hh in range(2):
                    mask = head0 if hh == 0 else ~head0
                    lane = hh * HEAD_DIM
                    qm = jnp.where(mask, q, zero)
                    dom = jnp.where(mask, do, zero)
                    s = _nt(qm * (HEAD_DIM ** -0.5), kw) + bias_ref[g, bi, hh, :, 0:nk]
                    prob = jnp.exp(s - lse[:, lane:lane + 1])
                    ds = (prob * (_nt(dom, vw) + cv[:, lane:lane + 1]) * (HEAD_DIM ** -0.5)).astype(BF16)
                    dqs.append(jnp.dot(ds, kw, preferred_element_type=F32))
                    dss.append(ds)
                    probs.append(prob.astype(BF16))
                    qms.append(qm)
                    doms.append(dom)
                dsub[0, pl.ds(row0, SUB), :] = jnp.where(head0, dqs[0], dqs[1])
                dk = _tn(jnp.concatenate(dss, axis=0), jnp.concatenate(qms, axis=0))
                dv = _tn(jnp.concatenate(probs, axis=0), jnp.concatenate(doms, axis=0))
                dsub[1, pl.ds(krow0, nk), :] += dk
                dsub[2, pl.ds(krow0, nk), :] += dv

            def two_blocks(i, carry, block=block):
                block(2 * i)
                block(2 * i + 1)
                return carry

            lax.fori_loop(0, n_blocks // 2, two_blocks, 0)
            for w in range(3):
                cols = slice((3 * w + g) * SLAB, (3 * w + g + 1) * SLAB)
                if dil == 1:
                    dp_ref[:, cols] = dsub[w].astype(BF16)
                else:
                    n = seq // dil
                    for r in range(dil):
                        stage[pl.ds(r, n, stride=dil), :] = dsub[w, pl.ds(r * n, n), :]
                    dp_ref[:, cols] = stage[...].astype(BF16)

    return pl.pallas_call(
        body, name="attn_bwd", grid=(bsz, 2),
        in_specs=[pl.BlockSpec(memory_space=pl.ANY), pl.BlockSpec((seq, PAIR_COLS), lambda b, p: (b, p)),
                  pl.BlockSpec((N_GROUPS, seq, SLAB), lambda b, p: (0, b, p)),
                  pl.BlockSpec((N_GROUPS, seq, SLAB), lambda b, p: (0, b, p)),
                  pl.BlockSpec((seq, SLAB), lambda b, p: (b, p))],
        out_specs=pl.BlockSpec((seq, PAIR_COLS), lambda b, p: (b, p)),
        out_shape=jax.ShapeDtypeStruct(dproj.shape, BF16), input_output_aliases={0: 0},
        scratch_shapes=[pltpu.VMEM((3, seq, SLAB), BF16), pltpu.VMEM((seq, SLAB), F32), pltpu.VMEM((3, seq, SLAB), F32),
                        pltpu.VMEM((3, seq, SLAB), F32), pltpu.VMEM((3, seq, SLAB), F32),
                        pltpu.VMEM((N_GROUPS, 2, 2, SUB, 2 * SUB), F32)],
        compiler_params=_params("arbitrary", "arbitrary"))(dproj, pa, o_all, lse_all, da)


def _grad_h(dproj, w_all, gx0, x2, mod3, seq, lay):
    t, d = x2.shape
    tm, tn = min(512, seq), min(512, d)
    per_seq = seq // tm

    def body(dp_ref, w_ref, gx0_ref, x_ref, scale_ref, gx_ref, dmod_ref):
        dh = _nt(dp_ref[:, 0:ATT], w_ref[:, 0:ATT]) + _nt(dp_ref[:, lay.c0:], w_ref[:, lay.c0:])
        gx_ref[...] = gx0_ref[...] + dh * (1.0 + scale_ref[...])

        @pl.when(pl.program_id(1) % per_seq == 0)
        def _():
            dmod_ref[...] = jnp.zeros_like(dmod_ref)

        dmod_ref[0:1, :] += jnp.sum(dh, axis=0, keepdims=True)
        dmod_ref[1:2, :] += jnp.sum(dh * x_ref[...], axis=0, keepdims=True)

    tile = pl.BlockSpec((tm, tn), lambda j, i: (i, j))
    return pl.pallas_call(
        body, name="grad_h", grid=(d // tn, t // tm),
        in_specs=[pl.BlockSpec((tm, lay.np), lambda j, i: (i, 0)), pl.BlockSpec((tn, lay.np), lambda j, i: (j, 0)), tile, tile,
                  pl.BlockSpec((None, 1, tn), lambda j, i: (i // per_seq, 0, d // tn + j))],
        out_specs=[tile, pl.BlockSpec((None, 8, tn), lambda j, i: (i // per_seq, 0, j))],
        out_shape=[jax.ShapeDtypeStruct((t, d), F32), jax.ShapeDtypeStruct((t // seq, 8, d), F32)],
        compiler_params=_params("arbitrary", "arbitrary"))(dproj, w_all, gx0, x2, mod3)


def _grad_w_in(ht, dproj, seq, lay):
    d, t = ht.shape
    tm = seq
    n_i = t // tm

    def make_body(n_skip, n_pieces):
        def body(*refs):
            refs = refs[n_skip:]
            ht_ref, dp_refs = refs[0], refs[1:1 + n_pieces]
            gw_ref, gb_ref, acc, bacc = refs[1 + n_pieces:]
            i = pl.program_id(1)

            @pl.when(i == 0)
            def _():
                acc[...] = jnp.zeros_like(acc)
                bacc[...] = jnp.zeros_like(bacc)

            dp = dp_refs[0][...] if n_pieces == 1 else jnp.concatenate([r[...] for r in dp_refs], axis=1)
            acc[...] += jnp.dot(ht_ref[...], dp, preferred_element_type=F32)
            bacc[...] += jnp.sum(dp.astype(F32), axis=0, keepdims=True)

            @pl.when(i == n_i - 1)
            def _():
                gw_ref[...] = acc[...].astype(BF16)
                gb_ref[...] = bacc[...]
        return body

    def call(name, pieces, n_tiles, nat_tile, prev):
        tn = sum(w for w, _ in pieces)
        in_specs = [pl.BlockSpec((d, tm), lambda j, i: (0, i))]
        in_specs += [pl.BlockSpec((tm, w), lambda j, i, f=f: (i, f(j))) for w, f in pieces]
        args = [ht] + [dproj] * len(pieces)
        aliases = {}
        if prev is not None:
            in_specs = [pl.BlockSpec(memory_space=pl.ANY)] * 2 + in_specs
            args = list(prev) + args
            aliases = {0: 0, 1: 1}
        return pl.pallas_call(
            make_body(0 if prev is None else 2, len(pieces)), name=name, grid=(n_tiles, n_i), in_specs=in_specs,
            out_specs=[pl.BlockSpec((d, tn), lambda j, i: (0, nat_tile(j))), pl.BlockSpec((1, tn), lambda j, i: (0, nat_tile(j)))],
            out_shape=[jax.ShapeDtypeStruct((d, lay.din), BF16), jax.ShapeDtypeStruct((1, lay.din), F32)],
            input_output_aliases=aliases,
            scratch_shapes=[pltpu.VMEM((d, tn), F32), pltpu.VMEM((1, tn), F32)],
            compiler_params=_params("arbitrary", "arbitrary"))(*args)

    attn_pieces = [(SLAB, lambda j, m=m: (m % 2) * PAIR_SLABS + 2 * j + m // 2) for m in range(4)]
    first = call("grad_w_in_attn", attn_pieces, ATT // 512, lambda j: j, None)
    base = lay.c0 // CONV_TILE
    nct = lay.n_conv_tiles
    if nct % 2:
        return call("grad_w_in_rest", [(CONV_TILE, lambda j: base + j)], 6 * d // CONV_TILE, lay.rest_nat_tile, first)
    half = nct // 2

    def rest_piece(m):
        def perm_tile(j):
            conv = base + 4 * (2 * (j % half) + m) + j // half
            return jnp.where(j < 4 * half, conv, base + 2 * j + m)
        return (CONV_TILE, perm_tile)

    return call("grad_w_in_rest", [rest_piece(0), rest_piece(1)], 6 * d // 512, lambda j: ATT // 512 + j, first)


def _pack_rows(parts, width=128):
    flat = [p.reshape(-1) for p in parts]
    spans, rows = [], 0
    padded = []
    for f in flat:
        n = -(-f.shape[0] // (8 * width)) * 8
        padded.append(jnp.pad(f, (0, n * width - f.shape[0])).reshape(n, width))
        spans.append((rows, f.shape[0]))
        rows += n
    return jnp.concatenate(padded, axis=0), spans


def _unpack_rows(packed, spans, shapes, width=128):
    out = []
    for (row, n), shp in zip(spans, shapes):
        rows = -(-n // width)
        out.append(packed[row:row + rows].reshape(-1)[:n].reshape(shp))
    return out


def kernel(x, c, w_ada, b_ada, w_in, b_in, conv_w, w_proj_attn, w_proj_conv, w_out, b_out, ln_g, ln_b, loss_target, m_w_ada, m_b_ada, m_w_in, m_b_in, m_conv_w, m_w_proj_attn, m_w_proj_conv, m_w_out, m_b_out, m_ln_g, m_ln_b, v_w_ada, v_b_ada, v_w_in, v_b_in, v_conv_w, v_w_proj_attn, v_w_proj_conv, v_w_out, v_b_out, v_ln_g, v_ln_b):
    bsz, seq, d = x.shape
    t = bsz * seq
    lay = _Layout(d)
    col_sharded = [True, True, False, False]
    red_w = [w_in[0], w_proj_attn[0], w_proj_conv[0], w_out[0]]
    chip = 2 * lax.axis_index("x") + lax.axis_index("y")
    chip1 = chip.astype(jnp.int32).reshape(1)
    core1 = lax.axis_index("c").astype(jnp.int32).reshape(1)
    place = jnp.stack([chip, lax.axis_index("c")]).astype(jnp.int32)
    x2 = x.reshape(t, d)
    target2 = loss_target.reshape(t, d)

    mod, act_all = _ada_exchange(c, w_ada[0], b_ada)
    mod3 = mod.reshape(bsz, 1, 3 * d)

    cw_pad = jnp.pad(conv_w[0], ((0, 5), (0, 0))) + 0.0 * mod[0, 0]
    own_in_full = [_cast_into_full(red_w[w], col_sharded[w], chip1, f"cast_shard_{w}") for w in range(4)]
    (wi_f,), cw8 = _gather_weights(own_in_full[:1], col_sharded[:1], cw_pad)
    cw_full = cw8[0:3]
    late_copies = _direct_gather_copies(col_sharded[1:])
    late_send, late_recv, late_flying, late_token = _start_copies("gather_late_start", own_in_full[1:], (18,), cw8, late_copies)
    w_all = _permute_w_in(wi_f, lay)
    b_all = lay.perm_vector(b_in) + late_token

    rest_tn = 1024 if (6 * d) % 1024 == 0 else 512
    pa, = _project(x2, mod3, w_all, b_all, seq, 0, ATT, PAIR_COLS, BF16, False, "project_attn")
    pr, ht = _project(x2, mod3, w_all, b_all, seq, lay.c0, 6 * d, rest_tn, BF16, True, "project_rest")
    o_all, lse_all, a_in = _attn_fwd(pa, bsz, seq)
    b_in_act = _conv_fwd(pr, cw_full, bsz, seq, d)
    wpa_f, wpc_f, wo_f = _wait_copies("gather_late_wait", late_flying, late_send, late_recv, b_in_act, late_copies)
    (dproj, da_in, db_in, gx0, dgate, small_tail, gw_pa, gw_pc, gw_out) = _tail(
        a_in, b_in_act, pr, x2, target2, mod3, wpa_f, wpc_f, wo_f, b_out, ln_g, ln_b, seq, lay)

    def mixers_backward(token):
        dp, gcw_ = _conv_bwd(dproj, db_in, pr, cw_full + token, bsz, seq, lay)
        dp = _attn_bwd(dp, pa, o_all, lse_all, da_in, bsz, seq)
        gw_in_bf_, gb_in_ = _grad_w_in(ht, dp, seq, lay)
        return (dp, gcw_, gw_in_bf_, gb_in_), gw_in_bf_

    late_views = _shard_views([gw_pa, gw_pc, gw_out], col_sharded[1:])
    late_got = _pair_exchange_halves(late_views, "grads_pair_exchange_late")
    late_parts = [_pair_sum(late_views[w], late_got[w], core1, f"grads_pair_sum_late_{w}") for w in range(3)]
    late_red, (dproj, gcw, gw_in_bf, gb_in) = _reduce_over_chips(
        late_parts, col_sharded[1:], place, late_parts[0], mixers_backward, "late")

    def input_backward(token):
        gx, dm = _grad_h(dproj, w_all, gx0, x2, mod3 + token, seq, lay)
        return (gx, dm), gx

    in_view = _shard_views([gw_in_bf], col_sharded[:1])
    in_got = _pair_exchange_halves(in_view, "grads_pair_exchange_in")
    in_part = _pair_sum(in_view[0], in_got[0], core1, "grads_pair_sum_in")
    in_red, (grad_x2, dmod) = _reduce_over_chips([in_part], col_sharded[:1], place, late_red[0], input_backward, "in")
    g_red = _pair_join_halves(in_red + late_red)

    d_ada = jnp.concatenate([dmod[:, 0, :], dmod[:, 1, :], dgate[:, 0, :]], axis=1)
    pieces = [small_tail[3], jnp.sum(d_ada, axis=0), gb_in[0], small_tail[0], small_tail[1], small_tail[2], gcw[0:3]]
    packed, spans = _pack_rows(pieces)
    kept_in, _ = _pack_rows([d_ada])
    summed, kept = _all_sum_small(jnp.concatenate([packed, kept_in], axis=0), packed.shape[0], d)
    loss = summed[0, 0]
    _, g_b_ada, g_b_in, g_b_out, g_ln_g, g_ln_b, g_cw_full = _unpack_rows(
        summed, spans, [(d,), (3 * d,), (lay.din,), (d,), (d,), (d,), (3, d)])
    g_cw = lax.dynamic_slice(g_cw_full, (0, chip * (d // N_CHIPS)), (3, d // N_CHIPS))
    d_ada_all = kept.reshape(N_DEV, -1)[:, :bsz * 3 * d].reshape(N_DEV * bsz, 3 * d)
    ada_cols = 3 * d // N_CHIPS
    g_w_ada = _grad_w_ada_cols(act_all.T, lax.dynamic_slice(d_ada_all, (0, chip * ada_cols), (N_DEV * bsz, ada_cols)))

    big_w = [w_ada[0]] + red_w
    big_m = [m_w_ada[0], m_w_in[0], m_w_proj_attn[0], m_w_proj_conv[0], m_w_out[0]]
    big_v = [v_w_ada[0], v_w_in[0], v_w_proj_attn[0], v_w_proj_conv[0], v_w_out[0]]
    g_big = [g_w_ada] + g_red
    big_out = [_adamw(big_w[w], g_big[w], big_m[w], big_v[w], f"adamw_{w}") for w in range(5)]
    small_w = [b_ada, b_in, conv_w[0], b_out, ln_g, ln_b]
    small_g = [g_b_ada, g_b_in, g_cw, g_b_out, g_ln_g, g_ln_b]
    small_m = [m_b_ada, m_b_in, m_conv_w[0], m_b_out, m_ln_g, m_ln_b]
    small_v = [v_b_ada, v_b_in, v_conv_w[0], v_b_out, v_ln_g, v_ln_b]
    pw, sp = _pack_rows(small_w)
    pg, _ = _pack_rows(small_g)
    pm, _ = _pack_rows(small_m)
    pv, _ = _pack_rows(small_v)
    sd, sm, sv = _adamw(pw, pg, pm, pv, "adamw_small")
    shapes = [a.shape for a in small_w]
    sd, sm, sv = _unpack_rows(sd, sp, shapes), _unpack_rows(sm, sp, shapes), _unpack_rows(sv, sp, shapes)

    def order(wa, bA, wi, bI, cw, wpa, wpc, wo, bO, lg, lb):
        return (wa[None], bA, wi[None], bI, cw[None], wpa[None], wpc[None], wo[None], bO, lg, lb)

    sg = [g.reshape(s) for g, s in zip(small_g, shapes)]
    grads_out = order(g_big[0], sg[0], g_big[1], sg[1], sg[2], g_big[2], g_big[3], g_big[4], sg[3], sg[4], sg[5])
    outs = []
    for idx, small in enumerate((sd, sm, sv)):
        outs.append(order(big_out[0][idx], small[0], big_out[1][idx], small[1], small[2], big_out[2][idx],
                          big_out[3][idx], big_out[4][idx], small[3], small[4], small[5]))
    return (loss, grad_x2.reshape(bsz, seq, d), *grads_out, *outs[0], *outs[1], *outs[2])
```

```python
import functools

import jax
import jax.numpy as jnp
from jax import lax
from jax.experimental import pallas as pl
from jax.experimental.pallas import tpu as pltpu

F32 = jnp.float32
BF16 = jnp.bfloat16
MESH = pl.DeviceIdType.MESH

HEAD_DIM = 64
N_GROUPS = 3
DILATIONS = (1, 4, 16)
N_HEADS = 12
SUB = 128
Q_WIDTH = 768
Z_WIDTH = 256
ATT = 3 * Q_WIDTH + Z_WIDTH
SLAB = 128
PAIR_SLABS = 10
PAIR_COLS = PAIR_SLABS * SLAB
CONV_TILE = 256
SOFTMAX_ROWS = 32
BLOCKS_PER_TRIP = 4
ALIBI_MAX_EXP = 8.0
ALPHA = 2.0 ** 0.25
LN_EPS = 1e-5
ADAM_LR, ADAM_B1, ADAM_B2, ADAM_EPS, ADAM_WD, ADAM_STEP = 0.001, 0.9, 0.999, 1e-08, 0.01, 10
N_CHIPS = 4
N_DEV = 8
VMEM_LIMIT_V7X = 60 * 1024 * 1024
NEG = -1e30


def _params(*sem):
    return pltpu.CompilerParams(dimension_semantics=sem, vmem_limit_bytes=VMEM_LIMIT_V7X)


def _sigmoid(v):
    return 1.0 / (1.0 + jnp.exp(-v))


class _Layout:
    def __init__(self, d):
        self.d = d
        self.din = ATT + 6 * d
        c0 = 3072
        while c0 % (2 * d):
            c0 += 1024
        self.c0, self.g0, self.np = c0, c0 + 4 * d, c0 + 6 * d
        self.n_conv_tiles = d // CONV_TILE

    def attn_nat_slab(self, s):
        p, i = s // PAIR_SLABS, s % PAIR_SLABS
        return jnp.where(i < 9, (i // 3) * 6 + (i % 3) * 2 + p, 18 + p)

    def rest_nat_tile(self, t):
        n4 = 4 * self.n_conv_tiles
        conv = ATT // CONV_TILE + (t % 4) * self.n_conv_tiles + t // 4
        return jnp.where(t < n4, conv, ATT // CONV_TILE + t)

    def perm_vector(self, v):
        parts = []
        for s in range(2 * PAIR_SLABS):
            p, i = divmod(s, PAIR_SLABS)
            ns = (i // 3) * 6 + (i % 3) * 2 + p if i < 9 else 18 + p
            parts.append(v[:, ns * SLAB:(ns + 1) * SLAB])
        parts.append(jnp.zeros((1, self.c0 - ATT), v.dtype))
        for j in range(self.n_conv_tiles):
            for k in range(4):
                a = ATT + k * self.d + j * CONV_TILE
                parts.append(v[:, a:a + CONV_TILE])
        parts.append(v[:, ATT + 4 * self.d:])
        return jnp.concatenate(parts, axis=1)


def _place():
    return lax.axis_index("x"), lax.axis_index("y"), lax.axis_index("c")


def _other_chips(x, y):
    return [(1 - x, y), (x, 1 - y), (1 - x, 1 - y)]


def _shard_of(ref, col_sharded, chip, half=None):
    if col_sharded:
        cs = ref.shape[1] // N_CHIPS
        cols = pl.ds(pl.multiple_of(chip * cs, SLAB), cs)
        if half is None:
            return ref.at[:, cols]
        n = ref.shape[0] // 2
        return ref.at[pl.ds(half * n, n), cols]
    rs = ref.shape[0] // N_CHIPS
    if half is None:
        return ref.at[pl.ds(chip * rs, rs)]
    return ref.at[pl.ds(chip * rs + half * (rs // 2), rs // 2)]


def _cast_into_full(shard, col_sharded, chip, name):
    rows, cols = shard.shape
    tr = _row_tile(rows, cols)
    nb = rows // tr

    def body(chip_ref, s_ref, o_ref):
        del chip_ref
        o_ref[...] = s_ref[...].astype(BF16)

    if col_sharded:
        full, out_spec = (rows, cols * N_CHIPS), pl.BlockSpec((tr, cols), lambda i, ch: (i, ch[0]))
    else:
        full, out_spec = (rows * N_CHIPS, cols), pl.BlockSpec((tr, cols), lambda i, ch: (ch[0] * nb + i, 0))
    return pl.pallas_call(
        body, name=name,
        grid_spec=pltpu.PrefetchScalarGridSpec(num_scalar_prefetch=1, grid=(nb,),
                                               in_specs=[pl.BlockSpec((tr, cols), lambda i, ch: (i, 0))], out_specs=out_spec),
        out_shape=jax.ShapeDtypeStruct(full, BF16), compiler_params=_params("parallel"))(chip, shard)


def _gather_weights(fulls, col_sharded, small):
    n = len(fulls)

    def body(*refs):
        sm_in, outs, sm_out = refs[n], refs[n + 1:2 * n + 1], refs[2 * n + 1]
        send, recv, fsend, frecv, lsem, ssend, srecv = refs[2 * n + 2:]
        x, y, c = _place()
        mine = 2 * x + y
        sibling = (x, y, 1 - c)
        chips = _other_chips(x, y)

        local = [pltpu.make_async_copy(sm_in, _shard_of(sm_out, True, mine), lsem)]
        for cp in local:
            cp.start()
        sends = []
        for k, (cx, cy) in enumerate(chips):
            cp = pltpu.make_async_remote_copy(src_ref=sm_in, dst_ref=_shard_of(sm_out, True, mine), send_sem=ssend.at[k],
                                              recv_sem=srecv.at[k], device_id=(cx, cy, c), device_id_type=MESH)
            cp.start()
            sends.append(cp)
        for k, (cx, cy) in enumerate(chips):
            for w in range(n):
                own_half = _shard_of(outs[w], col_sharded[w], mine, c)
                cp = pltpu.make_async_remote_copy(
                    src_ref=own_half, dst_ref=own_half,
                    send_sem=send.at[w, k], recv_sem=recv.at[w, k], device_id=(cx, cy, c), device_id_type=MESH)
                cp.start()
                sends.append(cp)
        for k, (cx, cy) in enumerate(chips):
            theirs = 2 * cx + cy
            for w in range(n):
                landed = _shard_of(outs[w], col_sharded[w], theirs, c)
                pltpu.make_async_remote_copy(src_ref=landed, dst_ref=landed, send_sem=send.at[w, k], recv_sem=recv.at[w, k],
                                             device_id=(cx, cy, c), device_id_type=MESH).wait_recv()
                cp = pltpu.make_async_remote_copy(src_ref=landed, dst_ref=landed, send_sem=fsend.at[w, k],
                                                  recv_sem=frecv.at[w, k], device_id=sibling, device_id_type=MESH)
                cp.start()
                sends.append(cp)
        for k, (cx, cy) in enumerate(chips):
            theirs = 2 * cx + cy
            for w in range(n):
                passed = _shard_of(outs[w], col_sharded[w], theirs, 1 - c)
                pltpu.make_async_remote_copy(src_ref=passed, dst_ref=passed, send_sem=fsend.at[w, k], recv_sem=frecv.at[w, k],
                                             device_id=sibling, device_id_type=MESH).wait_recv()
        for k, (cx, cy) in enumerate(chips):
            theirs = _shard_of(sm_out, True, 2 * cx + cy)
            pltpu.make_async_remote_copy(src_ref=theirs, dst_ref=theirs, send_sem=ssend.at[k], recv_sem=srecv.at[k],
                                         device_id=(cx, cy, c), device_id_type=MESH).wait_recv()
        for cp in sends:
            cp.wait_send()
        for cp in local:
            cp.wait()

    any_spec = pl.BlockSpec(memory_space=pl.ANY)
    outs = pl.pallas_call(
        body, name="gather_weights",
        out_shape=[jax.ShapeDtypeStruct(f.shape, BF16) for f in fulls]
        + [jax.ShapeDtypeStruct((small.shape[0], small.shape[1] * N_CHIPS), small.dtype)],
        in_specs=[any_spec] * (n + 1), out_specs=[any_spec] * (n + 1), input_output_aliases={w: w for w in range(n)},
        scratch_shapes=[pltpu.SemaphoreType.DMA((n, 3)), pltpu.SemaphoreType.DMA((n, 3)),
                        pltpu.SemaphoreType.DMA((n, 3)), pltpu.SemaphoreType.DMA((n, 3)), pltpu.SemaphoreType.DMA,
                        pltpu.SemaphoreType.DMA((3,)), pltpu.SemaphoreType.DMA((3,))],
    )(*fulls, small)
    return outs[:n], outs[n]


HBM_SPEC = pl.BlockSpec(memory_space=pltpu.HBM)
SEM_SPEC = pl.BlockSpec(memory_space=pltpu.SEMAPHORE)
DATAFLOW = pltpu.SideEffectType.DATAFLOW_SIDE_EFFECTING


def _start_copies(name, arrays, sem_shape, after, copies):
    n = len(arrays)

    def body(*refs):
        for cp in copies(refs[:n], refs[n + 1], refs[n + 2]):
            cp.start()
        token = refs[2 * n + 3]
        token[...] = jnp.zeros_like(token)

    res = pl.pallas_call(
        body, name=name,
        out_shape=(pltpu.SemaphoreType.DMA(sem_shape), pltpu.SemaphoreType.DMA(sem_shape),
                   *[pltpu.HBM(a.shape, a.dtype) for a in arrays], jax.ShapeDtypeStruct((8, 128), F32)),
        in_specs=[HBM_SPEC] * n + [pl.BlockSpec(memory_space=pl.ANY)],
        out_specs=(SEM_SPEC, SEM_SPEC, *([HBM_SPEC] * n), pl.BlockSpec(memory_space=pltpu.VMEM)),
        input_output_aliases={i: 2 + i for i in range(n)},
        compiler_params=pltpu.CompilerParams(has_side_effects=DATAFLOW),
    )(*[pltpu.with_memory_space_constraint(a, pltpu.HBM) for a in arrays], after)
    return res[0], res[1], list(res[2:2 + n]), res[2 + n][0, 0]


def _wait_copies(name, arrays, send, recv, after, copies):
    n = len(arrays)

    def body(*refs):
        for cp in copies(refs[:n], refs[n], refs[n + 1]):
            cp.wait_send()
            cp.wait_recv()

    return pl.pallas_call(
        body, name=name, out_shape=[pltpu.HBM(a.shape, a.dtype) for a in arrays],
        in_specs=[HBM_SPEC] * n + [SEM_SPEC, SEM_SPEC, pl.BlockSpec(memory_space=pl.ANY)], out_specs=[HBM_SPEC] * n,
        input_output_aliases={i: i for i in range(n)},
        compiler_params=pltpu.CompilerParams(has_side_effects=DATAFLOW),
    )(*arrays, send, recv, after)


def _direct_gather_copies(col_sharded):
    def copies(refs, send, recv):
        x, y, c = _place()
        mine = 2 * x + y
        out = []
        for w, ref in enumerate(refs):
            own_half = _shard_of(ref, col_sharded[w], mine, c)
            k = 0
            for cx, cy in _other_chips(x, y):
                for pc in (c, 1 - c):
                    out.append(pltpu.make_async_remote_copy(
                        src_ref=own_half, dst_ref=own_half, send_sem=send.at[6 * w + k], recv_sem=recv.at[6 * w + k],
                        device_id=(cx, cy, pc), device_id_type=MESH))
                    k += 1
        return out
    return copies


def _chip_scatter_copies(n, col_sharded):
    def piece(ref, cs, chip):
        if cs:
            w = ref.shape[2] // N_CHIPS
            return ref.at[:, :, pl.ds(pl.multiple_of(chip * w, SLAB), w)]
        return ref.at[pl.ds(chip, 1)]

    def copies(refs, send, recv):
        x, y, c = _place()
        out = []
        for k, (cx, cy) in enumerate(_other_chips(x, y)):
            for w in range(n):
                out.append(pltpu.make_async_remote_copy(
                    src_ref=piece(refs[w], col_sharded[w], 2 * cx + cy), dst_ref=refs[n + w].at[pl.ds(k, 1)],
                    send_sem=send.at[3 * w + k], recv_sem=recv.at[3 * w + k], device_id=(cx, cy, c), device_id_type=MESH))
        return out
    return copies


def _shard_views(gs, col_sharded):
    return [g.reshape(1, *g.shape) if cs else g.reshape(N_CHIPS, g.shape[0] // N_CHIPS, g.shape[1])
            for g, cs in zip(gs, col_sharded)]


DMA_CHUNK_BYTES = 1 << 20


def _chunk_rows(shape, itemsize):
    s, rows, cols = shape
    n = 1
    while s * (rows // n) * cols * itemsize > DMA_CHUNK_BYTES and (rows // n) % 32 == 0:
        n *= 2
    return rows // n


def _send_rows(src, src_row0, dst, dst_row0, rows, send_sem, recv_sem, device):
    step = _chunk_rows((src.shape[0], rows, src.shape[2]), src.dtype.itemsize)
    for r in range(0, rows, step):
        pltpu.make_async_remote_copy(src_ref=src.at[:, pl.ds(src_row0 + r, step)], dst_ref=dst.at[:, pl.ds(dst_row0 + r, step)],
                                     send_sem=send_sem, recv_sem=recv_sem, device_id=device, device_id_type=MESH).start()


def _pair_exchange_halves(views, name):
    n = len(views)
    half_shapes = [(v.shape[0], v.shape[1] // 2, v.shape[2]) for v in views]

    def body(*refs):
        ins, got = refs[:n], refs[n:2 * n]
        send, recv = refs[2 * n:]
        x, y, c = _place()
        sibling = (x, y, 1 - c)
        for w in range(n):
            hr = half_shapes[w][1]
            _send_rows(ins[w], (1 - c) * hr, got[w], 0, hr, send.at[w], recv.at[w], sibling)
        for w in range(n):
            hr = half_shapes[w][1]
            pltpu.make_async_remote_copy(src_ref=ins[w].at[:, pl.ds((1 - c) * hr, hr)], dst_ref=got[w], send_sem=send.at[w],
                                         recv_sem=recv.at[w], device_id=sibling, device_id_type=MESH).wait()

    any_spec = pl.BlockSpec(memory_space=pl.ANY)
    return pl.pallas_call(
        body, name=name,
        out_shape=[jax.ShapeDtypeStruct(s, v.dtype) for s, v in zip(half_shapes, views)],
        in_specs=[any_spec] * n, out_specs=[any_spec] * n,
        scratch_shapes=[pltpu.SemaphoreType.DMA((n,)), pltpu.SemaphoreType.DMA((n,))],
    )(*views)


def _pair_sum(view, got, core, name):
    s, r, cols = view.shape
    hr = r // 2
    tr = _row_tile(hr, cols)
    nb = hr // tr

    def body(core_ref, a_ref, b_ref, o_ref):
        del core_ref
        o_ref[...] = (a_ref[...].astype(F32) + b_ref[...].astype(F32)).astype(BF16)

    same = pl.BlockSpec((None, tr, cols), lambda j, i, core_ref: (j, i, 0))
    return pl.pallas_call(
        body, name=name,
        grid_spec=pltpu.PrefetchScalarGridSpec(
            num_scalar_prefetch=1, grid=(s, nb),
            in_specs=[pl.BlockSpec((None, tr, cols), lambda j, i, core_ref: (j, core_ref[0] * nb + i, 0)), same],
            out_specs=same),
        out_shape=jax.ShapeDtypeStruct((s, hr, cols), BF16), compiler_params=_params("parallel", "parallel"))(core, view, got)


def _piece_cols(part, col_sharded):
    return part.shape[2] // N_CHIPS if col_sharded else part.shape[2]


def _reduce_over_chips(parts, col_sharded, place, after, between, tag):
    n = len(parts)
    lands = [lax.empty((3, p.shape[1], _piece_cols(p, cs)), p.dtype) for p, cs in zip(parts, col_sharded)]
    copies = _chip_scatter_copies(n, col_sharded)
    send, recv, flying, token = _start_copies(f"grads_scatter_{tag}_start", list(parts) + lands, (3 * n,), after, copies)
    done, marker = between(token)
    landed = _wait_copies(f"grads_scatter_{tag}_wait", flying, send, recv, marker, copies)
    reduced = [_chip_sum(landed[w], landed[n + w], col_sharded[w], place, f"grads_chip_sum_{tag}_{w}") for w in range(n)]
    return reduced, done


def _chip_sum(part, got, col_sharded, place, name):
    _, hr, _ = part.shape
    cols = _piece_cols(part, col_sharded)
    tr = _row_tile(hr, cols)
    nb = hr // tr

    def body(place_ref, own_ref, g0_ref, g1_ref, g2_ref, o_ref):
        del place_ref
        acc = own_ref[...].astype(F32) + g0_ref[...].astype(F32)
        o_ref[...] = acc + g1_ref[...].astype(F32) + g2_ref[...].astype(F32)

    if col_sharded:
        own = pl.BlockSpec((None, tr, cols), lambda i, pr: (0, i, pr[0]))
    else:
        own = pl.BlockSpec((None, tr, cols), lambda i, pr: (pr[0], i, 0))
    others = [pl.BlockSpec((None, tr, cols), lambda i, pr, k=k: (k, i, 0)) for k in range(3)]
    return pl.pallas_call(
        body, name=name,
        grid_spec=pltpu.PrefetchScalarGridSpec(
            num_scalar_prefetch=1, grid=(nb,), in_specs=[own] + others,
            out_specs=pl.BlockSpec((tr, cols), lambda i, pr: (pr[1] * nb + i, 0))),
        out_shape=jax.ShapeDtypeStruct((2 * hr, cols), F32), compiler_params=_params("parallel"))(place, part, got, got, got)


def _pair_join_halves(fulls):
    n = len(fulls)
    views = [f.reshape(1, *f.shape) for f in fulls]

    def body(*refs):
        outs = refs[n:2 * n]
        send, recv = refs[2 * n:]
        x, y, c = _place()
        sibling = (x, y, 1 - c)
        for w in range(n):
            hr = outs[w].shape[1] // 2
            _send_rows(outs[w], c * hr, outs[w], c * hr, hr, send.at[w], recv.at[w], sibling)
        for w in range(n):
            hr = outs[w].shape[1] // 2
            pltpu.make_async_remote_copy(
                src_ref=outs[w].at[:, pl.ds(c * hr, hr)], dst_ref=outs[w].at[:, pl.ds((1 - c) * hr, hr)], send_sem=send.at[w],
                recv_sem=recv.at[w], device_id=sibling, device_id_type=MESH).wait()

    any_spec = pl.BlockSpec(memory_space=pl.ANY)
    outs = pl.pallas_call(
        body, name="grads_pair_join",
        out_shape=[jax.ShapeDtypeStruct(v.shape, v.dtype) for v in views],
        in_specs=[any_spec] * n, out_specs=[any_spec] * n, input_output_aliases={w: w for w in range(n)},
        scratch_shapes=[pltpu.SemaphoreType.DMA((n,)), pltpu.SemaphoreType.DMA((n,))],
    )(*views)
    return [o[0] for o in outs]


def _all_sum_small(vec, n_sum, d):
    rows = vec.shape[0]

    def body(v_ref, sum_ref, kept_ref, all_ref, send, recv):
        x, y, c = _place()
        me = 4 * x + 2 * y + c
        all_ref[me] = v_ref[...]
        peers = _all_devices(x, y, c)
        copies = []
        for k, (peer, _) in enumerate(peers):
            cp = pltpu.make_async_remote_copy(src_ref=v_ref, dst_ref=all_ref.at[me], send_sem=send.at[k], recv_sem=recv.at[k],
                                              device_id=peer, device_id_type=MESH)
            cp.start()
            copies.append(cp)
        for k, (_, src) in enumerate(peers):
            pltpu.make_async_remote_copy(src_ref=v_ref, dst_ref=all_ref.at[src], send_sem=send.at[k], recv_sem=recv.at[k],
                                         device_id=(x, y, c), device_id_type=MESH).wait_recv()
        for cp in copies:
            cp.wait_send()
        total = all_ref[0, 0:n_sum, :]
        for i in range(1, N_DEV):
            total = total + all_ref[i, 0:n_sum, :]
        sum_ref[...] = total
        loss = 0.5 / d * jnp.sum(total[0:8, :])
        sum_ref[0:8, :] = jnp.full((8, 128), loss, F32)
        for i in range(N_DEV):
            kept_ref[i] = all_ref[i, n_sum:rows, :]

    vm = pl.BlockSpec(memory_space=pltpu.VMEM)
    return pl.pallas_call(
        body, name="all_sum_small",
        out_shape=[jax.ShapeDtypeStruct((n_sum, 128), F32), jax.ShapeDtypeStruct((N_DEV, rows - n_sum, 128), F32)],
        in_specs=[vm], out_specs=[vm, vm],
        scratch_shapes=[pltpu.VMEM((N_DEV, rows, 128), F32), pltpu.SemaphoreType.DMA((N_DEV - 1,)),
                        pltpu.SemaphoreType.DMA((N_DEV - 1,))],
        compiler_params=pltpu.CompilerParams(vmem_limit_bytes=VMEM_LIMIT_V7X),
    )(vec)


def _row_tile(rows, cols, itemsize=4, budget=2 << 20):
    t = rows
    while t * cols * itemsize > budget and t % 16 == 0:
        t //= 2
    return t


def _adamw(w, g, m, v, name):
    rows, cols = w.shape
    tr = _row_tile(rows, cols, budget=1 << 20)

    def body(w_ref, g_ref, m_ref, v_ref, d_ref, nm_ref, nv_ref):
        g_ = g_ref[...]
        nm = ADAM_B1 * m_ref[...] + (1.0 - ADAM_B1) * g_
        nv = ADAM_B2 * v_ref[...] + (1.0 - ADAM_B2) * (g_ * g_)
        m_hat = nm / (1.0 - ADAM_B1 ** ADAM_STEP)
        v_hat = nv / (1.0 - ADAM_B2 ** ADAM_STEP)
        d_ref[...] = -ADAM_LR * (m_hat / (jnp.sqrt(v_hat) + ADAM_EPS) + ADAM_WD * w_ref[...])
        nm_ref[...] = nm
        nv_ref[...] = nv

    spec = pl.BlockSpec((tr, cols), lambda i: (i, 0))
    shp = jax.ShapeDtypeStruct((rows, cols), F32)
    return pl.pallas_call(body, name=name, grid=(rows // tr,), in_specs=[spec] * 4, out_specs=[spec] * 3,
                          out_shape=[shp] * 3, compiler_params=_params("parallel"))(w, g, m, v)


def _all_devices(x, y, c):
    out = []
    for k in range(1, N_DEV):
        peer = (x ^ ((k >> 2) & 1), y ^ ((k >> 1) & 1), c ^ (k & 1))
        out.append((peer, 4 * peer[0] + 2 * peer[1] + peer[2]))
    return out


def _ada_exchange(c, w_shard, b_ada):
    bsz, d = c.shape
    cs = w_shard.shape[1]

    def body(c_ref, w_ref, b_ref, mod_ref, act_ref, c_all, part, pieces, csend, crecv, psend, precv):
        x, y, core = _place()
        me = 4 * x + 2 * y + core
        chip = 2 * x + y
        c_all[me] = c_ref[...]
        peers = _all_devices(x, y, core)
        copies = []
        for k, (peer, _) in enumerate(peers):
            cp = pltpu.make_async_remote_copy(src_ref=c_ref, dst_ref=c_all.at[me], send_sem=csend.at[k], recv_sem=crecv.at[k],
                                              device_id=peer, device_id_type=MESH)
            cp.start()
            copies.append(cp)
        for k, (_, src) in enumerate(peers):
            pltpu.make_async_remote_copy(src_ref=c_ref, dst_ref=c_all.at[src], send_sem=csend.at[k], recv_sem=crecv.at[k],
                                         device_id=(x, y, core), device_id_type=MESH).wait_recv()
        rows = jnp.concatenate([c_all[i] for i in range(N_DEV)], axis=0)
        act = rows * _sigmoid(rows)
        act_ref[...] = act
        prod = jnp.dot(act.astype(BF16), w_ref[...].astype(BF16), preferred_element_type=F32)
        for i in range(N_DEV):
            part[i] = prod[i * bsz:(i + 1) * bsz, :]
        pieces[chip] = part[me]
        chips = _other_chips(x, y)
        for k, (cx, cy) in enumerate(chips):
            cp = pltpu.make_async_remote_copy(src_ref=part.at[4 * cx + 2 * cy + core], dst_ref=pieces.at[chip],
                                              send_sem=psend.at[k], recv_sem=precv.at[k], device_id=(cx, cy, core),
                                              device_id_type=MESH)
            cp.start()
            copies.append(cp)
        for k, (cx, cy) in enumerate(chips):
            pltpu.make_async_remote_copy(src_ref=part.at[me], dst_ref=pieces.at[2 * cx + cy], send_sem=psend.at[k],
                                         recv_sem=precv.at[k], device_id=(cx, cy, core), device_id_type=MESH).wait_recv()
        for cp in copies:
            cp.wait_send()
        mod_ref[...] = jnp.concatenate([pieces[j] for j in range(N_CHIPS)], axis=1) + b_ref[...]

    vm = pl.BlockSpec(memory_space=pltpu.VMEM)
    return pl.pallas_call(
        body, name="ada_exchange", in_specs=[vm] * 3, out_specs=[vm] * 2,
        out_shape=[jax.ShapeDtypeStruct((bsz, 3 * d), F32), jax.ShapeDtypeStruct((N_DEV * bsz, d), F32)],
        scratch_shapes=[pltpu.VMEM((N_DEV, bsz, d), F32), pltpu.VMEM((N_DEV, bsz, cs), F32), pltpu.VMEM((N_CHIPS, bsz, cs), F32),
                        pltpu.SemaphoreType.DMA((N_DEV - 1,)), pltpu.SemaphoreType.DMA((N_DEV - 1,)),
                        pltpu.SemaphoreType.DMA((3,)), pltpu.SemaphoreType.DMA((3,))],
        compiler_params=pltpu.CompilerParams(vmem_limit_bytes=VMEM_LIMIT_V7X))(c, w_shard, b_ada)


def _grad_w_ada_cols(act_t, d_cols):
    d, n = act_t.shape
    cs = d_cols.shape[1]

    def body(a_ref, g_ref, o_ref):
        a, g = a_ref[...], g_ref[...]
        acc = a[:, 0:1] * g[0:1, :]
        for b in range(1, n):
            acc = acc + a[:, b:b + 1] * g[b:b + 1, :]
        o_ref[...] = acc

    vm = pl.BlockSpec(memory_space=pltpu.VMEM)
    return pl.pallas_call(body, name="grad_w_ada", in_specs=[vm] * 2, out_specs=vm,
                          out_shape=jax.ShapeDtypeStruct((d, cs), F32),
                          compiler_params=pltpu.CompilerParams(vmem_limit_bytes=VMEM_LIMIT_V7X))(act_t, d_cols)


def _permute_w_in(w_nat, lay):
    d = lay.d

    def copy_body(src_ref, dst_in_ref, dst_ref):
        del dst_in_ref
        dst_ref[...] = src_ref[...]

    def first_body(src_ref, dst_ref):
        dst_ref[...] = src_ref[...]

    w_all = pl.pallas_call(
        first_body, name="permute_w_attn", grid=(2 * PAIR_SLABS,),
        in_specs=[pl.BlockSpec((d, SLAB), lambda s: (0, lay.attn_nat_slab(s)))],
        out_specs=pl.BlockSpec((d, SLAB), lambda s: (0, s)),
        out_shape=jax.ShapeDtypeStruct((d, lay.np), BF16), compiler_params=_params("arbitrary"))(w_nat)
    n_rest = 6 * d // CONV_TILE
    base = lay.c0 // CONV_TILE
    return pl.pallas_call(
        copy_body, name="permute_w_rest", grid=(n_rest,),
        in_specs=[pl.BlockSpec((d, CONV_TILE), lambda t: (0, lay.rest_nat_tile(t))), pl.BlockSpec(memory_space=pl.ANY)],
        out_specs=pl.BlockSpec((d, CONV_TILE), lambda t: (0, base + t)),
        out_shape=jax.ShapeDtypeStruct((d, lay.np), BF16), input_output_aliases={1: 0},
        compiler_params=_params("arbitrary"))(w_nat, w_all)


def _project(x2, mod3, w_all, b_all, seq, col0, ncols, tn, out_dtype, want_ht, name):
    t, d = x2.shape
    tm = min(1024, seq)
    per_seq = seq // tm
    j0 = col0 // tn

    def body(x_ref, mod_ref, w_ref, b_ref, o_ref, *rest):
        h_ref = rest[-1]

        @pl.when(pl.program_id(1) == 0)
        def _():
            h = x_ref[...] * (1.0 + mod_ref[:, d:2 * d]) + mod_ref[:, 0:d]
            h_ref[...] = h.astype(BF16)
            if want_ht:
                rest[0][...] = h.T.astype(BF16)

        o_ref[...] = (jnp.dot(h_ref[...], w_ref[...], preferred_element_type=F32) + b_ref[...]).astype(out_dtype)

    out_shape = [jax.ShapeDtypeStruct((t, ncols), out_dtype)]
    out_specs = [pl.BlockSpec((tm, tn), lambda i, j: (i, j))]
    if want_ht:
        out_shape.append(jax.ShapeDtypeStruct((d, t), BF16))
        out_specs.append(pl.BlockSpec((d, tm), lambda i, j: (0, i)))
    return pl.pallas_call(
        body, name=name, grid=(t // tm, ncols // tn),
        in_specs=[pl.BlockSpec((tm, d), lambda i, j: (i, 0)),
                  pl.BlockSpec((None, 1, 3 * d), lambda i, j: (i // per_seq, 0, 0)),
                  pl.BlockSpec((d, tn), lambda i, j: (0, j0 + j)),
                  pl.BlockSpec((1, tn), lambda i, j: (0, j0 + j))],
        out_specs=out_specs, out_shape=out_shape,
        scratch_shapes=[pltpu.VMEM((tm, d), BF16)],
        compiler_params=_params("arbitrary", "arbitrary"))(x2, mod3, w_all, b_all)


def _slope(g, p, hh):
    head = 4 * g + 2 * p + hh
    return 2.0 ** (-ALIBI_MAX_EXP * (head + 1.0) / N_HEADS)


def _ld_rows(ref, start, n, stride):
    if stride == 1:
        return ref[pl.ds(start, n), :]
    return ref[pl.ds(start, n, stride=stride), :]


def _st_rows(ref, start, n, stride, val):
    if stride == 1:
        ref[pl.ds(start, n), :] = val
    else:
        ref[pl.ds(start, n, stride=stride), :] = val


def _sub_blocks(g, seq):
    return seq // DILATIONS[g] // SUB


def _key_rows(g, seq):
    return SUB if _sub_blocks(g, seq) == 1 else 2 * SUB


def _fill_bias(bias_ref, p, seq):
    for g in range(N_GROUPS):
        nk = _key_rows(g, seq)
        diff = lax.broadcasted_iota(jnp.int32, (SUB, nk), 0) - lax.broadcasted_iota(jnp.int32, (SUB, nk), 1)
        for i, off in enumerate((0, SUB)):
            if i == 1 and nk == SUB:
                continue
            delta = diff + off
            ok = (delta >= 0) & (delta <= SUB)
            dist = (delta * DILATIONS[g]).astype(F32)
            for hh in range(2):
                slope = jnp.where(p == 0, _slope(g, 0, hh), _slope(g, 1, hh))
                bias_ref[g, i, hh, :, 0:nk] = jnp.where(ok, -slope * dist, NEG)


def _to_sub_major(pa_ref, col, sub_ref, stage, dil, seq):
    cols = slice(col * SLAB, (col + 1) * SLAB)
    if dil == 1:
        sub_ref[...] = pa_ref[:, cols]
        return
    n = seq // dil
    stage[...] = pa_ref[:, cols].astype(F32)
    for r in range(dil):
        sub_ref[pl.ds(r * n, n), :] = stage[pl.ds(r, n, stride=dil), :].astype(BF16)


def _block_rows(it, g, seq):
    dil, nb = DILATIONS[g], _sub_blocks(g, seq)
    row0 = pl.multiple_of(it * SUB, SUB)
    if nb == 1:
        return row0, row0, 0, it
    blk = it % nb
    first = blk == 0
    krow0 = pl.multiple_of(row0 - jnp.where(first, 0, SUB), SUB)
    nat = row0 if dil == 1 else it // nb + dil * SUB * blk
    return row0, krow0, jnp.where(first, 0, 1), nat


def _nt(a, b):
    return lax.dot_general(a, b, (((1,), (1,)), ((), ())), preferred_element_type=F32)


def _tn(a, b):
    return lax.dot_general(a, b, (((0,), (0,)), ((), ())), preferred_element_type=F32)


def _head_sums(t):
    rows = t.shape[0]
    lo = jnp.broadcast_to(jnp.sum(t[:, :HEAD_DIM], axis=-1, keepdims=True), (rows, HEAD_DIM))
    hi = jnp.broadcast_to(jnp.sum(t[:, HEAD_DIM:], axis=-1, keepdims=True), (rows, HEAD_DIM))
    return jnp.concatenate([lo, hi], axis=-1)


def _attn_fwd(pa, bsz, seq):
    t = pa.shape[0]
    n_blocks = seq // SUB
    chunk = 256

    def body(pa_ref, o_ref, lse_ref, a_ref, sub, stage, bias_ref, s_buf, p_buf, l_buf):
        p = pl.program_id(1)
        _fill_bias(bias_ref, p, seq)
        head0 = lax.broadcasted_iota(jnp.int32, (SUB, SLAB), 1) < HEAD_DIM
        for g in range(N_GROUPS):
            dil = DILATIONS[g]
            for w in range(3):
                _to_sub_major(pa_ref, 3 * w + g, sub.at[w], stage, dil, seq)
            nk = _key_rows(g, seq)

            def trip(i, carry, g=g, dil=dil, nk=nk):
                places = [_block_rows(BLOCKS_PER_TRIP * i + j, g, seq) for j in range(BLOCKS_PER_TRIP)]
                for j, (row0, krow0, _, _) in enumerate(places):
                    q = sub[0, pl.ds(row0, SUB), :]
                    zero = jnp.zeros_like(q)
                    q2 = jnp.concatenate([jnp.where(head0, q, zero), jnp.where(head0, zero, q)], axis=0) * (HEAD_DIM ** -0.5)
                    s_buf[j, :, 0:nk] = _nt(q2, sub[1, pl.ds(krow0, nk), :])
                for j, (_, _, bi, _) in enumerate(places):
                    for c in range(0, 2 * SUB, SOFTMAX_ROWS):
                        hh, r = divmod(c, SUB)
                        s = s_buf[j, c:c + SOFTMAX_ROWS, 0:nk] + bias_ref[g, bi, hh, r:r + SOFTMAX_ROWS, 0:nk]
                        m = jnp.max(s, axis=-1, keepdims=True)
                        e = jnp.exp(s - m)
                        den = jnp.sum(e, axis=-1, keepdims=True)
                        p_buf[j, c:c + SOFTMAX_ROWS, 0:nk] = (e * (1.0 / den)).astype(BF16)
                        l_buf[j, c:c + SOFTMAX_ROWS, :] = jnp.broadcast_to(m + jnp.log(den), (SOFTMAX_ROWS, SLAB))
                for j, (_, krow0, _, nat) in enumerate(places):
                    o2 = jnp.dot(p_buf[j, :, 0:nk], sub[2, pl.ds(krow0, nk), :], preferred_element_type=F32)
                    _st_rows(o_ref.at[g], nat, SUB, dil, jnp.where(head0, o2[0:SUB], o2[SUB:2 * SUB]))
                    _st_rows(lse_ref.at[g], nat, SUB, dil, jnp.where(head0, l_buf[j, 0:SUB, :], l_buf[j, SUB:2 * SUB, :]))
                return carry

            lax.fori_loop(0, n_blocks // BLOCKS_PER_TRIP, trip, 0)

        def mix(i, carry):
            rows = pl.ds(pl.multiple_of(i * chunk, chunk), chunk)
            l0, l1, l2 = lse_ref[0, rows, :], lse_ref[1, rows, :], lse_ref[2, rows, :]
            m = jnp.maximum(jnp.maximum(l0, l1), l2)
            e0, e1, e2 = jnp.exp(l0 - m), jnp.exp(l1 - m), jnp.exp(l2 - m)
            tot = e0 + e1 + e2
            o = (e0 / tot) * o_ref[0, rows, :] + (e1 / tot) * o_ref[1, rows, :] + (e2 / tot) * o_ref[2, rows, :]
            z = pa_ref[rows, 9 * SLAB:10 * SLAB].astype(F32)
            a_ref[rows, :] = (o * (z * _sigmoid(z))).astype(BF16)
            return carry

        lax.fori_loop(0, seq // chunk, mix, 0)

    big = jax.ShapeDtypeStruct((N_GROUPS, t, 2 * SLAB), F32)
    return pl.pallas_call(
        body, name="attn_fwd", grid=(bsz, 2),
        in_specs=[pl.BlockSpec((seq, PAIR_COLS), lambda b, p: (b, p))],
        out_specs=[pl.BlockSpec((N_GROUPS, seq, SLAB), lambda b, p: (0, b, p)),
                   pl.BlockSpec((N_GROUPS, seq, SLAB), lambda b, p: (0, b, p)),
                   pl.BlockSpec((seq, SLAB), lambda b, p: (b, p))],
        out_shape=[big, big, jax.ShapeDtypeStruct((t, 2 * SLAB), BF16)],
        scratch_shapes=[pltpu.VMEM((3, seq, SLAB), BF16), pltpu.VMEM((seq, SLAB), F32),
                        pltpu.VMEM((N_GROUPS, 2, 2, SUB, 2 * SUB), F32), pltpu.VMEM((BLOCKS_PER_TRIP, 2 * SUB, 2 * SUB), F32),
                        pltpu.VMEM((BLOCKS_PER_TRIP, 2 * SUB, 2 * SUB), BF16), pltpu.VMEM((BLOCKS_PER_TRIP, 2 * SUB, SLAB), F32)],
        compiler_params=_params("arbitrary", "arbitrary"))(pa)


def _shift_down(v, k):
    rows = lax.broadcasted_iota(jnp.int32, v.shape, 0)
    return jnp.where(rows >= k, pltpu.roll(v, k, 0), 0.0)


def _shift_up(v, k):
    n = v.shape[0]
    rows = lax.broadcasted_iota(jnp.int32, v.shape, 0)
    return jnp.where(rows < n - k, pltpu.roll(v, n - k, 0), 0.0)


def _conv_fwd(pr, conv_w, bsz, seq, d):
    t = pr.shape[0]
    ct = CONV_TILE

    def body(p_ref, cw_ref, o_ref):
        u = p_ref[:, 2 * ct:3 * ct].astype(F32) * p_ref[:, 0:ct].astype(F32)
        cw = cw_ref[...]
        conv = cw[0:1, :] * _shift_down(u, 2)
        conv = conv + cw[1:2, :] * _shift_down(u, 1)
        conv = conv + cw[2:3, :] * u
        z = p_ref[:, 3 * ct:4 * ct].astype(F32)
        o_ref[...] = (p_ref[:, ct:2 * ct].astype(F32) * conv * (z * _sigmoid(z))).astype(BF16)

    return pl.pallas_call(
        body, name="conv_fwd", grid=(bsz, d // ct),
        in_specs=[pl.BlockSpec((seq, 4 * ct), lambda b, j: (b, j)), pl.BlockSpec((3, ct), lambda b, j: (0, j))],
        out_specs=pl.BlockSpec((seq, ct), lambda b, j: (b, j)),
        out_shape=jax.ShapeDtypeStruct((t, d), BF16), compiler_params=_params("parallel", "parallel"))(pr, conv_w)


def _tail(a_in, b_in, pr, x2, target2, mod3, w_pa, w_pc, w_out, b_out, ln_g, ln_b, seq, lay):
    t, d = x2.shape
    tm = 256
    per_seq = seq // tm
    n_steps = t // tm
    gate_blk = 4 * d // d

    def nt(a, b):
        return lax.dot_general(a, b, (((1,), (1,)), ((), ())), preferred_element_type=F32)

    def tn(a, b):
        return lax.dot_general(a, b, (((0,), (0,)), ((), ())), preferred_element_type=F32)

    def body(a_ref, b_ref, ga_ref, gb_ref, x_ref, tg_ref, mod_ref, wpa_ref, wpc_ref, wo_ref, bo_ref, lg_ref, lb_ref,
             dpg_ref, da_ref, db_ref, gx_ref, dgate_ref, small_ref, gwpa_hbm, gwpc_hbm, gwo_hbm,
             acc_pa, acc_pc, acc_o, sem):
        i = pl.program_id(0)

        @pl.when(i == 0)
        def _():
            acc_pa[...] = jnp.zeros_like(acc_pa)
            acc_pc[...] = jnp.zeros_like(acc_pc)
            acc_o[...] = jnp.zeros_like(acc_o)
            small_ref[...] = jnp.zeros_like(small_ref)

        @pl.when(i % per_seq == 0)
        def _():
            dgate_ref[...] = jnp.zeros_like(dgate_ref)

        a_bf, b_bf = a_ref[...], b_ref[...]
        y_attn = jnp.dot(a_bf, wpa_ref[...], preferred_element_type=F32)
        y_conv = jnp.dot(b_bf, wpc_ref[...], preferred_element_type=F32)
        sa, sb = _sigmoid(ga_ref[...].astype(F32)), _sigmoid(gb_ref[...].astype(F32))
        merged = (sa * y_attn + sb * y_conv).astype(BF16)
        mo = jnp.dot(merged, wo_ref[...], preferred_element_type=F32) + bo_ref[...]
        gate = mod_ref[:, 2 * d:3 * d]
        r = ALPHA * x_ref[...] + gate * mo
        mu = jnp.mean(r, axis=-1, keepdims=True)
        cen = r - mu
        var = jnp.mean(cen * cen, axis=-1, keepdims=True)
        rstd = lax.rsqrt(var + LN_EPS)
        xhat = cen * rstd
        err = xhat * lg_ref[...] + lb_ref[...] - tg_ref[...]
        dy = err * (1.0 / d)
        dxhat = dy * lg_ref[...]
        dr = rstd * (dxhat - jnp.mean(dxhat, axis=-1, keepdims=True) - xhat * jnp.mean(dxhat * xhat, axis=-1, keepdims=True))
        gx_ref[...] = ALPHA * dr
        dgate_ref[...] += jnp.sum(dr * mo, axis=0, keepdims=True)
        d_mo = dr * gate
        small_ref[0:1, :] += jnp.sum(d_mo, axis=0, keepdims=True)
        small_ref[1:2, :] += jnp.sum(dy * xhat, axis=0, keepdims=True)
        small_ref[2:3, :] += jnp.sum(dy, axis=0, keepdims=True)
        small_ref[3:4, :] += jnp.sum(err * err, axis=0, keepdims=True)
        d_mo_bf = d_mo.astype(BF16)
        acc_o[...] += tn(merged, d_mo_bf)
        dmerged = nt(d_mo_bf, wo_ref[...])
        dy_attn = (dmerged * sa).astype(BF16)
        dy_conv = (dmerged * sb).astype(BF16)
        dpg_ref[:, 0:d] = (dmerged * y_attn * sa * (1.0 - sa)).astype(BF16)
        dpg_ref[:, d:2 * d] = (dmerged * y_conv * sb * (1.0 - sb)).astype(BF16)
        acc_pa[...] += tn(a_bf, dy_attn)
        acc_pc[...] += tn(b_bf, dy_conv)
        da_ref[...] = nt(dy_attn, wpa_ref[...])
        db_ref[...] = nt(dy_conv, wpc_ref[...])

        @pl.when(i == n_steps - 1)
        def _():
            copies = [pltpu.make_async_copy(acc_pa, gwpa_hbm, sem.at[0]), pltpu.make_async_copy(acc_pc, gwpc_hbm, sem.at[1]),
                      pltpu.make_async_copy(acc_o, gwo_hbm, sem.at[2])]
            for cp in copies:
                cp.start()
            for cp in copies:
                cp.wait()

    row = lambda w: pl.BlockSpec((tm, w), lambda i: (i, 0))
    const = lambda shp: pl.BlockSpec(shp, lambda i: (0,) * len(shp), pipeline_mode=pl.Buffered(1))
    any_spec = pl.BlockSpec(memory_space=pl.ANY)
    return pl.pallas_call(
        body, name="tail", grid=(n_steps,),
        in_specs=[row(Z_WIDTH), row(d),
                  pl.BlockSpec((tm, d), lambda i: (i, gate_blk)), pl.BlockSpec((tm, d), lambda i: (i, gate_blk + 1)),
                  row(d), row(d), pl.BlockSpec((None, 1, 3 * d), lambda i: (i // per_seq, 0, 0)),
                  const((Z_WIDTH, d)), const((d, d)), const((d, d)), const((1, d)), const((1, d)), const((1, d))],
        out_specs=[pl.BlockSpec((tm, 2 * d), lambda i: (i, lay.g0 // (2 * d))), row(Z_WIDTH), row(d), row(d),
                   pl.BlockSpec((None, 1, d), lambda i: (i // per_seq, 0, 0)), pl.BlockSpec((8, d), lambda i: (0, 0)),
                   any_spec, any_spec, any_spec],
        out_shape=[jax.ShapeDtypeStruct((t, lay.np), BF16), jax.ShapeDtypeStruct((t, Z_WIDTH), F32),
                   jax.ShapeDtypeStruct((t, d), F32), jax.ShapeDtypeStruct((t, d), F32),
                   jax.ShapeDtypeStruct((t // seq, 1, d), F32), jax.ShapeDtypeStruct((8, d), F32),
                   jax.ShapeDtypeStruct((Z_WIDTH, d), F32), jax.ShapeDtypeStruct((d, d), F32),
                   jax.ShapeDtypeStruct((d, d), F32)],
        scratch_shapes=[pltpu.VMEM((Z_WIDTH, d), F32), pltpu.VMEM((d, d), F32), pltpu.VMEM((d, d), F32),
                        pltpu.SemaphoreType.DMA((3,))],
        compiler_params=_params("arbitrary"),
    )(a_in, b_in, pr, pr, x2, target2, mod3, w_pa, w_pc, w_out, b_out, ln_g, ln_b)


def _conv_bwd(dproj, db, pr, conv_w, bsz, seq, lay):
    d = lay.d
    ct = CONV_TILE
    base = lay.c0 // (4 * ct)

    def body(dp_in, db_ref, p_ref, cw_ref, dp_ref, gcw_ref):
        del dp_in
        u_x, g_b, g_c, z = [p_ref[:, k * ct:(k + 1) * ct].astype(F32) for k in range(4)]
        cw = cw_ref[...]
        u = g_c * u_x
        u1, u2 = _shift_down(u, 1), _shift_down(u, 2)
        conv = cw[0:1, :] * u2 + cw[1:2, :] * u1 + cw[2:3, :] * u
        sig = _sigmoid(z)
        sl = z * sig
        dbv = db_ref[...]
        gbc = g_b * conv
        dp_ref[:, ct:2 * ct] = (dbv * sl * conv).astype(BF16)
        dp_ref[:, 3 * ct:4 * ct] = (dbv * gbc * (sig * (1.0 + z * (1.0 - sig)))).astype(BF16)
        dconv = dbv * sl * g_b

        @pl.when(pl.program_id(1) == 0)
        def _():
            gcw_ref[...] = jnp.zeros_like(gcw_ref)

        gcw_ref[0:1, :] += jnp.sum(dconv * u2, axis=0, keepdims=True)
        gcw_ref[1:2, :] += jnp.sum(dconv * u1, axis=0, keepdims=True)
        gcw_ref[2:3, :] += jnp.sum(dconv * u, axis=0, keepdims=True)
        du = cw[2:3, :] * dconv + cw[1:2, :] * _shift_up(dconv, 1) + cw[0:1, :] * _shift_up(dconv, 2)
        dp_ref[:, 0:ct] = (du * g_c).astype(BF16)
        dp_ref[:, 2 * ct:3 * ct] = (du * u_x).astype(BF16)

    return pl.pallas_call(
        body, name="conv_bwd", grid=(d // ct, bsz),
        in_specs=[pl.BlockSpec(memory_space=pl.ANY), pl.BlockSpec((seq, ct), lambda j, b: (b, j)),
                  pl.BlockSpec((seq, 4 * ct), lambda j, b: (b, j)), pl.BlockSpec((3, ct), lambda j, b: (0, j))],
        out_specs=[pl.BlockSpec((seq, 4 * ct), lambda j, b: (b, base + j)), pl.BlockSpec((8, ct), lambda j, b: (0, j))],
        out_shape=[jax.ShapeDtypeStruct(dproj.shape, BF16), jax.ShapeDtypeStruct((8, d), F32)],
        input_output_aliases={0: 0}, compiler_params=_params("arbitrary", "arbitrary"))(dproj, db, pr, conv_w)


def _attn_bwd(dproj, pa, o_all, lse_all, da, bsz, seq):
    n_blocks = seq // SUB
    chunk = 256

    def body(dp_in, pa_ref, o_ref, lse_ref, da_ref, dp_ref, sub, stage, dsub, dog, cvec, bias_ref,
             s_buf, dp_buf, ds_buf, pb_buf, q2_buf, do2_buf, l_buf, c_buf):
        del dp_in
        p = pl.program_id(1)
        _fill_bias(bias_ref, p, seq)
        head0 = lax.broadcasted_iota(jnp.int32, (SUB, SLAB), 1) < HEAD_DIM

        def mix_bwd(i, carry):
            rows = pl.ds(pl.multiple_of(i * chunk, chunk), chunk)
            ls = [lse_ref[g, rows, :] for g in range(N_GROUPS)]
            os_ = [o_ref[g, rows, :] for g in range(N_GROUPS)]
            m = jnp.maximum(jnp.maximum(ls[0], ls[1]), ls[2])
            es = [jnp.exp(l - m) for l in ls]
            tot = es[0] + es[1] + es[2]
            ws = [e / tot for e in es]
            o = ws[0] * os_[0] + ws[1] * os_[1] + ws[2] * os_[2]
            z = pa_ref[rows, 9 * SLAB:10 * SLAB].astype(F32)
            sig = _sigmoid(z)
            dav = da_ref[rows, :]
            do = dav * (z * sig)
            dp_ref[rows, 9 * SLAB:10 * SLAB] = (dav * o * (sig * (1.0 + z * (1.0 - sig)))).astype(BF16)
            wsum = _head_sums(do * o)
            for g in range(N_GROUPS):
                dog[g, rows, :] = ws[g] * do
                cvec[g, rows, :] = -(ws[g] * wsum)
            return carry

        lax.fori_loop(0, seq // chunk, mix_bwd, 0)

        for g in range(N_GROUPS):
            dil = DILATIONS[g]
            for w in range(3):
                _to_sub_major(pa_ref, 3 * w + g, sub.at[w], stage, dil, seq)
            dsub[1] = jnp.zeros((seq, SLAB), F32)
            dsub[2] = jnp.zeros((seq, SLAB), F32)
            nk = _key_rows(g, seq)

            def trip(i, carry, g=g, dil=dil, nk=nk):
                places = [_block_rows(BLOCKS_PER_TRIP * i + j, g, seq) for j in range(BLOCKS_PER_TRIP)]
                for j, (row0, krow0, _, nat) in enumerate(places):
                    q = sub[0, pl.ds(row0, SUB), :]
                    do = _ld_rows(dog.at[g], nat, SUB, dil).astype(BF16)
                    zero = jnp.zeros_like(q)
                    q2 = jnp.concatenate([jnp.where(head0, q, zero), jnp.where(head0, zero, q)], axis=0)
                    do2 = jnp.concatenate([jnp.where(head0, do, zero), jnp.where(head0, zero, do)], axis=0)
                    q2_buf[j] = q2
                    do2_buf[j] = do2
                    s_buf[j, :, 0:nk] = _nt(q2 * (HEAD_DIM ** -0.5), sub[1, pl.ds(krow0, nk), :])
                    dp_buf[j, :, 0:nk] = _nt(do2, sub[2, pl.ds(krow0, nk), :])
                    l_buf[j] = _ld_rows(lse_ref.at[g], nat, SUB, dil)
                    c_buf[j] = _ld_rows(cvec.at[g], nat, SUB, dil)
                for j, (_, _, bi, _) in enumerate(places):
                    for c in range(0, 2 * SUB, SOFTMAX_ROWS):
                        hh, r = divmod(c, SUB)
                        lane = hh * HEAD_DIM
                        s = s_buf[j, c:c + SOFTMAX_ROWS, 0:nk] + bias_ref[g, bi, hh, r:r + SOFTMAX_ROWS, 0:nk]
                        prob = jnp.exp(s - l_buf[j, r:r + SOFTMAX_ROWS, lane:lane + 1])
                        dprob = dp_buf[j, c:c + SOFTMAX_ROWS, 0:nk] + c_buf[j, r:r + SOFTMAX_ROWS, lane:lane + 1]
                        ds_buf[j, c:c + SOFTMAX_ROWS, 0:nk] = (prob * dprob * (HEAD_DIM ** -0.5)).astype(BF16)
                        pb_buf[j, c:c + SOFTMAX_ROWS, 0:nk] = prob.astype(BF16)
                for j, (row0, krow0, _, _) in enumerate(places):
                    ds = ds_buf[j, :, 0:nk]
                    dq2 = jnp.dot(ds, sub[1, pl.ds(krow0, nk), :], preferred_element_type=F32)
                    dsub[0, pl.ds(row0, SUB), :] = jnp.where(head0, dq2[0:SUB], dq2[SUB:2 * SUB])
                    dsub[1, pl.ds(krow0, nk), :] += _tn(ds, q2_buf[j])
                    dsub[2, pl.ds(krow0, nk), :] += _tn(pb_buf[j, :, 0:nk], do2_buf[j])
                return carry

            lax.fori_loop(0, n_blocks // BLOCKS_PER_TRIP, trip, 0)
            for w in range(3):
                cols = slice((3 * w + g) * SLAB, (3 * w + g + 1) * SLAB)
                if dil == 1:
                    dp_ref[:, cols] = dsub[w].astype(BF16)
                else:
                    n = seq // dil
                    for r in range(dil):
                        stage[pl.ds(r, n, stride=dil), :] = dsub[w, pl.ds(r * n, n), :]
                    dp_ref[:, cols] = stage[...].astype(BF16)

    return pl.pallas_call(
        body, name="attn_bwd", grid=(bsz, 2),
        in_specs=[pl.BlockSpec(memory_space=pl.ANY), pl.BlockSpec((seq, PAIR_COLS), lambda b, p: (b, p)),
                  pl.BlockSpec((N_GROUPS, seq, SLAB), lambda b, p: (0, b, p)),
                  pl.BlockSpec((N_GROUPS, seq, SLAB), lambda b, p: (0, b, p)),
                  pl.BlockSpec((seq, SLAB), lambda b, p: (b, p))],
        out_specs=pl.BlockSpec((seq, PAIR_COLS), lambda b, p: (b, p)),
        out_shape=jax.ShapeDtypeStruct(dproj.shape, BF16), input_output_aliases={0: 0},
        scratch_shapes=[pltpu.VMEM((3, seq, SLAB), BF16), pltpu.VMEM((seq, SLAB), F32), pltpu.VMEM((3, seq, SLAB), F32),
                        pltpu.VMEM((3, seq, SLAB), F32), pltpu.VMEM((3, seq, SLAB), F32),
                        pltpu.VMEM((N_GROUPS, 2, 2, SUB, 2 * SUB), F32),
                        pltpu.VMEM((BLOCKS_PER_TRIP, 2 * SUB, 2 * SUB), F32), pltpu.VMEM((BLOCKS_PER_TRIP, 2 * SUB, 2 * SUB), F32),
                        pltpu.VMEM((BLOCKS_PER_TRIP, 2 * SUB, 2 * SUB), BF16), pltpu.VMEM((BLOCKS_PER_TRIP, 2 * SUB, 2 * SUB), BF16),
                        pltpu.VMEM((BLOCKS_PER_TRIP, 2 * SUB, SLAB), BF16), pltpu.VMEM((BLOCKS_PER_TRIP, 2 * SUB, SLAB), BF16),
                        pltpu.VMEM((BLOCKS_PER_TRIP, SUB, SLAB), F32), pltpu.VMEM((BLOCKS_PER_TRIP, SUB, SLAB), F32)],
        compiler_params=_params("arbitrary", "arbitrary"))(dproj, pa, o_all, lse_all, da)


def _grad_h(dproj, w_all, gx0, x2, mod3, seq, lay):
    t, d = x2.shape
    tm, tn = min(512, seq), min(512, d)
    per_seq = seq // tm

    def body(dp_ref, w_ref, gx0_ref, x_ref, scale_ref, gx_ref, dmod_ref):
        dh = _nt(dp_ref[:, 0:ATT], w_ref[:, 0:ATT]) + _nt(dp_ref[:, lay.c0:], w_ref[:, lay.c0:])
        gx_ref[...] = gx0_ref[...] + dh * (1.0 + scale_ref[...])

        @pl.when(pl.program_id(1) % per_seq == 0)
        def _():
            dmod_ref[...] = jnp.zeros_like(dmod_ref)

        dmod_ref[0:1, :] += jnp.sum(dh, axis=0, keepdims=True)
        dmod_ref[1:2, :] += jnp.sum(dh * x_ref[...], axis=0, keepdims=True)

    tile = pl.BlockSpec((tm, tn), lambda j, i: (i, j))
    return pl.pallas_call(
        body, name="grad_h", grid=(d // tn, t // tm),
        in_specs=[pl.BlockSpec((tm, lay.np), lambda j, i: (i, 0)), pl.BlockSpec((tn, lay.np), lambda j, i: (j, 0)), tile, tile,
                  pl.BlockSpec((None, 1, tn), lambda j, i: (i // per_seq, 0, d // tn + j))],
        out_specs=[tile, pl.BlockSpec((None, 8, tn), lambda j, i: (i // per_seq, 0, j))],
        out_shape=[jax.ShapeDtypeStruct((t, d), F32), jax.ShapeDtypeStruct((t // seq, 8, d), F32)],
        compiler_params=_params("arbitrary", "arbitrary"))(dproj, w_all, gx0, x2, mod3)


def _grad_w_in(ht, dproj, seq, lay):
    d, t = ht.shape
    tm = seq
    n_i = t // tm

    def make_body(n_skip, n_pieces):
        def body(*refs):
            refs = refs[n_skip:]
            ht_ref, dp_refs = refs[0], refs[1:1 + n_pieces]
            gw_ref, gb_ref, acc, bacc = refs[1 + n_pieces:]
            i = pl.program_id(1)

            @pl.when(i == 0)
            def _():
                acc[...] = jnp.zeros_like(acc)
                bacc[...] = jnp.zeros_like(bacc)

            dp = dp_refs[0][...] if n_pieces == 1 else jnp.concatenate([r[...] for r in dp_refs], axis=1)
            acc[...] += jnp.dot(ht_ref[...], dp, preferred_element_type=F32)
            bacc[...] += jnp.sum(dp.astype(F32), axis=0, keepdims=True)

            @pl.when(i == n_i - 1)
            def _():
                gw_ref[...] = acc[...].astype(BF16)
                gb_ref[...] = bacc[...]
        return body

    def call(name, pieces, n_tiles, nat_tile, prev):
        tn = sum(w for w, _ in pieces)
        in_specs = [pl.BlockSpec((d, tm), lambda j, i: (0, i))]
        in_specs += [pl.BlockSpec((tm, w), lambda j, i, f=f: (i, f(j))) for w, f in pieces]
        args = [ht] + [dproj] * len(pieces)
        aliases = {}
        if prev is not None:
            in_specs = [pl.BlockSpec(memory_space=pl.ANY)] * 2 + in_specs
            args = list(prev) + args
            aliases = {0: 0, 1: 1}
        return pl.pallas_call(
            make_body(0 if prev is None else 2, len(pieces)), name=name, grid=(n_tiles, n_i), in_specs=in_specs,
            out_specs=[pl.BlockSpec((d, tn), lambda j, i: (0, nat_tile(j))), pl.BlockSpec((1, tn), lambda j, i: (0, nat_tile(j)))],
            out_shape=[jax.ShapeDtypeStruct((d, lay.din), BF16), jax.ShapeDtypeStruct((1, lay.din), F32)],
            input_output_aliases=aliases,
            scratch_shapes=[pltpu.VMEM((d, tn), F32), pltpu.VMEM((1, tn), F32)],
            compiler_params=_params("arbitrary", "arbitrary"))(*args)

    attn_pieces = [(SLAB, lambda j, m=m: (m % 2) * PAIR_SLABS + 2 * j + m // 2) for m in range(4)]
    first = call("grad_w_in_attn", attn_pieces, ATT // 512, lambda j: j, None)
    base = lay.c0 // CONV_TILE
    nct = lay.n_conv_tiles
    if nct % 2:
        return call("grad_w_in_rest", [(CONV_TILE, lambda j: base + j)], 6 * d // CONV_TILE, lay.rest_nat_tile, first)
    half = nct // 2

    def rest_piece(m):
        def perm_tile(j):
            conv = base + 4 * (2 * (j % half) + m) + j // half
            return jnp.where(j < 4 * half, conv, base + 2 * j + m)
        return (CONV_TILE, perm_tile)

    return call("grad_w_in_rest", [rest_piece(0), rest_piece(1)], 6 * d // 512, lambda j: ATT // 512 + j, first)


def _pack_rows(parts, width=128):
    flat = [p.reshape(-1) for p in parts]
    spans, rows = [], 0
    padded = []
    for f in flat:
        n = -(-f.shape[0] // (8 * width)) * 8
        padded.append(jnp.pad(f, (0, n * width - f.shape[0])).reshape(n, width))
        spans.append((rows, f.shape[0]))
        rows += n
    return jnp.concatenate(padded, axis=0), spans


def _unpack_rows(packed, spans, shapes, width=128):
    out = []
    for (row, n), shp in zip(spans, shapes):
        rows = -(-n // width)
        out.append(packed[row:row + rows].reshape(-1)[:n].reshape(shp))
    return out


def kernel(x, c, w_ada, b_ada, w_in, b_in, conv_w, w_proj_attn, w_proj_conv, w_out, b_out, ln_g, ln_b, loss_target, m_w_ada, m_b_ada, m_w_in, m_b_in, m_conv_w, m_w_proj_attn, m_w_proj_conv, m_w_out, m_b_out, m_ln_g, m_ln_b, v_w_ada, v_b_ada, v_w_in, v_b_in, v_conv_w, v_w_proj_attn, v_w_proj_conv, v_w_out, v_b_out, v_ln_g, v_ln_b):
    bsz, seq, d = x.shape
    t = bsz * seq
    lay = _Layout(d)
    col_sharded = [True, True, False, False]
    red_w = [w_in[0], w_proj_attn[0], w_proj_conv[0], w_out[0]]
    chip = 2 * lax.axis_index("x") + lax.axis_index("y")
    chip1 = chip.astype(jnp.int32).reshape(1)
    core1 = lax.axis_index("c").astype(jnp.int32).reshape(1)
    place = jnp.stack([chip, lax.axis_index("c")]).astype(jnp.int32)
    x2 = x.reshape(t, d)
    target2 = loss_target.reshape(t, d)

    mod, act_all = _ada_exchange(c, w_ada[0], b_ada)
    mod3 = mod.reshape(bsz, 1, 3 * d)

    cw_pad = jnp.pad(conv_w[0], ((0, 5), (0, 0))) + 0.0 * mod[0, 0]
    own_in_full = [_cast_into_full(red_w[w], col_sharded[w], chip1, f"cast_shard_{w}") for w in range(4)]
    (wi_f,), cw8 = _gather_weights(own_in_full[:1], col_sharded[:1], cw_pad)
    cw_full = cw8[0:3]
    late_copies = _direct_gather_copies(col_sharded[1:])
    late_send, late_recv, late_flying, late_token = _start_copies("gather_late_start", own_in_full[1:], (18,), cw8, late_copies)
    w_all = _permute_w_in(wi_f, lay)
    b_all = lay.perm_vector(b_in) + late_token

    rest_tn = 1024 if (6 * d) % 1024 == 0 else 512
    pa, = _project(x2, mod3, w_all, b_all, seq, 0, ATT, PAIR_COLS, BF16, False, "project_attn")
    pr, ht = _project(x2, mod3, w_all, b_all, seq, lay.c0, 6 * d, rest_tn, BF16, True, "project_rest")
    o_all, lse_all, a_in = _attn_fwd(pa, bsz, seq)
    b_in_act = _conv_fwd(pr, cw_full, bsz, seq, d)
    wpa_f, wpc_f, wo_f = _wait_copies("gather_late_wait", late_flying, late_send, late_recv, b_in_act, late_copies)
    (dproj, da_in, db_in, gx0, dgate, small_tail, gw_pa, gw_pc, gw_out) = _tail(
        a_in, b_in_act, pr, x2, target2, mod3, wpa_f, wpc_f, wo_f, b_out, ln_g, ln_b, seq, lay)

    def mixers_backward(token):
        dp, gcw_ = _conv_bwd(dproj, db_in, pr, cw_full + token, bsz, seq, lay)
        dp = _attn_bwd(dp, pa, o_all, lse_all, da_in, bsz, seq)
        gw_in_bf_, gb_in_ = _grad_w_in(ht, dp, seq, lay)
        return (dp, gcw_, gw_in_bf_, gb_in_), gw_in_bf_

    late_views = _shard_views([gw_pa, gw_pc, gw_out], col_sharded[1:])
    late_got = _pair_exchange_halves(late_views, "grads_pair_exchange_late")
    late_parts = [_pair_sum(late_views[w], late_got[w], core1, f"grads_pair_sum_late_{w}") for w in range(3)]
    late_red, (dproj, gcw, gw_in_bf, gb_in) = _reduce_over_chips(
        late_parts, col_sharded[1:], place, late_parts[0], mixers_backward, "late")

    def input_backward(token):
        gx, dm = _grad_h(dproj, w_all, gx0, x2, mod3 + token, seq, lay)
        return (gx, dm), gx

    in_view = _shard_views([gw_in_bf], col_sharded[:1])
    in_got = _pair_exchange_halves(in_view, "grads_pair_exchange_in")
    in_part = _pair_sum(in_view[0], in_got[0], core1, "grads_pair_sum_in")
    in_red, (grad_x2, dmod) = _reduce_over_chips([in_part], col_sharded[:1], place, late_red[0], input_backward, "in")
    g_red = _pair_join_halves(in_red + late_red)

    d_ada = jnp.concatenate([dmod[:, 0, :], dmod[:, 1, :], dgate[:, 0, :]], axis=1)
    pieces = [small_tail[3], jnp.sum(d_ada, axis=0), gb_in[0], small_tail[0], small_tail[1], small_tail[2], gcw[0:3]]
    packed, spans = _pack_rows(pieces)
    kept_in, _ = _pack_rows([d_ada])
    summed, kept = _all_sum_small(jnp.concatenate([packed, kept_in], axis=0), packed.shape[0], d)
    loss = summed[0, 0]
    _, g_b_ada, g_b_in, g_b_out, g_ln_g, g_ln_b, g_cw_full = _unpack_rows(
        summed, spans, [(d,), (3 * d,), (lay.din,), (d,), (d,), (d,), (3, d)])
    g_cw = lax.dynamic_slice(g_cw_full, (0, chip * (d // N_CHIPS)), (3, d // N_CHIPS))
    d_ada_all = kept.reshape(N_DEV, -1)[:, :bsz * 3 * d].reshape(N_DEV * bsz, 3 * d)
    ada_cols = 3 * d // N_CHIPS
    g_w_ada = _grad_w_ada_cols(act_all.T, lax.dynamic_slice(d_ada_all, (0, chip * ada_cols), (N_DEV * bsz, ada_cols)))

    big_w = [w_ada[0]] + red_w
    big_m = [m_w_ada[0], m_w_in[0], m_w_proj_attn[0], m_w_proj_conv[0], m_w_out[0]]
    big_v = [v_w_ada[0], v_w_in[0], v_w_proj_attn[0], v_w_proj_conv[0], v_w_out[0]]
    g_big = [g_w_ada] + g_red
    big_out = [_adamw(big_w[w], g_big[w], big_m[w], big_v[w], f"adamw_{w}") for w in range(5)]
    small_w = [b_ada, b_in, conv_w[0], b_out, ln_g, ln_b]
    small_g = [g_b_ada, g_b_in, g_cw, g_b_out, g_ln_g, g_ln_b]
    small_m = [m_b_ada, m_b_in, m_conv_w[0], m_b_out, m_ln_g, m_ln_b]
    small_v = [v_b_ada, v_b_in, v_conv_w[0], v_b_out, v_ln_g, v_ln_b]
    pw, sp = _pack_rows(small_w)
    pg, _ = _pack_rows(small_g)
    pm, _ = _pack_rows(small_m)
    pv, _ = _pack_rows(small_v)
    sd, sm, sv = _adamw(pw, pg, pm, pv, "adamw_small")
    shapes = [a.shape for a in small_w]
    sd, sm, sv = _unpack_rows(sd, sp, shapes), _unpack_rows(sm, sp, shapes), _unpack_rows(sv, sp, shapes)

    def order(wa, bA, wi, bI, cw, wpa, wpc, wo, bO, lg, lb):
        return (wa[None], bA, wi[None], bI, cw[None], wpa[None], wpc[None], wo[None], bO, lg, lb)

    sg = [g.reshape(s) for g, s in zip(small_g, shapes)]
    grads_out = order(g_big[0], sg[0], g_big[1], sg[1], sg[2], g_big[2], g_big[3], g_big[4], sg[3], sg[4], sg[5])
    outs = []
    for idx, small in enumerate((sd, sm, sv)):
        outs.append(order(big_out[0][idx], small[0], big_out[1][idx], small[1], small[2], big_out[2][idx],
                          big_out[3][idx], big_out[4][idx], small[3], small[4], small[5]))
    return (loss, grad_x2.reshape(bsz, seq, d), *grads_out, *outs[0], *outs[1], *outs[2])
```

```python
import functools

import jax
import jax.numpy as jnp
from jax import lax
from jax.experimental import pallas as pl
from jax.experimental.pallas import tpu as pltpu

F32 = jnp.float32
BF16 = jnp.bfloat16
MESH = pl.DeviceIdType.MESH

HEAD_DIM = 64
N_GROUPS = 3
DILATIONS = (1, 4, 16)
N_HEADS = 12
SUB = 128
Q_WIDTH = 768
Z_WIDTH = 256
ATT = 3 * Q_WIDTH + Z_WIDTH
SLAB = 128
PAIR_SLABS = 10
PAIR_COLS = PAIR_SLABS * SLAB
CONV_TILE = 256
SOFTMAX_ROWS = 32
BLOCKS_PER_TRIP = 8
ALIBI_MAX_EXP = 8.0
ALPHA = 2.0 ** 0.25
LN_EPS = 1e-5
ADAM_LR, ADAM_B1, ADAM_B2, ADAM_EPS, ADAM_WD, ADAM_STEP = 0.001, 0.9, 0.999, 1e-08, 0.01, 10
N_CHIPS = 4
N_DEV = 8
VMEM_LIMIT_V7X = 60 * 1024 * 1024
NEG = -1e30


def _params(*sem):
    return pltpu.CompilerParams(dimension_semantics=sem, vmem_limit_bytes=VMEM_LIMIT_V7X)


def _sigmoid(v):
    return 1.0 / (1.0 + jnp.exp(-v))


class _Layout:
    def __init__(self, d):
        self.d = d
        self.din = ATT + 6 * d
        c0 = 3072
        while c0 % (2 * d):
            c0 += 1024
        self.c0, self.g0, self.np = c0, c0 + 4 * d, c0 + 6 * d
        self.n_conv_tiles = d // CONV_TILE

    def attn_nat_slab(self, s):
        p, i = s // PAIR_SLABS, s % PAIR_SLABS
        return jnp.where(i < 9, (i // 3) * 6 + (i % 3) * 2 + p, 18 + p)

    def rest_nat_tile(self, t):
        n4 = 4 * self.n_conv_tiles
        conv = ATT // CONV_TILE + (t % 4) * self.n_conv_tiles + t // 4
        return jnp.where(t < n4, conv, ATT // CONV_TILE + t)

    def perm_vector(self, v):
        parts = []
        for s in range(2 * PAIR_SLABS):
            p, i = divmod(s, PAIR_SLABS)
            ns = (i // 3) * 6 + (i % 3) * 2 + p if i < 9 else 18 + p
            parts.append(v[:, ns * SLAB:(ns + 1) * SLAB])
        parts.append(jnp.zeros((1, self.c0 - ATT), v.dtype))
        for j in range(self.n_conv_tiles):
            for k in range(4):
                a = ATT + k * self.d + j * CONV_TILE
                parts.append(v[:, a:a + CONV_TILE])
        parts.append(v[:, ATT + 4 * self.d:])
        return jnp.concatenate(parts, axis=1)


def _place():
    return lax.axis_index("x"), lax.axis_index("y"), lax.axis_index("c")


def _other_chips(x, y):
    return [(1 - x, y), (x, 1 - y), (1 - x, 1 - y)]


def _shard_of(ref, col_sharded, chip, half=None):
    if col_sharded:
        cs = ref.shape[1] // N_CHIPS
        cols = pl.ds(pl.multiple_of(chip * cs, SLAB), cs)
        if half is None:
            return ref.at[:, cols]
        n = ref.shape[0] // 2
        return ref.at[pl.ds(half * n, n), cols]
    rs = ref.shape[0] // N_CHIPS
    if half is None:
        return ref.at[pl.ds(chip * rs, rs)]
    return ref.at[pl.ds(chip * rs + half * (rs // 2), rs // 2)]


GATHER_PIECES = 4


def _cast_into_full(shard, col_sharded, chip, name):
    rows, cols = shard.shape
    tr = _row_tile(rows, cols)
    nb = rows // tr

    def body(chip_ref, s_ref, o_ref):
        del chip_ref
        o_ref[...] = s_ref[...].astype(BF16)

    if col_sharded:
        full, out_spec = (rows, cols * N_CHIPS), pl.BlockSpec((tr, cols), lambda i, ch: (i, ch[0]))
    else:
        full, out_spec = (rows * N_CHIPS, cols), pl.BlockSpec((tr, cols), lambda i, ch: (ch[0] * nb + i, 0))
    return pl.pallas_call(
        body, name=name,
        grid_spec=pltpu.PrefetchScalarGridSpec(num_scalar_prefetch=1, grid=(nb,),
                                               in_specs=[pl.BlockSpec((tr, cols), lambda i, ch: (i, 0))], out_specs=out_spec),
        out_shape=jax.ShapeDtypeStruct(full, BF16), compiler_params=_params("parallel"))(chip, shard)


def _gather_weights(fulls, col_sharded, small):
    n = len(fulls)
    kp = GATHER_PIECES

    def piece(ref, cs, chip, half, k):
        if cs:
            width = ref.shape[1] // N_CHIPS
            rows = ref.shape[0] // 2 // kp
            return ref.at[pl.ds(half * (ref.shape[0] // 2) + k * rows, rows), pl.ds(pl.multiple_of(chip * width, SLAB), width)]
        rs = ref.shape[0] // N_CHIPS
        rows = rs // 2 // kp
        return ref.at[pl.ds(chip * rs + half * (rs // 2) + k * rows, rows)]

    def body(*refs):
        sm_in, outs, sm_out = refs[n], refs[n + 1:2 * n + 1], refs[2 * n + 1]
        send, recv, fsend, frecv, lsem, ssend, srecv = refs[2 * n + 2:]
        x, y, c = _place()
        mine = 2 * x + y
        sibling = (x, y, 1 - c)
        first = (x ^ (1 - c), y ^ c)
        second = (x ^ c, y ^ (1 - c))
        diagonal = (1 - x, 1 - y)
        sources = [first, second, diagonal]
        senders = [first, second, second]

        def copy(ref, sems, slot, to):
            return pltpu.make_async_remote_copy(src_ref=ref, dst_ref=ref, send_sem=sems[0].at[slot], recv_sem=sems[1].at[slot],
                                                device_id=to, device_id_type=MESH)

        local = pltpu.make_async_copy(sm_in, _shard_of(sm_out, True, mine), lsem)
        local.start()
        sends = []
        for k, (cx, cy) in enumerate(_other_chips(x, y)):
            cp = pltpu.make_async_remote_copy(src_ref=sm_in, dst_ref=_shard_of(sm_out, True, mine), send_sem=ssend.at[k],
                                              recv_sem=srecv.at[k], device_id=(cx, cy, c), device_id_type=MESH)
            cp.start()
            sends.append(cp)
        for k in range(kp):
            for w in range(n):
                own = piece(outs[w], col_sharded[w], mine, c, k)
                for slot, chip in enumerate((first, second)):
                    cp = copy(own, (send, recv), (w * 3 + slot) * kp + k, (*chip, c))
                    cp.start()
                    sends.append(cp)
        for slot in range(3):
            source = 2 * sources[slot][0] + sources[slot][1]
            for k in range(kp):
                for w in range(n):
                    landed = piece(outs[w], col_sharded[w], source, c, k)
                    copy(landed, (send, recv), (w * 3 + slot) * kp + k, (*senders[slot], c)).wait_recv()
                    if slot == 0:
                        cp = copy(landed, (send, recv), (w * 3 + 2) * kp + k, (*second, c))
                        cp.start()
                        sends.append(cp)
                    cp = copy(landed, (fsend, frecv), (w * 3 + slot) * kp + k, sibling)
                    cp.start()
                    sends.append(cp)
        for slot, chip in enumerate((second, first, diagonal)):
            for k in range(kp):
                for w in range(n):
                    passed = piece(outs[w], col_sharded[w], 2 * chip[0] + chip[1], 1 - c, k)
                    copy(passed, (fsend, frecv), (w * 3 + slot) * kp + k, sibling).wait_recv()
        for k, (cx, cy) in enumerate(_other_chips(x, y)):
            theirs = _shard_of(sm_out, True, 2 * cx + cy)
            pltpu.make_async_remote_copy(src_ref=theirs, dst_ref=theirs, send_sem=ssend.at[k], recv_sem=srecv.at[k],
                                         device_id=(cx, cy, c), device_id_type=MESH).wait_recv()
        for cp in sends:
            cp.wait_send()
        local.wait()

    any_spec = pl.BlockSpec(memory_space=pl.ANY)
    outs = pl.pallas_call(
        body, name="gather_weights",
        out_shape=[jax.ShapeDtypeStruct(f.shape, BF16) for f in fulls]
        + [jax.ShapeDtypeStruct((small.shape[0], small.shape[1] * N_CHIPS), small.dtype)],
        in_specs=[any_spec] * (n + 1), out_specs=[any_spec] * (n + 1), input_output_aliases={w: w for w in range(n)},
        scratch_shapes=[pltpu.SemaphoreType.DMA((n * 3 * kp,)), pltpu.SemaphoreType.DMA((n * 3 * kp,)),
                        pltpu.SemaphoreType.DMA((n * 3 * kp,)), pltpu.SemaphoreType.DMA((n * 3 * kp,)), pltpu.SemaphoreType.DMA,
                        pltpu.SemaphoreType.DMA((3,)), pltpu.SemaphoreType.DMA((3,))],
    )(*fulls, small)
    return outs[:n], outs[n]


HBM_SPEC = pl.BlockSpec(memory_space=pltpu.HBM)
SEM_SPEC = pl.BlockSpec(memory_space=pltpu.SEMAPHORE)
DATAFLOW = pltpu.SideEffectType.DATAFLOW_SIDE_EFFECTING


def _start_copies(name, arrays, sem_shape, after, copies):
    n = len(arrays)

    def body(*refs):
        for cp in copies(refs[:n], refs[n + 1], refs[n + 2]):
            cp.start()
        token = refs[2 * n + 3]
        token[...] = jnp.zeros_like(token)

    res = pl.pallas_call(
        body, name=name,
        out_shape=(pltpu.SemaphoreType.DMA(sem_shape), pltpu.SemaphoreType.DMA(sem_shape),
                   *[pltpu.HBM(a.shape, a.dtype) for a in arrays], jax.ShapeDtypeStruct((8, 128), F32)),
        in_specs=[HBM_SPEC] * n + [pl.BlockSpec(memory_space=pl.ANY)],
        out_specs=(SEM_SPEC, SEM_SPEC, *([HBM_SPEC] * n), pl.BlockSpec(memory_space=pltpu.VMEM)),
        input_output_aliases={i: 2 + i for i in range(n)},
        compiler_params=pltpu.CompilerParams(has_side_effects=DATAFLOW),
    )(*[pltpu.with_memory_space_constraint(a, pltpu.HBM) for a in arrays], after)
    return res[0], res[1], list(res[2:2 + n]), res[2 + n][0, 0]


def _wait_copies(name, arrays, send, recv, after, copies):
    n = len(arrays)

    def body(*refs):
        for cp in copies(refs[:n], refs[n], refs[n + 1]):
            cp.wait_send()
            cp.wait_recv()

    return pl.pallas_call(
        body, name=name, out_shape=[pltpu.HBM(a.shape, a.dtype) for a in arrays],
        in_specs=[HBM_SPEC] * n + [SEM_SPEC, SEM_SPEC, pl.BlockSpec(memory_space=pl.ANY)], out_specs=[HBM_SPEC] * n,
        input_output_aliases={i: i for i in range(n)},
        compiler_params=pltpu.CompilerParams(has_side_effects=DATAFLOW),
    )(*arrays, send, recv, after)


def _direct_gather_copies(col_sharded):
    def copies(refs, send, recv):
        x, y, c = _place()
        mine = 2 * x + y
        out = []
        for w, ref in enumerate(refs):
            own_half = _shard_of(ref, col_sharded[w], mine, c)
            k = 0
            for cx, cy in _other_chips(x, y):
                for pc in (c, 1 - c):
                    out.append(pltpu.make_async_remote_copy(
                        src_ref=own_half, dst_ref=own_half, send_sem=send.at[6 * w + k], recv_sem=recv.at[6 * w + k],
                        device_id=(cx, cy, pc), device_id_type=MESH))
                    k += 1
        return out
    return copies


def _chip_scatter_copies(n, col_sharded):
    def piece(ref, cs, chip):
        if cs:
            w = ref.shape[2] // N_CHIPS
            return ref.at[:, :, pl.ds(pl.multiple_of(chip * w, SLAB), w)]
        return ref.at[pl.ds(chip, 1)]

    def copies(refs, send, recv):
        x, y, c = _place()
        out = []
        for k, (cx, cy) in enumerate(_other_chips(x, y)):
            for w in range(n):
                out.append(pltpu.make_async_remote_copy(
                    src_ref=piece(refs[w], col_sharded[w], 2 * cx + cy), dst_ref=refs[n + w].at[pl.ds(k, 1)],
                    send_sem=send.at[3 * w + k], recv_sem=recv.at[3 * w + k], device_id=(cx, cy, c), device_id_type=MESH))
        return out
    return copies


def _shard_views(gs, col_sharded):
    return [g.reshape(1, *g.shape) if cs else g.reshape(N_CHIPS, g.shape[0] // N_CHIPS, g.shape[1])
            for g, cs in zip(gs, col_sharded)]


DMA_CHUNK_BYTES = 1 << 20


def _chunk_rows(shape, itemsize):
    s, rows, cols = shape
    n = 1
    while s * (rows // n) * cols * itemsize > DMA_CHUNK_BYTES and (rows // n) % 32 == 0:
        n *= 2
    return rows // n


def _send_rows(src, src_row0, dst, dst_row0, rows, send_sem, recv_sem, device):
    step = _chunk_rows((src.shape[0], rows, src.shape[2]), src.dtype.itemsize)
    for r in range(0, rows, step):
        pltpu.make_async_remote_copy(src_ref=src.at[:, pl.ds(src_row0 + r, step)], dst_ref=dst.at[:, pl.ds(dst_row0 + r, step)],
                                     send_sem=send_sem, recv_sem=recv_sem, device_id=device, device_id_type=MESH).start()


def _pair_exchange_halves(views, name):
    n = len(views)
    half_shapes = [(v.shape[0], v.shape[1] // 2, v.shape[2]) for v in views]

    def body(*refs):
        ins, got = refs[:n], refs[n:2 * n]
        send, recv = refs[2 * n:]
        x, y, c = _place()
        sibling = (x, y, 1 - c)
        for w in range(n):
            hr = half_shapes[w][1]
            _send_rows(ins[w], (1 - c) * hr, got[w], 0, hr, send.at[w], recv.at[w], sibling)
        for w in range(n):
            hr = half_shapes[w][1]
            pltpu.make_async_remote_copy(src_ref=ins[w].at[:, pl.ds((1 - c) * hr, hr)], dst_ref=got[w], send_sem=send.at[w],
                                         recv_sem=recv.at[w], device_id=sibling, device_id_type=MESH).wait()

    any_spec = pl.BlockSpec(memory_space=pl.ANY)
    return pl.pallas_call(
        body, name=name,
        out_shape=[jax.ShapeDtypeStruct(s, v.dtype) for s, v in zip(half_shapes, views)],
        in_specs=[any_spec] * n, out_specs=[any_spec] * n,
        scratch_shapes=[pltpu.SemaphoreType.DMA((n,)), pltpu.SemaphoreType.DMA((n,))],
    )(*views)


def _pair_sum(view, got, core, name):
    s, r, cols = view.shape
    hr = r // 2
    tr = _row_tile(hr, cols)
    nb = hr // tr

    def body(core_ref, a_ref, b_ref, o_ref):
        del core_ref
        o_ref[...] = (a_ref[...].astype(F32) + b_ref[...].astype(F32)).astype(BF16)

    same = pl.BlockSpec((None, tr, cols), lambda j, i, core_ref: (j, i, 0))
    return pl.pallas_call(
        body, name=name,
        grid_spec=pltpu.PrefetchScalarGridSpec(
            num_scalar_prefetch=1, grid=(s, nb),
            in_specs=[pl.BlockSpec((None, tr, cols), lambda j, i, core_ref: (j, core_ref[0] * nb + i, 0)), same],
            out_specs=same),
        out_shape=jax.ShapeDtypeStruct((s, hr, cols), BF16), compiler_params=_params("parallel", "parallel"))(core, view, got)


def _piece_cols(part, col_sharded):
    return part.shape[2] // N_CHIPS if col_sharded else part.shape[2]


def _reduce_over_chips(parts, col_sharded, place, after, between, tag):
    n = len(parts)
    lands = [lax.empty((3, p.shape[1], _piece_cols(p, cs)), p.dtype) for p, cs in zip(parts, col_sharded)]
    copies = _chip_scatter_copies(n, col_sharded)
    send, recv, flying, token = _start_copies(f"grads_scatter_{tag}_start", list(parts) + lands, (3 * n,), after, copies)
    done, marker = between(token)
    landed = _wait_copies(f"grads_scatter_{tag}_wait", flying, send, recv, marker, copies)
    reduced = [_chip_sum(landed[w], landed[n + w], col_sharded[w], place, f"grads_chip_sum_{tag}_{w}") for w in range(n)]
    return reduced, done


def _chip_sum(part, got, col_sharded, place, name):
    _, hr, _ = part.shape
    cols = _piece_cols(part, col_sharded)
    tr = _row_tile(hr, cols)
    nb = hr // tr

    def body(place_ref, own_ref, g0_ref, g1_ref, g2_ref, o_ref):
        del place_ref
        acc = own_ref[...].astype(F32) + g0_ref[...].astype(F32)
        o_ref[...] = acc + g1_ref[...].astype(F32) + g2_ref[...].astype(F32)

    if col_sharded:
        own = pl.BlockSpec((None, tr, cols), lambda i, pr: (0, i, pr[0]))
    else:
        own = pl.BlockSpec((None, tr, cols), lambda i, pr: (pr[0], i, 0))
    others = [pl.BlockSpec((None, tr, cols), lambda i, pr, k=k: (k, i, 0)) for k in range(3)]
    return pl.pallas_call(
        body, name=name,
        grid_spec=pltpu.PrefetchScalarGridSpec(
            num_scalar_prefetch=1, grid=(nb,), in_specs=[own] + others,
            out_specs=pl.BlockSpec((tr, cols), lambda i, pr: (pr[1] * nb + i, 0))),
        out_shape=jax.ShapeDtypeStruct((2 * hr, cols), F32), compiler_params=_params("parallel"))(place, part, got, got, got)


def _pair_join_halves(fulls):
    n = len(fulls)
    views = [f.reshape(1, *f.shape) for f in fulls]

    def body(*refs):
        outs = refs[n:2 * n]
        send, recv = refs[2 * n:]
        x, y, c = _place()
        sibling = (x, y, 1 - c)
        for w in range(n):
            hr = outs[w].shape[1] // 2
            _send_rows(outs[w], c * hr, outs[w], c * hr, hr, send.at[w], recv.at[w], sibling)
        for w in range(n):
            hr = outs[w].shape[1] // 2
            pltpu.make_async_remote_copy(
                src_ref=outs[w].at[:, pl.ds(c * hr, hr)], dst_ref=outs[w].at[:, pl.ds((1 - c) * hr, hr)], send_sem=send.at[w],
                recv_sem=recv.at[w], device_id=sibling, device_id_type=MESH).wait()

    any_spec = pl.BlockSpec(memory_space=pl.ANY)
    outs = pl.pallas_call(
        body, name="grads_pair_join",
        out_shape=[jax.ShapeDtypeStruct(v.shape, v.dtype) for v in views],
        in_specs=[any_spec] * n, out_specs=[any_spec] * n, input_output_aliases={w: w for w in range(n)},
        scratch_shapes=[pltpu.SemaphoreType.DMA((n,)), pltpu.SemaphoreType.DMA((n,))],
    )(*views)
    return [o[0] for o in outs]


def _all_sum_small(vec, n_sum, d):
    rows = vec.shape[0]

    def body(v_ref, sum_ref, kept_ref, all_ref, send, recv):
        x, y, c = _place()
        me = 4 * x + 2 * y + c
        all_ref[me] = v_ref[...]
        peers = _all_devices(x, y, c)
        copies = []
        for k, (peer, _) in enumerate(peers):
            cp = pltpu.make_async_remote_copy(src_ref=v_ref, dst_ref=all_ref.at[me], send_sem=send.at[k], recv_sem=recv.at[k],
                                              device_id=peer, device_id_type=MESH)
            cp.start()
            copies.append(cp)
        for k, (_, src) in enumerate(peers):
            pltpu.make_async_remote_copy(src_ref=v_ref, dst_ref=all_ref.at[src], send_sem=send.at[k], recv_sem=recv.at[k],
                                         device_id=(x, y, c), device_id_type=MESH).wait_recv()
        for cp in copies:
            cp.wait_send()
        total = all_ref[0, 0:n_sum, :]
        for i in range(1, N_DEV):
            total = total + all_ref[i, 0:n_sum, :]
        sum_ref[...] = total
        loss = 0.5 / d * jnp.sum(total[0:8, :])
        sum_ref[0:8, :] = jnp.full((8, 128), loss, F32)
        for i in range(N_DEV):
            kept_ref[i] = all_ref[i, n_sum:rows, :]

    vm = pl.BlockSpec(memory_space=pltpu.VMEM)
    return pl.pallas_call(
        body, name="all_sum_small",
        out_shape=[jax.ShapeDtypeStruct((n_sum, 128), F32), jax.ShapeDtypeStruct((N_DEV, rows - n_sum, 128), F32)],
        in_specs=[vm], out_specs=[vm, vm],
        scratch_shapes=[pltpu.VMEM((N_DEV, rows, 128), F32), pltpu.SemaphoreType.DMA((N_DEV - 1,)),
                        pltpu.SemaphoreType.DMA((N_DEV - 1,))],
        compiler_params=pltpu.CompilerParams(vmem_limit_bytes=VMEM_LIMIT_V7X),
    )(vec)


def _row_tile(rows, cols, itemsize=4, budget=2 << 20):
    t = rows
    while t * cols * itemsize > budget and t % 16 == 0:
        t //= 2
    return t


def _adamw(w, g, m, v, name):
    rows, cols = w.shape
    tr = _row_tile(rows, cols, budget=1 << 20)

    def body(w_ref, g_ref, m_ref, v_ref, d_ref, nm_ref, nv_ref):
        g_ = g_ref[...]
        nm = ADAM_B1 * m_ref[...] + (1.0 - ADAM_B1) * g_
        nv = ADAM_B2 * v_ref[...] + (1.0 - ADAM_B2) * (g_ * g_)
        m_hat = nm / (1.0 - ADAM_B1 ** ADAM_STEP)
        v_hat = nv / (1.0 - ADAM_B2 ** ADAM_STEP)
        d_ref[...] = -ADAM_LR * (m_hat / (jnp.sqrt(v_hat) + ADAM_EPS) + ADAM_WD * w_ref[...])
        nm_ref[...] = nm
        nv_ref[...] = nv

    spec = pl.BlockSpec((tr, cols), lambda i: (i, 0))
    shp = jax.ShapeDtypeStruct((rows, cols), F32)
    return pl.pallas_call(body, name=name, grid=(rows // tr,), in_specs=[spec] * 4, out_specs=[spec] * 3,
                          out_shape=[shp] * 3, compiler_params=_params("parallel"))(w, g, m, v)


def _all_devices(x, y, c):
    out = []
    for k in range(1, N_DEV):
        peer = (x ^ ((k >> 2) & 1), y ^ ((k >> 1) & 1), c ^ (k & 1))
        out.append((peer, 4 * peer[0] + 2 * peer[1] + peer[2]))
    return out


def _ada_exchange(c, w_shard, b_ada):
    bsz, d = c.shape
    cs = w_shard.shape[1]

    def body(c_ref, w_ref, b_ref, mod_ref, act_ref, c_all, part, pieces, csend, crecv, psend, precv):
        x, y, core = _place()
        me = 4 * x + 2 * y + core
        chip = 2 * x + y
        c_all[me] = c_ref[...]
        peers = _all_devices(x, y, core)
        copies = []
        for k, (peer, _) in enumerate(peers):
            cp = pltpu.make_async_remote_copy(src_ref=c_ref, dst_ref=c_all.at[me], send_sem=csend.at[k], recv_sem=crecv.at[k],
                                              device_id=peer, device_id_type=MESH)
            cp.start()
            copies.append(cp)
        for k, (_, src) in enumerate(peers):
            pltpu.make_async_remote_copy(src_ref=c_ref, dst_ref=c_all.at[src], send_sem=csend.at[k], recv_sem=crecv.at[k],
                                         device_id=(x, y, core), device_id_type=MESH).wait_recv()
        rows = jnp.concatenate([c_all[i] for i in range(N_DEV)], axis=0)
        act = rows * _sigmoid(rows)
        act_ref[...] = act
        prod = jnp.dot(act.astype(BF16), w_ref[...].astype(BF16), preferred_element_type=F32)
        for i in range(N_DEV):
            part[i] = prod[i * bsz:(i + 1) * bsz, :]
        pieces[chip] = part[me]
        chips = _other_chips(x, y)
        for k, (cx, cy) in enumerate(chips):
            cp = pltpu.make_async_remote_copy(src_ref=part.at[4 * cx + 2 * cy + core], dst_ref=pieces.at[chip],
                                              send_sem=psend.at[k], recv_sem=precv.at[k], device_id=(cx, cy, core),
                                              device_id_type=MESH)
            cp.start()
            copies.append(cp)
        for k, (cx, cy) in enumerate(chips):
            pltpu.make_async_remote_copy(src_ref=part.at[me], dst_ref=pieces.at[2 * cx + cy], send_sem=psend.at[k],
                                         recv_sem=precv.at[k], device_id=(cx, cy, core), device_id_type=MESH).wait_recv()
        for cp in copies:
            cp.wait_send()
        mod_ref[...] = jnp.concatenate([pieces[j] for j in range(N_CHIPS)], axis=1) + b_ref[...]

    vm = pl.BlockSpec(memory_space=pltpu.VMEM)
    return pl.pallas_call(
        body, name="ada_exchange", in_specs=[vm] * 3, out_specs=[vm] * 2,
        out_shape=[jax.ShapeDtypeStruct((bsz, 3 * d), F32), jax.ShapeDtypeStruct((N_DEV * bsz, d), F32)],
        scratch_shapes=[pltpu.VMEM((N_DEV, bsz, d), F32), pltpu.VMEM((N_DEV, bsz, cs), F32), pltpu.VMEM((N_CHIPS, bsz, cs), F32),
                        pltpu.SemaphoreType.DMA((N_DEV - 1,)), pltpu.SemaphoreType.DMA((N_DEV - 1,)),
                        pltpu.SemaphoreType.DMA((3,)), pltpu.SemaphoreType.DMA((3,))],
        compiler_params=pltpu.CompilerParams(vmem_limit_bytes=VMEM_LIMIT_V7X))(c, w_shard, b_ada)


def _grad_w_ada_cols(act_t, d_cols):
    d, n = act_t.shape
    cs = d_cols.shape[1]

    def body(a_ref, g_ref, o_ref):
        a, g = a_ref[...], g_ref[...]
        acc = a[:, 0:1] * g[0:1, :]
        for b in range(1, n):
            acc = acc + a[:, b:b + 1] * g[b:b + 1, :]
        o_ref[...] = acc

    vm = pl.BlockSpec(memory_space=pltpu.VMEM)
    return pl.pallas_call(body, name="grad_w_ada", in_specs=[vm] * 2, out_specs=vm,
                          out_shape=jax.ShapeDtypeStruct((d, cs), F32),
                          compiler_params=pltpu.CompilerParams(vmem_limit_bytes=VMEM_LIMIT_V7X))(act_t, d_cols)


def _permute_w_in(w_nat, lay):
    d = lay.d
    group = 4

    def call(name, width, n_pieces, nat_piece, out_block0, prev):
        def body(*refs):
            refs[-1][...] = jnp.concatenate([r[...] for r in refs[:group]], axis=1)

        in_specs = [pl.BlockSpec((d, width), lambda s, m=m: (0, nat_piece(group * s + m))) for m in range(group)]
        args = [w_nat] * group
        aliases = {}
        if prev is not None:
            in_specs.append(pl.BlockSpec(memory_space=pl.ANY))
            args.append(prev)
            aliases = {group: 0}
        return pl.pallas_call(
            body, name=name, grid=(n_pieces // group,), in_specs=in_specs,
            out_specs=pl.BlockSpec((d, group * width), lambda s: (0, out_block0 + s)),
            out_shape=jax.ShapeDtypeStruct((d, lay.np), BF16), input_output_aliases=aliases,
            compiler_params=_params("arbitrary"))(*args)

    w_all = call("permute_w_attn", SLAB, 2 * PAIR_SLABS, lay.attn_nat_slab, 0, None)
    n_rest = 6 * d // CONV_TILE
    if n_rest % group:
        group = 2
    return call("permute_w_rest", CONV_TILE, n_rest, lay.rest_nat_tile, lay.c0 // (group * CONV_TILE), w_all)


def _project(x2, mod3, w_all, b_all, seq, col0, ncols, tn, out_dtype, want_ht, name):
    t, d = x2.shape
    tm = min(1024, seq)
    per_seq = seq // tm
    j0 = col0 // tn

    def body(x_ref, mod_ref, w_ref, b_ref, o_ref, *rest):
        h_ref = rest[-1]

        @pl.when(pl.program_id(1) == 0)
        def _():
            h = x_ref[...] * (1.0 + mod_ref[:, d:2 * d]) + mod_ref[:, 0:d]
            h_ref[...] = h.astype(BF16)
            if want_ht:
                rest[0][...] = h.T.astype(BF16)

        o_ref[...] = (jnp.dot(h_ref[...], w_ref[...], preferred_element_type=F32) + b_ref[...]).astype(out_dtype)

    out_shape = [jax.ShapeDtypeStruct((t, ncols), out_dtype)]
    out_specs = [pl.BlockSpec((tm, tn), lambda i, j: (i, j))]
    if want_ht:
        out_shape.append(jax.ShapeDtypeStruct((d, t), BF16))
        out_specs.append(pl.BlockSpec((d, tm), lambda i, j: (0, i)))
    return pl.pallas_call(
        body, name=name, grid=(t // tm, ncols // tn),
        in_specs=[pl.BlockSpec((tm, d), lambda i, j: (i, 0)),
                  pl.BlockSpec((None, 1, 3 * d), lambda i, j: (i // per_seq, 0, 0)),
                  pl.BlockSpec((d, tn), lambda i, j: (0, j0 + j)),
                  pl.BlockSpec((1, tn), lambda i, j: (0, j0 + j))],
        out_specs=out_specs, out_shape=out_shape,
        scratch_shapes=[pltpu.VMEM((tm, d), BF16)],
        compiler_params=_params("arbitrary", "arbitrary"))(x2, mod3, w_all, b_all)


def _slope(g, p, hh):
    head = 4 * g + 2 * p + hh
    return 2.0 ** (-ALIBI_MAX_EXP * (head + 1.0) / N_HEADS)


def _ld_rows(ref, start, n, stride):
    if stride == 1:
        return ref[pl.ds(start, n), :]
    return ref[pl.ds(start, n, stride=stride), :]


def _st_rows(ref, start, n, stride, val):
    if stride == 1:
        ref[pl.ds(start, n), :] = val
    else:
        ref[pl.ds(start, n, stride=stride), :] = val


def _sub_blocks(g, seq):
    return seq // DILATIONS[g] // SUB


def _key_rows(g, seq):
    return SUB if _sub_blocks(g, seq) == 1 else 2 * SUB


def _fill_bias(bias_ref, p, seq):
    for g in range(N_GROUPS):
        nk = _key_rows(g, seq)
        diff = lax.broadcasted_iota(jnp.int32, (SUB, nk), 0) - lax.broadcasted_iota(jnp.int32, (SUB, nk), 1)
        for i, off in enumerate((0, SUB)):
            if i == 1 and nk == SUB:
                continue
            delta = diff + off
            ok = (delta >= 0) & (delta <= SUB)
            dist = (delta * DILATIONS[g]).astype(F32)
            for hh in range(2):
                slope = jnp.where(p == 0, _slope(g, 0, hh), _slope(g, 1, hh))
                bias_ref[g, i, hh, :, 0:nk] = jnp.where(ok, -slope * dist, NEG)


def _to_sub_major(pa_ref, col, sub_ref, stage, dil, seq):
    cols = slice(col * SLAB, (col + 1) * SLAB)
    if dil == 1:
        sub_ref[...] = pa_ref[:, cols]
        return
    n = seq // dil
    stage[...] = pa_ref[:, cols].astype(F32)
    for r in range(dil):
        sub_ref[pl.ds(r * n, n), :] = stage[pl.ds(r, n, stride=dil), :].astype(BF16)


def _block_rows(it, g, seq):
    dil, nb = DILATIONS[g], _sub_blocks(g, seq)
    row0 = pl.multiple_of(it * SUB, SUB)
    if nb == 1:
        return row0, row0, 0, it
    blk = it % nb
    first = blk == 0
    krow0 = pl.multiple_of(row0 - jnp.where(first, 0, SUB), SUB)
    nat = row0 if dil == 1 else it // nb + dil * SUB * blk
    return row0, krow0, jnp.where(first, 0, 1), nat


def _nt(a, b):
    return lax.dot_general(a, b, (((1,), (1,)), ((), ())), preferred_element_type=F32)


def _tn(a, b):
    return lax.dot_general(a, b, (((0,), (0,)), ((), ())), preferred_element_type=F32)


def _head_sums(t):
    rows = t.shape[0]
    lo = jnp.broadcast_to(jnp.sum(t[:, :HEAD_DIM], axis=-1, keepdims=True), (rows, HEAD_DIM))
    hi = jnp.broadcast_to(jnp.sum(t[:, HEAD_DIM:], axis=-1, keepdims=True), (rows, HEAD_DIM))
    return jnp.concatenate([lo, hi], axis=-1)


def _attn_fwd(pa, bsz, seq):
    t = pa.shape[0]
    n_blocks = seq // SUB
    chunk = 256

    def body(pa_ref, o_ref, lse_ref, a_ref, sub, stage, bias_ref, s_buf, p_buf, l_buf):
        p = pl.program_id(1)
        _fill_bias(bias_ref, p, seq)
        head0 = lax.broadcasted_iota(jnp.int32, (SUB, SLAB), 1) < HEAD_DIM
        for g in range(N_GROUPS):
            dil = DILATIONS[g]
            for w in range(3):
                _to_sub_major(pa_ref, 3 * w + g, sub.at[w], stage, dil, seq)
            nk = _key_rows(g, seq)

            def trip(i, carry, g=g, dil=dil, nk=nk):
                places = [_block_rows(BLOCKS_PER_TRIP * i + j, g, seq) for j in range(BLOCKS_PER_TRIP)]
                for j, (row0, krow0, _, _) in enumerate(places):
                    q = sub[0, pl.ds(row0, SUB), :]
                    zero = jnp.zeros_like(q)
                    q2 = jnp.concatenate([jnp.where(head0, q, zero), jnp.where(head0, zero, q)], axis=0) * (HEAD_DIM ** -0.5)
                    s_buf[j, :, 0:nk] = _nt(q2, sub[1, pl.ds(krow0, nk), :])
                for j, (_, _, bi, _) in enumerate(places):
                    for c in range(0, 2 * SUB, SOFTMAX_ROWS):
                        hh, r = divmod(c, SUB)
                        s = s_buf[j, c:c + SOFTMAX_ROWS, 0:nk] + bias_ref[g, bi, hh, r:r + SOFTMAX_ROWS, 0:nk]
                        m = jnp.max(s, axis=-1, keepdims=True)
                        e = jnp.exp(s - m)
                        den = jnp.sum(e, axis=-1, keepdims=True)
                        p_buf[j, c:c + SOFTMAX_ROWS, 0:nk] = (e * (1.0 / den)).astype(BF16)
                        l_buf[j, c:c + SOFTMAX_ROWS, :] = jnp.broadcast_to(m + jnp.log(den), (SOFTMAX_ROWS, SLAB))
                for j, (_, krow0, _, nat) in enumerate(places):
                    o2 = jnp.dot(p_buf[j, :, 0:nk], sub[2, pl.ds(krow0, nk), :], preferred_element_type=F32)
                    _st_rows(o_ref.at[g], nat, SUB, dil, jnp.where(head0, o2[0:SUB], o2[SUB:2 * SUB]))
                    _st_rows(lse_ref.at[g], nat, SUB, dil, jnp.where(head0, l_buf[j, 0:SUB, :], l_buf[j, SUB:2 * SUB, :]))
                return carry

            lax.fori_loop(0, n_blocks // BLOCKS_PER_TRIP, trip, 0)

        def mix(i, carry):
            rows = pl.ds(pl.multiple_of(i * chunk, chunk), chunk)
            l0, l1, l2 = lse_ref[0, rows, :], lse_ref[1, rows, :], lse_ref[2, rows, :]
            m = jnp.maximum(jnp.maximum(l0, l1), l2)
            e0, e1, e2 = jnp.exp(l0 - m), jnp.exp(l1 - m), jnp.exp(l2 - m)
            tot = e0 + e1 + e2
            o = (e0 / tot) * o_ref[0, rows, :] + (e1 / tot) * o_ref[1, rows, :] + (e2 / tot) * o_ref[2, rows, :]
            z = pa_ref[rows, 9 * SLAB:10 * SLAB].astype(F32)
            a_ref[rows, :] = (o * (z * _sigmoid(z))).astype(BF16)
            return carry

        lax.fori_loop(0, seq // chunk, mix, 0)

    big = jax.ShapeDtypeStruct((N_GROUPS, t, 2 * SLAB), F32)
    return pl.pallas_call(
        body, name="attn_fwd", grid=(bsz, 2),
        in_specs=[pl.BlockSpec((seq, PAIR_COLS), lambda b, p: (b, p))],
        out_specs=[pl.BlockSpec((N_GROUPS, seq, SLAB), lambda b, p: (0, b, p)),
                   pl.BlockSpec((N_GROUPS, seq, SLAB), lambda b, p: (0, b, p)),
                   pl.BlockSpec((seq, SLAB), lambda b, p: (b, p))],
        out_shape=[big, big, jax.ShapeDtypeStruct((t, 2 * SLAB), BF16)],
        scratch_shapes=[pltpu.VMEM((3, seq, SLAB), BF16), pltpu.VMEM((seq, SLAB), F32),
                        pltpu.VMEM((N_GROUPS, 2, 2, SUB, 2 * SUB), F32), pltpu.VMEM((BLOCKS_PER_TRIP, 2 * SUB, 2 * SUB), F32),
                        pltpu.VMEM((BLOCKS_PER_TRIP, 2 * SUB, 2 * SUB), BF16), pltpu.VMEM((BLOCKS_PER_TRIP, 2 * SUB, SLAB), F32)],
        compiler_params=_params("arbitrary", "arbitrary"))(pa)


def _shift_down(v, k):
    rows = lax.broadcasted_iota(jnp.int32, v.shape, 0)
    return jnp.where(rows >= k, pltpu.roll(v, k, 0), 0.0)


def _shift_up(v, k):
    n = v.shape[0]
    rows = lax.broadcasted_iota(jnp.int32, v.shape, 0)
    return jnp.where(rows < n - k, pltpu.roll(v, n - k, 0), 0.0)


def _conv_fwd(pr, conv_w, bsz, seq, d):
    t = pr.shape[0]
    ct = CONV_TILE

    def body(p_ref, cw_ref, o_ref):
        u = p_ref[:, 2 * ct:3 * ct].astype(F32) * p_ref[:, 0:ct].astype(F32)
        cw = cw_ref[...]
        conv = cw[0:1, :] * _shift_down(u, 2)
        conv = conv + cw[1:2, :] * _shift_down(u, 1)
        conv = conv + cw[2:3, :] * u
        z = p_ref[:, 3 * ct:4 * ct].astype(F32)
        o_ref[...] = (p_ref[:, ct:2 * ct].astype(F32) * conv * (z * _sigmoid(z))).astype(BF16)

    return pl.pallas_call(
        body, name="conv_fwd", grid=(bsz, d // ct),
        in_specs=[pl.BlockSpec((seq, 4 * ct), lambda b, j: (b, j)), pl.BlockSpec((3, ct), lambda b, j: (0, j))],
        out_specs=pl.BlockSpec((seq, ct), lambda b, j: (b, j)),
        out_shape=jax.ShapeDtypeStruct((t, d), BF16), compiler_params=_params("parallel", "parallel"))(pr, conv_w)


def _tail(a_in, b_in, pr, x2, target2, mod3, w_pa, w_pc, w_out, b_out, ln_g, ln_b, seq, lay):
    t, d = x2.shape
    tm = 256
    per_seq = seq // tm
    n_steps = t // tm
    gate_blk = 4 * d // d

    def nt(a, b):
        return lax.dot_general(a, b, (((1,), (1,)), ((), ())), preferred_element_type=F32)

    def tn(a, b):
        return lax.dot_general(a, b, (((0,), (0,)), ((), ())), preferred_element_type=F32)

    def body(a_ref, b_ref, ga_ref, gb_ref, x_ref, tg_ref, mod_ref, wpa_ref, wpc_ref, wo_ref, bo_ref, lg_ref, lb_ref,
             dpg_ref, da_ref, db_ref, gx_ref, dgate_ref, small_ref, gwpa_hbm, gwpc_hbm, gwo_hbm,
             acc_pa, acc_pc, acc_o, sem):
        i = pl.program_id(0)

        @pl.when(i == 0)
        def _():
            acc_pa[...] = jnp.zeros_like(acc_pa)
            acc_pc[...] = jnp.zeros_like(acc_pc)
            acc_o[...] = jnp.zeros_like(acc_o)
            small_ref[...] = jnp.zeros_like(small_ref)

        @pl.when(i % per_seq == 0)
        def _():
            dgate_ref[...] = jnp.zeros_like(dgate_ref)

        a_bf, b_bf = a_ref[...], b_ref[...]
        y_attn = jnp.dot(a_bf, wpa_ref[...], preferred_element_type=F32)
        y_conv = jnp.dot(b_bf, wpc_ref[...], preferred_element_type=F32)
        sa, sb = _sigmoid(ga_ref[...].astype(F32)), _sigmoid(gb_ref[...].astype(F32))
        merged = (sa * y_attn + sb * y_conv).astype(BF16)
        mo = jnp.dot(merged, wo_ref[...], preferred_element_type=F32) + bo_ref[...]
        gate = mod_ref[:, 2 * d:3 * d]
        r = ALPHA * x_ref[...] + gate * mo
        mu = jnp.mean(r, axis=-1, keepdims=True)
        cen = r - mu
        var = jnp.mean(cen * cen, axis=-1, keepdims=True)
        rstd = lax.rsqrt(var + LN_EPS)
        xhat = cen * rstd
        err = xhat * lg_ref[...] + lb_ref[...] - tg_ref[...]
        dy = err * (1.0 / d)
        dxhat = dy * lg_ref[...]
        dr = rstd * (dxhat - jnp.mean(dxhat, axis=-1, keepdims=True) - xhat * jnp.mean(dxhat * xhat, axis=-1, keepdims=True))
        gx_ref[...] = ALPHA * dr
        dgate_ref[...] += jnp.sum(dr * mo, axis=0, keepdims=True)
        d_mo = dr * gate
        small_ref[0:1, :] += jnp.sum(d_mo, axis=0, keepdims=True)
        small_ref[1:2, :] += jnp.sum(dy * xhat, axis=0, keepdims=True)
        small_ref[2:3, :] += jnp.sum(dy, axis=0, keepdims=True)
        small_ref[3:4, :] += jnp.sum(err * err, axis=0, keepdims=True)
        d_mo_bf = d_mo.astype(BF16)
        acc_o[...] += tn(merged, d_mo_bf)
        dmerged = nt(d_mo_bf, wo_ref[...])
        dy_attn = (dmerged * sa).astype(BF16)
        dy_conv = (dmerged * sb).astype(BF16)
        dpg_ref[:, 0:d] = (dmerged * y_attn * sa * (1.0 - sa)).astype(BF16)
        dpg_ref[:, d:2 * d] = (dmerged * y_conv * sb * (1.0 - sb)).astype(BF16)
        acc_pa[...] += tn(a_bf, dy_attn)
        acc_pc[...] += tn(b_bf, dy_conv)
        da_ref[...] = nt(dy_attn, wpa_ref[...])
        db_ref[...] = nt(dy_conv, wpc_ref[...])

        @pl.when(i == n_steps - 1)
        def _():
            copies = [pltpu.make_async_copy(acc_pa, gwpa_hbm, sem.at[0]), pltpu.make_async_copy(acc_pc, gwpc_hbm, sem.at[1]),
                      pltpu.make_async_copy(acc_o, gwo_hbm, sem.at[2])]
            for cp in copies:
                cp.start()
            for cp in copies:
                cp.wait()

    row = lambda w: pl.BlockSpec((tm, w), lambda i: (i, 0))
    const = lambda shp: pl.BlockSpec(shp, lambda i: (0,) * len(shp), pipeline_mode=pl.Buffered(1))
    any_spec = pl.BlockSpec(memory_space=pl.ANY)
    return pl.pallas_call(
        body, name="tail", grid=(n_steps,),
        in_specs=[row(Z_WIDTH), row(d),
                  pl.BlockSpec((tm, d), lambda i: (i, gate_blk)), pl.BlockSpec((tm, d), lambda i: (i, gate_blk + 1)),
                  row(d), row(d), pl.BlockSpec((None, 1, 3 * d), lambda i: (i // per_seq, 0, 0)),
                  const((Z_WIDTH, d)), const((d, d)), const((d, d)), const((1, d)), const((1, d)), const((1, d))],
        out_specs=[pl.BlockSpec((tm, 2 * d), lambda i: (i, lay.g0 // (2 * d))), row(Z_WIDTH), row(d), row(d),
                   pl.BlockSpec((None, 1, d), lambda i: (i // per_seq, 0, 0)), pl.BlockSpec((8, d), lambda i: (0, 0)),
                   any_spec, any_spec, any_spec],
        out_shape=[jax.ShapeDtypeStruct((t, lay.np), BF16), jax.ShapeDtypeStruct((t, Z_WIDTH), F32),
                   jax.ShapeDtypeStruct((t, d), F32), jax.ShapeDtypeStruct((t, d), F32),
                   jax.ShapeDtypeStruct((t // seq, 1, d), F32), jax.ShapeDtypeStruct((8, d), F32),
                   jax.ShapeDtypeStruct((Z_WIDTH, d), F32), jax.ShapeDtypeStruct((d, d), F32),
                   jax.ShapeDtypeStruct((d, d), F32)],
        scratch_shapes=[pltpu.VMEM((Z_WIDTH, d), F32), pltpu.VMEM((d, d), F32), pltpu.VMEM((d, d), F32),
                        pltpu.SemaphoreType.DMA((3,))],
        compiler_params=_params("arbitrary"),
    )(a_in, b_in, pr, pr, x2, target2, mod3, w_pa, w_pc, w_out, b_out, ln_g, ln_b)


def _conv_bwd(dproj, db, pr, conv_w, bsz, seq, lay):
    d = lay.d
    ct = CONV_TILE
    base = lay.c0 // (4 * ct)

    def body(dp_in, db_ref, p_ref, cw_ref, dp_ref, gcw_ref):
        del dp_in
        u_x, g_b, g_c, z = [p_ref[:, k * ct:(k + 1) * ct].astype(F32) for k in range(4)]
        cw = cw_ref[...]
        u = g_c * u_x
        u1, u2 = _shift_down(u, 1), _shift_down(u, 2)
        conv = cw[0:1, :] * u2 + cw[1:2, :] * u1 + cw[2:3, :] * u
        sig = _sigmoid(z)
        sl = z * sig
        dbv = db_ref[...]
        gbc = g_b * conv
        dp_ref[:, ct:2 * ct] = (dbv * sl * conv).astype(BF16)
        dp_ref[:, 3 * ct:4 * ct] = (dbv * gbc * (sig * (1.0 + z * (1.0 - sig)))).astype(BF16)
        dconv = dbv * sl * g_b

        @pl.when(pl.program_id(1) == 0)
        def _():
            gcw_ref[...] = jnp.zeros_like(gcw_ref)

        gcw_ref[0:1, :] += jnp.sum(dconv * u2, axis=0, keepdims=True)
        gcw_ref[1:2, :] += jnp.sum(dconv * u1, axis=0, keepdims=True)
        gcw_ref[2:3, :] += jnp.sum(dconv * u, axis=0, keepdims=True)
        du = cw[2:3, :] * dconv + cw[1:2, :] * _shift_up(dconv, 1) + cw[0:1, :] * _shift_up(dconv, 2)
        dp_ref[:, 0:ct] = (du * g_c).astype(BF16)
        dp_ref[:, 2 * ct:3 * ct] = (du * u_x).astype(BF16)

    return pl.pallas_call(
        body, name="conv_bwd", grid=(d // ct, bsz),
        in_specs=[pl.BlockSpec(memory_space=pl.ANY), pl.BlockSpec((seq, ct), lambda j, b: (b, j)),
                  pl.BlockSpec((seq, 4 * ct), lambda j, b: (b, j)), pl.BlockSpec((3, ct), lambda j, b: (0, j))],
        out_specs=[pl.BlockSpec((seq, 4 * ct), lambda j, b: (b, base + j)), pl.BlockSpec((8, ct), lambda j, b: (0, j))],
        out_shape=[jax.ShapeDtypeStruct(dproj.shape, BF16), jax.ShapeDtypeStruct((8, d), F32)],
        input_output_aliases={0: 0}, compiler_params=_params("arbitrary", "arbitrary"))(dproj, db, pr, conv_w)


def _attn_bwd(dproj, pa, o_all, lse_all, da, bsz, seq):
    n_blocks = seq // SUB
    chunk = 256

    def body(dp_in, pa_ref, o_ref, lse_ref, da_ref, dp_ref, sub, stage, dsub, dog, cvec, bias_ref,
             s_buf, dp_buf, ds_buf, pb_buf, q2_buf, do2_buf, l_buf, c_buf):
        del dp_in
        p = pl.program_id(1)
        _fill_bias(bias_ref, p, seq)
        head0 = lax.broadcasted_iota(jnp.int32, (SUB, SLAB), 1) < HEAD_DIM

        def mix_bwd(i, carry):
            rows = pl.ds(pl.multiple_of(i * chunk, chunk), chunk)
            ls = [lse_ref[g, rows, :] for g in range(N_GROUPS)]
            os_ = [o_ref[g, rows, :] for g in range(N_GROUPS)]
            m = jnp.maximum(jnp.maximum(ls[0], ls[1]), ls[2])
            es = [jnp.exp(l - m) for l in ls]
            tot = es[0] + es[1] + es[2]
            ws = [e / tot for e in es]
            o = ws[0] * os_[0] + ws[1] * os_[1] + ws[2] * os_[2]
            z = pa_ref[rows, 9 * SLAB:10 * SLAB].astype(F32)
            sig = _sigmoid(z)
            dav = da_ref[rows, :]
            do = dav * (z * sig)
            dp_ref[rows, 9 * SLAB:10 * SLAB] = (dav * o * (sig * (1.0 + z * (1.0 - sig)))).astype(BF16)
            wsum = _head_sums(do * o)
            for g in range(N_GROUPS):
                dog[g, rows, :] = ws[g] * do
                cvec[g, rows, :] = -(ws[g] * wsum)
            return carry

        lax.fori_loop(0, seq // chunk, mix_bwd, 0)

        for g in range(N_GROUPS):
            dil = DILATIONS[g]
            for w in range(3):
                _to_sub_major(pa_ref, 3 * w + g, sub.at[w], stage, dil, seq)
            dsub[1] = jnp.zeros((seq, SLAB), F32)
            dsub[2] = jnp.zeros((seq, SLAB), F32)
            nk = _key_rows(g, seq)

            def trip(i, carry, g=g, dil=dil, nk=nk):
                places = [_block_rows(BLOCKS_PER_TRIP * i + j, g, seq) for j in range(BLOCKS_PER_TRIP)]
                for j, (row0, krow0, _, nat) in enumerate(places):
                    q = sub[0, pl.ds(row0, SUB), :]
                    do = _ld_rows(dog.at[g], nat, SUB, dil).astype(BF16)
                    zero = jnp.zeros_like(q)
                    q2 = jnp.concatenate([jnp.where(head0, q, zero), jnp.where(head0, zero, q)], axis=0)
                    do2 = jnp.concatenate([jnp.where(head0, do, zero), jnp.where(head0, zero, do)], axis=0)
                    q2_buf[j] = q2
                    do2_buf[j] = do2
                    s_buf[j, :, 0:nk] = _nt(q2 * (HEAD_DIM ** -0.5), sub[1, pl.ds(krow0, nk), :])
                    dp_buf[j, :, 0:nk] = _nt(do2, sub[2, pl.ds(krow0, nk), :])
                    l_buf[j] = _ld_rows(lse_ref.at[g], nat, SUB, dil)
                    c_buf[j] = _ld_rows(cvec.at[g], nat, SUB, dil)
                for j, (_, _, bi, _) in enumerate(places):
                    for c in range(0, 2 * SUB, SOFTMAX_ROWS):
                        hh, r = divmod(c, SUB)
                        lane = hh * HEAD_DIM
                        s = s_buf[j, c:c + SOFTMAX_ROWS, 0:nk] + bias_ref[g, bi, hh, r:r + SOFTMAX_ROWS, 0:nk]
                        prob = jnp.exp(s - l_buf[j, r:r + SOFTMAX_ROWS, lane:lane + 1])
                        dprob = dp_buf[j, c:c + SOFTMAX_ROWS, 0:nk] + c_buf[j, r:r + SOFTMAX_ROWS, lane:lane + 1]
                        ds_buf[j, c:c + SOFTMAX_ROWS, 0:nk] = (prob * dprob * (HEAD_DIM ** -0.5)).astype(BF16)
                        pb_buf[j, c:c + SOFTMAX_ROWS, 0:nk] = prob.astype(BF16)
                for j, (row0, krow0, _, _) in enumerate(places):
                    ds = ds_buf[j, :, 0:nk]
                    dq2 = jnp.dot(ds, sub[1, pl.ds(krow0, nk), :], preferred_element_type=F32)
                    dsub[0, pl.ds(row0, SUB), :] = jnp.where(head0, dq2[0:SUB], dq2[SUB:2 * SUB])
                    dsub[1, pl.ds(krow0, nk), :] += _tn(ds, q2_buf[j])
                    dsub[2, pl.ds(krow0, nk), :] += _tn(pb_buf[j, :, 0:nk], do2_buf[j])
                return carry

            lax.fori_loop(0, n_blocks // BLOCKS_PER_TRIP, trip, 0)
            for w in range(3):
                cols = slice((3 * w + g) * SLAB, (3 * w + g + 1) * SLAB)
                if dil == 1:
                    dp_ref[:, cols] = dsub[w].astype(BF16)
                else:
                    n = seq // dil
                    for r in range(dil):
                        stage[pl.ds(r, n, stride=dil), :] = dsub[w, pl.ds(r * n, n), :]
                    dp_ref[:, cols] = stage[...].astype(BF16)

    return pl.pallas_call(
        body, name="attn_bwd", grid=(bsz, 2),
        in_specs=[pl.BlockSpec(memory_space=pl.ANY), pl.BlockSpec((seq, PAIR_COLS), lambda b, p: (b, p)),
                  pl.BlockSpec((N_GROUPS, seq, SLAB), lambda b, p: (0, b, p)),
                  pl.BlockSpec((N_GROUPS, seq, SLAB), lambda b, p: (0, b, p)),
                  pl.BlockSpec((seq, SLAB), lambda b, p: (b, p))],
        out_specs=pl.BlockSpec((seq, PAIR_COLS), lambda b, p: (b, p)),
        out_shape=jax.ShapeDtypeStruct(dproj.shape, BF16), input_output_aliases={0: 0},
        scratch_shapes=[pltpu.VMEM((3, seq, SLAB), BF16), pltpu.VMEM((seq, SLAB), F32), pltpu.VMEM((3, seq, SLAB), F32),
                        pltpu.VMEM((3, seq, SLAB), F32), pltpu.VMEM((3, seq, SLAB), F32),
                        pltpu.VMEM((N_GROUPS, 2, 2, SUB, 2 * SUB), F32),
                        pltpu.VMEM((BLOCKS_PER_TRIP, 2 * SUB, 2 * SUB), F32), pltpu.VMEM((BLOCKS_PER_TRIP, 2 * SUB, 2 * SUB), F32),
                        pltpu.VMEM((BLOCKS_PER_TRIP, 2 * SUB, 2 * SUB), BF16), pltpu.VMEM((BLOCKS_PER_TRIP, 2 * SUB, 2 * SUB), BF16),
                        pltpu.VMEM((BLOCKS_PER_TRIP, 2 * SUB, SLAB), BF16), pltpu.VMEM((BLOCKS_PER_TRIP, 2 * SUB, SLAB), BF16),
                        pltpu.VMEM((BLOCKS_PER_TRIP, SUB, SLAB), F32), pltpu.VMEM((BLOCKS_PER_TRIP, SUB, SLAB), F32)],
        compiler_params=_params("arbitrary", "arbitrary"))(dproj, pa, o_all, lse_all, da)


def _grad_h(dproj, w_all, gx0, x2, mod3, seq, lay):
    t, d = x2.shape
    tm, tn = min(512, seq), min(512, d)
    per_seq = seq // tm

    def body(dp_ref, w_ref, gx0_ref, x_ref, scale_ref, gx_ref, dmod_ref):
        dh = _nt(dp_ref[:, 0:ATT], w_ref[:, 0:ATT]) + _nt(dp_ref[:, lay.c0:], w_ref[:, lay.c0:])
        gx_ref[...] = gx0_ref[...] + dh * (1.0 + scale_ref[...])

        @pl.when(pl.program_id(1) % per_seq == 0)
        def _():
            dmod_ref[...] = jnp.zeros_like(dmod_ref)

        dmod_ref[0:1, :] += jnp.sum(dh, axis=0, keepdims=True)
        dmod_ref[1:2, :] += jnp.sum(dh * x_ref[...], axis=0, keepdims=True)

    tile = pl.BlockSpec((tm, tn), lambda j, i: (i, j))
    return pl.pallas_call(
        body, name="grad_h", grid=(d // tn, t // tm),
        in_specs=[pl.BlockSpec((tm, lay.np), lambda j, i: (i, 0)), pl.BlockSpec((tn, lay.np), lambda j, i: (j, 0)), tile, tile,
                  pl.BlockSpec((None, 1, tn), lambda j, i: (i // per_seq, 0, d // tn + j))],
        out_specs=[tile, pl.BlockSpec((None, 8, tn), lambda j, i: (i // per_seq, 0, j))],
        out_shape=[jax.ShapeDtypeStruct((t, d), F32), jax.ShapeDtypeStruct((t // seq, 8, d), F32)],
        compiler_params=_params("arbitrary", "arbitrary"))(dproj, w_all, gx0, x2, mod3)


def _grad_w_in(ht, dproj, seq, lay):
    d, t = ht.shape
    tm = seq
    n_i = t // tm

    def make_body(n_skip, n_pieces):
        def body(*refs):
            refs = refs[n_skip:]
            ht_ref, dp_refs = refs[0], refs[1:1 + n_pieces]
            gw_ref, gb_ref, acc, bacc = refs[1 + n_pieces:]
            i = pl.program_id(1)

            @pl.when(i == 0)
            def _():
                acc[...] = jnp.zeros_like(acc)
                bacc[...] = jnp.zeros_like(bacc)

            dp = dp_refs[0][...] if n_pieces == 1 else jnp.concatenate([r[...] for r in dp_refs], axis=1)
            acc[...] += jnp.dot(ht_ref[...], dp, preferred_element_type=F32)
            bacc[...] += jnp.sum(dp.astype(F32), axis=0, keepdims=True)

            @pl.when(i == n_i - 1)
            def _():
                gw_ref[...] = acc[...].astype(BF16)
                gb_ref[...] = bacc[...]
        return body

    def call(name, pieces, n_tiles, nat_tile, prev):
        tn = sum(w for w, _ in pieces)
        in_specs = [pl.BlockSpec((d, tm), lambda j, i: (0, i))]
        in_specs += [pl.BlockSpec((tm, w), lambda j, i, f=f: (i, f(j))) for w, f in pieces]
        args = [ht] + [dproj] * len(pieces)
        aliases = {}
        if prev is not None:
            in_specs = [pl.BlockSpec(memory_space=pl.ANY)] * 2 + in_specs
            args = list(prev) + args
            aliases = {0: 0, 1: 1}
        return pl.pallas_call(
            make_body(0 if prev is None else 2, len(pieces)), name=name, grid=(n_tiles, n_i), in_specs=in_specs,
            out_specs=[pl.BlockSpec((d, tn), lambda j, i: (0, nat_tile(j))), pl.BlockSpec((1, tn), lambda j, i: (0, nat_tile(j)))],
            out_shape=[jax.ShapeDtypeStruct((d, lay.din), BF16), jax.ShapeDtypeStruct((1, lay.din), F32)],
            input_output_aliases=aliases,
            scratch_shapes=[pltpu.VMEM((d, tn), F32), pltpu.VMEM((1, tn), F32)],
            compiler_params=_params("arbitrary", "arbitrary"))(*args)

    attn_pieces = [(SLAB, lambda j, m=m: (m % 2) * PAIR_SLABS + 2 * j + m // 2) for m in range(4)]
    first = call("grad_w_in_attn", attn_pieces, ATT // 512, lambda j: j, None)
    base = lay.c0 // CONV_TILE
    nct = lay.n_conv_tiles
    if nct % 2:
        return call("grad_w_in_rest", [(CONV_TILE, lambda j: base + j)], 6 * d // CONV_TILE, lay.rest_nat_tile, first)
    half = nct // 2

    def rest_piece(m):
        def perm_tile(j):
            conv = base + 4 * (2 * (j % half) + m) + j // half
            return jnp.where(j < 4 * half, conv, base + 2 * j + m)
        return (CONV_TILE, perm_tile)

    return call("grad_w_in_rest", [rest_piece(0), rest_piece(1)], 6 * d // 512, lambda j: ATT // 512 + j, first)


def _pack_rows(parts, width=128):
    flat = [p.reshape(-1) for p in parts]
    spans, rows = [], 0
    padded = []
    for f in flat:
        n = -(-f.shape[0] // (8 * width)) * 8
        padded.append(jnp.pad(f, (0, n * width - f.shape[0])).reshape(n, width))
        spans.append((rows, f.shape[0]))
        rows += n
    return jnp.concatenate(padded, axis=0), spans


def _unpack_rows(packed, spans, shapes, width=128):
    out = []
    for (row, n), shp in zip(spans, shapes):
        rows = -(-n // width)
        out.append(packed[row:row + rows].reshape(-1)[:n].reshape(shp))
    return out


def kernel(x, c, w_ada, b_ada, w_in, b_in, conv_w, w_proj_attn, w_proj_conv, w_out, b_out, ln_g, ln_b, loss_target, m_w_ada, m_b_ada, m_w_in, m_b_in, m_conv_w, m_w_proj_attn, m_w_proj_conv, m_w_out, m_b_out, m_ln_g, m_ln_b, v_w_ada, v_b_ada, v_w_in, v_b_in, v_conv_w, v_w_proj_attn, v_w_proj_conv, v_w_out, v_b_out, v_ln_g, v_ln_b):
    bsz, seq, d = x.shape
    t = bsz * seq
    lay = _Layout(d)
    col_sharded = [True, True, False, False]
    red_w = [w_in[0], w_proj_attn[0], w_proj_conv[0], w_out[0]]
    chip = 2 * lax.axis_index("x") + lax.axis_index("y")
    chip1 = chip.astype(jnp.int32).reshape(1)
    core1 = lax.axis_index("c").astype(jnp.int32).reshape(1)
    place = jnp.stack([chip, lax.axis_index("c")]).astype(jnp.int32)
    x2 = x.reshape(t, d)
    target2 = loss_target.reshape(t, d)

    mod, act_all = _ada_exchange(c, w_ada[0], b_ada)
    mod3 = mod.reshape(bsz, 1, 3 * d)

    cw_pad = jnp.pad(conv_w[0], ((0, 5), (0, 0))) + 0.0 * mod[0, 0]
    own_in_full = [_cast_into_full(red_w[w], col_sharded[w], chip1, f"cast_shard_{w}") for w in range(4)]
    (wi_f,), cw8 = _gather_weights(own_in_full[:1], col_sharded[:1], cw_pad)
    cw_full = cw8[0:3]
    late_copies = _direct_gather_copies(col_sharded[1:])
    late_send, late_recv, late_flying, late_token = _start_copies("gather_late_start", own_in_full[1:], (18,), cw8, late_copies)
    w_all = _permute_w_in(wi_f, lay)
    b_all = lay.perm_vector(b_in) + late_token

    rest_tn = 1024 if (6 * d) % 1024 == 0 else 512
    pa, = _project(x2, mod3, w_all, b_all, seq, 0, ATT, PAIR_COLS, BF16, False, "project_attn")
    pr, ht = _project(x2, mod3, w_all, b_all, seq, lay.c0, 6 * d, rest_tn, BF16, True, "project_rest")
    o_all, lse_all, a_in = _attn_fwd(pa, bsz, seq)
    b_in_act = _conv_fwd(pr, cw_full, bsz, seq, d)
    wpa_f, wpc_f, wo_f = _wait_copies("gather_late_wait", late_flying, late_send, late_recv, b_in_act, late_copies)
    (dproj, da_in, db_in, gx0, dgate, small_tail, gw_pa, gw_pc, gw_out) = _tail(
        a_in, b_in_act, pr, x2, target2, mod3, wpa_f, wpc_f, wo_f, b_out, ln_g, ln_b, seq, lay)

    def mixers_backward(token):
        dp, gcw_ = _conv_bwd(dproj, db_in, pr, cw_full + token, bsz, seq, lay)
        dp = _attn_bwd(dp, pa, o_all, lse_all, da_in, bsz, seq)
        gw_in_bf_, gb_in_ = _grad_w_in(ht, dp, seq, lay)
        return (dp, gcw_, gw_in_bf_, gb_in_), gw_in_bf_

    late_views = _shard_views([gw_pa, gw_pc, gw_out], col_sharded[1:])
    late_got = _pair_exchange_halves(late_views, "grads_pair_exchange_late")
    late_parts = [_pair_sum(late_views[w], late_got[w], core1, f"grads_pair_sum_late_{w}") for w in range(3)]
    late_red, (dproj, gcw, gw_in_bf, gb_in) = _reduce_over_chips(
        late_parts, col_sharded[1:], place, late_parts[0], mixers_backward, "late")

    def input_backward(token):
        gx, dm = _grad_h(dproj, w_all, gx0, x2, mod3 + token, seq, lay)
        return (gx, dm), gx

    in_view = _shard_views([gw_in_bf], col_sharded[:1])
    in_got = _pair_exchange_halves(in_view, "grads_pair_exchange_in")
    in_part = _pair_sum(in_view[0], in_got[0], core1, "grads_pair_sum_in")
    in_red, (grad_x2, dmod) = _reduce_over_chips([in_part], col_sharded[:1], place, late_red[0], input_backward, "in")
    g_red = _pair_join_halves(in_red + late_red)

    d_ada = jnp.concatenate([dmod[:, 0, :], dmod[:, 1, :], dgate[:, 0, :]], axis=1)
    pieces = [small_tail[3], jnp.sum(d_ada, axis=0), gb_in[0], small_tail[0], small_tail[1], small_tail[2], gcw[0:3]]
    packed, spans = _pack_rows(pieces)
    kept_in, _ = _pack_rows([d_ada])
    summed, kept = _all_sum_small(jnp.concatenate([packed, kept_in], axis=0), packed.shape[0], d)
    loss = summed[0, 0]
    _, g_b_ada, g_b_in, g_b_out, g_ln_g, g_ln_b, g_cw_full = _unpack_rows(
        summed, spans, [(d,), (3 * d,), (lay.din,), (d,), (d,), (d,), (3, d)])
    g_cw = lax.dynamic_slice(g_cw_full, (0, chip * (d // N_CHIPS)), (3, d // N_CHIPS))
    d_ada_all = kept.reshape(N_DEV, -1)[:, :bsz * 3 * d].reshape(N_DEV * bsz, 3 * d)
    ada_cols = 3 * d // N_CHIPS
    g_w_ada = _grad_w_ada_cols(act_all.T, lax.dynamic_slice(d_ada_all, (0, chip * ada_cols), (N_DEV * bsz, ada_cols)))

    big_w = [w_ada[0]] + red_w
    big_m = [m_w_ada[0], m_w_in[0], m_w_proj_attn[0], m_w_proj_conv[0], m_w_out[0]]
    big_v = [v_w_ada[0], v_w_in[0], v_w_proj_attn[0], v_w_proj_conv[0], v_w_out[0]]
    g_big = [g_w_ada] + g_red
    big_out = [_adamw(big_w[w], g_big[w], big_m[w], big_v[w], f"adamw_{w}") for w in range(5)]
    small_w = [b_ada, b_in, conv_w[0], b_out, ln_g, ln_b]
    small_g = [g_b_ada, g_b_in, g_cw, g_b_out, g_ln_g, g_ln_b]
    small_m = [m_b_ada, m_b_in, m_conv_w[0], m_b_out, m_ln_g, m_ln_b]
    small_v = [v_b_ada, v_b_in, v_conv_w[0], v_b_out, v_ln_g, v_ln_b]
    pw, sp = _pack_rows(small_w)
    pg, _ = _pack_rows(small_g)
    pm, _ = _pack_rows(small_m)
    pv, _ = _pack_rows(small_v)
    sd, sm, sv = _adamw(pw, pg, pm, pv, "adamw_small")
    shapes = [a.shape for a in small_w]
    sd, sm, sv = _unpack_rows(sd, sp, shapes), _unpack_rows(sm, sp, shapes), _unpack_rows(sv, sp, shapes)

    def order(wa, bA, wi, bI, cw, wpa, wpc, wo, bO, lg, lb):
        return (wa[None], bA, wi[None], bI, cw[None], wpa[None], wpc[None], wo[None], bO, lg, lb)

    sg = [g.reshape(s) for g, s in zip(small_g, shapes)]
    grads_out = order(g_big[0], sg[0], g_big[1], sg[1], sg[2], g_big[2], g_big[3], g_big[4], sg[3], sg[4], sg[5])
    outs = []
    for idx, small in enumerate((sd, sm, sv)):
        outs.append(order(big_out[0][idx], small[0], big_out[1][idx], small[1], small[2], big_out[2][idx],
                          big_out[3][idx], big_out[4][idx], small[3], small[4], small[5]))
    return (loss, grad_x2.reshape(bsz, seq, d), *grads_out, *outs[0], *outs[1], *outs[2])
```

```python
import functools

import jax
import jax.numpy as jnp
from jax import lax
from jax.experimental import pallas as pl
from jax.experimental.pallas import tpu as pltpu

F32 = jnp.float32
BF16 = jnp.bfloat16
MESH = pl.DeviceIdType.MESH

HEAD_DIM = 64
N_GROUPS = 3
DILATIONS = (1, 4, 16)
N_HEADS = 12
SUB = 128
Q_WIDTH = 768
Z_WIDTH = 256
ATT = 3 * Q_WIDTH + Z_WIDTH
SLAB = 128
PAIR_SLABS = 10
PAIR_COLS = PAIR_SLABS * SLAB
CONV_TILE = 256
SOFTMAX_ROWS = 32
BLOCKS_PER_TRIP = 8
ALIBI_MAX_EXP = 8.0
ALPHA = 2.0 ** 0.25
LN_EPS = 1e-5
ADAM_LR, ADAM_B1, ADAM_B2, ADAM_EPS, ADAM_WD, ADAM_STEP = 0.001, 0.9, 0.999, 1e-08, 0.01, 10
N_CHIPS = 4
N_DEV = 8
VMEM_LIMIT_V7X = 60 * 1024 * 1024
NEG = -1e30


def _params(*sem):
    return pltpu.CompilerParams(dimension_semantics=sem, vmem_limit_bytes=VMEM_LIMIT_V7X)


def _sigmoid(v):
    return 1.0 / (1.0 + jnp.exp(-v))


class _Layout:
    def __init__(self, d):
        self.d = d
        self.din = ATT + 6 * d
        c0 = 3072
        while c0 % (2 * d):
            c0 += 1024
        self.c0, self.g0, self.np = c0, c0 + 4 * d, c0 + 6 * d
        self.n_conv_tiles = d // CONV_TILE

    def attn_nat_slab(self, s):
        p, i = s // PAIR_SLABS, s % PAIR_SLABS
        return jnp.where(i < 9, (i // 3) * 6 + (i % 3) * 2 + p, 18 + p)

    def rest_nat_tile(self, t):
        n4 = 4 * self.n_conv_tiles
        conv = ATT // CONV_TILE + (t % 4) * self.n_conv_tiles + t // 4
        return jnp.where(t < n4, conv, ATT // CONV_TILE + t)

    def perm_vector(self, v):
        parts = []
        for s in range(2 * PAIR_SLABS):
            p, i = divmod(s, PAIR_SLABS)
            ns = (i // 3) * 6 + (i % 3) * 2 + p if i < 9 else 18 + p
            parts.append(v[:, ns * SLAB:(ns + 1) * SLAB])
        parts.append(jnp.zeros((1, self.c0 - ATT), v.dtype))
        for j in range(self.n_conv_tiles):
            for k in range(4):
                a = ATT + k * self.d + j * CONV_TILE
                parts.append(v[:, a:a + CONV_TILE])
        parts.append(v[:, ATT + 4 * self.d:])
        return jnp.concatenate(parts, axis=1)


def _place():
    return lax.axis_index("x"), lax.axis_index("y"), lax.axis_index("c")


def _other_chips(x, y):
    return [(1 - x, y), (x, 1 - y), (1 - x, 1 - y)]


def _shard_of(ref, col_sharded, chip, half=None):
    if col_sharded:
        cs = ref.shape[1] // N_CHIPS
        cols = pl.ds(pl.multiple_of(chip * cs, SLAB), cs)
        if half is None:
            return ref.at[:, cols]
        n = ref.shape[0] // 2
        return ref.at[pl.ds(half * n, n), cols]
    rs = ref.shape[0] // N_CHIPS
    if half is None:
        return ref.at[pl.ds(chip * rs, rs)]
    return ref.at[pl.ds(chip * rs + half * (rs // 2), rs // 2)]


GATHER_PIECES = 4


def _cast_into_full(shard, col_sharded, chip, name):
    rows, cols = shard.shape
    tr = _row_tile(rows, cols)
    nb = rows // tr

    def body(chip_ref, s_ref, o_ref):
        del chip_ref
        o_ref[...] = s_ref[...].astype(BF16)

    if col_sharded:
        full, out_spec = (rows, cols * N_CHIPS), pl.BlockSpec((tr, cols), lambda i, ch: (i, ch[0]))
    else:
        full, out_spec = (rows * N_CHIPS, cols), pl.BlockSpec((tr, cols), lambda i, ch: (ch[0] * nb + i, 0))
    return pl.pallas_call(
        body, name=name,
        grid_spec=pltpu.PrefetchScalarGridSpec(num_scalar_prefetch=1, grid=(nb,),
                                               in_specs=[pl.BlockSpec((tr, cols), lambda i, ch: (i, 0))], out_specs=out_spec),
        out_shape=jax.ShapeDtypeStruct(full, BF16), compiler_params=_params("parallel"))(chip, shard)


def _gather_weights(fulls, col_sharded, small):
    n = len(fulls)
    kp = GATHER_PIECES

    def piece(ref, cs, chip, half, k):
        if cs:
            width = ref.shape[1] // N_CHIPS
            rows = ref.shape[0] // 2 // kp
            return ref.at[pl.ds(half * (ref.shape[0] // 2) + k * rows, rows), pl.ds(pl.multiple_of(chip * width, SLAB), width)]
        rs = ref.shape[0] // N_CHIPS
        rows = rs // 2 // kp
        return ref.at[pl.ds(chip * rs + half * (rs // 2) + k * rows, rows)]

    def body(*refs):
        sm_in, outs, sm_out = refs[n], refs[n + 1:2 * n + 1], refs[2 * n + 1]
        send, recv, fsend, frecv, lsem, ssend, srecv = refs[2 * n + 2:]
        x, y, c = _place()
        mine = 2 * x + y
        sibling = (x, y, 1 - c)
        first = (x ^ (1 - c), y ^ c)
        second = (x ^ c, y ^ (1 - c))
        diagonal = (1 - x, 1 - y)
        sources = [first, second, diagonal]
        senders = [first, second, second]

        def copy(ref, sems, slot, to):
            return pltpu.make_async_remote_copy(src_ref=ref, dst_ref=ref, send_sem=sems[0].at[slot], recv_sem=sems[1].at[slot],
                                                device_id=to, device_id_type=MESH)

        local = pltpu.make_async_copy(sm_in, _shard_of(sm_out, True, mine), lsem)
        local.start()
        sends = []
        for k, (cx, cy) in enumerate(_other_chips(x, y)):
            cp = pltpu.make_async_remote_copy(src_ref=sm_in, dst_ref=_shard_of(sm_out, True, mine), send_sem=ssend.at[k],
                                              recv_sem=srecv.at[k], device_id=(cx, cy, c), device_id_type=MESH)
            cp.start()
            sends.append(cp)
        for k in range(kp):
            for w in range(n):
                own = piece(outs[w], col_sharded[w], mine, c, k)
                for slot, chip in enumerate((first, second)):
                    cp = copy(own, (send, recv), (w * 3 + slot) * kp + k, (*chip, c))
                    cp.start()
                    sends.append(cp)
        for slot in range(3):
            source = 2 * sources[slot][0] + sources[slot][1]
            for k in range(kp):
                for w in range(n):
                    landed = piece(outs[w], col_sharded[w], source, c, k)
                    copy(landed, (send, recv), (w * 3 + slot) * kp + k, (*senders[slot], c)).wait_recv()
                    if slot == 0:
                        cp = copy(landed, (send, recv), (w * 3 + 2) * kp + k, (*second, c))
                        cp.start()
                        sends.append(cp)
                    cp = copy(landed, (fsend, frecv), (w * 3 + slot) * kp + k, sibling)
                    cp.start()
                    sends.append(cp)
        for slot, chip in enumerate((second, first, diagonal)):
            for k in range(kp):
                for w in range(n):
                    passed = piece(outs[w], col_sharded[w], 2 * chip[0] + chip[1], 1 - c, k)
                    copy(passed, (fsend, frecv), (w * 3 + slot) * kp + k, sibling).wait_recv()
        for k, (cx, cy) in enumerate(_other_chips(x, y)):
            theirs = _shard_of(sm_out, True, 2 * cx + cy)
            pltpu.make_async_remote_copy(src_ref=theirs, dst_ref=theirs, send_sem=ssend.at[k], recv_sem=srecv.at[k],
                                         device_id=(cx, cy, c), device_id_type=MESH).wait_recv()
        for cp in sends:
            cp.wait_send()
        local.wait()

    any_spec = pl.BlockSpec(memory_space=pl.ANY)
    outs = pl.pallas_call(
        body, name="gather_weights",
        out_shape=[jax.ShapeDtypeStruct(f.shape, BF16) for f in fulls]
        + [jax.ShapeDtypeStruct((small.shape[0], small.shape[1] * N_CHIPS), small.dtype)],
        in_specs=[any_spec] * (n + 1), out_specs=[any_spec] * (n + 1), input_output_aliases={w: w for w in range(n)},
        scratch_shapes=[pltpu.SemaphoreType.DMA((n * 3 * kp,)), pltpu.SemaphoreType.DMA((n * 3 * kp,)),
                        pltpu.SemaphoreType.DMA((n * 3 * kp,)), pltpu.SemaphoreType.DMA((n * 3 * kp,)), pltpu.SemaphoreType.DMA,
                        pltpu.SemaphoreType.DMA((3,)), pltpu.SemaphoreType.DMA((3,))],
    )(*fulls, small)
    return outs[:n], outs[n]


HBM_SPEC = pl.BlockSpec(memory_space=pltpu.HBM)
SEM_SPEC = pl.BlockSpec(memory_space=pltpu.SEMAPHORE)
DATAFLOW = pltpu.SideEffectType.DATAFLOW_SIDE_EFFECTING


def _start_copies(name, arrays, sem_shape, after, copies):
    n = len(arrays)

    def body(*refs):
        for cp in copies(refs[:n], refs[n + 1], refs[n + 2]):
            cp.start()
        token = refs[2 * n + 3]
        token[...] = jnp.zeros_like(token)

    res = pl.pallas_call(
        body, name=name,
        out_shape=(pltpu.SemaphoreType.DMA(sem_shape), pltpu.SemaphoreType.DMA(sem_shape),
                   *[pltpu.HBM(a.shape, a.dtype) for a in arrays], jax.ShapeDtypeStruct((8, 128), F32)),
        in_specs=[HBM_SPEC] * n + [pl.BlockSpec(memory_space=pl.ANY)],
        out_specs=(SEM_SPEC, SEM_SPEC, *([HBM_SPEC] * n), pl.BlockSpec(memory_space=pltpu.VMEM)),
        input_output_aliases={i: 2 + i for i in range(n)},
        compiler_params=pltpu.CompilerParams(has_side_effects=DATAFLOW),
    )(*[pltpu.with_memory_space_constraint(a, pltpu.HBM) for a in arrays], after)
    return res[0], res[1], list(res[2:2 + n]), res[2 + n][0, 0]


def _wait_copies(name, arrays, send, recv, after, copies):
    n = len(arrays)

    def body(*refs):
        for cp in copies(refs[:n], refs[n], refs[n + 1]):
            cp.wait_send()
            cp.wait_recv()

    return pl.pallas_call(
        body, name=name, out_shape=[pltpu.HBM(a.shape, a.dtype) for a in arrays],
        in_specs=[HBM_SPEC] * n + [SEM_SPEC, SEM_SPEC, pl.BlockSpec(memory_space=pl.ANY)], out_specs=[HBM_SPEC] * n,
        input_output_aliases={i: i for i in range(n)},
        compiler_params=pltpu.CompilerParams(has_side_effects=DATAFLOW),
    )(*arrays, send, recv, after)


def _direct_gather_copies(col_sharded):
    def copies(refs, send, recv):
        x, y, c = _place()
        mine = 2 * x + y
        out = []
        for w, ref in enumerate(refs):
            own_half = _shard_of(ref, col_sharded[w], mine, c)
            k = 0
            for cx, cy in _other_chips(x, y):
                for pc in (c, 1 - c):
                    out.append(pltpu.make_async_remote_copy(
                        src_ref=own_half, dst_ref=own_half, send_sem=send.at[6 * w + k], recv_sem=recv.at[6 * w + k],
                        device_id=(cx, cy, pc), device_id_type=MESH))
                    k += 1
        return out
    return copies


def _chip_scatter_copies(n, col_sharded):
    def piece(ref, cs, chip):
        if cs:
            w = ref.shape[2] // N_CHIPS
            return ref.at[:, :, pl.ds(pl.multiple_of(chip * w, SLAB), w)]
        return ref.at[pl.ds(chip, 1)]

    def copies(refs, send, recv):
        x, y, c = _place()
        out = []
        for k, (cx, cy) in enumerate(_other_chips(x, y)):
            for w in range(n):
                out.append(pltpu.make_async_remote_copy(
                    src_ref=piece(refs[w], col_sharded[w], 2 * cx + cy), dst_ref=refs[n + w].at[pl.ds(k, 1)],
                    send_sem=send.at[3 * w + k], recv_sem=recv.at[3 * w + k], device_id=(cx, cy, c), device_id_type=MESH))
        return out
    return copies


def _shard_views(gs, col_sharded):
    return [g.reshape(1, *g.shape) if cs else g.reshape(N_CHIPS, g.shape[0] // N_CHIPS, g.shape[1])
            for g, cs in zip(gs, col_sharded)]


DMA_CHUNK_BYTES = 1 << 20


def _chunk_rows(shape, itemsize):
    s, rows, cols = shape
    n = 1
    while s * (rows // n) * cols * itemsize > DMA_CHUNK_BYTES and (rows // n) % 32 == 0:
        n *= 2
    return rows // n


def _send_rows(src, src_row0, dst, dst_row0, rows, send_sem, recv_sem, device):
    step = _chunk_rows((src.shape[0], rows, src.shape[2]), src.dtype.itemsize)
    for r in range(0, rows, step):
        pltpu.make_async_remote_copy(src_ref=src.at[:, pl.ds(src_row0 + r, step)], dst_ref=dst.at[:, pl.ds(dst_row0 + r, step)],
                                     send_sem=send_sem, recv_sem=recv_sem, device_id=device, device_id_type=MESH).start()


def _pair_exchange_halves(views, name):
    n = len(views)
    half_shapes = [(v.shape[0], v.shape[1] // 2, v.shape[2]) for v in views]

    def body(*refs):
        ins, got = refs[:n], refs[n:2 * n]
        send, recv = refs[2 * n:]
        x, y, c = _place()
        sibling = (x, y, 1 - c)
        for w in range(n):
            hr = half_shapes[w][1]
            _send_rows(ins[w], (1 - c) * hr, got[w], 0, hr, send.at[w], recv.at[w], sibling)
        for w in range(n):
            hr = half_shapes[w][1]
            pltpu.make_async_remote_copy(src_ref=ins[w].at[:, pl.ds((1 - c) * hr, hr)], dst_ref=got[w], send_sem=send.at[w],
                                         recv_sem=recv.at[w], device_id=sibling, device_id_type=MESH).wait()

    any_spec = pl.BlockSpec(memory_space=pl.ANY)
    return pl.pallas_call(
        body, name=name,
        out_shape=[jax.ShapeDtypeStruct(s, v.dtype) for s, v in zip(half_shapes, views)],
        in_specs=[any_spec] * n, out_specs=[any_spec] * n,
        scratch_shapes=[pltpu.SemaphoreType.DMA((n,)), pltpu.SemaphoreType.DMA((n,))],
    )(*views)


def _pair_sum(view, got, core, name):
    s, r, cols = view.shape
    hr = r // 2
    tr = _row_tile(hr, cols)
    nb = hr // tr

    def body(core_ref, a_ref, b_ref, o_ref):
        del core_ref
        o_ref[...] = (a_ref[...].astype(F32) + b_ref[...].astype(F32)).astype(BF16)

    same = pl.BlockSpec((None, tr, cols), lambda j, i, core_ref: (j, i, 0))
    return pl.pallas_call(
        body, name=name,
        grid_spec=pltpu.PrefetchScalarGridSpec(
            num_scalar_prefetch=1, grid=(s, nb),
            in_specs=[pl.BlockSpec((None, tr, cols), lambda j, i, core_ref: (j, core_ref[0] * nb + i, 0)), same],
            out_specs=same),
        out_shape=jax.ShapeDtypeStruct((s, hr, cols), BF16), compiler_params=_params("parallel", "parallel"))(core, view, got)


def _piece_cols(part, col_sharded):
    return part.shape[2] // N_CHIPS if col_sharded else part.shape[2]


def _reduce_over_chips(parts, col_sharded, place, after, between, tag):
    n = len(parts)
    lands = [lax.empty((3, p.shape[1], _piece_cols(p, cs)), p.dtype) for p, cs in zip(parts, col_sharded)]
    copies = _chip_scatter_copies(n, col_sharded)
    send, recv, flying, token = _start_copies(f"grads_scatter_{tag}_start", list(parts) + lands, (3 * n,), after, copies)
    done, marker = between(token)
    landed = _wait_copies(f"grads_scatter_{tag}_wait", flying, send, recv, marker, copies)
    reduced = [_chip_sum(landed[w], landed[n + w], col_sharded[w], place, f"grads_chip_sum_{tag}_{w}") for w in range(n)]
    return reduced, done


def _chip_sum(part, got, col_sharded, place, name):
    _, hr, _ = part.shape
    cols = _piece_cols(part, col_sharded)
    tr = _row_tile(hr, cols)
    nb = hr // tr

    def body(place_ref, own_ref, g0_ref, g1_ref, g2_ref, o_ref):
        del place_ref
        acc = own_ref[...].astype(F32) + g0_ref[...].astype(F32)
        o_ref[...] = acc + g1_ref[...].astype(F32) + g2_ref[...].astype(F32)

    if col_sharded:
        own = pl.BlockSpec((None, tr, cols), lambda i, pr: (0, i, pr[0]))
    else:
        own = pl.BlockSpec((None, tr, cols), lambda i, pr: (pr[0], i, 0))
    others = [pl.BlockSpec((None, tr, cols), lambda i, pr, k=k: (k, i, 0)) for k in range(3)]
    return pl.pallas_call(
        body, name=name,
        grid_spec=pltpu.PrefetchScalarGridSpec(
            num_scalar_prefetch=1, grid=(nb,), in_specs=[own] + others,
            out_specs=pl.BlockSpec((tr, cols), lambda i, pr: (pr[1] * nb + i, 0))),
        out_shape=jax.ShapeDtypeStruct((2 * hr, cols), F32), compiler_params=_params("parallel"))(place, part, got, got, got)


def _pair_join_halves(fulls):
    n = len(fulls)
    views = [f.reshape(1, *f.shape) for f in fulls]

    def body(*refs):
        outs = refs[n:2 * n]
        send, recv = refs[2 * n:]
        x, y, c = _place()
        sibling = (x, y, 1 - c)
        for w in range(n):
            hr = outs[w].shape[1] // 2
            _send_rows(outs[w], c * hr, outs[w], c * hr, hr, send.at[w], recv.at[w], sibling)
        for w in range(n):
            hr = outs[w].shape[1] // 2
            pltpu.make_async_remote_copy(
                src_ref=outs[w].at[:, pl.ds(c * hr, hr)], dst_ref=outs[w].at[:, pl.ds((1 - c) * hr, hr)], send_sem=send.at[w],
                recv_sem=recv.at[w], device_id=sibling, device_id_type=MESH).wait()

    any_spec = pl.BlockSpec(memory_space=pl.ANY)
    outs = pl.pallas_call(
        body, name="grads_pair_join",
        out_shape=[jax.ShapeDtypeStruct(v.shape, v.dtype) for v in views],
        in_specs=[any_spec] * n, out_specs=[any_spec] * n, input_output_aliases={w: w for w in range(n)},
        scratch_shapes=[pltpu.SemaphoreType.DMA((n,)), pltpu.SemaphoreType.DMA((n,))],
    )(*views)
    return [o[0] for o in outs]


def _all_sum_small(vec, n_sum, d):
    rows = vec.shape[0]

    def body(v_ref, sum_ref, kept_ref, all_ref, send, recv):
        x, y, c = _place()
        me = 4 * x + 2 * y + c
        all_ref[me] = v_ref[...]
        peers = _all_devices(x, y, c)
        copies = []
        for k, (peer, _) in enumerate(peers):
            cp = pltpu.make_async_remote_copy(src_ref=v_ref, dst_ref=all_ref.at[me], send_sem=send.at[k], recv_sem=recv.at[k],
                                              device_id=peer, device_id_type=MESH)
            cp.start()
            copies.append(cp)
        for k, (_, src) in enumerate(peers):
            pltpu.make_async_remote_copy(src_ref=v_ref, dst_ref=all_ref.at[src], send_sem=send.at[k], recv_sem=recv.at[k],
                                         device_id=(x, y, c), device_id_type=MESH).wait_recv()
        for cp in copies:
            cp.wait_send()
        total = all_ref[0, 0:n_sum, :]
        for i in range(1, N_DEV):
            total = total + all_ref[i, 0:n_sum, :]
        sum_ref[...] = total
        loss = 0.5 / d * jnp.sum(total[0:8, :])
        sum_ref[0:8, :] = jnp.full((8, 128), loss, F32)
        for i in range(N_DEV):
            kept_ref[i] = all_ref[i, n_sum:rows, :]

    vm = pl.BlockSpec(memory_space=pltpu.VMEM)
    return pl.pallas_call(
        body, name="all_sum_small",
        out_shape=[jax.ShapeDtypeStruct((n_sum, 128), F32), jax.ShapeDtypeStruct((N_DEV, rows - n_sum, 128), F32)],
        in_specs=[vm], out_specs=[vm, vm],
        scratch_shapes=[pltpu.VMEM((N_DEV, rows, 128), F32), pltpu.SemaphoreType.DMA((N_DEV - 1,)),
                        pltpu.SemaphoreType.DMA((N_DEV - 1,))],
        compiler_params=pltpu.CompilerParams(vmem_limit_bytes=VMEM_LIMIT_V7X),
    )(vec)


def _row_tile(rows, cols, itemsize=4, budget=2 << 20):
    t = rows
    while t * cols * itemsize > budget and t % 16 == 0:
        t //= 2
    return t


def _adamw(w, g, m, v, name):
    rows, cols = w.shape
    tr = _row_tile(rows, cols, budget=1 << 20)

    def body(w_ref, g_ref, m_ref, v_ref, d_ref, nm_ref, nv_ref):
        g_ = g_ref[...]
        nm = ADAM_B1 * m_ref[...] + (1.0 - ADAM_B1) * g_
        nv = ADAM_B2 * v_ref[...] + (1.0 - ADAM_B2) * (g_ * g_)
        m_hat = nm / (1.0 - ADAM_B1 ** ADAM_STEP)
        v_hat = nv / (1.0 - ADAM_B2 ** ADAM_STEP)
        d_ref[...] = -ADAM_LR * (m_hat / (jnp.sqrt(v_hat) + ADAM_EPS) + ADAM_WD * w_ref[...])
        nm_ref[...] = nm
        nv_ref[...] = nv

    spec = pl.BlockSpec((tr, cols), lambda i: (i, 0))
    shp = jax.ShapeDtypeStruct((rows, cols), F32)
    return pl.pallas_call(body, name=name, grid=(rows // tr,), in_specs=[spec] * 4, out_specs=[spec] * 3,
                          out_shape=[shp] * 3, compiler_params=_params("parallel"))(w, g, m, v)


def _all_devices(x, y, c):
    out = []
    for k in range(1, N_DEV):
        peer = (x ^ ((k >> 2) & 1), y ^ ((k >> 1) & 1), c ^ (k & 1))
        out.append((peer, 4 * peer[0] + 2 * peer[1] + peer[2]))
    return out


def _ada_exchange(c, w_shard, b_ada):
    bsz, d = c.shape
    cs = w_shard.shape[1]

    def body(c_ref, w_ref, b_ref, mod_ref, act_ref, c_all, part, pieces, csend, crecv, psend, precv):
        x, y, core = _place()
        me = 4 * x + 2 * y + core
        chip = 2 * x + y
        c_all[me] = c_ref[...]
        peers = _all_devices(x, y, core)
        copies = []
        for k, (peer, _) in enumerate(peers):
            cp = pltpu.make_async_remote_copy(src_ref=c_ref, dst_ref=c_all.at[me], send_sem=csend.at[k], recv_sem=crecv.at[k],
                                              device_id=peer, device_id_type=MESH)
            cp.start()
            copies.append(cp)
        for k, (_, src) in enumerate(peers):
            pltpu.make_async_remote_copy(src_ref=c_ref, dst_ref=c_all.at[src], send_sem=csend.at[k], recv_sem=crecv.at[k],
                                         device_id=(x, y, core), device_id_type=MESH).wait_recv()
        rows = jnp.concatenate([c_all[i] for i in range(N_DEV)], axis=0)
        act = rows * _sigmoid(rows)
        act_ref[...] = act
        prod = jnp.dot(act.astype(BF16), w_ref[...].astype(BF16), preferred_element_type=F32)
        for i in range(N_DEV):
            part[i] = prod[i * bsz:(i + 1) * bsz, :]
        pieces[chip] = part[me]
        chips = _other_chips(x, y)
        for k, (cx, cy) in enumerate(chips):
            cp = pltpu.make_async_remote_copy(src_ref=part.at[4 * cx + 2 * cy + core], dst_ref=pieces.at[chip],
                                              send_sem=psend.at[k], recv_sem=precv.at[k], device_id=(cx, cy, core),
                                              device_id_type=MESH)
            cp.start()
            copies.append(cp)
        for k, (cx, cy) in enumerate(chips):
            pltpu.make_async_remote_copy(src_ref=part.at[me], dst_ref=pieces.at[2 * cx + cy], send_sem=psend.at[k],
                                         recv_sem=precv.at[k], device_id=(cx, cy, core), device_id_type=MESH).wait_recv()
        for cp in copies:
            cp.wait_send()
        mod_ref[...] = jnp.concatenate([pieces[j] for j in range(N_CHIPS)], axis=1) + b_ref[...]

    vm = pl.BlockSpec(memory_space=pltpu.VMEM)
    return pl.pallas_call(
        body, name="ada_exchange", in_specs=[vm] * 3, out_specs=[vm] * 2,
        out_shape=[jax.ShapeDtypeStruct((bsz, 3 * d), F32), jax.ShapeDtypeStruct((N_DEV * bsz, d), F32)],
        scratch_shapes=[pltpu.VMEM((N_DEV, bsz, d), F32), pltpu.VMEM((N_DEV, bsz, cs), F32), pltpu.VMEM((N_CHIPS, bsz, cs), F32),
                        pltpu.SemaphoreType.DMA((N_DEV - 1,)), pltpu.SemaphoreType.DMA((N_DEV - 1,)),
                        pltpu.SemaphoreType.DMA((3,)), pltpu.SemaphoreType.DMA((3,))],
        compiler_params=pltpu.CompilerParams(vmem_limit_bytes=VMEM_LIMIT_V7X))(c, w_shard, b_ada)


def _grad_w_ada_cols(act_t, d_cols):
    d, n = act_t.shape
    cs = d_cols.shape[1]

    def body(a_ref, g_ref, o_ref):
        a, g = a_ref[...], g_ref[...]
        acc = a[:, 0:1] * g[0:1, :]
        for b in range(1, n):
            acc = acc + a[:, b:b + 1] * g[b:b + 1, :]
        o_ref[...] = acc

    vm = pl.BlockSpec(memory_space=pltpu.VMEM)
    return pl.pallas_call(body, name="grad_w_ada", in_specs=[vm] * 2, out_specs=vm,
                          out_shape=jax.ShapeDtypeStruct((d, cs), F32),
                          compiler_params=pltpu.CompilerParams(vmem_limit_bytes=VMEM_LIMIT_V7X))(act_t, d_cols)


def _permute_w_in(w_nat, lay):
    d = lay.d
    group = 4

    def call(name, width, n_pieces, nat_piece, out_block0, prev):
        def body(*refs):
            refs[-1][...] = jnp.concatenate([r[...] for r in refs[:group]], axis=1)

        in_specs = [pl.BlockSpec((d, width), lambda s, m=m: (0, nat_piece(group * s + m))) for m in range(group)]
        args = [w_nat] * group
        aliases = {}
        if prev is not None:
            in_specs.append(pl.BlockSpec(memory_space=pl.ANY))
            args.append(prev)
            aliases = {group: 0}
        return pl.pallas_call(
            body, name=name, grid=(n_pieces // group,), in_specs=in_specs,
            out_specs=pl.BlockSpec((d, group * width), lambda s: (0, out_block0 + s)),
            out_shape=jax.ShapeDtypeStruct((d, lay.np), BF16), input_output_aliases=aliases,
            compiler_params=_params("arbitrary"))(*args)

    w_all = call("permute_w_attn", SLAB, 2 * PAIR_SLABS, lay.attn_nat_slab, 0, None)
    n_rest = 6 * d // CONV_TILE
    if n_rest % group:
        group = 2
    return call("permute_w_rest", CONV_TILE, n_rest, lay.rest_nat_tile, lay.c0 // (group * CONV_TILE), w_all)


def _project(x2, mod3, w_all, b_all, seq, col0, ncols, tn, out_dtype, want_ht, name):
    t, d = x2.shape
    tm = min(1024, seq)
    per_seq = seq // tm
    j0 = col0 // tn

    def body(x_ref, mod_ref, w_ref, b_ref, o_ref, *rest):
        h_ref = rest[-1]

        @pl.when(pl.program_id(1) == 0)
        def _():
            h = x_ref[...] * (1.0 + mod_ref[:, d:2 * d]) + mod_ref[:, 0:d]
            h_ref[...] = h.astype(BF16)
            if want_ht:
                rest[0][...] = h.T.astype(BF16)

        o_ref[...] = (jnp.dot(h_ref[...], w_ref[...], preferred_element_type=F32) + b_ref[...]).astype(out_dtype)

    out_shape = [jax.ShapeDtypeStruct((t, ncols), out_dtype)]
    out_specs = [pl.BlockSpec((tm, tn), lambda i, j: (i, j))]
    if want_ht:
        out_shape.append(jax.ShapeDtypeStruct((d, t), BF16))
        out_specs.append(pl.BlockSpec((d, tm), lambda i, j: (0, i)))
    return pl.pallas_call(
        body, name=name, grid=(t // tm, ncols // tn),
        in_specs=[pl.BlockSpec((tm, d), lambda i, j: (i, 0)),
                  pl.BlockSpec((None, 1, 3 * d), lambda i, j: (i // per_seq, 0, 0)),
                  pl.BlockSpec((d, tn), lambda i, j: (0, j0 + j)),
                  pl.BlockSpec((1, tn), lambda i, j: (0, j0 + j))],
        out_specs=out_specs, out_shape=out_shape,
        scratch_shapes=[pltpu.VMEM((tm, d), BF16)],
        compiler_params=_params("arbitrary", "arbitrary"))(x2, mod3, w_all, b_all)


def _slope(g, p, hh):
    head = 4 * g + 2 * p + hh
    return 2.0 ** (-ALIBI_MAX_EXP * (head + 1.0) / N_HEADS)


def _ld_rows(ref, start, n, stride):
    if stride == 1:
        return ref[pl.ds(start, n), :]
    return ref[pl.ds(start, n, stride=stride), :]


def _st_rows(ref, start, n, stride, val):
    if stride == 1:
        ref[pl.ds(start, n), :] = val
    else:
        ref[pl.ds(start, n, stride=stride), :] = val


def _sub_blocks(g, seq):
    return seq // DILATIONS[g] // SUB


def _key_rows(g, seq):
    return SUB if _sub_blocks(g, seq) == 1 else 2 * SUB


def _fill_bias(bias_ref, p, seq):
    for g in range(N_GROUPS):
        nk = _key_rows(g, seq)
        diff = lax.broadcasted_iota(jnp.int32, (SUB, nk), 0) - lax.broadcasted_iota(jnp.int32, (SUB, nk), 1)
        for i, off in enumerate((0, SUB)):
            if i == 1 and nk == SUB:
                continue
            delta = diff + off
            ok = (delta >= 0) & (delta <= SUB)
            dist = (delta * DILATIONS[g]).astype(F32)
            for hh in range(2):
                slope = jnp.where(p == 0, _slope(g, 0, hh), _slope(g, 1, hh))
                bias_ref[g, i, hh, :, 0:nk] = jnp.where(ok, -slope * dist, NEG)


def _to_sub_major(pa_ref, col, sub_ref, stage, dil, seq):
    cols = slice(col * SLAB, (col + 1) * SLAB)
    if dil == 1:
        sub_ref[...] = pa_ref[:, cols]
        return
    n = seq // dil
    stage[...] = pa_ref[:, cols].astype(F32)
    for r in range(dil):
        sub_ref[pl.ds(r * n, n), :] = stage[pl.ds(r, n, stride=dil), :].astype(BF16)


def _block_rows(it, g, seq):
    dil, nb = DILATIONS[g], _sub_blocks(g, seq)
    row0 = pl.multiple_of(it * SUB, SUB)
    if nb == 1:
        return row0, row0, 0, it
    blk = it % nb
    first = blk == 0
    krow0 = pl.multiple_of(row0 - jnp.where(first, 0, SUB), SUB)
    nat = row0 if dil == 1 else it // nb + dil * SUB * blk
    return row0, krow0, jnp.where(first, 0, 1), nat


def _nt(a, b):
    return lax.dot_general(a, b, (((1,), (1,)), ((), ())), preferred_element_type=F32)


def _tn(a, b):
    return lax.dot_general(a, b, (((0,), (0,)), ((), ())), preferred_element_type=F32)


def _head_sums(t):
    rows = t.shape[0]
    lo = jnp.broadcast_to(jnp.sum(t[:, :HEAD_DIM], axis=-1, keepdims=True), (rows, HEAD_DIM))
    hi = jnp.broadcast_to(jnp.sum(t[:, HEAD_DIM:], axis=-1, keepdims=True), (rows, HEAD_DIM))
    return jnp.concatenate([lo, hi], axis=-1)


def _attn_fwd(pa, bsz, seq):
    t = pa.shape[0]
    n_blocks = seq // SUB
    chunk = 256

    def body(pa_ref, o_ref, lse_ref, a_ref, sub, stage, bias_ref, s_buf, p_buf, l_buf):
        p = pl.program_id(1)
        _fill_bias(bias_ref, p, seq)
        head0 = lax.broadcasted_iota(jnp.int32, (SUB, SLAB), 1) < HEAD_DIM
        for g in range(N_GROUPS):
            dil = DILATIONS[g]
            for w in range(3):
                _to_sub_major(pa_ref, 3 * w + g, sub.at[w], stage, dil, seq)
            nk = _key_rows(g, seq)

            def trip(i, carry, g=g, dil=dil, nk=nk):
                places = [_block_rows(BLOCKS_PER_TRIP * i + j, g, seq) for j in range(BLOCKS_PER_TRIP)]
                for j, (row0, krow0, _, _) in enumerate(places):
                    q = sub[0, pl.ds(row0, SUB), :]
                    zero = jnp.zeros_like(q)
                    q2 = jnp.concatenate([jnp.where(head0, q, zero), jnp.where(head0, zero, q)], axis=0) * (HEAD_DIM ** -0.5)
                    s_buf[j, :, 0:nk] = _nt(q2, sub[1, pl.ds(krow0, nk), :])
                for j, (_, _, bi, _) in enumerate(places):
                    for c in range(0, 2 * SUB, SOFTMAX_ROWS):
                        hh, r = divmod(c, SUB)
                        s = s_buf[j, c:c + SOFTMAX_ROWS, 0:nk] + bias_ref[g, bi, hh, r:r + SOFTMAX_ROWS, 0:nk]
                        m = jnp.max(s, axis=-1, keepdims=True)
                        e = jnp.exp(s - m)
                        den = jnp.sum(e, axis=-1, keepdims=True)
                        p_buf[j, c:c + SOFTMAX_ROWS, 0:nk] = (e * (1.0 / den)).astype(BF16)
                        l_buf[j, c:c + SOFTMAX_ROWS, :] = jnp.broadcast_to(m + jnp.log(den), (SOFTMAX_ROWS, SLAB))
                for j, (_, krow0, _, nat) in enumerate(places):
                    o2 = jnp.dot(p_buf[j, :, 0:nk], sub[2, pl.ds(krow0, nk), :], preferred_element_type=F32)
                    _st_rows(o_ref.at[g], nat, SUB, dil, jnp.where(head0, o2[0:SUB], o2[SUB:2 * SUB]))
                    _st_rows(lse_ref.at[g], nat, SUB, dil, jnp.where(head0, l_buf[j, 0:SUB, :], l_buf[j, SUB:2 * SUB, :]))
                return carry

            lax.fori_loop(0, n_blocks // BLOCKS_PER_TRIP, trip, 0)

        def mix(i, carry):
            rows = pl.ds(pl.multiple_of(i * chunk, chunk), chunk)
            l0, l1, l2 = lse_ref[0, rows, :], lse_ref[1, rows, :], lse_ref[2, rows, :]
            m = jnp.maximum(jnp.maximum(l0, l1), l2)
            e0, e1, e2 = jnp.exp(l0 - m), jnp.exp(l1 - m), jnp.exp(l2 - m)
            tot = e0 + e1 + e2
            o = (e0 / tot) * o_ref[0, rows, :] + (e1 / tot) * o_ref[1, rows, :] + (e2 / tot) * o_ref[2, rows, :]
            z = pa_ref[rows, 9 * SLAB:10 * SLAB].astype(F32)
            a_ref[rows, :] = (o * (z * _sigmoid(z))).astype(BF16)
            return carry

        lax.fori_loop(0, seq // chunk, mix, 0)

    big = jax.ShapeDtypeStruct((N_GROUPS, t, 2 * SLAB), F32)
    return pl.pallas_call(
        body, name="attn_fwd", grid=(bsz, 2),
        in_specs=[pl.BlockSpec((seq, PAIR_COLS), lambda b, p: (b, p))],
        out_specs=[pl.BlockSpec((N_GROUPS, seq, SLAB), lambda b, p: (0, b, p)),
                   pl.BlockSpec((N_GROUPS, seq, SLAB), lambda b, p: (0, b, p)),
                   pl.BlockSpec((seq, SLAB), lambda b, p: (b, p))],
        out_shape=[big, big, jax.ShapeDtypeStruct((t, 2 * SLAB), BF16)],
        scratch_shapes=[pltpu.VMEM((3, seq, SLAB), BF16), pltpu.VMEM((seq, SLAB), F32),
                        pltpu.VMEM((N_GROUPS, 2, 2, SUB, 2 * SUB), F32), pltpu.VMEM((BLOCKS_PER_TRIP, 2 * SUB, 2 * SUB), F32),
                        pltpu.VMEM((BLOCKS_PER_TRIP, 2 * SUB, 2 * SUB), BF16), pltpu.VMEM((BLOCKS_PER_TRIP, 2 * SUB, SLAB), F32)],
        compiler_params=_params("arbitrary", "arbitrary"))(pa)


def _shift_down(v, k):
    rows = lax.broadcasted_iota(jnp.int32, v.shape, 0)
    return jnp.where(rows >= k, pltpu.roll(v, k, 0), 0.0)


def _shift_up(v, k):
    n = v.shape[0]
    rows = lax.broadcasted_iota(jnp.int32, v.shape, 0)
    return jnp.where(rows < n - k, pltpu.roll(v, n - k, 0), 0.0)


def _conv_fwd(pr, conv_w, bsz, seq, d):
    t = pr.shape[0]
    ct = CONV_TILE

    def body(p_ref, cw_ref, o_ref):
        u = p_ref[:, 2 * ct:3 * ct].astype(F32) * p_ref[:, 0:ct].astype(F32)
        cw = cw_ref[...]
        conv = cw[0:1, :] * _shift_down(u, 2)
        conv = conv + cw[1:2, :] * _shift_down(u, 1)
        conv = conv + cw[2:3, :] * u
        z = p_ref[:, 3 * ct:4 * ct].astype(F32)
        o_ref[...] = (p_ref[:, ct:2 * ct].astype(F32) * conv * (z * _sigmoid(z))).astype(BF16)

    return pl.pallas_call(
        body, name="conv_fwd", grid=(bsz, d // ct),
        in_specs=[pl.BlockSpec((seq, 4 * ct), lambda b, j: (b, j)), pl.BlockSpec((3, ct), lambda b, j: (0, j))],
        out_specs=pl.BlockSpec((seq, ct), lambda b, j: (b, j)),
        out_shape=jax.ShapeDtypeStruct((t, d), BF16), compiler_params=_params("parallel", "parallel"))(pr, conv_w)


def _tail(a_in, b_in, pr, x2, target2, mod3, w_pa, w_pc, w_out, b_out, ln_g, ln_b, seq, lay):
    t, d = x2.shape
    tm = 512
    per_seq = seq // tm
    n_steps = t // tm
    gate_blk = 4 * d // d

    def nt(a, b):
        return lax.dot_general(a, b, (((1,), (1,)), ((), ())), preferred_element_type=F32)

    def tn(a, b):
        return lax.dot_general(a, b, (((0,), (0,)), ((), ())), preferred_element_type=F32)

    def body(a_ref, b_ref, ga_ref, gb_ref, x_ref, tg_ref, mod_ref, wpa_ref, wpc_ref, wo_ref, bo_ref, lg_ref, lb_ref,
             dpg_ref, da_ref, db_ref, gx_ref, dgate_ref, small_ref, gwpa_hbm, gwpc_hbm, gwo_hbm,
             acc_pa, acc_pc, acc_o, sem):
        i = pl.program_id(0)

        @pl.when(i == 0)
        def _():
            acc_pa[...] = jnp.zeros_like(acc_pa)
            acc_pc[...] = jnp.zeros_like(acc_pc)
            acc_o[...] = jnp.zeros_like(acc_o)
            small_ref[...] = jnp.zeros_like(small_ref)

        @pl.when(i % per_seq == 0)
        def _():
            dgate_ref[...] = jnp.zeros_like(dgate_ref)

        a_bf, b_bf = a_ref[...], b_ref[...]
        y_attn = jnp.dot(a_bf, wpa_ref[...], preferred_element_type=F32)
        y_conv = jnp.dot(b_bf, wpc_ref[...], preferred_element_type=F32)
        sa, sb = _sigmoid(ga_ref[...].astype(F32)), _sigmoid(gb_ref[...].astype(F32))
        merged = (sa * y_attn + sb * y_conv).astype(BF16)
        mo = jnp.dot(merged, wo_ref[...], preferred_element_type=F32) + bo_ref[...]
        gate = mod_ref[:, 2 * d:3 * d]
        r = ALPHA * x_ref[...] + gate * mo
        mu = jnp.mean(r, axis=-1, keepdims=True)
        cen = r - mu
        var = jnp.mean(cen * cen, axis=-1, keepdims=True)
        rstd = lax.rsqrt(var + LN_EPS)
        xhat = cen * rstd
        err = xhat * lg_ref[...] + lb_ref[...] - tg_ref[...]
        dy = err * (1.0 / d)
        dxhat = dy * lg_ref[...]
        dr = rstd * (dxhat - jnp.mean(dxhat, axis=-1, keepdims=True) - xhat * jnp.mean(dxhat * xhat, axis=-1, keepdims=True))
        gx_ref[...] = ALPHA * dr
        dgate_ref[...] += jnp.sum(dr * mo, axis=0, keepdims=True)
        d_mo = dr * gate
        small_ref[0:1, :] += jnp.sum(d_mo, axis=0, keepdims=True)
        small_ref[1:2, :] += jnp.sum(dy * xhat, axis=0, keepdims=True)
        small_ref[2:3, :] += jnp.sum(dy, axis=0, keepdims=True)
        small_ref[3:4, :] += jnp.sum(err * err, axis=0, keepdims=True)
        d_mo_bf = d_mo.astype(BF16)
        acc_o[...] += tn(merged, d_mo_bf)
        dmerged = nt(d_mo_bf, wo_ref[...])
        dy_attn = (dmerged * sa).astype(BF16)
        dy_conv = (dmerged * sb).astype(BF16)
        dpg_ref[:, 0:d] = (dmerged * y_attn * sa * (1.0 - sa)).astype(BF16)
        dpg_ref[:, d:2 * d] = (dmerged * y_conv * sb * (1.0 - sb)).astype(BF16)
        acc_pa[...] += tn(a_bf, dy_attn)
        acc_pc[...] += tn(b_bf, dy_conv)
        da_ref[...] = nt(dy_attn, wpa_ref[...])
        db_ref[...] = nt(dy_conv, wpc_ref[...])

        @pl.when(i == n_steps - 1)
        def _():
            copies = [pltpu.make_async_copy(acc_pa, gwpa_hbm, sem.at[0]), pltpu.make_async_copy(acc_pc, gwpc_hbm, sem.at[1]),
                      pltpu.make_async_copy(acc_o, gwo_hbm, sem.at[2])]
            for cp in copies:
                cp.start()
            for cp in copies:
                cp.wait()

    row = lambda w: pl.BlockSpec((tm, w), lambda i: (i, 0))
    const = lambda shp: pl.BlockSpec(shp, lambda i: (0,) * len(shp), pipeline_mode=pl.Buffered(1))
    any_spec = pl.BlockSpec(memory_space=pl.ANY)
    return pl.pallas_call(
        body, name="tail", grid=(n_steps,),
        in_specs=[row(Z_WIDTH), row(d),
                  pl.BlockSpec((tm, d), lambda i: (i, gate_blk)), pl.BlockSpec((tm, d), lambda i: (i, gate_blk + 1)),
                  row(d), row(d), pl.BlockSpec((None, 1, 3 * d), lambda i: (i // per_seq, 0, 0)),
                  const((Z_WIDTH, d)), const((d, d)), const((d, d)), const((1, d)), const((1, d)), const((1, d))],
        out_specs=[pl.BlockSpec((tm, 2 * d), lambda i: (i, lay.g0 // (2 * d))), row(Z_WIDTH), row(d), row(d),
                   pl.BlockSpec((None, 1, d), lambda i: (i // per_seq, 0, 0)), pl.BlockSpec((8, d), lambda i: (0, 0)),
                   any_spec, any_spec, any_spec],
        out_shape=[jax.ShapeDtypeStruct((t, lay.np), BF16), jax.ShapeDtypeStruct((t, Z_WIDTH), F32),
                   jax.ShapeDtypeStruct((t, d), F32), jax.ShapeDtypeStruct((t, d), F32),
                   jax.ShapeDtypeStruct((t // seq, 1, d), F32), jax.ShapeDtypeStruct((8, d), F32),
                   jax.ShapeDtypeStruct((Z_WIDTH, d), F32), jax.ShapeDtypeStruct((d, d), F32),
                   jax.ShapeDtypeStruct((d, d), F32)],
        scratch_shapes=[pltpu.VMEM((Z_WIDTH, d), F32), pltpu.VMEM((d, d), F32), pltpu.VMEM((d, d), F32),
                        pltpu.SemaphoreType.DMA((3,))],
        compiler_params=_params("arbitrary"),
    )(a_in, b_in, pr, pr, x2, target2, mod3, w_pa, w_pc, w_out, b_out, ln_g, ln_b)


def _conv_bwd(dproj, db, pr, conv_w, bsz, seq, lay):
    d = lay.d
    ct = CONV_TILE
    base = lay.c0 // (4 * ct)

    def body(dp_in, db_ref, p_ref, cw_ref, dp_ref, gcw_ref):
        del dp_in
        u_x, g_b, g_c, z = [p_ref[:, k * ct:(k + 1) * ct].astype(F32) for k in range(4)]
        cw = cw_ref[...]
        u = g_c * u_x
        u1, u2 = _shift_down(u, 1), _shift_down(u, 2)
        conv = cw[0:1, :] * u2 + cw[1:2, :] * u1 + cw[2:3, :] * u
        sig = _sigmoid(z)
        sl = z * sig
        dbv = db_ref[...]
        gbc = g_b * conv
        dp_ref[:, ct:2 * ct] = (dbv * sl * conv).astype(BF16)
        dp_ref[:, 3 * ct:4 * ct] = (dbv * gbc * (sig * (1.0 + z * (1.0 - sig)))).astype(BF16)
        dconv = dbv * sl * g_b

        @pl.when(pl.program_id(1) == 0)
        def _():
            gcw_ref[...] = jnp.zeros_like(gcw_ref)

        gcw_ref[0:1, :] += jnp.sum(dconv * u2, axis=0, keepdims=True)
        gcw_ref[1:2, :] += jnp.sum(dconv * u1, axis=0, keepdims=True)
        gcw_ref[2:3, :] += jnp.sum(dconv * u, axis=0, keepdims=True)
        du = cw[2:3, :] * dconv + cw[1:2, :] * _shift_up(dconv, 1) + cw[0:1, :] * _shift_up(dconv, 2)
        dp_ref[:, 0:ct] = (du * g_c).astype(BF16)
        dp_ref[:, 2 * ct:3 * ct] = (du * u_x).astype(BF16)

    return pl.pallas_call(
        body, name="conv_bwd", grid=(d // ct, bsz),
        in_specs=[pl.BlockSpec(memory_space=pl.ANY), pl.BlockSpec((seq, ct), lambda j, b: (b, j)),
                  pl.BlockSpec((seq, 4 * ct), lambda j, b: (b, j)), pl.BlockSpec((3, ct), lambda j, b: (0, j))],
        out_specs=[pl.BlockSpec((seq, 4 * ct), lambda j, b: (b, base + j)), pl.BlockSpec((8, ct), lambda j, b: (0, j))],
        out_shape=[jax.ShapeDtypeStruct(dproj.shape, BF16), jax.ShapeDtypeStruct((8, d), F32)],
        input_output_aliases={0: 0}, compiler_params=_params("arbitrary", "arbitrary"))(dproj, db, pr, conv_w)


def _attn_bwd(dproj, pa, o_all, lse_all, da, bsz, seq):
    n_blocks = seq // SUB
    chunk = 256

    def body(dp_in, pa_ref, o_ref, lse_ref, da_ref, dp_ref, sub, stage, dsub, dog, cvec, bias_ref,
             s_buf, dp_buf, ds_buf, pb_buf, q2_buf, do2_buf, l_buf, c_buf):
        del dp_in
        p = pl.program_id(1)
        _fill_bias(bias_ref, p, seq)
        head0 = lax.broadcasted_iota(jnp.int32, (SUB, SLAB), 1) < HEAD_DIM

        def mix_bwd(i, carry):
            rows = pl.ds(pl.multiple_of(i * chunk, chunk), chunk)
            ls = [lse_ref[g, rows, :] for g in range(N_GROUPS)]
            os_ = [o_ref[g, rows, :] for g in range(N_GROUPS)]
            m = jnp.maximum(jnp.maximum(ls[0], ls[1]), ls[2])
            es = [jnp.exp(l - m) for l in ls]
            tot = es[0] + es[1] + es[2]
            ws = [e / tot for e in es]
            o = ws[0] * os_[0] + ws[1] * os_[1] + ws[2] * os_[2]
            z = pa_ref[rows, 9 * SLAB:10 * SLAB].astype(F32)
            sig = _sigmoid(z)
            dav = da_ref[rows, :]
            do = dav * (z * sig)
            dp_ref[rows, 9 * SLAB:10 * SLAB] = (dav * o * (sig * (1.0 + z * (1.0 - sig)))).astype(BF16)
            wsum = _head_sums(do * o)
            for g in range(N_GROUPS):
                dog[g, rows, :] = ws[g] * do
                cvec[g, rows, :] = -(ws[g] * wsum)
            return carry

        lax.fori_loop(0, seq // chunk, mix_bwd, 0)

        for g in range(N_GROUPS):
            dil = DILATIONS[g]
            for w in range(3):
                _to_sub_major(pa_ref, 3 * w + g, sub.at[w], stage, dil, seq)
            dsub[1] = jnp.zeros((seq, SLAB), F32)
            dsub[2] = jnp.zeros((seq, SLAB), F32)
            nk = _key_rows(g, seq)

            def trip(i, carry, g=g, dil=dil, nk=nk):
                places = [_block_rows(BLOCKS_PER_TRIP * i + j, g, seq) for j in range(BLOCKS_PER_TRIP)]
                for j, (row0, krow0, _, nat) in enumerate(places):
                    q = sub[0, pl.ds(row0, SUB), :]
                    do = _ld_rows(dog.at[g], nat, SUB, dil).astype(BF16)
                    zero = jnp.zeros_like(q)
                    q2 = jnp.concatenate([jnp.where(head0, q, zero), jnp.where(head0, zero, q)], axis=0)
                    do2 = jnp.concatenate([jnp.where(head0, do, zero), jnp.where(head0, zero, do)], axis=0)
                    q2_buf[j] = q2
                    do2_buf[j] = do2
                    s_buf[j, :, 0:nk] = _nt(q2 * (HEAD_DIM ** -0.5), sub[1, pl.ds(krow0, nk), :])
                    dp_buf[j, :, 0:nk] = _nt(do2, sub[2, pl.ds(krow0, nk), :])
                    l_buf[j] = _ld_rows(lse_ref.at[g], nat, SUB, dil)
                    c_buf[j] = _ld_rows(cvec.at[g], nat, SUB, dil)
                for j, (_, _, bi, _) in enumerate(places):
                    for c in range(0, 2 * SUB, SOFTMAX_ROWS):
                        hh, r = divmod(c, SUB)
                        lane = hh * HEAD_DIM
                        s = s_buf[j, c:c + SOFTMAX_ROWS, 0:nk] + bias_ref[g, bi, hh, r:r + SOFTMAX_ROWS, 0:nk]
                        prob = jnp.exp(s - l_buf[j, r:r + SOFTMAX_ROWS, lane:lane + 1])
                        dprob = dp_buf[j, c:c + SOFTMAX_ROWS, 0:nk] + c_buf[j, r:r + SOFTMAX_ROWS, lane:lane + 1]
                        ds_buf[j, c:c + SOFTMAX_ROWS, 0:nk] = (prob * dprob * (HEAD_DIM ** -0.5)).astype(BF16)
                        pb_buf[j, c:c + SOFTMAX_ROWS, 0:nk] = prob.astype(BF16)
                for j, (row0, krow0, _, _) in enumerate(places):
                    ds = ds_buf[j, :, 0:nk]
                    dq2 = jnp.dot(ds, sub[1, pl.ds(krow0, nk), :], preferred_element_type=F32)
                    dsub[0, pl.ds(row0, SUB), :] = jnp.where(head0, dq2[0:SUB], dq2[SUB:2 * SUB])
                    dsub[1, pl.ds(krow0, nk), :] += _tn(ds, q2_buf[j])
                    dsub[2, pl.ds(krow0, nk), :] += _tn(pb_buf[j, :, 0:nk], do2_buf[j])
                return carry

            lax.fori_loop(0, n_blocks // BLOCKS_PER_TRIP, trip, 0)
            for w in range(3):
                cols = slice((3 * w + g) * SLAB, (3 * w + g + 1) * SLAB)
                if dil == 1:
                    dp_ref[:, cols] = dsub[w].astype(BF16)
                else:
                    n = seq // dil
                    for r in range(dil):
                        stage[pl.ds(r, n, stride=dil), :] = dsub[w, pl.ds(r * n, n), :]
                    dp_ref[:, cols] = stage[...].astype(BF16)

    return pl.pallas_call(
        body, name="attn_bwd", grid=(bsz, 2),
        in_specs=[pl.BlockSpec(memory_space=pl.ANY), pl.BlockSpec((seq, PAIR_COLS), lambda b, p: (b, p)),
                  pl.BlockSpec((N_GROUPS, seq, SLAB), lambda b, p: (0, b, p)),
                  pl.BlockSpec((N_GROUPS, seq, SLAB), lambda b, p: (0, b, p)),
                  pl.BlockSpec((seq, SLAB), lambda b, p: (b, p))],
        out_specs=pl.BlockSpec((seq, PAIR_COLS), lambda b, p: (b, p)),
        out_shape=jax.ShapeDtypeStruct(dproj.shape, BF16), input_output_aliases={0: 0},
        scratch_shapes=[pltpu.VMEM((3, seq, SLAB), BF16), pltpu.VMEM((seq, SLAB), F32), pltpu.VMEM((3, seq, SLAB), F32),
                        pltpu.VMEM((3, seq, SLAB), F32), pltpu.VMEM((3, seq, SLAB), F32),
                        pltpu.VMEM((N_GROUPS, 2, 2, SUB, 2 * SUB), F32),
                        pltpu.VMEM((BLOCKS_PER_TRIP, 2 * SUB, 2 * SUB), F32), pltpu.VMEM((BLOCKS_PER_TRIP, 2 * SUB, 2 * SUB), F32),
                        pltpu.VMEM((BLOCKS_PER_TRIP, 2 * SUB, 2 * SUB), BF16), pltpu.VMEM((BLOCKS_PER_TRIP, 2 * SUB, 2 * SUB), BF16),
                        pltpu.VMEM((BLOCKS_PER_TRIP, 2 * SUB, SLAB), BF16), pltpu.VMEM((BLOCKS_PER_TRIP, 2 * SUB, SLAB), BF16),
                        pltpu.VMEM((BLOCKS_PER_TRIP, SUB, SLAB), F32), pltpu.VMEM((BLOCKS_PER_TRIP, SUB, SLAB), F32)],
        compiler_params=_params("arbitrary", "arbitrary"))(dproj, pa, o_all, lse_all, da)


def _grad_h(dproj, w_all, gx0, x2, mod3, seq, lay):
    t, d = x2.shape
    tm, tn = min(512, seq), min(1024, d)
    per_seq = seq // tm
    w_mode = dict(pipeline_mode=pl.Buffered(1)) if d == tn else {}

    def body(dp_ref, w_ref, gx0_ref, x_ref, scale_ref, gx_ref, dmod_ref):
        dh = _nt(dp_ref[:, 0:ATT], w_ref[:, 0:ATT]) + _nt(dp_ref[:, lay.c0:], w_ref[:, lay.c0:])
        gx_ref[...] = gx0_ref[...] + dh * (1.0 + scale_ref[...])

        @pl.when(pl.program_id(1) % per_seq == 0)
        def _():
            dmod_ref[...] = jnp.zeros_like(dmod_ref)

        dmod_ref[0:1, :] += jnp.sum(dh, axis=0, keepdims=True)
        dmod_ref[1:2, :] += jnp.sum(dh * x_ref[...], axis=0, keepdims=True)

    tile = pl.BlockSpec((tm, tn), lambda j, i: (i, j))
    return pl.pallas_call(
        body, name="grad_h", grid=(d // tn, t // tm),
        in_specs=[pl.BlockSpec((tm, lay.np), lambda j, i: (i, 0)), pl.BlockSpec((tn, lay.np), lambda j, i: (j, 0), **w_mode),
                  tile, tile,
                  pl.BlockSpec((None, 1, tn), lambda j, i: (i // per_seq, 0, d // tn + j))],
        out_specs=[tile, pl.BlockSpec((None, 8, tn), lambda j, i: (i // per_seq, 0, j))],
        out_shape=[jax.ShapeDtypeStruct((t, d), F32), jax.ShapeDtypeStruct((t // seq, 8, d), F32)],
        compiler_params=_params("arbitrary", "arbitrary"))(dproj, w_all, gx0, x2, mod3)


def _grad_w_in(ht, dproj, seq, lay):
    d, t = ht.shape
    tm = seq
    n_i = t // tm

    def make_body(n_skip, n_pieces, tn, nat_tile):
        def body(*refs):
            refs = refs[n_skip:]
            ht_ref, dp_refs = refs[0], refs[1:1 + n_pieces]
            gw_hbm, gb_hbm, acc, bacc, gw_out, gb_out, sem = refs[1 + n_pieces:]
            i, j = pl.program_id(0), pl.program_id(1)
            dp = dp_refs[0][...] if n_pieces == 1 else jnp.concatenate([r[...] for r in dp_refs], axis=1)
            part = jnp.dot(ht_ref[...], dp, preferred_element_type=F32)
            bpart = jnp.sum(dp.astype(F32), axis=0, keepdims=True)

            if n_i > 1:
                @pl.when(i == 0)
                def _():
                    acc[j] = part
                    bacc[j] = bpart

                @pl.when((i > 0) & (i < n_i - 1))
                def _():
                    acc[j] += part
                    bacc[j] += bpart

            @pl.when(i == n_i - 1)
            def _():
                gw_out[...] = ((part + acc[j]) if n_i > 1 else part).astype(BF16)
                gb_out[...] = (bpart + bacc[j]) if n_i > 1 else bpart
                cols = pl.ds(pl.multiple_of(nat_tile(j) * tn, SLAB), tn)
                copies = [pltpu.make_async_copy(gw_out, gw_hbm.at[:, cols], sem.at[0]),
                          pltpu.make_async_copy(gb_out, gb_hbm.at[:, cols], sem.at[1])]
                for cp in copies:
                    cp.start()
                for cp in copies:
                    cp.wait()
        return body

    def call(name, pieces, n_tiles, nat_tile, prev):
        tn = sum(w for w, _ in pieces)
        any_spec = pl.BlockSpec(memory_space=pl.ANY)
        in_specs = [pl.BlockSpec((d, tm), lambda i, j: (0, i))]
        in_specs += [pl.BlockSpec((tm, w), lambda i, j, f=f: (i, f(j))) for w, f in pieces]
        args = [ht] + [dproj] * len(pieces)
        aliases = {}
        if prev is not None:
            in_specs = [any_spec] * 2 + in_specs
            args = list(prev) + args
            aliases = {0: 0, 1: 1}
        return pl.pallas_call(
            make_body(0 if prev is None else 2, len(pieces), tn, nat_tile), name=name, grid=(n_i, n_tiles), in_specs=in_specs,
            out_specs=[any_spec, any_spec],
            out_shape=[jax.ShapeDtypeStruct((d, lay.din), BF16), jax.ShapeDtypeStruct((1, lay.din), F32)],
            input_output_aliases=aliases,
            scratch_shapes=[pltpu.VMEM((n_tiles, d, tn), F32), pltpu.VMEM((n_tiles, 1, tn), F32), pltpu.VMEM((d, tn), BF16),
                            pltpu.VMEM((1, tn), F32), pltpu.SemaphoreType.DMA((2,))],
            compiler_params=_params("arbitrary", "arbitrary"))(*args)

    attn_pieces = [(SLAB, lambda j, m=m: (m % 2) * PAIR_SLABS + 2 * j + m // 2) for m in range(4)]
    first = call("grad_w_in_attn", attn_pieces, ATT // 512, lambda j: j, None)
    base = lay.c0 // CONV_TILE
    nct = lay.n_conv_tiles
    if nct % 2:
        return call("grad_w_in_rest", [(CONV_TILE, lambda j: base + j)], 6 * d // CONV_TILE, lay.rest_nat_tile, first)
    half = nct // 2

    def rest_piece(m):
        def perm_tile(j):
            conv = base + 4 * (2 * (j % half) + m) + j // half
            return jnp.where(j < 4 * half, conv, base + 2 * j + m)
        return (CONV_TILE, perm_tile)

    return call("grad_w_in_rest", [rest_piece(0), rest_piece(1)], 6 * d // 512, lambda j: ATT // 512 + j, first)


def _pack_rows(parts, width=128):
    flat = [p.reshape(-1) for p in parts]
    spans, rows = [], 0
    padded = []
    for f in flat:
        n = -(-f.shape[0] // (8 * width)) * 8
        padded.append(jnp.pad(f, (0, n * width - f.shape[0])).reshape(n, width))
        spans.append((rows, f.shape[0]))
        rows += n
    return jnp.concatenate(padded, axis=0), spans


def _unpack_rows(packed, spans, shapes, width=128):
    out = []
    for (row, n), shp in zip(spans, shapes):
        rows = -(-n // width)
        out.append(packed[row:row + rows].reshape(-1)[:n].reshape(shp))
    return out


def kernel(x, c, w_ada, b_ada, w_in, b_in, conv_w, w_proj_attn, w_proj_conv, w_out, b_out, ln_g, ln_b, loss_target, m_w_ada, m_b_ada, m_w_in, m_b_in, m_conv_w, m_w_proj_attn, m_w_proj_conv, m_w_out, m_b_out, m_ln_g, m_ln_b, v_w_ada, v_b_ada, v_w_in, v_b_in, v_conv_w, v_w_proj_attn, v_w_proj_conv, v_w_out, v_b_out, v_ln_g, v_ln_b):
    bsz, seq, d = x.shape
    t = bsz * seq
    lay = _Layout(d)
    col_sharded = [True, True, False, False]
    red_w = [w_in[0], w_proj_attn[0], w_proj_conv[0], w_out[0]]
    chip = 2 * lax.axis_index("x") + lax.axis_index("y")
    chip1 = chip.astype(jnp.int32).reshape(1)
    core1 = lax.axis_index("c").astype(jnp.int32).reshape(1)
    place = jnp.stack([chip, lax.axis_index("c")]).astype(jnp.int32)
    x2 = x.reshape(t, d)
    target2 = loss_target.reshape(t, d)

    mod, act_all = _ada_exchange(c, w_ada[0], b_ada)
    mod3 = mod.reshape(bsz, 1, 3 * d)

    cw_pad = jnp.pad(conv_w[0], ((0, 5), (0, 0))) + 0.0 * mod[0, 0]
    own_in_full = [_cast_into_full(red_w[w], col_sharded[w], chip1, f"cast_shard_{w}") for w in range(4)]
    (wi_f,), cw8 = _gather_weights(own_in_full[:1], col_sharded[:1], cw_pad)
    cw_full = cw8[0:3]
    late_copies = _direct_gather_copies(col_sharded[1:])
    late_send, late_recv, late_flying, late_token = _start_copies("gather_late_start", own_in_full[1:], (18,), cw8, late_copies)
    w_all = _permute_w_in(wi_f, lay)
    b_all = lay.perm_vector(b_in) + late_token

    rest_tn = 1024 if (6 * d) % 1024 == 0 else 512
    pa, = _project(x2, mod3, w_all, b_all, seq, 0, ATT, PAIR_COLS, BF16, False, "project_attn")
    pr, ht = _project(x2, mod3, w_all, b_all, seq, lay.c0, 6 * d, rest_tn, BF16, True, "project_rest")
    o_all, lse_all, a_in = _attn_fwd(pa, bsz, seq)
    b_in_act = _conv_fwd(pr, cw_full, bsz, seq, d)
    wpa_f, wpc_f, wo_f = _wait_copies("gather_late_wait", late_flying, late_send, late_recv, b_in_act, late_copies)
    (dproj, da_in, db_in, gx0, dgate, small_tail, gw_pa, gw_pc, gw_out) = _tail(
        a_in, b_in_act, pr, x2, target2, mod3, wpa_f, wpc_f, wo_f, b_out, ln_g, ln_b, seq, lay)

    def mixers_backward(token):
        dp, gcw_ = _conv_bwd(dproj, db_in, pr, cw_full + token, bsz, seq, lay)
        dp = _attn_bwd(dp, pa, o_all, lse_all, da_in, bsz, seq)
        gw_in_bf_, gb_in_ = _grad_w_in(ht, dp, seq, lay)
        return (dp, gcw_, gw_in_bf_, gb_in_), gw_in_bf_

    late_views = _shard_views([gw_pa, gw_pc, gw_out], col_sharded[1:])
    late_got = _pair_exchange_halves(late_views, "grads_pair_exchange_late")
    late_parts = [_pair_sum(late_views[w], late_got[w], core1, f"grads_pair_sum_late_{w}") for w in range(3)]
    late_red, (dproj, gcw, gw_in_bf, gb_in) = _reduce_over_chips(
        late_parts, col_sharded[1:], place, late_parts[0], mixers_backward, "late")

    def input_backward(token):
        gx, dm = _grad_h(dproj, w_all, gx0, x2, mod3 + token, seq, lay)
        return (gx, dm), gx

    in_view = _shard_views([gw_in_bf], col_sharded[:1])
    in_got = _pair_exchange_halves(in_view, "grads_pair_exchange_in")
    in_part = _pair_sum(in_view[0], in_got[0], core1, "grads_pair_sum_in")
    in_red, (grad_x2, dmod) = _reduce_over_chips([in_part], col_sharded[:1], place, late_red[0], input_backward, "in")
    g_red = _pair_join_halves(in_red + late_red)

    d_ada = jnp.concatenate([dmod[:, 0, :], dmod[:, 1, :], dgate[:, 0, :]], axis=1)
    pieces = [small_tail[3], jnp.sum(d_ada, axis=0), gb_in[0], small_tail[0], small_tail[1], small_tail[2], gcw[0:3]]
    packed, spans = _pack_rows(pieces)
    kept_in, _ = _pack_rows([d_ada])
    summed, kept = _all_sum_small(jnp.concatenate([packed, kept_in], axis=0), packed.shape[0], d)
    loss = summed[0, 0]
    _, g_b_ada, g_b_in, g_b_out, g_ln_g, g_ln_b, g_cw_full = _unpack_rows(
        summed, spans, [(d,), (3 * d,), (lay.din,), (d,), (d,), (d,), (3, d)])
    g_cw = lax.dynamic_slice(g_cw_full, (0, chip * (d // N_CHIPS)), (3, d // N_CHIPS))
    d_ada_all = kept.reshape(N_DEV, -1)[:, :bsz * 3 * d].reshape(N_DEV * bsz, 3 * d)
    ada_cols = 3 * d // N_CHIPS
    g_w_ada = _grad_w_ada_cols(act_all.T, lax.dynamic_slice(d_ada_all, (0, chip * ada_cols), (N_DEV * bsz, ada_cols)))

    big_w = [w_ada[0]] + red_w
    big_m = [m_w_ada[0], m_w_in[0], m_w_proj_attn[0], m_w_proj_conv[0], m_w_out[0]]
    big_v = [v_w_ada[0], v_w_in[0], v_w_proj_attn[0], v_w_proj_conv[0], v_w_out[0]]
    g_big = [g_w_ada] + g_red
    big_out = [_adamw(big_w[w], g_big[w], big_m[w], big_v[w], f"adamw_{w}") for w in range(5)]
    small_w = [b_ada, b_in, conv_w[0], b_out, ln_g, ln_b]
    small_g = [g_b_ada, g_b_in, g_cw, g_b_out, g_ln_g, g_ln_b]
    small_m = [m_b_ada, m_b_in, m_conv_w[0], m_b_out, m_ln_g, m_ln_b]
    small_v = [v_b_ada, v_b_in, v_conv_w[0], v_b_out, v_ln_g, v_ln_b]
    pw, sp = _pack_rows(small_w)
    pg, _ = _pack_rows(small_g)
    pm, _ = _pack_rows(small_m)
    pv, _ = _pack_rows(small_v)
    sd, sm, sv = _adamw(pw, pg, pm, pv, "adamw_small")
    shapes = [a.shape for a in small_w]
    sd, sm, sv = _unpack_rows(sd, sp, shapes), _unpack_rows(sm, sp, shapes), _unpack_rows(sv, sp, shapes)

    def order(wa, bA, wi, bI, cw, wpa, wpc, wo, bO, lg, lb):
        return (wa[None], bA, wi[None], bI, cw[None], wpa[None], wpc[None], wo[None], bO, lg, lb)

    sg = [g.reshape(s) for g, s in zip(small_g, shapes)]
    grads_out = order(g_big[0], sg[0], g_big[1], sg[1], sg[2], g_big[2], g_big[3], g_big[4], sg[3], sg[4], sg[5])
    outs = []
    for idx, small in enumerate((sd, sm, sv)):
        outs.append(order(big_out[0][idx], small[0], big_out[1][idx], small[1], small[2], big_out[2][idx],
                          big_out[3][idx], big_out[4][idx], small[3], small[4], small[5]))
    return (loss, grad_x2.reshape(bsz, seq, d), *grads_out, *outs[0], *outs[1], *outs[2])
```

```python
import functools

import jax
import jax.numpy as jnp
from jax import lax
from jax.experimental import pallas as pl
from jax.experimental.pallas import tpu as pltpu

F32 = jnp.float32
BF16 = jnp.bfloat16
MESH = pl.DeviceIdType.MESH

HEAD_DIM = 64
N_GROUPS = 3
DILATIONS = (1, 4, 16)
N_HEADS = 12
SUB = 128
Q_WIDTH = 768
Z_WIDTH = 256
ATT = 3 * Q_WIDTH + Z_WIDTH
SLAB = 128
PAIR_SLABS = 10
PAIR_COLS = PAIR_SLABS * SLAB
CONV_TILE = 256
SOFTMAX_ROWS = 32
BLOCKS_PER_TRIP = 8
ALIBI_MAX_EXP = 8.0
ALPHA = 2.0 ** 0.25
LN_EPS = 1e-5
ADAM_LR, ADAM_B1, ADAM_B2, ADAM_EPS, ADAM_WD, ADAM_STEP = 0.001, 0.9, 0.999, 1e-08, 0.01, 10
N_CHIPS = 4
N_DEV = 8
VMEM_LIMIT_V7X = 60 * 1024 * 1024
NEG = -1e30


def _params(*sem):
    return pltpu.CompilerParams(dimension_semantics=sem, vmem_limit_bytes=VMEM_LIMIT_V7X)


def _sigmoid(v):
    return 1.0 / (1.0 + jnp.exp(-v))


class _Layout:
    def __init__(self, d):
        self.d = d
        self.din = ATT + 6 * d
        c0 = 3072
        while c0 % (2 * d):
            c0 += 1024
        self.c0, self.g0, self.np = c0, c0 + 4 * d, c0 + 6 * d
        self.n_conv_tiles = d // CONV_TILE

    def attn_nat_slab(self, s):
        p, i = s // PAIR_SLABS, s % PAIR_SLABS
        return jnp.where(i < 9, (i // 3) * 6 + (i % 3) * 2 + p, 18 + p)

    def rest_nat_tile(self, t):
        n4 = 4 * self.n_conv_tiles
        conv = ATT // CONV_TILE + (t % 4) * self.n_conv_tiles + t // 4
        return jnp.where(t < n4, conv, ATT // CONV_TILE + t)

    def perm_vector(self, v):
        parts = []
        for s in range(2 * PAIR_SLABS):
            p, i = divmod(s, PAIR_SLABS)
            ns = (i // 3) * 6 + (i % 3) * 2 + p if i < 9 else 18 + p
            parts.append(v[:, ns * SLAB:(ns + 1) * SLAB])
        parts.append(jnp.zeros((1, self.c0 - ATT), v.dtype))
        for j in range(self.n_conv_tiles):
            for k in range(4):
                a = ATT + k * self.d + j * CONV_TILE
                parts.append(v[:, a:a + CONV_TILE])
        parts.append(v[:, ATT + 4 * self.d:])
        return jnp.concatenate(parts, axis=1)


def _place():
    return lax.axis_index("x"), lax.axis_index("y"), lax.axis_index("c")


def _other_chips(x, y):
    return [(1 - x, y), (x, 1 - y), (1 - x, 1 - y)]


def _shard_of(ref, col_sharded, chip, half=None):
    if col_sharded:
        cs = ref.shape[1] // N_CHIPS
        cols = pl.ds(pl.multiple_of(chip * cs, SLAB), cs)
        if half is None:
            return ref.at[:, cols]
        n = ref.shape[0] // 2
        return ref.at[pl.ds(half * n, n), cols]
    rs = ref.shape[0] // N_CHIPS
    if half is None:
        return ref.at[pl.ds(chip * rs, rs)]
    return ref.at[pl.ds(chip * rs + half * (rs // 2), rs // 2)]


GATHER_PIECES = 4


def _cast_into_full(shard, col_sharded, chip, name):
    rows, cols = shard.shape
    tr = _row_tile(rows, cols)
    nb = rows // tr

    def body(chip_ref, s_ref, o_ref):
        del chip_ref
        o_ref[...] = s_ref[...].astype(BF16)

    if col_sharded:
        full, out_spec = (rows, cols * N_CHIPS), pl.BlockSpec((tr, cols), lambda i, ch: (i, ch[0]))
    else:
        full, out_spec = (rows * N_CHIPS, cols), pl.BlockSpec((tr, cols), lambda i, ch: (ch[0] * nb + i, 0))
    return pl.pallas_call(
        body, name=name,
        grid_spec=pltpu.PrefetchScalarGridSpec(num_scalar_prefetch=1, grid=(nb,),
                                               in_specs=[pl.BlockSpec((tr, cols), lambda i, ch: (i, 0))], out_specs=out_spec),
        out_shape=jax.ShapeDtypeStruct(full, BF16), compiler_params=_params("parallel"))(chip, shard)


def _gather_weights(fulls, col_sharded, small):
    n = len(fulls)
    kp = GATHER_PIECES

    def piece(ref, cs, chip, half, k):
        if cs:
            width = ref.shape[1] // N_CHIPS
            rows = ref.shape[0] // 2 // kp
            return ref.at[pl.ds(half * (ref.shape[0] // 2) + k * rows, rows), pl.ds(pl.multiple_of(chip * width, SLAB), width)]
        rs = ref.shape[0] // N_CHIPS
        rows = rs // 2 // kp
        return ref.at[pl.ds(chip * rs + half * (rs // 2) + k * rows, rows)]

    def body(*refs):
        sm_in, outs, sm_out = refs[n], refs[n + 1:2 * n + 1], refs[2 * n + 1]
        send, recv, fsend, frecv, lsem, ssend, srecv = refs[2 * n + 2:]
        x, y, c = _place()
        mine = 2 * x + y
        sibling = (x, y, 1 - c)
        first = (x ^ (1 - c), y ^ c)
        second = (x ^ c, y ^ (1 - c))
        diagonal = (1 - x, 1 - y)
        sources = [first, second, diagonal]
        senders = [first, second, second]

        def copy(ref, sems, slot, to):
            return pltpu.make_async_remote_copy(src_ref=ref, dst_ref=ref, send_sem=sems[0].at[slot], recv_sem=sems[1].at[slot],
                                                device_id=to, device_id_type=MESH)

        local = pltpu.make_async_copy(sm_in, _shard_of(sm_out, True, mine), lsem)
        local.start()
        sends = []
        for k, (cx, cy) in enumerate(_other_chips(x, y)):
            cp = pltpu.make_async_remote_copy(src_ref=sm_in, dst_ref=_shard_of(sm_out, True, mine), send_sem=ssend.at[k],
                                              recv_sem=srecv.at[k], device_id=(cx, cy, c), device_id_type=MESH)
            cp.start()
            sends.append(cp)
        for k in range(kp):
            for w in range(n):
                own = piece(outs[w], col_sharded[w], mine, c, k)
                for slot, chip in enumerate((first, second)):
                    cp = copy(own, (send, recv), (w * 3 + slot) * kp + k, (*chip, c))
                    cp.start()
                    sends.append(cp)
        for slot in range(3):
            source = 2 * sources[slot][0] + sources[slot][1]
            for k in range(kp):
                for w in range(n):
                    landed = piece(outs[w], col_sharded[w], source, c, k)
                    copy(landed, (send, recv), (w * 3 + slot) * kp + k, (*senders[slot], c)).wait_recv()
                    if slot == 0:
                        cp = copy(landed, (send, recv), (w * 3 + 2) * kp + k, (*second, c))
                        cp.start()
                        sends.append(cp)
                    cp = copy(landed, (fsend, frecv), (w * 3 + slot) * kp + k, sibling)
                    cp.start()
                    sends.append(cp)
        for slot, chip in enumerate((second, first, diagonal)):
            for k in range(kp):
                for w in range(n):
                    passed = piece(outs[w], col_sharded[w], 2 * chip[0] + chip[1], 1 - c, k)
                    copy(passed, (fsend, frecv), (w * 3 + slot) * kp + k, sibling).wait_recv()
        for k, (cx, cy) in enumerate(_other_chips(x, y)):
            theirs = _shard_of(sm_out, True, 2 * cx + cy)
            pltpu.make_async_remote_copy(src_ref=theirs, dst_ref=theirs, send_sem=ssend.at[k], recv_sem=srecv.at[k],
                                         device_id=(cx, cy, c), device_id_type=MESH).wait_recv()
        for cp in sends:
            cp.wait_send()
        local.wait()

    any_spec = pl.BlockSpec(memory_space=pl.ANY)
    outs = pl.pallas_call(
        body, name="gather_weights",
        out_shape=[jax.ShapeDtypeStruct(f.shape, BF16) for f in fulls]
        + [jax.ShapeDtypeStruct((small.shape[0], small.shape[1] * N_CHIPS), small.dtype)],
        in_specs=[any_spec] * (n + 1), out_specs=[any_spec] * (n + 1), input_output_aliases={w: w for w in range(n)},
        scratch_shapes=[pltpu.SemaphoreType.DMA((n * 3 * kp,)), pltpu.SemaphoreType.DMA((n * 3 * kp,)),
                        pltpu.SemaphoreType.DMA((n * 3 * kp,)), pltpu.SemaphoreType.DMA((n * 3 * kp,)), pltpu.SemaphoreType.DMA,
                        pltpu.SemaphoreType.DMA((3,)), pltpu.SemaphoreType.DMA((3,))],
    )(*fulls, small)
    return outs[:n], outs[n]


HBM_SPEC = pl.BlockSpec(memory_space=pltpu.HBM)
SEM_SPEC = pl.BlockSpec(memory_space=pltpu.SEMAPHORE)
DATAFLOW = pltpu.SideEffectType.DATAFLOW_SIDE_EFFECTING


def _start_copies(name, arrays, sem_shape, after, copies):
    n = len(arrays)

    def body(*refs):
        for cp in copies(refs[:n], refs[n + 1], refs[n + 2]):
            cp.start()
        token = refs[2 * n + 3]
        token[...] = jnp.zeros_like(token)

    res = pl.pallas_call(
        body, name=name,
        out_shape=(pltpu.SemaphoreType.DMA(sem_shape), pltpu.SemaphoreType.DMA(sem_shape),
                   *[pltpu.HBM(a.shape, a.dtype) for a in arrays], jax.ShapeDtypeStruct((8, 128), F32)),
        in_specs=[HBM_SPEC] * n + [pl.BlockSpec(memory_space=pl.ANY)],
        out_specs=(SEM_SPEC, SEM_SPEC, *([HBM_SPEC] * n), pl.BlockSpec(memory_space=pltpu.VMEM)),
        input_output_aliases={i: 2 + i for i in range(n)},
        compiler_params=pltpu.CompilerParams(has_side_effects=DATAFLOW),
    )(*[pltpu.with_memory_space_constraint(a, pltpu.HBM) for a in arrays], after)
    return res[0], res[1], list(res[2:2 + n]), res[2 + n]


def _wait_copies(name, arrays, send, recv, after, copies):
    n = len(arrays)

    def body(*refs):
        for cp in copies(refs[:n], refs[n], refs[n + 1]):
            cp.wait_send()
            cp.wait_recv()

    return pl.pallas_call(
        body, name=name, out_shape=[pltpu.HBM(a.shape, a.dtype) for a in arrays],
        in_specs=[HBM_SPEC] * n + [SEM_SPEC, SEM_SPEC, pl.BlockSpec(memory_space=pl.ANY)], out_specs=[HBM_SPEC] * n,
        input_output_aliases={i: i for i in range(n)},
        compiler_params=pltpu.CompilerParams(has_side_effects=DATAFLOW),
    )(*arrays, send, recv, after)


def _direct_gather_copies(col_sharded):
    def copies(refs, send, recv):
        x, y, c = _place()
        mine = 2 * x + y
        out = []
        for w, ref in enumerate(refs):
            own_half = _shard_of(ref, col_sharded[w], mine, c)
            k = 0
            for cx, cy in _other_chips(x, y):
                for pc in (c, 1 - c):
                    out.append(pltpu.make_async_remote_copy(
                        src_ref=own_half, dst_ref=own_half, send_sem=send.at[6 * w + k], recv_sem=recv.at[6 * w + k],
                        device_id=(cx, cy, pc), device_id_type=MESH))
                    k += 1
        return out
    return copies


def _chip_scatter_copies(n, col_sharded):
    def piece(ref, cs, chip):
        if cs:
            w = ref.shape[2] // N_CHIPS
            return ref.at[:, :, pl.ds(pl.multiple_of(chip * w, SLAB), w)]
        return ref.at[pl.ds(chip, 1)]

    def copies(refs, send, recv):
        x, y, c = _place()
        out = []
        for k, (cx, cy) in enumerate(_other_chips(x, y)):
            for w in range(n):
                out.append(pltpu.make_async_remote_copy(
                    src_ref=piece(refs[w], col_sharded[w], 2 * cx + cy), dst_ref=refs[n + w].at[pl.ds(k, 1)],
                    send_sem=send.at[3 * w + k], recv_sem=recv.at[3 * w + k], device_id=(cx, cy, c), device_id_type=MESH))
        return out
    return copies


def _shard_views(gs, col_sharded):
    return [g.reshape(1, *g.shape) if cs else g.reshape(N_CHIPS, g.shape[0] // N_CHIPS, g.shape[1])
            for g, cs in zip(gs, col_sharded)]


DMA_CHUNK_BYTES = 1 << 20


def _chunk_rows(shape, itemsize):
    s, rows, cols = shape
    n = 1
    while s * (rows // n) * cols * itemsize > DMA_CHUNK_BYTES and (rows // n) % 32 == 0:
        n *= 2
    return rows // n


def _row_pieces(src, src_row0, dst, dst_row0, rows, send_sem, recv_sem, device):
    step = _chunk_rows((src.shape[0], rows, src.shape[2]), src.dtype.itemsize)
    return [pltpu.make_async_remote_copy(src_ref=src.at[:, pl.ds(src_row0 + r, step)], dst_ref=dst.at[:, pl.ds(dst_row0 + r, step)],
                                         send_sem=send_sem, recv_sem=recv_sem, device_id=device, device_id_type=MESH)
            for r in range(0, rows, step)]


def _pair_exchange_copies(n, whole):
    def copies(refs, send, recv):
        x, y, c = _place()
        sibling = (x, y, 1 - c)
        out = []
        for w in range(n):
            hr = refs[n + w].shape[1]
            if whole:
                out.append(pltpu.make_async_remote_copy(
                    src_ref=refs[w].at[:, pl.ds((1 - c) * hr, hr)], dst_ref=refs[n + w], send_sem=send.at[w], recv_sem=recv.at[w],
                    device_id=sibling, device_id_type=MESH))
            else:
                out += _row_pieces(refs[w], (1 - c) * hr, refs[n + w], 0, hr, send.at[w], recv.at[w], sibling)
        return out
    return copies


def _pair_join_copies(n, whole):
    def copies(refs, send, recv):
        x, y, c = _place()
        sibling = (x, y, 1 - c)
        out = []
        for w in range(n):
            hr = refs[w].shape[1] // 2
            if whole:
                out.append(pltpu.make_async_remote_copy(
                    src_ref=refs[w].at[:, pl.ds(c * hr, hr)], dst_ref=refs[w].at[:, pl.ds((1 - c) * hr, hr)], send_sem=send.at[w],
                    recv_sem=recv.at[w], device_id=sibling, device_id_type=MESH))
            else:
                out += _row_pieces(refs[w], c * hr, refs[w], c * hr, hr, send.at[w], recv.at[w], sibling)
        return out
    return copies


def _small_gather_copies(refs, send, recv):
    vec, land = refs
    x, y, c = _place()
    me = 4 * x + 2 * y + c
    return [pltpu.make_async_remote_copy(src_ref=vec, dst_ref=land.at[me], send_sem=send.at[k], recv_sem=recv.at[k],
                                         device_id=peer, device_id_type=MESH) for k, (peer, _) in enumerate(_all_devices(x, y, c))]


def _small_sum(vec, land, me, n_sum, d):
    rows = vec.shape[0]

    def body(me_ref, v_ref, l_ref, sum_ref, kept_ref):
        def slot(k):
            return jnp.where(me_ref[0] == k, v_ref[...], l_ref[k])

        total = slot(0)[0:n_sum, :]
        kept_ref[0] = slot(0)[n_sum:rows, :]
        for k in range(1, N_DEV):
            total = total + slot(k)[0:n_sum, :]
            kept_ref[k] = slot(k)[n_sum:rows, :]
        sum_ref[...] = total
        loss = 0.5 / d * jnp.sum(total[0:8, :])
        sum_ref[0:8, :] = jnp.full((8, 128), loss, F32)

    vm = pl.BlockSpec(memory_space=pltpu.VMEM)
    return pl.pallas_call(
        body, name="small_sum", in_specs=[pl.BlockSpec(memory_space=pltpu.SMEM), vm, vm], out_specs=[vm, vm],
        out_shape=[jax.ShapeDtypeStruct((n_sum, 128), F32), jax.ShapeDtypeStruct((N_DEV, rows - n_sum, 128), F32)],
        compiler_params=pltpu.CompilerParams(vmem_limit_bytes=VMEM_LIMIT_V7X))(me, vec, land)


def _pair_exchange_halves(views, name):
    n = len(views)
    half_shapes = [(v.shape[0], v.shape[1] // 2, v.shape[2]) for v in views]

    def body(*refs):
        ins, got = refs[:n], refs[n:2 * n]
        send, recv = refs[2 * n:]
        x, y, c = _place()
        sibling = (x, y, 1 - c)
        for w in range(n):
            hr = half_shapes[w][1]
            for cp in _row_pieces(ins[w], (1 - c) * hr, got[w], 0, hr, send.at[w], recv.at[w], sibling):
                cp.start()
        for w in range(n):
            hr = half_shapes[w][1]
            pltpu.make_async_remote_copy(src_ref=ins[w].at[:, pl.ds((1 - c) * hr, hr)], dst_ref=got[w], send_sem=send.at[w],
                                         recv_sem=recv.at[w], device_id=sibling, device_id_type=MESH).wait()

    any_spec = pl.BlockSpec(memory_space=pl.ANY)
    return pl.pallas_call(
        body, name=name,
        out_shape=[jax.ShapeDtypeStruct(s, v.dtype) for s, v in zip(half_shapes, views)],
        in_specs=[any_spec] * n, out_specs=[any_spec] * n,
        scratch_shapes=[pltpu.SemaphoreType.DMA((n,)), pltpu.SemaphoreType.DMA((n,))],
    )(*views)


def _pair_sum(view, got, core, name):
    s, r, cols = view.shape
    hr = r // 2
    tr = _row_tile(hr, cols)
    nb = hr // tr

    def body(core_ref, a_ref, b_ref, o_ref):
        del core_ref
        o_ref[...] = (a_ref[...].astype(F32) + b_ref[...].astype(F32)).astype(BF16)

    same = pl.BlockSpec((None, tr, cols), lambda j, i, core_ref: (j, i, 0))
    return pl.pallas_call(
        body, name=name,
        grid_spec=pltpu.PrefetchScalarGridSpec(
            num_scalar_prefetch=1, grid=(s, nb),
            in_specs=[pl.BlockSpec((None, tr, cols), lambda j, i, core_ref: (j, core_ref[0] * nb + i, 0)), same],
            out_specs=same),
        out_shape=jax.ShapeDtypeStruct((s, hr, cols), BF16), compiler_params=_params("parallel", "parallel"))(core, view, got)


def _piece_cols(part, col_sharded):
    return part.shape[2] // N_CHIPS if col_sharded else part.shape[2]


def _chip_sum(part, got, col_sharded, place, name):
    _, hr, _ = part.shape
    cols = _piece_cols(part, col_sharded)
    tr = _row_tile(hr, cols)
    nb = hr // tr

    def body(place_ref, own_ref, g0_ref, g1_ref, g2_ref, o_ref):
        del place_ref
        acc = own_ref[...].astype(F32) + g0_ref[...].astype(F32)
        o_ref[...] = acc + g1_ref[...].astype(F32) + g2_ref[...].astype(F32)

    if col_sharded:
        own = pl.BlockSpec((None, tr, cols), lambda i, pr: (0, i, pr[0]))
    else:
        own = pl.BlockSpec((None, tr, cols), lambda i, pr: (pr[0], i, 0))
    others = [pl.BlockSpec((None, tr, cols), lambda i, pr, k=k: (k, i, 0)) for k in range(3)]
    return pl.pallas_call(
        body, name=name,
        grid_spec=pltpu.PrefetchScalarGridSpec(
            num_scalar_prefetch=1, grid=(nb,), in_specs=[own] + others,
            out_specs=pl.BlockSpec((tr, cols), lambda i, pr: (pr[1] * nb + i, 0))),
        out_shape=jax.ShapeDtypeStruct((2 * hr, cols), F32), compiler_params=_params("parallel"))(place, part, got, got, got)


def _row_tile(rows, cols, itemsize=4, budget=2 << 20):
    t = rows
    while t * cols * itemsize > budget and t % 16 == 0:
        t //= 2
    return t


def _adamw(w, g, m, v, name, after):
    rows, cols = w.shape
    tr = _row_tile(rows, cols, budget=1 << 20)
    extra = [] if after is None else [after]

    def body(w_ref, g_ref, m_ref, v_ref, *rest):
        d_ref, nm_ref, nv_ref = rest[len(extra):]
        g_ = g_ref[...]
        nm = ADAM_B1 * m_ref[...] + (1.0 - ADAM_B1) * g_
        nv = ADAM_B2 * v_ref[...] + (1.0 - ADAM_B2) * (g_ * g_)
        m_hat = nm / (1.0 - ADAM_B1 ** ADAM_STEP)
        v_hat = nv / (1.0 - ADAM_B2 ** ADAM_STEP)
        d_ref[...] = -ADAM_LR * (m_hat / (jnp.sqrt(v_hat) + ADAM_EPS) + ADAM_WD * w_ref[...])
        nm_ref[...] = nm
        nv_ref[...] = nv

    spec = pl.BlockSpec((tr, cols), lambda i: (i, 0))
    shp = jax.ShapeDtypeStruct((rows, cols), F32)
    return pl.pallas_call(body, name=name, grid=(rows // tr,),
                          in_specs=[spec] * 4 + [pl.BlockSpec(memory_space=pl.ANY)] * len(extra), out_specs=[spec] * 3,
                          out_shape=[shp] * 3, compiler_params=_params("parallel"))(w, g, m, v, *extra)


def _all_devices(x, y, c):
    out = []
    for k in range(1, N_DEV):
        peer = (x ^ ((k >> 2) & 1), y ^ ((k >> 1) & 1), c ^ (k & 1))
        out.append((peer, 4 * peer[0] + 2 * peer[1] + peer[2]))
    return out


def _ada_exchange(c, w_shard, b_ada):
    bsz, d = c.shape
    cs = w_shard.shape[1]

    def body(c_ref, w_ref, b_ref, mod_ref, act_ref, c_all, part, pieces, csend, crecv, psend, precv):
        x, y, core = _place()
        me = 4 * x + 2 * y + core
        chip = 2 * x + y
        c_all[me] = c_ref[...]
        peers = _all_devices(x, y, core)
        copies = []
        for k, (peer, _) in enumerate(peers):
            cp = pltpu.make_async_remote_copy(src_ref=c_ref, dst_ref=c_all.at[me], send_sem=csend.at[k], recv_sem=crecv.at[k],
                                              device_id=peer, device_id_type=MESH)
            cp.start()
            copies.append(cp)
        for k, (_, src) in enumerate(peers):
            pltpu.make_async_remote_copy(src_ref=c_ref, dst_ref=c_all.at[src], send_sem=csend.at[k], recv_sem=crecv.at[k],
                                         device_id=(x, y, core), device_id_type=MESH).wait_recv()
        rows = jnp.concatenate([c_all[i] for i in range(N_DEV)], axis=0)
        act = rows * _sigmoid(rows)
        act_ref[...] = act
        prod = jnp.dot(act.astype(BF16), w_ref[...].astype(BF16), preferred_element_type=F32)
        for i in range(N_DEV):
            part[i] = prod[i * bsz:(i + 1) * bsz, :]
        pieces[chip] = part[me]
        chips = _other_chips(x, y)
        for k, (cx, cy) in enumerate(chips):
            cp = pltpu.make_async_remote_copy(src_ref=part.at[4 * cx + 2 * cy + core], dst_ref=pieces.at[chip],
                                              send_sem=psend.at[k], recv_sem=precv.at[k], device_id=(cx, cy, core),
                                              device_id_type=MESH)
            cp.start()
            copies.append(cp)
        for k, (cx, cy) in enumerate(chips):
            pltpu.make_async_remote_copy(src_ref=part.at[me], dst_ref=pieces.at[2 * cx + cy], send_sem=psend.at[k],
                                         recv_sem=precv.at[k], device_id=(cx, cy, core), device_id_type=MESH).wait_recv()
        for cp in copies:
            cp.wait_send()
        mod_ref[...] = jnp.concatenate([pieces[j] for j in range(N_CHIPS)], axis=1) + b_ref[...]

    vm = pl.BlockSpec(memory_space=pltpu.VMEM)
    return pl.pallas_call(
        body, name="ada_exchange", in_specs=[vm] * 3, out_specs=[vm] * 2,
        out_shape=[jax.ShapeDtypeStruct((bsz, 3 * d), F32), jax.ShapeDtypeStruct((N_DEV * bsz, d), F32)],
        scratch_shapes=[pltpu.VMEM((N_DEV, bsz, d), F32), pltpu.VMEM((N_DEV, bsz, cs), F32), pltpu.VMEM((N_CHIPS, bsz, cs), F32),
                        pltpu.SemaphoreType.DMA((N_DEV - 1,)), pltpu.SemaphoreType.DMA((N_DEV - 1,)),
                        pltpu.SemaphoreType.DMA((3,)), pltpu.SemaphoreType.DMA((3,))],
        compiler_params=pltpu.CompilerParams(vmem_limit_bytes=VMEM_LIMIT_V7X))(c, w_shard, b_ada)


def _grad_w_ada_cols(act_t, d_cols):
    d, n = act_t.shape
    cs = d_cols.shape[1]

    def body(a_ref, g_ref, o_ref):
        a, g = a_ref[...], g_ref[...]
        acc = a[:, 0:1] * g[0:1, :]
        for b in range(1, n):
            acc = acc + a[:, b:b + 1] * g[b:b + 1, :]
        o_ref[...] = acc

    vm = pl.BlockSpec(memory_space=pltpu.VMEM)
    return pl.pallas_call(body, name="grad_w_ada", in_specs=[vm] * 2, out_specs=vm,
                          out_shape=jax.ShapeDtypeStruct((d, cs), F32),
                          compiler_params=pltpu.CompilerParams(vmem_limit_bytes=VMEM_LIMIT_V7X))(act_t, d_cols)


def _permute_w_in(w_nat, lay):
    d = lay.d
    group = 4

    def call(name, width, n_pieces, nat_piece, out_block0, prev):
        def body(*refs):
            refs[-1][...] = jnp.concatenate([r[...] for r in refs[:group]], axis=1)

        in_specs = [pl.BlockSpec((d, width), lambda s, m=m: (0, nat_piece(group * s + m))) for m in range(group)]
        args = [w_nat] * group
        aliases = {}
        if prev is not None:
            in_specs.append(pl.BlockSpec(memory_space=pl.ANY))
            args.append(prev)
            aliases = {group: 0}
        return pl.pallas_call(
            body, name=name, grid=(n_pieces // group,), in_specs=in_specs,
            out_specs=pl.BlockSpec((d, group * width), lambda s: (0, out_block0 + s)),
            out_shape=jax.ShapeDtypeStruct((d, lay.np), BF16), input_output_aliases=aliases,
            compiler_params=_params("arbitrary"))(*args)

    w_all = call("permute_w_attn", SLAB, 2 * PAIR_SLABS, lay.attn_nat_slab, 0, None)
    n_rest = 6 * d // CONV_TILE
    if n_rest % group:
        group = 2
    return call("permute_w_rest", CONV_TILE, n_rest, lay.rest_nat_tile, lay.c0 // (group * CONV_TILE), w_all)


def _project(x2, mod3, w_all, b_all, seq, col0, ncols, tn, out_dtype, want_ht, name):
    t, d = x2.shape
    tm = min(1024, seq)
    per_seq = seq // tm
    j0 = col0 // tn

    def body(x_ref, mod_ref, w_ref, b_ref, o_ref, *rest):
        h_ref = rest[-1]

        @pl.when(pl.program_id(1) == 0)
        def _():
            h = x_ref[...] * (1.0 + mod_ref[:, d:2 * d]) + mod_ref[:, 0:d]
            h_ref[...] = h.astype(BF16)
            if want_ht:
                rest[0][...] = h.T.astype(BF16)

        o_ref[...] = (jnp.dot(h_ref[...], w_ref[...], preferred_element_type=F32) + b_ref[...]).astype(out_dtype)

    out_shape = [jax.ShapeDtypeStruct((t, ncols), out_dtype)]
    out_specs = [pl.BlockSpec((tm, tn), lambda i, j: (i, j))]
    if want_ht:
        out_shape.append(jax.ShapeDtypeStruct((d, t), BF16))
        out_specs.append(pl.BlockSpec((d, tm), lambda i, j: (0, i)))
    return pl.pallas_call(
        body, name=name, grid=(t // tm, ncols // tn),
        in_specs=[pl.BlockSpec((tm, d), lambda i, j: (i, 0)),
                  pl.BlockSpec((None, 1, 3 * d), lambda i, j: (i // per_seq, 0, 0)),
                  pl.BlockSpec((d, tn), lambda i, j: (0, j0 + j)),
                  pl.BlockSpec((1, tn), lambda i, j: (0, j0 + j))],
        out_specs=out_specs, out_shape=out_shape,
        scratch_shapes=[pltpu.VMEM((tm, d), BF16)],
        compiler_params=_params("arbitrary", "arbitrary"))(x2, mod3, w_all, b_all)


def _slope(g, p, hh):
    head = 4 * g + 2 * p + hh
    return 2.0 ** (-ALIBI_MAX_EXP * (head + 1.0) / N_HEADS)


def _ld_rows(ref, start, n, stride):
    if stride == 1:
        return ref[pl.ds(start, n), :]
    return ref[pl.ds(start, n, stride=stride), :]


def _st_rows(ref, start, n, stride, val):
    if stride == 1:
        ref[pl.ds(start, n), :] = val
    else:
        ref[pl.ds(start, n, stride=stride), :] = val


def _sub_blocks(g, seq):
    return seq // DILATIONS[g] // SUB


def _key_rows(g, seq):
    return SUB if _sub_blocks(g, seq) == 1 else 2 * SUB


def _fill_bias(bias_ref, p, seq):
    for g in range(N_GROUPS):
        nk = _key_rows(g, seq)
        diff = lax.broadcasted_iota(jnp.int32, (SUB, nk), 0) - lax.broadcasted_iota(jnp.int32, (SUB, nk), 1)
        for i, off in enumerate((0, SUB)):
            if i == 1 and nk == SUB:
                continue
            delta = diff + off
            ok = (delta >= 0) & (delta <= SUB)
            dist = (delta * DILATIONS[g]).astype(F32)
            for hh in range(2):
                slope = jnp.where(p == 0, _slope(g, 0, hh), _slope(g, 1, hh))
                bias_ref[g, i, hh, :, 0:nk] = jnp.where(ok, -slope * dist, NEG)


def _to_sub_major(pa_ref, col, sub_ref, stage, dil, seq):
    cols = slice(col * SLAB, (col + 1) * SLAB)
    if dil == 1:
        sub_ref[...] = pa_ref[:, cols]
        return
    n = seq // dil
    stage[...] = pa_ref[:, cols].astype(F32)
    for r in range(dil):
        sub_ref[pl.ds(r * n, n), :] = stage[pl.ds(r, n, stride=dil), :].astype(BF16)


def _block_rows(it, g, seq):
    dil, nb = DILATIONS[g], _sub_blocks(g, seq)
    row0 = pl.multiple_of(it * SUB, SUB)
    if nb == 1:
        return row0, row0, 0, it
    blk = it % nb
    first = blk == 0
    krow0 = pl.multiple_of(row0 - jnp.where(first, 0, SUB), SUB)
    nat = row0 if dil == 1 else it // nb + dil * SUB * blk
    return row0, krow0, jnp.where(first, 0, 1), nat


def _nt(a, b):
    return lax.dot_general(a, b, (((1,), (1,)), ((), ())), preferred_element_type=F32)


def _tn(a, b):
    return lax.dot_general(a, b, (((0,), (0,)), ((), ())), preferred_element_type=F32)


def _head_sums(t):
    rows = t.shape[0]
    lo = jnp.broadcast_to(jnp.sum(t[:, :HEAD_DIM], axis=-1, keepdims=True), (rows, HEAD_DIM))
    hi = jnp.broadcast_to(jnp.sum(t[:, HEAD_DIM:], axis=-1, keepdims=True), (rows, HEAD_DIM))
    return jnp.concatenate([lo, hi], axis=-1)


def _attn_fwd(pa, bsz, seq):
    t = pa.shape[0]
    n_blocks = seq // SUB
    chunk = 256

    def body(pa_ref, o_ref, lse_ref, a_ref, sub, stage, bias_ref, s_buf, p_buf, l_buf):
        p = pl.program_id(1)
        _fill_bias(bias_ref, p, seq)
        head0 = lax.broadcasted_iota(jnp.int32, (SUB, SLAB), 1) < HEAD_DIM
        for g in range(N_GROUPS):
            dil = DILATIONS[g]
            for w in range(3):
                _to_sub_major(pa_ref, 3 * w + g, sub.at[w], stage, dil, seq)
            nk = _key_rows(g, seq)

            def trip(i, carry, g=g, dil=dil, nk=nk):
                places = [_block_rows(BLOCKS_PER_TRIP * i + j, g, seq) for j in range(BLOCKS_PER_TRIP)]
                for j, (row0, krow0, _, _) in enumerate(places):
                    q = sub[0, pl.ds(row0, SUB), :]
                    zero = jnp.zeros_like(q)
                    q2 = jnp.concatenate([jnp.where(head0, q, zero), jnp.where(head0, zero, q)], axis=0) * (HEAD_DIM ** -0.5)
                    s_buf[j, :, 0:nk] = _nt(q2, sub[1, pl.ds(krow0, nk), :])
                for j, (_, _, bi, _) in enumerate(places):
                    for c in range(0, 2 * SUB, SOFTMAX_ROWS):
                        hh, r = divmod(c, SUB)
                        s = s_buf[j, c:c + SOFTMAX_ROWS, 0:nk] + bias_ref[g, bi, hh, r:r + SOFTMAX_ROWS, 0:nk]
                        m = jnp.max(s, axis=-1, keepdims=True)
                        e = jnp.exp(s - m)
                        den = jnp.sum(e, axis=-1, keepdims=True)
                        p_buf[j, c:c + SOFTMAX_ROWS, 0:nk] = (e * (1.0 / den)).astype(BF16)
                        l_buf[j, c:c + SOFTMAX_ROWS, :] = jnp.broadcast_to(m + jnp.log(den), (SOFTMAX_ROWS, SLAB))
                for j, (_, krow0, _, nat) in enumerate(places):
                    o2 = jnp.dot(p_buf[j, :, 0:nk], sub[2, pl.ds(krow0, nk), :], preferred_element_type=F32)
                    _st_rows(o_ref.at[g], nat, SUB, dil, jnp.where(head0, o2[0:SUB], o2[SUB:2 * SUB]))
                    _st_rows(lse_ref.at[g], nat, SUB, dil, jnp.where(head0, l_buf[j, 0:SUB, :], l_buf[j, SUB:2 * SUB, :]))
                return carry

            lax.fori_loop(0, n_blocks // BLOCKS_PER_TRIP, trip, 0)

        def mix(i, carry):
            rows = pl.ds(pl.multiple_of(i * chunk, chunk), chunk)
            l0, l1, l2 = lse_ref[0, rows, :], lse_ref[1, rows, :], lse_ref[2, rows, :]
            m = jnp.maximum(jnp.maximum(l0, l1), l2)
            e0, e1, e2 = jnp.exp(l0 - m), jnp.exp(l1 - m), jnp.exp(l2 - m)
            tot = e0 + e1 + e2
            o = (e0 / tot) * o_ref[0, rows, :] + (e1 / tot) * o_ref[1, rows, :] + (e2 / tot) * o_ref[2, rows, :]
            z = pa_ref[rows, 9 * SLAB:10 * SLAB].astype(F32)
            a_ref[rows, :] = (o * (z * _sigmoid(z))).astype(BF16)
            return carry

        lax.fori_loop(0, seq // chunk, mix, 0)

    big = jax.ShapeDtypeStruct((N_GROUPS, t, 2 * SLAB), F32)
    return pl.pallas_call(
        body, name="attn_fwd", grid=(bsz, 2),
        in_specs=[pl.BlockSpec((seq, PAIR_COLS), lambda b, p: (b, p))],
        out_specs=[pl.BlockSpec((N_GROUPS, seq, SLAB), lambda b, p: (0, b, p)),
                   pl.BlockSpec((N_GROUPS, seq, SLAB), lambda b, p: (0, b, p)),
                   pl.BlockSpec((seq, SLAB), lambda b, p: (b, p))],
        out_shape=[big, big, jax.ShapeDtypeStruct((t, 2 * SLAB), BF16)],
        scratch_shapes=[pltpu.VMEM((3, seq, SLAB), BF16), pltpu.VMEM((seq, SLAB), F32),
                        pltpu.VMEM((N_GROUPS, 2, 2, SUB, 2 * SUB), F32), pltpu.VMEM((BLOCKS_PER_TRIP, 2 * SUB, 2 * SUB), F32),
                        pltpu.VMEM((BLOCKS_PER_TRIP, 2 * SUB, 2 * SUB), BF16), pltpu.VMEM((BLOCKS_PER_TRIP, 2 * SUB, SLAB), F32)],
        compiler_params=_params("arbitrary", "arbitrary"))(pa)


def _shift_down(v, k):
    rows = lax.broadcasted_iota(jnp.int32, v.shape, 0)
    return jnp.where(rows >= k, pltpu.roll(v, k, 0), 0.0)


def _shift_up(v, k):
    n = v.shape[0]
    rows = lax.broadcasted_iota(jnp.int32, v.shape, 0)
    return jnp.where(rows < n - k, pltpu.roll(v, n - k, 0), 0.0)


def _conv_fwd(pr, conv_w, bsz, seq, d):
    t = pr.shape[0]
    ct = CONV_TILE

    def body(p_ref, cw_ref, o_ref):
        u = p_ref[:, 2 * ct:3 * ct].astype(F32) * p_ref[:, 0:ct].astype(F32)
        cw = cw_ref[...]
        conv = cw[0:1, :] * _shift_down(u, 2)
        conv = conv + cw[1:2, :] * _shift_down(u, 1)
        conv = conv + cw[2:3, :] * u
        z = p_ref[:, 3 * ct:4 * ct].astype(F32)
        o_ref[...] = (p_ref[:, ct:2 * ct].astype(F32) * conv * (z * _sigmoid(z))).astype(BF16)

    return pl.pallas_call(
        body, name="conv_fwd", grid=(bsz, d // ct),
        in_specs=[pl.BlockSpec((seq, 4 * ct), lambda b, j: (b, j)), pl.BlockSpec((3, ct), lambda b, j: (0, j))],
        out_specs=pl.BlockSpec((seq, ct), lambda b, j: (b, j)),
        out_shape=jax.ShapeDtypeStruct((t, d), BF16), compiler_params=_params("parallel", "parallel"))(pr, conv_w)


def _tail(a_in, b_in, pr, x2, target2, mod3, w_pa, w_pc, w_out, b_out, ln_g, ln_b, seq, lay):
    t, d = x2.shape
    tm = 512
    per_seq = seq // tm
    n_steps = t // tm
    gate_blk = 4 * d // d

    def nt(a, b):
        return lax.dot_general(a, b, (((1,), (1,)), ((), ())), preferred_element_type=F32)

    def tn(a, b):
        return lax.dot_general(a, b, (((0,), (0,)), ((), ())), preferred_element_type=F32)

    def body(a_ref, b_ref, ga_ref, gb_ref, x_ref, tg_ref, mod_ref, wpa_ref, wpc_ref, wo_ref, bo_ref, lg_ref, lb_ref,
             dpg_ref, da_ref, db_ref, gx_ref, dgate_ref, small_ref, gwpa_hbm, gwpc_hbm, gwo_hbm,
             acc_pa, acc_pc, acc_o, sem):
        i = pl.program_id(0)

        @pl.when(i == 0)
        def _():
            acc_pa[...] = jnp.zeros_like(acc_pa)
            acc_pc[...] = jnp.zeros_like(acc_pc)
            acc_o[...] = jnp.zeros_like(acc_o)
            small_ref[...] = jnp.zeros_like(small_ref)

        @pl.when(i % per_seq == 0)
        def _():
            dgate_ref[...] = jnp.zeros_like(dgate_ref)

        a_bf, b_bf = a_ref[...], b_ref[...]
        y_attn = jnp.dot(a_bf, wpa_ref[...], preferred_element_type=F32)
        y_conv = jnp.dot(b_bf, wpc_ref[...], preferred_element_type=F32)
        sa, sb = _sigmoid(ga_ref[...].astype(F32)), _sigmoid(gb_ref[...].astype(F32))
        merged = (sa * y_attn + sb * y_conv).astype(BF16)
        mo = jnp.dot(merged, wo_ref[...], preferred_element_type=F32) + bo_ref[...]
        gate = mod_ref[:, 2 * d:3 * d]
        r = ALPHA * x_ref[...] + gate * mo
        mu = jnp.mean(r, axis=-1, keepdims=True)
        cen = r - mu
        var = jnp.mean(cen * cen, axis=-1, keepdims=True)
        rstd = lax.rsqrt(var + LN_EPS)
        xhat = cen * rstd
        err = xhat * lg_ref[...] + lb_ref[...] - tg_ref[...]
        dy = err * (1.0 / d)
        dxhat = dy * lg_ref[...]
        dr = rstd * (dxhat - jnp.mean(dxhat, axis=-1, keepdims=True) - xhat * jnp.mean(dxhat * xhat, axis=-1, keepdims=True))
        gx_ref[...] = ALPHA * dr
        dgate_ref[...] += jnp.sum(dr * mo, axis=0, keepdims=True)
        d_mo = dr * gate
        small_ref[0:1, :] += jnp.sum(d_mo, axis=0, keepdims=True)
        small_ref[1:2, :] += jnp.sum(dy * xhat, axis=0, keepdims=True)
        small_ref[2:3, :] += jnp.sum(dy, axis=0, keepdims=True)
        small_ref[3:4, :] += jnp.sum(err * err, axis=0, keepdims=True)
        d_mo_bf = d_mo.astype(BF16)
        acc_o[...] += tn(merged, d_mo_bf)
        dmerged = nt(d_mo_bf, wo_ref[...])
        dy_attn = (dmerged * sa).astype(BF16)
        dy_conv = (dmerged * sb).astype(BF16)
        dpg_ref[:, 0:d] = (dmerged * y_attn * sa * (1.0 - sa)).astype(BF16)
        dpg_ref[:, d:2 * d] = (dmerged * y_conv * sb * (1.0 - sb)).astype(BF16)
        acc_pa[...] += tn(a_bf, dy_attn)
        acc_pc[...] += tn(b_bf, dy_conv)
        da_ref[...] = nt(dy_attn, wpa_ref[...])
        db_ref[...] = nt(dy_conv, wpc_ref[...])

        @pl.when(i == n_steps - 1)
        def _():
            copies = [pltpu.make_async_copy(acc_pa, gwpa_hbm, sem.at[0]), pltpu.make_async_copy(acc_pc, gwpc_hbm, sem.at[1]),
                      pltpu.make_async_copy(acc_o, gwo_hbm, sem.at[2])]
            for cp in copies:
                cp.start()
            for cp in copies:
                cp.wait()

    row = lambda w: pl.BlockSpec((tm, w), lambda i: (i, 0))
    const = lambda shp: pl.BlockSpec(shp, lambda i: (0,) * len(shp), pipeline_mode=pl.Buffered(1))
    any_spec = pl.BlockSpec(memory_space=pl.ANY)
    return pl.pallas_call(
        body, name="tail", grid=(n_steps,),
        in_specs=[row(Z_WIDTH), row(d),
                  pl.BlockSpec((tm, d), lambda i: (i, gate_blk)), pl.BlockSpec((tm, d), lambda i: (i, gate_blk + 1)),
                  row(d), row(d), pl.BlockSpec((None, 1, 3 * d), lambda i: (i // per_seq, 0, 0)),
                  const((Z_WIDTH, d)), const((d, d)), const((d, d)), const((1, d)), const((1, d)), const((1, d))],
        out_specs=[pl.BlockSpec((tm, 2 * d), lambda i: (i, lay.g0 // (2 * d))), row(Z_WIDTH), row(d), row(d),
                   pl.BlockSpec((None, 1, d), lambda i: (i // per_seq, 0, 0)), pl.BlockSpec((8, d), lambda i: (0, 0)),
                   any_spec, any_spec, any_spec],
        out_shape=[jax.ShapeDtypeStruct((t, lay.np), BF16), jax.ShapeDtypeStruct((t, Z_WIDTH), F32),
                   jax.ShapeDtypeStruct((t, d), F32), jax.ShapeDtypeStruct((t, d), F32),
                   jax.ShapeDtypeStruct((t // seq, 1, d), F32), jax.ShapeDtypeStruct((8, d), F32),
                   jax.ShapeDtypeStruct((Z_WIDTH, d), F32), jax.ShapeDtypeStruct((d, d), F32),
                   jax.ShapeDtypeStruct((d, d), F32)],
        scratch_shapes=[pltpu.VMEM((Z_WIDTH, d), F32), pltpu.VMEM((d, d), F32), pltpu.VMEM((d, d), F32),
                        pltpu.SemaphoreType.DMA((3,))],
        compiler_params=_params("arbitrary"),
    )(a_in, b_in, pr, pr, x2, target2, mod3, w_pa, w_pc, w_out, b_out, ln_g, ln_b)


def _conv_bwd(dproj, db, pr, conv_w, bsz, seq, lay):
    d = lay.d
    ct = CONV_TILE
    base = lay.c0 // (4 * ct)

    def body(dp_in, db_ref, p_ref, cw_ref, dp_ref, gcw_ref):
        del dp_in
        u_x, g_b, g_c, z = [p_ref[:, k * ct:(k + 1) * ct].astype(F32) for k in range(4)]
        cw = cw_ref[...]
        u = g_c * u_x
        u1, u2 = _shift_down(u, 1), _shift_down(u, 2)
        conv = cw[0:1, :] * u2 + cw[1:2, :] * u1 + cw[2:3, :] * u
        sig = _sigmoid(z)
        sl = z * sig
        dbv = db_ref[...]
        gbc = g_b * conv
        dp_ref[:, ct:2 * ct] = (dbv * sl * conv).astype(BF16)
        dp_ref[:, 3 * ct:4 * ct] = (dbv * gbc * (sig * (1.0 + z * (1.0 - sig)))).astype(BF16)
        dconv = dbv * sl * g_b

        @pl.when(pl.program_id(1) == 0)
        def _():
            gcw_ref[...] = jnp.zeros_like(gcw_ref)

        gcw_ref[0:1, :] += jnp.sum(dconv * u2, axis=0, keepdims=True)
        gcw_ref[1:2, :] += jnp.sum(dconv * u1, axis=0, keepdims=True)
        gcw_ref[2:3, :] += jnp.sum(dconv * u, axis=0, keepdims=True)
        du = cw[2:3, :] * dconv + cw[1:2, :] * _shift_up(dconv, 1) + cw[0:1, :] * _shift_up(dconv, 2)
        dp_ref[:, 0:ct] = (du * g_c).astype(BF16)
        dp_ref[:, 2 * ct:3 * ct] = (du * u_x).astype(BF16)

    return pl.pallas_call(
        body, name="conv_bwd", grid=(d // ct, bsz),
        in_specs=[pl.BlockSpec(memory_space=pl.ANY), pl.BlockSpec((seq, ct), lambda j, b: (b, j)),
                  pl.BlockSpec((seq, 4 * ct), lambda j, b: (b, j)), pl.BlockSpec((3, ct), lambda j, b: (0, j))],
        out_specs=[pl.BlockSpec((seq, 4 * ct), lambda j, b: (b, base + j)), pl.BlockSpec((8, ct), lambda j, b: (0, j))],
        out_shape=[jax.ShapeDtypeStruct(dproj.shape, BF16), jax.ShapeDtypeStruct((8, d), F32)],
        input_output_aliases={0: 0}, compiler_params=_params("arbitrary", "arbitrary"))(dproj, db, pr, conv_w)


def _attn_bwd(dproj, pa, o_all, lse_all, da, bsz, seq, after):
    n_blocks = seq // SUB
    chunk = 256

    def body(dp_in, pa_ref, o_ref, lse_ref, da_ref, after_ref, dp_ref, sub, stage, dsub, dog, cvec, bias_ref,
             s_buf, dp_buf, ds_buf, pb_buf, q2_buf, do2_buf, l_buf, c_buf):
        del dp_in, after_ref
        p = pl.program_id(1)
        _fill_bias(bias_ref, p, seq)
        head0 = lax.broadcasted_iota(jnp.int32, (SUB, SLAB), 1) < HEAD_DIM

        def mix_bwd(i, carry):
            rows = pl.ds(pl.multiple_of(i * chunk, chunk), chunk)
            ls = [lse_ref[g, rows, :] for g in range(N_GROUPS)]
            os_ = [o_ref[g, rows, :] for g in range(N_GROUPS)]
            m = jnp.maximum(jnp.maximum(ls[0], ls[1]), ls[2])
            es = [jnp.exp(l - m) for l in ls]
            tot = es[0] + es[1] + es[2]
            ws = [e / tot for e in es]
            o = ws[0] * os_[0] + ws[1] * os_[1] + ws[2] * os_[2]
            z = pa_ref[rows, 9 * SLAB:10 * SLAB].astype(F32)
            sig = _sigmoid(z)
            dav = da_ref[rows, :]
            do = dav * (z * sig)
            dp_ref[rows, 9 * SLAB:10 * SLAB] = (dav * o * (sig * (1.0 + z * (1.0 - sig)))).astype(BF16)
            wsum = _head_sums(do * o)
            for g in range(N_GROUPS):
                dog[g, rows, :] = ws[g] * do
                cvec[g, rows, :] = -(ws[g] * wsum)
            return carry

        lax.fori_loop(0, seq // chunk, mix_bwd, 0)

        for g in range(N_GROUPS):
            dil = DILATIONS[g]
            for w in range(3):
                _to_sub_major(pa_ref, 3 * w + g, sub.at[w], stage, dil, seq)
            dsub[1] = jnp.zeros((seq, SLAB), F32)
            dsub[2] = jnp.zeros((seq, SLAB), F32)
            nk = _key_rows(g, seq)

            def trip(i, carry, g=g, dil=dil, nk=nk):
                places = [_block_rows(BLOCKS_PER_TRIP * i + j, g, seq) for j in range(BLOCKS_PER_TRIP)]
                for j, (row0, krow0, _, nat) in enumerate(places):
                    q = sub[0, pl.ds(row0, SUB), :]
                    do = _ld_rows(dog.at[g], nat, SUB, dil).astype(BF16)
                    zero = jnp.zeros_like(q)
                    q2 = jnp.concatenate([jnp.where(head0, q, zero), jnp.where(head0, zero, q)], axis=0)
                    do2 = jnp.concatenate([jnp.where(head0, do, zero), jnp.where(head0, zero, do)], axis=0)
                    q2_buf[j] = q2
                    do2_buf[j] = do2
                    s_buf[j, :, 0:nk] = _nt(q2 * (HEAD_DIM ** -0.5), sub[1, pl.ds(krow0, nk), :])
                    dp_buf[j, :, 0:nk] = _nt(do2, sub[2, pl.ds(krow0, nk), :])
                    l_buf[j] = _ld_rows(lse_ref.at[g], nat, SUB, dil)
                    c_buf[j] = _ld_rows(cvec.at[g], nat, SUB, dil)
                for j, (_, _, bi, _) in enumerate(places):
                    for c in range(0, 2 * SUB, SOFTMAX_ROWS):
                        hh, r = divmod(c, SUB)
                        lane = hh * HEAD_DIM
                        s = s_buf[j, c:c + SOFTMAX_ROWS, 0:nk] + bias_ref[g, bi, hh, r:r + SOFTMAX_ROWS, 0:nk]
                        prob = jnp.exp(s - l_buf[j, r:r + SOFTMAX_ROWS, lane:lane + 1])
                        dprob = dp_buf[j, c:c + SOFTMAX_ROWS, 0:nk] + c_buf[j, r:r + SOFTMAX_ROWS, lane:lane + 1]
                        ds_buf[j, c:c + SOFTMAX_ROWS, 0:nk] = (prob * dprob * (HEAD_DIM ** -0.5)).astype(BF16)
                        pb_buf[j, c:c + SOFTMAX_ROWS, 0:nk] = prob.astype(BF16)
                for j, (row0, krow0, _, _) in enumerate(places):
                    ds = ds_buf[j, :, 0:nk]
                    dq2 = jnp.dot(ds, sub[1, pl.ds(krow0, nk), :], preferred_element_type=F32)
                    dsub[0, pl.ds(row0, SUB), :] = jnp.where(head0, dq2[0:SUB], dq2[SUB:2 * SUB])
                    dsub[1, pl.ds(krow0, nk), :] += _tn(ds, q2_buf[j])
                    dsub[2, pl.ds(krow0, nk), :] += _tn(pb_buf[j, :, 0:nk], do2_buf[j])
                return carry

            lax.fori_loop(0, n_blocks // BLOCKS_PER_TRIP, trip, 0)
            for w in range(3):
                cols = slice((3 * w + g) * SLAB, (3 * w + g + 1) * SLAB)
                if dil == 1:
                    dp_ref[:, cols] = dsub[w].astype(BF16)
                else:
                    n = seq // dil
                    for r in range(dil):
                        stage[pl.ds(r, n, stride=dil), :] = dsub[w, pl.ds(r * n, n), :]
                    dp_ref[:, cols] = stage[...].astype(BF16)

    return pl.pallas_call(
        body, name="attn_bwd", grid=(bsz, 2),
        in_specs=[pl.BlockSpec(memory_space=pl.ANY), pl.BlockSpec((seq, PAIR_COLS), lambda b, p: (b, p)),
                  pl.BlockSpec((N_GROUPS, seq, SLAB), lambda b, p: (0, b, p)),
                  pl.BlockSpec((N_GROUPS, seq, SLAB), lambda b, p: (0, b, p)),
                  pl.BlockSpec((seq, SLAB), lambda b, p: (b, p)), pl.BlockSpec(memory_space=pl.ANY)],
        out_specs=pl.BlockSpec((seq, PAIR_COLS), lambda b, p: (b, p)),
        out_shape=jax.ShapeDtypeStruct(dproj.shape, BF16), input_output_aliases={0: 0},
        scratch_shapes=[pltpu.VMEM((3, seq, SLAB), BF16), pltpu.VMEM((seq, SLAB), F32), pltpu.VMEM((3, seq, SLAB), F32),
                        pltpu.VMEM((3, seq, SLAB), F32), pltpu.VMEM((3, seq, SLAB), F32),
                        pltpu.VMEM((N_GROUPS, 2, 2, SUB, 2 * SUB), F32),
                        pltpu.VMEM((BLOCKS_PER_TRIP, 2 * SUB, 2 * SUB), F32), pltpu.VMEM((BLOCKS_PER_TRIP, 2 * SUB, 2 * SUB), F32),
                        pltpu.VMEM((BLOCKS_PER_TRIP, 2 * SUB, 2 * SUB), BF16), pltpu.VMEM((BLOCKS_PER_TRIP, 2 * SUB, 2 * SUB), BF16),
                        pltpu.VMEM((BLOCKS_PER_TRIP, 2 * SUB, SLAB), BF16), pltpu.VMEM((BLOCKS_PER_TRIP, 2 * SUB, SLAB), BF16),
                        pltpu.VMEM((BLOCKS_PER_TRIP, SUB, SLAB), F32), pltpu.VMEM((BLOCKS_PER_TRIP, SUB, SLAB), F32)],
        compiler_params=_params("arbitrary", "arbitrary"))(dproj, pa, o_all, lse_all, da, after)


def _grad_h(dproj, w_all, gx0, x2, mod3, seq, lay):
    t, d = x2.shape
    tm, tn = min(512, seq), min(512, d)
    per_seq = seq // tm

    def body(dp_ref, w_ref, gx0_ref, x_ref, scale_ref, gx_ref, dmod_ref):
        dh = _nt(dp_ref[:, 0:ATT], w_ref[:, 0:ATT]) + _nt(dp_ref[:, lay.c0:], w_ref[:, lay.c0:])
        gx_ref[...] = gx0_ref[...] + dh * (1.0 + scale_ref[...])

        @pl.when(pl.program_id(1) % per_seq == 0)
        def _():
            dmod_ref[...] = jnp.zeros_like(dmod_ref)

        dmod_ref[0:1, :] += jnp.sum(dh, axis=0, keepdims=True)
        dmod_ref[1:2, :] += jnp.sum(dh * x_ref[...], axis=0, keepdims=True)

    tile = pl.BlockSpec((tm, tn), lambda j, i: (i, j))
    return pl.pallas_call(
        body, name="grad_h", grid=(d // tn, t // tm),
        in_specs=[pl.BlockSpec((tm, lay.np), lambda j, i: (i, 0)), pl.BlockSpec((tn, lay.np), lambda j, i: (j, 0)),
                  tile, tile,
                  pl.BlockSpec((None, 1, tn), lambda j, i: (i // per_seq, 0, d // tn + j))],
        out_specs=[tile, pl.BlockSpec((None, 8, tn), lambda j, i: (i // per_seq, 0, j))],
        out_shape=[jax.ShapeDtypeStruct((t, d), F32), jax.ShapeDtypeStruct((t // seq, 8, d), F32)],
        compiler_params=_params("arbitrary", "arbitrary"))(dproj, w_all, gx0, x2, mod3)


def _grad_w_in(ht, dproj, seq, lay):
    d, t = ht.shape
    tm = seq
    n_i = t // tm

    def make_body(n_skip, n_pieces, tn, nat_tile):
        def body(*refs):
            refs = refs[n_skip:]
            ht_ref, dp_refs = refs[0], refs[1:1 + n_pieces]
            gw_hbm, gb_hbm, acc, bacc, gw_out, gb_out, sem = refs[1 + n_pieces:]
            i, j = pl.program_id(0), pl.program_id(1)
            dp = dp_refs[0][...] if n_pieces == 1 else jnp.concatenate([r[...] for r in dp_refs], axis=1)
            part = jnp.dot(ht_ref[...], dp, preferred_element_type=F32)
            bpart = jnp.sum(dp.astype(F32), axis=0, keepdims=True)

            if n_i > 1:
                @pl.when(i == 0)
                def _():
                    acc[j] = part
                    bacc[j] = bpart

                @pl.when((i > 0) & (i < n_i - 1))
                def _():
                    acc[j] += part
                    bacc[j] += bpart

            @pl.when(i == n_i - 1)
            def _():
                gw_out[...] = ((part + acc[j]) if n_i > 1 else part).astype(BF16)
                gb_out[...] = (bpart + bacc[j]) if n_i > 1 else bpart
                cols = pl.ds(pl.multiple_of(nat_tile(j) * tn, SLAB), tn)
                copies = [pltpu.make_async_copy(gw_out, gw_hbm.at[:, cols], sem.at[0]),
                          pltpu.make_async_copy(gb_out, gb_hbm.at[:, cols], sem.at[1])]
                for cp in copies:
                    cp.start()
                for cp in copies:
                    cp.wait()
        return body

    def call(name, pieces, n_tiles, nat_tile, prev):
        tn = sum(w for w, _ in pieces)
        any_spec = pl.BlockSpec(memory_space=pl.ANY)
        in_specs = [pl.BlockSpec((d, tm), lambda i, j: (0, i))]
        in_specs += [pl.BlockSpec((tm, w), lambda i, j, f=f: (i, f(j))) for w, f in pieces]
        args = [ht] + [dproj] * len(pieces)
        aliases = {}
        if prev is not None:
            in_specs = [any_spec] * 2 + in_specs
            args = list(prev) + args
            aliases = {0: 0, 1: 1}
        return pl.pallas_call(
            make_body(0 if prev is None else 2, len(pieces), tn, nat_tile), name=name, grid=(n_i, n_tiles), in_specs=in_specs,
            out_specs=[any_spec, any_spec],
            out_shape=[jax.ShapeDtypeStruct((d, lay.din), BF16), jax.ShapeDtypeStruct((1, lay.din), F32)],
            input_output_aliases=aliases,
            scratch_shapes=[pltpu.VMEM((n_tiles, d, tn), F32), pltpu.VMEM((n_tiles, 1, tn), F32), pltpu.VMEM((d, tn), BF16),
                            pltpu.VMEM((1, tn), F32), pltpu.SemaphoreType.DMA((2,))],
            compiler_params=_params("arbitrary", "arbitrary"))(*args)

    attn_pieces = [(SLAB, lambda j, m=m: (m % 2) * PAIR_SLABS + 2 * j + m // 2) for m in range(4)]
    first = call("grad_w_in_attn", attn_pieces, ATT // 512, lambda j: j, None)
    base = lay.c0 // CONV_TILE
    nct = lay.n_conv_tiles
    if nct % 2:
        return call("grad_w_in_rest", [(CONV_TILE, lambda j: base + j)], 6 * d // CONV_TILE, lay.rest_nat_tile, first)
    half = nct // 2

    def rest_piece(m):
        def perm_tile(j):
            conv = base + 4 * (2 * (j % half) + m) + j // half
            return jnp.where(j < 4 * half, conv, base + 2 * j + m)
        return (CONV_TILE, perm_tile)

    return call("grad_w_in_rest", [rest_piece(0), rest_piece(1)], 6 * d // 512, lambda j: ATT // 512 + j, first)


def _pack_rows(parts, width=128):
    flat = [p.reshape(-1) for p in parts]
    spans, rows = [], 0
    padded = []
    for f in flat:
        n = -(-f.shape[0] // (8 * width)) * 8
        padded.append(jnp.pad(f, (0, n * width - f.shape[0])).reshape(n, width))
        spans.append((rows, f.shape[0]))
        rows += n
    return jnp.concatenate(padded, axis=0), spans


def _unpack_rows(packed, spans, shapes, width=128):
    out = []
    for (row, n), shp in zip(spans, shapes):
        rows = -(-n // width)
        out.append(packed[row:row + rows].reshape(-1)[:n].reshape(shp))
    return out


def kernel(x, c, w_ada, b_ada, w_in, b_in, conv_w, w_proj_attn, w_proj_conv, w_out, b_out, ln_g, ln_b, loss_target, m_w_ada, m_b_ada, m_w_in, m_b_in, m_conv_w, m_w_proj_attn, m_w_proj_conv, m_w_out, m_b_out, m_ln_g, m_ln_b, v_w_ada, v_b_ada, v_w_in, v_b_in, v_conv_w, v_w_proj_attn, v_w_proj_conv, v_w_out, v_b_out, v_ln_g, v_ln_b):
    bsz, seq, d = x.shape
    t = bsz * seq
    lay = _Layout(d)
    col_sharded = [True, True, False, False]
    red_w = [w_in[0], w_proj_attn[0], w_proj_conv[0], w_out[0]]
    chip = 2 * lax.axis_index("x") + lax.axis_index("y")
    chip1 = chip.astype(jnp.int32).reshape(1)
    core1 = lax.axis_index("c").astype(jnp.int32).reshape(1)
    place = jnp.stack([chip, lax.axis_index("c")]).astype(jnp.int32)
    x2 = x.reshape(t, d)
    target2 = loss_target.reshape(t, d)

    mod, act_all = _ada_exchange(c, w_ada[0], b_ada)
    mod3 = mod.reshape(bsz, 1, 3 * d)

    cw_pad = jnp.pad(conv_w[0], ((0, 5), (0, 0))) + 0.0 * mod[0, 0]
    own_in_full = [_cast_into_full(red_w[w], col_sharded[w], chip1, f"cast_shard_{w}") for w in range(4)]
    (wi_f,), cw8 = _gather_weights(own_in_full[:1], col_sharded[:1], cw_pad)
    cw_full = cw8[0:3]
    late_copies = _direct_gather_copies(col_sharded[1:])
    late_send, late_recv, late_flying, late_token = _start_copies("gather_late_start", own_in_full[1:], (18,), cw8, late_copies)
    w_all = _permute_w_in(wi_f, lay)
    b_all = lay.perm_vector(b_in) + late_token[0, 0]

    rest_tn = 1024 if (6 * d) % 1024 == 0 else 512
    pa, = _project(x2, mod3, w_all, b_all, seq, 0, ATT, PAIR_COLS, BF16, False, "project_attn")
    pr, ht = _project(x2, mod3, w_all, b_all, seq, lay.c0, 6 * d, rest_tn, BF16, True, "project_rest")
    o_all, lse_all, a_in = _attn_fwd(pa, bsz, seq)
    b_in_act = _conv_fwd(pr, cw_full, bsz, seq, d)
    wpa_f, wpc_f, wo_f = _wait_copies("gather_late_wait", late_flying, late_send, late_recv, b_in_act, late_copies)
    (dproj, da_in, db_in, gx0, dgate, small_tail, gw_pa, gw_pc, gw_out) = _tail(
        a_in, b_in_act, pr, x2, target2, mod3, wpa_f, wpc_f, wo_f, b_out, ln_g, ln_b, seq, lay)

    late_views = _shard_views([gw_pa, gw_pc, gw_out], col_sharded[1:])
    late_lands = [lax.empty((v.shape[0], v.shape[1] // 2, v.shape[2]), v.dtype) for v in late_views]
    xl_send, xl_recv, xl_fly, xl_tok = _start_copies("grads_pair_exchange_late_start", late_views + late_lands, (3,), gw_out,
                                                     _pair_exchange_copies(3, False))
    dproj, gcw = _conv_bwd(dproj, db_in, pr, cw_full + xl_tok[0, 0], bsz, seq, lay)
    xl_done = _wait_copies("grads_pair_exchange_late_wait", xl_fly, xl_send, xl_recv, gcw, _pair_exchange_copies(3, True))
    late_parts = [_pair_sum(xl_done[w], xl_done[3 + w], core1, f"grads_pair_sum_late_{w}") for w in range(3)]

    late_cross = _chip_scatter_copies(3, col_sharded[1:])
    late_zone = [lax.empty((3, p.shape[1], _piece_cols(p, cs)), p.dtype) for p, cs in zip(late_parts, col_sharded[1:])]
    sl_send, sl_recv, sl_fly, sl_tok = _start_copies("grads_scatter_late_start", late_parts + late_zone, (9,), late_parts[0],
                                                     late_cross)
    dproj = _attn_bwd(dproj, pa, o_all, lse_all, da_in, bsz, seq, sl_tok)
    gw_in_bf, gb_in = _grad_w_in(ht, dproj, seq, lay)
    sl_done = _wait_copies("grads_scatter_late_wait", sl_fly, sl_send, sl_recv, gw_in_bf, late_cross)
    late_red = [_chip_sum(sl_done[w], sl_done[3 + w], col_sharded[1 + w], place, f"grads_chip_sum_late_{w}") for w in range(3)]

    in_view = _shard_views([gw_in_bf], col_sharded[:1])
    in_got = _pair_exchange_halves(in_view, "grads_pair_exchange_in")
    in_part = _pair_sum(in_view[0], in_got[0], core1, "grads_pair_sum_in")
    in_cross = _chip_scatter_copies(1, col_sharded[:1])
    in_zone = [lax.empty((3, in_part.shape[1], _piece_cols(in_part, True)), in_part.dtype)]
    si_send, si_recv, si_fly, si_tok = _start_copies("grads_scatter_in_start", [in_part] + in_zone, (3,), late_red[0], in_cross)
    jl_send, jl_recv, jl_fly, jl_tok = _start_copies("grads_pair_join_late_start", [f.reshape(1, *f.shape) for f in late_red], (3,),
                                                     si_tok, _pair_join_copies(3, False))
    grad_x2, dmod = _grad_h(dproj, w_all, gx0, x2, mod3 + (si_tok[0, 0] + jl_tok[0, 0]), seq, lay)
    jl_done = _wait_copies("grads_pair_join_late_wait", jl_fly, jl_send, jl_recv, grad_x2, _pair_join_copies(3, True))
    late_joined = [f[0] for f in jl_done]
    si_done = _wait_copies("grads_scatter_in_wait", si_fly, si_send, si_recv, grad_x2, in_cross)
    in_red = _chip_sum(si_done[0], si_done[1], True, place, "grads_chip_sum_in")

    d_ada = jnp.concatenate([dmod[:, 0, :], dmod[:, 1, :], dgate[:, 0, :]], axis=1)
    pieces = [small_tail[3], jnp.sum(d_ada, axis=0), gb_in[0], small_tail[0], small_tail[1], small_tail[2], gcw[0:3]]
    packed, spans = _pack_rows(pieces)
    kept_in, _ = _pack_rows([d_ada])
    rows_all = jnp.concatenate([packed, kept_in], axis=0)
    small_land = lax.empty((N_DEV,) + rows_all.shape, F32)
    sm_send, sm_recv, sm_fly, sm_tok = _start_copies("small_gather_start", [rows_all, small_land], (N_DEV - 1,), in_red,
                                                     _small_gather_copies)
    ji_send, ji_recv, ji_fly, ji_tok = _start_copies("grads_pair_join_in_start", [in_red.reshape(1, *in_red.shape)], (1,), sm_tok,
                                                     _pair_join_copies(1, False))
    big_w = [w_ada[0]] + red_w
    big_m = [m_w_ada[0], m_w_in[0], m_w_proj_attn[0], m_w_proj_conv[0], m_w_out[0]]
    big_v = [v_w_ada[0], v_w_in[0], v_w_proj_attn[0], v_w_proj_conv[0], v_w_out[0]]
    big_out = [None] * 5
    for w in range(3):
        big_out[2 + w] = _adamw(big_w[2 + w], late_joined[w], big_m[2 + w], big_v[2 + w], f"adamw_{2 + w}", ji_tok)
    sm_done = _wait_copies("small_gather_wait", sm_fly, sm_send, sm_recv, big_out[4][0], _small_gather_copies)
    me1 = (4 * lax.axis_index("x") + 2 * lax.axis_index("y") + lax.axis_index("c")).astype(jnp.int32).reshape(1)
    summed, kept = _small_sum(sm_done[0], sm_done[1], me1, packed.shape[0], d)
    loss = summed[0, 0]
    _, g_b_ada, g_b_in, g_b_out, g_ln_g, g_ln_b, g_cw_full = _unpack_rows(
        summed, spans, [(d,), (3 * d,), (lay.din,), (d,), (d,), (d,), (3, d)])
    g_cw = lax.dynamic_slice(g_cw_full, (0, chip * (d // N_CHIPS)), (3, d // N_CHIPS))
    d_ada_all = kept.reshape(N_DEV, -1)[:, :bsz * 3 * d].reshape(N_DEV * bsz, 3 * d)
    ada_cols = 3 * d // N_CHIPS
    g_w_ada = _grad_w_ada_cols(act_all.T, lax.dynamic_slice(d_ada_all, (0, chip * ada_cols), (N_DEV * bsz, ada_cols)))

    big_out[0] = _adamw(big_w[0], g_w_ada, big_m[0], big_v[0], "adamw_0", None)
    small_w = [b_ada, b_in, conv_w[0], b_out, ln_g, ln_b]
    small_g = [g_b_ada, g_b_in, g_cw, g_b_out, g_ln_g, g_ln_b]
    small_m = [m_b_ada, m_b_in, m_conv_w[0], m_b_out, m_ln_g, m_ln_b]
    small_v = [v_b_ada, v_b_in, v_conv_w[0], v_b_out, v_ln_g, v_ln_b]
    pw, sp = _pack_rows(small_w)
    pg, _ = _pack_rows(small_g)
    pm, _ = _pack_rows(small_m)
    pv, _ = _pack_rows(small_v)
    sd, sm, sv = _adamw(pw, pg, pm, pv, "adamw_small", None)
    ji_done = _wait_copies("grads_pair_join_in_wait", ji_fly, ji_send, ji_recv, sd, _pair_join_copies(1, True))
    g_big = [g_w_ada, ji_done[0][0]] + late_joined
    big_out[1] = _adamw(big_w[1], g_big[1], big_m[1], big_v[1], "adamw_1", None)
    shapes = [a.shape for a in small_w]
    sd, sm, sv = _unpack_rows(sd, sp, shapes), _unpack_rows(sm, sp, shapes), _unpack_rows(sv, sp, shapes)

    def order(wa, bA, wi, bI, cw, wpa, wpc, wo, bO, lg, lb):
        return (wa[None], bA, wi[None], bI, cw[None], wpa[None], wpc[None], wo[None], bO, lg, lb)

    sg = [g.reshape(s) for g, s in zip(small_g, shapes)]
    grads_out = order(g_big[0], sg[0], g_big[1], sg[1], sg[2], g_big[2], g_big[3], g_big[4], sg[3], sg[4], sg[5])
    outs = []
    for idx, small in enumerate((sd, sm, sv)):
        outs.append(order(big_out[0][idx], small[0], big_out[1][idx], small[1], small[2], big_out[2][idx],
                          big_out[3][idx], big_out[4][idx], small[3], small[4], small[5]))
    return (loss, grad_x2.reshape(bsz, seq, d), *grads_out, *outs[0], *outs[1], *outs[2])
```

```python
import functools

import jax
import jax.numpy as jnp
from jax import lax
from jax.experimental import pallas as pl
from jax.experimental.pallas import tpu as pltpu

F32 = jnp.float32
BF16 = jnp.bfloat16
MESH = pl.DeviceIdType.MESH

HEAD_DIM = 64
N_GROUPS = 3
DILATIONS = (1, 4, 16)
N_HEADS = 12
SUB = 128
Q_WIDTH = 768
Z_WIDTH = 256
ATT = 3 * Q_WIDTH + Z_WIDTH
SLAB = 128
PAIR_SLABS = 10
PAIR_COLS = PAIR_SLABS * SLAB
CONV_TILE = 256
SOFTMAX_ROWS = 32
BLOCKS_PER_TRIP = 8
ALIBI_MAX_EXP = 8.0
ALPHA = 2.0 ** 0.25
LN_EPS = 1e-5
ADAM_LR, ADAM_B1, ADAM_B2, ADAM_EPS, ADAM_WD, ADAM_STEP = 0.001, 0.9, 0.999, 1e-08, 0.01, 10
N_CHIPS = 4
N_DEV = 8
VMEM_LIMIT_V7X = 60 * 1024 * 1024
NEG = -1e30


def _params(*sem):
    return pltpu.CompilerParams(dimension_semantics=sem, vmem_limit_bytes=VMEM_LIMIT_V7X)


def _sigmoid(v):
    return 0.5 * jnp.tanh(0.5 * v) + 0.5


class _Layout:
    def __init__(self, d):
        self.d = d
        self.din = ATT + 6 * d
        c0 = 3072
        while c0 % (2 * d):
            c0 += 1024
        self.c0, self.g0, self.np = c0, c0 + 4 * d, c0 + 6 * d
        self.n_conv_tiles = d // CONV_TILE

    def attn_nat_slab(self, s):
        p, i = s // PAIR_SLABS, s % PAIR_SLABS
        return jnp.where(i < 9, (i // 3) * 6 + (i % 3) * 2 + p, 18 + p)

    def rest_nat_tile(self, t):
        n4 = 4 * self.n_conv_tiles
        conv = ATT // CONV_TILE + (t % 4) * self.n_conv_tiles + t // 4
        return jnp.where(t < n4, conv, ATT // CONV_TILE + t)

    def perm_vector(self, v):
        parts = []
        for s in range(2 * PAIR_SLABS):
            p, i = divmod(s, PAIR_SLABS)
            ns = (i // 3) * 6 + (i % 3) * 2 + p if i < 9 else 18 + p
            parts.append(v[:, ns * SLAB:(ns + 1) * SLAB])
        parts.append(jnp.zeros((1, self.c0 - ATT), v.dtype))
        for j in range(self.n_conv_tiles):
            for k in range(4):
                a = ATT + k * self.d + j * CONV_TILE
                parts.append(v[:, a:a + CONV_TILE])
        parts.append(v[:, ATT + 4 * self.d:])
        return jnp.concatenate(parts, axis=1)


def _place():
    return lax.axis_index("x"), lax.axis_index("y"), lax.axis_index("c")


def _other_chips(x, y):
    return [(1 - x, y), (x, 1 - y), (1 - x, 1 - y)]


def _shard_of(ref, col_sharded, chip, half=None):
    if col_sharded:
        cs = ref.shape[1] // N_CHIPS
        cols = pl.ds(pl.multiple_of(chip * cs, SLAB), cs)
        if half is None:
            return ref.at[:, cols]
        n = ref.shape[0] // 2
        return ref.at[pl.ds(half * n, n), cols]
    rs = ref.shape[0] // N_CHIPS
    if half is None:
        return ref.at[pl.ds(chip * rs, rs)]
    return ref.at[pl.ds(chip * rs + half * (rs // 2), rs // 2)]


GATHER_PIECES = 4


def _cast_into_full(shard, col_sharded, chip, name):
    rows, cols = shard.shape
    tr = _row_tile(rows, cols)
    nb = rows // tr

    def body(chip_ref, s_ref, o_ref):
        del chip_ref
        o_ref[...] = s_ref[...].astype(BF16)

    if col_sharded:
        full, out_spec = (rows, cols * N_CHIPS), pl.BlockSpec((tr, cols), lambda i, ch: (i, ch[0]))
    else:
        full, out_spec = (rows * N_CHIPS, cols), pl.BlockSpec((tr, cols), lambda i, ch: (ch[0] * nb + i, 0))
    return pl.pallas_call(
        body, name=name,
        grid_spec=pltpu.PrefetchScalarGridSpec(num_scalar_prefetch=1, grid=(nb,),
                                               in_specs=[pl.BlockSpec((tr, cols), lambda i, ch: (i, 0))], out_specs=out_spec),
        out_shape=jax.ShapeDtypeStruct(full, BF16), compiler_params=_params("parallel"))(chip, shard)


def _gather_weights(fulls, col_sharded, small):
    n = len(fulls)
    kp = GATHER_PIECES

    def piece(ref, cs, chip, half, k):
        if cs:
            width = ref.shape[1] // N_CHIPS
            rows = ref.shape[0] // 2 // kp
            return ref.at[pl.ds(half * (ref.shape[0] // 2) + k * rows, rows), pl.ds(pl.multiple_of(chip * width, SLAB), width)]
        rs = ref.shape[0] // N_CHIPS
        rows = rs // 2 // kp
        return ref.at[pl.ds(chip * rs + half * (rs // 2) + k * rows, rows)]

    def body(*refs):
        sm_in, outs, sm_out = refs[n], refs[n + 1:2 * n + 1], refs[2 * n + 1]
        send, recv, fsend, frecv, lsem, ssend, srecv = refs[2 * n + 2:]
        x, y, c = _place()
        mine = 2 * x + y
        sibling = (x, y, 1 - c)
        first = (x ^ (1 - c), y ^ c)
        second = (x ^ c, y ^ (1 - c))
        diagonal = (1 - x, 1 - y)
        sources = [first, second, diagonal]
        senders = [first, second, second]

        def copy(ref, sems, slot, to):
            return pltpu.make_async_remote_copy(src_ref=ref, dst_ref=ref, send_sem=sems[0].at[slot], recv_sem=sems[1].at[slot],
                                                device_id=to, device_id_type=MESH)

        local = pltpu.make_async_copy(sm_in, _shard_of(sm_out, True, mine), lsem)
        local.start()
        sends = []
        for k, (cx, cy) in enumerate(_other_chips(x, y)):
            cp = pltpu.make_async_remote_copy(src_ref=sm_in, dst_ref=_shard_of(sm_out, True, mine), send_sem=ssend.at[k],
                                              recv_sem=srecv.at[k], device_id=(cx, cy, c), device_id_type=MESH)
            cp.start()
            sends.append(cp)
        for k in range(kp):
            for w in range(n):
                own = piece(outs[w], col_sharded[w], mine, c, k)
                for slot, chip in enumerate((first, second)):
                    cp = copy(own, (send, recv), (w * 3 + slot) * kp + k, (*chip, c))
                    cp.start()
                    sends.append(cp)
        for slot in range(3):
            source = 2 * sources[slot][0] + sources[slot][1]
            for k in range(kp):
                for w in range(n):
                    landed = piece(outs[w], col_sharded[w], source, c, k)
                    copy(landed, (send, recv), (w * 3 + slot) * kp + k, (*senders[slot], c)).wait_recv()
                    if slot == 0:
                        cp = copy(landed, (send, recv), (w * 3 + 2) * kp + k, (*second, c))
                        cp.start()
                        sends.append(cp)
                    cp = copy(landed, (fsend, frecv), (w * 3 + slot) * kp + k, sibling)
                    cp.start()
                    sends.append(cp)
        for slot, chip in enumerate((second, first, diagonal)):
            for k in range(kp):
                for w in range(n):
                    passed = piece(outs[w], col_sharded[w], 2 * chip[0] + chip[1], 1 - c, k)
                    copy(passed, (fsend, frecv), (w * 3 + slot) * kp + k, sibling).wait_recv()
        for k, (cx, cy) in enumerate(_other_chips(x, y)):
            theirs = _shard_of(sm_out, True, 2 * cx + cy)
            pltpu.make_async_remote_copy(src_ref=theirs, dst_ref=theirs, send_sem=ssend.at[k], recv_sem=srecv.at[k],
                                         device_id=(cx, cy, c), device_id_type=MESH).wait_recv()
        for cp in sends:
            cp.wait_send()
        local.wait()

    any_spec = pl.BlockSpec(memory_space=pl.ANY)
    outs = pl.pallas_call(
        body, name="gather_weights",
        out_shape=[jax.ShapeDtypeStruct(f.shape, BF16) for f in fulls]
        + [jax.ShapeDtypeStruct((small.shape[0], small.shape[1] * N_CHIPS), small.dtype)],
        in_specs=[any_spec] * (n + 1), out_specs=[any_spec] * (n + 1), input_output_aliases={w: w for w in range(n)},
        scratch_shapes=[pltpu.SemaphoreType.DMA((n * 3 * kp,)), pltpu.SemaphoreType.DMA((n * 3 * kp,)),
                        pltpu.SemaphoreType.DMA((n * 3 * kp,)), pltpu.SemaphoreType.DMA((n * 3 * kp,)), pltpu.SemaphoreType.DMA,
                        pltpu.SemaphoreType.DMA((3,)), pltpu.SemaphoreType.DMA((3,))],
    )(*fulls, small)
    return outs[:n], outs[n]


HBM_SPEC = pl.BlockSpec(memory_space=pltpu.HBM)
SEM_SPEC = pl.BlockSpec(memory_space=pltpu.SEMAPHORE)
DATAFLOW = pltpu.SideEffectType.DATAFLOW_SIDE_EFFECTING


def _start_copies(name, arrays, sem_shape, after, copies):
    n = len(arrays)

    def body(*refs):
        for cp in copies(refs[:n], refs[n + 1], refs[n + 2]):
            cp.start()
        token = refs[2 * n + 3]
        token[...] = jnp.zeros_like(token)

    res = pl.pallas_call(
        body, name=name,
        out_shape=(pltpu.SemaphoreType.DMA(sem_shape), pltpu.SemaphoreType.DMA(sem_shape),
                   *[pltpu.HBM(a.shape, a.dtype) for a in arrays], jax.ShapeDtypeStruct((8, 128), F32)),
        in_specs=[HBM_SPEC] * n + [pl.BlockSpec(memory_space=pl.ANY)],
        out_specs=(SEM_SPEC, SEM_SPEC, *([HBM_SPEC] * n), pl.BlockSpec(memory_space=pltpu.VMEM)),
        input_output_aliases={i: 2 + i for i in range(n)},
        compiler_params=pltpu.CompilerParams(has_side_effects=DATAFLOW),
    )(*[pltpu.with_memory_space_constraint(a, pltpu.HBM) for a in arrays], after)
    return res[0], res[1], list(res[2:2 + n]), res[2 + n]


def _wait_copies(name, arrays, send, recv, after, copies):
    n = len(arrays)

    def body(*refs):
        for cp in copies(refs[:n], refs[n], refs[n + 1]):
            cp.wait_send()
            cp.wait_recv()

    return pl.pallas_call(
        body, name=name, out_shape=[pltpu.HBM(a.shape, a.dtype) for a in arrays],
        in_specs=[HBM_SPEC] * n + [SEM_SPEC, SEM_SPEC, pl.BlockSpec(memory_space=pl.ANY)], out_specs=[HBM_SPEC] * n,
        input_output_aliases={i: i for i in range(n)},
        compiler_params=pltpu.CompilerParams(has_side_effects=DATAFLOW),
    )(*arrays, send, recv, after)


def _direct_gather_copies(col_sharded):
    def copies(refs, send, recv):
        x, y, c = _place()
        mine = 2 * x + y
        out = []
        for w, ref in enumerate(refs):
            own_half = _shard_of(ref, col_sharded[w], mine, c)
            k = 0
            for cx, cy in _other_chips(x, y):
                for pc in (c, 1 - c):
                    out.append(pltpu.make_async_remote_copy(
                        src_ref=own_half, dst_ref=own_half, send_sem=send.at[6 * w + k], recv_sem=recv.at[6 * w + k],
                        device_id=(cx, cy, pc), device_id_type=MESH))
                    k += 1
        return out
    return copies


def _chip_scatter_copies(n, col_sharded):
    def piece(ref, cs, chip):
        if cs:
            w = ref.shape[2] // N_CHIPS
            return ref.at[:, :, pl.ds(pl.multiple_of(chip * w, SLAB), w)]
        return ref.at[pl.ds(chip, 1)]

    def copies(refs, send, recv):
        x, y, c = _place()
        out = []
        for k, (cx, cy) in enumerate(_other_chips(x, y)):
            for w in range(n):
                out.append(pltpu.make_async_remote_copy(
                    src_ref=piece(refs[w], col_sharded[w], 2 * cx + cy), dst_ref=refs[n + w].at[pl.ds(k, 1)],
                    send_sem=send.at[3 * w + k], recv_sem=recv.at[3 * w + k], device_id=(cx, cy, c), device_id_type=MESH))
        return out
    return copies


def _shard_views(gs, col_sharded):
    return [g.reshape(1, *g.shape) if cs else g.reshape(N_CHIPS, g.shape[0] // N_CHIPS, g.shape[1])
            for g, cs in zip(gs, col_sharded)]


DMA_CHUNK_BYTES = 1 << 20


def _chunk_rows(shape, itemsize):
    s, rows, cols = shape
    n = 1
    while s * (rows // n) * cols * itemsize > DMA_CHUNK_BYTES and (rows // n) % 32 == 0:
        n *= 2
    return rows // n


def _row_pieces(src, src_row0, dst, dst_row0, rows, send_sem, recv_sem, device):
    step = _chunk_rows((src.shape[0], rows, src.shape[2]), src.dtype.itemsize)
    return [pltpu.make_async_remote_copy(src_ref=src.at[:, pl.ds(src_row0 + r, step)], dst_ref=dst.at[:, pl.ds(dst_row0 + r, step)],
                                         send_sem=send_sem, recv_sem=recv_sem, device_id=device, device_id_type=MESH)
            for r in range(0, rows, step)]


def _pair_exchange_copies(n, whole):
    def copies(refs, send, recv):
        x, y, c = _place()
        sibling = (x, y, 1 - c)
        out = []
        for w in range(n):
            hr = refs[n + w].shape[1]
            if whole:
                out.append(pltpu.make_async_remote_copy(
                    src_ref=refs[w].at[:, pl.ds((1 - c) * hr, hr)], dst_ref=refs[n + w], send_sem=send.at[w], recv_sem=recv.at[w],
                    device_id=sibling, device_id_type=MESH))
            else:
                out += _row_pieces(refs[w], (1 - c) * hr, refs[n + w], 0, hr, send.at[w], recv.at[w], sibling)
        return out
    return copies


def _pair_join_copies(n, whole):
    def copies(refs, send, recv):
        x, y, c = _place()
        sibling = (x, y, 1 - c)
        out = []
        for w in range(n):
            hr = refs[w].shape[1] // 2
            if whole:
                out.append(pltpu.make_async_remote_copy(
                    src_ref=refs[w].at[:, pl.ds(c * hr, hr)], dst_ref=refs[w].at[:, pl.ds((1 - c) * hr, hr)], send_sem=send.at[w],
                    recv_sem=recv.at[w], device_id=sibling, device_id_type=MESH))
            else:
                out += _row_pieces(refs[w], c * hr, refs[w], c * hr, hr, send.at[w], recv.at[w], sibling)
        return out
    return copies


def _small_gather_copies(refs, send, recv):
    vec, land = refs
    x, y, c = _place()
    me = 4 * x + 2 * y + c
    return [pltpu.make_async_remote_copy(src_ref=vec, dst_ref=land.at[me], send_sem=send.at[k], recv_sem=recv.at[k],
                                         device_id=peer, device_id_type=MESH) for k, (peer, _) in enumerate(_all_devices(x, y, c))]


def _small_sum(vec, land, me, n_sum, d):
    rows = vec.shape[0]

    def body(me_ref, v_ref, l_ref, sum_ref, kept_ref):
        def slot(k):
            return jnp.where(me_ref[0] == k, v_ref[...], l_ref[k])

        total = slot(0)[0:n_sum, :]
        kept_ref[0] = slot(0)[n_sum:rows, :]
        for k in range(1, N_DEV):
            total = total + slot(k)[0:n_sum, :]
            kept_ref[k] = slot(k)[n_sum:rows, :]
        sum_ref[...] = total
        loss = 0.5 / d * jnp.sum(total[0:8, :])
        sum_ref[0:8, :] = jnp.full((8, 128), loss, F32)

    vm = pl.BlockSpec(memory_space=pltpu.VMEM)
    return pl.pallas_call(
        body, name="small_sum", in_specs=[pl.BlockSpec(memory_space=pltpu.SMEM), vm, vm], out_specs=[vm, vm],
        out_shape=[jax.ShapeDtypeStruct((n_sum, 128), F32), jax.ShapeDtypeStruct((N_DEV, rows - n_sum, 128), F32)],
        compiler_params=pltpu.CompilerParams(vmem_limit_bytes=VMEM_LIMIT_V7X))(me, vec, land)


def _pair_exchange_halves(views, name):
    n = len(views)
    half_shapes = [(v.shape[0], v.shape[1] // 2, v.shape[2]) for v in views]

    def body(*refs):
        ins, got = refs[:n], refs[n:2 * n]
        send, recv = refs[2 * n:]
        x, y, c = _place()
        sibling = (x, y, 1 - c)
        for w in range(n):
            hr = half_shapes[w][1]
            for cp in _row_pieces(ins[w], (1 - c) * hr, got[w], 0, hr, send.at[w], recv.at[w], sibling):
                cp.start()
        for w in range(n):
            hr = half_shapes[w][1]
            pltpu.make_async_remote_copy(src_ref=ins[w].at[:, pl.ds((1 - c) * hr, hr)], dst_ref=got[w], send_sem=send.at[w],
                                         recv_sem=recv.at[w], device_id=sibling, device_id_type=MESH).wait()

    any_spec = pl.BlockSpec(memory_space=pl.ANY)
    return pl.pallas_call(
        body, name=name,
        out_shape=[jax.ShapeDtypeStruct(s, v.dtype) for s, v in zip(half_shapes, views)],
        in_specs=[any_spec] * n, out_specs=[any_spec] * n,
        scratch_shapes=[pltpu.SemaphoreType.DMA((n,)), pltpu.SemaphoreType.DMA((n,))],
    )(*views)


def _pair_sum(view, got, core, name):
    s, r, cols = view.shape
    hr = r // 2
    tr = _row_tile(hr, cols)
    nb = hr // tr

    def body(core_ref, a_ref, b_ref, o_ref):
        del core_ref
        o_ref[...] = (a_ref[...].astype(F32) + b_ref[...].astype(F32)).astype(BF16)

    same = pl.BlockSpec((None, tr, cols), lambda j, i, core_ref: (j, i, 0))
    return pl.pallas_call(
        body, name=name,
        grid_spec=pltpu.PrefetchScalarGridSpec(
            num_scalar_prefetch=1, grid=(s, nb),
            in_specs=[pl.BlockSpec((None, tr, cols), lambda j, i, core_ref: (j, core_ref[0] * nb + i, 0)), same],
            out_specs=same),
        out_shape=jax.ShapeDtypeStruct((s, hr, cols), BF16), compiler_params=_params("parallel", "parallel"))(core, view, got)


def _piece_cols(part, col_sharded):
    return part.shape[2] // N_CHIPS if col_sharded else part.shape[2]


def _chip_sum(part, got, col_sharded, place, name):
    _, hr, _ = part.shape
    cols = _piece_cols(part, col_sharded)
    tr = _row_tile(hr, cols)
    nb = hr // tr

    def body(place_ref, own_ref, g0_ref, g1_ref, g2_ref, o_ref):
        del place_ref
        acc = own_ref[...].astype(F32) + g0_ref[...].astype(F32)
        o_ref[...] = acc + g1_ref[...].astype(F32) + g2_ref[...].astype(F32)

    if col_sharded:
        own = pl.BlockSpec((None, tr, cols), lambda i, pr: (0, i, pr[0]))
    else:
        own = pl.BlockSpec((None, tr, cols), lambda i, pr: (pr[0], i, 0))
    others = [pl.BlockSpec((None, tr, cols), lambda i, pr, k=k: (k, i, 0)) for k in range(3)]
    return pl.pallas_call(
        body, name=name,
        grid_spec=pltpu.PrefetchScalarGridSpec(
            num_scalar_prefetch=1, grid=(nb,), in_specs=[own] + others,
            out_specs=pl.BlockSpec((tr, cols), lambda i, pr: (pr[1] * nb + i, 0))),
        out_shape=jax.ShapeDtypeStruct((2 * hr, cols), F32), compiler_params=_params("parallel"))(place, part, got, got, got)


def _row_tile(rows, cols, itemsize=4, budget=2 << 20):
    t = rows
    while t * cols * itemsize > budget and t % 16 == 0:
        t //= 2
    return t


def _adamw(w, g, m, v, name, after):
    rows, cols = w.shape
    tr = _row_tile(rows, cols, budget=1 << 20)
    extra = [] if after is None else [after]

    def body(w_ref, g_ref, m_ref, v_ref, *rest):
        d_ref, nm_ref, nv_ref = rest[len(extra):]
        g_ = g_ref[...]
        nm = ADAM_B1 * m_ref[...] + (1.0 - ADAM_B1) * g_
        nv = ADAM_B2 * v_ref[...] + (1.0 - ADAM_B2) * (g_ * g_)
        m_hat = nm / (1.0 - ADAM_B1 ** ADAM_STEP)
        v_hat = nv / (1.0 - ADAM_B2 ** ADAM_STEP)
        d_ref[...] = -ADAM_LR * (m_hat / (jnp.sqrt(v_hat) + ADAM_EPS) + ADAM_WD * w_ref[...])
        nm_ref[...] = nm
        nv_ref[...] = nv

    spec = pl.BlockSpec((tr, cols), lambda i: (i, 0))
    shp = jax.ShapeDtypeStruct((rows, cols), F32)
    return pl.pallas_call(body, name=name, grid=(rows // tr,),
                          in_specs=[spec] * 4 + [pl.BlockSpec(memory_space=pl.ANY)] * len(extra), out_specs=[spec] * 3,
                          out_shape=[shp] * 3, compiler_params=_params("parallel"))(w, g, m, v, *extra)


def _all_devices(x, y, c):
    out = []
    for k in range(1, N_DEV):
        peer = (x ^ ((k >> 2) & 1), y ^ ((k >> 1) & 1), c ^ (k & 1))
        out.append((peer, 4 * peer[0] + 2 * peer[1] + peer[2]))
    return out


def _ada_exchange(c, w_shard, b_ada):
    bsz, d = c.shape
    cs = w_shard.shape[1]

    def body(c_ref, w_ref, b_ref, mod_ref, act_ref, c_all, part, pieces, csend, crecv, psend, precv):
        x, y, core = _place()
        me = 4 * x + 2 * y + core
        chip = 2 * x + y
        c_all[me] = c_ref[...]
        peers = _all_devices(x, y, core)
        copies = []
        for k, (peer, _) in enumerate(peers):
            cp = pltpu.make_async_remote_copy(src_ref=c_ref, dst_ref=c_all.at[me], send_sem=csend.at[k], recv_sem=crecv.at[k],
                                              device_id=peer, device_id_type=MESH)
            cp.start()
            copies.append(cp)
        for k, (_, src) in enumerate(peers):
            pltpu.make_async_remote_copy(src_ref=c_ref, dst_ref=c_all.at[src], send_sem=csend.at[k], recv_sem=crecv.at[k],
                                         device_id=(x, y, core), device_id_type=MESH).wait_recv()
        rows = jnp.concatenate([c_all[i] for i in range(N_DEV)], axis=0)
        act = rows * _sigmoid(rows)
        act_ref[...] = act
        prod = jnp.dot(act.astype(BF16), w_ref[...].astype(BF16), preferred_element_type=F32)
        for i in range(N_DEV):
            part[i] = prod[i * bsz:(i + 1) * bsz, :]
        pieces[chip] = part[me]
        chips = _other_chips(x, y)
        for k, (cx, cy) in enumerate(chips):
            cp = pltpu.make_async_remote_copy(src_ref=part.at[4 * cx + 2 * cy + core], dst_ref=pieces.at[chip],
                                              send_sem=psend.at[k], recv_sem=precv.at[k], device_id=(cx, cy, core),
                                              device_id_type=MESH)
            cp.start()
            copies.append(cp)
        for k, (cx, cy) in enumerate(chips):
            pltpu.make_async_remote_copy(src_ref=part.at[me], dst_ref=pieces.at[2 * cx + cy], send_sem=psend.at[k],
                                         recv_sem=precv.at[k], device_id=(cx, cy, core), device_id_type=MESH).wait_recv()
        for cp in copies:
            cp.wait_send()
        mod_ref[...] = jnp.concatenate([pieces[j] for j in range(N_CHIPS)], axis=1) + b_ref[...]

    vm = pl.BlockSpec(memory_space=pltpu.VMEM)
    return pl.pallas_call(
        body, name="ada_exchange", in_specs=[vm] * 3, out_specs=[vm] * 2,
        out_shape=[jax.ShapeDtypeStruct((bsz, 3 * d), F32), jax.ShapeDtypeStruct((N_DEV * bsz, d), F32)],
        scratch_shapes=[pltpu.VMEM((N_DEV, bsz, d), F32), pltpu.VMEM((N_DEV, bsz, cs), F32), pltpu.VMEM((N_CHIPS, bsz, cs), F32),
                        pltpu.SemaphoreType.DMA((N_DEV - 1,)), pltpu.SemaphoreType.DMA((N_DEV - 1,)),
                        pltpu.SemaphoreType.DMA((3,)), pltpu.SemaphoreType.DMA((3,))],
        compiler_params=pltpu.CompilerParams(vmem_limit_bytes=VMEM_LIMIT_V7X))(c, w_shard, b_ada)


def _grad_w_ada_cols(act_t, d_cols):
    d, n = act_t.shape
    cs = d_cols.shape[1]

    def body(a_ref, g_ref, o_ref):
        a, g = a_ref[...], g_ref[...]
        acc = a[:, 0:1] * g[0:1, :]
        for b in range(1, n):
            acc = acc + a[:, b:b + 1] * g[b:b + 1, :]
        o_ref[...] = acc

    vm = pl.BlockSpec(memory_space=pltpu.VMEM)
    return pl.pallas_call(body, name="grad_w_ada", in_specs=[vm] * 2, out_specs=vm,
                          out_shape=jax.ShapeDtypeStruct((d, cs), F32),
                          compiler_params=pltpu.CompilerParams(vmem_limit_bytes=VMEM_LIMIT_V7X))(act_t, d_cols)


def _permute_w_in(w_nat, lay):
    d = lay.d
    group = 4

    def call(name, width, n_pieces, nat_piece, out_block0, prev):
        def body(*refs):
            refs[-1][...] = jnp.concatenate([r[...] for r in refs[:group]], axis=1)

        in_specs = [pl.BlockSpec((d, width), lambda s, m=m: (0, nat_piece(group * s + m))) for m in range(group)]
        args = [w_nat] * group
        aliases = {}
        if prev is not None:
            in_specs.append(pl.BlockSpec(memory_space=pl.ANY))
            args.append(prev)
            aliases = {group: 0}
        return pl.pallas_call(
            body, name=name, grid=(n_pieces // group,), in_specs=in_specs,
            out_specs=pl.BlockSpec((d, group * width), lambda s: (0, out_block0 + s)),
            out_shape=jax.ShapeDtypeStruct((d, lay.np), BF16), input_output_aliases=aliases,
            compiler_params=_params("arbitrary"))(*args)

    w_all = call("permute_w_attn", SLAB, 2 * PAIR_SLABS, lay.attn_nat_slab, 0, None)
    n_rest = 6 * d // CONV_TILE
    if n_rest % group:
        group = 2
    return call("permute_w_rest", CONV_TILE, n_rest, lay.rest_nat_tile, lay.c0 // (group * CONV_TILE), w_all)


def _project(x2, mod3, w_all, b_all, seq, col0, ncols, tn, out_dtype, want_ht, name):
    t, d = x2.shape
    tm = min(1024, seq)
    per_seq = seq // tm
    j0 = col0 // tn

    def body(x_ref, mod_ref, w_ref, b_ref, o_ref, *rest):
        h_ref = rest[-1]

        @pl.when(pl.program_id(1) == 0)
        def _():
            h = x_ref[...] * (1.0 + mod_ref[:, d:2 * d]) + mod_ref[:, 0:d]
            h_ref[...] = h.astype(BF16)
            if want_ht:
                rest[0][...] = h.T.astype(BF16)

        o_ref[...] = (jnp.dot(h_ref[...], w_ref[...], preferred_element_type=F32) + b_ref[...]).astype(out_dtype)

    out_shape = [jax.ShapeDtypeStruct((t, ncols), out_dtype)]
    out_specs = [pl.BlockSpec((tm, tn), lambda i, j: (i, j))]
    if want_ht:
        out_shape.append(jax.ShapeDtypeStruct((d, t), BF16))
        out_specs.append(pl.BlockSpec((d, tm), lambda i, j: (0, i)))
    return pl.pallas_call(
        body, name=name, grid=(t // tm, ncols // tn),
        in_specs=[pl.BlockSpec((tm, d), lambda i, j: (i, 0)),
                  pl.BlockSpec((None, 1, 3 * d), lambda i, j: (i // per_seq, 0, 0)),
                  pl.BlockSpec((d, tn), lambda i, j: (0, j0 + j)),
                  pl.BlockSpec((1, tn), lambda i, j: (0, j0 + j))],
        out_specs=out_specs, out_shape=out_shape,
        scratch_shapes=[pltpu.VMEM((tm, d), BF16)],
        compiler_params=_params("arbitrary", "arbitrary"))(x2, mod3, w_all, b_all)


def _slope(g, p, hh):
    head = 4 * g + 2 * p + hh
    return 2.0 ** (-ALIBI_MAX_EXP * (head + 1.0) / N_HEADS)


def _ld_rows(ref, start, n, stride):
    if stride == 1:
        return ref[pl.ds(start, n), :]
    return ref[pl.ds(start, n, stride=stride), :]


def _st_rows(ref, start, n, stride, val):
    if stride == 1:
        ref[pl.ds(start, n), :] = val
    else:
        ref[pl.ds(start, n, stride=stride), :] = val


def _sub_blocks(g, seq):
    return seq // DILATIONS[g] // SUB


def _key_rows(g, seq):
    return SUB if _sub_blocks(g, seq) == 1 else 2 * SUB


def _fill_bias(bias_ref, p, seq):
    for g in range(N_GROUPS):
        nk = _key_rows(g, seq)
        diff = lax.broadcasted_iota(jnp.int32, (SUB, nk), 0) - lax.broadcasted_iota(jnp.int32, (SUB, nk), 1)
        for i, off in enumerate((0, SUB)):
            if i == 1 and nk == SUB:
                continue
            delta = diff + off
            ok = (delta >= 0) & (delta <= SUB)
            dist = (delta * DILATIONS[g]).astype(F32)
            for hh in range(2):
                slope = jnp.where(p == 0, _slope(g, 0, hh), _slope(g, 1, hh))
                bias_ref[g, i, hh, :, 0:nk] = jnp.where(ok, -slope * dist, NEG)


def _to_sub_major(pa_ref, col, sub_ref, stage, dil, seq):
    cols = slice(col * SLAB, (col + 1) * SLAB)
    if dil == 1:
        sub_ref[...] = pa_ref[:, cols]
        return
    n = seq // dil
    stage[...] = pa_ref[:, cols].astype(F32)
    for r in range(dil):
        sub_ref[pl.ds(r * n, n), :] = stage[pl.ds(r, n, stride=dil), :].astype(BF16)


def _block_rows(it, g, seq):
    dil, nb = DILATIONS[g], _sub_blocks(g, seq)
    row0 = pl.multiple_of(it * SUB, SUB)
    if nb == 1:
        return row0, row0, 0, it
    blk = it % nb
    first = blk == 0
    krow0 = pl.multiple_of(row0 - jnp.where(first, 0, SUB), SUB)
    nat = row0 if dil == 1 else it // nb + dil * SUB * blk
    return row0, krow0, jnp.where(first, 0, 1), nat


def _nt(a, b):
    return lax.dot_general(a, b, (((1,), (1,)), ((), ())), preferred_element_type=F32)


def _tn(a, b):
    return lax.dot_general(a, b, (((0,), (0,)), ((), ())), preferred_element_type=F32)


def _head_sums(t):
    rows = t.shape[0]
    lo = jnp.broadcast_to(jnp.sum(t[:, :HEAD_DIM], axis=-1, keepdims=True), (rows, HEAD_DIM))
    hi = jnp.broadcast_to(jnp.sum(t[:, HEAD_DIM:], axis=-1, keepdims=True), (rows, HEAD_DIM))
    return jnp.concatenate([lo, hi], axis=-1)


def _attn_fwd(pa, bsz, seq):
    t = pa.shape[0]
    n_blocks = seq // SUB
    chunk = 256

    def body(pa_ref, o_ref, lse_ref, a_ref, sub, stage, bias_ref, s_buf, p_buf, l_buf):
        p = pl.program_id(1)
        _fill_bias(bias_ref, p, seq)
        head0 = lax.broadcasted_iota(jnp.int32, (SUB, SLAB), 1) < HEAD_DIM
        for g in range(N_GROUPS):
            dil = DILATIONS[g]
            for w in range(3):
                _to_sub_major(pa_ref, 3 * w + g, sub.at[w], stage, dil, seq)
            nk = _key_rows(g, seq)

            def trip(i, carry, g=g, dil=dil, nk=nk):
                places = [_block_rows(BLOCKS_PER_TRIP * i + j, g, seq) for j in range(BLOCKS_PER_TRIP)]
                for j, (row0, krow0, _, _) in enumerate(places):
                    q = sub[0, pl.ds(row0, SUB), :]
                    zero = jnp.zeros_like(q)
                    q2 = jnp.concatenate([jnp.where(head0, q, zero), jnp.where(head0, zero, q)], axis=0) * (HEAD_DIM ** -0.5)
                    s_buf[j, :, 0:nk] = _nt(q2, sub[1, pl.ds(krow0, nk), :])
                for j, (_, _, bi, _) in enumerate(places):
                    for c in range(0, 2 * SUB, SOFTMAX_ROWS):
                        hh, r = divmod(c, SUB)
                        s = s_buf[j, c:c + SOFTMAX_ROWS, 0:nk] + bias_ref[g, bi, hh, r:r + SOFTMAX_ROWS, 0:nk]
                        m = jnp.max(s, axis=-1, keepdims=True)
                        e = jnp.exp(s - m)
                        den = jnp.sum(e, axis=-1, keepdims=True)
                        p_buf[j, c:c + SOFTMAX_ROWS, 0:nk] = (e * (1.0 / den)).astype(BF16)
                        l_buf[j, c:c + SOFTMAX_ROWS, :] = jnp.broadcast_to(m + jnp.log(den), (SOFTMAX_ROWS, SLAB))
                for j, (_, krow0, _, nat) in enumerate(places):
                    o2 = jnp.dot(p_buf[j, :, 0:nk], sub[2, pl.ds(krow0, nk), :], preferred_element_type=F32)
                    _st_rows(o_ref.at[g], nat, SUB, dil, jnp.where(head0, o2[0:SUB], o2[SUB:2 * SUB]))
                    _st_rows(lse_ref.at[g], nat, SUB, dil, jnp.where(head0, l_buf[j, 0:SUB, :], l_buf[j, SUB:2 * SUB, :]))
                return carry

            lax.fori_loop(0, n_blocks // BLOCKS_PER_TRIP, trip, 0)

        def mix(i, carry):
            rows = pl.ds(pl.multiple_of(i * chunk, chunk), chunk)
            l0, l1, l2 = lse_ref[0, rows, :], lse_ref[1, rows, :], lse_ref[2, rows, :]
            m = jnp.maximum(jnp.maximum(l0, l1), l2)
            e0, e1, e2 = jnp.exp(l0 - m), jnp.exp(l1 - m), jnp.exp(l2 - m)
            tot = e0 + e1 + e2
            o = (e0 / tot) * o_ref[0, rows, :] + (e1 / tot) * o_ref[1, rows, :] + (e2 / tot) * o_ref[2, rows, :]
            z = pa_ref[rows, 9 * SLAB:10 * SLAB].astype(F32)
            a_ref[rows, :] = (o * (z * _sigmoid(z))).astype(BF16)
            return carry

        lax.fori_loop(0, seq // chunk, mix, 0)

    big = jax.ShapeDtypeStruct((N_GROUPS, t, 2 * SLAB), F32)
    return pl.pallas_call(
        body, name="attn_fwd", grid=(bsz, 2),
        in_specs=[pl.BlockSpec((seq, PAIR_COLS), lambda b, p: (b, p))],
        out_specs=[pl.BlockSpec((N_GROUPS, seq, SLAB), lambda b, p: (0, b, p)),
                   pl.BlockSpec((N_GROUPS, seq, SLAB), lambda b, p: (0, b, p)),
                   pl.BlockSpec((seq, SLAB), lambda b, p: (b, p))],
        out_shape=[big, big, jax.ShapeDtypeStruct((t, 2 * SLAB), BF16)],
        scratch_shapes=[pltpu.VMEM((3, seq, SLAB), BF16), pltpu.VMEM((seq, SLAB), F32),
                        pltpu.VMEM((N_GROUPS, 2, 2, SUB, 2 * SUB), F32), pltpu.VMEM((BLOCKS_PER_TRIP, 2 * SUB, 2 * SUB), F32),
                        pltpu.VMEM((BLOCKS_PER_TRIP, 2 * SUB, 2 * SUB), BF16), pltpu.VMEM((BLOCKS_PER_TRIP, 2 * SUB, SLAB), F32)],
        compiler_params=_params("arbitrary", "arbitrary"))(pa)


def _shift_down(v, k, rows):
    return jnp.where(rows >= k, pltpu.roll(v, k, 0), 0.0)


def _shift_up(v, k, rows):
    n = v.shape[0]
    return jnp.where(rows < n - k, pltpu.roll(v, n - k, 0), 0.0)


def _conv_fwd(pr, conv_w, bsz, seq, d):
    t = pr.shape[0]
    ct = CONV_TILE

    def body(p_ref, cw_ref, o_ref):
        u = p_ref[:, 2 * ct:3 * ct].astype(F32) * p_ref[:, 0:ct].astype(F32)
        cw = cw_ref[...]
        rows = lax.broadcasted_iota(jnp.int32, u.shape, 0)
        conv = cw[0:1, :] * _shift_down(u, 2, rows)
        conv = conv + cw[1:2, :] * _shift_down(u, 1, rows)
        conv = conv + cw[2:3, :] * u
        z = p_ref[:, 3 * ct:4 * ct].astype(F32)
        o_ref[...] = (p_ref[:, ct:2 * ct].astype(F32) * conv * (z * _sigmoid(z))).astype(BF16)

    return pl.pallas_call(
        body, name="conv_fwd", grid=(bsz, d // ct),
        in_specs=[pl.BlockSpec((seq, 4 * ct), lambda b, j: (b, j)), pl.BlockSpec((3, ct), lambda b, j: (0, j))],
        out_specs=pl.BlockSpec((seq, ct), lambda b, j: (b, j)),
        out_shape=jax.ShapeDtypeStruct((t, d), BF16), compiler_params=_params("parallel", "parallel"))(pr, conv_w)


def _tail(a_in, b_in, pr, x2, target2, mod3, w_pa, w_pc, w_out, b_out, ln_g, ln_b, seq, lay):
    t, d = x2.shape
    tm = 512
    per_seq = seq // tm
    n_steps = t // tm
    gate_blk = 4 * d // d

    def nt(a, b):
        return lax.dot_general(a, b, (((1,), (1,)), ((), ())), preferred_element_type=F32)

    def tn(a, b):
        return lax.dot_general(a, b, (((0,), (0,)), ((), ())), preferred_element_type=F32)

    def body(a_ref, b_ref, ga_ref, gb_ref, x_ref, tg_ref, mod_ref, wpa_ref, wpc_ref, wo_ref, bo_ref, lg_ref, lb_ref,
             dpg_ref, da_ref, db_ref, gx_ref, dgate_ref, small_ref, gwpa_hbm, gwpc_hbm, gwo_hbm,
             acc_pa, acc_pc, acc_o, sem):
        i = pl.program_id(0)

        @pl.when(i == 0)
        def _():
            acc_pa[...] = jnp.zeros_like(acc_pa)
            acc_pc[...] = jnp.zeros_like(acc_pc)
            acc_o[...] = jnp.zeros_like(acc_o)
            small_ref[...] = jnp.zeros_like(small_ref)

        @pl.when(i % per_seq == 0)
        def _():
            dgate_ref[...] = jnp.zeros_like(dgate_ref)

        halves = [slice(k * (tm // 2), (k + 1) * (tm // 2)) for k in range(2)]
        gate = mod_ref[:, 2 * d:3 * d]
        a_bf = [a_ref[rs, :] for rs in halves]
        b_bf = [b_ref[rs, :] for rs in halves]
        y_attn = [jnp.dot(a, wpa_ref[...], preferred_element_type=F32) for a in a_bf]
        y_conv = [jnp.dot(b, wpc_ref[...], preferred_element_type=F32) for b in b_bf]
        sa = [_sigmoid(ga_ref[rs, :].astype(F32)) for rs in halves]
        sb = [_sigmoid(gb_ref[rs, :].astype(F32)) for rs in halves]
        merged = [(sa[k] * y_attn[k] + sb[k] * y_conv[k]).astype(BF16) for k in range(2)]
        mo = [jnp.dot(m, wo_ref[...], preferred_element_type=F32) + bo_ref[...] for m in merged]
        d_mo_bf = []
        for k, rs in enumerate(halves):
            r = ALPHA * x_ref[rs, :] + gate * mo[k]
            mu = jnp.mean(r, axis=-1, keepdims=True)
            cen = r - mu
            var = jnp.mean(cen * cen, axis=-1, keepdims=True)
            rstd = lax.rsqrt(var + LN_EPS)
            xhat = cen * rstd
            err = xhat * lg_ref[...] + lb_ref[...] - tg_ref[rs, :]
            dy = err * (1.0 / d)
            dxhat = dy * lg_ref[...]
            dr = rstd * (dxhat - jnp.mean(dxhat, axis=-1, keepdims=True)
                         - xhat * jnp.mean(dxhat * xhat, axis=-1, keepdims=True))
            gx_ref[rs, :] = ALPHA * dr
            dgate_ref[...] += jnp.sum(dr * mo[k], axis=0, keepdims=True)
            d_mo = dr * gate
            small_ref[0:1, :] += jnp.sum(d_mo, axis=0, keepdims=True)
            small_ref[1:2, :] += jnp.sum(dy * xhat, axis=0, keepdims=True)
            small_ref[2:3, :] += jnp.sum(dy, axis=0, keepdims=True)
            small_ref[3:4, :] += jnp.sum(err * err, axis=0, keepdims=True)
            d_mo_bf.append(d_mo.astype(BF16))
        for k in range(2):
            acc_o[...] += tn(merged[k], d_mo_bf[k])
        dmerged = [nt(g, wo_ref[...]) for g in d_mo_bf]
        dy_attn, dy_conv = [], []
        for k, rs in enumerate(halves):
            dy_attn.append((dmerged[k] * sa[k]).astype(BF16))
            dy_conv.append((dmerged[k] * sb[k]).astype(BF16))
            dpg_ref[rs, 0:d] = (dmerged[k] * y_attn[k] * sa[k] * (1.0 - sa[k])).astype(BF16)
            dpg_ref[rs, d:2 * d] = (dmerged[k] * y_conv[k] * sb[k] * (1.0 - sb[k])).astype(BF16)
        for k, rs in enumerate(halves):
            acc_pa[...] += tn(a_bf[k], dy_attn[k])
            acc_pc[...] += tn(b_bf[k], dy_conv[k])
            da_ref[rs, :] = nt(dy_attn[k], wpa_ref[...])
            db_ref[rs, :] = nt(dy_conv[k], wpc_ref[...])

        @pl.when(i == n_steps - 1)
        def _():
            copies = [pltpu.make_async_copy(acc_pa, gwpa_hbm, sem.at[0]), pltpu.make_async_copy(acc_pc, gwpc_hbm, sem.at[1]),
                      pltpu.make_async_copy(acc_o, gwo_hbm, sem.at[2])]
            for cp in copies:
                cp.start()
            for cp in copies:
                cp.wait()

    row = lambda w: pl.BlockSpec((tm, w), lambda i: (i, 0))
    const = lambda shp: pl.BlockSpec(shp, lambda i: (0,) * len(shp), pipeline_mode=pl.Buffered(1))
    any_spec = pl.BlockSpec(memory_space=pl.ANY)
    return pl.pallas_call(
        body, name="tail", grid=(n_steps,),
        in_specs=[row(Z_WIDTH), row(d),
                  pl.BlockSpec((tm, d), lambda i: (i, gate_blk)), pl.BlockSpec((tm, d), lambda i: (i, gate_blk + 1)),
                  row(d), row(d), pl.BlockSpec((None, 1, 3 * d), lambda i: (i // per_seq, 0, 0)),
                  const((Z_WIDTH, d)), const((d, d)), const((d, d)), const((1, d)), const((1, d)), const((1, d))],
        out_specs=[pl.BlockSpec((tm, 2 * d), lambda i: (i, lay.g0 // (2 * d))), row(Z_WIDTH), row(d), row(d),
                   pl.BlockSpec((None, 1, d), lambda i: (i // per_seq, 0, 0)), pl.BlockSpec((8, d), lambda i: (0, 0)),
                   any_spec, any_spec, any_spec],
        out_shape=[jax.ShapeDtypeStruct((t, lay.np), BF16), jax.ShapeDtypeStruct((t, Z_WIDTH), F32),
                   jax.ShapeDtypeStruct((t, d), F32), jax.ShapeDtypeStruct((t, d), F32),
                   jax.ShapeDtypeStruct((t // seq, 1, d), F32), jax.ShapeDtypeStruct((8, d), F32),
                   jax.ShapeDtypeStruct((Z_WIDTH, d), F32), jax.ShapeDtypeStruct((d, d), F32),
                   jax.ShapeDtypeStruct((d, d), F32)],
        scratch_shapes=[pltpu.VMEM((Z_WIDTH, d), F32), pltpu.VMEM((d, d), F32), pltpu.VMEM((d, d), F32),
                        pltpu.SemaphoreType.DMA((3,))],
        compiler_params=_params("arbitrary"),
    )(a_in, b_in, pr, pr, x2, target2, mod3, w_pa, w_pc, w_out, b_out, ln_g, ln_b)


def _conv_bwd(dproj, db, pr, conv_w, bsz, seq, lay):
    d = lay.d
    ct = CONV_TILE
    base = lay.c0 // (4 * ct)

    def body(dp_in, db_ref, p_ref, cw_ref, dp_ref, gcw_ref):
        del dp_in
        u_x, g_b, g_c, z = [p_ref[:, k * ct:(k + 1) * ct].astype(F32) for k in range(4)]
        cw = cw_ref[...]
        u = g_c * u_x
        rows = lax.broadcasted_iota(jnp.int32, u.shape, 0)
        u1, u2 = _shift_down(u, 1, rows), _shift_down(u, 2, rows)
        conv = cw[0:1, :] * u2 + cw[1:2, :] * u1 + cw[2:3, :] * u
        sig = _sigmoid(z)
        sl = z * sig
        dbv = db_ref[...]
        gbc = g_b * conv
        dp_ref[:, ct:2 * ct] = (dbv * sl * conv).astype(BF16)
        dp_ref[:, 3 * ct:4 * ct] = (dbv * gbc * (sig * (1.0 + z * (1.0 - sig)))).astype(BF16)
        dconv = dbv * sl * g_b

        @pl.when(pl.program_id(1) == 0)
        def _():
            gcw_ref[...] = jnp.zeros_like(gcw_ref)

        gcw_ref[0:1, :] += jnp.sum(dconv * u2, axis=0, keepdims=True)
        gcw_ref[1:2, :] += jnp.sum(dconv * u1, axis=0, keepdims=True)
        gcw_ref[2:3, :] += jnp.sum(dconv * u, axis=0, keepdims=True)
        du = cw[2:3, :] * dconv + cw[1:2, :] * _shift_up(dconv, 1, rows) + cw[0:1, :] * _shift_up(dconv, 2, rows)
        dp_ref[:, 0:ct] = (du * g_c).astype(BF16)
        dp_ref[:, 2 * ct:3 * ct] = (du * u_x).astype(BF16)

    return pl.pallas_call(
        body, name="conv_bwd", grid=(d // ct, bsz),
        in_specs=[pl.BlockSpec(memory_space=pl.ANY), pl.BlockSpec((seq, ct), lambda j, b: (b, j)),
                  pl.BlockSpec((seq, 4 * ct), lambda j, b: (b, j)), pl.BlockSpec((3, ct), lambda j, b: (0, j))],
        out_specs=[pl.BlockSpec((seq, 4 * ct), lambda j, b: (b, base + j)), pl.BlockSpec((8, ct), lambda j, b: (0, j))],
        out_shape=[jax.ShapeDtypeStruct(dproj.shape, BF16), jax.ShapeDtypeStruct((8, d), F32)],
        input_output_aliases={0: 0}, compiler_params=_params("arbitrary", "arbitrary"))(dproj, db, pr, conv_w)


def _attn_bwd(dproj, pa, o_all, lse_all, da, bsz, seq, after):
    n_blocks = seq // SUB
    chunk = 256

    def body(dp_in, pa_ref, o_ref, lse_ref, da_ref, after_ref, dp_ref, sub, stage, dsub, dog, cvec, bias_ref,
             s_buf, dp_buf, ds_buf, pb_buf, q2_buf, do2_buf, l_buf, c_buf):
        del dp_in, after_ref
        p = pl.program_id(1)
        _fill_bias(bias_ref, p, seq)
        head0 = lax.broadcasted_iota(jnp.int32, (SUB, SLAB), 1) < HEAD_DIM

        def mix_bwd(i, carry):
            rows = pl.ds(pl.multiple_of(i * chunk, chunk), chunk)
            ls = [lse_ref[g, rows, :] for g in range(N_GROUPS)]
            os_ = [o_ref[g, rows, :] for g in range(N_GROUPS)]
            m = jnp.maximum(jnp.maximum(ls[0], ls[1]), ls[2])
            es = [jnp.exp(l - m) for l in ls]
            tot = es[0] + es[1] + es[2]
            ws = [e / tot for e in es]
            o = ws[0] * os_[0] + ws[1] * os_[1] + ws[2] * os_[2]
            z = pa_ref[rows, 9 * SLAB:10 * SLAB].astype(F32)
            sig = _sigmoid(z)
            dav = da_ref[rows, :]
            do = dav * (z * sig)
            dp_ref[rows, 9 * SLAB:10 * SLAB] = (dav * o * (sig * (1.0 + z * (1.0 - sig)))).astype(BF16)
            wsum = _head_sums(do * o)
            for g in range(N_GROUPS):
                dog[g, rows, :] = ws[g] * do
                cvec[g, rows, :] = -(ws[g] * wsum)
            return carry

        lax.fori_loop(0, seq // chunk, mix_bwd, 0)

        for g in range(N_GROUPS):
            dil = DILATIONS[g]
            for w in range(3):
                _to_sub_major(pa_ref, 3 * w + g, sub.at[w], stage, dil, seq)
            dsub[1] = jnp.zeros((seq, SLAB), F32)
            dsub[2] = jnp.zeros((seq, SLAB), F32)
            nk = _key_rows(g, seq)

            def trip(i, carry, g=g, dil=dil, nk=nk):
                places = [_block_rows(BLOCKS_PER_TRIP * i + j, g, seq) for j in range(BLOCKS_PER_TRIP)]
                for j, (row0, krow0, _, nat) in enumerate(places):
                    q = sub[0, pl.ds(row0, SUB), :]
                    do = _ld_rows(dog.at[g], nat, SUB, dil).astype(BF16)
                    zero = jnp.zeros_like(q)
                    q2 = jnp.concatenate([jnp.where(head0, q, zero), jnp.where(head0, zero, q)], axis=0)
                    do2 = jnp.concatenate([jnp.where(head0, do, zero), jnp.where(head0, zero, do)], axis=0)
                    q2_buf[j] = q2
                    do2_buf[j] = do2
                    s_buf[j, :, 0:nk] = _nt(q2 * (HEAD_DIM ** -0.5), sub[1, pl.ds(krow0, nk), :])
                    dp_buf[j, :, 0:nk] = _nt(do2, sub[2, pl.ds(krow0, nk), :])
                    l_buf[j] = _ld_rows(lse_ref.at[g], nat, SUB, dil)
                    c_buf[j] = _ld_rows(cvec.at[g], nat, SUB, dil)
                for j, (_, _, bi, _) in enumerate(places):
                    for c in range(0, 2 * SUB, SOFTMAX_ROWS):
                        hh, r = divmod(c, SUB)
                        lane = hh * HEAD_DIM
                        s = s_buf[j, c:c + SOFTMAX_ROWS, 0:nk] + bias_ref[g, bi, hh, r:r + SOFTMAX_ROWS, 0:nk]
                        prob = jnp.exp(s - l_buf[j, r:r + SOFTMAX_ROWS, lane:lane + 1])
                        dprob = dp_buf[j, c:c + SOFTMAX_ROWS, 0:nk] + c_buf[j, r:r + SOFTMAX_ROWS, lane:lane + 1]
                        ds_buf[j, c:c + SOFTMAX_ROWS, 0:nk] = (prob * dprob * (HEAD_DIM ** -0.5)).astype(BF16)
                        pb_buf[j, c:c + SOFTMAX_ROWS, 0:nk] = prob.astype(BF16)
                for j, (row0, krow0, _, _) in enumerate(places):
                    ds = ds_buf[j, :, 0:nk]
                    dq2 = jnp.dot(ds, sub[1, pl.ds(krow0, nk), :], preferred_element_type=F32)
                    dsub[0, pl.ds(row0, SUB), :] = jnp.where(head0, dq2[0:SUB], dq2[SUB:2 * SUB])
                    dsub[1, pl.ds(krow0, nk), :] += _tn(ds, q2_buf[j])
                    dsub[2, pl.ds(krow0, nk), :] += _tn(pb_buf[j, :, 0:nk], do2_buf[j])
                return carry

            lax.fori_loop(0, n_blocks // BLOCKS_PER_TRIP, trip, 0)
            for w in range(3):
                cols = slice((3 * w + g) * SLAB, (3 * w + g + 1) * SLAB)
                if dil == 1:
                    dp_ref[:, cols] = dsub[w].astype(BF16)
                else:
                    n = seq // dil
                    for r in range(dil):
                        stage[pl.ds(r, n, stride=dil), :] = dsub[w, pl.ds(r * n, n), :]
                    dp_ref[:, cols] = stage[...].astype(BF16)

    return pl.pallas_call(
        body, name="attn_bwd", grid=(bsz, 2),
        in_specs=[pl.BlockSpec(memory_space=pl.ANY), pl.BlockSpec((seq, PAIR_COLS), lambda b, p: (b, p)),
                  pl.BlockSpec((N_GROUPS, seq, SLAB), lambda b, p: (0, b, p)),
                  pl.BlockSpec((N_GROUPS, seq, SLAB), lambda b, p: (0, b, p)),
                  pl.BlockSpec((seq, SLAB), lambda b, p: (b, p)), pl.BlockSpec(memory_space=pl.ANY)],
        out_specs=pl.BlockSpec((seq, PAIR_COLS), lambda b, p: (b, p)),
        out_shape=jax.ShapeDtypeStruct(dproj.shape, BF16), input_output_aliases={0: 0},
        scratch_shapes=[pltpu.VMEM((3, seq, SLAB), BF16), pltpu.VMEM((seq, SLAB), F32), pltpu.VMEM((3, seq, SLAB), F32),
                        pltpu.VMEM((3, seq, SLAB), F32), pltpu.VMEM((3, seq, SLAB), F32),
                        pltpu.VMEM((N_GROUPS, 2, 2, SUB, 2 * SUB), F32),
                        pltpu.VMEM((BLOCKS_PER_TRIP, 2 * SUB, 2 * SUB), F32), pltpu.VMEM((BLOCKS_PER_TRIP, 2 * SUB, 2 * SUB), F32),
                        pltpu.VMEM((BLOCKS_PER_TRIP, 2 * SUB, 2 * SUB), BF16), pltpu.VMEM((BLOCKS_PER_TRIP, 2 * SUB, 2 * SUB), BF16),
                        pltpu.VMEM((BLOCKS_PER_TRIP, 2 * SUB, SLAB), BF16), pltpu.VMEM((BLOCKS_PER_TRIP, 2 * SUB, SLAB), BF16),
                        pltpu.VMEM((BLOCKS_PER_TRIP, SUB, SLAB), F32), pltpu.VMEM((BLOCKS_PER_TRIP, SUB, SLAB), F32)],
        compiler_params=_params("arbitrary", "arbitrary"))(dproj, pa, o_all, lse_all, da, after)


def _grad_h(dproj, w_all, gx0, x2, mod3, seq, lay):
    t, d = x2.shape
    tm, tn = min(512, seq), min(512, d)
    per_seq = seq // tm

    def body(dp_ref, w_ref, gx0_ref, x_ref, scale_ref, gx_ref, dmod_ref):
        dh = _nt(dp_ref[:, 0:ATT], w_ref[:, 0:ATT]) + _nt(dp_ref[:, lay.c0:], w_ref[:, lay.c0:])
        gx_ref[...] = gx0_ref[...] + dh * (1.0 + scale_ref[...])

        @pl.when(pl.program_id(1) % per_seq == 0)
        def _():
            dmod_ref[...] = jnp.zeros_like(dmod_ref)

        dmod_ref[0:1, :] += jnp.sum(dh, axis=0, keepdims=True)
        dmod_ref[1:2, :] += jnp.sum(dh * x_ref[...], axis=0, keepdims=True)

    tile = pl.BlockSpec((tm, tn), lambda j, i: (i, j))
    return pl.pallas_call(
        body, name="grad_h", grid=(d // tn, t // tm),
        in_specs=[pl.BlockSpec((tm, lay.np), lambda j, i: (i, 0)), pl.BlockSpec((tn, lay.np), lambda j, i: (j, 0)),
                  tile, tile,
                  pl.BlockSpec((None, 1, tn), lambda j, i: (i // per_seq, 0, d // tn + j))],
        out_specs=[tile, pl.BlockSpec((None, 8, tn), lambda j, i: (i // per_seq, 0, j))],
        out_shape=[jax.ShapeDtypeStruct((t, d), F32), jax.ShapeDtypeStruct((t // seq, 8, d), F32)],
        compiler_params=_params("arbitrary", "arbitrary"))(dproj, w_all, gx0, x2, mod3)


def _grad_w_in(ht, dproj, seq, lay):
    d, t = ht.shape
    tm = seq
    n_i = t // tm

    def make_body(n_skip, n_pieces, tn, nat_tile):
        def body(*refs):
            refs = refs[n_skip:]
            ht_ref, dp_refs = refs[0], refs[1:1 + n_pieces]
            gw_hbm, gb_hbm, acc, bacc, gw_out, gb_out, sem = refs[1 + n_pieces:]
            i, j = pl.program_id(0), pl.program_id(1)
            dp = dp_refs[0][...] if n_pieces == 1 else jnp.concatenate([r[...] for r in dp_refs], axis=1)
            part = jnp.dot(ht_ref[...], dp, preferred_element_type=F32)
            bpart = jnp.sum(dp.astype(F32), axis=0, keepdims=True)

            if n_i > 1:
                @pl.when(i == 0)
                def _():
                    acc[j] = part
                    bacc[j] = bpart

                @pl.when((i > 0) & (i < n_i - 1))
                def _():
                    acc[j] += part
                    bacc[j] += bpart

            @pl.when(i == n_i - 1)
            def _():
                gw_out[...] = ((part + acc[j]) if n_i > 1 else part).astype(BF16)
                gb_out[...] = (bpart + bacc[j]) if n_i > 1 else bpart
                cols = pl.ds(pl.multiple_of(nat_tile(j) * tn, SLAB), tn)
                copies = [pltpu.make_async_copy(gw_out, gw_hbm.at[:, cols], sem.at[0]),
                          pltpu.make_async_copy(gb_out, gb_hbm.at[:, cols], sem.at[1])]
                for cp in copies:
                    cp.start()
                for cp in copies:
                    cp.wait()
        return body

    def call(name, pieces, n_tiles, nat_tile, prev):
        tn = sum(w for w, _ in pieces)
        any_spec = pl.BlockSpec(memory_space=pl.ANY)
        in_specs = [pl.BlockSpec((d, tm), lambda i, j: (0, i))]
        in_specs += [pl.BlockSpec((tm, w), lambda i, j, f=f: (i, f(j))) for w, f in pieces]
        args = [ht] + [dproj] * len(pieces)
        aliases = {}
        if prev is not None:
            in_specs = [any_spec] * 2 + in_specs
            args = list(prev) + args
            aliases = {0: 0, 1: 1}
        return pl.pallas_call(
            make_body(0 if prev is None else 2, len(pieces), tn, nat_tile), name=name, grid=(n_i, n_tiles), in_specs=in_specs,
            out_specs=[any_spec, any_spec],
            out_shape=[jax.ShapeDtypeStruct((d, lay.din), BF16), jax.ShapeDtypeStruct((1, lay.din), F32)],
            input_output_aliases=aliases,
            scratch_shapes=[pltpu.VMEM((n_tiles, d, tn), F32), pltpu.VMEM((n_tiles, 1, tn), F32), pltpu.VMEM((d, tn), BF16),
                            pltpu.VMEM((1, tn), F32), pltpu.SemaphoreType.DMA((2,))],
            compiler_params=_params("arbitrary", "arbitrary"))(*args)

    attn_pieces = [(SLAB, lambda j, m=m: (m % 2) * PAIR_SLABS + 2 * j + m // 2) for m in range(4)]
    first = call("grad_w_in_attn", attn_pieces, ATT // 512, lambda j: j, None)
    base = lay.c0 // CONV_TILE
    nct = lay.n_conv_tiles
    if nct % 2:
        return call("grad_w_in_rest", [(CONV_TILE, lambda j: base + j)], 6 * d // CONV_TILE, lay.rest_nat_tile, first)
    half = nct // 2

    def rest_piece(m):
        def perm_tile(j):
            conv = base + 4 * (2 * (j % half) + m) + j // half
            return jnp.where(j < 4 * half, conv, base + 2 * j + m)
        return (CONV_TILE, perm_tile)

    return call("grad_w_in_rest", [rest_piece(0), rest_piece(1)], 6 * d // 512, lambda j: ATT // 512 + j, first)


def _pack_rows(parts, width=128):
    flat = [p.reshape(-1) for p in parts]
    spans, rows = [], 0
    padded = []
    for f in flat:
        n = -(-f.shape[0] // (8 * width)) * 8
        padded.append(jnp.pad(f, (0, n * width - f.shape[0])).reshape(n, width))
        spans.append((rows, f.shape[0]))
        rows += n
    return jnp.concatenate(padded, axis=0), spans


def _unpack_rows(packed, spans, shapes, width=128):
    out = []
    for (row, n), shp in zip(spans, shapes):
        rows = -(-n // width)
        out.append(packed[row:row + rows].reshape(-1)[:n].reshape(shp))
    return out


def kernel(x, c, w_ada, b_ada, w_in, b_in, conv_w, w_proj_attn, w_proj_conv, w_out, b_out, ln_g, ln_b, loss_target, m_w_ada, m_b_ada, m_w_in, m_b_in, m_conv_w, m_w_proj_attn, m_w_proj_conv, m_w_out, m_b_out, m_ln_g, m_ln_b, v_w_ada, v_b_ada, v_w_in, v_b_in, v_conv_w, v_w_proj_attn, v_w_proj_conv, v_w_out, v_b_out, v_ln_g, v_ln_b):
    bsz, seq, d = x.shape
    t = bsz * seq
    lay = _Layout(d)
    col_sharded = [True, True, False, False]
    red_w = [w_in[0], w_proj_attn[0], w_proj_conv[0], w_out[0]]
    chip = 2 * lax.axis_index("x") + lax.axis_index("y")
    chip1 = chip.astype(jnp.int32).reshape(1)
    core1 = lax.axis_index("c").astype(jnp.int32).reshape(1)
    place = jnp.stack([chip, lax.axis_index("c")]).astype(jnp.int32)
    x2 = x.reshape(t, d)
    target2 = loss_target.reshape(t, d)

    mod, act_all = _ada_exchange(c, w_ada[0], b_ada)
    mod3 = mod.reshape(bsz, 1, 3 * d)

    cw_pad = jnp.pad(conv_w[0], ((0, 5), (0, 0))) + 0.0 * mod[0, 0]
    own_in_full = [_cast_into_full(red_w[w], col_sharded[w], chip1, f"cast_shard_{w}") for w in range(4)]
    (wi_f,), cw8 = _gather_weights(own_in_full[:1], col_sharded[:1], cw_pad)
    cw_full = cw8[0:3]
    late_copies = _direct_gather_copies(col_sharded[1:])
    late_send, late_recv, late_flying, late_token = _start_copies("gather_late_start", own_in_full[1:], (18,), cw8, late_copies)
    w_all = _permute_w_in(wi_f, lay)
    b_all = lay.perm_vector(b_in) + late_token[0, 0]

    rest_tn = 1024 if (6 * d) % 1024 == 0 else 512
    pa, = _project(x2, mod3, w_all, b_all, seq, 0, ATT, PAIR_COLS, BF16, False, "project_attn")
    pr, ht = _project(x2, mod3, w_all, b_all, seq, lay.c0, 6 * d, rest_tn, BF16, True, "project_rest")
    o_all, lse_all, a_in = _attn_fwd(pa, bsz, seq)
    b_in_act = _conv_fwd(pr, cw_full, bsz, seq, d)
    wpa_f, wpc_f, wo_f = _wait_copies("gather_late_wait", late_flying, late_send, late_recv, b_in_act, late_copies)
    (dproj, da_in, db_in, gx0, dgate, small_tail, gw_pa, gw_pc, gw_out) = _tail(
        a_in, b_in_act, pr, x2, target2, mod3, wpa_f, wpc_f, wo_f, b_out, ln_g, ln_b, seq, lay)

    late_views = _shard_views([gw_pa, gw_pc, gw_out], col_sharded[1:])
    late_lands = [lax.empty((v.shape[0], v.shape[1] // 2, v.shape[2]), v.dtype) for v in late_views]
    xl_send, xl_recv, xl_fly, xl_tok = _start_copies("grads_pair_exchange_late_start", late_views + late_lands, (3,), gw_out,
                                                     _pair_exchange_copies(3, False))
    dproj, gcw = _conv_bwd(dproj, db_in, pr, cw_full + xl_tok[0, 0], bsz, seq, lay)
    xl_done = _wait_copies("grads_pair_exchange_late_wait", xl_fly, xl_send, xl_recv, gcw, _pair_exchange_copies(3, True))
    late_parts = [_pair_sum(xl_done[w], xl_done[3 + w], core1, f"grads_pair_sum_late_{w}") for w in range(3)]

    late_cross = _chip_scatter_copies(3, col_sharded[1:])
    late_zone = [lax.empty((3, p.shape[1], _piece_cols(p, cs)), p.dtype) for p, cs in zip(late_parts, col_sharded[1:])]
    sl_send, sl_recv, sl_fly, sl_tok = _start_copies("grads_scatter_late_start", late_parts + late_zone, (9,), late_parts[0],
                                                     late_cross)
    dproj = _attn_bwd(dproj, pa, o_all, lse_all, da_in, bsz, seq, sl_tok)
    gw_in_bf, gb_in = _grad_w_in(ht, dproj, seq, lay)
    sl_done = _wait_copies("grads_scatter_late_wait", sl_fly, sl_send, sl_recv, gw_in_bf, late_cross)
    late_red = [_chip_sum(sl_done[w], sl_done[3 + w], col_sharded[1 + w], place, f"grads_chip_sum_late_{w}") for w in range(3)]

    in_view = _shard_views([gw_in_bf], col_sharded[:1])
    in_got = _pair_exchange_halves(in_view, "grads_pair_exchange_in")
    in_part = _pair_sum(in_view[0], in_got[0], core1, "grads_pair_sum_in")
    in_cross = _chip_scatter_copies(1, col_sharded[:1])
    in_zone = [lax.empty((3, in_part.shape[1], _piece_cols(in_part, True)), in_part.dtype)]
    si_send, si_recv, si_fly, si_tok = _start_copies("grads_scatter_in_start", [in_part] + in_zone, (3,), late_red[0], in_cross)
    jl_send, jl_recv, jl_fly, jl_tok = _start_copies("grads_pair_join_late_start", [f.reshape(1, *f.shape) for f in late_red], (3,),
                                                     si_tok, _pair_join_copies(3, False))
    grad_x2, dmod = _grad_h(dproj, w_all, gx0, x2, mod3 + (si_tok[0, 0] + jl_tok[0, 0]), seq, lay)
    jl_done = _wait_copies("grads_pair_join_late_wait", jl_fly, jl_send, jl_recv, grad_x2, _pair_join_copies(3, True))
    late_joined = [f[0] for f in jl_done]
    si_done = _wait_copies("grads_scatter_in_wait", si_fly, si_send, si_recv, grad_x2, in_cross)
    in_red = _chip_sum(si_done[0], si_done[1], True, place, "grads_chip_sum_in")

    d_ada = jnp.concatenate([dmod[:, 0, :], dmod[:, 1, :], dgate[:, 0, :]], axis=1)
    pieces = [small_tail[3], jnp.sum(d_ada, axis=0), gb_in[0], small_tail[0], small_tail[1], small_tail[2], gcw[0:3]]
    packed, spans = _pack_rows(pieces)
    kept_in, _ = _pack_rows([d_ada])
    rows_all = jnp.concatenate([packed, kept_in], axis=0)
    small_land = lax.empty((N_DEV,) + rows_all.shape, F32)
    sm_send, sm_recv, sm_fly, sm_tok = _start_copies("small_gather_start", [rows_all, small_land], (N_DEV - 1,), in_red,
                                                     _small_gather_copies)
    ji_send, ji_recv, ji_fly, ji_tok = _start_copies("grads_pair_join_in_start", [in_red.reshape(1, *in_red.shape)], (1,), sm_tok,
                                                     _pair_join_copies(1, False))
    big_w = [w_ada[0]] + red_w
    big_m = [m_w_ada[0], m_w_in[0], m_w_proj_attn[0], m_w_proj_conv[0], m_w_out[0]]
    big_v = [v_w_ada[0], v_w_in[0], v_w_proj_attn[0], v_w_proj_conv[0], v_w_out[0]]
    big_out = [None] * 5
    for w in range(3):
        big_out[2 + w] = _adamw(big_w[2 + w], late_joined[w], big_m[2 + w], big_v[2 + w], f"adamw_{2 + w}", ji_tok)
    sm_done = _wait_copies("small_gather_wait", sm_fly, sm_send, sm_recv, big_out[4][0], _small_gather_copies)
    me1 = (4 * lax.axis_index("x") + 2 * lax.axis_index("y") + lax.axis_index("c")).astype(jnp.int32).reshape(1)
    summed, kept = _small_sum(sm_done[0], sm_done[1], me1, packed.shape[0], d)
    loss = summed[0, 0]
    _, g_b_ada, g_b_in, g_b_out, g_ln_g, g_ln_b, g_cw_full = _unpack_rows(
        summed, spans, [(d,), (3 * d,), (lay.din,), (d,), (d,), (d,), (3, d)])
    g_cw = lax.dynamic_slice(g_cw_full, (0, chip * (d // N_CHIPS)), (3, d // N_CHIPS))
    d_ada_all = kept.reshape(N_DEV, -1)[:, :bsz * 3 * d].reshape(N_DEV * bsz, 3 * d)
    ada_cols = 3 * d // N_CHIPS
    g_w_ada = _grad_w_ada_cols(act_all.T, lax.dynamic_slice(d_ada_all, (0, chip * ada_cols), (N_DEV * bsz, ada_cols)))

    big_out[0] = _adamw(big_w[0], g_w_ada, big_m[0], big_v[0], "adamw_0", None)
    small_w = [b_ada, b_in, conv_w[0], b_out, ln_g, ln_b]
    small_g = [g_b_ada, g_b_in, g_cw, g_b_out, g_ln_g, g_ln_b]
    small_m = [m_b_ada, m_b_in, m_conv_w[0], m_b_out, m_ln_g, m_ln_b]
    small_v = [v_b_ada, v_b_in, v_conv_w[0], v_b_out, v_ln_g, v_ln_b]
    pw, sp = _pack_rows(small_w)
    pg, _ = _pack_rows(small_g)
    pm, _ = _pack_rows(small_m)
    pv, _ = _pack_rows(small_v)
    sd, sm, sv = _adamw(pw, pg, pm, pv, "adamw_small", None)
    ji_done = _wait_copies("grads_pair_join_in_wait", ji_fly, ji_send, ji_recv, sd, _pair_join_copies(1, True))
    g_big = [g_w_ada, ji_done[0][0]] + late_joined
    big_out[1] = _adamw(big_w[1], g_big[1], big_m[1], big_v[1], "adamw_1", None)
    shapes = [a.shape for a in small_w]
    sd, sm, sv = _unpack_rows(sd, sp, shapes), _unpack_rows(sm, sp, shapes), _unpack_rows(sv, sp, shapes)

    def order(wa, bA, wi, bI, cw, wpa, wpc, wo, bO, lg, lb):
        return (wa[None], bA, wi[None], bI, cw[None], wpa[None], wpc[None], wo[None], bO, lg, lb)

    sg = [g.reshape(s) for g, s in zip(small_g, shapes)]
    grads_out = order(g_big[0], sg[0], g_big[1], sg[1], sg[2], g_big[2], g_big[3], g_big[4], sg[3], sg[4], sg[5])
    outs = []
    for idx, small in enumerate((sd, sm, sv)):
        outs.append(order(big_out[0][idx], small[0], big_out[1][idx], small[1], small[2], big_out[2][idx],
                          big_out[3][idx], big_out[4][idx], small[3], small[4], small[5]))
    return (loss, grad_x2.reshape(bsz, seq, d), *grads_out, *outs[0], *outs[1], *outs[2])
```

```python
import functools

import jax
import jax.numpy as jnp
from jax import lax
from jax.experimental import pallas as pl
from jax.experimental.pallas import tpu as pltpu

F32 = jnp.float32
BF16 = jnp.bfloat16
MESH = pl.DeviceIdType.MESH

HEAD_DIM = 64
N_GROUPS = 3
DILATIONS = (1, 4, 16)
N_HEADS = 12
SUB = 128
Q_WIDTH = 768
Z_WIDTH = 256
ATT = 3 * Q_WIDTH + Z_WIDTH
SLAB = 128
PAIR_SLABS = 10
PAIR_COLS = PAIR_SLABS * SLAB
CONV_TILE = 256
SOFTMAX_ROWS = 32
BLOCKS_PER_TRIP = 8
ALIBI_MAX_EXP = 8.0
ALPHA = 2.0 ** 0.25
LN_EPS = 1e-5
ADAM_LR, ADAM_B1, ADAM_B2, ADAM_EPS, ADAM_WD, ADAM_STEP = 0.001, 0.9, 0.999, 1e-08, 0.01, 10
N_CHIPS = 4
N_DEV = 8
VMEM_LIMIT_V7X = 60 * 1024 * 1024
NEG = -1e30


def _params(*sem):
    return pltpu.CompilerParams(dimension_semantics=sem, vmem_limit_bytes=VMEM_LIMIT_V7X)


def _sigmoid(v):
    return 0.5 * jnp.tanh(0.5 * v) + 0.5


class _Layout:
    def __init__(self, d):
        self.d = d
        self.din = ATT + 6 * d
        c0 = 3072
        while c0 % (2 * d):
            c0 += 1024
        self.c0, self.g0, self.np = c0, c0 + 4 * d, c0 + 6 * d
        self.n_conv_tiles = d // CONV_TILE

    def attn_nat_slab(self, s):
        p, i = s // PAIR_SLABS, s % PAIR_SLABS
        return jnp.where(i < 9, (i // 3) * 6 + (i % 3) * 2 + p, 18 + p)

    def rest_nat_tile(self, t):
        n4 = 4 * self.n_conv_tiles
        conv = ATT // CONV_TILE + (t % 4) * self.n_conv_tiles + t // 4
        return jnp.where(t < n4, conv, ATT // CONV_TILE + t)

    def perm_vector(self, v):
        parts = []
        for s in range(2 * PAIR_SLABS):
            p, i = divmod(s, PAIR_SLABS)
            ns = (i // 3) * 6 + (i % 3) * 2 + p if i < 9 else 18 + p
            parts.append(v[:, ns * SLAB:(ns + 1) * SLAB])
        parts.append(jnp.zeros((1, self.c0 - ATT), v.dtype))
        for j in range(self.n_conv_tiles):
            for k in range(4):
                a = ATT + k * self.d + j * CONV_TILE
                parts.append(v[:, a:a + CONV_TILE])
        parts.append(v[:, ATT + 4 * self.d:])
        return jnp.concatenate(parts, axis=1)


def _place():
    return lax.axis_index("x"), lax.axis_index("y"), lax.axis_index("c")


def _other_chips(x, y):
    return [(1 - x, y), (x, 1 - y), (1 - x, 1 - y)]


def _shard_of(ref, col_sharded, chip, half=None):
    if col_sharded:
        cs = ref.shape[1] // N_CHIPS
        cols = pl.ds(pl.multiple_of(chip * cs, SLAB), cs)
        if half is None:
            return ref.at[:, cols]
        n = ref.shape[0] // 2
        return ref.at[pl.ds(half * n, n), cols]
    rs = ref.shape[0] // N_CHIPS
    if half is None:
        return ref.at[pl.ds(chip * rs, rs)]
    return ref.at[pl.ds(chip * rs + half * (rs // 2), rs // 2)]


GATHER_PIECES = 4


def _cast_into_full(shard, col_sharded, chip, name):
    rows, cols = shard.shape
    tr = _row_tile(rows, cols)
    nb = rows // tr

    def body(chip_ref, s_ref, o_ref):
        del chip_ref
        o_ref[...] = s_ref[...].astype(BF16)

    if col_sharded:
        full, out_spec = (rows, cols * N_CHIPS), pl.BlockSpec((tr, cols), lambda i, ch: (i, ch[0]))
    else:
        full, out_spec = (rows * N_CHIPS, cols), pl.BlockSpec((tr, cols), lambda i, ch: (ch[0] * nb + i, 0))
    return pl.pallas_call(
        body, name=name,
        grid_spec=pltpu.PrefetchScalarGridSpec(num_scalar_prefetch=1, grid=(nb,),
                                               in_specs=[pl.BlockSpec((tr, cols), lambda i, ch: (i, 0))], out_specs=out_spec),
        out_shape=jax.ShapeDtypeStruct(full, BF16), compiler_params=_params("parallel"))(chip, shard)


def _cast_into_full_small(shards, col_sharded, chip, name):
    n = len(shards)

    def body(chip_ref, *refs):
        del chip_ref
        for w in range(n):
            refs[n + w][...] = refs[w][...].astype(BF16)

    in_specs = [pl.BlockSpec(s.shape, lambda i, ch: (0, 0)) for s in shards]
    out_specs = [pl.BlockSpec(s.shape, (lambda i, ch: (0, ch[0])) if cs else (lambda i, ch: (ch[0], 0)))
                 for s, cs in zip(shards, col_sharded)]
    fulls = [(s.shape[0], s.shape[1] * N_CHIPS) if cs else (s.shape[0] * N_CHIPS, s.shape[1]) for s, cs in zip(shards, col_sharded)]
    return pl.pallas_call(
        body, name=name,
        grid_spec=pltpu.PrefetchScalarGridSpec(num_scalar_prefetch=1, grid=(1,), in_specs=in_specs, out_specs=out_specs),
        out_shape=[jax.ShapeDtypeStruct(f, BF16) for f in fulls], compiler_params=_params("arbitrary"))(chip, *shards)


def _gather_weights(fulls, col_sharded, small):
    n = len(fulls)
    kp = GATHER_PIECES

    def piece(ref, cs, chip, half, k):
        if cs:
            width = ref.shape[1] // N_CHIPS
            rows = ref.shape[0] // 2 // kp
            return ref.at[pl.ds(half * (ref.shape[0] // 2) + k * rows, rows), pl.ds(pl.multiple_of(chip * width, SLAB), width)]
        rs = ref.shape[0] // N_CHIPS
        rows = rs // 2 // kp
        return ref.at[pl.ds(chip * rs + half * (rs // 2) + k * rows, rows)]

    def body(*refs):
        sm_in, outs, sm_out = refs[n], refs[n + 1:2 * n + 1], refs[2 * n + 1]
        send, recv, fsend, frecv, lsem, ssend, srecv = refs[2 * n + 2:]
        x, y, c = _place()
        mine = 2 * x + y
        sibling = (x, y, 1 - c)
        first = (x ^ (1 - c), y ^ c)
        second = (x ^ c, y ^ (1 - c))
        diagonal = (1 - x, 1 - y)
        sources = [first, second, diagonal]
        senders = [first, second, second]

        def copy(ref, sems, slot, to):
            return pltpu.make_async_remote_copy(src_ref=ref, dst_ref=ref, send_sem=sems[0].at[slot], recv_sem=sems[1].at[slot],
                                                device_id=to, device_id_type=MESH)

        local = pltpu.make_async_copy(sm_in, _shard_of(sm_out, True, mine), lsem)
        local.start()
        sends = []
        for k, (cx, cy) in enumerate(_other_chips(x, y)):
            cp = pltpu.make_async_remote_copy(src_ref=sm_in, dst_ref=_shard_of(sm_out, True, mine), send_sem=ssend.at[k],
                                              recv_sem=srecv.at[k], device_id=(cx, cy, c), device_id_type=MESH)
            cp.start()
            sends.append(cp)
        for k in range(kp):
            for w in range(n):
                own = piece(outs[w], col_sharded[w], mine, c, k)
                for slot, chip in enumerate((first, second)):
                    cp = copy(own, (send, recv), (w * 3 + slot) * kp + k, (*chip, c))
                    cp.start()
                    sends.append(cp)
        for slot in range(3):
            source = 2 * sources[slot][0] + sources[slot][1]
            for k in range(kp):
                for w in range(n):
                    landed = piece(outs[w], col_sharded[w], source, c, k)
                    copy(landed, (send, recv), (w * 3 + slot) * kp + k, (*senders[slot], c)).wait_recv()
                    if slot == 0:
                        cp = copy(landed, (send, recv), (w * 3 + 2) * kp + k, (*second, c))
                        cp.start()
                        sends.append(cp)
                    cp = copy(landed, (fsend, frecv), (w * 3 + slot) * kp + k, sibling)
                    cp.start()
                    sends.append(cp)
        for slot, chip in enumerate((second, first, diagonal)):
            for k in range(kp):
                for w in range(n):
                    passed = piece(outs[w], col_sharded[w], 2 * chip[0] + chip[1], 1 - c, k)
                    copy(passed, (fsend, frecv), (w * 3 + slot) * kp + k, sibling).wait_recv()
        for k, (cx, cy) in enumerate(_other_chips(x, y)):
            theirs = _shard_of(sm_out, True, 2 * cx + cy)
            pltpu.make_async_remote_copy(src_ref=theirs, dst_ref=theirs, send_sem=ssend.at[k], recv_sem=srecv.at[k],
                                         device_id=(cx, cy, c), device_id_type=MESH).wait_recv()
        for cp in sends:
            cp.wait_send()
        local.wait()

    any_spec = pl.BlockSpec(memory_space=pl.ANY)
    outs = pl.pallas_call(
        body, name="gather_weights",
        out_shape=[jax.ShapeDtypeStruct(f.shape, BF16) for f in fulls]
        + [jax.ShapeDtypeStruct((small.shape[0], small.shape[1] * N_CHIPS), small.dtype)],
        in_specs=[any_spec] * (n + 1), out_specs=[any_spec] * (n + 1), input_output_aliases={w: w for w in range(n)},
        scratch_shapes=[pltpu.SemaphoreType.DMA((n * 3 * kp,)), pltpu.SemaphoreType.DMA((n * 3 * kp,)),
                        pltpu.SemaphoreType.DMA((n * 3 * kp,)), pltpu.SemaphoreType.DMA((n * 3 * kp,)), pltpu.SemaphoreType.DMA,
                        pltpu.SemaphoreType.DMA((3,)), pltpu.SemaphoreType.DMA((3,))],
    )(*fulls, small)
    return outs[:n], outs[n]


HBM_SPEC = pl.BlockSpec(memory_space=pltpu.HBM)
SEM_SPEC = pl.BlockSpec(memory_space=pltpu.SEMAPHORE)
DATAFLOW = pltpu.SideEffectType.DATAFLOW_SIDE_EFFECTING


def _start_copies(name, arrays, sem_shape, after, copies):
    n = len(arrays)

    def body(*refs):
        for cp in copies(refs[:n], refs[n + 1], refs[n + 2]):
            cp.start()
        token = refs[2 * n + 3]
        token[...] = jnp.zeros_like(token)

    res = pl.pallas_call(
        body, name=name,
        out_shape=(pltpu.SemaphoreType.DMA(sem_shape), pltpu.SemaphoreType.DMA(sem_shape),
                   *[pltpu.HBM(a.shape, a.dtype) for a in arrays], jax.ShapeDtypeStruct((8, 128), F32)),
        in_specs=[HBM_SPEC] * n + [pl.BlockSpec(memory_space=pl.ANY)],
        out_specs=(SEM_SPEC, SEM_SPEC, *([HBM_SPEC] * n), pl.BlockSpec(memory_space=pltpu.VMEM)),
        input_output_aliases={i: 2 + i for i in range(n)},
        compiler_params=pltpu.CompilerParams(has_side_effects=DATAFLOW),
    )(*[pltpu.with_memory_space_constraint(a, pltpu.HBM) for a in arrays], after)
    return res[0], res[1], list(res[2:2 + n]), res[2 + n]


def _wait_copies(name, arrays, send, recv, after, copies):
    n = len(arrays)

    def body(*refs):
        for cp in copies(refs[:n], refs[n], refs[n + 1]):
            cp.wait_send()
            cp.wait_recv()

    return pl.pallas_call(
        body, name=name, out_shape=[pltpu.HBM(a.shape, a.dtype) for a in arrays],
        in_specs=[HBM_SPEC] * n + [SEM_SPEC, SEM_SPEC, pl.BlockSpec(memory_space=pl.ANY)], out_specs=[HBM_SPEC] * n,
        input_output_aliases={i: i for i in range(n)},
        compiler_params=pltpu.CompilerParams(has_side_effects=DATAFLOW),
    )(*arrays, send, recv, after)


def _direct_gather_copies(col_sharded):
    def copies(refs, send, recv):
        x, y, c = _place()
        mine = 2 * x + y
        out = []
        for w, ref in enumerate(refs):
            own_half = _shard_of(ref, col_sharded[w], mine, c)
            k = 0
            for cx, cy in _other_chips(x, y):
                for pc in (c, 1 - c):
                    out.append(pltpu.make_async_remote_copy(
                        src_ref=own_half, dst_ref=own_half, send_sem=send.at[6 * w + k], recv_sem=recv.at[6 * w + k],
                        device_id=(cx, cy, pc), device_id_type=MESH))
                    k += 1
        return out
    return copies


def _chip_scatter_copies(n, col_sharded):
    def piece(ref, cs, chip):
        if cs:
            w = ref.shape[2] // N_CHIPS
            return ref.at[:, :, pl.ds(pl.multiple_of(chip * w, SLAB), w)]
        return ref.at[pl.ds(chip, 1)]

    def copies(refs, send, recv):
        x, y, c = _place()
        out = []
        for k, (cx, cy) in enumerate(_other_chips(x, y)):
            for w in range(n):
                out.append(pltpu.make_async_remote_copy(
                    src_ref=piece(refs[w], col_sharded[w], 2 * cx + cy), dst_ref=refs[n + w].at[pl.ds(k, 1)],
                    send_sem=send.at[3 * w + k], recv_sem=recv.at[3 * w + k], device_id=(cx, cy, c), device_id_type=MESH))
        return out
    return copies


def _shard_views(gs, col_sharded):
    return [g.reshape(1, *g.shape) if cs else g.reshape(N_CHIPS, g.shape[0] // N_CHIPS, g.shape[1])
            for g, cs in zip(gs, col_sharded)]


DMA_CHUNK_BYTES = 1 << 20


def _chunk_rows(shape, itemsize):
    s, rows, cols = shape
    n = 1
    while s * (rows // n) * cols * itemsize > DMA_CHUNK_BYTES and (rows // n) % 32 == 0:
        n *= 2
    return rows // n


def _row_pieces(src, src_row0, dst, dst_row0, rows, send_sem, recv_sem, device):
    step = _chunk_rows((src.shape[0], rows, src.shape[2]), src.dtype.itemsize)
    return [pltpu.make_async_remote_copy(src_ref=src.at[:, pl.ds(src_row0 + r, step)], dst_ref=dst.at[:, pl.ds(dst_row0 + r, step)],
                                         send_sem=send_sem, recv_sem=recv_sem, device_id=device, device_id_type=MESH)
            for r in range(0, rows, step)]


def _pair_exchange_copies(n, whole):
    def copies(refs, send, recv):
        x, y, c = _place()
        sibling = (x, y, 1 - c)
        out = []
        for w in range(n):
            hr = refs[n + w].shape[1]
            if whole:
                out.append(pltpu.make_async_remote_copy(
                    src_ref=refs[w].at[:, pl.ds((1 - c) * hr, hr)], dst_ref=refs[n + w], send_sem=send.at[w], recv_sem=recv.at[w],
                    device_id=sibling, device_id_type=MESH))
            else:
                out += _row_pieces(refs[w], (1 - c) * hr, refs[n + w], 0, hr, send.at[w], recv.at[w], sibling)
        return out
    return copies


def _pair_join_copies(n, whole, sem0=0):
    def copies(refs, send, recv):
        x, y, c = _place()
        sibling = (x, y, 1 - c)
        out = []
        for w in range(n):
            hr = refs[w].shape[1] // 2
            sems = dict(send_sem=send.at[sem0 + w], recv_sem=recv.at[sem0 + w])
            if whole:
                out.append(pltpu.make_async_remote_copy(
                    src_ref=refs[w].at[:, pl.ds(c * hr, hr)], dst_ref=refs[w].at[:, pl.ds((1 - c) * hr, hr)],
                    device_id=sibling, device_id_type=MESH, **sems))
            else:
                out += _row_pieces(refs[w], c * hr, refs[w], c * hr, hr, sems["send_sem"], sems["recv_sem"], sibling)
        return out
    return copies


def _both_copies(first, n_first, second):
    def copies(refs, send, recv):
        return first(refs[:n_first], send, recv) + second(refs[n_first:], send, recv)
    return copies


def _small_gather_copies(refs, send, recv):
    vec, land = refs
    x, y, c = _place()
    me = 4 * x + 2 * y + c
    return [pltpu.make_async_remote_copy(src_ref=vec, dst_ref=land.at[me], send_sem=send.at[k], recv_sem=recv.at[k],
                                         device_id=peer, device_id_type=MESH) for k, (peer, _) in enumerate(_all_devices(x, y, c))]


def _small_sum(vec, land, me, n_sum, d):
    rows = vec.shape[0]

    def body(me_ref, v_ref, l_ref, sum_ref, kept_ref):
        def slot(k):
            return jnp.where(me_ref[0] == k, v_ref[...], l_ref[k])

        total = slot(0)[0:n_sum, :]
        kept_ref[0] = slot(0)[n_sum:rows, :]
        for k in range(1, N_DEV):
            total = total + slot(k)[0:n_sum, :]
            kept_ref[k] = slot(k)[n_sum:rows, :]
        sum_ref[...] = total
        loss = 0.5 / d * jnp.sum(total[0:8, :])
        sum_ref[0:8, :] = jnp.full((8, 128), loss, F32)

    vm = pl.BlockSpec(memory_space=pltpu.VMEM)
    return pl.pallas_call(
        body, name="small_sum", in_specs=[pl.BlockSpec(memory_space=pltpu.SMEM), vm, vm], out_specs=[vm, vm],
        out_shape=[jax.ShapeDtypeStruct((n_sum, 128), F32), jax.ShapeDtypeStruct((N_DEV, rows - n_sum, 128), F32)],
        compiler_params=pltpu.CompilerParams(vmem_limit_bytes=VMEM_LIMIT_V7X))(me, vec, land)


def _pair_exchange_halves(views, name):
    n = len(views)
    half_shapes = [(v.shape[0], v.shape[1] // 2, v.shape[2]) for v in views]

    def body(*refs):
        ins, got = refs[:n], refs[n:2 * n]
        send, recv = refs[2 * n:]
        x, y, c = _place()
        sibling = (x, y, 1 - c)
        for w in range(n):
            hr = half_shapes[w][1]
            for cp in _row_pieces(ins[w], (1 - c) * hr, got[w], 0, hr, send.at[w], recv.at[w], sibling):
                cp.start()
        for w in range(n):
            hr = half_shapes[w][1]
            pltpu.make_async_remote_copy(src_ref=ins[w].at[:, pl.ds((1 - c) * hr, hr)], dst_ref=got[w], send_sem=send.at[w],
                                         recv_sem=recv.at[w], device_id=sibling, device_id_type=MESH).wait()

    any_spec = pl.BlockSpec(memory_space=pl.ANY)
    return pl.pallas_call(
        body, name=name,
        out_shape=[jax.ShapeDtypeStruct(s, v.dtype) for s, v in zip(half_shapes, views)],
        in_specs=[any_spec] * n, out_specs=[any_spec] * n,
        scratch_shapes=[pltpu.SemaphoreType.DMA((n,)), pltpu.SemaphoreType.DMA((n,))],
    )(*views)


def _pair_sum(view, got, core, name):
    s, r, cols = view.shape
    hr = r // 2
    tr = _row_tile(hr, cols)
    nb = hr // tr

    def body(core_ref, a_ref, b_ref, o_ref):
        del core_ref
        o_ref[...] = (a_ref[...].astype(F32) + b_ref[...].astype(F32)).astype(BF16)

    same = pl.BlockSpec((None, tr, cols), lambda j, i, core_ref: (j, i, 0))
    return pl.pallas_call(
        body, name=name,
        grid_spec=pltpu.PrefetchScalarGridSpec(
            num_scalar_prefetch=1, grid=(s, nb),
            in_specs=[pl.BlockSpec((None, tr, cols), lambda j, i, core_ref: (j, core_ref[0] * nb + i, 0)), same],
            out_specs=same),
        out_shape=jax.ShapeDtypeStruct((s, hr, cols), BF16), compiler_params=_params("parallel", "parallel"))(core, view, got)


def _pair_sum_small(views, gots, core, name):
    n = len(views)

    def body(core_ref, *refs):
        del core_ref
        for w in range(n):
            refs[2 * n + w][...] = (refs[w][...].astype(F32) + refs[n + w][...].astype(F32)).astype(BF16)

    halves = [(v.shape[0], v.shape[1] // 2, v.shape[2]) for v in views]
    own = [pl.BlockSpec(h, lambda i, core_ref: (0, core_ref[0], 0)) for h in halves]
    whole = [pl.BlockSpec(h, lambda i, core_ref: (0, 0, 0)) for h in halves]
    return pl.pallas_call(
        body, name=name,
        grid_spec=pltpu.PrefetchScalarGridSpec(num_scalar_prefetch=1, grid=(1,), in_specs=own + whole, out_specs=whole),
        out_shape=[jax.ShapeDtypeStruct(h, BF16) for h in halves], compiler_params=_params("arbitrary"))(core, *views, *gots)


def _piece_cols(part, col_sharded):
    return part.shape[2] // N_CHIPS if col_sharded else part.shape[2]


def _chip_sum_small(parts, gots, col_sharded, place, name):
    n = len(parts)

    def body(place_ref, *refs):
        del place_ref
        for w in range(n):
            got = refs[n + w]
            acc = refs[w][...].astype(F32) + got[0].astype(F32)
            refs[2 * n + w][...] = acc + got[1].astype(F32) + got[2].astype(F32)

    own, others, outs, shapes = [], [], [], []
    for p, cs in zip(parts, col_sharded):
        hr, cols = p.shape[1], _piece_cols(p, cs)
        own.append(pl.BlockSpec((None, hr, cols), (lambda i, pr: (0, 0, pr[0])) if cs else (lambda i, pr: (pr[0], 0, 0))))
        others.append(pl.BlockSpec((3, hr, cols), lambda i, pr: (0, 0, 0)))
        outs.append(pl.BlockSpec((hr, cols), lambda i, pr: (pr[1], 0)))
        shapes.append(jax.ShapeDtypeStruct((2 * hr, cols), F32))
    return pl.pallas_call(
        body, name=name,
        grid_spec=pltpu.PrefetchScalarGridSpec(num_scalar_prefetch=1, grid=(1,), in_specs=own + others, out_specs=outs),
        out_shape=shapes, compiler_params=_params("arbitrary"))(place, *parts, *gots)


def _chip_sum(part, got, col_sharded, place, name):
    _, hr, _ = part.shape
    cols = _piece_cols(part, col_sharded)
    tr = _row_tile(hr, cols)
    nb = hr // tr

    def body(place_ref, own_ref, g0_ref, g1_ref, g2_ref, o_ref):
        del place_ref
        acc = own_ref[...].astype(F32) + g0_ref[...].astype(F32)
        o_ref[...] = acc + g1_ref[...].astype(F32) + g2_ref[...].astype(F32)

    if col_sharded:
        own = pl.BlockSpec((None, tr, cols), lambda i, pr: (0, i, pr[0]))
    else:
        own = pl.BlockSpec((None, tr, cols), lambda i, pr: (pr[0], i, 0))
    others = [pl.BlockSpec((None, tr, cols), lambda i, pr, k=k: (k, i, 0)) for k in range(3)]
    return pl.pallas_call(
        body, name=name,
        grid_spec=pltpu.PrefetchScalarGridSpec(
            num_scalar_prefetch=1, grid=(nb,), in_specs=[own] + others,
            out_specs=pl.BlockSpec((tr, cols), lambda i, pr: (pr[1] * nb + i, 0))),
        out_shape=jax.ShapeDtypeStruct((2 * hr, cols), F32), compiler_params=_params("parallel"))(place, part, got, got, got)


def _row_tile(rows, cols, itemsize=4, budget=2 << 20):
    t = rows
    while t * cols * itemsize > budget and t % 16 == 0:
        t //= 2
    return t


def _adamw(w, g, m, v, name, after, also_g=False):
    rows, cols = w.shape
    tr = _row_tile(rows, cols, budget=1 << 20)
    extra = [] if after is None else [after]
    n_out = 4 if also_g else 3

    def body(w_ref, g_ref, m_ref, v_ref, *rest):
        d_ref, nm_ref, nv_ref = rest[len(extra):len(extra) + 3]
        g_ = g_ref[...]
        if also_g:
            rest[len(extra) + 3][...] = g_
        nm = ADAM_B1 * m_ref[...] + (1.0 - ADAM_B1) * g_
        nv = ADAM_B2 * v_ref[...] + (1.0 - ADAM_B2) * (g_ * g_)
        m_hat = nm / (1.0 - ADAM_B1 ** ADAM_STEP)
        v_hat = nv / (1.0 - ADAM_B2 ** ADAM_STEP)
        d_ref[...] = -ADAM_LR * (m_hat / (jnp.sqrt(v_hat) + ADAM_EPS) + ADAM_WD * w_ref[...])
        nm_ref[...] = nm
        nv_ref[...] = nv

    spec = pl.BlockSpec((tr, cols), lambda i: (i, 0))
    shp = jax.ShapeDtypeStruct((rows, cols), F32)
    return pl.pallas_call(body, name=name, grid=(rows // tr,),
                          in_specs=[spec] * 4 + [pl.BlockSpec(memory_space=pl.ANY)] * len(extra), out_specs=[spec] * n_out,
                          out_shape=[shp] * n_out, compiler_params=_params("parallel"))(w, g, m, v, *extra)


def _all_devices(x, y, c):
    out = []
    for k in range(1, N_DEV):
        peer = (x ^ ((k >> 2) & 1), y ^ ((k >> 1) & 1), c ^ (k & 1))
        out.append((peer, 4 * peer[0] + 2 * peer[1] + peer[2]))
    return out


def _ada_exchange(c, w_shard, b_ada):
    bsz, d = c.shape
    cs = w_shard.shape[1]

    def body(c_ref, w_ref, b_ref, mod_ref, act_ref, c_all, part, pieces, csend, crecv, psend, precv):
        x, y, core = _place()
        me = 4 * x + 2 * y + core
        chip = 2 * x + y
        c_all[me] = c_ref[...]
        peers = _all_devices(x, y, core)
        copies = []
        for k, (peer, _) in enumerate(peers):
            cp = pltpu.make_async_remote_copy(src_ref=c_ref, dst_ref=c_all.at[me], send_sem=csend.at[k], recv_sem=crecv.at[k],
                                              device_id=peer, device_id_type=MESH)
            cp.start()
            copies.append(cp)
        for k, (_, src) in enumerate(peers):
            pltpu.make_async_remote_copy(src_ref=c_ref, dst_ref=c_all.at[src], send_sem=csend.at[k], recv_sem=crecv.at[k],
                                         device_id=(x, y, core), device_id_type=MESH).wait_recv()
        rows = jnp.concatenate([c_all[i] for i in range(N_DEV)], axis=0)
        act = rows * _sigmoid(rows)
        act_ref[...] = act
        prod = jnp.dot(act.astype(BF16), w_ref[...].astype(BF16), preferred_element_type=F32)
        for i in range(N_DEV):
            part[i] = prod[i * bsz:(i + 1) * bsz, :]
        pieces[chip] = part[me]
        chips = _other_chips(x, y)
        for k, (cx, cy) in enumerate(chips):
            cp = pltpu.make_async_remote_copy(src_ref=part.at[4 * cx + 2 * cy + core], dst_ref=pieces.at[chip],
                                              send_sem=psend.at[k], recv_sem=precv.at[k], device_id=(cx, cy, core),
                                              device_id_type=MESH)
            cp.start()
            copies.append(cp)
        for k, (cx, cy) in enumerate(chips):
            pltpu.make_async_remote_copy(src_ref=part.at[me], dst_ref=pieces.at[2 * cx + cy], send_sem=psend.at[k],
                                         recv_sem=precv.at[k], device_id=(cx, cy, core), device_id_type=MESH).wait_recv()
        for cp in copies:
            cp.wait_send()
        mod_ref[...] = jnp.concatenate([pieces[j] for j in range(N_CHIPS)], axis=1) + b_ref[...]

    vm = pl.BlockSpec(memory_space=pltpu.VMEM)
    return pl.pallas_call(
        body, name="ada_exchange", in_specs=[vm] * 3, out_specs=[vm] * 2,
        out_shape=[jax.ShapeDtypeStruct((bsz, 3 * d), F32), jax.ShapeDtypeStruct((N_DEV * bsz, d), F32)],
        scratch_shapes=[pltpu.VMEM((N_DEV, bsz, d), F32), pltpu.VMEM((N_DEV, bsz, cs), F32), pltpu.VMEM((N_CHIPS, bsz, cs), F32),
                        pltpu.SemaphoreType.DMA((N_DEV - 1,)), pltpu.SemaphoreType.DMA((N_DEV - 1,)),
                        pltpu.SemaphoreType.DMA((3,)), pltpu.SemaphoreType.DMA((3,))],
        compiler_params=pltpu.CompilerParams(vmem_limit_bytes=VMEM_LIMIT_V7X))(c, w_shard, b_ada)


def _grad_w_ada_cols(act_t, d_cols):
    d, n = act_t.shape
    cs = d_cols.shape[1]

    def body(a_ref, g_ref, o_ref):
        a, g = a_ref[...], g_ref[...]
        acc = a[:, 0:1] * g[0:1, :]
        for b in range(1, n):
            acc = acc + a[:, b:b + 1] * g[b:b + 1, :]
        o_ref[...] = acc

    vm = pl.BlockSpec(memory_space=pltpu.VMEM)
    return pl.pallas_call(body, name="grad_w_ada", in_specs=[vm] * 2, out_specs=vm,
                          out_shape=jax.ShapeDtypeStruct((d, cs), F32),
                          compiler_params=pltpu.CompilerParams(vmem_limit_bytes=VMEM_LIMIT_V7X))(act_t, d_cols)


def _permute_w_in(w_nat, lay):
    d = lay.d
    group = 4

    def call(name, width, n_pieces, nat_piece, out_block0, prev):
        def body(*refs):
            refs[-1][...] = jnp.concatenate([r[...] for r in refs[:group]], axis=1)

        in_specs = [pl.BlockSpec((d, width), lambda s, m=m: (0, nat_piece(group * s + m))) for m in range(group)]
        args = [w_nat] * group
        aliases = {}
        if prev is not None:
            in_specs.append(pl.BlockSpec(memory_space=pl.ANY))
            args.append(prev)
            aliases = {group: 0}
        return pl.pallas_call(
            body, name=name, grid=(n_pieces // group,), in_specs=in_specs,
            out_specs=pl.BlockSpec((d, group * width), lambda s: (0, out_block0 + s)),
            out_shape=jax.ShapeDtypeStruct((d, lay.np), BF16), input_output_aliases=aliases,
            compiler_params=_params("arbitrary"))(*args)

    w_all = call("permute_w_attn", SLAB, 2 * PAIR_SLABS, lay.attn_nat_slab, 0, None)
    n_rest = 6 * d // CONV_TILE
    if n_rest % group:
        group = 2
    return call("permute_w_rest", CONV_TILE, n_rest, lay.rest_nat_tile, lay.c0 // (group * CONV_TILE), w_all)


def _project(x2, mod3, w_all, b_all, seq, col0, ncols, tn, out_dtype, want_ht, name):
    t, d = x2.shape
    tm = min(1024, seq)
    per_seq = seq // tm
    j0 = col0 // tn

    def body(x_ref, mod_ref, w_ref, b_ref, o_ref, *rest):
        h_ref = rest[-1]

        @pl.when(pl.program_id(1) == 0)
        def _():
            h = x_ref[...] * (1.0 + mod_ref[:, d:2 * d]) + mod_ref[:, 0:d]
            h_ref[...] = h.astype(BF16)
            if want_ht:
                rest[0][...] = h.T.astype(BF16)

        o_ref[...] = (jnp.dot(h_ref[...], w_ref[...], preferred_element_type=F32) + b_ref[...]).astype(out_dtype)

    out_shape = [jax.ShapeDtypeStruct((t, ncols), out_dtype)]
    out_specs = [pl.BlockSpec((tm, tn), lambda i, j: (i, j))]
    if want_ht:
        out_shape.append(jax.ShapeDtypeStruct((d, t), BF16))
        out_specs.append(pl.BlockSpec((d, tm), lambda i, j: (0, i)))
    return pl.pallas_call(
        body, name=name, grid=(t // tm, ncols // tn),
        in_specs=[pl.BlockSpec((tm, d), lambda i, j: (i, 0)),
                  pl.BlockSpec((None, 1, 3 * d), lambda i, j: (i // per_seq, 0, 0)),
                  pl.BlockSpec((d, tn), lambda i, j: (0, j0 + j)),
                  pl.BlockSpec((1, tn), lambda i, j: (0, j0 + j))],
        out_specs=out_specs, out_shape=out_shape,
        scratch_shapes=[pltpu.VMEM((tm, d), BF16)],
        compiler_params=_params("arbitrary", "arbitrary"))(x2, mod3, w_all, b_all)


def _slope(g, p, hh):
    head = 4 * g + 2 * p + hh
    return 2.0 ** (-ALIBI_MAX_EXP * (head + 1.0) / N_HEADS)


def _ld_rows(ref, start, n, stride):
    if stride == 1:
        return ref[pl.ds(start, n), :]
    return ref[pl.ds(start, n, stride=stride), :]


def _st_rows(ref, start, n, stride, val):
    if stride == 1:
        ref[pl.ds(start, n), :] = val
    else:
        ref[pl.ds(start, n, stride=stride), :] = val


def _sub_blocks(g, seq):
    return seq // DILATIONS[g] // SUB


def _key_rows(g, seq):
    return SUB if _sub_blocks(g, seq) == 1 else 2 * SUB


def _fill_bias(bias_ref, p, seq):
    for g in range(N_GROUPS):
        nk = _key_rows(g, seq)
        diff = lax.broadcasted_iota(jnp.int32, (SUB, nk), 0) - lax.broadcasted_iota(jnp.int32, (SUB, nk), 1)
        for i, off in enumerate((0, SUB)):
            if i == 1 and nk == SUB:
                continue
            delta = diff + off
            ok = (delta >= 0) & (delta <= SUB)
            dist = (delta * DILATIONS[g]).astype(F32)
            for hh in range(2):
                slope = jnp.where(p == 0, _slope(g, 0, hh), _slope(g, 1, hh))
                bias_ref[g, i, hh, :, 0:nk] = jnp.where(ok, -slope * dist, NEG)


def _to_sub_major(pa_ref, col, sub_ref, stage, dil, seq):
    cols = slice(col * SLAB, (col + 1) * SLAB)
    if dil == 1:
        sub_ref[...] = pa_ref[:, cols]
        return
    n = seq // dil
    stage[...] = pa_ref[:, cols].astype(F32)
    for r in range(dil):
        sub_ref[pl.ds(r * n, n), :] = stage[pl.ds(r, n, stride=dil), :].astype(BF16)


def _block_rows(it, g, seq):
    dil, nb = DILATIONS[g], _sub_blocks(g, seq)
    row0 = pl.multiple_of(it * SUB, SUB)
    if nb == 1:
        return row0, row0, 0, it
    blk = it % nb
    first = blk == 0
    krow0 = pl.multiple_of(row0 - jnp.where(first, 0, SUB), SUB)
    nat = row0 if dil == 1 else it // nb + dil * SUB * blk
    return row0, krow0, jnp.where(first, 0, 1), nat


def _nt(a, b):
    return lax.dot_general(a, b, (((1,), (1,)), ((), ())), preferred_element_type=F32)


def _tn(a, b):
    return lax.dot_general(a, b, (((0,), (0,)), ((), ())), preferred_element_type=F32)


def _head_sums(t):
    rows = t.shape[0]
    lo = jnp.broadcast_to(jnp.sum(t[:, :HEAD_DIM], axis=-1, keepdims=True), (rows, HEAD_DIM))
    hi = jnp.broadcast_to(jnp.sum(t[:, HEAD_DIM:], axis=-1, keepdims=True), (rows, HEAD_DIM))
    return jnp.concatenate([lo, hi], axis=-1)


def _attn_fwd(pa, bsz, seq):
    t = pa.shape[0]
    n_blocks = seq // SUB
    chunk = 256

    def body(pa_ref, o_ref, lse_ref, a_ref, sub, stage, bias_ref, s_buf, p_buf, l_buf):
        p = pl.program_id(1)
        _fill_bias(bias_ref, p, seq)
        head0 = lax.broadcasted_iota(jnp.int32, (SUB, SLAB), 1) < HEAD_DIM
        for g in range(N_GROUPS):
            dil = DILATIONS[g]
            for w in range(3):
                _to_sub_major(pa_ref, 3 * w + g, sub.at[w], stage, dil, seq)
            nk = _key_rows(g, seq)

            def trip(i, carry, g=g, dil=dil, nk=nk):
                places = [_block_rows(BLOCKS_PER_TRIP * i + j, g, seq) for j in range(BLOCKS_PER_TRIP)]
                for j, (row0, krow0, _, _) in enumerate(places):
                    q = sub[0, pl.ds(row0, SUB), :]
                    zero = jnp.zeros_like(q)
                    q2 = jnp.concatenate([jnp.where(head0, q, zero), jnp.where(head0, zero, q)], axis=0) * (HEAD_DIM ** -0.5)
                    s_buf[j, :, 0:nk] = _nt(q2, sub[1, pl.ds(krow0, nk), :])
                for j, (_, _, bi, _) in enumerate(places):
                    for c in range(0, 2 * SUB, SOFTMAX_ROWS):
                        hh, r = divmod(c, SUB)
                        s = s_buf[j, c:c + SOFTMAX_ROWS, 0:nk] + bias_ref[g, bi, hh, r:r + SOFTMAX_ROWS, 0:nk]
                        m = jnp.max(s, axis=-1, keepdims=True)
                        e = jnp.exp(s - m)
                        den = jnp.sum(e, axis=-1, keepdims=True)
                        p_buf[j, c:c + SOFTMAX_ROWS, 0:nk] = (e * (1.0 / den)).astype(BF16)
                        l_buf[j, c:c + SOFTMAX_ROWS, :] = jnp.broadcast_to(m + jnp.log(den), (SOFTMAX_ROWS, SLAB))
                for j, (_, krow0, _, nat) in enumerate(places):
                    o2 = jnp.dot(p_buf[j, :, 0:nk], sub[2, pl.ds(krow0, nk), :], preferred_element_type=F32)
                    _st_rows(o_ref.at[g], nat, SUB, dil, jnp.where(head0, o2[0:SUB], o2[SUB:2 * SUB]))
                    _st_rows(lse_ref.at[g], nat, SUB, dil, jnp.where(head0, l_buf[j, 0:SUB, :], l_buf[j, SUB:2 * SUB, :]))
                return carry

            lax.fori_loop(0, n_blocks // BLOCKS_PER_TRIP, trip, 0)

        def mix(i, carry):
            rows = pl.ds(pl.multiple_of(i * chunk, chunk), chunk)
            l0, l1, l2 = lse_ref[0, rows, :], lse_ref[1, rows, :], lse_ref[2, rows, :]
            m = jnp.maximum(jnp.maximum(l0, l1), l2)
            e0, e1, e2 = jnp.exp(l0 - m), jnp.exp(l1 - m), jnp.exp(l2 - m)
            tot = e0 + e1 + e2
            o = (e0 / tot) * o_ref[0, rows, :] + (e1 / tot) * o_ref[1, rows, :] + (e2 / tot) * o_ref[2, rows, :]
            z = pa_ref[rows, 9 * SLAB:10 * SLAB].astype(F32)
            a_ref[rows, :] = (o * (z * _sigmoid(z))).astype(BF16)
            return carry

        lax.fori_loop(0, seq // chunk, mix, 0)

    big = jax.ShapeDtypeStruct((N_GROUPS, t, 2 * SLAB), F32)
    return pl.pallas_call(
        body, name="attn_fwd", grid=(bsz, 2),
        in_specs=[pl.BlockSpec((seq, PAIR_COLS), lambda b, p: (b, p))],
        out_specs=[pl.BlockSpec((N_GROUPS, seq, SLAB), lambda b, p: (0, b, p)),
                   pl.BlockSpec((N_GROUPS, seq, SLAB), lambda b, p: (0, b, p)),
                   pl.BlockSpec((seq, SLAB), lambda b, p: (b, p))],
        out_shape=[big, big, jax.ShapeDtypeStruct((t, 2 * SLAB), BF16)],
        scratch_shapes=[pltpu.VMEM((3, seq, SLAB), BF16), pltpu.VMEM((seq, SLAB), F32),
                        pltpu.VMEM((N_GROUPS, 2, 2, SUB, 2 * SUB), F32), pltpu.VMEM((BLOCKS_PER_TRIP, 2 * SUB, 2 * SUB), F32),
                        pltpu.VMEM((BLOCKS_PER_TRIP, 2 * SUB, 2 * SUB), BF16), pltpu.VMEM((BLOCKS_PER_TRIP, 2 * SUB, SLAB), F32)],
        compiler_params=_params("arbitrary", "arbitrary"))(pa)


def _shift_down(v, k, rows):
    return jnp.where(rows >= k, pltpu.roll(v, k, 0), 0.0)


def _shift_up(v, k, rows):
    n = v.shape[0]
    return jnp.where(rows < n - k, pltpu.roll(v, n - k, 0), 0.0)


def _conv_fwd(pr, conv_w, bsz, seq, d):
    t = pr.shape[0]
    ct = CONV_TILE

    def body(p_ref, cw_ref, o_ref):
        u = p_ref[:, 2 * ct:3 * ct].astype(F32) * p_ref[:, 0:ct].astype(F32)
        cw = cw_ref[...]
        rows = lax.broadcasted_iota(jnp.int32, u.shape, 0)
        conv = cw[0:1, :] * _shift_down(u, 2, rows)
        conv = conv + cw[1:2, :] * _shift_down(u, 1, rows)
        conv = conv + cw[2:3, :] * u
        z = p_ref[:, 3 * ct:4 * ct].astype(F32)
        o_ref[...] = (p_ref[:, ct:2 * ct].astype(F32) * conv * (z * _sigmoid(z))).astype(BF16)

    return pl.pallas_call(
        body, name="conv_fwd", grid=(bsz, d // ct),
        in_specs=[pl.BlockSpec((seq, 4 * ct), lambda b, j: (b, j)), pl.BlockSpec((3, ct), lambda b, j: (0, j))],
        out_specs=pl.BlockSpec((seq, ct), lambda b, j: (b, j)),
        out_shape=jax.ShapeDtypeStruct((t, d), BF16), compiler_params=_params("parallel", "parallel"))(pr, conv_w)


def _tail(a_in, b_in, pr, x2, target2, mod3, w_pa, w_pc, w_out, b_out, ln_g, ln_b, seq, lay):
    t, d = x2.shape
    tm = 512
    per_seq = seq // tm
    n_steps = t // tm
    gate_blk = 4 * d // d

    def nt(a, b):
        return lax.dot_general(a, b, (((1,), (1,)), ((), ())), preferred_element_type=F32)

    def tn(a, b):
        return lax.dot_general(a, b, (((0,), (0,)), ((), ())), preferred_element_type=F32)

    def body(a_ref, b_ref, ga_ref, gb_ref, x_ref, tg_ref, mod_ref, wpa_ref, wpc_ref, wo_ref, bo_ref, lg_ref, lb_ref,
             dpg_ref, da_ref, db_ref, gx_ref, dgate_ref, small_ref, gwpa_hbm, gwpc_hbm, gwo_hbm,
             acc_pa, acc_pc, acc_o, sem):
        i = pl.program_id(0)

        @pl.when(i == 0)
        def _():
            acc_pa[...] = jnp.zeros_like(acc_pa)
            acc_pc[...] = jnp.zeros_like(acc_pc)
            acc_o[...] = jnp.zeros_like(acc_o)
            small_ref[...] = jnp.zeros_like(small_ref)

        @pl.when(i % per_seq == 0)
        def _():
            dgate_ref[...] = jnp.zeros_like(dgate_ref)

        halves = [slice(k * (tm // 2), (k + 1) * (tm // 2)) for k in range(2)]
        gate = mod_ref[:, 2 * d:3 * d]
        a_bf = [a_ref[rs, :] for rs in halves]
        b_bf = [b_ref[rs, :] for rs in halves]
        y_attn = [jnp.dot(a, wpa_ref[...], preferred_element_type=F32) for a in a_bf]
        y_conv = [jnp.dot(b, wpc_ref[...], preferred_element_type=F32) for b in b_bf]
        sa = [_sigmoid(ga_ref[rs, :].astype(F32)) for rs in halves]
        sb = [_sigmoid(gb_ref[rs, :].astype(F32)) for rs in halves]
        merged = [(sa[k] * y_attn[k] + sb[k] * y_conv[k]).astype(BF16) for k in range(2)]
        mo = [jnp.dot(m, wo_ref[...], preferred_element_type=F32) + bo_ref[...] for m in merged]
        d_mo_bf = []
        for k, rs in enumerate(halves):
            r = ALPHA * x_ref[rs, :] + gate * mo[k]
            mu = jnp.mean(r, axis=-1, keepdims=True)
            cen = r - mu
            var = jnp.mean(cen * cen, axis=-1, keepdims=True)
            rstd = lax.rsqrt(var + LN_EPS)
            xhat = cen * rstd
            err = xhat * lg_ref[...] + lb_ref[...] - tg_ref[rs, :]
            dy = err * (1.0 / d)
            dxhat = dy * lg_ref[...]
            dr = rstd * (dxhat - jnp.mean(dxhat, axis=-1, keepdims=True)
                         - xhat * jnp.mean(dxhat * xhat, axis=-1, keepdims=True))
            gx_ref[rs, :] = ALPHA * dr
            dgate_ref[...] += jnp.sum(dr * mo[k], axis=0, keepdims=True)
            d_mo = dr * gate
            small_ref[0:1, :] += jnp.sum(d_mo, axis=0, keepdims=True)
            small_ref[1:2, :] += jnp.sum(dy * xhat, axis=0, keepdims=True)
            small_ref[2:3, :] += jnp.sum(dy, axis=0, keepdims=True)
            small_ref[3:4, :] += jnp.sum(err * err, axis=0, keepdims=True)
            d_mo_bf.append(d_mo.astype(BF16))
        for k in range(2):
            acc_o[...] += tn(merged[k], d_mo_bf[k])
        dmerged = [nt(g, wo_ref[...]) for g in d_mo_bf]
        dy_attn, dy_conv = [], []
        for k, rs in enumerate(halves):
            dy_attn.append((dmerged[k] * sa[k]).astype(BF16))
            dy_conv.append((dmerged[k] * sb[k]).astype(BF16))
            dpg_ref[rs, 0:d] = (dmerged[k] * y_attn[k] * sa[k] * (1.0 - sa[k])).astype(BF16)
            dpg_ref[rs, d:2 * d] = (dmerged[k] * y_conv[k] * sb[k] * (1.0 - sb[k])).astype(BF16)
        for k, rs in enumerate(halves):
            acc_pa[...] += tn(a_bf[k], dy_attn[k])
            acc_pc[...] += tn(b_bf[k], dy_conv[k])
            da_ref[rs, :] = nt(dy_attn[k], wpa_ref[...])
            db_ref[rs, :] = nt(dy_conv[k], wpc_ref[...])

        @pl.when(i == n_steps - 1)
        def _():
            copies = [pltpu.make_async_copy(acc_pa, gwpa_hbm, sem.at[0]), pltpu.make_async_copy(acc_pc, gwpc_hbm, sem.at[1]),
                      pltpu.make_async_copy(acc_o, gwo_hbm, sem.at[2])]
            for cp in copies:
                cp.start()
            for cp in copies:
                cp.wait()

    row = lambda w: pl.BlockSpec((tm, w), lambda i: (i, 0))
    const = lambda shp: pl.BlockSpec(shp, lambda i: (0,) * len(shp), pipeline_mode=pl.Buffered(1))
    any_spec = pl.BlockSpec(memory_space=pl.ANY)
    return pl.pallas_call(
        body, name="tail", grid=(n_steps,),
        in_specs=[row(Z_WIDTH), row(d),
                  pl.BlockSpec((tm, d), lambda i: (i, gate_blk)), pl.BlockSpec((tm, d), lambda i: (i, gate_blk + 1)),
                  row(d), row(d), pl.BlockSpec((None, 1, 3 * d), lambda i: (i // per_seq, 0, 0)),
                  const((Z_WIDTH, d)), const((d, d)), const((d, d)), const((1, d)), const((1, d)), const((1, d))],
        out_specs=[pl.BlockSpec((tm, 2 * d), lambda i: (i, lay.g0 // (2 * d))), row(Z_WIDTH), row(d), row(d),
                   pl.BlockSpec((None, 1, d), lambda i: (i // per_seq, 0, 0)), pl.BlockSpec((8, d), lambda i: (0, 0)),
                   HBM_SPEC, HBM_SPEC, HBM_SPEC],
        out_shape=[jax.ShapeDtypeStruct((t, lay.np), BF16), jax.ShapeDtypeStruct((t, Z_WIDTH), F32),
                   jax.ShapeDtypeStruct((t, d), F32), jax.ShapeDtypeStruct((t, d), F32),
                   jax.ShapeDtypeStruct((t // seq, 1, d), F32), jax.ShapeDtypeStruct((8, d), F32),
                   pltpu.HBM((Z_WIDTH, d), F32), pltpu.HBM((d, d), F32), pltpu.HBM((d, d), F32)],
        scratch_shapes=[pltpu.VMEM((Z_WIDTH, d), F32), pltpu.VMEM((d, d), F32), pltpu.VMEM((d, d), F32),
                        pltpu.SemaphoreType.DMA((3,))],
        compiler_params=_params("arbitrary"),
    )(a_in, b_in, pr, pr, x2, target2, mod3, w_pa, w_pc, w_out, b_out, ln_g, ln_b)


def _conv_bwd(dproj, db, pr, conv_w, bsz, seq, lay):
    d = lay.d
    ct = CONV_TILE
    base = lay.c0 // (4 * ct)

    def body(dp_in, db_ref, p_ref, cw_ref, dp_ref, gcw_ref):
        del dp_in
        u_x, g_b, g_c, z = [p_ref[:, k * ct:(k + 1) * ct].astype(F32) for k in range(4)]
        cw = cw_ref[...]
        u = g_c * u_x
        rows = lax.broadcasted_iota(jnp.int32, u.shape, 0)
        u1, u2 = _shift_down(u, 1, rows), _shift_down(u, 2, rows)
        conv = cw[0:1, :] * u2 + cw[1:2, :] * u1 + cw[2:3, :] * u
        sig = _sigmoid(z)
        sl = z * sig
        dbv = db_ref[...]
        gbc = g_b * conv
        dp_ref[:, ct:2 * ct] = (dbv * sl * conv).astype(BF16)
        dp_ref[:, 3 * ct:4 * ct] = (dbv * gbc * (sig * (1.0 + z * (1.0 - sig)))).astype(BF16)
        dconv = dbv * sl * g_b

        @pl.when(pl.program_id(1) == 0)
        def _():
            gcw_ref[...] = jnp.zeros_like(gcw_ref)

        gcw_ref[0:1, :] += jnp.sum(dconv * u2, axis=0, keepdims=True)
        gcw_ref[1:2, :] += jnp.sum(dconv * u1, axis=0, keepdims=True)
        gcw_ref[2:3, :] += jnp.sum(dconv * u, axis=0, keepdims=True)
        du = cw[2:3, :] * dconv + cw[1:2, :] * _shift_up(dconv, 1, rows) + cw[0:1, :] * _shift_up(dconv, 2, rows)
        dp_ref[:, 0:ct] = (du * g_c).astype(BF16)
        dp_ref[:, 2 * ct:3 * ct] = (du * u_x).astype(BF16)

    return pl.pallas_call(
        body, name="conv_bwd", grid=(d // ct, bsz),
        in_specs=[pl.BlockSpec(memory_space=pl.ANY), pl.BlockSpec((seq, ct), lambda j, b: (b, j)),
                  pl.BlockSpec((seq, 4 * ct), lambda j, b: (b, j)), pl.BlockSpec((3, ct), lambda j, b: (0, j))],
        out_specs=[pl.BlockSpec((seq, 4 * ct), lambda j, b: (b, base + j)), pl.BlockSpec((8, ct), lambda j, b: (0, j))],
        out_shape=[jax.ShapeDtypeStruct(dproj.shape, BF16), jax.ShapeDtypeStruct((8, d), F32)],
        input_output_aliases={0: 0}, compiler_params=_params("arbitrary", "arbitrary"))(dproj, db, pr, conv_w)


def _attn_bwd(dproj, pa, o_all, lse_all, da, bsz, seq, after):
    n_blocks = seq // SUB
    chunk = 256

    def body(dp_in, pa_ref, o_ref, lse_ref, da_ref, after_ref, dp_ref, sub, stage, dsub, dog, cvec, bias_ref,
             s_buf, dp_buf, ds_buf, pb_buf, q2_buf, do2_buf, l_buf, c_buf):
        del dp_in, after_ref
        p = pl.program_id(1)
        _fill_bias(bias_ref, p, seq)
        head0 = lax.broadcasted_iota(jnp.int32, (SUB, SLAB), 1) < HEAD_DIM

        def mix_bwd(i, carry):
            rows = pl.ds(pl.multiple_of(i * chunk, chunk), chunk)
            ls = [lse_ref[g, rows, :] for g in range(N_GROUPS)]
            os_ = [o_ref[g, rows, :] for g in range(N_GROUPS)]
            m = jnp.maximum(jnp.maximum(ls[0], ls[1]), ls[2])
            es = [jnp.exp(l - m) for l in ls]
            tot = es[0] + es[1] + es[2]
            ws = [e / tot for e in es]
            o = ws[0] * os_[0] + ws[1] * os_[1] + ws[2] * os_[2]
            z = pa_ref[rows, 9 * SLAB:10 * SLAB].astype(F32)
            sig = _sigmoid(z)
            dav = da_ref[rows, :]
            do = dav * (z * sig)
            dp_ref[rows, 9 * SLAB:10 * SLAB] = (dav * o * (sig * (1.0 + z * (1.0 - sig)))).astype(BF16)
            wsum = _head_sums(do * o)
            for g in range(N_GROUPS):
                dog[g, rows, :] = ws[g] * do
                cvec[g, rows, :] = -(ws[g] * wsum)
            return carry

        lax.fori_loop(0, seq // chunk, mix_bwd, 0)

        for g in range(N_GROUPS):
            dil = DILATIONS[g]
            for w in range(3):
                _to_sub_major(pa_ref, 3 * w + g, sub.at[w], stage, dil, seq)
            dsub[1] = jnp.zeros((seq, SLAB), F32)
            dsub[2] = jnp.zeros((seq, SLAB), F32)
            nk = _key_rows(g, seq)

            def trip(i, carry, g=g, dil=dil, nk=nk):
                places = [_block_rows(BLOCKS_PER_TRIP * i + j, g, seq) for j in range(BLOCKS_PER_TRIP)]
                for j, (row0, krow0, _, nat) in enumerate(places):
                    q = sub[0, pl.ds(row0, SUB), :]
                    do = _ld_rows(dog.at[g], nat, SUB, dil).astype(BF16)
                    zero = jnp.zeros_like(q)
                    q2 = jnp.concatenate([jnp.where(head0, q, zero), jnp.where(head0, zero, q)], axis=0)
                    do2 = jnp.concatenate([jnp.where(head0, do, zero), jnp.where(head0, zero, do)], axis=0)
                    q2_buf[j] = q2
                    do2_buf[j] = do2
                    s_buf[j, :, 0:nk] = _nt(q2 * (HEAD_DIM ** -0.5), sub[1, pl.ds(krow0, nk), :])
                    dp_buf[j, :, 0:nk] = _nt(do2, sub[2, pl.ds(krow0, nk), :])
                    l_buf[j] = _ld_rows(lse_ref.at[g], nat, SUB, dil)
                    c_buf[j] = _ld_rows(cvec.at[g], nat, SUB, dil)
                for j, (_, _, bi, _) in enumerate(places):
                    for c in range(0, 2 * SUB, SOFTMAX_ROWS):
                        hh, r = divmod(c, SUB)
                        lane = hh * HEAD_DIM
                        s = s_buf[j, c:c + SOFTMAX_ROWS, 0:nk] + bias_ref[g, bi, hh, r:r + SOFTMAX_ROWS, 0:nk]
                        prob = jnp.exp(s - l_buf[j, r:r + SOFTMAX_ROWS, lane:lane + 1])
                        dprob = dp_buf[j, c:c + SOFTMAX_ROWS, 0:nk] + c_buf[j, r:r + SOFTMAX_ROWS, lane:lane + 1]
                        ds_buf[j, c:c + SOFTMAX_ROWS, 0:nk] = (prob * dprob * (HEAD_DIM ** -0.5)).astype(BF16)
                        pb_buf[j, c:c + SOFTMAX_ROWS, 0:nk] = prob.astype(BF16)
                for j, (row0, krow0, _, _) in enumerate(places):
                    ds = ds_buf[j, :, 0:nk]
                    dq2 = jnp.dot(ds, sub[1, pl.ds(krow0, nk), :], preferred_element_type=F32)
                    dsub[0, pl.ds(row0, SUB), :] = jnp.where(head0, dq2[0:SUB], dq2[SUB:2 * SUB])
                    dsub[1, pl.ds(krow0, nk), :] += _tn(ds, q2_buf[j])
                    dsub[2, pl.ds(krow0, nk), :] += _tn(pb_buf[j, :, 0:nk], do2_buf[j])
                return carry

            lax.fori_loop(0, n_blocks // BLOCKS_PER_TRIP, trip, 0)
            for w in range(3):
                cols = slice((3 * w + g) * SLAB, (3 * w + g + 1) * SLAB)
                if dil == 1:
                    dp_ref[:, cols] = dsub[w].astype(BF16)
                else:
                    n = seq // dil
                    for r in range(dil):
                        stage[pl.ds(r, n, stride=dil), :] = dsub[w, pl.ds(r * n, n), :]
                    dp_ref[:, cols] = stage[...].astype(BF16)

    return pl.pallas_call(
        body, name="attn_bwd", grid=(bsz, 2),
        in_specs=[pl.BlockSpec(memory_space=pl.ANY), pl.BlockSpec((seq, PAIR_COLS), lambda b, p: (b, p)),
                  pl.BlockSpec((N_GROUPS, seq, SLAB), lambda b, p: (0, b, p)),
                  pl.BlockSpec((N_GROUPS, seq, SLAB), lambda b, p: (0, b, p)),
                  pl.BlockSpec((seq, SLAB), lambda b, p: (b, p)), pl.BlockSpec(memory_space=pl.ANY)],
        out_specs=pl.BlockSpec((seq, PAIR_COLS), lambda b, p: (b, p)),
        out_shape=jax.ShapeDtypeStruct(dproj.shape, BF16), input_output_aliases={0: 0},
        scratch_shapes=[pltpu.VMEM((3, seq, SLAB), BF16), pltpu.VMEM((seq, SLAB), F32), pltpu.VMEM((3, seq, SLAB), F32),
                        pltpu.VMEM((3, seq, SLAB), F32), pltpu.VMEM((3, seq, SLAB), F32),
                        pltpu.VMEM((N_GROUPS, 2, 2, SUB, 2 * SUB), F32),
                        pltpu.VMEM((BLOCKS_PER_TRIP, 2 * SUB, 2 * SUB), F32), pltpu.VMEM((BLOCKS_PER_TRIP, 2 * SUB, 2 * SUB), F32),
                        pltpu.VMEM((BLOCKS_PER_TRIP, 2 * SUB, 2 * SUB), BF16), pltpu.VMEM((BLOCKS_PER_TRIP, 2 * SUB, 2 * SUB), BF16),
                        pltpu.VMEM((BLOCKS_PER_TRIP, 2 * SUB, SLAB), BF16), pltpu.VMEM((BLOCKS_PER_TRIP, 2 * SUB, SLAB), BF16),
                        pltpu.VMEM((BLOCKS_PER_TRIP, SUB, SLAB), F32), pltpu.VMEM((BLOCKS_PER_TRIP, SUB, SLAB), F32)],
        compiler_params=_params("arbitrary", "arbitrary"))(dproj, pa, o_all, lse_all, da, after)


def _grad_h(dproj, w_all, gx0, x2, mod3, seq, lay):
    t, d = x2.shape
    tm, tn = min(512, seq), min(512, d)
    per_seq = seq // tm

    def body(dp_ref, w_ref, gx0_ref, x_ref, scale_ref, gx_ref, dmod_ref):
        dh = _nt(dp_ref[:, 0:ATT], w_ref[:, 0:ATT]) + _nt(dp_ref[:, lay.c0:], w_ref[:, lay.c0:])
        gx_ref[...] = gx0_ref[...] + dh * (1.0 + scale_ref[...])

        @pl.when(pl.program_id(1) % per_seq == 0)
        def _():
            dmod_ref[...] = jnp.zeros_like(dmod_ref)

        dmod_ref[0:1, :] += jnp.sum(dh, axis=0, keepdims=True)
        dmod_ref[1:2, :] += jnp.sum(dh * x_ref[...], axis=0, keepdims=True)

    tile = pl.BlockSpec((tm, tn), lambda j, i: (i, j))
    return pl.pallas_call(
        body, name="grad_h", grid=(d // tn, t // tm),
        in_specs=[pl.BlockSpec((tm, lay.np), lambda j, i: (i, 0)), pl.BlockSpec((tn, lay.np), lambda j, i: (j, 0)),
                  tile, tile,
                  pl.BlockSpec((None, 1, tn), lambda j, i: (i // per_seq, 0, d // tn + j))],
        out_specs=[tile, pl.BlockSpec((None, 8, tn), lambda j, i: (i // per_seq, 0, j))],
        out_shape=[jax.ShapeDtypeStruct((t, d), F32), jax.ShapeDtypeStruct((t // seq, 8, d), F32)],
        compiler_params=_params("arbitrary", "arbitrary"))(dproj, w_all, gx0, x2, mod3)


def _grad_w_in(ht, dproj, seq, lay):
    d, t = ht.shape
    tm = seq
    n_i = t // tm

    def make_body(n_skip, n_pieces, tn, nat_tile):
        def body(*refs):
            refs = refs[n_skip:]
            ht_ref, dp_refs = refs[0], refs[1:1 + n_pieces]
            gw_hbm, gb_hbm, acc, bacc, gw_out, gb_out, sem = refs[1 + n_pieces:]
            i, j = pl.program_id(0), pl.program_id(1)
            dp = dp_refs[0][...] if n_pieces == 1 else jnp.concatenate([r[...] for r in dp_refs], axis=1)
            part = jnp.dot(ht_ref[...], dp, preferred_element_type=F32)
            bpart = jnp.sum(dp.astype(F32), axis=0, keepdims=True)

            if n_i > 1:
                @pl.when(i == 0)
                def _():
                    acc[j] = part
                    bacc[j] = bpart

                @pl.when((i > 0) & (i < n_i - 1))
                def _():
                    acc[j] += part
                    bacc[j] += bpart

            @pl.when(i == n_i - 1)
            def _():
                gw_out[...] = ((part + acc[j]) if n_i > 1 else part).astype(BF16)
                gb_out[...] = (bpart + bacc[j]) if n_i > 1 else bpart
                cols = pl.ds(pl.multiple_of(nat_tile(j) * tn, SLAB), tn)
                copies = [pltpu.make_async_copy(gw_out, gw_hbm.at[:, cols], sem.at[0]),
                          pltpu.make_async_copy(gb_out, gb_hbm.at[:, cols], sem.at[1])]
                for cp in copies:
                    cp.start()
                for cp in copies:
                    cp.wait()
        return body

    def call(name, pieces, n_tiles, nat_tile, prev):
        tn = sum(w for w, _ in pieces)
        any_spec = pl.BlockSpec(memory_space=pl.ANY)
        in_specs = [pl.BlockSpec((d, tm), lambda i, j: (0, i))]
        in_specs += [pl.BlockSpec((tm, w), lambda i, j, f=f: (i, f(j))) for w, f in pieces]
        args = [ht] + [dproj] * len(pieces)
        aliases = {}
        if prev is not None:
            in_specs = [any_spec] * 2 + in_specs
            args = list(prev) + args
            aliases = {0: 0, 1: 1}
        return pl.pallas_call(
            make_body(0 if prev is None else 2, len(pieces), tn, nat_tile), name=name, grid=(n_i, n_tiles), in_specs=in_specs,
            out_specs=[any_spec, any_spec],
            out_shape=[jax.ShapeDtypeStruct((d, lay.din), BF16), jax.ShapeDtypeStruct((1, lay.din), F32)],
            input_output_aliases=aliases,
            scratch_shapes=[pltpu.VMEM((n_tiles, d, tn), F32), pltpu.VMEM((n_tiles, 1, tn), F32), pltpu.VMEM((d, tn), BF16),
                            pltpu.VMEM((1, tn), F32), pltpu.SemaphoreType.DMA((2,))],
            compiler_params=_params("arbitrary", "arbitrary"))(*args)

    attn_pieces = [(SLAB, lambda j, m=m: (m % 2) * PAIR_SLABS + 2 * j + m // 2) for m in range(4)]
    first = call("grad_w_in_attn", attn_pieces, ATT // 512, lambda j: j, None)
    base = lay.c0 // CONV_TILE
    nct = lay.n_conv_tiles
    if nct % 2:
        return call("grad_w_in_rest", [(CONV_TILE, lambda j: base + j)], 6 * d // CONV_TILE, lay.rest_nat_tile, first)
    half = nct // 2

    def rest_piece(m):
        def perm_tile(j):
            conv = base + 4 * (2 * (j % half) + m) + j // half
            return jnp.where(j < 4 * half, conv, base + 2 * j + m)
        return (CONV_TILE, perm_tile)

    return call("grad_w_in_rest", [rest_piece(0), rest_piece(1)], 6 * d // 512, lambda j: ATT // 512 + j, first)


def _pack_rows(parts, width=128):
    flat = [p.reshape(-1) for p in parts]
    spans, rows = [], 0
    padded = []
    for f in flat:
        n = -(-f.shape[0] // (8 * width)) * 8
        padded.append(jnp.pad(f, (0, n * width - f.shape[0])).reshape(n, width))
        spans.append((rows, f.shape[0]))
        rows += n
    return jnp.concatenate(padded, axis=0), spans


def _unpack_rows(packed, spans, shapes, width=128):
    out = []
    for (row, n), shp in zip(spans, shapes):
        rows = -(-n // width)
        out.append(packed[row:row + rows].reshape(-1)[:n].reshape(shp))
    return out


def kernel(x, c, w_ada, b_ada, w_in, b_in, conv_w, w_proj_attn, w_proj_conv, w_out, b_out, ln_g, ln_b, loss_target, m_w_ada, m_b_ada, m_w_in, m_b_in, m_conv_w, m_w_proj_attn, m_w_proj_conv, m_w_out, m_b_out, m_ln_g, m_ln_b, v_w_ada, v_b_ada, v_w_in, v_b_in, v_conv_w, v_w_proj_attn, v_w_proj_conv, v_w_out, v_b_out, v_ln_g, v_ln_b):
    bsz, seq, d = x.shape
    t = bsz * seq
    lay = _Layout(d)
    col_sharded = [True, True, False, False]
    red_w = [w_in[0], w_proj_attn[0], w_proj_conv[0], w_out[0]]
    chip = 2 * lax.axis_index("x") + lax.axis_index("y")
    chip1 = chip.astype(jnp.int32).reshape(1)
    core1 = lax.axis_index("c").astype(jnp.int32).reshape(1)
    place = jnp.stack([chip, lax.axis_index("c")]).astype(jnp.int32)
    x2 = x.reshape(t, d)
    target2 = loss_target.reshape(t, d)

    mod, act_all = _ada_exchange(c, w_ada[0], b_ada)
    mod3 = mod.reshape(bsz, 1, 3 * d)

    cw_pad = jnp.pad(conv_w[0], ((0, 5), (0, 0))) + 0.0 * mod[0, 0]
    own_in_full = [_cast_into_full(red_w[0], col_sharded[0], chip1, "cast_shard_in")]
    own_in_full += list(_cast_into_full_small(red_w[1:], col_sharded[1:], chip1, "cast_shards_late"))
    (wi_f,), cw8 = _gather_weights(own_in_full[:1], col_sharded[:1], cw_pad)
    cw_full = cw8[0:3]
    late_copies = _direct_gather_copies(col_sharded[1:])
    late_send, late_recv, late_flying, late_token = _start_copies("gather_late_start", own_in_full[1:], (18,), cw8, late_copies)
    w_all = _permute_w_in(wi_f, lay)
    b_all = lay.perm_vector(b_in) + late_token[0, 0]

    rest_tn = 1024 if (6 * d) % 1024 == 0 else 512
    pa, = _project(x2, mod3, w_all, b_all, seq, 0, ATT, PAIR_COLS, BF16, False, "project_attn")
    pr, ht = _project(x2, mod3, w_all, b_all, seq, lay.c0, 6 * d, rest_tn, BF16, True, "project_rest")
    o_all, lse_all, a_in = _attn_fwd(pa, bsz, seq)
    b_in_act = _conv_fwd(pr, cw_full, bsz, seq, d)
    wpa_f, wpc_f, wo_f = _wait_copies("gather_late_wait", late_flying, late_send, late_recv, b_in_act, late_copies)
    (dproj, da_in, db_in, gx0, dgate, small_tail, gw_pa, gw_pc, gw_out) = _tail(
        a_in, b_in_act, pr, x2, target2, mod3, wpa_f, wpc_f, wo_f, b_out, ln_g, ln_b, seq, lay)

    late_views = _shard_views([gw_pa, gw_pc, gw_out], col_sharded[1:])
    late_lands = [lax.empty((v.shape[0], v.shape[1] // 2, v.shape[2]), v.dtype) for v in late_views]
    xl_send, xl_recv, xl_fly, xl_tok = _start_copies("grads_pair_exchange_late_start", late_views + late_lands, (3,), gw_out,
                                                     _pair_exchange_copies(3, False))
    dproj, gcw = _conv_bwd(dproj, db_in, pr, cw_full + xl_tok[0, 0], bsz, seq, lay)
    xl_done = _wait_copies("grads_pair_exchange_late_wait", xl_fly, xl_send, xl_recv, gcw, _pair_exchange_copies(3, True))
    late_parts = list(_pair_sum_small(xl_done[:3], xl_done[3:], core1, "grads_pair_sum_late"))

    late_cross = _chip_scatter_copies(3, col_sharded[1:])
    late_zone = [lax.empty((3, p.shape[1], _piece_cols(p, cs)), p.dtype) for p, cs in zip(late_parts, col_sharded[1:])]
    sl_send, sl_recv, sl_fly, sl_tok = _start_copies("grads_scatter_late_start", late_parts + late_zone, (9,), late_parts[0],
                                                     late_cross)
    dproj = _attn_bwd(dproj, pa, o_all, lse_all, da_in, bsz, seq, sl_tok)
    gw_in_bf, gb_in = _grad_w_in(ht, dproj, seq, lay)
    sl_done = _wait_copies("grads_scatter_late_wait", sl_fly, sl_send, sl_recv, gw_in_bf, late_cross)
    late_red = list(_chip_sum_small(sl_done[:3], sl_done[3:], col_sharded[1:], place, "grads_chip_sum_late"))

    in_view = _shard_views([gw_in_bf], col_sharded[:1])
    in_got = _pair_exchange_halves(in_view, "grads_pair_exchange_in")
    in_part = _pair_sum(in_view[0], in_got[0], core1, "grads_pair_sum_in")
    in_cross = _chip_scatter_copies(1, col_sharded[:1])
    in_zone = [lax.empty((3, in_part.shape[1], _piece_cols(in_part, True)), in_part.dtype)]
    late_views1 = [f.reshape(1, *f.shape) for f in late_red]
    si_send, si_recv, si_fly, si_tok = _start_copies(
        "grads_scatter_in_join_late_start", [in_part] + in_zone + late_views1, (6,), late_red[0],
        _both_copies(in_cross, 2, _pair_join_copies(3, False, sem0=3)))
    grad_x2, dmod = _grad_h(dproj, w_all, gx0, x2, mod3 + si_tok[0, 0], seq, lay)
    si_done = _wait_copies("grads_scatter_in_join_late_wait", si_fly, si_send, si_recv, grad_x2,
                           _both_copies(in_cross, 2, _pair_join_copies(3, True, sem0=3)))
    late_joined = [f[0] for f in si_done[2:]]
    in_red = _chip_sum(si_done[0], si_done[1], True, place, "grads_chip_sum_in")

    d_ada = jnp.concatenate([dmod[:, 0, :], dmod[:, 1, :], dgate[:, 0, :]], axis=1)
    pieces = [small_tail[3], jnp.sum(d_ada, axis=0), gb_in[0], small_tail[0], small_tail[1], small_tail[2], gcw[0:3]]
    packed, spans = _pack_rows(pieces)
    kept_in, _ = _pack_rows([d_ada])
    rows_all = jnp.concatenate([packed, kept_in], axis=0)
    small_land = lax.empty((N_DEV,) + rows_all.shape, F32)
    sm_send, sm_recv, sm_fly, sm_tok = _start_copies(
        "small_gather_join_in_start", [rows_all, small_land, in_red.reshape(1, *in_red.shape)], (N_DEV,), in_red,
        _both_copies(_small_gather_copies, 2, _pair_join_copies(1, False, sem0=N_DEV - 1)))
    big_w = [w_ada[0]] + red_w
    big_m = [m_w_ada[0], m_w_in[0], m_w_proj_attn[0], m_w_proj_conv[0], m_w_out[0]]
    big_v = [v_w_ada[0], v_w_in[0], v_w_proj_attn[0], v_w_proj_conv[0], v_w_out[0]]
    big_out = [None] * 5
    for w in range(3):
        big_out[2 + w] = _adamw(big_w[2 + w], late_joined[w], big_m[2 + w], big_v[2 + w], f"adamw_{2 + w}", sm_tok)
    sm_done = _wait_copies("small_gather_join_in_wait", sm_fly, sm_send, sm_recv, big_out[4][0],
                           _both_copies(_small_gather_copies, 2, _pair_join_copies(1, True, sem0=N_DEV - 1)))
    me1 = (4 * lax.axis_index("x") + 2 * lax.axis_index("y") + lax.axis_index("c")).astype(jnp.int32).reshape(1)
    summed, kept = _small_sum(sm_done[0], sm_done[1], me1, packed.shape[0], d)
    loss = summed[0, 0]
    _, g_b_ada, g_b_in, g_b_out, g_ln_g, g_ln_b, g_cw_full = _unpack_rows(
        summed, spans, [(d,), (3 * d,), (lay.din,), (d,), (d,), (d,), (3, d)])
    g_cw = lax.dynamic_slice(g_cw_full, (0, chip * (d // N_CHIPS)), (3, d // N_CHIPS))
    d_ada_all = kept.reshape(N_DEV, -1)[:, :bsz * 3 * d].reshape(N_DEV * bsz, 3 * d)
    ada_cols = 3 * d // N_CHIPS
    g_w_ada = _grad_w_ada_cols(act_all.T, lax.dynamic_slice(d_ada_all, (0, chip * ada_cols), (N_DEV * bsz, ada_cols)))

    big_out[0] = _adamw(big_w[0], g_w_ada, big_m[0], big_v[0], "adamw_0", None)
    small_w = [b_ada, b_in, conv_w[0], b_out, ln_g, ln_b]
    small_g = [g_b_ada, g_b_in, g_cw, g_b_out, g_ln_g, g_ln_b]
    small_m = [m_b_ada, m_b_in, m_conv_w[0], m_b_out, m_ln_g, m_ln_b]
    small_v = [v_b_ada, v_b_in, v_conv_w[0], v_b_out, v_ln_g, v_ln_b]
    pw, sp = _pack_rows(small_w)
    pg, _ = _pack_rows(small_g)
    pm, _ = _pack_rows(small_m)
    pv, _ = _pack_rows(small_v)
    sd, sm, sv = _adamw(pw, pg, pm, pv, "adamw_small", None)
    big_out[1] = _adamw(big_w[1], sm_done[2][0], big_m[1], big_v[1], "adamw_1", None, also_g=True)
    g_big = [g_w_ada, big_out[1][3]] + late_joined
    shapes = [a.shape for a in small_w]
    sd, sm, sv = _unpack_rows(sd, sp, shapes), _unpack_rows(sm, sp, shapes), _unpack_rows(sv, sp, shapes)

    def order(wa, bA, wi, bI, cw, wpa, wpc, wo, bO, lg, lb):
        return (wa[None], bA, wi[None], bI, cw[None], wpa[None], wpc[None], wo[None], bO, lg, lb)

    sg = [g.reshape(s) for g, s in zip(small_g, shapes)]
    grads_out = order(g_big[0], sg[0], g_big[1], sg[1], sg[2], g_big[2], g_big[3], g_big[4], sg[3], sg[4], sg[5])
    outs = []
    for idx, small in enumerate((sd, sm, sv)):
        outs.append(order(big_out[0][idx], small[0], big_out[1][idx], small[1], small[2], big_out[2][idx],
                          big_out[3][idx], big_out[4][idx], small[3], small[4], small[5]))
    return (loss, grad_x2.reshape(bsz, seq, d), *grads_out, *outs[0], *outs[1], *outs[2])
```

```python
import functools

import jax
import jax.numpy as jnp
from jax import lax
from jax.experimental import pallas as pl
from jax.experimental.pallas import tpu as pltpu

F32 = jnp.float32
BF16 = jnp.bfloat16
MESH = pl.DeviceIdType.MESH

HEAD_DIM = 64
N_GROUPS = 3
DILATIONS = (1, 4, 16)
N_HEADS = 12
SUB = 128
Q_WIDTH = 768
Z_WIDTH = 256
ATT = 3 * Q_WIDTH + Z_WIDTH
SLAB = 128
PAIR_SLABS = 10
PAIR_COLS = PAIR_SLABS * SLAB
CONV_TILE = 256
SOFTMAX_ROWS = 32
BLOCKS_PER_TRIP = 8
ALIBI_MAX_EXP = 8.0
ALPHA = 2.0 ** 0.25
LN_EPS = 1e-5
ADAM_LR, ADAM_B1, ADAM_B2, ADAM_EPS, ADAM_WD, ADAM_STEP = 0.001, 0.9, 0.999, 1e-08, 0.01, 10
N_CHIPS = 4
N_DEV = 8
VMEM_LIMIT_V7X = 60 * 1024 * 1024
NEG = -1e30


def _params(*sem):
    return pltpu.CompilerParams(dimension_semantics=sem, vmem_limit_bytes=VMEM_LIMIT_V7X)


def _sigmoid(v):
    return 0.5 * jnp.tanh(0.5 * v) + 0.5


class _Layout:
    def __init__(self, d):
        self.d = d
        self.din = ATT + 6 * d
        c0 = 3072
        while c0 % (2 * d):
            c0 += 1024
        self.c0, self.g0, self.np = c0, c0 + 4 * d, c0 + 6 * d
        self.n_conv_tiles = d // CONV_TILE

    def attn_nat_slab(self, s):
        p, i = s // PAIR_SLABS, s % PAIR_SLABS
        return jnp.where(i < 9, (i // 3) * 6 + (i % 3) * 2 + p, 18 + p)

    def rest_nat_tile(self, t):
        n4 = 4 * self.n_conv_tiles
        conv = ATT // CONV_TILE + (t % 4) * self.n_conv_tiles + t // 4
        return jnp.where(t < n4, conv, ATT // CONV_TILE + t)

    def perm_vector(self, v):
        parts = []
        for s in range(2 * PAIR_SLABS):
            p, i = divmod(s, PAIR_SLABS)
            ns = (i // 3) * 6 + (i % 3) * 2 + p if i < 9 else 18 + p
            parts.append(v[:, ns * SLAB:(ns + 1) * SLAB])
        parts.append(jnp.zeros((1, self.c0 - ATT), v.dtype))
        for j in range(self.n_conv_tiles):
            for k in range(4):
                a = ATT + k * self.d + j * CONV_TILE
                parts.append(v[:, a:a + CONV_TILE])
        parts.append(v[:, ATT + 4 * self.d:])
        return jnp.concatenate(parts, axis=1)


def _place():
    return lax.axis_index("x"), lax.axis_index("y"), lax.axis_index("c")


def _other_chips(x, y):
    return [(1 - x, y), (x, 1 - y), (1 - x, 1 - y)]


def _shard_of(ref, col_sharded, chip, half=None):
    if col_sharded:
        cs = ref.shape[1] // N_CHIPS
        cols = pl.ds(pl.multiple_of(chip * cs, SLAB), cs)
        if half is None:
            return ref.at[:, cols]
        n = ref.shape[0] // 2
        return ref.at[pl.ds(half * n, n), cols]
    rs = ref.shape[0] // N_CHIPS
    if half is None:
        return ref.at[pl.ds(chip * rs, rs)]
    return ref.at[pl.ds(chip * rs + half * (rs // 2), rs // 2)]


GATHER_PIECES = 4


def _cast_into_full(shard, col_sharded, chip, name):
    rows, cols = shard.shape
    tr = _row_tile(rows, cols)
    nb = rows // tr

    def body(chip_ref, s_ref, o_ref):
        del chip_ref
        o_ref[...] = s_ref[...].astype(BF16)

    if col_sharded:
        full, out_spec = (rows, cols * N_CHIPS), pl.BlockSpec((tr, cols), lambda i, ch: (i, ch[0]))
    else:
        full, out_spec = (rows * N_CHIPS, cols), pl.BlockSpec((tr, cols), lambda i, ch: (ch[0] * nb + i, 0))
    return pl.pallas_call(
        body, name=name,
        grid_spec=pltpu.PrefetchScalarGridSpec(num_scalar_prefetch=1, grid=(nb,),
                                               in_specs=[pl.BlockSpec((tr, cols), lambda i, ch: (i, 0))], out_specs=out_spec),
        out_shape=jax.ShapeDtypeStruct(full, BF16), compiler_params=_params("parallel"))(chip, shard)


def _cast_into_full_small(shards, col_sharded, chip, name):
    n = len(shards)

    def body(chip_ref, *refs):
        del chip_ref
        for w in range(n):
            refs[n + w][...] = refs[w][...].astype(BF16)

    in_specs = [pl.BlockSpec(s.shape, lambda i, ch: (0, 0)) for s in shards]
    out_specs = [pl.BlockSpec(s.shape, (lambda i, ch: (0, ch[0])) if cs else (lambda i, ch: (ch[0], 0)))
                 for s, cs in zip(shards, col_sharded)]
    fulls = [(s.shape[0], s.shape[1] * N_CHIPS) if cs else (s.shape[0] * N_CHIPS, s.shape[1]) for s, cs in zip(shards, col_sharded)]
    return pl.pallas_call(
        body, name=name,
        grid_spec=pltpu.PrefetchScalarGridSpec(num_scalar_prefetch=1, grid=(1,), in_specs=in_specs, out_specs=out_specs),
        out_shape=[jax.ShapeDtypeStruct(f, BF16) for f in fulls], compiler_params=_params("arbitrary"))(chip, *shards)


def _gather_weights(fulls, col_sharded, small):
    n = len(fulls)
    kp = GATHER_PIECES

    def piece(ref, cs, chip, half, k):
        if cs:
            width = ref.shape[1] // N_CHIPS
            rows = ref.shape[0] // 2 // kp
            return ref.at[pl.ds(half * (ref.shape[0] // 2) + k * rows, rows), pl.ds(pl.multiple_of(chip * width, SLAB), width)]
        rs = ref.shape[0] // N_CHIPS
        rows = rs // 2 // kp
        return ref.at[pl.ds(chip * rs + half * (rs // 2) + k * rows, rows)]

    def body(*refs):
        sm_in, outs, sm_out = refs[n], refs[n + 1:2 * n + 1], refs[2 * n + 1]
        send, recv, fsend, frecv, lsem, ssend, srecv = refs[2 * n + 2:]
        x, y, c = _place()
        mine = 2 * x + y
        sibling = (x, y, 1 - c)
        first = (x ^ (1 - c), y ^ c)
        second = (x ^ c, y ^ (1 - c))
        diagonal = (1 - x, 1 - y)
        sources = [first, second, diagonal]
        senders = [first, second, second]

        def copy(ref, sems, slot, to):
            return pltpu.make_async_remote_copy(src_ref=ref, dst_ref=ref, send_sem=sems[0].at[slot], recv_sem=sems[1].at[slot],
                                                device_id=to, device_id_type=MESH)

        local = pltpu.make_async_copy(sm_in, _shard_of(sm_out, True, mine), lsem)
        local.start()
        sends = []
        for k, (cx, cy) in enumerate(_other_chips(x, y)):
            cp = pltpu.make_async_remote_copy(src_ref=sm_in, dst_ref=_shard_of(sm_out, True, mine), send_sem=ssend.at[k],
                                              recv_sem=srecv.at[k], device_id=(cx, cy, c), device_id_type=MESH)
            cp.start()
            sends.append(cp)
        for k in range(kp):
            for w in range(n):
                own = piece(outs[w], col_sharded[w], mine, c, k)
                for slot, chip in enumerate((first, second)):
                    cp = copy(own, (send, recv), (w * 3 + slot) * kp + k, (*chip, c))
                    cp.start()
                    sends.append(cp)
        for slot in range(3):
            source = 2 * sources[slot][0] + sources[slot][1]
            for k in range(kp):
                for w in range(n):
                    landed = piece(outs[w], col_sharded[w], source, c, k)
                    copy(landed, (send, recv), (w * 3 + slot) * kp + k, (*senders[slot], c)).wait_recv()
                    if slot == 0:
                        cp = copy(landed, (send, recv), (w * 3 + 2) * kp + k, (*second, c))
                        cp.start()
                        sends.append(cp)
                    cp = copy(landed, (fsend, frecv), (w * 3 + slot) * kp + k, sibling)
                    cp.start()
                    sends.append(cp)
        for slot, chip in enumerate((second, first, diagonal)):
            for k in range(kp):
                for w in range(n):
                    passed = piece(outs[w], col_sharded[w], 2 * chip[0] + chip[1], 1 - c, k)
                    copy(passed, (fsend, frecv), (w * 3 + slot) * kp + k, sibling).wait_recv()
        for k, (cx, cy) in enumerate(_other_chips(x, y)):
            theirs = _shard_of(sm_out, True, 2 * cx + cy)
            pltpu.make_async_remote_copy(src_ref=theirs, dst_ref=theirs, send_sem=ssend.at[k], recv_sem=srecv.at[k],
                                         device_id=(cx, cy, c), device_id_type=MESH).wait_recv()
        for cp in sends:
            cp.wait_send()
        local.wait()

    any_spec = pl.BlockSpec(memory_space=pl.ANY)
    outs = pl.pallas_call(
        body, name="gather_weights",
        out_shape=[jax.ShapeDtypeStruct(f.shape, BF16) for f in fulls]
        + [jax.ShapeDtypeStruct((small.shape[0], small.shape[1] * N_CHIPS), small.dtype)],
        in_specs=[any_spec] * (n + 1), out_specs=[any_spec] * (n + 1), input_output_aliases={w: w for w in range(n)},
        scratch_shapes=[pltpu.SemaphoreType.DMA((n * 3 * kp,)), pltpu.SemaphoreType.DMA((n * 3 * kp,)),
                        pltpu.SemaphoreType.DMA((n * 3 * kp,)), pltpu.SemaphoreType.DMA((n * 3 * kp,)), pltpu.SemaphoreType.DMA,
                        pltpu.SemaphoreType.DMA((3,)), pltpu.SemaphoreType.DMA((3,))],
    )(*fulls, small)
    return outs[:n], outs[n]


HBM_SPEC = pl.BlockSpec(memory_space=pltpu.HBM)
SEM_SPEC = pl.BlockSpec(memory_space=pltpu.SEMAPHORE)
DATAFLOW = pltpu.SideEffectType.DATAFLOW_SIDE_EFFECTING


def _start_copies(name, arrays, sem_shape, after, copies):
    n = len(arrays)

    def body(*refs):
        for cp in copies(refs[:n], refs[n + 1], refs[n + 2]):
            cp.start()
        token = refs[2 * n + 3]
        token[...] = jnp.zeros_like(token)

    res = pl.pallas_call(
        body, name=name,
        out_shape=(pltpu.SemaphoreType.DMA(sem_shape), pltpu.SemaphoreType.DMA(sem_shape),
                   *[pltpu.HBM(a.shape, a.dtype) for a in arrays], jax.ShapeDtypeStruct((8, 128), F32)),
        in_specs=[HBM_SPEC] * n + [pl.BlockSpec(memory_space=pl.ANY)],
        out_specs=(SEM_SPEC, SEM_SPEC, *([HBM_SPEC] * n), pl.BlockSpec(memory_space=pltpu.VMEM)),
        input_output_aliases={i: 2 + i for i in range(n)},
        compiler_params=pltpu.CompilerParams(has_side_effects=DATAFLOW),
    )(*[pltpu.with_memory_space_constraint(a, pltpu.HBM) for a in arrays], after)
    return res[0], res[1], list(res[2:2 + n]), res[2 + n]


def _wait_copies(name, arrays, send, recv, after, copies):
    n = len(arrays)

    def body(*refs):
        for cp in copies(refs[:n], refs[n], refs[n + 1]):
            cp.wait_send()
            cp.wait_recv()

    return pl.pallas_call(
        body, name=name, out_shape=[pltpu.HBM(a.shape, a.dtype) for a in arrays],
        in_specs=[HBM_SPEC] * n + [SEM_SPEC, SEM_SPEC, pl.BlockSpec(memory_space=pl.ANY)], out_specs=[HBM_SPEC] * n,
        input_output_aliases={i: i for i in range(n)},
        compiler_params=pltpu.CompilerParams(has_side_effects=DATAFLOW),
    )(*arrays, send, recv, after)


def _direct_gather_copies(col_sharded):
    def copies(refs, send, recv):
        x, y, c = _place()
        mine = 2 * x + y
        out = []
        for w, ref in enumerate(refs):
            own_half = _shard_of(ref, col_sharded[w], mine, c)
            k = 0
            for cx, cy in _other_chips(x, y):
                for pc in (c, 1 - c):
                    out.append(pltpu.make_async_remote_copy(
                        src_ref=own_half, dst_ref=own_half, send_sem=send.at[6 * w + k], recv_sem=recv.at[6 * w + k],
                        device_id=(cx, cy, pc), device_id_type=MESH))
                    k += 1
        return out
    return copies


def _chip_scatter_copies(n, col_sharded):
    def piece(ref, cs, chip):
        if cs:
            w = ref.shape[2] // N_CHIPS
            return ref.at[:, :, pl.ds(pl.multiple_of(chip * w, SLAB), w)]
        return ref.at[pl.ds(chip, 1)]

    def copies(refs, send, recv):
        x, y, c = _place()
        out = []
        for k, (cx, cy) in enumerate(_other_chips(x, y)):
            for w in range(n):
                out.append(pltpu.make_async_remote_copy(
                    src_ref=piece(refs[w], col_sharded[w], 2 * cx + cy), dst_ref=refs[n + w].at[pl.ds(k, 1)],
                    send_sem=send.at[3 * w + k], recv_sem=recv.at[3 * w + k], device_id=(cx, cy, c), device_id_type=MESH))
        return out
    return copies


def _shard_views(gs, col_sharded):
    return [g.reshape(1, *g.shape) if cs else g.reshape(N_CHIPS, g.shape[0] // N_CHIPS, g.shape[1])
            for g, cs in zip(gs, col_sharded)]


DMA_CHUNK_BYTES = 1 << 20


def _chunk_rows(shape, itemsize):
    s, rows, cols = shape
    n = 1
    while s * (rows // n) * cols * itemsize > DMA_CHUNK_BYTES and (rows // n) % 32 == 0:
        n *= 2
    return rows // n


def _row_pieces(src, src_row0, dst, dst_row0, rows, send_sem, recv_sem, device):
    step = _chunk_rows((src.shape[0], rows, src.shape[2]), src.dtype.itemsize)
    return [pltpu.make_async_remote_copy(src_ref=src.at[:, pl.ds(src_row0 + r, step)], dst_ref=dst.at[:, pl.ds(dst_row0 + r, step)],
                                         send_sem=send_sem, recv_sem=recv_sem, device_id=device, device_id_type=MESH)
            for r in range(0, rows, step)]


def _pair_exchange_copies(n, whole):
    def copies(refs, send, recv):
        x, y, c = _place()
        sibling = (x, y, 1 - c)
        out = []
        for w in range(n):
            hr = refs[n + w].shape[1]
            if whole:
                out.append(pltpu.make_async_remote_copy(
                    src_ref=refs[w].at[:, pl.ds((1 - c) * hr, hr)], dst_ref=refs[n + w], send_sem=send.at[w], recv_sem=recv.at[w],
                    device_id=sibling, device_id_type=MESH))
            else:
                out += _row_pieces(refs[w], (1 - c) * hr, refs[n + w], 0, hr, send.at[w], recv.at[w], sibling)
        return out
    return copies


def _pair_join_copies(n, whole, sem0=0):
    def copies(refs, send, recv):
        x, y, c = _place()
        sibling = (x, y, 1 - c)
        out = []
        for w in range(n):
            hr = refs[w].shape[1] // 2
            sems = dict(send_sem=send.at[sem0 + w], recv_sem=recv.at[sem0 + w])
            if whole:
                out.append(pltpu.make_async_remote_copy(
                    src_ref=refs[w].at[:, pl.ds(c * hr, hr)], dst_ref=refs[w].at[:, pl.ds((1 - c) * hr, hr)],
                    device_id=sibling, device_id_type=MESH, **sems))
            else:
                out += _row_pieces(refs[w], c * hr, refs[w], c * hr, hr, sems["send_sem"], sems["recv_sem"], sibling)
        return out
    return copies


def _both_copies(first, n_first, second):
    def copies(refs, send, recv):
        return first(refs[:n_first], send, recv) + second(refs[n_first:], send, recv)
    return copies


def _small_gather_copies(refs, send, recv):
    vec, land = refs
    x, y, c = _place()
    me = 4 * x + 2 * y + c
    return [pltpu.make_async_remote_copy(src_ref=vec, dst_ref=land.at[me], send_sem=send.at[k], recv_sem=recv.at[k],
                                         device_id=peer, device_id_type=MESH) for k, (peer, _) in enumerate(_all_devices(x, y, c))]


def _small_sum(vec, land, me, n_sum, d):
    rows = vec.shape[0]

    def body(me_ref, v_ref, l_ref, sum_ref, kept_ref):
        def slot(k):
            return jnp.where(me_ref[0] == k, v_ref[...], l_ref[k])

        total = slot(0)[0:n_sum, :]
        kept_ref[0] = slot(0)[n_sum:rows, :]
        for k in range(1, N_DEV):
            total = total + slot(k)[0:n_sum, :]
            kept_ref[k] = slot(k)[n_sum:rows, :]
        sum_ref[...] = total
        loss = 0.5 / d * jnp.sum(total[0:8, :])
        sum_ref[0:8, :] = jnp.full((8, 128), loss, F32)

    vm = pl.BlockSpec(memory_space=pltpu.VMEM)
    return pl.pallas_call(
        body, name="small_sum", in_specs=[pl.BlockSpec(memory_space=pltpu.SMEM), vm, vm], out_specs=[vm, vm],
        out_shape=[jax.ShapeDtypeStruct((n_sum, 128), F32), jax.ShapeDtypeStruct((N_DEV, rows - n_sum, 128), F32)],
        compiler_params=pltpu.CompilerParams(vmem_limit_bytes=VMEM_LIMIT_V7X))(me, vec, land)


def _pair_exchange_halves(views, name):
    n = len(views)
    half_shapes = [(v.shape[0], v.shape[1] // 2, v.shape[2]) for v in views]

    def body(*refs):
        ins, got = refs[:n], refs[n:2 * n]
        send, recv = refs[2 * n:]
        x, y, c = _place()
        sibling = (x, y, 1 - c)
        for w in range(n):
            hr = half_shapes[w][1]
            for cp in _row_pieces(ins[w], (1 - c) * hr, got[w], 0, hr, send.at[w], recv.at[w], sibling):
                cp.start()
        for w in range(n):
            hr = half_shapes[w][1]
            pltpu.make_async_remote_copy(src_ref=ins[w].at[:, pl.ds((1 - c) * hr, hr)], dst_ref=got[w], send_sem=send.at[w],
                                         recv_sem=recv.at[w], device_id=sibling, device_id_type=MESH).wait()

    any_spec = pl.BlockSpec(memory_space=pl.ANY)
    return pl.pallas_call(
        body, name=name,
        out_shape=[jax.ShapeDtypeStruct(s, v.dtype) for s, v in zip(half_shapes, views)],
        in_specs=[any_spec] * n, out_specs=[any_spec] * n,
        scratch_shapes=[pltpu.SemaphoreType.DMA((n,)), pltpu.SemaphoreType.DMA((n,))],
    )(*views)


def _pair_sum(view, got, core, name):
    s, r, cols = view.shape
    hr = r // 2
    tr = _row_tile(hr, cols)
    nb = hr // tr

    def body(core_ref, a_ref, b_ref, o_ref):
        del core_ref
        o_ref[...] = (a_ref[...].astype(F32) + b_ref[...].astype(F32)).astype(BF16)

    same = pl.BlockSpec((None, tr, cols), lambda j, i, core_ref: (j, i, 0))
    return pl.pallas_call(
        body, name=name,
        grid_spec=pltpu.PrefetchScalarGridSpec(
            num_scalar_prefetch=1, grid=(s, nb),
            in_specs=[pl.BlockSpec((None, tr, cols), lambda j, i, core_ref: (j, core_ref[0] * nb + i, 0)), same],
            out_specs=same),
        out_shape=jax.ShapeDtypeStruct((s, hr, cols), BF16), compiler_params=_params("parallel", "parallel"))(core, view, got)


def _pair_sum_small(views, gots, core, name):
    n = len(views)

    def body(core_ref, *refs):
        del core_ref
        for w in range(n):
            refs[2 * n + w][...] = (refs[w][...].astype(F32) + refs[n + w][...].astype(F32)).astype(BF16)

    halves = [(v.shape[0], v.shape[1] // 2, v.shape[2]) for v in views]
    own = [pl.BlockSpec(h, lambda i, core_ref: (0, core_ref[0], 0)) for h in halves]
    whole = [pl.BlockSpec(h, lambda i, core_ref: (0, 0, 0)) for h in halves]
    return pl.pallas_call(
        body, name=name,
        grid_spec=pltpu.PrefetchScalarGridSpec(num_scalar_prefetch=1, grid=(1,), in_specs=own + whole, out_specs=whole),
        out_shape=[jax.ShapeDtypeStruct(h, BF16) for h in halves], compiler_params=_params("arbitrary"))(core, *views, *gots)


def _piece_cols(part, col_sharded):
    return part.shape[2] // N_CHIPS if col_sharded else part.shape[2]


def _chip_sum_small(parts, gots, col_sharded, place, name):
    n = len(parts)

    def body(place_ref, *refs):
        del place_ref
        for w in range(n):
            got = refs[n + w]
            acc = refs[w][...].astype(F32) + got[0].astype(F32)
            refs[2 * n + w][...] = acc + got[1].astype(F32) + got[2].astype(F32)

    own, others, outs, shapes = [], [], [], []
    for p, cs in zip(parts, col_sharded):
        hr, cols = p.shape[1], _piece_cols(p, cs)
        own.append(pl.BlockSpec((None, hr, cols), (lambda i, pr: (0, 0, pr[0])) if cs else (lambda i, pr: (pr[0], 0, 0))))
        others.append(pl.BlockSpec((3, hr, cols), lambda i, pr: (0, 0, 0)))
        outs.append(pl.BlockSpec((hr, cols), lambda i, pr: (pr[1], 0)))
        shapes.append(jax.ShapeDtypeStruct((2 * hr, cols), F32))
    return pl.pallas_call(
        body, name=name,
        grid_spec=pltpu.PrefetchScalarGridSpec(num_scalar_prefetch=1, grid=(1,), in_specs=own + others, out_specs=outs),
        out_shape=shapes, compiler_params=_params("arbitrary"))(place, *parts, *gots)


def _chip_sum(part, got, col_sharded, place, name):
    _, hr, _ = part.shape
    cols = _piece_cols(part, col_sharded)
    tr = _row_tile(hr, cols)
    nb = hr // tr

    def body(place_ref, own_ref, g0_ref, g1_ref, g2_ref, o_ref):
        del place_ref
        acc = own_ref[...].astype(F32) + g0_ref[...].astype(F32)
        o_ref[...] = acc + g1_ref[...].astype(F32) + g2_ref[...].astype(F32)

    if col_sharded:
        own = pl.BlockSpec((None, tr, cols), lambda i, pr: (0, i, pr[0]))
    else:
        own = pl.BlockSpec((None, tr, cols), lambda i, pr: (pr[0], i, 0))
    others = [pl.BlockSpec((None, tr, cols), lambda i, pr, k=k: (k, i, 0)) for k in range(3)]
    return pl.pallas_call(
        body, name=name,
        grid_spec=pltpu.PrefetchScalarGridSpec(
            num_scalar_prefetch=1, grid=(nb,), in_specs=[own] + others,
            out_specs=pl.BlockSpec((tr, cols), lambda i, pr: (pr[1] * nb + i, 0))),
        out_shape=jax.ShapeDtypeStruct((2 * hr, cols), F32), compiler_params=_params("parallel"))(place, part, got, got, got)


def _row_tile(rows, cols, itemsize=4, budget=2 << 20):
    t = rows
    while t * cols * itemsize > budget and t % 16 == 0:
        t //= 2
    return t


def _adamw(w, g, m, v, name, after, also_g=False):
    rows, cols = w.shape
    tr = _row_tile(rows, cols, budget=1 << 20)
    extra = [] if after is None else [after]
    n_out = 4 if also_g else 3

    def body(w_ref, g_ref, m_ref, v_ref, *rest):
        d_ref, nm_ref, nv_ref = rest[len(extra):len(extra) + 3]
        g_ = g_ref[...]
        if also_g:
            rest[len(extra) + 3][...] = g_
        nm = ADAM_B1 * m_ref[...] + (1.0 - ADAM_B1) * g_
        nv = ADAM_B2 * v_ref[...] + (1.0 - ADAM_B2) * (g_ * g_)
        m_hat = nm / (1.0 - ADAM_B1 ** ADAM_STEP)
        v_hat = nv / (1.0 - ADAM_B2 ** ADAM_STEP)
        d_ref[...] = -ADAM_LR * (m_hat / (jnp.sqrt(v_hat) + ADAM_EPS) + ADAM_WD * w_ref[...])
        nm_ref[...] = nm
        nv_ref[...] = nv

    spec = pl.BlockSpec((tr, cols), lambda i: (i, 0))
    shp = jax.ShapeDtypeStruct((rows, cols), F32)
    return pl.pallas_call(body, name=name, grid=(rows // tr,),
                          in_specs=[spec] * 4 + [pl.BlockSpec(memory_space=pl.ANY)] * len(extra), out_specs=[spec] * n_out,
                          out_shape=[shp] * n_out, compiler_params=_params("parallel"))(w, g, m, v, *extra)


def _all_devices(x, y, c):
    out = []
    for k in range(1, N_DEV):
        peer = (x ^ ((k >> 2) & 1), y ^ ((k >> 1) & 1), c ^ (k & 1))
        out.append((peer, 4 * peer[0] + 2 * peer[1] + peer[2]))
    return out


def _ada_exchange(c, w_shard, b_ada):
    bsz, d = c.shape
    cs = w_shard.shape[1]

    def body(c_ref, w_ref, b_ref, mod_ref, act_ref, c_all, part, pieces, csend, crecv, psend, precv):
        x, y, core = _place()
        me = 4 * x + 2 * y + core
        chip = 2 * x + y
        c_all[me] = c_ref[...]
        peers = _all_devices(x, y, core)
        copies = []
        for k, (peer, _) in enumerate(peers):
            cp = pltpu.make_async_remote_copy(src_ref=c_ref, dst_ref=c_all.at[me], send_sem=csend.at[k], recv_sem=crecv.at[k],
                                              device_id=peer, device_id_type=MESH)
            cp.start()
            copies.append(cp)
        for k, (_, src) in enumerate(peers):
            pltpu.make_async_remote_copy(src_ref=c_ref, dst_ref=c_all.at[src], send_sem=csend.at[k], recv_sem=crecv.at[k],
                                         device_id=(x, y, core), device_id_type=MESH).wait_recv()
        rows = jnp.concatenate([c_all[i] for i in range(N_DEV)], axis=0)
        act = rows * _sigmoid(rows)
        act_ref[...] = act
        prod = jnp.dot(act.astype(BF16), w_ref[...].astype(BF16), preferred_element_type=F32)
        for i in range(N_DEV):
            part[i] = prod[i * bsz:(i + 1) * bsz, :]
        pieces[chip] = part[me]
        chips = _other_chips(x, y)
        for k, (cx, cy) in enumerate(chips):
            cp = pltpu.make_async_remote_copy(src_ref=part.at[4 * cx + 2 * cy + core], dst_ref=pieces.at[chip],
                                              send_sem=psend.at[k], recv_sem=precv.at[k], device_id=(cx, cy, core),
                                              device_id_type=MESH)
            cp.start()
            copies.append(cp)
        for k, (cx, cy) in enumerate(chips):
            pltpu.make_async_remote_copy(src_ref=part.at[me], dst_ref=pieces.at[2 * cx + cy], send_sem=psend.at[k],
                                         recv_sem=precv.at[k], device_id=(cx, cy, core), device_id_type=MESH).wait_recv()
        for cp in copies:
            cp.wait_send()
        mod_ref[...] = jnp.concatenate([pieces[j] for j in range(N_CHIPS)], axis=1) + b_ref[...]

    vm = pl.BlockSpec(memory_space=pltpu.VMEM)
    return pl.pallas_call(
        body, name="ada_exchange", in_specs=[vm] * 3, out_specs=[vm] * 2,
        out_shape=[jax.ShapeDtypeStruct((bsz, 3 * d), F32), jax.ShapeDtypeStruct((N_DEV * bsz, d), F32)],
        scratch_shapes=[pltpu.VMEM((N_DEV, bsz, d), F32), pltpu.VMEM((N_DEV, bsz, cs), F32), pltpu.VMEM((N_CHIPS, bsz, cs), F32),
                        pltpu.SemaphoreType.DMA((N_DEV - 1,)), pltpu.SemaphoreType.DMA((N_DEV - 1,)),
                        pltpu.SemaphoreType.DMA((3,)), pltpu.SemaphoreType.DMA((3,))],
        compiler_params=pltpu.CompilerParams(vmem_limit_bytes=VMEM_LIMIT_V7X))(c, w_shard, b_ada)


def _grad_w_ada_cols(act_t, d_cols):
    d, n = act_t.shape
    cs = d_cols.shape[1]

    def body(a_ref, g_ref, o_ref):
        a, g = a_ref[...], g_ref[...]
        acc = a[:, 0:1] * g[0:1, :]
        for b in range(1, n):
            acc = acc + a[:, b:b + 1] * g[b:b + 1, :]
        o_ref[...] = acc

    vm = pl.BlockSpec(memory_space=pltpu.VMEM)
    return pl.pallas_call(body, name="grad_w_ada", in_specs=[vm] * 2, out_specs=vm,
                          out_shape=jax.ShapeDtypeStruct((d, cs), F32),
                          compiler_params=pltpu.CompilerParams(vmem_limit_bytes=VMEM_LIMIT_V7X))(act_t, d_cols)


def _permute_w_in(w_nat, lay):
    d = lay.d
    group = 4

    def call(name, width, n_pieces, nat_piece, out_block0, prev):
        def body(*refs):
            refs[-1][...] = jnp.concatenate([r[...] for r in refs[:group]], axis=1)

        in_specs = [pl.BlockSpec((d, width), lambda s, m=m: (0, nat_piece(group * s + m))) for m in range(group)]
        args = [w_nat] * group
        aliases = {}
        if prev is not None:
            in_specs.append(pl.BlockSpec(memory_space=pl.ANY))
            args.append(prev)
            aliases = {group: 0}
        return pl.pallas_call(
            body, name=name, grid=(n_pieces // group,), in_specs=in_specs,
            out_specs=pl.BlockSpec((d, group * width), lambda s: (0, out_block0 + s)),
            out_shape=jax.ShapeDtypeStruct((d, lay.np), BF16), input_output_aliases=aliases,
            compiler_params=_params("arbitrary"))(*args)

    w_all = call("permute_w_attn", SLAB, 2 * PAIR_SLABS, lay.attn_nat_slab, 0, None)
    n_rest = 6 * d // CONV_TILE
    if n_rest % group:
        group = 2
    return call("permute_w_rest", CONV_TILE, n_rest, lay.rest_nat_tile, lay.c0 // (group * CONV_TILE), w_all)


def _project(x2, mod3, w_all, b_all, seq, col0, ncols, tn, out_dtype, want_ht, name):
    t, d = x2.shape
    tm = min(2048, seq)
    per_seq = seq // tm
    j0 = col0 // tn

    def body(x_ref, mod_ref, w_ref, b_ref, o_ref, *rest):
        h_ref = rest[-1]

        @pl.when(pl.program_id(1) == 0)
        def _():
            h = x_ref[...] * (1.0 + mod_ref[:, d:2 * d]) + mod_ref[:, 0:d]
            h_ref[...] = h.astype(BF16)
            if want_ht:
                rest[0][...] = h.T.astype(BF16)

        o_ref[...] = (jnp.dot(h_ref[...], w_ref[...], preferred_element_type=F32) + b_ref[...]).astype(out_dtype)

    out_shape = [jax.ShapeDtypeStruct((t, ncols), out_dtype)]
    out_specs = [pl.BlockSpec((tm, tn), lambda i, j: (i, j))]
    if want_ht:
        out_shape.append(jax.ShapeDtypeStruct((d, t), BF16))
        out_specs.append(pl.BlockSpec((d, tm), lambda i, j: (0, i)))
    return pl.pallas_call(
        body, name=name, grid=(t // tm, ncols // tn),
        in_specs=[pl.BlockSpec((tm, d), lambda i, j: (i, 0)),
                  pl.BlockSpec((None, 1, 3 * d), lambda i, j: (i // per_seq, 0, 0)),
                  pl.BlockSpec((d, tn), lambda i, j: (0, j0 + j)),
                  pl.BlockSpec((1, tn), lambda i, j: (0, j0 + j))],
        out_specs=out_specs, out_shape=out_shape,
        scratch_shapes=[pltpu.VMEM((tm, d), BF16)],
        compiler_params=_params("arbitrary", "arbitrary"))(x2, mod3, w_all, b_all)


def _slope(g, p, hh):
    head = 4 * g + 2 * p + hh
    return 2.0 ** (-ALIBI_MAX_EXP * (head + 1.0) / N_HEADS)


def _ld_rows(ref, start, n, stride):
    if stride == 1:
        return ref[pl.ds(start, n), :]
    return ref[pl.ds(start, n, stride=stride), :]


def _st_rows(ref, start, n, stride, val):
    if stride == 1:
        ref[pl.ds(start, n), :] = val
    else:
        ref[pl.ds(start, n, stride=stride), :] = val


def _sub_blocks(g, seq):
    return seq // DILATIONS[g] // SUB


def _key_rows(g, seq):
    return SUB if _sub_blocks(g, seq) == 1 else 2 * SUB


def _fill_bias(bias_ref, p, seq):
    for g in range(N_GROUPS):
        nk = _key_rows(g, seq)
        diff = lax.broadcasted_iota(jnp.int32, (SUB, nk), 0) - lax.broadcasted_iota(jnp.int32, (SUB, nk), 1)
        for i, off in enumerate((0, SUB)):
            if i == 1 and nk == SUB:
                continue
            delta = diff + off
            ok = (delta >= 0) & (delta <= SUB)
            dist = (delta * DILATIONS[g]).astype(F32)
            for hh in range(2):
                slope = jnp.where(p == 0, _slope(g, 0, hh), _slope(g, 1, hh))
                bias_ref[g, i, hh, :, 0:nk] = jnp.where(ok, -slope * dist, NEG)


def _to_sub_major(pa_ref, col, sub_ref, stage, dil, seq):
    cols = slice(col * SLAB, (col + 1) * SLAB)
    if dil == 1:
        sub_ref[...] = pa_ref[:, cols]
        return
    n = seq // dil
    stage[...] = pa_ref[:, cols].astype(F32)
    for r in range(dil):
        sub_ref[pl.ds(r * n, n), :] = stage[pl.ds(r, n, stride=dil), :].astype(BF16)


def _block_rows(it, g, seq):
    dil, nb = DILATIONS[g], _sub_blocks(g, seq)
    row0 = pl.multiple_of(it * SUB, SUB)
    if nb == 1:
        return row0, row0, 0, it
    blk = it % nb
    first = blk == 0
    krow0 = pl.multiple_of(row0 - jnp.where(first, 0, SUB), SUB)
    nat = row0 if dil == 1 else it // nb + dil * SUB * blk
    return row0, krow0, jnp.where(first, 0, 1), nat


def _nt(a, b):
    return lax.dot_general(a, b, (((1,), (1,)), ((), ())), preferred_element_type=F32)


def _tn(a, b):
    return lax.dot_general(a, b, (((0,), (0,)), ((), ())), preferred_element_type=F32)


def _head_sums(t):
    rows = t.shape[0]
    lo = jnp.broadcast_to(jnp.sum(t[:, :HEAD_DIM], axis=-1, keepdims=True), (rows, HEAD_DIM))
    hi = jnp.broadcast_to(jnp.sum(t[:, HEAD_DIM:], axis=-1, keepdims=True), (rows, HEAD_DIM))
    return jnp.concatenate([lo, hi], axis=-1)


def _attn_fwd(pa, bsz, seq):
    t = pa.shape[0]
    n_blocks = seq // SUB
    chunk = 256

    def body(pa_ref, o_ref, lse_ref, a_ref, sub, stage, bias_ref, s_buf, p_buf, l_buf):
        p = pl.program_id(1)
        _fill_bias(bias_ref, p, seq)
        head0 = lax.broadcasted_iota(jnp.int32, (SUB, SLAB), 1) < HEAD_DIM
        for g in range(N_GROUPS):
            dil = DILATIONS[g]
            for w in range(3):
                _to_sub_major(pa_ref, 3 * w + g, sub.at[w], stage, dil, seq)
            nk = _key_rows(g, seq)

            def trip(i, carry, g=g, dil=dil, nk=nk):
                places = [_block_rows(BLOCKS_PER_TRIP * i + j, g, seq) for j in range(BLOCKS_PER_TRIP)]
                for j, (row0, krow0, _, _) in enumerate(places):
                    q = sub[0, pl.ds(row0, SUB), :]
                    zero = jnp.zeros_like(q)
                    q2 = jnp.concatenate([jnp.where(head0, q, zero), jnp.where(head0, zero, q)], axis=0) * (HEAD_DIM ** -0.5)
                    s_buf[j, :, 0:nk] = _nt(q2, sub[1, pl.ds(krow0, nk), :])
                for j, (_, _, bi, _) in enumerate(places):
                    for c in range(0, 2 * SUB, SOFTMAX_ROWS):
                        hh, r = divmod(c, SUB)
                        s = s_buf[j, c:c + SOFTMAX_ROWS, 0:nk] + bias_ref[g, bi, hh, r:r + SOFTMAX_ROWS, 0:nk]
                        m = jnp.max(s, axis=-1, keepdims=True)
                        e = jnp.exp(s - m)
                        den = jnp.sum(e, axis=-1, keepdims=True)
                        p_buf[j, c:c + SOFTMAX_ROWS, 0:nk] = (e * (1.0 / den)).astype(BF16)
                        l_buf[j, c:c + SOFTMAX_ROWS, :] = jnp.broadcast_to(m + jnp.log(den), (SOFTMAX_ROWS, SLAB))
                for j, (_, krow0, _, nat) in enumerate(places):
                    o2 = jnp.dot(p_buf[j, :, 0:nk], sub[2, pl.ds(krow0, nk), :], preferred_element_type=F32)
                    _st_rows(o_ref.at[g], nat, SUB, dil, jnp.where(head0, o2[0:SUB], o2[SUB:2 * SUB]))
                    _st_rows(lse_ref.at[g], nat, SUB, dil, jnp.where(head0, l_buf[j, 0:SUB, :], l_buf[j, SUB:2 * SUB, :]))
                return carry

            lax.fori_loop(0, n_blocks // BLOCKS_PER_TRIP, trip, 0)

        def mix(i, carry):
            rows = pl.ds(pl.multiple_of(i * chunk, chunk), chunk)
            l0, l1, l2 = lse_ref[0, rows, :], lse_ref[1, rows, :], lse_ref[2, rows, :]
            m = jnp.maximum(jnp.maximum(l0, l1), l2)
            e0, e1, e2 = jnp.exp(l0 - m), jnp.exp(l1 - m), jnp.exp(l2 - m)
            tot = e0 + e1 + e2
            o = (e0 / tot) * o_ref[0, rows, :] + (e1 / tot) * o_ref[1, rows, :] + (e2 / tot) * o_ref[2, rows, :]
            z = pa_ref[rows, 9 * SLAB:10 * SLAB].astype(F32)
            a_ref[rows, :] = (o * (z * _sigmoid(z))).astype(BF16)
            return carry

        lax.fori_loop(0, seq // chunk, mix, 0)

    big = jax.ShapeDtypeStruct((N_GROUPS, t, 2 * SLAB), F32)
    return pl.pallas_call(
        body, name="attn_fwd", grid=(bsz, 2),
        in_specs=[pl.BlockSpec((seq, PAIR_COLS), lambda b, p: (b, p))],
        out_specs=[pl.BlockSpec((N_GROUPS, seq, SLAB), lambda b, p: (0, b, p)),
                   pl.BlockSpec((N_GROUPS, seq, SLAB), lambda b, p: (0, b, p)),
                   pl.BlockSpec((seq, SLAB), lambda b, p: (b, p))],
        out_shape=[big, big, jax.ShapeDtypeStruct((t, 2 * SLAB), BF16)],
        scratch_shapes=[pltpu.VMEM((3, seq, SLAB), BF16), pltpu.VMEM((seq, SLAB), F32),
                        pltpu.VMEM((N_GROUPS, 2, 2, SUB, 2 * SUB), F32), pltpu.VMEM((BLOCKS_PER_TRIP, 2 * SUB, 2 * SUB), F32),
                        pltpu.VMEM((BLOCKS_PER_TRIP, 2 * SUB, 2 * SUB), BF16), pltpu.VMEM((BLOCKS_PER_TRIP, 2 * SUB, SLAB), F32)],
        compiler_params=_params("arbitrary", "arbitrary"))(pa)


def _shift_down(v, k, rows):
    return jnp.where(rows >= k, pltpu.roll(v, k, 0), 0.0)


def _shift_up(v, k, rows):
    n = v.shape[0]
    return jnp.where(rows < n - k, pltpu.roll(v, n - k, 0), 0.0)


def _conv_fwd(pr, conv_w, bsz, seq, d):
    t = pr.shape[0]
    ct = CONV_TILE

    def body(p_ref, cw_ref, o_ref):
        u = p_ref[:, 2 * ct:3 * ct].astype(F32) * p_ref[:, 0:ct].astype(F32)
        cw = cw_ref[...]
        rows = lax.broadcasted_iota(jnp.int32, u.shape, 0)
        conv = cw[0:1, :] * _shift_down(u, 2, rows)
        conv = conv + cw[1:2, :] * _shift_down(u, 1, rows)
        conv = conv + cw[2:3, :] * u
        z = p_ref[:, 3 * ct:4 * ct].astype(F32)
        o_ref[...] = (p_ref[:, ct:2 * ct].astype(F32) * conv * (z * _sigmoid(z))).astype(BF16)

    return pl.pallas_call(
        body, name="conv_fwd", grid=(bsz, d // ct),
        in_specs=[pl.BlockSpec((seq, 4 * ct), lambda b, j: (b, j)), pl.BlockSpec((3, ct), lambda b, j: (0, j))],
        out_specs=pl.BlockSpec((seq, ct), lambda b, j: (b, j)),
        out_shape=jax.ShapeDtypeStruct((t, d), BF16), compiler_params=_params("parallel", "parallel"))(pr, conv_w)


def _tail(a_in, b_in, pr, x2, target2, mod3, w_pa, w_pc, w_out, b_out, ln_g, ln_b, seq, lay):
    t, d = x2.shape
    tm = 512
    per_seq = seq // tm
    n_steps = t // tm
    gate_blk = 4 * d // d

    def nt(a, b):
        return lax.dot_general(a, b, (((1,), (1,)), ((), ())), preferred_element_type=F32)

    def tn(a, b):
        return lax.dot_general(a, b, (((0,), (0,)), ((), ())), preferred_element_type=F32)

    def body(a_ref, b_ref, ga_ref, gb_ref, x_ref, tg_ref, mod_ref, wpa_ref, wpc_ref, wo_ref, bo_ref, lg_ref, lb_ref,
             dpg_ref, da_ref, db_ref, gx_ref, dgate_ref, small_ref, gwpa_hbm, gwpc_hbm, gwo_hbm,
             acc_pa, acc_pc, acc_o, sem):
        i = pl.program_id(0)

        @pl.when(i == 0)
        def _():
            acc_pa[...] = jnp.zeros_like(acc_pa)
            acc_pc[...] = jnp.zeros_like(acc_pc)
            acc_o[...] = jnp.zeros_like(acc_o)
            small_ref[...] = jnp.zeros_like(small_ref)

        @pl.when(i % per_seq == 0)
        def _():
            dgate_ref[...] = jnp.zeros_like(dgate_ref)

        halves = [slice(k * (tm // 2), (k + 1) * (tm // 2)) for k in range(2)]
        gate = mod_ref[:, 2 * d:3 * d]
        a_bf = [a_ref[rs, :] for rs in halves]
        b_bf = [b_ref[rs, :] for rs in halves]
        y_attn = [jnp.dot(a, wpa_ref[...], preferred_element_type=F32) for a in a_bf]
        y_conv = [jnp.dot(b, wpc_ref[...], preferred_element_type=F32) for b in b_bf]
        sa = [_sigmoid(ga_ref[rs, :].astype(F32)) for rs in halves]
        sb = [_sigmoid(gb_ref[rs, :].astype(F32)) for rs in halves]
        merged = [(sa[k] * y_attn[k] + sb[k] * y_conv[k]).astype(BF16) for k in range(2)]
        mo = [jnp.dot(m, wo_ref[...], preferred_element_type=F32) + bo_ref[...] for m in merged]
        d_mo_bf = []
        for k, rs in enumerate(halves):
            r = ALPHA * x_ref[rs, :] + gate * mo[k]
            mu = jnp.mean(r, axis=-1, keepdims=True)
            cen = r - mu
            var = jnp.mean(cen * cen, axis=-1, keepdims=True)
            rstd = lax.rsqrt(var + LN_EPS)
            xhat = cen * rstd
            err = xhat * lg_ref[...] + lb_ref[...] - tg_ref[rs, :]
            dy = err * (1.0 / d)
            dxhat = dy * lg_ref[...]
            dr = rstd * (dxhat - jnp.mean(dxhat, axis=-1, keepdims=True)
                         - xhat * jnp.mean(dxhat * xhat, axis=-1, keepdims=True))
            gx_ref[rs, :] = ALPHA * dr
            dgate_ref[...] += jnp.sum(dr * mo[k], axis=0, keepdims=True)
            d_mo = dr * gate
            small_ref[0:1, :] += jnp.sum(d_mo, axis=0, keepdims=True)
            small_ref[1:2, :] += jnp.sum(dy * xhat, axis=0, keepdims=True)
            small_ref[2:3, :] += jnp.sum(dy, axis=0, keepdims=True)
            small_ref[3:4, :] += jnp.sum(err * err, axis=0, keepdims=True)
            d_mo_bf.append(d_mo.astype(BF16))
        for k in range(2):
            acc_o[...] += tn(merged[k], d_mo_bf[k])
        dmerged = [nt(g, wo_ref[...]) for g in d_mo_bf]
        dy_attn, dy_conv = [], []
        for k, rs in enumerate(halves):
            dy_attn.append((dmerged[k] * sa[k]).astype(BF16))
            dy_conv.append((dmerged[k] * sb[k]).astype(BF16))
            dpg_ref[rs, 0:d] = (dmerged[k] * y_attn[k] * sa[k] * (1.0 - sa[k])).astype(BF16)
            dpg_ref[rs, d:2 * d] = (dmerged[k] * y_conv[k] * sb[k] * (1.0 - sb[k])).astype(BF16)
        for k, rs in enumerate(halves):
            acc_pa[...] += tn(a_bf[k], dy_attn[k])
            acc_pc[...] += tn(b_bf[k], dy_conv[k])
            da_ref[rs, :] = nt(dy_attn[k], wpa_ref[...])
            db_ref[rs, :] = nt(dy_conv[k], wpc_ref[...])

        @pl.when(i == n_steps - 1)
        def _():
            copies = [pltpu.make_async_copy(acc_pa, gwpa_hbm, sem.at[0]), pltpu.make_async_copy(acc_pc, gwpc_hbm, sem.at[1]),
                      pltpu.make_async_copy(acc_o, gwo_hbm, sem.at[2])]
            for cp in copies:
                cp.start()
            for cp in copies:
                cp.wait()

    row = lambda w: pl.BlockSpec((tm, w), lambda i: (i, 0))
    const = lambda shp: pl.BlockSpec(shp, lambda i: (0,) * len(shp), pipeline_mode=pl.Buffered(1))
    any_spec = pl.BlockSpec(memory_space=pl.ANY)
    return pl.pallas_call(
        body, name="tail", grid=(n_steps,),
        in_specs=[row(Z_WIDTH), row(d),
                  pl.BlockSpec((tm, d), lambda i: (i, gate_blk)), pl.BlockSpec((tm, d), lambda i: (i, gate_blk + 1)),
                  row(d), row(d), pl.BlockSpec((None, 1, 3 * d), lambda i: (i // per_seq, 0, 0)),
                  const((Z_WIDTH, d)), const((d, d)), const((d, d)), const((1, d)), const((1, d)), const((1, d))],
        out_specs=[pl.BlockSpec((tm, 2 * d), lambda i: (i, lay.g0 // (2 * d))), row(Z_WIDTH), row(d), row(d),
                   pl.BlockSpec((None, 1, d), lambda i: (i // per_seq, 0, 0)), pl.BlockSpec((8, d), lambda i: (0, 0)),
                   HBM_SPEC, HBM_SPEC, HBM_SPEC],
        out_shape=[jax.ShapeDtypeStruct((t, lay.np), BF16), jax.ShapeDtypeStruct((t, Z_WIDTH), F32),
                   jax.ShapeDtypeStruct((t, d), F32), jax.ShapeDtypeStruct((t, d), F32),
                   jax.ShapeDtypeStruct((t // seq, 1, d), F32), jax.ShapeDtypeStruct((8, d), F32),
                   pltpu.HBM((Z_WIDTH, d), F32), pltpu.HBM((d, d), F32), pltpu.HBM((d, d), F32)],
        scratch_shapes=[pltpu.VMEM((Z_WIDTH, d), F32), pltpu.VMEM((d, d), F32), pltpu.VMEM((d, d), F32),
                        pltpu.SemaphoreType.DMA((3,))],
        compiler_params=_params("arbitrary"),
    )(a_in, b_in, pr, pr, x2, target2, mod3, w_pa, w_pc, w_out, b_out, ln_g, ln_b)


def _conv_bwd(dproj, db, pr, conv_w, bsz, seq, lay):
    d = lay.d
    ct = CONV_TILE
    base = lay.c0 // (4 * ct)

    def body(dp_in, db_ref, p_ref, cw_ref, dp_ref, gcw_ref):
        del dp_in
        u_x, g_b, g_c, z = [p_ref[:, k * ct:(k + 1) * ct].astype(F32) for k in range(4)]
        cw = cw_ref[...]
        u = g_c * u_x
        rows = lax.broadcasted_iota(jnp.int32, u.shape, 0)
        u1, u2 = _shift_down(u, 1, rows), _shift_down(u, 2, rows)
        conv = cw[0:1, :] * u2 + cw[1:2, :] * u1 + cw[2:3, :] * u
        sig = _sigmoid(z)
        sl = z * sig
        dbv = db_ref[...]
        gbc = g_b * conv
        dp_ref[:, ct:2 * ct] = (dbv * sl * conv).astype(BF16)
        dp_ref[:, 3 * ct:4 * ct] = (dbv * gbc * (sig * (1.0 + z * (1.0 - sig)))).astype(BF16)
        dconv = dbv * sl * g_b

        @pl.when(pl.program_id(1) == 0)
        def _():
            gcw_ref[...] = jnp.zeros_like(gcw_ref)

        gcw_ref[0:1, :] += jnp.sum(dconv * u2, axis=0, keepdims=True)
        gcw_ref[1:2, :] += jnp.sum(dconv * u1, axis=0, keepdims=True)
        gcw_ref[2:3, :] += jnp.sum(dconv * u, axis=0, keepdims=True)
        du = cw[2:3, :] * dconv + cw[1:2, :] * _shift_up(dconv, 1, rows) + cw[0:1, :] * _shift_up(dconv, 2, rows)
        dp_ref[:, 0:ct] = (du * g_c).astype(BF16)
        dp_ref[:, 2 * ct:3 * ct] = (du * u_x).astype(BF16)

    return pl.pallas_call(
        body, name="conv_bwd", grid=(d // ct, bsz),
        in_specs=[pl.BlockSpec(memory_space=pl.ANY), pl.BlockSpec((seq, ct), lambda j, b: (b, j)),
                  pl.BlockSpec((seq, 4 * ct), lambda j, b: (b, j)), pl.BlockSpec((3, ct), lambda j, b: (0, j))],
        out_specs=[pl.BlockSpec((seq, 4 * ct), lambda j, b: (b, base + j)), pl.BlockSpec((8, ct), lambda j, b: (0, j))],
        out_shape=[jax.ShapeDtypeStruct(dproj.shape, BF16), jax.ShapeDtypeStruct((8, d), F32)],
        input_output_aliases={0: 0}, compiler_params=_params("arbitrary", "arbitrary"))(dproj, db, pr, conv_w)


def _attn_bwd(dproj, pa, o_all, lse_all, da, bsz, seq, after):
    n_blocks = seq // SUB
    chunk = 256

    def body(dp_in, pa_ref, o_ref, lse_ref, da_ref, after_ref, dp_ref, sub, stage, dsub, dog, cvec, bias_ref,
             s_buf, dp_buf, ds_buf, pb_buf, q2_buf, do2_buf, l_buf, c_buf):
        del dp_in, after_ref
        p = pl.program_id(1)
        _fill_bias(bias_ref, p, seq)
        head0 = lax.broadcasted_iota(jnp.int32, (SUB, SLAB), 1) < HEAD_DIM

        def mix_bwd(i, carry):
            rows = pl.ds(pl.multiple_of(i * chunk, chunk), chunk)
            ls = [lse_ref[g, rows, :] for g in range(N_GROUPS)]
            os_ = [o_ref[g, rows, :] for g in range(N_GROUPS)]
            m = jnp.maximum(jnp.maximum(ls[0], ls[1]), ls[2])
            es = [jnp.exp(l - m) for l in ls]
            tot = es[0] + es[1] + es[2]
            ws = [e / tot for e in es]
            o = ws[0] * os_[0] + ws[1] * os_[1] + ws[2] * os_[2]
            z = pa_ref[rows, 9 * SLAB:10 * SLAB].astype(F32)
            sig = _sigmoid(z)
            dav = da_ref[rows, :]
            do = dav * (z * sig)
            dp_ref[rows, 9 * SLAB:10 * SLAB] = (dav * o * (sig * (1.0 + z * (1.0 - sig)))).astype(BF16)
            wsum = _head_sums(do * o)
            for g in range(N_GROUPS):
                dog[g, rows, :] = ws[g] * do
                cvec[g, rows, :] = -(ws[g] * wsum)
            return carry

        lax.fori_loop(0, seq // chunk, mix_bwd, 0)

        for g in range(N_GROUPS):
            dil = DILATIONS[g]
            for w in range(3):
                _to_sub_major(pa_ref, 3 * w + g, sub.at[w], stage, dil, seq)
            dsub[1] = jnp.zeros((seq, SLAB), F32)
            dsub[2] = jnp.zeros((seq, SLAB), F32)
            nk = _key_rows(g, seq)

            def trip(i, carry, g=g, dil=dil, nk=nk):
                places = [_block_rows(BLOCKS_PER_TRIP * i + j, g, seq) for j in range(BLOCKS_PER_TRIP)]
                for j, (row0, krow0, _, nat) in enumerate(places):
                    q = sub[0, pl.ds(row0, SUB), :]
                    do = _ld_rows(dog.at[g], nat, SUB, dil).astype(BF16)
                    zero = jnp.zeros_like(q)
                    q2 = jnp.concatenate([jnp.where(head0, q, zero), jnp.where(head0, zero, q)], axis=0)
                    do2 = jnp.concatenate([jnp.where(head0, do, zero), jnp.where(head0, zero, do)], axis=0)
                    q2_buf[j] = q2
                    do2_buf[j] = do2
                    s_buf[j, :, 0:nk] = _nt(q2 * (HEAD_DIM ** -0.5), sub[1, pl.ds(krow0, nk), :])
                    dp_buf[j, :, 0:nk] = _nt(do2, sub[2, pl.ds(krow0, nk), :])
                    l_buf[j] = _ld_rows(lse_ref.at[g], nat, SUB, dil)
                    c_buf[j] = _ld_rows(cvec.at[g], nat, SUB, dil)
                for j, (_, _, bi, _) in enumerate(places):
                    for c in range(0, 2 * SUB, SOFTMAX_ROWS):
                        hh, r = divmod(c, SUB)
                        lane = hh * HEAD_DIM
                        s = s_buf[j, c:c + SOFTMAX_ROWS, 0:nk] + bias_ref[g, bi, hh, r:r + SOFTMAX_ROWS, 0:nk]
                        prob = jnp.exp(s - l_buf[j, r:r + SOFTMAX_ROWS, lane:lane + 1])
                        dprob = dp_buf[j, c:c + SOFTMAX_ROWS, 0:nk] + c_buf[j, r:r + SOFTMAX_ROWS, lane:lane + 1]
                        ds_buf[j, c:c + SOFTMAX_ROWS, 0:nk] = (prob * dprob * (HEAD_DIM ** -0.5)).astype(BF16)
                        pb_buf[j, c:c + SOFTMAX_ROWS, 0:nk] = prob.astype(BF16)
                for j, (row0, krow0, _, _) in enumerate(places):
                    ds = ds_buf[j, :, 0:nk]
                    dq2 = jnp.dot(ds, sub[1, pl.ds(krow0, nk), :], preferred_element_type=F32)
                    dsub[0, pl.ds(row0, SUB), :] = jnp.where(head0, dq2[0:SUB], dq2[SUB:2 * SUB])
                    dsub[1, pl.ds(krow0, nk), :] += _tn(ds, q2_buf[j])
                    dsub[2, pl.ds(krow0, nk), :] += _tn(pb_buf[j, :, 0:nk], do2_buf[j])
                return carry

            lax.fori_loop(0, n_blocks // BLOCKS_PER_TRIP, trip, 0)
            for w in range(3):
                cols = slice((3 * w + g) * SLAB, (3 * w + g + 1) * SLAB)
                if dil == 1:
                    dp_ref[:, cols] = dsub[w].astype(BF16)
                else:
                    n = seq // dil
                    for r in range(dil):
                        stage[pl.ds(r, n, stride=dil), :] = dsub[w, pl.ds(r * n, n), :]
                    dp_ref[:, cols] = stage[...].astype(BF16)

    return pl.pallas_call(
        body, name="attn_bwd", grid=(bsz, 2),
        in_specs=[pl.BlockSpec(memory_space=pl.ANY), pl.BlockSpec((seq, PAIR_COLS), lambda b, p: (b, p)),
                  pl.BlockSpec((N_GROUPS, seq, SLAB), lambda b, p: (0, b, p)),
                  pl.BlockSpec((N_GROUPS, seq, SLAB), lambda b, p: (0, b, p)),
                  pl.BlockSpec((seq, SLAB), lambda b, p: (b, p)), pl.BlockSpec(memory_space=pl.ANY)],
        out_specs=pl.BlockSpec((seq, PAIR_COLS), lambda b, p: (b, p)),
        out_shape=jax.ShapeDtypeStruct(dproj.shape, BF16), input_output_aliases={0: 0},
        scratch_shapes=[pltpu.VMEM((3, seq, SLAB), BF16), pltpu.VMEM((seq, SLAB), F32), pltpu.VMEM((3, seq, SLAB), F32),
                        pltpu.VMEM((3, seq, SLAB), F32), pltpu.VMEM((3, seq, SLAB), F32),
                        pltpu.VMEM((N_GROUPS, 2, 2, SUB, 2 * SUB), F32),
                        pltpu.VMEM((BLOCKS_PER_TRIP, 2 * SUB, 2 * SUB), F32), pltpu.VMEM((BLOCKS_PER_TRIP, 2 * SUB, 2 * SUB), F32),
                        pltpu.VMEM((BLOCKS_PER_TRIP, 2 * SUB, 2 * SUB), BF16), pltpu.VMEM((BLOCKS_PER_TRIP, 2 * SUB, 2 * SUB), BF16),
                        pltpu.VMEM((BLOCKS_PER_TRIP, 2 * SUB, SLAB), BF16), pltpu.VMEM((BLOCKS_PER_TRIP, 2 * SUB, SLAB), BF16),
                        pltpu.VMEM((BLOCKS_PER_TRIP, SUB, SLAB), F32), pltpu.VMEM((BLOCKS_PER_TRIP, SUB, SLAB), F32)],
        compiler_params=_params("arbitrary", "arbitrary"))(dproj, pa, o_all, lse_all, da, after)


def _grad_h(dproj, w_all, gx0, x2, mod3, seq, lay):
    t, d = x2.shape
    tm, tn = min(512, seq), min(512, d)
    per_seq = seq // tm

    def body(dp_ref, w_ref, gx0_ref, x_ref, scale_ref, gx_ref, dmod_ref):
        dh = _nt(dp_ref[:, 0:ATT], w_ref[:, 0:ATT]) + _nt(dp_ref[:, lay.c0:], w_ref[:, lay.c0:])
        gx_ref[...] = gx0_ref[...] + dh * (1.0 + scale_ref[...])

        @pl.when(pl.program_id(1) % per_seq == 0)
        def _():
            dmod_ref[...] = jnp.zeros_like(dmod_ref)

        dmod_ref[0:1, :] += jnp.sum(dh, axis=0, keepdims=True)
        dmod_ref[1:2, :] += jnp.sum(dh * x_ref[...], axis=0, keepdims=True)

    tile = pl.BlockSpec((tm, tn), lambda j, i: (i, j))
    return pl.pallas_call(
        body, name="grad_h", grid=(d // tn, t // tm),
        in_specs=[pl.BlockSpec((tm, lay.np), lambda j, i: (i, 0)), pl.BlockSpec((tn, lay.np), lambda j, i: (j, 0)),
                  tile, tile,
                  pl.BlockSpec((None, 1, tn), lambda j, i: (i // per_seq, 0, d // tn + j))],
        out_specs=[tile, pl.BlockSpec((None, 8, tn), lambda j, i: (i // per_seq, 0, j))],
        out_shape=[jax.ShapeDtypeStruct((t, d), F32), jax.ShapeDtypeStruct((t // seq, 8, d), F32)],
        compiler_params=_params("arbitrary", "arbitrary"))(dproj, w_all, gx0, x2, mod3)


def _grad_w_in(ht, dproj, seq, lay):
    d, t = ht.shape
    tm = seq
    n_i = t // tm

    def make_body(n_skip, n_pieces, tn, nat_tile):
        def body(*refs):
            refs = refs[n_skip:]
            ht_ref, dp_refs = refs[0], refs[1:1 + n_pieces]
            gw_hbm, gb_hbm, acc, bacc, gw_out, gb_out, sem = refs[1 + n_pieces:]
            i, j = pl.program_id(0), pl.program_id(1)
            dp = dp_refs[0][...] if n_pieces == 1 else jnp.concatenate([r[...] for r in dp_refs], axis=1)
            part = jnp.dot(ht_ref[...], dp, preferred_element_type=F32)
            bpart = jnp.sum(dp.astype(F32), axis=0, keepdims=True)

            if n_i > 1:
                @pl.when(i == 0)
                def _():
                    acc[j] = part
                    bacc[j] = bpart

                @pl.when((i > 0) & (i < n_i - 1))
                def _():
                    acc[j] += part
                    bacc[j] += bpart

            @pl.when(i == n_i - 1)
            def _():
                gw_out[...] = ((part + acc[j]) if n_i > 1 else part).astype(BF16)
                gb_out[...] = (bpart + bacc[j]) if n_i > 1 else bpart
                cols = pl.ds(pl.multiple_of(nat_tile(j) * tn, SLAB), tn)
                copies = [pltpu.make_async_copy(gw_out, gw_hbm.at[:, cols], sem.at[0]),
                          pltpu.make_async_copy(gb_out, gb_hbm.at[:, cols], sem.at[1])]
                for cp in copies:
                    cp.start()
                for cp in copies:
                    cp.wait()
        return body

    def call(name, pieces, n_tiles, nat_tile, prev):
        tn = sum(w for w, _ in pieces)
        any_spec = pl.BlockSpec(memory_space=pl.ANY)
        in_specs = [pl.BlockSpec((d, tm), lambda i, j: (0, i))]
        in_specs += [pl.BlockSpec((tm, w), lambda i, j, f=f: (i, f(j))) for w, f in pieces]
        args = [ht] + [dproj] * len(pieces)
        aliases = {}
        if prev is not None:
            in_specs = [any_spec] * 2 + in_specs
            args = list(prev) + args
            aliases = {0: 0, 1: 1}
        return pl.pallas_call(
            make_body(0 if prev is None else 2, len(pieces), tn, nat_tile), name=name, grid=(n_i, n_tiles), in_specs=in_specs,
            out_specs=[any_spec, any_spec],
            out_shape=[jax.ShapeDtypeStruct((d, lay.din), BF16), jax.ShapeDtypeStruct((1, lay.din), F32)],
            input_output_aliases=aliases,
            scratch_shapes=[pltpu.VMEM((n_tiles, d, tn), F32), pltpu.VMEM((n_tiles, 1, tn), F32), pltpu.VMEM((d, tn), BF16),
                            pltpu.VMEM((1, tn), F32), pltpu.SemaphoreType.DMA((2,))],
            compiler_params=_params("arbitrary", "arbitrary"))(*args)

    attn_pieces = [(SLAB, lambda j, m=m: (m % 2) * PAIR_SLABS + 2 * j + m // 2) for m in range(4)]
    first = call("grad_w_in_attn", attn_pieces, ATT // 512, lambda j: j, None)
    base = lay.c0 // CONV_TILE
    nct = lay.n_conv_tiles
    if nct % 2:
        return call("grad_w_in_rest", [(CONV_TILE, lambda j: base + j)], 6 * d // CONV_TILE, lay.rest_nat_tile, first)
    half = nct // 2

    def rest_piece(m):
        def perm_tile(j):
            conv = base + 4 * (2 * (j % half) + m) + j // half
            return jnp.where(j < 4 * half, conv, base + 2 * j + m)
        return (CONV_TILE, perm_tile)

    return call("grad_w_in_rest", [rest_piece(0), rest_piece(1)], 6 * d // 512, lambda j: ATT // 512 + j, first)


def _pack_rows(parts, width=128):
    flat = [p.reshape(-1) for p in parts]
    spans, rows = [], 0
    padded = []
    for f in flat:
        n = -(-f.shape[0] // (8 * width)) * 8
        padded.append(jnp.pad(f, (0, n * width - f.shape[0])).reshape(n, width))
        spans.append((rows, f.shape[0]))
        rows += n
    return jnp.concatenate(padded, axis=0), spans


def _unpack_rows(packed, spans, shapes, width=128):
    out = []
    for (row, n), shp in zip(spans, shapes):
        rows = -(-n // width)
        out.append(packed[row:row + rows].reshape(-1)[:n].reshape(shp))
    return out


def kernel(x, c, w_ada, b_ada, w_in, b_in, conv_w, w_proj_attn, w_proj_conv, w_out, b_out, ln_g, ln_b, loss_target, m_w_ada, m_b_ada, m_w_in, m_b_in, m_conv_w, m_w_proj_attn, m_w_proj_conv, m_w_out, m_b_out, m_ln_g, m_ln_b, v_w_ada, v_b_ada, v_w_in, v_b_in, v_conv_w, v_w_proj_attn, v_w_proj_conv, v_w_out, v_b_out, v_ln_g, v_ln_b):
    bsz, seq, d = x.shape
    t = bsz * seq
    lay = _Layout(d)
    col_sharded = [True, True, False, False]
    red_w = [w_in[0], w_proj_attn[0], w_proj_conv[0], w_out[0]]
    chip = 2 * lax.axis_index("x") + lax.axis_index("y")
    chip1 = chip.astype(jnp.int32).reshape(1)
    core1 = lax.axis_index("c").astype(jnp.int32).reshape(1)
    place = jnp.stack([chip, lax.axis_index("c")]).astype(jnp.int32)
    x2 = x.reshape(t, d)
    target2 = loss_target.reshape(t, d)

    mod, act_all = _ada_exchange(c, w_ada[0], b_ada)
    mod3 = mod.reshape(bsz, 1, 3 * d)

    cw_pad = jnp.pad(conv_w[0], ((0, 5), (0, 0))) + 0.0 * mod[0, 0]
    own_in_full = [_cast_into_full(red_w[0], col_sharded[0], chip1, "cast_shard_in")]
    own_in_full += list(_cast_into_full_small(red_w[1:], col_sharded[1:], chip1, "cast_shards_late"))
    (wi_f,), cw8 = _gather_weights(own_in_full[:1], col_sharded[:1], cw_pad)
    cw_full = cw8[0:3]
    late_copies = _direct_gather_copies(col_sharded[1:])
    late_send, late_recv, late_flying, late_token = _start_copies("gather_late_start", own_in_full[1:], (18,), cw8, late_copies)
    w_all = _permute_w_in(wi_f, lay)
    b_all = lay.perm_vector(b_in) + late_token[0, 0]

    rest_tn = 1024 if (6 * d) % 1024 == 0 else 512
    pa, = _project(x2, mod3, w_all, b_all, seq, 0, ATT, PAIR_COLS, BF16, False, "project_attn")
    pr, ht = _project(x2, mod3, w_all, b_all, seq, lay.c0, 6 * d, rest_tn, BF16, True, "project_rest")
    o_all, lse_all, a_in = _attn_fwd(pa, bsz, seq)
    b_in_act = _conv_fwd(pr, cw_full, bsz, seq, d)
    wpa_f, wpc_f, wo_f = _wait_copies("gather_late_wait", late_flying, late_send, late_recv, b_in_act, late_copies)
    (dproj, da_in, db_in, gx0, dgate, small_tail, gw_pa, gw_pc, gw_out) = _tail(
        a_in, b_in_act, pr, x2, target2, mod3, wpa_f, wpc_f, wo_f, b_out, ln_g, ln_b, seq, lay)

    late_views = _shard_views([gw_pa, gw_pc, gw_out], col_sharded[1:])
    late_lands = [lax.empty((v.shape[0], v.shape[1] // 2, v.shape[2]), v.dtype) for v in late_views]
    xl_send, xl_recv, xl_fly, xl_tok = _start_copies("grads_pair_exchange_late_start", late_views + late_lands, (3,), gw_out,
                                                     _pair_exchange_copies(3, False))
    dproj, gcw = _conv_bwd(dproj, db_in, pr, cw_full + xl_tok[0, 0], bsz, seq, lay)
    xl_done = _wait_copies("grads_pair_exchange_late_wait", xl_fly, xl_send, xl_recv, gcw, _pair_exchange_copies(3, True))
    late_parts = list(_pair_sum_small(xl_done[:3], xl_done[3:], core1, "grads_pair_sum_late"))

    late_cross = _chip_scatter_copies(3, col_sharded[1:])
    late_zone = [lax.empty((3, p.shape[1], _piece_cols(p, cs)), p.dtype) for p, cs in zip(late_parts, col_sharded[1:])]
    sl_send, sl_recv, sl_fly, sl_tok = _start_copies("grads_scatter_late_start", late_parts + late_zone, (9,), late_parts[0],
                                                     late_cross)
    dproj = _attn_bwd(dproj, pa, o_all, lse_all, da_in, bsz, seq, sl_tok)
    gw_in_bf, gb_in = _grad_w_in(ht, dproj, seq, lay)
    sl_done = _wait_copies("grads_scatter_late_wait", sl_fly, sl_send, sl_recv, gw_in_bf, late_cross)
    late_red = list(_chip_sum_small(sl_done[:3], sl_done[3:], col_sharded[1:], place, "grads_chip_sum_late"))

    in_view = _shard_views([gw_in_bf], col_sharded[:1])
    in_got = _pair_exchange_halves(in_view, "grads_pair_exchange_in")
    in_part = _pair_sum(in_view[0], in_got[0], core1, "grads_pair_sum_in")
    in_cross = _chip_scatter_copies(1, col_sharded[:1])
    in_zone = [lax.empty((3, in_part.shape[1], _piece_cols(in_part, True)), in_part.dtype)]
    late_views1 = [f.reshape(1, *f.shape) for f in late_red]
    si_send, si_recv, si_fly, si_tok = _start_copies(
        "grads_scatter_in_join_late_start", [in_part] + in_zone + late_views1, (6,), late_red[0],
        _both_copies(in_cross, 2, _pair_join_copies(3, False, sem0=3)))
    grad_x2, dmod = _grad_h(dproj, w_all, gx0, x2, mod3 + si_tok[0, 0], seq, lay)
    si_done = _wait_copies("grads_scatter_in_join_late_wait", si_fly, si_send, si_recv, grad_x2,
                           _both_copies(in_cross, 2, _pair_join_copies(3, True, sem0=3)))
    late_joined = [f[0] for f in si_done[2:]]
    in_red = _chip_sum(si_done[0], si_done[1], True, place, "grads_chip_sum_in")

    d_ada = jnp.concatenate([dmod[:, 0, :], dmod[:, 1, :], dgate[:, 0, :]], axis=1)
    pieces = [small_tail[3], jnp.sum(d_ada, axis=0), gb_in[0], small_tail[0], small_tail[1], small_tail[2], gcw[0:3]]
    packed, spans = _pack_rows(pieces)
    kept_in, _ = _pack_rows([d_ada])
    rows_all = jnp.concatenate([packed, kept_in], axis=0)
    small_land = lax.empty((N_DEV,) + rows_all.shape, F32)
    sm_send, sm_recv, sm_fly, sm_tok = _start_copies(
        "small_gather_join_in_start", [rows_all, small_land, in_red.reshape(1, *in_red.shape)], (N_DEV,), in_red,
        _both_copies(_small_gather_copies, 2, _pair_join_copies(1, False, sem0=N_DEV - 1)))
    big_w = [w_ada[0]] + red_w
    big_m = [m_w_ada[0], m_w_in[0], m_w_proj_attn[0], m_w_proj_conv[0], m_w_out[0]]
    big_v = [v_w_ada[0], v_w_in[0], v_w_proj_attn[0], v_w_proj_conv[0], v_w_out[0]]
    big_out = [None] * 5
    for w in range(3):
        big_out[2 + w] = _adamw(big_w[2 + w], late_joined[w], big_m[2 + w], big_v[2 + w], f"adamw_{2 + w}", sm_tok)
    sm_done = _wait_copies("small_gather_join_in_wait", sm_fly, sm_send, sm_recv, big_out[4][0],
                           _both_copies(_small_gather_copies, 2, _pair_join_copies(1, True, sem0=N_DEV - 1)))
    me1 = (4 * lax.axis_index("x") + 2 * lax.axis_index("y") + lax.axis_index("c")).astype(jnp.int32).reshape(1)
    summed, kept = _small_sum(sm_done[0], sm_done[1], me1, packed.shape[0], d)
    loss = summed[0, 0]
    _, g_b_ada, g_b_in, g_b_out, g_ln_g, g_ln_b, g_cw_full = _unpack_rows(
        summed, spans, [(d,), (3 * d,), (lay.din,), (d,), (d,), (d,), (3, d)])
    g_cw = lax.dynamic_slice(g_cw_full, (0, chip * (d // N_CHIPS)), (3, d // N_CHIPS))
    d_ada_all = kept.reshape(N_DEV, -1)[:, :bsz * 3 * d].reshape(N_DEV * bsz, 3 * d)
    ada_cols = 3 * d // N_CHIPS
    g_w_ada = _grad_w_ada_cols(act_all.T, lax.dynamic_slice(d_ada_all, (0, chip * ada_cols), (N_DEV * bsz, ada_cols)))

    big_out[0] = _adamw(big_w[0], g_w_ada, big_m[0], big_v[0], "adamw_0", None)
    small_w = [b_ada, b_in, conv_w[0], b_out, ln_g, ln_b]
    small_g = [g_b_ada, g_b_in, g_cw, g_b_out, g_ln_g, g_ln_b]
    small_m = [m_b_ada, m_b_in, m_conv_w[0], m_b_out, m_ln_g, m_ln_b]
    small_v = [v_b_ada, v_b_in, v_conv_w[0], v_b_out, v_ln_g, v_ln_b]
    pw, sp = _pack_rows(small_w)
    pg, _ = _pack_rows(small_g)
    pm, _ = _pack_rows(small_m)
    pv, _ = _pack_rows(small_v)
    sd, sm, sv = _adamw(pw, pg, pm, pv, "adamw_small", None)
    big_out[1] = _adamw(big_w[1], sm_done[2][0], big_m[1], big_v[1], "adamw_1", None, also_g=True)
    g_big = [g_w_ada, big_out[1][3]] + late_joined
    shapes = [a.shape for a in small_w]
    sd, sm, sv = _unpack_rows(sd, sp, shapes), _unpack_rows(sm, sp, shapes), _unpack_rows(sv, sp, shapes)

    def order(wa, bA, wi, bI, cw, wpa, wpc, wo, bO, lg, lb):
        return (wa[None], bA, wi[None], bI, cw[None], wpa[None], wpc[None], wo[None], bO, lg, lb)

    sg = [g.reshape(s) for g, s in zip(small_g, shapes)]
    grads_out = order(g_big[0], sg[0], g_big[1], sg[1], sg[2], g_big[2], g_big[3], g_big[4], sg[3], sg[4], sg[5])
    outs = []
    for idx, small in enumerate((sd, sm, sv)):
        outs.append(order(big_out[0][idx], small[0], big_out[1][idx], small[1], small[2], big_out[2][idx],
                          big_out[3][idx], big_out[4][idx], small[3], small[4], small[5]))
    return (loss, grad_x2.reshape(bsz, seq, d), *grads_out, *outs[0], *outs[1], *outs[2])
```

```python
import jax
import jax.numpy as jnp
from jax import lax
from jax.experimental import pallas as pl
from jax.experimental.pallas import tpu as pltpu

F32 = jnp.float32
BF16 = jnp.bfloat16
MESH = pl.DeviceIdType.MESH

HEAD_DIM = 64
N_GROUPS = 3
DILATIONS = (1, 4, 16)
N_HEADS = 12
SUB = 128
Q_WIDTH = 768
Z_WIDTH = 256
ATT = 3 * Q_WIDTH + Z_WIDTH
SLAB = 128
PAIR_SLABS = 10
PAIR_COLS = PAIR_SLABS * SLAB
CONV_TILE = 256
SOFTMAX_ROWS = 32
BLOCKS_PER_TRIP = 8
ALIBI_MAX_EXP = 8.0
ALPHA = 2.0 ** 0.25
LN_EPS = 1e-5
ADAM_LR, ADAM_B1, ADAM_B2, ADAM_EPS, ADAM_WD, ADAM_STEP = 0.001, 0.9, 0.999, 1e-08, 0.01, 10
N_CHIPS = 4
N_DEV = 8
VMEM_LIMIT_V7X = 60 * 1024 * 1024
NEG = -1e30


def _params(*sem):
    return pltpu.CompilerParams(dimension_semantics=sem, vmem_limit_bytes=VMEM_LIMIT_V7X)


def _sigmoid(v):
    return 0.5 * jnp.tanh(0.5 * v) + 0.5


class _Layout:
    def __init__(self, d):
        self.d = d
        self.din = ATT + 6 * d
        c0 = 3072
        while c0 % (2 * d):
            c0 += 1024
        self.c0, self.g0, self.np = c0, c0 + 4 * d, c0 + 6 * d
        self.n_conv_tiles = d // CONV_TILE

    def attn_nat_slab(self, s):
        p, i = s // PAIR_SLABS, s % PAIR_SLABS
        return jnp.where(i < 9, (i // 3) * 6 + (i % 3) * 2 + p, 18 + p)

    def rest_nat_tile(self, t):
        n4 = 4 * self.n_conv_tiles
        conv = ATT // CONV_TILE + (t % 4) * self.n_conv_tiles + t // 4
        return jnp.where(t < n4, conv, ATT // CONV_TILE + t)

    def perm_vector(self, v):
        parts = []
        for s in range(2 * PAIR_SLABS):
            p, i = divmod(s, PAIR_SLABS)
            ns = (i // 3) * 6 + (i % 3) * 2 + p if i < 9 else 18 + p
            parts.append(v[:, ns * SLAB:(ns + 1) * SLAB])
        parts.append(jnp.zeros((1, self.c0 - ATT), v.dtype))
        for j in range(self.n_conv_tiles):
            for k in range(4):
                a = ATT + k * self.d + j * CONV_TILE
                parts.append(v[:, a:a + CONV_TILE])
        parts.append(v[:, ATT + 4 * self.d:])
        return jnp.concatenate(parts, axis=1)


def _place():
    return lax.axis_index("x"), lax.axis_index("y"), lax.axis_index("c")


def _other_chips(x, y):
    return [(1 - x, y), (x, 1 - y), (1 - x, 1 - y)]


def _shard_of(ref, col_sharded, chip, half=None):
    if col_sharded:
        cs = ref.shape[1] // N_CHIPS
        cols = pl.ds(pl.multiple_of(chip * cs, SLAB), cs)
        if half is None:
            return ref.at[:, cols]
        n = ref.shape[0] // 2
        return ref.at[pl.ds(half * n, n), cols]
    rs = ref.shape[0] // N_CHIPS
    if half is None:
        return ref.at[pl.ds(chip * rs, rs)]
    return ref.at[pl.ds(chip * rs + half * (rs // 2), rs // 2)]


GATHER_PIECES = 4


def _cast_into_full(shard, col_sharded, chip, name):
    rows, cols = shard.shape
    tr = _row_tile(rows, cols)
    nb = rows // tr

    def body(chip_ref, s_ref, o_ref):
        del chip_ref
        o_ref[...] = s_ref[...].astype(BF16)

    if col_sharded:
        full, out_spec = (rows, cols * N_CHIPS), pl.BlockSpec((tr, cols), lambda i, ch: (i, ch[0]))
    else:
        full, out_spec = (rows * N_CHIPS, cols), pl.BlockSpec((tr, cols), lambda i, ch: (ch[0] * nb + i, 0))
    return pl.pallas_call(
        body, name=name,
        grid_spec=pltpu.PrefetchScalarGridSpec(num_scalar_prefetch=1, grid=(nb,),
                                               in_specs=[pl.BlockSpec((tr, cols), lambda i, ch: (i, 0))], out_specs=out_spec),
        out_shape=jax.ShapeDtypeStruct(full, BF16), compiler_params=_params("parallel"))(chip, shard)


def _cast_into_full_small(shards, col_sharded, chip, name):
    n = len(shards)

    def body(chip_ref, *refs):
        del chip_ref
        for w in range(n):
            refs[n + w][...] = refs[w][...].astype(BF16)

    in_specs = [pl.BlockSpec(s.shape, lambda i, ch: (0, 0)) for s in shards]
    out_specs = [pl.BlockSpec(s.shape, (lambda i, ch: (0, ch[0])) if cs else (lambda i, ch: (ch[0], 0)))
                 for s, cs in zip(shards, col_sharded)]
    fulls = [(s.shape[0], s.shape[1] * N_CHIPS) if cs else (s.shape[0] * N_CHIPS, s.shape[1]) for s, cs in zip(shards, col_sharded)]
    return pl.pallas_call(
        body, name=name,
        grid_spec=pltpu.PrefetchScalarGridSpec(num_scalar_prefetch=1, grid=(1,), in_specs=in_specs, out_specs=out_specs),
        out_shape=[jax.ShapeDtypeStruct(f, BF16) for f in fulls], compiler_params=_params("arbitrary"))(chip, *shards)


def _gather_weights(fulls, col_sharded, small):
    n = len(fulls)
    kp = GATHER_PIECES

    def piece(ref, cs, chip, half, k):
        if cs:
            width = ref.shape[1] // N_CHIPS
            rows = ref.shape[0] // 2 // kp
            return ref.at[pl.ds(half * (ref.shape[0] // 2) + k * rows, rows), pl.ds(pl.multiple_of(chip * width, SLAB), width)]
        rs = ref.shape[0] // N_CHIPS
        rows = rs // 2 // kp
        return ref.at[pl.ds(chip * rs + half * (rs // 2) + k * rows, rows)]

    def body(*refs):
        sm_in, outs, sm_out = refs[n], refs[n + 1:2 * n + 1], refs[2 * n + 1]
        send, recv, fsend, frecv, lsem, ssend, srecv = refs[2 * n + 2:]
        x, y, c = _place()
        mine = 2 * x + y
        sibling = (x, y, 1 - c)
        first = (x ^ (1 - c), y ^ c)
        second = (x ^ c, y ^ (1 - c))
        diagonal = (1 - x, 1 - y)
        sources = [first, second, diagonal]
        senders = [first, second, second]

        def copy(ref, sems, slot, to):
            return pltpu.make_async_remote_copy(src_ref=ref, dst_ref=ref, send_sem=sems[0].at[slot], recv_sem=sems[1].at[slot],
                                                device_id=to, device_id_type=MESH)

        local = pltpu.make_async_copy(sm_in, _shard_of(sm_out, True, mine), lsem)
        local.start()
        sends = []
        for k, (cx, cy) in enumerate(_other_chips(x, y)):
            cp = pltpu.make_async_remote_copy(src_ref=sm_in, dst_ref=_shard_of(sm_out, True, mine), send_sem=ssend.at[k],
                                              recv_sem=srecv.at[k], device_id=(cx, cy, c), device_id_type=MESH)
            cp.start()
            sends.append(cp)
        for k in range(kp):
            for w in range(n):
                own = piece(outs[w], col_sharded[w], mine, c, k)
                for slot, chip in enumerate((first, second)):
                    cp = copy(own, (send, recv), (w * 3 + slot) * kp + k, (*chip, c))
                    cp.start()
                    sends.append(cp)
        for slot in range(3):
            source = 2 * sources[slot][0] + sources[slot][1]
            for k in range(kp):
                for w in range(n):
                    landed = piece(outs[w], col_sharded[w], source, c, k)
                    copy(landed, (send, recv), (w * 3 + slot) * kp + k, (*senders[slot], c)).wait_recv()
                    if slot == 0:
                        cp = copy(landed, (send, recv), (w * 3 + 2) * kp + k, (*second, c))
                        cp.start()
                        sends.append(cp)
                    cp = copy(landed, (fsend, frecv), (w * 3 + slot) * kp + k, sibling)
                    cp.start()
                    sends.append(cp)
        for slot, chip in enumerate((second, first, diagonal)):
            for k in range(kp):
                for w in range(n):
                    passed = piece(outs[w], col_sharded[w], 2 * chip[0] + chip[1], 1 - c, k)
                    copy(passed, (fsend, frecv), (w * 3 + slot) * kp + k, sibling).wait_recv()
        for k, (cx, cy) in enumerate(_other_chips(x, y)):
            theirs = _shard_of(sm_out, True, 2 * cx + cy)
            pltpu.make_async_remote_copy(src_ref=theirs, dst_ref=theirs, send_sem=ssend.at[k], recv_sem=srecv.at[k],
                                         device_id=(cx, cy, c), device_id_type=MESH).wait_recv()
        for cp in sends:
            cp.wait_send()
        local.wait()

    any_spec = pl.BlockSpec(memory_space=pl.ANY)
    outs = pl.pallas_call(
        body, name="gather_weights",
        out_shape=[jax.ShapeDtypeStruct(f.shape, BF16) for f in fulls]
        + [jax.ShapeDtypeStruct((small.shape[0], small.shape[1] * N_CHIPS), small.dtype)],
        in_specs=[any_spec] * (n + 1), out_specs=[any_spec] * (n + 1), input_output_aliases={w: w for w in range(n)},
        scratch_shapes=[pltpu.SemaphoreType.DMA((n * 3 * kp,)), pltpu.SemaphoreType.DMA((n * 3 * kp,)),
                        pltpu.SemaphoreType.DMA((n * 3 * kp,)), pltpu.SemaphoreType.DMA((n * 3 * kp,)), pltpu.SemaphoreType.DMA,
                        pltpu.SemaphoreType.DMA((3,)), pltpu.SemaphoreType.DMA((3,))],
    )(*fulls, small)
    return outs[:n], outs[n]


HBM_SPEC = pl.BlockSpec(memory_space=pltpu.HBM)
SEM_SPEC = pl.BlockSpec(memory_space=pltpu.SEMAPHORE)
DATAFLOW = pltpu.SideEffectType.DATAFLOW_SIDE_EFFECTING


def _start_copies(name, arrays, sem_shape, after, copies):
    n = len(arrays)

    def body(*refs):
        for cp in copies(refs[:n], refs[n + 1], refs[n + 2]):
            cp.start()
        token = refs[2 * n + 3]
        token[...] = jnp.zeros_like(token)

    res = pl.pallas_call(
        body, name=name,
        out_shape=(pltpu.SemaphoreType.DMA(sem_shape), pltpu.SemaphoreType.DMA(sem_shape),
                   *[pltpu.HBM(a.shape, a.dtype) for a in arrays], jax.ShapeDtypeStruct((8, 128), F32)),
        in_specs=[HBM_SPEC] * n + [pl.BlockSpec(memory_space=pl.ANY)],
        out_specs=(SEM_SPEC, SEM_SPEC, *([HBM_SPEC] * n), pl.BlockSpec(memory_space=pltpu.VMEM)),
        input_output_aliases={i: 2 + i for i in range(n)},
        compiler_params=pltpu.CompilerParams(has_side_effects=DATAFLOW),
    )(*[pltpu.with_memory_space_constraint(a, pltpu.HBM) for a in arrays], after)
    return res[0], res[1], list(res[2:2 + n]), res[2 + n]


def _wait_copies(name, arrays, send, recv, after, copies):
    n = len(arrays)

    def body(*refs):
        for cp in copies(refs[:n], refs[n], refs[n + 1]):
            cp.wait_send()
            cp.wait_recv()

    return pl.pallas_call(
        body, name=name, out_shape=[pltpu.HBM(a.shape, a.dtype) for a in arrays],
        in_specs=[HBM_SPEC] * n + [SEM_SPEC, SEM_SPEC, pl.BlockSpec(memory_space=pl.ANY)], out_specs=[HBM_SPEC] * n,
        input_output_aliases={i: i for i in range(n)},
        compiler_params=pltpu.CompilerParams(has_side_effects=DATAFLOW),
    )(*arrays, send, recv, after)


def _direct_gather_copies(col_sharded):
    def copies(refs, send, recv):
        x, y, c = _place()
        mine = 2 * x + y
        out = []
        for w, ref in enumerate(refs):
            own_half = _shard_of(ref, col_sharded[w], mine, c)
            k = 0
            for cx, cy in _other_chips(x, y):
                for pc in (c, 1 - c):
                    out.append(pltpu.make_async_remote_copy(
                        src_ref=own_half, dst_ref=own_half, send_sem=send.at[6 * w + k], recv_sem=recv.at[6 * w + k],
                        device_id=(cx, cy, pc), device_id_type=MESH))
                    k += 1
        return out
    return copies


def _chip_scatter_copies(n, col_sharded):
    def piece(ref, cs, chip):
        if cs:
            w = ref.shape[2] // N_CHIPS
            return ref.at[:, :, pl.ds(pl.multiple_of(chip * w, SLAB), w)]
        return ref.at[pl.ds(chip, 1)]

    def copies(refs, send, recv):
        x, y, c = _place()
        out = []
        for k, (cx, cy) in enumerate(_other_chips(x, y)):
            for w in range(n):
                out.append(pltpu.make_async_remote_copy(
                    src_ref=piece(refs[w], col_sharded[w], 2 * cx + cy), dst_ref=refs[n + w].at[pl.ds(k, 1)],
                    send_sem=send.at[3 * w + k], recv_sem=recv.at[3 * w + k], device_id=(cx, cy, c), device_id_type=MESH))
        return out
    return copies


def _shard_views(gs, col_sharded):
    return [g.reshape(1, *g.shape) if cs else g.reshape(N_CHIPS, g.shape[0] // N_CHIPS, g.shape[1])
            for g, cs in zip(gs, col_sharded)]


DMA_CHUNK_BYTES = 1 << 20


def _chunk_rows(shape, itemsize):
    s, rows, cols = shape
    n = 1
    while s * (rows // n) * cols * itemsize > DMA_CHUNK_BYTES and (rows // n) % 32 == 0:
        n *= 2
    return rows // n


def _row_pieces(src, src_row0, dst, dst_row0, rows, send_sem, recv_sem, device):
    step = _chunk_rows((src.shape[0], rows, src.shape[2]), src.dtype.itemsize)
    return [pltpu.make_async_remote_copy(src_ref=src.at[:, pl.ds(src_row0 + r, step)], dst_ref=dst.at[:, pl.ds(dst_row0 + r, step)],
                                         send_sem=send_sem, recv_sem=recv_sem, device_id=device, device_id_type=MESH)
            for r in range(0, rows, step)]


def _pair_exchange_copies(n, whole):
    def copies(refs, send, recv):
        x, y, c = _place()
        sibling = (x, y, 1 - c)
        out = []
        for w in range(n):
            hr = refs[n + w].shape[1]
            if whole:
                out.append(pltpu.make_async_remote_copy(
                    src_ref=refs[w].at[:, pl.ds((1 - c) * hr, hr)], dst_ref=refs[n + w], send_sem=send.at[w], recv_sem=recv.at[w],
                    device_id=sibling, device_id_type=MESH))
            else:
                out += _row_pieces(refs[w], (1 - c) * hr, refs[n + w], 0, hr, send.at[w], recv.at[w], sibling)
        return out
    return copies


def _pair_join_copies(n, whole, sem0=0):
    def copies(refs, send, recv):
        x, y, c = _place()
        sibling = (x, y, 1 - c)
        out = []
        for w in range(n):
            hr = refs[w].shape[1] // 2
            sems = dict(send_sem=send.at[sem0 + w], recv_sem=recv.at[sem0 + w])
            if whole:
                out.append(pltpu.make_async_remote_copy(
                    src_ref=refs[w].at[:, pl.ds(c * hr, hr)], dst_ref=refs[w].at[:, pl.ds((1 - c) * hr, hr)],
                    device_id=sibling, device_id_type=MESH, **sems))
            else:
                out += _row_pieces(refs[w], c * hr, refs[w], c * hr, hr, sems["send_sem"], sems["recv_sem"], sibling)
        return out
    return copies


def _both_copies(first, n_first, second):
    def copies(refs, send, recv):
        return first(refs[:n_first], send, recv) + second(refs[n_first:], send, recv)
    return copies


def _small_gather_copies(refs, send, recv):
    vec, land = refs
    x, y, c = _place()
    me = 4 * x + 2 * y + c
    return [pltpu.make_async_remote_copy(src_ref=vec, dst_ref=land.at[me], send_sem=send.at[k], recv_sem=recv.at[k],
                                         device_id=peer, device_id_type=MESH) for k, (peer, _) in enumerate(_all_devices(x, y, c))]


def _small_sum(vec, land, me, n_sum, d):
    rows = vec.shape[0]

    def body(me_ref, v_ref, l_ref, sum_ref, kept_ref):
        def slot(k):
            return jnp.where(me_ref[0] == k, v_ref[...], l_ref[k])

        total = slot(0)[0:n_sum, :]
        kept_ref[0] = slot(0)[n_sum:rows, :]
        for k in range(1, N_DEV):
            total = total + slot(k)[0:n_sum, :]
            kept_ref[k] = slot(k)[n_sum:rows, :]
        sum_ref[...] = total
        loss = 0.5 / d * jnp.sum(total[0:8, :])
        sum_ref[0:8, :] = jnp.full((8, 128), loss, F32)

    vm = pl.BlockSpec(memory_space=pltpu.VMEM)
    return pl.pallas_call(
        body, name="small_sum", in_specs=[pl.BlockSpec(memory_space=pltpu.SMEM), vm, vm], out_specs=[vm, vm],
        out_shape=[jax.ShapeDtypeStruct((n_sum, 128), F32), jax.ShapeDtypeStruct((N_DEV, rows - n_sum, 128), F32)],
        compiler_params=pltpu.CompilerParams(vmem_limit_bytes=VMEM_LIMIT_V7X))(me, vec, land)


def _pair_exchange_halves(views, name):
    n = len(views)
    half_shapes = [(v.shape[0], v.shape[1] // 2, v.shape[2]) for v in views]

    def body(*refs):
        ins, got = refs[:n], refs[n:2 * n]
        send, recv = refs[2 * n:]
        x, y, c = _place()
        sibling = (x, y, 1 - c)
        for w in range(n):
            hr = half_shapes[w][1]
            for cp in _row_pieces(ins[w], (1 - c) * hr, got[w], 0, hr, send.at[w], recv.at[w], sibling):
                cp.start()
        for w in range(n):
            hr = half_shapes[w][1]
            pltpu.make_async_remote_copy(src_ref=ins[w].at[:, pl.ds((1 - c) * hr, hr)], dst_ref=got[w], send_sem=send.at[w],
                                         recv_sem=recv.at[w], device_id=sibling, device_id_type=MESH).wait()

    any_spec = pl.BlockSpec(memory_space=pl.ANY)
    return pl.pallas_call(
        body, name=name,
        out_shape=[jax.ShapeDtypeStruct(s, v.dtype) for s, v in zip(half_shapes, views)],
        in_specs=[any_spec] * n, out_specs=[any_spec] * n,
        scratch_shapes=[pltpu.SemaphoreType.DMA((n,)), pltpu.SemaphoreType.DMA((n,))],
    )(*views)


def _pair_sum(view, got, core, name):
    s, r, cols = view.shape
    hr = r // 2
    tr = _row_tile(hr, cols)
    nb = hr // tr

    def body(core_ref, a_ref, b_ref, o_ref):
        del core_ref
        o_ref[...] = (a_ref[...].astype(F32) + b_ref[...].astype(F32)).astype(BF16)

    same = pl.BlockSpec((None, tr, cols), lambda j, i, core_ref: (j, i, 0))
    return pl.pallas_call(
        body, name=name,
        grid_spec=pltpu.PrefetchScalarGridSpec(
            num_scalar_prefetch=1, grid=(s, nb),
            in_specs=[pl.BlockSpec((None, tr, cols), lambda j, i, core_ref: (j, core_ref[0] * nb + i, 0)), same],
            out_specs=same),
        out_shape=jax.ShapeDtypeStruct((s, hr, cols), BF16), compiler_params=_params("parallel", "parallel"))(core, view, got)


def _pair_sum_small(views, gots, core, name):
    n = len(views)

    def body(core_ref, *refs):
        del core_ref
        for w in range(n):
            refs[2 * n + w][...] = (refs[w][...].astype(F32) + refs[n + w][...].astype(F32)).astype(BF16)

    halves = [(v.shape[0], v.shape[1] // 2, v.shape[2]) for v in views]
    own = [pl.BlockSpec(h, lambda i, core_ref: (0, core_ref[0], 0)) for h in halves]
    whole = [pl.BlockSpec(h, lambda i, core_ref: (0, 0, 0)) for h in halves]
    return pl.pallas_call(
        body, name=name,
        grid_spec=pltpu.PrefetchScalarGridSpec(num_scalar_prefetch=1, grid=(1,), in_specs=own + whole, out_specs=whole),
        out_shape=[jax.ShapeDtypeStruct(h, BF16) for h in halves], compiler_params=_params("arbitrary"))(core, *views, *gots)


def _piece_cols(part, col_sharded):
    return part.shape[2] // N_CHIPS if col_sharded else part.shape[2]


def _chip_sum_small(parts, gots, col_sharded, place, name):
    n = len(parts)

    def body(place_ref, *refs):
        del place_ref
        for w in range(n):
            got = refs[n + w]
            acc = refs[w][...].astype(F32) + got[0].astype(F32)
            refs[2 * n + w][...] = acc + got[1].astype(F32) + got[2].astype(F32)

    own, others, outs, shapes = [], [], [], []
    for p, cs in zip(parts, col_sharded):
        hr, cols = p.shape[1], _piece_cols(p, cs)
        own.append(pl.BlockSpec((None, hr, cols), (lambda i, pr: (0, 0, pr[0])) if cs else (lambda i, pr: (pr[0], 0, 0))))
        others.append(pl.BlockSpec((3, hr, cols), lambda i, pr: (0, 0, 0)))
        outs.append(pl.BlockSpec((hr, cols), lambda i, pr: (pr[1], 0)))
        shapes.append(jax.ShapeDtypeStruct((2 * hr, cols), F32))
    return pl.pallas_call(
        body, name=name,
        grid_spec=pltpu.PrefetchScalarGridSpec(num_scalar_prefetch=1, grid=(1,), in_specs=own + others, out_specs=outs),
        out_shape=shapes, compiler_params=_params("arbitrary"))(place, *parts, *gots)


def _chip_sum(part, got, col_sharded, place, name):
    _, hr, _ = part.shape
    cols = _piece_cols(part, col_sharded)
    tr = _row_tile(hr, cols)
    nb = hr // tr

    def body(place_ref, own_ref, g0_ref, g1_ref, g2_ref, o_ref):
        del place_ref
        acc = own_ref[...].astype(F32) + g0_ref[...].astype(F32)
        o_ref[...] = acc + g1_ref[...].astype(F32) + g2_ref[...].astype(F32)

    if col_sharded:
        own = pl.BlockSpec((None, tr, cols), lambda i, pr: (0, i, pr[0]))
    else:
        own = pl.BlockSpec((None, tr, cols), lambda i, pr: (pr[0], i, 0))
    others = [pl.BlockSpec((None, tr, cols), lambda i, pr, k=k: (k, i, 0)) for k in range(3)]
    return pl.pallas_call(
        body, name=name,
        grid_spec=pltpu.PrefetchScalarGridSpec(
            num_scalar_prefetch=1, grid=(nb,), in_specs=[own] + others,
            out_specs=pl.BlockSpec((tr, cols), lambda i, pr: (pr[1] * nb + i, 0))),
        out_shape=jax.ShapeDtypeStruct((2 * hr, cols), F32), compiler_params=_params("parallel"))(place, part, got, got, got)


def _row_tile(rows, cols, itemsize=4, budget=2 << 20):
    t = rows
    while t * cols * itemsize > budget and t % 16 == 0:
        t //= 2
    return t


def _adamw(w, g, m, v, name, after, also_g=False):
    rows, cols = w.shape
    tr = _row_tile(rows, cols, budget=1 << 20)
    extra = [] if after is None else [after]
    n_out = 4 if also_g else 3

    def body(w_ref, g_ref, m_ref, v_ref, *rest):
        d_ref, nm_ref, nv_ref = rest[len(extra):len(extra) + 3]
        g_ = g_ref[...]
        if also_g:
            rest[len(extra) + 3][...] = g_
        nm = ADAM_B1 * m_ref[...] + (1.0 - ADAM_B1) * g_
        nv = ADAM_B2 * v_ref[...] + (1.0 - ADAM_B2) * (g_ * g_)
        m_hat = nm / (1.0 - ADAM_B1 ** ADAM_STEP)
        v_hat = nv / (1.0 - ADAM_B2 ** ADAM_STEP)
        d_ref[...] = -ADAM_LR * (m_hat / (jnp.sqrt(v_hat) + ADAM_EPS) + ADAM_WD * w_ref[...])
        nm_ref[...] = nm
        nv_ref[...] = nv

    spec = pl.BlockSpec((tr, cols), lambda i: (i, 0))
    shp = jax.ShapeDtypeStruct((rows, cols), F32)
    return pl.pallas_call(body, name=name, grid=(rows // tr,),
                          in_specs=[spec] * 4 + [pl.BlockSpec(memory_space=pl.ANY)] * len(extra), out_specs=[spec] * n_out,
                          out_shape=[shp] * n_out, compiler_params=_params("parallel"))(w, g, m, v, *extra)


def _all_devices(x, y, c):
    out = []
    for k in range(1, N_DEV):
        peer = (x ^ ((k >> 2) & 1), y ^ ((k >> 1) & 1), c ^ (k & 1))
        out.append((peer, 4 * peer[0] + 2 * peer[1] + peer[2]))
    return out


def _ada_exchange(c, w_shard, b_ada):
    bsz, d = c.shape
    cs = w_shard.shape[1]

    def body(c_ref, w_ref, b_ref, mod_ref, act_ref, c_all, part, pieces, csend, crecv, psend, precv):
        x, y, core = _place()
        me = 4 * x + 2 * y + core
        chip = 2 * x + y
        c_all[me] = c_ref[...]
        peers = _all_devices(x, y, core)
        copies = []
        for k, (peer, _) in enumerate(peers):
            cp = pltpu.make_async_remote_copy(src_ref=c_ref, dst_ref=c_all.at[me], send_sem=csend.at[k], recv_sem=crecv.at[k],
                                              device_id=peer, device_id_type=MESH)
            cp.start()
            copies.append(cp)
        for k, (_, src) in enumerate(peers):
            pltpu.make_async_remote_copy(src_ref=c_ref, dst_ref=c_all.at[src], send_sem=csend.at[k], recv_sem=crecv.at[k],
                                         device_id=(x, y, core), device_id_type=MESH).wait_recv()
        rows = jnp.concatenate([c_all[i] for i in range(N_DEV)], axis=0)
        act = rows * _sigmoid(rows)
        act_ref[...] = act
        prod = jnp.dot(act.astype(BF16), w_ref[...].astype(BF16), preferred_element_type=F32)
        for i in range(N_DEV):
            part[i] = prod[i * bsz:(i + 1) * bsz, :]
        pieces[chip] = part[me]
        chips = _other_chips(x, y)
        for k, (cx, cy) in enumerate(chips):
            cp = pltpu.make_async_remote_copy(src_ref=part.at[4 * cx + 2 * cy + core], dst_ref=pieces.at[chip],
                                              send_sem=psend.at[k], recv_sem=precv.at[k], device_id=(cx, cy, core),
                                              device_id_type=MESH)
            cp.start()
            copies.append(cp)
        for k, (cx, cy) in enumerate(chips):
            pltpu.make_async_remote_copy(src_ref=part.at[me], dst_ref=pieces.at[2 * cx + cy], send_sem=psend.at[k],
                                         recv_sem=precv.at[k], device_id=(cx, cy, core), device_id_type=MESH).wait_recv()
        for cp in copies:
            cp.wait_send()
        mod_ref[...] = jnp.concatenate([pieces[j] for j in range(N_CHIPS)], axis=1) + b_ref[...]

    vm = pl.BlockSpec(memory_space=pltpu.VMEM)
    return pl.pallas_call(
        body, name="ada_exchange", in_specs=[vm] * 3, out_specs=[vm] * 2,
        out_shape=[jax.ShapeDtypeStruct((bsz, 3 * d), F32), jax.ShapeDtypeStruct((N_DEV * bsz, d), F32)],
        scratch_shapes=[pltpu.VMEM((N_DEV, bsz, d), F32), pltpu.VMEM((N_DEV, bsz, cs), F32), pltpu.VMEM((N_CHIPS, bsz, cs), F32),
                        pltpu.SemaphoreType.DMA((N_DEV - 1,)), pltpu.SemaphoreType.DMA((N_DEV - 1,)),
                        pltpu.SemaphoreType.DMA((3,)), pltpu.SemaphoreType.DMA((3,))],
        compiler_params=pltpu.CompilerParams(vmem_limit_bytes=VMEM_LIMIT_V7X))(c, w_shard, b_ada)


def _grad_w_ada_cols(act_t, d_cols):
    d, n = act_t.shape
    cs = d_cols.shape[1]

    def body(a_ref, g_ref, o_ref):
        a, g = a_ref[...], g_ref[...]
        acc = a[:, 0:1] * g[0:1, :]
        for b in range(1, n):
            acc = acc + a[:, b:b + 1] * g[b:b + 1, :]
        o_ref[...] = acc

    vm = pl.BlockSpec(memory_space=pltpu.VMEM)
    return pl.pallas_call(body, name="grad_w_ada", in_specs=[vm] * 2, out_specs=vm,
                          out_shape=jax.ShapeDtypeStruct((d, cs), F32),
                          compiler_params=pltpu.CompilerParams(vmem_limit_bytes=VMEM_LIMIT_V7X))(act_t, d_cols)


def _permute_w_in(w_nat, lay):
    d = lay.d
    group = 4

    def call(name, width, n_pieces, nat_piece, out_block0, prev):
        def body(*refs):
            refs[-1][...] = jnp.concatenate([r[...] for r in refs[:group]], axis=1)

        in_specs = [pl.BlockSpec((d, width), lambda s, m=m: (0, nat_piece(group * s + m))) for m in range(group)]
        args = [w_nat] * group
        aliases = {}
        if prev is not None:
            in_specs.append(pl.BlockSpec(memory_space=pl.ANY))
            args.append(prev)
            aliases = {group: 0}
        return pl.pallas_call(
            body, name=name, grid=(n_pieces // group,), in_specs=in_specs,
            out_specs=pl.BlockSpec((d, group * width), lambda s: (0, out_block0 + s)),
            out_shape=jax.ShapeDtypeStruct((d, lay.np), BF16), input_output_aliases=aliases,
            compiler_params=_params("arbitrary"))(*args)

    w_all = call("permute_w_attn", SLAB, 2 * PAIR_SLABS, lay.attn_nat_slab, 0, None)
    n_rest = 6 * d // CONV_TILE
    if n_rest % group:
        group = 2
    return call("permute_w_rest", CONV_TILE, n_rest, lay.rest_nat_tile, lay.c0 // (group * CONV_TILE), w_all)


def _project(x2, mod3, w_all, b_all, seq, col0, ncols, tn, out_dtype, want_ht, name):
    t, d = x2.shape
    tm = min(2048, seq)
    per_seq = seq // tm
    j0 = col0 // tn

    def body(x_ref, mod_ref, w_ref, b_ref, o_ref, *rest):
        h_ref = rest[-1]

        @pl.when(pl.program_id(1) == 0)
        def _():
            h = x_ref[...] * (1.0 + mod_ref[:, d:2 * d]) + mod_ref[:, 0:d]
            h_ref[...] = h.astype(BF16)
            if want_ht:
                rest[0][...] = h.T.astype(BF16)

        o_ref[...] = (jnp.dot(h_ref[...], w_ref[...], preferred_element_type=F32) + b_ref[...]).astype(out_dtype)

    out_shape = [jax.ShapeDtypeStruct((t, ncols), out_dtype)]
    out_specs = [pl.BlockSpec((tm, tn), lambda i, j: (i, j))]
    if want_ht:
        out_shape.append(jax.ShapeDtypeStruct((d, t), BF16))
        out_specs.append(pl.BlockSpec((d, tm), lambda i, j: (0, i)))
    return pl.pallas_call(
        body, name=name, grid=(t // tm, ncols // tn),
        in_specs=[pl.BlockSpec((tm, d), lambda i, j: (i, 0)),
                  pl.BlockSpec((None, 1, 3 * d), lambda i, j: (i // per_seq, 0, 0)),
                  pl.BlockSpec((d, tn), lambda i, j: (0, j0 + j)),
                  pl.BlockSpec((1, tn), lambda i, j: (0, j0 + j))],
        out_specs=out_specs, out_shape=out_shape,
        scratch_shapes=[pltpu.VMEM((tm, d), BF16)],
        compiler_params=_params("arbitrary", "arbitrary"))(x2, mod3, w_all, b_all)


def _slope(g, p, hh):
    head = 4 * g + 2 * p + hh
    return 2.0 ** (-ALIBI_MAX_EXP * (head + 1.0) / N_HEADS)


def _ld_rows(ref, start, n, stride):
    if stride == 1:
        return ref[pl.ds(start, n), :]
    return ref[pl.ds(start, n, stride=stride), :]


def _st_rows(ref, start, n, stride, val):
    if stride == 1:
        ref[pl.ds(start, n), :] = val
    else:
        ref[pl.ds(start, n, stride=stride), :] = val


def _sub_blocks(g, seq):
    return seq // DILATIONS[g] // SUB


def _key_rows(g, seq):
    return SUB if _sub_blocks(g, seq) == 1 else 2 * SUB


def _fill_bias(bias_ref, p, seq):
    for g in range(N_GROUPS):
        nk = _key_rows(g, seq)
        diff = lax.broadcasted_iota(jnp.int32, (SUB, nk), 0) - lax.broadcasted_iota(jnp.int32, (SUB, nk), 1)
        for i, off in enumerate((0, SUB)):
            if i == 1 and nk == SUB:
                continue
            delta = diff + off
            ok = (delta >= 0) & (delta <= SUB)
            dist = (delta * DILATIONS[g]).astype(F32)
            for hh in range(2):
                slope = jnp.where(p == 0, _slope(g, 0, hh), _slope(g, 1, hh))
                bias_ref[g, i, hh, :, 0:nk] = jnp.where(ok, -slope * dist, NEG)


def _to_sub_major(pa_ref, col, sub_ref, stage, dil, seq):
    cols = slice(col * SLAB, (col + 1) * SLAB)
    if dil == 1:
        sub_ref[...] = pa_ref[:, cols]
        return
    n = seq // dil
    stage[...] = pa_ref[:, cols].astype(F32)
    for r in range(dil):
        sub_ref[pl.ds(r * n, n), :] = stage[pl.ds(r, n, stride=dil), :].astype(BF16)


def _block_rows(it, g, seq):
    dil, nb = DILATIONS[g], _sub_blocks(g, seq)
    row0 = pl.multiple_of(it * SUB, SUB)
    if nb == 1:
        return row0, row0, 0, it
    blk = it % nb
    first = blk == 0
    krow0 = pl.multiple_of(row0 - jnp.where(first, 0, SUB), SUB)
    nat = row0 if dil == 1 else it // nb + dil * SUB * blk
    return row0, krow0, jnp.where(first, 0, 1), nat


def _nt(a, b):
    return lax.dot_general(a, b, (((1,), (1,)), ((), ())), preferred_element_type=F32)


def _tn(a, b):
    return lax.dot_general(a, b, (((0,), (0,)), ((), ())), preferred_element_type=F32)


def _head_sums(t):
    rows = t.shape[0]
    lo = jnp.broadcast_to(jnp.sum(t[:, :HEAD_DIM], axis=-1, keepdims=True), (rows, HEAD_DIM))
    hi = jnp.broadcast_to(jnp.sum(t[:, HEAD_DIM:], axis=-1, keepdims=True), (rows, HEAD_DIM))
    return jnp.concatenate([lo, hi], axis=-1)


def _attn_fwd(pa, bsz, seq):
    t = pa.shape[0]
    n_blocks = seq // SUB
    chunk = 256

    def body(pa_ref, o_ref, lse_ref, a_ref, sub, stage, bias_ref, s_buf, p_buf, l_buf):
        p = pl.program_id(1)
        _fill_bias(bias_ref, p, seq)
        head0 = lax.broadcasted_iota(jnp.int32, (SUB, SLAB), 1) < HEAD_DIM
        for g in range(N_GROUPS):
            dil = DILATIONS[g]
            for w in range(3):
                _to_sub_major(pa_ref, 3 * w + g, sub.at[w], stage, dil, seq)
            nk = _key_rows(g, seq)

            def trip(i, carry, g=g, dil=dil, nk=nk):
                places = [_block_rows(BLOCKS_PER_TRIP * i + j, g, seq) for j in range(BLOCKS_PER_TRIP)]
                for j, (row0, krow0, _, _) in enumerate(places):
                    q = sub[0, pl.ds(row0, SUB), :]
                    zero = jnp.zeros_like(q)
                    q2 = jnp.concatenate([jnp.where(head0, q, zero), jnp.where(head0, zero, q)], axis=0) * (HEAD_DIM ** -0.5)
                    s_buf[j, :, 0:nk] = _nt(q2, sub[1, pl.ds(krow0, nk), :])
                for j, (_, _, bi, _) in enumerate(places):
                    for c in range(0, 2 * SUB, SOFTMAX_ROWS):
                        hh, r = divmod(c, SUB)
                        s = s_buf[j, c:c + SOFTMAX_ROWS, 0:nk] + bias_ref[g, bi, hh, r:r + SOFTMAX_ROWS, 0:nk]
                        m = jnp.max(s, axis=-1, keepdims=True)
                        e = jnp.exp(s - m)
                        den = jnp.sum(e, axis=-1, keepdims=True)
                        p_buf[j, c:c + SOFTMAX_ROWS, 0:nk] = (e * (1.0 / den)).astype(BF16)
                        l_buf[j, c:c + SOFTMAX_ROWS, :] = jnp.broadcast_to(m + jnp.log(den), (SOFTMAX_ROWS, SLAB))
                for j, (_, krow0, _, nat) in enumerate(places):
                    o2 = jnp.dot(p_buf[j, :, 0:nk], sub[2, pl.ds(krow0, nk), :], preferred_element_type=F32)
                    _st_rows(o_ref.at[g], nat, SUB, dil, jnp.where(head0, o2[0:SUB], o2[SUB:2 * SUB]))
                    _st_rows(lse_ref.at[g], nat, SUB, dil, jnp.where(head0, l_buf[j, 0:SUB, :], l_buf[j, SUB:2 * SUB, :]))
                return carry

            lax.fori_loop(0, n_blocks // BLOCKS_PER_TRIP, trip, 0)

        def mix(i, carry):
            rows = pl.ds(pl.multiple_of(i * chunk, chunk), chunk)
            l0, l1, l2 = lse_ref[0, rows, :], lse_ref[1, rows, :], lse_ref[2, rows, :]
            m = jnp.maximum(jnp.maximum(l0, l1), l2)
            e0, e1, e2 = jnp.exp(l0 - m), jnp.exp(l1 - m), jnp.exp(l2 - m)
            tot = e0 + e1 + e2
            o = (e0 / tot) * o_ref[0, rows, :] + (e1 / tot) * o_ref[1, rows, :] + (e2 / tot) * o_ref[2, rows, :]
            z = pa_ref[rows, 9 * SLAB:10 * SLAB].astype(F32)
            a_ref[rows, :] = (o * (z * _sigmoid(z))).astype(BF16)
            return carry

        lax.fori_loop(0, seq // chunk, mix, 0)

    big = jax.ShapeDtypeStruct((N_GROUPS, t, 2 * SLAB), F32)
    return pl.pallas_call(
        body, name="attn_fwd", grid=(bsz, 2),
        in_specs=[pl.BlockSpec((seq, PAIR_COLS), lambda b, p: (b, p))],
        out_specs=[pl.BlockSpec((N_GROUPS, seq, SLAB), lambda b, p: (0, b, p)),
                   pl.BlockSpec((N_GROUPS, seq, SLAB), lambda b, p: (0, b, p)),
                   pl.BlockSpec((seq, SLAB), lambda b, p: (b, p))],
        out_shape=[big, big, jax.ShapeDtypeStruct((t, 2 * SLAB), BF16)],
        scratch_shapes=[pltpu.VMEM((3, seq, SLAB), BF16), pltpu.VMEM((seq, SLAB), F32),
                        pltpu.VMEM((N_GROUPS, 2, 2, SUB, 2 * SUB), F32), pltpu.VMEM((BLOCKS_PER_TRIP, 2 * SUB, 2 * SUB), F32),
                        pltpu.VMEM((BLOCKS_PER_TRIP, 2 * SUB, 2 * SUB), BF16), pltpu.VMEM((BLOCKS_PER_TRIP, 2 * SUB, SLAB), F32)],
        compiler_params=_params("arbitrary", "arbitrary"))(pa)


def _shift_down(v, k, rows):
    return jnp.where(rows >= k, pltpu.roll(v, k, 0), 0.0)


def _shift_up(v, k, rows):
    n = v.shape[0]
    return jnp.where(rows < n - k, pltpu.roll(v, n - k, 0), 0.0)


def _conv_fwd(pr, conv_w, bsz, seq, d):
    t = pr.shape[0]
    ct = CONV_TILE

    def body(p_ref, cw_ref, o_ref):
        u = p_ref[:, 2 * ct:3 * ct].astype(F32) * p_ref[:, 0:ct].astype(F32)
        cw = cw_ref[...]
        rows = lax.broadcasted_iota(jnp.int32, u.shape, 0)
        conv = cw[0:1, :] * _shift_down(u, 2, rows)
        conv = conv + cw[1:2, :] * _shift_down(u, 1, rows)
        conv = conv + cw[2:3, :] * u
        z = p_ref[:, 3 * ct:4 * ct].astype(F32)
        o_ref[...] = (p_ref[:, ct:2 * ct].astype(F32) * conv * (z * _sigmoid(z))).astype(BF16)

    return pl.pallas_call(
        body, name="conv_fwd", grid=(bsz, d // ct),
        in_specs=[pl.BlockSpec((seq, 4 * ct), lambda b, j: (b, j)), pl.BlockSpec((3, ct), lambda b, j: (0, j))],
        out_specs=pl.BlockSpec((seq, ct), lambda b, j: (b, j)),
        out_shape=jax.ShapeDtypeStruct((t, d), BF16), compiler_params=_params("parallel", "parallel"))(pr, conv_w)


def _tail(a_in, b_in, pr, x2, target2, mod3, w_pa, w_pc, w_out, b_out, ln_g, ln_b, seq, lay):
    t, d = x2.shape
    tm = 512
    per_seq = seq // tm
    n_steps = t // tm
    gate_blk = 4 * d // d

    def nt(a, b):
        return lax.dot_general(a, b, (((1,), (1,)), ((), ())), preferred_element_type=F32)

    def tn(a, b):
        return lax.dot_general(a, b, (((0,), (0,)), ((), ())), preferred_element_type=F32)

    def body(a_ref, b_ref, ga_ref, gb_ref, x_ref, tg_ref, mod_ref, wpa_ref, wpc_ref, wo_ref, bo_ref, lg_ref, lb_ref,
             dpg_ref, da_ref, db_ref, gx_ref, dgate_ref, small_ref, gwpa_hbm, gwpc_hbm, gwo_hbm,
             acc_pa, acc_pc, acc_o, sem):
        i = pl.program_id(0)

        @pl.when(i == 0)
        def _():
            acc_pa[...] = jnp.zeros_like(acc_pa)
            acc_pc[...] = jnp.zeros_like(acc_pc)
            acc_o[...] = jnp.zeros_like(acc_o)
            small_ref[...] = jnp.zeros_like(small_ref)

        @pl.when(i % per_seq == 0)
        def _():
            dgate_ref[...] = jnp.zeros_like(dgate_ref)

        halves = [slice(k * (tm // 2), (k + 1) * (tm // 2)) for k in range(2)]
        gate = mod_ref[:, 2 * d:3 * d]
        a_bf = [a_ref[rs, :] for rs in halves]
        b_bf = [b_ref[rs, :] for rs in halves]
        y_attn = [jnp.dot(a, wpa_ref[...], preferred_element_type=F32) for a in a_bf]
        y_conv = [jnp.dot(b, wpc_ref[...], preferred_element_type=F32) for b in b_bf]
        sa = [_sigmoid(ga_ref[rs, :].astype(F32)) for rs in halves]
        sb = [_sigmoid(gb_ref[rs, :].astype(F32)) for rs in halves]
        merged = [(sa[k] * y_attn[k] + sb[k] * y_conv[k]).astype(BF16) for k in range(2)]
        mo = [jnp.dot(m, wo_ref[...], preferred_element_type=F32) + bo_ref[...] for m in merged]
        d_mo_bf = []
        for k, rs in enumerate(halves):
            r = ALPHA * x_ref[rs, :] + gate * mo[k]
            mu = jnp.mean(r, axis=-1, keepdims=True)
            cen = r - mu
            var = jnp.mean(cen * cen, axis=-1, keepdims=True)
            rstd = lax.rsqrt(var + LN_EPS)
            xhat = cen * rstd
            err = xhat * lg_ref[...] + lb_ref[...] - tg_ref[rs, :]
            dy = err * (1.0 / d)
            dxhat = dy * lg_ref[...]
            dr = rstd * (dxhat - jnp.mean(dxhat, axis=-1, keepdims=True)
                         - xhat * jnp.mean(dxhat * xhat, axis=-1, keepdims=True))
            gx_ref[rs, :] = ALPHA * dr
            dgate_ref[...] += jnp.sum(dr * mo[k], axis=0, keepdims=True)
            d_mo = dr * gate
            small_ref[0:1, :] += jnp.sum(d_mo, axis=0, keepdims=True)
            small_ref[1:2, :] += jnp.sum(dy * xhat, axis=0, keepdims=True)
            small_ref[2:3, :] += jnp.sum(dy, axis=0, keepdims=True)
            small_ref[3:4, :] += jnp.sum(err * err, axis=0, keepdims=True)
            d_mo_bf.append(d_mo.astype(BF16))
        acc_o[...] += tn(jnp.concatenate(merged, axis=0), jnp.concatenate(d_mo_bf, axis=0))
        dmerged = [nt(g, wo_ref[...]) for g in d_mo_bf]
        dy_attn, dy_conv = [], []
        for k, rs in enumerate(halves):
            dy_attn.append((dmerged[k] * sa[k]).astype(BF16))
            dy_conv.append((dmerged[k] * sb[k]).astype(BF16))
            dpg_ref[rs, 0:d] = (dmerged[k] * y_attn[k] * sa[k] * (1.0 - sa[k])).astype(BF16)
            dpg_ref[rs, d:2 * d] = (dmerged[k] * y_conv[k] * sb[k] * (1.0 - sb[k])).astype(BF16)
        acc_pa[...] += tn(a_ref[...], jnp.concatenate(dy_attn, axis=0))
        acc_pc[...] += tn(b_ref[...], jnp.concatenate(dy_conv, axis=0))
        for k, rs in enumerate(halves):
            da_ref[rs, :] = nt(dy_attn[k], wpa_ref[...])
            db_ref[rs, :] = nt(dy_conv[k], wpc_ref[...])

        @pl.when(i == n_steps - 1)
        def _():
            copies = [pltpu.make_async_copy(acc_pa, gwpa_hbm, sem.at[0]), pltpu.make_async_copy(acc_pc, gwpc_hbm, sem.at[1]),
                      pltpu.make_async_copy(acc_o, gwo_hbm, sem.at[2])]
            for cp in copies:
                cp.start()
            for cp in copies:
                cp.wait()

    row = lambda w: pl.BlockSpec((tm, w), lambda i: (i, 0))
    const = lambda shp: pl.BlockSpec(shp, lambda i: (0,) * len(shp), pipeline_mode=pl.Buffered(1))
    any_spec = pl.BlockSpec(memory_space=pl.ANY)
    return pl.pallas_call(
        body, name="tail", grid=(n_steps,),
        in_specs=[row(Z_WIDTH), row(d),
                  pl.BlockSpec((tm, d), lambda i: (i, gate_blk)), pl.BlockSpec((tm, d), lambda i: (i, gate_blk + 1)),
                  row(d), row(d), pl.BlockSpec((None, 1, 3 * d), lambda i: (i // per_seq, 0, 0)),
                  const((Z_WIDTH, d)), const((d, d)), const((d, d)), const((1, d)), const((1, d)), const((1, d))],
        out_specs=[pl.BlockSpec((tm, 2 * d), lambda i: (i, lay.g0 // (2 * d))), row(Z_WIDTH), row(d), row(d),
                   pl.BlockSpec((None, 1, d), lambda i: (i // per_seq, 0, 0)), pl.BlockSpec((8, d), lambda i: (0, 0)),
                   HBM_SPEC, HBM_SPEC, HBM_SPEC],
        out_shape=[jax.ShapeDtypeStruct((t, lay.np), BF16), jax.ShapeDtypeStruct((t, Z_WIDTH), F32),
                   jax.ShapeDtypeStruct((t, d), F32), jax.ShapeDtypeStruct((t, d), F32),
                   jax.ShapeDtypeStruct((t // seq, 1, d), F32), jax.ShapeDtypeStruct((8, d), F32),
                   pltpu.HBM((Z_WIDTH, d), F32), pltpu.HBM((d, d), F32), pltpu.HBM((d, d), F32)],
        scratch_shapes=[pltpu.VMEM((Z_WIDTH, d), F32), pltpu.VMEM((d, d), F32), pltpu.VMEM((d, d), F32),
                        pltpu.SemaphoreType.DMA((3,))],
        compiler_params=_params("arbitrary"),
    )(a_in, b_in, pr, pr, x2, target2, mod3, w_pa, w_pc, w_out, b_out, ln_g, ln_b)


def _conv_bwd(dproj, db, pr, conv_w, bsz, seq, lay):
    d = lay.d
    ct = CONV_TILE
    base = lay.c0 // (4 * ct)

    def body(dp_in, db_ref, p_ref, cw_ref, dp_ref, gcw_ref):
        del dp_in
        u_x, g_b, g_c, z = [p_ref[:, k * ct:(k + 1) * ct].astype(F32) for k in range(4)]
        cw = cw_ref[...]
        u = g_c * u_x
        rows = lax.broadcasted_iota(jnp.int32, u.shape, 0)
        u1, u2 = _shift_down(u, 1, rows), _shift_down(u, 2, rows)
        conv = cw[0:1, :] * u2 + cw[1:2, :] * u1 + cw[2:3, :] * u
        sig = _sigmoid(z)
        sl = z * sig
        dbv = db_ref[...]
        gbc = g_b * conv
        dp_ref[:, ct:2 * ct] = (dbv * sl * conv).astype(BF16)
        dp_ref[:, 3 * ct:4 * ct] = (dbv * gbc * (sig * (1.0 + z * (1.0 - sig)))).astype(BF16)
        dconv = dbv * sl * g_b

        @pl.when(pl.program_id(1) == 0)
        def _():
            gcw_ref[...] = jnp.zeros_like(gcw_ref)

        gcw_ref[0:1, :] += jnp.sum(dconv * u2, axis=0, keepdims=True)
        gcw_ref[1:2, :] += jnp.sum(dconv * u1, axis=0, keepdims=True)
        gcw_ref[2:3, :] += jnp.sum(dconv * u, axis=0, keepdims=True)
        du = cw[2:3, :] * dconv + cw[1:2, :] * _shift_up(dconv, 1, rows) + cw[0:1, :] * _shift_up(dconv, 2, rows)
        dp_ref[:, 0:ct] = (du * g_c).astype(BF16)
        dp_ref[:, 2 * ct:3 * ct] = (du * u_x).astype(BF16)

    return pl.pallas_call(
        body, name="conv_bwd", grid=(d // ct, bsz),
        in_specs=[pl.BlockSpec(memory_space=pl.ANY), pl.BlockSpec((seq, ct), lambda j, b: (b, j)),
                  pl.BlockSpec((seq, 4 * ct), lambda j, b: (b, j)), pl.BlockSpec((3, ct), lambda j, b: (0, j))],
        out_specs=[pl.BlockSpec((seq, 4 * ct), lambda j, b: (b, base + j)), pl.BlockSpec((8, ct), lambda j, b: (0, j))],
        out_shape=[jax.ShapeDtypeStruct(dproj.shape, BF16), jax.ShapeDtypeStruct((8, d), F32)],
        input_output_aliases={0: 0}, compiler_params=_params("arbitrary", "arbitrary"))(dproj, db, pr, conv_w)


def _attn_bwd(dproj, pa, o_all, lse_all, da, bsz, seq, after):
    n_blocks = seq // SUB
    chunk = 256

    def body(dp_in, pa_ref, o_ref, lse_ref, da_ref, after_ref, dp_ref, sub, stage, dsub, dog, cvec, bias_ref,
             s_buf, dp_buf, ds_buf, pb_buf, q2_buf, do2_buf, l_buf, c_buf):
        del dp_in, after_ref
        p = pl.program_id(1)
        _fill_bias(bias_ref, p, seq)
        head0 = lax.broadcasted_iota(jnp.int32, (SUB, SLAB), 1) < HEAD_DIM

        def mix_bwd(i, carry):
            rows = pl.ds(pl.multiple_of(i * chunk, chunk), chunk)
            ls = [lse_ref[g, rows, :] for g in range(N_GROUPS)]
            os_ = [o_ref[g, rows, :] for g in range(N_GROUPS)]
            m = jnp.maximum(jnp.maximum(ls[0], ls[1]), ls[2])
            es = [jnp.exp(l - m) for l in ls]
            tot = es[0] + es[1] + es[2]
            ws = [e / tot for e in es]
            o = ws[0] * os_[0] + ws[1] * os_[1] + ws[2] * os_[2]
            z = pa_ref[rows, 9 * SLAB:10 * SLAB].astype(F32)
            sig = _sigmoid(z)
            dav = da_ref[rows, :]
            do = dav * (z * sig)
            dp_ref[rows, 9 * SLAB:10 * SLAB] = (dav * o * (sig * (1.0 + z * (1.0 - sig)))).astype(BF16)
            wsum = _head_sums(do * o)
            for g in range(N_GROUPS):
                dog[g, rows, :] = ws[g] * do
                cvec[g, rows, :] = -(ws[g] * wsum)
            return carry

        lax.fori_loop(0, seq // chunk, mix_bwd, 0)

        for g in range(N_GROUPS):
            dil = DILATIONS[g]
            for w in range(3):
                _to_sub_major(pa_ref, 3 * w + g, sub.at[w], stage, dil, seq)
            dsub[1] = jnp.zeros((seq, SLAB), F32)
            dsub[2] = jnp.zeros((seq, SLAB), F32)
            nk = _key_rows(g, seq)

            def trip(i, carry, g=g, dil=dil, nk=nk):
                places = [_block_rows(BLOCKS_PER_TRIP * i + j, g, seq) for j in range(BLOCKS_PER_TRIP)]
                for j, (row0, krow0, _, nat) in enumerate(places):
                    q = sub[0, pl.ds(row0, SUB), :]
                    do = _ld_rows(dog.at[g], nat, SUB, dil).astype(BF16)
                    zero = jnp.zeros_like(q)
                    q2 = jnp.concatenate([jnp.where(head0, q, zero), jnp.where(head0, zero, q)], axis=0)
                    do2 = jnp.concatenate([jnp.where(head0, do, zero), jnp.where(head0, zero, do)], axis=0)
                    q2_buf[j] = q2
                    do2_buf[j] = do2
                    s_buf[j, :, 0:nk] = _nt(q2 * (HEAD_DIM ** -0.5), sub[1, pl.ds(krow0, nk), :])
                    dp_buf[j, :, 0:nk] = _nt(do2, sub[2, pl.ds(krow0, nk), :])
                    l_buf[j] = _ld_rows(lse_ref.at[g], nat, SUB, dil)
                    c_buf[j] = _ld_rows(cvec.at[g], nat, SUB, dil)
                for j, (_, _, bi, _) in enumerate(places):
                    for c in range(0, 2 * SUB, SOFTMAX_ROWS):
                        hh, r = divmod(c, SUB)
                        lane = hh * HEAD_DIM
                        s = s_buf[j, c:c + SOFTMAX_ROWS, 0:nk] + bias_ref[g, bi, hh, r:r + SOFTMAX_ROWS, 0:nk]
                        prob = jnp.exp(s - l_buf[j, r:r + SOFTMAX_ROWS, lane:lane + 1])
                        dprob = dp_buf[j, c:c + SOFTMAX_ROWS, 0:nk] + c_buf[j, r:r + SOFTMAX_ROWS, lane:lane + 1]
                        ds_buf[j, c:c + SOFTMAX_ROWS, 0:nk] = (prob * dprob * (HEAD_DIM ** -0.5)).astype(BF16)
                        pb_buf[j, c:c + SOFTMAX_ROWS, 0:nk] = prob.astype(BF16)
                for j, (row0, krow0, _, _) in enumerate(places):
                    ds = ds_buf[j, :, 0:nk]
                    dq2 = jnp.dot(ds, sub[1, pl.ds(krow0, nk), :], preferred_element_type=F32)
                    dsub[0, pl.ds(row0, SUB), :] = jnp.where(head0, dq2[0:SUB], dq2[SUB:2 * SUB])
                    dsub[1, pl.ds(krow0, nk), :] += _tn(ds, q2_buf[j])
                    dsub[2, pl.ds(krow0, nk), :] += _tn(pb_buf[j, :, 0:nk], do2_buf[j])
                return carry

            lax.fori_loop(0, n_blocks // BLOCKS_PER_TRIP, trip, 0)
            for w in range(3):
                cols = slice((3 * w + g) * SLAB, (3 * w + g + 1) * SLAB)
                if dil == 1:
                    dp_ref[:, cols] = dsub[w].astype(BF16)
                else:
                    n = seq // dil
                    for r in range(dil):
                        stage[pl.ds(r, n, stride=dil), :] = dsub[w, pl.ds(r * n, n), :]
                    dp_ref[:, cols] = stage[...].astype(BF16)

    return pl.pallas_call(
        body, name="attn_bwd", grid=(bsz, 2),
        in_specs=[pl.BlockSpec(memory_space=pl.ANY), pl.BlockSpec((seq, PAIR_COLS), lambda b, p: (b, p)),
                  pl.BlockSpec((N_GROUPS, seq, SLAB), lambda b, p: (0, b, p)),
                  pl.BlockSpec((N_GROUPS, seq, SLAB), lambda b, p: (0, b, p)),
                  pl.BlockSpec((seq, SLAB), lambda b, p: (b, p)), pl.BlockSpec(memory_space=pl.ANY)],
        out_specs=pl.BlockSpec((seq, PAIR_COLS), lambda b, p: (b, p)),
        out_shape=jax.ShapeDtypeStruct(dproj.shape, BF16), input_output_aliases={0: 0},
        scratch_shapes=[pltpu.VMEM((3, seq, SLAB), BF16), pltpu.VMEM((seq, SLAB), F32), pltpu.VMEM((3, seq, SLAB), F32),
                        pltpu.VMEM((3, seq, SLAB), F32), pltpu.VMEM((3, seq, SLAB), F32),
                        pltpu.VMEM((N_GROUPS, 2, 2, SUB, 2 * SUB), F32),
                        pltpu.VMEM((BLOCKS_PER_TRIP, 2 * SUB, 2 * SUB), F32), pltpu.VMEM((BLOCKS_PER_TRIP, 2 * SUB, 2 * SUB), F32),
                        pltpu.VMEM((BLOCKS_PER_TRIP, 2 * SUB, 2 * SUB), BF16), pltpu.VMEM((BLOCKS_PER_TRIP, 2 * SUB, 2 * SUB), BF16),
                        pltpu.VMEM((BLOCKS_PER_TRIP, 2 * SUB, SLAB), BF16), pltpu.VMEM((BLOCKS_PER_TRIP, 2 * SUB, SLAB), BF16),
                        pltpu.VMEM((BLOCKS_PER_TRIP, SUB, SLAB), F32), pltpu.VMEM((BLOCKS_PER_TRIP, SUB, SLAB), F32)],
        compiler_params=_params("arbitrary", "arbitrary"))(dproj, pa, o_all, lse_all, da, after)


def _grad_h(dproj, w_all, gx0, x2, mod3, seq, lay):
    t, d = x2.shape
    tm, tn = min(512, seq), min(512, d)
    per_seq = seq // tm

    def body(dp_ref, w_ref, gx0_ref, x_ref, scale_ref, gx_ref, dmod_ref):
        dh = _nt(dp_ref[:, 0:ATT], w_ref[:, 0:ATT]) + _nt(dp_ref[:, lay.c0:], w_ref[:, lay.c0:])
        gx_ref[...] = gx0_ref[...] + dh * (1.0 + scale_ref[...])

        @pl.when(pl.program_id(1) % per_seq == 0)
        def _():
            dmod_ref[...] = jnp.zeros_like(dmod_ref)

        dmod_ref[0:1, :] += jnp.sum(dh, axis=0, keepdims=True)
        dmod_ref[1:2, :] += jnp.sum(dh * x_ref[...], axis=0, keepdims=True)

    tile = pl.BlockSpec((tm, tn), lambda j, i: (i, j))
    return pl.pallas_call(
        body, name="grad_h", grid=(d // tn, t // tm),
        in_specs=[pl.BlockSpec((tm, lay.np), lambda j, i: (i, 0)), pl.BlockSpec((tn, lay.np), lambda j, i: (j, 0)),
                  tile, tile,
                  pl.BlockSpec((None, 1, tn), lambda j, i: (i // per_seq, 0, d // tn + j))],
        out_specs=[tile, pl.BlockSpec((None, 8, tn), lambda j, i: (i // per_seq, 0, j))],
        out_shape=[jax.ShapeDtypeStruct((t, d), F32), jax.ShapeDtypeStruct((t // seq, 8, d), F32)],
        compiler_params=_params("arbitrary", "arbitrary"))(dproj, w_all, gx0, x2, mod3)


def _grad_w_in(ht, dproj, seq, lay):
    d, t = ht.shape
    tm = seq
    n_i = t // tm

    def make_body(n_skip, n_pieces, tn, nat_tile):
        def body(*refs):
            refs = refs[n_skip:]
            ht_ref, dp_refs = refs[0], refs[1:1 + n_pieces]
            gw_hbm, gb_hbm, acc, bacc, gw_out, gb_out, sem = refs[1 + n_pieces:]
            i, j = pl.program_id(0), pl.program_id(1)
            dp = dp_refs[0][...] if n_pieces == 1 else jnp.concatenate([r[...] for r in dp_refs], axis=1)
            part = jnp.dot(ht_ref[...], dp, preferred_element_type=F32)
            bpart = jnp.sum(dp.astype(F32), axis=0, keepdims=True)

            if n_i > 1:
                @pl.when(i == 0)
                def _():
                    acc[j] = part
                    bacc[j] = bpart

                @pl.when((i > 0) & (i < n_i - 1))
                def _():
                    acc[j] += part
                    bacc[j] += bpart

            @pl.when(i == n_i - 1)
            def _():
                gw_out[...] = ((part + acc[j]) if n_i > 1 else part).astype(BF16)
                gb_out[...] = (bpart + bacc[j]) if n_i > 1 else bpart
                cols = pl.ds(pl.multiple_of(nat_tile(j) * tn, SLAB), tn)
                copies = [pltpu.make_async_copy(gw_out, gw_hbm.at[:, cols], sem.at[0]),
                          pltpu.make_async_copy(gb_out, gb_hbm.at[:, cols], sem.at[1])]
                for cp in copies:
                    cp.start()
                for cp in copies:
                    cp.wait()
        return body

    def call(name, pieces, n_tiles, nat_tile, prev):
        tn = sum(w for w, _ in pieces)
        any_spec = pl.BlockSpec(memory_space=pl.ANY)
        in_specs = [pl.BlockSpec((d, tm), lambda i, j: (0, i))]
        in_specs += [pl.BlockSpec((tm, w), lambda i, j, f=f: (i, f(j))) for w, f in pieces]
        args = [ht] + [dproj] * len(pieces)
        aliases = {}
        if prev is not None:
            in_specs = [any_spec] * 2 + in_specs
            args = list(prev) + args
            aliases = {0: 0, 1: 1}
        return pl.pallas_call(
            make_body(0 if prev is None else 2, len(pieces), tn, nat_tile), name=name, grid=(n_i, n_tiles), in_specs=in_specs,
            out_specs=[any_spec, any_spec],
            out_shape=[jax.ShapeDtypeStruct((d, lay.din), BF16), jax.ShapeDtypeStruct((1, lay.din), F32)],
            input_output_aliases=aliases,
            scratch_shapes=[pltpu.VMEM((n_tiles, d, tn), F32), pltpu.VMEM((n_tiles, 1, tn), F32), pltpu.VMEM((d, tn), BF16),
                            pltpu.VMEM((1, tn), F32), pltpu.SemaphoreType.DMA((2,))],
            compiler_params=_params("arbitrary", "arbitrary"))(*args)

    attn_pieces = [(SLAB, lambda j, m=m: (m % 2) * PAIR_SLABS + 2 * j + m // 2) for m in range(4)]
    first = call("grad_w_in_attn", attn_pieces, ATT // 512, lambda j: j, None)
    base = lay.c0 // CONV_TILE
    nct = lay.n_conv_tiles
    if nct % 2:
        return call("grad_w_in_rest", [(CONV_TILE, lambda j: base + j)], 6 * d // CONV_TILE, lay.rest_nat_tile, first)
    half = nct // 2

    def rest_piece(m):
        def perm_tile(j):
            conv = base + 4 * (2 * (j % half) + m) + j // half
            return jnp.where(j < 4 * half, conv, base + 2 * j + m)
        return (CONV_TILE, perm_tile)

    return call("grad_w_in_rest", [rest_piece(0), rest_piece(1)], 6 * d // 512, lambda j: ATT // 512 + j, first)


def _pack_rows(parts, width=128):
    flat = [p.reshape(-1) for p in parts]
    spans, rows = [], 0
    padded = []
    for f in flat:
        n = -(-f.shape[0] // (8 * width)) * 8
        padded.append(jnp.pad(f, (0, n * width - f.shape[0])).reshape(n, width))
        spans.append((rows, f.shape[0]))
        rows += n
    return jnp.concatenate(padded, axis=0), spans


def _unpack_rows(packed, spans, shapes, width=128):
    out = []
    for (row, n), shp in zip(spans, shapes):
        rows = -(-n // width)
        out.append(packed[row:row + rows].reshape(-1)[:n].reshape(shp))
    return out


def kernel(x, c, w_ada, b_ada, w_in, b_in, conv_w, w_proj_attn, w_proj_conv, w_out, b_out, ln_g, ln_b, loss_target, m_w_ada, m_b_ada, m_w_in, m_b_in, m_conv_w, m_w_proj_attn, m_w_proj_conv, m_w_out, m_b_out, m_ln_g, m_ln_b, v_w_ada, v_b_ada, v_w_in, v_b_in, v_conv_w, v_w_proj_attn, v_w_proj_conv, v_w_out, v_b_out, v_ln_g, v_ln_b):
    bsz, seq, d = x.shape
    t = bsz * seq
    lay = _Layout(d)
    col_sharded = [True, True, False, False]
    red_w = [w_in[0], w_proj_attn[0], w_proj_conv[0], w_out[0]]
    chip = 2 * lax.axis_index("x") + lax.axis_index("y")
    chip1 = chip.astype(jnp.int32).reshape(1)
    core1 = lax.axis_index("c").astype(jnp.int32).reshape(1)
    place = jnp.stack([chip, lax.axis_index("c")]).astype(jnp.int32)
    x2 = x.reshape(t, d)
    target2 = loss_target.reshape(t, d)

    mod, act_all = _ada_exchange(c, w_ada[0], b_ada)
    mod3 = mod.reshape(bsz, 1, 3 * d)

    cw_pad = jnp.pad(conv_w[0], ((0, 5), (0, 0))) + 0.0 * mod[0, 0]
    own_in_full = [_cast_into_full(red_w[0], col_sharded[0], chip1, "cast_shard_in")]
    own_in_full += list(_cast_into_full_small(red_w[1:], col_sharded[1:], chip1, "cast_shards_late"))
    (wi_f,), cw8 = _gather_weights(own_in_full[:1], col_sharded[:1], cw_pad)
    cw_full = cw8[0:3]
    late_copies = _direct_gather_copies(col_sharded[1:])
    late_send, late_recv, late_flying, late_token = _start_copies("gather_late_start", own_in_full[1:], (18,), cw8, late_copies)
    w_all = _permute_w_in(wi_f, lay)
    b_all = lay.perm_vector(b_in) + late_token[0, 0]

    rest_tn = 1024 if (6 * d) % 1024 == 0 else 512
    pa, = _project(x2, mod3, w_all, b_all, seq, 0, ATT, PAIR_COLS, BF16, False, "project_attn")
    pr, ht = _project(x2, mod3, w_all, b_all, seq, lay.c0, 6 * d, rest_tn, BF16, True, "project_rest")
    o_all, lse_all, a_in = _attn_fwd(pa, bsz, seq)
    b_in_act = _conv_fwd(pr, cw_full, bsz, seq, d)
    wpa_f, wpc_f, wo_f = _wait_copies("gather_late_wait", late_flying, late_send, late_recv, b_in_act, late_copies)
    (dproj, da_in, db_in, gx0, dgate, small_tail, gw_pa, gw_pc, gw_out) = _tail(
        a_in, b_in_act, pr, x2, target2, mod3, wpa_f, wpc_f, wo_f, b_out, ln_g, ln_b, seq, lay)

    late_views = _shard_views([gw_pa, gw_pc, gw_out], col_sharded[1:])
    late_lands = [lax.empty((v.shape[0], v.shape[1] // 2, v.shape[2]), v.dtype) for v in late_views]
    xl_send, xl_recv, xl_fly, xl_tok = _start_copies("grads_pair_exchange_late_start", late_views + late_lands, (3,), gw_out,
                                                     _pair_exchange_copies(3, False))
    dproj, gcw = _conv_bwd(dproj, db_in, pr, cw_full + xl_tok[0, 0], bsz, seq, lay)
    xl_done = _wait_copies("grads_pair_exchange_late_wait", xl_fly, xl_send, xl_recv, gcw, _pair_exchange_copies(3, True))
    late_parts = list(_pair_sum_small(xl_done[:3], xl_done[3:], core1, "grads_pair_sum_late"))

    late_cross = _chip_scatter_copies(3, col_sharded[1:])
    late_zone = [lax.empty((3, p.shape[1], _piece_cols(p, cs)), p.dtype) for p, cs in zip(late_parts, col_sharded[1:])]
    sl_send, sl_recv, sl_fly, sl_tok = _start_copies("grads_scatter_late_start", late_parts + late_zone, (9,), late_parts[0],
                                                     late_cross)
    dproj = _attn_bwd(dproj, pa, o_all, lse_all, da_in, bsz, seq, sl_tok)
    gw_in_bf, gb_in = _grad_w_in(ht, dproj, seq, lay)
    sl_done = _wait_copies("grads_scatter_late_wait", sl_fly, sl_send, sl_recv, gw_in_bf, late_cross)
    late_red = list(_chip_sum_small(sl_done[:3], sl_done[3:], col_sharded[1:], place, "grads_chip_sum_late"))

    in_view = _shard_views([gw_in_bf], col_sharded[:1])
    in_got = _pair_exchange_halves(in_view, "grads_pair_exchange_in")
    in_part = _pair_sum(in_view[0], in_got[0], core1, "grads_pair_sum_in")
    in_cross = _chip_scatter_copies(1, col_sharded[:1])
    in_zone = [lax.empty((3, in_part.shape[1], _piece_cols(in_part, True)), in_part.dtype)]
    late_views1 = [f.reshape(1, *f.shape) for f in late_red]
    si_send, si_recv, si_fly, si_tok = _start_copies(
        "grads_scatter_in_join_late_start", [in_part] + in_zone + late_views1, (6,), late_red[0],
        _both_copies(in_cross, 2, _pair_join_copies(3, False, sem0=3)))
    grad_x2, dmod = _grad_h(dproj, w_all, gx0, x2, mod3 + si_tok[0, 0], seq, lay)
    si_done = _wait_copies("grads_scatter_in_join_late_wait", si_fly, si_send, si_recv, grad_x2,
                           _both_copies(in_cross, 2, _pair_join_copies(3, True, sem0=3)))
    late_joined = [f[0] for f in si_done[2:]]
    in_red = _chip_sum(si_done[0], si_done[1], True, place, "grads_chip_sum_in")

    d_ada = jnp.concatenate([dmod[:, 0, :], dmod[:, 1, :], dgate[:, 0, :]], axis=1)
    pieces = [small_tail[3], jnp.sum(d_ada, axis=0), gb_in[0], small_tail[0], small_tail[1], small_tail[2], gcw[0:3]]
    packed, spans = _pack_rows(pieces)
    kept_in, _ = _pack_rows([d_ada])
    rows_all = jnp.concatenate([packed, kept_in], axis=0)
    small_land = lax.empty((N_DEV,) + rows_all.shape, F32)
    sm_send, sm_recv, sm_fly, sm_tok = _start_copies(
        "small_gather_join_in_start", [rows_all, small_land, in_red.reshape(1, *in_red.shape)], (N_DEV,), in_red,
        _both_copies(_small_gather_copies, 2, _pair_join_copies(1, False, sem0=N_DEV - 1)))
    big_w = [w_ada[0]] + red_w
    big_m = [m_w_ada[0], m_w_in[0], m_w_proj_attn[0], m_w_proj_conv[0], m_w_out[0]]
    big_v = [v_w_ada[0], v_w_in[0], v_w_proj_attn[0], v_w_proj_conv[0], v_w_out[0]]
    big_out = [None] * 5
    for w in range(3):
        big_out[2 + w] = _adamw(big_w[2 + w], late_joined[w], big_m[2 + w], big_v[2 + w], f"adamw_{2 + w}", sm_tok)
    sm_done = _wait_copies("small_gather_join_in_wait", sm_fly, sm_send, sm_recv, big_out[4][0],
                           _both_copies(_small_gather_copies, 2, _pair_join_copies(1, True, sem0=N_DEV - 1)))
    me1 = (4 * lax.axis_index("x") + 2 * lax.axis_index("y") + lax.axis_index("c")).astype(jnp.int32).reshape(1)
    summed, kept = _small_sum(sm_done[0], sm_done[1], me1, packed.shape[0], d)
    loss = summed[0, 0]
    _, g_b_ada, g_b_in, g_b_out, g_ln_g, g_ln_b, g_cw_full = _unpack_rows(
        summed, spans, [(d,), (3 * d,), (lay.din,), (d,), (d,), (d,), (3, d)])
    g_cw = lax.dynamic_slice(g_cw_full, (0, chip * (d // N_CHIPS)), (3, d // N_CHIPS))
    d_ada_all = kept.reshape(N_DEV, -1)[:, :bsz * 3 * d].reshape(N_DEV * bsz, 3 * d)
    ada_cols = 3 * d // N_CHIPS
    g_w_ada = _grad_w_ada_cols(act_all.T, lax.dynamic_slice(d_ada_all, (0, chip * ada_cols), (N_DEV * bsz, ada_cols)))

    big_out[0] = _adamw(big_w[0], g_w_ada, big_m[0], big_v[0], "adamw_0", None)
    small_w = [b_ada, b_in, conv_w[0], b_out, ln_g, ln_b]
    small_g = [g_b_ada, g_b_in, g_cw, g_b_out, g_ln_g, g_ln_b]
    small_m = [m_b_ada, m_b_in, m_conv_w[0], m_b_out, m_ln_g, m_ln_b]
    small_v = [v_b_ada, v_b_in, v_conv_w[0], v_b_out, v_ln_g, v_ln_b]
    pw, sp = _pack_rows(small_w)
    pg, _ = _pack_rows(small_g)
    pm, _ = _pack_rows(small_m)
    pv, _ = _pack_rows(small_v)
    sd, sm, sv = _adamw(pw, pg, pm, pv, "adamw_small", None)
    big_out[1] = _adamw(big_w[1], sm_done[2][0], big_m[1], big_v[1], "adamw_1", None, also_g=True)
    g_big = [g_w_ada, big_out[1][3]] + late_joined
    shapes = [a.shape for a in small_w]
    sd, sm, sv = _unpack_rows(sd, sp, shapes), _unpack_rows(sm, sp, shapes), _unpack_rows(sv, sp, shapes)

    def order(wa, bA, wi, bI, cw, wpa, wpc, wo, bO, lg, lb):
        return (wa[None], bA, wi[None], bI, cw[None], wpa[None], wpc[None], wo[None], bO, lg, lb)

    sg = [g.reshape(s) for g, s in zip(small_g, shapes)]
    grads_out = order(g_big[0], sg[0], g_big[1], sg[1], sg[2], g_big[2], g_big[3], g_big[4], sg[3], sg[4], sg[5])
    outs = []
    for idx, small in enumerate((sd, sm, sv)):
        outs.append(order(big_out[0][idx], small[0], big_out[1][idx], small[1], small[2], big_out[2][idx],
                          big_out[3][idx], big_out[4][idx], small[3], small[4], small[5]))
    return (loss, grad_x2.reshape(bsz, seq, d), *grads_out, *outs[0], *outs[1], *outs[2])
```

```python
import jax
import jax.numpy as jnp
from jax import lax
from jax.experimental import pallas as pl
from jax.experimental.pallas import tpu as pltpu

F32 = jnp.float32
BF16 = jnp.bfloat16
MESH = pl.DeviceIdType.MESH

HEAD_DIM = 64
N_GROUPS = 3
DILATIONS = (1, 4, 16)
N_HEADS = 12
SUB = 128
Q_WIDTH = 768
Z_WIDTH = 256
ATT = 3 * Q_WIDTH + Z_WIDTH
SLAB = 128
PAIR_SLABS = 10
PAIR_COLS = PAIR_SLABS * SLAB
CONV_TILE = 256
SOFTMAX_ROWS = 32
BLOCKS_PER_TRIP = 8
ALIBI_MAX_EXP = 8.0
ALPHA = 2.0 ** 0.25
LN_EPS = 1e-5
ADAM_LR, ADAM_B1, ADAM_B2, ADAM_EPS, ADAM_WD, ADAM_STEP = 0.001, 0.9, 0.999, 1e-08, 0.01, 10
N_CHIPS = 4
N_DEV = 8
VMEM_LIMIT_V7X = 60 * 1024 * 1024
NEG = -1e30


def _params(*sem):
    return pltpu.CompilerParams(dimension_semantics=sem, vmem_limit_bytes=VMEM_LIMIT_V7X)


def _sigmoid(v):
    return 0.5 * jnp.tanh(0.5 * v) + 0.5


class _Layout:
    def __init__(self, d):
        self.d = d
        self.din = ATT + 6 * d
        c0 = 3072
        while c0 % (2 * d):
            c0 += 1024
        self.c0, self.g0, self.np = c0, c0 + 4 * d, c0 + 6 * d
        self.n_conv_tiles = d // CONV_TILE

    def attn_nat_slab(self, s):
        p, i = s // PAIR_SLABS, s % PAIR_SLABS
        return jnp.where(i < 9, (i // 3) * 6 + (i % 3) * 2 + p, 18 + p)

    def rest_nat_tile(self, t):
        n4 = 4 * self.n_conv_tiles
        conv = ATT // CONV_TILE + (t % 4) * self.n_conv_tiles + t // 4
        return jnp.where(t < n4, conv, ATT // CONV_TILE + t)

    def perm_vector(self, v):
        parts = []
        for s in range(2 * PAIR_SLABS):
            p, i = divmod(s, PAIR_SLABS)
            ns = (i // 3) * 6 + (i % 3) * 2 + p if i < 9 else 18 + p
            parts.append(v[:, ns * SLAB:(ns + 1) * SLAB])
        parts.append(jnp.zeros((1, self.c0 - ATT), v.dtype))
        for j in range(self.n_conv_tiles):
            for k in range(4):
                a = ATT + k * self.d + j * CONV_TILE
                parts.append(v[:, a:a + CONV_TILE])
        parts.append(v[:, ATT + 4 * self.d:])
        return jnp.concatenate(parts, axis=1)


def _place():
    return lax.axis_index("x"), lax.axis_index("y"), lax.axis_index("c")


def _other_chips(x, y):
    return [(1 - x, y), (x, 1 - y), (1 - x, 1 - y)]


def _shard_of(ref, col_sharded, chip, half=None):
    if col_sharded:
        cs = ref.shape[1] // N_CHIPS
        cols = pl.ds(pl.multiple_of(chip * cs, SLAB), cs)
        if half is None:
            return ref.at[:, cols]
        n = ref.shape[0] // 2
        return ref.at[pl.ds(half * n, n), cols]
    rs = ref.shape[0] // N_CHIPS
    if half is None:
        return ref.at[pl.ds(chip * rs, rs)]
    return ref.at[pl.ds(chip * rs + half * (rs // 2), rs // 2)]


GATHER_PIECES = 4


def _cast_into_full(shard, col_sharded, chip, name):
    rows, cols = shard.shape
    tr = _row_tile(rows, cols)
    nb = rows // tr

    def body(chip_ref, s_ref, o_ref):
        del chip_ref
        o_ref[...] = s_ref[...].astype(BF16)

    if col_sharded:
        full, out_spec = (rows, cols * N_CHIPS), pl.BlockSpec((tr, cols), lambda i, ch: (i, ch[0]))
    else:
        full, out_spec = (rows * N_CHIPS, cols), pl.BlockSpec((tr, cols), lambda i, ch: (ch[0] * nb + i, 0))
    return pl.pallas_call(
        body, name=name,
        grid_spec=pltpu.PrefetchScalarGridSpec(num_scalar_prefetch=1, grid=(nb,),
                                               in_specs=[pl.BlockSpec((tr, cols), lambda i, ch: (i, 0))], out_specs=out_spec),
        out_shape=jax.ShapeDtypeStruct(full, BF16), compiler_params=_params("parallel"))(chip, shard)


def _cast_into_full_small(shards, col_sharded, chip, name):
    n = len(shards)

    def body(chip_ref, *refs):
        del chip_ref
        for w in range(n):
            refs[n + w][...] = refs[w][...].astype(BF16)

    in_specs = [pl.BlockSpec(s.shape, lambda i, ch: (0, 0)) for s in shards]
    out_specs = [pl.BlockSpec(s.shape, (lambda i, ch: (0, ch[0])) if cs else (lambda i, ch: (ch[0], 0)))
                 for s, cs in zip(shards, col_sharded)]
    fulls = [(s.shape[0], s.shape[1] * N_CHIPS) if cs else (s.shape[0] * N_CHIPS, s.shape[1]) for s, cs in zip(shards, col_sharded)]
    return pl.pallas_call(
        body, name=name,
        grid_spec=pltpu.PrefetchScalarGridSpec(num_scalar_prefetch=1, grid=(1,), in_specs=in_specs, out_specs=out_specs),
        out_shape=[jax.ShapeDtypeStruct(f, BF16) for f in fulls], compiler_params=_params("arbitrary"))(chip, *shards)


def _gather_weights(fulls, col_sharded, small):
    n = len(fulls)
    kp = GATHER_PIECES

    def piece(ref, cs, chip, half, k):
        if cs:
            width = ref.shape[1] // N_CHIPS
            rows = ref.shape[0] // 2 // kp
            return ref.at[pl.ds(half * (ref.shape[0] // 2) + k * rows, rows), pl.ds(pl.multiple_of(chip * width, SLAB), width)]
        rs = ref.shape[0] // N_CHIPS
        rows = rs // 2 // kp
        return ref.at[pl.ds(chip * rs + half * (rs // 2) + k * rows, rows)]

    def body(*refs):
        sm_in, outs, sm_out = refs[n], refs[n + 1:2 * n + 1], refs[2 * n + 1]
        send, recv, fsend, frecv, lsem, ssend, srecv = refs[2 * n + 2:]
        x, y, c = _place()
        mine = 2 * x + y
        sibling = (x, y, 1 - c)
        first = (x ^ (1 - c), y ^ c)
        second = (x ^ c, y ^ (1 - c))
        diagonal = (1 - x, 1 - y)
        sources = [first, second, diagonal]
        senders = [first, second, second]

        def copy(ref, sems, slot, to):
            return pltpu.make_async_remote_copy(src_ref=ref, dst_ref=ref, send_sem=sems[0].at[slot], recv_sem=sems[1].at[slot],
                                                device_id=to, device_id_type=MESH)

        local = pltpu.make_async_copy(sm_in, _shard_of(sm_out, True, mine), lsem)
        local.start()
        sends = []
        for k, (cx, cy) in enumerate(_other_chips(x, y)):
            cp = pltpu.make_async_remote_copy(src_ref=sm_in, dst_ref=_shard_of(sm_out, True, mine), send_sem=ssend.at[k],
                                              recv_sem=srecv.at[k], device_id=(cx, cy, c), device_id_type=MESH)
            cp.start()
            sends.append(cp)
        for k in range(kp):
            for w in range(n):
                own = piece(outs[w], col_sharded[w], mine, c, k)
                for slot, chip in enumerate((first, second)):
                    cp = copy(own, (send, recv), (w * 3 + slot) * kp + k, (*chip, c))
                    cp.start()
                    sends.append(cp)
        for slot in range(3):
            source = 2 * sources[slot][0] + sources[slot][1]
            for k in range(kp):
                for w in range(n):
                    landed = piece(outs[w], col_sharded[w], source, c, k)
                    copy(landed, (send, recv), (w * 3 + slot) * kp + k, (*senders[slot], c)).wait_recv()
                    if slot == 0:
                        cp = copy(landed, (send, recv), (w * 3 + 2) * kp + k, (*second, c))
                        cp.start()
                        sends.append(cp)
                    cp = copy(landed, (fsend, frecv), (w * 3 + slot) * kp + k, sibling)
                    cp.start()
                    sends.append(cp)
        for slot, chip in enumerate((second, first, diagonal)):
            for k in range(kp):
                for w in range(n):
                    passed = piece(outs[w], col_sharded[w], 2 * chip[0] + chip[1], 1 - c, k)
                    copy(passed, (fsend, frecv), (w * 3 + slot) * kp + k, sibling).wait_recv()
        for k, (cx, cy) in enumerate(_other_chips(x, y)):
            theirs = _shard_of(sm_out, True, 2 * cx + cy)
            pltpu.make_async_remote_copy(src_ref=theirs, dst_ref=theirs, send_sem=ssend.at[k], recv_sem=srecv.at[k],
                                         device_id=(cx, cy, c), device_id_type=MESH).wait_recv()
        for cp in sends:
            cp.wait_send()
        local.wait()

    any_spec = pl.BlockSpec(memory_space=pl.ANY)
    outs = pl.pallas_call(
        body, name="gather_weights",
        out_shape=[jax.ShapeDtypeStruct(f.shape, BF16) for f in fulls]
        + [jax.ShapeDtypeStruct((small.shape[0], small.shape[1] * N_CHIPS), small.dtype)],
        in_specs=[any_spec] * (n + 1), out_specs=[any_spec] * (n + 1), input_output_aliases={w: w for w in range(n)},
        scratch_shapes=[pltpu.SemaphoreType.DMA((n * 3 * kp,)), pltpu.SemaphoreType.DMA((n * 3 * kp,)),
                        pltpu.SemaphoreType.DMA((n * 3 * kp,)), pltpu.SemaphoreType.DMA((n * 3 * kp,)), pltpu.SemaphoreType.DMA,
                        pltpu.SemaphoreType.DMA((3,)), pltpu.SemaphoreType.DMA((3,))],
    )(*fulls, small)
    return outs[:n], outs[n]


HBM_SPEC = pl.BlockSpec(memory_space=pltpu.HBM)
SEM_SPEC = pl.BlockSpec(memory_space=pltpu.SEMAPHORE)
DATAFLOW = pltpu.SideEffectType.DATAFLOW_SIDE_EFFECTING


def _start_copies(name, arrays, sem_shape, after, copies):
    n = len(arrays)

    def body(*refs):
        for cp in copies(refs[:n], refs[n + 1], refs[n + 2]):
            cp.start()
        token = refs[2 * n + 3]
        token[...] = jnp.zeros_like(token)

    res = pl.pallas_call(
        body, name=name,
        out_shape=(pltpu.SemaphoreType.DMA(sem_shape), pltpu.SemaphoreType.DMA(sem_shape),
                   *[pltpu.HBM(a.shape, a.dtype) for a in arrays], jax.ShapeDtypeStruct((8, 128), F32)),
        in_specs=[HBM_SPEC] * n + [pl.BlockSpec(memory_space=pl.ANY)],
        out_specs=(SEM_SPEC, SEM_SPEC, *([HBM_SPEC] * n), pl.BlockSpec(memory_space=pltpu.VMEM)),
        input_output_aliases={i: 2 + i for i in range(n)},
        compiler_params=pltpu.CompilerParams(has_side_effects=DATAFLOW),
    )(*[pltpu.with_memory_space_constraint(a, pltpu.HBM) for a in arrays], after)
    return res[0], res[1], list(res[2:2 + n]), res[2 + n]


def _wait_copies(name, arrays, send, recv, after, copies):
    n = len(arrays)

    def body(*refs):
        for cp in copies(refs[:n], refs[n], refs[n + 1]):
            cp.wait_send()
            cp.wait_recv()

    return pl.pallas_call(
        body, name=name, out_shape=[pltpu.HBM(a.shape, a.dtype) for a in arrays],
        in_specs=[HBM_SPEC] * n + [SEM_SPEC, SEM_SPEC, pl.BlockSpec(memory_space=pl.ANY)], out_specs=[HBM_SPEC] * n,
        input_output_aliases={i: i for i in range(n)},
        compiler_params=pltpu.CompilerParams(has_side_effects=DATAFLOW),
    )(*arrays, send, recv, after)


def _direct_gather_copies(col_sharded):
    def copies(refs, send, recv):
        x, y, c = _place()
        mine = 2 * x + y
        out = []
        for w, ref in enumerate(refs):
            own_half = _shard_of(ref, col_sharded[w], mine, c)
            k = 0
            for cx, cy in _other_chips(x, y):
                for pc in (c, 1 - c):
                    out.append(pltpu.make_async_remote_copy(
                        src_ref=own_half, dst_ref=own_half, send_sem=send.at[6 * w + k], recv_sem=recv.at[6 * w + k],
                        device_id=(cx, cy, pc), device_id_type=MESH))
                    k += 1
        return out
    return copies


def _chip_scatter_copies(n, col_sharded):
    def piece(ref, cs, chip):
        if cs:
            w = ref.shape[2] // N_CHIPS
            return ref.at[:, :, pl.ds(pl.multiple_of(chip * w, SLAB), w)]
        return ref.at[pl.ds(chip, 1)]

    def copies(refs, send, recv):
        x, y, c = _place()
        out = []
        for k, (cx, cy) in enumerate(_other_chips(x, y)):
            for w in range(n):
                out.append(pltpu.make_async_remote_copy(
                    src_ref=piece(refs[w], col_sharded[w], 2 * cx + cy), dst_ref=refs[n + w].at[pl.ds(k, 1)],
                    send_sem=send.at[3 * w + k], recv_sem=recv.at[3 * w + k], device_id=(cx, cy, c), device_id_type=MESH))
        return out
    return copies


def _shard_views(gs, col_sharded):
    return [g.reshape(1, *g.shape) if cs else g.reshape(N_CHIPS, g.shape[0] // N_CHIPS, g.shape[1])
            for g, cs in zip(gs, col_sharded)]


DMA_CHUNK_BYTES = 1 << 20


def _chunk_rows(shape, itemsize):
    s, rows, cols = shape
    n = 1
    while s * (rows // n) * cols * itemsize > DMA_CHUNK_BYTES and (rows // n) % 32 == 0:
        n *= 2
    return rows // n


def _rows_of(ref, row0, rows, cols):
    if cols is None:
        return ref.at[:, pl.ds(row0, rows)]
    return ref.at[:, pl.ds(row0, rows), pl.ds(cols[0], cols[1])]


def _row_pieces(src, src_row0, dst, dst_row0, rows, send_sem, recv_sem, device, cols=None):
    width = src.shape[2] if cols is None else cols[1]
    step = _chunk_rows((src.shape[0], rows, width), src.dtype.itemsize)
    return [pltpu.make_async_remote_copy(src_ref=_rows_of(src, src_row0 + r, step, cols), dst_ref=_rows_of(dst, dst_row0 + r, step, cols),
                                         send_sem=send_sem, recv_sem=recv_sem, device_id=device, device_id_type=MESH)
            for r in range(0, rows, step)]


def _pair_exchange_copies(n, whole, cols=None):
    def copies(refs, send, recv):
        x, y, c = _place()
        sibling = (x, y, 1 - c)
        out = []
        for w in range(n):
            hr = refs[n + w].shape[1]
            if whole:
                out.append(pltpu.make_async_remote_copy(
                    src_ref=_rows_of(refs[w], (1 - c) * hr, hr, cols), dst_ref=_rows_of(refs[n + w], 0, hr, cols),
                    send_sem=send.at[w], recv_sem=recv.at[w], device_id=sibling, device_id_type=MESH))
            else:
                out += _row_pieces(refs[w], (1 - c) * hr, refs[n + w], 0, hr, send.at[w], recv.at[w], sibling, cols)
        return out
    return copies


def _pair_join_copies(n, whole, sem0=0):
    def copies(refs, send, recv):
        x, y, c = _place()
        sibling = (x, y, 1 - c)
        out = []
        for w in range(n):
            hr = refs[w].shape[1] // 2
            sems = dict(send_sem=send.at[sem0 + w], recv_sem=recv.at[sem0 + w])
            if whole:
                out.append(pltpu.make_async_remote_copy(
                    src_ref=refs[w].at[:, pl.ds(c * hr, hr)], dst_ref=refs[w].at[:, pl.ds((1 - c) * hr, hr)],
                    device_id=sibling, device_id_type=MESH, **sems))
            else:
                out += _row_pieces(refs[w], c * hr, refs[w], c * hr, hr, sems["send_sem"], sems["recv_sem"], sibling)
        return out
    return copies


def _both_copies(first, n_first, second):
    def copies(refs, send, recv):
        return first(refs[:n_first], send, recv) + second(refs[n_first:], send, recv)
    return copies


def _small_gather_copies(refs, send, recv):
    vec, land = refs
    x, y, c = _place()
    me = 4 * x + 2 * y + c
    return [pltpu.make_async_remote_copy(src_ref=vec, dst_ref=land.at[me], send_sem=send.at[k], recv_sem=recv.at[k],
                                         device_id=peer, device_id_type=MESH) for k, (peer, _) in enumerate(_all_devices(x, y, c))]


def _small_sum(vec, land, me, n_sum, d):
    rows = vec.shape[0]

    def body(me_ref, v_ref, l_ref, sum_ref, kept_ref):
        def slot(k):
            return jnp.where(me_ref[0] == k, v_ref[...], l_ref[k])

        total = slot(0)[0:n_sum, :]
        kept_ref[0] = slot(0)[n_sum:rows, :]
        for k in range(1, N_DEV):
            total = total + slot(k)[0:n_sum, :]
            kept_ref[k] = slot(k)[n_sum:rows, :]
        sum_ref[...] = total
        loss = 0.5 / d * jnp.sum(total[0:8, :])
        sum_ref[0:8, :] = jnp.full((8, 128), loss, F32)

    vm = pl.BlockSpec(memory_space=pltpu.VMEM)
    return pl.pallas_call(
        body, name="small_sum", in_specs=[pl.BlockSpec(memory_space=pltpu.SMEM), vm, vm], out_specs=[vm, vm],
        out_shape=[jax.ShapeDtypeStruct((n_sum, 128), F32), jax.ShapeDtypeStruct((N_DEV, rows - n_sum, 128), F32)],
        compiler_params=pltpu.CompilerParams(vmem_limit_bytes=VMEM_LIMIT_V7X))(me, vec, land)


def _pair_exchange_into(view, land, cols, name):
    def body(v_ref, l_in, l_ref, send, recv):
        del l_in
        x, y, c = _place()
        sibling = (x, y, 1 - c)
        hr = l_ref.shape[1]
        for cp in _row_pieces(v_ref, (1 - c) * hr, l_ref, 0, hr, send, recv, sibling, cols):
            cp.start()
        pltpu.make_async_remote_copy(src_ref=_rows_of(v_ref, (1 - c) * hr, hr, cols), dst_ref=_rows_of(l_ref, 0, hr, cols),
                                     send_sem=send, recv_sem=recv, device_id=sibling, device_id_type=MESH).wait()

    any_spec = pl.BlockSpec(memory_space=pl.ANY)
    return pl.pallas_call(
        body, name=name, out_shape=jax.ShapeDtypeStruct(land.shape, land.dtype),
        in_specs=[any_spec, any_spec], out_specs=any_spec, input_output_aliases={1: 0},
        scratch_shapes=[pltpu.SemaphoreType.DMA, pltpu.SemaphoreType.DMA],
    )(view, land)


def _pair_sum(view, got, core, name):
    s, r, cols = view.shape
    hr = r // 2
    tr = _row_tile(hr, cols)
    nb = hr // tr

    def body(core_ref, a_ref, b_ref, o_ref):
        del core_ref
        o_ref[...] = (a_ref[...].astype(F32) + b_ref[...].astype(F32)).astype(BF16)

    same = pl.BlockSpec((None, tr, cols), lambda j, i, core_ref: (j, i, 0))
    return pl.pallas_call(
        body, name=name,
        grid_spec=pltpu.PrefetchScalarGridSpec(
            num_scalar_prefetch=1, grid=(s, nb),
            in_specs=[pl.BlockSpec((None, tr, cols), lambda j, i, core_ref: (j, core_ref[0] * nb + i, 0)), same],
            out_specs=same),
        out_shape=jax.ShapeDtypeStruct((s, hr, cols), BF16), compiler_params=_params("parallel", "parallel"))(core, view, got)


def _pair_sum_small(views, gots, core, name):
    n = len(views)

    def body(core_ref, *refs):
        del core_ref
        for w in range(n):
            refs[2 * n + w][...] = (refs[w][...].astype(F32) + refs[n + w][...].astype(F32)).astype(BF16)

    halves = [(v.shape[0], v.shape[1] // 2, v.shape[2]) for v in views]
    own = [pl.BlockSpec(h, lambda i, core_ref: (0, core_ref[0], 0)) for h in halves]
    whole = [pl.BlockSpec(h, lambda i, core_ref: (0, 0, 0)) for h in halves]
    return pl.pallas_call(
        body, name=name,
        grid_spec=pltpu.PrefetchScalarGridSpec(num_scalar_prefetch=1, grid=(1,), in_specs=own + whole, out_specs=whole),
        out_shape=[jax.ShapeDtypeStruct(h, BF16) for h in halves], compiler_params=_params("arbitrary"))(core, *views, *gots)


def _piece_cols(part, col_sharded):
    return part.shape[2] // N_CHIPS if col_sharded else part.shape[2]


def _chip_sum_small(parts, gots, col_sharded, place, name):
    n = len(parts)

    def body(place_ref, *refs):
        del place_ref
        for w in range(n):
            got = refs[n + w]
            acc = refs[w][...].astype(F32) + got[0].astype(F32)
            refs[2 * n + w][...] = acc + got[1].astype(F32) + got[2].astype(F32)

    own, others, outs, shapes = [], [], [], []
    for p, cs in zip(parts, col_sharded):
        hr, cols = p.shape[1], _piece_cols(p, cs)
        own.append(pl.BlockSpec((None, hr, cols), (lambda i, pr: (0, 0, pr[0])) if cs else (lambda i, pr: (pr[0], 0, 0))))
        others.append(pl.BlockSpec((3, hr, cols), lambda i, pr: (0, 0, 0)))
        outs.append(pl.BlockSpec((hr, cols), lambda i, pr: (pr[1], 0)))
        shapes.append(jax.ShapeDtypeStruct((2 * hr, cols), F32))
    return pl.pallas_call(
        body, name=name,
        grid_spec=pltpu.PrefetchScalarGridSpec(num_scalar_prefetch=1, grid=(1,), in_specs=own + others, out_specs=outs),
        out_shape=shapes, compiler_params=_params("arbitrary"))(place, *parts, *gots)


def _chip_sum(part, got, col_sharded, place, name):
    _, hr, _ = part.shape
    cols = _piece_cols(part, col_sharded)
    tr = _row_tile(hr, cols)
    nb = hr // tr

    def body(place_ref, own_ref, g0_ref, g1_ref, g2_ref, o_ref):
        del place_ref
        acc = own_ref[...].astype(F32) + g0_ref[...].astype(F32)
        o_ref[...] = acc + g1_ref[...].astype(F32) + g2_ref[...].astype(F32)

    if col_sharded:
        own = pl.BlockSpec((None, tr, cols), lambda i, pr: (0, i, pr[0]))
    else:
        own = pl.BlockSpec((None, tr, cols), lambda i, pr: (pr[0], i, 0))
    others = [pl.BlockSpec((None, tr, cols), lambda i, pr, k=k: (k, i, 0)) for k in range(3)]
    return pl.pallas_call(
        body, name=name,
        grid_spec=pltpu.PrefetchScalarGridSpec(
            num_scalar_prefetch=1, grid=(nb,), in_specs=[own] + others,
            out_specs=pl.BlockSpec((tr, cols), lambda i, pr: (pr[1] * nb + i, 0))),
        out_shape=jax.ShapeDtypeStruct((2 * hr, cols), F32), compiler_params=_params("parallel"))(place, part, got, got, got)


def _row_tile(rows, cols, itemsize=4, budget=2 << 20):
    t = rows
    while t * cols * itemsize > budget and t % 16 == 0:
        t //= 2
    return t


def _adamw(w, g, m, v, name, after, also_g=False):
    rows, cols = w.shape
    tr = _row_tile(rows, cols, budget=1 << 20)
    extra = [] if after is None else [after]
    n_out = 4 if also_g else 3

    def body(w_ref, g_ref, m_ref, v_ref, *rest):
        d_ref, nm_ref, nv_ref = rest[len(extra):len(extra) + 3]
        g_ = g_ref[...]
        if also_g:
            rest[len(extra) + 3][...] = g_
        nm = ADAM_B1 * m_ref[...] + (1.0 - ADAM_B1) * g_
        nv = ADAM_B2 * v_ref[...] + (1.0 - ADAM_B2) * (g_ * g_)
        m_hat = nm / (1.0 - ADAM_B1 ** ADAM_STEP)
        v_hat = nv / (1.0 - ADAM_B2 ** ADAM_STEP)
        d_ref[...] = -ADAM_LR * (m_hat / (jnp.sqrt(v_hat) + ADAM_EPS) + ADAM_WD * w_ref[...])
        nm_ref[...] = nm
        nv_ref[...] = nv

    spec = pl.BlockSpec((tr, cols), lambda i: (i, 0))
    shp = jax.ShapeDtypeStruct((rows, cols), F32)
    return pl.pallas_call(body, name=name, grid=(rows // tr,),
                          in_specs=[spec] * 4 + [pl.BlockSpec(memory_space=pl.ANY)] * len(extra), out_specs=[spec] * n_out,
                          out_shape=[shp] * n_out, compiler_params=_params("parallel"))(w, g, m, v, *extra)


def _all_devices(x, y, c):
    out = []
    for k in range(1, N_DEV):
        peer = (x ^ ((k >> 2) & 1), y ^ ((k >> 1) & 1), c ^ (k & 1))
        out.append((peer, 4 * peer[0] + 2 * peer[1] + peer[2]))
    return out


def _ada_exchange(c, w_shard, b_ada):
    bsz, d = c.shape
    cs = w_shard.shape[1]

    def body(c_ref, w_ref, b_ref, mod_ref, act_ref, c_all, part, pieces, csend, crecv, psend, precv):
        x, y, core = _place()
        me = 4 * x + 2 * y + core
        chip = 2 * x + y
        c_all[me] = c_ref[...]
        peers = _all_devices(x, y, core)
        copies = []
        for k, (peer, _) in enumerate(peers):
            cp = pltpu.make_async_remote_copy(src_ref=c_ref, dst_ref=c_all.at[me], send_sem=csend.at[k], recv_sem=crecv.at[k],
                                              device_id=peer, device_id_type=MESH)
            cp.start()
            copies.append(cp)
        for k, (_, src) in enumerate(peers):
            pltpu.make_async_remote_copy(src_ref=c_ref, dst_ref=c_all.at[src], send_sem=csend.at[k], recv_sem=crecv.at[k],
                                         device_id=(x, y, core), device_id_type=MESH).wait_recv()
        rows = jnp.concatenate([c_all[i] for i in range(N_DEV)], axis=0)
        act = rows * _sigmoid(rows)
        act_ref[...] = act
        prod = jnp.dot(act.astype(BF16), w_ref[...].astype(BF16), preferred_element_type=F32)
        for i in range(N_DEV):
            part[i] = prod[i * bsz:(i + 1) * bsz, :]
        pieces[chip] = part[me]
        chips = _other_chips(x, y)
        for k, (cx, cy) in enumerate(chips):
            cp = pltpu.make_async_remote_copy(src_ref=part.at[4 * cx + 2 * cy + core], dst_ref=pieces.at[chip],
                                              send_sem=psend.at[k], recv_sem=precv.at[k], device_id=(cx, cy, core),
                                              device_id_type=MESH)
            cp.start()
            copies.append(cp)
        for k, (cx, cy) in enumerate(chips):
            pltpu.make_async_remote_copy(src_ref=part.at[me], dst_ref=pieces.at[2 * cx + cy], send_sem=psend.at[k],
                                         recv_sem=precv.at[k], device_id=(cx, cy, core), device_id_type=MESH).wait_recv()
        for cp in copies:
            cp.wait_send()
        mod_ref[...] = jnp.concatenate([pieces[j] for j in range(N_CHIPS)], axis=1) + b_ref[...]

    vm = pl.BlockSpec(memory_space=pltpu.VMEM)
    return pl.pallas_call(
        body, name="ada_exchange", in_specs=[vm] * 3, out_specs=[vm] * 2,
        out_shape=[jax.ShapeDtypeStruct((bsz, 3 * d), F32), jax.ShapeDtypeStruct((N_DEV * bsz, d), F32)],
        scratch_shapes=[pltpu.VMEM((N_DEV, bsz, d), F32), pltpu.VMEM((N_DEV, bsz, cs), F32), pltpu.VMEM((N_CHIPS, bsz, cs), F32),
                        pltpu.SemaphoreType.DMA((N_DEV - 1,)), pltpu.SemaphoreType.DMA((N_DEV - 1,)),
                        pltpu.SemaphoreType.DMA((3,)), pltpu.SemaphoreType.DMA((3,))],
        compiler_params=pltpu.CompilerParams(vmem_limit_bytes=VMEM_LIMIT_V7X))(c, w_shard, b_ada)


def _grad_w_ada_cols(act_t, d_cols):
    d, n = act_t.shape
    cs = d_cols.shape[1]

    def body(a_ref, g_ref, o_ref):
        a, g = a_ref[...], g_ref[...]
        acc = a[:, 0:1] * g[0:1, :]
        for b in range(1, n):
            acc = acc + a[:, b:b + 1] * g[b:b + 1, :]
        o_ref[...] = acc

    vm = pl.BlockSpec(memory_space=pltpu.VMEM)
    return pl.pallas_call(body, name="grad_w_ada", in_specs=[vm] * 2, out_specs=vm,
                          out_shape=jax.ShapeDtypeStruct((d, cs), F32),
                          compiler_params=pltpu.CompilerParams(vmem_limit_bytes=VMEM_LIMIT_V7X))(act_t, d_cols)


def _permute_w_in(w_nat, lay):
    d = lay.d
    group = 4

    def call(name, width, n_pieces, nat_piece, out_block0, prev):
        def body(*refs):
            refs[-1][...] = jnp.concatenate([r[...] for r in refs[:group]], axis=1)

        in_specs = [pl.BlockSpec((d, width), lambda s, m=m: (0, nat_piece(group * s + m))) for m in range(group)]
        args = [w_nat] * group
        aliases = {}
        if prev is not None:
            in_specs.append(pl.BlockSpec(memory_space=pl.ANY))
            args.append(prev)
            aliases = {group: 0}
        return pl.pallas_call(
            body, name=name, grid=(n_pieces // group,), in_specs=in_specs,
            out_specs=pl.BlockSpec((d, group * width), lambda s: (0, out_block0 + s)),
            out_shape=jax.ShapeDtypeStruct((d, lay.np), BF16), input_output_aliases=aliases,
            compiler_params=_params("arbitrary"))(*args)

    w_all = call("permute_w_attn", SLAB, 2 * PAIR_SLABS, lay.attn_nat_slab, 0, None)
    n_rest = 6 * d // CONV_TILE
    if n_rest % group:
        group = 2
    return call("permute_w_rest", CONV_TILE, n_rest, lay.rest_nat_tile, lay.c0 // (group * CONV_TILE), w_all)


def _project(x2, mod3, w_all, b_all, seq, col0, ncols, tn, out_dtype, want_ht, name):
    t, d = x2.shape
    tm = min(2048, seq)
    per_seq = seq // tm
    j0 = col0 // tn

    def body(x_ref, mod_ref, w_ref, b_ref, o_ref, *rest):
        h_ref = rest[-1]

        @pl.when(pl.program_id(1) == 0)
        def _():
            h = x_ref[...] * (1.0 + mod_ref[:, d:2 * d]) + mod_ref[:, 0:d]
            h_ref[...] = h.astype(BF16)
            if want_ht:
                rest[0][...] = h.T.astype(BF16)

        o_ref[...] = (jnp.dot(h_ref[...], w_ref[...], preferred_element_type=F32) + b_ref[...]).astype(out_dtype)

    out_shape = [jax.ShapeDtypeStruct((t, ncols), out_dtype)]
    out_specs = [pl.BlockSpec((tm, tn), lambda i, j: (i, j))]
    if want_ht:
        out_shape.append(jax.ShapeDtypeStruct((d, t), BF16))
        out_specs.append(pl.BlockSpec((d, tm), lambda i, j: (0, i)))
    return pl.pallas_call(
        body, name=name, grid=(t // tm, ncols // tn),
        in_specs=[pl.BlockSpec((tm, d), lambda i, j: (i, 0)),
                  pl.BlockSpec((None, 1, 3 * d), lambda i, j: (i // per_seq, 0, 0)),
                  pl.BlockSpec((d, tn), lambda i, j: (0, j0 + j)),
                  pl.BlockSpec((1, tn), lambda i, j: (0, j0 + j))],
        out_specs=out_specs, out_shape=out_shape,
        scratch_shapes=[pltpu.VMEM((tm, d), BF16)],
        compiler_params=_params("arbitrary", "arbitrary"))(x2, mod3, w_all, b_all)


def _slope(g, p, hh):
    head = 4 * g + 2 * p + hh
    return 2.0 ** (-ALIBI_MAX_EXP * (head + 1.0) / N_HEADS)


def _ld_rows(ref, start, n, stride):
    if stride == 1:
        return ref[pl.ds(start, n), :]
    return ref[pl.ds(start, n, stride=stride), :]


def _st_rows(ref, start, n, stride, val):
    if stride == 1:
        ref[pl.ds(start, n), :] = val
    else:
        ref[pl.ds(start, n, stride=stride), :] = val


def _sub_blocks(g, seq):
    return seq // DILATIONS[g] // SUB


def _key_rows(g, seq):
    return SUB if _sub_blocks(g, seq) == 1 else 2 * SUB


def _fill_bias(bias_ref, p, seq):
    for g in range(N_GROUPS):
        nk = _key_rows(g, seq)
        diff = lax.broadcasted_iota(jnp.int32, (SUB, nk), 0) - lax.broadcasted_iota(jnp.int32, (SUB, nk), 1)
        for i, off in enumerate((0, SUB)):
            if i == 1 and nk == SUB:
                continue
            delta = diff + off
            ok = (delta >= 0) & (delta <= SUB)
            dist = (delta * DILATIONS[g]).astype(F32)
            for hh in range(2):
                slope = jnp.where(p == 0, _slope(g, 0, hh), _slope(g, 1, hh))
                bias_ref[g, i, hh, :, 0:nk] = jnp.where(ok, -slope * dist, NEG)


def _to_sub_major(pa_ref, col, sub_ref, stage, dil, seq):
    cols = slice(col * SLAB, (col + 1) * SLAB)
    if dil == 1:
        sub_ref[...] = pa_ref[:, cols]
        return
    n = seq // dil
    stage[...] = pa_ref[:, cols].astype(F32)
    for r in range(dil):
        sub_ref[pl.ds(r * n, n), :] = stage[pl.ds(r, n, stride=dil), :].astype(BF16)


def _block_rows(it, g, seq):
    dil, nb = DILATIONS[g], _sub_blocks(g, seq)
    row0 = pl.multiple_of(it * SUB, SUB)
    if nb == 1:
        return row0, row0, 0, it
    blk = it % nb
    first = blk == 0
    krow0 = pl.multiple_of(row0 - jnp.where(first, 0, SUB), SUB)
    nat = row0 if dil == 1 else it // nb + dil * SUB * blk
    return row0, krow0, jnp.where(first, 0, 1), nat


def _nt(a, b):
    return lax.dot_general(a, b, (((1,), (1,)), ((), ())), preferred_element_type=F32)


def _tn(a, b):
    return lax.dot_general(a, b, (((0,), (0,)), ((), ())), preferred_element_type=F32)


def _head_sums(t):
    rows = t.shape[0]
    lo = jnp.broadcast_to(jnp.sum(t[:, :HEAD_DIM], axis=-1, keepdims=True), (rows, HEAD_DIM))
    hi = jnp.broadcast_to(jnp.sum(t[:, HEAD_DIM:], axis=-1, keepdims=True), (rows, HEAD_DIM))
    return jnp.concatenate([lo, hi], axis=-1)


def _attn_fwd(pa, bsz, seq):
    t = pa.shape[0]
    n_blocks = seq // SUB
    chunk = 256

    def body(pa_ref, o_ref, lse_ref, a_ref, sub, stage, bias_ref, s_buf, p_buf, l_buf):
        p = pl.program_id(1)
        _fill_bias(bias_ref, p, seq)
        head0 = lax.broadcasted_iota(jnp.int32, (SUB, SLAB), 1) < HEAD_DIM
        for g in range(N_GROUPS):
            dil = DILATIONS[g]
            for w in range(3):
                _to_sub_major(pa_ref, 3 * w + g, sub.at[w], stage, dil, seq)
            nk = _key_rows(g, seq)

            def trip(i, carry, g=g, dil=dil, nk=nk):
                places = [_block_rows(BLOCKS_PER_TRIP * i + j, g, seq) for j in range(BLOCKS_PER_TRIP)]
                for j, (row0, krow0, _, _) in enumerate(places):
                    q = sub[0, pl.ds(row0, SUB), :]
                    zero = jnp.zeros_like(q)
                    q2 = jnp.concatenate([jnp.where(head0, q, zero), jnp.where(head0, zero, q)], axis=0) * (HEAD_DIM ** -0.5)
                    s_buf[j, :, 0:nk] = _nt(q2, sub[1, pl.ds(krow0, nk), :])
                for j, (_, _, bi, _) in enumerate(places):
                    for c in range(0, 2 * SUB, SOFTMAX_ROWS):
                        hh, r = divmod(c, SUB)
                        s = s_buf[j, c:c + SOFTMAX_ROWS, 0:nk] + bias_ref[g, bi, hh, r:r + SOFTMAX_ROWS, 0:nk]
                        m = jnp.max(s, axis=-1, keepdims=True)
                        e = jnp.exp(s - m)
                        den = jnp.sum(e, axis=-1, keepdims=True)
                        p_buf[j, c:c + SOFTMAX_ROWS, 0:nk] = (e * (1.0 / den)).astype(BF16)
                        l_buf[j, c:c + SOFTMAX_ROWS, :] = jnp.broadcast_to(m + jnp.log(den), (SOFTMAX_ROWS, SLAB))
                for j, (_, krow0, _, nat) in enumerate(places):
                    o2 = jnp.dot(p_buf[j, :, 0:nk], sub[2, pl.ds(krow0, nk), :], preferred_element_type=F32)
                    _st_rows(o_ref.at[g], nat, SUB, dil, jnp.where(head0, o2[0:SUB], o2[SUB:2 * SUB]))
                    _st_rows(lse_ref.at[g], nat, SUB, dil, jnp.where(head0, l_buf[j, 0:SUB, :], l_buf[j, SUB:2 * SUB, :]))
                return carry

            lax.fori_loop(0, n_blocks // BLOCKS_PER_TRIP, trip, 0)

        def mix(i, carry):
            rows = pl.ds(pl.multiple_of(i * chunk, chunk), chunk)
            l0, l1, l2 = lse_ref[0, rows, :], lse_ref[1, rows, :], lse_ref[2, rows, :]
            m = jnp.maximum(jnp.maximum(l0, l1), l2)
            e0, e1, e2 = jnp.exp(l0 - m), jnp.exp(l1 - m), jnp.exp(l2 - m)
            tot = e0 + e1 + e2
            o = (e0 / tot) * o_ref[0, rows, :] + (e1 / tot) * o_ref[1, rows, :] + (e2 / tot) * o_ref[2, rows, :]
            z = pa_ref[rows, 9 * SLAB:10 * SLAB].astype(F32)
            a_ref[rows, :] = (o * (z * _sigmoid(z))).astype(BF16)
            return carry

        lax.fori_loop(0, seq // chunk, mix, 0)

    big = jax.ShapeDtypeStruct((N_GROUPS, t, 2 * SLAB), F32)
    return pl.pallas_call(
        body, name="attn_fwd", grid=(bsz, 2),
        in_specs=[pl.BlockSpec((seq, PAIR_COLS), lambda b, p: (b, p))],
        out_specs=[pl.BlockSpec((N_GROUPS, seq, SLAB), lambda b, p: (0, b, p)),
                   pl.BlockSpec((N_GROUPS, seq, SLAB), lambda b, p: (0, b, p)),
                   pl.BlockSpec((seq, SLAB), lambda b, p: (b, p))],
        out_shape=[big, big, jax.ShapeDtypeStruct((t, 2 * SLAB), BF16)],
        scratch_shapes=[pltpu.VMEM((3, seq, SLAB), BF16), pltpu.VMEM((seq, SLAB), F32),
                        pltpu.VMEM((N_GROUPS, 2, 2, SUB, 2 * SUB), F32), pltpu.VMEM((BLOCKS_PER_TRIP, 2 * SUB, 2 * SUB), F32),
                        pltpu.VMEM((BLOCKS_PER_TRIP, 2 * SUB, 2 * SUB), BF16), pltpu.VMEM((BLOCKS_PER_TRIP, 2 * SUB, SLAB), F32)],
        compiler_params=_params("arbitrary", "arbitrary"))(pa)


def _shift_down(v, k, rows):
    return jnp.where(rows >= k, pltpu.roll(v, k, 0), 0.0)


def _shift_up(v, k, rows):
    n = v.shape[0]
    return jnp.where(rows < n - k, pltpu.roll(v, n - k, 0), 0.0)


def _conv_fwd(pr, conv_w, bsz, seq, d):
    t = pr.shape[0]
    ct = CONV_TILE

    def body(p_ref, cw_ref, o_ref):
        u = p_ref[:, 2 * ct:3 * ct].astype(F32) * p_ref[:, 0:ct].astype(F32)
        cw = cw_ref[...]
        rows = lax.broadcasted_iota(jnp.int32, u.shape, 0)
        conv = cw[0:1, :] * _shift_down(u, 2, rows)
        conv = conv + cw[1:2, :] * _shift_down(u, 1, rows)
        conv = conv + cw[2:3, :] * u
        z = p_ref[:, 3 * ct:4 * ct].astype(F32)
        o_ref[...] = (p_ref[:, ct:2 * ct].astype(F32) * conv * (z * _sigmoid(z))).astype(BF16)

    return pl.pallas_call(
        body, name="conv_fwd", grid=(bsz, d // ct),
        in_specs=[pl.BlockSpec((seq, 4 * ct), lambda b, j: (b, j)), pl.BlockSpec((3, ct), lambda b, j: (0, j))],
        out_specs=pl.BlockSpec((seq, ct), lambda b, j: (b, j)),
        out_shape=jax.ShapeDtypeStruct((t, d), BF16), compiler_params=_params("parallel", "parallel"))(pr, conv_w)


def _tail(a_in, b_in, pr, x2, target2, mod3, w_pa, w_pc, w_out, b_out, ln_g, ln_b, seq, lay):
    t, d = x2.shape
    tm = 512
    per_seq = seq // tm
    n_steps = t // tm
    gate_blk = 4 * d // d

    def nt(a, b):
        return lax.dot_general(a, b, (((1,), (1,)), ((), ())), preferred_element_type=F32)

    def tn(a, b):
        return lax.dot_general(a, b, (((0,), (0,)), ((), ())), preferred_element_type=F32)

    def body(a_ref, b_ref, ga_ref, gb_ref, x_ref, tg_ref, mod_ref, wpa_ref, wpc_ref, wo_ref, bo_ref, lg_ref, lb_ref,
             dpg_ref, da_ref, db_ref, gx_ref, dgate_ref, small_ref, gwpa_hbm, gwpc_hbm, gwo_hbm,
             acc_pa, acc_pc, acc_o, sem):
        i = pl.program_id(0)

        @pl.when(i == 0)
        def _():
            acc_pa[...] = jnp.zeros_like(acc_pa)
            acc_pc[...] = jnp.zeros_like(acc_pc)
            acc_o[...] = jnp.zeros_like(acc_o)
            small_ref[...] = jnp.zeros_like(small_ref)

        @pl.when(i % per_seq == 0)
        def _():
            dgate_ref[...] = jnp.zeros_like(dgate_ref)

        halves = [slice(k * (tm // 2), (k + 1) * (tm // 2)) for k in range(2)]
        gate = mod_ref[:, 2 * d:3 * d]
        a_bf = [a_ref[rs, :] for rs in halves]
        b_bf = [b_ref[rs, :] for rs in halves]
        y_attn = [jnp.dot(a, wpa_ref[...], preferred_element_type=F32) for a in a_bf]
        y_conv = [jnp.dot(b, wpc_ref[...], preferred_element_type=F32) for b in b_bf]
        sa = [_sigmoid(ga_ref[rs, :].astype(F32)) for rs in halves]
        sb = [_sigmoid(gb_ref[rs, :].astype(F32)) for rs in halves]
        merged = [(sa[k] * y_attn[k] + sb[k] * y_conv[k]).astype(BF16) for k in range(2)]
        mo = [jnp.dot(m, wo_ref[...], preferred_element_type=F32) + bo_ref[...] for m in merged]
        d_mo_bf = []
        for k, rs in enumerate(halves):
            r = ALPHA * x_ref[rs, :] + gate * mo[k]
            mu = jnp.mean(r, axis=-1, keepdims=True)
            cen = r - mu
            var = jnp.mean(cen * cen, axis=-1, keepdims=True)
            rstd = lax.rsqrt(var + LN_EPS)
            xhat = cen * rstd
            err = xhat * lg_ref[...] + lb_ref[...] - tg_ref[rs, :]
            dy = err * (1.0 / d)
            dxhat = dy * lg_ref[...]
            dr = rstd * (dxhat - jnp.mean(dxhat, axis=-1, keepdims=True)
                         - xhat * jnp.mean(dxhat * xhat, axis=-1, keepdims=True))
            gx_ref[rs, :] = ALPHA * dr
            dgate_ref[...] += jnp.sum(dr * mo[k], axis=0, keepdims=True)
            d_mo = dr * gate
            small_ref[0:1, :] += jnp.sum(d_mo, axis=0, keepdims=True)
            small_ref[1:2, :] += jnp.sum(dy * xhat, axis=0, keepdims=True)
            small_ref[2:3, :] += jnp.sum(dy, axis=0, keepdims=True)
            small_ref[3:4, :] += jnp.sum(err * err, axis=0, keepdims=True)
            d_mo_bf.append(d_mo.astype(BF16))
        acc_o[...] += tn(jnp.concatenate(merged, axis=0), jnp.concatenate(d_mo_bf, axis=0))
        dmerged = [nt(g, wo_ref[...]) for g in d_mo_bf]
        dy_attn, dy_conv = [], []
        for k, rs in enumerate(halves):
            dy_attn.append((dmerged[k] * sa[k]).astype(BF16))
            dy_conv.append((dmerged[k] * sb[k]).astype(BF16))
            dpg_ref[rs, 0:d] = (dmerged[k] * y_attn[k] * sa[k] * (1.0 - sa[k])).astype(BF16)
            dpg_ref[rs, d:2 * d] = (dmerged[k] * y_conv[k] * sb[k] * (1.0 - sb[k])).astype(BF16)
        acc_pa[...] += tn(a_ref[...], jnp.concatenate(dy_attn, axis=0))
        acc_pc[...] += tn(b_ref[...], jnp.concatenate(dy_conv, axis=0))
        for k, rs in enumerate(halves):
            da_ref[rs, :] = nt(dy_attn[k], wpa_ref[...])
            db_ref[rs, :] = nt(dy_conv[k], wpc_ref[...])

        @pl.when(i == n_steps - 1)
        def _():
            copies = [pltpu.make_async_copy(acc_pa, gwpa_hbm, sem.at[0]), pltpu.make_async_copy(acc_pc, gwpc_hbm, sem.at[1]),
                      pltpu.make_async_copy(acc_o, gwo_hbm, sem.at[2])]
            for cp in copies:
                cp.start()
            for cp in copies:
                cp.wait()

    row = lambda w: pl.BlockSpec((tm, w), lambda i: (i, 0))
    const = lambda shp: pl.BlockSpec(shp, lambda i: (0,) * len(shp), pipeline_mode=pl.Buffered(1))
    any_spec = pl.BlockSpec(memory_space=pl.ANY)
    return pl.pallas_call(
        body, name="tail", grid=(n_steps,),
        in_specs=[row(Z_WIDTH), row(d),
                  pl.BlockSpec((tm, d), lambda i: (i, gate_blk)), pl.BlockSpec((tm, d), lambda i: (i, gate_blk + 1)),
                  row(d), row(d), pl.BlockSpec((None, 1, 3 * d), lambda i: (i // per_seq, 0, 0)),
                  const((Z_WIDTH, d)), const((d, d)), const((d, d)), const((1, d)), const((1, d)), const((1, d))],
        out_specs=[pl.BlockSpec((tm, 2 * d), lambda i: (i, lay.g0 // (2 * d))), row(Z_WIDTH), row(d), row(d),
                   pl.BlockSpec((None, 1, d), lambda i: (i // per_seq, 0, 0)), pl.BlockSpec((8, d), lambda i: (0, 0)),
                   HBM_SPEC, HBM_SPEC, HBM_SPEC],
        out_shape=[jax.ShapeDtypeStruct((t, lay.np), BF16), jax.ShapeDtypeStruct((t, Z_WIDTH), F32),
                   jax.ShapeDtypeStruct((t, d), F32), jax.ShapeDtypeStruct((t, d), F32),
                   jax.ShapeDtypeStruct((t // seq, 1, d), F32), jax.ShapeDtypeStruct((8, d), F32),
                   pltpu.HBM((Z_WIDTH, d), F32), pltpu.HBM((d, d), F32), pltpu.HBM((d, d), F32)],
        scratch_shapes=[pltpu.VMEM((Z_WIDTH, d), F32), pltpu.VMEM((d, d), F32), pltpu.VMEM((d, d), F32),
                        pltpu.SemaphoreType.DMA((3,))],
        compiler_params=_params("arbitrary"),
    )(a_in, b_in, pr, pr, x2, target2, mod3, w_pa, w_pc, w_out, b_out, ln_g, ln_b)


def _conv_bwd(dproj, db, pr, conv_w, bsz, seq, lay):
    d = lay.d
    ct = CONV_TILE
    base = lay.c0 // (4 * ct)

    def body(dp_in, db_ref, p_ref, cw_ref, dp_ref, gcw_ref):
        del dp_in
        u_x, g_b, g_c, z = [p_ref[:, k * ct:(k + 1) * ct].astype(F32) for k in range(4)]
        cw = cw_ref[...]
        u = g_c * u_x
        rows = lax.broadcasted_iota(jnp.int32, u.shape, 0)
        u1, u2 = _shift_down(u, 1, rows), _shift_down(u, 2, rows)
        conv = cw[0:1, :] * u2 + cw[1:2, :] * u1 + cw[2:3, :] * u
        sig = _sigmoid(z)
        sl = z * sig
        dbv = db_ref[...]
        gbc = g_b * conv
        dp_ref[:, ct:2 * ct] = (dbv * sl * conv).astype(BF16)
        dp_ref[:, 3 * ct:4 * ct] = (dbv * gbc * (sig * (1.0 + z * (1.0 - sig)))).astype(BF16)
        dconv = dbv * sl * g_b

        @pl.when(pl.program_id(1) == 0)
        def _():
            gcw_ref[...] = jnp.zeros_like(gcw_ref)

        gcw_ref[0:1, :] += jnp.sum(dconv * u2, axis=0, keepdims=True)
        gcw_ref[1:2, :] += jnp.sum(dconv * u1, axis=0, keepdims=True)
        gcw_ref[2:3, :] += jnp.sum(dconv * u, axis=0, keepdims=True)
        du = cw[2:3, :] * dconv + cw[1:2, :] * _shift_up(dconv, 1, rows) + cw[0:1, :] * _shift_up(dconv, 2, rows)
        dp_ref[:, 0:ct] = (du * g_c).astype(BF16)
        dp_ref[:, 2 * ct:3 * ct] = (du * u_x).astype(BF16)

    return pl.pallas_call(
        body, name="conv_bwd", grid=(d // ct, bsz),
        in_specs=[pl.BlockSpec(memory_space=pl.ANY), pl.BlockSpec((seq, ct), lambda j, b: (b, j)),
                  pl.BlockSpec((seq, 4 * ct), lambda j, b: (b, j)), pl.BlockSpec((3, ct), lambda j, b: (0, j))],
        out_specs=[pl.BlockSpec((seq, 4 * ct), lambda j, b: (b, base + j)), pl.BlockSpec((8, ct), lambda j, b: (0, j))],
        out_shape=[jax.ShapeDtypeStruct(dproj.shape, BF16), jax.ShapeDtypeStruct((8, d), F32)],
        input_output_aliases={0: 0}, compiler_params=_params("arbitrary", "arbitrary"))(dproj, db, pr, conv_w)


def _attn_bwd(dproj, pa, o_all, lse_all, da, bsz, seq, after):
    n_blocks = seq // SUB
    chunk = 256

    def body(dp_in, pa_ref, o_ref, lse_ref, da_ref, after_ref, dp_ref, sub, stage, dsub, dog, cvec, bias_ref,
             s_buf, dp_buf, ds_buf, pb_buf, q2_buf, do2_buf, l_buf, c_buf):
        del dp_in, after_ref
        p = pl.program_id(1)
        _fill_bias(bias_ref, p, seq)
        head0 = lax.broadcasted_iota(jnp.int32, (SUB, SLAB), 1) < HEAD_DIM

        def mix_bwd(i, carry):
            rows = pl.ds(pl.multiple_of(i * chunk, chunk), chunk)
            ls = [lse_ref[g, rows, :] for g in range(N_GROUPS)]
            os_ = [o_ref[g, rows, :] for g in range(N_GROUPS)]
            m = jnp.maximum(jnp.maximum(ls[0], ls[1]), ls[2])
            es = [jnp.exp(l - m) for l in ls]
            tot = es[0] + es[1] + es[2]
            ws = [e / tot for e in es]
            o = ws[0] * os_[0] + ws[1] * os_[1] + ws[2] * os_[2]
            z = pa_ref[rows, 9 * SLAB:10 * SLAB].astype(F32)
            sig = _sigmoid(z)
            dav = da_ref[rows, :]
            do = dav * (z * sig)
            dp_ref[rows, 9 * SLAB:10 * SLAB] = (dav * o * (sig * (1.0 + z * (1.0 - sig)))).astype(BF16)
            wsum = _head_sums(do * o)
            for g in range(N_GROUPS):
                dog[g, rows, :] = ws[g] * do
                cvec[g, rows, :] = -(ws[g] * wsum)
            return carry

        lax.fori_loop(0, seq // chunk, mix_bwd, 0)

        for g in range(N_GROUPS):
            dil = DILATIONS[g]
            for w in range(3):
                _to_sub_major(pa_ref, 3 * w + g, sub.at[w], stage, dil, seq)
            dsub[1] = jnp.zeros((seq, SLAB), F32)
            dsub[2] = jnp.zeros((seq, SLAB), F32)
            nk = _key_rows(g, seq)

            def trip(i, carry, g=g, dil=dil, nk=nk):
                places = [_block_rows(BLOCKS_PER_TRIP * i + j, g, seq) for j in range(BLOCKS_PER_TRIP)]
                for j, (row0, krow0, _, nat) in enumerate(places):
                    q = sub[0, pl.ds(row0, SUB), :]
                    do = _ld_rows(dog.at[g], nat, SUB, dil).astype(BF16)
                    zero = jnp.zeros_like(q)
                    q2 = jnp.concatenate([jnp.where(head0, q, zero), jnp.where(head0, zero, q)], axis=0)
                    do2 = jnp.concatenate([jnp.where(head0, do, zero), jnp.where(head0, zero, do)], axis=0)
                    q2_buf[j] = q2
                    do2_buf[j] = do2
                    s_buf[j, :, 0:nk] = _nt(q2 * (HEAD_DIM ** -0.5), sub[1, pl.ds(krow0, nk), :])
                    dp_buf[j, :, 0:nk] = _nt(do2, sub[2, pl.ds(krow0, nk), :])
                    l_buf[j] = _ld_rows(lse_ref.at[g], nat, SUB, dil)
                    c_buf[j] = _ld_rows(cvec.at[g], nat, SUB, dil)
                for j, (_, _, bi, _) in enumerate(places):
                    for c in range(0, 2 * SUB, SOFTMAX_ROWS):
                        hh, r = divmod(c, SUB)
                        lane = hh * HEAD_DIM
                        s = s_buf[j, c:c + SOFTMAX_ROWS, 0:nk] + bias_ref[g, bi, hh, r:r + SOFTMAX_ROWS, 0:nk]
                        prob = jnp.exp(s - l_buf[j, r:r + SOFTMAX_ROWS, lane:lane + 1])
                        dprob = dp_buf[j, c:c + SOFTMAX_ROWS, 0:nk] + c_buf[j, r:r + SOFTMAX_ROWS, lane:lane + 1]
                        ds_buf[j, c:c + SOFTMAX_ROWS, 0:nk] = (prob * dprob * (HEAD_DIM ** -0.5)).astype(BF16)
                        pb_buf[j, c:c + SOFTMAX_ROWS, 0:nk] = prob.astype(BF16)
                for j, (row0, krow0, _, _) in enumerate(places):
                    ds = ds_buf[j, :, 0:nk]
                    dq2 = jnp.dot(ds, sub[1, pl.ds(krow0, nk), :], preferred_element_type=F32)
                    dsub[0, pl.ds(row0, SUB), :] = jnp.where(head0, dq2[0:SUB], dq2[SUB:2 * SUB])
                    dsub[1, pl.ds(krow0, nk), :] += _tn(ds, q2_buf[j])
                    dsub[2, pl.ds(krow0, nk), :] += _tn(pb_buf[j, :, 0:nk], do2_buf[j])
                return carry

            lax.fori_loop(0, n_blocks // BLOCKS_PER_TRIP, trip, 0)
            for w in range(3):
                cols = slice((3 * w + g) * SLAB, (3 * w + g + 1) * SLAB)
                if dil == 1:
                    dp_ref[:, cols] = dsub[w].astype(BF16)
                else:
                    n = seq // dil
                    for r in range(dil):
                        stage[pl.ds(r, n, stride=dil), :] = dsub[w, pl.ds(r * n, n), :]
                    dp_ref[:, cols] = stage[...].astype(BF16)

    return pl.pallas_call(
        body, name="attn_bwd", grid=(bsz, 2),
        in_specs=[pl.BlockSpec(memory_space=pl.ANY), pl.BlockSpec((seq, PAIR_COLS), lambda b, p: (b, p)),
                  pl.BlockSpec((N_GROUPS, seq, SLAB), lambda b, p: (0, b, p)),
                  pl.BlockSpec((N_GROUPS, seq, SLAB), lambda b, p: (0, b, p)),
                  pl.BlockSpec((seq, SLAB), lambda b, p: (b, p)), pl.BlockSpec(memory_space=pl.ANY)],
        out_specs=pl.BlockSpec((seq, PAIR_COLS), lambda b, p: (b, p)),
        out_shape=jax.ShapeDtypeStruct(dproj.shape, BF16), input_output_aliases={0: 0},
        scratch_shapes=[pltpu.VMEM((3, seq, SLAB), BF16), pltpu.VMEM((seq, SLAB), F32), pltpu.VMEM((3, seq, SLAB), F32),
                        pltpu.VMEM((3, seq, SLAB), F32), pltpu.VMEM((3, seq, SLAB), F32),
                        pltpu.VMEM((N_GROUPS, 2, 2, SUB, 2 * SUB), F32),
                        pltpu.VMEM((BLOCKS_PER_TRIP, 2 * SUB, 2 * SUB), F32), pltpu.VMEM((BLOCKS_PER_TRIP, 2 * SUB, 2 * SUB), F32),
                        pltpu.VMEM((BLOCKS_PER_TRIP, 2 * SUB, 2 * SUB), BF16), pltpu.VMEM((BLOCKS_PER_TRIP, 2 * SUB, 2 * SUB), BF16),
                        pltpu.VMEM((BLOCKS_PER_TRIP, 2 * SUB, SLAB), BF16), pltpu.VMEM((BLOCKS_PER_TRIP, 2 * SUB, SLAB), BF16),
                        pltpu.VMEM((BLOCKS_PER_TRIP, SUB, SLAB), F32), pltpu.VMEM((BLOCKS_PER_TRIP, SUB, SLAB), F32)],
        compiler_params=_params("arbitrary", "arbitrary"))(dproj, pa, o_all, lse_all, da, after)


def _grad_h(dproj, w_all, gx0, x2, mod3, seq, lay):
    t, d = x2.shape
    tm, tn = min(512, seq), min(512, d)
    per_seq = seq // tm

    def body(dp_ref, w_ref, gx0_ref, x_ref, scale_ref, gx_ref, dmod_ref):
        dh = _nt(dp_ref[:, 0:ATT], w_ref[:, 0:ATT]) + _nt(dp_ref[:, lay.c0:], w_ref[:, lay.c0:])
        gx_ref[...] = gx0_ref[...] + dh * (1.0 + scale_ref[...])

        @pl.when(pl.program_id(1) % per_seq == 0)
        def _():
            dmod_ref[...] = jnp.zeros_like(dmod_ref)

        dmod_ref[0:1, :] += jnp.sum(dh, axis=0, keepdims=True)
        dmod_ref[1:2, :] += jnp.sum(dh * x_ref[...], axis=0, keepdims=True)

    tile = pl.BlockSpec((tm, tn), lambda j, i: (i, j))
    return pl.pallas_call(
        body, name="grad_h", grid=(d // tn, t // tm),
        in_specs=[pl.BlockSpec((tm, lay.np), lambda j, i: (i, 0)), pl.BlockSpec((tn, lay.np), lambda j, i: (j, 0)),
                  tile, tile,
                  pl.BlockSpec((None, 1, tn), lambda j, i: (i // per_seq, 0, d // tn + j))],
        out_specs=[tile, pl.BlockSpec((None, 8, tn), lambda j, i: (i // per_seq, 0, j))],
        out_shape=[jax.ShapeDtypeStruct((t, d), F32), jax.ShapeDtypeStruct((t // seq, 8, d), F32)],
        compiler_params=_params("arbitrary", "arbitrary"))(dproj, w_all, gx0, x2, mod3)


def _grad_w_in(ht, dproj, seq, lay, part, prev):
    d, t = ht.shape
    tm = seq
    n_i = t // tm

    def make_body(n_skip, n_pieces, tn, nat_tile):
        def body(*refs):
            refs = refs[n_skip:]
            ht_ref, dp_refs = refs[0], refs[1:1 + n_pieces]
            gw_hbm, gb_hbm, acc, bacc, gw_out, gb_out, sem = refs[1 + n_pieces:]
            i, j = pl.program_id(0), pl.program_id(1)
            dp = dp_refs[0][...] if n_pieces == 1 else jnp.concatenate([r[...] for r in dp_refs], axis=1)
            part = jnp.dot(ht_ref[...], dp, preferred_element_type=F32)
            bpart = jnp.sum(dp.astype(F32), axis=0, keepdims=True)

            if n_i > 1:
                @pl.when(i == 0)
                def _():
                    acc[j] = part
                    bacc[j] = bpart

                @pl.when((i > 0) & (i < n_i - 1))
                def _():
                    acc[j] += part
                    bacc[j] += bpart

            @pl.when(i == n_i - 1)
            def _():
                gw_out[...] = ((part + acc[j]) if n_i > 1 else part).astype(BF16)
                gb_out[...] = (bpart + bacc[j]) if n_i > 1 else bpart
                cols = pl.ds(pl.multiple_of(nat_tile(j) * tn, SLAB), tn)
                copies = [pltpu.make_async_copy(gw_out, gw_hbm.at[:, cols], sem.at[0]),
                          pltpu.make_async_copy(gb_out, gb_hbm.at[:, cols], sem.at[1])]
                for cp in copies:
                    cp.start()
                for cp in copies:
                    cp.wait()
        return body

    def call(name, pieces, n_tiles, nat_tile, prev):
        tn = sum(w for w, _ in pieces)
        any_spec = pl.BlockSpec(memory_space=pl.ANY)
        in_specs = [pl.BlockSpec((d, tm), lambda i, j: (0, i))]
        in_specs += [pl.BlockSpec((tm, w), lambda i, j, f=f: (i, f(j))) for w, f in pieces]
        args = [ht] + [dproj] * len(pieces)
        aliases = {}
        if prev is not None:
            in_specs = [any_spec] * 2 + in_specs
            args = list(prev) + args
            aliases = {0: 0, 1: 1}
        return pl.pallas_call(
            make_body(0 if prev is None else 2, len(pieces), tn, nat_tile), name=name, grid=(n_i, n_tiles), in_specs=in_specs,
            out_specs=[any_spec, any_spec],
            out_shape=[jax.ShapeDtypeStruct((d, lay.din), BF16), jax.ShapeDtypeStruct((1, lay.din), F32)],
            input_output_aliases=aliases,
            scratch_shapes=[pltpu.VMEM((n_tiles, d, tn), F32), pltpu.VMEM((n_tiles, 1, tn), F32), pltpu.VMEM((d, tn), BF16),
                            pltpu.VMEM((1, tn), F32), pltpu.SemaphoreType.DMA((2,))],
            compiler_params=_params("arbitrary", "arbitrary"))(*args)

    if part == "attn":
        attn_pieces = [(SLAB, lambda j, m=m: (m % 2) * PAIR_SLABS + 2 * j + m // 2) for m in range(4)]
        return call("grad_w_in_attn", attn_pieces, ATT // 512, lambda j: j, prev)
    base = lay.c0 // CONV_TILE
    nct = lay.n_conv_tiles
    if nct % 2:
        return call("grad_w_in_rest", [(CONV_TILE, lambda j: base + j)], 6 * d // CONV_TILE, lay.rest_nat_tile, prev)
    half = nct // 2

    def rest_piece(m):
        def perm_tile(j):
            conv = base + 4 * (2 * (j % half) + m) + j // half
            return jnp.where(j < 4 * half, conv, base + 2 * j + m)
        return (CONV_TILE, perm_tile)

    return call("grad_w_in_rest", [rest_piece(0), rest_piece(1)], 6 * d // 512, lambda j: ATT // 512 + j, prev)


def _pack_rows(parts, width=128):
    flat = [p.reshape(-1) for p in parts]
    spans, rows = [], 0
    padded = []
    for f in flat:
        n = -(-f.shape[0] // (8 * width)) * 8
        padded.append(jnp.pad(f, (0, n * width - f.shape[0])).reshape(n, width))
        spans.append((rows, f.shape[0]))
        rows += n
    return jnp.concatenate(padded, axis=0), spans


def _unpack_rows(packed, spans, shapes, width=128):
    out = []
    for (row, n), shp in zip(spans, shapes):
        rows = -(-n // width)
        out.append(packed[row:row + rows].reshape(-1)[:n].reshape(shp))
    return out


def kernel(x, c, w_ada, b_ada, w_in, b_in, conv_w, w_proj_attn, w_proj_conv, w_out, b_out, ln_g, ln_b, loss_target, m_w_ada, m_b_ada, m_w_in, m_b_in, m_conv_w, m_w_proj_attn, m_w_proj_conv, m_w_out, m_b_out, m_ln_g, m_ln_b, v_w_ada, v_b_ada, v_w_in, v_b_in, v_conv_w, v_w_proj_attn, v_w_proj_conv, v_w_out, v_b_out, v_ln_g, v_ln_b):
    bsz, seq, d = x.shape
    t = bsz * seq
    lay = _Layout(d)
    col_sharded = [True, True, False, False]
    red_w = [w_in[0], w_proj_attn[0], w_proj_conv[0], w_out[0]]
    chip = 2 * lax.axis_index("x") + lax.axis_index("y")
    chip1 = chip.astype(jnp.int32).reshape(1)
    core1 = lax.axis_index("c").astype(jnp.int32).reshape(1)
    place = jnp.stack([chip, lax.axis_index("c")]).astype(jnp.int32)
    x2 = x.reshape(t, d)
    target2 = loss_target.reshape(t, d)

    mod, act_all = _ada_exchange(c, w_ada[0], b_ada)
    mod3 = mod.reshape(bsz, 1, 3 * d)

    cw_pad = jnp.pad(conv_w[0], ((0, 5), (0, 0))) + 0.0 * mod[0, 0]
    own_in_full = [_cast_into_full(red_w[0], col_sharded[0], chip1, "cast_shard_in")]
    own_in_full += list(_cast_into_full_small(red_w[1:], col_sharded[1:], chip1, "cast_shards_late"))
    (wi_f,), cw8 = _gather_weights(own_in_full[:1], col_sharded[:1], cw_pad)
    cw_full = cw8[0:3]
    late_copies = _direct_gather_copies(col_sharded[1:])
    late_send, late_recv, late_flying, late_token = _start_copies("gather_late_start", own_in_full[1:], (18,), cw8, late_copies)
    w_all = _permute_w_in(wi_f, lay)
    b_all = lay.perm_vector(b_in) + late_token[0, 0]

    rest_tn = 1024 if (6 * d) % 1024 == 0 else 512
    pa, = _project(x2, mod3, w_all, b_all, seq, 0, ATT, PAIR_COLS, BF16, False, "project_attn")
    pr, ht = _project(x2, mod3, w_all, b_all, seq, lay.c0, 6 * d, rest_tn, BF16, True, "project_rest")
    o_all, lse_all, a_in = _attn_fwd(pa, bsz, seq)
    b_in_act = _conv_fwd(pr, cw_full, bsz, seq, d)
    wpa_f, wpc_f, wo_f = _wait_copies("gather_late_wait", late_flying, late_send, late_recv, b_in_act, late_copies)
    (dproj, da_in, db_in, gx0, dgate, small_tail, gw_pa, gw_pc, gw_out) = _tail(
        a_in, b_in_act, pr, x2, target2, mod3, wpa_f, wpc_f, wo_f, b_out, ln_g, ln_b, seq, lay)

    late_views = _shard_views([gw_pa, gw_pc, gw_out], col_sharded[1:])
    late_lands = [lax.empty((v.shape[0], v.shape[1] // 2, v.shape[2]), v.dtype) for v in late_views]
    xl_send, xl_recv, xl_fly, xl_tok = _start_copies("grads_pair_exchange_late_start", late_views + late_lands, (3,), small_tail,
                                                     _pair_exchange_copies(3, False))
    dproj, gcw = _conv_bwd(dproj, db_in, pr, cw_full + xl_tok[0, 0], bsz, seq, lay)
    xl_done = _wait_copies("grads_pair_exchange_late_wait", xl_fly, xl_send, xl_recv, gcw, _pair_exchange_copies(3, True))
    late_parts = list(_pair_sum_small(xl_done[:3], xl_done[3:], core1, "grads_pair_sum_late"))

    late_cross = _chip_scatter_copies(3, col_sharded[1:])
    late_zone = [lax.empty((3, p.shape[1], _piece_cols(p, cs)), p.dtype) for p, cs in zip(late_parts, col_sharded[1:])]
    sl_send, sl_recv, sl_fly, sl_tok = _start_copies("grads_scatter_late_start", late_parts + late_zone, (9,), core1,
                                                     late_cross)
    dproj = _attn_bwd(dproj, pa, o_all, lse_all, da_in, bsz, seq, sl_tok)
    gw_in_bf, gb_in = _grad_w_in(ht, dproj, seq, lay, "rest", None)
    rest_cols, attn_cols = (ATT, 6 * d), (0, ATT)
    in_land = lax.empty((1, d // 2, lay.din), BF16)
    xi_copies = _pair_exchange_copies(1, False, rest_cols)
    xi_send, xi_recv, xi_fly, _ = _start_copies("grads_pair_exchange_in_start", [gw_in_bf.reshape(1, d, lay.din), in_land], (1,),
                                                gb_in, xi_copies)
    gw_in_bf, gb_in = _grad_w_in(ht, dproj, seq, lay, "attn", (xi_fly[0].reshape(d, lay.din), gb_in))
    xi_done = _wait_copies("grads_pair_exchange_in_wait", [gw_in_bf.reshape(1, d, lay.din), xi_fly[1]], xi_send, xi_recv, gb_in,
                           _pair_exchange_copies(1, True, rest_cols))
    in_got = _pair_exchange_into(xi_done[0], xi_done[1], attn_cols, "grads_pair_exchange_in_attn")
    sl_done = _wait_copies("grads_scatter_late_wait", sl_fly, sl_send, sl_recv, in_got, late_cross)
    late_red = list(_chip_sum_small(sl_done[:3], sl_done[3:], col_sharded[1:], place, "grads_chip_sum_late"))

    in_part = _pair_sum(xi_done[0], in_got, core1, "grads_pair_sum_in")
    in_cross = _chip_scatter_copies(1, col_sharded[:1])
    in_zone = [lax.empty((3, in_part.shape[1], _piece_cols(in_part, True)), in_part.dtype)]
    late_views1 = [f.reshape(1, *f.shape) for f in late_red]
    si_send, si_recv, si_fly, si_tok = _start_copies(
        "grads_scatter_in_join_late_start", [in_part] + in_zone + late_views1, (6,), core1,
        _both_copies(in_cross, 2, _pair_join_copies(3, False, sem0=3)))
    grad_x2, dmod = _grad_h(dproj, w_all, gx0, x2, mod3 + si_tok[0, 0], seq, lay)
    si_done = _wait_copies("grads_scatter_in_join_late_wait", si_fly, si_send, si_recv, grad_x2,
                           _both_copies(in_cross, 2, _pair_join_copies(3, True, sem0=3)))
    late_joined = [f[0] for f in si_done[2:]]
    in_red = _chip_sum(si_done[0], si_done[1], True, place, "grads_chip_sum_in")

    d_ada = jnp.concatenate([dmod[:, 0, :], dmod[:, 1, :], dgate[:, 0, :]], axis=1)
    pieces = [small_tail[3], jnp.sum(d_ada, axis=0), gb_in[0], small_tail[0], small_tail[1], small_tail[2], gcw[0:3]]
    packed, spans = _pack_rows(pieces)
    kept_in, _ = _pack_rows([d_ada])
    rows_all = jnp.concatenate([packed, kept_in], axis=0)
    small_land = lax.empty((N_DEV,) + rows_all.shape, F32)
    sm_send, sm_recv, sm_fly, sm_tok = _start_copies(
        "small_gather_join_in_start", [rows_all, small_land, in_red.reshape(1, *in_red.shape)], (N_DEV,), core1,
        _both_copies(_small_gather_copies, 2, _pair_join_copies(1, False, sem0=N_DEV - 1)))
    big_w = [w_ada[0]] + red_w
    big_m = [m_w_ada[0], m_w_in[0], m_w_proj_attn[0], m_w_proj_conv[0], m_w_out[0]]
    big_v = [v_w_ada[0], v_w_in[0], v_w_proj_attn[0], v_w_proj_conv[0], v_w_out[0]]
    big_out = [None] * 5
    for w in range(3):
        big_out[2 + w] = _adamw(big_w[2 + w], late_joined[w], big_m[2 + w], big_v[2 + w], f"adamw_{2 + w}", sm_tok)
    sm_done = _wait_copies("small_gather_join_in_wait", sm_fly, sm_send, sm_recv, big_out[4][0],
                           _both_copies(_small_gather_copies, 2, _pair_join_copies(1, True, sem0=N_DEV - 1)))
    me1 = (4 * lax.axis_index("x") + 2 * lax.axis_index("y") + lax.axis_index("c")).astype(jnp.int32).reshape(1)
    summed, kept = _small_sum(sm_done[0], sm_done[1], me1, packed.shape[0], d)
    loss = summed[0, 0]
    _, g_b_ada, g_b_in, g_b_out, g_ln_g, g_ln_b, g_cw_full = _unpack_rows(
        summed, spans, [(d,), (3 * d,), (lay.din,), (d,), (d,), (d,), (3, d)])
    g_cw = lax.dynamic_slice(g_cw_full, (0, chip * (d // N_CHIPS)), (3, d // N_CHIPS))
    d_ada_all = kept.reshape(N_DEV, -1)[:, :bsz * 3 * d].reshape(N_DEV * bsz, 3 * d)
    ada_cols = 3 * d // N_CHIPS
    g_w_ada = _grad_w_ada_cols(act_all.T, lax.dynamic_slice(d_ada_all, (0, chip * ada_cols), (N_DEV * bsz, ada_cols)))

    big_out[0] = _adamw(big_w[0], g_w_ada, big_m[0], big_v[0], "adamw_0", None)
    small_w = [b_ada, b_in, conv_w[0], b_out, ln_g, ln_b]
    small_g = [g_b_ada, g_b_in, g_cw, g_b_out, g_ln_g, g_ln_b]
    small_m = [m_b_ada, m_b_in, m_conv_w[0], m_b_out, m_ln_g, m_ln_b]
    small_v = [v_b_ada, v_b_in, v_conv_w[0], v_b_out, v_ln_g, v_ln_b]
    pw, sp = _pack_rows(small_w)
    pg, _ = _pack_rows(small_g)
    pm, _ = _pack_rows(small_m)
    pv, _ = _pack_rows(small_v)
    sd, sm, sv = _adamw(pw, pg, pm, pv, "adamw_small", None)
    big_out[1] = _adamw(big_w[1], sm_done[2][0], big_m[1], big_v[1], "adamw_1", None, also_g=True)
    g_big = [g_w_ada, big_out[1][3]] + late_joined
    shapes = [a.shape for a in small_w]
    sd, sm, sv = _unpack_rows(sd, sp, shapes), _unpack_rows(sm, sp, shapes), _unpack_rows(sv, sp, shapes)

    def order(wa, bA, wi, bI, cw, wpa, wpc, wo, bO, lg, lb):
        return (wa[None], bA, wi[None], bI, cw[None], wpa[None], wpc[None], wo[None], bO, lg, lb)

    sg = [g.reshape(s) for g, s in zip(small_g, shapes)]
    grads_out = order(g_big[0], sg[0], g_big[1], sg[1], sg[2], g_big[2], g_big[3], g_big[4], sg[3], sg[4], sg[5])
    outs = []
    for idx, small in enumerate((sd, sm, sv)):
        outs.append(order(big_out[0][idx], small[0], big_out[1][idx], small[1], small[2], big_out[2][idx],
                          big_out[3][idx], big_out[4][idx], small[3], small[4], small[5]))
    return (loss, grad_x2.reshape(bsz, seq, d), *grads_out, *outs[0], *outs[1], *outs[2])
```

```python
import jax
import jax.numpy as jnp
from jax import lax
from jax.experimental import pallas as pl
from jax.experimental.pallas import tpu as pltpu

F32 = jnp.float32
BF16 = jnp.bfloat16
MESH = pl.DeviceIdType.MESH

HEAD_DIM = 64
N_GROUPS = 3
DILATIONS = (1, 4, 16)
N_HEADS = 12
SUB = 128
Q_WIDTH = 768
Z_WIDTH = 256
ATT = 3 * Q_WIDTH + Z_WIDTH
SLAB = 128
PAIR_SLABS = 10
PAIR_COLS = PAIR_SLABS * SLAB
CONV_TILE = 256
SOFTMAX_ROWS = 32
BLOCKS_PER_TRIP = 8
ALIBI_MAX_EXP = 8.0
ALPHA = 2.0 ** 0.25
LN_EPS = 1e-5
ADAM_LR, ADAM_B1, ADAM_B2, ADAM_EPS, ADAM_WD, ADAM_STEP = 0.001, 0.9, 0.999, 1e-08, 0.01, 10
N_CHIPS = 4
N_DEV = 8
VMEM_LIMIT_V7X = 60 * 1024 * 1024
NEG = -1e30


def _params(*sem):
    return pltpu.CompilerParams(dimension_semantics=sem, vmem_limit_bytes=VMEM_LIMIT_V7X)


def _sigmoid(v):
    return 0.5 * jnp.tanh(0.5 * v) + 0.5


class _Layout:
    def __init__(self, d):
        self.d = d
        self.din = ATT + 6 * d
        c0 = 3072
        while c0 % (2 * d):
            c0 += 1024
        self.c0, self.g0, self.np = c0, c0 + 4 * d, c0 + 6 * d
        self.n_conv_tiles = d // CONV_TILE

    def attn_nat_slab(self, s):
        p, i = s // PAIR_SLABS, s % PAIR_SLABS
        return jnp.where(i < 9, (i // 3) * 6 + (i % 3) * 2 + p, 18 + p)

    def rest_nat_tile(self, t):
        n4 = 4 * self.n_conv_tiles
        conv = ATT // CONV_TILE + (t % 4) * self.n_conv_tiles + t // 4
        return jnp.where(t < n4, conv, ATT // CONV_TILE + t)

    def perm_vector(self, v):
        parts = []
        for s in range(2 * PAIR_SLABS):
            p, i = divmod(s, PAIR_SLABS)
            ns = (i // 3) * 6 + (i % 3) * 2 + p if i < 9 else 18 + p
            parts.append(v[:, ns * SLAB:(ns + 1) * SLAB])
        parts.append(jnp.zeros((1, self.c0 - ATT), v.dtype))
        for j in range(self.n_conv_tiles):
            for k in range(4):
                a = ATT + k * self.d + j * CONV_TILE
                parts.append(v[:, a:a + CONV_TILE])
        parts.append(v[:, ATT + 4 * self.d:])
        return jnp.concatenate(parts, axis=1)


def _place():
    return lax.axis_index("x"), lax.axis_index("y"), lax.axis_index("c")


def _other_chips(x, y):
    return [(1 - x, y), (x, 1 - y), (1 - x, 1 - y)]


def _shard_of(ref, col_sharded, chip, half=None):
    if col_sharded:
        cs = ref.shape[1] // N_CHIPS
        cols = pl.ds(pl.multiple_of(chip * cs, SLAB), cs)
        if half is None:
            return ref.at[:, cols]
        n = ref.shape[0] // 2
        return ref.at[pl.ds(half * n, n), cols]
    rs = ref.shape[0] // N_CHIPS
    if half is None:
        return ref.at[pl.ds(chip * rs, rs)]
    return ref.at[pl.ds(chip * rs + half * (rs // 2), rs // 2)]


GATHER_PIECES = 4


def _cast_into_full(shard, col_sharded, chip, name):
    rows, cols = shard.shape
    tr = _row_tile(rows, cols)
    nb = rows // tr

    def body(chip_ref, s_ref, o_ref):
        del chip_ref
        o_ref[...] = s_ref[...].astype(BF16)

    if col_sharded:
        full, out_spec = (rows, cols * N_CHIPS), pl.BlockSpec((tr, cols), lambda i, ch: (i, ch[0]))
    else:
        full, out_spec = (rows * N_CHIPS, cols), pl.BlockSpec((tr, cols), lambda i, ch: (ch[0] * nb + i, 0))
    return pl.pallas_call(
        body, name=name,
        grid_spec=pltpu.PrefetchScalarGridSpec(num_scalar_prefetch=1, grid=(nb,),
                                               in_specs=[pl.BlockSpec((tr, cols), lambda i, ch: (i, 0))], out_specs=out_spec),
        out_shape=jax.ShapeDtypeStruct(full, BF16), compiler_params=_params("parallel"))(chip, shard)


def _cast_into_full_small(shards, col_sharded, chip, name):
    n = len(shards)

    def body(chip_ref, *refs):
        del chip_ref
        for w in range(n):
            refs[n + w][...] = refs[w][...].astype(BF16)

    in_specs = [pl.BlockSpec(s.shape, lambda i, ch: (0, 0)) for s in shards]
    out_specs = [pl.BlockSpec(s.shape, (lambda i, ch: (0, ch[0])) if cs else (lambda i, ch: (ch[0], 0)))
                 for s, cs in zip(shards, col_sharded)]
    fulls = [(s.shape[0], s.shape[1] * N_CHIPS) if cs else (s.shape[0] * N_CHIPS, s.shape[1]) for s, cs in zip(shards, col_sharded)]
    return pl.pallas_call(
        body, name=name,
        grid_spec=pltpu.PrefetchScalarGridSpec(num_scalar_prefetch=1, grid=(1,), in_specs=in_specs, out_specs=out_specs),
        out_shape=[jax.ShapeDtypeStruct(f, BF16) for f in fulls], compiler_params=_params("arbitrary"))(chip, *shards)


def _gather_weights(fulls, col_sharded, small):
    n = len(fulls)
    kp = GATHER_PIECES

    def piece(ref, cs, chip, half, k):
        if cs:
            width = ref.shape[1] // N_CHIPS
            rows = ref.shape[0] // 2 // kp
            return ref.at[pl.ds(half * (ref.shape[0] // 2) + k * rows, rows), pl.ds(pl.multiple_of(chip * width, SLAB), width)]
        rs = ref.shape[0] // N_CHIPS
        rows = rs // 2 // kp
        return ref.at[pl.ds(chip * rs + half * (rs // 2) + k * rows, rows)]

    def body(*refs):
        sm_in, outs, sm_out = refs[n], refs[n + 1:2 * n + 1], refs[2 * n + 1]
        send, recv, fsend, frecv, lsem, ssend, srecv = refs[2 * n + 2:]
        x, y, c = _place()
        mine = 2 * x + y
        sibling = (x, y, 1 - c)
        first = (x ^ (1 - c), y ^ c)
        second = (x ^ c, y ^ (1 - c))
        diagonal = (1 - x, 1 - y)
        sources = [first, second, diagonal]
        senders = [first, second, second]

        def copy(ref, sems, slot, to):
            return pltpu.make_async_remote_copy(src_ref=ref, dst_ref=ref, send_sem=sems[0].at[slot], recv_sem=sems[1].at[slot],
                                                device_id=to, device_id_type=MESH)

        local = pltpu.make_async_copy(sm_in, _shard_of(sm_out, True, mine), lsem)
        local.start()
        sends = []
        for k, (cx, cy) in enumerate(_other_chips(x, y)):
            cp = pltpu.make_async_remote_copy(src_ref=sm_in, dst_ref=_shard_of(sm_out, True, mine), send_sem=ssend.at[k],
                                              recv_sem=srecv.at[k], device_id=(cx, cy, c), device_id_type=MESH)
            cp.start()
            sends.append(cp)
        for k in range(kp):
            for w in range(n):
                own = piece(outs[w], col_sharded[w], mine, c, k)
                for slot, chip in enumerate((first, second)):
                    cp = copy(own, (send, recv), (w * 3 + slot) * kp + k, (*chip, c))
                    cp.start()
                    sends.append(cp)
        for slot in range(3):
            source = 2 * sources[slot][0] + sources[slot][1]
            for k in range(kp):
                for w in range(n):
                    landed = piece(outs[w], col_sharded[w], source, c, k)
                    copy(landed, (send, recv), (w * 3 + slot) * kp + k, (*senders[slot], c)).wait_recv()
                    if slot == 0:
                        cp = copy(landed, (send, recv), (w * 3 + 2) * kp + k, (*second, c))
                        cp.start()
                        sends.append(cp)
                    cp = copy(landed, (fsend, frecv), (w * 3 + slot) * kp + k, sibling)
                    cp.start()
                    sends.append(cp)
        for slot, chip in enumerate((second, first, diagonal)):
            for k in range(kp):
                for w in range(n):
                    passed = piece(outs[w], col_sharded[w], 2 * chip[0] + chip[1], 1 - c, k)
                    copy(passed, (fsend, frecv), (w * 3 + slot) * kp + k, sibling).wait_recv()
        for k, (cx, cy) in enumerate(_other_chips(x, y)):
            theirs = _shard_of(sm_out, True, 2 * cx + cy)
            pltpu.make_async_remote_copy(src_ref=theirs, dst_ref=theirs, send_sem=ssend.at[k], recv_sem=srecv.at[k],
                                         device_id=(cx, cy, c), device_id_type=MESH).wait_recv()
        for cp in sends:
            cp.wait_send()
        local.wait()

    any_spec = pl.BlockSpec(memory_space=pl.ANY)
    outs = pl.pallas_call(
        body, name="gather_weights",
        out_shape=[jax.ShapeDtypeStruct(f.shape, BF16) for f in fulls]
        + [jax.ShapeDtypeStruct((small.shape[0], small.shape[1] * N_CHIPS), small.dtype)],
        in_specs=[any_spec] * (n + 1), out_specs=[any_spec] * (n + 1), input_output_aliases={w: w for w in range(n)},
        scratch_shapes=[pltpu.SemaphoreType.DMA((n * 3 * kp,)), pltpu.SemaphoreType.DMA((n * 3 * kp,)),
                        pltpu.SemaphoreType.DMA((n * 3 * kp,)), pltpu.SemaphoreType.DMA((n * 3 * kp,)), pltpu.SemaphoreType.DMA,
                        pltpu.SemaphoreType.DMA((3,)), pltpu.SemaphoreType.DMA((3,))],
    )(*fulls, small)
    return outs[:n], outs[n]


HBM_SPEC = pl.BlockSpec(memory_space=pltpu.HBM)
SEM_SPEC = pl.BlockSpec(memory_space=pltpu.SEMAPHORE)
DATAFLOW = pltpu.SideEffectType.DATAFLOW_SIDE_EFFECTING


def _start_copies(name, arrays, sem_shape, after, copies):
    n = len(arrays)

    def body(*refs):
        for cp in copies(refs[:n], refs[n + 1], refs[n + 2]):
            cp.start()
        token = refs[2 * n + 3]
        token[...] = jnp.zeros_like(token)

    res = pl.pallas_call(
        body, name=name,
        out_shape=(pltpu.SemaphoreType.DMA(sem_shape), pltpu.SemaphoreType.DMA(sem_shape),
                   *[pltpu.HBM(a.shape, a.dtype) for a in arrays], jax.ShapeDtypeStruct((8, 128), F32)),
        in_specs=[HBM_SPEC] * n + [pl.BlockSpec(memory_space=pl.ANY)],
        out_specs=(SEM_SPEC, SEM_SPEC, *([HBM_SPEC] * n), pl.BlockSpec(memory_space=pltpu.VMEM)),
        input_output_aliases={i: 2 + i for i in range(n)},
        compiler_params=pltpu.CompilerParams(has_side_effects=DATAFLOW),
    )(*[pltpu.with_memory_space_constraint(a, pltpu.HBM) for a in arrays], after)
    return res[0], res[1], list(res[2:2 + n]), res[2 + n]


def _wait_copies(name, arrays, send, recv, after, copies):
    n = len(arrays)

    def body(*refs):
        for cp in copies(refs[:n], refs[n], refs[n + 1]):
            cp.wait_send()
            cp.wait_recv()

    return pl.pallas_call(
        body, name=name, out_shape=[pltpu.HBM(a.shape, a.dtype) for a in arrays],
        in_specs=[HBM_SPEC] * n + [SEM_SPEC, SEM_SPEC, pl.BlockSpec(memory_space=pl.ANY)], out_specs=[HBM_SPEC] * n,
        input_output_aliases={i: i for i in range(n)},
        compiler_params=pltpu.CompilerParams(has_side_effects=DATAFLOW),
    )(*arrays, send, recv, after)


def _direct_gather_copies(col_sharded):
    def copies(refs, send, recv):
        x, y, c = _place()
        mine = 2 * x + y
        out = []
        for w, ref in enumerate(refs):
            own_half = _shard_of(ref, col_sharded[w], mine, c)
            k = 0
            for cx, cy in _other_chips(x, y):
                for pc in (c, 1 - c):
                    out.append(pltpu.make_async_remote_copy(
                        src_ref=own_half, dst_ref=own_half, send_sem=send.at[6 * w + k], recv_sem=recv.at[6 * w + k],
                        device_id=(cx, cy, pc), device_id_type=MESH))
                    k += 1
        return out
    return copies


def _chip_scatter_copies(n, col_sharded):
    def piece(ref, cs, chip):
        if cs:
            w = ref.shape[2] // N_CHIPS
            return ref.at[:, :, pl.ds(pl.multiple_of(chip * w, SLAB), w)]
        return ref.at[pl.ds(chip, 1)]

    def copies(refs, send, recv):
        x, y, c = _place()
        out = []
        for k, (cx, cy) in enumerate(_other_chips(x, y)):
            for w in range(n):
                out.append(pltpu.make_async_remote_copy(
                    src_ref=piece(refs[w], col_sharded[w], 2 * cx + cy), dst_ref=refs[n + w].at[pl.ds(k, 1)],
                    send_sem=send.at[3 * w + k], recv_sem=recv.at[3 * w + k], device_id=(cx, cy, c), device_id_type=MESH))
        return out
    return copies


def _shard_views(gs, col_sharded):
    return [g.reshape(1, *g.shape) if cs else g.reshape(N_CHIPS, g.shape[0] // N_CHIPS, g.shape[1])
            for g, cs in zip(gs, col_sharded)]


DMA_CHUNK_BYTES = 1 << 20


def _chunk_rows(shape, itemsize):
    s, rows, cols = shape
    n = 1
    while s * (rows // n) * cols * itemsize > DMA_CHUNK_BYTES and (rows // n) % 32 == 0:
        n *= 2
    return rows // n


def _rows_of(ref, row0, rows, cols):
    if cols is None:
        return ref.at[:, pl.ds(row0, rows)]
    return ref.at[:, pl.ds(row0, rows), pl.ds(cols[0], cols[1])]


def _row_pieces(src, src_row0, dst, dst_row0, rows, send_sem, recv_sem, device, cols=None):
    width = src.shape[2] if cols is None else cols[1]
    step = _chunk_rows((src.shape[0], rows, width), src.dtype.itemsize)
    return [pltpu.make_async_remote_copy(src_ref=_rows_of(src, src_row0 + r, step, cols), dst_ref=_rows_of(dst, dst_row0 + r, step, cols),
                                         send_sem=send_sem, recv_sem=recv_sem, device_id=device, device_id_type=MESH)
            for r in range(0, rows, step)]


def _pair_exchange_copies(n, whole, cols=None):
    def copies(refs, send, recv):
        x, y, c = _place()
        sibling = (x, y, 1 - c)
        out = []
        for w in range(n):
            hr = refs[n + w].shape[1]
            if whole:
                out.append(pltpu.make_async_remote_copy(
                    src_ref=_rows_of(refs[w], (1 - c) * hr, hr, cols), dst_ref=_rows_of(refs[n + w], 0, hr, cols),
                    send_sem=send.at[w], recv_sem=recv.at[w], device_id=sibling, device_id_type=MESH))
            else:
                out += _row_pieces(refs[w], (1 - c) * hr, refs[n + w], 0, hr, send.at[w], recv.at[w], sibling, cols)
        return out
    return copies


def _pair_join_copies(n, whole, sem0=0):
    def copies(refs, send, recv):
        x, y, c = _place()
        sibling = (x, y, 1 - c)
        out = []
        for w in range(n):
            hr = refs[w].shape[1] // 2
            sems = dict(send_sem=send.at[sem0 + w], recv_sem=recv.at[sem0 + w])
            if whole:
                out.append(pltpu.make_async_remote_copy(
                    src_ref=refs[w].at[:, pl.ds(c * hr, hr)], dst_ref=refs[w].at[:, pl.ds((1 - c) * hr, hr)],
                    device_id=sibling, device_id_type=MESH, **sems))
            else:
                out += _row_pieces(refs[w], c * hr, refs[w], c * hr, hr, sems["send_sem"], sems["recv_sem"], sibling)
        return out
    return copies


def _both_copies(first, n_first, second):
    def copies(refs, send, recv):
        return first(refs[:n_first], send, recv) + second(refs[n_first:], send, recv)
    return copies


def _small_gather_copies(refs, send, recv):
    vec, land = refs
    x, y, c = _place()
    me = 4 * x + 2 * y + c
    return [pltpu.make_async_remote_copy(src_ref=vec, dst_ref=land.at[me], send_sem=send.at[k], recv_sem=recv.at[k],
                                         device_id=peer, device_id_type=MESH) for k, (peer, _) in enumerate(_all_devices(x, y, c))]


def _small_sum(vec, land, me, n_sum, d):
    rows = vec.shape[0]

    def body(me_ref, v_ref, l_ref, sum_ref, kept_ref):
        def slot(k):
            return jnp.where(me_ref[0] == k, v_ref[...], l_ref[k])

        total = slot(0)[0:n_sum, :]
        kept_ref[0] = slot(0)[n_sum:rows, :]
        for k in range(1, N_DEV):
            total = total + slot(k)[0:n_sum, :]
            kept_ref[k] = slot(k)[n_sum:rows, :]
        sum_ref[...] = total
        loss = 0.5 / d * jnp.sum(total[0:8, :])
        sum_ref[0:8, :] = jnp.full((8, 128), loss, F32)

    vm = pl.BlockSpec(memory_space=pltpu.VMEM)
    return pl.pallas_call(
        body, name="small_sum", in_specs=[pl.BlockSpec(memory_space=pltpu.SMEM), vm, vm], out_specs=[vm, vm],
        out_shape=[jax.ShapeDtypeStruct((n_sum, 128), F32), jax.ShapeDtypeStruct((N_DEV, rows - n_sum, 128), F32)],
        compiler_params=pltpu.CompilerParams(vmem_limit_bytes=VMEM_LIMIT_V7X))(me, vec, land)


def _pair_exchange_into(view, land, cols, name):
    def body(v_ref, l_in, l_ref, send, recv):
        del l_in
        x, y, c = _place()
        sibling = (x, y, 1 - c)
        hr = l_ref.shape[1]
        for cp in _row_pieces(v_ref, (1 - c) * hr, l_ref, 0, hr, send, recv, sibling, cols):
            cp.start()
        pltpu.make_async_remote_copy(src_ref=_rows_of(v_ref, (1 - c) * hr, hr, cols), dst_ref=_rows_of(l_ref, 0, hr, cols),
                                     send_sem=send, recv_sem=recv, device_id=sibling, device_id_type=MESH).wait()

    any_spec = pl.BlockSpec(memory_space=pl.ANY)
    return pl.pallas_call(
        body, name=name, out_shape=jax.ShapeDtypeStruct(land.shape, land.dtype),
        in_specs=[any_spec, any_spec], out_specs=any_spec, input_output_aliases={1: 0},
        scratch_shapes=[pltpu.SemaphoreType.DMA, pltpu.SemaphoreType.DMA],
    )(view, land)


def _pair_sum(view, got, core, name):
    s, r, cols = view.shape
    hr = r // 2
    tr = _row_tile(hr, cols, itemsize=view.dtype.itemsize, budget=4 << 20)
    nb = hr // tr

    def body(core_ref, a_ref, b_ref, o_ref):
        del core_ref
        o_ref[...] = (a_ref[...].astype(F32) + b_ref[...].astype(F32)).astype(BF16)

    same = pl.BlockSpec((None, tr, cols), lambda j, i, core_ref: (j, i, 0))
    return pl.pallas_call(
        body, name=name,
        grid_spec=pltpu.PrefetchScalarGridSpec(
            num_scalar_prefetch=1, grid=(s, nb),
            in_specs=[pl.BlockSpec((None, tr, cols), lambda j, i, core_ref: (j, core_ref[0] * nb + i, 0)), same],
            out_specs=same),
        out_shape=jax.ShapeDtypeStruct((s, hr, cols), BF16), compiler_params=_params("parallel", "parallel"))(core, view, got)


def _pair_sum_small(views, gots, core, name):
    n = len(views)

    def body(core_ref, *refs):
        del core_ref
        for w in range(n):
            refs[2 * n + w][...] = (refs[w][...].astype(F32) + refs[n + w][...].astype(F32)).astype(BF16)

    halves = [(v.shape[0], v.shape[1] // 2, v.shape[2]) for v in views]
    own = [pl.BlockSpec(h, lambda i, core_ref: (0, core_ref[0], 0)) for h in halves]
    whole = [pl.BlockSpec(h, lambda i, core_ref: (0, 0, 0)) for h in halves]
    return pl.pallas_call(
        body, name=name,
        grid_spec=pltpu.PrefetchScalarGridSpec(num_scalar_prefetch=1, grid=(1,), in_specs=own + whole, out_specs=whole),
        out_shape=[jax.ShapeDtypeStruct(h, BF16) for h in halves], compiler_params=_params("arbitrary"))(core, *views, *gots)


def _piece_cols(part, col_sharded):
    return part.shape[2] // N_CHIPS if col_sharded else part.shape[2]


def _chip_sum_small(parts, gots, col_sharded, place, name):
    n = len(parts)

    def body(place_ref, *refs):
        del place_ref
        for w in range(n):
            got = refs[n + w]
            acc = refs[w][...].astype(F32) + got[0].astype(F32)
            refs[2 * n + w][...] = acc + got[1].astype(F32) + got[2].astype(F32)

    own, others, outs, shapes = [], [], [], []
    for p, cs in zip(parts, col_sharded):
        hr, cols = p.shape[1], _piece_cols(p, cs)
        own.append(pl.BlockSpec((None, hr, cols), (lambda i, pr: (0, 0, pr[0])) if cs else (lambda i, pr: (pr[0], 0, 0))))
        others.append(pl.BlockSpec((3, hr, cols), lambda i, pr: (0, 0, 0)))
        outs.append(pl.BlockSpec((hr, cols), lambda i, pr: (pr[1], 0)))
        shapes.append(jax.ShapeDtypeStruct((2 * hr, cols), F32))
    return pl.pallas_call(
        body, name=name,
        grid_spec=pltpu.PrefetchScalarGridSpec(num_scalar_prefetch=1, grid=(1,), in_specs=own + others, out_specs=outs),
        out_shape=shapes, compiler_params=_params("arbitrary"))(place, *parts, *gots)


def _chip_sum(part, got, col_sharded, place, name):
    _, hr, _ = part.shape
    cols = _piece_cols(part, col_sharded)
    tr = _row_tile(hr, cols)
    nb = hr // tr

    def body(place_ref, own_ref, g0_ref, g1_ref, g2_ref, o_ref):
        del place_ref
        acc = own_ref[...].astype(F32) + g0_ref[...].astype(F32)
        o_ref[...] = acc + g1_ref[...].astype(F32) + g2_ref[...].astype(F32)

    if col_sharded:
        own = pl.BlockSpec((None, tr, cols), lambda i, pr: (0, i, pr[0]))
    else:
        own = pl.BlockSpec((None, tr, cols), lambda i, pr: (pr[0], i, 0))
    others = [pl.BlockSpec((None, tr, cols), lambda i, pr, k=k: (k, i, 0)) for k in range(3)]
    return pl.pallas_call(
        body, name=name,
        grid_spec=pltpu.PrefetchScalarGridSpec(
            num_scalar_prefetch=1, grid=(nb,), in_specs=[own] + others,
            out_specs=pl.BlockSpec((tr, cols), lambda i, pr: (pr[1] * nb + i, 0))),
        out_shape=jax.ShapeDtypeStruct((2 * hr, cols), F32), compiler_params=_params("parallel"))(place, part, got, got, got)


def _row_tile(rows, cols, itemsize=4, budget=2 << 20):
    t = rows
    while t * cols * itemsize > budget and t % 16 == 0:
        t //= 2
    return t


def _adamw(w, g, m, v, name, after, also_g=False):
    rows, cols = w.shape
    tr = _row_tile(rows, cols, budget=1 << 20)
    extra = [] if after is None else [after]
    n_out = 4 if also_g else 3

    def body(w_ref, g_ref, m_ref, v_ref, *rest):
        d_ref, nm_ref, nv_ref = rest[len(extra):len(extra) + 3]
        g_ = g_ref[...]
        if also_g:
            rest[len(extra) + 3][...] = g_
        nm = ADAM_B1 * m_ref[...] + (1.0 - ADAM_B1) * g_
        nv = ADAM_B2 * v_ref[...] + (1.0 - ADAM_B2) * (g_ * g_)
        m_hat = nm / (1.0 - ADAM_B1 ** ADAM_STEP)
        v_hat = nv / (1.0 - ADAM_B2 ** ADAM_STEP)
        d_ref[...] = -ADAM_LR * (m_hat / (jnp.sqrt(v_hat) + ADAM_EPS) + ADAM_WD * w_ref[...])
        nm_ref[...] = nm
        nv_ref[...] = nv

    spec = pl.BlockSpec((tr, cols), lambda i: (i, 0))
    shp = jax.ShapeDtypeStruct((rows, cols), F32)
    return pl.pallas_call(body, name=name, grid=(rows // tr,),
                          in_specs=[spec] * 4 + [pl.BlockSpec(memory_space=pl.ANY)] * len(extra), out_specs=[spec] * n_out,
                          out_shape=[shp] * n_out, compiler_params=_params("parallel"))(w, g, m, v, *extra)


def _all_devices(x, y, c):
    out = []
    for k in range(1, N_DEV):
        peer = (x ^ ((k >> 2) & 1), y ^ ((k >> 1) & 1), c ^ (k & 1))
        out.append((peer, 4 * peer[0] + 2 * peer[1] + peer[2]))
    return out


def _ada_exchange(c, w_shard, b_ada):
    bsz, d = c.shape
    cs = w_shard.shape[1]

    def body(c_ref, w_ref, b_ref, mod_ref, act_ref, c_all, part, pieces, csend, crecv, psend, precv):
        x, y, core = _place()
        me = 4 * x + 2 * y + core
        chip = 2 * x + y
        c_all[me] = c_ref[...]
        peers = _all_devices(x, y, core)
        copies = []
        for k, (peer, _) in enumerate(peers):
            cp = pltpu.make_async_remote_copy(src_ref=c_ref, dst_ref=c_all.at[me], send_sem=csend.at[k], recv_sem=crecv.at[k],
                                              device_id=peer, device_id_type=MESH)
            cp.start()
            copies.append(cp)
        for k, (_, src) in enumerate(peers):
            pltpu.make_async_remote_copy(src_ref=c_ref, dst_ref=c_all.at[src], send_sem=csend.at[k], recv_sem=crecv.at[k],
                                         device_id=(x, y, core), device_id_type=MESH).wait_recv()
        rows = jnp.concatenate([c_all[i] for i in range(N_DEV)], axis=0)
        act = rows * _sigmoid(rows)
        act_ref[...] = act
        prod = jnp.dot(act.astype(BF16), w_ref[...].astype(BF16), preferred_element_type=F32)
        for i in range(N_DEV):
            part[i] = prod[i * bsz:(i + 1) * bsz, :]
        pieces[chip] = part[me]
        chips = _other_chips(x, y)
        for k, (cx, cy) in enumerate(chips):
            cp = pltpu.make_async_remote_copy(src_ref=part.at[4 * cx + 2 * cy + core], dst_ref=pieces.at[chip],
                                              send_sem=psend.at[k], recv_sem=precv.at[k], device_id=(cx, cy, core),
                                              device_id_type=MESH)
            cp.start()
            copies.append(cp)
        for k, (cx, cy) in enumerate(chips):
            pltpu.make_async_remote_copy(src_ref=part.at[me], dst_ref=pieces.at[2 * cx + cy], send_sem=psend.at[k],
                                         recv_sem=precv.at[k], device_id=(cx, cy, core), device_id_type=MESH).wait_recv()
        for cp in copies:
            cp.wait_send()
        mod_ref[...] = jnp.concatenate([pieces[j] for j in range(N_CHIPS)], axis=1) + b_ref[...]

    vm = pl.BlockSpec(memory_space=pltpu.VMEM)
    return pl.pallas_call(
        body, name="ada_exchange", in_specs=[vm] * 3, out_specs=[vm] * 2,
        out_shape=[jax.ShapeDtypeStruct((bsz, 3 * d), F32), jax.ShapeDtypeStruct((N_DEV * bsz, d), F32)],
        scratch_shapes=[pltpu.VMEM((N_DEV, bsz, d), F32), pltpu.VMEM((N_DEV, bsz, cs), F32), pltpu.VMEM((N_CHIPS, bsz, cs), F32),
                        pltpu.SemaphoreType.DMA((N_DEV - 1,)), pltpu.SemaphoreType.DMA((N_DEV - 1,)),
                        pltpu.SemaphoreType.DMA((3,)), pltpu.SemaphoreType.DMA((3,))],
        compiler_params=pltpu.CompilerParams(vmem_limit_bytes=VMEM_LIMIT_V7X))(c, w_shard, b_ada)


def _grad_and_adamw_w_ada(act_t, d_cols, w, m, v):
    d, n = act_t.shape
    cs = d_cols.shape[1]
    tn = min(256, cs)

    def body(a_ref, g_ref, w_ref, m_ref, v_ref, go_ref, d_ref, nm_ref, nv_ref):
        a, g = a_ref[...], g_ref[...]
        acc = a[:, 0:1] * g[0:1, :]
        for b in range(1, n):
            acc = acc + a[:, b:b + 1] * g[b:b + 1, :]
        go_ref[...] = acc
        nm = ADAM_B1 * m_ref[...] + (1.0 - ADAM_B1) * acc
        nv = ADAM_B2 * v_ref[...] + (1.0 - ADAM_B2) * (acc * acc)
        m_hat = nm / (1.0 - ADAM_B1 ** ADAM_STEP)
        v_hat = nv / (1.0 - ADAM_B2 ** ADAM_STEP)
        d_ref[...] = -ADAM_LR * (m_hat / (jnp.sqrt(v_hat) + ADAM_EPS) + ADAM_WD * w_ref[...])
        nm_ref[...] = nm
        nv_ref[...] = nv

    tile = pl.BlockSpec((d, tn), lambda j: (0, j))
    shp = jax.ShapeDtypeStruct((d, cs), F32)
    return pl.pallas_call(body, name="adamw_w_ada", grid=(cs // tn,),
                          in_specs=[pl.BlockSpec((d, n), lambda j: (0, 0)), pl.BlockSpec((n, tn), lambda j: (0, j)), tile, tile, tile],
                          out_specs=[tile] * 4, out_shape=[shp] * 4, compiler_params=_params("parallel"))(act_t, d_cols, w, m, v)


def _permute_w_in(w_nat, lay):
    d = lay.d
    group = 4

    def call(name, width, n_pieces, nat_piece, out_block0, prev):
        def body(*refs):
            refs[-1][...] = jnp.concatenate([r[...] for r in refs[:group]], axis=1)

        in_specs = [pl.BlockSpec((d, width), lambda s, m=m: (0, nat_piece(group * s + m))) for m in range(group)]
        args = [w_nat] * group
        aliases = {}
        if prev is not None:
            in_specs.append(pl.BlockSpec(memory_space=pl.ANY))
            args.append(prev)
            aliases = {group: 0}
        return pl.pallas_call(
            body, name=name, grid=(n_pieces // group,), in_specs=in_specs,
            out_specs=pl.BlockSpec((d, group * width), lambda s: (0, out_block0 + s)),
            out_shape=jax.ShapeDtypeStruct((d, lay.np), BF16), input_output_aliases=aliases,
            compiler_params=_params("arbitrary"))(*args)

    w_all = call("permute_w_attn", SLAB, 2 * PAIR_SLABS, lay.attn_nat_slab, 0, None)
    n_rest = 6 * d // CONV_TILE
    if n_rest % group:
        group = 2
    return call("permute_w_rest", CONV_TILE, n_rest, lay.rest_nat_tile, lay.c0 // (group * CONV_TILE), w_all)


def _project(x2, mod3, w_all, b_all, seq, col0, ncols, tn, out_dtype, want_ht, name):
    t, d = x2.shape
    tm = min(2048, seq)
    per_seq = seq // tm
    j0 = col0 // tn

    def body(x_ref, mod_ref, w_ref, b_ref, o_ref, *rest):
        h_ref = rest[-1]

        @pl.when(pl.program_id(1) == 0)
        def _():
            h = x_ref[...] * (1.0 + mod_ref[:, d:2 * d]) + mod_ref[:, 0:d]
            h_ref[...] = h.astype(BF16)
            if want_ht:
                rest[0][...] = h.T.astype(BF16)

        o_ref[...] = (jnp.dot(h_ref[...], w_ref[...], preferred_element_type=F32) + b_ref[...]).astype(out_dtype)

    out_shape = [jax.ShapeDtypeStruct((t, ncols), out_dtype)]
    out_specs = [pl.BlockSpec((tm, tn), lambda i, j: (i, j))]
    if want_ht:
        out_shape.append(jax.ShapeDtypeStruct((d, t), BF16))
        out_specs.append(pl.BlockSpec((d, tm), lambda i, j: (0, i)))
    return pl.pallas_call(
        body, name=name, grid=(t // tm, ncols // tn),
        in_specs=[pl.BlockSpec((tm, d), lambda i, j: (i, 0)),
                  pl.BlockSpec((None, 1, 3 * d), lambda i, j: (i // per_seq, 0, 0)),
                  pl.BlockSpec((d, tn), lambda i, j: (0, j0 + j)),
                  pl.BlockSpec((1, tn), lambda i, j: (0, j0 + j))],
        out_specs=out_specs, out_shape=out_shape,
        scratch_shapes=[pltpu.VMEM((tm, d), BF16)],
        compiler_params=_params("arbitrary", "arbitrary"))(x2, mod3, w_all, b_all)


def _slope(g, p, hh):
    head = 4 * g + 2 * p + hh
    return 2.0 ** (-ALIBI_MAX_EXP * (head + 1.0) / N_HEADS)


def _ld_rows(ref, start, n, stride):
    if stride == 1:
        return ref[pl.ds(start, n), :]
    return ref[pl.ds(start, n, stride=stride), :]


def _st_rows(ref, start, n, stride, val):
    if stride == 1:
        ref[pl.ds(start, n), :] = val
    else:
        ref[pl.ds(start, n, stride=stride), :] = val


def _sub_blocks(g, seq):
    return seq // DILATIONS[g] // SUB


def _key_rows(g, seq):
    return SUB if _sub_blocks(g, seq) == 1 else 2 * SUB


def _fill_bias(bias_ref, p, seq):
    for g in range(N_GROUPS):
        nk = _key_rows(g, seq)
        diff = lax.broadcasted_iota(jnp.int32, (SUB, nk), 0) - lax.broadcasted_iota(jnp.int32, (SUB, nk), 1)
        for i, off in enumerate((0, SUB)):
            if i == 1 and nk == SUB:
                continue
            delta = diff + off
            ok = (delta >= 0) & (delta <= SUB)
            dist = (delta * DILATIONS[g]).astype(F32)
            for hh in range(2):
                slope = jnp.where(p == 0, _slope(g, 0, hh), _slope(g, 1, hh))
                bias_ref[g, i, hh, :, 0:nk] = jnp.where(ok, -slope * dist, NEG)


def _to_sub_major(pa_ref, col, sub_ref, stage, dil, seq):
    cols = slice(col * SLAB, (col + 1) * SLAB)
    if dil == 1:
        sub_ref[...] = pa_ref[:, cols]
        return
    n = seq // dil
    stage[...] = pa_ref[:, cols].astype(F32)
    for r in range(dil):
        sub_ref[pl.ds(r * n, n), :] = stage[pl.ds(r, n, stride=dil), :].astype(BF16)


def _block_rows(it, g, seq):
    dil, nb = DILATIONS[g], _sub_blocks(g, seq)
    row0 = pl.multiple_of(it * SUB, SUB)
    if nb == 1:
        return row0, row0, 0, it
    blk = it % nb
    first = blk == 0
    krow0 = pl.multiple_of(row0 - jnp.where(first, 0, SUB), SUB)
    nat = row0 if dil == 1 else it // nb + dil * SUB * blk
    return row0, krow0, jnp.where(first, 0, 1), nat


def _nt(a, b):
    return lax.dot_general(a, b, (((1,), (1,)), ((), ())), preferred_element_type=F32)


def _tn(a, b):
    return lax.dot_general(a, b, (((0,), (0,)), ((), ())), preferred_element_type=F32)


def _head_sums(t):
    rows = t.shape[0]
    lo = jnp.broadcast_to(jnp.sum(t[:, :HEAD_DIM], axis=-1, keepdims=True), (rows, HEAD_DIM))
    hi = jnp.broadcast_to(jnp.sum(t[:, HEAD_DIM:], axis=-1, keepdims=True), (rows, HEAD_DIM))
    return jnp.concatenate([lo, hi], axis=-1)


def _attn_fwd(pa, bsz, seq):
    t = pa.shape[0]
    n_blocks = seq // SUB
    chunk = 256

    def body(pa_ref, o_ref, lse_ref, a_ref, sub, stage, bias_ref, s_buf, p_buf, l_buf):
        p = pl.program_id(1)
        _fill_bias(bias_ref, p, seq)
        head0 = lax.broadcasted_iota(jnp.int32, (SUB, SLAB), 1) < HEAD_DIM
        for g in range(N_GROUPS):
            dil = DILATIONS[g]
            for w in range(3):
                _to_sub_major(pa_ref, 3 * w + g, sub.at[w], stage, dil, seq)
            nk = _key_rows(g, seq)

            def trip(i, carry, g=g, dil=dil, nk=nk):
                places = [_block_rows(BLOCKS_PER_TRIP * i + j, g, seq) for j in range(BLOCKS_PER_TRIP)]
                for j, (row0, krow0, _, _) in enumerate(places):
                    q = sub[0, pl.ds(row0, SUB), :]
                    zero = jnp.zeros_like(q)
                    q2 = jnp.concatenate([jnp.where(head0, q, zero), jnp.where(head0, zero, q)], axis=0) * (HEAD_DIM ** -0.5)
                    s_buf[j, :, 0:nk] = _nt(q2, sub[1, pl.ds(krow0, nk), :])
                for j, (_, _, bi, _) in enumerate(places):
                    for c in range(0, 2 * SUB, SOFTMAX_ROWS):
                        hh, r = divmod(c, SUB)
                        s = s_buf[j, c:c + SOFTMAX_ROWS, 0:nk] + bias_ref[g, bi, hh, r:r + SOFTMAX_ROWS, 0:nk]
                        m = jnp.max(s, axis=-1, keepdims=True)
                        e = jnp.exp(s - m)
                        den = jnp.sum(e, axis=-1, keepdims=True)
                        p_buf[j, c:c + SOFTMAX_ROWS, 0:nk] = (e * (1.0 / den)).astype(BF16)
                        l_buf[j, c:c + SOFTMAX_ROWS, :] = jnp.broadcast_to(m + jnp.log(den), (SOFTMAX_ROWS, SLAB))
                for j, (_, krow0, _, nat) in enumerate(places):
                    o2 = jnp.dot(p_buf[j, :, 0:nk], sub[2, pl.ds(krow0, nk), :], preferred_element_type=F32)
                    _st_rows(o_ref.at[g], nat, SUB, dil, jnp.where(head0, o2[0:SUB], o2[SUB:2 * SUB]))
                    _st_rows(lse_ref.at[g], nat, SUB, dil, jnp.where(head0, l_buf[j, 0:SUB, :], l_buf[j, SUB:2 * SUB, :]))
                return carry

            lax.fori_loop(0, n_blocks // BLOCKS_PER_TRIP, trip, 0)

        def mix(i, carry):
            rows = pl.ds(pl.multiple_of(i * chunk, chunk), chunk)
            l0, l1, l2 = lse_ref[0, rows, :], lse_ref[1, rows, :], lse_ref[2, rows, :]
            m = jnp.maximum(jnp.maximum(l0, l1), l2)
            e0, e1, e2 = jnp.exp(l0 - m), jnp.exp(l1 - m), jnp.exp(l2 - m)
            tot = e0 + e1 + e2
            o = (e0 / tot) * o_ref[0, rows, :] + (e1 / tot) * o_ref[1, rows, :] + (e2 / tot) * o_ref[2, rows, :]
            z = pa_ref[rows, 9 * SLAB:10 * SLAB].astype(F32)
            a_ref[rows, :] = (o * (z * _sigmoid(z))).astype(BF16)
            return carry

        lax.fori_loop(0, seq // chunk, mix, 0)

    big = jax.ShapeDtypeStruct((N_GROUPS, t, 2 * SLAB), F32)
    return pl.pallas_call(
        body, name="attn_fwd", grid=(bsz, 2),
        in_specs=[pl.BlockSpec((seq, PAIR_COLS), lambda b, p: (b, p))],
        out_specs=[pl.BlockSpec((N_GROUPS, seq, SLAB), lambda b, p: (0, b, p)),
                   pl.BlockSpec((N_GROUPS, seq, SLAB), lambda b, p: (0, b, p)),
                   pl.BlockSpec((seq, SLAB), lambda b, p: (b, p))],
        out_shape=[big, big, jax.ShapeDtypeStruct((t, 2 * SLAB), BF16)],
        scratch_shapes=[pltpu.VMEM((3, seq, SLAB), BF16), pltpu.VMEM((seq, SLAB), F32),
                        pltpu.VMEM((N_GROUPS, 2, 2, SUB, 2 * SUB), F32), pltpu.VMEM((BLOCKS_PER_TRIP, 2 * SUB, 2 * SUB), F32),
                        pltpu.VMEM((BLOCKS_PER_TRIP, 2 * SUB, 2 * SUB), BF16), pltpu.VMEM((BLOCKS_PER_TRIP, 2 * SUB, SLAB), F32)],
        compiler_params=_params("arbitrary", "arbitrary"))(pa)


def _shift_down(v, k, rows):
    return jnp.where(rows >= k, pltpu.roll(v, k, 0), 0.0)


def _shift_up(v, k, rows):
    n = v.shape[0]
    return jnp.where(rows < n - k, pltpu.roll(v, n - k, 0), 0.0)


def _conv_fwd(pr, conv_w, bsz, seq, d):
    t = pr.shape[0]
    ct = CONV_TILE

    def body(p_ref, cw_ref, o_ref):
        u = p_ref[:, 2 * ct:3 * ct].astype(F32) * p_ref[:, 0:ct].astype(F32)
        cw = cw_ref[...]
        rows = lax.broadcasted_iota(jnp.int32, u.shape, 0)
        conv = cw[0:1, :] * _shift_down(u, 2, rows)
        conv = conv + cw[1:2, :] * _shift_down(u, 1, rows)
        conv = conv + cw[2:3, :] * u
        z = p_ref[:, 3 * ct:4 * ct].astype(F32)
        o_ref[...] = (p_ref[:, ct:2 * ct].astype(F32) * conv * (z * _sigmoid(z))).astype(BF16)

    return pl.pallas_call(
        body, name="conv_fwd", grid=(bsz, d // ct),
        in_specs=[pl.BlockSpec((seq, 4 * ct), lambda b, j: (b, j)), pl.BlockSpec((3, ct), lambda b, j: (0, j))],
        out_specs=pl.BlockSpec((seq, ct), lambda b, j: (b, j)),
        out_shape=jax.ShapeDtypeStruct((t, d), BF16), compiler_params=_params("parallel", "parallel"))(pr, conv_w)


def _tail(a_in, b_in, pr, x2, target2, mod3, w_pa, w_pc, w_out, b_out, ln_g, ln_b, seq, lay):
    t, d = x2.shape
    tm = 512
    per_seq = seq // tm
    n_steps = t // tm
    gate_blk = 4 * d // d

    def nt(a, b):
        return lax.dot_general(a, b, (((1,), (1,)), ((), ())), preferred_element_type=F32)

    def tn(a, b):
        return lax.dot_general(a, b, (((0,), (0,)), ((), ())), preferred_element_type=F32)

    def body(a_ref, b_ref, ga_ref, gb_ref, x_ref, tg_ref, mod_ref, wpa_ref, wpc_ref, wo_ref, bo_ref, lg_ref, lb_ref,
             dpg_ref, da_ref, db_ref, gx_ref, dgate_ref, small_ref, gwpa_hbm, gwpc_hbm, gwo_hbm,
             acc_pa, acc_pc, acc_o, sem):
        i = pl.program_id(0)

        @pl.when(i == 0)
        def _():
            acc_pa[...] = jnp.zeros_like(acc_pa)
            acc_pc[...] = jnp.zeros_like(acc_pc)
            acc_o[...] = jnp.zeros_like(acc_o)
            small_ref[...] = jnp.zeros_like(small_ref)

        @pl.when(i % per_seq == 0)
        def _():
            dgate_ref[...] = jnp.zeros_like(dgate_ref)

        halves = [slice(k * (tm // 2), (k + 1) * (tm // 2)) for k in range(2)]
        gate = mod_ref[:, 2 * d:3 * d]
        a_bf = [a_ref[rs, :] for rs in halves]
        b_bf = [b_ref[rs, :] for rs in halves]
        y_attn = [jnp.dot(a, wpa_ref[...], preferred_element_type=F32) for a in a_bf]
        y_conv = [jnp.dot(b, wpc_ref[...], preferred_element_type=F32) for b in b_bf]
        sa = [_sigmoid(ga_ref[rs, :].astype(F32)) for rs in halves]
        sb = [_sigmoid(gb_ref[rs, :].astype(F32)) for rs in halves]
        merged = [(sa[k] * y_attn[k] + sb[k] * y_conv[k]).astype(BF16) for k in range(2)]
        mo = [jnp.dot(m, wo_ref[...], preferred_element_type=F32) + bo_ref[...] for m in merged]
        d_mo_bf = []
        for k, rs in enumerate(halves):
            r = ALPHA * x_ref[rs, :] + gate * mo[k]
            mu = jnp.mean(r, axis=-1, keepdims=True)
            cen = r - mu
            var = jnp.mean(cen * cen, axis=-1, keepdims=True)
            rstd = lax.rsqrt(var + LN_EPS)
            xhat = cen * rstd
            err = xhat * lg_ref[...] + lb_ref[...] - tg_ref[rs, :]
            dy = err * (1.0 / d)
            dxhat = dy * lg_ref[...]
            dr = rstd * (dxhat - jnp.mean(dxhat, axis=-1, keepdims=True)
                         - xhat * jnp.mean(dxhat * xhat, axis=-1, keepdims=True))
            gx_ref[rs, :] = ALPHA * dr
            dgate_ref[...] += jnp.sum(dr * mo[k], axis=0, keepdims=True)
            d_mo = dr * gate
            small_ref[0:1, :] += jnp.sum(d_mo, axis=0, keepdims=True)
            small_ref[1:2, :] += jnp.sum(dy * xhat, axis=0, keepdims=True)
            small_ref[2:3, :] += jnp.sum(dy, axis=0, keepdims=True)
            small_ref[3:4, :] += jnp.sum(err * err, axis=0, keepdims=True)
            d_mo_bf.append(d_mo.astype(BF16))
        acc_o[...] += tn(jnp.concatenate(merged, axis=0), jnp.concatenate(d_mo_bf, axis=0))
        dmerged = [nt(g, wo_ref[...]) for g in d_mo_bf]
        dy_attn, dy_conv = [], []
        for k, rs in enumerate(halves):
            dy_attn.append((dmerged[k] * sa[k]).astype(BF16))
            dy_conv.append((dmerged[k] * sb[k]).astype(BF16))
            dpg_ref[rs, 0:d] = (dmerged[k] * y_attn[k] * sa[k] * (1.0 - sa[k])).astype(BF16)
            dpg_ref[rs, d:2 * d] = (dmerged[k] * y_conv[k] * sb[k] * (1.0 - sb[k])).astype(BF16)
        acc_pa[...] += tn(a_ref[...], jnp.concatenate(dy_attn, axis=0))
        acc_pc[...] += tn(b_ref[...], jnp.concatenate(dy_conv, axis=0))
        for k, rs in enumerate(halves):
            da_ref[rs, :] = nt(dy_attn[k], wpa_ref[...])
            db_ref[rs, :] = nt(dy_conv[k], wpc_ref[...])

        @pl.when(i == n_steps - 1)
        def _():
            copies = [pltpu.make_async_copy(acc_pa, gwpa_hbm, sem.at[0]), pltpu.make_async_copy(acc_pc, gwpc_hbm, sem.at[1]),
                      pltpu.make_async_copy(acc_o, gwo_hbm, sem.at[2])]
            for cp in copies:
                cp.start()
            for cp in copies:
                cp.wait()

    row = lambda w: pl.BlockSpec((tm, w), lambda i: (i, 0))
    const = lambda shp: pl.BlockSpec(shp, lambda i: (0,) * len(shp), pipeline_mode=pl.Buffered(1))
    any_spec = pl.BlockSpec(memory_space=pl.ANY)
    return pl.pallas_call(
        body, name="tail", grid=(n_steps,),
        in_specs=[row(Z_WIDTH), row(d),
                  pl.BlockSpec((tm, d), lambda i: (i, gate_blk)), pl.BlockSpec((tm, d), lambda i: (i, gate_blk + 1)),
                  row(d), row(d), pl.BlockSpec((None, 1, 3 * d), lambda i: (i // per_seq, 0, 0)),
                  const((Z_WIDTH, d)), const((d, d)), const((d, d)), const((1, d)), const((1, d)), const((1, d))],
        out_specs=[pl.BlockSpec((tm, 2 * d), lambda i: (i, lay.g0 // (2 * d))), row(Z_WIDTH), row(d), row(d),
                   pl.BlockSpec((None, 1, d), lambda i: (i // per_seq, 0, 0)), pl.BlockSpec((8, d), lambda i: (0, 0)),
                   HBM_SPEC, HBM_SPEC, HBM_SPEC],
        out_shape=[jax.ShapeDtypeStruct((t, lay.np), BF16), jax.ShapeDtypeStruct((t, Z_WIDTH), F32),
                   jax.ShapeDtypeStruct((t, d), F32), jax.ShapeDtypeStruct((t, d), F32),
                   jax.ShapeDtypeStruct((t // seq, 1, d), F32), jax.ShapeDtypeStruct((8, d), F32),
                   pltpu.HBM((Z_WIDTH, d), F32), pltpu.HBM((d, d), F32), pltpu.HBM((d, d), F32)],
        scratch_shapes=[pltpu.VMEM((Z_WIDTH, d), F32), pltpu.VMEM((d, d), F32), pltpu.VMEM((d, d), F32),
                        pltpu.SemaphoreType.DMA((3,))],
        compiler_params=_params("arbitrary"),
    )(a_in, b_in, pr, pr, x2, target2, mod3, w_pa, w_pc, w_out, b_out, ln_g, ln_b)


def _conv_bwd(dproj, db, pr, conv_w, bsz, seq, lay):
    d = lay.d
    ct = CONV_TILE
    base = lay.c0 // (4 * ct)

    def body(dp_in, db_ref, p_ref, cw_ref, dp_ref, gcw_ref):
        del dp_in
        u_x, g_b, g_c, z = [p_ref[:, k * ct:(k + 1) * ct].astype(F32) for k in range(4)]
        cw = cw_ref[...]
        u = g_c * u_x
        rows = lax.broadcasted_iota(jnp.int32, u.shape, 0)
        u1, u2 = _shift_down(u, 1, rows), _shift_down(u, 2, rows)
        conv = cw[0:1, :] * u2 + cw[1:2, :] * u1 + cw[2:3, :] * u
        sig = _sigmoid(z)
        sl = z * sig
        dbv = db_ref[...]
        gbc = g_b * conv
        dp_ref[:, ct:2 * ct] = (dbv * sl * conv).astype(BF16)
        dp_ref[:, 3 * ct:4 * ct] = (dbv * gbc * (sig * (1.0 + z * (1.0 - sig)))).astype(BF16)
        dconv = dbv * sl * g_b

        @pl.when(pl.program_id(1) == 0)
        def _():
            gcw_ref[...] = jnp.zeros_like(gcw_ref)

        gcw_ref[0:1, :] += jnp.sum(dconv * u2, axis=0, keepdims=True)
        gcw_ref[1:2, :] += jnp.sum(dconv * u1, axis=0, keepdims=True)
        gcw_ref[2:3, :] += jnp.sum(dconv * u, axis=0, keepdims=True)
        du = cw[2:3, :] * dconv + cw[1:2, :] * _shift_up(dconv, 1, rows) + cw[0:1, :] * _shift_up(dconv, 2, rows)
        dp_ref[:, 0:ct] = (du * g_c).astype(BF16)
        dp_ref[:, 2 * ct:3 * ct] = (du * u_x).astype(BF16)

    return pl.pallas_call(
        body, name="conv_bwd", grid=(d // ct, bsz),
        in_specs=[pl.BlockSpec(memory_space=pl.ANY), pl.BlockSpec((seq, ct), lambda j, b: (b, j)),
                  pl.BlockSpec((seq, 4 * ct), lambda j, b: (b, j)), pl.BlockSpec((3, ct), lambda j, b: (0, j))],
        out_specs=[pl.BlockSpec((seq, 4 * ct), lambda j, b: (b, base + j)), pl.BlockSpec((8, ct), lambda j, b: (0, j))],
        out_shape=[jax.ShapeDtypeStruct(dproj.shape, BF16), jax.ShapeDtypeStruct((8, d), F32)],
        input_output_aliases={0: 0}, compiler_params=_params("arbitrary", "arbitrary"))(dproj, db, pr, conv_w)


def _attn_bwd(dproj, pa, o_all, lse_all, da, bsz, seq, after):
    n_blocks = seq // SUB
    chunk = 256

    def body(dp_in, pa_ref, o_ref, lse_ref, da_ref, after_ref, dp_ref, sub, stage, dsub, dog, cvec, bias_ref,
             s_buf, dp_buf, ds_buf, pb_buf, q2_buf, do2_buf, l_buf, c_buf):
        del dp_in, after_ref
        p = pl.program_id(1)
        _fill_bias(bias_ref, p, seq)
        head0 = lax.broadcasted_iota(jnp.int32, (SUB, SLAB), 1) < HEAD_DIM

        def mix_bwd(i, carry):
            rows = pl.ds(pl.multiple_of(i * chunk, chunk), chunk)
            ls = [lse_ref[g, rows, :] for g in range(N_GROUPS)]
            os_ = [o_ref[g, rows, :] for g in range(N_GROUPS)]
            m = jnp.maximum(jnp.maximum(ls[0], ls[1]), ls[2])
            es = [jnp.exp(l - m) for l in ls]
            tot = es[0] + es[1] + es[2]
            ws = [e / tot for e in es]
            o = ws[0] * os_[0] + ws[1] * os_[1] + ws[2] * os_[2]
            z = pa_ref[rows, 9 * SLAB:10 * SLAB].astype(F32)
            sig = _sigmoid(z)
            dav = da_ref[rows, :]
            do = dav * (z * sig)
            dp_ref[rows, 9 * SLAB:10 * SLAB] = (dav * o * (sig * (1.0 + z * (1.0 - sig)))).astype(BF16)
            wsum = _head_sums(do * o)
            for g in range(N_GROUPS):
                dog[g, rows, :] = ws[g] * do
                cvec[g, rows, :] = -(ws[g] * wsum)
            return carry

        lax.fori_loop(0, seq // chunk, mix_bwd, 0)

        for g in range(N_GROUPS):
            dil = DILATIONS[g]
            for w in range(3):
                _to_sub_major(pa_ref, 3 * w + g, sub.at[w], stage, dil, seq)
            dsub[1] = jnp.zeros((seq, SLAB), F32)
            dsub[2] = jnp.zeros((seq, SLAB), F32)
            nk = _key_rows(g, seq)

            def trip(i, carry, g=g, dil=dil, nk=nk):
                places = [_block_rows(BLOCKS_PER_TRIP * i + j, g, seq) for j in range(BLOCKS_PER_TRIP)]
                for j, (row0, krow0, _, nat) in enumerate(places):
                    q = sub[0, pl.ds(row0, SUB), :]
                    do = _ld_rows(dog.at[g], nat, SUB, dil).astype(BF16)
                    zero = jnp.zeros_like(q)
                    q2 = jnp.concatenate([jnp.where(head0, q, zero), jnp.where(head0, zero, q)], axis=0)
                    do2 = jnp.concatenate([jnp.where(head0, do, zero), jnp.where(head0, zero, do)], axis=0)
                    q2_buf[j] = q2
                    do2_buf[j] = do2
                    s_buf[j, :, 0:nk] = _nt(q2 * (HEAD_DIM ** -0.5), sub[1, pl.ds(krow0, nk), :])
                    dp_buf[j, :, 0:nk] = _nt(do2, sub[2, pl.ds(krow0, nk), :])
                    l_buf[j] = _ld_rows(lse_ref.at[g], nat, SUB, dil)
                    c_buf[j] = _ld_rows(cvec.at[g], nat, SUB, dil)
                for j, (_, _, bi, _) in enumerate(places):
                    for c in range(0, 2 * SUB, SOFTMAX_ROWS):
                        hh, r = divmod(c, SUB)
                        lane = hh * HEAD_DIM
                        s = s_buf[j, c:c + SOFTMAX_ROWS, 0:nk] + bias_ref[g, bi, hh, r:r + SOFTMAX_ROWS, 0:nk]
                        prob = jnp.exp(s - l_buf[j, r:r + SOFTMAX_ROWS, lane:lane + 1])
                        dprob = dp_buf[j, c:c + SOFTMAX_ROWS, 0:nk] + c_buf[j, r:r + SOFTMAX_ROWS, lane:lane + 1]
                        ds_buf[j, c:c + SOFTMAX_ROWS, 0:nk] = (prob * dprob * (HEAD_DIM ** -0.5)).astype(BF16)
                        pb_buf[j, c:c + SOFTMAX_ROWS, 0:nk] = prob.astype(BF16)
                for j, (row0, krow0, _, _) in enumerate(places):
                    ds = ds_buf[j, :, 0:nk]
                    dq2 = jnp.dot(ds, sub[1, pl.ds(krow0, nk), :], preferred_element_type=F32)
                    dsub[0, pl.ds(row0, SUB), :] = jnp.where(head0, dq2[0:SUB], dq2[SUB:2 * SUB])
                    dsub[1, pl.ds(krow0, nk), :] += _tn(ds, q2_buf[j])
                    dsub[2, pl.ds(krow0, nk), :] += _tn(pb_buf[j, :, 0:nk], do2_buf[j])
                return carry

            lax.fori_loop(0, n_blocks // BLOCKS_PER_TRIP, trip, 0)
            for w in range(3):
                cols = slice((3 * w + g) * SLAB, (3 * w + g + 1) * SLAB)
                if dil == 1:
                    dp_ref[:, cols] = dsub[w].astype(BF16)
                else:
                    n = seq // dil
                    for r in range(dil):
                        stage[pl.ds(r, n, stride=dil), :] = dsub[w, pl.ds(r * n, n), :]
                    dp_ref[:, cols] = stage[...].astype(BF16)

    return pl.pallas_call(
        body, name="attn_bwd", grid=(bsz, 2),
        in_specs=[pl.BlockSpec(memory_space=pl.ANY), pl.BlockSpec((seq, PAIR_COLS), lambda b, p: (b, p)),
                  pl.BlockSpec((N_GROUPS, seq, SLAB), lambda b, p: (0, b, p)),
                  pl.BlockSpec((N_GROUPS, seq, SLAB), lambda b, p: (0, b, p)),
                  pl.BlockSpec((seq, SLAB), lambda b, p: (b, p)), pl.BlockSpec(memory_space=pl.ANY)],
        out_specs=pl.BlockSpec((seq, PAIR_COLS), lambda b, p: (b, p)),
        out_shape=jax.ShapeDtypeStruct(dproj.shape, BF16), input_output_aliases={0: 0},
        scratch_shapes=[pltpu.VMEM((3, seq, SLAB), BF16), pltpu.VMEM((seq, SLAB), F32), pltpu.VMEM((3, seq, SLAB), F32),
                        pltpu.VMEM((3, seq, SLAB), F32), pltpu.VMEM((3, seq, SLAB), F32),
                        pltpu.VMEM((N_GROUPS, 2, 2, SUB, 2 * SUB), F32),
                        pltpu.VMEM((BLOCKS_PER_TRIP, 2 * SUB, 2 * SUB), F32), pltpu.VMEM((BLOCKS_PER_TRIP, 2 * SUB, 2 * SUB), F32),
                        pltpu.VMEM((BLOCKS_PER_TRIP, 2 * SUB, 2 * SUB), BF16), pltpu.VMEM((BLOCKS_PER_TRIP, 2 * SUB, 2 * SUB), BF16),
                        pltpu.VMEM((BLOCKS_PER_TRIP, 2 * SUB, SLAB), BF16), pltpu.VMEM((BLOCKS_PER_TRIP, 2 * SUB, SLAB), BF16),
                        pltpu.VMEM((BLOCKS_PER_TRIP, SUB, SLAB), F32), pltpu.VMEM((BLOCKS_PER_TRIP, SUB, SLAB), F32)],
        compiler_params=_params("arbitrary", "arbitrary"))(dproj, pa, o_all, lse_all, da, after)


def _grad_h(dproj, w_all, gx0, x2, mod3, seq, lay):
    t, d = x2.shape
    tm, tn = min(512, seq), min(512, d)
    per_seq = seq // tm

    def body(dp_ref, w_ref, gx0_ref, x_ref, scale_ref, gx_ref, dmod_ref):
        dh = _nt(dp_ref[:, 0:ATT], w_ref[:, 0:ATT]) + _nt(dp_ref[:, lay.c0:], w_ref[:, lay.c0:])
        gx_ref[...] = gx0_ref[...] + dh * (1.0 + scale_ref[...])

        @pl.when(pl.program_id(1) % per_seq == 0)
        def _():
            dmod_ref[...] = jnp.zeros_like(dmod_ref)

        dmod_ref[0:1, :] += jnp.sum(dh, axis=0, keepdims=True)
        dmod_ref[1:2, :] += jnp.sum(dh * x_ref[...], axis=0, keepdims=True)

    tile = pl.BlockSpec((tm, tn), lambda j, i: (i, j))
    return pl.pallas_call(
        body, name="grad_h", grid=(d // tn, t // tm),
        in_specs=[pl.BlockSpec((tm, lay.np), lambda j, i: (i, 0)), pl.BlockSpec((tn, lay.np), lambda j, i: (j, 0)),
                  tile, tile,
                  pl.BlockSpec((None, 1, tn), lambda j, i: (i // per_seq, 0, d // tn + j))],
        out_specs=[tile, pl.BlockSpec((None, 8, tn), lambda j, i: (i // per_seq, 0, j))],
        out_shape=[jax.ShapeDtypeStruct((t, d), F32), jax.ShapeDtypeStruct((t // seq, 8, d), F32)],
        compiler_params=_params("arbitrary", "arbitrary"))(dproj, w_all, gx0, x2, mod3)


def _grad_w_in(ht, dproj, seq, lay, part, prev):
    d, t = ht.shape
    tm = seq
    n_i = t // tm

    def make_body(n_skip, n_pieces, tn, nat_tile):
        def body(*refs):
            refs = refs[n_skip:]
            ht_ref, dp_refs = refs[0], refs[1:1 + n_pieces]
            gw_hbm, gb_hbm, acc, bacc, gw_out, gb_out, sem = refs[1 + n_pieces:]
            i, j = pl.program_id(0), pl.program_id(1)
            dp = dp_refs[0][...] if n_pieces == 1 else jnp.concatenate([r[...] for r in dp_refs], axis=1)
            part = jnp.dot(ht_ref[...], dp, preferred_element_type=F32)
            bpart = jnp.sum(dp.astype(F32), axis=0, keepdims=True)

            if n_i > 1:
                @pl.when(i == 0)
                def _():
                    acc[j] = part
                    bacc[j] = bpart

                @pl.when((i > 0) & (i < n_i - 1))
                def _():
                    acc[j] += part
                    bacc[j] += bpart

            @pl.when(i == n_i - 1)
            def _():
                gw_out[...] = ((part + acc[j]) if n_i > 1 else part).astype(BF16)
                gb_out[...] = (bpart + bacc[j]) if n_i > 1 else bpart
                cols = pl.ds(pl.multiple_of(nat_tile(j) * tn, SLAB), tn)
                copies = [pltpu.make_async_copy(gw_out, gw_hbm.at[:, cols], sem.at[0]),
                          pltpu.make_async_copy(gb_out, gb_hbm.at[:, cols], sem.at[1])]
                for cp in copies:
                    cp.start()
                for cp in copies:
                    cp.wait()
        return body

    def call(name, pieces, n_tiles, nat_tile, prev):
        tn = sum(w for w, _ in pieces)
        any_spec = pl.BlockSpec(memory_space=pl.ANY)
        in_specs = [pl.BlockSpec((d, tm), lambda i, j: (0, i))]
        in_specs += [pl.BlockSpec((tm, w), lambda i, j, f=f: (i, f(j))) for w, f in pieces]
        args = [ht] + [dproj] * len(pieces)
        aliases = {}
        if prev is not None:
            in_specs = [any_spec] * 2 + in_specs
            args = list(prev) + args
            aliases = {0: 0, 1: 1}
        return pl.pallas_call(
            make_body(0 if prev is None else 2, len(pieces), tn, nat_tile), name=name, grid=(n_i, n_tiles), in_specs=in_specs,
            out_specs=[any_spec, any_spec],
            out_shape=[jax.ShapeDtypeStruct((d, lay.din), BF16), jax.ShapeDtypeStruct((1, lay.din), F32)],
            input_output_aliases=aliases,
            scratch_shapes=[pltpu.VMEM((n_tiles, d, tn), F32), pltpu.VMEM((n_tiles, 1, tn), F32), pltpu.VMEM((d, tn), BF16),
                            pltpu.VMEM((1, tn), F32), pltpu.SemaphoreType.DMA((2,))],
            compiler_params=_params("arbitrary", "arbitrary"))(*args)

    if part == "attn":
        attn_pieces = [(SLAB, lambda j, m=m: (m % 2) * PAIR_SLABS + 2 * j + m // 2) for m in range(4)]
        return call("grad_w_in_attn", attn_pieces, ATT // 512, lambda j: j, prev)
    base = lay.c0 // CONV_TILE
    nct = lay.n_conv_tiles
    if nct % 2:
        return call("grad_w_in_rest", [(CONV_TILE, lambda j: base + j)], 6 * d // CONV_TILE, lay.rest_nat_tile, prev)
    half = nct // 2

    def rest_piece(m):
        def perm_tile(j):
            conv = base + 4 * (2 * (j % half) + m) + j // half
            return jnp.where(j < 4 * half, conv, base + 2 * j + m)
        return (CONV_TILE, perm_tile)

    return call("grad_w_in_rest", [rest_piece(0), rest_piece(1)], 6 * d // 512, lambda j: ATT // 512 + j, prev)


def _pack_rows(parts, width=128):
    flat = [p.reshape(-1) for p in parts]
    spans, rows = [], 0
    padded = []
    for f in flat:
        n = -(-f.shape[0] // (8 * width)) * 8
        padded.append(jnp.pad(f, (0, n * width - f.shape[0])).reshape(n, width))
        spans.append((rows, f.shape[0]))
        rows += n
    return jnp.concatenate(padded, axis=0), spans


def _unpack_rows(packed, spans, shapes, width=128):
    out = []
    for (row, n), shp in zip(spans, shapes):
        rows = -(-n // width)
        out.append(packed[row:row + rows].reshape(-1)[:n].reshape(shp))
    return out


def kernel(x, c, w_ada, b_ada, w_in, b_in, conv_w, w_proj_attn, w_proj_conv, w_out, b_out, ln_g, ln_b, loss_target, m_w_ada, m_b_ada, m_w_in, m_b_in, m_conv_w, m_w_proj_attn, m_w_proj_conv, m_w_out, m_b_out, m_ln_g, m_ln_b, v_w_ada, v_b_ada, v_w_in, v_b_in, v_conv_w, v_w_proj_attn, v_w_proj_conv, v_w_out, v_b_out, v_ln_g, v_ln_b):
    bsz, seq, d = x.shape
    t = bsz * seq
    lay = _Layout(d)
    col_sharded = [True, True, False, False]
    red_w = [w_in[0], w_proj_attn[0], w_proj_conv[0], w_out[0]]
    chip = 2 * lax.axis_index("x") + lax.axis_index("y")
    chip1 = chip.astype(jnp.int32).reshape(1)
    core1 = lax.axis_index("c").astype(jnp.int32).reshape(1)
    place = jnp.stack([chip, lax.axis_index("c")]).astype(jnp.int32)
    x2 = x.reshape(t, d)
    target2 = loss_target.reshape(t, d)

    mod, act_all = _ada_exchange(c, w_ada[0], b_ada)
    mod3 = mod.reshape(bsz, 1, 3 * d)

    cw_pad = jnp.pad(conv_w[0], ((0, 5), (0, 0))) + 0.0 * mod[0, 0]
    own_in_full = [_cast_into_full(red_w[0], col_sharded[0], chip1, "cast_shard_in")]
    own_in_full += list(_cast_into_full_small(red_w[1:], col_sharded[1:], chip1, "cast_shards_late"))
    (wi_f,), cw8 = _gather_weights(own_in_full[:1], col_sharded[:1], cw_pad)
    cw_full = cw8[0:3]
    late_copies = _direct_gather_copies(col_sharded[1:])
    late_send, late_recv, late_flying, late_token = _start_copies("gather_late_start", own_in_full[1:], (18,), cw8, late_copies)
    w_all = _permute_w_in(wi_f, lay)
    b_all = lay.perm_vector(b_in) + late_token[0, 0]

    rest_tn = 1024 if (6 * d) % 1024 == 0 else 512
    pa, = _project(x2, mod3, w_all, b_all, seq, 0, ATT, PAIR_COLS, BF16, False, "project_attn")
    pr, ht = _project(x2, mod3, w_all, b_all, seq, lay.c0, 6 * d, rest_tn, BF16, True, "project_rest")
    o_all, lse_all, a_in = _attn_fwd(pa, bsz, seq)
    b_in_act = _conv_fwd(pr, cw_full, bsz, seq, d)
    wpa_f, wpc_f, wo_f = _wait_copies("gather_late_wait", late_flying, late_send, late_recv, b_in_act, late_copies)
    (dproj, da_in, db_in, gx0, dgate, small_tail, gw_pa, gw_pc, gw_out) = _tail(
        a_in, b_in_act, pr, x2, target2, mod3, wpa_f, wpc_f, wo_f, b_out, ln_g, ln_b, seq, lay)

    late_views = _shard_views([gw_pa, gw_pc, gw_out], col_sharded[1:])
    late_lands = [lax.empty((v.shape[0], v.shape[1] // 2, v.shape[2]), v.dtype) for v in late_views]
    xl_send, xl_recv, xl_fly, xl_tok = _start_copies("grads_pair_exchange_late_start", late_views + late_lands, (3,), small_tail,
                                                     _pair_exchange_copies(3, False))
    dproj, gcw = _conv_bwd(dproj, db_in, pr, cw_full + xl_tok[0, 0], bsz, seq, lay)
    xl_done = _wait_copies("grads_pair_exchange_late_wait", xl_fly, xl_send, xl_recv, gcw, _pair_exchange_copies(3, True))
    late_parts = list(_pair_sum_small(xl_done[:3], xl_done[3:], core1, "grads_pair_sum_late"))

    late_cross = _chip_scatter_copies(3, col_sharded[1:])
    late_zone = [lax.empty((3, p.shape[1], _piece_cols(p, cs)), p.dtype) for p, cs in zip(late_parts, col_sharded[1:])]
    sl_send, sl_recv, sl_fly, sl_tok = _start_copies("grads_scatter_late_start", late_parts + late_zone, (9,), core1,
                                                     late_cross)
    dproj = _attn_bwd(dproj, pa, o_all, lse_all, da_in, bsz, seq, sl_tok)
    gw_in_bf, gb_in = _grad_w_in(ht, dproj, seq, lay, "rest", None)
    rest_cols, attn_cols = (ATT, 6 * d), (0, ATT)
    in_land = lax.empty((1, d // 2, lay.din), BF16)
    xi_copies = _pair_exchange_copies(1, False, rest_cols)
    xi_send, xi_recv, xi_fly, _ = _start_copies("grads_pair_exchange_in_start", [gw_in_bf.reshape(1, d, lay.din), in_land], (1,),
                                                gb_in, xi_copies)
    gw_in_bf, gb_in = _grad_w_in(ht, dproj, seq, lay, "attn", (xi_fly[0].reshape(d, lay.din), gb_in))
    xi_done = _wait_copies("grads_pair_exchange_in_wait", [gw_in_bf.reshape(1, d, lay.din), xi_fly[1]], xi_send, xi_recv, gb_in,
                           _pair_exchange_copies(1, True, rest_cols))
    in_got = _pair_exchange_into(xi_done[0], xi_done[1], attn_cols, "grads_pair_exchange_in_attn")
    sl_done = _wait_copies("grads_scatter_late_wait", sl_fly, sl_send, sl_recv, in_got, late_cross)
    late_red = list(_chip_sum_small(sl_done[:3], sl_done[3:], col_sharded[1:], place, "grads_chip_sum_late"))

    in_part = _pair_sum(xi_done[0], in_got, core1, "grads_pair_sum_in")
    in_cross = _chip_scatter_copies(1, col_sharded[:1])
    in_zone = [lax.empty((3, in_part.shape[1], _piece_cols(in_part, True)), in_part.dtype)]
    late_views1 = [f.reshape(1, *f.shape) for f in late_red]
    si_send, si_recv, si_fly, si_tok = _start_copies(
        "grads_scatter_in_join_late_start", [in_part] + in_zone + late_views1, (6,), core1,
        _both_copies(in_cross, 2, _pair_join_copies(3, False, sem0=3)))
    grad_x2, dmod = _grad_h(dproj, w_all, gx0, x2, mod3 + si_tok[0, 0], seq, lay)
    si_done = _wait_copies("grads_scatter_in_join_late_wait", si_fly, si_send, si_recv, grad_x2,
                           _both_copies(in_cross, 2, _pair_join_copies(3, True, sem0=3)))
    late_joined = [f[0] for f in si_done[2:]]
    in_red = _chip_sum(si_done[0], si_done[1], True, place, "grads_chip_sum_in")

    d_ada = jnp.concatenate([dmod[:, 0, :], dmod[:, 1, :], dgate[:, 0, :]], axis=1)
    pieces = [small_tail[3], jnp.sum(d_ada, axis=0), gb_in[0], small_tail[0], small_tail[1], small_tail[2], gcw[0:3]]
    packed, spans = _pack_rows(pieces)
    kept_in, _ = _pack_rows([d_ada])
    rows_all = jnp.concatenate([packed, kept_in], axis=0)
    small_land = lax.empty((N_DEV,) + rows_all.shape, F32)
    sm_send, sm_recv, sm_fly, sm_tok = _start_copies(
        "small_gather_join_in_start", [rows_all, small_land, in_red.reshape(1, *in_red.shape)], (N_DEV,), core1,
        _both_copies(_small_gather_copies, 2, _pair_join_copies(1, False, sem0=N_DEV - 1)))
    big_w = [w_ada[0]] + red_w
    big_m = [m_w_ada[0], m_w_in[0], m_w_proj_attn[0], m_w_proj_conv[0], m_w_out[0]]
    big_v = [v_w_ada[0], v_w_in[0], v_w_proj_attn[0], v_w_proj_conv[0], v_w_out[0]]
    big_out = [None] * 5
    for w in range(3):
        big_out[2 + w] = _adamw(big_w[2 + w], late_joined[w], big_m[2 + w], big_v[2 + w], f"adamw_{2 + w}", sm_tok)
    sm_done = _wait_copies("small_gather_join_in_wait", sm_fly, sm_send, sm_recv, big_out[4][0],
                           _both_copies(_small_gather_copies, 2, _pair_join_copies(1, True, sem0=N_DEV - 1)))
    me1 = (4 * lax.axis_index("x") + 2 * lax.axis_index("y") + lax.axis_index("c")).astype(jnp.int32).reshape(1)
    summed, kept = _small_sum(sm_done[0], sm_done[1], me1, packed.shape[0], d)
    loss = summed[0, 0]
    _, g_b_ada, g_b_in, g_b_out, g_ln_g, g_ln_b, g_cw_full = _unpack_rows(
        summed, spans, [(d,), (3 * d,), (lay.din,), (d,), (d,), (d,), (3, d)])
    g_cw = lax.dynamic_slice(g_cw_full, (0, chip * (d // N_CHIPS)), (3, d // N_CHIPS))
    d_ada_all = kept.reshape(N_DEV, -1)[:, :bsz * 3 * d].reshape(N_DEV * bsz, 3 * d)
    ada_cols = 3 * d // N_CHIPS
    d_ada_cols = lax.dynamic_slice(d_ada_all, (0, chip * ada_cols), (N_DEV * bsz, ada_cols))

    g_w_ada, *big_out[0] = _grad_and_adamw_w_ada(act_all.T, d_ada_cols, big_w[0], big_m[0], big_v[0])
    small_w = [b_ada, b_in, conv_w[0], b_out, ln_g, ln_b]
    small_g = [g_b_ada, g_b_in, g_cw, g_b_out, g_ln_g, g_ln_b]
    small_m = [m_b_ada, m_b_in, m_conv_w[0], m_b_out, m_ln_g, m_ln_b]
    small_v = [v_b_ada, v_b_in, v_conv_w[0], v_b_out, v_ln_g, v_ln_b]
    pw, sp = _pack_rows(small_w)
    pg, _ = _pack_rows(small_g)
    pm, _ = _pack_rows(small_m)
    pv, _ = _pack_rows(small_v)
    sd, sm, sv = _adamw(pw, pg, pm, pv, "adamw_small", None)
    big_out[1] = _adamw(big_w[1], sm_done[2][0], big_m[1], big_v[1], "adamw_1", None, also_g=True)
    g_big = [g_w_ada, big_out[1][3]] + late_joined
    shapes = [a.shape for a in small_w]
    sd, sm, sv = _unpack_rows(sd, sp, shapes), _unpack_rows(sm, sp, shapes), _unpack_rows(sv, sp, shapes)

    def order(wa, bA, wi, bI, cw, wpa, wpc, wo, bO, lg, lb):
        return (wa[None], bA, wi[None], bI, cw[None], wpa[None], wpc[None], wo[None], bO, lg, lb)

    sg = [g.reshape(s) for g, s in zip(small_g, shapes)]
    grads_out = order(g_big[0], sg[0], g_big[1], sg[1], sg[2], g_big[2], g_big[3], g_big[4], sg[3], sg[4], sg[5])
    outs = []
    for idx, small in enumerate((sd, sm, sv)):
        outs.append(order(big_out[0][idx], small[0], big_out[1][idx], small[1], small[2], big_out[2][idx],
                          big_out[3][idx], big_out[4][idx], small[3], small[4], small[5]))
    return (loss, grad_x2.reshape(bsz, seq, d), *grads_out, *outs[0], *outs[1], *outs[2])
```

```python
import jax
import jax.numpy as jnp
from jax import lax
from jax.experimental import pallas as pl
from jax.experimental.pallas import tpu as pltpu

F32 = jnp.float32
BF16 = jnp.bfloat16
MESH = pl.DeviceIdType.MESH

HEAD_DIM = 64
N_GROUPS = 3
DILATIONS = (1, 4, 16)
N_HEADS = 12
SUB = 128
Q_WIDTH = 768
Z_WIDTH = 256
ATT = 3 * Q_WIDTH + Z_WIDTH
SLAB = 128
PAIR_SLABS = 10
PAIR_COLS = PAIR_SLABS * SLAB
CONV_TILE = 256
SOFTMAX_ROWS = 32
BLOCKS_PER_TRIP = 8
BLOCKS_PER_TRIP_BWD = 16
ALIBI_MAX_EXP = 8.0
ALPHA = 2.0 ** 0.25
LN_EPS = 1e-5
ADAM_LR, ADAM_B1, ADAM_B2, ADAM_EPS, ADAM_WD, ADAM_STEP = 0.001, 0.9, 0.999, 1e-08, 0.01, 10
N_CHIPS = 4
N_DEV = 8
VMEM_LIMIT_V7X = 60 * 1024 * 1024
NEG = -1e30


def _params(*sem):
    return pltpu.CompilerParams(dimension_semantics=sem, vmem_limit_bytes=VMEM_LIMIT_V7X)


def _sigmoid(v):
    return 0.5 * jnp.tanh(0.5 * v) + 0.5


class _Layout:
    def __init__(self, d):
        self.d = d
        self.din = ATT + 6 * d
        c0 = 3072
        while c0 % (2 * d):
            c0 += 1024
        self.c0, self.g0, self.np = c0, c0 + 4 * d, c0 + 6 * d
        self.n_conv_tiles = d // CONV_TILE

    def attn_nat_slab(self, s):
        p, i = s // PAIR_SLABS, s % PAIR_SLABS
        return jnp.where(i < 9, (i // 3) * 6 + (i % 3) * 2 + p, 18 + p)

    def rest_nat_tile(self, t):
        n4 = 4 * self.n_conv_tiles
        conv = ATT // CONV_TILE + (t % 4) * self.n_conv_tiles + t // 4
        return jnp.where(t < n4, conv, ATT // CONV_TILE + t)

    def perm_vector(self, v):
        parts = []
        for s in range(2 * PAIR_SLABS):
            p, i = divmod(s, PAIR_SLABS)
            ns = (i // 3) * 6 + (i % 3) * 2 + p if i < 9 else 18 + p
            parts.append(v[:, ns * SLAB:(ns + 1) * SLAB])
        parts.append(jnp.zeros((1, self.c0 - ATT), v.dtype))
        for j in range(self.n_conv_tiles):
            for k in range(4):
                a = ATT + k * self.d + j * CONV_TILE
                parts.append(v[:, a:a + CONV_TILE])
        parts.append(v[:, ATT + 4 * self.d:])
        return jnp.concatenate(parts, axis=1)


def _place():
    return lax.axis_index("x"), lax.axis_index("y"), lax.axis_index("c")


def _other_chips(x, y):
    return [(1 - x, y), (x, 1 - y), (1 - x, 1 - y)]


def _shard_of(ref, col_sharded, chip, half=None):
    if col_sharded:
        cs = ref.shape[1] // N_CHIPS
        cols = pl.ds(pl.multiple_of(chip * cs, SLAB), cs)
        if half is None:
            return ref.at[:, cols]
        n = ref.shape[0] // 2
        return ref.at[pl.ds(half * n, n), cols]
    rs = ref.shape[0] // N_CHIPS
    if half is None:
        return ref.at[pl.ds(chip * rs, rs)]
    return ref.at[pl.ds(chip * rs + half * (rs // 2), rs // 2)]


GATHER_PIECES = 8


def _cast_into_full(shard, col_sharded, chip, name):
    rows, cols = shard.shape
    tr = _row_tile(rows, cols)
    nb = rows // tr

    def body(chip_ref, s_ref, o_ref):
        del chip_ref
        o_ref[...] = s_ref[...].astype(BF16)

    if col_sharded:
        full, out_spec = (rows, cols * N_CHIPS), pl.BlockSpec((tr, cols), lambda i, ch: (i, ch[0]))
    else:
        full, out_spec = (rows * N_CHIPS, cols), pl.BlockSpec((tr, cols), lambda i, ch: (ch[0] * nb + i, 0))
    return pl.pallas_call(
        body, name=name,
        grid_spec=pltpu.PrefetchScalarGridSpec(num_scalar_prefetch=1, grid=(nb,),
                                               in_specs=[pl.BlockSpec((tr, cols), lambda i, ch: (i, 0))], out_specs=out_spec),
        out_shape=jax.ShapeDtypeStruct(full, BF16), compiler_params=_params("parallel"))(chip, shard)


def _cast_into_full_small(shards, col_sharded, chip, name):
    n = len(shards)

    def body(chip_ref, *refs):
        del chip_ref
        for w in range(n):
            refs[n + w][...] = refs[w][...].astype(BF16)

    in_specs = [pl.BlockSpec(s.shape, lambda i, ch: (0, 0)) for s in shards]
    out_specs = [pl.BlockSpec(s.shape, (lambda i, ch: (0, ch[0])) if cs else (lambda i, ch: (ch[0], 0)))
                 for s, cs in zip(shards, col_sharded)]
    fulls = [(s.shape[0], s.shape[1] * N_CHIPS) if cs else (s.shape[0] * N_CHIPS, s.shape[1]) for s, cs in zip(shards, col_sharded)]
    return pl.pallas_call(
        body, name=name,
        grid_spec=pltpu.PrefetchScalarGridSpec(num_scalar_prefetch=1, grid=(1,), in_specs=in_specs, out_specs=out_specs),
        out_shape=[jax.ShapeDtypeStruct(f, BF16) for f in fulls], compiler_params=_params("arbitrary"))(chip, *shards)


def _gather_weights(fulls, col_sharded, small):
    n = len(fulls)
    kp = GATHER_PIECES

    def piece(ref, cs, chip, half, k):
        if cs:
            width = ref.shape[1] // N_CHIPS
            rows = ref.shape[0] // 2 // kp
            return ref.at[pl.ds(half * (ref.shape[0] // 2) + k * rows, rows), pl.ds(pl.multiple_of(chip * width, SLAB), width)]
        rs = ref.shape[0] // N_CHIPS
        rows = rs // 2 // kp
        return ref.at[pl.ds(chip * rs + half * (rs // 2) + k * rows, rows)]

    def body(*refs):
        sm_in, outs, sm_out = refs[n], refs[n + 1:2 * n + 1], refs[2 * n + 1]
        send, recv, fsend, frecv, lsem, ssend, srecv = refs[2 * n + 2:]
        x, y, c = _place()
        mine = 2 * x + y
        sibling = (x, y, 1 - c)
        first = (x ^ (1 - c), y ^ c)
        second = (x ^ c, y ^ (1 - c))
        diagonal = (1 - x, 1 - y)
        sources = [first, second, diagonal]
        senders = [first, second, second]

        def copy(ref, sems, slot, to):
            return pltpu.make_async_remote_copy(src_ref=ref, dst_ref=ref, send_sem=sems[0].at[slot], recv_sem=sems[1].at[slot],
                                                device_id=to, device_id_type=MESH)

        local = pltpu.make_async_copy(sm_in, _shard_of(sm_out, True, mine), lsem)
        local.start()
        sends = []
        for k, (cx, cy) in enumerate(_other_chips(x, y)):
            cp = pltpu.make_async_remote_copy(src_ref=sm_in, dst_ref=_shard_of(sm_out, True, mine), send_sem=ssend.at[k],
                                              recv_sem=srecv.at[k], device_id=(cx, cy, c), device_id_type=MESH)
            cp.start()
            sends.append(cp)
        for k in range(kp):
            for w in range(n):
                own = piece(outs[w], col_sharded[w], mine, c, k)
                for slot, chip in enumerate((first, second)):
                    cp = copy(own, (send, recv), (w * 3 + slot) * kp + k, (*chip, c))
                    cp.start()
                    sends.append(cp)
        for slot in range(3):
            source = 2 * sources[slot][0] + sources[slot][1]
            for k in range(kp):
                for w in range(n):
                    landed = piece(outs[w], col_sharded[w], source, c, k)
                    copy(landed, (send, recv), (w * 3 + slot) * kp + k, (*senders[slot], c)).wait_recv()
                    if slot == 0:
                        cp = copy(landed, (send, recv), (w * 3 + 2) * kp + k, (*second, c))
                        cp.start()
                        sends.append(cp)
                    cp = copy(landed, (fsend, frecv), (w * 3 + slot) * kp + k, sibling)
                    cp.start()
                    sends.append(cp)
        for slot, chip in enumerate((second, first, diagonal)):
            for k in range(kp):
                for w in range(n):
                    passed = piece(outs[w], col_sharded[w], 2 * chip[0] + chip[1], 1 - c, k)
                    copy(passed, (fsend, frecv), (w * 3 + slot) * kp + k, sibling).wait_recv()
        for k, (cx, cy) in enumerate(_other_chips(x, y)):
            theirs = _shard_of(sm_out, True, 2 * cx + cy)
            pltpu.make_async_remote_copy(src_ref=theirs, dst_ref=theirs, send_sem=ssend.at[k], recv_sem=srecv.at[k],
                                         device_id=(cx, cy, c), device_id_type=MESH).wait_recv()
        for cp in sends:
            cp.wait_send()
        local.wait()

    any_spec = pl.BlockSpec(memory_space=pl.ANY)
    outs = pl.pallas_call(
        body, name="gather_weights",
        out_shape=[jax.ShapeDtypeStruct(f.shape, BF16) for f in fulls]
        + [jax.ShapeDtypeStruct((small.shape[0], small.shape[1] * N_CHIPS), small.dtype)],
        in_specs=[any_spec] * (n + 1), out_specs=[any_spec] * (n + 1), input_output_aliases={w: w for w in range(n)},
        scratch_shapes=[pltpu.SemaphoreType.DMA((n * 3 * kp,)), pltpu.SemaphoreType.DMA((n * 3 * kp,)),
                        pltpu.SemaphoreType.DMA((n * 3 * kp,)), pltpu.SemaphoreType.DMA((n * 3 * kp,)), pltpu.SemaphoreType.DMA,
                        pltpu.SemaphoreType.DMA((3,)), pltpu.SemaphoreType.DMA((3,))],
    )(*fulls, small)
    return outs[:n], outs[n]


HBM_SPEC = pl.BlockSpec(memory_space=pltpu.HBM)
SEM_SPEC = pl.BlockSpec(memory_space=pltpu.SEMAPHORE)
DATAFLOW = pltpu.SideEffectType.DATAFLOW_SIDE_EFFECTING


def _start_copies(name, arrays, sem_shape, after, copies):
    n = len(arrays)

    def body(*refs):
        for cp in copies(refs[:n], refs[n + 1], refs[n + 2]):
            cp.start()
        token = refs[2 * n + 3]
        token[...] = jnp.zeros_like(token)

    res = pl.pallas_call(
        body, name=name,
        out_shape=(pltpu.SemaphoreType.DMA(sem_shape), pltpu.SemaphoreType.DMA(sem_shape),
                   *[pltpu.HBM(a.shape, a.dtype) for a in arrays], jax.ShapeDtypeStruct((8, 128), F32)),
        in_specs=[HBM_SPEC] * n + [pl.BlockSpec(memory_space=pl.ANY)],
        out_specs=(SEM_SPEC, SEM_SPEC, *([HBM_SPEC] * n), pl.BlockSpec(memory_space=pltpu.VMEM)),
        input_output_aliases={i: 2 + i for i in range(n)},
        compiler_params=pltpu.CompilerParams(has_side_effects=DATAFLOW),
    )(*[pltpu.with_memory_space_constraint(a, pltpu.HBM) for a in arrays], after)
    return res[0], res[1], list(res[2:2 + n]), res[2 + n]


def _wait_copies(name, arrays, send, recv, after, copies):
    n = len(arrays)

    def body(*refs):
        for cp in copies(refs[:n], refs[n], refs[n + 1]):
            cp.wait_send()
            cp.wait_recv()

    return pl.pallas_call(
        body, name=name, out_shape=[pltpu.HBM(a.shape, a.dtype) for a in arrays],
        in_specs=[HBM_SPEC] * n + [SEM_SPEC, SEM_SPEC, pl.BlockSpec(memory_space=pl.ANY)], out_specs=[HBM_SPEC] * n,
        input_output_aliases={i: i for i in range(n)},
        compiler_params=pltpu.CompilerParams(has_side_effects=DATAFLOW),
    )(*arrays, send, recv, after)


def _direct_gather_copies(col_sharded):
    def copies(refs, send, recv):
        x, y, c = _place()
        mine = 2 * x + y
        out = []
        for w, ref in enumerate(refs):
            own_half = _shard_of(ref, col_sharded[w], mine, c)
            k = 0
            for cx, cy in _other_chips(x, y):
                for pc in (c, 1 - c):
                    out.append(pltpu.make_async_remote_copy(
                        src_ref=own_half, dst_ref=own_half, send_sem=send.at[6 * w + k], recv_sem=recv.at[6 * w + k],
                        device_id=(cx, cy, pc), device_id_type=MESH))
                    k += 1
        return out
    return copies


def _chip_scatter_copies(n, col_sharded):
    def piece(ref, cs, chip):
        if cs:
            w = ref.shape[2] // N_CHIPS
            return ref.at[:, :, pl.ds(pl.multiple_of(chip * w, SLAB), w)]
        return ref.at[pl.ds(chip, 1)]

    def copies(refs, send, recv):
        x, y, c = _place()
        out = []
        for k, (cx, cy) in enumerate(_other_chips(x, y)):
            for w in range(n):
                out.append(pltpu.make_async_remote_copy(
                    src_ref=piece(refs[w], col_sharded[w], 2 * cx + cy), dst_ref=refs[n + w].at[pl.ds(k, 1)],
                    send_sem=send.at[3 * w + k], recv_sem=recv.at[3 * w + k], device_id=(cx, cy, c), device_id_type=MESH))
        return out
    return copies


def _shard_views(gs, col_sharded):
    return [g.reshape(1, *g.shape) if cs else g.reshape(N_CHIPS, g.shape[0] // N_CHIPS, g.shape[1])
            for g, cs in zip(gs, col_sharded)]


DMA_CHUNK_BYTES = 1 << 20


def _chunk_rows(shape, itemsize):
    s, rows, cols = shape
    n = 1
    while s * (rows // n) * cols * itemsize > DMA_CHUNK_BYTES and (rows // n) % 32 == 0:
        n *= 2
    return rows // n


def _rows_of(ref, row0, rows, cols):
    if cols is None:
        return ref.at[:, pl.ds(row0, rows)]
    return ref.at[:, pl.ds(row0, rows), pl.ds(cols[0], cols[1])]


def _row_pieces(src, src_row0, dst, dst_row0, rows, send_sem, recv_sem, device, cols=None):
    width = src.shape[2] if cols is None else cols[1]
    step = _chunk_rows((src.shape[0], rows, width), src.dtype.itemsize)
    return [pltpu.make_async_remote_copy(src_ref=_rows_of(src, src_row0 + r, step, cols), dst_ref=_rows_of(dst, dst_row0 + r, step, cols),
                                         send_sem=send_sem, recv_sem=recv_sem, device_id=device, device_id_type=MESH)
            for r in range(0, rows, step)]


def _pair_exchange_copies(n, whole, cols=None):
    def copies(refs, send, recv):
        x, y, c = _place()
        sibling = (x, y, 1 - c)
        out = []
        for w in range(n):
            hr = refs[n + w].shape[1]
            if whole:
                out.append(pltpu.make_async_remote_copy(
                    src_ref=_rows_of(refs[w], (1 - c) * hr, hr, cols), dst_ref=_rows_of(refs[n + w], 0, hr, cols),
                    send_sem=send.at[w], recv_sem=recv.at[w], device_id=sibling, device_id_type=MESH))
            else:
                out += _row_pieces(refs[w], (1 - c) * hr, refs[n + w], 0, hr, send.at[w], recv.at[w], sibling, cols)
        return out
    return copies


def _pair_join_copies(n, whole, sem0=0):
    def copies(refs, send, recv):
        x, y, c = _place()
        sibling = (x, y, 1 - c)
        out = []
        for w in range(n):
            hr = refs[w].shape[1] // 2
            sems = dict(send_sem=send.at[sem0 + w], recv_sem=recv.at[sem0 + w])
            if whole:
                out.append(pltpu.make_async_remote_copy(
                    src_ref=refs[w].at[:, pl.ds(c * hr, hr)], dst_ref=refs[w].at[:, pl.ds((1 - c) * hr, hr)],
                    device_id=sibling, device_id_type=MESH, **sems))
            else:
                out += _row_pieces(refs[w], c * hr, refs[w], c * hr, hr, sems["send_sem"], sems["recv_sem"], sibling)
        return out
    return copies


def _both_copies(first, n_first, second):
    def copies(refs, send, recv):
        return first(refs[:n_first], send, recv) + second(refs[n_first:], send, recv)
    return copies


def _small_gather_copies(refs, send, recv):
    vec, land = refs
    x, y, c = _place()
    me = 4 * x + 2 * y + c
    return [pltpu.make_async_remote_copy(src_ref=vec, dst_ref=land.at[me], send_sem=send.at[k], recv_sem=recv.at[k],
                                         device_id=peer, device_id_type=MESH) for k, (peer, _) in enumerate(_all_devices(x, y, c))]


def _small_sum(vec, land, me, n_sum, d):
    rows = vec.shape[0]

    def body(me_ref, v_ref, l_ref, sum_ref, kept_ref):
        def slot(k):
            return jnp.where(me_ref[0] == k, v_ref[...], l_ref[k])

        total = slot(0)[0:n_sum, :]
        kept_ref[0] = slot(0)[n_sum:rows, :]
        for k in range(1, N_DEV):
            total = total + slot(k)[0:n_sum, :]
            kept_ref[k] = slot(k)[n_sum:rows, :]
        sum_ref[...] = total
        loss = 0.5 / d * jnp.sum(total[0:8, :])
        sum_ref[0:8, :] = jnp.full((8, 128), loss, F32)

    vm = pl.BlockSpec(memory_space=pltpu.VMEM)
    return pl.pallas_call(
        body, name="small_sum", in_specs=[pl.BlockSpec(memory_space=pltpu.SMEM), vm, vm], out_specs=[vm, vm],
        out_shape=[jax.ShapeDtypeStruct((n_sum, 128), F32), jax.ShapeDtypeStruct((N_DEV, rows - n_sum, 128), F32)],
        compiler_params=pltpu.CompilerParams(vmem_limit_bytes=VMEM_LIMIT_V7X))(me, vec, land)


def _pair_exchange_into(view, land, cols, name):
    def body(v_ref, l_in, l_ref, send, recv):
        del l_in
        x, y, c = _place()
        sibling = (x, y, 1 - c)
        hr = l_ref.shape[1]
        for cp in _row_pieces(v_ref, (1 - c) * hr, l_ref, 0, hr, send, recv, sibling, cols):
            cp.start()
        pltpu.make_async_remote_copy(src_ref=_rows_of(v_ref, (1 - c) * hr, hr, cols), dst_ref=_rows_of(l_ref, 0, hr, cols),
                                     send_sem=send, recv_sem=recv, device_id=sibling, device_id_type=MESH).wait()

    any_spec = pl.BlockSpec(memory_space=pl.ANY)
    return pl.pallas_call(
        body, name=name, out_shape=jax.ShapeDtypeStruct(land.shape, land.dtype),
        in_specs=[any_spec, any_spec], out_specs=any_spec, input_output_aliases={1: 0},
        scratch_shapes=[pltpu.SemaphoreType.DMA, pltpu.SemaphoreType.DMA],
    )(view, land)


def _pair_sum(view, got, core, name):
    s, r, cols = view.shape
    hr = r // 2
    tr = _row_tile(hr, cols, itemsize=view.dtype.itemsize, budget=4 << 20)
    nb = hr // tr

    def body(core_ref, a_ref, b_ref, o_ref):
        del core_ref
        o_ref[...] = (a_ref[...].astype(F32) + b_ref[...].astype(F32)).astype(BF16)

    same = pl.BlockSpec((None, tr, cols), lambda j, i, core_ref: (j, i, 0))
    return pl.pallas_call(
        body, name=name,
        grid_spec=pltpu.PrefetchScalarGridSpec(
            num_scalar_prefetch=1, grid=(s, nb),
            in_specs=[pl.BlockSpec((None, tr, cols), lambda j, i, core_ref: (j, core_ref[0] * nb + i, 0)), same],
            out_specs=same),
        out_shape=jax.ShapeDtypeStruct((s, hr, cols), BF16), compiler_params=_params("parallel", "parallel"))(core, view, got)


def _pair_sum_small(views, gots, core, name):
    n = len(views)

    def body(core_ref, *refs):
        del core_ref
        for w in range(n):
            refs[2 * n + w][...] = (refs[w][...].astype(F32) + refs[n + w][...].astype(F32)).astype(BF16)

    halves = [(v.shape[0], v.shape[1] // 2, v.shape[2]) for v in views]
    own = [pl.BlockSpec(h, lambda i, core_ref: (0, core_ref[0], 0)) for h in halves]
    whole = [pl.BlockSpec(h, lambda i, core_ref: (0, 0, 0)) for h in halves]
    return pl.pallas_call(
        body, name=name,
        grid_spec=pltpu.PrefetchScalarGridSpec(num_scalar_prefetch=1, grid=(1,), in_specs=own + whole, out_specs=whole),
        out_shape=[jax.ShapeDtypeStruct(h, BF16) for h in halves], compiler_params=_params("arbitrary"))(core, *views, *gots)


def _piece_cols(part, col_sharded):
    return part.shape[2] // N_CHIPS if col_sharded else part.shape[2]


def _chip_sum_small(parts, gots, col_sharded, place, name):
    n = len(parts)

    def body(place_ref, *refs):
        del place_ref
        for w in range(n):
            got = refs[n + w]
            acc = refs[w][...].astype(F32) + got[0].astype(F32)
            refs[2 * n + w][...] = acc + got[1].astype(F32) + got[2].astype(F32)

    own, others, outs, shapes = [], [], [], []
    for p, cs in zip(parts, col_sharded):
        hr, cols = p.shape[1], _piece_cols(p, cs)
        own.append(pl.BlockSpec((None, hr, cols), (lambda i, pr: (0, 0, pr[0])) if cs else (lambda i, pr: (pr[0], 0, 0))))
        others.append(pl.BlockSpec((3, hr, cols), lambda i, pr: (0, 0, 0)))
        outs.append(pl.BlockSpec((hr, cols), lambda i, pr: (pr[1], 0)))
        shapes.append(jax.ShapeDtypeStruct((2 * hr, cols), F32))
    return pl.pallas_call(
        body, name=name,
        grid_spec=pltpu.PrefetchScalarGridSpec(num_scalar_prefetch=1, grid=(1,), in_specs=own + others, out_specs=outs),
        out_shape=shapes, compiler_params=_params("arbitrary"))(place, *parts, *gots)


def _chip_sum(part, got, col_sharded, place, name):
    _, hr, _ = part.shape
    cols = _piece_cols(part, col_sharded)
    tr = _row_tile(hr, cols)
    nb = hr // tr

    def body(place_ref, own_ref, g0_ref, g1_ref, g2_ref, o_ref):
        del place_ref
        acc = own_ref[...].astype(F32) + g0_ref[...].astype(F32)
        o_ref[...] = acc + g1_ref[...].astype(F32) + g2_ref[...].astype(F32)

    if col_sharded:
        own = pl.BlockSpec((None, tr, cols), lambda i, pr: (0, i, pr[0]))
    else:
        own = pl.BlockSpec((None, tr, cols), lambda i, pr: (pr[0], i, 0))
    others = [pl.BlockSpec((None, tr, cols), lambda i, pr, k=k: (k, i, 0)) for k in range(3)]
    return pl.pallas_call(
        body, name=name,
        grid_spec=pltpu.PrefetchScalarGridSpec(
            num_scalar_prefetch=1, grid=(nb,), in_specs=[own] + others,
            out_specs=pl.BlockSpec((tr, cols), lambda i, pr: (pr[1] * nb + i, 0))),
        out_shape=jax.ShapeDtypeStruct((2 * hr, cols), F32), compiler_params=_params("parallel"))(place, part, got, got, got)


def _row_tile(rows, cols, itemsize=4, budget=2 << 20):
    t = rows
    while t * cols * itemsize > budget and t % 16 == 0:
        t //= 2
    return t


def _adamw(w, g, m, v, name, after, also_g=False):
    rows, cols = w.shape
    tr = _row_tile(rows, cols, budget=2 << 20)
    extra = [] if after is None else [after]
    n_out = 4 if also_g else 3

    def body(w_ref, g_ref, m_ref, v_ref, *rest):
        d_ref, nm_ref, nv_ref = rest[len(extra):len(extra) + 3]
        g_ = g_ref[...]
        if also_g:
            rest[len(extra) + 3][...] = g_
        nm = ADAM_B1 * m_ref[...] + (1.0 - ADAM_B1) * g_
        nv = ADAM_B2 * v_ref[...] + (1.0 - ADAM_B2) * (g_ * g_)
        m_hat = nm / (1.0 - ADAM_B1 ** ADAM_STEP)
        v_hat = nv / (1.0 - ADAM_B2 ** ADAM_STEP)
        d_ref[...] = -ADAM_LR * (m_hat / (jnp.sqrt(v_hat) + ADAM_EPS) + ADAM_WD * w_ref[...])
        nm_ref[...] = nm
        nv_ref[...] = nv

    spec = pl.BlockSpec((tr, cols), lambda i: (i, 0))
    shp = jax.ShapeDtypeStruct((rows, cols), F32)
    return pl.pallas_call(body, name=name, grid=(rows // tr,),
                          in_specs=[spec] * 4 + [pl.BlockSpec(memory_space=pl.ANY)] * len(extra), out_specs=[spec] * n_out,
                          out_shape=[shp] * n_out, compiler_params=_params("parallel"))(w, g, m, v, *extra)


def _all_devices(x, y, c):
    out = []
    for k in range(1, N_DEV):
        peer = (x ^ ((k >> 2) & 1), y ^ ((k >> 1) & 1), c ^ (k & 1))
        out.append((peer, 4 * peer[0] + 2 * peer[1] + peer[2]))
    return out


def _ada_exchange(c, w_shard, b_ada):
    bsz, d = c.shape
    cs = w_shard.shape[1]

    def body(c_ref, w_ref, b_ref, mod_ref, act_ref, c_all, part, pieces, csend, crecv, psend, precv):
        x, y, core = _place()
        me = 4 * x + 2 * y + core
        chip = 2 * x + y
        c_all[me] = c_ref[...]
        peers = _all_devices(x, y, core)
        copies = []
        for k, (peer, _) in enumerate(peers):
            cp = pltpu.make_async_remote_copy(src_ref=c_ref, dst_ref=c_all.at[me], send_sem=csend.at[k], recv_sem=crecv.at[k],
                                              device_id=peer, device_id_type=MESH)
            cp.start()
            copies.append(cp)
        for k, (_, src) in enumerate(peers):
            pltpu.make_async_remote_copy(src_ref=c_ref, dst_ref=c_all.at[src], send_sem=csend.at[k], recv_sem=crecv.at[k],
                                         device_id=(x, y, core), device_id_type=MESH).wait_recv()
        rows = jnp.concatenate([c_all[i] for i in range(N_DEV)], axis=0)
        act = rows * _sigmoid(rows)
        act_ref[...] = act
        prod = jnp.dot(act.astype(BF16), w_ref[...].astype(BF16), preferred_element_type=F32)
        for i in range(N_DEV):
            part[i] = prod[i * bsz:(i + 1) * bsz, :]
        pieces[chip] = part[me]
        chips = _other_chips(x, y)
        for k, (cx, cy) in enumerate(chips):
            cp = pltpu.make_async_remote_copy(src_ref=part.at[4 * cx + 2 * cy + core], dst_ref=pieces.at[chip],
                                              send_sem=psend.at[k], recv_sem=precv.at[k], device_id=(cx, cy, core),
                                              device_id_type=MESH)
            cp.start()
            copies.append(cp)
        for k, (cx, cy) in enumerate(chips):
            pltpu.make_async_remote_copy(src_ref=part.at[me], dst_ref=pieces.at[2 * cx + cy], send_sem=psend.at[k],
                                         recv_sem=precv.at[k], device_id=(cx, cy, core), device_id_type=MESH).wait_recv()
        for cp in copies:
            cp.wait_send()
        mod_ref[...] = jnp.concatenate([pieces[j] for j in range(N_CHIPS)], axis=1) + b_ref[...]

    vm = pl.BlockSpec(memory_space=pltpu.VMEM)
    return pl.pallas_call(
        body, name="ada_exchange", in_specs=[vm] * 3, out_specs=[vm] * 2,
        out_shape=[jax.ShapeDtypeStruct((bsz, 3 * d), F32), jax.ShapeDtypeStruct((N_DEV * bsz, d), F32)],
        scratch_shapes=[pltpu.VMEM((N_DEV, bsz, d), F32), pltpu.VMEM((N_DEV, bsz, cs), F32), pltpu.VMEM((N_CHIPS, bsz, cs), F32),
                        pltpu.SemaphoreType.DMA((N_DEV - 1,)), pltpu.SemaphoreType.DMA((N_DEV - 1,)),
                        pltpu.SemaphoreType.DMA((3,)), pltpu.SemaphoreType.DMA((3,))],
        compiler_params=pltpu.CompilerParams(vmem_limit_bytes=VMEM_LIMIT_V7X))(c, w_shard, b_ada)


def _grad_and_adamw_w_ada(act_t, d_cols, w, m, v):
    d, n = act_t.shape
    cs = d_cols.shape[1]
    tn = min(256, cs)

    def body(a_ref, g_ref, w_ref, m_ref, v_ref, go_ref, d_ref, nm_ref, nv_ref):
        a, g = a_ref[...], g_ref[...]
        acc = a[:, 0:1] * g[0:1, :]
        for b in range(1, n):
            acc = acc + a[:, b:b + 1] * g[b:b + 1, :]
        go_ref[...] = acc
        nm = ADAM_B1 * m_ref[...] + (1.0 - ADAM_B1) * acc
        nv = ADAM_B2 * v_ref[...] + (1.0 - ADAM_B2) * (acc * acc)
        m_hat = nm / (1.0 - ADAM_B1 ** ADAM_STEP)
        v_hat = nv / (1.0 - ADAM_B2 ** ADAM_STEP)
        d_ref[...] = -ADAM_LR * (m_hat / (jnp.sqrt(v_hat) + ADAM_EPS) + ADAM_WD * w_ref[...])
        nm_ref[...] = nm
        nv_ref[...] = nv

    tile = pl.BlockSpec((d, tn), lambda j: (0, j))
    shp = jax.ShapeDtypeStruct((d, cs), F32)
    return pl.pallas_call(body, name="adamw_w_ada", grid=(cs // tn,),
                          in_specs=[pl.BlockSpec((d, n), lambda j: (0, 0)), pl.BlockSpec((n, tn), lambda j: (0, j)), tile, tile, tile],
                          out_specs=[tile] * 4, out_shape=[shp] * 4, compiler_params=_params("parallel"))(act_t, d_cols, w, m, v)


def _permute_w_in(w_nat, lay):
    d = lay.d
    group = 4

    def call(name, width, n_pieces, nat_piece, out_block0, prev):
        def body(*refs):
            refs[-1][...] = jnp.concatenate([r[...] for r in refs[:group]], axis=1)

        in_specs = [pl.BlockSpec((d, width), lambda s, m=m: (0, nat_piece(group * s + m))) for m in range(group)]
        args = [w_nat] * group
        aliases = {}
        if prev is not None:
            in_specs.append(pl.BlockSpec(memory_space=pl.ANY))
            args.append(prev)
            aliases = {group: 0}
        return pl.pallas_call(
            body, name=name, grid=(n_pieces // group,), in_specs=in_specs,
            out_specs=pl.BlockSpec((d, group * width), lambda s: (0, out_block0 + s)),
            out_shape=jax.ShapeDtypeStruct((d, lay.np), BF16), input_output_aliases=aliases,
            compiler_params=_params("arbitrary"))(*args)

    w_all = call("permute_w_attn", SLAB, 2 * PAIR_SLABS, lay.attn_nat_slab, 0, None)
    n_rest = 6 * d // CONV_TILE
    if n_rest % group:
        group = 2
    return call("permute_w_rest", CONV_TILE, n_rest, lay.rest_nat_tile, lay.c0 // (group * CONV_TILE), w_all)


def _project(x2, mod3, w_all, b_all, seq, col0, ncols, tn, out_dtype, want_ht, name):
    t, d = x2.shape
    tm = min(2048, seq)
    per_seq = seq // tm
    j0 = col0 // tn

    def body(x_ref, mod_ref, w_ref, b_ref, o_ref, *rest):
        h_ref = rest[-1]

        @pl.when(pl.program_id(1) == 0)
        def _():
            h = x_ref[...] * (1.0 + mod_ref[:, d:2 * d]) + mod_ref[:, 0:d]
            h_ref[...] = h.astype(BF16)
            if want_ht:
                rest[0][...] = h.T.astype(BF16)

        o_ref[...] = (jnp.dot(h_ref[...], w_ref[...], preferred_element_type=F32) + b_ref[...]).astype(out_dtype)

    out_shape = [jax.ShapeDtypeStruct((t, ncols), out_dtype)]
    out_specs = [pl.BlockSpec((tm, tn), lambda i, j: (i, j))]
    if want_ht:
        out_shape.append(jax.ShapeDtypeStruct((d, t), BF16))
        out_specs.append(pl.BlockSpec((d, tm), lambda i, j: (0, i)))
    return pl.pallas_call(
        body, name=name, grid=(t // tm, ncols // tn),
        in_specs=[pl.BlockSpec((tm, d), lambda i, j: (i, 0)),
                  pl.BlockSpec((None, 1, 3 * d), lambda i, j: (i // per_seq, 0, 0)),
                  pl.BlockSpec((d, tn), lambda i, j: (0, j0 + j)),
                  pl.BlockSpec((1, tn), lambda i, j: (0, j0 + j))],
        out_specs=out_specs, out_shape=out_shape,
        scratch_shapes=[pltpu.VMEM((tm, d), BF16)],
        compiler_params=_params("arbitrary", "arbitrary"))(x2, mod3, w_all, b_all)


def _slope(g, p, hh):
    head = 4 * g + 2 * p + hh
    return 2.0 ** (-ALIBI_MAX_EXP * (head + 1.0) / N_HEADS)


def _ld_rows(ref, start, n, stride):
    if stride == 1:
        return ref[pl.ds(start, n), :]
    return ref[pl.ds(start, n, stride=stride), :]


def _st_rows(ref, start, n, stride, val):
    if stride == 1:
        ref[pl.ds(start, n), :] = val
    else:
        ref[pl.ds(start, n, stride=stride), :] = val


def _sub_blocks(g, seq):
    return seq // DILATIONS[g] // SUB


def _key_rows(g, seq):
    return SUB if _sub_blocks(g, seq) == 1 else 2 * SUB


def _fill_bias(bias_ref, p, seq):
    for g in range(N_GROUPS):
        nk = _key_rows(g, seq)
        diff = lax.broadcasted_iota(jnp.int32, (SUB, nk), 0) - lax.broadcasted_iota(jnp.int32, (SUB, nk), 1)
        for i, off in enumerate((0, SUB)):
            if i == 1 and nk == SUB:
                continue
            delta = diff + off
            ok = (delta >= 0) & (delta <= SUB)
            dist = (delta * DILATIONS[g]).astype(F32)
            for hh in range(2):
                slope = jnp.where(p == 0, _slope(g, 0, hh), _slope(g, 1, hh))
                bias_ref[g, i, hh, :, 0:nk] = jnp.where(ok, -slope * dist, NEG)


def _to_sub_major(pa_ref, col, sub_ref, stage, dil, seq):
    cols = slice(col * SLAB, (col + 1) * SLAB)
    if dil == 1:
        sub_ref[...] = pa_ref[:, cols]
        return
    n = seq // dil
    stage[...] = pa_ref[:, cols].astype(F32)
    for r in range(dil):
        sub_ref[pl.ds(r * n, n), :] = stage[pl.ds(r, n, stride=dil), :].astype(BF16)


def _block_rows(it, g, seq):
    dil, nb = DILATIONS[g], _sub_blocks(g, seq)
    row0 = pl.multiple_of(it * SUB, SUB)
    if nb == 1:
        return row0, row0, 0, it
    blk = it % nb
    first = blk == 0
    krow0 = pl.multiple_of(row0 - jnp.where(first, 0, SUB), SUB)
    nat = row0 if dil == 1 else it // nb + dil * SUB * blk
    return row0, krow0, jnp.where(first, 0, 1), nat


def _nt(a, b):
    return lax.dot_general(a, b, (((1,), (1,)), ((), ())), preferred_element_type=F32)


def _tn(a, b):
    return lax.dot_general(a, b, (((0,), (0,)), ((), ())), preferred_element_type=F32)


def _head_sums(t):
    rows = t.shape[0]
    lo = jnp.broadcast_to(jnp.sum(t[:, :HEAD_DIM], axis=-1, keepdims=True), (rows, HEAD_DIM))
    hi = jnp.broadcast_to(jnp.sum(t[:, HEAD_DIM:], axis=-1, keepdims=True), (rows, HEAD_DIM))
    return jnp.concatenate([lo, hi], axis=-1)


def _attn_fwd(pa, bsz, seq):
    t = pa.shape[0]
    n_blocks = seq // SUB
    chunk = 256

    def body(pa_ref, o_ref, lse_ref, a_ref, sub, stage, bias_ref, s_buf, p_buf, l_buf):
        p = pl.program_id(1)
        _fill_bias(bias_ref, p, seq)
        head0 = lax.broadcasted_iota(jnp.int32, (SUB, SLAB), 1) < HEAD_DIM
        for g in range(N_GROUPS):
            dil = DILATIONS[g]
            for w in range(3):
                _to_sub_major(pa_ref, 3 * w + g, sub.at[w], stage, dil, seq)
            nk = _key_rows(g, seq)

            def trip(i, carry, g=g, dil=dil, nk=nk):
                places = [_block_rows(BLOCKS_PER_TRIP * i + j, g, seq) for j in range(BLOCKS_PER_TRIP)]
                for j, (row0, krow0, _, _) in enumerate(places):
                    q = sub[0, pl.ds(row0, SUB), :]
                    zero = jnp.zeros_like(q)
                    q2 = jnp.concatenate([jnp.where(head0, q, zero), jnp.where(head0, zero, q)], axis=0) * (HEAD_DIM ** -0.5)
                    s_buf[j, :, 0:nk] = _nt(q2, sub[1, pl.ds(krow0, nk), :])
                for j, (_, _, bi, _) in enumerate(places):
                    for c in range(0, 2 * SUB, SOFTMAX_ROWS):
                        hh, r = divmod(c, SUB)
                        s = s_buf[j, c:c + SOFTMAX_ROWS, 0:nk] + bias_ref[g, bi, hh, r:r + SOFTMAX_ROWS, 0:nk]
                        m = jnp.max(s, axis=-1, keepdims=True)
                        e = jnp.exp(s - m)
                        den = jnp.sum(e, axis=-1, keepdims=True)
                        p_buf[j, c:c + SOFTMAX_ROWS, 0:nk] = (e * (1.0 / den)).astype(BF16)
                        l_buf[j, c:c + SOFTMAX_ROWS, :] = jnp.broadcast_to(m + jnp.log(den), (SOFTMAX_ROWS, SLAB))
                for j, (_, krow0, _, nat) in enumerate(places):
                    o2 = jnp.dot(p_buf[j, :, 0:nk], sub[2, pl.ds(krow0, nk), :], preferred_element_type=F32)
                    _st_rows(o_ref.at[g], nat, SUB, dil, jnp.where(head0, o2[0:SUB], o2[SUB:2 * SUB]))
                    _st_rows(lse_ref.at[g], nat, SUB, dil, jnp.where(head0, l_buf[j, 0:SUB, :], l_buf[j, SUB:2 * SUB, :]))
                return carry

            lax.fori_loop(0, n_blocks // BLOCKS_PER_TRIP, trip, 0)

        def mix(i, carry):
            rows = pl.ds(pl.multiple_of(i * chunk, chunk), chunk)
            l0, l1, l2 = lse_ref[0, rows, :], lse_ref[1, rows, :], lse_ref[2, rows, :]
            m = jnp.maximum(jnp.maximum(l0, l1), l2)
            e0, e1, e2 = jnp.exp(l0 - m), jnp.exp(l1 - m), jnp.exp(l2 - m)
            tot = e0 + e1 + e2
            o = (e0 / tot) * o_ref[0, rows, :] + (e1 / tot) * o_ref[1, rows, :] + (e2 / tot) * o_ref[2, rows, :]
            z = pa_ref[rows, 9 * SLAB:10 * SLAB].astype(F32)
            a_ref[rows, :] = (o * (z * _sigmoid(z))).astype(BF16)
            return carry

        lax.fori_loop(0, seq // chunk, mix, 0)

    big = jax.ShapeDtypeStruct((N_GROUPS, t, 2 * SLAB), F32)
    return pl.pallas_call(
        body, name="attn_fwd", grid=(bsz, 2),
        in_specs=[pl.BlockSpec((seq, PAIR_COLS), lambda b, p: (b, p))],
        out_specs=[pl.BlockSpec((N_GROUPS, seq, SLAB), lambda b, p: (0, b, p)),
                   pl.BlockSpec((N_GROUPS, seq, SLAB), lambda b, p: (0, b, p)),
                   pl.BlockSpec((seq, SLAB), lambda b, p: (b, p))],
        out_shape=[big, big, jax.ShapeDtypeStruct((t, 2 * SLAB), BF16)],
        scratch_shapes=[pltpu.VMEM((3, seq, SLAB), BF16), pltpu.VMEM((seq, SLAB), F32),
                        pltpu.VMEM((N_GROUPS, 2, 2, SUB, 2 * SUB), F32), pltpu.VMEM((BLOCKS_PER_TRIP, 2 * SUB, 2 * SUB), F32),
                        pltpu.VMEM((BLOCKS_PER_TRIP, 2 * SUB, 2 * SUB), BF16), pltpu.VMEM((BLOCKS_PER_TRIP, 2 * SUB, SLAB), F32)],
        compiler_params=_params("arbitrary", "arbitrary"))(pa)


def _shift_down(v, k, rows):
    return jnp.where(rows >= k, pltpu.roll(v, k, 0), 0.0)


def _shift_up(v, k, rows):
    n = v.shape[0]
    return jnp.where(rows < n - k, pltpu.roll(v, n - k, 0), 0.0)


def _conv_fwd(pr, conv_w, bsz, seq, d):
    t = pr.shape[0]
    ct = CONV_TILE

    def body(p_ref, cw_ref, o_ref):
        u = p_ref[:, 2 * ct:3 * ct].astype(F32) * p_ref[:, 0:ct].astype(F32)
        cw = cw_ref[...]
        rows = lax.broadcasted_iota(jnp.int32, u.shape, 0)
        conv = cw[0:1, :] * _shift_down(u, 2, rows)
        conv = conv + cw[1:2, :] * _shift_down(u, 1, rows)
        conv = conv + cw[2:3, :] * u
        z = p_ref[:, 3 * ct:4 * ct].astype(F32)
        o_ref[...] = (p_ref[:, ct:2 * ct].astype(F32) * conv * (z * _sigmoid(z))).astype(BF16)

    return pl.pallas_call(
        body, name="conv_fwd", grid=(bsz, d // ct),
        in_specs=[pl.BlockSpec((seq, 4 * ct), lambda b, j: (b, j)), pl.BlockSpec((3, ct), lambda b, j: (0, j))],
        out_specs=pl.BlockSpec((seq, ct), lambda b, j: (b, j)),
        out_shape=jax.ShapeDtypeStruct((t, d), BF16), compiler_params=_params("parallel", "parallel"))(pr, conv_w)


def _tail(a_in, b_in, pr, x2, target2, mod3, w_pa, w_pc, w_out, b_out, ln_g, ln_b, seq, lay):
    t, d = x2.shape
    tm = 512
    per_seq = seq // tm
    n_steps = t // tm
    gate_blk = 4 * d // d

    def nt(a, b):
        return lax.dot_general(a, b, (((1,), (1,)), ((), ())), preferred_element_type=F32)

    def tn(a, b):
        return lax.dot_general(a, b, (((0,), (0,)), ((), ())), preferred_element_type=F32)

    def body(a_ref, b_ref, ga_ref, gb_ref, x_ref, tg_ref, mod_ref, wpa_ref, wpc_ref, wo_ref, bo_ref, lg_ref, lb_ref,
             dpg_ref, da_ref, db_ref, gx_ref, dgate_ref, small_ref, gwpa_hbm, gwpc_hbm, gwo_hbm,
             acc_pa, acc_pc, acc_o, sem):
        i = pl.program_id(0)

        @pl.when(i == 0)
        def _():
            acc_pa[...] = jnp.zeros_like(acc_pa)
            acc_pc[...] = jnp.zeros_like(acc_pc)
            acc_o[...] = jnp.zeros_like(acc_o)
            small_ref[...] = jnp.zeros_like(small_ref)

        @pl.when(i % per_seq == 0)
        def _():
            dgate_ref[...] = jnp.zeros_like(dgate_ref)

        halves = [slice(k * (tm // 2), (k + 1) * (tm // 2)) for k in range(2)]
        gate = mod_ref[:, 2 * d:3 * d]
        a_bf = [a_ref[rs, :] for rs in halves]
        b_bf = [b_ref[rs, :] for rs in halves]
        y_attn = [jnp.dot(a, wpa_ref[...], preferred_element_type=F32) for a in a_bf]
        y_conv = [jnp.dot(b, wpc_ref[...], preferred_element_type=F32) for b in b_bf]
        sa = [_sigmoid(ga_ref[rs, :].astype(F32)) for rs in halves]
        sb = [_sigmoid(gb_ref[rs, :].astype(F32)) for rs in halves]
        merged = [(sa[k] * y_attn[k] + sb[k] * y_conv[k]).astype(BF16) for k in range(2)]
        mo = [jnp.dot(m, wo_ref[...], preferred_element_type=F32) + bo_ref[...] for m in merged]
        d_mo_bf = []
        for k, rs in enumerate(halves):
            r = ALPHA * x_ref[rs, :] + gate * mo[k]
            mu = jnp.mean(r, axis=-1, keepdims=True)
            cen = r - mu
            var = jnp.mean(cen * cen, axis=-1, keepdims=True)
            rstd = lax.rsqrt(var + LN_EPS)
            xhat = cen * rstd
            err = xhat * lg_ref[...] + lb_ref[...] - tg_ref[rs, :]
            dy = err * (1.0 / d)
            dxhat = dy * lg_ref[...]
            dr = rstd * (dxhat - jnp.mean(dxhat, axis=-1, keepdims=True)
                         - xhat * jnp.mean(dxhat * xhat, axis=-1, keepdims=True))
            gx_ref[rs, :] = ALPHA * dr
            dgate_ref[...] += jnp.sum(dr * mo[k], axis=0, keepdims=True)
            d_mo = dr * gate
            small_ref[0:1, :] += jnp.sum(d_mo, axis=0, keepdims=True)
            small_ref[1:2, :] += jnp.sum(dy * xhat, axis=0, keepdims=True)
            small_ref[2:3, :] += jnp.sum(dy, axis=0, keepdims=True)
            small_ref[3:4, :] += jnp.sum(err * err, axis=0, keepdims=True)
            d_mo_bf.append(d_mo.astype(BF16))
        acc_o[...] += tn(jnp.concatenate(merged, axis=0), jnp.concatenate(d_mo_bf, axis=0))
        dmerged = [nt(g, wo_ref[...]) for g in d_mo_bf]
        dy_attn, dy_conv = [], []
        for k, rs in enumerate(halves):
            dy_attn.append((dmerged[k] * sa[k]).astype(BF16))
            dy_conv.append((dmerged[k] * sb[k]).astype(BF16))
            dpg_ref[rs, 0:d] = (dmerged[k] * y_attn[k] * sa[k] * (1.0 - sa[k])).astype(BF16)
            dpg_ref[rs, d:2 * d] = (dmerged[k] * y_conv[k] * sb[k] * (1.0 - sb[k])).astype(BF16)
        acc_pa[...] += tn(a_ref[...], jnp.concatenate(dy_attn, axis=0))
        acc_pc[...] += tn(b_ref[...], jnp.concatenate(dy_conv, axis=0))
        for k, rs in enumerate(halves):
            da_ref[rs, :] = nt(dy_attn[k], wpa_ref[...])
            db_ref[rs, :] = nt(dy_conv[k], wpc_ref[...])

        @pl.when(i == n_steps - 1)
        def _():
            copies = [pltpu.make_async_copy(acc_pa, gwpa_hbm, sem.at[0]), pltpu.make_async_copy(acc_pc, gwpc_hbm, sem.at[1]),
                      pltpu.make_async_copy(acc_o, gwo_hbm, sem.at[2])]
            for cp in copies:
                cp.start()
            for cp in copies:
                cp.wait()

    row = lambda w: pl.BlockSpec((tm, w), lambda i: (i, 0))
    const = lambda shp: pl.BlockSpec(shp, lambda i: (0,) * len(shp), pipeline_mode=pl.Buffered(1))
    any_spec = pl.BlockSpec(memory_space=pl.ANY)
    return pl.pallas_call(
        body, name="tail", grid=(n_steps,),
        in_specs=[row(Z_WIDTH), row(d),
                  pl.BlockSpec((tm, d), lambda i: (i, gate_blk)), pl.BlockSpec((tm, d), lambda i: (i, gate_blk + 1)),
                  row(d), row(d), pl.BlockSpec((None, 1, 3 * d), lambda i: (i // per_seq, 0, 0)),
                  const((Z_WIDTH, d)), const((d, d)), const((d, d)), const((1, d)), const((1, d)), const((1, d))],
        out_specs=[pl.BlockSpec((tm, 2 * d), lambda i: (i, lay.g0 // (2 * d))), row(Z_WIDTH), row(d), row(d),
                   pl.BlockSpec((None, 1, d), lambda i: (i // per_seq, 0, 0)), pl.BlockSpec((8, d), lambda i: (0, 0)),
                   HBM_SPEC, HBM_SPEC, HBM_SPEC],
        out_shape=[jax.ShapeDtypeStruct((t, lay.np), BF16), jax.ShapeDtypeStruct((t, Z_WIDTH), F32),
                   jax.ShapeDtypeStruct((t, d), F32), jax.ShapeDtypeStruct((t, d), F32),
                   jax.ShapeDtypeStruct((t // seq, 1, d), F32), jax.ShapeDtypeStruct((8, d), F32),
                   pltpu.HBM((Z_WIDTH, d), F32), pltpu.HBM((d, d), F32), pltpu.HBM((d, d), F32)],
        scratch_shapes=[pltpu.VMEM((Z_WIDTH, d), F32), pltpu.VMEM((d, d), F32), pltpu.VMEM((d, d), F32),
                        pltpu.SemaphoreType.DMA((3,))],
        compiler_params=_params("arbitrary"),
    )(a_in, b_in, pr, pr, x2, target2, mod3, w_pa, w_pc, w_out, b_out, ln_g, ln_b)


def _conv_bwd(dproj, db, pr, conv_w, bsz, seq, lay):
    d = lay.d
    ct = CONV_TILE
    base = lay.c0 // (4 * ct)

    def body(dp_in, db_ref, p_ref, cw_ref, dp_ref, gcw_ref):
        del dp_in
        u_x, g_b, g_c, z = [p_ref[:, k * ct:(k + 1) * ct].astype(F32) for k in range(4)]
        cw = cw_ref[...]
        u = g_c * u_x
        rows = lax.broadcasted_iota(jnp.int32, u.shape, 0)
        u1, u2 = _shift_down(u, 1, rows), _shift_down(u, 2, rows)
        conv = cw[0:1, :] * u2 + cw[1:2, :] * u1 + cw[2:3, :] * u
        sig = _sigmoid(z)
        sl = z * sig
        dbv = db_ref[...]
        gbc = g_b * conv
        dp_ref[:, ct:2 * ct] = (dbv * sl * conv).astype(BF16)
        dp_ref[:, 3 * ct:4 * ct] = (dbv * gbc * (sig * (1.0 + z * (1.0 - sig)))).astype(BF16)
        dconv = dbv * sl * g_b

        @pl.when(pl.program_id(1) == 0)
        def _():
            gcw_ref[...] = jnp.zeros_like(gcw_ref)

        gcw_ref[0:1, :] += jnp.sum(dconv * u2, axis=0, keepdims=True)
        gcw_ref[1:2, :] += jnp.sum(dconv * u1, axis=0, keepdims=True)
        gcw_ref[2:3, :] += jnp.sum(dconv * u, axis=0, keepdims=True)
        du = cw[2:3, :] * dconv + cw[1:2, :] * _shift_up(dconv, 1, rows) + cw[0:1, :] * _shift_up(dconv, 2, rows)
        dp_ref[:, 0:ct] = (du * g_c).astype(BF16)
        dp_ref[:, 2 * ct:3 * ct] = (du * u_x).astype(BF16)

    return pl.pallas_call(
        body, name="conv_bwd", grid=(d // ct, bsz),
        in_specs=[pl.BlockSpec(memory_space=pl.ANY), pl.BlockSpec((seq, ct), lambda j, b: (b, j)),
                  pl.BlockSpec((seq, 4 * ct), lambda j, b: (b, j)), pl.BlockSpec((3, ct), lambda j, b: (0, j))],
        out_specs=[pl.BlockSpec((seq, 4 * ct), lambda j, b: (b, base + j)), pl.BlockSpec((8, ct), lambda j, b: (0, j))],
        out_shape=[jax.ShapeDtypeStruct(dproj.shape, BF16), jax.ShapeDtypeStruct((8, d), F32)],
        input_output_aliases={0: 0}, compiler_params=_params("arbitrary", "arbitrary"))(dproj, db, pr, conv_w)


def _attn_bwd(dproj, pa, o_all, lse_all, da, bsz, seq, after):
    n_blocks = seq // SUB
    chunk = 256
    per_trip = BLOCKS_PER_TRIP_BWD

    def body(dp_in, pa_ref, o_ref, lse_ref, da_ref, after_ref, dp_ref, sub, stage, dsub, dog, cvec, bias_ref,
             s_buf, dp_buf, ds_buf, pb_buf, q2_buf, do2_buf, l_buf, c_buf):
        del dp_in, after_ref
        p = pl.program_id(1)
        _fill_bias(bias_ref, p, seq)
        head0 = lax.broadcasted_iota(jnp.int32, (SUB, SLAB), 1) < HEAD_DIM

        def mix_bwd(i, carry):
            rows = pl.ds(pl.multiple_of(i * chunk, chunk), chunk)
            ls = [lse_ref[g, rows, :] for g in range(N_GROUPS)]
            os_ = [o_ref[g, rows, :] for g in range(N_GROUPS)]
            m = jnp.maximum(jnp.maximum(ls[0], ls[1]), ls[2])
            es = [jnp.exp(l - m) for l in ls]
            tot = es[0] + es[1] + es[2]
            ws = [e / tot for e in es]
            o = ws[0] * os_[0] + ws[1] * os_[1] + ws[2] * os_[2]
            z = pa_ref[rows, 9 * SLAB:10 * SLAB].astype(F32)
            sig = _sigmoid(z)
            dav = da_ref[rows, :]
            do = dav * (z * sig)
            dp_ref[rows, 9 * SLAB:10 * SLAB] = (dav * o * (sig * (1.0 + z * (1.0 - sig)))).astype(BF16)
            wsum = _head_sums(do * o)
            for g in range(N_GROUPS):
                dog[g, rows, :] = ws[g] * do
                cvec[g, rows, :] = -(ws[g] * wsum)
            return carry

        lax.fori_loop(0, seq // chunk, mix_bwd, 0)

        for g in range(N_GROUPS):
            dil = DILATIONS[g]
            for w in range(3):
                _to_sub_major(pa_ref, 3 * w + g, sub.at[w], stage, dil, seq)
            dsub[1] = jnp.zeros((seq, SLAB), F32)
            dsub[2] = jnp.zeros((seq, SLAB), F32)
            nk = _key_rows(g, seq)

            def trip(i, carry, g=g, dil=dil, nk=nk):
                places = [_block_rows(per_trip * i + j, g, seq) for j in range(per_trip)]
                for j, (row0, krow0, _, nat) in enumerate(places):
                    q = sub[0, pl.ds(row0, SUB), :]
                    do = _ld_rows(dog.at[g], nat, SUB, dil).astype(BF16)
                    zero = jnp.zeros_like(q)
                    q2 = jnp.concatenate([jnp.where(head0, q, zero), jnp.where(head0, zero, q)], axis=0)
                    do2 = jnp.concatenate([jnp.where(head0, do, zero), jnp.where(head0, zero, do)], axis=0)
                    q2_buf[j] = q2
                    do2_buf[j] = do2
                    s_buf[j, :, 0:nk] = _nt(q2 * (HEAD_DIM ** -0.5), sub[1, pl.ds(krow0, nk), :])
                    dp_buf[j, :, 0:nk] = _nt(do2, sub[2, pl.ds(krow0, nk), :])
                    l_buf[j] = _ld_rows(lse_ref.at[g], nat, SUB, dil)
                    c_buf[j] = _ld_rows(cvec.at[g], nat, SUB, dil)
                for j, (_, _, bi, _) in enumerate(places):
                    for c in range(0, 2 * SUB, SOFTMAX_ROWS):
                        hh, r = divmod(c, SUB)
                        lane = hh * HEAD_DIM
                        s = s_buf[j, c:c + SOFTMAX_ROWS, 0:nk] + bias_ref[g, bi, hh, r:r + SOFTMAX_ROWS, 0:nk]
                        prob = jnp.exp(s - l_buf[j, r:r + SOFTMAX_ROWS, lane:lane + 1])
                        dprob = dp_buf[j, c:c + SOFTMAX_ROWS, 0:nk] + c_buf[j, r:r + SOFTMAX_ROWS, lane:lane + 1]
                        ds_buf[j, c:c + SOFTMAX_ROWS, 0:nk] = (prob * dprob * (HEAD_DIM ** -0.5)).astype(BF16)
                        pb_buf[j, c:c + SOFTMAX_ROWS, 0:nk] = prob.astype(BF16)
                for j, (row0, krow0, _, _) in enumerate(places):
                    ds = ds_buf[j, :, 0:nk]
                    dq2 = jnp.dot(ds, sub[1, pl.ds(krow0, nk), :], preferred_element_type=F32)
                    dsub[0, pl.ds(row0, SUB), :] = jnp.where(head0, dq2[0:SUB], dq2[SUB:2 * SUB])
                    dsub[1, pl.ds(krow0, nk), :] += _tn(ds, q2_buf[j])
                    dsub[2, pl.ds(krow0, nk), :] += _tn(pb_buf[j, :, 0:nk], do2_buf[j])
                return carry

            lax.fori_loop(0, n_blocks // per_trip, trip, 0)
            for w in range(3):
                cols = slice((3 * w + g) * SLAB, (3 * w + g + 1) * SLAB)
                if dil == 1:
                    dp_ref[:, cols] = dsub[w].astype(BF16)
                else:
                    n = seq // dil
                    for r in range(dil):
                        stage[pl.ds(r, n, stride=dil), :] = dsub[w, pl.ds(r * n, n), :]
                    dp_ref[:, cols] = stage[...].astype(BF16)

    return pl.pallas_call(
        body, name="attn_bwd", grid=(bsz, 2),
        in_specs=[pl.BlockSpec(memory_space=pl.ANY), pl.BlockSpec((seq, PAIR_COLS), lambda b, p: (b, p)),
                  pl.BlockSpec((N_GROUPS, seq, SLAB), lambda b, p: (0, b, p)),
                  pl.BlockSpec((N_GROUPS, seq, SLAB), lambda b, p: (0, b, p)),
                  pl.BlockSpec((seq, SLAB), lambda b, p: (b, p)), pl.BlockSpec(memory_space=pl.ANY)],
        out_specs=pl.BlockSpec((seq, PAIR_COLS), lambda b, p: (b, p)),
        out_shape=jax.ShapeDtypeStruct(dproj.shape, BF16), input_output_aliases={0: 0},
        scratch_shapes=[pltpu.VMEM((3, seq, SLAB), BF16), pltpu.VMEM((seq, SLAB), F32), pltpu.VMEM((3, seq, SLAB), F32),
                        pltpu.VMEM((3, seq, SLAB), F32), pltpu.VMEM((3, seq, SLAB), F32),
                        pltpu.VMEM((N_GROUPS, 2, 2, SUB, 2 * SUB), F32),
                        pltpu.VMEM((per_trip, 2 * SUB, 2 * SUB), F32), pltpu.VMEM((per_trip, 2 * SUB, 2 * SUB), F32),
                        pltpu.VMEM((per_trip, 2 * SUB, 2 * SUB), BF16), pltpu.VMEM((per_trip, 2 * SUB, 2 * SUB), BF16),
                        pltpu.VMEM((per_trip, 2 * SUB, SLAB), BF16), pltpu.VMEM((per_trip, 2 * SUB, SLAB), BF16),
                        pltpu.VMEM((per_trip, SUB, SLAB), F32), pltpu.VMEM((per_trip, SUB, SLAB), F32)],
        compiler_params=_params("arbitrary", "arbitrary"))(dproj, pa, o_all, lse_all, da, after)


def _grad_h(dproj, w_all, gx0, x2, mod3, seq, lay):
    t, d = x2.shape
    tm, tn = min(512, seq), min(512, d)
    per_seq = seq // tm

    def body(dp_ref, w_ref, gx0_ref, x_ref, scale_ref, gx_ref, dmod_ref):
        dh = _nt(dp_ref[:, 0:ATT], w_ref[:, 0:ATT]) + _nt(dp_ref[:, lay.c0:], w_ref[:, lay.c0:])
        gx_ref[...] = gx0_ref[...] + dh * (1.0 + scale_ref[...])

        @pl.when(pl.program_id(1) % per_seq == 0)
        def _():
            dmod_ref[...] = jnp.zeros_like(dmod_ref)

        dmod_ref[0:1, :] += jnp.sum(dh, axis=0, keepdims=True)
        dmod_ref[1:2, :] += jnp.sum(dh * x_ref[...], axis=0, keepdims=True)

    tile = pl.BlockSpec((tm, tn), lambda j, i: (i, j))
    return pl.pallas_call(
        body, name="grad_h", grid=(d // tn, t // tm),
        in_specs=[pl.BlockSpec((tm, lay.np), lambda j, i: (i, 0)), pl.BlockSpec((tn, lay.np), lambda j, i: (j, 0)),
                  tile, tile,
                  pl.BlockSpec((None, 1, tn), lambda j, i: (i // per_seq, 0, d // tn + j))],
        out_specs=[tile, pl.BlockSpec((None, 8, tn), lambda j, i: (i // per_seq, 0, j))],
        out_shape=[jax.ShapeDtypeStruct((t, d), F32), jax.ShapeDtypeStruct((t // seq, 8, d), F32)],
        compiler_params=_params("arbitrary", "arbitrary"))(dproj, w_all, gx0, x2, mod3)


def _grad_w_in(ht, dproj, seq, lay, part, prev):
    d, t = ht.shape
    tm = seq
    n_i = t // tm

    def make_body(n_skip, n_pieces, tn, nat_tile):
        def body(*refs):
            refs = refs[n_skip:]
            ht_ref, dp_refs = refs[0], refs[1:1 + n_pieces]
            gw_hbm, gb_hbm, acc, bacc, gw_out, gb_out, sem = refs[1 + n_pieces:]
            i, j = pl.program_id(0), pl.program_id(1)
            dp = dp_refs[0][...] if n_pieces == 1 else jnp.concatenate([r[...] for r in dp_refs], axis=1)
            part = jnp.dot(ht_ref[...], dp, preferred_element_type=F32)
            bpart = jnp.sum(dp.astype(F32), axis=0, keepdims=True)

            if n_i > 1:
                @pl.when(i == 0)
                def _():
                    acc[j] = part
                    bacc[j] = bpart

                @pl.when((i > 0) & (i < n_i - 1))
                def _():
                    acc[j] += part
                    bacc[j] += bpart

            @pl.when(i == n_i - 1)
            def _():
                gw_out[...] = ((part + acc[j]) if n_i > 1 else part).astype(BF16)
                gb_out[...] = (bpart + bacc[j]) if n_i > 1 else bpart
                cols = pl.ds(pl.multiple_of(nat_tile(j) * tn, SLAB), tn)
                copies = [pltpu.make_async_copy(gw_out, gw_hbm.at[:, cols], sem.at[0]),
                          pltpu.make_async_copy(gb_out, gb_hbm.at[:, cols], sem.at[1])]
                for cp in copies:
                    cp.start()
                for cp in copies:
                    cp.wait()
        return body

    def call(name, pieces, n_tiles, nat_tile, prev):
        tn = sum(w for w, _ in pieces)
        any_spec = pl.BlockSpec(memory_space=pl.ANY)
        in_specs = [pl.BlockSpec((d, tm), lambda i, j: (0, i))]
        in_specs += [pl.BlockSpec((tm, w), lambda i, j, f=f: (i, f(j))) for w, f in pieces]
        args = [ht] + [dproj] * len(pieces)
        aliases = {}
        if prev is not None:
            in_specs = [any_spec] * 2 + in_specs
            args = list(prev) + args
            aliases = {0: 0, 1: 1}
        return pl.pallas_call(
            make_body(0 if prev is None else 2, len(pieces), tn, nat_tile), name=name, grid=(n_i, n_tiles), in_specs=in_specs,
            out_specs=[any_spec, any_spec],
            out_shape=[jax.ShapeDtypeStruct((d, lay.din), BF16), jax.ShapeDtypeStruct((1, lay.din), F32)],
            input_output_aliases=aliases,
            scratch_shapes=[pltpu.VMEM((n_tiles, d, tn), F32), pltpu.VMEM((n_tiles, 1, tn), F32), pltpu.VMEM((d, tn), BF16),
                            pltpu.VMEM((1, tn), F32), pltpu.SemaphoreType.DMA((2,))],
            compiler_params=_params("arbitrary", "arbitrary"))(*args)

    if part == "attn":
        attn_pieces = [(SLAB, lambda j, m=m: (m % 2) * PAIR_SLABS + 2 * j + m // 2) for m in range(4)]
        return call("grad_w_in_attn", attn_pieces, ATT // 512, lambda j: j, prev)
    base = lay.c0 // CONV_TILE
    nct = lay.n_conv_tiles
    if nct % 2:
        return call("grad_w_in_rest", [(CONV_TILE, lambda j: base + j)], 6 * d // CONV_TILE, lay.rest_nat_tile, prev)
    half = nct // 2

    def rest_piece(m):
        def perm_tile(j):
            conv = base + 4 * (2 * (j % half) + m) + j // half
            return jnp.where(j < 4 * half, conv, base + 2 * j + m)
        return (CONV_TILE, perm_tile)

    return call("grad_w_in_rest", [rest_piece(0), rest_piece(1)], 6 * d // 512, lambda j: ATT // 512 + j, prev)


def _pack_rows(parts, width=128):
    flat = [p.reshape(-1) for p in parts]
    spans, rows = [], 0
    padded = []
    for f in flat:
        n = -(-f.shape[0] // (8 * width)) * 8
        padded.append(jnp.pad(f, (0, n * width - f.shape[0])).reshape(n, width))
        spans.append((rows, f.shape[0]))
        rows += n
    return jnp.concatenate(padded, axis=0), spans


def _unpack_rows(packed, spans, shapes, width=128):
    out = []
    for (row, n), shp in zip(spans, shapes):
        rows = -(-n // width)
        out.append(packed[row:row + rows].reshape(-1)[:n].reshape(shp))
    return out


def kernel(x, c, w_ada, b_ada, w_in, b_in, conv_w, w_proj_attn, w_proj_conv, w_out, b_out, ln_g, ln_b, loss_target, m_w_ada, m_b_ada, m_w_in, m_b_in, m_conv_w, m_w_proj_attn, m_w_proj_conv, m_w_out, m_b_out, m_ln_g, m_ln_b, v_w_ada, v_b_ada, v_w_in, v_b_in, v_conv_w, v_w_proj_attn, v_w_proj_conv, v_w_out, v_b_out, v_ln_g, v_ln_b):
    bsz, seq, d = x.shape
    t = bsz * seq
    lay = _Layout(d)
    col_sharded = [True, True, False, False]
    red_w = [w_in[0], w_proj_attn[0], w_proj_conv[0], w_out[0]]
    chip = 2 * lax.axis_index("x") + lax.axis_index("y")
    chip1 = chip.astype(jnp.int32).reshape(1)
    core1 = lax.axis_index("c").astype(jnp.int32).reshape(1)
    place = jnp.stack([chip, lax.axis_index("c")]).astype(jnp.int32)
    x2 = x.reshape(t, d)
    target2 = loss_target.reshape(t, d)

    mod, act_all = _ada_exchange(c, w_ada[0], b_ada)
    mod3 = mod.reshape(bsz, 1, 3 * d)

    cw_pad = jnp.pad(conv_w[0], ((0, 5), (0, 0))) + 0.0 * mod[0, 0]
    own_in_full = [_cast_into_full(red_w[0], col_sharded[0], chip1, "cast_shard_in")]
    own_in_full += list(_cast_into_full_small(red_w[1:], col_sharded[1:], chip1, "cast_shards_late"))
    (wi_f,), cw8 = _gather_weights(own_in_full[:1], col_sharded[:1], cw_pad)
    cw_full = cw8[0:3]
    late_copies = _direct_gather_copies(col_sharded[1:])
    late_send, late_recv, late_flying, late_token = _start_copies("gather_late_start", own_in_full[1:], (18,), cw8, late_copies)
    w_all = _permute_w_in(wi_f, lay)
    b_all = lay.perm_vector(b_in) + late_token[0, 0]

    rest_tn = 1024 if (6 * d) % 1024 == 0 else 512
    pa, = _project(x2, mod3, w_all, b_all, seq, 0, ATT, PAIR_COLS, BF16, False, "project_attn")
    pr, ht = _project(x2, mod3, w_all, b_all, seq, lay.c0, 6 * d, rest_tn, BF16, True, "project_rest")
    o_all, lse_all, a_in = _attn_fwd(pa, bsz, seq)
    b_in_act = _conv_fwd(pr, cw_full, bsz, seq, d)
    wpa_f, wpc_f, wo_f = _wait_copies("gather_late_wait", late_flying, late_send, late_recv, b_in_act, late_copies)
    (dproj, da_in, db_in, gx0, dgate, small_tail, gw_pa, gw_pc, gw_out) = _tail(
        a_in, b_in_act, pr, x2, target2, mod3, wpa_f, wpc_f, wo_f, b_out, ln_g, ln_b, seq, lay)

    late_views = _shard_views([gw_pa, gw_pc, gw_out], col_sharded[1:])
    late_lands = [lax.empty((v.shape[0], v.shape[1] // 2, v.shape[2]), v.dtype) for v in late_views]
    xl_send, xl_recv, xl_fly, xl_tok = _start_copies("grads_pair_exchange_late_start", late_views + late_lands, (3,), small_tail,
                                                     _pair_exchange_copies(3, False))
    dproj, gcw = _conv_bwd(dproj, db_in, pr, cw_full + xl_tok[0, 0], bsz, seq, lay)
    xl_done = _wait_copies("grads_pair_exchange_late_wait", xl_fly, xl_send, xl_recv, gcw, _pair_exchange_copies(3, True))
    late_parts = list(_pair_sum_small(xl_done[:3], xl_done[3:], core1, "grads_pair_sum_late"))

    late_cross = _chip_scatter_copies(3, col_sharded[1:])
    late_zone = [lax.empty((3, p.shape[1], _piece_cols(p, cs)), p.dtype) for p, cs in zip(late_parts, col_sharded[1:])]
    sl_send, sl_recv, sl_fly, sl_tok = _start_copies("grads_scatter_late_start", late_parts + late_zone, (9,), core1,
                                                     late_cross)
    dproj = _attn_bwd(dproj, pa, o_all, lse_all, da_in, bsz, seq, sl_tok)
    gw_in_bf, gb_in = _grad_w_in(ht, dproj, seq, lay, "rest", None)
    rest_cols, attn_cols = (ATT, 6 * d), (0, ATT)
    in_land = lax.empty((1, d // 2, lay.din), BF16)
    xi_copies = _pair_exchange_copies(1, False, rest_cols)
    xi_send, xi_recv, xi_fly, _ = _start_copies("grads_pair_exchange_in_start", [gw_in_bf.reshape(1, d, lay.din), in_land], (1,),
                                                gb_in, xi_copies)
    gw_in_bf, gb_in = _grad_w_in(ht, dproj, seq, lay, "attn", (xi_fly[0].reshape(d, lay.din), gb_in))
    xi_done = _wait_copies("grads_pair_exchange_in_wait", [gw_in_bf.reshape(1, d, lay.din), xi_fly[1]], xi_send, xi_recv, gb_in,
                           _pair_exchange_copies(1, True, rest_cols))
    in_got = _pair_exchange_into(xi_done[0], xi_done[1], attn_cols, "grads_pair_exchange_in_attn")
    sl_done = _wait_copies("grads_scatter_late_wait", sl_fly, sl_send, sl_recv, in_got, late_cross)
    late_red = list(_chip_sum_small(sl_done[:3], sl_done[3:], col_sharded[1:], place, "grads_chip_sum_late"))

    in_part = _pair_sum(xi_done[0], in_got, core1, "grads_pair_sum_in")
    in_cross = _chip_scatter_copies(1, col_sharded[:1])
    in_zone = [lax.empty((3, in_part.shape[1], _piece_cols(in_part, True)), in_part.dtype)]
    late_views1 = [f.reshape(1, *f.shape) for f in late_red]
    si_send, si_recv, si_fly, si_tok = _start_copies(
        "grads_scatter_in_join_late_start", [in_part] + in_zone + late_views1, (6,), core1,
        _both_copies(in_cross, 2, _pair_join_copies(3, False, sem0=3)))
    grad_x2, dmod = _grad_h(dproj, w_all, gx0, x2, mod3 + si_tok[0, 0], seq, lay)
    si_done = _wait_copies("grads_scatter_in_join_late_wait", si_fly, si_send, si_recv, grad_x2,
                           _both_copies(in_cross, 2, _pair_join_copies(3, True, sem0=3)))
    late_joined = [f[0] for f in si_done[2:]]
    in_red = _chip_sum(si_done[0], si_done[1], True, place, "grads_chip_sum_in")

    d_ada = jnp.concatenate([dmod[:, 0, :], dmod[:, 1, :], dgate[:, 0, :]], axis=1)
    pieces = [small_tail[3], jnp.sum(d_ada, axis=0), gb_in[0], small_tail[0], small_tail[1], small_tail[2], gcw[0:3]]
    packed, spans = _pack_rows(pieces)
    kept_in, _ = _pack_rows([d_ada])
    rows_all = jnp.concatenate([packed, kept_in], axis=0)
    small_land = lax.empty((N_DEV,) + rows_all.shape, F32)
    sm_send, sm_recv, sm_fly, sm_tok = _start_copies(
        "small_gather_join_in_start", [rows_all, small_land, in_red.reshape(1, *in_red.shape)], (N_DEV,), core1,
        _both_copies(_small_gather_copies, 2, _pair_join_copies(1, False, sem0=N_DEV - 1)))
    big_w = [w_ada[0]] + red_w
    big_m = [m_w_ada[0], m_w_in[0], m_w_proj_attn[0], m_w_proj_conv[0], m_w_out[0]]
    big_v = [v_w_ada[0], v_w_in[0], v_w_proj_attn[0], v_w_proj_conv[0], v_w_out[0]]
    big_out = [None] * 5
    for w in range(3):
        big_out[2 + w] = _adamw(big_w[2 + w], late_joined[w], big_m[2 + w], big_v[2 + w], f"adamw_{2 + w}", sm_tok)
    sm_done = _wait_copies("small_gather_join_in_wait", sm_fly, sm_send, sm_recv, big_out[4][0],
                           _both_copies(_small_gather_copies, 2, _pair_join_copies(1, True, sem0=N_DEV - 1)))
    me1 = (4 * lax.axis_index("x") + 2 * lax.axis_index("y") + lax.axis_index("c")).astype(jnp.int32).reshape(1)
    summed, kept = _small_sum(sm_done[0], sm_done[1], me1, packed.shape[0], d)
    loss = summed[0, 0]
    _, g_b_ada, g_b_in, g_b_out, g_ln_g, g_ln_b, g_cw_full = _unpack_rows(
        summed, spans, [(d,), (3 * d,), (lay.din,), (d,), (d,), (d,), (3, d)])
    g_cw = lax.dynamic_slice(g_cw_full, (0, chip * (d // N_CHIPS)), (3, d // N_CHIPS))
    d_ada_all = kept.reshape(N_DEV, -1)[:, :bsz * 3 * d].reshape(N_DEV * bsz, 3 * d)
    ada_cols = 3 * d // N_CHIPS
    d_ada_cols = lax.dynamic_slice(d_ada_all, (0, chip * ada_cols), (N_DEV * bsz, ada_cols))

    g_w_ada, *big_out[0] = _grad_and_adamw_w_ada(act_all.T, d_ada_cols, big_w[0], big_m[0], big_v[0])
    small_w = [b_ada, b_in, conv_w[0], b_out, ln_g, ln_b]
    small_g = [g_b_ada, g_b_in, g_cw, g_b_out, g_ln_g, g_ln_b]
    small_m = [m_b_ada, m_b_in, m_conv_w[0], m_b_out, m_ln_g, m_ln_b]
    small_v = [v_b_ada, v_b_in, v_conv_w[0], v_b_out, v_ln_g, v_ln_b]
    pw, sp = _pack_rows(small_w)
    pg, _ = _pack_rows(small_g)
    pm, _ = _pack_rows(small_m)
    pv, _ = _pack_rows(small_v)
    sd, sm, sv = _adamw(pw, pg, pm, pv, "adamw_small", None)
    big_out[1] = _adamw(big_w[1], sm_done[2][0], big_m[1], big_v[1], "adamw_1", None, also_g=True)
    g_big = [g_w_ada, big_out[1][3]] + late_joined
    shapes = [a.shape for a in small_w]
    sd, sm, sv = _unpack_rows(sd, sp, shapes), _unpack_rows(sm, sp, shapes), _unpack_rows(sv, sp, shapes)

    def order(wa, bA, wi, bI, cw, wpa, wpc, wo, bO, lg, lb):
        return (wa[None], bA, wi[None], bI, cw[None], wpa[None], wpc[None], wo[None], bO, lg, lb)

    sg = [g.reshape(s) for g, s in zip(small_g, shapes)]
    grads_out = order(g_big[0], sg[0], g_big[1], sg[1], sg[2], g_big[2], g_big[3], g_big[4], sg[3], sg[4], sg[5])
    outs = []
    for idx, small in enumerate((sd, sm, sv)):
        outs.append(order(big_out[0][idx], small[0], big_out[1][idx], small[1], small[2], big_out[2][idx],
                          big_out[3][idx], big_out[4][idx], small[3], small[4], small[5]))
    return (loss, grad_x2.reshape(bsz, seq, d), *grads_out, *outs[0], *outs[1], *outs[2])
```

```python
import jax
import jax.numpy as jnp
from jax import lax
from jax.experimental import pallas as pl
from jax.experimental.pallas import tpu as pltpu

F32 = jnp.float32
BF16 = jnp.bfloat16
MESH = pl.DeviceIdType.MESH

HEAD_DIM = 64
N_GROUPS = 3
DILATIONS = (1, 4, 16)
N_HEADS = 12
SUB = 128
Q_WIDTH = 768
Z_WIDTH = 256
ATT = 3 * Q_WIDTH + Z_WIDTH
SLAB = 128
PAIR_SLABS = 10
PAIR_COLS = PAIR_SLABS * SLAB
CONV_TILE = 256
SOFTMAX_ROWS = 32
BLOCKS_PER_TRIP = 8
BLOCKS_PER_TRIP_BWD = 16
ALIBI_MAX_EXP = 8.0
ALPHA = 2.0 ** 0.25
LN_EPS = 1e-5
ADAM_LR, ADAM_B1, ADAM_B2, ADAM_EPS, ADAM_WD, ADAM_STEP = 0.001, 0.9, 0.999, 1e-08, 0.01, 10
N_CHIPS = 4
N_DEV = 8
VMEM_LIMIT_V7X = 60 * 1024 * 1024
NEG = -1e30


def _params(*sem):
    return pltpu.CompilerParams(dimension_semantics=sem, vmem_limit_bytes=VMEM_LIMIT_V7X)


def _sigmoid(v):
    return 0.5 * jnp.tanh(0.5 * v) + 0.5


class _Layout:
    def __init__(self, d):
        self.d = d
        self.din = ATT + 6 * d
        c0 = 3072
        while c0 % (2 * d):
            c0 += 1024
        self.c0, self.g0, self.np = c0, c0 + 4 * d, c0 + 6 * d
        self.n_conv_tiles = d // CONV_TILE

    def attn_nat_slab(self, s):
        p, i = s // PAIR_SLABS, s % PAIR_SLABS
        return jnp.where(i < 9, (i // 3) * 6 + (i % 3) * 2 + p, 18 + p)

    def rest_nat_tile(self, t):
        n4 = 4 * self.n_conv_tiles
        conv = ATT // CONV_TILE + (t % 4) * self.n_conv_tiles + t // 4
        return jnp.where(t < n4, conv, ATT // CONV_TILE + t)

    def perm_vector(self, v):
        parts = []
        for s in range(2 * PAIR_SLABS):
            p, i = divmod(s, PAIR_SLABS)
            ns = (i // 3) * 6 + (i % 3) * 2 + p if i < 9 else 18 + p
            parts.append(v[:, ns * SLAB:(ns + 1) * SLAB])
        parts.append(jnp.zeros((1, self.c0 - ATT), v.dtype))
        for j in range(self.n_conv_tiles):
            for k in range(4):
                a = ATT + k * self.d + j * CONV_TILE
                parts.append(v[:, a:a + CONV_TILE])
        parts.append(v[:, ATT + 4 * self.d:])
        return jnp.concatenate(parts, axis=1)


def _place():
    return lax.axis_index("x"), lax.axis_index("y"), lax.axis_index("c")


def _other_chips(x, y):
    return [(1 - x, y), (x, 1 - y), (1 - x, 1 - y)]


def _shard_of(ref, col_sharded, chip, half=None):
    if col_sharded:
        cs = ref.shape[1] // N_CHIPS
        cols = pl.ds(pl.multiple_of(chip * cs, SLAB), cs)
        if half is None:
            return ref.at[:, cols]
        n = ref.shape[0] // 2
        return ref.at[pl.ds(half * n, n), cols]
    rs = ref.shape[0] // N_CHIPS
    if half is None:
        return ref.at[pl.ds(chip * rs, rs)]
    return ref.at[pl.ds(chip * rs + half * (rs // 2), rs // 2)]


GATHER_PIECES = 8


def _cast_into_full(shard, col_sharded, chip, name):
    rows, cols = shard.shape
    tr = _row_tile(rows, cols)
    nb = rows // tr

    def body(chip_ref, s_ref, o_ref):
        del chip_ref
        o_ref[...] = s_ref[...].astype(BF16)

    if col_sharded:
        full, out_spec = (rows, cols * N_CHIPS), pl.BlockSpec((tr, cols), lambda i, ch: (i, ch[0]))
    else:
        full, out_spec = (rows * N_CHIPS, cols), pl.BlockSpec((tr, cols), lambda i, ch: (ch[0] * nb + i, 0))
    return pl.pallas_call(
        body, name=name,
        grid_spec=pltpu.PrefetchScalarGridSpec(num_scalar_prefetch=1, grid=(nb,),
                                               in_specs=[pl.BlockSpec((tr, cols), lambda i, ch: (i, 0))], out_specs=out_spec),
        out_shape=jax.ShapeDtypeStruct(full, BF16), compiler_params=_params("parallel"))(chip, shard)


def _cast_into_full_small(shards, col_sharded, chip, name):
    n = len(shards)

    def body(chip_ref, *refs):
        del chip_ref
        for w in range(n):
            refs[n + w][...] = refs[w][...].astype(BF16)

    in_specs = [pl.BlockSpec(s.shape, lambda i, ch: (0, 0)) for s in shards]
    out_specs = [pl.BlockSpec(s.shape, (lambda i, ch: (0, ch[0])) if cs else (lambda i, ch: (ch[0], 0)))
                 for s, cs in zip(shards, col_sharded)]
    fulls = [(s.shape[0], s.shape[1] * N_CHIPS) if cs else (s.shape[0] * N_CHIPS, s.shape[1]) for s, cs in zip(shards, col_sharded)]
    return pl.pallas_call(
        body, name=name,
        grid_spec=pltpu.PrefetchScalarGridSpec(num_scalar_prefetch=1, grid=(1,), in_specs=in_specs, out_specs=out_specs),
        out_shape=[jax.ShapeDtypeStruct(f, BF16) for f in fulls], compiler_params=_params("arbitrary"))(chip, *shards)


def _gather_weights(fulls, col_sharded, small):
    n = len(fulls)
    kp = GATHER_PIECES

    def piece(ref, cs, chip, half, k):
        if cs:
            width = ref.shape[1] // N_CHIPS
            rows = ref.shape[0] // 2 // kp
            return ref.at[pl.ds(half * (ref.shape[0] // 2) + k * rows, rows), pl.ds(pl.multiple_of(chip * width, SLAB), width)]
        rs = ref.shape[0] // N_CHIPS
        rows = rs // 2 // kp
        return ref.at[pl.ds(chip * rs + half * (rs // 2) + k * rows, rows)]

    def body(*refs):
        sm_in, outs, sm_out = refs[n], refs[n + 1:2 * n + 1], refs[2 * n + 1]
        send, recv, fsend, frecv, lsem, ssend, srecv = refs[2 * n + 2:]
        x, y, c = _place()
        mine = 2 * x + y
        sibling = (x, y, 1 - c)
        first = (x ^ (1 - c), y ^ c)
        second = (x ^ c, y ^ (1 - c))
        diagonal = (1 - x, 1 - y)
        sources = [first, second, diagonal]
        senders = [first, second, second]

        def copy(ref, sems, slot, to):
            return pltpu.make_async_remote_copy(src_ref=ref, dst_ref=ref, send_sem=sems[0].at[slot], recv_sem=sems[1].at[slot],
                                                device_id=to, device_id_type=MESH)

        local = pltpu.make_async_copy(sm_in, _shard_of(sm_out, True, mine), lsem)
        local.start()
        sends = []
        for k, (cx, cy) in enumerate(_other_chips(x, y)):
            cp = pltpu.make_async_remote_copy(src_ref=sm_in, dst_ref=_shard_of(sm_out, True, mine), send_sem=ssend.at[k],
                                              recv_sem=srecv.at[k], device_id=(cx, cy, c), device_id_type=MESH)
            cp.start()
            sends.append(cp)
        for k in range(kp):
            for w in range(n):
                own = piece(outs[w], col_sharded[w], mine, c, k)
                for slot, chip in enumerate((first, second)):
                    cp = copy(own, (send, recv), (w * 3 + slot) * kp + k, (*chip, c))
                    cp.start()
                    sends.append(cp)
        for slot in range(3):
            source = 2 * sources[slot][0] + sources[slot][1]
            for k in range(kp):
                for w in range(n):
                    landed = piece(outs[w], col_sharded[w], source, c, k)
                    copy(landed, (send, recv), (w * 3 + slot) * kp + k, (*senders[slot], c)).wait_recv()
                    if slot == 0:
                        cp = copy(landed, (send, recv), (w * 3 + 2) * kp + k, (*second, c))
                        cp.start()
                        sends.append(cp)
                    cp = copy(landed, (fsend, frecv), (w * 3 + slot) * kp + k, sibling)
                    cp.start()
                    sends.append(cp)
        for slot, chip in enumerate((second, first, diagonal)):
            for k in range(kp):
                for w in range(n):
                    passed = piece(outs[w], col_sharded[w], 2 * chip[0] + chip[1], 1 - c, k)
                    copy(passed, (fsend, frecv), (w * 3 + slot) * kp + k, sibling).wait_recv()
        for k, (cx, cy) in enumerate(_other_chips(x, y)):
            theirs = _shard_of(sm_out, True, 2 * cx + cy)
            pltpu.make_async_remote_copy(src_ref=theirs, dst_ref=theirs, send_sem=ssend.at[k], recv_sem=srecv.at[k],
                                         device_id=(cx, cy, c), device_id_type=MESH).wait_recv()
        for cp in sends:
            cp.wait_send()
        local.wait()

    any_spec = pl.BlockSpec(memory_space=pl.ANY)
    outs = pl.pallas_call(
        body, name="gather_weights",
        out_shape=[jax.ShapeDtypeStruct(f.shape, BF16) for f in fulls]
        + [jax.ShapeDtypeStruct((small.shape[0], small.shape[1] * N_CHIPS), small.dtype)],
        in_specs=[any_spec] * (n + 1), out_specs=[any_spec] * (n + 1), input_output_aliases={w: w for w in range(n)},
        scratch_shapes=[pltpu.SemaphoreType.DMA((n * 3 * kp,)), pltpu.SemaphoreType.DMA((n * 3 * kp,)),
                        pltpu.SemaphoreType.DMA((n * 3 * kp,)), pltpu.SemaphoreType.DMA((n * 3 * kp,)), pltpu.SemaphoreType.DMA,
                        pltpu.SemaphoreType.DMA((3,)), pltpu.SemaphoreType.DMA((3,))],
    )(*fulls, small)
    return outs[:n], outs[n]


HBM_SPEC = pl.BlockSpec(memory_space=pltpu.HBM)
SEM_SPEC = pl.BlockSpec(memory_space=pltpu.SEMAPHORE)
DATAFLOW = pltpu.SideEffectType.DATAFLOW_SIDE_EFFECTING


def _start_copies(name, arrays, sem_shape, after, copies):
    n = len(arrays)

    def body(*refs):
        for cp in copies(refs[:n], refs[n + 1], refs[n + 2]):
            cp.start()
        token = refs[2 * n + 3]
        token[...] = jnp.zeros_like(token)

    res = pl.pallas_call(
        body, name=name,
        out_shape=(pltpu.SemaphoreType.DMA(sem_shape), pltpu.SemaphoreType.DMA(sem_shape),
                   *[pltpu.HBM(a.shape, a.dtype) for a in arrays], jax.ShapeDtypeStruct((8, 128), F32)),
        in_specs=[HBM_SPEC] * n + [pl.BlockSpec(memory_space=pl.ANY)],
        out_specs=(SEM_SPEC, SEM_SPEC, *([HBM_SPEC] * n), pl.BlockSpec(memory_space=pltpu.VMEM)),
        input_output_aliases={i: 2 + i for i in range(n)},
        compiler_params=pltpu.CompilerParams(has_side_effects=DATAFLOW),
    )(*[pltpu.with_memory_space_constraint(a, pltpu.HBM) for a in arrays], after)
    return res[0], res[1], list(res[2:2 + n]), res[2 + n]


def _wait_copies(name, arrays, send, recv, after, copies):
    n = len(arrays)

    def body(*refs):
        for cp in copies(refs[:n], refs[n], refs[n + 1]):
            cp.wait_send()
            cp.wait_recv()

    return pl.pallas_call(
        body, name=name, out_shape=[pltpu.HBM(a.shape, a.dtype) for a in arrays],
        in_specs=[HBM_SPEC] * n + [SEM_SPEC, SEM_SPEC, pl.BlockSpec(memory_space=pl.ANY)], out_specs=[HBM_SPEC] * n,
        input_output_aliases={i: i for i in range(n)},
        compiler_params=pltpu.CompilerParams(has_side_effects=DATAFLOW),
    )(*arrays, send, recv, after)


def _direct_gather_copies(col_sharded):
    def copies(refs, send, recv):
        x, y, c = _place()
        mine = 2 * x + y
        out = []
        for w, ref in enumerate(refs):
            own_half = _shard_of(ref, col_sharded[w], mine, c)
            k = 0
            for cx, cy in _other_chips(x, y):
                for pc in (c, 1 - c):
                    out.append(pltpu.make_async_remote_copy(
                        src_ref=own_half, dst_ref=own_half, send_sem=send.at[6 * w + k], recv_sem=recv.at[6 * w + k],
                        device_id=(cx, cy, pc), device_id_type=MESH))
                    k += 1
        return out
    return copies


def _chip_scatter_copies(n, col_sharded):
    def piece(ref, cs, chip):
        if cs:
            w = ref.shape[2] // N_CHIPS
            return ref.at[:, :, pl.ds(pl.multiple_of(chip * w, SLAB), w)]
        return ref.at[pl.ds(chip, 1)]

    def copies(refs, send, recv):
        x, y, c = _place()
        out = []
        for k, (cx, cy) in enumerate(_other_chips(x, y)):
            for w in range(n):
                out.append(pltpu.make_async_remote_copy(
                    src_ref=piece(refs[w], col_sharded[w], 2 * cx + cy), dst_ref=refs[n + w].at[pl.ds(k, 1)],
                    send_sem=send.at[3 * w + k], recv_sem=recv.at[3 * w + k], device_id=(cx, cy, c), device_id_type=MESH))
        return out
    return copies


def _shard_views(gs, col_sharded):
    return [g.reshape(1, *g.shape) if cs else g.reshape(N_CHIPS, g.shape[0] // N_CHIPS, g.shape[1])
            for g, cs in zip(gs, col_sharded)]


DMA_CHUNK_BYTES = 1 << 20


def _chunk_rows(shape, itemsize):
    s, rows, cols = shape
    n = 1
    while s * (rows // n) * cols * itemsize > DMA_CHUNK_BYTES and (rows // n) % 32 == 0:
        n *= 2
    return rows // n


def _rows_of(ref, row0, rows, cols):
    if cols is None:
        return ref.at[:, pl.ds(row0, rows)]
    return ref.at[:, pl.ds(row0, rows), pl.ds(cols[0], cols[1])]


def _row_pieces(src, src_row0, dst, dst_row0, rows, send_sem, recv_sem, device, cols=None):
    width = src.shape[2] if cols is None else cols[1]
    step = _chunk_rows((src.shape[0], rows, width), src.dtype.itemsize)
    return [pltpu.make_async_remote_copy(src_ref=_rows_of(src, src_row0 + r, step, cols), dst_ref=_rows_of(dst, dst_row0 + r, step, cols),
                                         send_sem=send_sem, recv_sem=recv_sem, device_id=device, device_id_type=MESH)
            for r in range(0, rows, step)]


def _pair_exchange_copies(n, whole, cols=None):
    def copies(refs, send, recv):
        x, y, c = _place()
        sibling = (x, y, 1 - c)
        out = []
        for w in range(n):
            hr = refs[n + w].shape[1]
            if whole:
                out.append(pltpu.make_async_remote_copy(
                    src_ref=_rows_of(refs[w], (1 - c) * hr, hr, cols), dst_ref=_rows_of(refs[n + w], 0, hr, cols),
                    send_sem=send.at[w], recv_sem=recv.at[w], device_id=sibling, device_id_type=MESH))
            else:
                out += _row_pieces(refs[w], (1 - c) * hr, refs[n + w], 0, hr, send.at[w], recv.at[w], sibling, cols)
        return out
    return copies


def _pair_join_copies(n, whole, sem0=0):
    def copies(refs, send, recv):
        x, y, c = _place()
        sibling = (x, y, 1 - c)
        out = []
        for w in range(n):
            hr = refs[w].shape[1] // 2
            sems = dict(send_sem=send.at[sem0 + w], recv_sem=recv.at[sem0 + w])
            if whole:
                out.append(pltpu.make_async_remote_copy(
                    src_ref=refs[w].at[:, pl.ds(c * hr, hr)], dst_ref=refs[w].at[:, pl.ds((1 - c) * hr, hr)],
                    device_id=sibling, device_id_type=MESH, **sems))
            else:
                out += _row_pieces(refs[w], c * hr, refs[w], c * hr, hr, sems["send_sem"], sems["recv_sem"], sibling)
        return out
    return copies


def _both_copies(first, n_first, second):
    def copies(refs, send, recv):
        return first(refs[:n_first], send, recv) + second(refs[n_first:], send, recv)
    return copies


def _small_gather_copies(refs, send, recv):
    vec, land = refs
    x, y, c = _place()
    me = 4 * x + 2 * y + c
    return [pltpu.make_async_remote_copy(src_ref=vec, dst_ref=land.at[me], send_sem=send.at[k], recv_sem=recv.at[k],
                                         device_id=peer, device_id_type=MESH) for k, (peer, _) in enumerate(_all_devices(x, y, c))]


def _small_sum(vec, land, me, n_sum, d):
    rows = vec.shape[0]

    def body(me_ref, v_ref, l_ref, sum_ref, kept_ref):
        def slot(k):
            return jnp.where(me_ref[0] == k, v_ref[...], l_ref[k])

        total = slot(0)[0:n_sum, :]
        kept_ref[0] = slot(0)[n_sum:rows, :]
        for k in range(1, N_DEV):
            total = total + slot(k)[0:n_sum, :]
            kept_ref[k] = slot(k)[n_sum:rows, :]
        sum_ref[...] = total
        loss = 0.5 / d * jnp.sum(total[0:8, :])
        sum_ref[0:8, :] = jnp.full((8, 128), loss, F32)

    vm = pl.BlockSpec(memory_space=pltpu.VMEM)
    return pl.pallas_call(
        body, name="small_sum", in_specs=[pl.BlockSpec(memory_space=pltpu.SMEM), vm, vm], out_specs=[vm, vm],
        out_shape=[jax.ShapeDtypeStruct((n_sum, 128), F32), jax.ShapeDtypeStruct((N_DEV, rows - n_sum, 128), F32)],
        compiler_params=pltpu.CompilerParams(vmem_limit_bytes=VMEM_LIMIT_V7X))(me, vec, land)


def _pair_exchange_into(view, land, cols, name):
    def body(v_ref, l_in, l_ref, send, recv):
        del l_in
        x, y, c = _place()
        sibling = (x, y, 1 - c)
        hr = l_ref.shape[1]
        for cp in _row_pieces(v_ref, (1 - c) * hr, l_ref, 0, hr, send, recv, sibling, cols):
            cp.start()
        pltpu.make_async_remote_copy(src_ref=_rows_of(v_ref, (1 - c) * hr, hr, cols), dst_ref=_rows_of(l_ref, 0, hr, cols),
                                     send_sem=send, recv_sem=recv, device_id=sibling, device_id_type=MESH).wait()

    any_spec = pl.BlockSpec(memory_space=pl.ANY)
    return pl.pallas_call(
        body, name=name, out_shape=jax.ShapeDtypeStruct(land.shape, land.dtype),
        in_specs=[any_spec, any_spec], out_specs=any_spec, input_output_aliases={1: 0},
        scratch_shapes=[pltpu.SemaphoreType.DMA, pltpu.SemaphoreType.DMA],
    )(view, land)


def _pair_sum(view, got, core, name):
    s, r, cols = view.shape
    hr = r // 2
    tr = _row_tile(hr, cols, itemsize=view.dtype.itemsize, budget=4 << 20)
    nb = hr // tr

    def body(core_ref, a_ref, b_ref, o_ref):
        del core_ref
        o_ref[...] = (a_ref[...].astype(F32) + b_ref[...].astype(F32)).astype(BF16)

    same = pl.BlockSpec((None, tr, cols), lambda j, i, core_ref: (j, i, 0))
    return pl.pallas_call(
        body, name=name,
        grid_spec=pltpu.PrefetchScalarGridSpec(
            num_scalar_prefetch=1, grid=(s, nb),
            in_specs=[pl.BlockSpec((None, tr, cols), lambda j, i, core_ref: (j, core_ref[0] * nb + i, 0)), same],
            out_specs=same),
        out_shape=jax.ShapeDtypeStruct((s, hr, cols), BF16), compiler_params=_params("parallel", "parallel"))(core, view, got)


def _pair_sum_small(views, gots, core, name):
    n = len(views)

    def body(core_ref, *refs):
        del core_ref
        for w in range(n):
            refs[2 * n + w][...] = (refs[w][...].astype(F32) + refs[n + w][...].astype(F32)).astype(BF16)

    halves = [(v.shape[0], v.shape[1] // 2, v.shape[2]) for v in views]
    own = [pl.BlockSpec(h, lambda i, core_ref: (0, core_ref[0], 0)) for h in halves]
    whole = [pl.BlockSpec(h, lambda i, core_ref: (0, 0, 0)) for h in halves]
    return pl.pallas_call(
        body, name=name,
        grid_spec=pltpu.PrefetchScalarGridSpec(num_scalar_prefetch=1, grid=(1,), in_specs=own + whole, out_specs=whole),
        out_shape=[jax.ShapeDtypeStruct(h, BF16) for h in halves], compiler_params=_params("arbitrary"))(core, *views, *gots)


def _piece_cols(part, col_sharded):
    return part.shape[2] // N_CHIPS if col_sharded else part.shape[2]


def _chip_sum_small(parts, gots, col_sharded, place, name):
    n = len(parts)

    def body(place_ref, *refs):
        del place_ref
        for w in range(n):
            got = refs[n + w]
            acc = refs[w][...].astype(F32) + got[0].astype(F32)
            refs[2 * n + w][...] = acc + got[1].astype(F32) + got[2].astype(F32)

    own, others, outs, shapes = [], [], [], []
    for p, cs in zip(parts, col_sharded):
        hr, cols = p.shape[1], _piece_cols(p, cs)
        own.append(pl.BlockSpec((None, hr, cols), (lambda i, pr: (0, 0, pr[0])) if cs else (lambda i, pr: (pr[0], 0, 0))))
        others.append(pl.BlockSpec((3, hr, cols), lambda i, pr: (0, 0, 0)))
        outs.append(pl.BlockSpec((hr, cols), lambda i, pr: (pr[1], 0)))
        shapes.append(jax.ShapeDtypeStruct((2 * hr, cols), F32))
    return pl.pallas_call(
        body, name=name,
        grid_spec=pltpu.PrefetchScalarGridSpec(num_scalar_prefetch=1, grid=(1,), in_specs=own + others, out_specs=outs),
        out_shape=shapes, compiler_params=_params("arbitrary"))(place, *parts, *gots)


def _chip_sum(part, got, col_sharded, place, name):
    _, hr, _ = part.shape
    cols = _piece_cols(part, col_sharded)
    tr = _row_tile(hr, cols)
    nb = hr // tr

    def body(place_ref, own_ref, g0_ref, g1_ref, g2_ref, o_ref):
        del place_ref
        acc = own_ref[...].astype(F32) + g0_ref[...].astype(F32)
        o_ref[...] = acc + g1_ref[...].astype(F32) + g2_ref[...].astype(F32)

    if col_sharded:
        own = pl.BlockSpec((None, tr, cols), lambda i, pr: (0, i, pr[0]))
    else:
        own = pl.BlockSpec((None, tr, cols), lambda i, pr: (pr[0], i, 0))
    others = [pl.BlockSpec((None, tr, cols), lambda i, pr, k=k: (k, i, 0)) for k in range(3)]
    return pl.pallas_call(
        body, name=name,
        grid_spec=pltpu.PrefetchScalarGridSpec(
            num_scalar_prefetch=1, grid=(nb,), in_specs=[own] + others,
            out_specs=pl.BlockSpec((tr, cols), lambda i, pr: (pr[1] * nb + i, 0))),
        out_shape=jax.ShapeDtypeStruct((2 * hr, cols), F32), compiler_params=_params("parallel"))(place, part, got, got, got)


def _row_tile(rows, cols, itemsize=4, budget=2 << 20):
    t = rows
    while t * cols * itemsize > budget and t % 16 == 0:
        t //= 2
    return t


def _adamw(w, g, m, v, name, after, also_g=False):
    rows, cols = w.shape
    tr = _row_tile(rows, cols, budget=2 << 20)
    extra = [] if after is None else [after]
    n_out = 4 if also_g else 3

    def body(w_ref, g_ref, m_ref, v_ref, *rest):
        d_ref, nm_ref, nv_ref = rest[len(extra):len(extra) + 3]
        g_ = g_ref[...]
        if also_g:
            rest[len(extra) + 3][...] = g_
        nm = ADAM_B1 * m_ref[...] + (1.0 - ADAM_B1) * g_
        nv = ADAM_B2 * v_ref[...] + (1.0 - ADAM_B2) * (g_ * g_)
        m_hat = nm / (1.0 - ADAM_B1 ** ADAM_STEP)
        v_hat = nv / (1.0 - ADAM_B2 ** ADAM_STEP)
        d_ref[...] = -ADAM_LR * (m_hat / (jnp.sqrt(v_hat) + ADAM_EPS) + ADAM_WD * w_ref[...])
        nm_ref[...] = nm
        nv_ref[...] = nv

    spec = pl.BlockSpec((tr, cols), lambda i: (i, 0))
    shp = jax.ShapeDtypeStruct((rows, cols), F32)
    return pl.pallas_call(body, name=name, grid=(rows // tr,),
                          in_specs=[spec] * 4 + [pl.BlockSpec(memory_space=pl.ANY)] * len(extra), out_specs=[spec] * n_out,
                          out_shape=[shp] * n_out, compiler_params=_params("parallel"))(w, g, m, v, *extra)


def _adamw_together(ws, gs, ms, vs, name, after):
    n = len(ws)

    def body(*refs):
        outs = refs[4 * n + 1:]
        for k in range(n):
            g_ = refs[n + k][...]
            nm = ADAM_B1 * refs[2 * n + k][...] + (1.0 - ADAM_B1) * g_
            nv = ADAM_B2 * refs[3 * n + k][...] + (1.0 - ADAM_B2) * (g_ * g_)
            m_hat = nm / (1.0 - ADAM_B1 ** ADAM_STEP)
            v_hat = nv / (1.0 - ADAM_B2 ** ADAM_STEP)
            outs[3 * k][...] = -ADAM_LR * (m_hat / (jnp.sqrt(v_hat) + ADAM_EPS) + ADAM_WD * refs[k][...])
            outs[3 * k + 1][...] = nm
            outs[3 * k + 2][...] = nv

    specs = [pl.BlockSpec(w.shape, lambda i: (0, 0)) for w in ws]
    out_specs, out_shape = [], []
    for w in ws:
        out_specs += [pl.BlockSpec(w.shape, lambda i: (0, 0))] * 3
        out_shape += [jax.ShapeDtypeStruct(w.shape, F32)] * 3
    res = pl.pallas_call(body, name=name, grid=(1,), in_specs=specs * 4 + [pl.BlockSpec(memory_space=pl.ANY)],
                         out_specs=out_specs, out_shape=out_shape, compiler_params=_params("arbitrary"))(*ws, *gs, *ms, *vs, after)
    return [tuple(res[3 * k:3 * k + 3]) for k in range(n)]


def _all_devices(x, y, c):
    out = []
    for k in range(1, N_DEV):
        peer = (x ^ ((k >> 2) & 1), y ^ ((k >> 1) & 1), c ^ (k & 1))
        out.append((peer, 4 * peer[0] + 2 * peer[1] + peer[2]))
    return out


def _ada_exchange(c, w_shard, b_ada):
    bsz, d = c.shape
    cs = w_shard.shape[1]

    def body(c_ref, w_ref, b_ref, mod_ref, act_ref, c_all, part, pieces, csend, crecv, psend, precv):
        x, y, core = _place()
        me = 4 * x + 2 * y + core
        chip = 2 * x + y
        c_all[me] = c_ref[...]
        peers = _all_devices(x, y, core)
        copies = []
        for k, (peer, _) in enumerate(peers):
            cp = pltpu.make_async_remote_copy(src_ref=c_ref, dst_ref=c_all.at[me], send_sem=csend.at[k], recv_sem=crecv.at[k],
                                              device_id=peer, device_id_type=MESH)
            cp.start()
            copies.append(cp)
        for k, (_, src) in enumerate(peers):
            pltpu.make_async_remote_copy(src_ref=c_ref, dst_ref=c_all.at[src], send_sem=csend.at[k], recv_sem=crecv.at[k],
                                         device_id=(x, y, core), device_id_type=MESH).wait_recv()
        rows = jnp.concatenate([c_all[i] for i in range(N_DEV)], axis=0)
        act = rows * _sigmoid(rows)
        act_ref[...] = act
        prod = jnp.dot(act.astype(BF16), w_ref[...].astype(BF16), preferred_element_type=F32)
        for i in range(N_DEV):
            part[i] = prod[i * bsz:(i + 1) * bsz, :]
        pieces[chip] = part[me]
        chips = _other_chips(x, y)
        for k, (cx, cy) in enumerate(chips):
            cp = pltpu.make_async_remote_copy(src_ref=part.at[4 * cx + 2 * cy + core], dst_ref=pieces.at[chip],
                                              send_sem=psend.at[k], recv_sem=precv.at[k], device_id=(cx, cy, core),
                                              device_id_type=MESH)
            cp.start()
            copies.append(cp)
        for k, (cx, cy) in enumerate(chips):
            pltpu.make_async_remote_copy(src_ref=part.at[me], dst_ref=pieces.at[2 * cx + cy], send_sem=psend.at[k],
                                         recv_sem=precv.at[k], device_id=(cx, cy, core), device_id_type=MESH).wait_recv()
        for cp in copies:
            cp.wait_send()
        mod_ref[...] = jnp.concatenate([pieces[j] for j in range(N_CHIPS)], axis=1) + b_ref[...]

    vm = pl.BlockSpec(memory_space=pltpu.VMEM)
    return pl.pallas_call(
        body, name="ada_exchange", in_specs=[vm] * 3, out_specs=[vm] * 2,
        out_shape=[jax.ShapeDtypeStruct((bsz, 3 * d), F32), jax.ShapeDtypeStruct((N_DEV * bsz, d), F32)],
        scratch_shapes=[pltpu.VMEM((N_DEV, bsz, d), F32), pltpu.VMEM((N_DEV, bsz, cs), F32), pltpu.VMEM((N_CHIPS, bsz, cs), F32),
                        pltpu.SemaphoreType.DMA((N_DEV - 1,)), pltpu.SemaphoreType.DMA((N_DEV - 1,)),
                        pltpu.SemaphoreType.DMA((3,)), pltpu.SemaphoreType.DMA((3,))],
        compiler_params=pltpu.CompilerParams(vmem_limit_bytes=VMEM_LIMIT_V7X))(c, w_shard, b_ada)


def _grad_and_adamw_w_ada(act_t, d_cols, w, m, v):
    d, n = act_t.shape
    cs = d_cols.shape[1]
    tn = min(256, cs)

    def body(a_ref, g_ref, w_ref, m_ref, v_ref, go_ref, d_ref, nm_ref, nv_ref):
        a, g = a_ref[...], g_ref[...]
        acc = a[:, 0:1] * g[0:1, :]
        for b in range(1, n):
            acc = acc + a[:, b:b + 1] * g[b:b + 1, :]
        go_ref[...] = acc
        nm = ADAM_B1 * m_ref[...] + (1.0 - ADAM_B1) * acc
        nv = ADAM_B2 * v_ref[...] + (1.0 - ADAM_B2) * (acc * acc)
        m_hat = nm / (1.0 - ADAM_B1 ** ADAM_STEP)
        v_hat = nv / (1.0 - ADAM_B2 ** ADAM_STEP)
        d_ref[...] = -ADAM_LR * (m_hat / (jnp.sqrt(v_hat) + ADAM_EPS) + ADAM_WD * w_ref[...])
        nm_ref[...] = nm
        nv_ref[...] = nv

    tile = pl.BlockSpec((d, tn), lambda j: (0, j))
    shp = jax.ShapeDtypeStruct((d, cs), F32)
    return pl.pallas_call(body, name="adamw_w_ada", grid=(cs // tn,),
                          in_specs=[pl.BlockSpec((d, n), lambda j: (0, 0)), pl.BlockSpec((n, tn), lambda j: (0, j)), tile, tile, tile],
                          out_specs=[tile] * 4, out_shape=[shp] * 4, compiler_params=_params("parallel"))(act_t, d_cols, w, m, v)


def _permute_w_in(w_nat, lay):
    d = lay.d
    group = 4

    def call(name, width, n_pieces, nat_piece, out_block0, prev):
        def body(*refs):
            refs[-1][...] = jnp.concatenate([r[...] for r in refs[:group]], axis=1)

        in_specs = [pl.BlockSpec((d, width), lambda s, m=m: (0, nat_piece(group * s + m))) for m in range(group)]
        args = [w_nat] * group
        aliases = {}
        if prev is not None:
            in_specs.append(pl.BlockSpec(memory_space=pl.ANY))
            args.append(prev)
            aliases = {group: 0}
        return pl.pallas_call(
            body, name=name, grid=(n_pieces // group,), in_specs=in_specs,
            out_specs=pl.BlockSpec((d, group * width), lambda s: (0, out_block0 + s)),
            out_shape=jax.ShapeDtypeStruct((d, lay.np), BF16), input_output_aliases=aliases,
            compiler_params=_params("arbitrary"))(*args)

    w_all = call("permute_w_attn", SLAB, 2 * PAIR_SLABS, lay.attn_nat_slab, 0, None)
    n_rest = 6 * d // CONV_TILE
    if n_rest % group:
        group = 2
    return call("permute_w_rest", CONV_TILE, n_rest, lay.rest_nat_tile, lay.c0 // (group * CONV_TILE), w_all)


def _project(x2, mod3, w_all, b_all, seq, col0, ncols, tn, out_dtype, want_ht, name):
    t, d = x2.shape
    tm = min(2048, seq)
    per_seq = seq // tm
    j0 = col0 // tn

    def body(x_ref, mod_ref, w_ref, b_ref, o_ref, *rest):
        h_ref = rest[-1]

        @pl.when(pl.program_id(1) == 0)
        def _():
            h = x_ref[...] * (1.0 + mod_ref[:, d:2 * d]) + mod_ref[:, 0:d]
            h_ref[...] = h.astype(BF16)
            if want_ht:
                rest[0][...] = h.T.astype(BF16)

        o_ref[...] = (jnp.dot(h_ref[...], w_ref[...], preferred_element_type=F32) + b_ref[...]).astype(out_dtype)

    out_shape = [jax.ShapeDtypeStruct((t, ncols), out_dtype)]
    out_specs = [pl.BlockSpec((tm, tn), lambda i, j: (i, j))]
    if want_ht:
        out_shape.append(jax.ShapeDtypeStruct((d, t), BF16))
        out_specs.append(pl.BlockSpec((d, tm), lambda i, j: (0, i)))
    return pl.pallas_call(
        body, name=name, grid=(t // tm, ncols // tn),
        in_specs=[pl.BlockSpec((tm, d), lambda i, j: (i, 0)),
                  pl.BlockSpec((None, 1, 3 * d), lambda i, j: (i // per_seq, 0, 0)),
                  pl.BlockSpec((d, tn), lambda i, j: (0, j0 + j)),
                  pl.BlockSpec((1, tn), lambda i, j: (0, j0 + j))],
        out_specs=out_specs, out_shape=out_shape,
        scratch_shapes=[pltpu.VMEM((tm, d), BF16)],
        compiler_params=_params("arbitrary", "arbitrary"))(x2, mod3, w_all, b_all)


def _slope(g, p, hh):
    head = 4 * g + 2 * p + hh
    return 2.0 ** (-ALIBI_MAX_EXP * (head + 1.0) / N_HEADS)


def _ld_rows(ref, start, n, stride):
    if stride == 1:
        return ref[pl.ds(start, n), :]
    return ref[pl.ds(start, n, stride=stride), :]


def _st_rows(ref, start, n, stride, val):
    if stride == 1:
        ref[pl.ds(start, n), :] = val
    else:
        ref[pl.ds(start, n, stride=stride), :] = val


def _sub_blocks(g, seq):
    return seq // DILATIONS[g] // SUB


def _key_rows(g, seq):
    return SUB if _sub_blocks(g, seq) == 1 else 2 * SUB


def _fill_bias(bias_ref, p, seq):
    for g in range(N_GROUPS):
        nk = _key_rows(g, seq)
        diff = lax.broadcasted_iota(jnp.int32, (SUB, nk), 0) - lax.broadcasted_iota(jnp.int32, (SUB, nk), 1)
        for i, off in enumerate((0, SUB)):
            if i == 1 and nk == SUB:
                continue
            delta = diff + off
            ok = (delta >= 0) & (delta <= SUB)
            dist = (delta * DILATIONS[g]).astype(F32)
            for hh in range(2):
                slope = jnp.where(p == 0, _slope(g, 0, hh), _slope(g, 1, hh))
                bias_ref[g, i, hh, :, 0:nk] = jnp.where(ok, -slope * dist, NEG)


def _to_sub_major(pa_ref, col, sub_ref, stage, dil, seq):
    cols = slice(col * SLAB, (col + 1) * SLAB)
    if dil == 1:
        sub_ref[...] = pa_ref[:, cols]
        return
    n = seq // dil
    stage[...] = pa_ref[:, cols].astype(F32)
    for r in range(dil):
        sub_ref[pl.ds(r * n, n), :] = stage[pl.ds(r, n, stride=dil), :].astype(BF16)


def _block_rows(it, g, seq):
    dil, nb = DILATIONS[g], _sub_blocks(g, seq)
    row0 = pl.multiple_of(it * SUB, SUB)
    if nb == 1:
        return row0, row0, 0, it
    blk = it % nb
    first = blk == 0
    krow0 = pl.multiple_of(row0 - jnp.where(first, 0, SUB), SUB)
    nat = row0 if dil == 1 else it // nb + dil * SUB * blk
    return row0, krow0, jnp.where(first, 0, 1), nat


def _nt(a, b):
    return lax.dot_general(a, b, (((1,), (1,)), ((), ())), preferred_element_type=F32)


def _tn(a, b):
    return lax.dot_general(a, b, (((0,), (0,)), ((), ())), preferred_element_type=F32)


def _head_sums(t):
    rows = t.shape[0]
    lo = jnp.broadcast_to(jnp.sum(t[:, :HEAD_DIM], axis=-1, keepdims=True), (rows, HEAD_DIM))
    hi = jnp.broadcast_to(jnp.sum(t[:, HEAD_DIM:], axis=-1, keepdims=True), (rows, HEAD_DIM))
    return jnp.concatenate([lo, hi], axis=-1)


def _attn_fwd(pa, bsz, seq):
    t = pa.shape[0]
    n_blocks = seq // SUB
    chunk = 256

    def body(pa_ref, o_ref, lse_ref, a_ref, sub, stage, bias_ref, s_buf, p_buf, l_buf):
        p = pl.program_id(1)
        _fill_bias(bias_ref, p, seq)
        head0 = lax.broadcasted_iota(jnp.int32, (SUB, SLAB), 1) < HEAD_DIM
        for g in range(N_GROUPS):
            dil = DILATIONS[g]
            for w in range(3):
                _to_sub_major(pa_ref, 3 * w + g, sub.at[w], stage, dil, seq)
            nk = _key_rows(g, seq)

            def trip(i, carry, g=g, dil=dil, nk=nk):
                places = [_block_rows(BLOCKS_PER_TRIP * i + j, g, seq) for j in range(BLOCKS_PER_TRIP)]
                for j, (row0, krow0, _, _) in enumerate(places):
                    q = sub[0, pl.ds(row0, SUB), :]
                    zero = jnp.zeros_like(q)
                    q2 = jnp.concatenate([jnp.where(head0, q, zero), jnp.where(head0, zero, q)], axis=0) * (HEAD_DIM ** -0.5)
                    s_buf[j, :, 0:nk] = _nt(q2, sub[1, pl.ds(krow0, nk), :])
                for j, (_, _, bi, _) in enumerate(places):
                    for c in range(0, 2 * SUB, SOFTMAX_ROWS):
                        hh, r = divmod(c, SUB)
                        s = s_buf[j, c:c + SOFTMAX_ROWS, 0:nk] + bias_ref[g, bi, hh, r:r + SOFTMAX_ROWS, 0:nk]
                        m = jnp.max(s, axis=-1, keepdims=True)
                        e = jnp.exp(s - m)
                        den = jnp.sum(e, axis=-1, keepdims=True)
                        p_buf[j, c:c + SOFTMAX_ROWS, 0:nk] = (e * (1.0 / den)).astype(BF16)
                        l_buf[j, c:c + SOFTMAX_ROWS, :] = jnp.broadcast_to(m + jnp.log(den), (SOFTMAX_ROWS, SLAB))
                for j, (_, krow0, _, nat) in enumerate(places):
                    o2 = jnp.dot(p_buf[j, :, 0:nk], sub[2, pl.ds(krow0, nk), :], preferred_element_type=F32)
                    _st_rows(o_ref.at[g], nat, SUB, dil, jnp.where(head0, o2[0:SUB], o2[SUB:2 * SUB]))
                    _st_rows(lse_ref.at[g], nat, SUB, dil, jnp.where(head0, l_buf[j, 0:SUB, :], l_buf[j, SUB:2 * SUB, :]))
                return carry

            lax.fori_loop(0, n_blocks // BLOCKS_PER_TRIP, trip, 0)

        def mix(i, carry):
            rows = pl.ds(pl.multiple_of(i * chunk, chunk), chunk)
            l0, l1, l2 = lse_ref[0, rows, :], lse_ref[1, rows, :], lse_ref[2, rows, :]
            m = jnp.maximum(jnp.maximum(l0, l1), l2)
            e0, e1, e2 = jnp.exp(l0 - m), jnp.exp(l1 - m), jnp.exp(l2 - m)
            tot = e0 + e1 + e2
            o = (e0 / tot) * o_ref[0, rows, :] + (e1 / tot) * o_ref[1, rows, :] + (e2 / tot) * o_ref[2, rows, :]
            z = pa_ref[rows, 9 * SLAB:10 * SLAB].astype(F32)
            a_ref[rows, :] = (o * (z * _sigmoid(z))).astype(BF16)
            return carry

        lax.fori_loop(0, seq // chunk, mix, 0)

    big = jax.ShapeDtypeStruct((N_GROUPS, t, 2 * SLAB), F32)
    return pl.pallas_call(
        body, name="attn_fwd", grid=(bsz, 2),
        in_specs=[pl.BlockSpec((seq, PAIR_COLS), lambda b, p: (b, p))],
        out_specs=[pl.BlockSpec((N_GROUPS, seq, SLAB), lambda b, p: (0, b, p)),
                   pl.BlockSpec((N_GROUPS, seq, SLAB), lambda b, p: (0, b, p)),
                   pl.BlockSpec((seq, SLAB), lambda b, p: (b, p))],
        out_shape=[big, big, jax.ShapeDtypeStruct((t, 2 * SLAB), BF16)],
        scratch_shapes=[pltpu.VMEM((3, seq, SLAB), BF16), pltpu.VMEM((seq, SLAB), F32),
                        pltpu.VMEM((N_GROUPS, 2, 2, SUB, 2 * SUB), F32), pltpu.VMEM((BLOCKS_PER_TRIP, 2 * SUB, 2 * SUB), F32),
                        pltpu.VMEM((BLOCKS_PER_TRIP, 2 * SUB, 2 * SUB), BF16), pltpu.VMEM((BLOCKS_PER_TRIP, 2 * SUB, SLAB), F32)],
        compiler_params=_params("arbitrary", "arbitrary"))(pa)


def _shift_down(v, k, rows):
    return jnp.where(rows >= k, pltpu.roll(v, k, 0), 0.0)


def _shift_up(v, k, rows):
    n = v.shape[0]
    return jnp.where(rows < n - k, pltpu.roll(v, n - k, 0), 0.0)


def _conv_fwd(pr, conv_w, bsz, seq, d):
    t = pr.shape[0]
    ct = CONV_TILE

    def body(p_ref, cw_ref, o_ref):
        u = p_ref[:, 2 * ct:3 * ct].astype(F32) * p_ref[:, 0:ct].astype(F32)
        cw = cw_ref[...]
        rows = lax.broadcasted_iota(jnp.int32, u.shape, 0)
        conv = cw[0:1, :] * _shift_down(u, 2, rows)
        conv = conv + cw[1:2, :] * _shift_down(u, 1, rows)
        conv = conv + cw[2:3, :] * u
        z = p_ref[:, 3 * ct:4 * ct].astype(F32)
        o_ref[...] = (p_ref[:, ct:2 * ct].astype(F32) * conv * (z * _sigmoid(z))).astype(BF16)

    return pl.pallas_call(
        body, name="conv_fwd", grid=(bsz, d // ct),
        in_specs=[pl.BlockSpec((seq, 4 * ct), lambda b, j: (b, j)), pl.BlockSpec((3, ct), lambda b, j: (0, j))],
        out_specs=pl.BlockSpec((seq, ct), lambda b, j: (b, j)),
        out_shape=jax.ShapeDtypeStruct((t, d), BF16), compiler_params=_params("parallel", "parallel"))(pr, conv_w)


def _tail(a_in, b_in, pr, x2, target2, mod3, w_pa, w_pc, w_out, b_out, ln_g, ln_b, seq, lay):
    t, d = x2.shape
    tm = 512
    per_seq = seq // tm
    n_steps = t // tm
    gate_blk = 4 * d // d

    def nt(a, b):
        return lax.dot_general(a, b, (((1,), (1,)), ((), ())), preferred_element_type=F32)

    def tn(a, b):
        return lax.dot_general(a, b, (((0,), (0,)), ((), ())), preferred_element_type=F32)

    def body(a_ref, b_ref, ga_ref, gb_ref, x_ref, tg_ref, mod_ref, wpa_ref, wpc_ref, wo_ref, bo_ref, lg_ref, lb_ref,
             dpg_ref, da_ref, db_ref, gx_ref, dgate_ref, small_ref, gwpa_hbm, gwpc_hbm, gwo_hbm,
             acc_pa, acc_pc, acc_o, sem):
        i = pl.program_id(0)

        @pl.when(i == 0)
        def _():
            acc_pa[...] = jnp.zeros_like(acc_pa)
            acc_pc[...] = jnp.zeros_like(acc_pc)
            acc_o[...] = jnp.zeros_like(acc_o)
            small_ref[...] = jnp.zeros_like(small_ref)

        @pl.when(i % per_seq == 0)
        def _():
            dgate_ref[...] = jnp.zeros_like(dgate_ref)

        halves = [slice(k * (tm // 2), (k + 1) * (tm // 2)) for k in range(2)]
        gate = mod_ref[:, 2 * d:3 * d]
        a_bf = [a_ref[rs, :] for rs in halves]
        b_bf = [b_ref[rs, :] for rs in halves]
        y_attn = [jnp.dot(a, wpa_ref[...], preferred_element_type=F32) for a in a_bf]
        y_conv = [jnp.dot(b, wpc_ref[...], preferred_element_type=F32) for b in b_bf]
        sa = [_sigmoid(ga_ref[rs, :].astype(F32)) for rs in halves]
        sb = [_sigmoid(gb_ref[rs, :].astype(F32)) for rs in halves]
        merged = [(sa[k] * y_attn[k] + sb[k] * y_conv[k]).astype(BF16) for k in range(2)]
        mo = [jnp.dot(m, wo_ref[...], preferred_element_type=F32) + bo_ref[...] for m in merged]
        d_mo_bf = []
        for k, rs in enumerate(halves):
            r = ALPHA * x_ref[rs, :] + gate * mo[k]
            mu = jnp.mean(r, axis=-1, keepdims=True)
            cen = r - mu
            var = jnp.mean(cen * cen, axis=-1, keepdims=True)
            rstd = lax.rsqrt(var + LN_EPS)
            xhat = cen * rstd
            err = xhat * lg_ref[...] + lb_ref[...] - tg_ref[rs, :]
            dy = err * (1.0 / d)
            dxhat = dy * lg_ref[...]
            dr = rstd * (dxhat - jnp.mean(dxhat, axis=-1, keepdims=True)
                         - xhat * jnp.mean(dxhat * xhat, axis=-1, keepdims=True))
            gx_ref[rs, :] = ALPHA * dr
            dgate_ref[...] += jnp.sum(dr * mo[k], axis=0, keepdims=True)
            d_mo = dr * gate
            small_ref[0:1, :] += jnp.sum(d_mo, axis=0, keepdims=True)
            small_ref[1:2, :] += jnp.sum(dy * xhat, axis=0, keepdims=True)
            small_ref[2:3, :] += jnp.sum(dy, axis=0, keepdims=True)
            small_ref[3:4, :] += jnp.sum(err * err, axis=0, keepdims=True)
            d_mo_bf.append(d_mo.astype(BF16))
        acc_o[...] += tn(jnp.concatenate(merged, axis=0), jnp.concatenate(d_mo_bf, axis=0))
        dmerged = [nt(g, wo_ref[...]) for g in d_mo_bf]
        dy_attn, dy_conv = [], []
        for k, rs in enumerate(halves):
            dy_attn.append((dmerged[k] * sa[k]).astype(BF16))
            dy_conv.append((dmerged[k] * sb[k]).astype(BF16))
            dpg_ref[rs, 0:d] = (dmerged[k] * y_attn[k] * sa[k] * (1.0 - sa[k])).astype(BF16)
            dpg_ref[rs, d:2 * d] = (dmerged[k] * y_conv[k] * sb[k] * (1.0 - sb[k])).astype(BF16)
        acc_pa[...] += tn(a_ref[...], jnp.concatenate(dy_attn, axis=0))
        acc_pc[...] += tn(b_ref[...], jnp.concatenate(dy_conv, axis=0))
        for k, rs in enumerate(halves):
            da_ref[rs, :] = nt(dy_attn[k], wpa_ref[...])
            db_ref[rs, :] = nt(dy_conv[k], wpc_ref[...])

        @pl.when(i == n_steps - 1)
        def _():
            copies = [pltpu.make_async_copy(acc_pa, gwpa_hbm, sem.at[0]), pltpu.make_async_copy(acc_pc, gwpc_hbm, sem.at[1]),
                      pltpu.make_async_copy(acc_o, gwo_hbm, sem.at[2])]
            for cp in copies:
                cp.start()
            for cp in copies:
                cp.wait()

    row = lambda w: pl.BlockSpec((tm, w), lambda i: (i, 0))
    const = lambda shp: pl.BlockSpec(shp, lambda i: (0,) * len(shp), pipeline_mode=pl.Buffered(1))
    any_spec = pl.BlockSpec(memory_space=pl.ANY)
    return pl.pallas_call(
        body, name="tail", grid=(n_steps,),
        in_specs=[row(Z_WIDTH), row(d),
                  pl.BlockSpec((tm, d), lambda i: (i, gate_blk)), pl.BlockSpec((tm, d), lambda i: (i, gate_blk + 1)),
                  row(d), row(d), pl.BlockSpec((None, 1, 3 * d), lambda i: (i // per_seq, 0, 0)),
                  const((Z_WIDTH, d)), const((d, d)), const((d, d)), const((1, d)), const((1, d)), const((1, d))],
        out_specs=[pl.BlockSpec((tm, 2 * d), lambda i: (i, lay.g0 // (2 * d))), row(Z_WIDTH), row(d), row(d),
                   pl.BlockSpec((None, 1, d), lambda i: (i // per_seq, 0, 0)), pl.BlockSpec((8, d), lambda i: (0, 0)),
                   HBM_SPEC, HBM_SPEC, HBM_SPEC],
        out_shape=[jax.ShapeDtypeStruct((t, lay.np), BF16), jax.ShapeDtypeStruct((t, Z_WIDTH), F32),
                   jax.ShapeDtypeStruct((t, d), F32), jax.ShapeDtypeStruct((t, d), F32),
                   jax.ShapeDtypeStruct((t // seq, 1, d), F32), jax.ShapeDtypeStruct((8, d), F32),
                   pltpu.HBM((Z_WIDTH, d), F32), pltpu.HBM((d, d), F32), pltpu.HBM((d, d), F32)],
        scratch_shapes=[pltpu.VMEM((Z_WIDTH, d), F32), pltpu.VMEM((d, d), F32), pltpu.VMEM((d, d), F32),
                        pltpu.SemaphoreType.DMA((3,))],
        compiler_params=_params("arbitrary"),
    )(a_in, b_in, pr, pr, x2, target2, mod3, w_pa, w_pc, w_out, b_out, ln_g, ln_b)


def _conv_bwd(dproj, db, pr, conv_w, bsz, seq, lay):
    d = lay.d
    ct = CONV_TILE
    base = lay.c0 // (4 * ct)

    def body(dp_in, db_ref, p_ref, cw_ref, dp_ref, gcw_ref):
        del dp_in
        u_x, g_b, g_c, z = [p_ref[:, k * ct:(k + 1) * ct].astype(F32) for k in range(4)]
        cw = cw_ref[...]
        u = g_c * u_x
        rows = lax.broadcasted_iota(jnp.int32, u.shape, 0)
        u1, u2 = _shift_down(u, 1, rows), _shift_down(u, 2, rows)
        conv = cw[0:1, :] * u2 + cw[1:2, :] * u1 + cw[2:3, :] * u
        sig = _sigmoid(z)
        sl = z * sig
        dbv = db_ref[...]
        gbc = g_b * conv
        dp_ref[:, ct:2 * ct] = (dbv * sl * conv).astype(BF16)
        dp_ref[:, 3 * ct:4 * ct] = (dbv * gbc * (sig * (1.0 + z * (1.0 - sig)))).astype(BF16)
        dconv = dbv * sl * g_b

        @pl.when(pl.program_id(1) == 0)
        def _():
            gcw_ref[...] = jnp.zeros_like(gcw_ref)

        gcw_ref[0:1, :] += jnp.sum(dconv * u2, axis=0, keepdims=True)
        gcw_ref[1:2, :] += jnp.sum(dconv * u1, axis=0, keepdims=True)
        gcw_ref[2:3, :] += jnp.sum(dconv * u, axis=0, keepdims=True)
        du = cw[2:3, :] * dconv + cw[1:2, :] * _shift_up(dconv, 1, rows) + cw[0:1, :] * _shift_up(dconv, 2, rows)
        dp_ref[:, 0:ct] = (du * g_c).astype(BF16)
        dp_ref[:, 2 * ct:3 * ct] = (du * u_x).astype(BF16)

    return pl.pallas_call(
        body, name="conv_bwd", grid=(d // ct, bsz),
        in_specs=[pl.BlockSpec(memory_space=pl.ANY), pl.BlockSpec((seq, ct), lambda j, b: (b, j)),
                  pl.BlockSpec((seq, 4 * ct), lambda j, b: (b, j)), pl.BlockSpec((3, ct), lambda j, b: (0, j))],
        out_specs=[pl.BlockSpec((seq, 4 * ct), lambda j, b: (b, base + j)), pl.BlockSpec((8, ct), lambda j, b: (0, j))],
        out_shape=[jax.ShapeDtypeStruct(dproj.shape, BF16), jax.ShapeDtypeStruct((8, d), F32)],
        input_output_aliases={0: 0}, compiler_params=_params("arbitrary", "arbitrary"))(dproj, db, pr, conv_w)


def _attn_bwd(dproj, pa, o_all, lse_all, da, bsz, seq, after):
    n_blocks = seq // SUB
    chunk = 256
    per_trip = BLOCKS_PER_TRIP_BWD

    def body(dp_in, pa_ref, o_ref, lse_ref, da_ref, after_ref, dp_ref, sub, stage, dsub, dog, cvec, bias_ref,
             s_buf, dp_buf, ds_buf, pb_buf, q2_buf, do2_buf, l_buf, c_buf):
        del dp_in, after_ref
        p = pl.program_id(1)
        _fill_bias(bias_ref, p, seq)
        head0 = lax.broadcasted_iota(jnp.int32, (SUB, SLAB), 1) < HEAD_DIM

        def mix_bwd(i, carry):
            rows = pl.ds(pl.multiple_of(i * chunk, chunk), chunk)
            ls = [lse_ref[g, rows, :] for g in range(N_GROUPS)]
            os_ = [o_ref[g, rows, :] for g in range(N_GROUPS)]
            m = jnp.maximum(jnp.maximum(ls[0], ls[1]), ls[2])
            es = [jnp.exp(l - m) for l in ls]
            tot = es[0] + es[1] + es[2]
            ws = [e / tot for e in es]
            o = ws[0] * os_[0] + ws[1] * os_[1] + ws[2] * os_[2]
            z = pa_ref[rows, 9 * SLAB:10 * SLAB].astype(F32)
            sig = _sigmoid(z)
            dav = da_ref[rows, :]
            do = dav * (z * sig)
            dp_ref[rows, 9 * SLAB:10 * SLAB] = (dav * o * (sig * (1.0 + z * (1.0 - sig)))).astype(BF16)
            wsum = _head_sums(do * o)
            for g in range(N_GROUPS):
                dog[g, rows, :] = ws[g] * do
                cvec[g, rows, :] = -(ws[g] * wsum)
            return carry

        lax.fori_loop(0, seq // chunk, mix_bwd, 0)

        for g in range(N_GROUPS):
            dil = DILATIONS[g]
            for w in range(3):
                _to_sub_major(pa_ref, 3 * w + g, sub.at[w], stage, dil, seq)
            dsub[1] = jnp.zeros((seq, SLAB), F32)
            dsub[2] = jnp.zeros((seq, SLAB), F32)
            nk = _key_rows(g, seq)

            def trip(i, carry, g=g, dil=dil, nk=nk):
                places = [_block_rows(per_trip * i + j, g, seq) for j in range(per_trip)]
                for j, (row0, krow0, _, nat) in enumerate(places):
                    q = sub[0, pl.ds(row0, SUB), :]
                    do = _ld_rows(dog.at[g], nat, SUB, dil).astype(BF16)
                    zero = jnp.zeros_like(q)
                    q2 = jnp.concatenate([jnp.where(head0, q, zero), jnp.where(head0, zero, q)], axis=0)
                    do2 = jnp.concatenate([jnp.where(head0, do, zero), jnp.where(head0, zero, do)], axis=0)
                    q2_buf[j] = q2
                    do2_buf[j] = do2
                    s_buf[j, :, 0:nk] = _nt(q2 * (HEAD_DIM ** -0.5), sub[1, pl.ds(krow0, nk), :])
                    dp_buf[j, :, 0:nk] = _nt(do2, sub[2, pl.ds(krow0, nk), :])
                    l_buf[j] = _ld_rows(lse_ref.at[g], nat, SUB, dil)
                    c_buf[j] = _ld_rows(cvec.at[g], nat, SUB, dil)
                for j, (_, _, bi, _) in enumerate(places):
                    for c in range(0, 2 * SUB, SOFTMAX_ROWS):
                        hh, r = divmod(c, SUB)
                        lane = hh * HEAD_DIM
                        s = s_buf[j, c:c + SOFTMAX_ROWS, 0:nk] + bias_ref[g, bi, hh, r:r + SOFTMAX_ROWS, 0:nk]
                        prob = jnp.exp(s - l_buf[j, r:r + SOFTMAX_ROWS, lane:lane + 1])
                        dprob = dp_buf[j, c:c + SOFTMAX_ROWS, 0:nk] + c_buf[j, r:r + SOFTMAX_ROWS, lane:lane + 1]
                        ds_buf[j, c:c + SOFTMAX_ROWS, 0:nk] = (prob * dprob * (HEAD_DIM ** -0.5)).astype(BF16)
                        pb_buf[j, c:c + SOFTMAX_ROWS, 0:nk] = prob.astype(BF16)
                for j, (row0, krow0, _, _) in enumerate(places):
                    ds = ds_buf[j, :, 0:nk]
                    dq2 = jnp.dot(ds, sub[1, pl.ds(krow0, nk), :], preferred_element_type=F32)
                    dsub[0, pl.ds(row0, SUB), :] = jnp.where(head0, dq2[0:SUB], dq2[SUB:2 * SUB])
                    dsub[1, pl.ds(krow0, nk), :] += _tn(ds, q2_buf[j])
                    dsub[2, pl.ds(krow0, nk), :] += _tn(pb_buf[j, :, 0:nk], do2_buf[j])
                return carry

            lax.fori_loop(0, n_blocks // per_trip, trip, 0)
            for w in range(3):
                cols = slice((3 * w + g) * SLAB, (3 * w + g + 1) * SLAB)
                if dil == 1:
                    dp_ref[:, cols] = dsub[w].astype(BF16)
                else:
                    n = seq // dil
                    for r in range(dil):
                        stage[pl.ds(r, n, stride=dil), :] = dsub[w, pl.ds(r * n, n), :]
                    dp_ref[:, cols] = stage[...].astype(BF16)

    return pl.pallas_call(
        body, name="attn_bwd", grid=(bsz, 2),
        in_specs=[pl.BlockSpec(memory_space=pl.ANY), pl.BlockSpec((seq, PAIR_COLS), lambda b, p: (b, p)),
                  pl.BlockSpec((N_GROUPS, seq, SLAB), lambda b, p: (0, b, p)),
                  pl.BlockSpec((N_GROUPS, seq, SLAB), lambda b, p: (0, b, p)),
                  pl.BlockSpec((seq, SLAB), lambda b, p: (b, p)), pl.BlockSpec(memory_space=pl.ANY)],
        out_specs=pl.BlockSpec((seq, PAIR_COLS), lambda b, p: (b, p)),
        out_shape=jax.ShapeDtypeStruct(dproj.shape, BF16), input_output_aliases={0: 0},
        scratch_shapes=[pltpu.VMEM((3, seq, SLAB), BF16), pltpu.VMEM((seq, SLAB), F32), pltpu.VMEM((3, seq, SLAB), F32),
                        pltpu.VMEM((3, seq, SLAB), F32), pltpu.VMEM((3, seq, SLAB), F32),
                        pltpu.VMEM((N_GROUPS, 2, 2, SUB, 2 * SUB), F32),
                        pltpu.VMEM((per_trip, 2 * SUB, 2 * SUB), F32), pltpu.VMEM((per_trip, 2 * SUB, 2 * SUB), F32),
                        pltpu.VMEM((per_trip, 2 * SUB, 2 * SUB), BF16), pltpu.VMEM((per_trip, 2 * SUB, 2 * SUB), BF16),
                        pltpu.VMEM((per_trip, 2 * SUB, SLAB), BF16), pltpu.VMEM((per_trip, 2 * SUB, SLAB), BF16),
                        pltpu.VMEM((per_trip, SUB, SLAB), F32), pltpu.VMEM((per_trip, SUB, SLAB), F32)],
        compiler_params=_params("arbitrary", "arbitrary"))(dproj, pa, o_all, lse_all, da, after)


def _grad_h(dproj, w_all, gx0, x2, mod3, seq, lay):
    t, d = x2.shape
    tm, tn = min(512, seq), min(512, d)
    per_seq = seq // tm

    def body(dp_ref, w_ref, gx0_ref, x_ref, scale_ref, gx_ref, dmod_ref):
        dh = _nt(dp_ref[:, 0:ATT], w_ref[:, 0:ATT]) + _nt(dp_ref[:, lay.c0:], w_ref[:, lay.c0:])
        gx_ref[...] = gx0_ref[...] + dh * (1.0 + scale_ref[...])

        @pl.when(pl.program_id(1) % per_seq == 0)
        def _():
            dmod_ref[...] = jnp.zeros_like(dmod_ref)

        dmod_ref[0:1, :] += jnp.sum(dh, axis=0, keepdims=True)
        dmod_ref[1:2, :] += jnp.sum(dh * x_ref[...], axis=0, keepdims=True)

    tile = pl.BlockSpec((tm, tn), lambda j, i: (i, j))
    return pl.pallas_call(
        body, name="grad_h", grid=(d // tn, t // tm),
        in_specs=[pl.BlockSpec((tm, lay.np), lambda j, i: (i, 0)), pl.BlockSpec((tn, lay.np), lambda j, i: (j, 0)),
                  tile, tile,
                  pl.BlockSpec((None, 1, tn), lambda j, i: (i // per_seq, 0, d // tn + j))],
        out_specs=[tile, pl.BlockSpec((None, 8, tn), lambda j, i: (i // per_seq, 0, j))],
        out_shape=[jax.ShapeDtypeStruct((t, d), F32), jax.ShapeDtypeStruct((t // seq, 8, d), F32)],
        compiler_params=_params("arbitrary", "arbitrary"))(dproj, w_all, gx0, x2, mod3)


def _grad_w_in(ht, dproj, seq, lay, part, prev):
    d, t = ht.shape
    tm = seq
    n_i = t // tm

    def make_body(n_skip, n_pieces, tn, nat_tile):
        def body(*refs):
            refs = refs[n_skip:]
            ht_ref, dp_refs = refs[0], refs[1:1 + n_pieces]
            gw_hbm, gb_hbm, acc, bacc, gw_out, gb_out, sem = refs[1 + n_pieces:]
            i, j = pl.program_id(0), pl.program_id(1)
            dp = dp_refs[0][...] if n_pieces == 1 else jnp.concatenate([r[...] for r in dp_refs], axis=1)
            part = jnp.dot(ht_ref[...], dp, preferred_element_type=F32)
            bpart = jnp.sum(dp.astype(F32), axis=0, keepdims=True)

            if n_i > 1:
                @pl.when(i == 0)
                def _():
                    acc[j] = part
                    bacc[j] = bpart

                @pl.when((i > 0) & (i < n_i - 1))
                def _():
                    acc[j] += part
                    bacc[j] += bpart

            @pl.when(i == n_i - 1)
            def _():
                gw_out[...] = ((part + acc[j]) if n_i > 1 else part).astype(BF16)
                gb_out[...] = (bpart + bacc[j]) if n_i > 1 else bpart
                cols = pl.ds(pl.multiple_of(nat_tile(j) * tn, SLAB), tn)
                copies = [pltpu.make_async_copy(gw_out, gw_hbm.at[:, cols], sem.at[0]),
                          pltpu.make_async_copy(gb_out, gb_hbm.at[:, cols], sem.at[1])]
                for cp in copies:
                    cp.start()
                for cp in copies:
                    cp.wait()
        return body

    def call(name, pieces, n_tiles, nat_tile, prev):
        tn = sum(w for w, _ in pieces)
        any_spec = pl.BlockSpec(memory_space=pl.ANY)
        in_specs = [pl.BlockSpec((d, tm), lambda i, j: (0, i))]
        in_specs += [pl.BlockSpec((tm, w), lambda i, j, f=f: (i, f(j))) for w, f in pieces]
        args = [ht] + [dproj] * len(pieces)
        aliases = {}
        if prev is not None:
            in_specs = [any_spec] * 2 + in_specs
            args = list(prev) + args
            aliases = {0: 0, 1: 1}
        return pl.pallas_call(
            make_body(0 if prev is None else 2, len(pieces), tn, nat_tile), name=name, grid=(n_i, n_tiles), in_specs=in_specs,
            out_specs=[any_spec, any_spec],
            out_shape=[jax.ShapeDtypeStruct((d, lay.din), BF16), jax.ShapeDtypeStruct((1, lay.din), F32)],
            input_output_aliases=aliases,
            scratch_shapes=[pltpu.VMEM((n_tiles, d, tn), F32), pltpu.VMEM((n_tiles, 1, tn), F32), pltpu.VMEM((d, tn), BF16),
                            pltpu.VMEM((1, tn), F32), pltpu.SemaphoreType.DMA((2,))],
            compiler_params=_params("arbitrary", "arbitrary"))(*args)

    if part == "attn":
        attn_pieces = [(SLAB, lambda j, m=m: (m % 2) * PAIR_SLABS + 2 * j + m // 2) for m in range(4)]
        return call("grad_w_in_attn", attn_pieces, ATT // 512, lambda j: j, prev)
    base = lay.c0 // CONV_TILE
    nct = lay.n_conv_tiles
    if nct % 2:
        return call("grad_w_in_rest", [(CONV_TILE, lambda j: base + j)], 6 * d // CONV_TILE, lay.rest_nat_tile, prev)
    half = nct // 2

    def rest_piece(m):
        def perm_tile(j):
            conv = base + 4 * (2 * (j % half) + m) + j // half
            return jnp.where(j < 4 * half, conv, base + 2 * j + m)
        return (CONV_TILE, perm_tile)

    return call("grad_w_in_rest", [rest_piece(0), rest_piece(1)], 6 * d // 512, lambda j: ATT // 512 + j, prev)


def _pack_rows(parts, width=128):
    flat = [p.reshape(-1) for p in parts]
    spans, rows = [], 0
    padded = []
    for f in flat:
        n = -(-f.shape[0] // (8 * width)) * 8
        padded.append(jnp.pad(f, (0, n * width - f.shape[0])).reshape(n, width))
        spans.append((rows, f.shape[0]))
        rows += n
    return jnp.concatenate(padded, axis=0), spans


def _unpack_rows(packed, spans, shapes, width=128):
    out = []
    for (row, n), shp in zip(spans, shapes):
        rows = -(-n // width)
        out.append(packed[row:row + rows].reshape(-1)[:n].reshape(shp))
    return out


def kernel(x, c, w_ada, b_ada, w_in, b_in, conv_w, w_proj_attn, w_proj_conv, w_out, b_out, ln_g, ln_b, loss_target, m_w_ada, m_b_ada, m_w_in, m_b_in, m_conv_w, m_w_proj_attn, m_w_proj_conv, m_w_out, m_b_out, m_ln_g, m_ln_b, v_w_ada, v_b_ada, v_w_in, v_b_in, v_conv_w, v_w_proj_attn, v_w_proj_conv, v_w_out, v_b_out, v_ln_g, v_ln_b):
    bsz, seq, d = x.shape
    t = bsz * seq
    lay = _Layout(d)
    col_sharded = [True, True, False, False]
    red_w = [w_in[0], w_proj_attn[0], w_proj_conv[0], w_out[0]]
    chip = 2 * lax.axis_index("x") + lax.axis_index("y")
    chip1 = chip.astype(jnp.int32).reshape(1)
    core1 = lax.axis_index("c").astype(jnp.int32).reshape(1)
    place = jnp.stack([chip, lax.axis_index("c")]).astype(jnp.int32)
    x2 = x.reshape(t, d)
    target2 = loss_target.reshape(t, d)

    mod, act_all = _ada_exchange(c, w_ada[0], b_ada)
    mod3 = mod.reshape(bsz, 1, 3 * d)

    cw_pad = jnp.pad(conv_w[0], ((0, 5), (0, 0))) + 0.0 * mod[0, 0]
    own_in_full = [_cast_into_full(red_w[0], col_sharded[0], chip1, "cast_shard_in")]
    own_in_full += list(_cast_into_full_small(red_w[1:], col_sharded[1:], chip1, "cast_shards_late"))
    (wi_f,), cw8 = _gather_weights(own_in_full[:1], col_sharded[:1], cw_pad)
    cw_full = cw8[0:3]
    late_copies = _direct_gather_copies(col_sharded[1:])
    late_send, late_recv, late_flying, late_token = _start_copies("gather_late_start", own_in_full[1:], (18,), cw8, late_copies)
    w_all = _permute_w_in(wi_f, lay)
    b_all = lay.perm_vector(b_in) + late_token[0, 0]

    rest_tn = 1024 if (6 * d) % 1024 == 0 else 512
    pa, = _project(x2, mod3, w_all, b_all, seq, 0, ATT, PAIR_COLS, BF16, False, "project_attn")
    pr, ht = _project(x2, mod3, w_all, b_all, seq, lay.c0, 6 * d, rest_tn, BF16, True, "project_rest")
    o_all, lse_all, a_in = _attn_fwd(pa, bsz, seq)
    b_in_act = _conv_fwd(pr, cw_full, bsz, seq, d)
    wpa_f, wpc_f, wo_f = _wait_copies("gather_late_wait", late_flying, late_send, late_recv, b_in_act, late_copies)
    (dproj, da_in, db_in, gx0, dgate, small_tail, gw_pa, gw_pc, gw_out) = _tail(
        a_in, b_in_act, pr, x2, target2, mod3, wpa_f, wpc_f, wo_f, b_out, ln_g, ln_b, seq, lay)

    late_views = _shard_views([gw_pa, gw_pc, gw_out], col_sharded[1:])
    late_lands = [lax.empty((v.shape[0], v.shape[1] // 2, v.shape[2]), v.dtype) for v in late_views]
    xl_send, xl_recv, xl_fly, xl_tok = _start_copies("grads_pair_exchange_late_start", late_views + late_lands, (3,), small_tail,
                                                     _pair_exchange_copies(3, False))
    dproj, gcw = _conv_bwd(dproj, db_in, pr, cw_full + xl_tok[0, 0], bsz, seq, lay)
    xl_done = _wait_copies("grads_pair_exchange_late_wait", xl_fly, xl_send, xl_recv, gcw, _pair_exchange_copies(3, True))
    late_parts = list(_pair_sum_small(xl_done[:3], xl_done[3:], core1, "grads_pair_sum_late"))

    late_cross = _chip_scatter_copies(3, col_sharded[1:])
    late_zone = [lax.empty((3, p.shape[1], _piece_cols(p, cs)), p.dtype) for p, cs in zip(late_parts, col_sharded[1:])]
    sl_send, sl_recv, sl_fly, sl_tok = _start_copies("grads_scatter_late_start", late_parts + late_zone, (9,), core1,
                                                     late_cross)
    dproj = _attn_bwd(dproj, pa, o_all, lse_all, da_in, bsz, seq, sl_tok)
    gw_in_bf, gb_in = _grad_w_in(ht, dproj, seq, lay, "rest", None)
    rest_cols, attn_cols = (ATT, 6 * d), (0, ATT)
    in_land = lax.empty((1, d // 2, lay.din), BF16)
    xi_copies = _pair_exchange_copies(1, False, rest_cols)
    xi_send, xi_recv, xi_fly, _ = _start_copies("grads_pair_exchange_in_start", [gw_in_bf.reshape(1, d, lay.din), in_land], (1,),
                                                gb_in, xi_copies)
    gw_in_bf, gb_in = _grad_w_in(ht, dproj, seq, lay, "attn", (xi_fly[0].reshape(d, lay.din), gb_in))
    xi_done = _wait_copies("grads_pair_exchange_in_wait", [gw_in_bf.reshape(1, d, lay.din), xi_fly[1]], xi_send, xi_recv, gb_in,
                           _pair_exchange_copies(1, True, rest_cols))
    in_got = _pair_exchange_into(xi_done[0], xi_done[1], attn_cols, "grads_pair_exchange_in_attn")
    sl_done = _wait_copies("grads_scatter_late_wait", sl_fly, sl_send, sl_recv, in_got, late_cross)
    late_red = list(_chip_sum_small(sl_done[:3], sl_done[3:], col_sharded[1:], place, "grads_chip_sum_late"))

    in_part = _pair_sum(xi_done[0], in_got, core1, "grads_pair_sum_in")
    in_cross = _chip_scatter_copies(1, col_sharded[:1])
    in_zone = [lax.empty((3, in_part.shape[1], _piece_cols(in_part, True)), in_part.dtype)]
    late_views1 = [f.reshape(1, *f.shape) for f in late_red]
    si_send, si_recv, si_fly, si_tok = _start_copies(
        "grads_scatter_in_join_late_start", [in_part] + in_zone + late_views1, (6,), core1,
        _both_copies(in_cross, 2, _pair_join_copies(3, False, sem0=3)))
    grad_x2, dmod = _grad_h(dproj, w_all, gx0, x2, mod3 + si_tok[0, 0], seq, lay)
    si_done = _wait_copies("grads_scatter_in_join_late_wait", si_fly, si_send, si_recv, grad_x2,
                           _both_copies(in_cross, 2, _pair_join_copies(3, True, sem0=3)))
    late_joined = [f[0] for f in si_done[2:]]
    in_red = _chip_sum(si_done[0], si_done[1], True, place, "grads_chip_sum_in")

    d_ada = jnp.concatenate([dmod[:, 0, :], dmod[:, 1, :], dgate[:, 0, :]], axis=1)
    pieces = [small_tail[3], jnp.sum(d_ada, axis=0), gb_in[0], small_tail[0], small_tail[1], small_tail[2], gcw[0:3]]
    packed, spans = _pack_rows(pieces)
    kept_in, _ = _pack_rows([d_ada])
    rows_all = jnp.concatenate([packed, kept_in], axis=0)
    small_land = lax.empty((N_DEV,) + rows_all.shape, F32)
    sm_send, sm_recv, sm_fly, sm_tok = _start_copies(
        "small_gather_join_in_start", [rows_all, small_land, in_red.reshape(1, *in_red.shape)], (N_DEV,), core1,
        _both_copies(_small_gather_copies, 2, _pair_join_copies(1, False, sem0=N_DEV - 1)))
    big_w = [w_ada[0]] + red_w
    big_m = [m_w_ada[0], m_w_in[0], m_w_proj_attn[0], m_w_proj_conv[0], m_w_out[0]]
    big_v = [v_w_ada[0], v_w_in[0], v_w_proj_attn[0], v_w_proj_conv[0], v_w_out[0]]
    big_out = [None] * 5
    big_out[2:5] = _adamw_together(big_w[2:], late_joined, big_m[2:], big_v[2:], "adamw_late", sm_tok)
    sm_done = _wait_copies("small_gather_join_in_wait", sm_fly, sm_send, sm_recv, big_out[4][0],
                           _both_copies(_small_gather_copies, 2, _pair_join_copies(1, True, sem0=N_DEV - 1)))
    me1 = (4 * lax.axis_index("x") + 2 * lax.axis_index("y") + lax.axis_index("c")).astype(jnp.int32).reshape(1)
    summed, kept = _small_sum(sm_done[0], sm_done[1], me1, packed.shape[0], d)
    loss = summed[0, 0]
    _, g_b_ada, g_b_in, g_b_out, g_ln_g, g_ln_b, g_cw_full = _unpack_rows(
        summed, spans, [(d,), (3 * d,), (lay.din,), (d,), (d,), (d,), (3, d)])
    g_cw = lax.dynamic_slice(g_cw_full, (0, chip * (d // N_CHIPS)), (3, d // N_CHIPS))
    d_ada_all = kept.reshape(N_DEV, -1)[:, :bsz * 3 * d].reshape(N_DEV * bsz, 3 * d)
    ada_cols = 3 * d // N_CHIPS
    d_ada_cols = lax.dynamic_slice(d_ada_all, (0, chip * ada_cols), (N_DEV * bsz, ada_cols))

    g_w_ada, *big_out[0] = _grad_and_adamw_w_ada(act_all.T, d_ada_cols, big_w[0], big_m[0], big_v[0])
    small_w = [b_ada, b_in, conv_w[0], b_out, ln_g, ln_b]
    small_g = [g_b_ada, g_b_in, g_cw, g_b_out, g_ln_g, g_ln_b]
    small_m = [m_b_ada, m_b_in, m_conv_w[0], m_b_out, m_ln_g, m_ln_b]
    small_v = [v_b_ada, v_b_in, v_conv_w[0], v_b_out, v_ln_g, v_ln_b]
    pw, sp = _pack_rows(small_w)
    pg, _ = _pack_rows(small_g)
    pm, _ = _pack_rows(small_m)
    pv, _ = _pack_rows(small_v)
    sd, sm, sv = _adamw(pw, pg, pm, pv, "adamw_small", None)
    big_out[1] = _adamw(big_w[1], sm_done[2][0], big_m[1], big_v[1], "adamw_1", None, also_g=True)
    g_big = [g_w_ada, big_out[1][3]] + late_joined
    shapes = [a.shape for a in small_w]
    sd, sm, sv = _unpack_rows(sd, sp, shapes), _unpack_rows(sm, sp, shapes), _unpack_rows(sv, sp, shapes)

    def order(wa, bA, wi, bI, cw, wpa, wpc, wo, bO, lg, lb):
        return (wa[None], bA, wi[None], bI, cw[None], wpa[None], wpc[None], wo[None], bO, lg, lb)

    sg = [g.reshape(s) for g, s in zip(small_g, shapes)]
    grads_out = order(g_big[0], sg[0], g_big[1], sg[1], sg[2], g_big[2], g_big[3], g_big[4], sg[3], sg[4], sg[5])
    outs = []
    for idx, small in enumerate((sd, sm, sv)):
        outs.append(order(big_out[0][idx], small[0], big_out[1][idx], small[1], small[2], big_out[2][idx],
                          big_out[3][idx], big_out[4][idx], small[3], small[4], small[5]))
    return (loss, grad_x2.reshape(bsz, seq, d), *grads_out, *outs[0], *outs[1], *outs[2])
```

```python
import jax
import jax.numpy as jnp
from jax import lax
from jax.experimental import pallas as pl
from jax.experimental.pallas import tpu as pltpu

F32 = jnp.float32
BF16 = jnp.bfloat16
MESH = pl.DeviceIdType.MESH

HEAD_DIM = 64
N_GROUPS = 3
DILATIONS = (1, 4, 16)
N_HEADS = 12
SUB = 128
Q_WIDTH = 768
Z_WIDTH = 256
ATT = 3 * Q_WIDTH + Z_WIDTH
SLAB = 128
PAIR_SLABS = 10
PAIR_COLS = PAIR_SLABS * SLAB
CONV_TILE = 256
SOFTMAX_ROWS = 32
BLOCKS_PER_TRIP = 8
BLOCKS_PER_TRIP_BWD = 16
ALIBI_MAX_EXP = 8.0
ALPHA = 2.0 ** 0.25
LN_EPS = 1e-5
ADAM_LR, ADAM_B1, ADAM_B2, ADAM_EPS, ADAM_WD, ADAM_STEP = 0.001, 0.9, 0.999, 1e-08, 0.01, 10
N_CHIPS = 4
N_DEV = 8
VMEM_LIMIT_V7X = 60 * 1024 * 1024
NEG = -1e30


def _params(*sem):
    return pltpu.CompilerParams(dimension_semantics=sem, vmem_limit_bytes=VMEM_LIMIT_V7X)


def _sigmoid(v):
    return 0.5 * jnp.tanh(0.5 * v) + 0.5


class _Layout:
    def __init__(self, d):
        self.d = d
        self.din = ATT + 6 * d
        c0 = 3072
        while c0 % (2 * d):
            c0 += 1024
        self.c0, self.g0, self.np = c0, c0 + 4 * d, c0 + 6 * d
        self.n_conv_tiles = d // CONV_TILE

    def attn_nat_slab(self, s):
        p, i = s // PAIR_SLABS, s % PAIR_SLABS
        return jnp.where(i < 9, (i // 3) * 6 + (i % 3) * 2 + p, 18 + p)

    def rest_nat_tile(self, t):
        n4 = 4 * self.n_conv_tiles
        conv = ATT // CONV_TILE + (t % 4) * self.n_conv_tiles + t // 4
        return jnp.where(t < n4, conv, ATT // CONV_TILE + t)

    def perm_vector(self, v):
        parts = []
        for s in range(2 * PAIR_SLABS):
            p, i = divmod(s, PAIR_SLABS)
            ns = (i // 3) * 6 + (i % 3) * 2 + p if i < 9 else 18 + p
            parts.append(v[:, ns * SLAB:(ns + 1) * SLAB])
        parts.append(jnp.zeros((1, self.c0 - ATT), v.dtype))
        for j in range(self.n_conv_tiles):
            for k in range(4):
                a = ATT + k * self.d + j * CONV_TILE
                parts.append(v[:, a:a + CONV_TILE])
        parts.append(v[:, ATT + 4 * self.d:])
        return jnp.concatenate(parts, axis=1)


def _place():
    return lax.axis_index("x"), lax.axis_index("y"), lax.axis_index("c")


def _other_chips(x, y):
    return [(1 - x, y), (x, 1 - y), (1 - x, 1 - y)]


def _shard_of(ref, col_sharded, chip, half=None):
    if col_sharded:
        cs = ref.shape[1] // N_CHIPS
        cols = pl.ds(pl.multiple_of(chip * cs, SLAB), cs)
        if half is None:
            return ref.at[:, cols]
        n = ref.shape[0] // 2
        return ref.at[pl.ds(half * n, n), cols]
    rs = ref.shape[0] // N_CHIPS
    if half is None:
        return ref.at[pl.ds(chip * rs, rs)]
    return ref.at[pl.ds(chip * rs + half * (rs // 2), rs // 2)]


GATHER_PIECES = 8


def _cast_into_full(shard, col_sharded, chip, name):
    rows, cols = shard.shape
    tr = _row_tile(rows, cols)
    nb = rows // tr

    def body(chip_ref, s_ref, o_ref):
        del chip_ref
        o_ref[...] = s_ref[...].astype(BF16)

    if col_sharded:
        full, out_spec = (rows, cols * N_CHIPS), pl.BlockSpec((tr, cols), lambda i, ch: (i, ch[0]))
    else:
        full, out_spec = (rows * N_CHIPS, cols), pl.BlockSpec((tr, cols), lambda i, ch: (ch[0] * nb + i, 0))
    return pl.pallas_call(
        body, name=name,
        grid_spec=pltpu.PrefetchScalarGridSpec(num_scalar_prefetch=1, grid=(nb,),
                                               in_specs=[pl.BlockSpec((tr, cols), lambda i, ch: (i, 0))], out_specs=out_spec),
        out_shape=jax.ShapeDtypeStruct(full, BF16), compiler_params=_params("parallel"))(chip, shard)


def _cast_into_full_small(shards, col_sharded, chip, name):
    n = len(shards)

    def body(chip_ref, *refs):
        del chip_ref
        for w in range(n):
            refs[n + w][...] = refs[w][...].astype(BF16)

    in_specs = [pl.BlockSpec(s.shape, lambda i, ch: (0, 0)) for s in shards]
    out_specs = [pl.BlockSpec(s.shape, (lambda i, ch: (0, ch[0])) if cs else (lambda i, ch: (ch[0], 0)))
                 for s, cs in zip(shards, col_sharded)]
    fulls = [(s.shape[0], s.shape[1] * N_CHIPS) if cs else (s.shape[0] * N_CHIPS, s.shape[1]) for s, cs in zip(shards, col_sharded)]
    return pl.pallas_call(
        body, name=name,
        grid_spec=pltpu.PrefetchScalarGridSpec(num_scalar_prefetch=1, grid=(1,), in_specs=in_specs, out_specs=out_specs),
        out_shape=[jax.ShapeDtypeStruct(f, BF16) for f in fulls], compiler_params=_params("arbitrary"))(chip, *shards)


def _gather_weights(fulls, col_sharded, small):
    n = len(fulls)
    kp = GATHER_PIECES

    def piece(ref, cs, chip, half, k):
        if cs:
            width = ref.shape[1] // N_CHIPS
            rows = ref.shape[0] // 2 // kp
            return ref.at[pl.ds(half * (ref.shape[0] // 2) + k * rows, rows), pl.ds(pl.multiple_of(chip * width, SLAB), width)]
        rs = ref.shape[0] // N_CHIPS
        rows = rs // 2 // kp
        return ref.at[pl.ds(chip * rs + half * (rs // 2) + k * rows, rows)]

    def body(*refs):
        sm_in, outs, sm_out = refs[n], refs[n + 1:2 * n + 1], refs[2 * n + 1]
        send, recv, fsend, frecv, lsem, ssend, srecv = refs[2 * n + 2:]
        x, y, c = _place()
        mine = 2 * x + y
        sibling = (x, y, 1 - c)
        first = (x ^ (1 - c), y ^ c)
        second = (x ^ c, y ^ (1 - c))
        diagonal = (1 - x, 1 - y)
        sources = [first, second, diagonal]
        senders = [first, second, second]

        def copy(ref, sems, slot, to):
            return pltpu.make_async_remote_copy(src_ref=ref, dst_ref=ref, send_sem=sems[0].at[slot], recv_sem=sems[1].at[slot],
                                                device_id=to, device_id_type=MESH)

        local = pltpu.make_async_copy(sm_in, _shard_of(sm_out, True, mine), lsem)
        local.start()
        sends = []
        for k, (cx, cy) in enumerate(_other_chips(x, y)):
            cp = pltpu.make_async_remote_copy(src_ref=sm_in, dst_ref=_shard_of(sm_out, True, mine), send_sem=ssend.at[k],
                                              recv_sem=srecv.at[k], device_id=(cx, cy, c), device_id_type=MESH)
            cp.start()
            sends.append(cp)
        for k in range(kp):
            for w in range(n):
                own = piece(outs[w], col_sharded[w], mine, c, k)
                for slot, chip in enumerate((first, second)):
                    cp = copy(own, (send, recv), (w * 3 + slot) * kp + k, (*chip, c))
                    cp.start()
                    sends.append(cp)
        for slot in range(3):
            source = 2 * sources[slot][0] + sources[slot][1]
            for k in range(kp):
                for w in range(n):
                    landed = piece(outs[w], col_sharded[w], source, c, k)
                    copy(landed, (send, recv), (w * 3 + slot) * kp + k, (*senders[slot], c)).wait_recv()
                    if slot == 0:
                        cp = copy(landed, (send, recv), (w * 3 + 2) * kp + k, (*second, c))
                        cp.start()
                        sends.append(cp)
                    cp = copy(landed, (fsend, frecv), (w * 3 + slot) * kp + k, sibling)
                    cp.start()
                    sends.append(cp)
        for slot, chip in enumerate((second, first, diagonal)):
            for k in range(kp):
                for w in range(n):
                    passed = piece(outs[w], col_sharded[w], 2 * chip[0] + chip[1], 1 - c, k)
                    copy(passed, (fsend, frecv), (w * 3 + slot) * kp + k, sibling).wait_recv()
        for k, (cx, cy) in enumerate(_other_chips(x, y)):
            theirs = _shard_of(sm_out, True, 2 * cx + cy)
            pltpu.make_async_remote_copy(src_ref=theirs, dst_ref=theirs, send_sem=ssend.at[k], recv_sem=srecv.at[k],
                                         device_id=(cx, cy, c), device_id_type=MESH).wait_recv()
        for cp in sends:
            cp.wait_send()
        local.wait()

    any_spec = pl.BlockSpec(memory_space=pl.ANY)
    outs = pl.pallas_call(
        body, name="gather_weights",
        out_shape=[jax.ShapeDtypeStruct(f.shape, BF16) for f in fulls]
        + [jax.ShapeDtypeStruct((small.shape[0], small.shape[1] * N_CHIPS), small.dtype)],
        in_specs=[any_spec] * (n + 1), out_specs=[any_spec] * (n + 1), input_output_aliases={w: w for w in range(n)},
        scratch_shapes=[pltpu.SemaphoreType.DMA((n * 3 * kp,)), pltpu.SemaphoreType.DMA((n * 3 * kp,)),
                        pltpu.SemaphoreType.DMA((n * 3 * kp,)), pltpu.SemaphoreType.DMA((n * 3 * kp,)), pltpu.SemaphoreType.DMA,
                        pltpu.SemaphoreType.DMA((3,)), pltpu.SemaphoreType.DMA((3,))],
    )(*fulls, small)
    return outs[:n], outs[n]


HBM_SPEC = pl.BlockSpec(memory_space=pltpu.HBM)
SEM_SPEC = pl.BlockSpec(memory_space=pltpu.SEMAPHORE)
DATAFLOW = pltpu.SideEffectType.DATAFLOW_SIDE_EFFECTING


def _start_copies(name, arrays, sem_shape, after, copies):
    n = len(arrays)

    def body(*refs):
        for cp in copies(refs[:n], refs[n + 1], refs[n + 2]):
            cp.start()
        token = refs[2 * n + 3]
        token[...] = jnp.zeros_like(token)

    res = pl.pallas_call(
        body, name=name,
        out_shape=(pltpu.SemaphoreType.DMA(sem_shape), pltpu.SemaphoreType.DMA(sem_shape),
                   *[pltpu.HBM(a.shape, a.dtype) for a in arrays], jax.ShapeDtypeStruct((8, 128), F32)),
        in_specs=[HBM_SPEC] * n + [pl.BlockSpec(memory_space=pl.ANY)],
        out_specs=(SEM_SPEC, SEM_SPEC, *([HBM_SPEC] * n), pl.BlockSpec(memory_space=pltpu.VMEM)),
        input_output_aliases={i: 2 + i for i in range(n)},
        compiler_params=pltpu.CompilerParams(has_side_effects=DATAFLOW),
    )(*[pltpu.with_memory_space_constraint(a, pltpu.HBM) for a in arrays], after)
    return res[0], res[1], list(res[2:2 + n]), res[2 + n]


def _wait_copies(name, arrays, send, recv, after, copies):
    n = len(arrays)

    def body(*refs):
        for cp in copies(refs[:n], refs[n], refs[n + 1]):
            cp.wait_send()
            cp.wait_recv()

    return pl.pallas_call(
        body, name=name, out_shape=[pltpu.HBM(a.shape, a.dtype) for a in arrays],
        in_specs=[HBM_SPEC] * n + [SEM_SPEC, SEM_SPEC, pl.BlockSpec(memory_space=pl.ANY)], out_specs=[HBM_SPEC] * n,
        input_output_aliases={i: i for i in range(n)},
        compiler_params=pltpu.CompilerParams(has_side_effects=DATAFLOW),
    )(*arrays, send, recv, after)


def _direct_gather_copies(col_sharded):
    def copies(refs, send, recv):
        x, y, c = _place()
        mine = 2 * x + y
        out = []
        for w, ref in enumerate(refs):
            own_half = _shard_of(ref, col_sharded[w], mine, c)
            k = 0
            for cx, cy in _other_chips(x, y):
                for pc in (c, 1 - c):
                    out.append(pltpu.make_async_remote_copy(
                        src_ref=own_half, dst_ref=own_half, send_sem=send.at[6 * w + k], recv_sem=recv.at[6 * w + k],
                        device_id=(cx, cy, pc), device_id_type=MESH))
                    k += 1
        return out
    return copies


def _chip_scatter_copies(n, col_sharded):
    def piece(ref, cs, chip):
        if cs:
            w = ref.shape[2] // N_CHIPS
            return ref.at[:, :, pl.ds(pl.multiple_of(chip * w, SLAB), w)]
        return ref.at[pl.ds(chip, 1)]

    def copies(refs, send, recv):
        x, y, c = _place()
        out = []
        for k, (cx, cy) in enumerate(_other_chips(x, y)):
            for w in range(n):
                out.append(pltpu.make_async_remote_copy(
                    src_ref=piece(refs[w], col_sharded[w], 2 * cx + cy), dst_ref=refs[n + w].at[pl.ds(k, 1)],
                    send_sem=send.at[3 * w + k], recv_sem=recv.at[3 * w + k], device_id=(cx, cy, c), device_id_type=MESH))
        return out
    return copies


def _shard_views(gs, col_sharded):
    return [g.reshape(1, *g.shape) if cs else g.reshape(N_CHIPS, g.shape[0] // N_CHIPS, g.shape[1])
            for g, cs in zip(gs, col_sharded)]


DMA_CHUNK_BYTES = 1 << 20


def _chunk_rows(shape, itemsize):
    s, rows, cols = shape
    n = 1
    while s * (rows // n) * cols * itemsize > DMA_CHUNK_BYTES and (rows // n) % 32 == 0:
        n *= 2
    return rows // n


def _rows_of(ref, row0, rows, cols):
    if cols is None:
        return ref.at[:, pl.ds(row0, rows)]
    return ref.at[:, pl.ds(row0, rows), pl.ds(cols[0], cols[1])]


def _row_pieces(src, src_row0, dst, dst_row0, rows, send_sem, recv_sem, device, cols=None):
    width = src.shape[2] if cols is None else cols[1]
    step = _chunk_rows((src.shape[0], rows, width), src.dtype.itemsize)
    return [pltpu.make_async_remote_copy(src_ref=_rows_of(src, src_row0 + r, step, cols), dst_ref=_rows_of(dst, dst_row0 + r, step, cols),
                                         send_sem=send_sem, recv_sem=recv_sem, device_id=device, device_id_type=MESH)
            for r in range(0, rows, step)]


def _pair_exchange_copies(n, whole, cols=None):
    def copies(refs, send, recv):
        x, y, c = _place()
        sibling = (x, y, 1 - c)
        out = []
        for w in range(n):
            hr = refs[n + w].shape[1]
            if whole:
                out.append(pltpu.make_async_remote_copy(
                    src_ref=_rows_of(refs[w], (1 - c) * hr, hr, cols), dst_ref=_rows_of(refs[n + w], 0, hr, cols),
                    send_sem=send.at[w], recv_sem=recv.at[w], device_id=sibling, device_id_type=MESH))
            else:
                out += _row_pieces(refs[w], (1 - c) * hr, refs[n + w], 0, hr, send.at[w], recv.at[w], sibling, cols)
        return out
    return copies


def _pair_join_copies(n, whole, sem0=0):
    def copies(refs, send, recv):
        x, y, c = _place()
        sibling = (x, y, 1 - c)
        out = []
        for w in range(n):
            hr = refs[w].shape[1] // 2
            sems = dict(send_sem=send.at[sem0 + w], recv_sem=recv.at[sem0 + w])
            if whole:
                out.append(pltpu.make_async_remote_copy(
                    src_ref=refs[w].at[:, pl.ds(c * hr, hr)], dst_ref=refs[w].at[:, pl.ds((1 - c) * hr, hr)],
                    device_id=sibling, device_id_type=MESH, **sems))
            else:
                out += _row_pieces(refs[w], c * hr, refs[w], c * hr, hr, sems["send_sem"], sems["recv_sem"], sibling)
        return out
    return copies


def _both_copies(first, n_first, second):
    def copies(refs, send, recv):
        return first(refs[:n_first], send, recv) + second(refs[n_first:], send, recv)
    return copies


def _small_gather_copies(refs, send, recv):
    vec, land = refs
    x, y, c = _place()
    me = 4 * x + 2 * y + c
    return [pltpu.make_async_remote_copy(src_ref=vec, dst_ref=land.at[me], send_sem=send.at[k], recv_sem=recv.at[k],
                                         device_id=peer, device_id_type=MESH) for k, (peer, _) in enumerate(_all_devices(x, y, c))]


def _small_sum(vec, land, me, n_sum, d):
    rows = vec.shape[0]

    def body(me_ref, v_ref, l_ref, sum_ref, kept_ref):
        def slot(k):
            return jnp.where(me_ref[0] == k, v_ref[...], l_ref[k])

        total = slot(0)[0:n_sum, :]
        kept_ref[0] = slot(0)[n_sum:rows, :]
        for k in range(1, N_DEV):
            total = total + slot(k)[0:n_sum, :]
            kept_ref[k] = slot(k)[n_sum:rows, :]
        sum_ref[...] = total
        loss = 0.5 / d * jnp.sum(total[0:8, :])
        sum_ref[0:8, :] = jnp.full((8, 128), loss, F32)

    vm = pl.BlockSpec(memory_space=pltpu.VMEM)
    return pl.pallas_call(
        body, name="small_sum", in_specs=[pl.BlockSpec(memory_space=pltpu.SMEM), vm, vm], out_specs=[vm, vm],
        out_shape=[jax.ShapeDtypeStruct((n_sum, 128), F32), jax.ShapeDtypeStruct((N_DEV, rows - n_sum, 128), F32)],
        compiler_params=pltpu.CompilerParams(vmem_limit_bytes=VMEM_LIMIT_V7X))(me, vec, land)


def _pair_exchange_into(view, land, cols, name):
    def body(v_ref, l_in, l_ref, send, recv):
        del l_in
        x, y, c = _place()
        sibling = (x, y, 1 - c)
        hr = l_ref.shape[1]
        for cp in _row_pieces(v_ref, (1 - c) * hr, l_ref, 0, hr, send, recv, sibling, cols):
            cp.start()
        pltpu.make_async_remote_copy(src_ref=_rows_of(v_ref, (1 - c) * hr, hr, cols), dst_ref=_rows_of(l_ref, 0, hr, cols),
                                     send_sem=send, recv_sem=recv, device_id=sibling, device_id_type=MESH).wait()

    any_spec = pl.BlockSpec(memory_space=pl.ANY)
    return pl.pallas_call(
        body, name=name, out_shape=jax.ShapeDtypeStruct(land.shape, land.dtype),
        in_specs=[any_spec, any_spec], out_specs=any_spec, input_output_aliases={1: 0},
        scratch_shapes=[pltpu.SemaphoreType.DMA, pltpu.SemaphoreType.DMA],
    )(view, land)


def _pair_sum(view, got, core, name):
    s, r, cols = view.shape
    hr = r // 2
    tr = _row_tile(hr, cols, itemsize=view.dtype.itemsize, budget=4 << 20)
    nb = hr // tr

    def body(core_ref, a_ref, b_ref, o_ref):
        del core_ref
        o_ref[...] = (a_ref[...].astype(F32) + b_ref[...].astype(F32)).astype(BF16)

    same = pl.BlockSpec((None, tr, cols), lambda j, i, core_ref: (j, i, 0))
    return pl.pallas_call(
        body, name=name,
        grid_spec=pltpu.PrefetchScalarGridSpec(
            num_scalar_prefetch=1, grid=(s, nb),
            in_specs=[pl.BlockSpec((None, tr, cols), lambda j, i, core_ref: (j, core_ref[0] * nb + i, 0)), same],
            out_specs=same),
        out_shape=jax.ShapeDtypeStruct((s, hr, cols), BF16), compiler_params=_params("parallel", "parallel"))(core, view, got)


def _pair_sum_small(views, gots, core, name):
    n = len(views)

    def body(core_ref, *refs):
        del core_ref
        for w in range(n):
            refs[2 * n + w][...] = (refs[w][...].astype(F32) + refs[n + w][...].astype(F32)).astype(BF16)

    halves = [(v.shape[0], v.shape[1] // 2, v.shape[2]) for v in views]
    own = [pl.BlockSpec(h, lambda i, core_ref: (0, core_ref[0], 0)) for h in halves]
    whole = [pl.BlockSpec(h, lambda i, core_ref: (0, 0, 0)) for h in halves]
    return pl.pallas_call(
        body, name=name,
        grid_spec=pltpu.PrefetchScalarGridSpec(num_scalar_prefetch=1, grid=(1,), in_specs=own + whole, out_specs=whole),
        out_shape=[jax.ShapeDtypeStruct(h, BF16) for h in halves], compiler_params=_params("arbitrary"))(core, *views, *gots)


def _piece_cols(part, col_sharded):
    return part.shape[2] // N_CHIPS if col_sharded else part.shape[2]


def _chip_sum_small(parts, gots, col_sharded, place, name):
    n = len(parts)

    def body(place_ref, *refs):
        del place_ref
        for w in range(n):
            got = refs[n + w]
            acc = refs[w][...].astype(F32) + got[0].astype(F32)
            refs[2 * n + w][...] = acc + got[1].astype(F32) + got[2].astype(F32)

    own, others, outs, shapes = [], [], [], []
    for p, cs in zip(parts, col_sharded):
        hr, cols = p.shape[1], _piece_cols(p, cs)
        own.append(pl.BlockSpec((None, hr, cols), (lambda i, pr: (0, 0, pr[0])) if cs else (lambda i, pr: (pr[0], 0, 0))))
        others.append(pl.BlockSpec((3, hr, cols), lambda i, pr: (0, 0, 0)))
        outs.append(pl.BlockSpec((hr, cols), lambda i, pr: (pr[1], 0)))
        shapes.append(jax.ShapeDtypeStruct((2 * hr, cols), F32))
    return pl.pallas_call(
        body, name=name,
        grid_spec=pltpu.PrefetchScalarGridSpec(num_scalar_prefetch=1, grid=(1,), in_specs=own + others, out_specs=outs),
        out_shape=shapes, compiler_params=_params("arbitrary"))(place, *parts, *gots)


def _chip_sum(part, got, col_sharded, place, name):
    _, hr, _ = part.shape
    cols = _piece_cols(part, col_sharded)
    tr = _row_tile(hr, cols)
    nb = hr // tr

    def body(place_ref, own_ref, g0_ref, g1_ref, g2_ref, o_ref):
        del place_ref
        acc = own_ref[...].astype(F32) + g0_ref[...].astype(F32)
        o_ref[...] = acc + g1_ref[...].astype(F32) + g2_ref[...].astype(F32)

    if col_sharded:
        own = pl.BlockSpec((None, tr, cols), lambda i, pr: (0, i, pr[0]))
    else:
        own = pl.BlockSpec((None, tr, cols), lambda i, pr: (pr[0], i, 0))
    others = [pl.BlockSpec((None, tr, cols), lambda i, pr, k=k: (k, i, 0)) for k in range(3)]
    return pl.pallas_call(
        body, name=name,
        grid_spec=pltpu.PrefetchScalarGridSpec(
            num_scalar_prefetch=1, grid=(nb,), in_specs=[own] + others,
            out_specs=pl.BlockSpec((tr, cols), lambda i, pr: (pr[1] * nb + i, 0))),
        out_shape=jax.ShapeDtypeStruct((2 * hr, cols), F32), compiler_params=_params("parallel"))(place, part, got, got, got)


def _row_tile(rows, cols, itemsize=4, budget=2 << 20):
    t = rows
    while t * cols * itemsize > budget and t % 16 == 0:
        t //= 2
    return t


def _adamw(w, g, m, v, name, after, also_g=False):
    rows, cols = w.shape
    tr = _row_tile(rows, cols, budget=2 << 20)
    extra = [] if after is None else [after]
    n_out = 4 if also_g else 3

    def body(w_ref, g_ref, m_ref, v_ref, *rest):
        d_ref, nm_ref, nv_ref = rest[len(extra):len(extra) + 3]
        g_ = g_ref[...]
        if also_g:
            rest[len(extra) + 3][...] = g_
        nm = ADAM_B1 * m_ref[...] + (1.0 - ADAM_B1) * g_
        nv = ADAM_B2 * v_ref[...] + (1.0 - ADAM_B2) * (g_ * g_)
        m_hat = nm / (1.0 - ADAM_B1 ** ADAM_STEP)
        v_hat = nv / (1.0 - ADAM_B2 ** ADAM_STEP)
        d_ref[...] = -ADAM_LR * (m_hat / (jnp.sqrt(v_hat) + ADAM_EPS) + ADAM_WD * w_ref[...])
        nm_ref[...] = nm
        nv_ref[...] = nv

    spec = pl.BlockSpec((tr, cols), lambda i: (i, 0))
    shp = jax.ShapeDtypeStruct((rows, cols), F32)
    return pl.pallas_call(body, name=name, grid=(rows // tr,),
                          in_specs=[spec] * 4 + [pl.BlockSpec(memory_space=pl.ANY)] * len(extra), out_specs=[spec] * n_out,
                          out_shape=[shp] * n_out, compiler_params=_params("parallel"))(w, g, m, v, *extra)


def _adamw_together(ws, gs, ms, vs, name, after):
    n = len(ws)

    def body(*refs):
        outs = refs[4 * n + 1:]
        for k in range(n):
            g_ = refs[n + k][...]
            nm = ADAM_B1 * refs[2 * n + k][...] + (1.0 - ADAM_B1) * g_
            nv = ADAM_B2 * refs[3 * n + k][...] + (1.0 - ADAM_B2) * (g_ * g_)
            m_hat = nm / (1.0 - ADAM_B1 ** ADAM_STEP)
            v_hat = nv / (1.0 - ADAM_B2 ** ADAM_STEP)
            outs[3 * k][...] = -ADAM_LR * (m_hat / (jnp.sqrt(v_hat) + ADAM_EPS) + ADAM_WD * refs[k][...])
            outs[3 * k + 1][...] = nm
            outs[3 * k + 2][...] = nv

    specs = [pl.BlockSpec(w.shape, lambda i: (0, 0)) for w in ws]
    out_specs, out_shape = [], []
    for w in ws:
        out_specs += [pl.BlockSpec(w.shape, lambda i: (0, 0))] * 3
        out_shape += [jax.ShapeDtypeStruct(w.shape, F32)] * 3
    res = pl.pallas_call(body, name=name, grid=(1,), in_specs=specs * 4 + [pl.BlockSpec(memory_space=pl.ANY)],
                         out_specs=out_specs, out_shape=out_shape, compiler_params=_params("arbitrary"))(*ws, *gs, *ms, *vs, after)
    return [tuple(res[3 * k:3 * k + 3]) for k in range(n)]


def _all_devices(x, y, c):
    out = []
    for k in range(1, N_DEV):
        peer = (x ^ ((k >> 2) & 1), y ^ ((k >> 1) & 1), c ^ (k & 1))
        out.append((peer, 4 * peer[0] + 2 * peer[1] + peer[2]))
    return out


def _ada_exchange(c, w_shard, b_ada):
    bsz, d = c.shape
    cs = w_shard.shape[1]

    def body(c_ref, w_ref, b_ref, mod_ref, act_ref, c_all, part, pieces, csend, crecv, psend, precv):
        x, y, core = _place()
        me = 4 * x + 2 * y + core
        chip = 2 * x + y
        c_all[me] = c_ref[...]
        peers = _all_devices(x, y, core)
        copies = []
        for k, (peer, _) in enumerate(peers):
            cp = pltpu.make_async_remote_copy(src_ref=c_ref, dst_ref=c_all.at[me], send_sem=csend.at[k], recv_sem=crecv.at[k],
                                              device_id=peer, device_id_type=MESH)
            cp.start()
            copies.append(cp)
        for k, (_, src) in enumerate(peers):
            pltpu.make_async_remote_copy(src_ref=c_ref, dst_ref=c_all.at[src], send_sem=csend.at[k], recv_sem=crecv.at[k],
                                         device_id=(x, y, core), device_id_type=MESH).wait_recv()
        rows = jnp.concatenate([c_all[i] for i in range(N_DEV)], axis=0)
        act = rows * _sigmoid(rows)
        act_ref[...] = act
        prod = jnp.dot(act.astype(BF16), w_ref[...].astype(BF16), preferred_element_type=F32)
        for i in range(N_DEV):
            part[i] = prod[i * bsz:(i + 1) * bsz, :]
        pieces[chip] = part[me]
        chips = _other_chips(x, y)
        for k, (cx, cy) in enumerate(chips):
            cp = pltpu.make_async_remote_copy(src_ref=part.at[4 * cx + 2 * cy + core], dst_ref=pieces.at[chip],
                                              send_sem=psend.at[k], recv_sem=precv.at[k], device_id=(cx, cy, core),
                                              device_id_type=MESH)
            cp.start()
            copies.append(cp)
        for k, (cx, cy) in enumerate(chips):
            pltpu.make_async_remote_copy(src_ref=part.at[me], dst_ref=pieces.at[2 * cx + cy], send_sem=psend.at[k],
                                         recv_sem=precv.at[k], device_id=(cx, cy, core), device_id_type=MESH).wait_recv()
        for cp in copies:
            cp.wait_send()
        mod_ref[...] = jnp.concatenate([pieces[j] for j in range(N_CHIPS)], axis=1) + b_ref[...]

    vm = pl.BlockSpec(memory_space=pltpu.VMEM)
    return pl.pallas_call(
        body, name="ada_exchange", in_specs=[vm] * 3, out_specs=[vm] * 2,
        out_shape=[jax.ShapeDtypeStruct((bsz, 3 * d), F32), jax.ShapeDtypeStruct((N_DEV * bsz, d), F32)],
        scratch_shapes=[pltpu.VMEM((N_DEV, bsz, d), F32), pltpu.VMEM((N_DEV, bsz, cs), F32), pltpu.VMEM((N_CHIPS, bsz, cs), F32),
                        pltpu.SemaphoreType.DMA((N_DEV - 1,)), pltpu.SemaphoreType.DMA((N_DEV - 1,)),
                        pltpu.SemaphoreType.DMA((3,)), pltpu.SemaphoreType.DMA((3,))],
        compiler_params=pltpu.CompilerParams(vmem_limit_bytes=VMEM_LIMIT_V7X))(c, w_shard, b_ada)


def _grad_and_adamw_w_ada(act_t, d_cols, w, m, v):
    d, n = act_t.shape
    cs = d_cols.shape[1]
    tn = min(256, cs)

    def body(a_ref, g_ref, w_ref, m_ref, v_ref, go_ref, d_ref, nm_ref, nv_ref):
        a, g = a_ref[...], g_ref[...]
        acc = a[:, 0:1] * g[0:1, :]
        for b in range(1, n):
            acc = acc + a[:, b:b + 1] * g[b:b + 1, :]
        go_ref[...] = acc
        nm = ADAM_B1 * m_ref[...] + (1.0 - ADAM_B1) * acc
        nv = ADAM_B2 * v_ref[...] + (1.0 - ADAM_B2) * (acc * acc)
        m_hat = nm / (1.0 - ADAM_B1 ** ADAM_STEP)
        v_hat = nv / (1.0 - ADAM_B2 ** ADAM_STEP)
        d_ref[...] = -ADAM_LR * (m_hat / (jnp.sqrt(v_hat) + ADAM_EPS) + ADAM_WD * w_ref[...])
        nm_ref[...] = nm
        nv_ref[...] = nv

    tile = pl.BlockSpec((d, tn), lambda j: (0, j))
    shp = jax.ShapeDtypeStruct((d, cs), F32)
    return pl.pallas_call(body, name="adamw_w_ada", grid=(cs // tn,),
                          in_specs=[pl.BlockSpec((d, n), lambda j: (0, 0)), pl.BlockSpec((n, tn), lambda j: (0, j)), tile, tile, tile],
                          out_specs=[tile] * 4, out_shape=[shp] * 4, compiler_params=_params("parallel"))(act_t, d_cols, w, m, v)


def _permute_w_in(w_nat, lay):
    d = lay.d
    group = 4

    def call(name, width, n_pieces, nat_piece, out_block0, prev):
        def body(*refs):
            refs[-1][...] = jnp.concatenate([r[...] for r in refs[:group]], axis=1)

        in_specs = [pl.BlockSpec((d, width), lambda s, m=m: (0, nat_piece(group * s + m))) for m in range(group)]
        args = [w_nat] * group
        aliases = {}
        if prev is not None:
            in_specs.append(pl.BlockSpec(memory_space=pl.ANY))
            args.append(prev)
            aliases = {group: 0}
        return pl.pallas_call(
            body, name=name, grid=(n_pieces // group,), in_specs=in_specs,
            out_specs=pl.BlockSpec((d, group * width), lambda s: (0, out_block0 + s)),
            out_shape=jax.ShapeDtypeStruct((d, lay.np), BF16), input_output_aliases=aliases,
            compiler_params=_params("arbitrary"))(*args)

    w_all = call("permute_w_attn", SLAB, 2 * PAIR_SLABS, lay.attn_nat_slab, 0, None)
    n_rest = 6 * d // CONV_TILE
    if n_rest % group:
        group = 2
    return call("permute_w_rest", CONV_TILE, n_rest, lay.rest_nat_tile, lay.c0 // (group * CONV_TILE), w_all)


def _project(x2, mod3, w_all, b_all, seq, col0, ncols, tn, out_dtype, want_ht, name):
    t, d = x2.shape
    tm = min(2048, seq)
    per_seq = seq // tm
    j0 = col0 // tn

    def body(x_ref, mod_ref, w_ref, b_ref, o_ref, *rest):
        h_ref = rest[-1]

        @pl.when(pl.program_id(1) == 0)
        def _():
            h = x_ref[...] * (1.0 + mod_ref[:, d:2 * d]) + mod_ref[:, 0:d]
            h_ref[...] = h.astype(BF16)
            if want_ht:
                rest[0][...] = h.T.astype(BF16)

        o_ref[...] = (jnp.dot(h_ref[...], w_ref[...], preferred_element_type=F32) + b_ref[...]).astype(out_dtype)

    out_shape = [jax.ShapeDtypeStruct((t, ncols), out_dtype)]
    out_specs = [pl.BlockSpec((tm, tn), lambda i, j: (i, j))]
    if want_ht:
        out_shape.append(jax.ShapeDtypeStruct((d, t), BF16))
        out_specs.append(pl.BlockSpec((d, tm), lambda i, j: (0, i)))
    return pl.pallas_call(
        body, name=name, grid=(t // tm, ncols // tn),
        in_specs=[pl.BlockSpec((tm, d), lambda i, j: (i, 0)),
                  pl.BlockSpec((None, 1, 3 * d), lambda i, j: (i // per_seq, 0, 0)),
                  pl.BlockSpec((d, tn), lambda i, j: (0, j0 + j)),
                  pl.BlockSpec((1, tn), lambda i, j: (0, j0 + j))],
        out_specs=out_specs, out_shape=out_shape,
        scratch_shapes=[pltpu.VMEM((tm, d), BF16)],
        compiler_params=_params("arbitrary", "arbitrary"))(x2, mod3, w_all, b_all)


def _slope(g, p, hh):
    head = 4 * g + 2 * p + hh
    return 2.0 ** (-ALIBI_MAX_EXP * (head + 1.0) / N_HEADS)


def _ld_rows(ref, start, n, stride):
    if stride == 1:
        return ref[pl.ds(start, n), :]
    return ref[pl.ds(start, n, stride=stride), :]


def _st_rows(ref, start, n, stride, val):
    if stride == 1:
        ref[pl.ds(start, n), :] = val
    else:
        ref[pl.ds(start, n, stride=stride), :] = val


def _sub_blocks(g, seq):
    return seq // DILATIONS[g] // SUB


def _key_rows(g, seq):
    return SUB if _sub_blocks(g, seq) == 1 else 2 * SUB


def _fill_bias(bias_ref, p, seq):
    for g in range(N_GROUPS):
        nk = _key_rows(g, seq)
        diff = lax.broadcasted_iota(jnp.int32, (SUB, nk), 0) - lax.broadcasted_iota(jnp.int32, (SUB, nk), 1)
        for i, off in enumerate((0, SUB)):
            if i == 1 and nk == SUB:
                continue
            delta = diff + off
            ok = (delta >= 0) & (delta <= SUB)
            dist = (delta * DILATIONS[g]).astype(F32)
            for hh in range(2):
                slope = jnp.where(p == 0, _slope(g, 0, hh), _slope(g, 1, hh))
                bias_ref[g, i, hh, :, 0:nk] = jnp.where(ok, -slope * dist, NEG)


def _to_sub_major(pa_ref, col, sub_ref, stage, dil, seq):
    cols = slice(col * SLAB, (col + 1) * SLAB)
    if dil == 1:
        sub_ref[...] = pa_ref[:, cols]
        return
    n = seq // dil
    stage[...] = pa_ref[:, cols].astype(F32)
    for r in range(dil):
        sub_ref[pl.ds(r * n, n), :] = stage[pl.ds(r, n, stride=dil), :].astype(BF16)


def _block_rows(it, g, seq):
    dil, nb = DILATIONS[g], _sub_blocks(g, seq)
    row0 = pl.multiple_of(it * SUB, SUB)
    if nb == 1:
        return row0, row0, 0, it
    blk = it % nb
    first = blk == 0
    krow0 = pl.multiple_of(row0 - jnp.where(first, 0, SUB), SUB)
    nat = row0 if dil == 1 else it // nb + dil * SUB * blk
    return row0, krow0, jnp.where(first, 0, 1), nat


def _nt(a, b):
    return lax.dot_general(a, b, (((1,), (1,)), ((), ())), preferred_element_type=F32)


def _tn(a, b):
    return lax.dot_general(a, b, (((0,), (0,)), ((), ())), preferred_element_type=F32)


def _head_sums(t):
    rows = t.shape[0]
    lo = jnp.broadcast_to(jnp.sum(t[:, :HEAD_DIM], axis=-1, keepdims=True), (rows, HEAD_DIM))
    hi = jnp.broadcast_to(jnp.sum(t[:, HEAD_DIM:], axis=-1, keepdims=True), (rows, HEAD_DIM))
    return jnp.concatenate([lo, hi], axis=-1)


def _attn_fwd(pa, bsz, seq):
    t = pa.shape[0]
    n_blocks = seq // SUB
    chunk = 256

    def body(pa_ref, o_ref, lse_ref, a_ref, sub, stage, bias_ref, s_buf, p_buf, l_buf):
        p = pl.program_id(1)
        _fill_bias(bias_ref, p, seq)
        head0 = lax.broadcasted_iota(jnp.int32, (SUB, SLAB), 1) < HEAD_DIM
        for g in range(N_GROUPS):
            dil = DILATIONS[g]
            for w in range(3):
                _to_sub_major(pa_ref, 3 * w + g, sub.at[w], stage, dil, seq)
            nk = _key_rows(g, seq)

            def trip(i, carry, g=g, dil=dil, nk=nk):
                places = [_block_rows(BLOCKS_PER_TRIP * i + j, g, seq) for j in range(BLOCKS_PER_TRIP)]
                for j, (row0, krow0, _, _) in enumerate(places):
                    q = sub[0, pl.ds(row0, SUB), :]
                    zero = jnp.zeros_like(q)
                    q2 = jnp.concatenate([jnp.where(head0, q, zero), jnp.where(head0, zero, q)], axis=0) * (HEAD_DIM ** -0.5)
                    s_buf[j, :, 0:nk] = _nt(q2, sub[1, pl.ds(krow0, nk), :])
                for j, (_, _, bi, _) in enumerate(places):
                    for c in range(0, 2 * SUB, SOFTMAX_ROWS):
                        hh, r = divmod(c, SUB)
                        s = s_buf[j, c:c + SOFTMAX_ROWS, 0:nk] + bias_ref[g, bi, hh, r:r + SOFTMAX_ROWS, 0:nk]
                        m = jnp.max(s, axis=-1, keepdims=True)
                        e = jnp.exp(s - m)
                        den = jnp.sum(e, axis=-1, keepdims=True)
                        p_buf[j, c:c + SOFTMAX_ROWS, 0:nk] = (e * (1.0 / den)).astype(BF16)
                        l_buf[j, c:c + SOFTMAX_ROWS, :] = jnp.broadcast_to(m + jnp.log(den), (SOFTMAX_ROWS, SLAB))
                for j, (_, krow0, _, nat) in enumerate(places):
                    o2 = jnp.dot(p_buf[j, :, 0:nk], sub[2, pl.ds(krow0, nk), :], preferred_element_type=F32)
                    _st_rows(o_ref.at[g], nat, SUB, dil, jnp.where(head0, o2[0:SUB], o2[SUB:2 * SUB]))
                    _st_rows(lse_ref.at[g], nat, SUB, dil, jnp.where(head0, l_buf[j, 0:SUB, :], l_buf[j, SUB:2 * SUB, :]))
                return carry

            lax.fori_loop(0, n_blocks // BLOCKS_PER_TRIP, trip, 0)

        def mix(i, carry):
            rows = pl.ds(pl.multiple_of(i * chunk, chunk), chunk)
            l0, l1, l2 = lse_ref[0, rows, :], lse_ref[1, rows, :], lse_ref[2, rows, :]
            m = jnp.maximum(jnp.maximum(l0, l1), l2)
            e0, e1, e2 = jnp.exp(l0 - m), jnp.exp(l1 - m), jnp.exp(l2 - m)
            tot = e0 + e1 + e2
            o = (e0 / tot) * o_ref[0, rows, :] + (e1 / tot) * o_ref[1, rows, :] + (e2 / tot) * o_ref[2, rows, :]
            z = pa_ref[rows, 9 * SLAB:10 * SLAB].astype(F32)
            a_ref[rows, :] = (o * (z * _sigmoid(z))).astype(BF16)
            return carry

        lax.fori_loop(0, seq // chunk, mix, 0)

    big = jax.ShapeDtypeStruct((N_GROUPS, t, 2 * SLAB), F32)
    return pl.pallas_call(
        body, name="attn_fwd", grid=(bsz, 2),
        in_specs=[pl.BlockSpec((seq, PAIR_COLS), lambda b, p: (b, p))],
        out_specs=[pl.BlockSpec((N_GROUPS, seq, SLAB), lambda b, p: (0, b, p)),
                   pl.BlockSpec((N_GROUPS, seq, SLAB), lambda b, p: (0, b, p)),
                   pl.BlockSpec((seq, SLAB), lambda b, p: (b, p))],
        out_shape=[big, big, jax.ShapeDtypeStruct((t, 2 * SLAB), BF16)],
        scratch_shapes=[pltpu.VMEM((3, seq, SLAB), BF16), pltpu.VMEM((seq, SLAB), F32),
                        pltpu.VMEM((N_GROUPS, 2, 2, SUB, 2 * SUB), F32), pltpu.VMEM((BLOCKS_PER_TRIP, 2 * SUB, 2 * SUB), F32),
                        pltpu.VMEM((BLOCKS_PER_TRIP, 2 * SUB, 2 * SUB), BF16), pltpu.VMEM((BLOCKS_PER_TRIP, 2 * SUB, SLAB), F32)],
        compiler_params=_params("arbitrary", "arbitrary"))(pa)


def _shift_down(v, k, rows):
    return jnp.where(rows >= k, pltpu.roll(v, k, 0), 0.0)


def _shift_up(v, k, rows):
    n = v.shape[0]
    return jnp.where(rows < n - k, pltpu.roll(v, n - k, 0), 0.0)


def _conv_fwd(pr, conv_w, bsz, seq, d):
    t = pr.shape[0]
    ct = CONV_TILE

    def body(p_ref, cw_ref, o_ref):
        u = p_ref[:, 2 * ct:3 * ct].astype(F32) * p_ref[:, 0:ct].astype(F32)
        cw = cw_ref[...]
        rows = lax.broadcasted_iota(jnp.int32, u.shape, 0)
        conv = cw[0:1, :] * _shift_down(u, 2, rows)
        conv = conv + cw[1:2, :] * _shift_down(u, 1, rows)
        conv = conv + cw[2:3, :] * u
        z = p_ref[:, 3 * ct:4 * ct].astype(F32)
        o_ref[...] = (p_ref[:, ct:2 * ct].astype(F32) * conv * (z * _sigmoid(z))).astype(BF16)

    return pl.pallas_call(
        body, name="conv_fwd", grid=(bsz, d // ct),
        in_specs=[pl.BlockSpec((seq, 4 * ct), lambda b, j: (b, j)), pl.BlockSpec((3, ct), lambda b, j: (0, j))],
        out_specs=pl.BlockSpec((seq, ct), lambda b, j: (b, j)),
        out_shape=jax.ShapeDtypeStruct((t, d), BF16), compiler_params=_params("parallel", "parallel"))(pr, conv_w)


def _tail(a_in, b_in, pr, x2, target2, mod3, w_pa, w_pc, w_out, b_out, ln_g, ln_b, seq, lay):
    t, d = x2.shape
    tm = 512
    per_seq = seq // tm
    n_steps = t // tm
    gate_blk = 4 * d // d

    def nt(a, b):
        return lax.dot_general(a, b, (((1,), (1,)), ((), ())), preferred_element_type=F32)

    def tn(a, b):
        return lax.dot_general(a, b, (((0,), (0,)), ((), ())), preferred_element_type=F32)

    def body(a_ref, b_ref, ga_ref, gb_ref, x_ref, tg_ref, mod_ref, wpa_ref, wpc_ref, wo_ref, bo_ref, lg_ref, lb_ref,
             dpg_ref, da_ref, db_ref, gx_ref, dgate_ref, small_ref, gwpa_hbm, gwpc_hbm, gwo_hbm,
             acc_pa, acc_pc, acc_o, sem):
        i = pl.program_id(0)

        @pl.when(i == 0)
        def _():
            acc_pa[...] = jnp.zeros_like(acc_pa)
            acc_pc[...] = jnp.zeros_like(acc_pc)
            acc_o[...] = jnp.zeros_like(acc_o)
            small_ref[...] = jnp.zeros_like(small_ref)

        @pl.when(i % per_seq == 0)
        def _():
            dgate_ref[...] = jnp.zeros_like(dgate_ref)

        halves = [slice(k * (tm // 2), (k + 1) * (tm // 2)) for k in range(2)]
        gate = mod_ref[:, 2 * d:3 * d]
        a_bf = [a_ref[rs, :] for rs in halves]
        b_bf = [b_ref[rs, :] for rs in halves]
        y_attn = [jnp.dot(a, wpa_ref[...], preferred_element_type=F32) for a in a_bf]
        y_conv = [jnp.dot(b, wpc_ref[...], preferred_element_type=F32) for b in b_bf]
        sa = [_sigmoid(ga_ref[rs, :].astype(F32)) for rs in halves]
        sb = [_sigmoid(gb_ref[rs, :].astype(F32)) for rs in halves]
        merged = [(sa[k] * y_attn[k] + sb[k] * y_conv[k]).astype(BF16) for k in range(2)]
        mo = [jnp.dot(m, wo_ref[...], preferred_element_type=F32) + bo_ref[...] for m in merged]
        d_mo_bf = []
        for k, rs in enumerate(halves):
            r = ALPHA * x_ref[rs, :] + gate * mo[k]
            mu = jnp.mean(r, axis=-1, keepdims=True)
            cen = r - mu
            var = jnp.mean(cen * cen, axis=-1, keepdims=True)
            rstd = lax.rsqrt(var + LN_EPS)
            xhat = cen * rstd
            err = xhat * lg_ref[...] + lb_ref[...] - tg_ref[rs, :]
            dy = err * (1.0 / d)
            dxhat = dy * lg_ref[...]
            dr = rstd * (dxhat - jnp.mean(dxhat, axis=-1, keepdims=True)
                         - xhat * jnp.mean(dxhat * xhat, axis=-1, keepdims=True))
            gx_ref[rs, :] = ALPHA * dr
            dgate_ref[...] += jnp.sum(dr * mo[k], axis=0, keepdims=True)
            d_mo = dr * gate
            small_ref[0:1, :] += jnp.sum(d_mo, axis=0, keepdims=True)
            small_ref[1:2, :] += jnp.sum(dy * xhat, axis=0, keepdims=True)
            small_ref[2:3, :] += jnp.sum(dy, axis=0, keepdims=True)
            small_ref[3:4, :] += jnp.sum(err * err, axis=0, keepdims=True)
            d_mo_bf.append(d_mo.astype(BF16))
        acc_o[...] += tn(jnp.concatenate(merged, axis=0), jnp.concatenate(d_mo_bf, axis=0))
        dmerged = [nt(g, wo_ref[...]) for g in d_mo_bf]
        dy_attn, dy_conv = [], []
        for k, rs in enumerate(halves):
            dy_attn.append((dmerged[k] * sa[k]).astype(BF16))
            dy_conv.append((dmerged[k] * sb[k]).astype(BF16))
            dpg_ref[rs, 0:d] = (dmerged[k] * y_attn[k] * sa[k] * (1.0 - sa[k])).astype(BF16)
            dpg_ref[rs, d:2 * d] = (dmerged[k] * y_conv[k] * sb[k] * (1.0 - sb[k])).astype(BF16)
        acc_pa[...] += tn(a_ref[...], jnp.concatenate(dy_attn, axis=0))
        acc_pc[...] += tn(b_ref[...], jnp.concatenate(dy_conv, axis=0))
        for k, rs in enumerate(halves):
            da_ref[rs, :] = nt(dy_attn[k], wpa_ref[...])
            db_ref[rs, :] = nt(dy_conv[k], wpc_ref[...])

        @pl.when(i == n_steps - 1)
        def _():
            copies = [pltpu.make_async_copy(acc_pa, gwpa_hbm, sem.at[0]), pltpu.make_async_copy(acc_pc, gwpc_hbm, sem.at[1]),
                      pltpu.make_async_copy(acc_o, gwo_hbm, sem.at[2])]
            for cp in copies:
                cp.start()
            for cp in copies:
                cp.wait()

    row = lambda w: pl.BlockSpec((tm, w), lambda i: (i, 0))
    const = lambda shp: pl.BlockSpec(shp, lambda i: (0,) * len(shp), pipeline_mode=pl.Buffered(1))
    any_spec = pl.BlockSpec(memory_space=pl.ANY)
    return pl.pallas_call(
        body, name="tail", grid=(n_steps,),
        in_specs=[row(Z_WIDTH), row(d),
                  pl.BlockSpec((tm, d), lambda i: (i, gate_blk)), pl.BlockSpec((tm, d), lambda i: (i, gate_blk + 1)),
                  row(d), row(d), pl.BlockSpec((None, 1, 3 * d), lambda i: (i // per_seq, 0, 0)),
                  const((Z_WIDTH, d)), const((d, d)), const((d, d)), const((1, d)), const((1, d)), const((1, d))],
        out_specs=[pl.BlockSpec((tm, 2 * d), lambda i: (i, lay.g0 // (2 * d))), row(Z_WIDTH), row(d), row(d),
                   pl.BlockSpec((None, 1, d), lambda i: (i // per_seq, 0, 0)), pl.BlockSpec((8, d), lambda i: (0, 0)),
                   HBM_SPEC, HBM_SPEC, HBM_SPEC],
        out_shape=[jax.ShapeDtypeStruct((t, lay.np), BF16), jax.ShapeDtypeStruct((t, Z_WIDTH), F32),
                   jax.ShapeDtypeStruct((t, d), F32), jax.ShapeDtypeStruct((t, d), F32),
                   jax.ShapeDtypeStruct((t // seq, 1, d), F32), jax.ShapeDtypeStruct((8, d), F32),
                   pltpu.HBM((Z_WIDTH, d), F32), pltpu.HBM((d, d), F32), pltpu.HBM((d, d), F32)],
        scratch_shapes=[pltpu.VMEM((Z_WIDTH, d), F32), pltpu.VMEM((d, d), F32), pltpu.VMEM((d, d), F32),
                        pltpu.SemaphoreType.DMA((3,))],
        compiler_params=_params("arbitrary"),
    )(a_in, b_in, pr, pr, x2, target2, mod3, w_pa, w_pc, w_out, b_out, ln_g, ln_b)


def _conv_bwd(dproj, db, pr, conv_w, bsz, seq, lay):
    d = lay.d
    ct = CONV_TILE
    base = lay.c0 // (4 * ct)

    def body(dp_in, db_ref, p_ref, cw_ref, dp_ref, gcw_ref):
        del dp_in
        u_x, g_b, g_c, z = [p_ref[:, k * ct:(k + 1) * ct].astype(F32) for k in range(4)]
        cw = cw_ref[...]
        u = g_c * u_x
        rows = lax.broadcasted_iota(jnp.int32, u.shape, 0)
        u1, u2 = _shift_down(u, 1, rows), _shift_down(u, 2, rows)
        conv = cw[0:1, :] * u2 + cw[1:2, :] * u1 + cw[2:3, :] * u
        sig = _sigmoid(z)
        sl = z * sig
        dbv = db_ref[...]
        gbc = g_b * conv
        dp_ref[:, ct:2 * ct] = (dbv * sl * conv).astype(BF16)
        dp_ref[:, 3 * ct:4 * ct] = (dbv * gbc * (sig * (1.0 + z * (1.0 - sig)))).astype(BF16)
        dconv = dbv * sl * g_b

        @pl.when(pl.program_id(1) == 0)
        def _():
            gcw_ref[...] = jnp.zeros_like(gcw_ref)

        gcw_ref[0:1, :] += jnp.sum(dconv * u2, axis=0, keepdims=True)
        gcw_ref[1:2, :] += jnp.sum(dconv * u1, axis=0, keepdims=True)
        gcw_ref[2:3, :] += jnp.sum(dconv * u, axis=0, keepdims=True)
        du = cw[2:3, :] * dconv + cw[1:2, :] * _shift_up(dconv, 1, rows) + cw[0:1, :] * _shift_up(dconv, 2, rows)
        dp_ref[:, 0:ct] = (du * g_c).astype(BF16)
        dp_ref[:, 2 * ct:3 * ct] = (du * u_x).astype(BF16)

    return pl.pallas_call(
        body, name="conv_bwd", grid=(d // ct, bsz),
        in_specs=[pl.BlockSpec(memory_space=pl.ANY), pl.BlockSpec((seq, ct), lambda j, b: (b, j)),
                  pl.BlockSpec((seq, 4 * ct), lambda j, b: (b, j)), pl.BlockSpec((3, ct), lambda j, b: (0, j))],
        out_specs=[pl.BlockSpec((seq, 4 * ct), lambda j, b: (b, base + j)), pl.BlockSpec((8, ct), lambda j, b: (0, j))],
        out_shape=[jax.ShapeDtypeStruct(dproj.shape, BF16), jax.ShapeDtypeStruct((8, d), F32)],
        input_output_aliases={0: 0}, compiler_params=_params("arbitrary", "arbitrary"))(dproj, db, pr, conv_w)


def _attn_bwd(dproj, pa, o_all, lse_all, da, bsz, seq, after):
    n_blocks = seq // SUB
    chunk = 256
    per_trip = BLOCKS_PER_TRIP_BWD

    def body(dp_in, pa_ref, o_ref, lse_ref, da_ref, after_ref, dp_ref, sub, stage, dsub, dog, cvec, bias_ref,
             s_buf, dp_buf, ds_buf, pb_buf, q2_buf, do2_buf, l_buf, c_buf):
        del dp_in, after_ref
        p = pl.program_id(1)
        _fill_bias(bias_ref, p, seq)
        head0 = lax.broadcasted_iota(jnp.int32, (SUB, SLAB), 1) < HEAD_DIM

        def mix_bwd(i, carry):
            rows = pl.ds(pl.multiple_of(i * chunk, chunk), chunk)
            ls = [lse_ref[g, rows, :] for g in range(N_GROUPS)]
            os_ = [o_ref[g, rows, :] for g in range(N_GROUPS)]
            m = jnp.maximum(jnp.maximum(ls[0], ls[1]), ls[2])
            es = [jnp.exp(l - m) for l in ls]
            tot = es[0] + es[1] + es[2]
            ws = [e / tot for e in es]
            o = ws[0] * os_[0] + ws[1] * os_[1] + ws[2] * os_[2]
            z = pa_ref[rows, 9 * SLAB:10 * SLAB].astype(F32)
            sig = _sigmoid(z)
            dav = da_ref[rows, :]
            do = dav * (z * sig)
            dp_ref[rows, 9 * SLAB:10 * SLAB] = (dav * o * (sig * (1.0 + z * (1.0 - sig)))).astype(BF16)
            wsum = _head_sums(do * o)
            for g in range(N_GROUPS):
                dog[g, rows, :] = ws[g] * do
                cvec[g, rows, :] = -(ws[g] * wsum)
            return carry

        lax.fori_loop(0, seq // chunk, mix_bwd, 0)

        for g in range(N_GROUPS):
            dil = DILATIONS[g]
            for w in range(3):
                _to_sub_major(pa_ref, 3 * w + g, sub.at[w], stage, dil, seq)
            dsub[1] = jnp.zeros((seq, SLAB), F32)
            dsub[2] = jnp.zeros((seq, SLAB), F32)
            nk = _key_rows(g, seq)

            def trip(i, carry, g=g, dil=dil, nk=nk):
                places = [_block_rows(per_trip * i + j, g, seq) for j in range(per_trip)]
                for j, (row0, krow0, _, nat) in enumerate(places):
                    q = sub[0, pl.ds(row0, SUB), :]
                    do = _ld_rows(dog.at[g], nat, SUB, dil).astype(BF16)
                    zero = jnp.zeros_like(q)
                    q2 = jnp.concatenate([jnp.where(head0, q, zero), jnp.where(head0, zero, q)], axis=0)
                    do2 = jnp.concatenate([jnp.where(head0, do, zero), jnp.where(head0, zero, do)], axis=0)
                    q2_buf[j] = q2
                    do2_buf[j] = do2
                    s_buf[j, :, 0:nk] = _nt(q2 * (HEAD_DIM ** -0.5), sub[1, pl.ds(krow0, nk), :])
                    dp_buf[j, :, 0:nk] = _nt(do2, sub[2, pl.ds(krow0, nk), :])
                    l_buf[j] = _ld_rows(lse_ref.at[g], nat, SUB, dil)
                    c_buf[j] = _ld_rows(cvec.at[g], nat, SUB, dil)
                for j, (_, _, bi, _) in enumerate(places):
                    for c in range(0, 2 * SUB, SOFTMAX_ROWS):
                        hh, r = divmod(c, SUB)
                        lane = hh * HEAD_DIM
                        s = s_buf[j, c:c + SOFTMAX_ROWS, 0:nk] + bias_ref[g, bi, hh, r:r + SOFTMAX_ROWS, 0:nk]
                        prob = jnp.exp(s - l_buf[j, r:r + SOFTMAX_ROWS, lane:lane + 1])
                        dprob = dp_buf[j, c:c + SOFTMAX_ROWS, 0:nk] + c_buf[j, r:r + SOFTMAX_ROWS, lane:lane + 1]
                        ds_buf[j, c:c + SOFTMAX_ROWS, 0:nk] = (prob * dprob * (HEAD_DIM ** -0.5)).astype(BF16)
                        pb_buf[j, c:c + SOFTMAX_ROWS, 0:nk] = prob.astype(BF16)
                for j, (row0, krow0, _, _) in enumerate(places):
                    ds = ds_buf[j, :, 0:nk]
                    dq2 = jnp.dot(ds, sub[1, pl.ds(krow0, nk), :], preferred_element_type=F32)
                    dsub[0, pl.ds(row0, SUB), :] = jnp.where(head0, dq2[0:SUB], dq2[SUB:2 * SUB])
                    dsub[1, pl.ds(krow0, nk), :] += _tn(ds, q2_buf[j])
                    dsub[2, pl.ds(krow0, nk), :] += _tn(pb_buf[j, :, 0:nk], do2_buf[j])
                return carry

            lax.fori_loop(0, n_blocks // per_trip, trip, 0)
            for w in range(3):
                cols = slice((3 * w + g) * SLAB, (3 * w + g + 1) * SLAB)
                if dil == 1:
                    dp_ref[:, cols] = dsub[w].astype(BF16)
                else:
                    n = seq // dil
                    for r in range(dil):
                        stage[pl.ds(r, n, stride=dil), :] = dsub[w, pl.ds(r * n, n), :]
                    dp_ref[:, cols] = stage[...].astype(BF16)

    return pl.pallas_call(
        body, name="attn_bwd", grid=(bsz, 2),
        in_specs=[pl.BlockSpec(memory_space=pl.ANY), pl.BlockSpec((seq, PAIR_COLS), lambda b, p: (b, p)),
                  pl.BlockSpec((N_GROUPS, seq, SLAB), lambda b, p: (0, b, p)),
                  pl.BlockSpec((N_GROUPS, seq, SLAB), lambda b, p: (0, b, p)),
                  pl.BlockSpec((seq, SLAB), lambda b, p: (b, p)), pl.BlockSpec(memory_space=pl.ANY)],
        out_specs=pl.BlockSpec((seq, PAIR_COLS), lambda b, p: (b, p)),
        out_shape=jax.ShapeDtypeStruct(dproj.shape, BF16), input_output_aliases={0: 0},
        scratch_shapes=[pltpu.VMEM((3, seq, SLAB), BF16), pltpu.VMEM((seq, SLAB), F32), pltpu.VMEM((3, seq, SLAB), F32),
                        pltpu.VMEM((3, seq, SLAB), F32), pltpu.VMEM((3, seq, SLAB), F32),
                        pltpu.VMEM((N_GROUPS, 2, 2, SUB, 2 * SUB), F32),
                        pltpu.VMEM((per_trip, 2 * SUB, 2 * SUB), F32), pltpu.VMEM((per_trip, 2 * SUB, 2 * SUB), F32),
                        pltpu.VMEM((per_trip, 2 * SUB, 2 * SUB), BF16), pltpu.VMEM((per_trip, 2 * SUB, 2 * SUB), BF16),
                        pltpu.VMEM((per_trip, 2 * SUB, SLAB), BF16), pltpu.VMEM((per_trip, 2 * SUB, SLAB), BF16),
                        pltpu.VMEM((per_trip, SUB, SLAB), F32), pltpu.VMEM((per_trip, SUB, SLAB), F32)],
        compiler_params=_params("arbitrary", "arbitrary"))(dproj, pa, o_all, lse_all, da, after)


def _grad_h(dproj, w_all, gx0, x2, mod3, seq, lay):
    t, d = x2.shape
    tm, tn = min(512, seq), min(512, d)
    per_seq = seq // tm

    def body(dp_ref, w_ref, gx0_ref, x_ref, scale_ref, gx_ref, dmod_ref):
        dh = _nt(dp_ref[:, 0:ATT], w_ref[:, 0:ATT]) + _nt(dp_ref[:, lay.c0:], w_ref[:, lay.c0:])
        gx_ref[...] = gx0_ref[...] + dh * (1.0 + scale_ref[...])

        @pl.when(pl.program_id(1) % per_seq == 0)
        def _():
            dmod_ref[...] = jnp.zeros_like(dmod_ref)

        dmod_ref[0:1, :] += jnp.sum(dh, axis=0, keepdims=True)
        dmod_ref[1:2, :] += jnp.sum(dh * x_ref[...], axis=0, keepdims=True)

    tile = pl.BlockSpec((tm, tn), lambda j, i: (i, j))
    return pl.pallas_call(
        body, name="grad_h", grid=(d // tn, t // tm),
        in_specs=[pl.BlockSpec((tm, lay.np), lambda j, i: (i, 0)), pl.BlockSpec((tn, lay.np), lambda j, i: (j, 0)),
                  tile, tile,
                  pl.BlockSpec((None, 1, tn), lambda j, i: (i // per_seq, 0, d // tn + j))],
        out_specs=[tile, pl.BlockSpec((None, 8, tn), lambda j, i: (i // per_seq, 0, j))],
        out_shape=[jax.ShapeDtypeStruct((t, d), F32), jax.ShapeDtypeStruct((t // seq, 8, d), F32)],
        compiler_params=_params("arbitrary", "arbitrary"))(dproj, w_all, gx0, x2, mod3)


def _grad_w_in(ht, dproj, seq, lay, part, prev):
    d, t = ht.shape
    tm = min(t, 2 * seq)
    n_i = t // tm

    def make_body(n_skip, n_pieces, tn, nat_tile):
        def body(*refs):
            refs = refs[n_skip:]
            ht_ref, dp_refs = refs[0], refs[1:1 + n_pieces]
            gw_hbm, gb_hbm, acc, bacc, gw_out, gb_out, sem = refs[1 + n_pieces:]
            i, j = pl.program_id(0), pl.program_id(1)
            dp = dp_refs[0][...] if n_pieces == 1 else jnp.concatenate([r[...] for r in dp_refs], axis=1)
            part = jnp.dot(ht_ref[...], dp, preferred_element_type=F32)
            bpart = jnp.sum(dp.astype(F32), axis=0, keepdims=True)

            if n_i > 1:
                @pl.when(i == 0)
                def _():
                    acc[j] = part
                    bacc[j] = bpart

                @pl.when((i > 0) & (i < n_i - 1))
                def _():
                    acc[j] += part
                    bacc[j] += bpart

            @pl.when(i == n_i - 1)
            def _():
                gw_out[...] = ((part + acc[j]) if n_i > 1 else part).astype(BF16)
                gb_out[...] = (bpart + bacc[j]) if n_i > 1 else bpart
                cols = pl.ds(pl.multiple_of(nat_tile(j) * tn, SLAB), tn)
                copies = [pltpu.make_async_copy(gw_out, gw_hbm.at[:, cols], sem.at[0]),
                          pltpu.make_async_copy(gb_out, gb_hbm.at[:, cols], sem.at[1])]
                for cp in copies:
                    cp.start()
                for cp in copies:
                    cp.wait()
        return body

    def call(name, pieces, n_tiles, nat_tile, prev):
        tn = sum(w for w, _ in pieces)
        any_spec = pl.BlockSpec(memory_space=pl.ANY)
        in_specs = [pl.BlockSpec((d, tm), lambda i, j: (0, i))]
        in_specs += [pl.BlockSpec((tm, w), lambda i, j, f=f: (i, f(j))) for w, f in pieces]
        args = [ht] + [dproj] * len(pieces)
        aliases = {}
        if prev is not None:
            in_specs = [any_spec] * 2 + in_specs
            args = list(prev) + args
            aliases = {0: 0, 1: 1}
        return pl.pallas_call(
            make_body(0 if prev is None else 2, len(pieces), tn, nat_tile), name=name, grid=(n_i, n_tiles), in_specs=in_specs,
            out_specs=[any_spec, any_spec],
            out_shape=[jax.ShapeDtypeStruct((d, lay.din), BF16), jax.ShapeDtypeStruct((1, lay.din), F32)],
            input_output_aliases=aliases,
            scratch_shapes=[pltpu.VMEM((n_tiles if n_i > 1 else 1, d, tn), F32), pltpu.VMEM((n_tiles, 1, tn), F32),
                            pltpu.VMEM((d, tn), BF16),
                            pltpu.VMEM((1, tn), F32), pltpu.SemaphoreType.DMA((2,))],
            compiler_params=_params("arbitrary", "arbitrary"))(*args)

    if part == "attn":
        attn_pieces = [(SLAB, lambda j, m=m: (m % 2) * PAIR_SLABS + 2 * j + m // 2) for m in range(4)]
        return call("grad_w_in_attn", attn_pieces, ATT // 512, lambda j: j, prev)
    base = lay.c0 // CONV_TILE
    nct = lay.n_conv_tiles
    if nct % 2:
        return call("grad_w_in_rest", [(CONV_TILE, lambda j: base + j)], 6 * d // CONV_TILE, lay.rest_nat_tile, prev)
    half = nct // 2

    def rest_piece(m):
        def perm_tile(j):
            conv = base + 4 * (2 * (j % half) + m) + j // half
            return jnp.where(j < 4 * half, conv, base + 2 * j + m)
        return (CONV_TILE, perm_tile)

    return call("grad_w_in_rest", [rest_piece(0), rest_piece(1)], 6 * d // 512, lambda j: ATT // 512 + j, prev)


def _pack_rows(parts, width=128):
    flat = [p.reshape(-1) for p in parts]
    spans, rows = [], 0
    padded = []
    for f in flat:
        n = -(-f.shape[0] // (8 * width)) * 8
        padded.append(jnp.pad(f, (0, n * width - f.shape[0])).reshape(n, width))
        spans.append((rows, f.shape[0]))
        rows += n
    return jnp.concatenate(padded, axis=0), spans


def _unpack_rows(packed, spans, shapes, width=128):
    out = []
    for (row, n), shp in zip(spans, shapes):
        rows = -(-n // width)
        out.append(packed[row:row + rows].reshape(-1)[:n].reshape(shp))
    return out


def kernel(x, c, w_ada, b_ada, w_in, b_in, conv_w, w_proj_attn, w_proj_conv, w_out, b_out, ln_g, ln_b, loss_target, m_w_ada, m_b_ada, m_w_in, m_b_in, m_conv_w, m_w_proj_attn, m_w_proj_conv, m_w_out, m_b_out, m_ln_g, m_ln_b, v_w_ada, v_b_ada, v_w_in, v_b_in, v_conv_w, v_w_proj_attn, v_w_proj_conv, v_w_out, v_b_out, v_ln_g, v_ln_b):
    bsz, seq, d = x.shape
    t = bsz * seq
    lay = _Layout(d)
    col_sharded = [True, True, False, False]
    red_w = [w_in[0], w_proj_attn[0], w_proj_conv[0], w_out[0]]
    chip = 2 * lax.axis_index("x") + lax.axis_index("y")
    chip1 = chip.astype(jnp.int32).reshape(1)
    core1 = lax.axis_index("c").astype(jnp.int32).reshape(1)
    place = jnp.stack([chip, lax.axis_index("c")]).astype(jnp.int32)
    x2 = x.reshape(t, d)
    target2 = loss_target.reshape(t, d)

    mod, act_all = _ada_exchange(c, w_ada[0], b_ada)
    mod3 = mod.reshape(bsz, 1, 3 * d)

    cw_pad = jnp.pad(conv_w[0], ((0, 5), (0, 0))) + 0.0 * mod[0, 0]
    own_in_full = [_cast_into_full(red_w[0], col_sharded[0], chip1, "cast_shard_in")]
    own_in_full += list(_cast_into_full_small(red_w[1:], col_sharded[1:], chip1, "cast_shards_late"))
    (wi_f,), cw8 = _gather_weights(own_in_full[:1], col_sharded[:1], cw_pad)
    cw_full = cw8[0:3]
    late_copies = _direct_gather_copies(col_sharded[1:])
    late_send, late_recv, late_flying, late_token = _start_copies("gather_late_start", own_in_full[1:], (18,), cw8, late_copies)
    w_all = _permute_w_in(wi_f, lay)
    b_all = lay.perm_vector(b_in) + late_token[0, 0]

    rest_tn = 1024 if (6 * d) % 1024 == 0 else 512
    pa, = _project(x2, mod3, w_all, b_all, seq, 0, ATT, PAIR_COLS, BF16, False, "project_attn")
    pr, ht = _project(x2, mod3, w_all, b_all, seq, lay.c0, 6 * d, rest_tn, BF16, True, "project_rest")
    o_all, lse_all, a_in = _attn_fwd(pa, bsz, seq)
    b_in_act = _conv_fwd(pr, cw_full, bsz, seq, d)
    wpa_f, wpc_f, wo_f = _wait_copies("gather_late_wait", late_flying, late_send, late_recv, b_in_act, late_copies)
    (dproj, da_in, db_in, gx0, dgate, small_tail, gw_pa, gw_pc, gw_out) = _tail(
        a_in, b_in_act, pr, x2, target2, mod3, wpa_f, wpc_f, wo_f, b_out, ln_g, ln_b, seq, lay)

    late_views = _shard_views([gw_pa, gw_pc, gw_out], col_sharded[1:])
    late_lands = [lax.empty((v.shape[0], v.shape[1] // 2, v.shape[2]), v.dtype) for v in late_views]
    xl_send, xl_recv, xl_fly, xl_tok = _start_copies("grads_pair_exchange_late_start", late_views + late_lands, (3,), small_tail,
                                                     _pair_exchange_copies(3, False))
    dproj, gcw = _conv_bwd(dproj, db_in, pr, cw_full + xl_tok[0, 0], bsz, seq, lay)
    xl_done = _wait_copies("grads_pair_exchange_late_wait", xl_fly, xl_send, xl_recv, gcw, _pair_exchange_copies(3, True))
    late_parts = list(_pair_sum_small(xl_done[:3], xl_done[3:], core1, "grads_pair_sum_late"))

    late_cross = _chip_scatter_copies(3, col_sharded[1:])
    late_zone = [lax.empty((3, p.shape[1], _piece_cols(p, cs)), p.dtype) for p, cs in zip(late_parts, col_sharded[1:])]
    sl_send, sl_recv, sl_fly, sl_tok = _start_copies("grads_scatter_late_start", late_parts + late_zone, (9,), core1,
                                                     late_cross)
    dproj = _attn_bwd(dproj, pa, o_all, lse_all, da_in, bsz, seq, sl_tok)
    gw_in_bf, gb_in = _grad_w_in(ht, dproj, seq, lay, "rest", None)
    rest_cols, attn_cols = (ATT, 6 * d), (0, ATT)
    in_land = lax.empty((1, d // 2, lay.din), BF16)
    xi_copies = _pair_exchange_copies(1, False, rest_cols)
    xi_send, xi_recv, xi_fly, _ = _start_copies("grads_pair_exchange_in_start", [gw_in_bf.reshape(1, d, lay.din), in_land], (1,),
                                                gb_in, xi_copies)
    gw_in_bf, gb_in = _grad_w_in(ht, dproj, seq, lay, "attn", (xi_fly[0].reshape(d, lay.din), gb_in))
    xi_done = _wait_copies("grads_pair_exchange_in_wait", [gw_in_bf.reshape(1, d, lay.din), xi_fly[1]], xi_send, xi_recv, gb_in,
                           _pair_exchange_copies(1, True, rest_cols))
    in_got = _pair_exchange_into(xi_done[0], xi_done[1], attn_cols, "grads_pair_exchange_in_attn")
    sl_done = _wait_copies("grads_scatter_late_wait", sl_fly, sl_send, sl_recv, in_got, late_cross)
    late_red = list(_chip_sum_small(sl_done[:3], sl_done[3:], col_sharded[1:], place, "grads_chip_sum_late"))

    in_part = _pair_sum(xi_done[0], in_got, core1, "grads_pair_sum_in")
    in_cross = _chip_scatter_copies(1, col_sharded[:1])
    in_zone = [lax.empty((3, in_part.shape[1], _piece_cols(in_part, True)), in_part.dtype)]
    late_views1 = [f.reshape(1, *f.shape) for f in late_red]
    si_send, si_recv, si_fly, si_tok = _start_copies(
        "grads_scatter_in_join_late_start", [in_part] + in_zone + late_views1, (6,), core1,
        _both_copies(in_cross, 2, _pair_join_copies(3, False, sem0=3)))
    grad_x2, dmod = _grad_h(dproj, w_all, gx0, x2, mod3 + si_tok[0, 0], seq, lay)
    si_done = _wait_copies("grads_scatter_in_join_late_wait", si_fly, si_send, si_recv, grad_x2,
                           _both_copies(in_cross, 2, _pair_join_copies(3, True, sem0=3)))
    late_joined = [f[0] for f in si_done[2:]]
    in_red = _chip_sum(si_done[0], si_done[1], True, place, "grads_chip_sum_in")

    d_ada = jnp.concatenate([dmod[:, 0, :], dmod[:, 1, :], dgate[:, 0, :]], axis=1)
    pieces = [small_tail[3], jnp.sum(d_ada, axis=0), gb_in[0], small_tail[0], small_tail[1], small_tail[2], gcw[0:3]]
    packed, spans = _pack_rows(pieces)
    kept_in, _ = _pack_rows([d_ada])
    rows_all = jnp.concatenate([packed, kept_in], axis=0)
    small_land = lax.empty((N_DEV,) + rows_all.shape, F32)
    sm_send, sm_recv, sm_fly, sm_tok = _start_copies(
        "small_gather_join_in_start", [rows_all, small_land, in_red.reshape(1, *in_red.shape)], (N_DEV,), core1,
        _both_copies(_small_gather_copies, 2, _pair_join_copies(1, False, sem0=N_DEV - 1)))
    big_w = [w_ada[0]] + red_w
    big_m = [m_w_ada[0], m_w_in[0], m_w_proj_attn[0], m_w_proj_conv[0], m_w_out[0]]
    big_v = [v_w_ada[0], v_w_in[0], v_w_proj_attn[0], v_w_proj_conv[0], v_w_out[0]]
    big_out = [None] * 5
    big_out[2:5] = _adamw_together(big_w[2:], late_joined, big_m[2:], big_v[2:], "adamw_late", sm_tok)
    sm_done = _wait_copies("small_gather_join_in_wait", sm_fly, sm_send, sm_recv, big_out[4][0],
                           _both_copies(_small_gather_copies, 2, _pair_join_copies(1, True, sem0=N_DEV - 1)))
    me1 = (4 * lax.axis_index("x") + 2 * lax.axis_index("y") + lax.axis_index("c")).astype(jnp.int32).reshape(1)
    summed, kept = _small_sum(sm_done[0], sm_done[1], me1, packed.shape[0], d)
    loss = summed[0, 0]
    _, g_b_ada, g_b_in, g_b_out, g_ln_g, g_ln_b, g_cw_full = _unpack_rows(
        summed, spans, [(d,), (3 * d,), (lay.din,), (d,), (d,), (d,), (3, d)])
    g_cw = lax.dynamic_slice(g_cw_full, (0, chip * (d // N_CHIPS)), (3, d // N_CHIPS))
    d_ada_all = kept.reshape(N_DEV, -1)[:, :bsz * 3 * d].reshape(N_DEV * bsz, 3 * d)
    ada_cols = 3 * d // N_CHIPS
    d_ada_cols = lax.dynamic_slice(d_ada_all, (0, chip * ada_cols), (N_DEV * bsz, ada_cols))

    g_w_ada, *big_out[0] = _grad_and_adamw_w_ada(act_all.T, d_ada_cols, big_w[0], big_m[0], big_v[0])
    small_w = [b_ada, b_in, conv_w[0], b_out, ln_g, ln_b]
    small_g = [g_b_ada, g_b_in, g_cw, g_b_out, g_ln_g, g_ln_b]
    small_m = [m_b_ada, m_b_in, m_conv_w[0], m_b_out, m_ln_g, m_ln_b]
    small_v = [v_b_ada, v_b_in, v_conv_w[0], v_b_out, v_ln_g, v_ln_b]
    pw, sp = _pack_rows(small_w)
    pg, _ = _pack_rows(small_g)
    pm, _ = _pack_rows(small_m)
    pv, _ = _pack_rows(small_v)
    sd, sm, sv = _adamw(pw, pg, pm, pv, "adamw_small", None)
    big_out[1] = _adamw(big_w[1], sm_done[2][0], big_m[1], big_v[1], "adamw_1", None, also_g=True)
    g_big = [g_w_ada, big_out[1][3]] + late_joined
    shapes = [a.shape for a in small_w]
    sd, sm, sv = _unpack_rows(sd, sp, shapes), _unpack_rows(sm, sp, shapes), _unpack_rows(sv, sp, shapes)

    def order(wa, bA, wi, bI, cw, wpa, wpc, wo, bO, lg, lb):
        return (wa[None], bA, wi[None], bI, cw[None], wpa[None], wpc[None], wo[None], bO, lg, lb)

    sg = [g.reshape(s) for g, s in zip(small_g, shapes)]
    grads_out = order(g_big[0], sg[0], g_big[1], sg[1], sg[2], g_big[2], g_big[3], g_big[4], sg[3], sg[4], sg[5])
    outs = []
    for idx, small in enumerate((sd, sm, sv)):
        outs.append(order(big_out[0][idx], small[0], big_out[1][idx], small[1], small[2], big_out[2][idx],
                          big_out[3][idx], big_out[4][idx], small[3], small[4], small[5]))
    return (loss, grad_x2.reshape(bsz, seq, d), *grads_out, *outs[0], *outs[1], *outs[2])
```

```python
import jax
import jax.numpy as jnp
from jax import lax
from jax.experimental import pallas as pl
from jax.experimental.pallas import tpu as pltpu

F32 = jnp.float32
BF16 = jnp.bfloat16
MESH = pl.DeviceIdType.MESH

HEAD_DIM = 64
N_GROUPS = 3
DILATIONS = (1, 4, 16)
N_HEADS = 12
SUB = 128
Q_WIDTH = 768
Z_WIDTH = 256
ATT = 3 * Q_WIDTH + Z_WIDTH
SLAB = 128
PAIR_SLABS = 10
PAIR_COLS = PAIR_SLABS * SLAB
CONV_TILE = 256
SOFTMAX_ROWS = 32
BLOCKS_PER_TRIP = 8
BLOCKS_PER_TRIP_BWD = 16
ALIBI_MAX_EXP = 8.0
ALPHA = 2.0 ** 0.25
LN_EPS = 1e-5
ADAM_LR, ADAM_B1, ADAM_B2, ADAM_EPS, ADAM_WD, ADAM_STEP = 0.001, 0.9, 0.999, 1e-08, 0.01, 10
N_CHIPS = 4
N_DEV = 8
VMEM_LIMIT_V7X = 60 * 1024 * 1024
NEG = -1e30


def _params(*sem):
    return pltpu.CompilerParams(dimension_semantics=sem, vmem_limit_bytes=VMEM_LIMIT_V7X)


def _sigmoid(v):
    return 0.5 * jnp.tanh(0.5 * v) + 0.5


class _Layout:
    def __init__(self, d):
        self.d = d
        self.din = ATT + 6 * d
        c0 = 3072
        while c0 % (2 * d):
            c0 += 1024
        self.c0, self.g0, self.np = c0, c0 + 4 * d, c0 + 6 * d
        self.n_conv_tiles = d // CONV_TILE

    def attn_nat_slab(self, s):
        p, i = s // PAIR_SLABS, s % PAIR_SLABS
        return jnp.where(i < 9, (i // 3) * 6 + (i % 3) * 2 + p, 18 + p)

    def rest_nat_tile(self, t):
        n4 = 4 * self.n_conv_tiles
        conv = ATT // CONV_TILE + (t % 4) * self.n_conv_tiles + t // 4
        return jnp.where(t < n4, conv, ATT // CONV_TILE + t)

    def perm_vector(self, v):
        parts = []
        for s in range(2 * PAIR_SLABS):
            p, i = divmod(s, PAIR_SLABS)
            ns = (i // 3) * 6 + (i % 3) * 2 + p if i < 9 else 18 + p
            parts.append(v[:, ns * SLAB:(ns + 1) * SLAB])
        parts.append(jnp.zeros((1, self.c0 - ATT), v.dtype))
        for j in range(self.n_conv_tiles):
            for k in range(4):
                a = ATT + k * self.d + j * CONV_TILE
                parts.append(v[:, a:a + CONV_TILE])
        parts.append(v[:, ATT + 4 * self.d:])
        return jnp.concatenate(parts, axis=1)


def _place():
    return lax.axis_index("x"), lax.axis_index("y"), lax.axis_index("c")


def _other_chips(x, y):
    return [(1 - x, y), (x, 1 - y), (1 - x, 1 - y)]


def _shard_of(ref, col_sharded, chip, half=None):
    if col_sharded:
        cs = ref.shape[1] // N_CHIPS
        cols = pl.ds(pl.multiple_of(chip * cs, SLAB), cs)
        if half is None:
            return ref.at[:, cols]
        n = ref.shape[0] // 2
        return ref.at[pl.ds(half * n, n), cols]
    rs = ref.shape[0] // N_CHIPS
    if half is None:
        return ref.at[pl.ds(chip * rs, rs)]
    return ref.at[pl.ds(chip * rs + half * (rs // 2), rs // 2)]


GATHER_PIECES = 8


def _cast_into_full(shard, col_sharded, chip, name):
    rows, cols = shard.shape
    tr = _row_tile(rows, cols)
    nb = rows // tr

    def body(chip_ref, s_ref, o_ref):
        del chip_ref
        o_ref[...] = s_ref[...].astype(BF16)

    if col_sharded:
        full, out_spec = (rows, cols * N_CHIPS), pl.BlockSpec((tr, cols), lambda i, ch: (i, ch[0]))
    else:
        full, out_spec = (rows * N_CHIPS, cols), pl.BlockSpec((tr, cols), lambda i, ch: (ch[0] * nb + i, 0))
    return pl.pallas_call(
        body, name=name,
        grid_spec=pltpu.PrefetchScalarGridSpec(num_scalar_prefetch=1, grid=(nb,),
                                               in_specs=[pl.BlockSpec((tr, cols), lambda i, ch: (i, 0))], out_specs=out_spec),
        out_shape=jax.ShapeDtypeStruct(full, BF16), compiler_params=_params("parallel"))(chip, shard)


def _cast_into_full_small(shards, col_sharded, chip, name):
    n = len(shards)

    def body(chip_ref, *refs):
        del chip_ref
        for w in range(n):
            refs[n + w][...] = refs[w][...].astype(BF16)

    in_specs = [pl.BlockSpec(s.shape, lambda i, ch: (0, 0)) for s in shards]
    out_specs = [pl.BlockSpec(s.shape, (lambda i, ch: (0, ch[0])) if cs else (lambda i, ch: (ch[0], 0)))
                 for s, cs in zip(shards, col_sharded)]
    fulls = [(s.shape[0], s.shape[1] * N_CHIPS) if cs else (s.shape[0] * N_CHIPS, s.shape[1]) for s, cs in zip(shards, col_sharded)]
    return pl.pallas_call(
        body, name=name,
        grid_spec=pltpu.PrefetchScalarGridSpec(num_scalar_prefetch=1, grid=(1,), in_specs=in_specs, out_specs=out_specs),
        out_shape=[jax.ShapeDtypeStruct(f, BF16) for f in fulls], compiler_params=_params("arbitrary"))(chip, *shards)


def _gather_weights(fulls, col_sharded, small):
    n = len(fulls)
    kp = GATHER_PIECES

    def piece(ref, cs, chip, half, k):
        if cs:
            width = ref.shape[1] // N_CHIPS
            rows = ref.shape[0] // 2 // kp
            return ref.at[pl.ds(half * (ref.shape[0] // 2) + k * rows, rows), pl.ds(pl.multiple_of(chip * width, SLAB), width)]
        rs = ref.shape[0] // N_CHIPS
        rows = rs // 2 // kp
        return ref.at[pl.ds(chip * rs + half * (rs // 2) + k * rows, rows)]

    def body(*refs):
        sm_in, outs, sm_out = refs[n], refs[n + 1:2 * n + 1], refs[2 * n + 1]
        send, recv, fsend, frecv, lsem, ssend, srecv = refs[2 * n + 2:]
        x, y, c = _place()
        mine = 2 * x + y
        sibling = (x, y, 1 - c)
        first = (x ^ (1 - c), y ^ c)
        second = (x ^ c, y ^ (1 - c))
        diagonal = (1 - x, 1 - y)
        sources = [first, second, diagonal]
        senders = [first, second, second]

        def copy(ref, sems, slot, to):
            return pltpu.make_async_remote_copy(src_ref=ref, dst_ref=ref, send_sem=sems[0].at[slot], recv_sem=sems[1].at[slot],
                                                device_id=to, device_id_type=MESH)

        local = pltpu.make_async_copy(sm_in, _shard_of(sm_out, True, mine), lsem)
        local.start()
        sends = []
        for k, (cx, cy) in enumerate(_other_chips(x, y)):
            cp = pltpu.make_async_remote_copy(src_ref=sm_in, dst_ref=_shard_of(sm_out, True, mine), send_sem=ssend.at[k],
                                              recv_sem=srecv.at[k], device_id=(cx, cy, c), device_id_type=MESH)
            cp.start()
            sends.append(cp)
        for k in range(kp):
            for w in range(n):
                own = piece(outs[w], col_sharded[w], mine, c, k)
                for slot, chip in enumerate((first, second)):
                    cp = copy(own, (send, recv), (w * 3 + slot) * kp + k, (*chip, c))
                    cp.start()
                    sends.append(cp)
        for slot in range(3):
            source = 2 * sources[slot][0] + sources[slot][1]
            for k in range(kp):
                for w in range(n):
                    landed = piece(outs[w], col_sharded[w], source, c, k)
                    copy(landed, (send, recv), (w * 3 + slot) * kp + k, (*senders[slot], c)).wait_recv()
                    if slot == 0:
                        cp = copy(landed, (send, recv), (w * 3 + 2) * kp + k, (*second, c))
                        cp.start()
                        sends.append(cp)
                    cp = copy(landed, (fsend, frecv), (w * 3 + slot) * kp + k, sibling)
                    cp.start()
                    sends.append(cp)
        for slot, chip in enumerate((second, first, diagonal)):
            for k in range(kp):
                for w in range(n):
                    passed = piece(outs[w], col_sharded[w], 2 * chip[0] + chip[1], 1 - c, k)
                    copy(passed, (fsend, frecv), (w * 3 + slot) * kp + k, sibling).wait_recv()
        for k, (cx, cy) in enumerate(_other_chips(x, y)):
            theirs = _shard_of(sm_out, True, 2 * cx + cy)
            pltpu.make_async_remote_copy(src_ref=theirs, dst_ref=theirs, send_sem=ssend.at[k], recv_sem=srecv.at[k],
                                         device_id=(cx, cy, c), device_id_type=MESH).wait_recv()
        for cp in sends:
            cp.wait_send()
        local.wait()

    any_spec = pl.BlockSpec(memory_space=pl.ANY)
    outs = pl.pallas_call(
        body, name="gather_weights",
        out_shape=[jax.ShapeDtypeStruct(f.shape, BF16) for f in fulls]
        + [jax.ShapeDtypeStruct((small.shape[0], small.shape[1] * N_CHIPS), small.dtype)],
        in_specs=[any_spec] * (n + 1), out_specs=[any_spec] * (n + 1), input_output_aliases={w: w for w in range(n)},
        scratch_shapes=[pltpu.SemaphoreType.DMA((n * 3 * kp,)), pltpu.SemaphoreType.DMA((n * 3 * kp,)),
                        pltpu.SemaphoreType.DMA((n * 3 * kp,)), pltpu.SemaphoreType.DMA((n * 3 * kp,)), pltpu.SemaphoreType.DMA,
                        pltpu.SemaphoreType.DMA((3,)), pltpu.SemaphoreType.DMA((3,))],
    )(*fulls, small)
    return outs[:n], outs[n]


HBM_SPEC = pl.BlockSpec(memory_space=pltpu.HBM)
SEM_SPEC = pl.BlockSpec(memory_space=pltpu.SEMAPHORE)
DATAFLOW = pltpu.SideEffectType.DATAFLOW_SIDE_EFFECTING


def _start_copies(name, arrays, sem_shape, after, copies):
    n = len(arrays)

    def body(*refs):
        for cp in copies(refs[:n], refs[n + 1], refs[n + 2]):
            cp.start()
        token = refs[2 * n + 3]
        token[...] = jnp.zeros_like(token)

    res = pl.pallas_call(
        body, name=name,
        out_shape=(pltpu.SemaphoreType.DMA(sem_shape), pltpu.SemaphoreType.DMA(sem_shape),
                   *[pltpu.HBM(a.shape, a.dtype) for a in arrays], jax.ShapeDtypeStruct((8, 128), F32)),
        in_specs=[HBM_SPEC] * n + [pl.BlockSpec(memory_space=pl.ANY)],
        out_specs=(SEM_SPEC, SEM_SPEC, *([HBM_SPEC] * n), pl.BlockSpec(memory_space=pltpu.VMEM)),
        input_output_aliases={i: 2 + i for i in range(n)},
        compiler_params=pltpu.CompilerParams(has_side_effects=DATAFLOW),
    )(*[pltpu.with_memory_space_constraint(a, pltpu.HBM) for a in arrays], after)
    return res[0], res[1], list(res[2:2 + n]), res[2 + n]


def _wait_copies(name, arrays, send, recv, after, copies):
    n = len(arrays)

    def body(*refs):
        for cp in copies(refs[:n], refs[n], refs[n + 1]):
            cp.wait_send()
            cp.wait_recv()

    return pl.pallas_call(
        body, name=name, out_shape=[pltpu.HBM(a.shape, a.dtype) for a in arrays],
        in_specs=[HBM_SPEC] * n + [SEM_SPEC, SEM_SPEC, pl.BlockSpec(memory_space=pl.ANY)], out_specs=[HBM_SPEC] * n,
        input_output_aliases={i: i for i in range(n)},
        compiler_params=pltpu.CompilerParams(has_side_effects=DATAFLOW),
    )(*arrays, send, recv, after)


def _direct_gather_copies(col_sharded):
    def copies(refs, send, recv):
        x, y, c = _place()
        mine = 2 * x + y
        out = []
        for w, ref in enumerate(refs):
            own_half = _shard_of(ref, col_sharded[w], mine, c)
            k = 0
            for cx, cy in _other_chips(x, y):
                for pc in (c, 1 - c):
                    out.append(pltpu.make_async_remote_copy(
                        src_ref=own_half, dst_ref=own_half, send_sem=send.at[6 * w + k], recv_sem=recv.at[6 * w + k],
                        device_id=(cx, cy, pc), device_id_type=MESH))
                    k += 1
        return out
    return copies


def _chip_scatter_copies(n, col_sharded):
    def piece(ref, cs, chip):
        if cs:
            w = ref.shape[2] // N_CHIPS
            return ref.at[:, :, pl.ds(pl.multiple_of(chip * w, SLAB), w)]
        return ref.at[pl.ds(chip, 1)]

    def copies(refs, send, recv):
        x, y, c = _place()
        out = []
        for k, (cx, cy) in enumerate(_other_chips(x, y)):
            for w in range(n):
                out.append(pltpu.make_async_remote_copy(
                    src_ref=piece(refs[w], col_sharded[w], 2 * cx + cy), dst_ref=refs[n + w].at[pl.ds(k, 1)],
                    send_sem=send.at[3 * w + k], recv_sem=recv.at[3 * w + k], device_id=(cx, cy, c), device_id_type=MESH))
        return out
    return copies


def _shard_views(gs, col_sharded):
    return [g.reshape(1, *g.shape) if cs else g.reshape(N_CHIPS, g.shape[0] // N_CHIPS, g.shape[1])
            for g, cs in zip(gs, col_sharded)]


DMA_CHUNK_BYTES = 1 << 20


def _chunk_rows(shape, itemsize):
    s, rows, cols = shape
    n = 1
    while s * (rows // n) * cols * itemsize > DMA_CHUNK_BYTES and (rows // n) % 32 == 0:
        n *= 2
    return rows // n


def _rows_of(ref, row0, rows, cols):
    if cols is None:
        return ref.at[:, pl.ds(row0, rows)]
    return ref.at[:, pl.ds(row0, rows), pl.ds(cols[0], cols[1])]


def _row_pieces(src, src_row0, dst, dst_row0, rows, send_sem, recv_sem, device, cols=None):
    width = src.shape[2] if cols is None else cols[1]
    step = _chunk_rows((src.shape[0], rows, width), src.dtype.itemsize)
    return [pltpu.make_async_remote_copy(src_ref=_rows_of(src, src_row0 + r, step, cols), dst_ref=_rows_of(dst, dst_row0 + r, step, cols),
                                         send_sem=send_sem, recv_sem=recv_sem, device_id=device, device_id_type=MESH)
            for r in range(0, rows, step)]


def _pair_exchange_copies(n, whole, cols=None):
    def copies(refs, send, recv):
        x, y, c = _place()
        sibling = (x, y, 1 - c)
        out = []
        for w in range(n):
            hr = refs[n + w].shape[1]
            if whole:
                out.append(pltpu.make_async_remote_copy(
                    src_ref=_rows_of(refs[w], (1 - c) * hr, hr, cols), dst_ref=_rows_of(refs[n + w], 0, hr, cols),
                    send_sem=send.at[w], recv_sem=recv.at[w], device_id=sibling, device_id_type=MESH))
            else:
                out += _row_pieces(refs[w], (1 - c) * hr, refs[n + w], 0, hr, send.at[w], recv.at[w], sibling, cols)
        return out
    return copies


def _pair_join_copies(n, whole, sem0=0):
    def copies(refs, send, recv):
        x, y, c = _place()
        sibling = (x, y, 1 - c)
        out = []
        for w in range(n):
            hr = refs[w].shape[1] // 2
            sems = dict(send_sem=send.at[sem0 + w], recv_sem=recv.at[sem0 + w])
            if whole:
                out.append(pltpu.make_async_remote_copy(
                    src_ref=refs[w].at[:, pl.ds(c * hr, hr)], dst_ref=refs[w].at[:, pl.ds((1 - c) * hr, hr)],
                    device_id=sibling, device_id_type=MESH, **sems))
            else:
                out += _row_pieces(refs[w], c * hr, refs[w], c * hr, hr, sems["send_sem"], sems["recv_sem"], sibling)
        return out
    return copies


def _both_copies(first, n_first, second):
    def copies(refs, send, recv):
        return first(refs[:n_first], send, recv) + second(refs[n_first:], send, recv)
    return copies


def _small_gather_copies(refs, send, recv):
    vec, land = refs
    x, y, c = _place()
    me = 4 * x + 2 * y + c
    return [pltpu.make_async_remote_copy(src_ref=vec, dst_ref=land.at[me], send_sem=send.at[k], recv_sem=recv.at[k],
                                         device_id=peer, device_id_type=MESH) for k, (peer, _) in enumerate(_all_devices(x, y, c))]


def _small_sum(vec, land, me, n_sum, d):
    rows = vec.shape[0]

    def body(me_ref, v_ref, l_ref, sum_ref, kept_ref):
        def slot(k):
            return jnp.where(me_ref[0] == k, v_ref[...], l_ref[k])

        total = slot(0)[0:n_sum, :]
        kept_ref[0] = slot(0)[n_sum:rows, :]
        for k in range(1, N_DEV):
            total = total + slot(k)[0:n_sum, :]
            kept_ref[k] = slot(k)[n_sum:rows, :]
        sum_ref[...] = total
        loss = 0.5 / d * jnp.sum(total[0:8, :])
        sum_ref[0:8, :] = jnp.full((8, 128), loss, F32)

    vm = pl.BlockSpec(memory_space=pltpu.VMEM)
    return pl.pallas_call(
        body, name="small_sum", in_specs=[pl.BlockSpec(memory_space=pltpu.SMEM), vm, vm], out_specs=[vm, vm],
        out_shape=[jax.ShapeDtypeStruct((n_sum, 128), F32), jax.ShapeDtypeStruct((N_DEV, rows - n_sum, 128), F32)],
        compiler_params=pltpu.CompilerParams(vmem_limit_bytes=VMEM_LIMIT_V7X))(me, vec, land)


def _pair_exchange_into(view, land, cols, name):
    def body(v_ref, l_in, l_ref, send, recv):
        del l_in
        x, y, c = _place()
        sibling = (x, y, 1 - c)
        hr = l_ref.shape[1]
        for cp in _row_pieces(v_ref, (1 - c) * hr, l_ref, 0, hr, send, recv, sibling, cols):
            cp.start()
        pltpu.make_async_remote_copy(src_ref=_rows_of(v_ref, (1 - c) * hr, hr, cols), dst_ref=_rows_of(l_ref, 0, hr, cols),
                                     send_sem=send, recv_sem=recv, device_id=sibling, device_id_type=MESH).wait()

    any_spec = pl.BlockSpec(memory_space=pl.ANY)
    return pl.pallas_call(
        body, name=name, out_shape=jax.ShapeDtypeStruct(land.shape, land.dtype),
        in_specs=[any_spec, any_spec], out_specs=any_spec, input_output_aliases={1: 0},
        scratch_shapes=[pltpu.SemaphoreType.DMA, pltpu.SemaphoreType.DMA],
    )(view, land)


def _pair_sum(view, got, core, name):
    s, r, cols = view.shape
    hr = r // 2
    tr = _row_tile(hr, cols, itemsize=view.dtype.itemsize, budget=4 << 20)
    nb = hr // tr

    def body(core_ref, a_ref, b_ref, o_ref):
        del core_ref
        o_ref[...] = (a_ref[...].astype(F32) + b_ref[...].astype(F32)).astype(BF16)

    same = pl.BlockSpec((None, tr, cols), lambda j, i, core_ref: (j, i, 0))
    return pl.pallas_call(
        body, name=name,
        grid_spec=pltpu.PrefetchScalarGridSpec(
            num_scalar_prefetch=1, grid=(s, nb),
            in_specs=[pl.BlockSpec((None, tr, cols), lambda j, i, core_ref: (j, core_ref[0] * nb + i, 0)), same],
            out_specs=same),
        out_shape=jax.ShapeDtypeStruct((s, hr, cols), BF16), compiler_params=_params("parallel", "parallel"))(core, view, got)


def _pair_sum_small(views, gots, core, name):
    n = len(views)

    def body(core_ref, *refs):
        del core_ref
        for w in range(n):
            refs[2 * n + w][...] = (refs[w][...].astype(F32) + refs[n + w][...].astype(F32)).astype(BF16)

    halves = [(v.shape[0], v.shape[1] // 2, v.shape[2]) for v in views]
    own = [pl.BlockSpec(h, lambda i, core_ref: (0, core_ref[0], 0)) for h in halves]
    whole = [pl.BlockSpec(h, lambda i, core_ref: (0, 0, 0)) for h in halves]
    return pl.pallas_call(
        body, name=name,
        grid_spec=pltpu.PrefetchScalarGridSpec(num_scalar_prefetch=1, grid=(1,), in_specs=own + whole, out_specs=whole),
        out_shape=[jax.ShapeDtypeStruct(h, BF16) for h in halves], compiler_params=_params("arbitrary"))(core, *views, *gots)


def _piece_cols(part, col_sharded):
    return part.shape[2] // N_CHIPS if col_sharded else part.shape[2]


def _chip_sum_small(parts, gots, col_sharded, place, name):
    n = len(parts)

    def body(place_ref, *refs):
        del place_ref
        for w in range(n):
            got = refs[n + w]
            acc = refs[w][...].astype(F32) + got[0].astype(F32)
            refs[2 * n + w][...] = acc + got[1].astype(F32) + got[2].astype(F32)

    own, others, outs, shapes = [], [], [], []
    for p, cs in zip(parts, col_sharded):
        hr, cols = p.shape[1], _piece_cols(p, cs)
        own.append(pl.BlockSpec((None, hr, cols), (lambda i, pr: (0, 0, pr[0])) if cs else (lambda i, pr: (pr[0], 0, 0))))
        others.append(pl.BlockSpec((3, hr, cols), lambda i, pr: (0, 0, 0)))
        outs.append(pl.BlockSpec((hr, cols), lambda i, pr: (pr[1], 0)))
        shapes.append(jax.ShapeDtypeStruct((2 * hr, cols), F32))
    return pl.pallas_call(
        body, name=name,
        grid_spec=pltpu.PrefetchScalarGridSpec(num_scalar_prefetch=1, grid=(1,), in_specs=own + others, out_specs=outs),
        out_shape=shapes, compiler_params=_params("arbitrary"))(place, *parts, *gots)


def _chip_sum(part, got, col_sharded, place, name):
    _, hr, _ = part.shape
    cols = _piece_cols(part, col_sharded)
    tr = _row_tile(hr, cols)
    nb = hr // tr

    def body(place_ref, own_ref, g0_ref, g1_ref, g2_ref, o_ref):
        del place_ref
        acc = own_ref[...].astype(F32) + g0_ref[...].astype(F32)
        o_ref[...] = acc + g1_ref[...].astype(F32) + g2_ref[...].astype(F32)

    if col_sharded:
        own = pl.BlockSpec((None, tr, cols), lambda i, pr: (0, i, pr[0]))
    else:
        own = pl.BlockSpec((None, tr, cols), lambda i, pr: (pr[0], i, 0))
    others = [pl.BlockSpec((None, tr, cols), lambda i, pr, k=k: (k, i, 0)) for k in range(3)]
    return pl.pallas_call(
        body, name=name,
        grid_spec=pltpu.PrefetchScalarGridSpec(
            num_scalar_prefetch=1, grid=(nb,), in_specs=[own] + others,
            out_specs=pl.BlockSpec((tr, cols), lambda i, pr: (pr[1] * nb + i, 0))),
        out_shape=jax.ShapeDtypeStruct((2 * hr, cols), F32), compiler_params=_params("parallel"))(place, part, got, got, got)


def _row_tile(rows, cols, itemsize=4, budget=2 << 20):
    t = rows
    while t * cols * itemsize > budget and t % 16 == 0:
        t //= 2
    return t


def _adamw(w, g, m, v, name, after, also_g=False):
    rows, cols = w.shape
    tr = _row_tile(rows, cols, budget=2 << 20)
    extra = [] if after is None else [after]
    n_out = 4 if also_g else 3

    def body(w_ref, g_ref, m_ref, v_ref, *rest):
        d_ref, nm_ref, nv_ref = rest[len(extra):len(extra) + 3]
        g_ = g_ref[...]
        if also_g:
            rest[len(extra) + 3][...] = g_
        nm = ADAM_B1 * m_ref[...] + (1.0 - ADAM_B1) * g_
        nv = ADAM_B2 * v_ref[...] + (1.0 - ADAM_B2) * (g_ * g_)
        m_hat = nm / (1.0 - ADAM_B1 ** ADAM_STEP)
        v_hat = nv / (1.0 - ADAM_B2 ** ADAM_STEP)
        d_ref[...] = -ADAM_LR * (m_hat / (jnp.sqrt(v_hat) + ADAM_EPS) + ADAM_WD * w_ref[...])
        nm_ref[...] = nm
        nv_ref[...] = nv

    spec = pl.BlockSpec((tr, cols), lambda i: (i, 0))
    shp = jax.ShapeDtypeStruct((rows, cols), F32)
    return pl.pallas_call(body, name=name, grid=(rows // tr,),
                          in_specs=[spec] * 4 + [pl.BlockSpec(memory_space=pl.ANY)] * len(extra), out_specs=[spec] * n_out,
                          out_shape=[shp] * n_out, compiler_params=_params("parallel"))(w, g, m, v, *extra)


def _adamw_together(ws, gs, ms, vs, name, after):
    n = len(ws)

    def body(*refs):
        outs = refs[4 * n + 1:]
        for k in range(n):
            g_ = refs[n + k][...]
            nm = ADAM_B1 * refs[2 * n + k][...] + (1.0 - ADAM_B1) * g_
            nv = ADAM_B2 * refs[3 * n + k][...] + (1.0 - ADAM_B2) * (g_ * g_)
            m_hat = nm / (1.0 - ADAM_B1 ** ADAM_STEP)
            v_hat = nv / (1.0 - ADAM_B2 ** ADAM_STEP)
            outs[3 * k][...] = -ADAM_LR * (m_hat / (jnp.sqrt(v_hat) + ADAM_EPS) + ADAM_WD * refs[k][...])
            outs[3 * k + 1][...] = nm
            outs[3 * k + 2][...] = nv

    specs = [pl.BlockSpec(w.shape, lambda i: (0, 0)) for w in ws]
    out_specs, out_shape = [], []
    for w in ws:
        out_specs += [pl.BlockSpec(w.shape, lambda i: (0, 0))] * 3
        out_shape += [jax.ShapeDtypeStruct(w.shape, F32)] * 3
    res = pl.pallas_call(body, name=name, grid=(1,), in_specs=specs * 4 + [pl.BlockSpec(memory_space=pl.ANY)],
                         out_specs=out_specs, out_shape=out_shape, compiler_params=_params("arbitrary"))(*ws, *gs, *ms, *vs, after)
    return [tuple(res[3 * k:3 * k + 3]) for k in range(n)]


def _all_devices(x, y, c):
    out = []
    for k in range(1, N_DEV):
        peer = (x ^ ((k >> 2) & 1), y ^ ((k >> 1) & 1), c ^ (k & 1))
        out.append((peer, 4 * peer[0] + 2 * peer[1] + peer[2]))
    return out


def _ada_exchange(c, w_shard, b_ada):
    bsz, d = c.shape
    cs = w_shard.shape[1]

    def body(c_ref, w_ref, b_ref, mod_ref, act_ref, c_all, part, pieces, csend, crecv, psend, precv):
        x, y, core = _place()
        me = 4 * x + 2 * y + core
        chip = 2 * x + y
        c_all[me] = c_ref[...]
        peers = _all_devices(x, y, core)
        copies = []
        for k, (peer, _) in enumerate(peers):
            cp = pltpu.make_async_remote_copy(src_ref=c_ref, dst_ref=c_all.at[me], send_sem=csend.at[k], recv_sem=crecv.at[k],
                                              device_id=peer, device_id_type=MESH)
            cp.start()
            copies.append(cp)
        for k, (_, src) in enumerate(peers):
            pltpu.make_async_remote_copy(src_ref=c_ref, dst_ref=c_all.at[src], send_sem=csend.at[k], recv_sem=crecv.at[k],
                                         device_id=(x, y, core), device_id_type=MESH).wait_recv()
        rows = jnp.concatenate([c_all[i] for i in range(N_DEV)], axis=0)
        act = rows * _sigmoid(rows)
        act_ref[...] = act
        prod = jnp.dot(act.astype(BF16), w_ref[...].astype(BF16), preferred_element_type=F32)
        for i in range(N_DEV):
            part[i] = prod[i * bsz:(i + 1) * bsz, :]
        pieces[chip] = part[me]
        chips = _other_chips(x, y)
        for k, (cx, cy) in enumerate(chips):
            cp = pltpu.make_async_remote_copy(src_ref=part.at[4 * cx + 2 * cy + core], dst_ref=pieces.at[chip],
                                              send_sem=psend.at[k], recv_sem=precv.at[k], device_id=(cx, cy, core),
                                              device_id_type=MESH)
            cp.start()
            copies.append(cp)
        for k, (cx, cy) in enumerate(chips):
            pltpu.make_async_remote_copy(src_ref=part.at[me], dst_ref=pieces.at[2 * cx + cy], send_sem=psend.at[k],
                                         recv_sem=precv.at[k], device_id=(cx, cy, core), device_id_type=MESH).wait_recv()
        for cp in copies:
            cp.wait_send()
        mod_ref[...] = jnp.concatenate([pieces[j] for j in range(N_CHIPS)], axis=1) + b_ref[...]

    vm = pl.BlockSpec(memory_space=pltpu.VMEM)
    return pl.pallas_call(
        body, name="ada_exchange", in_specs=[vm] * 3, out_specs=[vm] * 2,
        out_shape=[jax.ShapeDtypeStruct((bsz, 3 * d), F32), jax.ShapeDtypeStruct((N_DEV * bsz, d), F32)],
        scratch_shapes=[pltpu.VMEM((N_DEV, bsz, d), F32), pltpu.VMEM((N_DEV, bsz, cs), F32), pltpu.VMEM((N_CHIPS, bsz, cs), F32),
                        pltpu.SemaphoreType.DMA((N_DEV - 1,)), pltpu.SemaphoreType.DMA((N_DEV - 1,)),
                        pltpu.SemaphoreType.DMA((3,)), pltpu.SemaphoreType.DMA((3,))],
        compiler_params=pltpu.CompilerParams(vmem_limit_bytes=VMEM_LIMIT_V7X))(c, w_shard, b_ada)


def _grad_and_adamw_w_ada(act_t, d_cols, w, m, v):
    d, n = act_t.shape
    cs = d_cols.shape[1]
    tn = min(256, cs)

    def body(a_ref, g_ref, w_ref, m_ref, v_ref, go_ref, d_ref, nm_ref, nv_ref):
        a, g = a_ref[...], g_ref[...]
        acc = a[:, 0:1] * g[0:1, :]
        for b in range(1, n):
            acc = acc + a[:, b:b + 1] * g[b:b + 1, :]
        go_ref[...] = acc
        nm = ADAM_B1 * m_ref[...] + (1.0 - ADAM_B1) * acc
        nv = ADAM_B2 * v_ref[...] + (1.0 - ADAM_B2) * (acc * acc)
        m_hat = nm / (1.0 - ADAM_B1 ** ADAM_STEP)
        v_hat = nv / (1.0 - ADAM_B2 ** ADAM_STEP)
        d_ref[...] = -ADAM_LR * (m_hat / (jnp.sqrt(v_hat) + ADAM_EPS) + ADAM_WD * w_ref[...])
        nm_ref[...] = nm
        nv_ref[...] = nv

    tile = pl.BlockSpec((d, tn), lambda j: (0, j))
    shp = jax.ShapeDtypeStruct((d, cs), F32)
    return pl.pallas_call(body, name="adamw_w_ada", grid=(cs // tn,),
                          in_specs=[pl.BlockSpec((d, n), lambda j: (0, 0)), pl.BlockSpec((n, tn), lambda j: (0, j)), tile, tile, tile],
                          out_specs=[tile] * 4, out_shape=[shp] * 4, compiler_params=_params("parallel"))(act_t, d_cols, w, m, v)


def _permute_w_in(w_nat, lay):
    d = lay.d
    group = 4

    def call(name, width, n_pieces, nat_piece, out_block0, prev):
        def body(*refs):
            refs[-1][...] = jnp.concatenate([r[...] for r in refs[:group]], axis=1)

        in_specs = [pl.BlockSpec((d, width), lambda s, m=m: (0, nat_piece(group * s + m))) for m in range(group)]
        args = [w_nat] * group
        aliases = {}
        if prev is not None:
            in_specs.append(pl.BlockSpec(memory_space=pl.ANY))
            args.append(prev)
            aliases = {group: 0}
        return pl.pallas_call(
            body, name=name, grid=(n_pieces // group,), in_specs=in_specs,
            out_specs=pl.BlockSpec((d, group * width), lambda s: (0, out_block0 + s)),
            out_shape=jax.ShapeDtypeStruct((d, lay.np), BF16), input_output_aliases=aliases,
            compiler_params=_params("arbitrary"))(*args)

    w_all = call("permute_w_attn", SLAB, 2 * PAIR_SLABS, lay.attn_nat_slab, 0, None)
    n_rest = 6 * d // CONV_TILE
    if n_rest % group:
        group = 2
    return call("permute_w_rest", CONV_TILE, n_rest, lay.rest_nat_tile, lay.c0 // (group * CONV_TILE), w_all)


def _project(x2, mod3, w_all, b_all, seq, col0, ncols, tn, out_dtype, want_ht, name):
    t, d = x2.shape
    tm = min(2048, seq)
    per_seq = seq // tm
    j0 = col0 // tn

    def body(x_ref, mod_ref, w_ref, b_ref, o_ref, *rest):
        h_ref = rest[-1]

        @pl.when(pl.program_id(1) == 0)
        def _():
            h = x_ref[...] * (1.0 + mod_ref[:, d:2 * d]) + mod_ref[:, 0:d]
            h_ref[...] = h.astype(BF16)
            if want_ht:
                rest[0][...] = h.T.astype(BF16)

        o_ref[...] = (jnp.dot(h_ref[...], w_ref[...], preferred_element_type=F32) + b_ref[...]).astype(out_dtype)

    out_shape = [jax.ShapeDtypeStruct((t, ncols), out_dtype)]
    out_specs = [pl.BlockSpec((tm, tn), lambda i, j: (i, j))]
    if want_ht:
        out_shape.append(jax.ShapeDtypeStruct((d, t), BF16))
        out_specs.append(pl.BlockSpec((d, tm), lambda i, j: (0, i)))
    return pl.pallas_call(
        body, name=name, grid=(t // tm, ncols // tn),
        in_specs=[pl.BlockSpec((tm, d), lambda i, j: (i, 0)),
                  pl.BlockSpec((None, 1, 3 * d), lambda i, j: (i // per_seq, 0, 0)),
                  pl.BlockSpec((d, tn), lambda i, j: (0, j0 + j)),
                  pl.BlockSpec((1, tn), lambda i, j: (0, j0 + j))],
        out_specs=out_specs, out_shape=out_shape,
        scratch_shapes=[pltpu.VMEM((tm, d), BF16)],
        compiler_params=_params("arbitrary", "arbitrary"))(x2, mod3, w_all, b_all)


def _slope(g, p, hh):
    head = 4 * g + 2 * p + hh
    return 2.0 ** (-ALIBI_MAX_EXP * (head + 1.0) / N_HEADS)


def _ld_rows(ref, start, n, stride):
    if stride == 1:
        return ref[pl.ds(start, n), :]
    return ref[pl.ds(start, n, stride=stride), :]


def _st_rows(ref, start, n, stride, val):
    if stride == 1:
        ref[pl.ds(start, n), :] = val
    else:
        ref[pl.ds(start, n, stride=stride), :] = val


def _sub_blocks(g, seq):
    return seq // DILATIONS[g] // SUB


def _key_rows(g, seq):
    return SUB if _sub_blocks(g, seq) == 1 else 2 * SUB


def _fill_bias(bias_ref, p, seq):
    for g in range(N_GROUPS):
        nk = _key_rows(g, seq)
        diff = lax.broadcasted_iota(jnp.int32, (SUB, nk), 0) - lax.broadcasted_iota(jnp.int32, (SUB, nk), 1)
        for i, off in enumerate((0, SUB)):
            if i == 1 and nk == SUB:
                continue
            delta = diff + off
            ok = (delta >= 0) & (delta <= SUB)
            dist = (delta * DILATIONS[g]).astype(F32)
            for hh in range(2):
                slope = jnp.where(p == 0, _slope(g, 0, hh), _slope(g, 1, hh))
                bias_ref[g, i, hh, :, 0:nk] = jnp.where(ok, -slope * dist, NEG)


def _to_sub_major(pa_ref, col, sub_ref, stage, dil, seq):
    cols = slice(col * SLAB, (col + 1) * SLAB)
    if dil == 1:
        sub_ref[...] = pa_ref[:, cols]
        return
    n = seq // dil
    stage[...] = pa_ref[:, cols].astype(F32)
    for r in range(dil):
        sub_ref[pl.ds(r * n, n), :] = stage[pl.ds(r, n, stride=dil), :].astype(BF16)


def _block_rows(it, g, seq):
    dil, nb = DILATIONS[g], _sub_blocks(g, seq)
    row0 = pl.multiple_of(it * SUB, SUB)
    if nb == 1:
        return row0, row0, 0, it
    blk = it % nb
    first = blk == 0
    krow0 = pl.multiple_of(row0 - jnp.where(first, 0, SUB), SUB)
    nat = row0 if dil == 1 else it // nb + dil * SUB * blk
    return row0, krow0, jnp.where(first, 0, 1), nat


def _nt(a, b):
    return lax.dot_general(a, b, (((1,), (1,)), ((), ())), preferred_element_type=F32)


def _tn(a, b):
    return lax.dot_general(a, b, (((0,), (0,)), ((), ())), preferred_element_type=F32)


def _head_sums(t):
    rows = t.shape[0]
    lo = jnp.broadcast_to(jnp.sum(t[:, :HEAD_DIM], axis=-1, keepdims=True), (rows, HEAD_DIM))
    hi = jnp.broadcast_to(jnp.sum(t[:, HEAD_DIM:], axis=-1, keepdims=True), (rows, HEAD_DIM))
    return jnp.concatenate([lo, hi], axis=-1)


def _attn_fwd(pa, bsz, seq):
    t = pa.shape[0]
    n_blocks = seq // SUB
    chunk = 256

    def body(pa_ref, o_ref, lse_ref, a_ref, sub, stage, bias_ref, s_buf, p_buf, l_buf):
        p = pl.program_id(1)
        _fill_bias(bias_ref, p, seq)
        head0 = lax.broadcasted_iota(jnp.int32, (SUB, SLAB), 1) < HEAD_DIM
        for g in range(N_GROUPS):
            dil = DILATIONS[g]
            for w in range(3):
                _to_sub_major(pa_ref, 3 * w + g, sub.at[w], stage, dil, seq)
            nk = _key_rows(g, seq)

            def trip(i, carry, g=g, dil=dil, nk=nk):
                places = [_block_rows(BLOCKS_PER_TRIP * i + j, g, seq) for j in range(BLOCKS_PER_TRIP)]
                for j, (row0, krow0, _, _) in enumerate(places):
                    q = sub[0, pl.ds(row0, SUB), :]
                    zero = jnp.zeros_like(q)
                    q2 = jnp.concatenate([jnp.where(head0, q, zero), jnp.where(head0, zero, q)], axis=0) * (HEAD_DIM ** -0.5)
                    s_buf[j, :, 0:nk] = _nt(q2, sub[1, pl.ds(krow0, nk), :])
                for j, (_, _, bi, _) in enumerate(places):
                    for c in range(0, 2 * SUB, SOFTMAX_ROWS):
                        hh, r = divmod(c, SUB)
                        s = s_buf[j, c:c + SOFTMAX_ROWS, 0:nk] + bias_ref[g, bi, hh, r:r + SOFTMAX_ROWS, 0:nk]
                        m = jnp.max(s, axis=-1, keepdims=True)
                        e = jnp.exp(s - m)
                        den = jnp.sum(e, axis=-1, keepdims=True)
                        p_buf[j, c:c + SOFTMAX_ROWS, 0:nk] = (e * (1.0 / den)).astype(BF16)
                        l_buf[j, c:c + SOFTMAX_ROWS, :] = jnp.broadcast_to(m + jnp.log(den), (SOFTMAX_ROWS, SLAB))
                for j, (_, krow0, _, nat) in enumerate(places):
                    o2 = jnp.dot(p_buf[j, :, 0:nk], sub[2, pl.ds(krow0, nk), :], preferred_element_type=F32)
                    _st_rows(o_ref.at[g], nat, SUB, dil, jnp.where(head0, o2[0:SUB], o2[SUB:2 * SUB]))
                    _st_rows(lse_ref.at[g], nat, SUB, dil, jnp.where(head0, l_buf[j, 0:SUB, :], l_buf[j, SUB:2 * SUB, :]))
                return carry

            lax.fori_loop(0, n_blocks // BLOCKS_PER_TRIP, trip, 0)

        def mix(i, carry):
            rows = pl.ds(pl.multiple_of(i * chunk, chunk), chunk)
            l0, l1, l2 = lse_ref[0, rows, :], lse_ref[1, rows, :], lse_ref[2, rows, :]
            m = jnp.maximum(jnp.maximum(l0, l1), l2)
            e0, e1, e2 = jnp.exp(l0 - m), jnp.exp(l1 - m), jnp.exp(l2 - m)
            tot = e0 + e1 + e2
            o = (e0 / tot) * o_ref[0, rows, :] + (e1 / tot) * o_ref[1, rows, :] + (e2 / tot) * o_ref[2, rows, :]
            z = pa_ref[rows, 9 * SLAB:10 * SLAB].astype(F32)
            a_ref[rows, :] = (o * (z * _sigmoid(z))).astype(BF16)
            return carry

        lax.fori_loop(0, seq // chunk, mix, 0)

    big = jax.ShapeDtypeStruct((N_GROUPS, t, 2 * SLAB), F32)
    return pl.pallas_call(
        body, name="attn_fwd", grid=(bsz, 2),
        in_specs=[pl.BlockSpec((seq, PAIR_COLS), lambda b, p: (b, p))],
        out_specs=[pl.BlockSpec((N_GROUPS, seq, SLAB), lambda b, p: (0, b, p)),
                   pl.BlockSpec((N_GROUPS, seq, SLAB), lambda b, p: (0, b, p)),
                   pl.BlockSpec((seq, SLAB), lambda b, p: (b, p))],
        out_shape=[big, big, jax.ShapeDtypeStruct((t, 2 * SLAB), BF16)],
        scratch_shapes=[pltpu.VMEM((3, seq, SLAB), BF16), pltpu.VMEM((seq, SLAB), F32),
                        pltpu.VMEM((N_GROUPS, 2, 2, SUB, 2 * SUB), F32), pltpu.VMEM((BLOCKS_PER_TRIP, 2 * SUB, 2 * SUB), F32),
                        pltpu.VMEM((BLOCKS_PER_TRIP, 2 * SUB, 2 * SUB), BF16), pltpu.VMEM((BLOCKS_PER_TRIP, 2 * SUB, SLAB), F32)],
        compiler_params=_params("arbitrary", "arbitrary"))(pa)


def _shift_down(v, k, rows):
    return jnp.where(rows >= k, pltpu.roll(v, k, 0), 0.0)


def _shift_up(v, k, rows):
    n = v.shape[0]
    return jnp.where(rows < n - k, pltpu.roll(v, n - k, 0), 0.0)


def _conv_fwd(pr, conv_w, bsz, seq, d):
    t = pr.shape[0]
    ct = CONV_TILE

    def body(p_ref, cw_ref, o_ref):
        u = p_ref[:, 2 * ct:3 * ct].astype(F32) * p_ref[:, 0:ct].astype(F32)
        cw = cw_ref[...]
        rows = lax.broadcasted_iota(jnp.int32, u.shape, 0)
        conv = cw[0:1, :] * _shift_down(u, 2, rows)
        conv = conv + cw[1:2, :] * _shift_down(u, 1, rows)
        conv = conv + cw[2:3, :] * u
        z = p_ref[:, 3 * ct:4 * ct].astype(F32)
        o_ref[...] = (p_ref[:, ct:2 * ct].astype(F32) * conv * (z * _sigmoid(z))).astype(BF16)

    return pl.pallas_call(
        body, name="conv_fwd", grid=(bsz, d // ct),
        in_specs=[pl.BlockSpec((seq, 4 * ct), lambda b, j: (b, j)), pl.BlockSpec((3, ct), lambda b, j: (0, j))],
        out_specs=pl.BlockSpec((seq, ct), lambda b, j: (b, j)),
        out_shape=jax.ShapeDtypeStruct((t, d), BF16), compiler_params=_params("parallel", "parallel"))(pr, conv_w)


def _tail(a_in, b_in, pr, x2, target2, mod3, w_pa, w_pc, w_out, b_out, ln_g, ln_b, seq, lay):
    t, d = x2.shape
    tm = 512
    per_seq = seq // tm
    n_steps = t // tm
    gate_blk = 4 * d // d

    def nt(a, b):
        return lax.dot_general(a, b, (((1,), (1,)), ((), ())), preferred_element_type=F32)

    def tn(a, b):
        return lax.dot_general(a, b, (((0,), (0,)), ((), ())), preferred_element_type=F32)

    def body(a_ref, b_ref, ga_ref, gb_ref, x_ref, tg_ref, mod_ref, wpa_ref, wpc_ref, wo_ref, bo_ref, lg_ref, lb_ref,
             dpg_ref, da_ref, db_ref, gx_ref, dgate_ref, small_ref, gwpa_hbm, gwpc_hbm, gwo_hbm,
             acc_pa, acc_pc, acc_o, sem):
        i = pl.program_id(0)

        @pl.when(i == 0)
        def _():
            acc_pa[...] = jnp.zeros_like(acc_pa)
            acc_pc[...] = jnp.zeros_like(acc_pc)
            acc_o[...] = jnp.zeros_like(acc_o)
            small_ref[...] = jnp.zeros_like(small_ref)

        @pl.when(i % per_seq == 0)
        def _():
            dgate_ref[...] = jnp.zeros_like(dgate_ref)

        halves = [slice(k * (tm // 2), (k + 1) * (tm // 2)) for k in range(2)]
        gate = mod_ref[:, 2 * d:3 * d]
        a_bf = [a_ref[rs, :] for rs in halves]
        b_bf = [b_ref[rs, :] for rs in halves]
        y_attn = [jnp.dot(a, wpa_ref[...], preferred_element_type=F32) for a in a_bf]
        y_conv = [jnp.dot(b, wpc_ref[...], preferred_element_type=F32) for b in b_bf]
        sa = [_sigmoid(ga_ref[rs, :].astype(F32)) for rs in halves]
        sb = [_sigmoid(gb_ref[rs, :].astype(F32)) for rs in halves]
        merged = [(sa[k] * y_attn[k] + sb[k] * y_conv[k]).astype(BF16) for k in range(2)]
        mo = [jnp.dot(m, wo_ref[...], preferred_element_type=F32) + bo_ref[...] for m in merged]
        d_mo_bf = []
        for k, rs in enumerate(halves):
            r = ALPHA * x_ref[rs, :] + gate * mo[k]
            mu = jnp.mean(r, axis=-1, keepdims=True)
            cen = r - mu
            var = jnp.mean(cen * cen, axis=-1, keepdims=True)
            rstd = lax.rsqrt(var + LN_EPS)
            xhat = cen * rstd
            err = xhat * lg_ref[...] + lb_ref[...] - tg_ref[rs, :]
            dy = err * (1.0 / d)
            dxhat = dy * lg_ref[...]
            dr = rstd * (dxhat - jnp.mean(dxhat, axis=-1, keepdims=True)
                         - xhat * jnp.mean(dxhat * xhat, axis=-1, keepdims=True))
            gx_ref[rs, :] = ALPHA * dr
            dgate_ref[...] += jnp.sum(dr * mo[k], axis=0, keepdims=True)
            d_mo = dr * gate
            small_ref[0:1, :] += jnp.sum(d_mo, axis=0, keepdims=True)
            small_ref[1:2, :] += jnp.sum(dy * xhat, axis=0, keepdims=True)
            small_ref[2:3, :] += jnp.sum(dy, axis=0, keepdims=True)
            small_ref[3:4, :] += jnp.sum(err * err, axis=0, keepdims=True)
            d_mo_bf.append(d_mo.astype(BF16))
        acc_o[...] += tn(jnp.concatenate(merged, axis=0), jnp.concatenate(d_mo_bf, axis=0))
        dmerged = [nt(g, wo_ref[...]) for g in d_mo_bf]
        dy_attn, dy_conv = [], []
        for k, rs in enumerate(halves):
            dy_attn.append((dmerged[k] * sa[k]).astype(BF16))
            dy_conv.append((dmerged[k] * sb[k]).astype(BF16))
            dpg_ref[rs, 0:d] = (dmerged[k] * y_attn[k] * sa[k] * (1.0 - sa[k])).astype(BF16)
            dpg_ref[rs, d:2 * d] = (dmerged[k] * y_conv[k] * sb[k] * (1.0 - sb[k])).astype(BF16)
        acc_pa[...] += tn(a_ref[...], jnp.concatenate(dy_attn, axis=0))
        acc_pc[...] += tn(b_ref[...], jnp.concatenate(dy_conv, axis=0))
        for k, rs in enumerate(halves):
            da_ref[rs, :] = nt(dy_attn[k], wpa_ref[...])
            db_ref[rs, :] = nt(dy_conv[k], wpc_ref[...])

        @pl.when(i == n_steps - 1)
        def _():
            copies = [pltpu.make_async_copy(acc_pa, gwpa_hbm, sem.at[0]), pltpu.make_async_copy(acc_pc, gwpc_hbm, sem.at[1]),
                      pltpu.make_async_copy(acc_o, gwo_hbm, sem.at[2])]
            for cp in copies:
                cp.start()
            for cp in copies:
                cp.wait()

    row = lambda w: pl.BlockSpec((tm, w), lambda i: (i, 0))
    const = lambda shp: pl.BlockSpec(shp, lambda i: (0,) * len(shp), pipeline_mode=pl.Buffered(1))
    any_spec = pl.BlockSpec(memory_space=pl.ANY)
    return pl.pallas_call(
        body, name="tail", grid=(n_steps,),
        in_specs=[row(Z_WIDTH), row(d),
                  pl.BlockSpec((tm, d), lambda i: (i, gate_blk)), pl.BlockSpec((tm, d), lambda i: (i, gate_blk + 1)),
                  row(d), row(d), pl.BlockSpec((None, 1, 3 * d), lambda i: (i // per_seq, 0, 0)),
                  const((Z_WIDTH, d)), const((d, d)), const((d, d)), const((1, d)), const((1, d)), const((1, d))],
        out_specs=[pl.BlockSpec((tm, 2 * d), lambda i: (i, lay.g0 // (2 * d))), row(Z_WIDTH), row(d), row(d),
                   pl.BlockSpec((None, 1, d), lambda i: (i // per_seq, 0, 0)), pl.BlockSpec((8, d), lambda i: (0, 0)),
                   HBM_SPEC, HBM_SPEC, HBM_SPEC],
        out_shape=[jax.ShapeDtypeStruct((t, lay.np), BF16), jax.ShapeDtypeStruct((t, Z_WIDTH), F32),
                   jax.ShapeDtypeStruct((t, d), F32), jax.ShapeDtypeStruct((t, d), F32),
                   jax.ShapeDtypeStruct((t // seq, 1, d), F32), jax.ShapeDtypeStruct((8, d), F32),
                   pltpu.HBM((Z_WIDTH, d), F32), pltpu.HBM((d, d), F32), pltpu.HBM((d, d), F32)],
        scratch_shapes=[pltpu.VMEM((Z_WIDTH, d), F32), pltpu.VMEM((d, d), F32), pltpu.VMEM((d, d), F32),
                        pltpu.SemaphoreType.DMA((3,))],
        compiler_params=_params("arbitrary"),
    )(a_in, b_in, pr, pr, x2, target2, mod3, w_pa, w_pc, w_out, b_out, ln_g, ln_b)


def _conv_bwd(dproj, db, pr, conv_w, bsz, seq, lay):
    d = lay.d
    ct = CONV_TILE
    base = lay.c0 // (4 * ct)

    def body(dp_in, db_ref, p_ref, cw_ref, dp_ref, gcw_ref):
        del dp_in
        u_x, g_b, g_c, z = [p_ref[:, k * ct:(k + 1) * ct].astype(F32) for k in range(4)]
        cw = cw_ref[...]
        u = g_c * u_x
        rows = lax.broadcasted_iota(jnp.int32, u.shape, 0)
        u1, u2 = _shift_down(u, 1, rows), _shift_down(u, 2, rows)
        conv = cw[0:1, :] * u2 + cw[1:2, :] * u1 + cw[2:3, :] * u
        sig = _sigmoid(z)
        sl = z * sig
        dbv = db_ref[...]
        gbc = g_b * conv
        dp_ref[:, ct:2 * ct] = (dbv * sl * conv).astype(BF16)
        dp_ref[:, 3 * ct:4 * ct] = (dbv * gbc * (sig * (1.0 + z * (1.0 - sig)))).astype(BF16)
        dconv = dbv * sl * g_b

        @pl.when(pl.program_id(1) == 0)
        def _():
            gcw_ref[...] = jnp.zeros_like(gcw_ref)

        gcw_ref[0:1, :] += jnp.sum(dconv * u2, axis=0, keepdims=True)
        gcw_ref[1:2, :] += jnp.sum(dconv * u1, axis=0, keepdims=True)
        gcw_ref[2:3, :] += jnp.sum(dconv * u, axis=0, keepdims=True)
        du = cw[2:3, :] * dconv + cw[1:2, :] * _shift_up(dconv, 1, rows) + cw[0:1, :] * _shift_up(dconv, 2, rows)
        dp_ref[:, 0:ct] = (du * g_c).astype(BF16)
        dp_ref[:, 2 * ct:3 * ct] = (du * u_x).astype(BF16)

    return pl.pallas_call(
        body, name="conv_bwd", grid=(d // ct, bsz),
        in_specs=[pl.BlockSpec(memory_space=pl.ANY), pl.BlockSpec((seq, ct), lambda j, b: (b, j)),
                  pl.BlockSpec((seq, 4 * ct), lambda j, b: (b, j)), pl.BlockSpec((3, ct), lambda j, b: (0, j))],
        out_specs=[pl.BlockSpec((seq, 4 * ct), lambda j, b: (b, base + j)), pl.BlockSpec((8, ct), lambda j, b: (0, j))],
        out_shape=[jax.ShapeDtypeStruct(dproj.shape, BF16), jax.ShapeDtypeStruct((8, d), F32)],
        input_output_aliases={0: 0}, compiler_params=_params("arbitrary", "arbitrary"))(dproj, db, pr, conv_w)


def _attn_bwd(dproj, pa, o_all, lse_all, da, bsz, seq, after):
    n_blocks = seq // SUB
    chunk = 256
    per_trip = BLOCKS_PER_TRIP_BWD

    def body(dp_in, pa_ref, o_ref, lse_ref, da_ref, after_ref, dp_ref, sub, stage, dsub, dog, cvec, bias_ref,
             s_buf, dp_buf, ds_buf, pb_buf, q2_buf, do2_buf, l_buf, c_buf):
        del dp_in, after_ref
        p = pl.program_id(1)
        _fill_bias(bias_ref, p, seq)
        head0 = lax.broadcasted_iota(jnp.int32, (SUB, SLAB), 1) < HEAD_DIM

        def mix_bwd(i, carry):
            rows = pl.ds(pl.multiple_of(i * chunk, chunk), chunk)
            ls = [lse_ref[g, rows, :] for g in range(N_GROUPS)]
            os_ = [o_ref[g, rows, :] for g in range(N_GROUPS)]
            m = jnp.maximum(jnp.maximum(ls[0], ls[1]), ls[2])
            es = [jnp.exp(l - m) for l in ls]
            tot = es[0] + es[1] + es[2]
            ws = [e / tot for e in es]
            o = ws[0] * os_[0] + ws[1] * os_[1] + ws[2] * os_[2]
            z = pa_ref[rows, 9 * SLAB:10 * SLAB].astype(F32)
            sig = _sigmoid(z)
            dav = da_ref[rows, :]
            do = dav * (z * sig)
            dp_ref[rows, 9 * SLAB:10 * SLAB] = (dav * o * (sig * (1.0 + z * (1.0 - sig)))).astype(BF16)
            wsum = _head_sums(do * o)
            for g in range(N_GROUPS):
                dog[g, rows, :] = ws[g] * do
                cvec[g, rows, :] = -(ws[g] * wsum)
            return carry

        lax.fori_loop(0, seq // chunk, mix_bwd, 0)

        for g in range(N_GROUPS):
            dil = DILATIONS[g]
            for w in range(3):
                _to_sub_major(pa_ref, 3 * w + g, sub.at[w], stage, dil, seq)
            dsub[1] = jnp.zeros((seq, SLAB), F32)
            dsub[2] = jnp.zeros((seq, SLAB), F32)
            nk = _key_rows(g, seq)

            def trip(i, carry, g=g, dil=dil, nk=nk):
                places = [_block_rows(per_trip * i + j, g, seq) for j in range(per_trip)]
                for j, (row0, krow0, _, nat) in enumerate(places):
                    q = sub[0, pl.ds(row0, SUB), :]
                    do = _ld_rows(dog.at[g], nat, SUB, dil).astype(BF16)
                    zero = jnp.zeros_like(q)
                    q2 = jnp.concatenate([jnp.where(head0, q, zero), jnp.where(head0, zero, q)], axis=0)
                    do2 = jnp.concatenate([jnp.where(head0, do, zero), jnp.where(head0, zero, do)], axis=0)
                    q2_buf[j] = q2
                    do2_buf[j] = do2
                    s_buf[j, :, 0:nk] = _nt(q2 * (HEAD_DIM ** -0.5), sub[1, pl.ds(krow0, nk), :])
                    dp_buf[j, :, 0:nk] = _nt(do2, sub[2, pl.ds(krow0, nk), :])
                    l_buf[j] = _ld_rows(lse_ref.at[g], nat, SUB, dil)
                    c_buf[j] = _ld_rows(cvec.at[g], nat, SUB, dil)
                for j, (_, _, bi, _) in enumerate(places):
                    for c in range(0, 2 * SUB, SOFTMAX_ROWS):
                        hh, r = divmod(c, SUB)
                        lane = hh * HEAD_DIM
                        s = s_buf[j, c:c + SOFTMAX_ROWS, 0:nk] + bias_ref[g, bi, hh, r:r + SOFTMAX_ROWS, 0:nk]
                        prob = jnp.exp(s - l_buf[j, r:r + SOFTMAX_ROWS, lane:lane + 1])
                        dprob = dp_buf[j, c:c + SOFTMAX_ROWS, 0:nk] + c_buf[j, r:r + SOFTMAX_ROWS, lane:lane + 1]
                        ds_buf[j, c:c + SOFTMAX_ROWS, 0:nk] = (prob * dprob * (HEAD_DIM ** -0.5)).astype(BF16)
                        pb_buf[j, c:c + SOFTMAX_ROWS, 0:nk] = prob.astype(BF16)
                for j, (row0, krow0, _, _) in enumerate(places):
                    ds = ds_buf[j, :, 0:nk]
                    dq2 = jnp.dot(ds, sub[1, pl.ds(krow0, nk), :], preferred_element_type=F32)
                    dsub[0, pl.ds(row0, SUB), :] = jnp.where(head0, dq2[0:SUB], dq2[SUB:2 * SUB])
                    dsub[1, pl.ds(krow0, nk), :] += _tn(ds, q2_buf[j])
                    dsub[2, pl.ds(krow0, nk), :] += _tn(pb_buf[j, :, 0:nk], do2_buf[j])
                return carry

            lax.fori_loop(0, n_blocks // per_trip, trip, 0)
            for w in range(3):
                cols = slice((3 * w + g) * SLAB, (3 * w + g + 1) * SLAB)
                if dil == 1:
                    dp_ref[:, cols] = dsub[w].astype(BF16)
                else:
                    n = seq // dil
                    for r in range(dil):
                        stage[pl.ds(r, n, stride=dil), :] = dsub[w, pl.ds(r * n, n), :]
                    dp_ref[:, cols] = stage[...].astype(BF16)

    return pl.pallas_call(
        body, name="attn_bwd", grid=(bsz, 2),
        in_specs=[pl.BlockSpec(memory_space=pl.ANY), pl.BlockSpec((seq, PAIR_COLS), lambda b, p: (b, p)),
                  pl.BlockSpec((N_GROUPS, seq, SLAB), lambda b, p: (0, b, p)),
                  pl.BlockSpec((N_GROUPS, seq, SLAB), lambda b, p: (0, b, p)),
                  pl.BlockSpec((seq, SLAB), lambda b, p: (b, p)), pl.BlockSpec(memory_space=pl.ANY)],
        out_specs=pl.BlockSpec((seq, PAIR_COLS), lambda b, p: (b, p)),
        out_shape=jax.ShapeDtypeStruct(dproj.shape, BF16), input_output_aliases={0: 0},
        scratch_shapes=[pltpu.VMEM((3, seq, SLAB), BF16), pltpu.VMEM((seq, SLAB), F32), pltpu.VMEM((3, seq, SLAB), F32),
                        pltpu.VMEM((3, seq, SLAB), F32), pltpu.VMEM((3, seq, SLAB), F32),
                        pltpu.VMEM((N_GROUPS, 2, 2, SUB, 2 * SUB), F32),
                        pltpu.VMEM((per_trip, 2 * SUB, 2 * SUB), F32), pltpu.VMEM((per_trip, 2 * SUB, 2 * SUB), F32),
                        pltpu.VMEM((per_trip, 2 * SUB, 2 * SUB), BF16), pltpu.VMEM((per_trip, 2 * SUB, 2 * SUB), BF16),
                        pltpu.VMEM((per_trip, 2 * SUB, SLAB), BF16), pltpu.VMEM((per_trip, 2 * SUB, SLAB), BF16),
                        pltpu.VMEM((per_trip, SUB, SLAB), F32), pltpu.VMEM((per_trip, SUB, SLAB), F32)],
        compiler_params=_params("arbitrary", "arbitrary"))(dproj, pa, o_all, lse_all, da, after)


def _grad_h(dproj, w_all, gx0, x2, mod3, seq, lay):
    t, d = x2.shape
    tm, tn = min(512, seq), min(512, d)
    per_seq = seq // tm

    def body(dp_ref, w_ref, gx0_ref, x_ref, scale_ref, gx_ref, dmod_ref):
        dh = _nt(dp_ref[:, 0:ATT], w_ref[:, 0:ATT]) + _nt(dp_ref[:, lay.c0:], w_ref[:, lay.c0:])
        gx_ref[...] = gx0_ref[...] + dh * (1.0 + scale_ref[...])

        @pl.when(pl.program_id(1) % per_seq == 0)
        def _():
            dmod_ref[...] = jnp.zeros_like(dmod_ref)

        dmod_ref[0:1, :] += jnp.sum(dh, axis=0, keepdims=True)
        dmod_ref[1:2, :] += jnp.sum(dh * x_ref[...], axis=0, keepdims=True)

    tile = pl.BlockSpec((tm, tn), lambda j, i: (i, j))
    return pl.pallas_call(
        body, name="grad_h", grid=(d // tn, t // tm),
        in_specs=[pl.BlockSpec((tm, lay.np), lambda j, i: (i, 0)), pl.BlockSpec((tn, lay.np), lambda j, i: (j, 0)),
                  tile, tile,
                  pl.BlockSpec((None, 1, tn), lambda j, i: (i // per_seq, 0, d // tn + j))],
        out_specs=[tile, pl.BlockSpec((None, 8, tn), lambda j, i: (i // per_seq, 0, j))],
        out_shape=[jax.ShapeDtypeStruct((t, d), F32), jax.ShapeDtypeStruct((t // seq, 8, d), F32)],
        compiler_params=_params("arbitrary", "arbitrary"))(dproj, w_all, gx0, x2, mod3)


def _grad_w_in(ht, dproj, seq, lay, part, prev):
    d, t = ht.shape
    tm = min(t, 2 * seq)
    n_i = t // tm

    def side_by_side(pieces):
        return pieces[0] if len(pieces) == 1 else jnp.concatenate(pieces, axis=1)

    def make_body(n_skip, n_pieces, tn, nat_tile, assemble):
        def body(*refs):
            refs = refs[n_skip:]
            ht_ref, dp_refs = refs[0], refs[1:1 + n_pieces]
            gw_hbm, gb_hbm, acc, bacc, gw_out, gb_out, sem = refs[1 + n_pieces:]
            i, j = pl.program_id(0), pl.program_id(1)
            dp = assemble([r[...] for r in dp_refs])
            part = jnp.dot(ht_ref[...], dp, preferred_element_type=F32)
            bpart = jnp.sum(dp.astype(F32), axis=0, keepdims=True)

            if n_i > 1:
                @pl.when(i == 0)
                def _():
                    acc[j] = part
                    bacc[j] = bpart

                @pl.when((i > 0) & (i < n_i - 1))
                def _():
                    acc[j] += part
                    bacc[j] += bpart

            @pl.when(i == n_i - 1)
            def _():
                gw_out[...] = ((part + acc[j]) if n_i > 1 else part).astype(BF16)
                gb_out[...] = (bpart + bacc[j]) if n_i > 1 else bpart
                cols = pl.ds(pl.multiple_of(nat_tile(j) * tn, SLAB), tn)
                copies = [pltpu.make_async_copy(gw_out, gw_hbm.at[:, cols], sem.at[0]),
                          pltpu.make_async_copy(gb_out, gb_hbm.at[:, cols], sem.at[1])]
                for cp in copies:
                    cp.start()
                for cp in copies:
                    cp.wait()
        return body

    def call(name, pieces, n_tiles, nat_tile, prev, assemble=side_by_side):
        tn = sum(w for w, _ in pieces)
        any_spec = pl.BlockSpec(memory_space=pl.ANY)
        in_specs = [pl.BlockSpec((d, tm), lambda i, j: (0, i))]
        in_specs += [pl.BlockSpec((tm, w), lambda i, j, f=f: (i, f(j))) for w, f in pieces]
        args = [ht] + [dproj] * len(pieces)
        aliases = {}
        if prev is not None:
            in_specs = [any_spec] * 2 + in_specs
            args = list(prev) + args
            aliases = {0: 0, 1: 1}
        return pl.pallas_call(
            make_body(0 if prev is None else 2, len(pieces), tn, nat_tile, assemble), name=name, grid=(n_i, n_tiles),
            in_specs=in_specs,
            out_specs=[any_spec, any_spec],
            out_shape=[jax.ShapeDtypeStruct((d, lay.din), BF16), jax.ShapeDtypeStruct((1, lay.din), F32)],
            input_output_aliases=aliases,
            scratch_shapes=[pltpu.VMEM((n_tiles if n_i > 1 else 1, d, tn), F32), pltpu.VMEM((n_tiles, 1, tn), F32),
                            pltpu.VMEM((d, tn), BF16),
                            pltpu.VMEM((1, tn), F32), pltpu.SemaphoreType.DMA((2,))],
            compiler_params=_params("arbitrary", "arbitrary"))(*args)

    if part == "attn":
        pair_blocks = PAIR_COLS // (2 * SLAB)
        attn_pieces = [(2 * SLAB, lambda j, p=p: p * pair_blocks + j) for p in range(2)]

        def interleave(pieces):
            a, b = pieces
            return jnp.concatenate([a[:, :SLAB], b[:, :SLAB], a[:, SLAB:], b[:, SLAB:]], axis=1)

        return call("grad_w_in_attn", attn_pieces, ATT // 512, lambda j: j, prev, interleave)
    base = lay.c0 // CONV_TILE
    nct = lay.n_conv_tiles
    if nct % 2:
        return call("grad_w_in_rest", [(CONV_TILE, lambda j: base + j)], 6 * d // CONV_TILE, lay.rest_nat_tile, prev)
    half = nct // 2

    def rest_piece(m):
        def perm_tile(j):
            conv = base + 4 * (2 * (j % half) + m) + j // half
            return jnp.where(j < 4 * half, conv, base + 2 * j + m)
        return (CONV_TILE, perm_tile)

    return call("grad_w_in_rest", [rest_piece(0), rest_piece(1)], 6 * d // 512, lambda j: ATT // 512 + j, prev)


def _pack_rows(parts, width=128):
    flat = [p.reshape(-1) for p in parts]
    spans, rows = [], 0
    padded = []
    for f in flat:
        n = -(-f.shape[0] // (8 * width)) * 8
        padded.append(jnp.pad(f, (0, n * width - f.shape[0])).reshape(n, width))
        spans.append((rows, f.shape[0]))
        rows += n
    return jnp.concatenate(padded, axis=0), spans


def _unpack_rows(packed, spans, shapes, width=128):
    out = []
    for (row, n), shp in zip(spans, shapes):
        rows = -(-n // width)
        out.append(packed[row:row + rows].reshape(-1)[:n].reshape(shp))
    return out


def kernel(x, c, w_ada, b_ada, w_in, b_in, conv_w, w_proj_attn, w_proj_conv, w_out, b_out, ln_g, ln_b, loss_target, m_w_ada, m_b_ada, m_w_in, m_b_in, m_conv_w, m_w_proj_attn, m_w_proj_conv, m_w_out, m_b_out, m_ln_g, m_ln_b, v_w_ada, v_b_ada, v_w_in, v_b_in, v_conv_w, v_w_proj_attn, v_w_proj_conv, v_w_out, v_b_out, v_ln_g, v_ln_b):
    bsz, seq, d = x.shape
    t = bsz * seq
    lay = _Layout(d)
    col_sharded = [True, True, False, False]
    red_w = [w_in[0], w_proj_attn[0], w_proj_conv[0], w_out[0]]
    chip = 2 * lax.axis_index("x") + lax.axis_index("y")
    chip1 = chip.astype(jnp.int32).reshape(1)
    core1 = lax.axis_index("c").astype(jnp.int32).reshape(1)
    place = jnp.stack([chip, lax.axis_index("c")]).astype(jnp.int32)
    x2 = x.reshape(t, d)
    target2 = loss_target.reshape(t, d)

    mod, act_all = _ada_exchange(c, w_ada[0], b_ada)
    mod3 = mod.reshape(bsz, 1, 3 * d)

    cw_pad = jnp.pad(conv_w[0], ((0, 5), (0, 0))) + 0.0 * mod[0, 0]
    own_in_full = [_cast_into_full(red_w[0], col_sharded[0], chip1, "cast_shard_in")]
    own_in_full += list(_cast_into_full_small(red_w[1:], col_sharded[1:], chip1, "cast_shards_late"))
    (wi_f,), cw8 = _gather_weights(own_in_full[:1], col_sharded[:1], cw_pad)
    cw_full = cw8[0:3]
    late_copies = _direct_gather_copies(col_sharded[1:])
    late_send, late_recv, late_flying, late_token = _start_copies("gather_late_start", own_in_full[1:], (18,), cw8, late_copies)
    w_all = _permute_w_in(wi_f, lay)
    b_all = lay.perm_vector(b_in) + late_token[0, 0]

    rest_tn = 1024 if (6 * d) % 1024 == 0 else 512
    pa, = _project(x2, mod3, w_all, b_all, seq, 0, ATT, PAIR_COLS, BF16, False, "project_attn")
    pr, ht = _project(x2, mod3, w_all, b_all, seq, lay.c0, 6 * d, rest_tn, BF16, True, "project_rest")
    o_all, lse_all, a_in = _attn_fwd(pa, bsz, seq)
    b_in_act = _conv_fwd(pr, cw_full, bsz, seq, d)
    wpa_f, wpc_f, wo_f = _wait_copies("gather_late_wait", late_flying, late_send, late_recv, b_in_act, late_copies)
    (dproj, da_in, db_in, gx0, dgate, small_tail, gw_pa, gw_pc, gw_out) = _tail(
        a_in, b_in_act, pr, x2, target2, mod3, wpa_f, wpc_f, wo_f, b_out, ln_g, ln_b, seq, lay)

    late_views = _shard_views([gw_pa, gw_pc, gw_out], col_sharded[1:])
    late_lands = [lax.empty((v.shape[0], v.shape[1] // 2, v.shape[2]), v.dtype) for v in late_views]
    xl_send, xl_recv, xl_fly, xl_tok = _start_copies("grads_pair_exchange_late_start", late_views + late_lands, (3,), small_tail,
                                                     _pair_exchange_copies(3, False))
    dproj, gcw = _conv_bwd(dproj, db_in, pr, cw_full + xl_tok[0, 0], bsz, seq, lay)
    xl_done = _wait_copies("grads_pair_exchange_late_wait", xl_fly, xl_send, xl_recv, gcw, _pair_exchange_copies(3, True))
    late_parts = list(_pair_sum_small(xl_done[:3], xl_done[3:], core1, "grads_pair_sum_late"))

    late_cross = _chip_scatter_copies(3, col_sharded[1:])
    late_zone = [lax.empty((3, p.shape[1], _piece_cols(p, cs)), p.dtype) for p, cs in zip(late_parts, col_sharded[1:])]
    sl_send, sl_recv, sl_fly, sl_tok = _start_copies("grads_scatter_late_start", late_parts + late_zone, (9,), core1,
                                                     late_cross)
    dproj = _attn_bwd(dproj, pa, o_all, lse_all, da_in, bsz, seq, sl_tok)
    gw_in_bf, gb_in = _grad_w_in(ht, dproj, seq, lay, "rest", None)
    rest_cols, attn_cols = (ATT, 6 * d), (0, ATT)
    in_land = lax.empty((1, d // 2, lay.din), BF16)
    xi_copies = _pair_exchange_copies(1, False, rest_cols)
    xi_send, xi_recv, xi_fly, _ = _start_copies("grads_pair_exchange_in_start", [gw_in_bf.reshape(1, d, lay.din), in_land], (1,),
                                                gb_in, xi_copies)
    gw_in_bf, gb_in = _grad_w_in(ht, dproj, seq, lay, "attn", (xi_fly[0].reshape(d, lay.din), gb_in))
    xi_done = _wait_copies("grads_pair_exchange_in_wait", [gw_in_bf.reshape(1, d, lay.din), xi_fly[1]], xi_send, xi_recv, gb_in,
                           _pair_exchange_copies(1, True, rest_cols))
    in_got = _pair_exchange_into(xi_done[0], xi_done[1], attn_cols, "grads_pair_exchange_in_attn")
    sl_done = _wait_copies("grads_scatter_late_wait", sl_fly, sl_send, sl_recv, in_got, late_cross)
    late_red = list(_chip_sum_small(sl_done[:3], sl_done[3:], col_sharded[1:], place, "grads_chip_sum_late"))

    in_part = _pair_sum(xi_done[0], in_got, core1, "grads_pair_sum_in")
    in_cross = _chip_scatter_copies(1, col_sharded[:1])
    in_zone = [lax.empty((3, in_part.shape[1], _piece_cols(in_part, True)), in_part.dtype)]
    late_views1 = [f.reshape(1, *f.shape) for f in late_red]
    si_send, si_recv, si_fly, si_tok = _start_copies(
        "grads_scatter_in_join_late_start", [in_part] + in_zone + late_views1, (6,), core1,
        _both_copies(in_cross, 2, _pair_join_copies(3, False, sem0=3)))
    grad_x2, dmod = _grad_h(dproj, w_all, gx0, x2, mod3 + si_tok[0, 0], seq, lay)
    si_done = _wait_copies("grads_scatter_in_join_late_wait", si_fly, si_send, si_recv, grad_x2,
                           _both_copies(in_cross, 2, _pair_join_copies(3, True, sem0=3)))
    late_joined = [f[0] for f in si_done[2:]]
    in_red = _chip_sum(si_done[0], si_done[1], True, place, "grads_chip_sum_in")

    d_ada = jnp.concatenate([dmod[:, 0, :], dmod[:, 1, :], dgate[:, 0, :]], axis=1)
    pieces = [small_tail[3], jnp.sum(d_ada, axis=0), gb_in[0], small_tail[0], small_tail[1], small_tail[2], gcw[0:3]]
    packed, spans = _pack_rows(pieces)
    kept_in, _ = _pack_rows([d_ada])
    rows_all = jnp.concatenate([packed, kept_in], axis=0)
    small_land = lax.empty((N_DEV,) + rows_all.shape, F32)
    sm_send, sm_recv, sm_fly, sm_tok = _start_copies(
        "small_gather_join_in_start", [rows_all, small_land, in_red.reshape(1, *in_red.shape)], (N_DEV,), core1,
        _both_copies(_small_gather_copies, 2, _pair_join_copies(1, False, sem0=N_DEV - 1)))
    big_w = [w_ada[0]] + red_w
    big_m = [m_w_ada[0], m_w_in[0], m_w_proj_attn[0], m_w_proj_conv[0], m_w_out[0]]
    big_v = [v_w_ada[0], v_w_in[0], v_w_proj_attn[0], v_w_proj_conv[0], v_w_out[0]]
    big_out = [None] * 5
    big_out[2:5] = _adamw_together(big_w[2:], late_joined, big_m[2:], big_v[2:], "adamw_late", sm_tok)
    sm_done = _wait_copies("small_gather_join_in_wait", sm_fly, sm_send, sm_recv, big_out[4][0],
                           _both_copies(_small_gather_copies, 2, _pair_join_copies(1, True, sem0=N_DEV - 1)))
    me1 = (4 * lax.axis_index("x") + 2 * lax.axis_index("y") + lax.axis_index("c")).astype(jnp.int32).reshape(1)
    summed, kept = _small_sum(sm_done[0], sm_done[1], me1, packed.shape[0], d)
    loss = summed[0, 0]
    _, g_b_ada, g_b_in, g_b_out, g_ln_g, g_ln_b, g_cw_full = _unpack_rows(
        summed, spans, [(d,), (3 * d,), (lay.din,), (d,), (d,), (d,), (3, d)])
    g_cw = lax.dynamic_slice(g_cw_full, (0, chip * (d // N_CHIPS)), (3, d // N_CHIPS))
    d_ada_all = kept.reshape(N_DEV, -1)[:, :bsz * 3 * d].reshape(N_DEV * bsz, 3 * d)
    ada_cols = 3 * d // N_CHIPS
    d_ada_cols = lax.dynamic_slice(d_ada_all, (0, chip * ada_cols), (N_DEV * bsz, ada_cols))

    g_w_ada, *big_out[0] = _grad_and_adamw_w_ada(act_all.T, d_ada_cols, big_w[0], big_m[0], big_v[0])
    small_w = [b_ada, b_in, conv_w[0], b_out, ln_g, ln_b]
    small_g = [g_b_ada, g_b_in, g_cw, g_b_out, g_ln_g, g_ln_b]
    small_m = [m_b_ada, m_b_in, m_conv_w[0], m_b_out, m_ln_g, m_ln_b]
    small_v = [v_b_ada, v_b_in, v_conv_w[0], v_b_out, v_ln_g, v_ln_b]
    pw, sp = _pack_rows(small_w)
    pg, _ = _pack_rows(small_g)
    pm, _ = _pack_rows(small_m)
    pv, _ = _pack_rows(small_v)
    sd, sm, sv = _adamw(pw, pg, pm, pv, "adamw_small", None)
    big_out[1] = _adamw(big_w[1], sm_done[2][0], big_m[1], big_v[1], "adamw_1", None, also_g=True)
    g_big = [g_w_ada, big_out[1][3]] + late_joined
    shapes = [a.shape for a in small_w]
    sd, sm, sv = _unpack_rows(sd, sp, shapes), _unpack_rows(sm, sp, shapes), _unpack_rows(sv, sp, shapes)

    def order(wa, bA, wi, bI, cw, wpa, wpc, wo, bO, lg, lb):
        return (wa[None], bA, wi[None], bI, cw[None], wpa[None], wpc[None], wo[None], bO, lg, lb)

    sg = [g.reshape(s) for g, s in zip(small_g, shapes)]
    grads_out = order(g_big[0], sg[0], g_big[1], sg[1], sg[2], g_big[2], g_big[3], g_big[4], sg[3], sg[4], sg[5])
    outs = []
    for idx, small in enumerate((sd, sm, sv)):
        outs.append(order(big_out[0][idx], small[0], big_out[1][idx], small[1], small[2], big_out[2][idx],
                          big_out[3][idx], big_out[4][idx], small[3], small[4], small[5]))
    return (loss, grad_x2.reshape(bsz, seq, d), *grads_out, *outs[0], *outs[1], *outs[2])
```

```python
import jax
import jax.numpy as jnp
from jax import lax
from jax.experimental import pallas as pl
from jax.experimental.pallas import tpu as pltpu

F32 = jnp.float32
BF16 = jnp.bfloat16
MESH = pl.DeviceIdType.MESH

HEAD_DIM = 64
N_GROUPS = 3
DILATIONS = (1, 4, 16)
N_HEADS = 12
SUB = 128
Q_WIDTH = 768
Z_WIDTH = 256
ATT = 3 * Q_WIDTH + Z_WIDTH
SLAB = 128
PAIR_SLABS = 10
PAIR_COLS = PAIR_SLABS * SLAB
CONV_TILE = 256
SOFTMAX_ROWS = 32
BLOCKS_PER_TRIP = 8
BLOCKS_PER_TRIP_BWD = 16
ALIBI_MAX_EXP = 8.0
ALPHA = 2.0 ** 0.25
LN_EPS = 1e-5
ADAM_LR, ADAM_B1, ADAM_B2, ADAM_EPS, ADAM_WD, ADAM_STEP = 0.001, 0.9, 0.999, 1e-08, 0.01, 10
N_CHIPS = 4
N_DEV = 8
VMEM_LIMIT_V7X = 60 * 1024 * 1024
NEG = -1e30


def _params(*sem):
    return pltpu.CompilerParams(dimension_semantics=sem, vmem_limit_bytes=VMEM_LIMIT_V7X)


def _sigmoid(v):
    return 0.5 * jnp.tanh(0.5 * v) + 0.5


class _Layout:
    def __init__(self, d):
        self.d = d
        self.din = ATT + 6 * d
        c0 = 3072
        while c0 % (2 * d):
            c0 += 1024
        self.c0, self.g0, self.np = c0, c0 + 4 * d, c0 + 6 * d
        self.n_conv_tiles = d // CONV_TILE

    def attn_nat_slab(self, s):
        p, i = s // PAIR_SLABS, s % PAIR_SLABS
        return jnp.where(i < 9, (i // 3) * 6 + (i % 3) * 2 + p, 18 + p)

    def rest_nat_tile(self, t):
        n4 = 4 * self.n_conv_tiles
        conv = ATT // CONV_TILE + (t % 4) * self.n_conv_tiles + t // 4
        return jnp.where(t < n4, conv, ATT // CONV_TILE + t)

    def perm_vector(self, v):
        parts = []
        for s in range(2 * PAIR_SLABS):
            p, i = divmod(s, PAIR_SLABS)
            ns = (i // 3) * 6 + (i % 3) * 2 + p if i < 9 else 18 + p
            parts.append(v[:, ns * SLAB:(ns + 1) * SLAB])
        parts.append(jnp.zeros((1, self.c0 - ATT), v.dtype))
        for j in range(self.n_conv_tiles):
            for k in range(4):
                a = ATT + k * self.d + j * CONV_TILE
                parts.append(v[:, a:a + CONV_TILE])
        parts.append(v[:, ATT + 4 * self.d:])
        return jnp.concatenate(parts, axis=1)


def _place():
    return lax.axis_index("x"), lax.axis_index("y"), lax.axis_index("c")


def _other_chips(x, y):
    return [(1 - x, y), (x, 1 - y), (1 - x, 1 - y)]


def _shard_of(ref, col_sharded, chip, half=None):
    if col_sharded:
        cs = ref.shape[1] // N_CHIPS
        cols = pl.ds(pl.multiple_of(chip * cs, SLAB), cs)
        if half is None:
            return ref.at[:, cols]
        n = ref.shape[0] // 2
        return ref.at[pl.ds(half * n, n), cols]
    rs = ref.shape[0] // N_CHIPS
    if half is None:
        return ref.at[pl.ds(chip * rs, rs)]
    return ref.at[pl.ds(chip * rs + half * (rs // 2), rs // 2)]


GATHER_PIECES = 8


def _cast_into_full(shard, col_sharded, chip, name):
    rows, cols = shard.shape
    tr = _row_tile(rows, cols)
    nb = rows // tr

    def body(chip_ref, s_ref, o_ref):
        del chip_ref
        o_ref[...] = s_ref[...].astype(BF16)

    if col_sharded:
        full, out_spec = (rows, cols * N_CHIPS), pl.BlockSpec((tr, cols), lambda i, ch: (i, ch[0]))
    else:
        full, out_spec = (rows * N_CHIPS, cols), pl.BlockSpec((tr, cols), lambda i, ch: (ch[0] * nb + i, 0))
    return pl.pallas_call(
        body, name=name,
        grid_spec=pltpu.PrefetchScalarGridSpec(num_scalar_prefetch=1, grid=(nb,),
                                               in_specs=[pl.BlockSpec((tr, cols), lambda i, ch: (i, 0))], out_specs=out_spec),
        out_shape=jax.ShapeDtypeStruct(full, BF16), compiler_params=_params("parallel"))(chip, shard)


def _cast_into_full_small(shards, col_sharded, chip, name):
    n = len(shards)

    def body(chip_ref, *refs):
        del chip_ref
        for w in range(n):
            refs[n + w][...] = refs[w][...].astype(BF16)

    in_specs = [pl.BlockSpec(s.shape, lambda i, ch: (0, 0)) for s in shards]
    out_specs = [pl.BlockSpec(s.shape, (lambda i, ch: (0, ch[0])) if cs else (lambda i, ch: (ch[0], 0)))
                 for s, cs in zip(shards, col_sharded)]
    fulls = [(s.shape[0], s.shape[1] * N_CHIPS) if cs else (s.shape[0] * N_CHIPS, s.shape[1]) for s, cs in zip(shards, col_sharded)]
    return pl.pallas_call(
        body, name=name,
        grid_spec=pltpu.PrefetchScalarGridSpec(num_scalar_prefetch=1, grid=(1,), in_specs=in_specs, out_specs=out_specs),
        out_shape=[jax.ShapeDtypeStruct(f, BF16) for f in fulls], compiler_params=_params("arbitrary"))(chip, *shards)


def _gather_weights(fulls, col_sharded, small):
    n = len(fulls)
    kp = GATHER_PIECES

    def piece(ref, cs, chip, half, k):
        if cs:
            width = ref.shape[1] // N_CHIPS
            rows = ref.shape[0] // 2 // kp
            return ref.at[pl.ds(half * (ref.shape[0] // 2) + k * rows, rows), pl.ds(pl.multiple_of(chip * width, SLAB), width)]
        rs = ref.shape[0] // N_CHIPS
        rows = rs // 2 // kp
        return ref.at[pl.ds(chip * rs + half * (rs // 2) + k * rows, rows)]

    def body(*refs):
        sm_in, outs, sm_out = refs[n], refs[n + 1:2 * n + 1], refs[2 * n + 1]
        send, recv, fsend, frecv, lsem, ssend, srecv = refs[2 * n + 2:]
        x, y, c = _place()
        mine = 2 * x + y
        sibling = (x, y, 1 - c)
        first = (x ^ (1 - c), y ^ c)
        second = (x ^ c, y ^ (1 - c))
        diagonal = (1 - x, 1 - y)
        sources = [first, second, diagonal]
        senders = [first, second, second]

        def copy(ref, sems, slot, to):
            return pltpu.make_async_remote_copy(src_ref=ref, dst_ref=ref, send_sem=sems[0].at[slot], recv_sem=sems[1].at[slot],
                                                device_id=to, device_id_type=MESH)

        local = pltpu.make_async_copy(sm_in, _shard_of(sm_out, True, mine), lsem)
        local.start()
        sends = []
        for k, (cx, cy) in enumerate(_other_chips(x, y)):
            cp = pltpu.make_async_remote_copy(src_ref=sm_in, dst_ref=_shard_of(sm_out, True, mine), send_sem=ssend.at[k],
                                              recv_sem=srecv.at[k], device_id=(cx, cy, c), device_id_type=MESH)
            cp.start()
            sends.append(cp)
        for k in range(kp):
            for w in range(n):
                own = piece(outs[w], col_sharded[w], mine, c, k)
                for slot, chip in enumerate((first, second)):
                    cp = copy(own, (send, recv), (w * 3 + slot) * kp + k, (*chip, c))
                    cp.start()
                    sends.append(cp)
        for slot in range(3):
            source = 2 * sources[slot][0] + sources[slot][1]
            for k in range(kp):
                for w in range(n):
                    landed = piece(outs[w], col_sharded[w], source, c, k)
                    copy(landed, (send, recv), (w * 3 + slot) * kp + k, (*senders[slot], c)).wait_recv()
                    if slot == 0:
                        cp = copy(landed, (send, recv), (w * 3 + 2) * kp + k, (*second, c))
                        cp.start()
                        sends.append(cp)
                    cp = copy(landed, (fsend, frecv), (w * 3 + slot) * kp + k, sibling)
                    cp.start()
                    sends.append(cp)
        for slot, chip in enumerate((second, first, diagonal)):
            for k in range(kp):
                for w in range(n):
                    passed = piece(outs[w], col_sharded[w], 2 * chip[0] + chip[1], 1 - c, k)
                    copy(passed, (fsend, frecv), (w * 3 + slot) * kp + k, sibling).wait_recv()
        for k, (cx, cy) in enumerate(_other_chips(x, y)):
            theirs = _shard_of(sm_out, True, 2 * cx + cy)
            pltpu.make_async_remote_copy(src_ref=theirs, dst_ref=theirs, send_sem=ssend.at[k], recv_sem=srecv.at[k],
                                         device_id=(cx, cy, c), device_id_type=MESH).wait_recv()
        for cp in sends:
            cp.wait_send()
        local.wait()

    any_spec = pl.BlockSpec(memory_space=pl.ANY)
    outs = pl.pallas_call(
        body, name="gather_weights",
        out_shape=[jax.ShapeDtypeStruct(f.shape, BF16) for f in fulls]
        + [jax.ShapeDtypeStruct((small.shape[0], small.shape[1] * N_CHIPS), small.dtype)],
        in_specs=[any_spec] * (n + 1), out_specs=[any_spec] * (n + 1), input_output_aliases={w: w for w in range(n)},
        scratch_shapes=[pltpu.SemaphoreType.DMA((n * 3 * kp,)), pltpu.SemaphoreType.DMA((n * 3 * kp,)),
                        pltpu.SemaphoreType.DMA((n * 3 * kp,)), pltpu.SemaphoreType.DMA((n * 3 * kp,)), pltpu.SemaphoreType.DMA,
                        pltpu.SemaphoreType.DMA((3,)), pltpu.SemaphoreType.DMA((3,))],
    )(*fulls, small)
    return outs[:n], outs[n]


PAIR_ID_LATE, PAIR_ID_IN = 1, 2
HBM_SPEC = pl.BlockSpec(memory_space=pltpu.HBM)
SEM_SPEC = pl.BlockSpec(memory_space=pltpu.SEMAPHORE)
DATAFLOW = pltpu.SideEffectType.DATAFLOW_SIDE_EFFECTING


def _start_copies(name, arrays, sem_shape, after, copies, pair_id=None):
    n = len(arrays)

    def body(*refs):
        if pair_id is not None:
            x, y, c = _place()
            barrier = pltpu.get_barrier_semaphore()
            pl.semaphore_signal(barrier, inc=1, device_id=(x, y, 1 - c), device_id_type=MESH)
            pl.semaphore_wait(barrier, 1)
        for cp in copies(refs[:n], refs[n + 1], refs[n + 2]):
            cp.start()
        token = refs[2 * n + 3]
        token[...] = jnp.zeros_like(token)

    res = pl.pallas_call(
        body, name=name,
        out_shape=(pltpu.SemaphoreType.DMA(sem_shape), pltpu.SemaphoreType.DMA(sem_shape),
                   *[pltpu.HBM(a.shape, a.dtype) for a in arrays], jax.ShapeDtypeStruct((8, 128), F32)),
        in_specs=[HBM_SPEC] * n + [pl.BlockSpec(memory_space=pl.ANY)],
        out_specs=(SEM_SPEC, SEM_SPEC, *([HBM_SPEC] * n), pl.BlockSpec(memory_space=pltpu.VMEM)),
        input_output_aliases={i: 2 + i for i in range(n)},
        compiler_params=pltpu.CompilerParams(has_side_effects=DATAFLOW, collective_id=pair_id),
    )(*[pltpu.with_memory_space_constraint(a, pltpu.HBM) for a in arrays], after)
    return res[0], res[1], list(res[2:2 + n]), res[2 + n]


def _wait_copies(name, arrays, send, recv, after, copies):
    n = len(arrays)

    def body(*refs):
        for cp in copies(refs[:n], refs[n], refs[n + 1]):
            cp.wait_send()
            cp.wait_recv()

    return pl.pallas_call(
        body, name=name, out_shape=[pltpu.HBM(a.shape, a.dtype) for a in arrays],
        in_specs=[HBM_SPEC] * n + [SEM_SPEC, SEM_SPEC, pl.BlockSpec(memory_space=pl.ANY)], out_specs=[HBM_SPEC] * n,
        input_output_aliases={i: i for i in range(n)},
        compiler_params=pltpu.CompilerParams(has_side_effects=DATAFLOW),
    )(*arrays, send, recv, after)


def _direct_gather_copies(col_sharded):
    def copies(refs, send, recv):
        x, y, c = _place()
        mine = 2 * x + y
        out = []
        for w, ref in enumerate(refs):
            own_half = _shard_of(ref, col_sharded[w], mine, c)
            k = 0
            for cx, cy in _other_chips(x, y):
                for pc in (c, 1 - c):
                    out.append(pltpu.make_async_remote_copy(
                        src_ref=own_half, dst_ref=own_half, send_sem=send.at[6 * w + k], recv_sem=recv.at[6 * w + k],
                        device_id=(cx, cy, pc), device_id_type=MESH))
                    k += 1
        return out
    return copies


def _chip_scatter_copies(n, col_sharded):
    def piece(ref, cs, chip):
        if cs:
            w = ref.shape[2] // N_CHIPS
            return ref.at[:, :, pl.ds(pl.multiple_of(chip * w, SLAB), w)]
        return ref.at[pl.ds(chip, 1)]

    def copies(refs, send, recv):
        x, y, c = _place()
        out = []
        for k, (cx, cy) in enumerate(_other_chips(x, y)):
            for w in range(n):
                out.append(pltpu.make_async_remote_copy(
                    src_ref=piece(refs[w], col_sharded[w], 2 * cx + cy), dst_ref=refs[n + w].at[pl.ds(k, 1)],
                    send_sem=send.at[3 * w + k], recv_sem=recv.at[3 * w + k], device_id=(cx, cy, c), device_id_type=MESH))
        return out
    return copies


def _shard_views(gs, col_sharded):
    return [g.reshape(1, *g.shape) if cs else g.reshape(N_CHIPS, g.shape[0] // N_CHIPS, g.shape[1])
            for g, cs in zip(gs, col_sharded)]


DMA_CHUNK_BYTES = 1 << 20


def _chunk_rows(shape, itemsize):
    s, rows, cols = shape
    n = 1
    while s * (rows // n) * cols * itemsize > DMA_CHUNK_BYTES and (rows // n) % 32 == 0:
        n *= 2
    return rows // n


def _rows_of(ref, row0, rows, cols):
    if cols is None:
        return ref.at[:, pl.ds(row0, rows)]
    return ref.at[:, pl.ds(row0, rows), pl.ds(cols[0], cols[1])]


def _row_pieces(src, src_row0, dst, dst_row0, rows, send_sem, recv_sem, device, cols=None):
    width = src.shape[2] if cols is None else cols[1]
    step = _chunk_rows((src.shape[0], rows, width), src.dtype.itemsize)
    return [pltpu.make_async_remote_copy(src_ref=_rows_of(src, src_row0 + r, step, cols), dst_ref=_rows_of(dst, dst_row0 + r, step, cols),
                                         send_sem=send_sem, recv_sem=recv_sem, device_id=device, device_id_type=MESH)
            for r in range(0, rows, step)]


def _pair_exchange_copies(n, whole, cols=None):
    def copies(refs, send, recv):
        x, y, c = _place()
        sibling = (x, y, 1 - c)
        out = []
        for w in range(n):
            hr = refs[n + w].shape[1]
            if whole:
                out.append(pltpu.make_async_remote_copy(
                    src_ref=_rows_of(refs[w], (1 - c) * hr, hr, cols), dst_ref=_rows_of(refs[n + w], 0, hr, cols),
                    send_sem=send.at[w], recv_sem=recv.at[w], device_id=sibling, device_id_type=MESH))
            else:
                out += _row_pieces(refs[w], (1 - c) * hr, refs[n + w], 0, hr, send.at[w], recv.at[w], sibling, cols)
        return out
    return copies


def _pair_join_copies(n, whole, sem0=0):
    def copies(refs, send, recv):
        x, y, c = _place()
        sibling = (x, y, 1 - c)
        out = []
        for w in range(n):
            hr = refs[w].shape[1] // 2
            sems = dict(send_sem=send.at[sem0 + w], recv_sem=recv.at[sem0 + w])
            if whole:
                out.append(pltpu.make_async_remote_copy(
                    src_ref=refs[w].at[:, pl.ds(c * hr, hr)], dst_ref=refs[w].at[:, pl.ds((1 - c) * hr, hr)],
                    device_id=sibling, device_id_type=MESH, **sems))
            else:
                out += _row_pieces(refs[w], c * hr, refs[w], c * hr, hr, sems["send_sem"], sems["recv_sem"], sibling)
        return out
    return copies


def _both_copies(first, n_first, second):
    def copies(refs, send, recv):
        return first(refs[:n_first], send, recv) + second(refs[n_first:], send, recv)
    return copies


def _small_gather_copies(refs, send, recv):
    vec, land = refs
    x, y, c = _place()
    me = 4 * x + 2 * y + c
    return [pltpu.make_async_remote_copy(src_ref=vec, dst_ref=land.at[me], send_sem=send.at[k], recv_sem=recv.at[k],
                                         device_id=peer, device_id_type=MESH) for k, (peer, _) in enumerate(_all_devices(x, y, c))]


def _small_sum(vec, land, me, n_sum, d):
    rows = vec.shape[0]

    def body(me_ref, v_ref, l_ref, sum_ref, kept_ref):
        def slot(k):
            return jnp.where(me_ref[0] == k, v_ref[...], l_ref[k])

        total = slot(0)[0:n_sum, :]
        kept_ref[0] = slot(0)[n_sum:rows, :]
        for k in range(1, N_DEV):
            total = total + slot(k)[0:n_sum, :]
            kept_ref[k] = slot(k)[n_sum:rows, :]
        sum_ref[...] = total
        loss = 0.5 / d * jnp.sum(total[0:8, :])
        sum_ref[0:8, :] = jnp.full((8, 128), loss, F32)

    vm = pl.BlockSpec(memory_space=pltpu.VMEM)
    return pl.pallas_call(
        body, name="small_sum", in_specs=[pl.BlockSpec(memory_space=pltpu.SMEM), vm, vm], out_specs=[vm, vm],
        out_shape=[jax.ShapeDtypeStruct((n_sum, 128), F32), jax.ShapeDtypeStruct((N_DEV, rows - n_sum, 128), F32)],
        compiler_params=pltpu.CompilerParams(vmem_limit_bytes=VMEM_LIMIT_V7X))(me, vec, land)


def _pair_exchange_into(view, land, cols, name):
    def body(v_ref, l_in, l_ref, send, recv):
        del l_in
        x, y, c = _place()
        sibling = (x, y, 1 - c)
        hr = l_ref.shape[1]
        for cp in _row_pieces(v_ref, (1 - c) * hr, l_ref, 0, hr, send, recv, sibling, cols):
            cp.start()
        pltpu.make_async_remote_copy(src_ref=_rows_of(v_ref, (1 - c) * hr, hr, cols), dst_ref=_rows_of(l_ref, 0, hr, cols),
                                     send_sem=send, recv_sem=recv, device_id=sibling, device_id_type=MESH).wait()

    any_spec = pl.BlockSpec(memory_space=pl.ANY)
    return pl.pallas_call(
        body, name=name, out_shape=jax.ShapeDtypeStruct(land.shape, land.dtype),
        in_specs=[any_spec, any_spec], out_specs=any_spec, input_output_aliases={1: 0},
        scratch_shapes=[pltpu.SemaphoreType.DMA, pltpu.SemaphoreType.DMA],
    )(view, land)


def _pair_sum(view, got, core, name):
    s, r, cols = view.shape
    hr = r // 2
    tr = _row_tile(hr, cols, itemsize=view.dtype.itemsize, budget=4 << 20)
    nb = hr // tr

    def body(core_ref, a_ref, b_ref, o_ref):
        del core_ref
        o_ref[...] = (a_ref[...].astype(F32) + b_ref[...].astype(F32)).astype(BF16)

    same = pl.BlockSpec((None, tr, cols), lambda j, i, core_ref: (j, i, 0))
    return pl.pallas_call(
        body, name=name,
        grid_spec=pltpu.PrefetchScalarGridSpec(
            num_scalar_prefetch=1, grid=(s, nb),
            in_specs=[pl.BlockSpec((None, tr, cols), lambda j, i, core_ref: (j, core_ref[0] * nb + i, 0)), same],
            out_specs=same),
        out_shape=jax.ShapeDtypeStruct((s, hr, cols), BF16), compiler_params=_params("parallel", "parallel"))(core, view, got)


def _pair_sum_small(views, gots, core, name):
    n = len(views)

    def body(core_ref, *refs):
        del core_ref
        for w in range(n):
            refs[2 * n + w][...] = (refs[w][...].astype(F32) + refs[n + w][...].astype(F32)).astype(BF16)

    halves = [(v.shape[0], v.shape[1] // 2, v.shape[2]) for v in views]
    own = [pl.BlockSpec(h, lambda i, core_ref: (0, core_ref[0], 0)) for h in halves]
    whole = [pl.BlockSpec(h, lambda i, core_ref: (0, 0, 0)) for h in halves]
    return pl.pallas_call(
        body, name=name,
        grid_spec=pltpu.PrefetchScalarGridSpec(num_scalar_prefetch=1, grid=(1,), in_specs=own + whole, out_specs=whole),
        out_shape=[jax.ShapeDtypeStruct(h, BF16) for h in halves], compiler_params=_params("arbitrary"))(core, *views, *gots)


def _piece_cols(part, col_sharded):
    return part.shape[2] // N_CHIPS if col_sharded else part.shape[2]


def _chip_sum_small(parts, gots, col_sharded, place, name):
    n = len(parts)

    def body(place_ref, *refs):
        del place_ref
        for w in range(n):
            got = refs[n + w]
            acc = refs[w][...].astype(F32) + got[0].astype(F32)
            refs[2 * n + w][...] = acc + got[1].astype(F32) + got[2].astype(F32)

    own, others, outs, shapes = [], [], [], []
    for p, cs in zip(parts, col_sharded):
        hr, cols = p.shape[1], _piece_cols(p, cs)
        own.append(pl.BlockSpec((None, hr, cols), (lambda i, pr: (0, 0, pr[0])) if cs else (lambda i, pr: (pr[0], 0, 0))))
        others.append(pl.BlockSpec((3, hr, cols), lambda i, pr: (0, 0, 0)))
        outs.append(pl.BlockSpec((hr, cols), lambda i, pr: (pr[1], 0)))
        shapes.append(jax.ShapeDtypeStruct((2 * hr, cols), F32))
    return pl.pallas_call(
        body, name=name,
        grid_spec=pltpu.PrefetchScalarGridSpec(num_scalar_prefetch=1, grid=(1,), in_specs=own + others, out_specs=outs),
        out_shape=shapes, compiler_params=_params("arbitrary"))(place, *parts, *gots)


def _chip_sum(part, got, col_sharded, place, name):
    _, hr, _ = part.shape
    cols = _piece_cols(part, col_sharded)
    tr = _row_tile(hr, cols)
    nb = hr // tr

    def body(place_ref, own_ref, g0_ref, g1_ref, g2_ref, o_ref):
        del place_ref
        acc = own_ref[...].astype(F32) + g0_ref[...].astype(F32)
        o_ref[...] = acc + g1_ref[...].astype(F32) + g2_ref[...].astype(F32)

    if col_sharded:
        own = pl.BlockSpec((None, tr, cols), lambda i, pr: (0, i, pr[0]))
    else:
        own = pl.BlockSpec((None, tr, cols), lambda i, pr: (pr[0], i, 0))
    others = [pl.BlockSpec((None, tr, cols), lambda i, pr, k=k: (k, i, 0)) for k in range(3)]
    return pl.pallas_call(
        body, name=name,
        grid_spec=pltpu.PrefetchScalarGridSpec(
            num_scalar_prefetch=1, grid=(nb,), in_specs=[own] + others,
            out_specs=pl.BlockSpec((tr, cols), lambda i, pr: (pr[1] * nb + i, 0))),
        out_shape=jax.ShapeDtypeStruct((2 * hr, cols), F32), compiler_params=_params("parallel"))(place, part, got, got, got)


def _row_tile(rows, cols, itemsize=4, budget=2 << 20):
    t = rows
    while t * cols * itemsize > budget and t % 16 == 0:
        t //= 2
    return t


def _adamw(w, g, m, v, name, after, also_g=False):
    rows, cols = w.shape
    tr = _row_tile(rows, cols, budget=2 << 20)
    extra = [] if after is None else [after]
    n_out = 4 if also_g else 3

    def body(w_ref, g_ref, m_ref, v_ref, *rest):
        d_ref, nm_ref, nv_ref = rest[len(extra):len(extra) + 3]
        g_ = g_ref[...]
        if also_g:
            rest[len(extra) + 3][...] = g_
        nm = ADAM_B1 * m_ref[...] + (1.0 - ADAM_B1) * g_
        nv = ADAM_B2 * v_ref[...] + (1.0 - ADAM_B2) * (g_ * g_)
        m_hat = nm / (1.0 - ADAM_B1 ** ADAM_STEP)
        v_hat = nv / (1.0 - ADAM_B2 ** ADAM_STEP)
        d_ref[...] = -ADAM_LR * (m_hat / (jnp.sqrt(v_hat) + ADAM_EPS) + ADAM_WD * w_ref[...])
        nm_ref[...] = nm
        nv_ref[...] = nv

    spec = pl.BlockSpec((tr, cols), lambda i: (i, 0))
    shp = jax.ShapeDtypeStruct((rows, cols), F32)
    return pl.pallas_call(body, name=name, grid=(rows // tr,),
                          in_specs=[spec] * 4 + [pl.BlockSpec(memory_space=pl.ANY)] * len(extra), out_specs=[spec] * n_out,
                          out_shape=[shp] * n_out, compiler_params=_params("parallel"))(w, g, m, v, *extra)


def _adamw_together(ws, gs, ms, vs, name, after):
    n = len(ws)

    def body(*refs):
        outs = refs[4 * n + 1:]
        for k in range(n):
            g_ = refs[n + k][...]
            nm = ADAM_B1 * refs[2 * n + k][...] + (1.0 - ADAM_B1) * g_
            nv = ADAM_B2 * refs[3 * n + k][...] + (1.0 - ADAM_B2) * (g_ * g_)
            m_hat = nm / (1.0 - ADAM_B1 ** ADAM_STEP)
            v_hat = nv / (1.0 - ADAM_B2 ** ADAM_STEP)
            outs[3 * k][...] = -ADAM_LR * (m_hat / (jnp.sqrt(v_hat) + ADAM_EPS) + ADAM_WD * refs[k][...])
            outs[3 * k + 1][...] = nm
            outs[3 * k + 2][...] = nv

    specs = [pl.BlockSpec(w.shape, lambda i: (0, 0)) for w in ws]
    out_specs, out_shape = [], []
    for w in ws:
        out_specs += [pl.BlockSpec(w.shape, lambda i: (0, 0))] * 3
        out_shape += [jax.ShapeDtypeStruct(w.shape, F32)] * 3
    res = pl.pallas_call(body, name=name, grid=(1,), in_specs=specs * 4 + [pl.BlockSpec(memory_space=pl.ANY)],
                         out_specs=out_specs, out_shape=out_shape, compiler_params=_params("arbitrary"))(*ws, *gs, *ms, *vs, after)
    return [tuple(res[3 * k:3 * k + 3]) for k in range(n)]


def _all_devices(x, y, c):
    out = []
    for k in range(1, N_DEV):
        peer = (x ^ ((k >> 2) & 1), y ^ ((k >> 1) & 1), c ^ (k & 1))
        out.append((peer, 4 * peer[0] + 2 * peer[1] + peer[2]))
    return out


def _ada_exchange(c, w_shard, b_ada):
    bsz, d = c.shape
    cs = w_shard.shape[1]

    def body(c_ref, w_ref, b_ref, mod_ref, act_ref, c_all, part, pieces, csend, crecv, psend, precv):
        x, y, core = _place()
        me = 4 * x + 2 * y + core
        chip = 2 * x + y
        c_all[me] = c_ref[...]
        peers = _all_devices(x, y, core)
        copies = []
        for k, (peer, _) in enumerate(peers):
            cp = pltpu.make_async_remote_copy(src_ref=c_ref, dst_ref=c_all.at[me], send_sem=csend.at[k], recv_sem=crecv.at[k],
                                              device_id=peer, device_id_type=MESH)
            cp.start()
            copies.append(cp)
        for k, (_, src) in enumerate(peers):
            pltpu.make_async_remote_copy(src_ref=c_ref, dst_ref=c_all.at[src], send_sem=csend.at[k], recv_sem=crecv.at[k],
                                         device_id=(x, y, core), device_id_type=MESH).wait_recv()
        rows = jnp.concatenate([c_all[i] for i in range(N_DEV)], axis=0)
        act = rows * _sigmoid(rows)
        act_ref[...] = act
        prod = jnp.dot(act.astype(BF16), w_ref[...].astype(BF16), preferred_element_type=F32)
        for i in range(N_DEV):
            part[i] = prod[i * bsz:(i + 1) * bsz, :]
        pieces[chip] = part[me]
        chips = _other_chips(x, y)
        for k, (cx, cy) in enumerate(chips):
            cp = pltpu.make_async_remote_copy(src_ref=part.at[4 * cx + 2 * cy + core], dst_ref=pieces.at[chip],
                                              send_sem=psend.at[k], recv_sem=precv.at[k], device_id=(cx, cy, core),
                                              device_id_type=MESH)
            cp.start()
            copies.append(cp)
        for k, (cx, cy) in enumerate(chips):
            pltpu.make_async_remote_copy(src_ref=part.at[me], dst_ref=pieces.at[2 * cx + cy], send_sem=psend.at[k],
                                         recv_sem=precv.at[k], device_id=(cx, cy, core), device_id_type=MESH).wait_recv()
        for cp in copies:
            cp.wait_send()
        mod_ref[...] = jnp.concatenate([pieces[j] for j in range(N_CHIPS)], axis=1) + b_ref[...]

    vm = pl.BlockSpec(memory_space=pltpu.VMEM)
    return pl.pallas_call(
        body, name="ada_exchange", in_specs=[vm] * 3, out_specs=[vm] * 2,
        out_shape=[jax.ShapeDtypeStruct((bsz, 3 * d), F32), jax.ShapeDtypeStruct((N_DEV * bsz, d), F32)],
        scratch_shapes=[pltpu.VMEM((N_DEV, bsz, d), F32), pltpu.VMEM((N_DEV, bsz, cs), F32), pltpu.VMEM((N_CHIPS, bsz, cs), F32),
                        pltpu.SemaphoreType.DMA((N_DEV - 1,)), pltpu.SemaphoreType.DMA((N_DEV - 1,)),
                        pltpu.SemaphoreType.DMA((3,)), pltpu.SemaphoreType.DMA((3,))],
        compiler_params=pltpu.CompilerParams(vmem_limit_bytes=VMEM_LIMIT_V7X))(c, w_shard, b_ada)


def _grad_and_adamw_w_ada(act_t, d_cols, w, m, v):
    d, n = act_t.shape
    cs = d_cols.shape[1]
    tn = min(256, cs)

    def body(a_ref, g_ref, w_ref, m_ref, v_ref, go_ref, d_ref, nm_ref, nv_ref):
        a, g = a_ref[...], g_ref[...]
        acc = a[:, 0:1] * g[0:1, :]
        for b in range(1, n):
            acc = acc + a[:, b:b + 1] * g[b:b + 1, :]
        go_ref[...] = acc
        nm = ADAM_B1 * m_ref[...] + (1.0 - ADAM_B1) * acc
        nv = ADAM_B2 * v_ref[...] + (1.0 - ADAM_B2) * (acc * acc)
        m_hat = nm / (1.0 - ADAM_B1 ** ADAM_STEP)
        v_hat = nv / (1.0 - ADAM_B2 ** ADAM_STEP)
        d_ref[...] = -ADAM_LR * (m_hat / (jnp.sqrt(v_hat) + ADAM_EPS) + ADAM_WD * w_ref[...])
        nm_ref[...] = nm
        nv_ref[...] = nv

    tile = pl.BlockSpec((d, tn), lambda j: (0, j))
    shp = jax.ShapeDtypeStruct((d, cs), F32)
    return pl.pallas_call(body, name="adamw_w_ada", grid=(cs // tn,),
                          in_specs=[pl.BlockSpec((d, n), lambda j: (0, 0)), pl.BlockSpec((n, tn), lambda j: (0, j)), tile, tile, tile],
                          out_specs=[tile] * 4, out_shape=[shp] * 4, compiler_params=_params("parallel"))(act_t, d_cols, w, m, v)


def _permute_w_in(w_nat, lay):
    d = lay.d
    group = 4

    def call(name, width, n_pieces, nat_piece, out_block0, prev):
        def body(*refs):
            refs[-1][...] = jnp.concatenate([r[...] for r in refs[:group]], axis=1)

        in_specs = [pl.BlockSpec((d, width), lambda s, m=m: (0, nat_piece(group * s + m))) for m in range(group)]
        args = [w_nat] * group
        aliases = {}
        if prev is not None:
            in_specs.append(pl.BlockSpec(memory_space=pl.ANY))
            args.append(prev)
            aliases = {group: 0}
        return pl.pallas_call(
            body, name=name, grid=(n_pieces // group,), in_specs=in_specs,
            out_specs=pl.BlockSpec((d, group * width), lambda s: (0, out_block0 + s)),
            out_shape=jax.ShapeDtypeStruct((d, lay.np), BF16), input_output_aliases=aliases,
            compiler_params=_params("arbitrary"))(*args)

    w_all = call("permute_w_attn", SLAB, 2 * PAIR_SLABS, lay.attn_nat_slab, 0, None)
    n_rest = 6 * d // CONV_TILE
    if n_rest % group:
        group = 2
    return call("permute_w_rest", CONV_TILE, n_rest, lay.rest_nat_tile, lay.c0 // (group * CONV_TILE), w_all)


def _project(x2, mod3, w_all, b_all, seq, col0, ncols, tn, out_dtype, want_ht, name):
    t, d = x2.shape
    tm = min(2048, seq)
    per_seq = seq // tm
    j0 = col0 // tn

    def body(x_ref, mod_ref, w_ref, b_ref, o_ref, *rest):
        h_ref = rest[-1]

        @pl.when(pl.program_id(1) == 0)
        def _():
            h = x_ref[...] * (1.0 + mod_ref[:, d:2 * d]) + mod_ref[:, 0:d]
            h_ref[...] = h.astype(BF16)
            if want_ht:
                rest[0][...] = h.T.astype(BF16)

        o_ref[...] = (jnp.dot(h_ref[...], w_ref[...], preferred_element_type=F32) + b_ref[...]).astype(out_dtype)

    out_shape = [jax.ShapeDtypeStruct((t, ncols), out_dtype)]
    out_specs = [pl.BlockSpec((tm, tn), lambda i, j: (i, j))]
    if want_ht:
        out_shape.append(jax.ShapeDtypeStruct((d, t), BF16))
        out_specs.append(pl.BlockSpec((d, tm), lambda i, j: (0, i)))
    return pl.pallas_call(
        body, name=name, grid=(t // tm, ncols // tn),
        in_specs=[pl.BlockSpec((tm, d), lambda i, j: (i, 0)),
                  pl.BlockSpec((None, 1, 3 * d), lambda i, j: (i // per_seq, 0, 0)),
                  pl.BlockSpec((d, tn), lambda i, j: (0, j0 + j)),
                  pl.BlockSpec((1, tn), lambda i, j: (0, j0 + j))],
        out_specs=out_specs, out_shape=out_shape,
        scratch_shapes=[pltpu.VMEM((tm, d), BF16)],
        compiler_params=_params("arbitrary", "arbitrary"))(x2, mod3, w_all, b_all)


def _slope(g, p, hh):
    head = 4 * g + 2 * p + hh
    return 2.0 ** (-ALIBI_MAX_EXP * (head + 1.0) / N_HEADS)


def _ld_rows(ref, start, n, stride):
    if stride == 1:
        return ref[pl.ds(start, n), :]
    return ref[pl.ds(start, n, stride=stride), :]


def _st_rows(ref, start, n, stride, val):
    if stride == 1:
        ref[pl.ds(start, n), :] = val
    else:
        ref[pl.ds(start, n, stride=stride), :] = val


def _sub_blocks(g, seq):
    return seq // DILATIONS[g] // SUB


def _key_rows(g, seq):
    return SUB if _sub_blocks(g, seq) == 1 else 2 * SUB


def _fill_bias(bias_ref, p, seq):
    for g in range(N_GROUPS):
        nk = _key_rows(g, seq)
        diff = lax.broadcasted_iota(jnp.int32, (SUB, nk), 0) - lax.broadcasted_iota(jnp.int32, (SUB, nk), 1)
        for i, off in enumerate((0, SUB)):
            if i == 1 and nk == SUB:
                continue
            delta = diff + off
            ok = (delta >= 0) & (delta <= SUB)
            dist = (delta * DILATIONS[g]).astype(F32)
            for hh in range(2):
                slope = jnp.where(p == 0, _slope(g, 0, hh), _slope(g, 1, hh))
                bias_ref[g, i, hh, :, 0:nk] = jnp.where(ok, -slope * dist, NEG)


def _to_sub_major(pa_ref, col, sub_ref, stage, dil, seq):
    cols = slice(col * SLAB, (col + 1) * SLAB)
    if dil == 1:
        sub_ref[...] = pa_ref[:, cols]
        return
    n = seq // dil
    stage[...] = pa_ref[:, cols].astype(F32)
    for r in range(dil):
        sub_ref[pl.ds(r * n, n), :] = stage[pl.ds(r, n, stride=dil), :].astype(BF16)


def _block_rows(it, g, seq):
    dil, nb = DILATIONS[g], _sub_blocks(g, seq)
    row0 = pl.multiple_of(it * SUB, SUB)
    if nb == 1:
        return row0, row0, 0, it
    blk = it % nb
    first = blk == 0
    krow0 = pl.multiple_of(row0 - jnp.where(first, 0, SUB), SUB)
    nat = row0 if dil == 1 else it // nb + dil * SUB * blk
    return row0, krow0, jnp.where(first, 0, 1), nat


def _nt(a, b):
    return lax.dot_general(a, b, (((1,), (1,)), ((), ())), preferred_element_type=F32)


def _tn(a, b):
    return lax.dot_general(a, b, (((0,), (0,)), ((), ())), preferred_element_type=F32)


def _head_sums(t):
    rows = t.shape[0]
    lo = jnp.broadcast_to(jnp.sum(t[:, :HEAD_DIM], axis=-1, keepdims=True), (rows, HEAD_DIM))
    hi = jnp.broadcast_to(jnp.sum(t[:, HEAD_DIM:], axis=-1, keepdims=True), (rows, HEAD_DIM))
    return jnp.concatenate([lo, hi], axis=-1)


def _attn_fwd(pa, bsz, seq):
    t = pa.shape[0]
    n_blocks = seq // SUB
    chunk = 256

    def body(pa_ref, o_ref, lse_ref, a_ref, sub, stage, bias_ref, s_buf, p_buf, l_buf):
        p = pl.program_id(1)
        _fill_bias(bias_ref, p, seq)
        head0 = lax.broadcasted_iota(jnp.int32, (SUB, SLAB), 1) < HEAD_DIM
        for g in range(N_GROUPS):
            dil = DILATIONS[g]
            for w in range(3):
                _to_sub_major(pa_ref, 3 * w + g, sub.at[w], stage, dil, seq)
            nk = _key_rows(g, seq)

            def trip(i, carry, g=g, dil=dil, nk=nk):
                places = [_block_rows(BLOCKS_PER_TRIP * i + j, g, seq) for j in range(BLOCKS_PER_TRIP)]
                for j, (row0, krow0, _, _) in enumerate(places):
                    q = sub[0, pl.ds(row0, SUB), :]
                    zero = jnp.zeros_like(q)
                    q2 = jnp.concatenate([jnp.where(head0, q, zero), jnp.where(head0, zero, q)], axis=0) * (HEAD_DIM ** -0.5)
                    s_buf[j, :, 0:nk] = _nt(q2, sub[1, pl.ds(krow0, nk), :])
                for j, (_, _, bi, _) in enumerate(places):
                    for c in range(0, 2 * SUB, SOFTMAX_ROWS):
                        hh, r = divmod(c, SUB)
                        s = s_buf[j, c:c + SOFTMAX_ROWS, 0:nk] + bias_ref[g, bi, hh, r:r + SOFTMAX_ROWS, 0:nk]
                        m = jnp.max(s, axis=-1, keepdims=True)
                        e = jnp.exp(s - m)
                        den = jnp.sum(e, axis=-1, keepdims=True)
                        p_buf[j, c:c + SOFTMAX_ROWS, 0:nk] = (e * (1.0 / den)).astype(BF16)
                        l_buf[j, c:c + SOFTMAX_ROWS, :] = jnp.broadcast_to(m + jnp.log(den), (SOFTMAX_ROWS, SLAB))
                for j, (_, krow0, _, nat) in enumerate(places):
                    o2 = jnp.dot(p_buf[j, :, 0:nk], sub[2, pl.ds(krow0, nk), :], preferred_element_type=F32)
                    _st_rows(o_ref.at[g], nat, SUB, dil, jnp.where(head0, o2[0:SUB], o2[SUB:2 * SUB]))
                    _st_rows(lse_ref.at[g], nat, SUB, dil, jnp.where(head0, l_buf[j, 0:SUB, :], l_buf[j, SUB:2 * SUB, :]))
                return carry

            lax.fori_loop(0, n_blocks // BLOCKS_PER_TRIP, trip, 0)

        def mix(i, carry):
            rows = pl.ds(pl.multiple_of(i * chunk, chunk), chunk)
            l0, l1, l2 = lse_ref[0, rows, :], lse_ref[1, rows, :], lse_ref[2, rows, :]
            m = jnp.maximum(jnp.maximum(l0, l1), l2)
            e0, e1, e2 = jnp.exp(l0 - m), jnp.exp(l1 - m), jnp.exp(l2 - m)
            tot = e0 + e1 + e2
            o = (e0 / tot) * o_ref[0, rows, :] + (e1 / tot) * o_ref[1, rows, :] + (e2 / tot) * o_ref[2, rows, :]
            z = pa_ref[rows, 9 * SLAB:10 * SLAB].astype(F32)
            a_ref[rows, :] = (o * (z * _sigmoid(z))).astype(BF16)
            return carry

        lax.fori_loop(0, seq // chunk, mix, 0)

    big = jax.ShapeDtypeStruct((N_GROUPS, t, 2 * SLAB), F32)
    return pl.pallas_call(
        body, name="attn_fwd", grid=(bsz, 2),
        in_specs=[pl.BlockSpec((seq, PAIR_COLS), lambda b, p: (b, p))],
        out_specs=[pl.BlockSpec((N_GROUPS, seq, SLAB), lambda b, p: (0, b, p)),
                   pl.BlockSpec((N_GROUPS, seq, SLAB), lambda b, p: (0, b, p)),
                   pl.BlockSpec((seq, SLAB), lambda b, p: (b, p))],
        out_shape=[big, big, jax.ShapeDtypeStruct((t, 2 * SLAB), BF16)],
        scratch_shapes=[pltpu.VMEM((3, seq, SLAB), BF16), pltpu.VMEM((seq, SLAB), F32),
                        pltpu.VMEM((N_GROUPS, 2, 2, SUB, 2 * SUB), F32), pltpu.VMEM((BLOCKS_PER_TRIP, 2 * SUB, 2 * SUB), F32),
                        pltpu.VMEM((BLOCKS_PER_TRIP, 2 * SUB, 2 * SUB), BF16), pltpu.VMEM((BLOCKS_PER_TRIP, 2 * SUB, SLAB), F32)],
        compiler_params=_params("arbitrary", "arbitrary"))(pa)


def _shift_down(v, k, rows):
    return jnp.where(rows >= k, pltpu.roll(v, k, 0), 0.0)


def _shift_up(v, k, rows):
    n = v.shape[0]
    return jnp.where(rows < n - k, pltpu.roll(v, n - k, 0), 0.0)


def _conv_fwd(pr, conv_w, bsz, seq, d):
    t = pr.shape[0]
    ct = CONV_TILE

    def body(p_ref, cw_ref, o_ref):
        u = p_ref[:, 2 * ct:3 * ct].astype(F32) * p_ref[:, 0:ct].astype(F32)
        cw = cw_ref[...]
        rows = lax.broadcasted_iota(jnp.int32, u.shape, 0)
        conv = cw[0:1, :] * _shift_down(u, 2, rows)
        conv = conv + cw[1:2, :] * _shift_down(u, 1, rows)
        conv = conv + cw[2:3, :] * u
        z = p_ref[:, 3 * ct:4 * ct].astype(F32)
        o_ref[...] = (p_ref[:, ct:2 * ct].astype(F32) * conv * (z * _sigmoid(z))).astype(BF16)

    return pl.pallas_call(
        body, name="conv_fwd", grid=(bsz, d // ct),
        in_specs=[pl.BlockSpec((seq, 4 * ct), lambda b, j: (b, j)), pl.BlockSpec((3, ct), lambda b, j: (0, j))],
        out_specs=pl.BlockSpec((seq, ct), lambda b, j: (b, j)),
        out_shape=jax.ShapeDtypeStruct((t, d), BF16), compiler_params=_params("parallel", "parallel"))(pr, conv_w)


def _tail(a_in, b_in, pr, x2, target2, mod3, w_pa, w_pc, w_out, b_out, ln_g, ln_b, seq, lay):
    t, d = x2.shape
    tm = 512
    per_seq = seq // tm
    n_steps = t // tm
    gate_blk = 4 * d // d

    def nt(a, b):
        return lax.dot_general(a, b, (((1,), (1,)), ((), ())), preferred_element_type=F32)

    def tn(a, b):
        return lax.dot_general(a, b, (((0,), (0,)), ((), ())), preferred_element_type=F32)

    def body(a_ref, b_ref, ga_ref, gb_ref, x_ref, tg_ref, mod_ref, wpa_ref, wpc_ref, wo_ref, bo_ref, lg_ref, lb_ref,
             dpg_ref, da_ref, db_ref, gx_ref, dgate_ref, small_ref, gwpa_hbm, gwpc_hbm, gwo_hbm,
             acc_pa, acc_pc, acc_o, sem):
        i = pl.program_id(0)

        @pl.when(i == 0)
        def _():
            acc_pa[...] = jnp.zeros_like(acc_pa)
            acc_pc[...] = jnp.zeros_like(acc_pc)
            acc_o[...] = jnp.zeros_like(acc_o)
            small_ref[...] = jnp.zeros_like(small_ref)

        @pl.when(i % per_seq == 0)
        def _():
            dgate_ref[...] = jnp.zeros_like(dgate_ref)

        halves = [slice(k * (tm // 2), (k + 1) * (tm // 2)) for k in range(2)]
        gate = mod_ref[:, 2 * d:3 * d]
        a_bf = [a_ref[rs, :] for rs in halves]
        b_bf = [b_ref[rs, :] for rs in halves]
        y_attn = [jnp.dot(a, wpa_ref[...], preferred_element_type=F32) for a in a_bf]
        y_conv = [jnp.dot(b, wpc_ref[...], preferred_element_type=F32) for b in b_bf]
        sa = [_sigmoid(ga_ref[rs, :].astype(F32)) for rs in halves]
        sb = [_sigmoid(gb_ref[rs, :].astype(F32)) for rs in halves]
        merged = [(sa[k] * y_attn[k] + sb[k] * y_conv[k]).astype(BF16) for k in range(2)]
        mo = [jnp.dot(m, wo_ref[...], preferred_element_type=F32) + bo_ref[...] for m in merged]
        d_mo_bf = []
        for k, rs in enumerate(halves):
            r = ALPHA * x_ref[rs, :] + gate * mo[k]
            mu = jnp.mean(r, axis=-1, keepdims=True)
            cen = r - mu
            var = jnp.mean(cen * cen, axis=-1, keepdims=True)
            rstd = lax.rsqrt(var + LN_EPS)
            xhat = cen * rstd
            err = xhat * lg_ref[...] + lb_ref[...] - tg_ref[rs, :]
            dy = err * (1.0 / d)
            dxhat = dy * lg_ref[...]
            dr = rstd * (dxhat - jnp.mean(dxhat, axis=-1, keepdims=True)
                         - xhat * jnp.mean(dxhat * xhat, axis=-1, keepdims=True))
            gx_ref[rs, :] = ALPHA * dr
            dgate_ref[...] += jnp.sum(dr * mo[k], axis=0, keepdims=True)
            d_mo = dr * gate
            small_ref[0:1, :] += jnp.sum(d_mo, axis=0, keepdims=True)
            small_ref[1:2, :] += jnp.sum(dy * xhat, axis=0, keepdims=True)
            small_ref[2:3, :] += jnp.sum(dy, axis=0, keepdims=True)
            small_ref[3:4, :] += jnp.sum(err * err, axis=0, keepdims=True)
            d_mo_bf.append(d_mo.astype(BF16))
        acc_o[...] += tn(jnp.concatenate(merged, axis=0), jnp.concatenate(d_mo_bf, axis=0))
        dmerged = [nt(g, wo_ref[...]) for g in d_mo_bf]
        dy_attn, dy_conv = [], []
        for k, rs in enumerate(halves):
            dy_attn.append((dmerged[k] * sa[k]).astype(BF16))
            dy_conv.append((dmerged[k] * sb[k]).astype(BF16))
            dpg_ref[rs, 0:d] = (dmerged[k] * y_attn[k] * sa[k] * (1.0 - sa[k])).astype(BF16)
            dpg_ref[rs, d:2 * d] = (dmerged[k] * y_conv[k] * sb[k] * (1.0 - sb[k])).astype(BF16)
        acc_pa[...] += tn(a_ref[...], jnp.concatenate(dy_attn, axis=0))
        acc_pc[...] += tn(b_ref[...], jnp.concatenate(dy_conv, axis=0))
        for k, rs in enumerate(halves):
            da_ref[rs, :] = nt(dy_attn[k], wpa_ref[...])
            db_ref[rs, :] = nt(dy_conv[k], wpc_ref[...])

        @pl.when(i == n_steps - 1)
        def _():
            copies = [pltpu.make_async_copy(acc_pa, gwpa_hbm, sem.at[0]), pltpu.make_async_copy(acc_pc, gwpc_hbm, sem.at[1]),
                      pltpu.make_async_copy(acc_o, gwo_hbm, sem.at[2])]
            for cp in copies:
                cp.start()
            for cp in copies:
                cp.wait()

    row = lambda w: pl.BlockSpec((tm, w), lambda i: (i, 0))
    const = lambda shp: pl.BlockSpec(shp, lambda i: (0,) * len(shp), pipeline_mode=pl.Buffered(1))
    any_spec = pl.BlockSpec(memory_space=pl.ANY)
    return pl.pallas_call(
        body, name="tail", grid=(n_steps,),
        in_specs=[row(Z_WIDTH), row(d),
                  pl.BlockSpec((tm, d), lambda i: (i, gate_blk)), pl.BlockSpec((tm, d), lambda i: (i, gate_blk + 1)),
                  row(d), row(d), pl.BlockSpec((None, 1, 3 * d), lambda i: (i // per_seq, 0, 0)),
                  const((Z_WIDTH, d)), const((d, d)), const((d, d)), const((1, d)), const((1, d)), const((1, d))],
        out_specs=[pl.BlockSpec((tm, 2 * d), lambda i: (i, lay.g0 // (2 * d))), row(Z_WIDTH), row(d), row(d),
                   pl.BlockSpec((None, 1, d), lambda i: (i // per_seq, 0, 0)), pl.BlockSpec((8, d), lambda i: (0, 0)),
                   HBM_SPEC, HBM_SPEC, HBM_SPEC],
        out_shape=[jax.ShapeDtypeStruct((t, lay.np), BF16), jax.ShapeDtypeStruct((t, Z_WIDTH), F32),
                   jax.ShapeDtypeStruct((t, d), F32), jax.ShapeDtypeStruct((t, d), F32),
                   jax.ShapeDtypeStruct((t // seq, 1, d), F32), jax.ShapeDtypeStruct((8, d), F32),
                   pltpu.HBM((Z_WIDTH, d), F32), pltpu.HBM((d, d), F32), pltpu.HBM((d, d), F32)],
        scratch_shapes=[pltpu.VMEM((Z_WIDTH, d), F32), pltpu.VMEM((d, d), F32), pltpu.VMEM((d, d), F32),
                        pltpu.SemaphoreType.DMA((3,))],
        compiler_params=_params("arbitrary"),
    )(a_in, b_in, pr, pr, x2, target2, mod3, w_pa, w_pc, w_out, b_out, ln_g, ln_b)


def _conv_bwd(dproj, db, pr, conv_w, bsz, seq, lay):
    d = lay.d
    ct = CONV_TILE
    base = lay.c0 // (4 * ct)

    def body(dp_in, db_ref, p_ref, cw_ref, dp_ref, gcw_ref):
        del dp_in
        u_x, g_b, g_c, z = [p_ref[:, k * ct:(k + 1) * ct].astype(F32) for k in range(4)]
        cw = cw_ref[...]
        u = g_c * u_x
        rows = lax.broadcasted_iota(jnp.int32, u.shape, 0)
        u1, u2 = _shift_down(u, 1, rows), _shift_down(u, 2, rows)
        conv = cw[0:1, :] * u2 + cw[1:2, :] * u1 + cw[2:3, :] * u
        sig = _sigmoid(z)
        sl = z * sig
        dbv = db_ref[...]
        gbc = g_b * conv
        dp_ref[:, ct:2 * ct] = (dbv * sl * conv).astype(BF16)
        dp_ref[:, 3 * ct:4 * ct] = (dbv * gbc * (sig * (1.0 + z * (1.0 - sig)))).astype(BF16)
        dconv = dbv * sl * g_b

        @pl.when(pl.program_id(1) == 0)
        def _():
            gcw_ref[...] = jnp.zeros_like(gcw_ref)

        gcw_ref[0:1, :] += jnp.sum(dconv * u2, axis=0, keepdims=True)
        gcw_ref[1:2, :] += jnp.sum(dconv * u1, axis=0, keepdims=True)
        gcw_ref[2:3, :] += jnp.sum(dconv * u, axis=0, keepdims=True)
        du = cw[2:3, :] * dconv + cw[1:2, :] * _shift_up(dconv, 1, rows) + cw[0:1, :] * _shift_up(dconv, 2, rows)
        dp_ref[:, 0:ct] = (du * g_c).astype(BF16)
        dp_ref[:, 2 * ct:3 * ct] = (du * u_x).astype(BF16)

    return pl.pallas_call(
        body, name="conv_bwd", grid=(d // ct, bsz),
        in_specs=[pl.BlockSpec(memory_space=pl.ANY), pl.BlockSpec((seq, ct), lambda j, b: (b, j)),
                  pl.BlockSpec((seq, 4 * ct), lambda j, b: (b, j)), pl.BlockSpec((3, ct), lambda j, b: (0, j))],
        out_specs=[pl.BlockSpec((seq, 4 * ct), lambda j, b: (b, base + j)), pl.BlockSpec((8, ct), lambda j, b: (0, j))],
        out_shape=[jax.ShapeDtypeStruct(dproj.shape, BF16), jax.ShapeDtypeStruct((8, d), F32)],
        input_output_aliases={0: 0}, compiler_params=_params("arbitrary", "arbitrary"))(dproj, db, pr, conv_w)


def _attn_bwd(dproj, pa, o_all, lse_all, da, bsz, seq, after):
    n_blocks = seq // SUB
    chunk = 256
    per_trip = BLOCKS_PER_TRIP_BWD

    def body(dp_in, pa_ref, o_ref, lse_ref, da_ref, after_ref, dp_ref, sub, stage, dsub, dog, cvec, bias_ref,
             s_buf, dp_buf, ds_buf, pb_buf, q2_buf, do2_buf, l_buf, c_buf):
        del dp_in, after_ref
        p = pl.program_id(1)
        _fill_bias(bias_ref, p, seq)
        head0 = lax.broadcasted_iota(jnp.int32, (SUB, SLAB), 1) < HEAD_DIM

        def mix_bwd(i, carry):
            rows = pl.ds(pl.multiple_of(i * chunk, chunk), chunk)
            ls = [lse_ref[g, rows, :] for g in range(N_GROUPS)]
            os_ = [o_ref[g, rows, :] for g in range(N_GROUPS)]
            m = jnp.maximum(jnp.maximum(ls[0], ls[1]), ls[2])
            es = [jnp.exp(l - m) for l in ls]
            tot = es[0] + es[1] + es[2]
            ws = [e / tot for e in es]
            o = ws[0] * os_[0] + ws[1] * os_[1] + ws[2] * os_[2]
            z = pa_ref[rows, 9 * SLAB:10 * SLAB].astype(F32)
            sig = _sigmoid(z)
            dav = da_ref[rows, :]
            do = dav * (z * sig)
            dp_ref[rows, 9 * SLAB:10 * SLAB] = (dav * o * (sig * (1.0 + z * (1.0 - sig)))).astype(BF16)
            wsum = _head_sums(do * o)
            for g in range(N_GROUPS):
                dog[g, rows, :] = ws[g] * do
                cvec[g, rows, :] = -(ws[g] * wsum)
            return carry

        lax.fori_loop(0, seq // chunk, mix_bwd, 0)

        for g in range(N_GROUPS):
            dil = DILATIONS[g]
            for w in range(3):
                _to_sub_major(pa_ref, 3 * w + g, sub.at[w], stage, dil, seq)
            dsub[1] = jnp.zeros((seq, SLAB), F32)
            dsub[2] = jnp.zeros((seq, SLAB), F32)
            nk = _key_rows(g, seq)

            def trip(i, carry, g=g, dil=dil, nk=nk):
                places = [_block_rows(per_trip * i + j, g, seq) for j in range(per_trip)]
                for j, (row0, krow0, _, nat) in enumerate(places):
                    q = sub[0, pl.ds(row0, SUB), :]
                    do = _ld_rows(dog.at[g], nat, SUB, dil).astype(BF16)
                    zero = jnp.zeros_like(q)
                    q2 = jnp.concatenate([jnp.where(head0, q, zero), jnp.where(head0, zero, q)], axis=0)
                    do2 = jnp.concatenate([jnp.where(head0, do, zero), jnp.where(head0, zero, do)], axis=0)
                    q2_buf[j] = q2
                    do2_buf[j] = do2
                    s_buf[j, :, 0:nk] = _nt(q2 * (HEAD_DIM ** -0.5), sub[1, pl.ds(krow0, nk), :])
                    dp_buf[j, :, 0:nk] = _nt(do2, sub[2, pl.ds(krow0, nk), :])
                    l_buf[j] = _ld_rows(lse_ref.at[g], nat, SUB, dil)
                    c_buf[j] = _ld_rows(cvec.at[g], nat, SUB, dil)
                for j, (_, _, bi, _) in enumerate(places):
                    for c in range(0, 2 * SUB, SOFTMAX_ROWS):
                        hh, r = divmod(c, SUB)
                        lane = hh * HEAD_DIM
                        s = s_buf[j, c:c + SOFTMAX_ROWS, 0:nk] + bias_ref[g, bi, hh, r:r + SOFTMAX_ROWS, 0:nk]
                        prob = jnp.exp(s - l_buf[j, r:r + SOFTMAX_ROWS, lane:lane + 1])
                        dprob = dp_buf[j, c:c + SOFTMAX_ROWS, 0:nk] + c_buf[j, r:r + SOFTMAX_ROWS, lane:lane + 1]
                        ds_buf[j, c:c + SOFTMAX_ROWS, 0:nk] = (prob * dprob * (HEAD_DIM ** -0.5)).astype(BF16)
                        pb_buf[j, c:c + SOFTMAX_ROWS, 0:nk] = prob.astype(BF16)
                for j, (row0, krow0, _, _) in enumerate(places):
                    ds = ds_buf[j, :, 0:nk]
                    dq2 = jnp.dot(ds, sub[1, pl.ds(krow0, nk), :], preferred_element_type=F32)
                    dsub[0, pl.ds(row0, SUB), :] = jnp.where(head0, dq2[0:SUB], dq2[SUB:2 * SUB])
                    dsub[1, pl.ds(krow0, nk), :] += _tn(ds, q2_buf[j])
                    dsub[2, pl.ds(krow0, nk), :] += _tn(pb_buf[j, :, 0:nk], do2_buf[j])
                return carry

            lax.fori_loop(0, n_blocks // per_trip, trip, 0)
            for w in range(3):
                cols = slice((3 * w + g) * SLAB, (3 * w + g + 1) * SLAB)
                if dil == 1:
                    dp_ref[:, cols] = dsub[w].astype(BF16)
                else:
                    n = seq // dil
                    for r in range(dil):
                        stage[pl.ds(r, n, stride=dil), :] = dsub[w, pl.ds(r * n, n), :]
                    dp_ref[:, cols] = stage[...].astype(BF16)

    return pl.pallas_call(
        body, name="attn_bwd", grid=(bsz, 2),
        in_specs=[pl.BlockSpec(memory_space=pl.ANY), pl.BlockSpec((seq, PAIR_COLS), lambda b, p: (b, p)),
                  pl.BlockSpec((N_GROUPS, seq, SLAB), lambda b, p: (0, b, p)),
                  pl.BlockSpec((N_GROUPS, seq, SLAB), lambda b, p: (0, b, p)),
                  pl.BlockSpec((seq, SLAB), lambda b, p: (b, p)), pl.BlockSpec(memory_space=pl.ANY)],
        out_specs=pl.BlockSpec((seq, PAIR_COLS), lambda b, p: (b, p)),
        out_shape=jax.ShapeDtypeStruct(dproj.shape, BF16), input_output_aliases={0: 0},
        scratch_shapes=[pltpu.VMEM((3, seq, SLAB), BF16), pltpu.VMEM((seq, SLAB), F32), pltpu.VMEM((3, seq, SLAB), F32),
                        pltpu.VMEM((3, seq, SLAB), F32), pltpu.VMEM((3, seq, SLAB), F32),
                        pltpu.VMEM((N_GROUPS, 2, 2, SUB, 2 * SUB), F32),
                        pltpu.VMEM((per_trip, 2 * SUB, 2 * SUB), F32), pltpu.VMEM((per_trip, 2 * SUB, 2 * SUB), F32),
                        pltpu.VMEM((per_trip, 2 * SUB, 2 * SUB), BF16), pltpu.VMEM((per_trip, 2 * SUB, 2 * SUB), BF16),
                        pltpu.VMEM((per_trip, 2 * SUB, SLAB), BF16), pltpu.VMEM((per_trip, 2 * SUB, SLAB), BF16),
                        pltpu.VMEM((per_trip, SUB, SLAB), F32), pltpu.VMEM((per_trip, SUB, SLAB), F32)],
        compiler_params=_params("arbitrary", "arbitrary"))(dproj, pa, o_all, lse_all, da, after)


def _grad_h(dproj, w_all, gx0, x2, mod3, seq, lay):
    t, d = x2.shape
    tm, tn = min(512, seq), min(512, d)
    per_seq = seq // tm

    def body(dp_ref, w_ref, gx0_ref, x_ref, scale_ref, gx_ref, dmod_ref):
        dh = _nt(dp_ref[:, 0:ATT], w_ref[:, 0:ATT]) + _nt(dp_ref[:, lay.c0:], w_ref[:, lay.c0:])
        gx_ref[...] = gx0_ref[...] + dh * (1.0 + scale_ref[...])

        @pl.when(pl.program_id(1) % per_seq == 0)
        def _():
            dmod_ref[...] = jnp.zeros_like(dmod_ref)

        dmod_ref[0:1, :] += jnp.sum(dh, axis=0, keepdims=True)
        dmod_ref[1:2, :] += jnp.sum(dh * x_ref[...], axis=0, keepdims=True)

    tile = pl.BlockSpec((tm, tn), lambda j, i: (i, j))
    return pl.pallas_call(
        body, name="grad_h", grid=(d // tn, t // tm),
        in_specs=[pl.BlockSpec((tm, lay.np), lambda j, i: (i, 0)), pl.BlockSpec((tn, lay.np), lambda j, i: (j, 0)),
                  tile, tile,
                  pl.BlockSpec((None, 1, tn), lambda j, i: (i // per_seq, 0, d // tn + j))],
        out_specs=[tile, pl.BlockSpec((None, 8, tn), lambda j, i: (i // per_seq, 0, j))],
        out_shape=[jax.ShapeDtypeStruct((t, d), F32), jax.ShapeDtypeStruct((t // seq, 8, d), F32)],
        compiler_params=_params("arbitrary", "arbitrary"))(dproj, w_all, gx0, x2, mod3)


def _grad_w_in(ht, dproj, seq, lay, part, prev):
    d, t = ht.shape
    tm = min(t, 2 * seq)
    n_i = t // tm

    def side_by_side(pieces):
        return pieces[0] if len(pieces) == 1 else jnp.concatenate(pieces, axis=1)

    def make_body(n_skip, n_pieces, tn, nat_tile, assemble):
        def body(*refs):
            refs = refs[n_skip:]
            ht_ref, dp_refs = refs[0], refs[1:1 + n_pieces]
            gw_hbm, gb_hbm, acc, bacc, gw_out, gb_out, sem = refs[1 + n_pieces:]
            i, j = pl.program_id(0), pl.program_id(1)
            dp = assemble([r[...] for r in dp_refs])
            part = jnp.dot(ht_ref[...], dp, preferred_element_type=F32)
            bpart = jnp.sum(dp.astype(F32), axis=0, keepdims=True)

            if n_i > 1:
                @pl.when(i == 0)
                def _():
                    acc[j] = part
                    bacc[j] = bpart

                @pl.when((i > 0) & (i < n_i - 1))
                def _():
                    acc[j] += part
                    bacc[j] += bpart

            @pl.when(i == n_i - 1)
            def _():
                gw_out[...] = ((part + acc[j]) if n_i > 1 else part).astype(BF16)
                gb_out[...] = (bpart + bacc[j]) if n_i > 1 else bpart
                cols = pl.ds(pl.multiple_of(nat_tile(j) * tn, SLAB), tn)
                copies = [pltpu.make_async_copy(gw_out, gw_hbm.at[:, cols], sem.at[0]),
                          pltpu.make_async_copy(gb_out, gb_hbm.at[:, cols], sem.at[1])]
                for cp in copies:
                    cp.start()
                for cp in copies:
                    cp.wait()
        return body

    def call(name, pieces, n_tiles, nat_tile, prev, assemble=side_by_side):
        tn = sum(w for w, _ in pieces)
        any_spec = pl.BlockSpec(memory_space=pl.ANY)
        in_specs = [pl.BlockSpec((d, tm), lambda i, j: (0, i))]
        in_specs += [pl.BlockSpec((tm, w), lambda i, j, f=f: (i, f(j))) for w, f in pieces]
        args = [ht] + [dproj] * len(pieces)
        aliases = {}
        if prev is not None:
            in_specs = [any_spec] * 2 + in_specs
            args = list(prev) + args
            aliases = {0: 0, 1: 1}
        return pl.pallas_call(
            make_body(0 if prev is None else 2, len(pieces), tn, nat_tile, assemble), name=name, grid=(n_i, n_tiles),
            in_specs=in_specs,
            out_specs=[any_spec, any_spec],
            out_shape=[jax.ShapeDtypeStruct((d, lay.din), BF16), jax.ShapeDtypeStruct((1, lay.din), F32)],
            input_output_aliases=aliases,
            scratch_shapes=[pltpu.VMEM((n_tiles if n_i > 1 else 1, d, tn), F32), pltpu.VMEM((n_tiles, 1, tn), F32),
                            pltpu.VMEM((d, tn), BF16),
                            pltpu.VMEM((1, tn), F32), pltpu.SemaphoreType.DMA((2,))],
            compiler_params=_params("arbitrary", "arbitrary"))(*args)

    if part == "attn":
        pair_blocks = PAIR_COLS // (2 * SLAB)
        attn_pieces = [(2 * SLAB, lambda j, p=p: p * pair_blocks + j) for p in range(2)]

        def interleave(pieces):
            a, b = pieces
            return jnp.concatenate([a[:, :SLAB], b[:, :SLAB], a[:, SLAB:], b[:, SLAB:]], axis=1)

        return call("grad_w_in_attn", attn_pieces, ATT // 512, lambda j: j, prev, interleave)
    base = lay.c0 // CONV_TILE
    nct = lay.n_conv_tiles
    if nct % 2:
        return call("grad_w_in_rest", [(CONV_TILE, lambda j: base + j)], 6 * d // CONV_TILE, lay.rest_nat_tile, prev)
    half = nct // 2

    def rest_piece(m):
        def perm_tile(j):
            conv = base + 4 * (2 * (j % half) + m) + j // half
            return jnp.where(j < 4 * half, conv, base + 2 * j + m)
        return (CONV_TILE, perm_tile)

    return call("grad_w_in_rest", [rest_piece(0), rest_piece(1)], 6 * d // 512, lambda j: ATT // 512 + j, prev)


def _pack_rows(parts, width=128):
    flat = [p.reshape(-1) for p in parts]
    spans, rows = [], 0
    padded = []
    for f in flat:
        n = -(-f.shape[0] // (8 * width)) * 8
        padded.append(jnp.pad(f, (0, n * width - f.shape[0])).reshape(n, width))
        spans.append((rows, f.shape[0]))
        rows += n
    return jnp.concatenate(padded, axis=0), spans


def _unpack_rows(packed, spans, shapes, width=128):
    out = []
    for (row, n), shp in zip(spans, shapes):
        rows = -(-n // width)
        out.append(packed[row:row + rows].reshape(-1)[:n].reshape(shp))
    return out


def kernel(x, c, w_ada, b_ada, w_in, b_in, conv_w, w_proj_attn, w_proj_conv, w_out, b_out, ln_g, ln_b, loss_target, m_w_ada, m_b_ada, m_w_in, m_b_in, m_conv_w, m_w_proj_attn, m_w_proj_conv, m_w_out, m_b_out, m_ln_g, m_ln_b, v_w_ada, v_b_ada, v_w_in, v_b_in, v_conv_w, v_w_proj_attn, v_w_proj_conv, v_w_out, v_b_out, v_ln_g, v_ln_b):
    bsz, seq, d = x.shape
    t = bsz * seq
    lay = _Layout(d)
    col_sharded = [True, True, False, False]
    red_w = [w_in[0], w_proj_attn[0], w_proj_conv[0], w_out[0]]
    chip = 2 * lax.axis_index("x") + lax.axis_index("y")
    chip1 = chip.astype(jnp.int32).reshape(1)
    core1 = lax.axis_index("c").astype(jnp.int32).reshape(1)
    place = jnp.stack([chip, lax.axis_index("c")]).astype(jnp.int32)
    x2 = x.reshape(t, d)
    target2 = loss_target.reshape(t, d)

    mod, act_all = _ada_exchange(c, w_ada[0], b_ada)
    mod3 = mod.reshape(bsz, 1, 3 * d)

    cw_pad = jnp.pad(conv_w[0], ((0, 5), (0, 0))) + 0.0 * mod[0, 0]
    own_in_full = [_cast_into_full(red_w[0], col_sharded[0], chip1, "cast_shard_in")]
    own_in_full += list(_cast_into_full_small(red_w[1:], col_sharded[1:], chip1, "cast_shards_late"))
    (wi_f,), cw8 = _gather_weights(own_in_full[:1], col_sharded[:1], cw_pad)
    cw_full = cw8[0:3]
    late_copies = _direct_gather_copies(col_sharded[1:])
    late_send, late_recv, late_flying, late_token = _start_copies("gather_late_start", own_in_full[1:], (18,), cw8, late_copies)
    w_all = _permute_w_in(wi_f, lay)
    b_all = lay.perm_vector(b_in) + late_token[0, 0]

    rest_tn = 1024 if (6 * d) % 1024 == 0 else 512
    pa, = _project(x2, mod3, w_all, b_all, seq, 0, ATT, PAIR_COLS, BF16, False, "project_attn")
    pr, ht = _project(x2, mod3, w_all, b_all, seq, lay.c0, 6 * d, rest_tn, BF16, True, "project_rest")
    o_all, lse_all, a_in = _attn_fwd(pa, bsz, seq)
    b_in_act = _conv_fwd(pr, cw_full, bsz, seq, d)
    wpa_f, wpc_f, wo_f = _wait_copies("gather_late_wait", late_flying, late_send, late_recv, b_in_act, late_copies)
    (dproj, da_in, db_in, gx0, dgate, small_tail, gw_pa, gw_pc, gw_out) = _tail(
        a_in, b_in_act, pr, x2, target2, mod3, wpa_f, wpc_f, wo_f, b_out, ln_g, ln_b, seq, lay)

    late_views = _shard_views([gw_pa, gw_pc, gw_out], col_sharded[1:])
    late_lands = [lax.empty((v.shape[0], v.shape[1] // 2, v.shape[2]), v.dtype) for v in late_views]
    xl_send, xl_recv, xl_fly, xl_tok = _start_copies("grads_pair_exchange_late_start", late_views + late_lands, (3,), small_tail,
                                                     _pair_exchange_copies(3, False), pair_id=PAIR_ID_LATE)
    dproj, gcw = _conv_bwd(dproj, db_in, pr, cw_full + xl_tok[0, 0], bsz, seq, lay)
    xl_done = _wait_copies("grads_pair_exchange_late_wait", xl_fly, xl_send, xl_recv, gcw, _pair_exchange_copies(3, True))
    late_parts = list(_pair_sum_small(xl_done[:3], xl_done[3:], core1, "grads_pair_sum_late"))

    late_cross = _chip_scatter_copies(3, col_sharded[1:])
    late_zone = [lax.empty((3, p.shape[1], _piece_cols(p, cs)), p.dtype) for p, cs in zip(late_parts, col_sharded[1:])]
    sl_send, sl_recv, sl_fly, sl_tok = _start_copies("grads_scatter_late_start", late_parts + late_zone, (9,), core1,
                                                     late_cross)
    dproj = _attn_bwd(dproj, pa, o_all, lse_all, da_in, bsz, seq, sl_tok)
    gw_in_bf, gb_in = _grad_w_in(ht, dproj, seq, lay, "rest", None)
    rest_cols, attn_cols = (ATT, 6 * d), (0, ATT)
    in_land = lax.empty((1, d // 2, lay.din), BF16)
    xi_copies = _pair_exchange_copies(1, False, rest_cols)
    xi_send, xi_recv, xi_fly, _ = _start_copies("grads_pair_exchange_in_start", [gw_in_bf.reshape(1, d, lay.din), in_land], (1,),
                                                gb_in, xi_copies, pair_id=PAIR_ID_IN)
    gw_in_bf, gb_in = _grad_w_in(ht, dproj, seq, lay, "attn", (xi_fly[0].reshape(d, lay.din), gb_in))
    xi_done = _wait_copies("grads_pair_exchange_in_wait", [gw_in_bf.reshape(1, d, lay.din), xi_fly[1]], xi_send, xi_recv, gb_in,
                           _pair_exchange_copies(1, True, rest_cols))
    in_got = _pair_exchange_into(xi_done[0], xi_done[1], attn_cols, "grads_pair_exchange_in_attn")
    sl_done = _wait_copies("grads_scatter_late_wait", sl_fly, sl_send, sl_recv, in_got, late_cross)
    late_red = list(_chip_sum_small(sl_done[:3], sl_done[3:], col_sharded[1:], place, "grads_chip_sum_late"))

    in_part = _pair_sum(xi_done[0], in_got, core1, "grads_pair_sum_in")
    in_cross = _chip_scatter_copies(1, col_sharded[:1])
    in_zone = [lax.empty((3, in_part.shape[1], _piece_cols(in_part, True)), in_part.dtype)]
    late_views1 = [f.reshape(1, *f.shape) for f in late_red]
    si_send, si_recv, si_fly, si_tok = _start_copies(
        "grads_scatter_in_join_late_start", [in_part] + in_zone + late_views1, (6,), core1,
        _both_copies(in_cross, 2, _pair_join_copies(3, False, sem0=3)))
    grad_x2, dmod = _grad_h(dproj, w_all, gx0, x2, mod3 + si_tok[0, 0], seq, lay)
    si_done = _wait_copies("grads_scatter_in_join_late_wait", si_fly, si_send, si_recv, grad_x2,
                           _both_copies(in_cross, 2, _pair_join_copies(3, True, sem0=3)))
    late_joined = [f[0] for f in si_done[2:]]
    in_red = _chip_sum(si_done[0], si_done[1], True, place, "grads_chip_sum_in")

    d_ada = jnp.concatenate([dmod[:, 0, :], dmod[:, 1, :], dgate[:, 0, :]], axis=1)
    pieces = [small_tail[3], jnp.sum(d_ada, axis=0), gb_in[0], small_tail[0], small_tail[1], small_tail[2], gcw[0:3]]
    packed, spans = _pack_rows(pieces)
    kept_in, _ = _pack_rows([d_ada])
    rows_all = jnp.concatenate([packed, kept_in], axis=0)
    small_land = lax.empty((N_DEV,) + rows_all.shape, F32)
    sm_send, sm_recv, sm_fly, sm_tok = _start_copies(
        "small_gather_join_in_start", [rows_all, small_land, in_red.reshape(1, *in_red.shape)], (N_DEV,), core1,
        _both_copies(_small_gather_copies, 2, _pair_join_copies(1, False, sem0=N_DEV - 1)))
    big_w = [w_ada[0]] + red_w
    big_m = [m_w_ada[0], m_w_in[0], m_w_proj_attn[0], m_w_proj_conv[0], m_w_out[0]]
    big_v = [v_w_ada[0], v_w_in[0], v_w_proj_attn[0], v_w_proj_conv[0], v_w_out[0]]
    big_out = [None] * 5
    big_out[2:5] = _adamw_together(big_w[2:], late_joined, big_m[2:], big_v[2:], "adamw_late", sm_tok)
    sm_done = _wait_copies("small_gather_join_in_wait", sm_fly, sm_send, sm_recv, big_out[4][0],
                           _both_copies(_small_gather_copies, 2, _pair_join_copies(1, True, sem0=N_DEV - 1)))
    me1 = (4 * lax.axis_index("x") + 2 * lax.axis_index("y") + lax.axis_index("c")).astype(jnp.int32).reshape(1)
    summed, kept = _small_sum(sm_done[0], sm_done[1], me1, packed.shape[0], d)
    loss = summed[0, 0]
    _, g_b_ada, g_b_in, g_b_out, g_ln_g, g_ln_b, g_cw_full = _unpack_rows(
        summed, spans, [(d,), (3 * d,), (lay.din,), (d,), (d,), (d,), (3, d)])
    g_cw = lax.dynamic_slice(g_cw_full, (0, chip * (d // N_CHIPS)), (3, d // N_CHIPS))
    d_ada_all = kept.reshape(N_DEV, -1)[:, :bsz * 3 * d].reshape(N_DEV * bsz, 3 * d)
    ada_cols = 3 * d // N_CHIPS
    d_ada_cols = lax.dynamic_slice(d_ada_all, (0, chip * ada_cols), (N_DEV * bsz, ada_cols))

    g_w_ada, *big_out[0] = _grad_and_adamw_w_ada(act_all.T, d_ada_cols, big_w[0], big_m[0], big_v[0])
    small_w = [b_ada, b_in, conv_w[0], b_out, ln_g, ln_b]
    small_g = [g_b_ada, g_b_in, g_cw, g_b_out, g_ln_g, g_ln_b]
    small_m = [m_b_ada, m_b_in, m_conv_w[0], m_b_out, m_ln_g, m_ln_b]
    small_v = [v_b_ada, v_b_in, v_conv_w[0], v_b_out, v_ln_g, v_ln_b]
    pw, sp = _pack_rows(small_w)
    pg, _ = _pack_rows(small_g)
    pm, _ = _pack_rows(small_m)
    pv, _ = _pack_rows(small_v)
    sd, sm, sv = _adamw(pw, pg, pm, pv, "adamw_small", None)
    big_out[1] = _adamw(big_w[1], sm_done[2][0], big_m[1], big_v[1], "adamw_1", None, also_g=True)
    g_big = [g_w_ada, big_out[1][3]] + late_joined
    shapes = [a.shape for a in small_w]
    sd, sm, sv = _unpack_rows(sd, sp, shapes), _unpack_rows(sm, sp, shapes), _unpack_rows(sv, sp, shapes)

    def order(wa, bA, wi, bI, cw, wpa, wpc, wo, bO, lg, lb):
        return (wa[None], bA, wi[None], bI, cw[None], wpa[None], wpc[None], wo[None], bO, lg, lb)

    sg = [g.reshape(s) for g, s in zip(small_g, shapes)]
    grads_out = order(g_big[0], sg[0], g_big[1], sg[1], sg[2], g_big[2], g_big[3], g_big[4], sg[3], sg[4], sg[5])
    outs = []
    for idx, small in enumerate((sd, sm, sv)):
        outs.append(order(big_out[0][idx], small[0], big_out[1][idx], small[1], small[2], big_out[2][idx],
                          big_out[3][idx], big_out[4][idx], small[3], small[4], small[5]))
    return (loss, grad_x2.reshape(bsz, seq, d), *grads_out, *outs[0], *outs[1], *outs[2])
```

```python
import jax
import jax.numpy as jnp
from jax import lax
from jax.experimental import pallas as pl
from jax.experimental.pallas import tpu as pltpu

F32 = jnp.float32
BF16 = jnp.bfloat16
MESH = pl.DeviceIdType.MESH

HEAD_DIM = 64
N_GROUPS = 3
DILATIONS = (1, 4, 16)
N_HEADS = 12
SUB = 128
Q_WIDTH = 768
Z_WIDTH = 256
ATT = 3 * Q_WIDTH + Z_WIDTH
SLAB = 128
PAIR_SLABS = 10
PAIR_COLS = PAIR_SLABS * SLAB
CONV_TILE = 256
SOFTMAX_ROWS = 32
BLOCKS_PER_TRIP = 8
BLOCKS_PER_TRIP_BWD = 16
ALIBI_MAX_EXP = 8.0
ALPHA = 2.0 ** 0.25
LN_EPS = 1e-5
ADAM_LR, ADAM_B1, ADAM_B2, ADAM_EPS, ADAM_WD, ADAM_STEP = 0.001, 0.9, 0.999, 1e-08, 0.01, 10
N_CHIPS = 4
N_DEV = 8
VMEM_LIMIT_V7X = 60 * 1024 * 1024
NEG = -1e30


def _params(*sem):
    return pltpu.CompilerParams(dimension_semantics=sem, vmem_limit_bytes=VMEM_LIMIT_V7X)


def _sigmoid(v):
    return 0.5 * jnp.tanh(0.5 * v) + 0.5


class _Layout:
    def __init__(self, d):
        self.d = d
        self.din = ATT + 6 * d
        c0 = 3072
        while c0 % (2 * d):
            c0 += 1024
        self.c0, self.g0, self.np = c0, c0 + 4 * d, c0 + 6 * d
        self.n_conv_tiles = d // CONV_TILE

    def attn_nat_slab(self, s):
        p, i = s // PAIR_SLABS, s % PAIR_SLABS
        return jnp.where(i < 9, (i // 3) * 6 + (i % 3) * 2 + p, 18 + p)

    def rest_nat_tile(self, t):
        n4 = 4 * self.n_conv_tiles
        conv = ATT // CONV_TILE + (t % 4) * self.n_conv_tiles + t // 4
        return jnp.where(t < n4, conv, ATT // CONV_TILE + t)

    def perm_vector(self, v):
        parts = []
        for s in range(2 * PAIR_SLABS):
            p, i = divmod(s, PAIR_SLABS)
            ns = (i // 3) * 6 + (i % 3) * 2 + p if i < 9 else 18 + p
            parts.append(v[:, ns * SLAB:(ns + 1) * SLAB])
        parts.append(jnp.zeros((1, self.c0 - ATT), v.dtype))
        for j in range(self.n_conv_tiles):
            for k in range(4):
                a = ATT + k * self.d + j * CONV_TILE
                parts.append(v[:, a:a + CONV_TILE])
        parts.append(v[:, ATT + 4 * self.d:])
        return jnp.concatenate(parts, axis=1)


def _place():
    return lax.axis_index("x"), lax.axis_index("y"), lax.axis_index("c")


def _other_chips(x, y):
    return [(1 - x, y), (x, 1 - y), (1 - x, 1 - y)]


def _shard_of(ref, col_sharded, chip, half=None):
    if col_sharded:
        cs = ref.shape[1] // N_CHIPS
        cols = pl.ds(pl.multiple_of(chip * cs, SLAB), cs)
        if half is None:
            return ref.at[:, cols]
        n = ref.shape[0] // 2
        return ref.at[pl.ds(half * n, n), cols]
    rs = ref.shape[0] // N_CHIPS
    if half is None:
        return ref.at[pl.ds(chip * rs, rs)]
    return ref.at[pl.ds(chip * rs + half * (rs // 2), rs // 2)]


GATHER_PIECES = 8


def _cast_into_full(shard, col_sharded, chip, name):
    rows, cols = shard.shape
    tr = _row_tile(rows, cols)
    nb = rows // tr

    def body(chip_ref, s_ref, o_ref):
        del chip_ref
        o_ref[...] = s_ref[...].astype(BF16)

    if col_sharded:
        full, out_spec = (rows, cols * N_CHIPS), pl.BlockSpec((tr, cols), lambda i, ch: (i, ch[0]))
    else:
        full, out_spec = (rows * N_CHIPS, cols), pl.BlockSpec((tr, cols), lambda i, ch: (ch[0] * nb + i, 0))
    return pl.pallas_call(
        body, name=name,
        grid_spec=pltpu.PrefetchScalarGridSpec(num_scalar_prefetch=1, grid=(nb,),
                                               in_specs=[pl.BlockSpec((tr, cols), lambda i, ch: (i, 0))], out_specs=out_spec),
        out_shape=jax.ShapeDtypeStruct(full, BF16), compiler_params=_params("parallel"))(chip, shard)


def _cast_into_full_small(shards, col_sharded, chip, name):
    n = len(shards)

    def body(chip_ref, *refs):
        del chip_ref
        for w in range(n):
            refs[n + w][...] = refs[w][...].astype(BF16)

    in_specs = [pl.BlockSpec(s.shape, lambda i, ch: (0, 0)) for s in shards]
    out_specs = [pl.BlockSpec(s.shape, (lambda i, ch: (0, ch[0])) if cs else (lambda i, ch: (ch[0], 0)))
                 for s, cs in zip(shards, col_sharded)]
    fulls = [(s.shape[0], s.shape[1] * N_CHIPS) if cs else (s.shape[0] * N_CHIPS, s.shape[1]) for s, cs in zip(shards, col_sharded)]
    return pl.pallas_call(
        body, name=name,
        grid_spec=pltpu.PrefetchScalarGridSpec(num_scalar_prefetch=1, grid=(1,), in_specs=in_specs, out_specs=out_specs),
        out_shape=[jax.ShapeDtypeStruct(f, BF16) for f in fulls], compiler_params=_params("arbitrary"))(chip, *shards)


def _gather_weights(fulls, col_sharded, small):
    n = len(fulls)
    kp = GATHER_PIECES

    def piece(ref, cs, chip, half, k):
        if cs:
            width = ref.shape[1] // N_CHIPS
            rows = ref.shape[0] // 2 // kp
            return ref.at[pl.ds(half * (ref.shape[0] // 2) + k * rows, rows), pl.ds(pl.multiple_of(chip * width, SLAB), width)]
        rs = ref.shape[0] // N_CHIPS
        rows = rs // 2 // kp
        return ref.at[pl.ds(chip * rs + half * (rs // 2) + k * rows, rows)]

    def body(*refs):
        sm_in, outs, sm_out = refs[n], refs[n + 1:2 * n + 1], refs[2 * n + 1]
        send, recv, fsend, frecv, lsem, ssend, srecv = refs[2 * n + 2:]
        x, y, c = _place()
        mine = 2 * x + y
        sibling = (x, y, 1 - c)
        first = (x ^ (1 - c), y ^ c)
        second = (x ^ c, y ^ (1 - c))
        diagonal = (1 - x, 1 - y)
        sources = [first, second, diagonal]
        senders = [first, second, second]

        def copy(ref, sems, slot, to):
            return pltpu.make_async_remote_copy(src_ref=ref, dst_ref=ref, send_sem=sems[0].at[slot], recv_sem=sems[1].at[slot],
                                                device_id=to, device_id_type=MESH)

        local = pltpu.make_async_copy(sm_in, _shard_of(sm_out, True, mine), lsem)
        local.start()
        sends = []
        for k, (cx, cy) in enumerate(_other_chips(x, y)):
            cp = pltpu.make_async_remote_copy(src_ref=sm_in, dst_ref=_shard_of(sm_out, True, mine), send_sem=ssend.at[k],
                                              recv_sem=srecv.at[k], device_id=(cx, cy, c), device_id_type=MESH)
            cp.start()
            sends.append(cp)
        for k in range(kp):
            for w in range(n):
                own = piece(outs[w], col_sharded[w], mine, c, k)
                for slot, chip in enumerate((first, second)):
                    cp = copy(own, (send, recv), (w * 3 + slot) * kp + k, (*chip, c))
                    cp.start()
                    sends.append(cp)
        for slot in range(3):
            source = 2 * sources[slot][0] + sources[slot][1]
            for k in range(kp):
                for w in range(n):
                    landed = piece(outs[w], col_sharded[w], source, c, k)
                    copy(landed, (send, recv), (w * 3 + slot) * kp + k, (*senders[slot], c)).wait_recv()
                    if slot == 0:
                        cp = copy(landed, (send, recv), (w * 3 + 2) * kp + k, (*second, c))
                        cp.start()
                        sends.append(cp)
                    cp = copy(landed, (fsend, frecv), (w * 3 + slot) * kp + k, sibling)
                    cp.start()
                    sends.append(cp)
        for slot, chip in enumerate((second, first, diagonal)):
            for k in range(kp):
                for w in range(n):
                    passed = piece(outs[w], col_sharded[w], 2 * chip[0] + chip[1], 1 - c, k)
                    copy(passed, (fsend, frecv), (w * 3 + slot) * kp + k, sibling).wait_recv()
        for k, (cx, cy) in enumerate(_other_chips(x, y)):
            theirs = _shard_of(sm_out, True, 2 * cx + cy)
            pltpu.make_async_remote_copy(src_ref=theirs, dst_ref=theirs, send_sem=ssend.at[k], recv_sem=srecv.at[k],
                                         device_id=(cx, cy, c), device_id_type=MESH).wait_recv()
        for cp in sends:
            cp.wait_send()
        local.wait()

    any_spec = pl.BlockSpec(memory_space=pl.ANY)
    outs = pl.pallas_call(
        body, name="gather_weights",
        out_shape=[jax.ShapeDtypeStruct(f.shape, BF16) for f in fulls]
        + [jax.ShapeDtypeStruct((small.shape[0], small.shape[1] * N_CHIPS), small.dtype)],
        in_specs=[any_spec] * (n + 1), out_specs=[any_spec] * (n + 1), input_output_aliases={w: w for w in range(n)},
        scratch_shapes=[pltpu.SemaphoreType.DMA((n * 3 * kp,)), pltpu.SemaphoreType.DMA((n * 3 * kp,)),
                        pltpu.SemaphoreType.DMA((n * 3 * kp,)), pltpu.SemaphoreType.DMA((n * 3 * kp,)), pltpu.SemaphoreType.DMA,
                        pltpu.SemaphoreType.DMA((3,)), pltpu.SemaphoreType.DMA((3,))],
    )(*fulls, small)
    return outs[:n], outs[n]


PAIR_ID_LATE, PAIR_ID_IN, PAIR_ID_INTO = 1, 2, 3
HBM_SPEC = pl.BlockSpec(memory_space=pltpu.HBM)
SEM_SPEC = pl.BlockSpec(memory_space=pltpu.SEMAPHORE)
DATAFLOW = pltpu.SideEffectType.DATAFLOW_SIDE_EFFECTING


def _sibling_handshake():
    x, y, c = _place()
    barrier = pltpu.get_barrier_semaphore()
    pl.semaphore_signal(barrier, inc=1, device_id=(x, y, 1 - c), device_id_type=MESH)
    pl.semaphore_wait(barrier, 1)


def _start_copies(name, arrays, sem_shape, after, copies, pair_id=None):
    n = len(arrays)

    def body(*refs):
        if pair_id is not None:
            _sibling_handshake()
        for cp in copies(refs[:n], refs[n + 1], refs[n + 2]):
            cp.start()
        token = refs[2 * n + 3]
        token[...] = jnp.zeros_like(token)

    res = pl.pallas_call(
        body, name=name,
        out_shape=(pltpu.SemaphoreType.DMA(sem_shape), pltpu.SemaphoreType.DMA(sem_shape),
                   *[pltpu.HBM(a.shape, a.dtype) for a in arrays], jax.ShapeDtypeStruct((8, 128), F32)),
        in_specs=[HBM_SPEC] * n + [pl.BlockSpec(memory_space=pl.ANY)],
        out_specs=(SEM_SPEC, SEM_SPEC, *([HBM_SPEC] * n), pl.BlockSpec(memory_space=pltpu.VMEM)),
        input_output_aliases={i: 2 + i for i in range(n)},
        compiler_params=pltpu.CompilerParams(has_side_effects=DATAFLOW, collective_id=pair_id),
    )(*[pltpu.with_memory_space_constraint(a, pltpu.HBM) for a in arrays], after)
    return res[0], res[1], list(res[2:2 + n]), res[2 + n]


def _wait_copies(name, arrays, send, recv, after, copies):
    n = len(arrays)

    def body(*refs):
        for cp in copies(refs[:n], refs[n], refs[n + 1]):
            cp.wait_send()
            cp.wait_recv()

    return pl.pallas_call(
        body, name=name, out_shape=[pltpu.HBM(a.shape, a.dtype) for a in arrays],
        in_specs=[HBM_SPEC] * n + [SEM_SPEC, SEM_SPEC, pl.BlockSpec(memory_space=pl.ANY)], out_specs=[HBM_SPEC] * n,
        input_output_aliases={i: i for i in range(n)},
        compiler_params=pltpu.CompilerParams(has_side_effects=DATAFLOW),
    )(*arrays, send, recv, after)


def _direct_gather_copies(col_sharded):
    def copies(refs, send, recv):
        x, y, c = _place()
        mine = 2 * x + y
        out = []
        for w, ref in enumerate(refs):
            own_half = _shard_of(ref, col_sharded[w], mine, c)
            k = 0
            for cx, cy in _other_chips(x, y):
                for pc in (c, 1 - c):
                    out.append(pltpu.make_async_remote_copy(
                        src_ref=own_half, dst_ref=own_half, send_sem=send.at[6 * w + k], recv_sem=recv.at[6 * w + k],
                        device_id=(cx, cy, pc), device_id_type=MESH))
                    k += 1
        return out
    return copies


def _chip_scatter_copies(n, col_sharded):
    def piece(ref, cs, chip):
        if cs:
            w = ref.shape[2] // N_CHIPS
            return ref.at[:, :, pl.ds(pl.multiple_of(chip * w, SLAB), w)]
        return ref.at[pl.ds(chip, 1)]

    def copies(refs, send, recv):
        x, y, c = _place()
        out = []
        for k, (cx, cy) in enumerate(_other_chips(x, y)):
            for w in range(n):
                out.append(pltpu.make_async_remote_copy(
                    src_ref=piece(refs[w], col_sharded[w], 2 * cx + cy), dst_ref=refs[n + w].at[pl.ds(k, 1)],
                    send_sem=send.at[3 * w + k], recv_sem=recv.at[3 * w + k], device_id=(cx, cy, c), device_id_type=MESH))
        return out
    return copies


def _shard_views(gs, col_sharded):
    return [g.reshape(1, *g.shape) if cs else g.reshape(N_CHIPS, g.shape[0] // N_CHIPS, g.shape[1])
            for g, cs in zip(gs, col_sharded)]


DMA_CHUNK_BYTES = 1 << 20


def _chunk_rows(shape, itemsize):
    s, rows, cols = shape
    n = 1
    while s * (rows // n) * cols * itemsize > DMA_CHUNK_BYTES and (rows // n) % 32 == 0:
        n *= 2
    return rows // n


def _rows_of(ref, row0, rows, cols):
    if cols is None:
        return ref.at[:, pl.ds(row0, rows)]
    return ref.at[:, pl.ds(row0, rows), pl.ds(cols[0], cols[1])]


def _row_pieces(src, src_row0, dst, dst_row0, rows, send_sem, recv_sem, device, cols=None):
    width = src.shape[2] if cols is None else cols[1]
    step = _chunk_rows((src.shape[0], rows, width), src.dtype.itemsize)
    return [pltpu.make_async_remote_copy(src_ref=_rows_of(src, src_row0 + r, step, cols), dst_ref=_rows_of(dst, dst_row0 + r, step, cols),
                                         send_sem=send_sem, recv_sem=recv_sem, device_id=device, device_id_type=MESH)
            for r in range(0, rows, step)]


def _pair_exchange_copies(n, whole, cols=None):
    def copies(refs, send, recv):
        x, y, c = _place()
        sibling = (x, y, 1 - c)
        out = []
        for w in range(n):
            hr = refs[n + w].shape[1]
            if whole:
                out.append(pltpu.make_async_remote_copy(
                    src_ref=_rows_of(refs[w], (1 - c) * hr, hr, cols), dst_ref=_rows_of(refs[n + w], 0, hr, cols),
                    send_sem=send.at[w], recv_sem=recv.at[w], device_id=sibling, device_id_type=MESH))
            else:
                out += _row_pieces(refs[w], (1 - c) * hr, refs[n + w], 0, hr, send.at[w], recv.at[w], sibling, cols)
        return out
    return copies


def _pair_join_copies(n, whole, sem0=0):
    def copies(refs, send, recv):
        x, y, c = _place()
        sibling = (x, y, 1 - c)
        out = []
        for w in range(n):
            hr = refs[w].shape[1] // 2
            sems = dict(send_sem=send.at[sem0 + w], recv_sem=recv.at[sem0 + w])
            if whole:
                out.append(pltpu.make_async_remote_copy(
                    src_ref=refs[w].at[:, pl.ds(c * hr, hr)], dst_ref=refs[w].at[:, pl.ds((1 - c) * hr, hr)],
                    device_id=sibling, device_id_type=MESH, **sems))
            else:
                out += _row_pieces(refs[w], c * hr, refs[w], c * hr, hr, sems["send_sem"], sems["recv_sem"], sibling)
        return out
    return copies


def _both_copies(first, n_first, second):
    def copies(refs, send, recv):
        return first(refs[:n_first], send, recv) + second(refs[n_first:], send, recv)
    return copies


def _small_gather_copies(refs, send, recv):
    vec, land = refs
    x, y, c = _place()
    me = 4 * x + 2 * y + c
    return [pltpu.make_async_remote_copy(src_ref=vec, dst_ref=land.at[me], send_sem=send.at[k], recv_sem=recv.at[k],
                                         device_id=peer, device_id_type=MESH) for k, (peer, _) in enumerate(_all_devices(x, y, c))]


def _small_sum(vec, land, me, n_sum, d):
    rows = vec.shape[0]

    def body(me_ref, v_ref, l_ref, sum_ref, kept_ref):
        def slot(k):
            return jnp.where(me_ref[0] == k, v_ref[...], l_ref[k])

        total = slot(0)[0:n_sum, :]
        kept_ref[0] = slot(0)[n_sum:rows, :]
        for k in range(1, N_DEV):
            total = total + slot(k)[0:n_sum, :]
            kept_ref[k] = slot(k)[n_sum:rows, :]
        sum_ref[...] = total
        loss = 0.5 / d * jnp.sum(total[0:8, :])
        sum_ref[0:8, :] = jnp.full((8, 128), loss, F32)

    vm = pl.BlockSpec(memory_space=pltpu.VMEM)
    return pl.pallas_call(
        body, name="small_sum", in_specs=[pl.BlockSpec(memory_space=pltpu.SMEM), vm, vm], out_specs=[vm, vm],
        out_shape=[jax.ShapeDtypeStruct((n_sum, 128), F32), jax.ShapeDtypeStruct((N_DEV, rows - n_sum, 128), F32)],
        compiler_params=pltpu.CompilerParams(vmem_limit_bytes=VMEM_LIMIT_V7X))(me, vec, land)


def _pair_exchange_into(view, land, cols, name):
    def body(v_ref, l_in, l_ref, send, recv):
        del l_in
        _sibling_handshake()
        x, y, c = _place()
        sibling = (x, y, 1 - c)
        hr = l_ref.shape[1]
        for cp in _row_pieces(v_ref, (1 - c) * hr, l_ref, 0, hr, send, recv, sibling, cols):
            cp.start()
        pltpu.make_async_remote_copy(src_ref=_rows_of(v_ref, (1 - c) * hr, hr, cols), dst_ref=_rows_of(l_ref, 0, hr, cols),
                                     send_sem=send, recv_sem=recv, device_id=sibling, device_id_type=MESH).wait()

    any_spec = pl.BlockSpec(memory_space=pl.ANY)
    return pl.pallas_call(
        body, name=name, out_shape=jax.ShapeDtypeStruct(land.shape, land.dtype),
        in_specs=[any_spec, any_spec], out_specs=any_spec, input_output_aliases={1: 0},
        scratch_shapes=[pltpu.SemaphoreType.DMA, pltpu.SemaphoreType.DMA],
        compiler_params=pltpu.CompilerParams(collective_id=PAIR_ID_INTO),
    )(view, land)


def _pair_sum(view, got, core, name):
    s, r, cols = view.shape
    hr = r // 2
    tr = _row_tile(hr, cols, itemsize=view.dtype.itemsize, budget=4 << 20)
    nb = hr // tr

    def body(core_ref, a_ref, b_ref, o_ref):
        del core_ref
        o_ref[...] = (a_ref[...].astype(F32) + b_ref[...].astype(F32)).astype(BF16)

    same = pl.BlockSpec((None, tr, cols), lambda j, i, core_ref: (j, i, 0))
    return pl.pallas_call(
        body, name=name,
        grid_spec=pltpu.PrefetchScalarGridSpec(
            num_scalar_prefetch=1, grid=(s, nb),
            in_specs=[pl.BlockSpec((None, tr, cols), lambda j, i, core_ref: (j, core_ref[0] * nb + i, 0)), same],
            out_specs=same),
        out_shape=jax.ShapeDtypeStruct((s, hr, cols), BF16), compiler_params=_params("parallel", "parallel"))(core, view, got)


def _pair_sum_small(views, gots, core, name):
    n = len(views)

    def body(core_ref, *refs):
        del core_ref
        for w in range(n):
            refs[2 * n + w][...] = (refs[w][...].astype(F32) + refs[n + w][...].astype(F32)).astype(BF16)

    halves = [(v.shape[0], v.shape[1] // 2, v.shape[2]) for v in views]
    own = [pl.BlockSpec(h, lambda i, core_ref: (0, core_ref[0], 0)) for h in halves]
    whole = [pl.BlockSpec(h, lambda i, core_ref: (0, 0, 0)) for h in halves]
    return pl.pallas_call(
        body, name=name,
        grid_spec=pltpu.PrefetchScalarGridSpec(num_scalar_prefetch=1, grid=(1,), in_specs=own + whole, out_specs=whole),
        out_shape=[jax.ShapeDtypeStruct(h, BF16) for h in halves], compiler_params=_params("arbitrary"))(core, *views, *gots)


def _piece_cols(part, col_sharded):
    return part.shape[2] // N_CHIPS if col_sharded else part.shape[2]


def _chip_sum_small(parts, gots, col_sharded, place, name):
    n = len(parts)

    def body(place_ref, *refs):
        del place_ref
        for w in range(n):
            got = refs[n + w]
            acc = refs[w][...].astype(F32) + got[0].astype(F32)
            refs[2 * n + w][...] = acc + got[1].astype(F32) + got[2].astype(F32)

    own, others, outs, shapes = [], [], [], []
    for p, cs in zip(parts, col_sharded):
        hr, cols = p.shape[1], _piece_cols(p, cs)
        own.append(pl.BlockSpec((None, hr, cols), (lambda i, pr: (0, 0, pr[0])) if cs else (lambda i, pr: (pr[0], 0, 0))))
        others.append(pl.BlockSpec((3, hr, cols), lambda i, pr: (0, 0, 0)))
        outs.append(pl.BlockSpec((hr, cols), lambda i, pr: (pr[1], 0)))
        shapes.append(jax.ShapeDtypeStruct((2 * hr, cols), F32))
    return pl.pallas_call(
        body, name=name,
        grid_spec=pltpu.PrefetchScalarGridSpec(num_scalar_prefetch=1, grid=(1,), in_specs=own + others, out_specs=outs),
        out_shape=shapes, compiler_params=_params("arbitrary"))(place, *parts, *gots)


def _chip_sum(part, got, col_sharded, place, name):
    _, hr, _ = part.shape
    cols = _piece_cols(part, col_sharded)
    tr = _row_tile(hr, cols)
    nb = hr // tr

    def body(place_ref, own_ref, g0_ref, g1_ref, g2_ref, o_ref):
        del place_ref
        acc = own_ref[...].astype(F32) + g0_ref[...].astype(F32)
        o_ref[...] = acc + g1_ref[...].astype(F32) + g2_ref[...].astype(F32)

    if col_sharded:
        own = pl.BlockSpec((None, tr, cols), lambda i, pr: (0, i, pr[0]))
    else:
        own = pl.BlockSpec((None, tr, cols), lambda i, pr: (pr[0], i, 0))
    others = [pl.BlockSpec((None, tr, cols), lambda i, pr, k=k: (k, i, 0)) for k in range(3)]
    return pl.pallas_call(
        body, name=name,
        grid_spec=pltpu.PrefetchScalarGridSpec(
            num_scalar_prefetch=1, grid=(nb,), in_specs=[own] + others,
            out_specs=pl.BlockSpec((tr, cols), lambda i, pr: (pr[1] * nb + i, 0))),
        out_shape=jax.ShapeDtypeStruct((2 * hr, cols), F32), compiler_params=_params("parallel"))(place, part, got, got, got)


def _row_tile(rows, cols, itemsize=4, budget=2 << 20):
    t = rows
    while t * cols * itemsize > budget and t % 16 == 0:
        t //= 2
    return t


def _adamw(w, g, m, v, name, after, also_g=False):
    rows, cols = w.shape
    tr = _row_tile(rows, cols, budget=2 << 20)
    extra = [] if after is None else [after]
    n_out = 4 if also_g else 3

    def body(w_ref, g_ref, m_ref, v_ref, *rest):
        d_ref, nm_ref, nv_ref = rest[len(extra):len(extra) + 3]
        g_ = g_ref[...]
        if also_g:
            rest[len(extra) + 3][...] = g_
        nm = ADAM_B1 * m_ref[...] + (1.0 - ADAM_B1) * g_
        nv = ADAM_B2 * v_ref[...] + (1.0 - ADAM_B2) * (g_ * g_)
        m_hat = nm / (1.0 - ADAM_B1 ** ADAM_STEP)
        v_hat = nv / (1.0 - ADAM_B2 ** ADAM_STEP)
        d_ref[...] = -ADAM_LR * (m_hat / (jnp.sqrt(v_hat) + ADAM_EPS) + ADAM_WD * w_ref[...])
        nm_ref[...] = nm
        nv_ref[...] = nv

    spec = pl.BlockSpec((tr, cols), lambda i: (i, 0))
    shp = jax.ShapeDtypeStruct((rows, cols), F32)
    return pl.pallas_call(body, name=name, grid=(rows // tr,),
                          in_specs=[spec] * 4 + [pl.BlockSpec(memory_space=pl.ANY)] * len(extra), out_specs=[spec] * n_out,
                          out_shape=[shp] * n_out, compiler_params=_params("parallel"))(w, g, m, v, *extra)


def _adamw_together(ws, gs, ms, vs, name, after):
    n = len(ws)

    def body(*refs):
        outs = refs[4 * n + 1:]
        for k in range(n):
            g_ = refs[n + k][...]
            nm = ADAM_B1 * refs[2 * n + k][...] + (1.0 - ADAM_B1) * g_
            nv = ADAM_B2 * refs[3 * n + k][...] + (1.0 - ADAM_B2) * (g_ * g_)
            m_hat = nm / (1.0 - ADAM_B1 ** ADAM_STEP)
            v_hat = nv / (1.0 - ADAM_B2 ** ADAM_STEP)
            outs[3 * k][...] = -ADAM_LR * (m_hat / (jnp.sqrt(v_hat) + ADAM_EPS) + ADAM_WD * refs[k][...])
            outs[3 * k + 1][...] = nm
            outs[3 * k + 2][...] = nv

    specs = [pl.BlockSpec(w.shape, lambda i: (0, 0)) for w in ws]
    out_specs, out_shape = [], []
    for w in ws:
        out_specs += [pl.BlockSpec(w.shape, lambda i: (0, 0))] * 3
        out_shape += [jax.ShapeDtypeStruct(w.shape, F32)] * 3
    res = pl.pallas_call(body, name=name, grid=(1,), in_specs=specs * 4 + [pl.BlockSpec(memory_space=pl.ANY)],
                         out_specs=out_specs, out_shape=out_shape, compiler_params=_params("arbitrary"))(*ws, *gs, *ms, *vs, after)
    return [tuple(res[3 * k:3 * k + 3]) for k in range(n)]


def _all_devices(x, y, c):
    out = []
    for k in range(1, N_DEV):
        peer = (x ^ ((k >> 2) & 1), y ^ ((k >> 1) & 1), c ^ (k & 1))
        out.append((peer, 4 * peer[0] + 2 * peer[1] + peer[2]))
    return out


def _ada_exchange(c, w_shard, b_ada):
    bsz, d = c.shape
    cs = w_shard.shape[1]

    def body(c_ref, w_ref, b_ref, mod_ref, act_ref, c_all, part, pieces, csend, crecv, psend, precv):
        x, y, core = _place()
        me = 4 * x + 2 * y + core
        chip = 2 * x + y
        c_all[me] = c_ref[...]
        peers = _all_devices(x, y, core)
        copies = []
        for k, (peer, _) in enumerate(peers):
            cp = pltpu.make_async_remote_copy(src_ref=c_ref, dst_ref=c_all.at[me], send_sem=csend.at[k], recv_sem=crecv.at[k],
                                              device_id=peer, device_id_type=MESH)
            cp.start()
            copies.append(cp)
        for k, (_, src) in enumerate(peers):
            pltpu.make_async_remote_copy(src_ref=c_ref, dst_ref=c_all.at[src], send_sem=csend.at[k], recv_sem=crecv.at[k],
                                         device_id=(x, y, core), device_id_type=MESH).wait_recv()
        rows = jnp.concatenate([c_all[i] for i in range(N_DEV)], axis=0)
        act = rows * _sigmoid(rows)
        act_ref[...] = act
        prod = jnp.dot(act.astype(BF16), w_ref[...].astype(BF16), preferred_element_type=F32)
        for i in range(N_DEV):
            part[i] = prod[i * bsz:(i + 1) * bsz, :]
        pieces[chip] = part[me]
        chips = _other_chips(x, y)
        for k, (cx, cy) in enumerate(chips):
            cp = pltpu.make_async_remote_copy(src_ref=part.at[4 * cx + 2 * cy + core], dst_ref=pieces.at[chip],
                                              send_sem=psend.at[k], recv_sem=precv.at[k], device_id=(cx, cy, core),
                                              device_id_type=MESH)
            cp.start()
            copies.append(cp)
        for k, (cx, cy) in enumerate(chips):
            pltpu.make_async_remote_copy(src_ref=part.at[me], dst_ref=pieces.at[2 * cx + cy], send_sem=psend.at[k],
                                         recv_sem=precv.at[k], device_id=(cx, cy, core), device_id_type=MESH).wait_recv()
        for cp in copies:
            cp.wait_send()
        mod_ref[...] = jnp.concatenate([pieces[j] for j in range(N_CHIPS)], axis=1) + b_ref[...]

    vm = pl.BlockSpec(memory_space=pltpu.VMEM)
    return pl.pallas_call(
        body, name="ada_exchange", in_specs=[vm] * 3, out_specs=[vm] * 2,
        out_shape=[jax.ShapeDtypeStruct((bsz, 3 * d), F32), jax.ShapeDtypeStruct((N_DEV * bsz, d), F32)],
        scratch_shapes=[pltpu.VMEM((N_DEV, bsz, d), F32), pltpu.VMEM((N_DEV, bsz, cs), F32), pltpu.VMEM((N_CHIPS, bsz, cs), F32),
                        pltpu.SemaphoreType.DMA((N_DEV - 1,)), pltpu.SemaphoreType.DMA((N_DEV - 1,)),
                        pltpu.SemaphoreType.DMA((3,)), pltpu.SemaphoreType.DMA((3,))],
        compiler_params=pltpu.CompilerParams(vmem_limit_bytes=VMEM_LIMIT_V7X))(c, w_shard, b_ada)


def _grad_and_adamw_w_ada(act_t, d_cols, w, m, v):
    d, n = act_t.shape
    cs = d_cols.shape[1]
    tn = min(256, cs)

    def body(a_ref, g_ref, w_ref, m_ref, v_ref, go_ref, d_ref, nm_ref, nv_ref):
        a, g = a_ref[...], g_ref[...]
        acc = a[:, 0:1] * g[0:1, :]
        for b in range(1, n):
            acc = acc + a[:, b:b + 1] * g[b:b + 1, :]
        go_ref[...] = acc
        nm = ADAM_B1 * m_ref[...] + (1.0 - ADAM_B1) * acc
        nv = ADAM_B2 * v_ref[...] + (1.0 - ADAM_B2) * (acc * acc)
        m_hat = nm / (1.0 - ADAM_B1 ** ADAM_STEP)
        v_hat = nv / (1.0 - ADAM_B2 ** ADAM_STEP)
        d_ref[...] = -ADAM_LR * (m_hat / (jnp.sqrt(v_hat) + ADAM_EPS) + ADAM_WD * w_ref[...])
        nm_ref[...] = nm
        nv_ref[...] = nv

    tile = pl.BlockSpec((d, tn), lambda j: (0, j))
    shp = jax.ShapeDtypeStruct((d, cs), F32)
    return pl.pallas_call(body, name="adamw_w_ada", grid=(cs // tn,),
                          in_specs=[pl.BlockSpec((d, n), lambda j: (0, 0)), pl.BlockSpec((n, tn), lambda j: (0, j)), tile, tile, tile],
                          out_specs=[tile] * 4, out_shape=[shp] * 4, compiler_params=_params("parallel"))(act_t, d_cols, w, m, v)


def _permute_w_in(w_nat, lay):
    d = lay.d
    group = 4

    def call(name, width, n_pieces, nat_piece, out_block0, prev):
        def body(*refs):
            refs[-1][...] = jnp.concatenate([r[...] for r in refs[:group]], axis=1)

        in_specs = [pl.BlockSpec((d, width), lambda s, m=m: (0, nat_piece(group * s + m))) for m in range(group)]
        args = [w_nat] * group
        aliases = {}
        if prev is not None:
            in_specs.append(pl.BlockSpec(memory_space=pl.ANY))
            args.append(prev)
            aliases = {group: 0}
        return pl.pallas_call(
            body, name=name, grid=(n_pieces // group,), in_specs=in_specs,
            out_specs=pl.BlockSpec((d, group * width), lambda s: (0, out_block0 + s)),
            out_shape=jax.ShapeDtypeStruct((d, lay.np), BF16), input_output_aliases=aliases,
            compiler_params=_params("arbitrary"))(*args)

    w_all = call("permute_w_attn", SLAB, 2 * PAIR_SLABS, lay.attn_nat_slab, 0, None)
    n_rest = 6 * d // CONV_TILE
    if n_rest % group:
        group = 2
    return call("permute_w_rest", CONV_TILE, n_rest, lay.rest_nat_tile, lay.c0 // (group * CONV_TILE), w_all)


def _project(x2, mod3, w_all, b_all, seq, col0, ncols, tn, out_dtype, want_ht, name):
    t, d = x2.shape
    tm = min(2048, seq)
    per_seq = seq // tm
    j0 = col0 // tn

    def body(x_ref, mod_ref, w_ref, b_ref, o_ref, *rest):
        h_ref = rest[-1]

        @pl.when(pl.program_id(1) == 0)
        def _():
            h = x_ref[...] * (1.0 + mod_ref[:, d:2 * d]) + mod_ref[:, 0:d]
            h_ref[...] = h.astype(BF16)
            if want_ht:
                rest[0][...] = h.T.astype(BF16)

        o_ref[...] = (jnp.dot(h_ref[...], w_ref[...], preferred_element_type=F32) + b_ref[...]).astype(out_dtype)

    out_shape = [jax.ShapeDtypeStruct((t, ncols), out_dtype)]
    out_specs = [pl.BlockSpec((tm, tn), lambda i, j: (i, j))]
    if want_ht:
        out_shape.append(jax.ShapeDtypeStruct((d, t), BF16))
        out_specs.append(pl.BlockSpec((d, tm), lambda i, j: (0, i)))
    return pl.pallas_call(
        body, name=name, grid=(t // tm, ncols // tn),
        in_specs=[pl.BlockSpec((tm, d), lambda i, j: (i, 0)),
                  pl.BlockSpec((None, 1, 3 * d), lambda i, j: (i // per_seq, 0, 0)),
                  pl.BlockSpec((d, tn), lambda i, j: (0, j0 + j)),
                  pl.BlockSpec((1, tn), lambda i, j: (0, j0 + j))],
        out_specs=out_specs, out_shape=out_shape,
        scratch_shapes=[pltpu.VMEM((tm, d), BF16)],
        compiler_params=_params("arbitrary", "arbitrary"))(x2, mod3, w_all, b_all)


def _slope(g, p, hh):
    head = 4 * g + 2 * p + hh
    return 2.0 ** (-ALIBI_MAX_EXP * (head + 1.0) / N_HEADS)


def _ld_rows(ref, start, n, stride):
    if stride == 1:
        return ref[pl.ds(start, n), :]
    return ref[pl.ds(start, n, stride=stride), :]


def _st_rows(ref, start, n, stride, val):
    if stride == 1:
        ref[pl.ds(start, n), :] = val
    else:
        ref[pl.ds(start, n, stride=stride), :] = val


def _sub_blocks(g, seq):
    return seq // DILATIONS[g] // SUB


def _key_rows(g, seq):
    return SUB if _sub_blocks(g, seq) == 1 else 2 * SUB


def _fill_bias(bias_ref, p, seq):
    for g in range(N_GROUPS):
        nk = _key_rows(g, seq)
        diff = lax.broadcasted_iota(jnp.int32, (SUB, nk), 0) - lax.broadcasted_iota(jnp.int32, (SUB, nk), 1)
        for i, off in enumerate((0, SUB)):
            if i == 1 and nk == SUB:
                continue
            delta = diff + off
            ok = (delta >= 0) & (delta <= SUB)
            dist = (delta * DILATIONS[g]).astype(F32)
            for hh in range(2):
                slope = jnp.where(p == 0, _slope(g, 0, hh), _slope(g, 1, hh))
                bias_ref[g, i, hh, :, 0:nk] = jnp.where(ok, -slope * dist, NEG)


def _to_sub_major(pa_ref, col, sub_ref, stage, dil, seq):
    cols = slice(col * SLAB, (col + 1) * SLAB)
    if dil == 1:
        sub_ref[...] = pa_ref[:, cols]
        return
    n = seq // dil
    stage[...] = pa_ref[:, cols].astype(F32)
    for r in range(dil):
        sub_ref[pl.ds(r * n, n), :] = stage[pl.ds(r, n, stride=dil), :].astype(BF16)


def _block_rows(it, g, seq):
    dil, nb = DILATIONS[g], _sub_blocks(g, seq)
    row0 = pl.multiple_of(it * SUB, SUB)
    if nb == 1:
        return row0, row0, 0, it
    blk = it % nb
    first = blk == 0
    krow0 = pl.multiple_of(row0 - jnp.where(first, 0, SUB), SUB)
    nat = row0 if dil == 1 else it // nb + dil * SUB * blk
    return row0, krow0, jnp.where(first, 0, 1), nat


def _nt(a, b):
    return lax.dot_general(a, b, (((1,), (1,)), ((), ())), preferred_element_type=F32)


def _tn(a, b):
    return lax.dot_general(a, b, (((0,), (0,)), ((), ())), preferred_element_type=F32)


def _head_sums(t):
    rows = t.shape[0]
    lo = jnp.broadcast_to(jnp.sum(t[:, :HEAD_DIM], axis=-1, keepdims=True), (rows, HEAD_DIM))
    hi = jnp.broadcast_to(jnp.sum(t[:, HEAD_DIM:], axis=-1, keepdims=True), (rows, HEAD_DIM))
    return jnp.concatenate([lo, hi], axis=-1)


def _attn_fwd(pa, bsz, seq):
    t = pa.shape[0]
    n_blocks = seq // SUB
    chunk = 256

    def body(pa_ref, o_ref, lse_ref, a_ref, sub, stage, bias_ref, s_buf, p_buf, l_buf):
        p = pl.program_id(1)
        _fill_bias(bias_ref, p, seq)
        head0 = lax.broadcasted_iota(jnp.int32, (SUB, SLAB), 1) < HEAD_DIM
        for g in range(N_GROUPS):
            dil = DILATIONS[g]
            for w in range(3):
                _to_sub_major(pa_ref, 3 * w + g, sub.at[w], stage, dil, seq)
            nk = _key_rows(g, seq)

            def trip(i, carry, g=g, dil=dil, nk=nk):
                places = [_block_rows(BLOCKS_PER_TRIP * i + j, g, seq) for j in range(BLOCKS_PER_TRIP)]
                for j, (row0, krow0, _, _) in enumerate(places):
                    q = sub[0, pl.ds(row0, SUB), :]
                    zero = jnp.zeros_like(q)
                    q2 = jnp.concatenate([jnp.where(head0, q, zero), jnp.where(head0, zero, q)], axis=0) * (HEAD_DIM ** -0.5)
                    s_buf[j, :, 0:nk] = _nt(q2, sub[1, pl.ds(krow0, nk), :])
                for j, (_, _, bi, _) in enumerate(places):
                    for c in range(0, 2 * SUB, SOFTMAX_ROWS):
                        hh, r = divmod(c, SUB)
                        s = s_buf[j, c:c + SOFTMAX_ROWS, 0:nk] + bias_ref[g, bi, hh, r:r + SOFTMAX_ROWS, 0:nk]
                        m = jnp.max(s, axis=-1, keepdims=True)
                        e = jnp.exp(s - m)
                        den = jnp.sum(e, axis=-1, keepdims=True)
                        p_buf[j, c:c + SOFTMAX_ROWS, 0:nk] = (e * (1.0 / den)).astype(BF16)
                        l_buf[j, c:c + SOFTMAX_ROWS, :] = jnp.broadcast_to(m + jnp.log(den), (SOFTMAX_ROWS, SLAB))
                for j, (_, krow0, _, nat) in enumerate(places):
                    o2 = jnp.dot(p_buf[j, :, 0:nk], sub[2, pl.ds(krow0, nk), :], preferred_element_type=F32)
                    _st_rows(o_ref.at[g], nat, SUB, dil, jnp.where(head0, o2[0:SUB], o2[SUB:2 * SUB]))
                    _st_rows(lse_ref.at[g], nat, SUB, dil, jnp.where(head0, l_buf[j, 0:SUB, :], l_buf[j, SUB:2 * SUB, :]))
                return carry

            lax.fori_loop(0, n_blocks // BLOCKS_PER_TRIP, trip, 0)

        def mix(i, carry):
            rows = pl.ds(pl.multiple_of(i * chunk, chunk), chunk)
            l0, l1, l2 = lse_ref[0, rows, :], lse_ref[1, rows, :], lse_ref[2, rows, :]
            m = jnp.maximum(jnp.maximum(l0, l1), l2)
            e0, e1, e2 = jnp.exp(l0 - m), jnp.exp(l1 - m), jnp.exp(l2 - m)
            tot = e0 + e1 + e2
            o = (e0 / tot) * o_ref[0, rows, :] + (e1 / tot) * o_ref[1, rows, :] + (e2 / tot) * o_ref[2, rows, :]
            z = pa_ref[rows, 9 * SLAB:10 * SLAB].astype(F32)
            a_ref[rows, :] = (o * (z * _sigmoid(z))).astype(BF16)
            return carry

        lax.fori_loop(0, seq // chunk, mix, 0)

    big = jax.ShapeDtypeStruct((N_GROUPS, t, 2 * SLAB), F32)
    return pl.pallas_call(
        body, name="attn_fwd", grid=(bsz, 2),
        in_specs=[pl.BlockSpec((seq, PAIR_COLS), lambda b, p: (b, p))],
        out_specs=[pl.BlockSpec((N_GROUPS, seq, SLAB), lambda b, p: (0, b, p)),
                   pl.BlockSpec((N_GROUPS, seq, SLAB), lambda b, p: (0, b, p)),
                   pl.BlockSpec((seq, SLAB), lambda b, p: (b, p))],
        out_shape=[big, big, jax.ShapeDtypeStruct((t, 2 * SLAB), BF16)],
        scratch_shapes=[pltpu.VMEM((3, seq, SLAB), BF16), pltpu.VMEM((seq, SLAB), F32),
                        pltpu.VMEM((N_GROUPS, 2, 2, SUB, 2 * SUB), F32), pltpu.VMEM((BLOCKS_PER_TRIP, 2 * SUB, 2 * SUB), F32),
                        pltpu.VMEM((BLOCKS_PER_TRIP, 2 * SUB, 2 * SUB), BF16), pltpu.VMEM((BLOCKS_PER_TRIP, 2 * SUB, SLAB), F32)],
        compiler_params=_params("arbitrary", "arbitrary"))(pa)


def _shift_down(v, k, rows):
    return jnp.where(rows >= k, pltpu.roll(v, k, 0), 0.0)


def _shift_up(v, k, rows):
    n = v.shape[0]
    return jnp.where(rows < n - k, pltpu.roll(v, n - k, 0), 0.0)


def _conv_fwd(pr, conv_w, bsz, seq, d):
    t = pr.shape[0]
    ct = CONV_TILE

    def body(p_ref, cw_ref, o_ref):
        u = p_ref[:, 2 * ct:3 * ct].astype(F32) * p_ref[:, 0:ct].astype(F32)
        cw = cw_ref[...]
        rows = lax.broadcasted_iota(jnp.int32, u.shape, 0)
        conv = cw[0:1, :] * _shift_down(u, 2, rows)
        conv = conv + cw[1:2, :] * _shift_down(u, 1, rows)
        conv = conv + cw[2:3, :] * u
        z = p_ref[:, 3 * ct:4 * ct].astype(F32)
        o_ref[...] = (p_ref[:, ct:2 * ct].astype(F32) * conv * (z * _sigmoid(z))).astype(BF16)

    return pl.pallas_call(
        body, name="conv_fwd", grid=(bsz, d // ct),
        in_specs=[pl.BlockSpec((seq, 4 * ct), lambda b, j: (b, j)), pl.BlockSpec((3, ct), lambda b, j: (0, j))],
        out_specs=pl.BlockSpec((seq, ct), lambda b, j: (b, j)),
        out_shape=jax.ShapeDtypeStruct((t, d), BF16), compiler_params=_params("parallel", "parallel"))(pr, conv_w)


def _tail(a_in, b_in, pr, x2, target2, mod3, w_pa, w_pc, w_out, b_out, ln_g, ln_b, seq, lay):
    t, d = x2.shape
    tm = 512
    per_seq = seq // tm
    n_steps = t // tm
    gate_blk = 4 * d // d

    def nt(a, b):
        return lax.dot_general(a, b, (((1,), (1,)), ((), ())), preferred_element_type=F32)

    def tn(a, b):
        return lax.dot_general(a, b, (((0,), (0,)), ((), ())), preferred_element_type=F32)

    def body(a_ref, b_ref, ga_ref, gb_ref, x_ref, tg_ref, mod_ref, wpa_ref, wpc_ref, wo_ref, bo_ref, lg_ref, lb_ref,
             dpg_ref, da_ref, db_ref, gx_ref, dgate_ref, small_ref, gwpa_hbm, gwpc_hbm, gwo_hbm,
             acc_pa, acc_pc, acc_o, sem):
        i = pl.program_id(0)

        @pl.when(i == 0)
        def _():
            acc_pa[...] = jnp.zeros_like(acc_pa)
            acc_pc[...] = jnp.zeros_like(acc_pc)
            acc_o[...] = jnp.zeros_like(acc_o)
            small_ref[...] = jnp.zeros_like(small_ref)

        @pl.when(i % per_seq == 0)
        def _():
            dgate_ref[...] = jnp.zeros_like(dgate_ref)

        halves = [slice(k * (tm // 2), (k + 1) * (tm // 2)) for k in range(2)]
        gate = mod_ref[:, 2 * d:3 * d]
        a_bf = [a_ref[rs, :] for rs in halves]
        b_bf = [b_ref[rs, :] for rs in halves]
        y_attn = [jnp.dot(a, wpa_ref[...], preferred_element_type=F32) for a in a_bf]
        y_conv = [jnp.dot(b, wpc_ref[...], preferred_element_type=F32) for b in b_bf]
        sa = [_sigmoid(ga_ref[rs, :].astype(F32)) for rs in halves]
        sb = [_sigmoid(gb_ref[rs, :].astype(F32)) for rs in halves]
        merged = [(sa[k] * y_attn[k] + sb[k] * y_conv[k]).astype(BF16) for k in range(2)]
        mo = [jnp.dot(m, wo_ref[...], preferred_element_type=F32) + bo_ref[...] for m in merged]
        d_mo_bf = []
        for k, rs in enumerate(halves):
            r = ALPHA * x_ref[rs, :] + gate * mo[k]
            mu = jnp.mean(r, axis=-1, keepdims=True)
            cen = r - mu
            var = jnp.mean(cen * cen, axis=-1, keepdims=True)
            rstd = lax.rsqrt(var + LN_EPS)
            xhat = cen * rstd
            err = xhat * lg_ref[...] + lb_ref[...] - tg_ref[rs, :]
            dy = err * (1.0 / d)
            dxhat = dy * lg_ref[...]
            dr = rstd * (dxhat - jnp.mean(dxhat, axis=-1, keepdims=True)
                         - xhat * jnp.mean(dxhat * xhat, axis=-1, keepdims=True))
            gx_ref[rs, :] = ALPHA * dr
            dgate_ref[...] += jnp.sum(dr * mo[k], axis=0, keepdims=True)
            d_mo = dr * gate
            small_ref[0:1, :] += jnp.sum(d_mo, axis=0, keepdims=True)
            small_ref[1:2, :] += jnp.sum(dy * xhat, axis=0, keepdims=True)
            small_ref[2:3, :] += jnp.sum(dy, axis=0, keepdims=True)
            small_ref[3:4, :] += jnp.sum(err * err, axis=0, keepdims=True)
            d_mo_bf.append(d_mo.astype(BF16))
        acc_o[...] += tn(jnp.concatenate(merged, axis=0), jnp.concatenate(d_mo_bf, axis=0))
        dmerged = [nt(g, wo_ref[...]) for g in d_mo_bf]
        dy_attn, dy_conv = [], []
        for k, rs in enumerate(halves):
            dy_attn.append((dmerged[k] * sa[k]).astype(BF16))
            dy_conv.append((dmerged[k] * sb[k]).astype(BF16))
            dpg_ref[rs, 0:d] = (dmerged[k] * y_attn[k] * sa[k] * (1.0 - sa[k])).astype(BF16)
            dpg_ref[rs, d:2 * d] = (dmerged[k] * y_conv[k] * sb[k] * (1.0 - sb[k])).astype(BF16)
        acc_pa[...] += tn(a_ref[...], jnp.concatenate(dy_attn, axis=0))
        acc_pc[...] += tn(b_ref[...], jnp.concatenate(dy_conv, axis=0))
        for k, rs in enumerate(halves):
            da_ref[rs, :] = nt(dy_attn[k], wpa_ref[...])
            db_ref[rs, :] = nt(dy_conv[k], wpc_ref[...])

        @pl.when(i == n_steps - 1)
        def _():
            copies = [pltpu.make_async_copy(acc_pa, gwpa_hbm, sem.at[0]), pltpu.make_async_copy(acc_pc, gwpc_hbm, sem.at[1]),
                      pltpu.make_async_copy(acc_o, gwo_hbm, sem.at[2])]
            for cp in copies:
                cp.start()
            for cp in copies:
                cp.wait()

    row = lambda w: pl.BlockSpec((tm, w), lambda i: (i, 0))
    const = lambda shp: pl.BlockSpec(shp, lambda i: (0,) * len(shp), pipeline_mode=pl.Buffered(1))
    any_spec = pl.BlockSpec(memory_space=pl.ANY)
    return pl.pallas_call(
        body, name="tail", grid=(n_steps,),
        in_specs=[row(Z_WIDTH), row(d),
                  pl.BlockSpec((tm, d), lambda i: (i, gate_blk)), pl.BlockSpec((tm, d), lambda i: (i, gate_blk + 1)),
                  row(d), row(d), pl.BlockSpec((None, 1, 3 * d), lambda i: (i // per_seq, 0, 0)),
                  const((Z_WIDTH, d)), const((d, d)), const((d, d)), const((1, d)), const((1, d)), const((1, d))],
        out_specs=[pl.BlockSpec((tm, 2 * d), lambda i: (i, lay.g0 // (2 * d))), row(Z_WIDTH), row(d), row(d),
                   pl.BlockSpec((None, 1, d), lambda i: (i // per_seq, 0, 0)), pl.BlockSpec((8, d), lambda i: (0, 0)),
                   HBM_SPEC, HBM_SPEC, HBM_SPEC],
        out_shape=[jax.ShapeDtypeStruct((t, lay.np), BF16), jax.ShapeDtypeStruct((t, Z_WIDTH), F32),
                   jax.ShapeDtypeStruct((t, d), F32), jax.ShapeDtypeStruct((t, d), F32),
                   jax.ShapeDtypeStruct((t // seq, 1, d), F32), jax.ShapeDtypeStruct((8, d), F32),
                   pltpu.HBM((Z_WIDTH, d), F32), pltpu.HBM((d, d), F32), pltpu.HBM((d, d), F32)],
        scratch_shapes=[pltpu.VMEM((Z_WIDTH, d), F32), pltpu.VMEM((d, d), F32), pltpu.VMEM((d, d), F32),
                        pltpu.SemaphoreType.DMA((3,))],
        compiler_params=_params("arbitrary"),
    )(a_in, b_in, pr, pr, x2, target2, mod3, w_pa, w_pc, w_out, b_out, ln_g, ln_b)


def _conv_bwd(dproj, db, pr, conv_w, bsz, seq, lay):
    d = lay.d
    ct = CONV_TILE
    base = lay.c0 // (4 * ct)

    def body(dp_in, db_ref, p_ref, cw_ref, dp_ref, gcw_ref):
        del dp_in
        u_x, g_b, g_c, z = [p_ref[:, k * ct:(k + 1) * ct].astype(F32) for k in range(4)]
        cw = cw_ref[...]
        u = g_c * u_x
        rows = lax.broadcasted_iota(jnp.int32, u.shape, 0)
        u1, u2 = _shift_down(u, 1, rows), _shift_down(u, 2, rows)
        conv = cw[0:1, :] * u2 + cw[1:2, :] * u1 + cw[2:3, :] * u
        sig = _sigmoid(z)
        sl = z * sig
        dbv = db_ref[...]
        gbc = g_b * conv
        dp_ref[:, ct:2 * ct] = (dbv * sl * conv).astype(BF16)
        dp_ref[:, 3 * ct:4 * ct] = (dbv * gbc * (sig * (1.0 + z * (1.0 - sig)))).astype(BF16)
        dconv = dbv * sl * g_b

        @pl.when(pl.program_id(1) == 0)
        def _():
            gcw_ref[...] = jnp.zeros_like(gcw_ref)

        gcw_ref[0:1, :] += jnp.sum(dconv * u2, axis=0, keepdims=True)
        gcw_ref[1:2, :] += jnp.sum(dconv * u1, axis=0, keepdims=True)
        gcw_ref[2:3, :] += jnp.sum(dconv * u, axis=0, keepdims=True)
        du = cw[2:3, :] * dconv + cw[1:2, :] * _shift_up(dconv, 1, rows) + cw[0:1, :] * _shift_up(dconv, 2, rows)
        dp_ref[:, 0:ct] = (du * g_c).astype(BF16)
        dp_ref[:, 2 * ct:3 * ct] = (du * u_x).astype(BF16)

    return pl.pallas_call(
        body, name="conv_bwd", grid=(d // ct, bsz),
        in_specs=[pl.BlockSpec(memory_space=pl.ANY), pl.BlockSpec((seq, ct), lambda j, b: (b, j)),
                  pl.BlockSpec((seq, 4 * ct), lambda j, b: (b, j)), pl.BlockSpec((3, ct), lambda j, b: (0, j))],
        out_specs=[pl.BlockSpec((seq, 4 * ct), lambda j, b: (b, base + j)), pl.BlockSpec((8, ct), lambda j, b: (0, j))],
        out_shape=[jax.ShapeDtypeStruct(dproj.shape, BF16), jax.ShapeDtypeStruct((8, d), F32)],
        input_output_aliases={0: 0}, compiler_params=_params("arbitrary", "arbitrary"))(dproj, db, pr, conv_w)


def _attn_bwd(dproj, pa, o_all, lse_all, da, bsz, seq, after):
    n_blocks = seq // SUB
    chunk = 256
    per_trip = BLOCKS_PER_TRIP_BWD

    def body(dp_in, pa_ref, o_ref, lse_ref, da_ref, after_ref, dp_ref, sub, stage, dsub, dog, cvec, bias_ref,
             s_buf, dp_buf, ds_buf, pb_buf, q2_buf, do2_buf, l_buf, c_buf):
        del dp_in, after_ref
        p = pl.program_id(1)
        _fill_bias(bias_ref, p, seq)
        head0 = lax.broadcasted_iota(jnp.int32, (SUB, SLAB), 1) < HEAD_DIM

        def mix_bwd(i, carry):
            rows = pl.ds(pl.multiple_of(i * chunk, chunk), chunk)
            ls = [lse_ref[g, rows, :] for g in range(N_GROUPS)]
            os_ = [o_ref[g, rows, :] for g in range(N_GROUPS)]
            m = jnp.maximum(jnp.maximum(ls[0], ls[1]), ls[2])
            es = [jnp.exp(l - m) for l in ls]
            tot = es[0] + es[1] + es[2]
            ws = [e / tot for e in es]
            o = ws[0] * os_[0] + ws[1] * os_[1] + ws[2] * os_[2]
            z = pa_ref[rows, 9 * SLAB:10 * SLAB].astype(F32)
            sig = _sigmoid(z)
            dav = da_ref[rows, :]
            do = dav * (z * sig)
            dp_ref[rows, 9 * SLAB:10 * SLAB] = (dav * o * (sig * (1.0 + z * (1.0 - sig)))).astype(BF16)
            wsum = _head_sums(do * o)
            for g in range(N_GROUPS):
                dog[g, rows, :] = ws[g] * do
                cvec[g, rows, :] = -(ws[g] * wsum)
            return carry

        lax.fori_loop(0, seq // chunk, mix_bwd, 0)

        for g in range(N_GROUPS):
            dil = DILATIONS[g]
            for w in range(3):
                _to_sub_major(pa_ref, 3 * w + g, sub.at[w], stage, dil, seq)
            dsub[1] = jnp.zeros((seq, SLAB), F32)
            dsub[2] = jnp.zeros((seq, SLAB), F32)
            nk = _key_rows(g, seq)

            def trip(i, carry, g=g, dil=dil, nk=nk):
                places = [_block_rows(per_trip * i + j, g, seq) for j in range(per_trip)]
                for j, (row0, krow0, _, nat) in enumerate(places):
                    q = sub[0, pl.ds(row0, SUB), :]
                    do = _ld_rows(dog.at[g], nat, SUB, dil).astype(BF16)
                    zero = jnp.zeros_like(q)
                    q2 = jnp.concatenate([jnp.where(head0, q, zero), jnp.where(head0, zero, q)], axis=0)
                    do2 = jnp.concatenate([jnp.where(head0, do, zero), jnp.where(head0, zero, do)], axis=0)
                    q2_buf[j] = q2
                    do2_buf[j] = do2
                    s_buf[j, :, 0:nk] = _nt(q2 * (HEAD_DIM ** -0.5), sub[1, pl.ds(krow0, nk), :])
                    dp_buf[j, :, 0:nk] = _nt(do2, sub[2, pl.ds(krow0, nk), :])
                    l_buf[j] = _ld_rows(lse_ref.at[g], nat, SUB, dil)
                    c_buf[j] = _ld_rows(cvec.at[g], nat, SUB, dil)
                for j, (_, _, bi, _) in enumerate(places):
                    for c in range(0, 2 * SUB, SOFTMAX_ROWS):
                        hh, r = divmod(c, SUB)
                        lane = hh * HEAD_DIM
                        s = s_buf[j, c:c + SOFTMAX_ROWS, 0:nk] + bias_ref[g, bi, hh, r:r + SOFTMAX_ROWS, 0:nk]
                        prob = jnp.exp(s - l_buf[j, r:r + SOFTMAX_ROWS, lane:lane + 1])
                        dprob = dp_buf[j, c:c + SOFTMAX_ROWS, 0:nk] + c_buf[j, r:r + SOFTMAX_ROWS, lane:lane + 1]
                        ds_buf[j, c:c + SOFTMAX_ROWS, 0:nk] = (prob * dprob * (HEAD_DIM ** -0.5)).astype(BF16)
                        pb_buf[j, c:c + SOFTMAX_ROWS, 0:nk] = prob.astype(BF16)
                for j, (row0, krow0, _, _) in enumerate(places):
                    ds = ds_buf[j, :, 0:nk]
                    dq2 = jnp.dot(ds, sub[1, pl.ds(krow0, nk), :], preferred_element_type=F32)
                    dsub[0, pl.ds(row0, SUB), :] = jnp.where(head0, dq2[0:SUB], dq2[SUB:2 * SUB])
                    dsub[1, pl.ds(krow0, nk), :] += _tn(ds, q2_buf[j])
                    dsub[2, pl.ds(krow0, nk), :] += _tn(pb_buf[j, :, 0:nk], do2_buf[j])
                return carry

            lax.fori_loop(0, n_blocks // per_trip, trip, 0)
            for w in range(3):
                cols = slice((3 * w + g) * SLAB, (3 * w + g + 1) * SLAB)
                if dil == 1:
                    dp_ref[:, cols] = dsub[w].astype(BF16)
                else:
                    n = seq // dil
                    for r in range(dil):
                        stage[pl.ds(r, n, stride=dil), :] = dsub[w, pl.ds(r * n, n), :]
                    dp_ref[:, cols] = stage[...].astype(BF16)

    return pl.pallas_call(
        body, name="attn_bwd", grid=(bsz, 2),
        in_specs=[pl.BlockSpec(memory_space=pl.ANY), pl.BlockSpec((seq, PAIR_COLS), lambda b, p: (b, p)),
                  pl.BlockSpec((N_GROUPS, seq, SLAB), lambda b, p: (0, b, p)),
                  pl.BlockSpec((N_GROUPS, seq, SLAB), lambda b, p: (0, b, p)),
                  pl.BlockSpec((seq, SLAB), lambda b, p: (b, p)), pl.BlockSpec(memory_space=pl.ANY)],
        out_specs=pl.BlockSpec((seq, PAIR_COLS), lambda b, p: (b, p)),
        out_shape=jax.ShapeDtypeStruct(dproj.shape, BF16), input_output_aliases={0: 0},
        scratch_shapes=[pltpu.VMEM((3, seq, SLAB), BF16), pltpu.VMEM((seq, SLAB), F32), pltpu.VMEM((3, seq, SLAB), F32),
                        pltpu.VMEM((3, seq, SLAB), F32), pltpu.VMEM((3, seq, SLAB), F32),
                        pltpu.VMEM((N_GROUPS, 2, 2, SUB, 2 * SUB), F32),
                        pltpu.VMEM((per_trip, 2 * SUB, 2 * SUB), F32), pltpu.VMEM((per_trip, 2 * SUB, 2 * SUB), F32),
                        pltpu.VMEM((per_trip, 2 * SUB, 2 * SUB), BF16), pltpu.VMEM((per_trip, 2 * SUB, 2 * SUB), BF16),
                        pltpu.VMEM((per_trip, 2 * SUB, SLAB), BF16), pltpu.VMEM((per_trip, 2 * SUB, SLAB), BF16),
                        pltpu.VMEM((per_trip, SUB, SLAB), F32), pltpu.VMEM((per_trip, SUB, SLAB), F32)],
        compiler_params=_params("arbitrary", "arbitrary"))(dproj, pa, o_all, lse_all, da, after)


def _grad_h(dproj, w_all, gx0, x2, mod3, seq, lay):
    t, d = x2.shape
    tm, tn = min(512, seq), min(512, d)
    per_seq = seq // tm

    def body(dp_ref, w_ref, gx0_ref, x_ref, scale_ref, gx_ref, dmod_ref):
        dh = _nt(dp_ref[:, 0:ATT], w_ref[:, 0:ATT]) + _nt(dp_ref[:, lay.c0:], w_ref[:, lay.c0:])
        gx_ref[...] = gx0_ref[...] + dh * (1.0 + scale_ref[...])

        @pl.when(pl.program_id(1) % per_seq == 0)
        def _():
            dmod_ref[...] = jnp.zeros_like(dmod_ref)

        dmod_ref[0:1, :] += jnp.sum(dh, axis=0, keepdims=True)
        dmod_ref[1:2, :] += jnp.sum(dh * x_ref[...], axis=0, keepdims=True)

    tile = pl.BlockSpec((tm, tn), lambda j, i: (i, j))
    return pl.pallas_call(
        body, name="grad_h", grid=(d // tn, t // tm),
        in_specs=[pl.BlockSpec((tm, lay.np), lambda j, i: (i, 0)), pl.BlockSpec((tn, lay.np), lambda j, i: (j, 0)),
                  tile, tile,
                  pl.BlockSpec((None, 1, tn), lambda j, i: (i // per_seq, 0, d // tn + j))],
        out_specs=[tile, pl.BlockSpec((None, 8, tn), lambda j, i: (i // per_seq, 0, j))],
        out_shape=[jax.ShapeDtypeStruct((t, d), F32), jax.ShapeDtypeStruct((t // seq, 8, d), F32)],
        compiler_params=_params("arbitrary", "arbitrary"))(dproj, w_all, gx0, x2, mod3)


def _grad_w_in(ht, dproj, seq, lay, part, prev):
    d, t = ht.shape
    tm = min(t, 2 * seq)
    n_i = t // tm

    def side_by_side(pieces):
        return pieces[0] if len(pieces) == 1 else jnp.concatenate(pieces, axis=1)

    def make_body(n_skip, n_pieces, tn, nat_tile, assemble):
        def body(*refs):
            refs = refs[n_skip:]
            ht_ref, dp_refs = refs[0], refs[1:1 + n_pieces]
            gw_hbm, gb_hbm, acc, bacc, gw_out, gb_out, sem = refs[1 + n_pieces:]
            i, j = pl.program_id(0), pl.program_id(1)
            dp = assemble([r[...] for r in dp_refs])
            part = jnp.dot(ht_ref[...], dp, preferred_element_type=F32)
            bpart = jnp.sum(dp.astype(F32), axis=0, keepdims=True)

            if n_i > 1:
                @pl.when(i == 0)
                def _():
                    acc[j] = part
                    bacc[j] = bpart

                @pl.when((i > 0) & (i < n_i - 1))
                def _():
                    acc[j] += part
                    bacc[j] += bpart

            @pl.when(i == n_i - 1)
            def _():
                gw_out[...] = ((part + acc[j]) if n_i > 1 else part).astype(BF16)
                gb_out[...] = (bpart + bacc[j]) if n_i > 1 else bpart
                cols = pl.ds(pl.multiple_of(nat_tile(j) * tn, SLAB), tn)
                copies = [pltpu.make_async_copy(gw_out, gw_hbm.at[:, cols], sem.at[0]),
                          pltpu.make_async_copy(gb_out, gb_hbm.at[:, cols], sem.at[1])]
                for cp in copies:
                    cp.start()
                for cp in copies:
                    cp.wait()
        return body

    def call(name, pieces, n_tiles, nat_tile, prev, assemble=side_by_side):
        tn = sum(w for w, _ in pieces)
        any_spec = pl.BlockSpec(memory_space=pl.ANY)
        in_specs = [pl.BlockSpec((d, tm), lambda i, j: (0, i))]
        in_specs += [pl.BlockSpec((tm, w), lambda i, j, f=f: (i, f(j))) for w, f in pieces]
        args = [ht] + [dproj] * len(pieces)
        aliases = {}
        if prev is not None:
            in_specs = [any_spec] * 2 + in_specs
            args = list(prev) + args
            aliases = {0: 0, 1: 1}
        return pl.pallas_call(
            make_body(0 if prev is None else 2, len(pieces), tn, nat_tile, assemble), name=name, grid=(n_i, n_tiles),
            in_specs=in_specs,
            out_specs=[any_spec, any_spec],
            out_shape=[jax.ShapeDtypeStruct((d, lay.din), BF16), jax.ShapeDtypeStruct((1, lay.din), F32)],
            input_output_aliases=aliases,
            scratch_shapes=[pltpu.VMEM((n_tiles if n_i > 1 else 1, d, tn), F32), pltpu.VMEM((n_tiles, 1, tn), F32),
                            pltpu.VMEM((d, tn), BF16),
                            pltpu.VMEM((1, tn), F32), pltpu.SemaphoreType.DMA((2,))],
            compiler_params=_params("arbitrary", "arbitrary"))(*args)

    if part == "attn":
        pair_blocks = PAIR_COLS // (2 * SLAB)
        attn_pieces = [(2 * SLAB, lambda j, p=p: p * pair_blocks + j) for p in range(2)]

        def interleave(pieces):
            a, b = pieces
            return jnp.concatenate([a[:, :SLAB], b[:, :SLAB], a[:, SLAB:], b[:, SLAB:]], axis=1)

        return call("grad_w_in_attn", attn_pieces, ATT // 512, lambda j: j, prev, interleave)
    base = lay.c0 // CONV_TILE
    nct = lay.n_conv_tiles
    if nct % 2:
        return call("grad_w_in_rest", [(CONV_TILE, lambda j: base + j)], 6 * d // CONV_TILE, lay.rest_nat_tile, prev)
    half = nct // 2

    def rest_piece(m):
        def perm_tile(j):
            conv = base + 4 * (2 * (j % half) + m) + j // half
            return jnp.where(j < 4 * half, conv, base + 2 * j + m)
        return (CONV_TILE, perm_tile)

    return call("grad_w_in_rest", [rest_piece(0), rest_piece(1)], 6 * d // 512, lambda j: ATT // 512 + j, prev)


def _pack_rows(parts, width=128):
    flat = [p.reshape(-1) for p in parts]
    spans, rows = [], 0
    padded = []
    for f in flat:
        n = -(-f.shape[0] // (8 * width)) * 8
        padded.append(jnp.pad(f, (0, n * width - f.shape[0])).reshape(n, width))
        spans.append((rows, f.shape[0]))
        rows += n
    return jnp.concatenate(padded, axis=0), spans


def _unpack_rows(packed, spans, shapes, width=128):
    out = []
    for (row, n), shp in zip(spans, shapes):
        rows = -(-n // width)
        out.append(packed[row:row + rows].reshape(-1)[:n].reshape(shp))
    return out


def kernel(x, c, w_ada, b_ada, w_in, b_in, conv_w, w_proj_attn, w_proj_conv, w_out, b_out, ln_g, ln_b, loss_target, m_w_ada, m_b_ada, m_w_in, m_b_in, m_conv_w, m_w_proj_attn, m_w_proj_conv, m_w_out, m_b_out, m_ln_g, m_ln_b, v_w_ada, v_b_ada, v_w_in, v_b_in, v_conv_w, v_w_proj_attn, v_w_proj_conv, v_w_out, v_b_out, v_ln_g, v_ln_b):
    bsz, seq, d = x.shape
    t = bsz * seq
    lay = _Layout(d)
    col_sharded = [True, True, False, False]
    red_w = [w_in[0], w_proj_attn[0], w_proj_conv[0], w_out[0]]
    chip = 2 * lax.axis_index("x") + lax.axis_index("y")
    chip1 = chip.astype(jnp.int32).reshape(1)
    core1 = lax.axis_index("c").astype(jnp.int32).reshape(1)
    place = jnp.stack([chip, lax.axis_index("c")]).astype(jnp.int32)
    x2 = x.reshape(t, d)
    target2 = loss_target.reshape(t, d)

    mod, act_all = _ada_exchange(c, w_ada[0], b_ada)
    mod3 = mod.reshape(bsz, 1, 3 * d)

    cw_pad = jnp.pad(conv_w[0], ((0, 5), (0, 0))) + 0.0 * mod[0, 0]
    own_in_full = [_cast_into_full(red_w[0], col_sharded[0], chip1, "cast_shard_in")]
    own_in_full += list(_cast_into_full_small(red_w[1:], col_sharded[1:], chip1, "cast_shards_late"))
    (wi_f,), cw8 = _gather_weights(own_in_full[:1], col_sharded[:1], cw_pad)
    cw_full = cw8[0:3]
    late_copies = _direct_gather_copies(col_sharded[1:])
    late_send, late_recv, late_flying, late_token = _start_copies("gather_late_start", own_in_full[1:], (18,), cw8, late_copies)
    w_all = _permute_w_in(wi_f, lay)
    b_all = lay.perm_vector(b_in) + late_token[0, 0]

    rest_tn = 1024 if (6 * d) % 1024 == 0 else 512
    pa, = _project(x2, mod3, w_all, b_all, seq, 0, ATT, PAIR_COLS, BF16, False, "project_attn")
    pr, ht = _project(x2, mod3, w_all, b_all, seq, lay.c0, 6 * d, rest_tn, BF16, True, "project_rest")
    o_all, lse_all, a_in = _attn_fwd(pa, bsz, seq)
    b_in_act = _conv_fwd(pr, cw_full, bsz, seq, d)
    wpa_f, wpc_f, wo_f = _wait_copies("gather_late_wait", late_flying, late_send, late_recv, b_in_act, late_copies)
    (dproj, da_in, db_in, gx0, dgate, small_tail, gw_pa, gw_pc, gw_out) = _tail(
        a_in, b_in_act, pr, x2, target2, mod3, wpa_f, wpc_f, wo_f, b_out, ln_g, ln_b, seq, lay)

    late_views = _shard_views([gw_pa, gw_pc, gw_out], col_sharded[1:])
    late_lands = [lax.empty((v.shape[0], v.shape[1] // 2, v.shape[2]), v.dtype) for v in late_views]
    xl_send, xl_recv, xl_fly, xl_tok = _start_copies("grads_pair_exchange_late_start", late_views + late_lands, (3,), small_tail,
                                                     _pair_exchange_copies(3, False), pair_id=PAIR_ID_LATE)
    dproj, gcw = _conv_bwd(dproj, db_in, pr, cw_full + xl_tok[0, 0], bsz, seq, lay)
    xl_done = _wait_copies("grads_pair_exchange_late_wait", xl_fly, xl_send, xl_recv, gcw, _pair_exchange_copies(3, True))
    late_parts = list(_pair_sum_small(xl_done[:3], xl_done[3:], core1, "grads_pair_sum_late"))

    late_cross = _chip_scatter_copies(3, col_sharded[1:])
    late_zone = [lax.empty((3, p.shape[1], _piece_cols(p, cs)), p.dtype) for p, cs in zip(late_parts, col_sharded[1:])]
    sl_send, sl_recv, sl_fly, sl_tok = _start_copies("grads_scatter_late_start", late_parts + late_zone, (9,), core1,
                                                     late_cross)
    dproj = _attn_bwd(dproj, pa, o_all, lse_all, da_in, bsz, seq, sl_tok)
    gw_in_bf, gb_in = _grad_w_in(ht, dproj, seq, lay, "rest", None)
    rest_cols, attn_cols = (ATT, 6 * d), (0, ATT)
    in_land = lax.empty((1, d // 2, lay.din), BF16)
    xi_copies = _pair_exchange_copies(1, False, rest_cols)
    xi_send, xi_recv, xi_fly, _ = _start_copies("grads_pair_exchange_in_start", [gw_in_bf.reshape(1, d, lay.din), in_land], (1,),
                                                gb_in, xi_copies, pair_id=PAIR_ID_IN)
    gw_in_bf, gb_in = _grad_w_in(ht, dproj, seq, lay, "attn", (xi_fly[0].reshape(d, lay.din), gb_in))
    xi_done = _wait_copies("grads_pair_exchange_in_wait", [gw_in_bf.reshape(1, d, lay.din), xi_fly[1]], xi_send, xi_recv, gb_in,
                           _pair_exchange_copies(1, True, rest_cols))
    in_got = _pair_exchange_into(xi_done[0], xi_done[1], attn_cols, "grads_pair_exchange_in_attn")
    sl_done = _wait_copies("grads_scatter_late_wait", sl_fly, sl_send, sl_recv, in_got, late_cross)
    late_red = list(_chip_sum_small(sl_done[:3], sl_done[3:], col_sharded[1:], place, "grads_chip_sum_late"))

    in_part = _pair_sum(xi_done[0], in_got, core1, "grads_pair_sum_in")
    in_cross = _chip_scatter_copies(1, col_sharded[:1])
    in_zone = [lax.empty((3, in_part.shape[1], _piece_cols(in_part, True)), in_part.dtype)]
    late_views1 = [f.reshape(1, *f.shape) for f in late_red]
    si_send, si_recv, si_fly, si_tok = _start_copies(
        "grads_scatter_in_join_late_start", [in_part] + in_zone + late_views1, (6,), core1,
        _both_copies(in_cross, 2, _pair_join_copies(3, False, sem0=3)))
    grad_x2, dmod = _grad_h(dproj, w_all, gx0, x2, mod3 + si_tok[0, 0], seq, lay)
    si_done = _wait_copies("grads_scatter_in_join_late_wait", si_fly, si_send, si_recv, grad_x2,
                           _both_copies(in_cross, 2, _pair_join_copies(3, True, sem0=3)))
    late_joined = [f[0] for f in si_done[2:]]
    in_red = _chip_sum(si_done[0], si_done[1], True, place, "grads_chip_sum_in")

    d_ada = jnp.concatenate([dmod[:, 0, :], dmod[:, 1, :], dgate[:, 0, :]], axis=1)
    pieces = [small_tail[3], jnp.sum(d_ada, axis=0), gb_in[0], small_tail[0], small_tail[1], small_tail[2], gcw[0:3]]
    packed, spans = _pack_rows(pieces)
    kept_in, _ = _pack_rows([d_ada])
    rows_all = jnp.concatenate([packed, kept_in], axis=0)
    small_land = lax.empty((N_DEV,) + rows_all.shape, F32)
    sm_send, sm_recv, sm_fly, sm_tok = _start_copies(
        "small_gather_join_in_start", [rows_all, small_land, in_red.reshape(1, *in_red.shape)], (N_DEV,), core1,
        _both_copies(_small_gather_copies, 2, _pair_join_copies(1, False, sem0=N_DEV - 1)))
    big_w = [w_ada[0]] + red_w
    big_m = [m_w_ada[0], m_w_in[0], m_w_proj_attn[0], m_w_proj_conv[0], m_w_out[0]]
    big_v = [v_w_ada[0], v_w_in[0], v_w_proj_attn[0], v_w_proj_conv[0], v_w_out[0]]
    big_out = [None] * 5
    big_out[2:5] = _adamw_together(big_w[2:], late_joined, big_m[2:], big_v[2:], "adamw_late", sm_tok)
    sm_done = _wait_copies("small_gather_join_in_wait", sm_fly, sm_send, sm_recv, big_out[4][0],
                           _both_copies(_small_gather_copies, 2, _pair_join_copies(1, True, sem0=N_DEV - 1)))
    me1 = (4 * lax.axis_index("x") + 2 * lax.axis_index("y") + lax.axis_index("c")).astype(jnp.int32).reshape(1)
    summed, kept = _small_sum(sm_done[0], sm_done[1], me1, packed.shape[0], d)
    loss = summed[0, 0]
    _, g_b_ada, g_b_in, g_b_out, g_ln_g, g_ln_b, g_cw_full = _unpack_rows(
        summed, spans, [(d,), (3 * d,), (lay.din,), (d,), (d,), (d,), (3, d)])
    g_cw = lax.dynamic_slice(g_cw_full, (0, chip * (d // N_CHIPS)), (3, d // N_CHIPS))
    d_ada_all = kept.reshape(N_DEV, -1)[:, :bsz * 3 * d].reshape(N_DEV * bsz, 3 * d)
    ada_cols = 3 * d // N_CHIPS
    d_ada_cols = lax.dynamic_slice(d_ada_all, (0, chip * ada_cols), (N_DEV * bsz, ada_cols))

    g_w_ada, *big_out[0] = _grad_and_adamw_w_ada(act_all.T, d_ada_cols, big_w[0], big_m[0], big_v[0])
    small_w = [b_ada, b_in, conv_w[0], b_out, ln_g, ln_b]
    small_g = [g_b_ada, g_b_in, g_cw, g_b_out, g_ln_g, g_ln_b]
    small_m = [m_b_ada, m_b_in, m_conv_w[0], m_b_out, m_ln_g, m_ln_b]
    small_v = [v_b_ada, v_b_in, v_conv_w[0], v_b_out, v_ln_g, v_ln_b]
    pw, sp = _pack_rows(small_w)
    pg, _ = _pack_rows(small_g)
    pm, _ = _pack_rows(small_m)
    pv, _ = _pack_rows(small_v)
    sd, sm, sv = _adamw(pw, pg, pm, pv, "adamw_small", None)
    big_out[1] = _adamw(big_w[1], sm_done[2][0], big_m[1], big_v[1], "adamw_1", None, also_g=True)
    g_big = [g_w_ada, big_out[1][3]] + late_joined
    shapes = [a.shape for a in small_w]
    sd, sm, sv = _unpack_rows(sd, sp, shapes), _unpack_rows(sm, sp, shapes), _unpack_rows(sv, sp, shapes)

    def order(wa, bA, wi, bI, cw, wpa, wpc, wo, bO, lg, lb):
        return (wa[None], bA, wi[None], bI, cw[None], wpa[None], wpc[None], wo[None], bO, lg, lb)

    sg = [g.reshape(s) for g, s in zip(small_g, shapes)]
    grads_out = order(g_big[0], sg[0], g_big[1], sg[1], sg[2], g_big[2], g_big[3], g_big[4], sg[3], sg[4], sg[5])
    outs = []
    for idx, small in enumerate((sd, sm, sv)):
        outs.append(order(big_out[0][idx], small[0], big_out[1][idx], small[1], small[2], big_out[2][idx],
                          big_out[3][idx], big_out[4][idx], small[3], small[4], small[5]))
    return (loss, grad_x2.reshape(bsz, seq, d), *grads_out, *outs[0], *outs[1], *outs[2])
```

```python
import jax
import jax.numpy as jnp
from jax import lax
from jax.experimental import pallas as pl
from jax.experimental.pallas import tpu as pltpu

F32 = jnp.float32
BF16 = jnp.bfloat16
MESH = pl.DeviceIdType.MESH

HEAD_DIM = 64
N_GROUPS = 3
DILATIONS = (1, 4, 16)
N_HEADS = 12
SUB = 128
Q_WIDTH = 768
Z_WIDTH = 256
ATT = 3 * Q_WIDTH + Z_WIDTH
SLAB = 128
PAIR_SLABS = 10
PAIR_COLS = PAIR_SLABS * SLAB
CONV_TILE = 256
SOFTMAX_ROWS = 32
BLOCKS_PER_TRIP = 8
BLOCKS_PER_TRIP_BWD = 16
ALIBI_MAX_EXP = 8.0
ALPHA = 2.0 ** 0.25
LN_EPS = 1e-5
ADAM_LR, ADAM_B1, ADAM_B2, ADAM_EPS, ADAM_WD, ADAM_STEP = 0.001, 0.9, 0.999, 1e-08, 0.01, 10
N_CHIPS = 4
N_DEV = 8
VMEM_LIMIT_V7X = 60 * 1024 * 1024
NEG = -1e30


def _params(*sem):
    return pltpu.CompilerParams(dimension_semantics=sem, vmem_limit_bytes=VMEM_LIMIT_V7X)


def _sigmoid(v):
    return 0.5 * jnp.tanh(0.5 * v) + 0.5


class _Layout:
    def __init__(self, d):
        self.d = d
        self.din = ATT + 6 * d
        c0 = 3072
        while c0 % (2 * d):
            c0 += 1024
        self.c0, self.g0, self.np = c0, c0 + 4 * d, c0 + 6 * d
        self.n_conv_tiles = d // CONV_TILE

    def attn_nat_slab(self, s):
        p, i = s // PAIR_SLABS, s % PAIR_SLABS
        return jnp.where(i < 9, (i // 3) * 6 + (i % 3) * 2 + p, 18 + p)

    def rest_nat_tile(self, t):
        n4 = 4 * self.n_conv_tiles
        conv = ATT // CONV_TILE + (t % 4) * self.n_conv_tiles + t // 4
        return jnp.where(t < n4, conv, ATT // CONV_TILE + t)

    def perm_vector(self, v):
        parts = []
        for s in range(2 * PAIR_SLABS):
            p, i = divmod(s, PAIR_SLABS)
            ns = (i // 3) * 6 + (i % 3) * 2 + p if i < 9 else 18 + p
            parts.append(v[:, ns * SLAB:(ns + 1) * SLAB])
        parts.append(jnp.zeros((1, self.c0 - ATT), v.dtype))
        for j in range(self.n_conv_tiles):
            for k in range(4):
                a = ATT + k * self.d + j * CONV_TILE
                parts.append(v[:, a:a + CONV_TILE])
        parts.append(v[:, ATT + 4 * self.d:])
        return jnp.concatenate(parts, axis=1)


def _place():
    return lax.axis_index("x"), lax.axis_index("y"), lax.axis_index("c")


def _other_chips(x, y):
    return [(1 - x, y), (x, 1 - y), (1 - x, 1 - y)]


def _shard_of(ref, col_sharded, chip, half=None):
    if col_sharded:
        cs = ref.shape[1] // N_CHIPS
        cols = pl.ds(pl.multiple_of(chip * cs, SLAB), cs)
        if half is None:
            return ref.at[:, cols]
        n = ref.shape[0] // 2
        return ref.at[pl.ds(half * n, n), cols]
    rs = ref.shape[0] // N_CHIPS
    if half is None:
        return ref.at[pl.ds(chip * rs, rs)]
    return ref.at[pl.ds(chip * rs + half * (rs // 2), rs // 2)]


GATHER_PIECES = 8


def _cast_into_full(shard, col_sharded, chip, name):
    rows, cols = shard.shape
    tr = _row_tile(rows, cols)
    nb = rows // tr

    def body(chip_ref, s_ref, o_ref):
        del chip_ref
        o_ref[...] = s_ref[...].astype(BF16)

    if col_sharded:
        full, out_spec = (rows, cols * N_CHIPS), pl.BlockSpec((tr, cols), lambda i, ch: (i, ch[0]))
    else:
        full, out_spec = (rows * N_CHIPS, cols), pl.BlockSpec((tr, cols), lambda i, ch: (ch[0] * nb + i, 0))
    return pl.pallas_call(
        body, name=name,
        grid_spec=pltpu.PrefetchScalarGridSpec(num_scalar_prefetch=1, grid=(nb,),
                                               in_specs=[pl.BlockSpec((tr, cols), lambda i, ch: (i, 0))], out_specs=out_spec),
        out_shape=jax.ShapeDtypeStruct(full, BF16), compiler_params=_params("parallel"))(chip, shard)


def _cast_into_full_small(shards, col_sharded, chip, name):
    n = len(shards)

    def body(chip_ref, *refs):
        del chip_ref
        for w in range(n):
            refs[n + w][...] = refs[w][...].astype(BF16)

    in_specs = [pl.BlockSpec(s.shape, lambda i, ch: (0, 0)) for s in shards]
    out_specs = [pl.BlockSpec(s.shape, (lambda i, ch: (0, ch[0])) if cs else (lambda i, ch: (ch[0], 0)))
                 for s, cs in zip(shards, col_sharded)]
    fulls = [(s.shape[0], s.shape[1] * N_CHIPS) if cs else (s.shape[0] * N_CHIPS, s.shape[1]) for s, cs in zip(shards, col_sharded)]
    return pl.pallas_call(
        body, name=name,
        grid_spec=pltpu.PrefetchScalarGridSpec(num_scalar_prefetch=1, grid=(1,), in_specs=in_specs, out_specs=out_specs),
        out_shape=[jax.ShapeDtypeStruct(f, BF16) for f in fulls], compiler_params=_params("arbitrary"))(chip, *shards)


def _gather_weights(fulls, col_sharded, small):
    n = len(fulls)
    kp = GATHER_PIECES

    def piece(ref, cs, chip, half, k):
        if cs:
            width = ref.shape[1] // N_CHIPS
            rows = ref.shape[0] // 2 // kp
            return ref.at[pl.ds(half * (ref.shape[0] // 2) + k * rows, rows), pl.ds(pl.multiple_of(chip * width, SLAB), width)]
        rs = ref.shape[0] // N_CHIPS
        rows = rs // 2 // kp
        return ref.at[pl.ds(chip * rs + half * (rs // 2) + k * rows, rows)]

    def body(*refs):
        sm_in, outs, sm_out = refs[n], refs[n + 1:2 * n + 1], refs[2 * n + 1]
        send, recv, fsend, frecv, lsem, ssend, srecv = refs[2 * n + 2:]
        x, y, c = _place()
        mine = 2 * x + y
        sibling = (x, y, 1 - c)
        first = (x ^ (1 - c), y ^ c)
        second = (x ^ c, y ^ (1 - c))
        diagonal = (1 - x, 1 - y)
        sources = [first, second, diagonal]
        senders = [first, second, second]

        def copy(ref, sems, slot, to):
            return pltpu.make_async_remote_copy(src_ref=ref, dst_ref=ref, send_sem=sems[0].at[slot], recv_sem=sems[1].at[slot],
                                                device_id=to, device_id_type=MESH)

        local = pltpu.make_async_copy(sm_in, _shard_of(sm_out, True, mine), lsem)
        local.start()
        sends = []
        for k, (cx, cy) in enumerate(_other_chips(x, y)):
            cp = pltpu.make_async_remote_copy(src_ref=sm_in, dst_ref=_shard_of(sm_out, True, mine), send_sem=ssend.at[k],
                                              recv_sem=srecv.at[k], device_id=(cx, cy, c), device_id_type=MESH)
            cp.start()
            sends.append(cp)
        for k in range(kp):
            for w in range(n):
                own = piece(outs[w], col_sharded[w], mine, c, k)
                for slot, chip in enumerate((first, second)):
                    cp = copy(own, (send, recv), (w * 3 + slot) * kp + k, (*chip, c))
                    cp.start()
                    sends.append(cp)
        for slot in range(3):
            source = 2 * sources[slot][0] + sources[slot][1]
            for k in range(kp):
                for w in range(n):
                    landed = piece(outs[w], col_sharded[w], source, c, k)
                    copy(landed, (send, recv), (w * 3 + slot) * kp + k, (*senders[slot], c)).wait_recv()
                    if slot == 0:
                        cp = copy(landed, (send, recv), (w * 3 + 2) * kp + k, (*second, c))
                        cp.start()
                        sends.append(cp)
                    cp = copy(landed, (fsend, frecv), (w * 3 + slot) * kp + k, sibling)
                    cp.start()
                    sends.append(cp)
        for slot, chip in enumerate((second, first, diagonal)):
            for k in range(kp):
                for w in range(n):
                    passed = piece(outs[w], col_sharded[w], 2 * chip[0] + chip[1], 1 - c, k)
                    copy(passed, (fsend, frecv), (w * 3 + slot) * kp + k, sibling).wait_recv()
        for k, (cx, cy) in enumerate(_other_chips(x, y)):
            theirs = _shard_of(sm_out, True, 2 * cx + cy)
            pltpu.make_async_remote_copy(src_ref=theirs, dst_ref=theirs, send_sem=ssend.at[k], recv_sem=srecv.at[k],
                                         device_id=(cx, cy, c), device_id_type=MESH).wait_recv()
        for cp in sends:
            cp.wait_send()
        local.wait()

    any_spec = pl.BlockSpec(memory_space=pl.ANY)
    outs = pl.pallas_call(
        body, name="gather_weights",
        out_shape=[jax.ShapeDtypeStruct(f.shape, BF16) for f in fulls]
        + [jax.ShapeDtypeStruct((small.shape[0], small.shape[1] * N_CHIPS), small.dtype)],
        in_specs=[any_spec] * (n + 1), out_specs=[any_spec] * (n + 1), input_output_aliases={w: w for w in range(n)},
        scratch_shapes=[pltpu.SemaphoreType.DMA((n * 3 * kp,)), pltpu.SemaphoreType.DMA((n * 3 * kp,)),
                        pltpu.SemaphoreType.DMA((n * 3 * kp,)), pltpu.SemaphoreType.DMA((n * 3 * kp,)), pltpu.SemaphoreType.DMA,
                        pltpu.SemaphoreType.DMA((3,)), pltpu.SemaphoreType.DMA((3,))],
    )(*fulls, small)
    return outs[:n], outs[n]


PAIR_ID_LATE, PAIR_ID_IN, PAIR_ID_INTO = 1, 2, 3
CHIPS_ID_LATE = 4
HBM_SPEC = pl.BlockSpec(memory_space=pltpu.HBM)
SEM_SPEC = pl.BlockSpec(memory_space=pltpu.SEMAPHORE)
DATAFLOW = pltpu.SideEffectType.DATAFLOW_SIDE_EFFECTING


def _sibling_handshake():
    x, y, c = _place()
    barrier = pltpu.get_barrier_semaphore()
    pl.semaphore_signal(barrier, inc=1, device_id=(x, y, 1 - c), device_id_type=MESH)
    pl.semaphore_wait(barrier, 1)


def _chips_handshake():
    x, y, c = _place()
    barrier = pltpu.get_barrier_semaphore()
    for cx, cy in _other_chips(x, y):
        pl.semaphore_signal(barrier, inc=1, device_id=(cx, cy, c), device_id_type=MESH)
    pl.semaphore_wait(barrier, 3)


def _start_copies(name, arrays, sem_shape, after, copies, pair_id=None, chips_id=None):
    n = len(arrays)

    def body(*refs):
        if pair_id is not None:
            _sibling_handshake()
        if chips_id is not None:
            _chips_handshake()
        for cp in copies(refs[:n], refs[n + 1], refs[n + 2]):
            cp.start()
        token = refs[2 * n + 3]
        token[...] = jnp.zeros_like(token)

    res = pl.pallas_call(
        body, name=name,
        out_shape=(pltpu.SemaphoreType.DMA(sem_shape), pltpu.SemaphoreType.DMA(sem_shape),
                   *[pltpu.HBM(a.shape, a.dtype) for a in arrays], jax.ShapeDtypeStruct((8, 128), F32)),
        in_specs=[HBM_SPEC] * n + [pl.BlockSpec(memory_space=pl.ANY)],
        out_specs=(SEM_SPEC, SEM_SPEC, *([HBM_SPEC] * n), pl.BlockSpec(memory_space=pltpu.VMEM)),
        input_output_aliases={i: 2 + i for i in range(n)},
        compiler_params=pltpu.CompilerParams(has_side_effects=DATAFLOW, collective_id=pair_id if chips_id is None else chips_id),
    )(*[pltpu.with_memory_space_constraint(a, pltpu.HBM) for a in arrays], after)
    return res[0], res[1], list(res[2:2 + n]), res[2 + n]


def _wait_copies(name, arrays, send, recv, after, copies):
    n = len(arrays)

    def body(*refs):
        for cp in copies(refs[:n], refs[n], refs[n + 1]):
            cp.wait_send()
            cp.wait_recv()

    return pl.pallas_call(
        body, name=name, out_shape=[pltpu.HBM(a.shape, a.dtype) for a in arrays],
        in_specs=[HBM_SPEC] * n + [SEM_SPEC, SEM_SPEC, pl.BlockSpec(memory_space=pl.ANY)], out_specs=[HBM_SPEC] * n,
        input_output_aliases={i: i for i in range(n)},
        compiler_params=pltpu.CompilerParams(has_side_effects=DATAFLOW),
    )(*arrays, send, recv, after)


def _direct_gather_copies(col_sharded):
    def copies(refs, send, recv):
        x, y, c = _place()
        mine = 2 * x + y
        out = []
        for w, ref in enumerate(refs):
            own_half = _shard_of(ref, col_sharded[w], mine, c)
            k = 0
            for cx, cy in _other_chips(x, y):
                for pc in (c, 1 - c):
                    out.append(pltpu.make_async_remote_copy(
                        src_ref=own_half, dst_ref=own_half, send_sem=send.at[6 * w + k], recv_sem=recv.at[6 * w + k],
                        device_id=(cx, cy, pc), device_id_type=MESH))
                    k += 1
        return out
    return copies


def _chip_scatter_copies(n, col_sharded):
    def piece(ref, cs, chip):
        if cs:
            w = ref.shape[2] // N_CHIPS
            return ref.at[:, :, pl.ds(pl.multiple_of(chip * w, SLAB), w)]
        return ref.at[pl.ds(chip, 1)]

    def copies(refs, send, recv):
        x, y, c = _place()
        out = []
        for k, (cx, cy) in enumerate(_other_chips(x, y)):
            for w in range(n):
                out.append(pltpu.make_async_remote_copy(
                    src_ref=piece(refs[w], col_sharded[w], 2 * cx + cy), dst_ref=refs[n + w].at[pl.ds(k, 1)],
                    send_sem=send.at[3 * w + k], recv_sem=recv.at[3 * w + k], device_id=(cx, cy, c), device_id_type=MESH))
        return out
    return copies


def _shard_views(gs, col_sharded):
    return [g.reshape(1, *g.shape) if cs else g.reshape(N_CHIPS, g.shape[0] // N_CHIPS, g.shape[1])
            for g, cs in zip(gs, col_sharded)]


DMA_CHUNK_BYTES = 1 << 20


def _chunk_rows(shape, itemsize):
    s, rows, cols = shape
    n = 1
    while s * (rows // n) * cols * itemsize > DMA_CHUNK_BYTES and (rows // n) % 32 == 0:
        n *= 2
    return rows // n


def _rows_of(ref, row0, rows, cols):
    if cols is None:
        return ref.at[:, pl.ds(row0, rows)]
    return ref.at[:, pl.ds(row0, rows), pl.ds(cols[0], cols[1])]


def _row_pieces(src, src_row0, dst, dst_row0, rows, send_sem, recv_sem, device, cols=None):
    width = src.shape[2] if cols is None else cols[1]
    step = _chunk_rows((src.shape[0], rows, width), src.dtype.itemsize)
    return [pltpu.make_async_remote_copy(src_ref=_rows_of(src, src_row0 + r, step, cols), dst_ref=_rows_of(dst, dst_row0 + r, step, cols),
                                         send_sem=send_sem, recv_sem=recv_sem, device_id=device, device_id_type=MESH)
            for r in range(0, rows, step)]


def _pair_exchange_copies(n, whole, cols=None):
    def copies(refs, send, recv):
        x, y, c = _place()
        sibling = (x, y, 1 - c)
        out = []
        for w in range(n):
            hr = refs[n + w].shape[1]
            if whole:
                out.append(pltpu.make_async_remote_copy(
                    src_ref=_rows_of(refs[w], (1 - c) * hr, hr, cols), dst_ref=_rows_of(refs[n + w], 0, hr, cols),
                    send_sem=send.at[w], recv_sem=recv.at[w], device_id=sibling, device_id_type=MESH))
            else:
                out += _row_pieces(refs[w], (1 - c) * hr, refs[n + w], 0, hr, send.at[w], recv.at[w], sibling, cols)
        return out
    return copies


def _pair_join_copies(n, whole, sem0=0):
    def copies(refs, send, recv):
        x, y, c = _place()
        sibling = (x, y, 1 - c)
        out = []
        for w in range(n):
            hr = refs[w].shape[1] // 2
            sems = dict(send_sem=send.at[sem0 + w], recv_sem=recv.at[sem0 + w])
            if whole:
                out.append(pltpu.make_async_remote_copy(
                    src_ref=refs[w].at[:, pl.ds(c * hr, hr)], dst_ref=refs[w].at[:, pl.ds((1 - c) * hr, hr)],
                    device_id=sibling, device_id_type=MESH, **sems))
            else:
                out += _row_pieces(refs[w], c * hr, refs[w], c * hr, hr, sems["send_sem"], sems["recv_sem"], sibling)
        return out
    return copies


def _both_copies(first, n_first, second):
    def copies(refs, send, recv):
        return first(refs[:n_first], send, recv) + second(refs[n_first:], send, recv)
    return copies


def _small_gather_copies(refs, send, recv):
    vec, land = refs
    x, y, c = _place()
    me = 4 * x + 2 * y + c
    return [pltpu.make_async_remote_copy(src_ref=vec, dst_ref=land.at[me], send_sem=send.at[k], recv_sem=recv.at[k],
                                         device_id=peer, device_id_type=MESH) for k, (peer, _) in enumerate(_all_devices(x, y, c))]


def _small_sum(vec, land, me, n_sum, d):
    rows = vec.shape[0]

    def body(me_ref, v_ref, l_ref, sum_ref, kept_ref):
        def slot(k):
            return jnp.where(me_ref[0] == k, v_ref[...], l_ref[k])

        total = slot(0)[0:n_sum, :]
        kept_ref[0] = slot(0)[n_sum:rows, :]
        for k in range(1, N_DEV):
            total = total + slot(k)[0:n_sum, :]
            kept_ref[k] = slot(k)[n_sum:rows, :]
        sum_ref[...] = total
        loss = 0.5 / d * jnp.sum(total[0:8, :])
        sum_ref[0:8, :] = jnp.full((8, 128), loss, F32)

    vm = pl.BlockSpec(memory_space=pltpu.VMEM)
    return pl.pallas_call(
        body, name="small_sum", in_specs=[pl.BlockSpec(memory_space=pltpu.SMEM), vm, vm], out_specs=[vm, vm],
        out_shape=[jax.ShapeDtypeStruct((n_sum, 128), F32), jax.ShapeDtypeStruct((N_DEV, rows - n_sum, 128), F32)],
        compiler_params=pltpu.CompilerParams(vmem_limit_bytes=VMEM_LIMIT_V7X))(me, vec, land)


def _pair_exchange_into(view, land, cols, name):
    def body(v_ref, l_in, l_ref, send, recv):
        del l_in
        _sibling_handshake()
        x, y, c = _place()
        sibling = (x, y, 1 - c)
        hr = l_ref.shape[1]
        for cp in _row_pieces(v_ref, (1 - c) * hr, l_ref, 0, hr, send, recv, sibling, cols):
            cp.start()
        pltpu.make_async_remote_copy(src_ref=_rows_of(v_ref, (1 - c) * hr, hr, cols), dst_ref=_rows_of(l_ref, 0, hr, cols),
                                     send_sem=send, recv_sem=recv, device_id=sibling, device_id_type=MESH).wait()

    any_spec = pl.BlockSpec(memory_space=pl.ANY)
    return pl.pallas_call(
        body, name=name, out_shape=jax.ShapeDtypeStruct(land.shape, land.dtype),
        in_specs=[any_spec, any_spec], out_specs=any_spec, input_output_aliases={1: 0},
        scratch_shapes=[pltpu.SemaphoreType.DMA, pltpu.SemaphoreType.DMA],
        compiler_params=pltpu.CompilerParams(collective_id=PAIR_ID_INTO),
    )(view, land)


def _pair_sum(view, got, core, name):
    s, r, cols = view.shape
    hr = r // 2
    tr = _row_tile(hr, cols, itemsize=view.dtype.itemsize, budget=4 << 20)
    nb = hr // tr

    def body(core_ref, a_ref, b_ref, o_ref):
        del core_ref
        o_ref[...] = (a_ref[...].astype(F32) + b_ref[...].astype(F32)).astype(BF16)

    same = pl.BlockSpec((None, tr, cols), lambda j, i, core_ref: (j, i, 0))
    return pl.pallas_call(
        body, name=name,
        grid_spec=pltpu.PrefetchScalarGridSpec(
            num_scalar_prefetch=1, grid=(s, nb),
            in_specs=[pl.BlockSpec((None, tr, cols), lambda j, i, core_ref: (j, core_ref[0] * nb + i, 0)), same],
            out_specs=same),
        out_shape=jax.ShapeDtypeStruct((s, hr, cols), BF16), compiler_params=_params("parallel", "parallel"))(core, view, got)


def _pair_sum_small(views, gots, core, name):
    n = len(views)

    def body(core_ref, *refs):
        del core_ref
        for w in range(n):
            refs[2 * n + w][...] = (refs[w][...].astype(F32) + refs[n + w][...].astype(F32)).astype(BF16)

    halves = [(v.shape[0], v.shape[1] // 2, v.shape[2]) for v in views]
    own = [pl.BlockSpec(h, lambda i, core_ref: (0, core_ref[0], 0)) for h in halves]
    whole = [pl.BlockSpec(h, lambda i, core_ref: (0, 0, 0)) for h in halves]
    return pl.pallas_call(
        body, name=name,
        grid_spec=pltpu.PrefetchScalarGridSpec(num_scalar_prefetch=1, grid=(1,), in_specs=own + whole, out_specs=whole),
        out_shape=[jax.ShapeDtypeStruct(h, BF16) for h in halves], compiler_params=_params("arbitrary"))(core, *views, *gots)


def _piece_cols(part, col_sharded):
    return part.shape[2] // N_CHIPS if col_sharded else part.shape[2]


def _chip_sum_small(parts, gots, col_sharded, place, name):
    n = len(parts)

    def body(place_ref, *refs):
        del place_ref
        for w in range(n):
            got = refs[n + w]
            acc = refs[w][...].astype(F32) + got[0].astype(F32)
            refs[2 * n + w][...] = acc + got[1].astype(F32) + got[2].astype(F32)

    own, others, outs, shapes = [], [], [], []
    for p, cs in zip(parts, col_sharded):
        hr, cols = p.shape[1], _piece_cols(p, cs)
        own.append(pl.BlockSpec((None, hr, cols), (lambda i, pr: (0, 0, pr[0])) if cs else (lambda i, pr: (pr[0], 0, 0))))
        others.append(pl.BlockSpec((3, hr, cols), lambda i, pr: (0, 0, 0)))
        outs.append(pl.BlockSpec((hr, cols), lambda i, pr: (pr[1], 0)))
        shapes.append(jax.ShapeDtypeStruct((2 * hr, cols), F32))
    return pl.pallas_call(
        body, name=name,
        grid_spec=pltpu.PrefetchScalarGridSpec(num_scalar_prefetch=1, grid=(1,), in_specs=own + others, out_specs=outs),
        out_shape=shapes, compiler_params=_params("arbitrary"))(place, *parts, *gots)


def _chip_sum(part, got, col_sharded, place, name):
    _, hr, _ = part.shape
    cols = _piece_cols(part, col_sharded)
    tr = _row_tile(hr, cols)
    nb = hr // tr

    def body(place_ref, own_ref, g0_ref, g1_ref, g2_ref, o_ref):
        del place_ref
        acc = own_ref[...].astype(F32) + g0_ref[...].astype(F32)
        o_ref[...] = acc + g1_ref[...].astype(F32) + g2_ref[...].astype(F32)

    if col_sharded:
        own = pl.BlockSpec((None, tr, cols), lambda i, pr: (0, i, pr[0]))
    else:
        own = pl.BlockSpec((None, tr, cols), lambda i, pr: (pr[0], i, 0))
    others = [pl.BlockSpec((None, tr, cols), lambda i, pr, k=k: (k, i, 0)) for k in range(3)]
    return pl.pallas_call(
        body, name=name,
        grid_spec=pltpu.PrefetchScalarGridSpec(
            num_scalar_prefetch=1, grid=(nb,), in_specs=[own] + others,
            out_specs=pl.BlockSpec((tr, cols), lambda i, pr: (pr[1] * nb + i, 0))),
        out_shape=jax.ShapeDtypeStruct((2 * hr, cols), F32), compiler_params=_params("parallel"))(place, part, got, got, got)


def _row_tile(rows, cols, itemsize=4, budget=2 << 20):
    t = rows
    while t * cols * itemsize > budget and t % 16 == 0:
        t //= 2
    return t


def _adamw(w, g, m, v, name, after, also_g=False):
    rows, cols = w.shape
    tr = _row_tile(rows, cols, budget=2 << 20)
    extra = [] if after is None else [after]
    n_out = 4 if also_g else 3

    def body(w_ref, g_ref, m_ref, v_ref, *rest):
        d_ref, nm_ref, nv_ref = rest[len(extra):len(extra) + 3]
        g_ = g_ref[...]
        if also_g:
            rest[len(extra) + 3][...] = g_
        nm = ADAM_B1 * m_ref[...] + (1.0 - ADAM_B1) * g_
        nv = ADAM_B2 * v_ref[...] + (1.0 - ADAM_B2) * (g_ * g_)
        m_hat = nm / (1.0 - ADAM_B1 ** ADAM_STEP)
        v_hat = nv / (1.0 - ADAM_B2 ** ADAM_STEP)
        d_ref[...] = -ADAM_LR * (m_hat / (jnp.sqrt(v_hat) + ADAM_EPS) + ADAM_WD * w_ref[...])
        nm_ref[...] = nm
        nv_ref[...] = nv

    spec = pl.BlockSpec((tr, cols), lambda i: (i, 0))
    shp = jax.ShapeDtypeStruct((rows, cols), F32)
    return pl.pallas_call(body, name=name, grid=(rows // tr,),
                          in_specs=[spec] * 4 + [pl.BlockSpec(memory_space=pl.ANY)] * len(extra), out_specs=[spec] * n_out,
                          out_shape=[shp] * n_out, compiler_params=_params("parallel"))(w, g, m, v, *extra)


def _adamw_together(ws, gs, ms, vs, name, after):
    n = len(ws)

    def body(*refs):
        outs = refs[4 * n + 1:]
        for k in range(n):
            g_ = refs[n + k][...]
            nm = ADAM_B1 * refs[2 * n + k][...] + (1.0 - ADAM_B1) * g_
            nv = ADAM_B2 * refs[3 * n + k][...] + (1.0 - ADAM_B2) * (g_ * g_)
            m_hat = nm / (1.0 - ADAM_B1 ** ADAM_STEP)
            v_hat = nv / (1.0 - ADAM_B2 ** ADAM_STEP)
            outs[3 * k][...] = -ADAM_LR * (m_hat / (jnp.sqrt(v_hat) + ADAM_EPS) + ADAM_WD * refs[k][...])
            outs[3 * k + 1][...] = nm
            outs[3 * k + 2][...] = nv

    specs = [pl.BlockSpec(w.shape, lambda i: (0, 0)) for w in ws]
    out_specs, out_shape = [], []
    for w in ws:
        out_specs += [pl.BlockSpec(w.shape, lambda i: (0, 0))] * 3
        out_shape += [jax.ShapeDtypeStruct(w.shape, F32)] * 3
    res = pl.pallas_call(body, name=name, grid=(1,), in_specs=specs * 4 + [pl.BlockSpec(memory_space=pl.ANY)],
                         out_specs=out_specs, out_shape=out_shape, compiler_params=_params("arbitrary"))(*ws, *gs, *ms, *vs, after)
    return [tuple(res[3 * k:3 * k + 3]) for k in range(n)]


def _all_devices(x, y, c):
    out = []
    for k in range(1, N_DEV):
        peer = (x ^ ((k >> 2) & 1), y ^ ((k >> 1) & 1), c ^ (k & 1))
        out.append((peer, 4 * peer[0] + 2 * peer[1] + peer[2]))
    return out


def _ada_exchange(c, w_shard, b_ada):
    bsz, d = c.shape
    cs = w_shard.shape[1]

    def body(c_ref, w_ref, b_ref, mod_ref, act_ref, c_all, part, pieces, csend, crecv, psend, precv):
        x, y, core = _place()
        me = 4 * x + 2 * y + core
        chip = 2 * x + y
        c_all[me] = c_ref[...]
        peers = _all_devices(x, y, core)
        copies = []
        for k, (peer, _) in enumerate(peers):
            cp = pltpu.make_async_remote_copy(src_ref=c_ref, dst_ref=c_all.at[me], send_sem=csend.at[k], recv_sem=crecv.at[k],
                                              device_id=peer, device_id_type=MESH)
            cp.start()
            copies.append(cp)
        for k, (_, src) in enumerate(peers):
            pltpu.make_async_remote_copy(src_ref=c_ref, dst_ref=c_all.at[src], send_sem=csend.at[k], recv_sem=crecv.at[k],
                                         device_id=(x, y, core), device_id_type=MESH).wait_recv()
        rows = jnp.concatenate([c_all[i] for i in range(N_DEV)], axis=0)
        act = rows * _sigmoid(rows)
        act_ref[...] = act
        prod = jnp.dot(act.astype(BF16), w_ref[...].astype(BF16), preferred_element_type=F32)
        for i in range(N_DEV):
            part[i] = prod[i * bsz:(i + 1) * bsz, :]
        pieces[chip] = part[me]
        chips = _other_chips(x, y)
        for k, (cx, cy) in enumerate(chips):
            cp = pltpu.make_async_remote_copy(src_ref=part.at[4 * cx + 2 * cy + core], dst_ref=pieces.at[chip],
                                              send_sem=psend.at[k], recv_sem=precv.at[k], device_id=(cx, cy, core),
                                              device_id_type=MESH)
            cp.start()
            copies.append(cp)
        for k, (cx, cy) in enumerate(chips):
            pltpu.make_async_remote_copy(src_ref=part.at[me], dst_ref=pieces.at[2 * cx + cy], send_sem=psend.at[k],
                                         recv_sem=precv.at[k], device_id=(cx, cy, core), device_id_type=MESH).wait_recv()
        for cp in copies:
            cp.wait_send()
        mod_ref[...] = jnp.concatenate([pieces[j] for j in range(N_CHIPS)], axis=1) + b_ref[...]

    vm = pl.BlockSpec(memory_space=pltpu.VMEM)
    return pl.pallas_call(
        body, name="ada_exchange", in_specs=[vm] * 3, out_specs=[vm] * 2,
        out_shape=[jax.ShapeDtypeStruct((bsz, 3 * d), F32), jax.ShapeDtypeStruct((N_DEV * bsz, d), F32)],
        scratch_shapes=[pltpu.VMEM((N_DEV, bsz, d), F32), pltpu.VMEM((N_DEV, bsz, cs), F32), pltpu.VMEM((N_CHIPS, bsz, cs), F32),
                        pltpu.SemaphoreType.DMA((N_DEV - 1,)), pltpu.SemaphoreType.DMA((N_DEV - 1,)),
                        pltpu.SemaphoreType.DMA((3,)), pltpu.SemaphoreType.DMA((3,))],
        compiler_params=pltpu.CompilerParams(vmem_limit_bytes=VMEM_LIMIT_V7X))(c, w_shard, b_ada)


def _grad_and_adamw_w_ada(act_t, d_cols, w, m, v):
    d, n = act_t.shape
    cs = d_cols.shape[1]
    tn = min(256, cs)

    def body(a_ref, g_ref, w_ref, m_ref, v_ref, go_ref, d_ref, nm_ref, nv_ref):
        a, g = a_ref[...], g_ref[...]
        acc = a[:, 0:1] * g[0:1, :]
        for b in range(1, n):
            acc = acc + a[:, b:b + 1] * g[b:b + 1, :]
        go_ref[...] = acc
        nm = ADAM_B1 * m_ref[...] + (1.0 - ADAM_B1) * acc
        nv = ADAM_B2 * v_ref[...] + (1.0 - ADAM_B2) * (acc * acc)
        m_hat = nm / (1.0 - ADAM_B1 ** ADAM_STEP)
        v_hat = nv / (1.0 - ADAM_B2 ** ADAM_STEP)
        d_ref[...] = -ADAM_LR * (m_hat / (jnp.sqrt(v_hat) + ADAM_EPS) + ADAM_WD * w_ref[...])
        nm_ref[...] = nm
        nv_ref[...] = nv

    tile = pl.BlockSpec((d, tn), lambda j: (0, j))
    shp = jax.ShapeDtypeStruct((d, cs), F32)
    return pl.pallas_call(body, name="adamw_w_ada", grid=(cs // tn,),
                          in_specs=[pl.BlockSpec((d, n), lambda j: (0, 0)), pl.BlockSpec((n, tn), lambda j: (0, j)), tile, tile, tile],
                          out_specs=[tile] * 4, out_shape=[shp] * 4, compiler_params=_params("parallel"))(act_t, d_cols, w, m, v)


def _permute_w_in(w_nat, lay):
    d = lay.d
    group = 4

    def call(name, width, n_pieces, nat_piece, out_block0, prev):
        def body(*refs):
            refs[-1][...] = jnp.concatenate([r[...] for r in refs[:group]], axis=1)

        in_specs = [pl.BlockSpec((d, width), lambda s, m=m: (0, nat_piece(group * s + m))) for m in range(group)]
        args = [w_nat] * group
        aliases = {}
        if prev is not None:
            in_specs.append(pl.BlockSpec(memory_space=pl.ANY))
            args.append(prev)
            aliases = {group: 0}
        return pl.pallas_call(
            body, name=name, grid=(n_pieces // group,), in_specs=in_specs,
            out_specs=pl.BlockSpec((d, group * width), lambda s: (0, out_block0 + s)),
            out_shape=jax.ShapeDtypeStruct((d, lay.np), BF16), input_output_aliases=aliases,
            compiler_params=_params("arbitrary"))(*args)

    w_all = call("permute_w_attn", SLAB, 2 * PAIR_SLABS, lay.attn_nat_slab, 0, None)
    n_rest = 6 * d // CONV_TILE
    if n_rest % group:
        group = 2
    return call("permute_w_rest", CONV_TILE, n_rest, lay.rest_nat_tile, lay.c0 // (group * CONV_TILE), w_all)


def _project(x2, mod3, w_all, b_all, seq, col0, ncols, tn, out_dtype, want_ht, name):
    t, d = x2.shape
    tm = min(2048, seq)
    per_seq = seq // tm
    j0 = col0 // tn

    def body(x_ref, mod_ref, w_ref, b_ref, o_ref, *rest):
        h_ref = rest[-1]

        @pl.when(pl.program_id(1) == 0)
        def _():
            h = x_ref[...] * (1.0 + mod_ref[:, d:2 * d]) + mod_ref[:, 0:d]
            h_ref[...] = h.astype(BF16)
            if want_ht:
                rest[0][...] = h.T.astype(BF16)

        o_ref[...] = (jnp.dot(h_ref[...], w_ref[...], preferred_element_type=F32) + b_ref[...]).astype(out_dtype)

    out_shape = [jax.ShapeDtypeStruct((t, ncols), out_dtype)]
    out_specs = [pl.BlockSpec((tm, tn), lambda i, j: (i, j))]
    if want_ht:
        out_shape.append(jax.ShapeDtypeStruct((d, t), BF16))
        out_specs.append(pl.BlockSpec((d, tm), lambda i, j: (0, i)))
    return pl.pallas_call(
        body, name=name, grid=(t // tm, ncols // tn),
        in_specs=[pl.BlockSpec((tm, d), lambda i, j: (i, 0)),
                  pl.BlockSpec((None, 1, 3 * d), lambda i, j: (i // per_seq, 0, 0)),
                  pl.BlockSpec((d, tn), lambda i, j: (0, j0 + j)),
                  pl.BlockSpec((1, tn), lambda i, j: (0, j0 + j))],
        out_specs=out_specs, out_shape=out_shape,
        scratch_shapes=[pltpu.VMEM((tm, d), BF16)],
        compiler_params=_params("arbitrary", "arbitrary"))(x2, mod3, w_all, b_all)


def _slope(g, p, hh):
    head = 4 * g + 2 * p + hh
    return 2.0 ** (-ALIBI_MAX_EXP * (head + 1.0) / N_HEADS)


def _ld_rows(ref, start, n, stride):
    if stride == 1:
        return ref[pl.ds(start, n), :]
    return ref[pl.ds(start, n, stride=stride), :]


def _st_rows(ref, start, n, stride, val):
    if stride == 1:
        ref[pl.ds(start, n), :] = val
    else:
        ref[pl.ds(start, n, stride=stride), :] = val


def _sub_blocks(g, seq):
    return seq // DILATIONS[g] // SUB


def _key_rows(g, seq):
    return SUB if _sub_blocks(g, seq) == 1 else 2 * SUB


def _fill_bias(bias_ref, p, seq):
    for g in range(N_GROUPS):
        nk = _key_rows(g, seq)
        diff = lax.broadcasted_iota(jnp.int32, (SUB, nk), 0) - lax.broadcasted_iota(jnp.int32, (SUB, nk), 1)
        for i, off in enumerate((0, SUB)):
            if i == 1 and nk == SUB:
                continue
            delta = diff + off
            ok = (delta >= 0) & (delta <= SUB)
            dist = (delta * DILATIONS[g]).astype(F32)
            for hh in range(2):
                slope = jnp.where(p == 0, _slope(g, 0, hh), _slope(g, 1, hh))
                bias_ref[g, i, hh, :, 0:nk] = jnp.where(ok, -slope * dist, NEG)


def _to_sub_major(pa_ref, col, sub_ref, stage, dil, seq):
    cols = slice(col * SLAB, (col + 1) * SLAB)
    if dil == 1:
        sub_ref[...] = pa_ref[:, cols]
        return
    n = seq // dil
    stage[...] = pa_ref[:, cols].astype(F32)
    for r in range(dil):
        sub_ref[pl.ds(r * n, n), :] = stage[pl.ds(r, n, stride=dil), :].astype(BF16)


def _block_rows(it, g, seq):
    dil, nb = DILATIONS[g], _sub_blocks(g, seq)
    row0 = pl.multiple_of(it * SUB, SUB)
    if nb == 1:
        return row0, row0, 0, it
    blk = it % nb
    first = blk == 0
    krow0 = pl.multiple_of(row0 - jnp.where(first, 0, SUB), SUB)
    nat = row0 if dil == 1 else it // nb + dil * SUB * blk
    return row0, krow0, jnp.where(first, 0, 1), nat


def _nt(a, b):
    return lax.dot_general(a, b, (((1,), (1,)), ((), ())), preferred_element_type=F32)


def _tn(a, b):
    return lax.dot_general(a, b, (((0,), (0,)), ((), ())), preferred_element_type=F32)


def _head_sums(t):
    rows = t.shape[0]
    lo = jnp.broadcast_to(jnp.sum(t[:, :HEAD_DIM], axis=-1, keepdims=True), (rows, HEAD_DIM))
    hi = jnp.broadcast_to(jnp.sum(t[:, HEAD_DIM:], axis=-1, keepdims=True), (rows, HEAD_DIM))
    return jnp.concatenate([lo, hi], axis=-1)


def _attn_fwd(pa, bsz, seq):
    t = pa.shape[0]
    n_blocks = seq // SUB
    chunk = 256

    def body(pa_ref, o_ref, lse_ref, a_ref, sub, stage, bias_ref, s_buf, p_buf, l_buf):
        p = pl.program_id(1)
        _fill_bias(bias_ref, p, seq)
        head0 = lax.broadcasted_iota(jnp.int32, (SUB, SLAB), 1) < HEAD_DIM
        for g in range(N_GROUPS):
            dil = DILATIONS[g]
            for w in range(3):
                _to_sub_major(pa_ref, 3 * w + g, sub.at[w], stage, dil, seq)
            nk = _key_rows(g, seq)

            def trip(i, carry, g=g, dil=dil, nk=nk):
                places = [_block_rows(BLOCKS_PER_TRIP * i + j, g, seq) for j in range(BLOCKS_PER_TRIP)]
                for j, (row0, krow0, _, _) in enumerate(places):
                    q = sub[0, pl.ds(row0, SUB), :]
                    zero = jnp.zeros_like(q)
                    q2 = jnp.concatenate([jnp.where(head0, q, zero), jnp.where(head0, zero, q)], axis=0) * (HEAD_DIM ** -0.5)
                    s_buf[j, :, 0:nk] = _nt(q2, sub[1, pl.ds(krow0, nk), :])
                for j, (_, _, bi, _) in enumerate(places):
                    for c in range(0, 2 * SUB, SOFTMAX_ROWS):
                        hh, r = divmod(c, SUB)
                        s = s_buf[j, c:c + SOFTMAX_ROWS, 0:nk] + bias_ref[g, bi, hh, r:r + SOFTMAX_ROWS, 0:nk]
                        m = jnp.max(s, axis=-1, keepdims=True)
                        e = jnp.exp(s - m)
                        den = jnp.sum(e, axis=-1, keepdims=True)
                        p_buf[j, c:c + SOFTMAX_ROWS, 0:nk] = (e * (1.0 / den)).astype(BF16)
                        l_buf[j, c:c + SOFTMAX_ROWS, :] = jnp.broadcast_to(m + jnp.log(den), (SOFTMAX_ROWS, SLAB))
                for j, (_, krow0, _, nat) in enumerate(places):
                    o2 = jnp.dot(p_buf[j, :, 0:nk], sub[2, pl.ds(krow0, nk), :], preferred_element_type=F32)
                    _st_rows(o_ref.at[g], nat, SUB, dil, jnp.where(head0, o2[0:SUB], o2[SUB:2 * SUB]))
                    _st_rows(lse_ref.at[g], nat, SUB, dil, jnp.where(head0, l_buf[j, 0:SUB, :], l_buf[j, SUB:2 * SUB, :]))
                return carry

            lax.fori_loop(0, n_blocks // BLOCKS_PER_TRIP, trip, 0)

        def mix(i, carry):
            rows = pl.ds(pl.multiple_of(i * chunk, chunk), chunk)
            l0, l1, l2 = lse_ref[0, rows, :], lse_ref[1, rows, :], lse_ref[2, rows, :]
            m = jnp.maximum(jnp.maximum(l0, l1), l2)
            e0, e1, e2 = jnp.exp(l0 - m), jnp.exp(l1 - m), jnp.exp(l2 - m)
            tot = e0 + e1 + e2
            o = (e0 / tot) * o_ref[0, rows, :] + (e1 / tot) * o_ref[1, rows, :] + (e2 / tot) * o_ref[2, rows, :]
            z = pa_ref[rows, 9 * SLAB:10 * SLAB].astype(F32)
            a_ref[rows, :] = (o * (z * _sigmoid(z))).astype(BF16)
            return carry

        lax.fori_loop(0, seq // chunk, mix, 0)

    big = jax.ShapeDtypeStruct((N_GROUPS, t, 2 * SLAB), F32)
    return pl.pallas_call(
        body, name="attn_fwd", grid=(bsz, 2),
        in_specs=[pl.BlockSpec((seq, PAIR_COLS), lambda b, p: (b, p))],
        out_specs=[pl.BlockSpec((N_GROUPS, seq, SLAB), lambda b, p: (0, b, p)),
                   pl.BlockSpec((N_GROUPS, seq, SLAB), lambda b, p: (0, b, p)),
                   pl.BlockSpec((seq, SLAB), lambda b, p: (b, p))],
        out_shape=[big, big, jax.ShapeDtypeStruct((t, 2 * SLAB), BF16)],
        scratch_shapes=[pltpu.VMEM((3, seq, SLAB), BF16), pltpu.VMEM((seq, SLAB), F32),
                        pltpu.VMEM((N_GROUPS, 2, 2, SUB, 2 * SUB), F32), pltpu.VMEM((BLOCKS_PER_TRIP, 2 * SUB, 2 * SUB), F32),
                        pltpu.VMEM((BLOCKS_PER_TRIP, 2 * SUB, 2 * SUB), BF16), pltpu.VMEM((BLOCKS_PER_TRIP, 2 * SUB, SLAB), F32)],
        compiler_params=_params("arbitrary", "arbitrary"))(pa)


def _shift_down(v, k, rows):
    return jnp.where(rows >= k, pltpu.roll(v, k, 0), 0.0)


def _shift_up(v, k, rows):
    n = v.shape[0]
    return jnp.where(rows < n - k, pltpu.roll(v, n - k, 0), 0.0)


def _conv_fwd(pr, conv_w, bsz, seq, d):
    t = pr.shape[0]
    ct = CONV_TILE

    def body(p_ref, cw_ref, o_ref):
        u = p_ref[:, 2 * ct:3 * ct].astype(F32) * p_ref[:, 0:ct].astype(F32)
        cw = cw_ref[...]
        rows = lax.broadcasted_iota(jnp.int32, u.shape, 0)
        conv = cw[0:1, :] * _shift_down(u, 2, rows)
        conv = conv + cw[1:2, :] * _shift_down(u, 1, rows)
        conv = conv + cw[2:3, :] * u
        z = p_ref[:, 3 * ct:4 * ct].astype(F32)
        o_ref[...] = (p_ref[:, ct:2 * ct].astype(F32) * conv * (z * _sigmoid(z))).astype(BF16)

    return pl.pallas_call(
        body, name="conv_fwd", grid=(bsz, d // ct),
        in_specs=[pl.BlockSpec((seq, 4 * ct), lambda b, j: (b, j)), pl.BlockSpec((3, ct), lambda b, j: (0, j))],
        out_specs=pl.BlockSpec((seq, ct), lambda b, j: (b, j)),
        out_shape=jax.ShapeDtypeStruct((t, d), BF16), compiler_params=_params("parallel", "parallel"))(pr, conv_w)


def _tail(a_in, b_in, pr, x2, target2, mod3, w_pa, w_pc, w_out, b_out, ln_g, ln_b, seq, lay):
    t, d = x2.shape
    tm = 512
    per_seq = seq // tm
    n_steps = t // tm
    gate_blk = 4 * d // d

    def nt(a, b):
        return lax.dot_general(a, b, (((1,), (1,)), ((), ())), preferred_element_type=F32)

    def tn(a, b):
        return lax.dot_general(a, b, (((0,), (0,)), ((), ())), preferred_element_type=F32)

    def body(a_ref, b_ref, ga_ref, gb_ref, x_ref, tg_ref, mod_ref, wpa_ref, wpc_ref, wo_ref, bo_ref, lg_ref, lb_ref,
             dpg_ref, da_ref, db_ref, gx_ref, dgate_ref, small_ref, gwpa_hbm, gwpc_hbm, gwo_hbm,
             acc_pa, acc_pc, acc_o, sem):
        i = pl.program_id(0)

        @pl.when(i == 0)
        def _():
            acc_pa[...] = jnp.zeros_like(acc_pa)
            acc_pc[...] = jnp.zeros_like(acc_pc)
            acc_o[...] = jnp.zeros_like(acc_o)
            small_ref[...] = jnp.zeros_like(small_ref)

        @pl.when(i % per_seq == 0)
        def _():
            dgate_ref[...] = jnp.zeros_like(dgate_ref)

        halves = [slice(k * (tm // 2), (k + 1) * (tm // 2)) for k in range(2)]
        gate = mod_ref[:, 2 * d:3 * d]
        a_bf = [a_ref[rs, :] for rs in halves]
        b_bf = [b_ref[rs, :] for rs in halves]
        y_attn = [jnp.dot(a, wpa_ref[...], preferred_element_type=F32) for a in a_bf]
        y_conv = [jnp.dot(b, wpc_ref[...], preferred_element_type=F32) for b in b_bf]
        sa = [_sigmoid(ga_ref[rs, :].astype(F32)) for rs in halves]
        sb = [_sigmoid(gb_ref[rs, :].astype(F32)) for rs in halves]
        merged = [(sa[k] * y_attn[k] + sb[k] * y_conv[k]).astype(BF16) for k in range(2)]
        mo = [jnp.dot(m, wo_ref[...], preferred_element_type=F32) + bo_ref[...] for m in merged]
        d_mo_bf = []
        for k, rs in enumerate(halves):
            r = ALPHA * x_ref[rs, :] + gate * mo[k]
            mu = jnp.mean(r, axis=-1, keepdims=True)
            cen = r - mu
            var = jnp.mean(cen * cen, axis=-1, keepdims=True)
            rstd = lax.rsqrt(var + LN_EPS)
            xhat = cen * rstd
            err = xhat * lg_ref[...] + lb_ref[...] - tg_ref[rs, :]
            dy = err * (1.0 / d)
            dxhat = dy * lg_ref[...]
            dr = rstd * (dxhat - jnp.mean(dxhat, axis=-1, keepdims=True)
                         - xhat * jnp.mean(dxhat * xhat, axis=-1, keepdims=True))
            gx_ref[rs, :] = ALPHA * dr
            dgate_ref[...] += jnp.sum(dr * mo[k], axis=0, keepdims=True)
            d_mo = dr * gate
            small_ref[0:1, :] += jnp.sum(d_mo, axis=0, keepdims=True)
            small_ref[1:2, :] += jnp.sum(dy * xhat, axis=0, keepdims=True)
            small_ref[2:3, :] += jnp.sum(dy, axis=0, keepdims=True)
            small_ref[3:4, :] += jnp.sum(err * err, axis=0, keepdims=True)
            d_mo_bf.append(d_mo.astype(BF16))
        acc_o[...] += tn(jnp.concatenate(merged, axis=0), jnp.concatenate(d_mo_bf, axis=0))
        dmerged = [nt(g, wo_ref[...]) for g in d_mo_bf]
        dy_attn, dy_conv = [], []
        for k, rs in enumerate(halves):
            dy_attn.append((dmerged[k] * sa[k]).astype(BF16))
            dy_conv.append((dmerged[k] * sb[k]).astype(BF16))
            dpg_ref[rs, 0:d] = (dmerged[k] * y_attn[k] * sa[k] * (1.0 - sa[k])).astype(BF16)
            dpg_ref[rs, d:2 * d] = (dmerged[k] * y_conv[k] * sb[k] * (1.0 - sb[k])).astype(BF16)
        acc_pa[...] += tn(a_ref[...], jnp.concatenate(dy_attn, axis=0))
        acc_pc[...] += tn(b_ref[...], jnp.concatenate(dy_conv, axis=0))
        for k, rs in enumerate(halves):
            da_ref[rs, :] = nt(dy_attn[k], wpa_ref[...])
            db_ref[rs, :] = nt(dy_conv[k], wpc_ref[...])

        @pl.when(i == n_steps - 1)
        def _():
            copies = [pltpu.make_async_copy(acc_pa, gwpa_hbm, sem.at[0]), pltpu.make_async_copy(acc_pc, gwpc_hbm, sem.at[1]),
                      pltpu.make_async_copy(acc_o, gwo_hbm, sem.at[2])]
            for cp in copies:
                cp.start()
            for cp in copies:
                cp.wait()

    row = lambda w: pl.BlockSpec((tm, w), lambda i: (i, 0))
    const = lambda shp: pl.BlockSpec(shp, lambda i: (0,) * len(shp), pipeline_mode=pl.Buffered(1))
    any_spec = pl.BlockSpec(memory_space=pl.ANY)
    return pl.pallas_call(
        body, name="tail", grid=(n_steps,),
        in_specs=[row(Z_WIDTH), row(d),
                  pl.BlockSpec((tm, d), lambda i: (i, gate_blk)), pl.BlockSpec((tm, d), lambda i: (i, gate_blk + 1)),
                  row(d), row(d), pl.BlockSpec((None, 1, 3 * d), lambda i: (i // per_seq, 0, 0)),
                  const((Z_WIDTH, d)), const((d, d)), const((d, d)), const((1, d)), const((1, d)), const((1, d))],
        out_specs=[pl.BlockSpec((tm, 2 * d), lambda i: (i, lay.g0 // (2 * d))), row(Z_WIDTH), row(d), row(d),
                   pl.BlockSpec((None, 1, d), lambda i: (i // per_seq, 0, 0)), pl.BlockSpec((8, d), lambda i: (0, 0)),
                   HBM_SPEC, HBM_SPEC, HBM_SPEC],
        out_shape=[jax.ShapeDtypeStruct((t, lay.np), BF16), jax.ShapeDtypeStruct((t, Z_WIDTH), F32),
                   jax.ShapeDtypeStruct((t, d), F32), jax.ShapeDtypeStruct((t, d), F32),
                   jax.ShapeDtypeStruct((t // seq, 1, d), F32), jax.ShapeDtypeStruct((8, d), F32),
                   pltpu.HBM((Z_WIDTH, d), F32), pltpu.HBM((d, d), F32), pltpu.HBM((d, d), F32)],
        scratch_shapes=[pltpu.VMEM((Z_WIDTH, d), F32), pltpu.VMEM((d, d), F32), pltpu.VMEM((d, d), F32),
                        pltpu.SemaphoreType.DMA((3,))],
        compiler_params=_params("arbitrary"),
    )(a_in, b_in, pr, pr, x2, target2, mod3, w_pa, w_pc, w_out, b_out, ln_g, ln_b)


def _conv_bwd(dproj, db, pr, conv_w, bsz, seq, lay):
    d = lay.d
    ct = CONV_TILE
    base = lay.c0 // (4 * ct)

    def body(dp_in, db_ref, p_ref, cw_ref, dp_ref, gcw_ref):
        del dp_in
        u_x, g_b, g_c, z = [p_ref[:, k * ct:(k + 1) * ct].astype(F32) for k in range(4)]
        cw = cw_ref[...]
        u = g_c * u_x
        rows = lax.broadcasted_iota(jnp.int32, u.shape, 0)
        u1, u2 = _shift_down(u, 1, rows), _shift_down(u, 2, rows)
        conv = cw[0:1, :] * u2 + cw[1:2, :] * u1 + cw[2:3, :] * u
        sig = _sigmoid(z)
        sl = z * sig
        dbv = db_ref[...]
        gbc = g_b * conv
        dp_ref[:, ct:2 * ct] = (dbv * sl * conv).astype(BF16)
        dp_ref[:, 3 * ct:4 * ct] = (dbv * gbc * (sig * (1.0 + z * (1.0 - sig)))).astype(BF16)
        dconv = dbv * sl * g_b

        @pl.when(pl.program_id(1) == 0)
        def _():
            gcw_ref[...] = jnp.zeros_like(gcw_ref)

        gcw_ref[0:1, :] += jnp.sum(dconv * u2, axis=0, keepdims=True)
        gcw_ref[1:2, :] += jnp.sum(dconv * u1, axis=0, keepdims=True)
        gcw_ref[2:3, :] += jnp.sum(dconv * u, axis=0, keepdims=True)
        du = cw[2:3, :] * dconv + cw[1:2, :] * _shift_up(dconv, 1, rows) + cw[0:1, :] * _shift_up(dconv, 2, rows)
        dp_ref[:, 0:ct] = (du * g_c).astype(BF16)
        dp_ref[:, 2 * ct:3 * ct] = (du * u_x).astype(BF16)

    return pl.pallas_call(
        body, name="conv_bwd", grid=(d // ct, bsz),
        in_specs=[pl.BlockSpec(memory_space=pl.ANY), pl.BlockSpec((seq, ct), lambda j, b: (b, j)),
                  pl.BlockSpec((seq, 4 * ct), lambda j, b: (b, j)), pl.BlockSpec((3, ct), lambda j, b: (0, j))],
        out_specs=[pl.BlockSpec((seq, 4 * ct), lambda j, b: (b, base + j)), pl.BlockSpec((8, ct), lambda j, b: (0, j))],
        out_shape=[jax.ShapeDtypeStruct(dproj.shape, BF16), jax.ShapeDtypeStruct((8, d), F32)],
        input_output_aliases={0: 0}, compiler_params=_params("arbitrary", "arbitrary"))(dproj, db, pr, conv_w)


def _attn_bwd(dproj, pa, o_all, lse_all, da, bsz, seq, after):
    n_blocks = seq // SUB
    chunk = 256
    per_trip = BLOCKS_PER_TRIP_BWD

    def body(dp_in, pa_ref, o_ref, lse_ref, da_ref, after_ref, dp_ref, sub, stage, dsub, dog, cvec, bias_ref,
             s_buf, dp_buf, ds_buf, pb_buf, q2_buf, do2_buf, l_buf, c_buf):
        del dp_in, after_ref
        p = pl.program_id(1)
        _fill_bias(bias_ref, p, seq)
        head0 = lax.broadcasted_iota(jnp.int32, (SUB, SLAB), 1) < HEAD_DIM

        def mix_bwd(i, carry):
            rows = pl.ds(pl.multiple_of(i * chunk, chunk), chunk)
            ls = [lse_ref[g, rows, :] for g in range(N_GROUPS)]
            os_ = [o_ref[g, rows, :] for g in range(N_GROUPS)]
            m = jnp.maximum(jnp.maximum(ls[0], ls[1]), ls[2])
            es = [jnp.exp(l - m) for l in ls]
            tot = es[0] + es[1] + es[2]
            ws = [e / tot for e in es]
            o = ws[0] * os_[0] + ws[1] * os_[1] + ws[2] * os_[2]
            z = pa_ref[rows, 9 * SLAB:10 * SLAB].astype(F32)
            sig = _sigmoid(z)
            dav = da_ref[rows, :]
            do = dav * (z * sig)
            dp_ref[rows, 9 * SLAB:10 * SLAB] = (dav * o * (sig * (1.0 + z * (1.0 - sig)))).astype(BF16)
            wsum = _head_sums(do * o)
            for g in range(N_GROUPS):
                dog[g, rows, :] = ws[g] * do
                cvec[g, rows, :] = -(ws[g] * wsum)
            return carry

        lax.fori_loop(0, seq // chunk, mix_bwd, 0)

        for g in range(N_GROUPS):
            dil = DILATIONS[g]
            for w in range(3):
                _to_sub_major(pa_ref, 3 * w + g, sub.at[w], stage, dil, seq)
            dsub[1] = jnp.zeros((seq, SLAB), F32)
            dsub[2] = jnp.zeros((seq, SLAB), F32)
            nk = _key_rows(g, seq)

            def trip(i, carry, g=g, dil=dil, nk=nk):
                places = [_block_rows(per_trip * i + j, g, seq) for j in range(per_trip)]
                for j, (row0, krow0, _, nat) in enumerate(places):
                    q = sub[0, pl.ds(row0, SUB), :]
                    do = _ld_rows(dog.at[g], nat, SUB, dil).astype(BF16)
                    zero = jnp.zeros_like(q)
                    q2 = jnp.concatenate([jnp.where(head0, q, zero), jnp.where(head0, zero, q)], axis=0)
                    do2 = jnp.concatenate([jnp.where(head0, do, zero), jnp.where(head0, zero, do)], axis=0)
                    q2_buf[j] = q2
                    do2_buf[j] = do2
                    s_buf[j, :, 0:nk] = _nt(q2 * (HEAD_DIM ** -0.5), sub[1, pl.ds(krow0, nk), :])
                    dp_buf[j, :, 0:nk] = _nt(do2, sub[2, pl.ds(krow0, nk), :])
                    l_buf[j] = _ld_rows(lse_ref.at[g], nat, SUB, dil)
                    c_buf[j] = _ld_rows(cvec.at[g], nat, SUB, dil)
                for j, (_, _, bi, _) in enumerate(places):
                    for c in range(0, 2 * SUB, SOFTMAX_ROWS):
                        hh, r = divmod(c, SUB)
                        lane = hh * HEAD_DIM
                        s = s_buf[j, c:c + SOFTMAX_ROWS, 0:nk] + bias_ref[g, bi, hh, r:r + SOFTMAX_ROWS, 0:nk]
                        prob = jnp.exp(s - l_buf[j, r:r + SOFTMAX_ROWS, lane:lane + 1])
                        dprob = dp_buf[j, c:c + SOFTMAX_ROWS, 0:nk] + c_buf[j, r:r + SOFTMAX_ROWS, lane:lane + 1]
                        ds_buf[j, c:c + SOFTMAX_ROWS, 0:nk] = (prob * dprob * (HEAD_DIM ** -0.5)).astype(BF16)
                        pb_buf[j, c:c + SOFTMAX_ROWS, 0:nk] = prob.astype(BF16)
                for j, (row0, krow0, _, _) in enumerate(places):
                    ds = ds_buf[j, :, 0:nk]
                    dq2 = jnp.dot(ds, sub[1, pl.ds(krow0, nk), :], preferred_element_type=F32)
                    dsub[0, pl.ds(row0, SUB), :] = jnp.where(head0, dq2[0:SUB], dq2[SUB:2 * SUB])
                    dsub[1, pl.ds(krow0, nk), :] += _tn(ds, q2_buf[j])
                    dsub[2, pl.ds(krow0, nk), :] += _tn(pb_buf[j, :, 0:nk], do2_buf[j])
                return carry

            lax.fori_loop(0, n_blocks // per_trip, trip, 0)
            for w in range(3):
                cols = slice((3 * w + g) * SLAB, (3 * w + g + 1) * SLAB)
                if dil == 1:
                    dp_ref[:, cols] = dsub[w].astype(BF16)
                else:
                    n = seq // dil
                    for r in range(dil):
                        stage[pl.ds(r, n, stride=dil), :] = dsub[w, pl.ds(r * n, n), :]
                    dp_ref[:, cols] = stage[...].astype(BF16)

    return pl.pallas_call(
        body, name="attn_bwd", grid=(bsz, 2),
        in_specs=[pl.BlockSpec(memory_space=pl.ANY), pl.BlockSpec((seq, PAIR_COLS), lambda b, p: (b, p)),
                  pl.BlockSpec((N_GROUPS, seq, SLAB), lambda b, p: (0, b, p)),
                  pl.BlockSpec((N_GROUPS, seq, SLAB), lambda b, p: (0, b, p)),
                  pl.BlockSpec((seq, SLAB), lambda b, p: (b, p)), pl.BlockSpec(memory_space=pl.ANY)],
        out_specs=pl.BlockSpec((seq, PAIR_COLS), lambda b, p: (b, p)),
        out_shape=jax.ShapeDtypeStruct(dproj.shape, BF16), input_output_aliases={0: 0},
        scratch_shapes=[pltpu.VMEM((3, seq, SLAB), BF16), pltpu.VMEM((seq, SLAB), F32), pltpu.VMEM((3, seq, SLAB), F32),
                        pltpu.VMEM((3, seq, SLAB), F32), pltpu.VMEM((3, seq, SLAB), F32),
                        pltpu.VMEM((N_GROUPS, 2, 2, SUB, 2 * SUB), F32),
                        pltpu.VMEM((per_trip, 2 * SUB, 2 * SUB), F32), pltpu.VMEM((per_trip, 2 * SUB, 2 * SUB), F32),
                        pltpu.VMEM((per_trip, 2 * SUB, 2 * SUB), BF16), pltpu.VMEM((per_trip, 2 * SUB, 2 * SUB), BF16),
                        pltpu.VMEM((per_trip, 2 * SUB, SLAB), BF16), pltpu.VMEM((per_trip, 2 * SUB, SLAB), BF16),
                        pltpu.VMEM((per_trip, SUB, SLAB), F32), pltpu.VMEM((per_trip, SUB, SLAB), F32)],
        compiler_params=_params("arbitrary", "arbitrary"))(dproj, pa, o_all, lse_all, da, after)


def _grad_h(dproj, w_all, gx0, x2, mod3, seq, lay):
    t, d = x2.shape
    tm, tn = min(512, seq), min(512, d)
    per_seq = seq // tm

    def body(dp_ref, w_ref, gx0_ref, x_ref, scale_ref, gx_ref, dmod_ref):
        dh = _nt(dp_ref[:, 0:ATT], w_ref[:, 0:ATT]) + _nt(dp_ref[:, lay.c0:], w_ref[:, lay.c0:])
        gx_ref[...] = gx0_ref[...] + dh * (1.0 + scale_ref[...])

        @pl.when(pl.program_id(1) % per_seq == 0)
        def _():
            dmod_ref[...] = jnp.zeros_like(dmod_ref)

        dmod_ref[0:1, :] += jnp.sum(dh, axis=0, keepdims=True)
        dmod_ref[1:2, :] += jnp.sum(dh * x_ref[...], axis=0, keepdims=True)

    tile = pl.BlockSpec((tm, tn), lambda j, i: (i, j))
    return pl.pallas_call(
        body, name="grad_h", grid=(d // tn, t // tm),
        in_specs=[pl.BlockSpec((tm, lay.np), lambda j, i: (i, 0)), pl.BlockSpec((tn, lay.np), lambda j, i: (j, 0)),
                  tile, tile,
                  pl.BlockSpec((None, 1, tn), lambda j, i: (i // per_seq, 0, d // tn + j))],
        out_specs=[tile, pl.BlockSpec((None, 8, tn), lambda j, i: (i // per_seq, 0, j))],
        out_shape=[jax.ShapeDtypeStruct((t, d), F32), jax.ShapeDtypeStruct((t // seq, 8, d), F32)],
        compiler_params=_params("arbitrary", "arbitrary"))(dproj, w_all, gx0, x2, mod3)


def _grad_w_in(ht, dproj, seq, lay, part, prev):
    d, t = ht.shape
    tm = min(t, 2 * seq)
    n_i = t // tm

    def side_by_side(pieces):
        return pieces[0] if len(pieces) == 1 else jnp.concatenate(pieces, axis=1)

    def make_body(n_skip, n_pieces, tn, nat_tile, assemble):
        def body(*refs):
            refs = refs[n_skip:]
            ht_ref, dp_refs = refs[0], refs[1:1 + n_pieces]
            gw_hbm, gb_hbm, acc, bacc, gw_out, gb_out, sem = refs[1 + n_pieces:]
            i, j = pl.program_id(0), pl.program_id(1)
            dp = assemble([r[...] for r in dp_refs])
            part = jnp.dot(ht_ref[...], dp, preferred_element_type=F32)
            bpart = jnp.sum(dp.astype(F32), axis=0, keepdims=True)

            if n_i > 1:
                @pl.when(i == 0)
                def _():
                    acc[j] = part
                    bacc[j] = bpart

                @pl.when((i > 0) & (i < n_i - 1))
                def _():
                    acc[j] += part
                    bacc[j] += bpart

            @pl.when(i == n_i - 1)
            def _():
                gw_out[...] = ((part + acc[j]) if n_i > 1 else part).astype(BF16)
                gb_out[...] = (bpart + bacc[j]) if n_i > 1 else bpart
                cols = pl.ds(pl.multiple_of(nat_tile(j) * tn, SLAB), tn)
                copies = [pltpu.make_async_copy(gw_out, gw_hbm.at[:, cols], sem.at[0]),
                          pltpu.make_async_copy(gb_out, gb_hbm.at[:, cols], sem.at[1])]
                for cp in copies:
                    cp.start()
                for cp in copies:
                    cp.wait()
        return body

    def call(name, pieces, n_tiles, nat_tile, prev, assemble=side_by_side):
        tn = sum(w for w, _ in pieces)
        any_spec = pl.BlockSpec(memory_space=pl.ANY)
        in_specs = [pl.BlockSpec((d, tm), lambda i, j: (0, i))]
        in_specs += [pl.BlockSpec((tm, w), lambda i, j, f=f: (i, f(j))) for w, f in pieces]
        args = [ht] + [dproj] * len(pieces)
        aliases = {}
        if prev is not None:
            in_specs = [any_spec] * 2 + in_specs
            args = list(prev) + args
            aliases = {0: 0, 1: 1}
        return pl.pallas_call(
            make_body(0 if prev is None else 2, len(pieces), tn, nat_tile, assemble), name=name, grid=(n_i, n_tiles),
            in_specs=in_specs,
            out_specs=[any_spec, any_spec],
            out_shape=[jax.ShapeDtypeStruct((d, lay.din), BF16), jax.ShapeDtypeStruct((1, lay.din), F32)],
            input_output_aliases=aliases,
            scratch_shapes=[pltpu.VMEM((n_tiles if n_i > 1 else 1, d, tn), F32), pltpu.VMEM((n_tiles, 1, tn), F32),
                            pltpu.VMEM((d, tn), BF16),
                            pltpu.VMEM((1, tn), F32), pltpu.SemaphoreType.DMA((2,))],
            compiler_params=_params("arbitrary", "arbitrary"))(*args)

    if part == "attn":
        pair_blocks = PAIR_COLS // (2 * SLAB)
        attn_pieces = [(2 * SLAB, lambda j, p=p: p * pair_blocks + j) for p in range(2)]

        def interleave(pieces):
            a, b = pieces
            return jnp.concatenate([a[:, :SLAB], b[:, :SLAB], a[:, SLAB:], b[:, SLAB:]], axis=1)

        return call("grad_w_in_attn", attn_pieces, ATT // 512, lambda j: j, prev, interleave)
    base = lay.c0 // CONV_TILE
    nct = lay.n_conv_tiles
    if nct % 2:
        return call("grad_w_in_rest", [(CONV_TILE, lambda j: base + j)], 6 * d // CONV_TILE, lay.rest_nat_tile, prev)
    half = nct // 2

    def rest_piece(m):
        def perm_tile(j):
            conv = base + 4 * (2 * (j % half) + m) + j // half
            return jnp.where(j < 4 * half, conv, base + 2 * j + m)
        return (CONV_TILE, perm_tile)

    return call("grad_w_in_rest", [rest_piece(0), rest_piece(1)], 6 * d // 512, lambda j: ATT // 512 + j, prev)


def _pack_rows(parts, width=128):
    flat = [p.reshape(-1) for p in parts]
    spans, rows = [], 0
    padded = []
    for f in flat:
        n = -(-f.shape[0] // (8 * width)) * 8
        padded.append(jnp.pad(f, (0, n * width - f.shape[0])).reshape(n, width))
        spans.append((rows, f.shape[0]))
        rows += n
    return jnp.concatenate(padded, axis=0), spans


def _unpack_rows(packed, spans, shapes, width=128):
    out = []
    for (row, n), shp in zip(spans, shapes):
        rows = -(-n // width)
        out.append(packed[row:row + rows].reshape(-1)[:n].reshape(shp))
    return out


def kernel(x, c, w_ada, b_ada, w_in, b_in, conv_w, w_proj_attn, w_proj_conv, w_out, b_out, ln_g, ln_b, loss_target, m_w_ada, m_b_ada, m_w_in, m_b_in, m_conv_w, m_w_proj_attn, m_w_proj_conv, m_w_out, m_b_out, m_ln_g, m_ln_b, v_w_ada, v_b_ada, v_w_in, v_b_in, v_conv_w, v_w_proj_attn, v_w_proj_conv, v_w_out, v_b_out, v_ln_g, v_ln_b):
    bsz, seq, d = x.shape
    t = bsz * seq
    lay = _Layout(d)
    col_sharded = [True, True, False, False]
    red_w = [w_in[0], w_proj_attn[0], w_proj_conv[0], w_out[0]]
    chip = 2 * lax.axis_index("x") + lax.axis_index("y")
    chip1 = chip.astype(jnp.int32).reshape(1)
    core1 = lax.axis_index("c").astype(jnp.int32).reshape(1)
    place = jnp.stack([chip, lax.axis_index("c")]).astype(jnp.int32)
    x2 = x.reshape(t, d)
    target2 = loss_target.reshape(t, d)

    mod, act_all = _ada_exchange(c, w_ada[0], b_ada)
    mod3 = mod.reshape(bsz, 1, 3 * d)

    cw_pad = jnp.pad(conv_w[0], ((0, 5), (0, 0))) + 0.0 * mod[0, 0]
    own_in_full = [_cast_into_full(red_w[0], col_sharded[0], chip1, "cast_shard_in")]
    own_in_full += list(_cast_into_full_small(red_w[1:], col_sharded[1:], chip1, "cast_shards_late"))
    (wi_f,), cw8 = _gather_weights(own_in_full[:1], col_sharded[:1], cw_pad)
    cw_full = cw8[0:3]
    late_copies = _direct_gather_copies(col_sharded[1:])
    late_send, late_recv, late_flying, late_token = _start_copies("gather_late_start", own_in_full[1:], (18,), cw8, late_copies)
    w_all = _permute_w_in(wi_f, lay)
    b_all = lay.perm_vector(b_in) + late_token[0, 0]

    rest_tn = 1024 if (6 * d) % 1024 == 0 else 512
    pa, = _project(x2, mod3, w_all, b_all, seq, 0, ATT, PAIR_COLS, BF16, False, "project_attn")
    pr, ht = _project(x2, mod3, w_all, b_all, seq, lay.c0, 6 * d, rest_tn, BF16, True, "project_rest")
    o_all, lse_all, a_in = _attn_fwd(pa, bsz, seq)
    b_in_act = _conv_fwd(pr, cw_full, bsz, seq, d)
    wpa_f, wpc_f, wo_f = _wait_copies("gather_late_wait", late_flying, late_send, late_recv, b_in_act, late_copies)
    (dproj, da_in, db_in, gx0, dgate, small_tail, gw_pa, gw_pc, gw_out) = _tail(
        a_in, b_in_act, pr, x2, target2, mod3, wpa_f, wpc_f, wo_f, b_out, ln_g, ln_b, seq, lay)

    late_views = _shard_views([gw_pa, gw_pc, gw_out], col_sharded[1:])
    late_lands = [lax.empty((v.shape[0], v.shape[1] // 2, v.shape[2]), v.dtype) for v in late_views]
    xl_send, xl_recv, xl_fly, xl_tok = _start_copies("grads_pair_exchange_late_start", late_views + late_lands, (3,), small_tail,
                                                     _pair_exchange_copies(3, False), pair_id=PAIR_ID_LATE)
    dproj, gcw = _conv_bwd(dproj, db_in, pr, cw_full + xl_tok[0, 0], bsz, seq, lay)
    xl_done = _wait_copies("grads_pair_exchange_late_wait", xl_fly, xl_send, xl_recv, gcw, _pair_exchange_copies(3, True))
    late_parts = list(_pair_sum_small(xl_done[:3], xl_done[3:], core1, "grads_pair_sum_late"))

    late_cross = _chip_scatter_copies(3, col_sharded[1:])
    late_zone = [lax.empty((3, p.shape[1], _piece_cols(p, cs)), p.dtype) for p, cs in zip(late_parts, col_sharded[1:])]
    sl_send, sl_recv, sl_fly, sl_tok = _start_copies("grads_scatter_late_start", late_parts + late_zone, (9,), core1,
                                                     late_cross, chips_id=CHIPS_ID_LATE)
    dproj = _attn_bwd(dproj, pa, o_all, lse_all, da_in, bsz, seq, sl_tok)
    gw_in_bf, gb_in = _grad_w_in(ht, dproj, seq, lay, "rest", None)
    rest_cols, attn_cols = (ATT, 6 * d), (0, ATT)
    in_land = lax.empty((1, d // 2, lay.din), BF16)
    xi_copies = _pair_exchange_copies(1, False, rest_cols)
    xi_send, xi_recv, xi_fly, _ = _start_copies("grads_pair_exchange_in_start", [gw_in_bf.reshape(1, d, lay.din), in_land], (1,),
                                                gb_in, xi_copies, pair_id=PAIR_ID_IN)
    gw_in_bf, gb_in = _grad_w_in(ht, dproj, seq, lay, "attn", (xi_fly[0].reshape(d, lay.din), gb_in))
    xi_done = _wait_copies("grads_pair_exchange_in_wait", [gw_in_bf.reshape(1, d, lay.din), xi_fly[1]], xi_send, xi_recv, gb_in,
                           _pair_exchange_copies(1, True, rest_cols))
    in_got = _pair_exchange_into(xi_done[0], xi_done[1], attn_cols, "grads_pair_exchange_in_attn")
    sl_done = _wait_copies("grads_scatter_late_wait", sl_fly, sl_send, sl_recv, in_got, late_cross)
    late_red = list(_chip_sum_small(sl_done[:3], sl_done[3:], col_sharded[1:], place, "grads_chip_sum_late"))

    in_part = _pair_sum(xi_done[0], in_got, core1, "grads_pair_sum_in")
    in_cross = _chip_scatter_copies(1, col_sharded[:1])
    in_zone = [lax.empty((3, in_part.shape[1], _piece_cols(in_part, True)), in_part.dtype)]
    late_views1 = [f.reshape(1, *f.shape) for f in late_red]
    si_send, si_recv, si_fly, si_tok = _start_copies(
        "grads_scatter_in_join_late_start", [in_part] + in_zone + late_views1, (6,), core1,
        _both_copies(in_cross, 2, _pair_join_copies(3, False, sem0=3)))
    grad_x2, dmod = _grad_h(dproj, w_all, gx0, x2, mod3 + si_tok[0, 0], seq, lay)
    si_done = _wait_copies("grads_scatter_in_join_late_wait", si_fly, si_send, si_recv, grad_x2,
                           _both_copies(in_cross, 2, _pair_join_copies(3, True, sem0=3)))
    late_joined = [f[0] for f in si_done[2:]]
    in_red = _chip_sum(si_done[0], si_done[1], True, place, "grads_chip_sum_in")

    d_ada = jnp.concatenate([dmod[:, 0, :], dmod[:, 1, :], dgate[:, 0, :]], axis=1)
    pieces = [small_tail[3], jnp.sum(d_ada, axis=0), gb_in[0], small_tail[0], small_tail[1], small_tail[2], gcw[0:3]]
    packed, spans = _pack_rows(pieces)
    kept_in, _ = _pack_rows([d_ada])
    rows_all = jnp.concatenate([packed, kept_in], axis=0)
    small_land = lax.empty((N_DEV,) + rows_all.shape, F32)
    sm_send, sm_recv, sm_fly, sm_tok = _start_copies(
        "small_gather_join_in_start", [rows_all, small_land, in_red.reshape(1, *in_red.shape)], (N_DEV,), core1,
        _both_copies(_small_gather_copies, 2, _pair_join_copies(1, False, sem0=N_DEV - 1)))
    big_w = [w_ada[0]] + red_w
    big_m = [m_w_ada[0], m_w_in[0], m_w_proj_attn[0], m_w_proj_conv[0], m_w_out[0]]
    big_v = [v_w_ada[0], v_w_in[0], v_w_proj_attn[0], v_w_proj_conv[0], v_w_out[0]]
    big_out = [None] * 5
    big_out[2:5] = _adamw_together(big_w[2:], late_joined, big_m[2:], big_v[2:], "adamw_late", sm_tok)
    sm_done = _wait_copies("small_gather_join_in_wait", sm_fly, sm_send, sm_recv, big_out[4][0],
                           _both_copies(_small_gather_copies, 2, _pair_join_copies(1, True, sem0=N_DEV - 1)))
    me1 = (4 * lax.axis_index("x") + 2 * lax.axis_index("y") + lax.axis_index("c")).astype(jnp.int32).reshape(1)
    summed, kept = _small_sum(sm_done[0], sm_done[1], me1, packed.shape[0], d)
    loss = summed[0, 0]
    _, g_b_ada, g_b_in, g_b_out, g_ln_g, g_ln_b, g_cw_full = _unpack_rows(
        summed, spans, [(d,), (3 * d,), (lay.din,), (d,), (d,), (d,), (3, d)])
    g_cw = lax.dynamic_slice(g_cw_full, (0, chip * (d // N_CHIPS)), (3, d // N_CHIPS))
    d_ada_all = kept.reshape(N_DEV, -1)[:, :bsz * 3 * d].reshape(N_DEV * bsz, 3 * d)
    ada_cols = 3 * d // N_CHIPS
    d_ada_cols = lax.dynamic_slice(d_ada_all, (0, chip * ada_cols), (N_DEV * bsz, ada_cols))

    g_w_ada, *big_out[0] = _grad_and_adamw_w_ada(act_all.T, d_ada_cols, big_w[0], big_m[0], big_v[0])
    small_w = [b_ada, b_in, conv_w[0], b_out, ln_g, ln_b]
    small_g = [g_b_ada, g_b_in, g_cw, g_b_out, g_ln_g, g_ln_b]
    small_m = [m_b_ada, m_b_in, m_conv_w[0], m_b_out, m_ln_g, m_ln_b]
    small_v = [v_b_ada, v_b_in, v_conv_w[0], v_b_out, v_ln_g, v_ln_b]
    pw, sp = _pack_rows(small_w)
    pg, _ = _pack_rows(small_g)
    pm, _ = _pack_rows(small_m)
    pv, _ = _pack_rows(small_v)
    sd, sm, sv = _adamw(pw, pg, pm, pv, "adamw_small", None)
    big_out[1] = _adamw(big_w[1], sm_done[2][0], big_m[1], big_v[1], "adamw_1", None, also_g=True)
    g_big = [g_w_ada, big_out[1][3]] + late_joined
    shapes = [a.shape for a in small_w]
    sd, sm, sv = _unpack_rows(sd, sp, shapes), _unpack_rows(sm, sp, shapes), _unpack_rows(sv, sp, shapes)

    def order(wa, bA, wi, bI, cw, wpa, wpc, wo, bO, lg, lb):
        return (wa[None], bA, wi[None], bI, cw[None], wpa[None], wpc[None], wo[None], bO, lg, lb)

    sg = [g.reshape(s) for g, s in zip(small_g, shapes)]
    grads_out = order(g_big[0], sg[0], g_big[1], sg[1], sg[2], g_big[2], g_big[3], g_big[4], sg[3], sg[4], sg[5])
    outs = []
    for idx, small in enumerate((sd, sm, sv)):
        outs.append(order(big_out[0][idx], small[0], big_out[1][idx], small[1], small[2], big_out[2][idx],
                          big_out[3][idx], big_out[4][idx], small[3], small[4], small[5]))
    return (loss, grad_x2.reshape(bsz, seq, d), *grads_out, *outs[0], *outs[1], *outs[2])
```
